```python
import jax
import jax.numpy as jnp
from jax import lax
import numpy as np

D_MODEL = 1024
BATCH = 16
SEQ = 2048
DEPTH = 2

GRID_W = 64
CTX_LEN = 256
HEAD_DIM = 64
ATT_HEADS = 8
ATT_KV_HEADS = 2
ATT_GROUPS = ATT_HEADS // ATT_KV_HEADS
ATT_WIDTH = ATT_HEADS * HEAD_DIM
KV_WIDTH = ATT_KV_HEADS * HEAD_DIM
WINDOW = 128
BLOCK = 128
ATT_SCALE = HEAD_DIM ** -0.5
ROPE_THETA = 10000.0
ROPE_FREQS = HEAD_DIM // 4
POOL_WINDOWS = (2, 4, 8, 16)
POOL_GROUPS = len(POOL_WINDOWS)
POOL_WIDTH = D_MODEL // 2
POOL_GROUP_W = POOL_WIDTH // POOL_GROUPS
MIX_AB_IN = ATT_WIDTH + 2 * KV_WIDTH + POOL_WIDTH
MIX_AB_OUT = ATT_WIDTH + POOL_WIDTH
LRU_WIDTH = D_MODEL
LRU_BLOCKS = 8
LRU_BLOCK_W = LRU_WIDTH // LRU_BLOCKS
LRU_C = 8.0
CONV_W = 4
CONV_LEFT = (CONV_W - 1) // 2
D_FF = 2816
N_MOD = 9
LN_EPS = 1e-5
NEG_INF = -1e30
DEEPNORM_ALPHA = (2 * DEPTH) ** 0.25
DEEPNORM_BETA = (8 * DEPTH) ** -0.25
N_EVEN = (DEPTH + 1) // 2
N_ODD = DEPTH // 2

kernel_name = 'hybrid_window_attn_pool_rglru_macaron_dit'


def layer_norm(x, g, b):
    xf = x.astype(jnp.float32)
    mu = jnp.mean(xf, axis=-1, keepdims=True)
    var = jnp.mean(jnp.square(xf - mu), axis=-1, keepdims=True)
    return ((xf - mu) * lax.rsqrt(var + LN_EPS)).astype(x.dtype) * g + b


def residual_post_norm(x, y, g, b):
    return layer_norm(DEEPNORM_ALPHA * x + y, g, b)


def modulate(x, shift, scale):
    return x * (1.0 + scale) + shift


def swiglu(x, w_gate, w_up, w_down):
    return (jax.nn.silu(x @ w_gate) * (x @ w_up)) @ w_down


def axial_rope(rows):
    row = jnp.repeat(jnp.arange(rows, dtype=jnp.float32), GRID_W)
    col = jnp.tile(jnp.arange(GRID_W, dtype=jnp.float32), rows)
    inv = ROPE_THETA ** (-jnp.arange(ROPE_FREQS, dtype=jnp.float32) / ROPE_FREQS)
    ang = jnp.concatenate([row[:, None] * inv, col[:, None] * inv], axis=-1)
    return jnp.cos(ang), jnp.sin(ang)


def apply_rope(x, cos, sin):
    half = HEAD_DIM // 2
    cs = cos[None, :, None, :].astype(x.dtype)
    sn = sin[None, :, None, :].astype(x.dtype)
    x1, x2 = x[..., :half], x[..., half:]
    return jnp.concatenate([x1 * cs - x2 * sn, x2 * cs + x1 * sn], axis=-1)


def windowed_sink_attention(q, k, v, k_ctx, v_ctx, sink):
    bsz, n = q.shape[0], q.shape[1]
    nb = n // BLOCK
    qb = q.reshape(bsz, nb, BLOCK, ATT_KV_HEADS, ATT_GROUPS, HEAD_DIM)

    def band(t):
        tp = jnp.pad(t, ((0, 0), (BLOCK, BLOCK), (0, 0), (0, 0)))
        tp = tp.reshape(bsz, nb + 2, BLOCK, ATT_KV_HEADS, HEAD_DIM)
        return jnp.concatenate([tp[:, :-2], tp[:, 1:-1], tp[:, 2:]], axis=2)

    kb, vb = band(k), band(v)
    s_win = jnp.einsum('bnqkgd,bnskd->bnkgqs', qb, kb, preferred_element_type=jnp.float32) * ATT_SCALE
    qpos = jnp.arange(n).reshape(nb, BLOCK)
    kpos = (jnp.arange(nb)[:, None] - 1) * BLOCK + jnp.arange(3 * BLOCK)[None, :]
    valid = ((jnp.abs(qpos[:, :, None] - kpos[:, None, :]) <= WINDOW)
             & (kpos[:, None, :] >= 0) & (kpos[:, None, :] < n))
    s_win = jnp.where(valid[None, :, None, None], s_win, NEG_INF)
    s_ctx = jnp.einsum('bnqkgd,bckd->bnkgqc', qb, k_ctx, preferred_element_type=jnp.float32) * ATT_SCALE
    sink_l = sink.astype(jnp.float32).reshape(ATT_KV_HEADS, ATT_GROUPS)[None, None, :, :, None, None]
    m = jnp.maximum(jnp.maximum(s_win.max(-1, keepdims=True), s_ctx.max(-1, keepdims=True)), sink_l)
    e_win = jnp.exp(s_win - m)
    e_ctx = jnp.exp(s_ctx - m)
    denom = e_win.sum(-1, keepdims=True) + e_ctx.sum(-1, keepdims=True) + jnp.exp(sink_l - m)
    o = (jnp.einsum('bnkgqs,bnskd->bnqkgd', (e_win / denom).astype(v.dtype), vb)
         + jnp.einsum('bnkgqc,bckd->bnqkgd', (e_ctx / denom).astype(v.dtype), v_ctx))
    return o.reshape(bsz, n, ATT_WIDTH)


def context_sink_attention(q_ctx, k_ctx, v_ctx, sink):
    bsz, n_c = q_ctx.shape[0], q_ctx.shape[1]
    qg = q_ctx.reshape(bsz, n_c, ATT_KV_HEADS, ATT_GROUPS, HEAD_DIM)
    s = jnp.einsum('bqkgd,bskd->bkgqs', qg, k_ctx, preferred_element_type=jnp.float32) * ATT_SCALE
    sink_c = jnp.broadcast_to(sink.astype(jnp.float32).reshape(ATT_KV_HEADS, ATT_GROUPS)[None, :, :, None, None],
                              s.shape[:-1] + (1,))
    p = jax.nn.softmax(jnp.concatenate([s, sink_c], axis=-1), axis=-1)[..., :n_c]
    o = jnp.einsum('bkgqs,bskd->bqkgd', p.astype(v_ctx.dtype), v_ctx)
    return o.reshape(bsz, n_c, ATT_WIDTH)


def multiscale_pool(u, w_pool, pool_scale):
    bsz, n = u.shape[0], u.shape[1]
    uf = u.astype(jnp.float32)
    cs = jnp.pad(jnp.cumsum(uf, axis=1), ((0, 0), (1, 0), (0, 0)))
    t = jnp.arange(n)
    diffs = []
    for g, w in enumerate(POOL_WINDOWS):
        r = w // 2
        lo = jnp.maximum(t - r, 0)
        hi = jnp.minimum(t + r, n - 1) + 1
        sl = slice(g * POOL_GROUP_W, (g + 1) * POOL_GROUP_W)
        seg = cs[:, :, sl]
        mean = (seg[:, hi] - seg[:, lo]) / (hi - lo).astype(jnp.float32)[None, :, None]
        diffs.append(mean - uf[:, :, sl])
    d = jnp.stack(diffs, axis=2).astype(u.dtype)
    y = jnp.einsum('blgc,gce->blge', d, w_pool).reshape(bsz, n, POOL_WIDTH)
    return y * pool_scale


def attention_pool_mixer(h, hc, cos, sin, w_in, sink, w_pool, pool_scale, w_out, need_ctx_out):
    bsz, n = h.shape[0], h.shape[1]
    n_c = hc.shape[1]
    splits = [ATT_WIDTH, ATT_WIDTH + KV_WIDTH, ATT_WIDTH + 2 * KV_WIDTH]
    q, k, v, u = jnp.split(h @ w_in, splits, axis=-1)
    if need_ctx_out:
        q_c, k_c, v_c, u_c = jnp.split(hc @ w_in, splits, axis=-1)
    else:
        k_c, v_c = jnp.split(hc @ w_in[:, ATT_WIDTH:ATT_WIDTH + 2 * KV_WIDTH], 2, axis=-1)
    q = apply_rope(q.reshape(bsz, n, ATT_HEADS, HEAD_DIM), cos, sin)
    k = apply_rope(k.reshape(bsz, n, ATT_KV_HEADS, HEAD_DIM), cos, sin)
    v = v.reshape(bsz, n, ATT_KV_HEADS, HEAD_DIM)
    k_c = k_c.reshape(bsz, n_c, ATT_KV_HEADS, HEAD_DIM)
    v_c = v_c.reshape(bsz, n_c, ATT_KV_HEADS, HEAD_DIM)
    att = windowed_sink_attention(q, k, v, k_c, v_c, sink)
    pool = multiscale_pool(u, w_pool, pool_scale)
    out = jnp.concatenate([att, pool], axis=-1) @ w_out
    if not need_ctx_out:
        return out, None
    att_c = context_sink_attention(q_c, k_c, v_c, sink)
    pool_c = multiscale_pool(u_c, w_pool, pool_scale)
    out_c = jnp.concatenate([att_c, pool_c], axis=-1) @ w_out
    return out, out_c


def centred_depthwise_conv(x, w, b):
    n = x.shape[1]
    xp = jnp.pad(x, ((0, 0), (CONV_LEFT, CONV_W - 1 - CONV_LEFT), (0, 0)))
    y = b
    for tap in range(CONV_W):
        y = y + xp[:, tap:tap + n] * w[tap]
    return y


def rglru_coeffs(u, wa, ba, wx, bx, lam):
    bsz, n = u.shape[0], u.shape[1]
    ub = u.reshape(bsz, n, LRU_BLOCKS, LRU_BLOCK_W)
    r = jax.nn.sigmoid(jnp.einsum('blhi,hij->blhj', ub, wa).reshape(bsz, n, LRU_WIDTH) + ba)
    gi = jax.nn.sigmoid(jnp.einsum('blhi,hij->blhj', ub, wx).reshape(bsz, n, LRU_WIDTH) + bx)
    log_a = -LRU_C * r.astype(jnp.float32) * jax.nn.softplus(-lam.astype(jnp.float32))
    a = jnp.exp(log_a)
    b = jnp.sqrt(-jnp.expm1(2.0 * log_a)) * (gi * u).astype(jnp.float32)
    return a, b


def linear_scan(a, b, h0):
    b = b.at[:, 0].add(a[:, 0] * h0)

    def combine(left, right):
        return left[0] * right[0], right[0] * left[1] + right[1]

    _, h = lax.associative_scan(combine, (a, b), axis=1)
    return h


def recurrent_mixer(h, hc, w_in, conv_w, conv_b, wa, ba, wx, bx, lam, w_out, need_ctx_out):
    gate, u = jnp.split(h @ w_in, 2, axis=-1)
    if need_ctx_out:
        gate_c, u_c = jnp.split(hc @ w_in, 2, axis=-1)
    else:
        u_c = hc @ w_in[:, LRU_WIDTH:]
    u = centred_depthwise_conv(u, conv_w, conv_b)
    u_c = centred_depthwise_conv(u_c, conv_w, conv_b)
    h0 = jnp.zeros((h.shape[0], LRU_WIDTH), jnp.float32)
    ys, ys_c = [], []
    for direction in range(2):
        a, b = rglru_coeffs(u, wa[direction], ba[direction], wx[direction], bx[direction], lam[direction])
        a_c, b_c = rglru_coeffs(u_c, wa[direction], ba[direction], wx[direction], bx[direction], lam[direction])
        if direction == 1:
            a, b, a_c, b_c = (jnp.flip(a, 1), jnp.flip(b, 1), jnp.flip(a_c, 1), jnp.flip(b_c, 1))
        s_c = linear_scan(a_c, b_c, h0)
        s = linear_scan(a, b, s_c[:, -1])
        if direction == 1:
            s, s_c = jnp.flip(s, 1), jnp.flip(s_c, 1)
        ys.append(s)
        ys_c.append(s_c)
    y = (ys[0] + ys[1]).astype(h.dtype)
    out = (jax.nn.gelu(gate) * y) @ w_out
    if not need_ctx_out:
        return out, None
    y_c = (ys_c[0] + ys_c[1]).astype(hc.dtype)
    out_c = (jax.nn.gelu(gate_c) * y_c) @ w_out
    return out, out_c


def _fwd_setup_inputs(seed: int = 0) -> dict:
    key = jax.random.key(seed)
    ks = jax.random.split(key, 26)

    def nrm(i, shape, scale):
        return jax.random.normal(ks[i], shape, jnp.float32) * scale

    lam_u = jax.random.uniform(ks[23], (N_ODD, 2, LRU_WIDTH), jnp.float32, 0.9, 0.999)
    return {
        'x': nrm(0, (BATCH, SEQ, D_MODEL), 1.0),
        'c': nrm(1, (BATCH, D_MODEL), 1.0),
        'ctx': nrm(2, (BATCH, CTX_LEN, D_MODEL), 1.0),
        'c_ctx': nrm(3, (D_MODEL,), 1.0),
        'w_mod': nrm(4, (DEPTH, D_MODEL, N_MOD * D_MODEL), 0.5 * D_MODEL ** -0.5),
        'b_mod': nrm(5, (DEPTH, N_MOD * D_MODEL), 0.02),
        'ln_g': 1.0 + nrm(6, (DEPTH, 3, D_MODEL), 0.02),
        'ln_b': nrm(7, (DEPTH, 3, D_MODEL), 0.02),
        'ffn_w_gate': nrm(8, (DEPTH, 2, D_MODEL, D_FF), D_MODEL ** -0.5),
        'ffn_w_up': nrm(9, (DEPTH, 2, D_MODEL, D_FF), D_MODEL ** -0.5),
        'ffn_w_down': nrm(10, (DEPTH, 2, D_FF, D_MODEL), DEEPNORM_BETA * D_FF ** -0.5),
        'mix_ab_w_in': nrm(11, (N_EVEN, D_MODEL, MIX_AB_IN), D_MODEL ** -0.5),
        'attn_sink': nrm(12, (N_EVEN, ATT_HEADS), 0.5),
        'pool_w': nrm(13, (N_EVEN, POOL_GROUPS, POOL_GROUP_W, POOL_GROUP_W), POOL_GROUP_W ** -0.5),
        'pool_scale': 1.0 + nrm(14, (N_EVEN, POOL_WIDTH), 0.1),
        'mix_ab_w_out': nrm(15, (N_EVEN, MIX_AB_OUT, D_MODEL), DEEPNORM_BETA * MIX_AB_OUT ** -0.5),
        'lru_w_in': nrm(16, (N_ODD, D_MODEL, 2 * LRU_WIDTH), D_MODEL ** -0.5),
        'lru_conv_w': nrm(17, (N_ODD, CONV_W, LRU_WIDTH), CONV_W ** -0.5),
        'lru_conv_b': nrm(18, (N_ODD, LRU_WIDTH), 0.02),
        'lru_wa': nrm(19, (N_ODD, 2, LRU_BLOCKS, LRU_BLOCK_W, LRU_BLOCK_W), LRU_BLOCK_W ** -0.5),
        'lru_ba': nrm(20, (N_ODD, 2, LRU_WIDTH), 0.02),
        'lru_wx': nrm(21, (N_ODD, 2, LRU_BLOCKS, LRU_BLOCK_W, LRU_BLOCK_W), LRU_BLOCK_W ** -0.5),
        'lru_bx': nrm(22, (N_ODD, 2, LRU_WIDTH), 0.02),
        'lru_lambda': jnp.log(lam_u) - jnp.log1p(-lam_u),
        'lru_w_out': nrm(24, (N_ODD, LRU_WIDTH, D_MODEL), DEEPNORM_BETA * LRU_WIDTH ** -0.5),
    }


def _fwd_reference(x, c, ctx, c_ctx, w_mod, b_mod, ln_g, ln_b, ffn_w_gate, ffn_w_up, ffn_w_down,
              mix_ab_w_in, attn_sink, pool_w, pool_scale, mix_ab_w_out,
              lru_w_in, lru_conv_w, lru_conv_b, lru_wa, lru_ba, lru_wx, lru_bx, lru_lambda, lru_w_out):
    rows = x.shape[1] // GRID_W
    cos, sin = axial_rope(rows)
    h, hc = x, ctx
    for layer in range(DEPTH):
        ctx_out = layer < DEPTH - 1
        m = jnp.split((jax.nn.silu(c) @ w_mod[layer] + b_mod[layer])[:, None, :], N_MOD, axis=-1)
        mc = jnp.split((jax.nn.silu(c_ctx) @ w_mod[layer] + b_mod[layer])[None, None, :], N_MOD, axis=-1)
        ffn1 = (ffn_w_gate[layer, 0], ffn_w_up[layer, 0], ffn_w_down[layer, 0])
        ffn2 = (ffn_w_gate[layer, 1], ffn_w_up[layer, 1], ffn_w_down[layer, 1])
        h = residual_post_norm(h, 0.5 * m[2] * swiglu(modulate(h, m[0], m[1]), *ffn1), ln_g[layer, 0], ln_b[layer, 0])
        hc = residual_post_norm(hc, 0.5 * mc[2] * swiglu(modulate(hc, mc[0], mc[1]), *ffn1), ln_g[layer, 0], ln_b[layer, 0])
        h_in = modulate(h, m[3], m[4])
        hc_in = modulate(hc, mc[3], mc[4])
        idx = layer // 2
        if layer % 2 == 0:
            y, y_c = attention_pool_mixer(h_in, hc_in, cos, sin, mix_ab_w_in[idx], attn_sink[idx],
                                          pool_w[idx], pool_scale[idx], mix_ab_w_out[idx], ctx_out)
        else:
            y, y_c = recurrent_mixer(h_in, hc_in, lru_w_in[idx], lru_conv_w[idx], lru_conv_b[idx],
                                     lru_wa[idx], lru_ba[idx], lru_wx[idx], lru_bx[idx], lru_lambda[idx],
                                     lru_w_out[idx], ctx_out)
        h = residual_post_norm(h, m[5] * y, ln_g[layer, 1], ln_b[layer, 1])
        h = residual_post_norm(h, 0.5 * m[8] * swiglu(modulate(h, m[6], m[7]), *ffn2), ln_g[layer, 2], ln_b[layer, 2])
        if ctx_out:
            hc = residual_post_norm(hc, mc[5] * y_c, ln_g[layer, 1], ln_b[layer, 1])
            hc = residual_post_norm(hc, 0.5 * mc[8] * swiglu(modulate(hc, mc[6], mc[7]), *ffn2), ln_g[layer, 2], ln_b[layer, 2])
    return h


import jax as _jax
import jax.numpy as _jnp

TWIN_FORMAT = 'train_step'
FWD_PARAMS = ['x', 'c', 'ctx', 'c_ctx', 'w_mod', 'b_mod', 'ln_g', 'ln_b', 'ffn_w_gate', 'ffn_w_up', 'ffn_w_down', 'mix_ab_w_in', 'attn_sink', 'pool_w', 'pool_scale', 'mix_ab_w_out', 'lru_w_in', 'lru_conv_w', 'lru_conv_b', 'lru_wa', 'lru_ba', 'lru_wx', 'lru_bx', 'lru_lambda', 'lru_w_out']
TWIN_WEIGHTS = ['c_ctx', 'w_mod', 'b_mod', 'ln_g', 'ln_b', 'ffn_w_gate', 'ffn_w_up', 'ffn_w_down', 'mix_ab_w_in', 'attn_sink', 'pool_w', 'pool_scale', 'mix_ab_w_out', 'lru_w_in', 'lru_conv_w', 'lru_conv_b', 'lru_wa', 'lru_ba', 'lru_wx', 'lru_bx', 'lru_lambda', 'lru_w_out']
TWIN_DIFF_INPUT = 'x'
TWIN_INPUTS = ['x', 'c', 'ctx', 'c_ctx', 'w_mod', 'b_mod', 'ln_g', 'ln_b', 'ffn_w_gate', 'ffn_w_up', 'ffn_w_down', 'mix_ab_w_in', 'attn_sink', 'pool_w', 'pool_scale', 'mix_ab_w_out', 'lru_w_in', 'lru_conv_w', 'lru_conv_b', 'lru_wa', 'lru_ba', 'lru_wx', 'lru_bx', 'lru_lambda', 'lru_w_out', 'loss_target', 'm_c_ctx', 'm_w_mod', 'm_b_mod', 'm_ln_g', 'm_ln_b', 'm_ffn_w_gate', 'm_ffn_w_up', 'm_ffn_w_down', 'm_mix_ab_w_in', 'm_attn_sink', 'm_pool_w', 'm_pool_scale', 'm_mix_ab_w_out', 'm_lru_w_in', 'm_lru_conv_w', 'm_lru_conv_b', 'm_lru_wa', 'm_lru_ba', 'm_lru_wx', 'm_lru_bx', 'm_lru_lambda', 'm_lru_w_out', 'v_c_ctx', 'v_w_mod', 'v_b_mod', 'v_ln_g', 'v_ln_b', 'v_ffn_w_gate', 'v_ffn_w_up', 'v_ffn_w_down', 'v_mix_ab_w_in', 'v_attn_sink', 'v_pool_w', 'v_pool_scale', 'v_mix_ab_w_out', 'v_lru_w_in', 'v_lru_conv_w', 'v_lru_conv_b', 'v_lru_wa', 'v_lru_ba', 'v_lru_wx', 'v_lru_bx', 'v_lru_lambda', 'v_lru_w_out']
TWIN_OUTPUTS = ['loss', 'grad_x', 'grad_c_ctx', 'grad_w_mod', 'grad_b_mod', 'grad_ln_g', 'grad_ln_b', 'grad_ffn_w_gate', 'grad_ffn_w_up', 'grad_ffn_w_down', 'grad_mix_ab_w_in', 'grad_attn_sink', 'grad_pool_w', 'grad_pool_scale', 'grad_mix_ab_w_out', 'grad_lru_w_in', 'grad_lru_conv_w', 'grad_lru_conv_b', 'grad_lru_wa', 'grad_lru_ba', 'grad_lru_wx', 'grad_lru_bx', 'grad_lru_lambda', 'grad_lru_w_out', 'delta_c_ctx', 'delta_w_mod', 'delta_b_mod', 'delta_ln_g', 'delta_ln_b', 'delta_ffn_w_gate', 'delta_ffn_w_up', 'delta_ffn_w_down', 'delta_mix_ab_w_in', 'delta_attn_sink', 'delta_pool_w', 'delta_pool_scale', 'delta_mix_ab_w_out', 'delta_lru_w_in', 'delta_lru_conv_w', 'delta_lru_conv_b', 'delta_lru_wa', 'delta_lru_ba', 'delta_lru_wx', 'delta_lru_bx', 'delta_lru_lambda', 'delta_lru_w_out', 'new_m_c_ctx', 'new_m_w_mod', 'new_m_b_mod', 'new_m_ln_g', 'new_m_ln_b', 'new_m_ffn_w_gate', 'new_m_ffn_w_up', 'new_m_ffn_w_down', 'new_m_mix_ab_w_in', 'new_m_attn_sink', 'new_m_pool_w', 'new_m_pool_scale', 'new_m_mix_ab_w_out', 'new_m_lru_w_in', 'new_m_lru_conv_w', 'new_m_lru_conv_b', 'new_m_lru_wa', 'new_m_lru_ba', 'new_m_lru_wx', 'new_m_lru_bx', 'new_m_lru_lambda', 'new_m_lru_w_out', 'new_v_c_ctx', 'new_v_w_mod', 'new_v_b_mod', 'new_v_ln_g', 'new_v_ln_b', 'new_v_ffn_w_gate', 'new_v_ffn_w_up', 'new_v_ffn_w_down', 'new_v_mix_ab_w_in', 'new_v_attn_sink', 'new_v_pool_w', 'new_v_pool_scale', 'new_v_mix_ab_w_out', 'new_v_lru_w_in', 'new_v_lru_conv_w', 'new_v_lru_conv_b', 'new_v_lru_wa', 'new_v_lru_ba', 'new_v_lru_wx', 'new_v_lru_bx', 'new_v_lru_lambda', 'new_v_lru_w_out']
TWIN_LEAF_KINDS = {'loss': 'loss', 'grad_x': 'grad_x', 'grad_c_ctx': 'grad_w', 'grad_w_mod': 'grad_w', 'grad_b_mod': 'grad_w', 'grad_ln_g': 'grad_w', 'grad_ln_b': 'grad_w', 'grad_ffn_w_gate': 'grad_w', 'grad_ffn_w_up': 'grad_w', 'grad_ffn_w_down': 'grad_w', 'grad_mix_ab_w_in': 'grad_w', 'grad_attn_sink': 'grad_w', 'grad_pool_w': 'grad_w', 'grad_pool_scale': 'grad_w', 'grad_mix_ab_w_out': 'grad_w', 'grad_lru_w_in': 'grad_w', 'grad_lru_conv_w': 'grad_w', 'grad_lru_conv_b': 'grad_w', 'grad_lru_wa': 'grad_w', 'grad_lru_ba': 'grad_w', 'grad_lru_wx': 'grad_w', 'grad_lru_bx': 'grad_w', 'grad_lru_lambda': 'grad_w', 'grad_lru_w_out': 'grad_w', 'delta_c_ctx': 'delta_w', 'delta_w_mod': 'delta_w', 'delta_b_mod': 'delta_w', 'delta_ln_g': 'delta_w', 'delta_ln_b': 'delta_w', 'delta_ffn_w_gate': 'delta_w', 'delta_ffn_w_up': 'delta_w', 'delta_ffn_w_down': 'delta_w', 'delta_mix_ab_w_in': 'delta_w', 'delta_attn_sink': 'delta_w', 'delta_pool_w': 'delta_w', 'delta_pool_scale': 'delta_w', 'delta_mix_ab_w_out': 'delta_w', 'delta_lru_w_in': 'delta_w', 'delta_lru_conv_w': 'delta_w', 'delta_lru_conv_b': 'delta_w', 'delta_lru_wa': 'delta_w', 'delta_lru_ba': 'delta_w', 'delta_lru_wx': 'delta_w', 'delta_lru_bx': 'delta_w', 'delta_lru_lambda': 'delta_w', 'delta_lru_w_out': 'delta_w', 'new_m_c_ctx': 'new_m', 'new_m_w_mod': 'new_m', 'new_m_b_mod': 'new_m', 'new_m_ln_g': 'new_m', 'new_m_ln_b': 'new_m', 'new_m_ffn_w_gate': 'new_m', 'new_m_ffn_w_up': 'new_m', 'new_m_ffn_w_down': 'new_m', 'new_m_mix_ab_w_in': 'new_m', 'new_m_attn_sink': 'new_m', 'new_m_pool_w': 'new_m', 'new_m_pool_scale': 'new_m', 'new_m_mix_ab_w_out': 'new_m', 'new_m_lru_w_in': 'new_m', 'new_m_lru_conv_w': 'new_m', 'new_m_lru_conv_b': 'new_m', 'new_m_lru_wa': 'new_m', 'new_m_lru_ba': 'new_m', 'new_m_lru_wx': 'new_m', 'new_m_lru_bx': 'new_m', 'new_m_lru_lambda': 'new_m', 'new_m_lru_w_out': 'new_m', 'new_v_c_ctx': 'new_v', 'new_v_w_mod': 'new_v', 'new_v_b_mod': 'new_v', 'new_v_ln_g': 'new_v', 'new_v_ln_b': 'new_v', 'new_v_ffn_w_gate': 'new_v', 'new_v_ffn_w_up': 'new_v', 'new_v_ffn_w_down': 'new_v', 'new_v_mix_ab_w_in': 'new_v', 'new_v_attn_sink': 'new_v', 'new_v_pool_w': 'new_v', 'new_v_pool_scale': 'new_v', 'new_v_mix_ab_w_out': 'new_v', 'new_v_lru_w_in': 'new_v', 'new_v_lru_conv_w': 'new_v', 'new_v_lru_conv_b': 'new_v', 'new_v_lru_wa': 'new_v', 'new_v_lru_ba': 'new_v', 'new_v_lru_wx': 'new_v', 'new_v_lru_bx': 'new_v', 'new_v_lru_lambda': 'new_v', 'new_v_lru_w_out': 'new_v'}


def _forward(args):
    return _fwd_reference(*[args[k] for k in FWD_PARAMS])


def _output_shape():
    out = _jax.eval_shape(lambda: _forward(_fwd_setup_inputs(0)))
    return out.shape, out.dtype

N_MICROBATCH = 1
ADAM_LR = 0.001
ADAM_B1 = 0.9
ADAM_B2 = 0.999
ADAM_EPS = 1e-08
ADAM_WD = 0.01
ADAM_STEP = 10
PER_EXAMPLE_BATCH_AXIS = {'x': 0, 'c': 0, 'ctx': 0, 'loss_target': 0}
SHARED_INPUTS = []
_WEIGHT_DTYPES = {'c_ctx': _jnp.float32, 'w_mod': _jnp.float32, 'b_mod': _jnp.float32, 'ln_g': _jnp.float32, 'ln_b': _jnp.float32, 'ffn_w_gate': _jnp.float32, 'ffn_w_up': _jnp.float32, 'ffn_w_down': _jnp.float32, 'mix_ab_w_in': _jnp.float32, 'attn_sink': _jnp.float32, 'pool_w': _jnp.float32, 'pool_scale': _jnp.float32, 'mix_ab_w_out': _jnp.float32, 'lru_w_in': _jnp.float32, 'lru_conv_w': _jnp.float32, 'lru_conv_b': _jnp.float32, 'lru_wa': _jnp.float32, 'lru_ba': _jnp.float32, 'lru_wx': _jnp.float32, 'lru_bx': _jnp.float32, 'lru_lambda': _jnp.float32, 'lru_w_out': _jnp.float32}
MOMENT_SCALE = {'c_ctx': 4.342700e-03, 'w_mod': 1.899804e-02, 'b_mod': 3.126869e-02, 'ln_g': 1.312180e+01, 'ln_b': 6.840066e-01, 'ffn_w_gate': 4.323468e-03, 'ffn_w_up': 4.210431e-03, 'ffn_w_down': 1.398710e-02, 'mix_ab_w_in': 1.263703e-02, 'attn_sink': 4.694415e-05, 'pool_w': 1.844815e-02, 'pool_scale': 1.931672e-02, 'mix_ab_w_out': 2.790438e-02, 'lru_w_in': 2.478221e-02, 'lru_conv_w': 2.674670e-02, 'lru_conv_b': 6.116770e-02, 'lru_wa': 2.480513e-03, 'lru_ba': 3.252364e-03, 'lru_wx': 4.284927e-03, 'lru_bx': 6.033316e-03, 'lru_lambda': 8.035676e-03, 'lru_w_out': 4.934090e-02}


def _to_microbatches(a, axis):
    t = _jnp.moveaxis(a, axis, 0)
    t = t.reshape((N_MICROBATCH, t.shape[0] // N_MICROBATCH) + t.shape[1:])
    return _jnp.moveaxis(t, 1, axis + 1)


def setup_inputs(seed: int = 0) -> dict:
    inp = _fwd_setup_inputs(seed)
    key = _jax.random.fold_in(_jax.random.key(seed), 7919)
    shape, _ = _output_shape()
    out = dict(inp)
    out["loss_target"] = _jax.random.normal(_jax.random.fold_in(key, 0), shape, _jnp.float32)
    for i, name in enumerate(TWIN_WEIGHTS):
        w = inp[name].astype(_jnp.float32)
        if MOMENT_SCALE is None:
            s = _jnp.sqrt(_jnp.mean(_jnp.square(w)) + 1e-30)
        else:
            s = MOMENT_SCALE[name]
        km, kv = _jax.random.split(_jax.random.fold_in(key, i + 1))
        out[name] = w
        out["m_" + name] = s * _jax.random.normal(km, w.shape, _jnp.float32)
        out["v_" + name] = (s * s) * _jax.random.uniform(kv, w.shape, _jnp.float32, 0.5, 1.5)
    if N_MICROBATCH > 1:
        for name, axis in PER_EXAMPLE_BATCH_AXIS.items():
            out[name] = _to_microbatches(out[name], axis)
    return {'x': out['x'], 'c': out['c'], 'ctx': out['ctx'], 'c_ctx': out['c_ctx'], 'w_mod': out['w_mod'], 'b_mod': out['b_mod'], 'ln_g': out['ln_g'], 'ln_b': out['ln_b'], 'ffn_w_gate': out['ffn_w_gate'], 'ffn_w_up': out['ffn_w_up'], 'ffn_w_down': out['ffn_w_down'], 'mix_ab_w_in': out['mix_ab_w_in'], 'attn_sink': out['attn_sink'], 'pool_w': out['pool_w'], 'pool_scale': out['pool_scale'], 'mix_ab_w_out': out['mix_ab_w_out'], 'lru_w_in': out['lru_w_in'], 'lru_conv_w': out['lru_conv_w'], 'lru_conv_b': out['lru_conv_b'], 'lru_wa': out['lru_wa'], 'lru_ba': out['lru_ba'], 'lru_wx': out['lru_wx'], 'lru_bx': out['lru_bx'], 'lru_lambda': out['lru_lambda'], 'lru_w_out': out['lru_w_out'], 'loss_target': out['loss_target'], 'm_c_ctx': out['m_c_ctx'], 'm_w_mod': out['m_w_mod'], 'm_b_mod': out['m_b_mod'], 'm_ln_g': out['m_ln_g'], 'm_ln_b': out['m_ln_b'], 'm_ffn_w_gate': out['m_ffn_w_gate'], 'm_ffn_w_up': out['m_ffn_w_up'], 'm_ffn_w_down': out['m_ffn_w_down'], 'm_mix_ab_w_in': out['m_mix_ab_w_in'], 'm_attn_sink': out['m_attn_sink'], 'm_pool_w': out['m_pool_w'], 'm_pool_scale': out['m_pool_scale'], 'm_mix_ab_w_out': out['m_mix_ab_w_out'], 'm_lru_w_in': out['m_lru_w_in'], 'm_lru_conv_w': out['m_lru_conv_w'], 'm_lru_conv_b': out['m_lru_conv_b'], 'm_lru_wa': out['m_lru_wa'], 'm_lru_ba': out['m_lru_ba'], 'm_lru_wx': out['m_lru_wx'], 'm_lru_bx': out['m_lru_bx'], 'm_lru_lambda': out['m_lru_lambda'], 'm_lru_w_out': out['m_lru_w_out'], 'v_c_ctx': out['v_c_ctx'], 'v_w_mod': out['v_w_mod'], 'v_b_mod': out['v_b_mod'], 'v_ln_g': out['v_ln_g'], 'v_ln_b': out['v_ln_b'], 'v_ffn_w_gate': out['v_ffn_w_gate'], 'v_ffn_w_up': out['v_ffn_w_up'], 'v_ffn_w_down': out['v_ffn_w_down'], 'v_mix_ab_w_in': out['v_mix_ab_w_in'], 'v_attn_sink': out['v_attn_sink'], 'v_pool_w': out['v_pool_w'], 'v_pool_scale': out['v_pool_scale'], 'v_mix_ab_w_out': out['v_mix_ab_w_out'], 'v_lru_w_in': out['v_lru_w_in'], 'v_lru_conv_w': out['v_lru_conv_w'], 'v_lru_conv_b': out['v_lru_conv_b'], 'v_lru_wa': out['v_lru_wa'], 'v_lru_ba': out['v_lru_ba'], 'v_lru_wx': out['v_lru_wx'], 'v_lru_bx': out['v_lru_bx'], 'v_lru_lambda': out['v_lru_lambda'], 'v_lru_w_out': out['v_lru_w_out']}


def _loss(weights, diff, rest, loss_target):
    with _jax.named_scope("forward"):
        args = {**rest, TWIN_DIFF_INPUT: diff, **{k: w.astype(_WEIGHT_DTYPES[k]) for k, w in weights.items()}}
        y = _forward(args)
    with _jax.named_scope("loss_head"):
        err = _jnp.square(y.astype(_jnp.float32) - loss_target)
        return 0.5 * _jnp.sum(_jnp.mean(err, axis=-1)) if err.ndim else 0.5 * err


def _adamw(w, g, m, v):
    m = ADAM_B1 * m + (1.0 - ADAM_B1) * g
    v = ADAM_B2 * v + (1.0 - ADAM_B2) * _jnp.square(g)
    m_hat = m / (1.0 - ADAM_B1 ** ADAM_STEP)
    v_hat = v / (1.0 - ADAM_B2 ** ADAM_STEP)
    delta = -ADAM_LR * (m_hat / (_jnp.sqrt(v_hat) + ADAM_EPS) + ADAM_WD * w)
    return delta, m, v


def reference(x, c, ctx, c_ctx, w_mod, b_mod, ln_g, ln_b, ffn_w_gate, ffn_w_up, ffn_w_down, mix_ab_w_in, attn_sink, pool_w, pool_scale, mix_ab_w_out, lru_w_in, lru_conv_w, lru_conv_b, lru_wa, lru_ba, lru_wx, lru_bx, lru_lambda, lru_w_out, loss_target, m_c_ctx, m_w_mod, m_b_mod, m_ln_g, m_ln_b, m_ffn_w_gate, m_ffn_w_up, m_ffn_w_down, m_mix_ab_w_in, m_attn_sink, m_pool_w, m_pool_scale, m_mix_ab_w_out, m_lru_w_in, m_lru_conv_w, m_lru_conv_b, m_lru_wa, m_lru_ba, m_lru_wx, m_lru_bx, m_lru_lambda, m_lru_w_out, v_c_ctx, v_w_mod, v_b_mod, v_ln_g, v_ln_b, v_ffn_w_gate, v_ffn_w_up, v_ffn_w_down, v_mix_ab_w_in, v_attn_sink, v_pool_w, v_pool_scale, v_mix_ab_w_out, v_lru_w_in, v_lru_conv_w, v_lru_conv_b, v_lru_wa, v_lru_ba, v_lru_wx, v_lru_bx, v_lru_lambda, v_lru_w_out):
    given = dict(x=x, c=c, ctx=ctx, c_ctx=c_ctx, w_mod=w_mod, b_mod=b_mod, ln_g=ln_g, ln_b=ln_b, ffn_w_gate=ffn_w_gate, ffn_w_up=ffn_w_up, ffn_w_down=ffn_w_down, mix_ab_w_in=mix_ab_w_in, attn_sink=attn_sink, pool_w=pool_w, pool_scale=pool_scale, mix_ab_w_out=mix_ab_w_out, lru_w_in=lru_w_in, lru_conv_w=lru_conv_w, lru_conv_b=lru_conv_b, lru_wa=lru_wa, lru_ba=lru_ba, lru_wx=lru_wx, lru_bx=lru_bx, lru_lambda=lru_lambda, lru_w_out=lru_w_out, loss_target=loss_target, m_c_ctx=m_c_ctx, m_w_mod=m_w_mod, m_b_mod=m_b_mod, m_ln_g=m_ln_g, m_ln_b=m_ln_b, m_ffn_w_gate=m_ffn_w_gate, m_ffn_w_up=m_ffn_w_up, m_ffn_w_down=m_ffn_w_down, m_mix_ab_w_in=m_mix_ab_w_in, m_attn_sink=m_attn_sink, m_pool_w=m_pool_w, m_pool_scale=m_pool_scale, m_mix_ab_w_out=m_mix_ab_w_out, m_lru_w_in=m_lru_w_in, m_lru_conv_w=m_lru_conv_w, m_lru_conv_b=m_lru_conv_b, m_lru_wa=m_lru_wa, m_lru_ba=m_lru_ba, m_lru_wx=m_lru_wx, m_lru_bx=m_lru_bx, m_lru_lambda=m_lru_lambda, m_lru_w_out=m_lru_w_out, v_c_ctx=v_c_ctx, v_w_mod=v_w_mod, v_b_mod=v_b_mod, v_ln_g=v_ln_g, v_ln_b=v_ln_b, v_ffn_w_gate=v_ffn_w_gate, v_ffn_w_up=v_ffn_w_up, v_ffn_w_down=v_ffn_w_down, v_mix_ab_w_in=v_mix_ab_w_in, v_attn_sink=v_attn_sink, v_pool_w=v_pool_w, v_pool_scale=v_pool_scale, v_mix_ab_w_out=v_mix_ab_w_out, v_lru_w_in=v_lru_w_in, v_lru_conv_w=v_lru_conv_w, v_lru_conv_b=v_lru_conv_b, v_lru_wa=v_lru_wa, v_lru_ba=v_lru_ba, v_lru_wx=v_lru_wx, v_lru_bx=v_lru_bx, v_lru_lambda=v_lru_lambda, v_lru_w_out=v_lru_w_out)
    weights = {n: given[n] for n in TWIN_WEIGHTS}
    shared = {n: given[n] for n in SHARED_INPUTS}
    per_example = {n: given[n] for n in ['x', 'c', 'ctx']}
    grad_fn = _jax.value_and_grad(_loss, argnums=(0, 1))

    def one_microbatch(ex, loss_target):
        ex = dict(ex)
        diff = ex.pop(TWIN_DIFF_INPUT)
        return grad_fn(weights, diff, {**shared, **ex}, loss_target)

    if N_MICROBATCH == 1:
        loss, (grad_w, grad_x) = one_microbatch(per_example, given["loss_target"])
    else:
        def body(carry, xs):
            loss_sum, grad_sum = carry
            l_k, (gw_k, gx_k) = one_microbatch(xs[0], xs[1])
            with _jax.named_scope("update"):
                return (loss_sum + l_k, _jax.tree.map(_jnp.add, grad_sum, gw_k)), gx_k

        init = (_jnp.zeros((), _jnp.float32), _jax.tree.map(_jnp.zeros_like, weights))
        (loss, grad_w), grad_x = _jax.lax.scan(body, init, (per_example, given["loss_target"]))
    with _jax.named_scope("update"):
        delta_w, new_m, new_v = {}, {}, {}
        for n in TWIN_WEIGHTS:
            delta_w[n], new_m[n], new_v[n] = _adamw(weights[n], grad_w[n], given["m_" + n], given["v_" + n])
    return (loss, grad_x, *[grad_w[n] for n in TWIN_WEIGHTS], *[delta_w[n] for n in TWIN_WEIGHTS],
            *[new_m[n] for n in TWIN_WEIGHTS], *[new_v[n] for n in TWIN_WEIGHTS])
```

```python
import functools
import math

import jax
import jax.numpy as jnp
from jax import lax
from jax.experimental import pallas as pl
from jax.experimental.pallas import tpu as pltpu

F32, BF16 = jnp.float32, jnp.bfloat16
MESH = pl.DeviceIdType.MESH
ANY = pl.BlockSpec(memory_space=pl.ANY)
VMEM_SPEC = pl.BlockSpec(memory_space=pltpu.VMEM)

D = 1024
N_CHIP = 4
HEAD_DIM, N_HEADS, N_KV = 64, 8, 2
ATT_W, KV_W, POOL_W = 512, 128, 512
POOL_WINDOWS = (2, 4, 8, 16)
BLK = 128
ATT_SCALE = HEAD_DIM ** -0.5
ROPE_THETA = 10000.0
GRID_W = 64
LRU_C = 8.0
LN_EPS = 1e-5
NEG_INF = -1e30
DEPTH = 2
ALPHA = (2 * DEPTH) ** 0.25
N_MOD = 9
ADAM_LR, ADAM_B1, ADAM_B2, ADAM_EPS, ADAM_WD, ADAM_STEP = 0.001, 0.9, 0.999, 1e-08, 0.01, 10
VMEM_BIG = 48 * 1024 * 1024


def _cp(sem=None, vmem=None):
    kw = {}
    if sem is not None:
        kw["dimension_semantics"] = sem
    if vmem is not None:
        kw["vmem_limit_bytes"] = vmem
    return pltpu.CompilerParams(**kw)


def _sds(shape, dtype):
    return jax.ShapeDtypeStruct(tuple(shape), dtype)


def _pick(n, cands):
    for c in cands:
        if n % c == 0:
            return c
    return n


def _dot(a, b, dims):
    return lax.dot_general(a, b, (dims, ((), ())), preferred_element_type=F32)


def _nn(a, b):
    return _dot(a, b, ((1,), (0,)))


def _nt(a, b):
    return _dot(a, b, ((1,), (1,)))


def _tn(a, b):
    return _dot(a, b, ((0,), (0,)))


def _sigmoid(x):
    return 1.0 / (1.0 + jnp.exp(-x))


def _me():
    return lax.axis_index("x"), lax.axis_index("y"), lax.axis_index("c")


def _rcopy(src, dst, ssem, rsem, dev):
    return pltpu.make_async_remote_copy(src_ref=src, dst_ref=dst, send_sem=ssem, recv_sem=rsem,
                                        device_id=dev, device_id_type=MESH)


def all_gather8(x, name):
    def body(x_ref, o_ref, ssem, rsem, lsem):
        mx, my, mc = _me()
        me = 4 * mx + 2 * my + mc
        loc = pltpu.make_async_copy(x_ref, o_ref.at[me], lsem)
        loc.start()
        peers = []
        for m in range(1, 8):
            px = 1 - mx if (m >> 2) & 1 else mx
            py = 1 - my if (m >> 1) & 1 else my
            pc = 1 - mc if m & 1 else mc
            peers.append((px, py, pc))
        sends = [_rcopy(x_ref, o_ref.at[me], ssem.at[k], rsem.at[k], p) for k, p in enumerate(peers)]
        for cp in sends:
            cp.start()
        for k, (px, py, pc) in enumerate(peers):
            _rcopy(x_ref, o_ref.at[4 * px + 2 * py + pc], ssem.at[k], rsem.at[k], (px, py, pc)).wait_recv()
        for cp in sends:
            cp.wait_send()
        loc.wait()

    return pl.pallas_call(
        body, name=name, out_shape=_sds((8,) + x.shape, x.dtype),
        in_specs=[VMEM_SPEC], out_specs=VMEM_SPEC,
        scratch_shapes=[pltpu.SemaphoreType.DMA((7,)), pltpu.SemaphoreType.DMA((7,)), pltpu.SemaphoreType.DMA],
    )(x)


def all_gather_chips(shard, name):
    def body(x_ref, o_ref, ssem, rsem, lsem):
        mx, my, mc = _me()
        s = 2 * mx + my
        sib = (mx, my, 1 - mc)
        loc = pltpu.make_async_copy(x_ref, o_ref.at[s], lsem)
        loc.start()
        chips = [(1 - mx, my), (mx, 1 - my), (1 - mx, 1 - my)]
        first = [_rcopy(x_ref.at[mc], o_ref.at[s, mc], ssem.at[j], rsem.at[j], (px, py, mc))
                 for j, (px, py) in enumerate(chips)]
        for cp in first:
            cp.start()
        passed = []
        for j, (px, py) in enumerate(chips):
            ps = 2 * px + py
            _rcopy(x_ref.at[mc], o_ref.at[ps, mc], ssem.at[j], rsem.at[j], (px, py, mc)).wait_recv()
            fw = _rcopy(o_ref.at[ps, mc], o_ref.at[ps, mc], ssem.at[3 + j], rsem.at[3 + j], sib)
            fw.start()
            passed.append(fw)
        for j, (px, py) in enumerate(chips):
            ps = 2 * px + py
            _rcopy(o_ref.at[ps, 1 - mc], o_ref.at[ps, 1 - mc], ssem.at[3 + j], rsem.at[3 + j], sib).wait_recv()
        for cp in first + passed:
            cp.wait_send()
        loc.wait()

    return pl.pallas_call(
        body, name=name, out_shape=_sds((N_CHIP,) + shard.shape, shard.dtype),
        in_specs=[ANY], out_specs=ANY,
        scratch_shapes=[pltpu.SemaphoreType.DMA((6,)), pltpu.SemaphoreType.DMA((6,)), pltpu.SemaphoreType.DMA],
    )(shard)


def sibling_send_other_half(buf, name):
    def body(x_ref, o_ref, ssem, rsem):
        mx, my, mc = _me()
        sib = (mx, my, 1 - mc)
        cps = [_rcopy(x_ref.at[k, 1 - mc], o_ref.at[k], ssem.at[k], rsem.at[k], sib) for k in range(N_CHIP)]
        for cp in cps:
            cp.start()
        for cp in cps:
            cp.wait_recv()
        for cp in cps:
            cp.wait_send()

    n, _, h, w = buf.shape
    return pl.pallas_call(
        body, name=name, out_shape=_sds((n, h, w), buf.dtype), in_specs=[ANY], out_specs=ANY,
        scratch_shapes=[pltpu.SemaphoreType.DMA((N_CHIP,)), pltpu.SemaphoreType.DMA((N_CHIP,))],
    )(buf)


def chips_all_to_all(q, name):
    def body(x_ref, o_ref, ssem, rsem, lsem):
        mx, my, mc = _me()
        s = 2 * mx + my
        loc = pltpu.make_async_copy(x_ref.at[s], o_ref.at[s], lsem)
        loc.start()
        chips = [(1 - mx, my), (mx, 1 - my), (1 - mx, 1 - my)]
        cps = [_rcopy(x_ref.at[2 * px + py], o_ref.at[s], ssem.at[j], rsem.at[j], (px, py, mc))
               for j, (px, py) in enumerate(chips)]
        for cp in cps:
            cp.start()
        for j, (px, py) in enumerate(chips):
            ps = 2 * px + py
            _rcopy(x_ref.at[ps], o_ref.at[ps], ssem.at[j], rsem.at[j], (px, py, mc)).wait_recv()
        for cp in cps:
            cp.wait_send()
        loc.wait()

    return pl.pallas_call(
        body, name=name, out_shape=_sds(q.shape, q.dtype), in_specs=[ANY], out_specs=ANY,
        scratch_shapes=[pltpu.SemaphoreType.DMA((3,)), pltpu.SemaphoreType.DMA((3,)), pltpu.SemaphoreType.DMA],
    )(q)


def sibling_join_halves(r, name):
    def body(x_ref, o_ref, ssem, rsem, lsem):
        mx, my, mc = _me()
        sib = (mx, my, 1 - mc)
        loc = pltpu.make_async_copy(x_ref, o_ref.at[mc], lsem)
        loc.start()
        cp = _rcopy(x_ref, o_ref.at[mc], ssem, rsem, sib)
        cp.start()
        _rcopy(x_ref, o_ref.at[1 - mc], ssem, rsem, sib).wait_recv()
        cp.wait_send()
        loc.wait()

    return pl.pallas_call(
        body, name=name, out_shape=_sds((2,) + r.shape, r.dtype), in_specs=[ANY], out_specs=ANY,
        scratch_shapes=[pltpu.SemaphoreType.DMA, pltpu.SemaphoreType.DMA, pltpu.SemaphoreType.DMA],
    )(r)


def add_own_half(buf, recv, name):
    n, _, h, w = buf.shape
    th = _pick(h, (512, 376, 256, 128, 64, 32, 16, 8))
    cidx = jnp.reshape(lax.axis_index("c"), (1,)).astype(jnp.int32)

    def body(c_ref, a_ref, b_ref, o_ref):
        o_ref[...] = a_ref[0] + b_ref[...]

    return pl.pallas_call(
        body, name=name, out_shape=_sds((n, h, w), buf.dtype),
        grid_spec=pltpu.PrefetchScalarGridSpec(
            num_scalar_prefetch=1, grid=(n, h // th),
            in_specs=[pl.BlockSpec((None, 1, th, w), lambda k, r, c: (k, c[0], r, 0)),
                      pl.BlockSpec((None, th, w), lambda k, r, c: (k, r, 0))],
            out_specs=pl.BlockSpec((None, th, w), lambda k, r, c: (k, r, 0))),
    )(cidx, buf, recv)


def sum_slots(r, name):
    n, h, w = r.shape
    th = _pick(h, (512, 376, 256, 128, 64, 32, 16, 8))

    def body(a_ref, o_ref):
        o_ref[...] = ((a_ref[0] + a_ref[1]) + a_ref[2]) + a_ref[3]

    return pl.pallas_call(
        body, name=name, out_shape=_sds((h, w), r.dtype), grid=(h // th,),
        in_specs=[pl.BlockSpec((n, th, w), lambda i: (0, i, 0))],
        out_specs=pl.BlockSpec((th, w), lambda i: (i, 0)),
    )(r)


def reduce_scatter_chips(buf, tag):
    recv = sibling_send_other_half(buf, f"rs_sib_{tag}")
    q = add_own_half(buf, recv, f"rs_add2_{tag}")
    r = chips_all_to_all(q, f"rs_a2a_{tag}")
    red = sum_slots(r, f"rs_add4_{tag}")
    return sibling_join_halves(red, f"rs_join_{tag}")


class Layout:
    def __init__(self, n_ctx, n_lat):
        self.C, self.L = n_ctx, n_lat
        self.PS = n_ctx + n_lat
        self.T = 2 * self.PS
        self.tr = _pick(math.gcd(n_ctx, n_lat), (256, 128))
        self.bps = self.PS // self.tr
        self.cb = n_ctx // self.tr
        self.nblk = self.T // self.tr
        self.tm = _pick(self.T, (512, 256, 128))

    def seg(self, i):
        return jnp.where(i % self.bps < self.cb, 2, i // self.bps)


def rowwise(lay, name, fn, rows, segs=(), vecs=(), outs=(), sums=()):
    tr, nblk = lay.tr, lay.nblk
    n_r, n_s, n_v, n_o = len(rows), len(segs), len(vecs), len(outs)

    def body(*refs):
        ins = refs[:n_r + n_s + n_v]
        ors = refs[n_r + n_s + n_v:]
        vals = [r[...] for r in ins[:n_r]] + [r[0] for r in ins[n_r:n_r + n_s]] + [r[...] for r in ins[n_r + n_s:]]
        res = fn(*vals)
        for k in range(n_o):
            ors[k][...] = res[k].astype(ors[k].dtype)
        for k in range(len(sums)):
            ors[n_o + k][0] = res[n_o + k]

    in_specs = [pl.BlockSpec((tr, a.shape[1]), lambda i: (i, 0)) for a in rows]
    in_specs += [pl.BlockSpec((1,) + a.shape[1:], lambda i: (lay.seg(i), 0, 0)) for a in segs]
    in_specs += [pl.BlockSpec(a.shape, lambda i: (0, 0)) for a in vecs]
    out_shape = [_sds((lay.T, w), dt) for w, dt in outs] + [_sds((nblk, r, w), F32) for r, w in sums]
    out_specs = [pl.BlockSpec((tr, w), lambda i: (i, 0)) for w, _ in outs]
    out_specs += [pl.BlockSpec((1, r, w), lambda i: (i, 0, 0)) for r, w in sums]
    return pl.pallas_call(body, name=name, out_shape=out_shape, grid=(nblk,), in_specs=in_specs,
                          out_specs=out_specs, compiler_params=_cp(("parallel",)))(*rows, *segs, *vecs)


def modulate(lay, h, mod, k_shift, k_scale, name):
    def fn(hb, m):
        return (hb * (1.0 + m[k_scale:k_scale + 1]) + m[k_shift:k_shift + 1],)
    return rowwise(lay, name, fn, [h], segs=[mod], outs=[(D, BF16)])[0]


def resid_ln(lay, h, y, mod, k_gate, coef, lnv, name):
    def fn(hb, yb, m, ln):
        z = ALPHA * hb + (coef * m[k_gate:k_gate + 1]) * yb
        mu = jnp.mean(z, axis=-1, keepdims=True)
        zc = z - mu
        var = jnp.mean(zc * zc, axis=-1, keepdims=True)
        rstd = lax.rsqrt(var + LN_EPS)
        xhat = zc * rstd
        return xhat * ln[0:1] + ln[1:2], xhat, rstd
    return rowwise(lay, name, fn, [h, y], segs=[mod], vecs=[lnv], outs=[(D, F32), (D, F32), (1, F32)])


def ln_bwd(lay, dout, xhat, rstd, y, mod, k_gate, coef, lnv, name):
    def fn(do, xh, rs, yb, m, ln):
        dxh = do * ln[0:1]
        m1 = jnp.mean(dxh, axis=-1, keepdims=True)
        m2 = jnp.mean(dxh * xh, axis=-1, keepdims=True)
        dz = rs * (dxh - m1 - xh * m2)
        dy = (coef * m[k_gate:k_gate + 1]) * dz
        s = jnp.concatenate([jnp.sum(do, axis=0, keepdims=True), jnp.sum(do * xh, axis=0, keepdims=True),
                             jnp.sum(coef * dz * yb, axis=0, keepdims=True)], axis=0)
        return dy, ALPHA * dz, s
    return rowwise(lay, name, fn, [dout, xhat, rstd, y], segs=[mod], vecs=[lnv],
                   outs=[(D, BF16), (D, F32)], sums=[(3, D)])


def mod_bwd(lay, dres, dhm, h, mod, k_scale, name):
    def fn(dr, dm, hb, m):
        dh = dr + dm * (1.0 + m[k_scale:k_scale + 1])
        s = jnp.concatenate([jnp.sum(dm, axis=0, keepdims=True), jnp.sum(dm * hb, axis=0, keepdims=True)], axis=0)
        return dh, s
    return rowwise(lay, name, fn, [dres, dhm, h], segs=[mod], outs=[(D, F32)], sums=[(2, D)])


def block_sums(lay, parts, name):
    nblk, r, w = parts.shape

    def body(p_ref, o_ref):
        acc = [None, None, None]
        for i in range(nblk):
            sg = 2 if i % lay.bps < lay.cb else i // lay.bps
            acc[sg] = p_ref[i] if acc[sg] is None else acc[sg] + p_ref[i]
        for k in range(3):
            o_ref[k] = acc[k]
        o_ref[3] = (acc[0] + acc[1]) + acc[2]

    return pl.pallas_call(body, name=name, out_shape=_sds((4, r, w), F32), in_specs=[VMEM_SPEC],
                          out_specs=VMEM_SPEC)(parts)


def mm_nn(a, b, name, out_dtype=F32, bias=None):
    m, k = a.shape
    n = b.shape[1]
    tm = _pick(m, (512, 256, 128, 64, 32, 16, 8))
    tn = _pick(n, (1024, 768, 640, 512, 384, 256, 128))

    def body(*refs):
        if bias is None:
            a_ref, b_ref, o_ref = refs
            o_ref[...] = _nn(a_ref[...].astype(BF16), b_ref[...].astype(BF16)).astype(o_ref.dtype)
        else:
            a_ref, b_ref, c_ref, o_ref = refs
            o_ref[...] = (_nn(a_ref[...].astype(BF16), b_ref[...].astype(BF16)) + c_ref[...]).astype(o_ref.dtype)

    in_specs = [pl.BlockSpec((tm, k), lambda i, j: (i, 0)), pl.BlockSpec((k, tn), lambda i, j: (0, j))]
    ops = [a, b]
    if bias is not None:
        in_specs.append(pl.BlockSpec((1, tn), lambda i, j: (0, j)))
        ops.append(bias)
    return pl.pallas_call(body, name=name, out_shape=_sds((m, n), out_dtype), grid=(m // tm, n // tn),
                          in_specs=in_specs, out_specs=pl.BlockSpec((tm, tn), lambda i, j: (i, j)),
                          compiler_params=_cp(("parallel", "parallel"), VMEM_BIG))(*ops)


def mm_nt(a, b, name, out_dtype=F32):
    m, k = a.shape
    n = b.shape[0]
    tm = _pick(m, (512, 256, 128, 64, 32, 16, 8))
    tn = _pick(n, (1024, 768, 640, 512, 384, 256, 128))

    def body(a_ref, b_ref, o_ref):
        o_ref[...] = _nt(a_ref[...].astype(BF16), b_ref[...].astype(BF16)).astype(o_ref.dtype)

    return pl.pallas_call(body, name=name, out_shape=_sds((m, n), out_dtype), grid=(m // tm, n // tn),
                          in_specs=[pl.BlockSpec((tm, k), lambda i, j: (i, 0)), pl.BlockSpec((tn, k), lambda i, j: (j, 0))],
                          out_specs=pl.BlockSpec((tm, tn), lambda i, j: (i, j)),
                          compiler_params=_cp(("parallel", "parallel"), VMEM_BIG))(a, b)


def mm_tn(a, b, name):
    t, m = a.shape
    n = b.shape[1]
    tk = _pick(t, (512, 256, 128, 64, 32, 16))
    tm = _pick(m, (512, 384, 256, 128))

    def body(a_ref, b_ref, o_ref):
        @pl.when(pl.program_id(1) == 0)
        def _():
            o_ref[...] = jnp.zeros_like(o_ref)
        o_ref[...] += _tn(a_ref[...].astype(BF16), b_ref[...].astype(BF16))

    return pl.pallas_call(body, name=name, out_shape=_sds((m, n), F32), grid=(m // tm, t // tk),
                          in_specs=[pl.BlockSpec((tk, tm), lambda i, k: (k, i)), pl.BlockSpec((tk, n), lambda i, k: (k, 0))],
                          out_specs=pl.BlockSpec((tm, n), lambda i, k: (i, 0)),
                          compiler_params=_cp(("parallel", "arbitrary"), VMEM_BIG))(a, b)


def ffn_up(lay, hm, wbuf, ig, iu, ns, name):
    tm = lay.tm

    def body(h_ref, wg_ref, wu_ref, g_ref, u_ref, a_ref):
        hb = h_ref[...]
        g = _nt(hb, wg_ref[0])
        u = _nt(hb, wu_ref[0])
        g_ref[0] = g.astype(BF16)
        u_ref[0] = u.astype(BF16)
        a_ref[0] = (g * _sigmoid(g) * u).astype(BF16)

    spec_o = pl.BlockSpec((1, tm, ns), lambda s, i: (s, i, 0))
    return pl.pallas_call(
        body, name=name, out_shape=[_sds((N_CHIP, lay.T, ns), BF16)] * 3, grid=(N_CHIP, lay.T // tm),
        in_specs=[pl.BlockSpec((tm, D), lambda s, i: (i, 0)),
                  pl.BlockSpec((1, ns, D), lambda s, i: (s, ig, 0)),
                  pl.BlockSpec((1, ns, D), lambda s, i: (s, iu, 0))],
        out_specs=[spec_o] * 3, compiler_params=_cp(("parallel", "parallel"), VMEM_BIG))(hm, wbuf, wbuf)


def slab_nn_acc(lay, zs, wbuf, idxs, ns, name):
    tm = lay.tm
    npair = len(zs)

    def body(*refs):
        o_ref = refs[-1]

        @pl.when(pl.program_id(1) == 0)
        def _():
            o_ref[...] = jnp.zeros_like(o_ref)
        acc = _nn(refs[0][0], refs[npair][0])
        for p in range(1, npair):
            acc += _nn(refs[p][0], refs[npair + p][0])
        o_ref[...] += acc

    in_specs = [pl.BlockSpec((1, tm, ns), lambda i, s: (s, i, 0)) for _ in zs]
    in_specs += [pl.BlockSpec((1, ns, D), functools.partial(lambda i, s, q: (s, q, 0), q=q)) for q in idxs]
    return pl.pallas_call(
        body, name=name, out_shape=_sds((lay.T, D), F32), grid=(lay.T // tm, N_CHIP), in_specs=in_specs,
        out_specs=pl.BlockSpec((tm, D), lambda i, s: (i, 0)),
        compiler_params=_cp(("parallel", "arbitrary"), VMEM_BIG))(*zs, *([wbuf] * npair))


def ffn_bwd_da(lay, dy, wbuf, idn, g, u, ns, name):
    tm = lay.tm

    def body(dy_ref, wd_ref, g_ref, u_ref, dg_ref, du_ref):
        da = _nt(dy_ref[...], wd_ref[0])
        gv = g_ref[0].astype(F32)
        uv = u_ref[0].astype(F32)
        sg = _sigmoid(gv)
        dg_ref[0] = (da * uv * (sg * (1.0 + gv * (1.0 - sg)))).astype(BF16)
        du_ref[0] = (da * (gv * sg)).astype(BF16)

    spec_z = pl.BlockSpec((1, tm, ns), lambda s, i: (s, i, 0))
    return pl.pallas_call(
        body, name=name, out_shape=[_sds((N_CHIP, lay.T, ns), BF16)] * 2, grid=(N_CHIP, lay.T // tm),
        in_specs=[pl.BlockSpec((tm, D), lambda s, i: (i, 0)), pl.BlockSpec((1, ns, D), lambda s, i: (s, idn, 0)),
                  spec_z, spec_z],
        out_specs=[spec_z] * 2, compiler_params=_cp(("parallel", "parallel"), VMEM_BIG))(dy, wbuf, g, u)


def slab_tn(lay, z, x, gbuf, idx, ns, name):
    tk = lay.tm

    def body(z_ref, x_ref, g_in, o_ref):
        del g_in

        @pl.when(pl.program_id(1) == 0)
        def _():
            o_ref[...] = jnp.zeros_like(o_ref)
        o_ref[0] += _tn(z_ref[0], x_ref[...])

    return pl.pallas_call(
        body, name=name, out_shape=_sds(gbuf.shape, F32), grid=(N_CHIP, lay.T // tk),
        in_specs=[pl.BlockSpec((1, tk, ns), lambda s, k: (s, k, 0)), pl.BlockSpec((tk, D), lambda s, k: (k, 0)), ANY],
        out_specs=pl.BlockSpec((1, ns, D), lambda s, k: (s, idx, 0)),
        input_output_aliases={2: 0}, compiler_params=_cp(("parallel", "arbitrary"), VMEM_BIG))(z, x, gbuf)


Q0, K0, V0, U0, QR0, KR0, PEXT = 0, 512, 640, 768, 1280, 1792, 1920


def rope_fwd(lay, p, cos, sin, name):
    def fn(pb, cs, sn):
        cs4 = jnp.concatenate([cs] * 4, axis=1)
        sn4 = jnp.concatenate([sn] * 4, axis=1)
        qr = pb[:, Q0:K0] * cs4 + pb[:, QR0:KR0] * sn4
        kr = pb[:, K0:V0] * cs + pb[:, KR0:PEXT] * sn
        return qr, kr, pb[:, V0:U0], pb[:, U0:QR0]
    return rowwise(lay, name, fn, [p, cos, sin], outs=[(ATT_W, BF16), (KV_W, BF16), (KV_W, BF16), (POOL_W, F32)])


def rope_bwd(lay, dqr, dkr, dv, du, cos, sin, name):
    def fn(dq, dk, dvb, dub, cs, sn):
        cs4 = jnp.concatenate([cs] * 4, axis=1)
        sn4 = jnp.concatenate([sn] * 4, axis=1)
        return (jnp.concatenate([dq * cs4, dk * cs, dvb, dub, dq * sn4, dk * sn], axis=1),)
    return rowwise(lay, name, fn, [dqr, dkr, dv, du, cos, sin], outs=[(PEXT, BF16)])[0]


def _attn_specs(lay):
    nbs, cbk, lbk = lay.PS // BLK, lay.C // BLK, lay.L // BLK

    def kv_map(j):
        return lambda s, n: (s * nbs + cbk + jnp.clip(n - cbk + j - 1, 0, lbk - 1), 0)

    win = [pl.BlockSpec((BLK, KV_W), kv_map(j)) for j in range(3)]
    ctx = pl.BlockSpec((lay.C, KV_W), lambda s, n: (s * (lay.PS // lay.C), 0))
    return nbs, cbk, lbk, win, ctx


def _attn_masks(n, cbk, lbk):
    row = lax.broadcasted_iota(jnp.int32, (BLK, BLK), 0)
    col = lax.broadcasted_iota(jnp.int32, (BLK, BLK), 1)
    m = n - cbk
    lat = n >= cbk
    valid = [lat & (m >= 1) & (col >= row), lat & (col >= 0), lat & (m <= lbk - 2) & (col <= row)]
    lane_lo = lax.broadcasted_iota(jnp.int32, (BLK, 2 * HEAD_DIM), 1) < HEAD_DIM
    return valid, lane_lo


def attn_fwd(lay, qr, kr, vb, sink_tab, name):
    nbs, cbk, lbk, win, ctx = _attn_specs(lay)

    def body(q_ref, k0, k1, k2, kc_ref, v0, v1, v2, vc_ref, sk_ref, o_ref, l_ref):
        n = pl.program_id(1)
        valid, lane_lo = _attn_masks(n, cbk, lbk)
        ks = [k0[...], k1[...], k2[...]]
        vs = [v0[...], v1[...], v2[...]]
        kc, vc = kc_ref[...], vc_ref[...]
        for p in range(4):
            q2 = q_ref[:, p * 128:(p + 1) * 128]
            outs, lses = [], []
            for hh in range(2):
                qm = jnp.where(lane_lo == (hh == 0), q2, jnp.zeros_like(q2))
                sk = sk_ref[p:p + 1, hh * HEAD_DIM:hh * HEAD_DIM + 1]
                sw = [jnp.where(valid[j], _nt(qm, ks[j]) * ATT_SCALE, NEG_INF) for j in range(3)]
                sc = _nt(qm, kc) * ATT_SCALE
                mx = jnp.maximum(jnp.maximum(jnp.maximum(sw[0].max(-1, keepdims=True), sw[1].max(-1, keepdims=True)),
                                             jnp.maximum(sw[2].max(-1, keepdims=True), sc.max(-1, keepdims=True))), sk)
                ew = [jnp.exp(s - mx) for s in sw]
                ec = jnp.exp(sc - mx)
                den = ew[0].sum(-1, keepdims=True) + ew[1].sum(-1, keepdims=True) + ew[2].sum(-1, keepdims=True)
                den = den + ec.sum(-1, keepdims=True) + jnp.exp(sk - mx)
                o = _nn((ec / den).astype(BF16), vc)
                for j in range(3):
                    o += _nn((ew[j] / den).astype(BF16), vs[j])
                outs.append(o)
                lses.append(jnp.broadcast_to(mx + jnp.log(den), (BLK, 128)))
            o_ref[:, p * 128:(p + 1) * 128] = jnp.where(lane_lo, outs[0], outs[1]).astype(o_ref.dtype)
            l_ref[:, p * 128:(p + 1) * 128] = jnp.where(lane_lo, lses[0], lses[1])

    qspec = pl.BlockSpec((BLK, ATT_W), lambda s, n: (s * nbs + n, 0))
    return pl.pallas_call(
        body, name=name, out_shape=[_sds((lay.T, ATT_W), BF16), _sds((lay.T, ATT_W), F32)], grid=(2, nbs),
        in_specs=[qspec] + win + [ctx] + win + [ctx] + [pl.BlockSpec((8, 128), lambda s, n: (0, 0))],
        out_specs=[qspec, qspec], compiler_params=_cp(("parallel", "parallel")))(qr, kr, kr, kr, kr, vb, vb, vb, vb, sink_tab)


def attn_bwd(lay, qr, kr, vb, sink_tab, lse, datt, name):
    nbs, cbk, lbk, win, ctx = _attn_specs(lay)
    C, PS = lay.C, lay.PS

    def body(q_ref, k0, k1, k2, kc_ref, v0, v1, v2, vc_ref, sk_ref, l_ref, do_ref, dq_ref, dk_ref, dv_ref, ds_ref):
        n = pl.program_id(1)
        valid, lane_lo = _attn_masks(n, cbk, lbk)

        @pl.when(n == 0)
        def _():
            dk_ref[...] = jnp.zeros_like(dk_ref)
            dv_ref[...] = jnp.zeros_like(dv_ref)
            ds_ref[...] = jnp.zeros_like(ds_ref)

        ks = [k0[...], k1[...], k2[...], kc_ref[...]]
        vs = [v0[...], v1[...], v2[...], vc_ref[...]]
        dks = [jnp.zeros((BLK, KV_W), F32)] * 3 + [jnp.zeros((C, KV_W), F32)]
        dvs = list(dks)
        for p in range(4):
            sl = slice(p * 128, (p + 1) * 128)
            q2 = q_ref[:, sl]
            do2 = do_ref[:, sl].astype(BF16)
            lse2 = l_ref[:, sl]
            dq_h, dd_h = [], []
            for hh in range(2):
                sel = lane_lo == (hh == 0)
                qm = jnp.where(sel, q2, jnp.zeros_like(q2))
                dom = jnp.where(sel, do2, jnp.zeros_like(do2))
                lse_h = lse2[:, hh * HEAD_DIM:hh * HEAD_DIM + 1]
                ps, dps = [], []
                for j in range(4):
                    s = _nt(qm, ks[j]) * ATT_SCALE
                    if j < 3:
                        s = jnp.where(valid[j], s, NEG_INF)
                    ps.append(jnp.exp(s - lse_h))
                    dps.append(_nt(dom, vs[j]))
                dd = (ps[0] * dps[0]).sum(-1, keepdims=True) + (ps[1] * dps[1]).sum(-1, keepdims=True)
                dd = dd + (ps[2] * dps[2]).sum(-1, keepdims=True) + (ps[3] * dps[3]).sum(-1, keepdims=True)
                dq = jnp.zeros((BLK, 128), F32)
                for j in range(4):
                    dsb = (ps[j] * (dps[j] - dd) * ATT_SCALE).astype(BF16)
                    dq += _nn(dsb, ks[j])
                    dks[j] = dks[j] + _tn(dsb, qm)
                    dvs[j] = dvs[j] + _tn(ps[j].astype(BF16), dom)
                dq_h.append(dq)
                dd_h.append(jnp.broadcast_to(dd, (BLK, 128)))
            dq_ref[:, sl] = jnp.where(lane_lo, dq_h[0], dq_h[1])
            dd2 = jnp.where(lane_lo, dd_h[0], dd_h[1])
            psink = jnp.exp(sk_ref[p:p + 1, :] - lse2)
            ds_ref[0, p:p + 1, :] += -jnp.sum(psink * dd2, axis=0, keepdims=True)
        dk_ref[0:C, :] += dks[3]
        dv_ref[0:C, :] += dvs[3]
        for j in range(3):
            r0 = pl.multiple_of((cbk + jnp.clip(n - cbk + j - 1, 0, lbk - 1)) * BLK, BLK)
            dk_ref[pl.ds(r0, BLK), :] += dks[j]
            dv_ref[pl.ds(r0, BLK), :] += dvs[j]

    qspec = pl.BlockSpec((BLK, ATT_W), lambda s, n: (s * nbs + n, 0))
    kvout = pl.BlockSpec((PS, KV_W), lambda s, n: (s, 0))
    return pl.pallas_call(
        body, name=name,
        out_shape=[_sds((lay.T, ATT_W), F32), _sds((lay.T, KV_W), F32), _sds((lay.T, KV_W), F32), _sds((2, 8, 128), F32)],
        grid=(2, nbs),
        in_specs=[qspec] + win + [ctx] + win + [ctx] + [pl.BlockSpec((8, 128), lambda s, n: (0, 0)), qspec, qspec],
        out_specs=[qspec, kvout, kvout, pl.BlockSpec((1, 8, 128), lambda s, n: (s, 0, 0))],
        compiler_params=_cp(("parallel", "arbitrary")))(qr, kr, kr, kr, kr, vb, vb, vb, vb, sink_tab, lse, datt)


def _winsum(x, r):
    n = x.shape[0]
    t = lax.broadcasted_iota(jnp.int32, x.shape, 0)
    acc = x
    for o in range(1, r + 1):
        acc = acc + jnp.where(t >= o, pltpu.roll(x, o, 0), 0.0) + jnp.where(t < n - o, pltpu.roll(x, n - o, 0), 0.0)
    return acc


def _wincount(n, r):
    t = lax.broadcasted_iota(jnp.int32, (n, 128), 0)
    return (jnp.minimum(t + r, n - 1) - jnp.maximum(t - r, 0) + 1).astype(F32)


def pool_fwd(lay, u, w_pool, scale, name):
    segs = [(0, lay.C), (lay.C, lay.L)]

    def body(u_ref, w_ref, s_ref, o_ref):
        for r0, n in segs:
            for g, wd in enumerate(POOL_WINDOWS):
                sl = slice(g * 128, (g + 1) * 128)
                x = u_ref[r0:r0 + n, sl]
                d = _winsum(x, wd // 2) / _wincount(n, wd // 2) - x
                y = _nn(d.astype(BF16), w_ref[g].astype(BF16)) * s_ref[:, sl]
                o_ref[r0:r0 + n, sl] = y.astype(o_ref.dtype)

    spec = pl.BlockSpec((lay.PS, POOL_W), lambda s: (s, 0))
    return pl.pallas_call(
        body, name=name, out_shape=_sds((lay.T, POOL_W), BF16), grid=(2,),
        in_specs=[spec, pl.BlockSpec(w_pool.shape, lambda s: (0, 0, 0)), pl.BlockSpec((1, POOL_W), lambda s: (0, 0))],
        out_specs=spec, compiler_params=_cp(("parallel",), VMEM_BIG))(u, w_pool, scale)


def pool_bwd(lay, u, dcat, w_pool, scale, name):
    segs = [(0, lay.C), (lay.C, lay.L)]

    def body(u_ref, dp_ref, w_ref, s_ref, du_ref, dw_ref, dsc_ref):
        for g, wd in enumerate(POOL_WINDOWS):
            sl = slice(g * 128, (g + 1) * 128)
            wb = w_ref[g].astype(BF16)
            dw = jnp.zeros((128, 128), F32)
            dsc = jnp.zeros((1, 128), F32)
            for r0, n in segs:
                x = u_ref[r0:r0 + n, sl]
                cnt = _wincount(n, wd // 2)
                d = (_winsum(x, wd // 2) / cnt - x).astype(BF16)
                dp = dp_ref[r0:r0 + n, sl]
                dsc += jnp.sum(_nn(d, wb) * dp, axis=0, keepdims=True)
                dyp = (dp * s_ref[:, sl]).astype(BF16)
                dw += _tn(d, dyp)
                dd = _nt(dyp, wb)
                du_ref[r0:r0 + n, sl] = _winsum(dd / cnt, wd // 2) - dd
            dw_ref[0, g] = dw
            dsc_ref[0, :, sl] = dsc

    spec = pl.BlockSpec((lay.PS, POOL_W), lambda s: (s, 0))
    return pl.pallas_call(
        body, name=name,
        out_shape=[_sds((lay.T, POOL_W), F32), _sds((2, 4, 128, 128), F32), _sds((2, 1, POOL_W), F32)], grid=(2,),
        in_specs=[spec, pl.BlockSpec((lay.PS, POOL_W), lambda s: (s, 1)), pl.BlockSpec(w_pool.shape, lambda s: (0, 0, 0)),
                  pl.BlockSpec((1, POOL_W), lambda s: (0, 0))],
        out_specs=[spec, pl.BlockSpec((1, 4, 128, 128), lambda s: (s, 0, 0, 0)), pl.BlockSpec((1, 1, POOL_W), lambda s: (s, 0, 0))],
        compiler_params=_cp(("parallel",), VMEM_BIG))(u, dcat, w_pool, scale)


CONV_OFFS = (-1, 0, 1, 2)
CW = 256


def _shift_rows(x, o):
    if o == 0:
        return x
    n = x.shape[0]
    t = lax.broadcasted_iota(jnp.int32, x.shape, 0)
    if o < 0:
        return jnp.where(t >= -o, pltpu.roll(x, -o, 0), 0.0)
    return jnp.where(t < n - o, pltpu.roll(x, n - o, 0), 0.0)


def conv_fwd(lay, p, col0, w, b, name):
    segs = [(0, lay.C), (lay.C, lay.L)]
    cb0 = col0 // CW

    def body(x_ref, w_ref, b_ref, o_ref):
        for r0, n in segs:
            x = x_ref[r0:r0 + n, :]
            y = jnp.broadcast_to(b_ref[...], x.shape)
            for k, o in enumerate(CONV_OFFS):
                y = y + _shift_rows(x, o) * w_ref[k:k + 1, :]
            o_ref[r0:r0 + n, :] = y

    return pl.pallas_call(
        body, name=name, out_shape=_sds((lay.T, D), F32), grid=(2, D // CW),
        in_specs=[pl.BlockSpec((lay.PS, CW), lambda s, j: (s, cb0 + j)), pl.BlockSpec((4, CW), lambda s, j: (0, j)),
                  pl.BlockSpec((1, CW), lambda s, j: (0, j))],
        out_specs=pl.BlockSpec((lay.PS, CW), lambda s, j: (s, j)),
        compiler_params=_cp(("parallel", "parallel")))(p, w, b)


def conv_bwd(lay, p, col0, w, duc, name):
    segs = [(0, lay.C), (lay.C, lay.L)]
    cb0 = col0 // CW

    def body(x_ref, w_ref, g_ref, du_ref, dw_ref, db_ref):
        dws = [jnp.zeros((1, CW), F32)] * 4
        db = jnp.zeros((1, CW), F32)
        for r0, n in segs:
            x = x_ref[r0:r0 + n, :]
            g = g_ref[r0:r0 + n, :]
            du = jnp.zeros_like(g)
            for k, o in enumerate(CONV_OFFS):
                du = du + _shift_rows(g, -o) * w_ref[k:k + 1, :]
                dws[k] = dws[k] + jnp.sum(g * _shift_rows(x, o), axis=0, keepdims=True)
            db = db + jnp.sum(g, axis=0, keepdims=True)
            du_ref[r0:r0 + n, :] = du
        dw_ref[0] = jnp.concatenate(dws, axis=0)
        db_ref[0] = db

    return pl.pallas_call(
        body, name=name, out_shape=[_sds((lay.T, D), F32), _sds((2, 4, D), F32), _sds((2, 1, D), F32)], grid=(2, D // CW),
        in_specs=[pl.BlockSpec((lay.PS, CW), lambda s, j: (s, cb0 + j)), pl.BlockSpec((4, CW), lambda s, j: (0, j)),
                  pl.BlockSpec((lay.PS, CW), lambda s, j: (s, j))],
        out_specs=[pl.BlockSpec((lay.PS, CW), lambda s, j: (s, j)), pl.BlockSpec((1, 4, CW), lambda s, j: (s, 0, j)),
                   pl.BlockSpec((1, 1, CW), lambda s, j: (s, 0, j))],
        compiler_params=_cp(("parallel", "parallel")))(p, w, duc)


def _softplus_neg(lam):
    z = -lam
    w = jnp.exp(-jnp.abs(z))
    log1p = jnp.where(w < 1e-2, w * (1.0 - w * (0.5 - w / 3.0)), jnp.log(1.0 + w))
    return jnp.maximum(z, 0.0) + log1p, -_sigmoid(z)


def _neg_expm1(x):
    series = -x * (1.0 + x * (0.5 + x * (1.0 / 6.0 + x * (1.0 / 24.0 + x * (1.0 / 120.0)))))
    return jnp.where(x > -0.05, series, 1.0 - jnp.exp(x))


def _lru_gates(x, xb, wa, wx, ba, bx, lam):
    r = _sigmoid(_nn(xb, wa.astype(BF16)) + ba)
    gi = _sigmoid(_nn(xb, wx.astype(BF16)) + bx)
    sp, dsp = _softplus_neg(lam)
    la = -LRU_C * r * sp
    a = jnp.exp(la)
    sq = jnp.sqrt(_neg_expm1(2.0 * la))
    return r, gi, sp, dsp, a, sq


def lru_coeffs(lay, uc, wa, wx, vec, name):
    tr = lay.tr

    def body(x_ref, wa_ref, wx_ref, v_ref, a_ref, b_ref):
        x = x_ref[...]
        xb = x.astype(BF16)
        for d in range(2):
            _, gi, _, _, a, sq = _lru_gates(x, xb, wa_ref[d, 0], wx_ref[d, 0], v_ref[d:d + 1], v_ref[2 + d:3 + d],
                                            v_ref[4 + d:5 + d])
            a_ref[d, 0] = a
            b_ref[d, 0] = sq * (gi * x)

    wspec = pl.BlockSpec((2, 1, 128, 128), lambda i, h: (0, h, 0, 0))
    ospec = pl.BlockSpec((2, 1, tr, 128), lambda i, h: (0, h, i, 0))
    return pl.pallas_call(
        body, name=name, out_shape=[_sds((2, 8, lay.T, 128), F32)] * 2, grid=(lay.nblk, 8),
        in_specs=[pl.BlockSpec((tr, 128), lambda i, h: (i, h)), wspec, wspec, pl.BlockSpec((6, 128), lambda i, h: (0, h))],
        out_specs=[ospec, ospec], compiler_params=_cp(("parallel", "parallel")))(uc, wa, wx, vec)


def lru_coeffs_bwd(lay, uc, wa, wx, vec, da, db, name):
    tr = lay.tr

    def body(x_ref, wa_ref, wx_ref, v_ref, da_ref, db_ref, dx_ref, dwa_ref, dwx_ref, dv_ref):
        @pl.when(pl.program_id(1) == 0)
        def _():
            dwa_ref[...] = jnp.zeros_like(dwa_ref)
            dwx_ref[...] = jnp.zeros_like(dwx_ref)
            dv_ref[...] = jnp.zeros_like(dv_ref)

        x = x_ref[...]
        xb = x.astype(BF16)
        dx = jnp.zeros_like(x)
        for d in range(2):
            wab, wxb = wa_ref[d, 0].astype(BF16), wx_ref[d, 0].astype(BF16)
            r, gi, sp, dsp, a, sq = _lru_gates(x, xb, wa_ref[d, 0], wx_ref[d, 0], v_ref[d:d + 1], v_ref[2 + d:3 + d],
                                               v_ref[4 + d:5 + d])
            dbv, dav = db_ref[d, 0], da_ref[d, 0]
            t1 = dbv * sq
            dgi = t1 * x
            dx = dx + t1 * gi
            dla = dav * a - (dbv * gi * x) * (a * a) / sq
            dr = dla * (-LRU_C * sp)
            dlam = jnp.sum(dla * (-LRU_C * r), axis=0, keepdims=True) * dsp
            dpa = dr * r * (1.0 - r)
            dpx = dgi * gi * (1.0 - gi)
            dpab, dpxb = dpa.astype(BF16), dpx.astype(BF16)
            dwa_ref[d, 0] += _tn(xb, dpab)
            dwx_ref[d, 0] += _tn(xb, dpxb)
            dx = dx + _nt(dpab, wab) + _nt(dpxb, wxb)
            dv_ref[d:d + 1, :] += jnp.sum(dpa, axis=0, keepdims=True)
            dv_ref[2 + d:3 + d, :] += jnp.sum(dpx, axis=0, keepdims=True)
            dv_ref[4 + d:5 + d, :] += dlam
        dx_ref[...] = dx

    wspec = pl.BlockSpec((2, 1, 128, 128), lambda h, i: (0, h, 0, 0))
    gspec = pl.BlockSpec((2, 1, tr, 128), lambda h, i: (0, h, i, 0))
    vspec = pl.BlockSpec((6, 128), lambda h, i: (0, h))
    xspec = pl.BlockSpec((tr, 128), lambda h, i: (i, h))
    return pl.pallas_call(
        body, name=name,
        out_shape=[_sds((lay.T, D), F32), _sds((2, 8, 128, 128), F32), _sds((2, 8, 128, 128), F32), _sds((6, D), F32)],
        grid=(8, lay.nblk), in_specs=[xspec, wspec, wspec, vspec, gspec, gspec],
        out_specs=[xspec, wspec, wspec, vspec], compiler_params=_cp(("parallel", "arbitrary")))(uc, wa, wx, vec, da, db)


GB = 4


def _flip8(x):
    return jnp.concatenate([x[7 - j:8 - j] for j in range(8)], axis=0)


def _chunk_starts(a_tot, b_tot, first):
    rows, cur = [], first
    for j in range(8):
        rows.append(cur)
        cur = a_tot[j:j + 1] * cur + b_tot[j:j + 1]
    return jnp.concatenate(rows, axis=0), cur


def lru_scan(lay, a, b, name):
    segs = [(0, lay.C), (lay.C, lay.L)]

    def body(a_ref, b_ref, s_ref, hs_ref):
        rev = pl.program_id(1) == 1

        def row(base, st, k):
            return base + jnp.where(rev, st - 1 - k, k)

        state = [jnp.zeros((1, 128), F32)] * GB
        for si, (base, n) in enumerate(segs):
            st = n // 8

            def p1(k, c):
                out = []
                for g in range(GB):
                    at = a_ref[0, g, pl.ds(row(base, st, k), 8, stride=st), :]
                    bt = b_ref[0, g, pl.ds(row(base, st, k), 8, stride=st), :]
                    out += [at * c[2 * g], at * c[2 * g + 1] + bt]
                return tuple(out)

            init = tuple(v for _ in range(GB) for v in (jnp.ones((8, 128), F32), jnp.zeros((8, 128), F32)))
            tot = lax.fori_loop(0, st, p1, init)
            starts = []
            for g in range(GB):
                at, bt = tot[2 * g], tot[2 * g + 1]
                at, bt = jnp.where(rev, _flip8(at), at), jnp.where(rev, _flip8(bt), bt)
                hs, state[g] = _chunk_starts(at, bt, state[g])
                hs = jnp.where(rev, _flip8(hs), hs)
                hs_ref[0, g, si * 8:(si + 1) * 8, :] = hs
                starts.append(hs)

            def p2(k, c):
                out = []
                for g in range(GB):
                    r = row(base, st, k)
                    h = a_ref[0, g, pl.ds(r, 8, stride=st), :] * c[g] + b_ref[0, g, pl.ds(r, 8, stride=st), :]
                    s_ref[0, g, pl.ds(r, 8, stride=st), :] = h
                    out.append(h)
                return tuple(out)

            lax.fori_loop(0, st, p2, tuple(starts))

    spec = pl.BlockSpec((1, GB, lay.PS, 128), lambda s, d, hb: (d, hb, s, 0))
    return pl.pallas_call(
        body, name=name, out_shape=[_sds((2, 8, lay.T, 128), F32), _sds((2, 8, 32, 128), F32)], grid=(2, 2, 8 // GB),
        in_specs=[spec, spec], out_specs=[spec, pl.BlockSpec((1, GB, 16, 128), lambda s, d, hb: (d, hb, s, 0))],
        compiler_params=_cp(("parallel", "parallel", "parallel"), VMEM_BIG))(a, b)


def lru_scan_bwd(lay, a, s, hs, dy, name):
    segs = [(0, lay.C), (lay.C, lay.L)]

    def body(a_ref, s_ref, hs_ref, g_ref, da_ref, db_ref):
        rev = pl.program_id(1) == 1

        def row(base, st, kk):
            return base + jnp.where(rev, st - 1 - kk, kk)

        carry = [jnp.zeros((1, 128), F32)] * GB
        for si in (1, 0):
            base, n = segs[si]
            st = n // 8

            def p1(k, c):
                r = row(base, st, st - 1 - k)
                out = []
                for g in range(GB):
                    at = a_ref[0, g, pl.ds(r, 8, stride=st), :]
                    gt = g_ref[g, pl.ds(r, 8, stride=st), :]
                    out += [at * c[2 * g], at * (c[2 * g + 1] + gt)]
                return tuple(out)

            init = tuple(v for _ in range(GB) for v in (jnp.ones((8, 128), F32), jnp.zeros((8, 128), F32)))
            tot = lax.fori_loop(0, st, p1, init)
            m_in = []
            for g in range(GB):
                at, bt = tot[2 * g], tot[2 * g + 1]
                at, bt = jnp.where(rev, at, _flip8(at)), jnp.where(rev, bt, _flip8(bt))
                ms, carry[g] = _chunk_starts(at, bt, carry[g])
                m_in.append(jnp.where(rev, ms, _flip8(ms)))

            def step(kk, c, prev_of):
                r = row(base, st, kk)
                out = []
                for g in range(GB):
                    lamv = g_ref[g, pl.ds(r, 8, stride=st), :] + c[g]
                    db_ref[0, g, pl.ds(r, 8, stride=st), :] = lamv
                    da_ref[0, g, pl.ds(r, 8, stride=st), :] = lamv * prev_of(g)
                    out.append(a_ref[0, g, pl.ds(r, 8, stride=st), :] * lamv)
                return tuple(out)

            def p2(k, c):
                kk = st - 1 - k
                rp = row(base, st, kk - 1)
                return step(kk, c, lambda g: s_ref[0, g, pl.ds(rp, 8, stride=st), :])

            c = lax.fori_loop(0, st - 1, p2, tuple(m_in))
            step(0, c, lambda g: hs_ref[0, g, si * 8:(si + 1) * 8, :])

    spec = pl.BlockSpec((1, GB, lay.PS, 128), lambda s, d, hb: (d, hb, s, 0))
    return pl.pallas_call(
        body, name=name, out_shape=[_sds((2, 8, lay.T, 128), F32)] * 2, grid=(2, 2, 8 // GB),
        in_specs=[spec, spec, pl.BlockSpec((1, GB, 16, 128), lambda s, d, hb: (d, hb, s, 0)),
                  pl.BlockSpec((GB, lay.PS, 128), lambda s, d, hb: (hb, s, 0))],
        out_specs=[spec, spec], compiler_params=_cp(("parallel", "parallel", "parallel"), VMEM_BIG))(a, s, hs, dy)


def _gelu(x):
    k = math.sqrt(2.0 / math.pi)
    t = jnp.tanh(k * (x + 0.044715 * x * x * x))
    return 0.5 * x * (1.0 + t), 0.5 * (1.0 + t) + 0.5 * x * (1.0 - t * t) * k * (1.0 + 3 * 0.044715 * x * x)


def lru_gate(lay, p, s, name):
    tr = lay.tr

    def body(g_ref, s_ref, o_ref):
        o_ref[...] = (_gelu(g_ref[...])[0] * (s_ref[0, 0] + s_ref[1, 0])).astype(o_ref.dtype)

    return pl.pallas_call(
        body, name=name, out_shape=_sds((lay.T, D), BF16), grid=(lay.nblk, 8),
        in_specs=[pl.BlockSpec((tr, 128), lambda i, h: (i, h)), pl.BlockSpec((2, 1, tr, 128), lambda i, h: (0, h, i, 0))],
        out_specs=pl.BlockSpec((tr, 128), lambda i, h: (i, h)), compiler_params=_cp(("parallel", "parallel")))(p, s)


def lru_gate_bwd(lay, p, s, do, name):
    tr = lay.tr

    def body(g_ref, s_ref, do_ref, dg_ref, dy_ref):
        ge, dge = _gelu(g_ref[...])
        dov = do_ref[...]
        dg_ref[...] = dov * (s_ref[0, 0] + s_ref[1, 0]) * dge
        dy_ref[0] = dov * ge

    xspec = pl.BlockSpec((tr, 128), lambda i, h: (i, h))
    return pl.pallas_call(
        body, name=name, out_shape=[_sds((lay.T, D), F32), _sds((8, lay.T, 128), F32)], grid=(lay.nblk, 8),
        in_specs=[xspec, pl.BlockSpec((2, 1, tr, 128), lambda i, h: (0, h, i, 0)), xspec],
        out_specs=[xspec, pl.BlockSpec((1, tr, 128), lambda i, h: (h, i, 0))],
        compiler_params=_cp(("parallel", "parallel")))(p, s, do)


def silu_rows(x, name):
    def body(x_ref, o_ref):
        v = x_ref[...]
        o_ref[...] = (v * _sigmoid(v)).astype(o_ref.dtype)
    return pl.pallas_call(body, name=name, out_shape=_sds(x.shape, BF16), in_specs=[VMEM_SPEC], out_specs=VMEM_SPEC)(x)


def mod_grad_rows(gath, name):
    w = gath.shape[-1]

    def body(g_ref, dm_ref, db_ref):
        dm_ref[...] = jnp.zeros_like(dm_ref)
        for l in range(2):
            ctx = g_ref[0, 3 * l + 2:3 * l + 3, :]
            tot = g_ref[0, 3 * l:3 * l + 1, :] + g_ref[0, 3 * l + 1:3 * l + 2, :]
            for k in range(8):
                dm_ref[l, 2 * k:2 * k + 2, :] = g_ref[k, 3 * l:3 * l + 2, :]
                if k:
                    ctx = ctx + g_ref[k, 3 * l + 2:3 * l + 3, :]
                    tot = tot + (g_ref[k, 3 * l:3 * l + 1, :] + g_ref[k, 3 * l + 1:3 * l + 2, :])
            dm_ref[l, 16:17, :] = ctx
            db_ref[l:l + 1, :] = tot + ctx

    return pl.pallas_call(body, name=name, out_shape=[_sds((2, 32, w), F32), _sds((2, w), F32)],
                          in_specs=[VMEM_SPEC], out_specs=[VMEM_SPEC, VMEM_SPEC])(gath)


def cctx_grad(p0, p1, c_ctx, name):
    def body(a_ref, b_ref, c_ref, o_ref):
        cv = c_ref[...]
        sg = _sigmoid(cv)
        o_ref[...] = 0.5 * (a_ref[0:1, :] + b_ref[0:1, :]) * (sg * (1.0 + cv * (1.0 - sg)))
    return pl.pallas_call(body, name=name, out_shape=_sds((1, D), F32), in_specs=[VMEM_SPEC] * 3,
                          out_specs=VMEM_SPEC)(p0, p1, c_ctx)


def loss_and_grad(lay, h, tgt, name):
    def fn(hb, tb):
        lat = (pl.program_id(0) % lay.bps) >= lay.cb
        e = jnp.where(lat, hb - tb, 0.0)
        return e * (1.0 / D), jnp.sum(e * e, axis=0, keepdims=True) * (0.5 / D)
    return rowwise(lay, name, fn, [h, tgt], outs=[(D, F32)], sums=[(1, D)])


def adamw(w, g, m, v, name):
    shape = w.shape
    w2, g2, m2, v2 = (t.reshape(-1, shape[-1]) for t in (w, g, m, v))
    rows, width = w2.shape
    tr = 256 if rows % 256 == 0 else rows
    c1 = 1.0 - ADAM_B1 ** ADAM_STEP
    c2 = 1.0 - ADAM_B2 ** ADAM_STEP

    def body(w_ref, g_ref, m_ref, v_ref, d_ref, mo_ref, vo_ref):
        gv = g_ref[...]
        mn = ADAM_B1 * m_ref[...] + (1.0 - ADAM_B1) * gv
        vn = ADAM_B2 * v_ref[...] + (1.0 - ADAM_B2) * (gv * gv)
        d_ref[...] = -ADAM_LR * ((mn / c1) / (jnp.sqrt(vn / c2) + ADAM_EPS) + ADAM_WD * w_ref[...])
        mo_ref[...] = mn
        vo_ref[...] = vn

    spec = pl.BlockSpec((tr, width), lambda i: (i, 0))
    d, mn, vn = pl.pallas_call(body, name=name, out_shape=[_sds((rows, width), F32)] * 3, grid=(rows // tr,),
                               in_specs=[spec] * 4, out_specs=[spec] * 3, compiler_params=_cp(("parallel",)))(w2, g2, m2, v2)
    return d.reshape(shape), mn.reshape(shape), vn.reshape(shape)


HEAD_PERM = (0, 4, 1, 5, 2, 6, 3, 7)


def _rot_rows(wt):
    return jnp.concatenate([-wt[32:64], wt[0:32]], axis=0)


def _unrot_rows(g):
    return jnp.concatenate([g[32:64], -g[0:32]], axis=0)


def _heads(a, n):
    return [a[64 * i:64 * (i + 1)] for i in range(n)]


def kernel(x, c, ctx, c_ctx, w_mod, b_mod, ln_g, ln_b, ffn_w_gate, ffn_w_up, ffn_w_down, mix_ab_w_in, attn_sink, pool_w, pool_scale, mix_ab_w_out, lru_w_in, lru_conv_w, lru_conv_b, lru_wa, lru_ba, lru_wx, lru_bx, lru_lambda, lru_w_out, loss_target, m_c_ctx, m_w_mod, m_b_mod, m_ln_g, m_ln_b, m_ffn_w_gate, m_ffn_w_up, m_ffn_w_down, m_mix_ab_w_in, m_attn_sink, m_pool_w, m_pool_scale, m_mix_ab_w_out, m_lru_w_in, m_lru_conv_w, m_lru_conv_b, m_lru_wa, m_lru_ba, m_lru_wx, m_lru_bx, m_lru_lambda, m_lru_w_out, v_c_ctx, v_w_mod, v_b_mod, v_ln_g, v_ln_b, v_ffn_w_gate, v_ffn_w_up, v_ffn_w_down, v_mix_ab_w_in, v_attn_sink, v_pool_w, v_pool_scale, v_mix_ab_w_out, v_lru_w_in, v_lru_conv_w, v_lru_conv_b, v_lru_wa, v_lru_ba, v_lru_wx, v_lru_bx, v_lru_lambda, v_lru_w_out):
    n_lat, n_ctx = x.shape[1], ctx.shape[1]
    lay = Layout(n_ctx, n_lat)
    T = lay.T
    ns = ffn_w_gate.shape[-1]
    n_li, n_ai = lru_w_in.shape[-1], mix_ab_w_in.shape[-1]
    n_ao, n_lo = mix_ab_w_out.shape[1], lru_w_out.shape[1]
    wm = w_mod.shape[-1]
    dsh = ln_g.shape[-1]
    mx, my, mc = lax.axis_index("x"), lax.axis_index("y"), lax.axis_index("c")
    chip = 2 * mx + my
    me = 2 * chip + mc

    c_all = all_gather8(c, "ag8_c").reshape(16, D)
    cc = jnp.concatenate([c_all, c_ctx[None, :], jnp.zeros((15, D), F32)], axis=0)
    sc = silu_rows(cc, "silu_c")
    slabs = []
    for l in range(DEPTH):
        bias = lax.dynamic_slice(b_mod[l][None, :], (0, chip * wm), (1, wm))
        slabs.append(mm_nn(sc, w_mod[l], f"mod_mm{l}", bias=bias))
    modg = all_gather_chips(jnp.stack(slabs), "ag_mod")
    modtab = []
    for l in range(DEPTH):
        full = jnp.transpose(modg[:, l], (1, 0, 2)).reshape(32, N_CHIP * wm)
        mine = lax.dynamic_slice(full, (2 * me, 0), (2, N_CHIP * wm))
        modtab.append(jnp.concatenate([mine, full[16:17]], axis=0).reshape(3, N_MOD, D))

    small = jnp.concatenate([ln_g.reshape(6, dsh), ln_b.reshape(6, dsh), lru_conv_w[0], lru_conv_b, lru_ba[0],
                             lru_bx[0], lru_lambda[0], jnp.zeros((9, dsh), F32)], axis=0)
    small = all_gather_chips(small.reshape(2, 16, dsh), "ag_small").reshape(N_CHIP, 32, dsh)
    small = jnp.transpose(small, (1, 0, 2)).reshape(32, D)
    ln_g_f, ln_b_f = small[0:6].reshape(2, 3, D), small[6:12].reshape(2, 3, D)
    conv_w_f, conv_b_f = small[12:16], small[16:17]
    lru_vec = small[17:23]

    ffn_sh = jnp.stack([jnp.swapaxes(ffn_w_gate, -1, -2), jnp.swapaxes(ffn_w_up, -1, -2), ffn_w_down], axis=2)
    ffn_sh = ffn_sh.astype(BF16).reshape(2, 6 * ns, D)
    wbuf = all_gather_chips(ffn_sh, "ag_ffn").reshape(N_CHIP, 12 * ns, D)
    mix_sh = jnp.concatenate([lru_w_in[0].T, mix_ab_w_in[0].T, mix_ab_w_out[0], lru_w_out[0]], axis=0).astype(BF16)
    n_mix = n_li + n_ai + n_ao + n_lo
    mixw = all_gather_chips(mix_sh.reshape(2, n_mix // 2, D), "ag_mix").reshape(N_CHIP, n_mix, D)
    o1, o2, o3 = n_li, n_li + n_ai, n_li + n_ai + n_ao
    lru_in_t = mixw[:, 0:o1].reshape(N_CHIP * n_li, D)
    ab_in_t = mixw[:, o1:o2].reshape(N_CHIP * n_ai, D)
    ab_out = mixw[:, o2:o3].reshape(N_CHIP * n_ao, D)
    lru_out = mixw[:, o3:].reshape(N_CHIP * n_lo, D)
    qh, kh = _heads(ab_in_t[Q0:K0], N_HEADS), _heads(ab_in_t[K0:V0], N_KV)
    w_ext_t = jnp.concatenate([qh[h] for h in HEAD_PERM] + [ab_in_t[K0:QR0]]
                              + [_rot_rows(qh[h]) for h in HEAD_PERM] + [_rot_rows(t) for t in kh], axis=0)
    oh = _heads(ab_out[0:ATT_W], N_HEADS)
    w_out_ext = jnp.concatenate([oh[h] for h in HEAD_PERM] + [ab_out[ATT_W:]], axis=0)

    t = jnp.arange(n_lat)
    inv = ROPE_THETA ** (-jnp.arange(16, dtype=F32) / 16.0)
    ang = jnp.concatenate([(t // GRID_W).astype(F32)[:, None] * inv, (t % GRID_W).astype(F32)[:, None] * inv], axis=-1)
    cos1 = jnp.concatenate([jnp.ones((n_ctx, 32), F32), jnp.cos(ang)], axis=0)
    sin1 = jnp.concatenate([jnp.zeros((n_ctx, 32), F32), jnp.sin(ang)], axis=0)
    cos_t = jnp.tile(cos1, (2, 4))
    sin_t = jnp.tile(sin1, (2, 4))
    sk = attn_sink[0]
    sink_tab = jnp.concatenate([jnp.repeat(jnp.stack([sk[:4], sk[4:]], axis=1), HEAD_DIM, axis=1),
                                jnp.zeros((4, 128), F32)], axis=0)
    pscale = pool_scale.reshape(1, POOL_W)

    h0 = jnp.concatenate([ctx, x], axis=1).reshape(T, D)
    tgt = jnp.concatenate([jnp.zeros_like(ctx), loss_target], axis=1).reshape(T, D)

    def lnv(l, j):
        return jnp.stack([ln_g_f[l, j], ln_b_f[l, j]])

    def fq(l, f, kind):
        return (l * 2 + f) * 3 + kind

    def ffn_fwd(h, l, f, k0, j):
        tag = f"l{l}f{f}"
        hm = modulate(lay, h, modtab[l], k0, k0 + 1, f"mod_{tag}")
        g, u, a = ffn_up(lay, hm, wbuf, fq(l, f, 0), fq(l, f, 1), ns, f"ffn_up_{tag}")
        y = slab_nn_acc(lay, [a], wbuf, [fq(l, f, 2)], ns, f"ffn_down_{tag}")
        out, xhat, rstd = resid_ln(lay, h, y, modtab[l], k0 + 2, 0.5, lnv(l, j), f"ln_{tag}")
        return out, dict(h=h, hm=hm, g=g, u=u, a=a, y=y, xhat=xhat, rstd=rstd)

    h1, r_f00 = ffn_fwd(h0, 0, 0, 0, 0)
    hm_a = modulate(lay, h1, modtab[0], 3, 4, "mod_mixa")
    p_a = mm_nt(hm_a, w_ext_t, "mixa_in")
    qr, kr, vb, u_a = rope_fwd(lay, p_a, cos_t, sin_t, "rope")
    att, lse = attn_fwd(lay, qr, kr, vb, sink_tab, "attn")
    pool = pool_fwd(lay, u_a, pool_w[0], pscale, "pool")
    cat = jnp.concatenate([att, pool], axis=1)
    y_a = mm_nn(cat, w_out_ext, "mixa_out")
    h2, xhat_a, rstd_a = resid_ln(lay, h1, y_a, modtab[0], 5, 1.0, lnv(0, 1), "ln_mixa")
    h3, r_f01 = ffn_fwd(h2, 0, 1, 6, 2)

    h4, r_f10 = ffn_fwd(h3, 1, 0, 0, 0)
    hm_c = modulate(lay, h4, modtab[1], 3, 4, "mod_mixc")
    p_c = mm_nt(hm_c, lru_in_t, "mixc_in")
    uc = conv_fwd(lay, p_c, D, conv_w_f, conv_b_f, "conv")
    a_c, b_c = lru_coeffs(lay, uc, lru_wa[0], lru_wx[0], lru_vec, "lru_coef")
    s_c, hs_c = lru_scan(lay, a_c, b_c, "lru_scan")
    o_c = lru_gate(lay, p_c, s_c, "lru_gate")
    y_c = mm_nn(o_c, lru_out, "mixc_out")
    h5, xhat_c, rstd_c = resid_ln(lay, h4, y_c, modtab[1], 5, 1.0, lnv(1, 1), "ln_mixc")
    h6, r_f11 = ffn_fwd(h5, 1, 1, 6, 2)

    dh, lparts = loss_and_grad(lay, h6, tgt, "loss")
    loss = lax.psum(jnp.sum(lparts), ("x", "y", "c"))

    gbuf = lax.empty((N_CHIP, 12 * ns, D), F32)
    dln = {}
    dms = {}

    def ffn_bwd(dout, r, l, f, k0, j, gbuf):
        tag = f"l{l}f{f}"
        dy, dres, s1 = ln_bwd(lay, dout, r["xhat"], r["rstd"], r["y"], modtab[l], k0 + 2, 0.5, lnv(l, j), f"lnb_{tag}")
        dg, du = ffn_bwd_da(lay, dy, wbuf, fq(l, f, 2), r["g"], r["u"], ns, f"ffn_da_{tag}")
        gbuf = slab_tn(lay, r["a"], dy, gbuf, fq(l, f, 2), ns, f"ffn_dwd_{tag}")
        gbuf = slab_tn(lay, dg, r["hm"], gbuf, fq(l, f, 0), ns, f"ffn_dwg_{tag}")
        gbuf = slab_tn(lay, du, r["hm"], gbuf, fq(l, f, 1), ns, f"ffn_dwu_{tag}")
        dhm = slab_nn_acc(lay, [dg, du], wbuf, [fq(l, f, 0), fq(l, f, 1)], ns, f"ffn_dh_{tag}")
        dh_in, s2 = mod_bwd(lay, dres, dhm, r["h"], modtab[l], k0 + 1, f"modb_{tag}")
        dln[(l, j)] = block_sums(lay, s1, f"bs_ln_{tag}")
        dms[(l, k0)] = block_sums(lay, s2, f"bs_mod_{tag}")
        return dh_in, gbuf

    dh, gbuf = ffn_bwd(dh, r_f11, 1, 1, 6, 2, gbuf)
    dy, dres, s1 = ln_bwd(lay, dh, xhat_c, rstd_c, y_c, modtab[1], 5, 1.0, lnv(1, 1), "lnb_mixc")
    do_c = mm_nt(dy, lru_out, "mixc_out_dx")
    g_lru_out = mm_tn(o_c, dy, "mixc_out_dw")
    dgate, dyg = lru_gate_bwd(lay, p_c, s_c, do_c, "lru_gate_b")
    da_c, db_c = lru_scan_bwd(lay, a_c, s_c, hs_c, dyg, "lru_scan_b")
    duc, g_wa, g_wx, g_vec = lru_coeffs_bwd(lay, uc, lru_wa[0], lru_wx[0], lru_vec, da_c, db_c, "lru_coef_b")
    du_c, g_cw, g_cb = conv_bwd(lay, p_c, D, conv_w_f, duc, "conv_b")
    dp_c = jnp.concatenate([dgate, du_c], axis=1).astype(BF16)
    dhm = mm_nn(dp_c, lru_in_t, "mixc_in_dx")
    g_lru_in_t = mm_tn(dp_c, hm_c, "mixc_in_dw")
    dh, s2 = mod_bwd(lay, dres, dhm, h4, modtab[1], 4, "modb_mixc")
    dln[(1, 1)] = block_sums(lay, s1, "bs_ln_mixc")
    dms[(1, 3)] = block_sums(lay, s2, "bs_mod_mixc")
    dh, gbuf = ffn_bwd(dh, r_f10, 1, 0, 0, 0, gbuf)

    dh, gbuf = ffn_bwd(dh, r_f01, 0, 1, 6, 2, gbuf)
    dy, dres, s1 = ln_bwd(lay, dh, xhat_a, rstd_a, y_a, modtab[0], 5, 1.0, lnv(0, 1), "lnb_mixa")
    dcat = mm_nt(dy, w_out_ext, "mixa_out_dx")
    g_out_ext = mm_tn(cat, dy, "mixa_out_dw")
    dqr, dkr, dv, g_sink = attn_bwd(lay, qr, kr, vb, sink_tab, lse, dcat, "attn_b")
    du_a, g_pw, g_ps = pool_bwd(lay, u_a, dcat, pool_w[0], pscale, "pool_b")
    dp_a = rope_bwd(lay, dqr, dkr, dv, du_a, cos_t, sin_t, "rope_b")
    dhm = mm_nn(dp_a, w_ext_t, "mixa_in_dx")
    g_ext_t = mm_tn(dp_a, hm_a, "mixa_in_dw")
    dh, s2 = mod_bwd(lay, dres, dhm, h1, modtab[0], 4, "modb_mixa")
    dln[(0, 1)] = block_sums(lay, s1, "bs_ln_mixa")
    dms[(0, 3)] = block_sums(lay, s2, "bs_mod_mixa")
    dh, gbuf = ffn_bwd(dh, r_f00, 0, 0, 0, 0, gbuf)
    grad_x = dh.reshape(2, lay.PS, D)[:, n_ctx:]

    rows = []
    for l in range(DEPTH):
        per_k = []
        for k0, j in ((0, 0), (3, 1), (6, 2)):
            per_k += [dms[(l, k0)][:3, 0], dms[(l, k0)][:3, 1], dln[(l, j)][:3, 2]]
        rows.append(jnp.stack(per_k, axis=1).reshape(3, N_MOD * D))
    dmod_loc = jnp.concatenate(rows + [jnp.zeros((2, N_MOD * D), F32)], axis=0)
    dmod_all, g_b_mod = mod_grad_rows(all_gather8(dmod_loc, "ag8_dmod"), "dmod_rows")
    g_w_mod, cparts = [], []
    for l in range(DEPTH):
        dcol = lax.dynamic_slice(dmod_all[l], (0, chip * wm), (32, wm))
        g_w_mod.append(mm_tn(sc, dcol.astype(BF16), f"wmod_dw{l}"))
        cparts.append(mm_nt(dcol[16:32], w_mod[l], f"cctx_dx{l}"))
    g_w_mod = jnp.stack(g_w_mod)
    g_cctx = cctx_grad(cparts[0], cparts[1], c_ctx[None, :], "cctx_grad")

    gq = _heads(g_ext_t[Q0:K0], N_HEADS)
    gqr = _heads(g_ext_t[QR0:KR0], N_HEADS)
    g_q = [None] * N_HEADS
    for i, h in enumerate(HEAD_PERM):
        g_q[h] = gq[i] + _unrot_rows(gqr[i])
    gk = [a + _unrot_rows(b) for a, b in zip(_heads(g_ext_t[K0:V0], N_KV), _heads(g_ext_t[KR0:PEXT], N_KV))]
    g_ab_in_t = jnp.concatenate(g_q + gk + [g_ext_t[V0:QR0]], axis=0)
    go = _heads(g_out_ext[0:ATT_W], N_HEADS)
    g_o = [None] * N_HEADS
    for i, h in enumerate(HEAD_PERM):
        g_o[h] = go[i]
    g_ab_out = jnp.concatenate(g_o + [g_out_ext[ATT_W:]], axis=0)
    mix_g = jnp.concatenate([g_lru_in_t.reshape(N_CHIP, n_li, D), g_ab_in_t.reshape(N_CHIP, n_ai, D),
                             g_ab_out.reshape(N_CHIP, n_ao, D), g_lru_out.reshape(N_CHIP, n_lo, D)], axis=1)

    g_ln_g = jnp.stack([jnp.stack([dln[(l, j)][3, 1] for j in range(3)]) for l in range(DEPTH)])
    g_ln_b = jnp.stack([jnp.stack([dln[(l, j)][3, 0] for j in range(3)]) for l in range(DEPTH)])
    sink_row = jnp.sum(g_sink, axis=0)[:4]
    g_sink8 = jnp.concatenate([sink_row[:, 0], sink_row[:, HEAD_DIM]])
    misc = jnp.concatenate([g_sink8, jnp.sum(g_ps, axis=0).reshape(POOL_W), jnp.zeros((D - 8 - POOL_W,), F32)])
    small_g = jnp.concatenate([
        g_ln_g.reshape(6, D), g_ln_b.reshape(6, D), jnp.sum(g_cw, axis=0), jnp.sum(g_cb, axis=0), g_vec,
        misc[None, :], jnp.sum(g_pw, axis=0).reshape(64, D), g_wa.reshape(256, D), g_wx.reshape(256, D), g_cctx,
        jnp.zeros((39, D), F32)], axis=0)
    n_small = small_g.shape[0] // N_CHIP
    mix_buf = jnp.concatenate([mix_g, small_g.reshape(N_CHIP, n_small, D)], axis=1)
    n_mb = n_mix + n_small

    ffn_red = reduce_scatter_chips(gbuf.reshape(N_CHIP, 2, 6 * ns, D), "ffn").reshape(12 * ns, D)
    mix_red = reduce_scatter_chips(mix_buf.reshape(N_CHIP, 2, n_mb // 2, D), "mix").reshape(n_mb, D)
    small_red = all_gather_chips(mix_red[n_mix:].reshape(2, n_small // 2, D), "ag_smallg").reshape(N_CHIP * n_small, D)

    fr = ffn_red.reshape(2, 2, 3, ns, D)
    g_gate, g_up, g_down = jnp.swapaxes(fr[:, :, 0], -1, -2), jnp.swapaxes(fr[:, :, 1], -1, -2), fr[:, :, 2]

    def cols(a):
        return lax.dynamic_slice_in_dim(a, chip * dsh, dsh, axis=a.ndim - 1)

    sr = small_red
    grads = dict(
        c_ctx=sr[600], w_mod=g_w_mod, b_mod=g_b_mod,
        ln_g=cols(sr[0:6]).reshape(2, 3, dsh), ln_b=cols(sr[6:12]).reshape(2, 3, dsh),
        ffn_w_gate=g_gate, ffn_w_up=g_up, ffn_w_down=g_down,
        mix_ab_w_in=mix_red[o1:o2].T[None], attn_sink=sr[23, 0:8][None], pool_w=sr[24:88].reshape(1, 4, 128, 128),
        pool_scale=sr[23, 8:8 + POOL_W][None], mix_ab_w_out=mix_red[o2:o3][None], lru_w_in=mix_red[0:o1].T[None],
        lru_conv_w=cols(sr[12:16])[None], lru_conv_b=cols(sr[16:17]), lru_wa=sr[88:344].reshape(1, 2, 8, 128, 128),
        lru_ba=cols(sr[17:19])[None], lru_wx=sr[344:600].reshape(1, 2, 8, 128, 128), lru_bx=cols(sr[19:21])[None],
        lru_lambda=cols(sr[21:23])[None], lru_w_out=mix_red[o3:n_mix][None])
    params = dict(c_ctx=(c_ctx, m_c_ctx, v_c_ctx), w_mod=(w_mod, m_w_mod, v_w_mod), b_mod=(b_mod, m_b_mod, v_b_mod),
                  ln_g=(ln_g, m_ln_g, v_ln_g), ln_b=(ln_b, m_ln_b, v_ln_b),
                  ffn_w_gate=(ffn_w_gate, m_ffn_w_gate, v_ffn_w_gate), ffn_w_up=(ffn_w_up, m_ffn_w_up, v_ffn_w_up),
                  ffn_w_down=(ffn_w_down, m_ffn_w_down, v_ffn_w_down),
                  mix_ab_w_in=(mix_ab_w_in, m_mix_ab_w_in, v_mix_ab_w_in), attn_sink=(attn_sink, m_attn_sink, v_attn_sink),
                  pool_w=(pool_w, m_pool_w, v_pool_w), pool_scale=(pool_scale, m_pool_scale, v_pool_scale),
                  mix_ab_w_out=(mix_ab_w_out, m_mix_ab_w_out, v_mix_ab_w_out), lru_w_in=(lru_w_in, m_lru_w_in, v_lru_w_in),
                  lru_conv_w=(lru_conv_w, m_lru_conv_w, v_lru_conv_w), lru_conv_b=(lru_conv_b, m_lru_conv_b, v_lru_conv_b),
                  lru_wa=(lru_wa, m_lru_wa, v_lru_wa), lru_ba=(lru_ba, m_lru_ba, v_lru_ba), lru_wx=(lru_wx, m_lru_wx, v_lru_wx),
                  lru_bx=(lru_bx, m_lru_bx, v_lru_bx), lru_lambda=(lru_lambda, m_lru_lambda, v_lru_lambda),
                  lru_w_out=(lru_w_out, m_lru_w_out, v_lru_w_out))
    gl, dl, ml, vl = [], [], [], []
    for name, (w, m, v) in params.items():
        g = grads[name].reshape(w.shape)
        d, mn, vn = adamw(w, g, m, v, f"adamw_{name}")
        gl.append(g)
        dl.append(d)
        ml.append(mn)
        vl.append(vn)
    return (loss, grad_x, *gl, *dl, *ml, *vl)
```

```python
import functools
import math

import jax
import jax.numpy as jnp
from jax import lax
from jax.experimental import pallas as pl
from jax.experimental.pallas import tpu as pltpu

F32, BF16 = jnp.float32, jnp.bfloat16
MESH = pl.DeviceIdType.MESH
ANY = pl.BlockSpec(memory_space=pl.ANY)
VMEM_SPEC = pl.BlockSpec(memory_space=pltpu.VMEM)

D = 1024
N_CHIP = 4
HEAD_DIM, N_HEADS, N_KV = 64, 8, 2
ATT_W, KV_W, POOL_W = 512, 128, 512
POOL_WINDOWS = (2, 4, 8, 16)
BLK = 128
ATT_SCALE = HEAD_DIM ** -0.5
ROPE_THETA = 10000.0
GRID_W = 64
LRU_C = 8.0
LN_EPS = 1e-5
NEG_INF = -1e30
DEPTH = 2
ALPHA = (2 * DEPTH) ** 0.25
N_MOD = 9
ADAM_LR, ADAM_B1, ADAM_B2, ADAM_EPS, ADAM_WD, ADAM_STEP = 0.001, 0.9, 0.999, 1e-08, 0.01, 10
VMEM_BIG = 48 * 1024 * 1024


def _cp(sem=None, vmem=None):
    kw = {}
    if sem is not None:
        kw["dimension_semantics"] = sem
    if vmem is not None:
        kw["vmem_limit_bytes"] = vmem
    return pltpu.CompilerParams(**kw)


def _sds(shape, dtype):
    return jax.ShapeDtypeStruct(tuple(shape), dtype)


def _pick(n, cands):
    for c in cands:
        if n % c == 0:
            return c
    return n


def _dot(a, b, dims):
    return lax.dot_general(a, b, (dims, ((), ())), preferred_element_type=F32)


def _nn(a, b):
    return _dot(a, b, ((1,), (0,)))


def _nt(a, b):
    return _dot(a, b, ((1,), (1,)))


def _tn(a, b):
    return _dot(a, b, ((0,), (0,)))


def _sigmoid(x):
    return 1.0 / (1.0 + jnp.exp(-x))


def _me():
    return lax.axis_index("x"), lax.axis_index("y"), lax.axis_index("c")


def _rcopy(src, dst, ssem, rsem, dev):
    return pltpu.make_async_remote_copy(src_ref=src, dst_ref=dst, send_sem=ssem, recv_sem=rsem,
                                        device_id=dev, device_id_type=MESH)


def all_gather8(x, name):
    def body(x_ref, o_ref, ssem, rsem, lsem):
        mx, my, mc = _me()
        me = 4 * mx + 2 * my + mc
        loc = pltpu.make_async_copy(x_ref, o_ref.at[me], lsem)
        loc.start()
        peers = []
        for m in range(1, 8):
            px = 1 - mx if (m >> 2) & 1 else mx
            py = 1 - my if (m >> 1) & 1 else my
            pc = 1 - mc if m & 1 else mc
            peers.append((px, py, pc))
        sends = [_rcopy(x_ref, o_ref.at[me], ssem.at[k], rsem.at[k], p) for k, p in enumerate(peers)]
        for cp in sends:
            cp.start()
        for k, (px, py, pc) in enumerate(peers):
            _rcopy(x_ref, o_ref.at[4 * px + 2 * py + pc], ssem.at[k], rsem.at[k], (px, py, pc)).wait_recv()
        for cp in sends:
            cp.wait_send()
        loc.wait()

    return pl.pallas_call(
        body, name=name, out_shape=_sds((8,) + x.shape, x.dtype),
        in_specs=[VMEM_SPEC], out_specs=VMEM_SPEC,
        scratch_shapes=[pltpu.SemaphoreType.DMA((7,)), pltpu.SemaphoreType.DMA((7,)), pltpu.SemaphoreType.DMA],
    )(x)


_ROW_BLOCKS = (512, 384, 256, 224, 128)


def _idx(v):
    return jnp.reshape(v, (1,)).astype(jnp.int32)


def place_slab(shard, name):
    _, h, w = shard.shape
    th = _pick(h, _ROW_BLOCKS)

    def body(s_ref, x_ref, o_ref):
        del s_ref
        o_ref[...] = x_ref[...]

    return pl.pallas_call(
        body, name=name, out_shape=_sds((N_CHIP,) + shard.shape, shard.dtype),
        grid_spec=pltpu.PrefetchScalarGridSpec(
            num_scalar_prefetch=1, grid=(2, h // th),
            in_specs=[pl.BlockSpec((None, th, w), lambda k, r, s: (k, r, 0))],
            out_specs=pl.BlockSpec((None, None, th, w), lambda k, r, s: (s[0], k, r, 0))),
    )(_idx(2 * lax.axis_index("x") + lax.axis_index("y")), shard)


def all_gather_chips(shard, name):
    def body(x_ref, o_ref, ssem, rsem):
        del x_ref
        mx, my, mc = _me()
        s = 2 * mx + my
        sib = (mx, my, 1 - mc)
        chips = [(1 - mx, my), (mx, 1 - my), (1 - mx, 1 - my)]
        first = [_rcopy(o_ref.at[s, mc], o_ref.at[s, mc], ssem.at[j], rsem.at[j], (px, py, mc))
                 for j, (px, py) in enumerate(chips)]
        for cp in first:
            cp.start()
        passed = []
        for j, (px, py) in enumerate(chips):
            ps = 2 * px + py
            _rcopy(o_ref.at[ps, mc], o_ref.at[ps, mc], ssem.at[j], rsem.at[j], (px, py, mc)).wait_recv()
            fw = _rcopy(o_ref.at[ps, mc], o_ref.at[ps, mc], ssem.at[3 + j], rsem.at[3 + j], sib)
            fw.start()
            passed.append(fw)
        for j, (px, py) in enumerate(chips):
            ps = 2 * px + py
            _rcopy(o_ref.at[ps, 1 - mc], o_ref.at[ps, 1 - mc], ssem.at[3 + j], rsem.at[3 + j], sib).wait_recv()
        for cp in first + passed:
            cp.wait_send()

    full = place_slab(shard, name + "_place")
    return pl.pallas_call(
        body, name=name, out_shape=_sds(full.shape, full.dtype), in_specs=[ANY], out_specs=ANY,
        input_output_aliases={0: 0},
        scratch_shapes=[pltpu.SemaphoreType.DMA((6,)), pltpu.SemaphoreType.DMA((6,))],
    )(full)


def sibling_send_other_half(buf, name):
    def body(x_ref, o_ref, ssem, rsem):
        mx, my, mc = _me()
        sib = (mx, my, 1 - mc)
        cps = [_rcopy(x_ref.at[k, 1 - mc], o_ref.at[k], ssem.at[k], rsem.at[k], sib) for k in range(N_CHIP)]
        for cp in cps:
            cp.start()
        for cp in cps:
            cp.wait_recv()
        for cp in cps:
            cp.wait_send()

    n, _, h, w = buf.shape
    return pl.pallas_call(
        body, name=name, out_shape=_sds((n, h, w), buf.dtype), in_specs=[ANY], out_specs=ANY,
        scratch_shapes=[pltpu.SemaphoreType.DMA((N_CHIP,)), pltpu.SemaphoreType.DMA((N_CHIP,))],
    )(buf)


def chips_all_to_all(q, name):
    def body(x_ref, o_ref, ssem, rsem):
        mx, my, mc = _me()
        s = 2 * mx + my
        chips = [(1 - mx, my), (mx, 1 - my), (1 - mx, 1 - my)]
        cps = [_rcopy(x_ref.at[2 * px + py], o_ref.at[s], ssem.at[j], rsem.at[j], (px, py, mc))
               for j, (px, py) in enumerate(chips)]
        for cp in cps:
            cp.start()
        for j, (px, py) in enumerate(chips):
            ps = 2 * px + py
            _rcopy(x_ref.at[ps], o_ref.at[ps], ssem.at[j], rsem.at[j], (px, py, mc)).wait_recv()
        for cp in cps:
            cp.wait_send()

    return pl.pallas_call(
        body, name=name, out_shape=_sds(q.shape, q.dtype), in_specs=[ANY], out_specs=ANY,
        scratch_shapes=[pltpu.SemaphoreType.DMA((3,)), pltpu.SemaphoreType.DMA((3,))],
    )(q)


def sibling_join_halves(both, name):
    def body(x_ref, o_ref, ssem, rsem):
        del x_ref
        mx, my, mc = _me()
        sib = (mx, my, 1 - mc)
        cp = _rcopy(o_ref.at[mc], o_ref.at[mc], ssem, rsem, sib)
        cp.start()
        _rcopy(o_ref.at[1 - mc], o_ref.at[1 - mc], ssem, rsem, sib).wait_recv()
        cp.wait_send()

    return pl.pallas_call(
        body, name=name, out_shape=_sds(both.shape, both.dtype), in_specs=[ANY], out_specs=ANY,
        input_output_aliases={0: 0}, scratch_shapes=[pltpu.SemaphoreType.DMA, pltpu.SemaphoreType.DMA],
    )(both)


def add_own_half(buf, recv, name):
    n, _, h, w = buf.shape
    th = _pick(h, _ROW_BLOCKS)

    def body(c_ref, a_ref, b_ref, o_ref):
        del c_ref
        o_ref[...] = a_ref[...] + b_ref[...]

    return pl.pallas_call(
        body, name=name, out_shape=_sds((n, h, w), buf.dtype),
        grid_spec=pltpu.PrefetchScalarGridSpec(
            num_scalar_prefetch=1, grid=(n, h // th),
            in_specs=[pl.BlockSpec((None, None, th, w), lambda k, r, c: (k, c[0], r, 0)),
                      pl.BlockSpec((None, th, w), lambda k, r, c: (k, r, 0))],
            out_specs=pl.BlockSpec((None, th, w), lambda k, r, c: (k, r, 0))),
    )(_idx(lax.axis_index("c")), buf, recv)


def sum_slots(q, r, name):
    n, h, w = r.shape
    th = _pick(h, _ROW_BLOCKS)

    def body(i_ref, q_ref, r1, r2, r3, o_ref):
        del i_ref
        o_ref[...] = ((q_ref[...] + r1[...]) + r2[...]) + r3[...]

    def slot(d):
        return lambda i, ix: ((ix[0] + d) % N_CHIP, i, 0)

    return pl.pallas_call(
        body, name=name, out_shape=_sds((2, h, w), r.dtype),
        grid_spec=pltpu.PrefetchScalarGridSpec(
            num_scalar_prefetch=1, grid=(h // th,),
            in_specs=[pl.BlockSpec((None, th, w), slot(d)) for d in (0, 1, 2, 3)],
            out_specs=pl.BlockSpec((None, th, w), lambda i, ix: (ix[1], i, 0))),
    )(jnp.stack([2 * lax.axis_index("x") + lax.axis_index("y"), lax.axis_index("c")]).astype(jnp.int32), q, r, r, r)


def reduce_scatter_chips(buf, tag):
    recv = sibling_send_other_half(buf, f"rs_sib_{tag}")
    q = add_own_half(buf, recv, f"rs_add2_{tag}")
    r = chips_all_to_all(q, f"rs_a2a_{tag}")
    red = sum_slots(q, r, f"rs_add4_{tag}")
    return sibling_join_halves(red, f"rs_join_{tag}")


class Layout:
    def __init__(self, n_ctx, n_lat):
        self.C, self.L = n_ctx, n_lat
        self.PS = n_ctx + n_lat
        self.T = 2 * self.PS
        self.tr = _pick(math.gcd(n_ctx, n_lat), (256, 128))
        self.bps = self.PS // self.tr
        self.cb = n_ctx // self.tr
        self.nblk = self.T // self.tr
        self.tm = _pick(self.T, (512, 256, 128))

    def seg(self, i):
        return jnp.where(i % self.bps < self.cb, 2, i // self.bps)


def rowwise(lay, name, fn, rows, segs=(), vecs=(), outs=(), sums=()):
    tr, nblk = lay.tr, lay.nblk
    n_r, n_s, n_v, n_o = len(rows), len(segs), len(vecs), len(outs)

    def body(*refs):
        ins = refs[:n_r + n_s + n_v]
        ors = refs[n_r + n_s + n_v:]
        vals = [r[...] for r in ins[:n_r]] + [r[0] for r in ins[n_r:n_r + n_s]] + [r[...] for r in ins[n_r + n_s:]]
        res = fn(*vals)
        for k in range(n_o):
            ors[k][...] = res[k].astype(ors[k].dtype)
        for k in range(len(sums)):
            ors[n_o + k][0] = res[n_o + k]

    in_specs = [pl.BlockSpec((tr, a.shape[1]), lambda i: (i, 0)) for a in rows]
    in_specs += [pl.BlockSpec((1,) + a.shape[1:], lambda i: (lay.seg(i), 0, 0)) for a in segs]
    in_specs += [pl.BlockSpec(a.shape, lambda i: (0, 0)) for a in vecs]
    out_shape = [_sds((lay.T, w), dt) for w, dt in outs] + [_sds((nblk, r, w), F32) for r, w in sums]
    out_specs = [pl.BlockSpec((tr, w), lambda i: (i, 0)) for w, _ in outs]
    out_specs += [pl.BlockSpec((1, r, w), lambda i: (i, 0, 0)) for r, w in sums]
    return pl.pallas_call(body, name=name, out_shape=out_shape, grid=(nblk,), in_specs=in_specs,
                          out_specs=out_specs, compiler_params=_cp(("parallel",)))(*rows, *segs, *vecs)


def modulate(lay, h, mod, k_shift, k_scale, name):
    def fn(hb, m):
        return (hb * (1.0 + m[k_scale:k_scale + 1]) + m[k_shift:k_shift + 1],)
    return rowwise(lay, name, fn, [h], segs=[mod], outs=[(D, BF16)])[0]


def resid_ln(lay, h, y, mod, k_gate, coef, lnv, name):
    def fn(hb, yb, m, ln):
        z = ALPHA * hb + (coef * m[k_gate:k_gate + 1]) * yb
        mu = jnp.mean(z, axis=-1, keepdims=True)
        zc = z - mu
        var = jnp.mean(zc * zc, axis=-1, keepdims=True)
        rstd = lax.rsqrt(var + LN_EPS)
        xhat = zc * rstd
        return xhat * ln[0:1] + ln[1:2], xhat, rstd
    return rowwise(lay, name, fn, [h, y], segs=[mod], vecs=[lnv], outs=[(D, F32), (D, F32), (1, F32)])


def ln_bwd(lay, dout, xhat, rstd, y, mod, k_gate, coef, lnv, name):
    def fn(do, xh, rs, yb, m, ln):
        dxh = do * ln[0:1]
        m1 = jnp.mean(dxh, axis=-1, keepdims=True)
        m2 = jnp.mean(dxh * xh, axis=-1, keepdims=True)
        dz = rs * (dxh - m1 - xh * m2)
        dy = (coef * m[k_gate:k_gate + 1]) * dz
        s = jnp.concatenate([jnp.sum(do, axis=0, keepdims=True), jnp.sum(do * xh, axis=0, keepdims=True),
                             jnp.sum(coef * dz * yb, axis=0, keepdims=True)], axis=0)
        return dy, ALPHA * dz, s
    return rowwise(lay, name, fn, [dout, xhat, rstd, y], segs=[mod], vecs=[lnv],
                   outs=[(D, BF16), (D, F32)], sums=[(3, D)])


def mod_bwd(lay, dres, dhm, h, mod, k_scale, name):
    def fn(dr, dm, hb, m):
        dh = dr + dm * (1.0 + m[k_scale:k_scale + 1])
        s = jnp.concatenate([jnp.sum(dm, axis=0, keepdims=True), jnp.sum(dm * hb, axis=0, keepdims=True)], axis=0)
        return dh, s
    return rowwise(lay, name, fn, [dres, dhm, h], segs=[mod], outs=[(D, F32)], sums=[(2, D)])


def block_sums(lay, parts, name):
    nblk, r, w = parts.shape

    def body(p_ref, o_ref):
        acc = [None, None, None]
        for i in range(nblk):
            sg = 2 if i % lay.bps < lay.cb else i // lay.bps
            acc[sg] = p_ref[i] if acc[sg] is None else acc[sg] + p_ref[i]
        for k in range(3):
            o_ref[k] = acc[k]
        o_ref[3] = (acc[0] + acc[1]) + acc[2]

    return pl.pallas_call(body, name=name, out_shape=_sds((4, r, w), F32), in_specs=[VMEM_SPEC],
                          out_specs=VMEM_SPEC)(parts)


def mm_nn(a, b, name, out_dtype=F32, bias=None):
    m, k = a.shape
    n = b.shape[1]
    tm = _pick(m, (512, 256, 128, 64, 32, 16, 8))
    tn = _pick(n, (1024, 768, 640, 512, 384, 256, 128))

    def body(*refs):
        if bias is None:
            a_ref, b_ref, o_ref = refs
            o_ref[...] = _nn(a_ref[...].astype(BF16), b_ref[...].astype(BF16)).astype(o_ref.dtype)
        else:
            a_ref, b_ref, c_ref, o_ref = refs
            o_ref[...] = (_nn(a_ref[...].astype(BF16), b_ref[...].astype(BF16)) + c_ref[...]).astype(o_ref.dtype)

    in_specs = [pl.BlockSpec((tm, k), lambda i, j: (i, 0)), pl.BlockSpec((k, tn), lambda i, j: (0, j))]
    ops = [a, b]
    if bias is not None:
        in_specs.append(pl.BlockSpec((1, tn), lambda i, j: (0, j)))
        ops.append(bias)
    return pl.pallas_call(body, name=name, out_shape=_sds((m, n), out_dtype), grid=(m // tm, n // tn),
                          in_specs=in_specs, out_specs=pl.BlockSpec((tm, tn), lambda i, j: (i, j)),
                          compiler_params=_cp(("parallel", "parallel"), VMEM_BIG))(*ops)


def mm_nt(a, b, name, out_dtype=F32):
    m, k = a.shape
    n = b.shape[0]
    tm = _pick(m, (512, 256, 128, 64, 32, 16, 8))
    tn = _pick(n, (1024, 768, 640, 512, 384, 256, 128))

    def body(a_ref, b_ref, o_ref):
        o_ref[...] = _nt(a_ref[...].astype(BF16), b_ref[...].astype(BF16)).astype(o_ref.dtype)

    return pl.pallas_call(body, name=name, out_shape=_sds((m, n), out_dtype), grid=(m // tm, n // tn),
                          in_specs=[pl.BlockSpec((tm, k), lambda i, j: (i, 0)), pl.BlockSpec((tn, k), lambda i, j: (j, 0))],
                          out_specs=pl.BlockSpec((tm, tn), lambda i, j: (i, j)),
                          compiler_params=_cp(("parallel", "parallel"), VMEM_BIG))(a, b)


def mm_tn(a, b, name):
    t, m = a.shape
    n = b.shape[1]
    tk = _pick(t, (512, 256, 128, 64, 32, 16))
    tm = _pick(m, (512, 384, 256, 128))

    def body(a_ref, b_ref, o_ref):
        @pl.when(pl.program_id(1) == 0)
        def _():
            o_ref[...] = jnp.zeros_like(o_ref)
        o_ref[...] += _tn(a_ref[...].astype(BF16), b_ref[...].astype(BF16))

    return pl.pallas_call(body, name=name, out_shape=_sds((m, n), F32), grid=(m // tm, t // tk),
                          in_specs=[pl.BlockSpec((tk, tm), lambda i, k: (k, i)), pl.BlockSpec((tk, n), lambda i, k: (k, 0))],
                          out_specs=pl.BlockSpec((tm, n), lambda i, k: (i, 0)),
                          compiler_params=_cp(("parallel", "arbitrary"), VMEM_BIG))(a, b)


def ffn_up(lay, hm, wbuf, ig, iu, ns, name):
    tm = lay.tm

    def body(h_ref, wg_ref, wu_ref, g_ref, u_ref, a_ref):
        hb = h_ref[...]
        g = _nt(hb, wg_ref[0])
        u = _nt(hb, wu_ref[0])
        g_ref[0] = g.astype(BF16)
        u_ref[0] = u.astype(BF16)
        a_ref[0] = (g * _sigmoid(g) * u).astype(BF16)

    spec_o = pl.BlockSpec((1, tm, ns), lambda s, i: (s, i, 0))
    return pl.pallas_call(
        body, name=name, out_shape=[_sds((N_CHIP, lay.T, ns), BF16)] * 3, grid=(N_CHIP, lay.T // tm),
        in_specs=[pl.BlockSpec((tm, D), lambda s, i: (i, 0)),
                  pl.BlockSpec((1, ns, D), lambda s, i: (s, ig, 0)),
                  pl.BlockSpec((1, ns, D), lambda s, i: (s, iu, 0))],
        out_specs=[spec_o] * 3, compiler_params=_cp(("parallel", "parallel"), VMEM_BIG))(hm, wbuf, wbuf)


def slab_nn_acc(lay, zs, wbuf, idxs, ns, name):
    tm = lay.tm
    npair = len(zs)

    def body(*refs):
        o_ref = refs[-1]

        @pl.when(pl.program_id(1) == 0)
        def _():
            o_ref[...] = jnp.zeros_like(o_ref)
        acc = _nn(refs[0][0], refs[npair][0])
        for p in range(1, npair):
            acc += _nn(refs[p][0], refs[npair + p][0])
        o_ref[...] += acc

    in_specs = [pl.BlockSpec((1, tm, ns), lambda i, s: (s, i, 0)) for _ in zs]
    in_specs += [pl.BlockSpec((1, ns, D), functools.partial(lambda i, s, q: (s, q, 0), q=q)) for q in idxs]
    return pl.pallas_call(
        body, name=name, out_shape=_sds((lay.T, D), F32), grid=(lay.T // tm, N_CHIP), in_specs=in_specs,
        out_specs=pl.BlockSpec((tm, D), lambda i, s: (i, 0)),
        compiler_params=_cp(("parallel", "arbitrary"), VMEM_BIG))(*zs, *([wbuf] * npair))


def ffn_bwd_da(lay, dy, wbuf, idn, g, u, ns, name):
    tm = lay.tm

    def body(dy_ref, wd_ref, g_ref, u_ref, dg_ref, du_ref):
        da = _nt(dy_ref[...], wd_ref[0])
        gv = g_ref[0].astype(F32)
        uv = u_ref[0].astype(F32)
        sg = _sigmoid(gv)
        dg_ref[0] = (da * uv * (sg * (1.0 + gv * (1.0 - sg)))).astype(BF16)
        du_ref[0] = (da * (gv * sg)).astype(BF16)

    spec_z = pl.BlockSpec((1, tm, ns), lambda s, i: (s, i, 0))
    return pl.pallas_call(
        body, name=name, out_shape=[_sds((N_CHIP, lay.T, ns), BF16)] * 2, grid=(N_CHIP, lay.T // tm),
        in_specs=[pl.BlockSpec((tm, D), lambda s, i: (i, 0)), pl.BlockSpec((1, ns, D), lambda s, i: (s, idn, 0)),
                  spec_z, spec_z],
        out_specs=[spec_z] * 2, compiler_params=_cp(("parallel", "parallel"), VMEM_BIG))(dy, wbuf, g, u)


def slab_tn(lay, z, x, gbuf, idx, ns, name):
    tk = lay.tm

    def body(z_ref, x_ref, g_in, o_ref):
        del g_in

        @pl.when(pl.program_id(1) == 0)
        def _():
            o_ref[...] = jnp.zeros_like(o_ref)
        o_ref[0] += _tn(z_ref[0], x_ref[...])

    return pl.pallas_call(
        body, name=name, out_shape=_sds(gbuf.shape, F32), grid=(N_CHIP, lay.T // tk),
        in_specs=[pl.BlockSpec((1, tk, ns), lambda s, k: (s, k, 0)), pl.BlockSpec((tk, D), lambda s, k: (k, 0)), ANY],
        out_specs=pl.BlockSpec((1, ns, D), lambda s, k: (s, idx, 0)),
        input_output_aliases={2: 0}, compiler_params=_cp(("parallel", "arbitrary"), VMEM_BIG))(z, x, gbuf)


Q0, K0, V0, U0, QR0, KR0, PEXT = 0, 512, 640, 768, 1280, 1792, 1920


def rope_fwd(lay, p, cos, sin, name):
    def fn(pb, cs, sn):
        cs4 = jnp.concatenate([cs] * 4, axis=1)
        sn4 = jnp.concatenate([sn] * 4, axis=1)
        qr = pb[:, Q0:K0] * cs4 + pb[:, QR0:KR0] * sn4
        kr = pb[:, K0:V0] * cs + pb[:, KR0:PEXT] * sn
        return qr, kr, pb[:, V0:U0], pb[:, U0:QR0]
    return rowwise(lay, name, fn, [p, cos, sin], outs=[(ATT_W, BF16), (KV_W, BF16), (KV_W, BF16), (POOL_W, F32)])


def rope_bwd(lay, dqr, dkr, dv, du, cos, sin, name):
    def fn(dq, dk, dvb, dub, cs, sn):
        cs4 = jnp.concatenate([cs] * 4, axis=1)
        sn4 = jnp.concatenate([sn] * 4, axis=1)
        return (jnp.concatenate([dq * cs4, dk * cs, dvb, dub, dq * sn4, dk * sn], axis=1),)
    return rowwise(lay, name, fn, [dqr, dkr, dv, du, cos, sin], outs=[(PEXT, BF16)])[0]


def _attn_specs(lay):
    nbs, cbk, lbk = lay.PS // BLK, lay.C // BLK, lay.L // BLK

    def kv_map(j):
        return lambda s, n: (s * nbs + cbk + jnp.clip(n - cbk + j - 1, 0, lbk - 1), 0)

    win = [pl.BlockSpec((BLK, KV_W), kv_map(j)) for j in range(3)]
    ctx = pl.BlockSpec((lay.C, KV_W), lambda s, n: (s * (lay.PS // lay.C), 0))
    return nbs, cbk, lbk, win, ctx


def _attn_masks(n, cbk, lbk):
    row = lax.broadcasted_iota(jnp.int32, (BLK, BLK), 0)
    col = lax.broadcasted_iota(jnp.int32, (BLK, BLK), 1)
    m = n - cbk
    lat = n >= cbk
    valid = [lat & (m >= 1) & (col >= row), lat & (col >= 0), lat & (m <= lbk - 2) & (col <= row)]
    lane_lo = lax.broadcasted_iota(jnp.int32, (BLK, 2 * HEAD_DIM), 1) < HEAD_DIM
    return valid, lane_lo


def attn_fwd(lay, qr, kr, vb, sink_tab, name):
    nbs, cbk, lbk, win, ctx = _attn_specs(lay)

    def body(q_ref, k0, k1, k2, kc_ref, v0, v1, v2, vc_ref, sk_ref, o_ref, l_ref):
        n = pl.program_id(1)
        valid, lane_lo = _attn_masks(n, cbk, lbk)
        ks = [k0[...], k1[...], k2[...]]
        vs = [v0[...], v1[...], v2[...]]
        kc, vc = kc_ref[...], vc_ref[...]
        for p in range(4):
            q2 = q_ref[:, p * 128:(p + 1) * 128]
            outs, lses = [], []
            for hh in range(2):
                qm = jnp.where(lane_lo == (hh == 0), q2, jnp.zeros_like(q2))
                sk = sk_ref[p:p + 1, hh * HEAD_DIM:hh * HEAD_DIM + 1]
                sw = [jnp.where(valid[j], _nt(qm, ks[j]) * ATT_SCALE, NEG_INF) for j in range(3)]
                sc = _nt(qm, kc) * ATT_SCALE
                mx = jnp.maximum(jnp.maximum(jnp.maximum(sw[0].max(-1, keepdims=True), sw[1].max(-1, keepdims=True)),
                                             jnp.maximum(sw[2].max(-1, keepdims=True), sc.max(-1, keepdims=True))), sk)
                ew = [jnp.exp(s - mx) for s in sw]
                ec = jnp.exp(sc - mx)
                den = ew[0].sum(-1, keepdims=True) + ew[1].sum(-1, keepdims=True) + ew[2].sum(-1, keepdims=True)
                den = den + ec.sum(-1, keepdims=True) + jnp.exp(sk - mx)
                o = _nn((ec / den).astype(BF16), vc)
                for j in range(3):
                    o += _nn((ew[j] / den).astype(BF16), vs[j])
                outs.append(o)
                lses.append(jnp.broadcast_to(mx + jnp.log(den), (BLK, 128)))
            o_ref[:, p * 128:(p + 1) * 128] = jnp.where(lane_lo, outs[0], outs[1]).astype(o_ref.dtype)
            l_ref[:, p * 128:(p + 1) * 128] = jnp.where(lane_lo, lses[0], lses[1])

    qspec = pl.BlockSpec((BLK, ATT_W), lambda s, n: (s * nbs + n, 0))
    return pl.pallas_call(
        body, name=name, out_shape=[_sds((lay.T, ATT_W), BF16), _sds((lay.T, ATT_W), F32)], grid=(2, nbs),
        in_specs=[qspec] + win + [ctx] + win + [ctx] + [pl.BlockSpec((8, 128), lambda s, n: (0, 0))],
        out_specs=[qspec, qspec], compiler_params=_cp(("parallel", "parallel")))(qr, kr, kr, kr, kr, vb, vb, vb, vb, sink_tab)


def attn_bwd(lay, qr, kr, vb, sink_tab, lse, datt, name):
    nbs, cbk, lbk, win, ctx = _attn_specs(lay)
    C, PS = lay.C, lay.PS

    def body(q_ref, k0, k1, k2, kc_ref, v0, v1, v2, vc_ref, sk_ref, l_ref, do_ref, dq_ref, dk_ref, dv_ref, ds_ref):
        n = pl.program_id(1)
        valid, lane_lo = _attn_masks(n, cbk, lbk)

        @pl.when(n == 0)
        def _():
            dk_ref[...] = jnp.zeros_like(dk_ref)
            dv_ref[...] = jnp.zeros_like(dv_ref)
            ds_ref[...] = jnp.zeros_like(ds_ref)

        ks = [k0[...], k1[...], k2[...], kc_ref[...]]
        vs = [v0[...], v1[...], v2[...], vc_ref[...]]
        dks = [jnp.zeros((BLK, KV_W), F32)] * 3 + [jnp.zeros((C, KV_W), F32)]
        dvs = list(dks)
        for p in range(4):
            sl = slice(p * 128, (p + 1) * 128)
            q2 = q_ref[:, sl]
            do2 = do_ref[:, sl].astype(BF16)
            lse2 = l_ref[:, sl]
            dq_h, dd_h = [], []
            for hh in range(2):
                sel = lane_lo == (hh == 0)
                qm = jnp.where(sel, q2, jnp.zeros_like(q2))
                dom = jnp.where(sel, do2, jnp.zeros_like(do2))
                lse_h = lse2[:, hh * HEAD_DIM:hh * HEAD_DIM + 1]
                ps, dps = [], []
                for j in range(4):
                    s = _nt(qm, ks[j]) * ATT_SCALE
                    if j < 3:
                        s = jnp.where(valid[j], s, NEG_INF)
                    ps.append(jnp.exp(s - lse_h))
                    dps.append(_nt(dom, vs[j]))
                dd = (ps[0] * dps[0]).sum(-1, keepdims=True) + (ps[1] * dps[1]).sum(-1, keepdims=True)
                dd = dd + (ps[2] * dps[2]).sum(-1, keepdims=True) + (ps[3] * dps[3]).sum(-1, keepdims=True)
                dq = jnp.zeros((BLK, 128), F32)
                for j in range(4):
                    dsb = (ps[j] * (dps[j] - dd) * ATT_SCALE).astype(BF16)
                    dq += _nn(dsb, ks[j])
                    dks[j] = dks[j] + _tn(dsb, qm)
                    dvs[j] = dvs[j] + _tn(ps[j].astype(BF16), dom)
                dq_h.append(dq)
                dd_h.append(jnp.broadcast_to(dd, (BLK, 128)))
            dq_ref[:, sl] = jnp.where(lane_lo, dq_h[0], dq_h[1])
            dd2 = jnp.where(lane_lo, dd_h[0], dd_h[1])
            psink = jnp.exp(sk_ref[p:p + 1, :] - lse2)
            ds_ref[0, p:p + 1, :] += -jnp.sum(psink * dd2, axis=0, keepdims=True)
        dk_ref[0:C, :] += dks[3]
        dv_ref[0:C, :] += dvs[3]
        for j in range(3):
            r0 = pl.multiple_of((cbk + jnp.clip(n - cbk + j - 1, 0, lbk - 1)) * BLK, BLK)
            dk_ref[pl.ds(r0, BLK), :] += dks[j]
            dv_ref[pl.ds(r0, BLK), :] += dvs[j]

    qspec = pl.BlockSpec((BLK, ATT_W), lambda s, n: (s * nbs + n, 0))
    kvout = pl.BlockSpec((PS, KV_W), lambda s, n: (s, 0))
    return pl.pallas_call(
        body, name=name,
        out_shape=[_sds((lay.T, ATT_W), F32), _sds((lay.T, KV_W), F32), _sds((lay.T, KV_W), F32), _sds((2, 8, 128), F32)],
        grid=(2, nbs),
        in_specs=[qspec] + win + [ctx] + win + [ctx] + [pl.BlockSpec((8, 128), lambda s, n: (0, 0)), qspec, qspec],
        out_specs=[qspec, kvout, kvout, pl.BlockSpec((1, 8, 128), lambda s, n: (s, 0, 0))],
        compiler_params=_cp(("parallel", "arbitrary")))(qr, kr, kr, kr, kr, vb, vb, vb, vb, sink_tab, lse, datt)


def _winsum(x, r):
    n = x.shape[0]
    t = lax.broadcasted_iota(jnp.int32, x.shape, 0)
    acc = x
    for o in range(1, r + 1):
        acc = acc + jnp.where(t >= o, pltpu.roll(x, o, 0), 0.0) + jnp.where(t < n - o, pltpu.roll(x, n - o, 0), 0.0)
    return acc


def _wincount(n, r):
    t = lax.broadcasted_iota(jnp.int32, (n, 128), 0)
    return (jnp.minimum(t + r, n - 1) - jnp.maximum(t - r, 0) + 1).astype(F32)


def pool_fwd(lay, u, w_pool, scale, name):
    segs = [(0, lay.C), (lay.C, lay.L)]

    def body(u_ref, w_ref, s_ref, o_ref):
        for r0, n in segs:
            for g, wd in enumerate(POOL_WINDOWS):
                sl = slice(g * 128, (g + 1) * 128)
                x = u_ref[r0:r0 + n, sl]
                d = _winsum(x, wd // 2) / _wincount(n, wd // 2) - x
                y = _nn(d.astype(BF16), w_ref[g].astype(BF16)) * s_ref[:, sl]
                o_ref[r0:r0 + n, sl] = y.astype(o_ref.dtype)

    spec = pl.BlockSpec((lay.PS, POOL_W), lambda s: (s, 0))
    return pl.pallas_call(
        body, name=name, out_shape=_sds((lay.T, POOL_W), BF16), grid=(2,),
        in_specs=[spec, pl.BlockSpec(w_pool.shape, lambda s: (0, 0, 0)), pl.BlockSpec((1, POOL_W), lambda s: (0, 0))],
        out_specs=spec, compiler_params=_cp(("parallel",), VMEM_BIG))(u, w_pool, scale)


def pool_bwd(lay, u, dcat, w_pool, scale, name):
    segs = [(0, lay.C), (lay.C, lay.L)]

    def body(u_ref, dp_ref, w_ref, s_ref, du_ref, dw_ref, dsc_ref):
        for g, wd in enumerate(POOL_WINDOWS):
            sl = slice(g * 128, (g + 1) * 128)
            wb = w_ref[g].astype(BF16)
            dw = jnp.zeros((128, 128), F32)
            dsc = jnp.zeros((1, 128), F32)
            for r0, n in segs:
                x = u_ref[r0:r0 + n, sl]
                cnt = _wincount(n, wd // 2)
                d = (_winsum(x, wd // 2) / cnt - x).astype(BF16)
                dp = dp_ref[r0:r0 + n, sl]
                dsc += jnp.sum(_nn(d, wb) * dp, axis=0, keepdims=True)
                dyp = (dp * s_ref[:, sl]).astype(BF16)
                dw += _tn(d, dyp)
                dd = _nt(dyp, wb)
                du_ref[r0:r0 + n, sl] = _winsum(dd / cnt, wd // 2) - dd
            dw_ref[0, g] = dw
            dsc_ref[0, :, sl] = dsc

    spec = pl.BlockSpec((lay.PS, POOL_W), lambda s: (s, 0))
    return pl.pallas_call(
        body, name=name,
        out_shape=[_sds((lay.T, POOL_W), F32), _sds((2, 4, 128, 128), F32), _sds((2, 1, POOL_W), F32)], grid=(2,),
        in_specs=[spec, pl.BlockSpec((lay.PS, POOL_W), lambda s: (s, 1)), pl.BlockSpec(w_pool.shape, lambda s: (0, 0, 0)),
                  pl.BlockSpec((1, POOL_W), lambda s: (0, 0))],
        out_specs=[spec, pl.BlockSpec((1, 4, 128, 128), lambda s: (s, 0, 0, 0)), pl.BlockSpec((1, 1, POOL_W), lambda s: (s, 0, 0))],
        compiler_params=_cp(("parallel",), VMEM_BIG))(u, dcat, w_pool, scale)


CONV_OFFS = (-1, 0, 1, 2)
CW = 256


def _shift_rows(x, o):
    if o == 0:
        return x
    n = x.shape[0]
    t = lax.broadcasted_iota(jnp.int32, x.shape, 0)
    if o < 0:
        return jnp.where(t >= -o, pltpu.roll(x, -o, 0), 0.0)
    return jnp.where(t < n - o, pltpu.roll(x, n - o, 0), 0.0)


def conv_fwd(lay, p, col0, w, b, name):
    segs = [(0, lay.C), (lay.C, lay.L)]
    cb0 = col0 // CW

    def body(x_ref, w_ref, b_ref, o_ref):
        for r0, n in segs:
            x = x_ref[r0:r0 + n, :]
            y = jnp.broadcast_to(b_ref[...], x.shape)
            for k, o in enumerate(CONV_OFFS):
                y = y + _shift_rows(x, o) * w_ref[k:k + 1, :]
            o_ref[r0:r0 + n, :] = y

    return pl.pallas_call(
        body, name=name, out_shape=_sds((lay.T, D), F32), grid=(2, D // CW),
        in_specs=[pl.BlockSpec((lay.PS, CW), lambda s, j: (s, cb0 + j)), pl.BlockSpec((4, CW), lambda s, j: (0, j)),
                  pl.BlockSpec((1, CW), lambda s, j: (0, j))],
        out_specs=pl.BlockSpec((lay.PS, CW), lambda s, j: (s, j)),
        compiler_params=_cp(("parallel", "parallel")))(p, w, b)


def conv_bwd(lay, p, col0, w, duc, name):
    segs = [(0, lay.C), (lay.C, lay.L)]
    cb0 = col0 // CW

    def body(x_ref, w_ref, g_ref, du_ref, dw_ref, db_ref):
        dws = [jnp.zeros((1, CW), F32)] * 4
        db = jnp.zeros((1, CW), F32)
        for r0, n in segs:
            x = x_ref[r0:r0 + n, :]
            g = g_ref[r0:r0 + n, :]
            du = jnp.zeros_like(g)
            for k, o in enumerate(CONV_OFFS):
                du = du + _shift_rows(g, -o) * w_ref[k:k + 1, :]
                dws[k] = dws[k] + jnp.sum(g * _shift_rows(x, o), axis=0, keepdims=True)
            db = db + jnp.sum(g, axis=0, keepdims=True)
            du_ref[r0:r0 + n, :] = du
        dw_ref[0] = jnp.concatenate(dws, axis=0)
        db_ref[0] = db

    return pl.pallas_call(
        body, name=name, out_shape=[_sds((lay.T, D), F32), _sds((2, 4, D), F32), _sds((2, 1, D), F32)], grid=(2, D // CW),
        in_specs=[pl.BlockSpec((lay.PS, CW), lambda s, j: (s, cb0 + j)), pl.BlockSpec((4, CW), lambda s, j: (0, j)),
                  pl.BlockSpec((lay.PS, CW), lambda s, j: (s, j))],
        out_specs=[pl.BlockSpec((lay.PS, CW), lambda s, j: (s, j)), pl.BlockSpec((1, 4, CW), lambda s, j: (s, 0, j)),
                   pl.BlockSpec((1, 1, CW), lambda s, j: (s, 0, j))],
        compiler_params=_cp(("parallel", "parallel")))(p, w, duc)


def _softplus_neg(lam):
    z = -lam
    w = jnp.exp(-jnp.abs(z))
    log1p = jnp.where(w < 1e-2, w * (1.0 - w * (0.5 - w / 3.0)), jnp.log(1.0 + w))
    return jnp.maximum(z, 0.0) + log1p, -_sigmoid(z)


def _neg_expm1(x):
    series = -x * (1.0 + x * (0.5 + x * (1.0 / 6.0 + x * (1.0 / 24.0 + x * (1.0 / 120.0)))))
    return jnp.where(x > -0.05, series, 1.0 - jnp.exp(x))


def _lru_gates(x, xb, wa, wx, ba, bx, lam):
    r = _sigmoid(_nn(xb, wa.astype(BF16)) + ba)
    gi = _sigmoid(_nn(xb, wx.astype(BF16)) + bx)
    sp, dsp = _softplus_neg(lam)
    la = -LRU_C * r * sp
    a = jnp.exp(la)
    sq = jnp.sqrt(_neg_expm1(2.0 * la))
    return r, gi, sp, dsp, a, sq


def lru_coeffs(lay, uc, wa, wx, vec, name):
    tr = lay.tr

    def body(x_ref, wa_ref, wx_ref, v_ref, a_ref, b_ref):
        for h in range(8):
            sl = slice(h * 128, (h + 1) * 128)
            x = x_ref[:, sl]
            xb = x.astype(BF16)
            for d in range(2):
                _, gi, _, _, a, sq = _lru_gates(x, xb, wa_ref[d, h], wx_ref[d, h], v_ref[d:d + 1, sl],
                                                v_ref[2 + d:3 + d, sl], v_ref[4 + d:5 + d, sl])
                a_ref[d, h] = a
                b_ref[d, h] = sq * (gi * x)

    wspec = pl.BlockSpec((2, 8, 128, 128), lambda i: (0, 0, 0, 0))
    ospec = pl.BlockSpec((2, 8, tr, 128), lambda i: (0, 0, i, 0))
    return pl.pallas_call(
        body, name=name, out_shape=[_sds((2, 8, lay.T, 128), F32)] * 2, grid=(lay.nblk,),
        in_specs=[pl.BlockSpec((tr, D), lambda i: (i, 0)), wspec, wspec, pl.BlockSpec((6, D), lambda i: (0, 0))],
        out_specs=[ospec, ospec], compiler_params=_cp(("parallel",)))(uc, wa, wx, vec)


def lru_coeffs_bwd(lay, uc, wa, wx, vec, da, db, name):
    tr = lay.tr

    def body(x_ref, wa_ref, wx_ref, v_ref, da_ref, db_ref, dx_ref, dwa_ref, dwx_ref, dv_ref):
        @pl.when(pl.program_id(0) == 0)
        def _():
            dwa_ref[...] = jnp.zeros_like(dwa_ref)
            dwx_ref[...] = jnp.zeros_like(dwx_ref)
            dv_ref[...] = jnp.zeros_like(dv_ref)

        for h in range(8):
            sl = slice(h * 128, (h + 1) * 128)
            x = x_ref[:, sl]
            xb = x.astype(BF16)
            dx = jnp.zeros_like(x)
            for d in range(2):
                wab, wxb = wa_ref[d, h].astype(BF16), wx_ref[d, h].astype(BF16)
                r, gi, sp, dsp, a, sq = _lru_gates(x, xb, wa_ref[d, h], wx_ref[d, h], v_ref[d:d + 1, sl],
                                                   v_ref[2 + d:3 + d, sl], v_ref[4 + d:5 + d, sl])
                dbv, dav = db_ref[d, h], da_ref[d, h]
                t1 = dbv * sq
                dgi = t1 * x
                dx = dx + t1 * gi
                dla = dav * a - (dbv * gi * x) * (a * a) / sq
                dr = dla * (-LRU_C * sp)
                dlam = jnp.sum(dla * (-LRU_C * r), axis=0, keepdims=True) * dsp
                dpa = dr * r * (1.0 - r)
                dpx = dgi * gi * (1.0 - gi)
                dpab, dpxb = dpa.astype(BF16), dpx.astype(BF16)
                dwa_ref[d, h] += _tn(xb, dpab)
                dwx_ref[d, h] += _tn(xb, dpxb)
                dx = dx + _nt(dpab, wab) + _nt(dpxb, wxb)
                dv_ref[d:d + 1, sl] += jnp.sum(dpa, axis=0, keepdims=True)
                dv_ref[2 + d:3 + d, sl] += jnp.sum(dpx, axis=0, keepdims=True)
                dv_ref[4 + d:5 + d, sl] += dlam
            dx_ref[:, sl] = dx

    wspec = pl.BlockSpec((2, 8, 128, 128), lambda i: (0, 0, 0, 0))
    gspec = pl.BlockSpec((2, 8, tr, 128), lambda i: (0, 0, i, 0))
    vspec = pl.BlockSpec((6, D), lambda i: (0, 0))
    xspec = pl.BlockSpec((tr, D), lambda i: (i, 0))
    return pl.pallas_call(
        body, name=name,
        out_shape=[_sds((lay.T, D), F32), _sds((2, 8, 128, 128), F32), _sds((2, 8, 128, 128), F32), _sds((6, D), F32)],
        grid=(lay.nblk,), in_specs=[xspec, wspec, wspec, vspec, gspec, gspec],
        out_specs=[xspec, wspec, wspec, vspec], compiler_params=_cp(("arbitrary",)))(uc, wa, wx, vec, da, db)


GB = 4


def _flip8(x):
    return jnp.concatenate([x[7 - j:8 - j] for j in range(8)], axis=0)


def _chunk_starts(a_tot, b_tot, first):
    rows, cur = [], first
    for j in range(8):
        rows.append(cur)
        cur = a_tot[j:j + 1] * cur + b_tot[j:j + 1]
    return jnp.concatenate(rows, axis=0), cur


def lru_scan(lay, a, b, name):
    segs = [(0, lay.C), (lay.C, lay.L)]

    def body(a_ref, b_ref, s_ref, hs_ref):
        rev = pl.program_id(1) == 1

        def row(base, st, k):
            return base + jnp.where(rev, st - 1 - k, k)

        state = [jnp.zeros((1, 128), F32)] * GB
        for si, (base, n) in enumerate(segs):
            st = n // 8

            def p1(k, c):
                out = []
                for g in range(GB):
                    at = a_ref[0, g, pl.ds(row(base, st, k), 8, stride=st), :]
                    bt = b_ref[0, g, pl.ds(row(base, st, k), 8, stride=st), :]
                    out += [at * c[2 * g], at * c[2 * g + 1] + bt]
                return tuple(out)

            init = tuple(v for _ in range(GB) for v in (jnp.ones((8, 128), F32), jnp.zeros((8, 128), F32)))
            tot = lax.fori_loop(0, st, p1, init, unroll=8)
            starts = []
            for g in range(GB):
                at, bt = tot[2 * g], tot[2 * g + 1]
                at, bt = jnp.where(rev, _flip8(at), at), jnp.where(rev, _flip8(bt), bt)
                hs, state[g] = _chunk_starts(at, bt, state[g])
                hs = jnp.where(rev, _flip8(hs), hs)
                hs_ref[0, g, si * 8:(si + 1) * 8, :] = hs
                starts.append(hs)

            def p2(k, c):
                out = []
                for g in range(GB):
                    r = row(base, st, k)
                    h = a_ref[0, g, pl.ds(r, 8, stride=st), :] * c[g] + b_ref[0, g, pl.ds(r, 8, stride=st), :]
                    s_ref[0, g, pl.ds(r, 8, stride=st), :] = h
                    out.append(h)
                return tuple(out)

            lax.fori_loop(0, st, p2, tuple(starts), unroll=8)

    spec = pl.BlockSpec((1, GB, lay.PS, 128), lambda s, d, hb: (d, hb, s, 0))
    return pl.pallas_call(
        body, name=name, out_shape=[_sds((2, 8, lay.T, 128), F32), _sds((2, 8, 32, 128), F32)], grid=(2, 2, 8 // GB),
        in_specs=[spec, spec], out_specs=[spec, pl.BlockSpec((1, GB, 16, 128), lambda s, d, hb: (d, hb, s, 0))],
        compiler_params=_cp(("parallel", "parallel", "parallel"), VMEM_BIG))(a, b)


def lru_scan_bwd(lay, a, s, hs, dy, name):
    segs = [(0, lay.C), (lay.C, lay.L)]

    def body(a_ref, s_ref, hs_ref, g_ref, da_ref, db_ref):
        rev = pl.program_id(1) == 1

        def row(base, st, kk):
            return base + jnp.where(rev, st - 1 - kk, kk)

        carry = [jnp.zeros((1, 128), F32)] * GB
        for si in (1, 0):
            base, n = segs[si]
            st = n // 8

            def p1(k, c):
                r = row(base, st, st - 1 - k)
                out = []
                for g in range(GB):
                    at = a_ref[0, g, pl.ds(r, 8, stride=st), :]
                    gt = g_ref[g, pl.ds(r, 8, stride=st), :]
                    out += [at * c[2 * g], at * (c[2 * g + 1] + gt)]
                return tuple(out)

            init = tuple(v for _ in range(GB) for v in (jnp.ones((8, 128), F32), jnp.zeros((8, 128), F32)))
            tot = lax.fori_loop(0, st, p1, init, unroll=8)
            m_in = []
            for g in range(GB):
                at, bt = tot[2 * g], tot[2 * g + 1]
                at, bt = jnp.where(rev, at, _flip8(at)), jnp.where(rev, bt, _flip8(bt))
                ms, carry[g] = _chunk_starts(at, bt, carry[g])
                m_in.append(jnp.where(rev, ms, _flip8(ms)))

            def step(kk, c, prev_of):
                r = row(base, st, kk)
                out = []
                for g in range(GB):
                    lamv = g_ref[g, pl.ds(r, 8, stride=st), :] + c[g]
                    db_ref[0, g, pl.ds(r, 8, stride=st), :] = lamv
                    da_ref[0, g, pl.ds(r, 8, stride=st), :] = lamv * prev_of(g)
                    out.append(a_ref[0, g, pl.ds(r, 8, stride=st), :] * lamv)
                return tuple(out)

            def p2(k, c):
                kk = st - 1 - k
                rp = row(base, st, kk - 1)
                return step(kk, c, lambda g: s_ref[0, g, pl.ds(rp, 8, stride=st), :])

            c = lax.fori_loop(0, st - 1, p2, tuple(m_in), unroll=8)
            step(0, c, lambda g: hs_ref[0, g, si * 8:(si + 1) * 8, :])

    spec = pl.BlockSpec((1, GB, lay.PS, 128), lambda s, d, hb: (d, hb, s, 0))
    return pl.pallas_call(
        body, name=name, out_shape=[_sds((2, 8, lay.T, 128), F32)] * 2, grid=(2, 2, 8 // GB),
        in_specs=[spec, spec, pl.BlockSpec((1, GB, 16, 128), lambda s, d, hb: (d, hb, s, 0)),
                  pl.BlockSpec((GB, lay.PS, 128), lambda s, d, hb: (hb, s, 0))],
        out_specs=[spec, spec], compiler_params=_cp(("parallel", "parallel", "parallel"), VMEM_BIG))(a, s, hs, dy)


def _gelu(x):
    k = math.sqrt(2.0 / math.pi)
    t = jnp.tanh(k * (x + 0.044715 * x * x * x))
    return 0.5 * x * (1.0 + t), 0.5 * (1.0 + t) + 0.5 * x * (1.0 - t * t) * k * (1.0 + 3 * 0.044715 * x * x)


def lru_gate(lay, p, s, name):
    tr = lay.tr

    def body(g_ref, s_ref, o_ref):
        for h in range(8):
            sl = slice(h * 128, (h + 1) * 128)
            o_ref[:, sl] = (_gelu(g_ref[:, sl])[0] * (s_ref[0, h] + s_ref[1, h])).astype(o_ref.dtype)

    return pl.pallas_call(
        body, name=name, out_shape=_sds((lay.T, D), BF16), grid=(lay.nblk,),
        in_specs=[pl.BlockSpec((tr, D), lambda i: (i, 0)), pl.BlockSpec((2, 8, tr, 128), lambda i: (0, 0, i, 0))],
        out_specs=pl.BlockSpec((tr, D), lambda i: (i, 0)), compiler_params=_cp(("parallel",)))(p, s)


def lru_gate_bwd(lay, p, s, do, name):
    tr = lay.tr

    def body(g_ref, s_ref, do_ref, dg_ref, dy_ref):
        for h in range(8):
            sl = slice(h * 128, (h + 1) * 128)
            ge, dge = _gelu(g_ref[:, sl])
            dov = do_ref[:, sl]
            dg_ref[:, sl] = dov * (s_ref[0, h] + s_ref[1, h]) * dge
            dy_ref[h] = dov * ge

    xspec = pl.BlockSpec((tr, D), lambda i: (i, 0))
    return pl.pallas_call(
        body, name=name, out_shape=[_sds((lay.T, D), F32), _sds((8, lay.T, 128), F32)], grid=(lay.nblk,),
        in_specs=[xspec, pl.BlockSpec((2, 8, tr, 128), lambda i: (0, 0, i, 0)), xspec],
        out_specs=[xspec, pl.BlockSpec((8, tr, 128), lambda i: (0, i, 0))],
        compiler_params=_cp(("parallel",)))(p, s, do)


def silu_rows(x, name):
    def body(x_ref, o_ref):
        v = x_ref[...]
        o_ref[...] = (v * _sigmoid(v)).astype(o_ref.dtype)
    return pl.pallas_call(body, name=name, out_shape=_sds(x.shape, BF16), in_specs=[VMEM_SPEC], out_specs=VMEM_SPEC)(x)


def mod_grad_rows(gath, name):
    w = gath.shape[-1]

    def body(g_ref, dm_ref, db_ref):
        dm_ref[...] = jnp.zeros_like(dm_ref)
        for l in range(2):
            ctx = g_ref[0, 3 * l + 2:3 * l + 3, :]
            tot = g_ref[0, 3 * l:3 * l + 1, :] + g_ref[0, 3 * l + 1:3 * l + 2, :]
            for k in range(8):
                dm_ref[l, 2 * k:2 * k + 2, :] = g_ref[k, 3 * l:3 * l + 2, :]
                if k:
                    ctx = ctx + g_ref[k, 3 * l + 2:3 * l + 3, :]
                    tot = tot + (g_ref[k, 3 * l:3 * l + 1, :] + g_ref[k, 3 * l + 1:3 * l + 2, :])
            dm_ref[l, 16:17, :] = ctx
            db_ref[l:l + 1, :] = tot + ctx

    return pl.pallas_call(body, name=name, out_shape=[_sds((2, 32, w), F32), _sds((2, w), F32)],
                          in_specs=[VMEM_SPEC], out_specs=[VMEM_SPEC, VMEM_SPEC])(gath)


def cctx_grad(p0, p1, c_ctx, name):
    def body(a_ref, b_ref, c_ref, o_ref):
        cv = c_ref[...]
        sg = _sigmoid(cv)
        o_ref[...] = 0.5 * (a_ref[0:1, :] + b_ref[0:1, :]) * (sg * (1.0 + cv * (1.0 - sg)))
    return pl.pallas_call(body, name=name, out_shape=_sds((1, D), F32), in_specs=[VMEM_SPEC] * 3,
                          out_specs=VMEM_SPEC)(p0, p1, c_ctx)


def loss_and_grad(lay, h, tgt, name):
    def fn(hb, tb):
        lat = (pl.program_id(0) % lay.bps) >= lay.cb
        e = jnp.where(lat, hb - tb, 0.0)
        return e * (1.0 / D), jnp.sum(e * e, axis=0, keepdims=True) * (0.5 / D)
    return rowwise(lay, name, fn, [h, tgt], outs=[(D, F32)], sums=[(1, D)])


def adamw(w, g, m, v, name):
    shape = w.shape
    w2, g2, m2, v2 = (t.reshape(-1, shape[-1]) for t in (w, g, m, v))
    rows, width = w2.shape
    tr = 256 if rows % 256 == 0 else rows
    c1 = 1.0 - ADAM_B1 ** ADAM_STEP
    c2 = 1.0 - ADAM_B2 ** ADAM_STEP

    def body(w_ref, g_ref, m_ref, v_ref, d_ref, mo_ref, vo_ref):
        gv = g_ref[...]
        mn = ADAM_B1 * m_ref[...] + (1.0 - ADAM_B1) * gv
        vn = ADAM_B2 * v_ref[...] + (1.0 - ADAM_B2) * (gv * gv)
        d_ref[...] = -ADAM_LR * ((mn / c1) / (jnp.sqrt(vn / c2) + ADAM_EPS) + ADAM_WD * w_ref[...])
        mo_ref[...] = mn
        vo_ref[...] = vn

    spec = pl.BlockSpec((tr, width), lambda i: (i, 0))
    d, mn, vn = pl.pallas_call(body, name=name, out_shape=[_sds((rows, width), F32)] * 3, grid=(rows // tr,),
                               in_specs=[spec] * 4, out_specs=[spec] * 3, compiler_params=_cp(("parallel",)))(w2, g2, m2, v2)
    return d.reshape(shape), mn.reshape(shape), vn.reshape(shape)


HEAD_PERM = (0, 4, 1, 5, 2, 6, 3, 7)


def _rot_rows(wt):
    return jnp.concatenate([-wt[32:64], wt[0:32]], axis=0)


def _unrot_rows(g):
    return jnp.concatenate([g[32:64], -g[0:32]], axis=0)


def _heads(a, n):
    return [a[64 * i:64 * (i + 1)] for i in range(n)]


def kernel(x, c, ctx, c_ctx, w_mod, b_mod, ln_g, ln_b, ffn_w_gate, ffn_w_up, ffn_w_down, mix_ab_w_in, attn_sink, pool_w, pool_scale, mix_ab_w_out, lru_w_in, lru_conv_w, lru_conv_b, lru_wa, lru_ba, lru_wx, lru_bx, lru_lambda, lru_w_out, loss_target, m_c_ctx, m_w_mod, m_b_mod, m_ln_g, m_ln_b, m_ffn_w_gate, m_ffn_w_up, m_ffn_w_down, m_mix_ab_w_in, m_attn_sink, m_pool_w, m_pool_scale, m_mix_ab_w_out, m_lru_w_in, m_lru_conv_w, m_lru_conv_b, m_lru_wa, m_lru_ba, m_lru_wx, m_lru_bx, m_lru_lambda, m_lru_w_out, v_c_ctx, v_w_mod, v_b_mod, v_ln_g, v_ln_b, v_ffn_w_gate, v_ffn_w_up, v_ffn_w_down, v_mix_ab_w_in, v_attn_sink, v_pool_w, v_pool_scale, v_mix_ab_w_out, v_lru_w_in, v_lru_conv_w, v_lru_conv_b, v_lru_wa, v_lru_ba, v_lru_wx, v_lru_bx, v_lru_lambda, v_lru_w_out):
    n_lat, n_ctx = x.shape[1], ctx.shape[1]
    lay = Layout(n_ctx, n_lat)
    T = lay.T
    ns = ffn_w_gate.shape[-1]
    n_li, n_ai = lru_w_in.shape[-1], mix_ab_w_in.shape[-1]
    n_ao, n_lo = mix_ab_w_out.shape[1], lru_w_out.shape[1]
    wm = w_mod.shape[-1]
    dsh = ln_g.shape[-1]
    mx, my, mc = lax.axis_index("x"), lax.axis_index("y"), lax.axis_index("c")
    chip = 2 * mx + my
    me = 2 * chip + mc

    c_all = all_gather8(c, "ag8_c").reshape(16, D)
    cc = jnp.concatenate([c_all, c_ctx[None, :], jnp.zeros((15, D), F32)], axis=0)
    sc = silu_rows(cc, "silu_c")
    slabs = []
    for l in range(DEPTH):
        bias = lax.dynamic_slice(b_mod[l][None, :], (0, chip * wm), (1, wm))
        slabs.append(mm_nn(sc, w_mod[l], f"mod_mm{l}", bias=bias))
    modg = all_gather_chips(jnp.stack(slabs), "ag_mod")
    modtab = []
    for l in range(DEPTH):
        full = jnp.transpose(modg[:, l], (1, 0, 2)).reshape(32, N_CHIP * wm)
        mine = lax.dynamic_slice(full, (2 * me, 0), (2, N_CHIP * wm))
        modtab.append(jnp.concatenate([mine, full[16:17]], axis=0).reshape(3, N_MOD, D))

    small = jnp.concatenate([ln_g.reshape(6, dsh), ln_b.reshape(6, dsh), lru_conv_w[0], lru_conv_b, lru_ba[0],
                             lru_bx[0], lru_lambda[0], jnp.zeros((9, dsh), F32)], axis=0)
    small = all_gather_chips(small.reshape(2, 16, dsh), "ag_small").reshape(N_CHIP, 32, dsh)
    small = jnp.transpose(small, (1, 0, 2)).reshape(32, D)
    ln_g_f, ln_b_f = small[0:6].reshape(2, 3, D), small[6:12].reshape(2, 3, D)
    conv_w_f, conv_b_f = small[12:16], small[16:17]
    lru_vec = small[17:23]

    ffn_sh = jnp.stack([jnp.swapaxes(ffn_w_gate, -1, -2), jnp.swapaxes(ffn_w_up, -1, -2), ffn_w_down], axis=2)
    ffn_sh = ffn_sh.astype(BF16).reshape(2, 6 * ns, D)
    wbuf = all_gather_chips(ffn_sh, "ag_ffn").reshape(N_CHIP, 12 * ns, D)
    mix_sh = jnp.concatenate([lru_w_in[0].T, mix_ab_w_in[0].T, mix_ab_w_out[0], lru_w_out[0]], axis=0).astype(BF16)
    n_mix = n_li + n_ai + n_ao + n_lo
    mixw = all_gather_chips(mix_sh.reshape(2, n_mix // 2, D), "ag_mix").reshape(N_CHIP, n_mix, D)
    o1, o2, o3 = n_li, n_li + n_ai, n_li + n_ai + n_ao
    lru_in_t = mixw[:, 0:o1].reshape(N_CHIP * n_li, D)
    ab_in_t = mixw[:, o1:o2].reshape(N_CHIP * n_ai, D)
    ab_out = mixw[:, o2:o3].reshape(N_CHIP * n_ao, D)
    lru_out = mixw[:, o3:].reshape(N_CHIP * n_lo, D)
    qh, kh = _heads(ab_in_t[Q0:K0], N_HEADS), _heads(ab_in_t[K0:V0], N_KV)
    w_ext_t = jnp.concatenate([qh[h] for h in HEAD_PERM] + [ab_in_t[K0:QR0]]
                              + [_rot_rows(qh[h]) for h in HEAD_PERM] + [_rot_rows(t) for t in kh], axis=0)
    oh = _heads(ab_out[0:ATT_W], N_HEADS)
    w_out_ext = jnp.concatenate([oh[h] for h in HEAD_PERM] + [ab_out[ATT_W:]], axis=0)

    t = jnp.arange(n_lat)
    inv = ROPE_THETA ** (-jnp.arange(16, dtype=F32) / 16.0)
    ang = jnp.concatenate([(t // GRID_W).astype(F32)[:, None] * inv, (t % GRID_W).astype(F32)[:, None] * inv], axis=-1)
    cos1 = jnp.concatenate([jnp.ones((n_ctx, 32), F32), jnp.cos(ang)], axis=0)
    sin1 = jnp.concatenate([jnp.zeros((n_ctx, 32), F32), jnp.sin(ang)], axis=0)
    cos_t = jnp.tile(cos1, (2, 4))
    sin_t = jnp.tile(sin1, (2, 4))
    sk = attn_sink[0]
    sink_tab = jnp.concatenate([jnp.repeat(jnp.stack([sk[:4], sk[4:]], axis=1), HEAD_DIM, axis=1),
                                jnp.zeros((4, 128), F32)], axis=0)
    pscale = pool_scale.reshape(1, POOL_W)

    h0 = jnp.concatenate([ctx, x], axis=1).reshape(T, D)
    tgt = jnp.concatenate([jnp.zeros_like(ctx), loss_target], axis=1).reshape(T, D)

    def lnv(l, j):
        return jnp.stack([ln_g_f[l, j], ln_b_f[l, j]])

    def fq(l, f, kind):
        return (l * 2 + f) * 3 + kind

    def ffn_fwd(h, l, f, k0, j):
        tag = f"l{l}f{f}"
        hm = modulate(lay, h, modtab[l], k0, k0 + 1, f"mod_{tag}")
        g, u, a = ffn_up(lay, hm, wbuf, fq(l, f, 0), fq(l, f, 1), ns, f"ffn_up_{tag}")
        y = slab_nn_acc(lay, [a], wbuf, [fq(l, f, 2)], ns, f"ffn_down_{tag}")
        out, xhat, rstd = resid_ln(lay, h, y, modtab[l], k0 + 2, 0.5, lnv(l, j), f"ln_{tag}")
        return out, dict(h=h, hm=hm, g=g, u=u, a=a, y=y, xhat=xhat, rstd=rstd)

    h1, r_f00 = ffn_fwd(h0, 0, 0, 0, 0)
    hm_a = modulate(lay, h1, modtab[0], 3, 4, "mod_mixa")
    p_a = mm_nt(hm_a, w_ext_t, "mixa_in")
    qr, kr, vb, u_a = rope_fwd(lay, p_a, cos_t, sin_t, "rope")
    att, lse = attn_fwd(lay, qr, kr, vb, sink_tab, "attn")
    pool = pool_fwd(lay, u_a, pool_w[0], pscale, "pool")
    cat = jnp.concatenate([att, pool], axis=1)
    y_a = mm_nn(cat, w_out_ext, "mixa_out")
    h2, xhat_a, rstd_a = resid_ln(lay, h1, y_a, modtab[0], 5, 1.0, lnv(0, 1), "ln_mixa")
    h3, r_f01 = ffn_fwd(h2, 0, 1, 6, 2)

    h4, r_f10 = ffn_fwd(h3, 1, 0, 0, 0)
    hm_c = modulate(lay, h4, modtab[1], 3, 4, "mod_mixc")
    p_c = mm_nt(hm_c, lru_in_t, "mixc_in")
    uc = conv_fwd(lay, p_c, D, conv_w_f, conv_b_f, "conv")
    a_c, b_c = lru_coeffs(lay, uc, lru_wa[0], lru_wx[0], lru_vec, "lru_coef")
    s_c, hs_c = lru_scan(lay, a_c, b_c, "lru_scan")
    o_c = lru_gate(lay, p_c, s_c, "lru_gate")
    y_c = mm_nn(o_c, lru_out, "mixc_out")
    h5, xhat_c, rstd_c = resid_ln(lay, h4, y_c, modtab[1], 5, 1.0, lnv(1, 1), "ln_mixc")
    h6, r_f11 = ffn_fwd(h5, 1, 1, 6, 2)

    dh, lparts = loss_and_grad(lay, h6, tgt, "loss")
    loss = lax.psum(jnp.sum(lparts), ("x", "y", "c"))

    gbuf = lax.empty((N_CHIP, 12 * ns, D), F32)
    dln = {}
    dms = {}

    def ffn_bwd(dout, r, l, f, k0, j, gbuf):
        tag = f"l{l}f{f}"
        dy, dres, s1 = ln_bwd(lay, dout, r["xhat"], r["rstd"], r["y"], modtab[l], k0 + 2, 0.5, lnv(l, j), f"lnb_{tag}")
        dg, du = ffn_bwd_da(lay, dy, wbuf, fq(l, f, 2), r["g"], r["u"], ns, f"ffn_da_{tag}")
        gbuf = slab_tn(lay, r["a"], dy, gbuf, fq(l, f, 2), ns, f"ffn_dwd_{tag}")
        gbuf = slab_tn(lay, dg, r["hm"], gbuf, fq(l, f, 0), ns, f"ffn_dwg_{tag}")
        gbuf = slab_tn(lay, du, r["hm"], gbuf, fq(l, f, 1), ns, f"ffn_dwu_{tag}")
        dhm = slab_nn_acc(lay, [dg, du], wbuf, [fq(l, f, 0), fq(l, f, 1)], ns, f"ffn_dh_{tag}")
        dh_in, s2 = mod_bwd(lay, dres, dhm, r["h"], modtab[l], k0 + 1, f"modb_{tag}")
        dln[(l, j)] = block_sums(lay, s1, f"bs_ln_{tag}")
        dms[(l, k0)] = block_sums(lay, s2, f"bs_mod_{tag}")
        return dh_in, gbuf

    dh, gbuf = ffn_bwd(dh, r_f11, 1, 1, 6, 2, gbuf)
    dy, dres, s1 = ln_bwd(lay, dh, xhat_c, rstd_c, y_c, modtab[1], 5, 1.0, lnv(1, 1), "lnb_mixc")
    do_c = mm_nt(dy, lru_out, "mixc_out_dx")
    g_lru_out = mm_tn(o_c, dy, "mixc_out_dw")
    dgate, dyg = lru_gate_bwd(lay, p_c, s_c, do_c, "lru_gate_b")
    da_c, db_c = lru_scan_bwd(lay, a_c, s_c, hs_c, dyg, "lru_scan_b")
    duc, g_wa, g_wx, g_vec = lru_coeffs_bwd(lay, uc, lru_wa[0], lru_wx[0], lru_vec, da_c, db_c, "lru_coef_b")
    du_c, g_cw, g_cb = conv_bwd(lay, p_c, D, conv_w_f, duc, "conv_b")
    dp_c = jnp.concatenate([dgate, du_c], axis=1).astype(BF16)
    dhm = mm_nn(dp_c, lru_in_t, "mixc_in_dx")
    g_lru_in_t = mm_tn(dp_c, hm_c, "mixc_in_dw")
    dh, s2 = mod_bwd(lay, dres, dhm, h4, modtab[1], 4, "modb_mixc")
    dln[(1, 1)] = block_sums(lay, s1, "bs_ln_mixc")
    dms[(1, 3)] = block_sums(lay, s2, "bs_mod_mixc")
    dh, gbuf = ffn_bwd(dh, r_f10, 1, 0, 0, 0, gbuf)

    dh, gbuf = ffn_bwd(dh, r_f01, 0, 1, 6, 2, gbuf)
    dy, dres, s1 = ln_bwd(lay, dh, xhat_a, rstd_a, y_a, modtab[0], 5, 1.0, lnv(0, 1), "lnb_mixa")
    dcat = mm_nt(dy, w_out_ext, "mixa_out_dx")
    g_out_ext = mm_tn(cat, dy, "mixa_out_dw")
    dqr, dkr, dv, g_sink = attn_bwd(lay, qr, kr, vb, sink_tab, lse, dcat, "attn_b")
    du_a, g_pw, g_ps = pool_bwd(lay, u_a, dcat, pool_w[0], pscale, "pool_b")
    dp_a = rope_bwd(lay, dqr, dkr, dv, du_a, cos_t, sin_t, "rope_b")
    dhm = mm_nn(dp_a, w_ext_t, "mixa_in_dx")
    g_ext_t = mm_tn(dp_a, hm_a, "mixa_in_dw")
    dh, s2 = mod_bwd(lay, dres, dhm, h1, modtab[0], 4, "modb_mixa")
    dln[(0, 1)] = block_sums(lay, s1, "bs_ln_mixa")
    dms[(0, 3)] = block_sums(lay, s2, "bs_mod_mixa")
    dh, gbuf = ffn_bwd(dh, r_f00, 0, 0, 0, 0, gbuf)
    grad_x = dh.reshape(2, lay.PS, D)[:, n_ctx:]

    rows = []
    for l in range(DEPTH):
        per_k = []
        for k0, j in ((0, 0), (3, 1), (6, 2)):
            per_k += [dms[(l, k0)][:3, 0], dms[(l, k0)][:3, 1], dln[(l, j)][:3, 2]]
        rows.append(jnp.stack(per_k, axis=1).reshape(3, N_MOD * D))
    dmod_loc = jnp.concatenate(rows + [jnp.zeros((2, N_MOD * D), F32)], axis=0)
    dmod_all, g_b_mod = mod_grad_rows(all_gather8(dmod_loc, "ag8_dmod"), "dmod_rows")
    g_w_mod, cparts = [], []
    for l in range(DEPTH):
        dcol = lax.dynamic_slice(dmod_all[l], (0, chip * wm), (32, wm))
        g_w_mod.append(mm_tn(sc, dcol.astype(BF16), f"wmod_dw{l}"))
        cparts.append(mm_nt(dcol[16:32], w_mod[l], f"cctx_dx{l}"))
    g_w_mod = jnp.stack(g_w_mod)
    g_cctx = cctx_grad(cparts[0], cparts[1], c_ctx[None, :], "cctx_grad")

    gq = _heads(g_ext_t[Q0:K0], N_HEADS)
    gqr = _heads(g_ext_t[QR0:KR0], N_HEADS)
    g_q = [None] * N_HEADS
    for i, h in enumerate(HEAD_PERM):
        g_q[h] = gq[i] + _unrot_rows(gqr[i])
    gk = [a + _unrot_rows(b) for a, b in zip(_heads(g_ext_t[K0:V0], N_KV), _heads(g_ext_t[KR0:PEXT], N_KV))]
    g_ab_in_t = jnp.concatenate(g_q + gk + [g_ext_t[V0:QR0]], axis=0)
    go = _heads(g_out_ext[0:ATT_W], N_HEADS)
    g_o = [None] * N_HEADS
    for i, h in enumerate(HEAD_PERM):
        g_o[h] = go[i]
    g_ab_out = jnp.concatenate(g_o + [g_out_ext[ATT_W:]], axis=0)
    mix_g = jnp.concatenate([g_lru_in_t.reshape(N_CHIP, n_li, D), g_ab_in_t.reshape(N_CHIP, n_ai, D),
                             g_ab_out.reshape(N_CHIP, n_ao, D), g_lru_out.reshape(N_CHIP, n_lo, D)], axis=1)

    g_ln_g = jnp.stack([jnp.stack([dln[(l, j)][3, 1] for j in range(3)]) for l in range(DEPTH)])
    g_ln_b = jnp.stack([jnp.stack([dln[(l, j)][3, 0] for j in range(3)]) for l in range(DEPTH)])
    sink_row = jnp.sum(g_sink, axis=0)[:4]
    g_sink8 = jnp.concatenate([sink_row[:, 0], sink_row[:, HEAD_DIM]])
    misc = jnp.concatenate([g_sink8, jnp.sum(g_ps, axis=0).reshape(POOL_W), jnp.zeros((D - 8 - POOL_W,), F32)])
    small_g = jnp.concatenate([
        g_ln_g.reshape(6, D), g_ln_b.reshape(6, D), jnp.sum(g_cw, axis=0), jnp.sum(g_cb, axis=0), g_vec,
        misc[None, :], jnp.sum(g_pw, axis=0).reshape(64, D), g_wa.reshape(256, D), g_wx.reshape(256, D), g_cctx,
        jnp.zeros((39, D), F32)], axis=0)
    n_small = small_g.shape[0] // N_CHIP
    mix_buf = jnp.concatenate([mix_g, small_g.reshape(N_CHIP, n_small, D)], axis=1)
    n_mb = n_mix + n_small

    ffn_red = reduce_scatter_chips(gbuf.reshape(N_CHIP, 2, 6 * ns, D), "ffn").reshape(12 * ns, D)
    mix_red = reduce_scatter_chips(mix_buf.reshape(N_CHIP, 2, n_mb // 2, D), "mix").reshape(n_mb, D)
    small_red = all_gather_chips(mix_red[n_mix:].reshape(2, n_small // 2, D), "ag_smallg").reshape(N_CHIP * n_small, D)

    fr = ffn_red.reshape(2, 2, 3, ns, D)
    g_gate, g_up, g_down = jnp.swapaxes(fr[:, :, 0], -1, -2), jnp.swapaxes(fr[:, :, 1], -1, -2), fr[:, :, 2]

    def cols(a):
        return lax.dynamic_slice_in_dim(a, chip * dsh, dsh, axis=a.ndim - 1)

    sr = small_red
    grads = dict(
        c_ctx=sr[600], w_mod=g_w_mod, b_mod=g_b_mod,
        ln_g=cols(sr[0:6]).reshape(2, 3, dsh), ln_b=cols(sr[6:12]).reshape(2, 3, dsh),
        ffn_w_gate=g_gate, ffn_w_up=g_up, ffn_w_down=g_down,
        mix_ab_w_in=mix_red[o1:o2].T[None], attn_sink=sr[23, 0:8][None], pool_w=sr[24:88].reshape(1, 4, 128, 128),
        pool_scale=sr[23, 8:8 + POOL_W][None], mix_ab_w_out=mix_red[o2:o3][None], lru_w_in=mix_red[0:o1].T[None],
        lru_conv_w=cols(sr[12:16])[None], lru_conv_b=cols(sr[16:17]), lru_wa=sr[88:344].reshape(1, 2, 8, 128, 128),
        lru_ba=cols(sr[17:19])[None], lru_wx=sr[344:600].reshape(1, 2, 8, 128, 128), lru_bx=cols(sr[19:21])[None],
        lru_lambda=cols(sr[21:23])[None], lru_w_out=mix_red[o3:n_mix][None])
    params = dict(c_ctx=(c_ctx, m_c_ctx, v_c_ctx), w_mod=(w_mod, m_w_mod, v_w_mod), b_mod=(b_mod, m_b_mod, v_b_mod),
                  ln_g=(ln_g, m_ln_g, v_ln_g), ln_b=(ln_b, m_ln_b, v_ln_b),
                  ffn_w_gate=(ffn_w_gate, m_ffn_w_gate, v_ffn_w_gate), ffn_w_up=(ffn_w_up, m_ffn_w_up, v_ffn_w_up),
                  ffn_w_down=(ffn_w_down, m_ffn_w_down, v_ffn_w_down),
                  mix_ab_w_in=(mix_ab_w_in, m_mix_ab_w_in, v_mix_ab_w_in), attn_sink=(attn_sink, m_attn_sink, v_attn_sink),
                  pool_w=(pool_w, m_pool_w, v_pool_w), pool_scale=(pool_scale, m_pool_scale, v_pool_scale),
                  mix_ab_w_out=(mix_ab_w_out, m_mix_ab_w_out, v_mix_ab_w_out), lru_w_in=(lru_w_in, m_lru_w_in, v_lru_w_in),
                  lru_conv_w=(lru_conv_w, m_lru_conv_w, v_lru_conv_w), lru_conv_b=(lru_conv_b, m_lru_conv_b, v_lru_conv_b),
                  lru_wa=(lru_wa, m_lru_wa, v_lru_wa), lru_ba=(lru_ba, m_lru_ba, v_lru_ba), lru_wx=(lru_wx, m_lru_wx, v_lru_wx),
                  lru_bx=(lru_bx, m_lru_bx, v_lru_bx), lru_lambda=(lru_lambda, m_lru_lambda, v_lru_lambda),
                  lru_w_out=(lru_w_out, m_lru_w_out, v_lru_w_out))
    gl, dl, ml, vl = [], [], [], []
    for name, (w, m, v) in params.items():
        g = grads[name].reshape(w.shape)
        d, mn, vn = adamw(w, g, m, v, f"adamw_{name}")
        gl.append(g)
        dl.append(d)
        ml.append(mn)
        vl.append(vn)
    return (loss, grad_x, *gl, *dl, *ml, *vl)
```

```python
import functools
import math

import jax
import jax.numpy as jnp
from jax import lax
from jax.experimental import pallas as pl
from jax.experimental.pallas import tpu as pltpu

F32, BF16 = jnp.float32, jnp.bfloat16
MESH = pl.DeviceIdType.MESH
ANY = pl.BlockSpec(memory_space=pl.ANY)
VMEM_SPEC = pl.BlockSpec(memory_space=pltpu.VMEM)

D = 1024
N_CHIP = 4
HEAD_DIM, N_HEADS, N_KV = 64, 8, 2
ATT_W, KV_W, POOL_W = 512, 128, 512
POOL_WINDOWS = (2, 4, 8, 16)
BLK = 128
ATT_SCALE = HEAD_DIM ** -0.5
ROPE_THETA = 10000.0
GRID_W = 64
LRU_C = 8.0
LN_EPS = 1e-5
NEG_INF = -1e30
DEPTH = 2
ALPHA = (2 * DEPTH) ** 0.25
N_MOD = 9
ADAM_LR, ADAM_B1, ADAM_B2, ADAM_EPS, ADAM_WD, ADAM_STEP = 0.001, 0.9, 0.999, 1e-08, 0.01, 10
VMEM_BIG = 48 * 1024 * 1024


def _cp(sem=None, vmem=None):
    kw = {}
    if sem is not None:
        kw["dimension_semantics"] = sem
    if vmem is not None:
        kw["vmem_limit_bytes"] = vmem
    return pltpu.CompilerParams(**kw)


def _sds(shape, dtype):
    return jax.ShapeDtypeStruct(tuple(shape), dtype)


def _pick(n, cands):
    for c in cands:
        if n % c == 0:
            return c
    return n


def _dot(a, b, dims):
    return lax.dot_general(a, b, (dims, ((), ())), preferred_element_type=F32)


def _nn(a, b):
    return _dot(a, b, ((1,), (0,)))


def _nt(a, b):
    return _dot(a, b, ((1,), (1,)))


def _tn(a, b):
    return _dot(a, b, ((0,), (0,)))


def _sigmoid(x):
    return 1.0 / (1.0 + jnp.exp(-x))


def _me():
    return lax.axis_index("x"), lax.axis_index("y"), lax.axis_index("c")


def _rcopy(src, dst, ssem, rsem, dev):
    return pltpu.make_async_remote_copy(src_ref=src, dst_ref=dst, send_sem=ssem, recv_sem=rsem,
                                        device_id=dev, device_id_type=MESH)


def all_gather8(x, name):
    def body(x_ref, o_ref, ssem, rsem, lsem):
        mx, my, mc = _me()
        me = 4 * mx + 2 * my + mc
        loc = pltpu.make_async_copy(x_ref, o_ref.at[me], lsem)
        loc.start()
        peers = []
        for m in range(1, 8):
            px = 1 - mx if (m >> 2) & 1 else mx
            py = 1 - my if (m >> 1) & 1 else my
            pc = 1 - mc if m & 1 else mc
            peers.append((px, py, pc))
        sends = [_rcopy(x_ref, o_ref.at[me], ssem.at[k], rsem.at[k], p) for k, p in enumerate(peers)]
        for cp in sends:
            cp.start()
        for k, (px, py, pc) in enumerate(peers):
            _rcopy(x_ref, o_ref.at[4 * px + 2 * py + pc], ssem.at[k], rsem.at[k], (px, py, pc)).wait_recv()
        for cp in sends:
            cp.wait_send()
        loc.wait()

    return pl.pallas_call(
        body, name=name, out_shape=_sds((8,) + x.shape, x.dtype),
        in_specs=[VMEM_SPEC], out_specs=VMEM_SPEC,
        scratch_shapes=[pltpu.SemaphoreType.DMA((7,)), pltpu.SemaphoreType.DMA((7,)), pltpu.SemaphoreType.DMA],
    )(x)


_ROW_BLOCKS = (512, 384, 256, 224, 128)


def _idx(v):
    return jnp.reshape(v, (1,)).astype(jnp.int32)


def place_slab(shard, name):
    _, h, w = shard.shape
    th = _pick(h, _ROW_BLOCKS)

    def body(s_ref, x_ref, o_ref):
        del s_ref
        o_ref[...] = x_ref[...]

    return pl.pallas_call(
        body, name=name, out_shape=_sds((N_CHIP,) + shard.shape, shard.dtype),
        grid_spec=pltpu.PrefetchScalarGridSpec(
            num_scalar_prefetch=1, grid=(2, h // th),
            in_specs=[pl.BlockSpec((None, th, w), lambda k, r, s: (k, r, 0))],
            out_specs=pl.BlockSpec((None, None, th, w), lambda k, r, s: (s[0], k, r, 0))),
    )(_idx(2 * lax.axis_index("x") + lax.axis_index("y")), shard)


def all_gather_chips(shard, name):
    def body(x_ref, o_ref, ssem, rsem):
        del x_ref
        mx, my, mc = _me()
        s = 2 * mx + my
        sib = (mx, my, 1 - mc)
        chips = [(1 - mx, my), (mx, 1 - my), (1 - mx, 1 - my)]
        first = [_rcopy(o_ref.at[s, mc], o_ref.at[s, mc], ssem.at[j], rsem.at[j], (px, py, mc))
                 for j, (px, py) in enumerate(chips)]
        for cp in first:
            cp.start()
        passed = []
        for j, (px, py) in enumerate(chips):
            ps = 2 * px + py
            _rcopy(o_ref.at[ps, mc], o_ref.at[ps, mc], ssem.at[j], rsem.at[j], (px, py, mc)).wait_recv()
            fw = _rcopy(o_ref.at[ps, mc], o_ref.at[ps, mc], ssem.at[3 + j], rsem.at[3 + j], sib)
            fw.start()
            passed.append(fw)
        for j, (px, py) in enumerate(chips):
            ps = 2 * px + py
            _rcopy(o_ref.at[ps, 1 - mc], o_ref.at[ps, 1 - mc], ssem.at[3 + j], rsem.at[3 + j], sib).wait_recv()
        for cp in first + passed:
            cp.wait_send()

    full = place_slab(shard, name + "_place")
    return pl.pallas_call(
        body, name=name, out_shape=_sds(full.shape, full.dtype), in_specs=[ANY], out_specs=ANY,
        input_output_aliases={0: 0},
        scratch_shapes=[pltpu.SemaphoreType.DMA((6,)), pltpu.SemaphoreType.DMA((6,))],
    )(full)


def sibling_send_other_half(buf, name):
    def body(x_ref, o_ref, ssem, rsem):
        mx, my, mc = _me()
        sib = (mx, my, 1 - mc)
        cps = [_rcopy(x_ref.at[k, 1 - mc], o_ref.at[k], ssem.at[k], rsem.at[k], sib) for k in range(N_CHIP)]
        for cp in cps:
            cp.start()
        for cp in cps:
            cp.wait_recv()
        for cp in cps:
            cp.wait_send()

    n, _, h, w = buf.shape
    return pl.pallas_call(
        body, name=name, out_shape=_sds((n, h, w), buf.dtype), in_specs=[ANY], out_specs=ANY,
        scratch_shapes=[pltpu.SemaphoreType.DMA((N_CHIP,)), pltpu.SemaphoreType.DMA((N_CHIP,))],
    )(buf)


def chips_all_to_all(q, name):
    def body(x_ref, o_ref, ssem, rsem):
        mx, my, mc = _me()
        s = 2 * mx + my
        chips = [(1 - mx, my), (mx, 1 - my), (1 - mx, 1 - my)]
        cps = [_rcopy(x_ref.at[2 * px + py], o_ref.at[s], ssem.at[j], rsem.at[j], (px, py, mc))
               for j, (px, py) in enumerate(chips)]
        for cp in cps:
            cp.start()
        for j, (px, py) in enumerate(chips):
            ps = 2 * px + py
            _rcopy(x_ref.at[ps], o_ref.at[ps], ssem.at[j], rsem.at[j], (px, py, mc)).wait_recv()
        for cp in cps:
            cp.wait_send()

    return pl.pallas_call(
        body, name=name, out_shape=_sds(q.shape, q.dtype), in_specs=[ANY], out_specs=ANY,
        scratch_shapes=[pltpu.SemaphoreType.DMA((3,)), pltpu.SemaphoreType.DMA((3,))],
    )(q)


def sibling_join_halves(both, name):
    def body(x_ref, o_ref, ssem, rsem):
        del x_ref
        mx, my, mc = _me()
        sib = (mx, my, 1 - mc)
        cp = _rcopy(o_ref.at[mc], o_ref.at[mc], ssem, rsem, sib)
        cp.start()
        _rcopy(o_ref.at[1 - mc], o_ref.at[1 - mc], ssem, rsem, sib).wait_recv()
        cp.wait_send()

    return pl.pallas_call(
        body, name=name, out_shape=_sds(both.shape, both.dtype), in_specs=[ANY], out_specs=ANY,
        input_output_aliases={0: 0}, scratch_shapes=[pltpu.SemaphoreType.DMA, pltpu.SemaphoreType.DMA],
    )(both)


def add_own_half(buf, recv, wire, name):
    n, _, h, w = buf.shape
    th = _pick(h, _ROW_BLOCKS)

    def body(c_ref, a_ref, b_ref, o_ref):
        del c_ref
        o_ref[...] = (a_ref[...] + b_ref[...]).astype(o_ref.dtype)

    return pl.pallas_call(
        body, name=name, out_shape=_sds((n, h, w), wire),
        grid_spec=pltpu.PrefetchScalarGridSpec(
            num_scalar_prefetch=1, grid=(n, h // th),
            in_specs=[pl.BlockSpec((None, None, th, w), lambda k, r, c: (k, c[0], r, 0)),
                      pl.BlockSpec((None, th, w), lambda k, r, c: (k, r, 0))],
            out_specs=pl.BlockSpec((None, th, w), lambda k, r, c: (k, r, 0))),
    )(_idx(lax.axis_index("c")), buf, recv)


def sum_slots(q, r, name):
    n, h, w = r.shape
    th = _pick(h, _ROW_BLOCKS)

    def body(i_ref, q_ref, r1, r2, r3, o_ref):
        del i_ref
        o_ref[...] = ((q_ref[...].astype(F32) + r1[...].astype(F32)) + r2[...].astype(F32)) + r3[...].astype(F32)

    def slot(d):
        return lambda i, ix: ((ix[0] + d) % N_CHIP, i, 0)

    return pl.pallas_call(
        body, name=name, out_shape=_sds((2, h, w), F32),
        grid_spec=pltpu.PrefetchScalarGridSpec(
            num_scalar_prefetch=1, grid=(h // th,),
            in_specs=[pl.BlockSpec((None, th, w), slot(d)) for d in (0, 1, 2, 3)],
            out_specs=pl.BlockSpec((None, th, w), lambda i, ix: (ix[1], i, 0))),
    )(jnp.stack([2 * lax.axis_index("x") + lax.axis_index("y"), lax.axis_index("c")]).astype(jnp.int32), q, r, r, r)


def reduce_scatter_chips(buf, tag, wire=F32):
    recv = sibling_send_other_half(buf, f"rs_sib_{tag}")
    q = add_own_half(buf, recv, wire, f"rs_add2_{tag}")
    r = chips_all_to_all(q, f"rs_a2a_{tag}")
    red = sum_slots(q, r, f"rs_add4_{tag}")
    return sibling_join_halves(red, f"rs_join_{tag}")


class Layout:
    def __init__(self, n_ctx, n_lat):
        self.C, self.L = n_ctx, n_lat
        self.PS = n_ctx + n_lat
        self.T = 2 * self.PS
        self.tr = _pick(math.gcd(n_ctx, n_lat), (256, 128))
        self.bps = self.PS // self.tr
        self.cb = n_ctx // self.tr
        self.nblk = self.T // self.tr
        self.tm = _pick(self.T, (512, 256, 128))

    def seg(self, i):
        return jnp.where(i % self.bps < self.cb, 2, i // self.bps)


def rowwise(lay, name, fn, rows, segs=(), vecs=(), outs=(), sums=()):
    tr, nblk = lay.tr, lay.nblk
    n_r, n_s, n_v, n_o = len(rows), len(segs), len(vecs), len(outs)

    def body(*refs):
        ins = refs[:n_r + n_s + n_v]
        ors = refs[n_r + n_s + n_v:]
        vals = [r[...] for r in ins[:n_r]] + [r[0] for r in ins[n_r:n_r + n_s]] + [r[...] for r in ins[n_r + n_s:]]
        res = fn(*vals)
        for k in range(n_o):
            ors[k][...] = res[k].astype(ors[k].dtype)
        for k in range(len(sums)):
            ors[n_o + k][0] = res[n_o + k]

    in_specs = [pl.BlockSpec((tr, a.shape[1]), lambda i: (i, 0)) for a in rows]
    in_specs += [pl.BlockSpec((1,) + a.shape[1:], lambda i: (lay.seg(i), 0, 0)) for a in segs]
    in_specs += [pl.BlockSpec(a.shape, lambda i: (0, 0)) for a in vecs]
    out_shape = [_sds((lay.T, w), dt) for w, dt in outs] + [_sds((nblk, r, w), F32) for r, w in sums]
    out_specs = [pl.BlockSpec((tr, w), lambda i: (i, 0)) for w, _ in outs]
    out_specs += [pl.BlockSpec((1, r, w), lambda i: (i, 0, 0)) for r, w in sums]
    return pl.pallas_call(body, name=name, out_shape=out_shape, grid=(nblk,), in_specs=in_specs,
                          out_specs=out_specs, compiler_params=_cp(("parallel",)))(*rows, *segs, *vecs)


def modulate(lay, h, mod, k_shift, k_scale, name):
    def fn(hb, m):
        return (hb * (1.0 + m[k_scale:k_scale + 1]) + m[k_shift:k_shift + 1],)
    return rowwise(lay, name, fn, [h], segs=[mod], outs=[(D, BF16)])[0]


def resid_ln(lay, h, y, mod, k_gate, coef, lnv, name):
    def fn(hb, yb, m, ln):
        z = ALPHA * hb + (coef * m[k_gate:k_gate + 1]) * yb
        mu = jnp.mean(z, axis=-1, keepdims=True)
        zc = z - mu
        var = jnp.mean(zc * zc, axis=-1, keepdims=True)
        rstd = lax.rsqrt(var + LN_EPS)
        xhat = zc * rstd
        return xhat * ln[0:1] + ln[1:2], xhat, rstd
    return rowwise(lay, name, fn, [h, y], segs=[mod], vecs=[lnv], outs=[(D, F32), (D, F32), (1, F32)])


def ln_bwd(lay, dout, xhat, rstd, y, mod, k_gate, coef, lnv, name):
    def fn(do, xh, rs, yb, m, ln):
        dxh = do * ln[0:1]
        m1 = jnp.mean(dxh, axis=-1, keepdims=True)
        m2 = jnp.mean(dxh * xh, axis=-1, keepdims=True)
        dz = rs * (dxh - m1 - xh * m2)
        dy = (coef * m[k_gate:k_gate + 1]) * dz
        s = jnp.concatenate([jnp.sum(do, axis=0, keepdims=True), jnp.sum(do * xh, axis=0, keepdims=True),
                             jnp.sum(coef * dz * yb, axis=0, keepdims=True)], axis=0)
        return dy, ALPHA * dz, s
    return rowwise(lay, name, fn, [dout, xhat, rstd, y], segs=[mod], vecs=[lnv],
                   outs=[(D, BF16), (D, F32)], sums=[(3, D)])


def mod_bwd(lay, dres, dhm, h, mod, k_scale, name):
    def fn(dr, dm, hb, m):
        dh = dr + dm * (1.0 + m[k_scale:k_scale + 1])
        s = jnp.concatenate([jnp.sum(dm, axis=0, keepdims=True), jnp.sum(dm * hb, axis=0, keepdims=True)], axis=0)
        return dh, s
    return rowwise(lay, name, fn, [dres, dhm, h], segs=[mod], outs=[(D, F32)], sums=[(2, D)])


def block_sums(lay, parts, name):
    nblk, r, w = parts.shape

    def body(p_ref, o_ref):
        acc = [None, None, None]
        for i in range(nblk):
            sg = 2 if i % lay.bps < lay.cb else i // lay.bps
            acc[sg] = p_ref[i] if acc[sg] is None else acc[sg] + p_ref[i]
        for k in range(3):
            o_ref[k] = acc[k]
        o_ref[3] = (acc[0] + acc[1]) + acc[2]

    return pl.pallas_call(body, name=name, out_shape=_sds((4, r, w), F32), in_specs=[VMEM_SPEC],
                          out_specs=VMEM_SPEC)(parts)


def mm_nn(a, b, name, out_dtype=F32, bias=None):
    m, k = a.shape
    n = b.shape[1]
    tm = _pick(m, (512, 256, 128, 64, 32, 16, 8))
    tn = _pick(n, (1024, 768, 640, 512, 384, 256, 128))

    def body(*refs):
        if bias is None:
            a_ref, b_ref, o_ref = refs
            o_ref[...] = _nn(a_ref[...].astype(BF16), b_ref[...].astype(BF16)).astype(o_ref.dtype)
        else:
            a_ref, b_ref, c_ref, o_ref = refs
            o_ref[...] = (_nn(a_ref[...].astype(BF16), b_ref[...].astype(BF16)) + c_ref[...]).astype(o_ref.dtype)

    in_specs = [pl.BlockSpec((tm, k), lambda i, j: (i, 0)), pl.BlockSpec((k, tn), lambda i, j: (0, j))]
    ops = [a, b]
    if bias is not None:
        in_specs.append(pl.BlockSpec((1, tn), lambda i, j: (0, j)))
        ops.append(bias)
    return pl.pallas_call(body, name=name, out_shape=_sds((m, n), out_dtype), grid=(m // tm, n // tn),
                          in_specs=in_specs, out_specs=pl.BlockSpec((tm, tn), lambda i, j: (i, j)),
                          compiler_params=_cp(("parallel", "parallel"), VMEM_BIG))(*ops)


def mm_nt(a, b, name, out_dtype=F32):
    m, k = a.shape
    n = b.shape[0]
    tm = _pick(m, (512, 256, 128, 64, 32, 16, 8))
    tn = _pick(n, (1024, 768, 640, 512, 384, 256, 128))

    def body(a_ref, b_ref, o_ref):
        o_ref[...] = _nt(a_ref[...].astype(BF16), b_ref[...].astype(BF16)).astype(o_ref.dtype)

    return pl.pallas_call(body, name=name, out_shape=_sds((m, n), out_dtype), grid=(m // tm, n // tn),
                          in_specs=[pl.BlockSpec((tm, k), lambda i, j: (i, 0)), pl.BlockSpec((tn, k), lambda i, j: (j, 0))],
                          out_specs=pl.BlockSpec((tm, tn), lambda i, j: (i, j)),
                          compiler_params=_cp(("parallel", "parallel"), VMEM_BIG))(a, b)


def mm_tn(a, b, name):
    t, m = a.shape
    n = b.shape[1]
    tk = _pick(t, (512, 256, 128, 64, 32, 16))
    tm = _pick(m, (512, 384, 256, 128))

    def body(a_ref, b_ref, o_ref):
        @pl.when(pl.program_id(1) == 0)
        def _():
            o_ref[...] = jnp.zeros_like(o_ref)
        o_ref[...] += _tn(a_ref[...].astype(BF16), b_ref[...].astype(BF16))

    return pl.pallas_call(body, name=name, out_shape=_sds((m, n), F32), grid=(m // tm, t // tk),
                          in_specs=[pl.BlockSpec((tk, tm), lambda i, k: (k, i)), pl.BlockSpec((tk, n), lambda i, k: (k, 0))],
                          out_specs=pl.BlockSpec((tm, n), lambda i, k: (i, 0)),
                          compiler_params=_cp(("parallel", "arbitrary"), VMEM_BIG))(a, b)


def ffn_up(lay, hm, wbuf, ig, iu, ns, name):
    tm = lay.tm

    def body(h_ref, wg_ref, wu_ref, g_ref, u_ref, a_ref):
        hb = h_ref[...]
        g = _nt(hb, wg_ref[0])
        u = _nt(hb, wu_ref[0])
        g_ref[0] = g.astype(BF16)
        u_ref[0] = u.astype(BF16)
        a_ref[0] = (g * _sigmoid(g) * u).astype(BF16)

    spec_o = pl.BlockSpec((1, tm, ns), lambda s, i: (s, i, 0))
    return pl.pallas_call(
        body, name=name, out_shape=[_sds((N_CHIP, lay.T, ns), BF16)] * 3, grid=(N_CHIP, lay.T // tm),
        in_specs=[pl.BlockSpec((tm, D), lambda s, i: (i, 0)),
                  pl.BlockSpec((1, ns, D), lambda s, i: (s, ig, 0)),
                  pl.BlockSpec((1, ns, D), lambda s, i: (s, iu, 0))],
        out_specs=[spec_o] * 3, compiler_params=_cp(("parallel", "parallel"), VMEM_BIG))(hm, wbuf, wbuf)


def slab_nn_acc(lay, zs, wbuf, idxs, ns, name):
    tm = lay.tm
    npair = len(zs)

    def body(*refs):
        o_ref = refs[-1]

        @pl.when(pl.program_id(1) == 0)
        def _():
            o_ref[...] = jnp.zeros_like(o_ref)
        acc = _nn(refs[0][0], refs[npair][0])
        for p in range(1, npair):
            acc += _nn(refs[p][0], refs[npair + p][0])
        o_ref[...] += acc

    in_specs = [pl.BlockSpec((1, tm, ns), lambda i, s: (s, i, 0)) for _ in zs]
    in_specs += [pl.BlockSpec((1, ns, D), functools.partial(lambda i, s, q: (s, q, 0), q=q)) for q in idxs]
    return pl.pallas_call(
        body, name=name, out_shape=_sds((lay.T, D), F32), grid=(lay.T // tm, N_CHIP), in_specs=in_specs,
        out_specs=pl.BlockSpec((tm, D), lambda i, s: (i, 0)),
        compiler_params=_cp(("parallel", "arbitrary"), VMEM_BIG))(*zs, *([wbuf] * npair))


def ffn_bwd_da(lay, dy, wbuf, idn, g, u, ns, name):
    tm = lay.tm

    def body(dy_ref, wd_ref, g_ref, u_ref, dg_ref, du_ref):
        da = _nt(dy_ref[...], wd_ref[0])
        gv = g_ref[0].astype(F32)
        uv = u_ref[0].astype(F32)
        sg = _sigmoid(gv)
        dg_ref[0] = (da * uv * (sg * (1.0 + gv * (1.0 - sg)))).astype(BF16)
        du_ref[0] = (da * (gv * sg)).astype(BF16)

    spec_z = pl.BlockSpec((1, tm, ns), lambda s, i: (s, i, 0))
    return pl.pallas_call(
        body, name=name, out_shape=[_sds((N_CHIP, lay.T, ns), BF16)] * 2, grid=(N_CHIP, lay.T // tm),
        in_specs=[pl.BlockSpec((tm, D), lambda s, i: (i, 0)), pl.BlockSpec((1, ns, D), lambda s, i: (s, idn, 0)),
                  spec_z, spec_z],
        out_specs=[spec_z] * 2, compiler_params=_cp(("parallel", "parallel"), VMEM_BIG))(dy, wbuf, g, u)


def slab_tn(lay, z, x, gbuf, idx, ns, name):
    tk = lay.tm

    def body(z_ref, x_ref, g_in, o_ref):
        del g_in

        @pl.when(pl.program_id(1) == 0)
        def _():
            o_ref[...] = jnp.zeros_like(o_ref)
        o_ref[0] += _tn(z_ref[0], x_ref[...])

    return pl.pallas_call(
        body, name=name, out_shape=_sds(gbuf.shape, F32), grid=(N_CHIP, lay.T // tk),
        in_specs=[pl.BlockSpec((1, tk, ns), lambda s, k: (s, k, 0)), pl.BlockSpec((tk, D), lambda s, k: (k, 0)), ANY],
        out_specs=pl.BlockSpec((1, ns, D), lambda s, k: (s, idx, 0)),
        input_output_aliases={2: 0}, compiler_params=_cp(("parallel", "arbitrary"), VMEM_BIG))(z, x, gbuf)


Q0, K0, V0, U0, QR0, KR0, PEXT = 0, 512, 640, 768, 1280, 1792, 1920


def rope_fwd(lay, p, cos, sin, name):
    def fn(pb, cs, sn):
        cs4 = jnp.concatenate([cs] * 4, axis=1)
        sn4 = jnp.concatenate([sn] * 4, axis=1)
        qr = pb[:, Q0:K0] * cs4 + pb[:, QR0:KR0] * sn4
        kr = pb[:, K0:V0] * cs + pb[:, KR0:PEXT] * sn
        return qr, kr, pb[:, V0:U0], pb[:, U0:QR0]
    return rowwise(lay, name, fn, [p, cos, sin], outs=[(ATT_W, BF16), (KV_W, BF16), (KV_W, BF16), (POOL_W, F32)])


def rope_bwd(lay, dqr, dkr, dv, du, cos, sin, name):
    def fn(dq, dk, dvb, dub, cs, sn):
        cs4 = jnp.concatenate([cs] * 4, axis=1)
        sn4 = jnp.concatenate([sn] * 4, axis=1)
        return (jnp.concatenate([dq * cs4, dk * cs, dvb, dub, dq * sn4, dk * sn], axis=1),)
    return rowwise(lay, name, fn, [dqr, dkr, dv, du, cos, sin], outs=[(PEXT, BF16)])[0]


def _attn_specs(lay):
    nbs, cbk, lbk = lay.PS // BLK, lay.C // BLK, lay.L // BLK

    def kv_map(j):
        return lambda s, n: (s * nbs + cbk + jnp.clip(n - cbk + j - 1, 0, lbk - 1), 0)

    win = [pl.BlockSpec((BLK, KV_W), kv_map(j)) for j in range(3)]
    ctx = pl.BlockSpec((lay.C, KV_W), lambda s, n: (s * (lay.PS // lay.C), 0))
    return nbs, cbk, lbk, win, ctx


def _attn_masks(n, cbk, lbk):
    row = lax.broadcasted_iota(jnp.int32, (BLK, BLK), 0)
    col = lax.broadcasted_iota(jnp.int32, (BLK, BLK), 1)
    m = n - cbk
    lat = n >= cbk
    valid = [lat & (m >= 1) & (col >= row), lat & (col >= 0), lat & (m <= lbk - 2) & (col <= row)]
    lane_lo = lax.broadcasted_iota(jnp.int32, (BLK, 2 * HEAD_DIM), 1) < HEAD_DIM
    return valid, lane_lo


def attn_fwd(lay, qr, kr, vb, sink_tab, name):
    nbs, cbk, lbk, win, ctx = _attn_specs(lay)

    def body(q_ref, k0, k1, k2, kc_ref, v0, v1, v2, vc_ref, sk_ref, o_ref, l_ref):
        n = pl.program_id(1)
        valid, lane_lo = _attn_masks(n, cbk, lbk)
        ks = [k0[...], k1[...], k2[...]]
        vs = [v0[...], v1[...], v2[...]]
        kc, vc = kc_ref[...], vc_ref[...]
        for p in range(4):
            q2 = q_ref[:, p * 128:(p + 1) * 128]
            outs, lses = [], []
            for hh in range(2):
                qm = jnp.where(lane_lo == (hh == 0), q2, jnp.zeros_like(q2))
                sk = sk_ref[p:p + 1, hh * HEAD_DIM:hh * HEAD_DIM + 1]
                sw = [jnp.where(valid[j], _nt(qm, ks[j]) * ATT_SCALE, NEG_INF) for j in range(3)]
                sc = _nt(qm, kc) * ATT_SCALE
                mx = jnp.maximum(jnp.maximum(jnp.maximum(sw[0].max(-1, keepdims=True), sw[1].max(-1, keepdims=True)),
                                             jnp.maximum(sw[2].max(-1, keepdims=True), sc.max(-1, keepdims=True))), sk)
                ew = [jnp.exp(s - mx) for s in sw]
                ec = jnp.exp(sc - mx)
                den = ew[0].sum(-1, keepdims=True) + ew[1].sum(-1, keepdims=True) + ew[2].sum(-1, keepdims=True)
                den = den + ec.sum(-1, keepdims=True) + jnp.exp(sk - mx)
                o = _nn((ec / den).astype(BF16), vc)
                for j in range(3):
                    o += _nn((ew[j] / den).astype(BF16), vs[j])
                outs.append(o)
                lses.append(jnp.broadcast_to(mx + jnp.log(den), (BLK, 128)))
            o_ref[:, p * 128:(p + 1) * 128] = jnp.where(lane_lo, outs[0], outs[1]).astype(o_ref.dtype)
            l_ref[:, p * 128:(p + 1) * 128] = jnp.where(lane_lo, lses[0], lses[1])

    qspec = pl.BlockSpec((BLK, ATT_W), lambda s, n: (s * nbs + n, 0))
    return pl.pallas_call(
        body, name=name, out_shape=[_sds((lay.T, ATT_W), BF16), _sds((lay.T, ATT_W), F32)], grid=(2, nbs),
        in_specs=[qspec] + win + [ctx] + win + [ctx] + [pl.BlockSpec((8, 128), lambda s, n: (0, 0))],
        out_specs=[qspec, qspec], compiler_params=_cp(("parallel", "parallel")))(qr, kr, kr, kr, kr, vb, vb, vb, vb, sink_tab)


def attn_bwd(lay, qr, kr, vb, sink_tab, lse, datt, name):
    nbs, cbk, lbk, win, ctx = _attn_specs(lay)
    C, PS = lay.C, lay.PS

    def body(q_ref, k0, k1, k2, kc_ref, v0, v1, v2, vc_ref, sk_ref, l_ref, do_ref, dq_ref, dk_ref, dv_ref, ds_ref):
        n = pl.program_id(1)
        valid, lane_lo = _attn_masks(n, cbk, lbk)

        @pl.when(n == 0)
        def _():
            dk_ref[...] = jnp.zeros_like(dk_ref)
            dv_ref[...] = jnp.zeros_like(dv_ref)
            ds_ref[...] = jnp.zeros_like(ds_ref)

        ks = [k0[...], k1[...], k2[...], kc_ref[...]]
        vs = [v0[...], v1[...], v2[...], vc_ref[...]]
        dks = [jnp.zeros((BLK, KV_W), F32)] * 3 + [jnp.zeros((C, KV_W), F32)]
        dvs = list(dks)
        for p in range(4):
            sl = slice(p * 128, (p + 1) * 128)
            q2 = q_ref[:, sl]
            do2 = do_ref[:, sl].astype(BF16)
            lse2 = l_ref[:, sl]
            dq_h, dd_h = [], []
            for hh in range(2):
                sel = lane_lo == (hh == 0)
                qm = jnp.where(sel, q2, jnp.zeros_like(q2))
                dom = jnp.where(sel, do2, jnp.zeros_like(do2))
                lse_h = lse2[:, hh * HEAD_DIM:hh * HEAD_DIM + 1]
                ps, dps = [], []
                for j in range(4):
                    s = _nt(qm, ks[j]) * ATT_SCALE
                    if j < 3:
                        s = jnp.where(valid[j], s, NEG_INF)
                    ps.append(jnp.exp(s - lse_h))
                    dps.append(_nt(dom, vs[j]))
                dd = (ps[0] * dps[0]).sum(-1, keepdims=True) + (ps[1] * dps[1]).sum(-1, keepdims=True)
                dd = dd + (ps[2] * dps[2]).sum(-1, keepdims=True) + (ps[3] * dps[3]).sum(-1, keepdims=True)
                dq = jnp.zeros((BLK, 128), F32)
                for j in range(4):
                    dsb = (ps[j] * (dps[j] - dd) * ATT_SCALE).astype(BF16)
                    dq += _nn(dsb, ks[j])
                    dks[j] = dks[j] + _tn(dsb, qm)
                    dvs[j] = dvs[j] + _tn(ps[j].astype(BF16), dom)
                dq_h.append(dq)
                dd_h.append(jnp.broadcast_to(dd, (BLK, 128)))
            dq_ref[:, sl] = jnp.where(lane_lo, dq_h[0], dq_h[1])
            dd2 = jnp.where(lane_lo, dd_h[0], dd_h[1])
            psink = jnp.exp(sk_ref[p:p + 1, :] - lse2)
            ds_ref[0, p:p + 1, :] += -jnp.sum(psink * dd2, axis=0, keepdims=True)
        dk_ref[0:C, :] += dks[3]
        dv_ref[0:C, :] += dvs[3]
        for j in range(3):
            r0 = pl.multiple_of((cbk + jnp.clip(n - cbk + j - 1, 0, lbk - 1)) * BLK, BLK)
            dk_ref[pl.ds(r0, BLK), :] += dks[j]
            dv_ref[pl.ds(r0, BLK), :] += dvs[j]

    qspec = pl.BlockSpec((BLK, ATT_W), lambda s, n: (s * nbs + n, 0))
    kvout = pl.BlockSpec((PS, KV_W), lambda s, n: (s, 0))
    return pl.pallas_call(
        body, name=name,
        out_shape=[_sds((lay.T, ATT_W), F32), _sds((lay.T, KV_W), F32), _sds((lay.T, KV_W), F32), _sds((2, 8, 128), F32)],
        grid=(2, nbs),
        in_specs=[qspec] + win + [ctx] + win + [ctx] + [pl.BlockSpec((8, 128), lambda s, n: (0, 0)), qspec, qspec],
        out_specs=[qspec, kvout, kvout, pl.BlockSpec((1, 8, 128), lambda s, n: (s, 0, 0))],
        compiler_params=_cp(("parallel", "arbitrary")))(qr, kr, kr, kr, kr, vb, vb, vb, vb, sink_tab, lse, datt)


def _winsum(x, r):
    n = x.shape[0]
    t = lax.broadcasted_iota(jnp.int32, x.shape, 0)
    acc = x
    for o in range(1, r + 1):
        acc = acc + jnp.where(t >= o, pltpu.roll(x, o, 0), 0.0) + jnp.where(t < n - o, pltpu.roll(x, n - o, 0), 0.0)
    return acc


def _wincount(n, r):
    t = lax.broadcasted_iota(jnp.int32, (n, 128), 0)
    return (jnp.minimum(t + r, n - 1) - jnp.maximum(t - r, 0) + 1).astype(F32)


def pool_fwd(lay, u, w_pool, scale, name):
    segs = [(0, lay.C), (lay.C, lay.L)]

    def body(u_ref, w_ref, s_ref, o_ref):
        for r0, n in segs:
            for g, wd in enumerate(POOL_WINDOWS):
                sl = slice(g * 128, (g + 1) * 128)
                x = u_ref[r0:r0 + n, sl]
                d = _winsum(x, wd // 2) / _wincount(n, wd // 2) - x
                y = _nn(d.astype(BF16), w_ref[g].astype(BF16)) * s_ref[:, sl]
                o_ref[r0:r0 + n, sl] = y.astype(o_ref.dtype)

    spec = pl.BlockSpec((lay.PS, POOL_W), lambda s: (s, 0))
    return pl.pallas_call(
        body, name=name, out_shape=_sds((lay.T, POOL_W), BF16), grid=(2,),
        in_specs=[spec, pl.BlockSpec(w_pool.shape, lambda s: (0, 0, 0)), pl.BlockSpec((1, POOL_W), lambda s: (0, 0))],
        out_specs=spec, compiler_params=_cp(("parallel",), VMEM_BIG))(u, w_pool, scale)


def pool_bwd(lay, u, dcat, w_pool, scale, name):
    segs = [(0, lay.C), (lay.C, lay.L)]

    def body(u_ref, dp_ref, w_ref, s_ref, du_ref, dw_ref, dsc_ref):
        for g, wd in enumerate(POOL_WINDOWS):
            sl = slice(g * 128, (g + 1) * 128)
            wb = w_ref[g].astype(BF16)
            dw = jnp.zeros((128, 128), F32)
            dsc = jnp.zeros((1, 128), F32)
            for r0, n in segs:
                x = u_ref[r0:r0 + n, sl]
                cnt = _wincount(n, wd // 2)
                d = (_winsum(x, wd // 2) / cnt - x).astype(BF16)
                dp = dp_ref[r0:r0 + n, sl]
                dsc += jnp.sum(_nn(d, wb) * dp, axis=0, keepdims=True)
                dyp = (dp * s_ref[:, sl]).astype(BF16)
                dw += _tn(d, dyp)
                dd = _nt(dyp, wb)
                du_ref[r0:r0 + n, sl] = _winsum(dd / cnt, wd // 2) - dd
            dw_ref[0, g] = dw
            dsc_ref[0, :, sl] = dsc

    spec = pl.BlockSpec((lay.PS, POOL_W), lambda s: (s, 0))
    return pl.pallas_call(
        body, name=name,
        out_shape=[_sds((lay.T, POOL_W), F32), _sds((2, 4, 128, 128), F32), _sds((2, 1, POOL_W), F32)], grid=(2,),
        in_specs=[spec, pl.BlockSpec((lay.PS, POOL_W), lambda s: (s, 1)), pl.BlockSpec(w_pool.shape, lambda s: (0, 0, 0)),
                  pl.BlockSpec((1, POOL_W), lambda s: (0, 0))],
        out_specs=[spec, pl.BlockSpec((1, 4, 128, 128), lambda s: (s, 0, 0, 0)), pl.BlockSpec((1, 1, POOL_W), lambda s: (s, 0, 0))],
        compiler_params=_cp(("parallel",), VMEM_BIG))(u, dcat, w_pool, scale)


CONV_OFFS = (-1, 0, 1, 2)
CW = 256


def _shift_rows(x, o):
    if o == 0:
        return x
    n = x.shape[0]
    t = lax.broadcasted_iota(jnp.int32, x.shape, 0)
    if o < 0:
        return jnp.where(t >= -o, pltpu.roll(x, -o, 0), 0.0)
    return jnp.where(t < n - o, pltpu.roll(x, n - o, 0), 0.0)


def conv_fwd(lay, p, col0, w, b, name):
    segs = [(0, lay.C), (lay.C, lay.L)]
    cb0 = col0 // CW

    def body(x_ref, w_ref, b_ref, o_ref):
        for r0, n in segs:
            x = x_ref[r0:r0 + n, :]
            y = jnp.broadcast_to(b_ref[...], x.shape)
            for k, o in enumerate(CONV_OFFS):
                y = y + _shift_rows(x, o) * w_ref[k:k + 1, :]
            o_ref[r0:r0 + n, :] = y

    return pl.pallas_call(
        body, name=name, out_shape=_sds((lay.T, D), F32), grid=(2, D // CW),
        in_specs=[pl.BlockSpec((lay.PS, CW), lambda s, j: (s, cb0 + j)), pl.BlockSpec((4, CW), lambda s, j: (0, j)),
                  pl.BlockSpec((1, CW), lambda s, j: (0, j))],
        out_specs=pl.BlockSpec((lay.PS, CW), lambda s, j: (s, j)),
        compiler_params=_cp(("parallel", "parallel")))(p, w, b)


def conv_bwd(lay, p, col0, w, duc, name):
    segs = [(0, lay.C), (lay.C, lay.L)]
    cb0 = col0 // CW

    def body(x_ref, w_ref, g_ref, du_ref, dw_ref, db_ref):
        dws = [jnp.zeros((1, CW), F32)] * 4
        db = jnp.zeros((1, CW), F32)
        for r0, n in segs:
            x = x_ref[r0:r0 + n, :]
            g = g_ref[r0:r0 + n, :]
            du = jnp.zeros_like(g)
            for k, o in enumerate(CONV_OFFS):
                du = du + _shift_rows(g, -o) * w_ref[k:k + 1, :]
                dws[k] = dws[k] + jnp.sum(g * _shift_rows(x, o), axis=0, keepdims=True)
            db = db + jnp.sum(g, axis=0, keepdims=True)
            du_ref[r0:r0 + n, :] = du
        dw_ref[0] = jnp.concatenate(dws, axis=0)
        db_ref[0] = db

    return pl.pallas_call(
        body, name=name, out_shape=[_sds((lay.T, D), F32), _sds((2, 4, D), F32), _sds((2, 1, D), F32)], grid=(2, D // CW),
        in_specs=[pl.BlockSpec((lay.PS, CW), lambda s, j: (s, cb0 + j)), pl.BlockSpec((4, CW), lambda s, j: (0, j)),
                  pl.BlockSpec((lay.PS, CW), lambda s, j: (s, j))],
        out_specs=[pl.BlockSpec((lay.PS, CW), lambda s, j: (s, j)), pl.BlockSpec((1, 4, CW), lambda s, j: (s, 0, j)),
                   pl.BlockSpec((1, 1, CW), lambda s, j: (s, 0, j))],
        compiler_params=_cp(("parallel", "parallel")))(p, w, duc)


def _softplus_neg(lam):
    z = -lam
    w = jnp.exp(-jnp.abs(z))
    log1p = jnp.where(w < 1e-2, w * (1.0 - w * (0.5 - w / 3.0)), jnp.log(1.0 + w))
    return jnp.maximum(z, 0.0) + log1p, -_sigmoid(z)


def _neg_expm1(x):
    series = -x * (1.0 + x * (0.5 + x * (1.0 / 6.0 + x * (1.0 / 24.0 + x * (1.0 / 120.0)))))
    return jnp.where(x > -0.05, series, 1.0 - jnp.exp(x))


def _lru_gates(x, xb, wa, wx, ba, bx, lam):
    r = _sigmoid(_nn(xb, wa.astype(BF16)) + ba)
    gi = _sigmoid(_nn(xb, wx.astype(BF16)) + bx)
    sp, dsp = _softplus_neg(lam)
    la = -LRU_C * r * sp
    a = jnp.exp(la)
    sq = jnp.sqrt(_neg_expm1(2.0 * la))
    return r, gi, sp, dsp, a, sq


def lru_coeffs(lay, uc, wa, wx, vec, name):
    tr = lay.tr

    def body(x_ref, wa_ref, wx_ref, v_ref, a_ref, b_ref):
        for h in range(8):
            sl = slice(h * 128, (h + 1) * 128)
            x = x_ref[:, sl]
            xb = x.astype(BF16)
            for d in range(2):
                _, gi, _, _, a, sq = _lru_gates(x, xb, wa_ref[d, h], wx_ref[d, h], v_ref[d:d + 1, sl],
                                                v_ref[2 + d:3 + d, sl], v_ref[4 + d:5 + d, sl])
                a_ref[d, h] = a
                b_ref[d, h] = sq * (gi * x)

    wspec = pl.BlockSpec((2, 8, 128, 128), lambda i: (0, 0, 0, 0))
    ospec = pl.BlockSpec((2, 8, tr, 128), lambda i: (0, 0, i, 0))
    return pl.pallas_call(
        body, name=name, out_shape=[_sds((2, 8, lay.T, 128), F32)] * 2, grid=(lay.nblk,),
        in_specs=[pl.BlockSpec((tr, D), lambda i: (i, 0)), wspec, wspec, pl.BlockSpec((6, D), lambda i: (0, 0))],
        out_specs=[ospec, ospec], compiler_params=_cp(("parallel",)))(uc, wa, wx, vec)


def lru_coeffs_bwd(lay, uc, wa, wx, vec, da, db, name):
    tr = lay.tr

    def body(x_ref, wa_ref, wx_ref, v_ref, da_ref, db_ref, dx_ref, dwa_ref, dwx_ref, dv_ref):
        @pl.when(pl.program_id(0) == 0)
        def _():
            dwa_ref[...] = jnp.zeros_like(dwa_ref)
            dwx_ref[...] = jnp.zeros_like(dwx_ref)
            dv_ref[...] = jnp.zeros_like(dv_ref)

        for h in range(8):
            sl = slice(h * 128, (h + 1) * 128)
            x = x_ref[:, sl]
            xb = x.astype(BF16)
            dx = jnp.zeros_like(x)
            for d in range(2):
                wab, wxb = wa_ref[d, h].astype(BF16), wx_ref[d, h].astype(BF16)
                r, gi, sp, dsp, a, sq = _lru_gates(x, xb, wa_ref[d, h], wx_ref[d, h], v_ref[d:d + 1, sl],
                                                   v_ref[2 + d:3 + d, sl], v_ref[4 + d:5 + d, sl])
                dbv, dav = db_ref[d, h], da_ref[d, h]
                t1 = dbv * sq
                dgi = t1 * x
                dx = dx + t1 * gi
                dla = dav * a - (dbv * gi * x) * (a * a) / sq
                dr = dla * (-LRU_C * sp)
                dlam = jnp.sum(dla * (-LRU_C * r), axis=0, keepdims=True) * dsp
                dpa = dr * r * (1.0 - r)
                dpx = dgi * gi * (1.0 - gi)
                dpab, dpxb = dpa.astype(BF16), dpx.astype(BF16)
                dwa_ref[d, h] += _tn(xb, dpab)
                dwx_ref[d, h] += _tn(xb, dpxb)
                dx = dx + _nt(dpab, wab) + _nt(dpxb, wxb)
                dv_ref[d:d + 1, sl] += jnp.sum(dpa, axis=0, keepdims=True)
                dv_ref[2 + d:3 + d, sl] += jnp.sum(dpx, axis=0, keepdims=True)
                dv_ref[4 + d:5 + d, sl] += dlam
            dx_ref[:, sl] = dx

    wspec = pl.BlockSpec((2, 8, 128, 128), lambda i: (0, 0, 0, 0))
    gspec = pl.BlockSpec((2, 8, tr, 128), lambda i: (0, 0, i, 0))
    vspec = pl.BlockSpec((6, D), lambda i: (0, 0))
    xspec = pl.BlockSpec((tr, D), lambda i: (i, 0))
    return pl.pallas_call(
        body, name=name,
        out_shape=[_sds((lay.T, D), F32), _sds((2, 8, 128, 128), F32), _sds((2, 8, 128, 128), F32), _sds((6, D), F32)],
        grid=(lay.nblk,), in_specs=[xspec, wspec, wspec, vspec, gspec, gspec],
        out_specs=[xspec, wspec, wspec, vspec], compiler_params=_cp(("arbitrary",)))(uc, wa, wx, vec, da, db)


GB = 2
SCAN_UNROLL = 4


def _tile_scan(a, b, up):
    t = lax.broadcasted_iota(jnp.int32, a.shape, 0)
    for d in (1, 2, 4):
        sh = 8 - d if up else d
        m = (t < 8 - d) if up else (t >= d)
        a_prev, b_prev = pltpu.roll(a, sh, 0), pltpu.roll(b, sh, 0)
        b = jnp.where(m, a * b_prev + b, b)
        a = jnp.where(m, a * a_prev, a)
    return a, b


def lru_scan(lay, a, b, name):
    segs = [(0, lay.C), (lay.C, lay.L)]

    def body(a_ref, b_ref, s_ref):
        for d in range(2):
            rev = d == 1
            state = tuple(jnp.zeros((1, 128), F32) for _ in range(GB))
            for base, n in segs:
                nt = n // 8

                def step(j, c, base=base, nt=nt, rev=rev, d=d):
                    c = list(c)
                    for u in range(SCAN_UNROLL):
                        jj = j * SCAN_UNROLL + u
                        r0 = pl.multiple_of(base + 8 * ((nt - 1 - jj) if rev else jj), 8)
                        for g in range(GB):
                            at, bt = _tile_scan(a_ref[d, g, pl.ds(r0, 8), :], b_ref[d, g, pl.ds(r0, 8), :], rev)
                            h = at * c[g] + bt
                            s_ref[d, g, pl.ds(r0, 8), :] = h
                            c[g] = h[0:1] if rev else h[7:8]
                    return tuple(c)

                state = lax.fori_loop(0, nt // SCAN_UNROLL, step, state)

    spec = pl.BlockSpec((2, GB, lay.PS, 128), lambda s, hb: (0, hb, s, 0))
    return pl.pallas_call(
        body, name=name, out_shape=_sds((2, 8, lay.T, 128), F32), grid=(2, 8 // GB),
        in_specs=[spec, spec], out_specs=spec, compiler_params=_cp(("parallel", "parallel"), VMEM_BIG))(a, b)


def lru_scan_bwd(lay, a, s, dy, name):
    segs = [(0, lay.C), (lay.C, lay.L)]
    C, PS = lay.C, lay.PS

    def body(a_ref, s_ref, g_ref, da_ref, db_ref):
        t = lax.broadcasted_iota(jnp.int32, (8, 128), 0)
        for d in range(2):
            rev = d == 1
            carry = tuple(jnp.zeros((1, 128), F32) for _ in range(GB))
            for si in (1, 0):
                base, n = segs[si]
                nt = n // 8

                def step(j, c, base=base, nt=nt, rev=rev, d=d):
                    c = list(c)
                    for u in range(SCAN_UNROLL):
                        jj = j * SCAN_UNROLL + u
                        r0 = pl.multiple_of(base + 8 * (jj if rev else (nt - 1 - jj)), 8)
                        if rev:
                            rn = pl.multiple_of(jnp.where(r0 == PS - 8, 0, r0 + 8), 8)
                            nb_zero = r0 == C - 8
                        else:
                            rn = pl.multiple_of(jnp.maximum(r0 - 8, 0), 8)
                            nb_zero = r0 == 0
                        for g in range(GB):
                            av = a_ref[d, g, pl.ds(r0, 8), :]
                            gv = g_ref[g, pl.ds(r0, 8), :]
                            sv = s_ref[d, g, pl.ds(r0, 8), :]
                            nbt = s_ref[d, g, pl.ds(rn, 8), :]
                            at, bt = _tile_scan(av, av * gv, not rev)
                            m = at * c[g] + bt
                            if rev:
                                m_next = jnp.where(t >= 1, pltpu.roll(m, 1, 0), c[g])
                                nb = jnp.where(nb_zero, 0.0, nbt[0:1])
                                h_prev = jnp.where(t < 7, pltpu.roll(sv, 7, 0), nb)
                                c[g] = m[7:8]
                            else:
                                m_next = jnp.where(t < 7, pltpu.roll(m, 7, 0), c[g])
                                nb = jnp.where(nb_zero, 0.0, nbt[7:8])
                                h_prev = jnp.where(t >= 1, pltpu.roll(sv, 1, 0), nb)
                                c[g] = m[0:1]
                            lam = gv + m_next
                            db_ref[d, g, pl.ds(r0, 8), :] = lam
                            da_ref[d, g, pl.ds(r0, 8), :] = lam * h_prev
                    return tuple(c)

                carry = lax.fori_loop(0, nt // SCAN_UNROLL, step, carry)

    spec = pl.BlockSpec((2, GB, lay.PS, 128), lambda s, hb: (0, hb, s, 0))
    return pl.pallas_call(
        body, name=name, out_shape=[_sds((2, 8, lay.T, 128), F32)] * 2, grid=(2, 8 // GB),
        in_specs=[spec, spec, pl.BlockSpec((GB, lay.PS, 128), lambda s, hb: (hb, s, 0))],
        out_specs=[spec, spec], compiler_params=_cp(("parallel", "parallel"), VMEM_BIG))(a, s, dy)


def _gelu(x):
    k = math.sqrt(2.0 / math.pi)
    t = jnp.tanh(k * (x + 0.044715 * x * x * x))
    return 0.5 * x * (1.0 + t), 0.5 * (1.0 + t) + 0.5 * x * (1.0 - t * t) * k * (1.0 + 3 * 0.044715 * x * x)


def lru_gate(lay, p, s, name):
    tr = lay.tr

    def body(g_ref, s_ref, o_ref):
        for h in range(8):
            sl = slice(h * 128, (h + 1) * 128)
            o_ref[:, sl] = (_gelu(g_ref[:, sl])[0] * (s_ref[0, h] + s_ref[1, h])).astype(o_ref.dtype)

    return pl.pallas_call(
        body, name=name, out_shape=_sds((lay.T, D), BF16), grid=(lay.nblk,),
        in_specs=[pl.BlockSpec((tr, D), lambda i: (i, 0)), pl.BlockSpec((2, 8, tr, 128), lambda i: (0, 0, i, 0))],
        out_specs=pl.BlockSpec((tr, D), lambda i: (i, 0)), compiler_params=_cp(("parallel",)))(p, s)


def lru_gate_bwd(lay, p, s, do, name):
    tr = lay.tr

    def body(g_ref, s_ref, do_ref, dg_ref, dy_ref):
        for h in range(8):
            sl = slice(h * 128, (h + 1) * 128)
            ge, dge = _gelu(g_ref[:, sl])
            dov = do_ref[:, sl]
            dg_ref[:, sl] = dov * (s_ref[0, h] + s_ref[1, h]) * dge
            dy_ref[h] = dov * ge

    xspec = pl.BlockSpec((tr, D), lambda i: (i, 0))
    return pl.pallas_call(
        body, name=name, out_shape=[_sds((lay.T, D), F32), _sds((8, lay.T, 128), F32)], grid=(lay.nblk,),
        in_specs=[xspec, pl.BlockSpec((2, 8, tr, 128), lambda i: (0, 0, i, 0)), xspec],
        out_specs=[xspec, pl.BlockSpec((8, tr, 128), lambda i: (0, i, 0))],
        compiler_params=_cp(("parallel",)))(p, s, do)


def silu_rows(x, name):
    def body(x_ref, o_ref):
        v = x_ref[...]
        o_ref[...] = (v * _sigmoid(v)).astype(o_ref.dtype)
    return pl.pallas_call(body, name=name, out_shape=_sds(x.shape, BF16), in_specs=[VMEM_SPEC], out_specs=VMEM_SPEC)(x)


def mod_grad_rows(gath, name):
    w = gath.shape[-1]

    def body(g_ref, dm_ref, db_ref):
        dm_ref[...] = jnp.zeros_like(dm_ref)
        for l in range(2):
            ctx = g_ref[0, 3 * l + 2:3 * l + 3, :]
            tot = g_ref[0, 3 * l:3 * l + 1, :] + g_ref[0, 3 * l + 1:3 * l + 2, :]
            for k in range(8):
                dm_ref[l, 2 * k:2 * k + 2, :] = g_ref[k, 3 * l:3 * l + 2, :]
                if k:
                    ctx = ctx + g_ref[k, 3 * l + 2:3 * l + 3, :]
                    tot = tot + (g_ref[k, 3 * l:3 * l + 1, :] + g_ref[k, 3 * l + 1:3 * l + 2, :])
            dm_ref[l, 16:17, :] = ctx
            db_ref[l:l + 1, :] = tot + ctx

    return pl.pallas_call(body, name=name, out_shape=[_sds((2, 32, w), F32), _sds((2, w), F32)],
                          in_specs=[VMEM_SPEC], out_specs=[VMEM_SPEC, VMEM_SPEC])(gath)


def cctx_grad(p0, p1, c_ctx, name):
    def body(a_ref, b_ref, c_ref, o_ref):
        cv = c_ref[...]
        sg = _sigmoid(cv)
        o_ref[...] = 0.5 * (a_ref[0:1, :] + b_ref[0:1, :]) * (sg * (1.0 + cv * (1.0 - sg)))
    return pl.pallas_call(body, name=name, out_shape=_sds((1, D), F32), in_specs=[VMEM_SPEC] * 3,
                          out_specs=VMEM_SPEC)(p0, p1, c_ctx)


def loss_and_grad(lay, h, tgt, name):
    def fn(hb, tb):
        lat = (pl.program_id(0) % lay.bps) >= lay.cb
        e = jnp.where(lat, hb - tb, 0.0)
        return e * (1.0 / D), jnp.sum(e * e, axis=0, keepdims=True) * (0.5 / D)
    return rowwise(lay, name, fn, [h, tgt], outs=[(D, F32)], sums=[(1, D)])


def adamw(w, g, m, v, name):
    shape = w.shape
    w2, g2, m2, v2 = (t.reshape(-1, shape[-1]) for t in (w, g, m, v))
    rows, width = w2.shape
    tr = 256 if rows % 256 == 0 else rows
    c1 = 1.0 - ADAM_B1 ** ADAM_STEP
    c2 = 1.0 - ADAM_B2 ** ADAM_STEP

    def body(w_ref, g_ref, m_ref, v_ref, d_ref, mo_ref, vo_ref):
        gv = g_ref[...]
        mn = ADAM_B1 * m_ref[...] + (1.0 - ADAM_B1) * gv
        vn = ADAM_B2 * v_ref[...] + (1.0 - ADAM_B2) * (gv * gv)
        d_ref[...] = -ADAM_LR * ((mn / c1) / (jnp.sqrt(vn / c2) + ADAM_EPS) + ADAM_WD * w_ref[...])
        mo_ref[...] = mn
        vo_ref[...] = vn

    spec = pl.BlockSpec((tr, width), lambda i: (i, 0))
    d, mn, vn = pl.pallas_call(body, name=name, out_shape=[_sds((rows, width), F32)] * 3, grid=(rows // tr,),
                               in_specs=[spec] * 4, out_specs=[spec] * 3, compiler_params=_cp(("parallel",)))(w2, g2, m2, v2)
    return d.reshape(shape), mn.reshape(shape), vn.reshape(shape)


HEAD_PERM = (0, 4, 1, 5, 2, 6, 3, 7)


def _rot_rows(wt):
    return jnp.concatenate([-wt[32:64], wt[0:32]], axis=0)


def _unrot_rows(g):
    return jnp.concatenate([g[32:64], -g[0:32]], axis=0)


def _heads(a, n):
    return [a[64 * i:64 * (i + 1)] for i in range(n)]


def kernel(x, c, ctx, c_ctx, w_mod, b_mod, ln_g, ln_b, ffn_w_gate, ffn_w_up, ffn_w_down, mix_ab_w_in, attn_sink, pool_w, pool_scale, mix_ab_w_out, lru_w_in, lru_conv_w, lru_conv_b, lru_wa, lru_ba, lru_wx, lru_bx, lru_lambda, lru_w_out, loss_target, m_c_ctx, m_w_mod, m_b_mod, m_ln_g, m_ln_b, m_ffn_w_gate, m_ffn_w_up, m_ffn_w_down, m_mix_ab_w_in, m_attn_sink, m_pool_w, m_pool_scale, m_mix_ab_w_out, m_lru_w_in, m_lru_conv_w, m_lru_conv_b, m_lru_wa, m_lru_ba, m_lru_wx, m_lru_bx, m_lru_lambda, m_lru_w_out, v_c_ctx, v_w_mod, v_b_mod, v_ln_g, v_ln_b, v_ffn_w_gate, v_ffn_w_up, v_ffn_w_down, v_mix_ab_w_in, v_attn_sink, v_pool_w, v_pool_scale, v_mix_ab_w_out, v_lru_w_in, v_lru_conv_w, v_lru_conv_b, v_lru_wa, v_lru_ba, v_lru_wx, v_lru_bx, v_lru_lambda, v_lru_w_out):
    n_lat, n_ctx = x.shape[1], ctx.shape[1]
    lay = Layout(n_ctx, n_lat)
    T = lay.T
    ns = ffn_w_gate.shape[-1]
    n_li, n_ai = lru_w_in.shape[-1], mix_ab_w_in.shape[-1]
    n_ao, n_lo = mix_ab_w_out.shape[1], lru_w_out.shape[1]
    wm = w_mod.shape[-1]
    dsh = ln_g.shape[-1]
    mx, my, mc = lax.axis_index("x"), lax.axis_index("y"), lax.axis_index("c")
    chip = 2 * mx + my
    me = 2 * chip + mc

    c_all = all_gather8(c, "ag8_c").reshape(16, D)
    cc = jnp.concatenate([c_all, c_ctx[None, :], jnp.zeros((15, D), F32)], axis=0)
    sc = silu_rows(cc, "silu_c")
    slabs = []
    for l in range(DEPTH):
        bias = lax.dynamic_slice(b_mod[l][None, :], (0, chip * wm), (1, wm))
        slabs.append(mm_nn(sc, w_mod[l], f"mod_mm{l}", bias=bias))
    modg = all_gather_chips(jnp.stack(slabs), "ag_mod")
    modtab = []
    for l in range(DEPTH):
        full = jnp.transpose(modg[:, l], (1, 0, 2)).reshape(32, N_CHIP * wm)
        mine = lax.dynamic_slice(full, (2 * me, 0), (2, N_CHIP * wm))
        modtab.append(jnp.concatenate([mine, full[16:17]], axis=0).reshape(3, N_MOD, D))

    small = jnp.concatenate([ln_g.reshape(6, dsh), ln_b.reshape(6, dsh), lru_conv_w[0], lru_conv_b, lru_ba[0],
                             lru_bx[0], lru_lambda[0], jnp.zeros((9, dsh), F32)], axis=0)
    small = all_gather_chips(small.reshape(2, 16, dsh), "ag_small").reshape(N_CHIP, 32, dsh)
    small = jnp.transpose(small, (1, 0, 2)).reshape(32, D)
    ln_g_f, ln_b_f = small[0:6].reshape(2, 3, D), small[6:12].reshape(2, 3, D)
    conv_w_f, conv_b_f = small[12:16], small[16:17]
    lru_vec = small[17:23]

    ffn_sh = jnp.stack([jnp.swapaxes(ffn_w_gate, -1, -2), jnp.swapaxes(ffn_w_up, -1, -2), ffn_w_down], axis=2)
    ffn_sh = ffn_sh.astype(BF16).reshape(2, 6 * ns, D)
    wbuf = all_gather_chips(ffn_sh, "ag_ffn").reshape(N_CHIP, 12 * ns, D)
    mix_sh = jnp.concatenate([lru_w_in[0].T, mix_ab_w_in[0].T, mix_ab_w_out[0], lru_w_out[0]], axis=0).astype(BF16)
    n_mix = n_li + n_ai + n_ao + n_lo
    mixw = all_gather_chips(mix_sh.reshape(2, n_mix // 2, D), "ag_mix").reshape(N_CHIP, n_mix, D)
    o1, o2, o3 = n_li, n_li + n_ai, n_li + n_ai + n_ao
    lru_in_t = mixw[:, 0:o1].reshape(N_CHIP * n_li, D)
    ab_in_t = mixw[:, o1:o2].reshape(N_CHIP * n_ai, D)
    ab_out = mixw[:, o2:o3].reshape(N_CHIP * n_ao, D)
    lru_out = mixw[:, o3:].reshape(N_CHIP * n_lo, D)
    qh, kh = _heads(ab_in_t[Q0:K0], N_HEADS), _heads(ab_in_t[K0:V0], N_KV)
    w_ext_t = jnp.concatenate([qh[h] for h in HEAD_PERM] + [ab_in_t[K0:QR0]]
                              + [_rot_rows(qh[h]) for h in HEAD_PERM] + [_rot_rows(t) for t in kh], axis=0)
    oh = _heads(ab_out[0:ATT_W], N_HEADS)
    w_out_ext = jnp.concatenate([oh[h] for h in HEAD_PERM] + [ab_out[ATT_W:]], axis=0)

    t = jnp.arange(n_lat)
    inv = ROPE_THETA ** (-jnp.arange(16, dtype=F32) / 16.0)
    ang = jnp.concatenate([(t // GRID_W).astype(F32)[:, None] * inv, (t % GRID_W).astype(F32)[:, None] * inv], axis=-1)
    cos1 = jnp.concatenate([jnp.ones((n_ctx, 32), F32), jnp.cos(ang)], axis=0)
    sin1 = jnp.concatenate([jnp.zeros((n_ctx, 32), F32), jnp.sin(ang)], axis=0)
    cos_t = jnp.tile(cos1, (2, 4))
    sin_t = jnp.tile(sin1, (2, 4))
    sk = attn_sink[0]
    sink_tab = jnp.concatenate([jnp.repeat(jnp.stack([sk[:4], sk[4:]], axis=1), HEAD_DIM, axis=1),
                                jnp.zeros((4, 128), F32)], axis=0)
    pscale = pool_scale.reshape(1, POOL_W)

    h0 = jnp.concatenate([ctx, x], axis=1).reshape(T, D)
    tgt = jnp.concatenate([jnp.zeros_like(ctx), loss_target], axis=1).reshape(T, D)

    def lnv(l, j):
        return jnp.stack([ln_g_f[l, j], ln_b_f[l, j]])

    def fq(l, f, kind):
        return (l * 2 + f) * 3 + kind

    def ffn_fwd(h, l, f, k0, j):
        tag = f"l{l}f{f}"
        hm = modulate(lay, h, modtab[l], k0, k0 + 1, f"mod_{tag}")
        g, u, a = ffn_up(lay, hm, wbuf, fq(l, f, 0), fq(l, f, 1), ns, f"ffn_up_{tag}")
        y = slab_nn_acc(lay, [a], wbuf, [fq(l, f, 2)], ns, f"ffn_down_{tag}")
        out, xhat, rstd = resid_ln(lay, h, y, modtab[l], k0 + 2, 0.5, lnv(l, j), f"ln_{tag}")
        return out, dict(h=h, hm=hm, g=g, u=u, a=a, y=y, xhat=xhat, rstd=rstd)

    h1, r_f00 = ffn_fwd(h0, 0, 0, 0, 0)
    hm_a = modulate(lay, h1, modtab[0], 3, 4, "mod_mixa")
    p_a = mm_nt(hm_a, w_ext_t, "mixa_in")
    qr, kr, vb, u_a = rope_fwd(lay, p_a, cos_t, sin_t, "rope")
    att, lse = attn_fwd(lay, qr, kr, vb, sink_tab, "attn")
    pool = pool_fwd(lay, u_a, pool_w[0], pscale, "pool")
    cat = jnp.concatenate([att, pool], axis=1)
    y_a = mm_nn(cat, w_out_ext, "mixa_out")
    h2, xhat_a, rstd_a = resid_ln(lay, h1, y_a, modtab[0], 5, 1.0, lnv(0, 1), "ln_mixa")
    h3, r_f01 = ffn_fwd(h2, 0, 1, 6, 2)

    h4, r_f10 = ffn_fwd(h3, 1, 0, 0, 0)
    hm_c = modulate(lay, h4, modtab[1], 3, 4, "mod_mixc")
    p_c = mm_nt(hm_c, lru_in_t, "mixc_in")
    uc = conv_fwd(lay, p_c, D, conv_w_f, conv_b_f, "conv")
    a_c, b_c = lru_coeffs(lay, uc, lru_wa[0], lru_wx[0], lru_vec, "lru_coef")
    s_c = lru_scan(lay, a_c, b_c, "lru_scan")
    o_c = lru_gate(lay, p_c, s_c, "lru_gate")
    y_c = mm_nn(o_c, lru_out, "mixc_out")
    h5, xhat_c, rstd_c = resid_ln(lay, h4, y_c, modtab[1], 5, 1.0, lnv(1, 1), "ln_mixc")
    h6, r_f11 = ffn_fwd(h5, 1, 1, 6, 2)

    dh, lparts = loss_and_grad(lay, h6, tgt, "loss")
    loss = lax.psum(jnp.sum(lparts), ("x", "y", "c"))

    gbuf = lax.empty((N_CHIP, 12 * ns, D), F32)
    dln = {}
    dms = {}

    def ffn_bwd(dout, r, l, f, k0, j, gbuf):
        tag = f"l{l}f{f}"
        dy, dres, s1 = ln_bwd(lay, dout, r["xhat"], r["rstd"], r["y"], modtab[l], k0 + 2, 0.5, lnv(l, j), f"lnb_{tag}")
        dg, du = ffn_bwd_da(lay, dy, wbuf, fq(l, f, 2), r["g"], r["u"], ns, f"ffn_da_{tag}")
        gbuf = slab_tn(lay, r["a"], dy, gbuf, fq(l, f, 2), ns, f"ffn_dwd_{tag}")
        gbuf = slab_tn(lay, dg, r["hm"], gbuf, fq(l, f, 0), ns, f"ffn_dwg_{tag}")
        gbuf = slab_tn(lay, du, r["hm"], gbuf, fq(l, f, 1), ns, f"ffn_dwu_{tag}")
        dhm = slab_nn_acc(lay, [dg, du], wbuf, [fq(l, f, 0), fq(l, f, 1)], ns, f"ffn_dh_{tag}")
        dh_in, s2 = mod_bwd(lay, dres, dhm, r["h"], modtab[l], k0 + 1, f"modb_{tag}")
        dln[(l, j)] = block_sums(lay, s1, f"bs_ln_{tag}")
        dms[(l, k0)] = block_sums(lay, s2, f"bs_mod_{tag}")
        return dh_in, gbuf

    dh, gbuf = ffn_bwd(dh, r_f11, 1, 1, 6, 2, gbuf)
    dy, dres, s1 = ln_bwd(lay, dh, xhat_c, rstd_c, y_c, modtab[1], 5, 1.0, lnv(1, 1), "lnb_mixc")
    do_c = mm_nt(dy, lru_out, "mixc_out_dx")
    g_lru_out = mm_tn(o_c, dy, "mixc_out_dw")
    dgate, dyg = lru_gate_bwd(lay, p_c, s_c, do_c, "lru_gate_b")
    da_c, db_c = lru_scan_bwd(lay, a_c, s_c, dyg, "lru_scan_b")
    duc, g_wa, g_wx, g_vec = lru_coeffs_bwd(lay, uc, lru_wa[0], lru_wx[0], lru_vec, da_c, db_c, "lru_coef_b")
    du_c, g_cw, g_cb = conv_bwd(lay, p_c, D, conv_w_f, duc, "conv_b")
    dp_c = jnp.concatenate([dgate, du_c], axis=1).astype(BF16)
    dhm = mm_nn(dp_c, lru_in_t, "mixc_in_dx")
    g_lru_in_t = mm_tn(dp_c, hm_c, "mixc_in_dw")
    dh, s2 = mod_bwd(lay, dres, dhm, h4, modtab[1], 4, "modb_mixc")
    dln[(1, 1)] = block_sums(lay, s1, "bs_ln_mixc")
    dms[(1, 3)] = block_sums(lay, s2, "bs_mod_mixc")
    dh, gbuf = ffn_bwd(dh, r_f10, 1, 0, 0, 0, gbuf)

    dh, gbuf = ffn_bwd(dh, r_f01, 0, 1, 6, 2, gbuf)
    dy, dres, s1 = ln_bwd(lay, dh, xhat_a, rstd_a, y_a, modtab[0], 5, 1.0, lnv(0, 1), "lnb_mixa")
    dcat = mm_nt(dy, w_out_ext, "mixa_out_dx")
    g_out_ext = mm_tn(cat, dy, "mixa_out_dw")
    dqr, dkr, dv, g_sink = attn_bwd(lay, qr, kr, vb, sink_tab, lse, dcat, "attn_b")
    du_a, g_pw, g_ps = pool_bwd(lay, u_a, dcat, pool_w[0], pscale, "pool_b")
    dp_a = rope_bwd(lay, dqr, dkr, dv, du_a, cos_t, sin_t, "rope_b")
    dhm = mm_nn(dp_a, w_ext_t, "mixa_in_dx")
    g_ext_t = mm_tn(dp_a, hm_a, "mixa_in_dw")
    dh, s2 = mod_bwd(lay, dres, dhm, h1, modtab[0], 4, "modb_mixa")
    dln[(0, 1)] = block_sums(lay, s1, "bs_ln_mixa")
    dms[(0, 3)] = block_sums(lay, s2, "bs_mod_mixa")
    dh, gbuf = ffn_bwd(dh, r_f00, 0, 0, 0, 0, gbuf)
    grad_x = dh.reshape(2, lay.PS, D)[:, n_ctx:]

    rows = []
    for l in range(DEPTH):
        per_k = []
        for k0, j in ((0, 0), (3, 1), (6, 2)):
            per_k += [dms[(l, k0)][:3, 0], dms[(l, k0)][:3, 1], dln[(l, j)][:3, 2]]
        rows.append(jnp.stack(per_k, axis=1).reshape(3, N_MOD * D))
    dmod_loc = jnp.concatenate(rows + [jnp.zeros((2, N_MOD * D), F32)], axis=0)
    dmod_all, g_b_mod = mod_grad_rows(all_gather8(dmod_loc, "ag8_dmod"), "dmod_rows")
    g_w_mod, cparts = [], []
    for l in range(DEPTH):
        dcol = lax.dynamic_slice(dmod_all[l], (0, chip * wm), (32, wm))
        g_w_mod.append(mm_tn(sc, dcol.astype(BF16), f"wmod_dw{l}"))
        cparts.append(mm_nt(dcol[16:32], w_mod[l], f"cctx_dx{l}"))
    g_w_mod = jnp.stack(g_w_mod)
    g_cctx = cctx_grad(cparts[0], cparts[1], c_ctx[None, :], "cctx_grad")

    gq = _heads(g_ext_t[Q0:K0], N_HEADS)
    gqr = _heads(g_ext_t[QR0:KR0], N_HEADS)
    g_q = [None] * N_HEADS
    for i, h in enumerate(HEAD_PERM):
        g_q[h] = gq[i] + _unrot_rows(gqr[i])
    gk = [a + _unrot_rows(b) for a, b in zip(_heads(g_ext_t[K0:V0], N_KV), _heads(g_ext_t[KR0:PEXT], N_KV))]
    g_ab_in_t = jnp.concatenate(g_q + gk + [g_ext_t[V0:QR0]], axis=0)
    go = _heads(g_out_ext[0:ATT_W], N_HEADS)
    g_o = [None] * N_HEADS
    for i, h in enumerate(HEAD_PERM):
        g_o[h] = go[i]
    g_ab_out = jnp.concatenate(g_o + [g_out_ext[ATT_W:]], axis=0)
    mix_g = jnp.concatenate([g_lru_in_t.reshape(N_CHIP, n_li, D), g_ab_in_t.reshape(N_CHIP, n_ai, D),
                             g_ab_out.reshape(N_CHIP, n_ao, D), g_lru_out.reshape(N_CHIP, n_lo, D)], axis=1)

    g_ln_g = jnp.stack([jnp.stack([dln[(l, j)][3, 1] for j in range(3)]) for l in range(DEPTH)])
    g_ln_b = jnp.stack([jnp.stack([dln[(l, j)][3, 0] for j in range(3)]) for l in range(DEPTH)])
    sink_row = jnp.sum(g_sink, axis=0)[:4]
    g_sink8 = jnp.concatenate([sink_row[:, 0], sink_row[:, HEAD_DIM]])
    misc = jnp.concatenate([g_sink8, jnp.sum(g_ps, axis=0).reshape(POOL_W), jnp.zeros((D - 8 - POOL_W,), F32)])
    small_g = jnp.concatenate([
        g_ln_g.reshape(6, D), g_ln_b.reshape(6, D), jnp.sum(g_cw, axis=0), jnp.sum(g_cb, axis=0), g_vec,
        misc[None, :], jnp.sum(g_pw, axis=0).reshape(64, D), g_wa.reshape(256, D), g_wx.reshape(256, D), g_cctx,
        jnp.zeros((39, D), F32)], axis=0)
    n_small = small_g.shape[0] // N_CHIP
    mix_buf = jnp.concatenate([mix_g, small_g.reshape(N_CHIP, n_small, D)], axis=1)
    n_mb = n_mix + n_small

    ffn_red = reduce_scatter_chips(gbuf.reshape(N_CHIP, 2, 6 * ns, D), "ffn", wire=BF16).reshape(12 * ns, D)
    mix_red = reduce_scatter_chips(mix_buf.reshape(N_CHIP, 2, n_mb // 2, D), "mix").reshape(n_mb, D)
    small_red = all_gather_chips(mix_red[n_mix:].reshape(2, n_small // 2, D), "ag_smallg").reshape(N_CHIP * n_small, D)

    fr = ffn_red.reshape(2, 2, 3, ns, D)
    g_gate, g_up, g_down = jnp.swapaxes(fr[:, :, 0], -1, -2), jnp.swapaxes(fr[:, :, 1], -1, -2), fr[:, :, 2]

    def cols(a):
        return lax.dynamic_slice_in_dim(a, chip * dsh, dsh, axis=a.ndim - 1)

    sr = small_red
    grads = dict(
        c_ctx=sr[600], w_mod=g_w_mod, b_mod=g_b_mod,
        ln_g=cols(sr[0:6]).reshape(2, 3, dsh), ln_b=cols(sr[6:12]).reshape(2, 3, dsh),
        ffn_w_gate=g_gate, ffn_w_up=g_up, ffn_w_down=g_down,
        mix_ab_w_in=mix_red[o1:o2].T[None], attn_sink=sr[23, 0:8][None], pool_w=sr[24:88].reshape(1, 4, 128, 128),
        pool_scale=sr[23, 8:8 + POOL_W][None], mix_ab_w_out=mix_red[o2:o3][None], lru_w_in=mix_red[0:o1].T[None],
        lru_conv_w=cols(sr[12:16])[None], lru_conv_b=cols(sr[16:17]), lru_wa=sr[88:344].reshape(1, 2, 8, 128, 128),
        lru_ba=cols(sr[17:19])[None], lru_wx=sr[344:600].reshape(1, 2, 8, 128, 128), lru_bx=cols(sr[19:21])[None],
        lru_lambda=cols(sr[21:23])[None], lru_w_out=mix_red[o3:n_mix][None])
    params = dict(c_ctx=(c_ctx, m_c_ctx, v_c_ctx), w_mod=(w_mod, m_w_mod, v_w_mod), b_mod=(b_mod, m_b_mod, v_b_mod),
                  ln_g=(ln_g, m_ln_g, v_ln_g), ln_b=(ln_b, m_ln_b, v_ln_b),
                  ffn_w_gate=(ffn_w_gate, m_ffn_w_gate, v_ffn_w_gate), ffn_w_up=(ffn_w_up, m_ffn_w_up, v_ffn_w_up),
                  ffn_w_down=(ffn_w_down, m_ffn_w_down, v_ffn_w_down),
                  mix_ab_w_in=(mix_ab_w_in, m_mix_ab_w_in, v_mix_ab_w_in), attn_sink=(attn_sink, m_attn_sink, v_attn_sink),
                  pool_w=(pool_w, m_pool_w, v_pool_w), pool_scale=(pool_scale, m_pool_scale, v_pool_scale),
                  mix_ab_w_out=(mix_ab_w_out, m_mix_ab_w_out, v_mix_ab_w_out), lru_w_in=(lru_w_in, m_lru_w_in, v_lru_w_in),
                  lru_conv_w=(lru_conv_w, m_lru_conv_w, v_lru_conv_w), lru_conv_b=(lru_conv_b, m_lru_conv_b, v_lru_conv_b),
                  lru_wa=(lru_wa, m_lru_wa, v_lru_wa), lru_ba=(lru_ba, m_lru_ba, v_lru_ba), lru_wx=(lru_wx, m_lru_wx, v_lru_wx),
                  lru_bx=(lru_bx, m_lru_bx, v_lru_bx), lru_lambda=(lru_lambda, m_lru_lambda, v_lru_lambda),
                  lru_w_out=(lru_w_out, m_lru_w_out, v_lru_w_out))
    gl, dl, ml, vl = [], [], [], []
    for name, (w, m, v) in params.items():
        g = grads[name].reshape(w.shape)
        d, mn, vn = adamw(w, g, m, v, f"adamw_{name}")
        gl.append(g)
        dl.append(d)
        ml.append(mn)
        vl.append(vn)
    return (loss, grad_x, *gl, *dl, *ml, *vl)
```

```python
import functools
import math

import jax
import jax.numpy as jnp
from jax import lax
from jax.experimental import pallas as pl
from jax.experimental.pallas import tpu as pltpu

F32, BF16 = jnp.float32, jnp.bfloat16
MESH = pl.DeviceIdType.MESH
ANY = pl.BlockSpec(memory_space=pl.ANY)
VMEM_SPEC = pl.BlockSpec(memory_space=pltpu.VMEM)

D = 1024
N_CHIP = 4
HEAD_DIM, N_HEADS, N_KV = 64, 8, 2
ATT_W, KV_W, POOL_W = 512, 128, 512
POOL_WINDOWS = (2, 4, 8, 16)
BLK = 128
ATT_SCALE = HEAD_DIM ** -0.5
ROPE_THETA = 10000.0
GRID_W = 64
LRU_C = 8.0
LN_EPS = 1e-5
NEG_INF = -1e30
DEPTH = 2
ALPHA = (2 * DEPTH) ** 0.25
N_MOD = 9
ADAM_LR, ADAM_B1, ADAM_B2, ADAM_EPS, ADAM_WD, ADAM_STEP = 0.001, 0.9, 0.999, 1e-08, 0.01, 10
VMEM_BIG = 48 * 1024 * 1024


def _cp(sem=None, vmem=None):
    kw = {}
    if sem is not None:
        kw["dimension_semantics"] = sem
    if vmem is not None:
        kw["vmem_limit_bytes"] = vmem
    return pltpu.CompilerParams(**kw)


def _sds(shape, dtype):
    return jax.ShapeDtypeStruct(tuple(shape), dtype)


def _pick(n, cands):
    for c in cands:
        if n % c == 0:
            return c
    return n


def _dot(a, b, dims):
    return lax.dot_general(a, b, (dims, ((), ())), preferred_element_type=F32)


def _nn(a, b):
    return _dot(a, b, ((1,), (0,)))


def _nt(a, b):
    return _dot(a, b, ((1,), (1,)))


def _tn(a, b):
    return _dot(a, b, ((0,), (0,)))


def _sigmoid(x):
    return 1.0 / (1.0 + jnp.exp(-x))


def _me():
    return lax.axis_index("x"), lax.axis_index("y"), lax.axis_index("c")


def _rcopy(src, dst, ssem, rsem, dev):
    return pltpu.make_async_remote_copy(src_ref=src, dst_ref=dst, send_sem=ssem, recv_sem=rsem,
                                        device_id=dev, device_id_type=MESH)


def all_gather8(x, name):
    def body(x_ref, o_ref, ssem, rsem, lsem):
        mx, my, mc = _me()
        me = 4 * mx + 2 * my + mc
        loc = pltpu.make_async_copy(x_ref, o_ref.at[me], lsem)
        loc.start()
        peers = []
        for m in range(1, 8):
            px = 1 - mx if (m >> 2) & 1 else mx
            py = 1 - my if (m >> 1) & 1 else my
            pc = 1 - mc if m & 1 else mc
            peers.append((px, py, pc))
        sends = [_rcopy(x_ref, o_ref.at[me], ssem.at[k], rsem.at[k], p) for k, p in enumerate(peers)]
        for cp in sends:
            cp.start()
        for k, (px, py, pc) in enumerate(peers):
            _rcopy(x_ref, o_ref.at[4 * px + 2 * py + pc], ssem.at[k], rsem.at[k], (px, py, pc)).wait_recv()
        for cp in sends:
            cp.wait_send()
        loc.wait()

    return pl.pallas_call(
        body, name=name, out_shape=_sds((8,) + x.shape, x.dtype),
        in_specs=[VMEM_SPEC], out_specs=VMEM_SPEC,
        scratch_shapes=[pltpu.SemaphoreType.DMA((7,)), pltpu.SemaphoreType.DMA((7,)), pltpu.SemaphoreType.DMA],
    )(x)


_ROW_BLOCKS = (512, 384, 256, 224, 128)


def _idx(v):
    return jnp.reshape(v, (1,)).astype(jnp.int32)


def place_slab(shard, name):
    _, h, w = shard.shape
    th = _pick(h, _ROW_BLOCKS)

    def body(s_ref, x_ref, o_ref):
        del s_ref
        o_ref[...] = x_ref[...]

    return pl.pallas_call(
        body, name=name, out_shape=_sds((N_CHIP,) + shard.shape, shard.dtype),
        grid_spec=pltpu.PrefetchScalarGridSpec(
            num_scalar_prefetch=1, grid=(2, h // th),
            in_specs=[pl.BlockSpec((None, th, w), lambda k, r, s: (k, r, 0))],
            out_specs=pl.BlockSpec((None, None, th, w), lambda k, r, s: (s[0], k, r, 0))),
    )(_idx(2 * lax.axis_index("x") + lax.axis_index("y")), shard)


def all_gather_chips(shard, name):
    def body(x_ref, o_ref, ssem, rsem):
        del x_ref
        mx, my, mc = _me()
        s = 2 * mx + my
        sib = (mx, my, 1 - mc)
        chips = [(1 - mx, my), (mx, 1 - my), (1 - mx, 1 - my)]
        first = [_rcopy(o_ref.at[s, mc], o_ref.at[s, mc], ssem.at[j], rsem.at[j], (px, py, mc))
                 for j, (px, py) in enumerate(chips)]
        for cp in first:
            cp.start()
        passed = []
        for j, (px, py) in enumerate(chips):
            ps = 2 * px + py
            _rcopy(o_ref.at[ps, mc], o_ref.at[ps, mc], ssem.at[j], rsem.at[j], (px, py, mc)).wait_recv()
            fw = _rcopy(o_ref.at[ps, mc], o_ref.at[ps, mc], ssem.at[3 + j], rsem.at[3 + j], sib)
            fw.start()
            passed.append(fw)
        for j, (px, py) in enumerate(chips):
            ps = 2 * px + py
            _rcopy(o_ref.at[ps, 1 - mc], o_ref.at[ps, 1 - mc], ssem.at[3 + j], rsem.at[3 + j], sib).wait_recv()
        for cp in first + passed:
            cp.wait_send()

    full = place_slab(shard, name + "_place")
    return pl.pallas_call(
        body, name=name, out_shape=_sds(full.shape, full.dtype), in_specs=[ANY], out_specs=ANY,
        input_output_aliases={0: 0},
        scratch_shapes=[pltpu.SemaphoreType.DMA((6,)), pltpu.SemaphoreType.DMA((6,))],
    )(full)


def sibling_send_other_half(buf, name):
    def body(x_ref, o_ref, ssem, rsem):
        mx, my, mc = _me()
        sib = (mx, my, 1 - mc)
        cps = [_rcopy(x_ref.at[k, 1 - mc], o_ref.at[k], ssem.at[k], rsem.at[k], sib) for k in range(N_CHIP)]
        for cp in cps:
            cp.start()
        for cp in cps:
            cp.wait_recv()
        for cp in cps:
            cp.wait_send()

    n, _, h, w = buf.shape
    return pl.pallas_call(
        body, name=name, out_shape=_sds((n, h, w), buf.dtype), in_specs=[ANY], out_specs=ANY,
        scratch_shapes=[pltpu.SemaphoreType.DMA((N_CHIP,)), pltpu.SemaphoreType.DMA((N_CHIP,))],
    )(buf)


def chips_all_to_all(q, name):
    def body(x_ref, o_ref, ssem, rsem):
        mx, my, mc = _me()
        s = 2 * mx + my
        chips = [(1 - mx, my), (mx, 1 - my), (1 - mx, 1 - my)]
        cps = [_rcopy(x_ref.at[2 * px + py], o_ref.at[s], ssem.at[j], rsem.at[j], (px, py, mc))
               for j, (px, py) in enumerate(chips)]
        for cp in cps:
            cp.start()
        for j, (px, py) in enumerate(chips):
            ps = 2 * px + py
            _rcopy(x_ref.at[ps], o_ref.at[ps], ssem.at[j], rsem.at[j], (px, py, mc)).wait_recv()
        for cp in cps:
            cp.wait_send()

    return pl.pallas_call(
        body, name=name, out_shape=_sds(q.shape, q.dtype), in_specs=[ANY], out_specs=ANY,
        scratch_shapes=[pltpu.SemaphoreType.DMA((3,)), pltpu.SemaphoreType.DMA((3,))],
    )(q)


def sibling_join_halves(both, name):
    def body(x_ref, o_ref, ssem, rsem):
        del x_ref
        mx, my, mc = _me()
        sib = (mx, my, 1 - mc)
        cp = _rcopy(o_ref.at[mc], o_ref.at[mc], ssem, rsem, sib)
        cp.start()
        _rcopy(o_ref.at[1 - mc], o_ref.at[1 - mc], ssem, rsem, sib).wait_recv()
        cp.wait_send()

    return pl.pallas_call(
        body, name=name, out_shape=_sds(both.shape, both.dtype), in_specs=[ANY], out_specs=ANY,
        input_output_aliases={0: 0}, scratch_shapes=[pltpu.SemaphoreType.DMA, pltpu.SemaphoreType.DMA],
    )(both)


def add_own_half(buf, recv, wire, name):
    n, _, h, w = buf.shape
    th = _pick(h, _ROW_BLOCKS)

    def body(c_ref, a_ref, b_ref, o_ref):
        del c_ref
        o_ref[...] = (a_ref[...] + b_ref[...]).astype(o_ref.dtype)

    return pl.pallas_call(
        body, name=name, out_shape=_sds((n, h, w), wire),
        grid_spec=pltpu.PrefetchScalarGridSpec(
            num_scalar_prefetch=1, grid=(n, h // th),
            in_specs=[pl.BlockSpec((None, None, th, w), lambda k, r, c: (k, c[0], r, 0)),
                      pl.BlockSpec((None, th, w), lambda k, r, c: (k, r, 0))],
            out_specs=pl.BlockSpec((None, th, w), lambda k, r, c: (k, r, 0))),
    )(_idx(lax.axis_index("c")), buf, recv)


def sum_slots(q, r, name):
    n, h, w = r.shape
    th = _pick(h, _ROW_BLOCKS)

    def body(i_ref, q_ref, r1, r2, r3, o_ref):
        del i_ref
        o_ref[...] = ((q_ref[...].astype(F32) + r1[...].astype(F32)) + r2[...].astype(F32)) + r3[...].astype(F32)

    def slot(d):
        return lambda i, ix: ((ix[0] + d) % N_CHIP, i, 0)

    return pl.pallas_call(
        body, name=name, out_shape=_sds((2, h, w), F32),
        grid_spec=pltpu.PrefetchScalarGridSpec(
            num_scalar_prefetch=1, grid=(h // th,),
            in_specs=[pl.BlockSpec((None, th, w), slot(d)) for d in (0, 1, 2, 3)],
            out_specs=pl.BlockSpec((None, th, w), lambda i, ix: (ix[1], i, 0))),
    )(jnp.stack([2 * lax.axis_index("x") + lax.axis_index("y"), lax.axis_index("c")]).astype(jnp.int32), q, r, r, r)


def reduce_scatter_chips(buf, tag, wire=F32):
    recv = sibling_send_other_half(buf, f"rs_sib_{tag}")
    q = add_own_half(buf, recv, wire, f"rs_add2_{tag}")
    r = chips_all_to_all(q, f"rs_a2a_{tag}")
    red = sum_slots(q, r, f"rs_add4_{tag}")
    return sibling_join_halves(red, f"rs_join_{tag}")


class Layout:
    def __init__(self, n_ctx, n_lat):
        self.C, self.L = n_ctx, n_lat
        self.PS = n_ctx + n_lat
        self.T = 2 * self.PS
        self.tr = _pick(math.gcd(n_ctx, n_lat), (256, 128))
        self.bps = self.PS // self.tr
        self.cb = n_ctx // self.tr
        self.nblk = self.T // self.tr
        self.tm = _pick(self.T, (512, 256, 128))

    def seg(self, i):
        return jnp.where(i % self.bps < self.cb, 2, i // self.bps)


def rowwise(lay, name, fn, rows, segs=(), vecs=(), outs=(), sums=()):
    tr, nblk = lay.tr, lay.nblk
    n_r, n_s, n_v, n_o = len(rows), len(segs), len(vecs), len(outs)

    def body(*refs):
        ins = refs[:n_r + n_s + n_v]
        ors = refs[n_r + n_s + n_v:]
        vals = [r[...] for r in ins[:n_r]] + [r[0] for r in ins[n_r:n_r + n_s]] + [r[...] for r in ins[n_r + n_s:]]
        res = fn(*vals)
        for k in range(n_o):
            ors[k][...] = res[k].astype(ors[k].dtype)
        for k in range(len(sums)):
            ors[n_o + k][0] = res[n_o + k]

    def all_rows(i):
        return (i, 0)

    def lat_rows(i):
        return ((i // lay.bps) * (lay.bps - lay.cb) + jnp.maximum(i % lay.bps - lay.cb, 0), 0)

    in_specs = [pl.BlockSpec((tr, a.shape[1]), all_rows if a.shape[0] == lay.T else lat_rows) for a in rows]
    in_specs += [pl.BlockSpec((1,) + a.shape[1:], lambda i: (lay.seg(i), 0, 0)) for a in segs]
    in_specs += [pl.BlockSpec(a.shape, lambda i: (0, 0)) for a in vecs]
    out_shape = [_sds((2 * lay.L if o[2:] else lay.T, o[0]), o[1]) for o in outs]
    out_shape += [_sds((nblk, r, w), F32) for r, w in sums]
    out_specs = [pl.BlockSpec((tr, o[0]), lat_rows if o[2:] else all_rows) for o in outs]
    out_specs += [pl.BlockSpec((1, r, w), lambda i: (i, 0, 0)) for r, w in sums]
    sem = "arbitrary" if any(o[2:] for o in outs) else "parallel"
    return pl.pallas_call(body, name=name, out_shape=out_shape, grid=(nblk,), in_specs=in_specs,
                          out_specs=out_specs, compiler_params=_cp((sem,)))(*rows, *segs, *vecs)


def modulate(lay, h, mod, k_shift, k_scale, name):
    def fn(hb, m):
        return (hb * (1.0 + m[k_scale:k_scale + 1]) + m[k_shift:k_shift + 1],)
    return rowwise(lay, name, fn, [h], segs=[mod], outs=[(D, BF16)])[0]


def resid_ln(lay, h, y, mod, k_gate, coef, lnv, name, nxt=None):
    def fn(hb, yb, m, *rest):
        ln = rest[-1]
        z = ALPHA * hb + (coef * m[k_gate:k_gate + 1]) * yb
        mu = jnp.mean(z, axis=-1, keepdims=True)
        zc = z - mu
        var = jnp.mean(zc * zc, axis=-1, keepdims=True)
        rstd = lax.rsqrt(var + LN_EPS)
        xhat = zc * rstd
        out = xhat * ln[0:1] + ln[1:2]
        if nxt is None:
            return out, xhat, rstd
        mn = rest[0]
        return out, xhat, rstd, out * (1.0 + mn[nxt[2]:nxt[2] + 1]) + mn[nxt[1]:nxt[1] + 1]
    segs = [mod] if nxt is None else [mod, nxt[0]]
    outs = [(D, F32), (D, F32), (1, F32)] + ([] if nxt is None else [(D, BF16)])
    return rowwise(lay, name, fn, [h, y], segs=segs, vecs=[lnv], outs=outs)


def _ln_bwd_math(do, xh, rs, yb, gate, coef, ln):
    dxh = do * ln[0:1]
    m1 = jnp.mean(dxh, axis=-1, keepdims=True)
    m2 = jnp.mean(dxh * xh, axis=-1, keepdims=True)
    dz = rs * (dxh - m1 - xh * m2)
    s = jnp.concatenate([jnp.sum(do, axis=0, keepdims=True), jnp.sum(do * xh, axis=0, keepdims=True),
                         jnp.sum(coef * dz * yb, axis=0, keepdims=True)], axis=0)
    return (coef * gate) * dz, ALPHA * dz, s


def _mod_bwd_math(dr, dm, hb, scale):
    s = jnp.concatenate([jnp.sum(dm, axis=0, keepdims=True), jnp.sum(dm * hb, axis=0, keepdims=True)], axis=0)
    return dr + dm * (1.0 + scale), s


def ln_bwd(lay, dout, xhat, rstd, y, mod, k_gate, coef, lnv, name):
    def fn(do, xh, rs, yb, m, ln):
        return _ln_bwd_math(do, xh, rs, yb, m[k_gate:k_gate + 1], coef, ln)
    return rowwise(lay, name, fn, [dout, xhat, rstd, y], segs=[mod], vecs=[lnv],
                   outs=[(D, BF16), (D, F32)], sums=[(3, D)])


def mod_bwd(lay, dres, dhm, h, mod, k_scale, name):
    def fn(dr, dm, hb, m):
        return _mod_bwd_math(dr, dm, hb, m[k_scale:k_scale + 1])
    return rowwise(lay, name, fn, [dres, dhm, h], segs=[mod], outs=[(D, F32, "lat")], sums=[(2, D)])


def modb_lnb(lay, dres, dhm, h, mod, k_scale, xhat, rstd, y, mod_p, k_gate, coef, lnv, name):
    def fn(dr, dm, hb, xh, rs, yb, m, mp, ln):
        dh, s2 = _mod_bwd_math(dr, dm, hb, m[k_scale:k_scale + 1])
        dy, dres_p, s1 = _ln_bwd_math(dh, xh, rs, yb, mp[k_gate:k_gate + 1], coef, ln)
        return dy, dres_p, s1, s2
    return rowwise(lay, name, fn, [dres, dhm, h, xhat, rstd, y], segs=[mod, mod_p], vecs=[lnv],
                   outs=[(D, BF16), (D, F32)], sums=[(3, D), (2, D)])


def block_sums(lay, parts, name):
    nblk, r, w = parts.shape

    def body(p_ref, o_ref):
        acc = [None, None, None]
        for i in range(nblk):
            sg = 2 if i % lay.bps < lay.cb else i // lay.bps
            acc[sg] = p_ref[i] if acc[sg] is None else acc[sg] + p_ref[i]
        for k in range(3):
            o_ref[k] = acc[k]
        o_ref[3] = (acc[0] + acc[1]) + acc[2]

    return pl.pallas_call(body, name=name, out_shape=_sds((4, r, w), F32), in_specs=[VMEM_SPEC],
                          out_specs=VMEM_SPEC)(parts)


def mm_nn(a, b, name, out_dtype=F32, bias=None):
    m, k = a.shape
    n = b.shape[1]
    tm = _pick(m, (512, 256, 128, 64, 32, 16, 8))
    tn = _pick(n, (1024, 768, 640, 512, 384, 256, 128))

    def body(*refs):
        if bias is None:
            a_ref, b_ref, o_ref = refs
            o_ref[...] = _nn(a_ref[...].astype(BF16), b_ref[...].astype(BF16)).astype(o_ref.dtype)
        else:
            a_ref, b_ref, c_ref, o_ref = refs
            o_ref[...] = (_nn(a_ref[...].astype(BF16), b_ref[...].astype(BF16)) + c_ref[...]).astype(o_ref.dtype)

    in_specs = [pl.BlockSpec((tm, k), lambda i, j: (i, 0)), pl.BlockSpec((k, tn), lambda i, j: (0, j))]
    ops = [a, b]
    if bias is not None:
        in_specs.append(pl.BlockSpec((1, tn), lambda i, j: (0, j)))
        ops.append(bias)
    return pl.pallas_call(body, name=name, out_shape=_sds((m, n), out_dtype), grid=(m // tm, n // tn),
                          in_specs=in_specs, out_specs=pl.BlockSpec((tm, tn), lambda i, j: (i, j)),
                          compiler_params=_cp(("parallel", "parallel"), VMEM_BIG))(*ops)


def mm_nt(a, b, name, out_dtype=F32):
    m, k = a.shape
    n = b.shape[0]
    tm = _pick(m, (512, 256, 128, 64, 32, 16, 8))
    tn = _pick(n, (1024, 768, 640, 512, 384, 256, 128))

    def body(a_ref, b_ref, o_ref):
        o_ref[...] = _nt(a_ref[...].astype(BF16), b_ref[...].astype(BF16)).astype(o_ref.dtype)

    return pl.pallas_call(body, name=name, out_shape=_sds((m, n), out_dtype), grid=(m // tm, n // tn),
                          in_specs=[pl.BlockSpec((tm, k), lambda i, j: (i, 0)), pl.BlockSpec((tn, k), lambda i, j: (j, 0))],
                          out_specs=pl.BlockSpec((tm, tn), lambda i, j: (i, j)),
                          compiler_params=_cp(("parallel", "parallel"), VMEM_BIG))(a, b)


def mm_tn(a, b, name):
    t, m = a.shape
    n = b.shape[1]
    tk = _pick(t, (512, 256, 128, 64, 32, 16))
    tm = _pick(m, (512, 384, 256, 128))

    def body(a_ref, b_ref, o_ref):
        @pl.when(pl.program_id(1) == 0)
        def _():
            o_ref[...] = jnp.zeros_like(o_ref)
        o_ref[...] += _tn(a_ref[...].astype(BF16), b_ref[...].astype(BF16))

    return pl.pallas_call(body, name=name, out_shape=_sds((m, n), F32), grid=(m // tm, t // tk),
                          in_specs=[pl.BlockSpec((tk, tm), lambda i, k: (k, i)), pl.BlockSpec((tk, n), lambda i, k: (k, 0))],
                          out_specs=pl.BlockSpec((tm, n), lambda i, k: (i, 0)),
                          compiler_params=_cp(("parallel", "arbitrary"), VMEM_BIG))(a, b)


def ffn_up(lay, hm, wbuf, ig, iu, ns, name):
    tm = lay.tm

    def body(h_ref, wg_ref, wu_ref, g_ref, u_ref, a_ref):
        hb = h_ref[...]
        g = _nt(hb, wg_ref[0])
        u = _nt(hb, wu_ref[0])
        g_ref[0] = g.astype(BF16)
        u_ref[0] = u.astype(BF16)
        a_ref[0] = (g * _sigmoid(g) * u).astype(BF16)

    spec_o = pl.BlockSpec((1, tm, ns), lambda s, i: (s, i, 0))
    return pl.pallas_call(
        body, name=name, out_shape=[_sds((N_CHIP, lay.T, ns), BF16)] * 3, grid=(N_CHIP, lay.T // tm),
        in_specs=[pl.BlockSpec((tm, D), lambda s, i: (i, 0)),
                  pl.BlockSpec((1, ns, D), lambda s, i: (s, ig, 0)),
                  pl.BlockSpec((1, ns, D), lambda s, i: (s, iu, 0))],
        out_specs=[spec_o] * 3, compiler_params=_cp(("parallel", "parallel"), VMEM_BIG))(hm, wbuf, wbuf)


def slab_nn_acc(lay, zs, wbuf, idxs, ns, name):
    tm = lay.tm
    npair = len(zs)

    def body(*refs):
        o_ref = refs[-1]

        @pl.when(pl.program_id(1) == 0)
        def _():
            o_ref[...] = jnp.zeros_like(o_ref)
        acc = _nn(refs[0][0], refs[npair][0])
        for p in range(1, npair):
            acc += _nn(refs[p][0], refs[npair + p][0])
        o_ref[...] += acc

    in_specs = [pl.BlockSpec((1, tm, ns), lambda i, s: (s, i, 0)) for _ in zs]
    in_specs += [pl.BlockSpec((1, ns, D), functools.partial(lambda i, s, q: (s, q, 0), q=q)) for q in idxs]
    return pl.pallas_call(
        body, name=name, out_shape=_sds((lay.T, D), F32), grid=(lay.T // tm, N_CHIP), in_specs=in_specs,
        out_specs=pl.BlockSpec((tm, D), lambda i, s: (i, 0)),
        compiler_params=_cp(("parallel", "arbitrary"), VMEM_BIG))(*zs, *([wbuf] * npair))


def ffn_bwd_da(lay, dy, wbuf, idn, g, u, ns, name):
    tm = lay.tm

    def body(dy_ref, wd_ref, g_ref, u_ref, dg_ref, du_ref):
        da = _nt(dy_ref[...], wd_ref[0])
        gv = g_ref[0].astype(F32)
        uv = u_ref[0].astype(F32)
        sg = _sigmoid(gv)
        dg_ref[0] = (da * uv * (sg * (1.0 + gv * (1.0 - sg)))).astype(BF16)
        du_ref[0] = (da * (gv * sg)).astype(BF16)

    spec_z = pl.BlockSpec((1, tm, ns), lambda s, i: (s, i, 0))
    return pl.pallas_call(
        body, name=name, out_shape=[_sds((N_CHIP, lay.T, ns), BF16)] * 2, grid=(N_CHIP, lay.T // tm),
        in_specs=[pl.BlockSpec((tm, D), lambda s, i: (i, 0)), pl.BlockSpec((1, ns, D), lambda s, i: (s, idn, 0)),
                  spec_z, spec_z],
        out_specs=[spec_z] * 2, compiler_params=_cp(("parallel", "parallel"), VMEM_BIG))(dy, wbuf, g, u)


def slab_tn(lay, z, x, gbuf, idx, ns, name):
    tk = lay.tm

    def body(z_ref, x_ref, g_in, o_ref):
        del g_in

        @pl.when(pl.program_id(1) == 0)
        def _():
            o_ref[...] = jnp.zeros_like(o_ref)
        o_ref[0] += _tn(z_ref[0], x_ref[...])

    return pl.pallas_call(
        body, name=name, out_shape=_sds(gbuf.shape, F32), grid=(N_CHIP, lay.T // tk),
        in_specs=[pl.BlockSpec((1, tk, ns), lambda s, k: (s, k, 0)), pl.BlockSpec((tk, D), lambda s, k: (k, 0)), ANY],
        out_specs=pl.BlockSpec((1, ns, D), lambda s, k: (s, idx, 0)),
        input_output_aliases={2: 0}, compiler_params=_cp(("parallel", "arbitrary"), VMEM_BIG))(z, x, gbuf)


Q0, K0, V0, U0, QR0, KR0, PEXT = 0, 512, 640, 768, 1280, 1792, 1920


def rope_fwd(lay, p, cos, sin, name):
    def fn(pb, cs, sn):
        cs4 = jnp.concatenate([cs] * 4, axis=1)
        sn4 = jnp.concatenate([sn] * 4, axis=1)
        qr = pb[:, Q0:K0] * cs4 + pb[:, QR0:KR0] * sn4
        kr = pb[:, K0:V0] * cs + pb[:, KR0:PEXT] * sn
        return qr, kr, pb[:, V0:U0], pb[:, U0:QR0]
    return rowwise(lay, name, fn, [p, cos, sin], outs=[(ATT_W, BF16), (KV_W, BF16), (KV_W, BF16), (POOL_W, F32)])


def rope_bwd(lay, dqr, dkr, dv, du, cos, sin, name):
    def fn(dq, dk, dvb, dub, cs, sn):
        cs4 = jnp.concatenate([cs] * 4, axis=1)
        sn4 = jnp.concatenate([sn] * 4, axis=1)
        return (jnp.concatenate([dq * cs4, dk * cs, dvb, dub, dq * sn4, dk * sn], axis=1),)
    return rowwise(lay, name, fn, [dqr, dkr, dv, du, cos, sin], outs=[(PEXT, BF16)])[0]


def _attn_specs(lay):
    nbs, cbk, lbk = lay.PS // BLK, lay.C // BLK, lay.L // BLK

    def kv_map(j):
        return lambda s, n: (s * nbs + cbk + jnp.clip(n - cbk + j - 1, 0, lbk - 1), 0)

    win = [pl.BlockSpec((BLK, KV_W), kv_map(j)) for j in range(3)]
    ctx = pl.BlockSpec((lay.C, KV_W), lambda s, n: (s * (lay.PS // lay.C), 0))
    return nbs, cbk, lbk, win, ctx


def _attn_masks(n, cbk, lbk):
    row = lax.broadcasted_iota(jnp.int32, (BLK, BLK), 0)
    col = lax.broadcasted_iota(jnp.int32, (BLK, BLK), 1)
    m = n - cbk
    lat = n >= cbk
    valid = [lat & (m >= 1) & (col >= row), lat & (col >= 0), lat & (m <= lbk - 2) & (col <= row)]
    lane_lo = lax.broadcasted_iota(jnp.int32, (BLK, 2 * HEAD_DIM), 1) < HEAD_DIM
    return valid, lane_lo


def attn_fwd(lay, qr, kr, vb, sink_tab, name):
    nbs, cbk, lbk, win, ctx = _attn_specs(lay)

    def body(q_ref, k0, k1, k2, kc_ref, v0, v1, v2, vc_ref, sk_ref, o_ref, l_ref):
        n = pl.program_id(1)
        valid, lane_lo = _attn_masks(n, cbk, lbk)
        ks = [k0[...], k1[...], k2[...]]
        vs = [v0[...], v1[...], v2[...]]
        kc, vc = kc_ref[...], vc_ref[...]
        for p in range(4):
            q2 = q_ref[:, p * 128:(p + 1) * 128]
            outs, lses = [], []
            for hh in range(2):
                qm = jnp.where(lane_lo == (hh == 0), q2, jnp.zeros_like(q2))
                sk = sk_ref[p:p + 1, hh * HEAD_DIM:hh * HEAD_DIM + 1]
                sw = [jnp.where(valid[j], _nt(qm, ks[j]) * ATT_SCALE, NEG_INF) for j in range(3)]
                sc = _nt(qm, kc) * ATT_SCALE
                mx = jnp.maximum(jnp.maximum(jnp.maximum(sw[0].max(-1, keepdims=True), sw[1].max(-1, keepdims=True)),
                                             jnp.maximum(sw[2].max(-1, keepdims=True), sc.max(-1, keepdims=True))), sk)
                ew = [jnp.exp(s - mx) for s in sw]
                ec = jnp.exp(sc - mx)
                den = ew[0].sum(-1, keepdims=True) + ew[1].sum(-1, keepdims=True) + ew[2].sum(-1, keepdims=True)
                den = den + ec.sum(-1, keepdims=True) + jnp.exp(sk - mx)
                o = _nn((ec / den).astype(BF16), vc)
                for j in range(3):
                    o += _nn((ew[j] / den).astype(BF16), vs[j])
                outs.append(o)
                lses.append(jnp.broadcast_to(mx + jnp.log(den), (BLK, 128)))
            o_ref[:, p * 128:(p + 1) * 128] = jnp.where(lane_lo, outs[0], outs[1]).astype(o_ref.dtype)
            l_ref[:, p * 128:(p + 1) * 128] = jnp.where(lane_lo, lses[0], lses[1])

    qspec = pl.BlockSpec((BLK, ATT_W), lambda s, n: (s * nbs + n, 0))
    return pl.pallas_call(
        body, name=name, out_shape=[_sds((lay.T, ATT_W), BF16), _sds((lay.T, ATT_W), F32)], grid=(2, nbs),
        in_specs=[qspec] + win + [ctx] + win + [ctx] + [pl.BlockSpec((8, 128), lambda s, n: (0, 0))],
        out_specs=[qspec, qspec], compiler_params=_cp(("parallel", "parallel")))(qr, kr, kr, kr, kr, vb, vb, vb, vb, sink_tab)


def attn_bwd(lay, qr, kr, vb, sink_tab, lse, datt, name):
    nbs, cbk, lbk, win, ctx = _attn_specs(lay)
    C, PS = lay.C, lay.PS

    def body(q_ref, k0, k1, k2, kc_ref, v0, v1, v2, vc_ref, sk_ref, l_ref, do_ref, dq_ref, dk_ref, dv_ref, ds_ref):
        n = pl.program_id(1)
        valid, lane_lo = _attn_masks(n, cbk, lbk)

        @pl.when(n == 0)
        def _():
            dk_ref[...] = jnp.zeros_like(dk_ref)
            dv_ref[...] = jnp.zeros_like(dv_ref)
            ds_ref[...] = jnp.zeros_like(ds_ref)

        ks = [k0[...], k1[...], k2[...], kc_ref[...]]
        vs = [v0[...], v1[...], v2[...], vc_ref[...]]
        dks = [jnp.zeros((BLK, KV_W), F32)] * 3 + [jnp.zeros((C, KV_W), F32)]
        dvs = list(dks)
        for p in range(4):
            sl = slice(p * 128, (p + 1) * 128)
            q2 = q_ref[:, sl]
            do2 = do_ref[:, sl].astype(BF16)
            lse2 = l_ref[:, sl]
            dq_h, dd_h = [], []
            for hh in range(2):
                sel = lane_lo == (hh == 0)
                qm = jnp.where(sel, q2, jnp.zeros_like(q2))
                dom = jnp.where(sel, do2, jnp.zeros_like(do2))
                lse_h = lse2[:, hh * HEAD_DIM:hh * HEAD_DIM + 1]
                ps, dps = [], []
                for j in range(4):
                    s = _nt(qm, ks[j]) * ATT_SCALE
                    if j < 3:
                        s = jnp.where(valid[j], s, NEG_INF)
                    ps.append(jnp.exp(s - lse_h))
                    dps.append(_nt(dom, vs[j]))
                dd = (ps[0] * dps[0]).sum(-1, keepdims=True) + (ps[1] * dps[1]).sum(-1, keepdims=True)
                dd = dd + (ps[2] * dps[2]).sum(-1, keepdims=True) + (ps[3] * dps[3]).sum(-1, keepdims=True)
                dq = jnp.zeros((BLK, 128), F32)
                for j in range(4):
                    dsb = (ps[j] * (dps[j] - dd) * ATT_SCALE).astype(BF16)
                    dq += _nn(dsb, ks[j])
                    dks[j] = dks[j] + _tn(dsb, qm)
                    dvs[j] = dvs[j] + _tn(ps[j].astype(BF16), dom)
                dq_h.append(dq)
                dd_h.append(jnp.broadcast_to(dd, (BLK, 128)))
            dq_ref[:, sl] = jnp.where(lane_lo, dq_h[0], dq_h[1])
            dd2 = jnp.where(lane_lo, dd_h[0], dd_h[1])
            psink = jnp.exp(sk_ref[p:p + 1, :] - lse2)
            ds_ref[0, p:p + 1, :] += -jnp.sum(psink * dd2, axis=0, keepdims=True)
        dk_ref[0:C, :] += dks[3]
        dv_ref[0:C, :] += dvs[3]
        for j in range(3):
            r0 = pl.multiple_of((cbk + jnp.clip(n - cbk + j - 1, 0, lbk - 1)) * BLK, BLK)
            dk_ref[pl.ds(r0, BLK), :] += dks[j]
            dv_ref[pl.ds(r0, BLK), :] += dvs[j]

    qspec = pl.BlockSpec((BLK, ATT_W), lambda s, n: (s * nbs + n, 0))
    kvout = pl.BlockSpec((PS, KV_W), lambda s, n: (s, 0))
    return pl.pallas_call(
        body, name=name,
        out_shape=[_sds((lay.T, ATT_W), F32), _sds((lay.T, KV_W), F32), _sds((lay.T, KV_W), F32), _sds((2, 8, 128), F32)],
        grid=(2, nbs),
        in_specs=[qspec] + win + [ctx] + win + [ctx] + [pl.BlockSpec((8, 128), lambda s, n: (0, 0)), qspec, qspec],
        out_specs=[qspec, kvout, kvout, pl.BlockSpec((1, 8, 128), lambda s, n: (s, 0, 0))],
        compiler_params=_cp(("parallel", "arbitrary")))(qr, kr, kr, kr, kr, vb, vb, vb, vb, sink_tab, lse, datt)


def _winsum(x, r):
    n = x.shape[0]
    t = lax.broadcasted_iota(jnp.int32, x.shape, 0)
    acc = x
    for o in range(1, r + 1):
        acc = acc + jnp.where(t >= o, pltpu.roll(x, o, 0), 0.0) + jnp.where(t < n - o, pltpu.roll(x, n - o, 0), 0.0)
    return acc


def _wincount(n, r):
    t = lax.broadcasted_iota(jnp.int32, (n, 128), 0)
    return (jnp.minimum(t + r, n - 1) - jnp.maximum(t - r, 0) + 1).astype(F32)


def pool_fwd(lay, u, w_pool, scale, name):
    segs = [(0, lay.C), (lay.C, lay.L)]

    def body(u_ref, w_ref, s_ref, o_ref):
        for r0, n in segs:
            for g, wd in enumerate(POOL_WINDOWS):
                sl = slice(g * 128, (g + 1) * 128)
                x = u_ref[r0:r0 + n, sl]
                d = _winsum(x, wd // 2) / _wincount(n, wd // 2) - x
                y = _nn(d.astype(BF16), w_ref[g].astype(BF16)) * s_ref[:, sl]
                o_ref[r0:r0 + n, sl] = y.astype(o_ref.dtype)

    spec = pl.BlockSpec((lay.PS, POOL_W), lambda s: (s, 0))
    return pl.pallas_call(
        body, name=name, out_shape=_sds((lay.T, POOL_W), BF16), grid=(2,),
        in_specs=[spec, pl.BlockSpec(w_pool.shape, lambda s: (0, 0, 0)), pl.BlockSpec((1, POOL_W), lambda s: (0, 0))],
        out_specs=spec, compiler_params=_cp(("parallel",), VMEM_BIG))(u, w_pool, scale)


def pool_bwd(lay, u, dcat, w_pool, scale, name):
    segs = [(0, lay.C), (lay.C, lay.L)]

    def body(u_ref, dp_ref, w_ref, s_ref, du_ref, dw_ref, dsc_ref):
        for g, wd in enumerate(POOL_WINDOWS):
            sl = slice(g * 128, (g + 1) * 128)
            wb = w_ref[g].astype(BF16)
            dw = jnp.zeros((128, 128), F32)
            dsc = jnp.zeros((1, 128), F32)
            for r0, n in segs:
                x = u_ref[r0:r0 + n, sl]
                cnt = _wincount(n, wd // 2)
                d = (_winsum(x, wd // 2) / cnt - x).astype(BF16)
                dp = dp_ref[r0:r0 + n, sl]
                dsc += jnp.sum(_nn(d, wb) * dp, axis=0, keepdims=True)
                dyp = (dp * s_ref[:, sl]).astype(BF16)
                dw += _tn(d, dyp)
                dd = _nt(dyp, wb)
                du_ref[r0:r0 + n, sl] = _winsum(dd / cnt, wd // 2) - dd
            dw_ref[0, g] = dw
            dsc_ref[0, :, sl] = dsc

    spec = pl.BlockSpec((lay.PS, POOL_W), lambda s: (s, 0))
    return pl.pallas_call(
        body, name=name,
        out_shape=[_sds((lay.T, POOL_W), F32), _sds((2, 4, 128, 128), F32), _sds((2, 1, POOL_W), F32)], grid=(2,),
        in_specs=[spec, pl.BlockSpec((lay.PS, POOL_W), lambda s: (s, 1)), pl.BlockSpec(w_pool.shape, lambda s: (0, 0, 0)),
                  pl.BlockSpec((1, POOL_W), lambda s: (0, 0))],
        out_specs=[spec, pl.BlockSpec((1, 4, 128, 128), lambda s: (s, 0, 0, 0)), pl.BlockSpec((1, 1, POOL_W), lambda s: (s, 0, 0))],
        compiler_params=_cp(("parallel",), VMEM_BIG))(u, dcat, w_pool, scale)


CONV_OFFS = (-1, 0, 1, 2)
CW = 256


def _shift_rows(x, o):
    if o == 0:
        return x
    n = x.shape[0]
    t = lax.broadcasted_iota(jnp.int32, x.shape, 0)
    if o < 0:
        return jnp.where(t >= -o, pltpu.roll(x, -o, 0), 0.0)
    return jnp.where(t < n - o, pltpu.roll(x, n - o, 0), 0.0)


def conv_fwd(lay, p, col0, w, b, name):
    segs = [(0, lay.C), (lay.C, lay.L)]
    cb0 = col0 // CW

    def body(x_ref, w_ref, b_ref, o_ref):
        for r0, n in segs:
            x = x_ref[r0:r0 + n, :]
            y = jnp.broadcast_to(b_ref[...], x.shape)
            for k, o in enumerate(CONV_OFFS):
                y = y + _shift_rows(x, o) * w_ref[k:k + 1, :]
            o_ref[r0:r0 + n, :] = y

    return pl.pallas_call(
        body, name=name, out_shape=_sds((lay.T, D), F32), grid=(2, D // CW),
        in_specs=[pl.BlockSpec((lay.PS, CW), lambda s, j: (s, cb0 + j)), pl.BlockSpec((4, CW), lambda s, j: (0, j)),
                  pl.BlockSpec((1, CW), lambda s, j: (0, j))],
        out_specs=pl.BlockSpec((lay.PS, CW), lambda s, j: (s, j)),
        compiler_params=_cp(("parallel", "parallel")))(p, w, b)


def conv_bwd(lay, p, col0, w, duc, name):
    segs = [(0, lay.C), (lay.C, lay.L)]
    cb0 = col0 // CW

    def body(x_ref, w_ref, g_ref, du_ref, dw_ref, db_ref):
        dws = [jnp.zeros((1, CW), F32)] * 4
        db = jnp.zeros((1, CW), F32)
        for r0, n in segs:
            x = x_ref[r0:r0 + n, :]
            g = g_ref[r0:r0 + n, :]
            du = jnp.zeros_like(g)
            for k, o in enumerate(CONV_OFFS):
                du = du + _shift_rows(g, -o) * w_ref[k:k + 1, :]
                dws[k] = dws[k] + jnp.sum(g * _shift_rows(x, o), axis=0, keepdims=True)
            db = db + jnp.sum(g, axis=0, keepdims=True)
            du_ref[r0:r0 + n, :] = du
        dw_ref[0] = jnp.concatenate(dws, axis=0)
        db_ref[0] = db

    return pl.pallas_call(
        body, name=name, out_shape=[_sds((lay.T, D), F32), _sds((2, 4, D), F32), _sds((2, 1, D), F32)], grid=(2, D // CW),
        in_specs=[pl.BlockSpec((lay.PS, CW), lambda s, j: (s, cb0 + j)), pl.BlockSpec((4, CW), lambda s, j: (0, j)),
                  pl.BlockSpec((lay.PS, CW), lambda s, j: (s, j))],
        out_specs=[pl.BlockSpec((lay.PS, CW), lambda s, j: (s, j)), pl.BlockSpec((1, 4, CW), lambda s, j: (s, 0, j)),
                   pl.BlockSpec((1, 1, CW), lambda s, j: (s, 0, j))],
        compiler_params=_cp(("parallel", "parallel")))(p, w, duc)


def _softplus_neg(lam):
    z = -lam
    w = jnp.exp(-jnp.abs(z))
    log1p = jnp.where(w < 1e-2, w * (1.0 - w * (0.5 - w / 3.0)), jnp.log(1.0 + w))
    return jnp.maximum(z, 0.0) + log1p, -_sigmoid(z)


def _neg_expm1(x):
    series = -x * (1.0 + x * (0.5 + x * (1.0 / 6.0 + x * (1.0 / 24.0 + x * (1.0 / 120.0)))))
    return jnp.where(x > -0.05, series, 1.0 - jnp.exp(x))


def _lru_gates(x, xb, wa, wx, ba, bx, lam):
    r = _sigmoid(_nn(xb, wa.astype(BF16)) + ba)
    gi = _sigmoid(_nn(xb, wx.astype(BF16)) + bx)
    sp, dsp = _softplus_neg(lam)
    la = -LRU_C * r * sp
    a = jnp.exp(la)
    sq = jnp.sqrt(_neg_expm1(2.0 * la))
    return r, gi, sp, dsp, a, sq


def lru_coeffs(lay, uc, wa, wx, vec, name):
    tr = lay.tr

    def body(x_ref, wa_ref, wx_ref, v_ref, a_ref, b_ref):
        for h in range(8):
            sl = slice(h * 128, (h + 1) * 128)
            x = x_ref[:, sl]
            xb = x.astype(BF16)
            for d in range(2):
                _, gi, _, _, a, sq = _lru_gates(x, xb, wa_ref[d, h], wx_ref[d, h], v_ref[d:d + 1, sl],
                                                v_ref[2 + d:3 + d, sl], v_ref[4 + d:5 + d, sl])
                a_ref[d, h] = a
                b_ref[d, h] = sq * (gi * x)

    wspec = pl.BlockSpec((2, 8, 128, 128), lambda i: (0, 0, 0, 0))
    ospec = pl.BlockSpec((2, 8, tr, 128), lambda i: (0, 0, i, 0))
    return pl.pallas_call(
        body, name=name, out_shape=[_sds((2, 8, lay.T, 128), F32)] * 2, grid=(lay.nblk,),
        in_specs=[pl.BlockSpec((tr, D), lambda i: (i, 0)), wspec, wspec, pl.BlockSpec((6, D), lambda i: (0, 0))],
        out_specs=[ospec, ospec], compiler_params=_cp(("parallel",)))(uc, wa, wx, vec)


def lru_coeffs_bwd(lay, uc, wa, wx, vec, da, db, name):
    tr = lay.tr

    def body(x_ref, wa_ref, wx_ref, v_ref, da_ref, db_ref, dx_ref, dwa_ref, dwx_ref, dv_ref):
        @pl.when(pl.program_id(0) == 0)
        def _():
            dwa_ref[...] = jnp.zeros_like(dwa_ref)
            dwx_ref[...] = jnp.zeros_like(dwx_ref)
            dv_ref[...] = jnp.zeros_like(dv_ref)

        for h in range(8):
            sl = slice(h * 128, (h + 1) * 128)
            x = x_ref[:, sl]
            xb = x.astype(BF16)
            dx = jnp.zeros_like(x)
            for d in range(2):
                wab, wxb = wa_ref[d, h].astype(BF16), wx_ref[d, h].astype(BF16)
                r, gi, sp, dsp, a, sq = _lru_gates(x, xb, wa_ref[d, h], wx_ref[d, h], v_ref[d:d + 1, sl],
                                                   v_ref[2 + d:3 + d, sl], v_ref[4 + d:5 + d, sl])
                dbv, dav = db_ref[d, h], da_ref[d, h]
                t1 = dbv * sq
                dgi = t1 * x
                dx = dx + t1 * gi
                dla = dav * a - (dbv * gi * x) * (a * a) / sq
                dr = dla * (-LRU_C * sp)
                dlam = jnp.sum(dla * (-LRU_C * r), axis=0, keepdims=True) * dsp
                dpa = dr * r * (1.0 - r)
                dpx = dgi * gi * (1.0 - gi)
                dpab, dpxb = dpa.astype(BF16), dpx.astype(BF16)
                dwa_ref[d, h] += _tn(xb, dpab)
                dwx_ref[d, h] += _tn(xb, dpxb)
                dx = dx + _nt(dpab, wab) + _nt(dpxb, wxb)
                dv_ref[d:d + 1, sl] += jnp.sum(dpa, axis=0, keepdims=True)
                dv_ref[2 + d:3 + d, sl] += jnp.sum(dpx, axis=0, keepdims=True)
                dv_ref[4 + d:5 + d, sl] += dlam
            dx_ref[:, sl] = dx

    wspec = pl.BlockSpec((2, 8, 128, 128), lambda i: (0, 0, 0, 0))
    gspec = pl.BlockSpec((2, 8, tr, 128), lambda i: (0, 0, i, 0))
    vspec = pl.BlockSpec((6, D), lambda i: (0, 0))
    xspec = pl.BlockSpec((tr, D), lambda i: (i, 0))
    return pl.pallas_call(
        body, name=name,
        out_shape=[_sds((lay.T, D), F32), _sds((2, 8, 128, 128), F32), _sds((2, 8, 128, 128), F32), _sds((6, D), F32)],
        grid=(lay.nblk,), in_specs=[xspec, wspec, wspec, vspec, gspec, gspec],
        out_specs=[xspec, wspec, wspec, vspec], compiler_params=_cp(("arbitrary",)))(uc, wa, wx, vec, da, db)


GB = 2
SCAN_UNROLL = 4


def _tile_scan(a, b, up):
    t = lax.broadcasted_iota(jnp.int32, a.shape, 0)
    for d in (1, 2, 4):
        sh = 8 - d if up else d
        m = (t < 8 - d) if up else (t >= d)
        a_prev, b_prev = pltpu.roll(a, sh, 0), pltpu.roll(b, sh, 0)
        b = jnp.where(m, a * b_prev + b, b)
        a = jnp.where(m, a * a_prev, a)
    return a, b


def lru_scan(lay, a, b, name):
    segs = [(0, lay.C), (lay.C, lay.L)]

    def body(a_ref, b_ref, s_ref):
        for d in range(2):
            rev = d == 1
            state = tuple(jnp.zeros((1, 128), F32) for _ in range(GB))
            for base, n in segs:
                nt = n // 8

                def step(j, c, base=base, nt=nt, rev=rev, d=d):
                    c = list(c)
                    for u in range(SCAN_UNROLL):
                        jj = j * SCAN_UNROLL + u
                        r0 = pl.multiple_of(base + 8 * ((nt - 1 - jj) if rev else jj), 8)
                        for g in range(GB):
                            at, bt = _tile_scan(a_ref[d, g, pl.ds(r0, 8), :], b_ref[d, g, pl.ds(r0, 8), :], rev)
                            h = at * c[g] + bt
                            s_ref[d, g, pl.ds(r0, 8), :] = h
                            c[g] = h[0:1] if rev else h[7:8]
                    return tuple(c)

                state = lax.fori_loop(0, nt // SCAN_UNROLL, step, state)

    spec = pl.BlockSpec((2, GB, lay.PS, 128), lambda s, hb: (0, hb, s, 0))
    return pl.pallas_call(
        body, name=name, out_shape=_sds((2, 8, lay.T, 128), F32), grid=(2, 8 // GB),
        in_specs=[spec, spec], out_specs=spec, compiler_params=_cp(("parallel", "parallel"), VMEM_BIG))(a, b)


def lru_scan_bwd(lay, a, s, dy, name):
    segs = [(0, lay.C), (lay.C, lay.L)]
    C, PS = lay.C, lay.PS

    def body(a_ref, s_ref, g_ref, da_ref, db_ref):
        t = lax.broadcasted_iota(jnp.int32, (8, 128), 0)
        for d in range(2):
            rev = d == 1
            carry = tuple(jnp.zeros((1, 128), F32) for _ in range(GB))
            for si in (1, 0):
                base, n = segs[si]
                nt = n // 8

                def step(j, c, base=base, nt=nt, rev=rev, d=d):
                    c = list(c)
                    for u in range(SCAN_UNROLL):
                        jj = j * SCAN_UNROLL + u
                        r0 = pl.multiple_of(base + 8 * (jj if rev else (nt - 1 - jj)), 8)
                        if rev:
                            rn = pl.multiple_of(jnp.where(r0 == PS - 8, 0, r0 + 8), 8)
                            nb_zero = r0 == C - 8
                        else:
                            rn = pl.multiple_of(jnp.maximum(r0 - 8, 0), 8)
                            nb_zero = r0 == 0
                        for g in range(GB):
                            av = a_ref[d, g, pl.ds(r0, 8), :]
                            gv = g_ref[g, pl.ds(r0, 8), :]
                            sv = s_ref[d, g, pl.ds(r0, 8), :]
                            nbt = s_ref[d, g, pl.ds(rn, 8), :]
                            at, bt = _tile_scan(av, av * gv, not rev)
                            m = at * c[g] + bt
                            if rev:
                                m_next = jnp.where(t >= 1, pltpu.roll(m, 1, 0), c[g])
                                nb = jnp.where(nb_zero, 0.0, nbt[0:1])
                                h_prev = jnp.where(t < 7, pltpu.roll(sv, 7, 0), nb)
                                c[g] = m[7:8]
                            else:
                                m_next = jnp.where(t < 7, pltpu.roll(m, 7, 0), c[g])
                                nb = jnp.where(nb_zero, 0.0, nbt[7:8])
                                h_prev = jnp.where(t >= 1, pltpu.roll(sv, 1, 0), nb)
                                c[g] = m[0:1]
                            lam = gv + m_next
                            db_ref[d, g, pl.ds(r0, 8), :] = lam
                            da_ref[d, g, pl.ds(r0, 8), :] = lam * h_prev
                    return tuple(c)

                carry = lax.fori_loop(0, nt // SCAN_UNROLL, step, carry)

    spec = pl.BlockSpec((2, GB, lay.PS, 128), lambda s, hb: (0, hb, s, 0))
    return pl.pallas_call(
        body, name=name, out_shape=[_sds((2, 8, lay.T, 128), F32)] * 2, grid=(2, 8 // GB),
        in_specs=[spec, spec, pl.BlockSpec((GB, lay.PS, 128), lambda s, hb: (hb, s, 0))],
        out_specs=[spec, spec], compiler_params=_cp(("parallel", "parallel"), VMEM_BIG))(a, s, dy)


def _gelu(x):
    k = math.sqrt(2.0 / math.pi)
    t = jnp.tanh(k * (x + 0.044715 * x * x * x))
    return 0.5 * x * (1.0 + t), 0.5 * (1.0 + t) + 0.5 * x * (1.0 - t * t) * k * (1.0 + 3 * 0.044715 * x * x)


def lru_gate(lay, p, s, name):
    tr = lay.tr

    def body(g_ref, s_ref, o_ref):
        for h in range(8):
            sl = slice(h * 128, (h + 1) * 128)
            o_ref[:, sl] = (_gelu(g_ref[:, sl])[0] * (s_ref[0, h] + s_ref[1, h])).astype(o_ref.dtype)

    return pl.pallas_call(
        body, name=name, out_shape=_sds((lay.T, D), BF16), grid=(lay.nblk,),
        in_specs=[pl.BlockSpec((tr, D), lambda i: (i, 0)), pl.BlockSpec((2, 8, tr, 128), lambda i: (0, 0, i, 0))],
        out_specs=pl.BlockSpec((tr, D), lambda i: (i, 0)), compiler_params=_cp(("parallel",)))(p, s)


def lru_gate_bwd(lay, p, s, do, name):
    tr = lay.tr

    def body(g_ref, s_ref, do_ref, dg_ref, dy_ref):
        for h in range(8):
            sl = slice(h * 128, (h + 1) * 128)
            ge, dge = _gelu(g_ref[:, sl])
            dov = do_ref[:, sl]
            dg_ref[:, sl] = dov * (s_ref[0, h] + s_ref[1, h]) * dge
            dy_ref[h] = dov * ge

    xspec = pl.BlockSpec((tr, D), lambda i: (i, 0))
    return pl.pallas_call(
        body, name=name, out_shape=[_sds((lay.T, D), F32), _sds((8, lay.T, 128), F32)], grid=(lay.nblk,),
        in_specs=[xspec, pl.BlockSpec((2, 8, tr, 128), lambda i: (0, 0, i, 0)), xspec],
        out_specs=[xspec, pl.BlockSpec((8, tr, 128), lambda i: (0, i, 0))],
        compiler_params=_cp(("parallel",)))(p, s, do)


def silu_rows(x, name):
    def body(x_ref, o_ref):
        v = x_ref[...]
        o_ref[...] = (v * _sigmoid(v)).astype(o_ref.dtype)
    return pl.pallas_call(body, name=name, out_shape=_sds(x.shape, BF16), in_specs=[VMEM_SPEC], out_specs=VMEM_SPEC)(x)


def mod_grad_rows(gath, name):
    w = gath.shape[-1]

    def body(g_ref, dm_ref, db_ref):
        dm_ref[...] = jnp.zeros_like(dm_ref)
        for l in range(2):
            ctx = g_ref[0, 3 * l + 2:3 * l + 3, :]
            tot = g_ref[0, 3 * l:3 * l + 1, :] + g_ref[0, 3 * l + 1:3 * l + 2, :]
            for k in range(8):
                dm_ref[l, 2 * k:2 * k + 2, :] = g_ref[k, 3 * l:3 * l + 2, :]
                if k:
                    ctx = ctx + g_ref[k, 3 * l + 2:3 * l + 3, :]
                    tot = tot + (g_ref[k, 3 * l:3 * l + 1, :] + g_ref[k, 3 * l + 1:3 * l + 2, :])
            dm_ref[l, 16:17, :] = ctx
            db_ref[l:l + 1, :] = tot + ctx

    return pl.pallas_call(body, name=name, out_shape=[_sds((2, 32, w), F32), _sds((2, w), F32)],
                          in_specs=[VMEM_SPEC], out_specs=[VMEM_SPEC, VMEM_SPEC])(gath)


def cctx_grad(p, c_ctx, name):
    def body(a_ref, c_ref, o_ref):
        cv = c_ref[...]
        sg = _sigmoid(cv)
        o_ref[...] = 0.5 * (a_ref[0, 0:1, :] + a_ref[1, 0:1, :]) * (sg * (1.0 + cv * (1.0 - sg)))
    return pl.pallas_call(body, name=name, out_shape=_sds((1, D), F32), in_specs=[VMEM_SPEC] * 2,
                          out_specs=VMEM_SPEC)(p, c_ctx)


def loss_and_grad(lay, h, tgt, name):
    def fn(hb, tb):
        lat = (pl.program_id(0) % lay.bps) >= lay.cb
        e = jnp.where(lat, hb - tb, 0.0)
        return e * (1.0 / D), jnp.sum(e * e, axis=0, keepdims=True) * (0.5 / D)
    return rowwise(lay, name, fn, [h, tgt], outs=[(D, F32)], sums=[(1, D)])


def adamw(w, g, m, v, name):
    shape = w.shape
    w2, g2, m2, v2 = (t.reshape(-1, shape[-1]) for t in (w, g, m, v))
    rows, width = w2.shape
    tr = 256 if rows % 256 == 0 else rows
    c1 = 1.0 - ADAM_B1 ** ADAM_STEP
    c2 = 1.0 - ADAM_B2 ** ADAM_STEP

    def body(w_ref, g_ref, m_ref, v_ref, d_ref, mo_ref, vo_ref):
        gv = g_ref[...]
        mn = ADAM_B1 * m_ref[...] + (1.0 - ADAM_B1) * gv
        vn = ADAM_B2 * v_ref[...] + (1.0 - ADAM_B2) * (gv * gv)
        d_ref[...] = -ADAM_LR * ((mn / c1) / (jnp.sqrt(vn / c2) + ADAM_EPS) + ADAM_WD * w_ref[...])
        mo_ref[...] = mn
        vo_ref[...] = vn

    spec = pl.BlockSpec((tr, width), lambda i: (i, 0))
    d, mn, vn = pl.pallas_call(body, name=name, out_shape=[_sds((rows, width), F32)] * 3, grid=(rows // tr,),
                               in_specs=[spec] * 4, out_specs=[spec] * 3, compiler_params=_cp(("parallel",)))(w2, g2, m2, v2)
    return d.reshape(shape), mn.reshape(shape), vn.reshape(shape)


def adamw_ffn(w, m, v, red, kind, ns, name):
    shape = w.shape
    w2, m2, v2 = (t.reshape(-1, shape[-1]) for t in (w, m, v))
    rows, width = w2.shape
    c1 = 1.0 - ADAM_B1 ** ADAM_STEP
    c2 = 1.0 - ADAM_B2 ** ADAM_STEP
    if kind < 2:
        tr, nb = 256, D // 256
        gspec = pl.BlockSpec((ns, tr), lambda i: ((i // nb) * 3 + kind, i % nb))
    else:
        tr, nb = ns // 2, 2
        gspec = pl.BlockSpec((tr, D), lambda i: (((i // nb) * 3 + kind) * nb + i % nb, 0))

    def body(w_ref, g_ref, m_ref, v_ref, go_ref, d_ref, mo_ref, vo_ref):
        gv = g_ref[...].T if kind < 2 else g_ref[...]
        mn = ADAM_B1 * m_ref[...] + (1.0 - ADAM_B1) * gv
        vn = ADAM_B2 * v_ref[...] + (1.0 - ADAM_B2) * (gv * gv)
        go_ref[...] = gv
        d_ref[...] = -ADAM_LR * ((mn / c1) / (jnp.sqrt(vn / c2) + ADAM_EPS) + ADAM_WD * w_ref[...])
        mo_ref[...] = mn
        vo_ref[...] = vn

    spec = pl.BlockSpec((tr, width), lambda i: (i, 0))
    outs = pl.pallas_call(body, name=name, out_shape=[_sds((rows, width), F32)] * 4, grid=(rows // tr,),
                          in_specs=[spec, gspec, spec, spec], out_specs=[spec] * 4,
                          compiler_params=_cp(("parallel",)))(w2, red, m2, v2)
    return tuple(t.reshape(shape) for t in outs)


def mod_mm(sc, w_mod, bias, name):
    wm = w_mod.shape[-1]
    tn = _pick(wm, (768, 512, 384, 256, 128))

    def body(a_ref, b_ref, c_ref, o_ref):
        o_ref[...] = _nn(a_ref[...], b_ref[...].astype(BF16)) + c_ref[...]

    return pl.pallas_call(
        body, name=name, out_shape=_sds((DEPTH, 32, wm), F32), grid=(DEPTH, wm // tn),
        in_specs=[pl.BlockSpec((32, D), lambda l, j: (0, 0)), pl.BlockSpec((None, D, tn), lambda l, j: (l, 0, j)),
                  pl.BlockSpec((None, 1, tn), lambda l, j: (l, 0, j))],
        out_specs=pl.BlockSpec((None, 32, tn), lambda l, j: (l, 0, j)),
        compiler_params=_cp(("parallel", "parallel")))(sc, w_mod, bias)


def wmod_dw(sc, dcol, name):
    wm = dcol.shape[-1]
    tm = 256

    def body(a_ref, b_ref, o_ref):
        o_ref[...] = _tn(a_ref[...], b_ref[...].astype(BF16))

    return pl.pallas_call(
        body, name=name, out_shape=_sds((DEPTH, D, wm), F32), grid=(DEPTH, D // tm),
        in_specs=[pl.BlockSpec((32, tm), lambda l, i: (0, i)), pl.BlockSpec((None, 32, wm), lambda l, i: (l, 0, 0))],
        out_specs=pl.BlockSpec((None, tm, wm), lambda l, i: (l, i, 0)),
        compiler_params=_cp(("parallel", "parallel")))(sc, dcol)


def cctx_dx(drow, w_mod, name):
    wm = w_mod.shape[-1]

    def body(a_ref, b_ref, o_ref):
        o_ref[...] = _nt(a_ref[...].astype(BF16), b_ref[...].astype(BF16))

    return pl.pallas_call(
        body, name=name, out_shape=_sds((DEPTH, 16, D), F32), grid=(DEPTH,),
        in_specs=[pl.BlockSpec((None, 16, wm), lambda l: (l, 0, 0)), pl.BlockSpec((None, D, wm), lambda l: (l, 0, 0))],
        out_specs=pl.BlockSpec((None, 16, D), lambda l: (l, 0, 0)), compiler_params=_cp(("parallel",), VMEM_BIG))(drow, w_mod)


HEAD_PERM = (0, 4, 1, 5, 2, 6, 3, 7)


def _rot_rows(wt):
    return jnp.concatenate([-wt[32:64], wt[0:32]], axis=0)


def _unrot_rows(g):
    return jnp.concatenate([g[32:64], -g[0:32]], axis=0)


def _heads(a, n):
    return [a[64 * i:64 * (i + 1)] for i in range(n)]


def kernel(x, c, ctx, c_ctx, w_mod, b_mod, ln_g, ln_b, ffn_w_gate, ffn_w_up, ffn_w_down, mix_ab_w_in, attn_sink, pool_w, pool_scale, mix_ab_w_out, lru_w_in, lru_conv_w, lru_conv_b, lru_wa, lru_ba, lru_wx, lru_bx, lru_lambda, lru_w_out, loss_target, m_c_ctx, m_w_mod, m_b_mod, m_ln_g, m_ln_b, m_ffn_w_gate, m_ffn_w_up, m_ffn_w_down, m_mix_ab_w_in, m_attn_sink, m_pool_w, m_pool_scale, m_mix_ab_w_out, m_lru_w_in, m_lru_conv_w, m_lru_conv_b, m_lru_wa, m_lru_ba, m_lru_wx, m_lru_bx, m_lru_lambda, m_lru_w_out, v_c_ctx, v_w_mod, v_b_mod, v_ln_g, v_ln_b, v_ffn_w_gate, v_ffn_w_up, v_ffn_w_down, v_mix_ab_w_in, v_attn_sink, v_pool_w, v_pool_scale, v_mix_ab_w_out, v_lru_w_in, v_lru_conv_w, v_lru_conv_b, v_lru_wa, v_lru_ba, v_lru_wx, v_lru_bx, v_lru_lambda, v_lru_w_out):
    n_lat, n_ctx = x.shape[1], ctx.shape[1]
    lay = Layout(n_ctx, n_lat)
    T = lay.T
    ns = ffn_w_gate.shape[-1]
    n_li, n_ai = lru_w_in.shape[-1], mix_ab_w_in.shape[-1]
    n_ao, n_lo = mix_ab_w_out.shape[1], lru_w_out.shape[1]
    wm = w_mod.shape[-1]
    dsh = ln_g.shape[-1]
    mx, my, mc = lax.axis_index("x"), lax.axis_index("y"), lax.axis_index("c")
    chip = 2 * mx + my
    me = 2 * chip + mc

    c_all = all_gather8(c, "ag8_c").reshape(16, D)
    cc = jnp.concatenate([c_all, c_ctx[None, :], jnp.zeros((15, D), F32)], axis=0)
    sc = silu_rows(cc, "silu_c")
    bias = lax.dynamic_slice(b_mod, (0, chip * wm), (DEPTH, wm)).reshape(DEPTH, 1, wm)
    modg = all_gather_chips(mod_mm(sc, w_mod, bias, "mod_mm"), "ag_mod")
    modtab = []
    for l in range(DEPTH):
        full = jnp.transpose(modg[:, l], (1, 0, 2)).reshape(32, N_CHIP * wm)
        mine = lax.dynamic_slice(full, (2 * me, 0), (2, N_CHIP * wm))
        modtab.append(jnp.concatenate([mine, full[16:17]], axis=0).reshape(3, N_MOD, D))

    small = jnp.concatenate([ln_g.reshape(6, dsh), ln_b.reshape(6, dsh), lru_conv_w[0], lru_conv_b, lru_ba[0],
                             lru_bx[0], lru_lambda[0], jnp.zeros((9, dsh), F32)], axis=0)
    small = all_gather_chips(small.reshape(2, 16, dsh), "ag_small").reshape(N_CHIP, 32, dsh)
    small = jnp.transpose(small, (1, 0, 2)).reshape(32, D)
    ln_g_f, ln_b_f = small[0:6].reshape(2, 3, D), small[6:12].reshape(2, 3, D)
    conv_w_f, conv_b_f = small[12:16], small[16:17]
    lru_vec = small[17:23]

    ffn_sh = jnp.stack([jnp.swapaxes(ffn_w_gate, -1, -2), jnp.swapaxes(ffn_w_up, -1, -2), ffn_w_down], axis=2)
    ffn_sh = ffn_sh.astype(BF16).reshape(2, 6 * ns, D)
    wbuf = all_gather_chips(ffn_sh, "ag_ffn").reshape(N_CHIP, 12 * ns, D)
    mix_sh = jnp.concatenate([lru_w_in[0].T, mix_ab_w_in[0].T, mix_ab_w_out[0], lru_w_out[0]], axis=0).astype(BF16)
    n_mix = n_li + n_ai + n_ao + n_lo
    mixw = all_gather_chips(mix_sh.reshape(2, n_mix // 2, D), "ag_mix").reshape(N_CHIP, n_mix, D)
    o1, o2, o3 = n_li, n_li + n_ai, n_li + n_ai + n_ao
    lru_in_t = mixw[:, 0:o1].reshape(N_CHIP * n_li, D)
    ab_in_t = mixw[:, o1:o2].reshape(N_CHIP * n_ai, D)
    ab_out = mixw[:, o2:o3].reshape(N_CHIP * n_ao, D)
    lru_out = mixw[:, o3:].reshape(N_CHIP * n_lo, D)
    qh, kh = _heads(ab_in_t[Q0:K0], N_HEADS), _heads(ab_in_t[K0:V0], N_KV)
    w_ext_t = jnp.concatenate([qh[h] for h in HEAD_PERM] + [ab_in_t[K0:QR0]]
                              + [_rot_rows(qh[h]) for h in HEAD_PERM] + [_rot_rows(t) for t in kh], axis=0)
    oh = _heads(ab_out[0:ATT_W], N_HEADS)
    w_out_ext = jnp.concatenate([oh[h] for h in HEAD_PERM] + [ab_out[ATT_W:]], axis=0)

    t = jnp.arange(n_lat)
    inv = ROPE_THETA ** (-jnp.arange(16, dtype=F32) / 16.0)
    ang = jnp.concatenate([(t // GRID_W).astype(F32)[:, None] * inv, (t % GRID_W).astype(F32)[:, None] * inv], axis=-1)
    cos1 = jnp.concatenate([jnp.ones((n_ctx, 32), F32), jnp.cos(ang)], axis=0)
    sin1 = jnp.concatenate([jnp.zeros((n_ctx, 32), F32), jnp.sin(ang)], axis=0)
    cos_t = jnp.tile(cos1, (2, 4))
    sin_t = jnp.tile(sin1, (2, 4))
    sk = attn_sink[0]
    sink_tab = jnp.concatenate([jnp.repeat(jnp.stack([sk[:4], sk[4:]], axis=1), HEAD_DIM, axis=1),
                                jnp.zeros((4, 128), F32)], axis=0)
    pscale = pool_scale.reshape(1, POOL_W)

    h0 = jnp.concatenate([ctx, x], axis=1).reshape(T, D)
    tgt = loss_target.reshape(2 * n_lat, D)

    def lnv(l, j):
        return jnp.stack([ln_g_f[l, j], ln_b_f[l, j]])

    def fq(l, f, kind):
        return (l * 2 + f) * 3 + kind

    subs = [(0, 0, 0.5, 0), (0, 3, 1.0, 1), (0, 6, 0.5, 2), (1, 0, 0.5, 0), (1, 3, 1.0, 1), (1, 6, 0.5, 2)]

    def ffn_core(hm, l, f):
        tag = f"l{l}f{f}"
        g, u, a = ffn_up(lay, hm, wbuf, fq(l, f, 0), fq(l, f, 1), ns, f"ffn_up_{tag}")
        return slab_nn_acc(lay, [a], wbuf, [fq(l, f, 2)], ns, f"ffn_down_{tag}"), dict(g=g, u=u, a=a)

    def mixa_core(hm):
        p = mm_nt(hm, w_ext_t, "mixa_in")
        qr, kr, vb, u = rope_fwd(lay, p, cos_t, sin_t, "rope")
        att, lse = attn_fwd(lay, qr, kr, vb, sink_tab, "attn")
        pool = pool_fwd(lay, u, pool_w[0], pscale, "pool")
        cat = jnp.concatenate([att, pool], axis=1)
        return mm_nn(cat, w_out_ext, "mixa_out"), dict(qr=qr, kr=kr, vb=vb, u=u, lse=lse, cat=cat)

    def mixc_core(hm):
        p = mm_nt(hm, lru_in_t, "mixc_in")
        uc = conv_fwd(lay, p, D, conv_w_f, conv_b_f, "conv")
        a, b = lru_coeffs(lay, uc, lru_wa[0], lru_wx[0], lru_vec, "lru_coef")
        s = lru_scan(lay, a, b, "lru_scan")
        o = lru_gate(lay, p, s, "lru_gate")
        return mm_nn(o, lru_out, "mixc_out"), dict(p=p, uc=uc, a=a, s=s, o=o)

    recs = []
    h = h0
    hm = modulate(lay, h0, modtab[0], 0, 1, "mod_first")
    for k, (l, k0, coef, j) in enumerate(subs):
        if k0 == 3:
            y, core = mixa_core(hm) if l == 0 else mixc_core(hm)
        else:
            y, core = ffn_core(hm, l, k0 // 6)
        nxt = None if k == 5 else (modtab[subs[k + 1][0]], subs[k + 1][1], subs[k + 1][1] + 1)
        res = resid_ln(lay, h, y, modtab[l], k0 + 2, coef, lnv(l, j), f"ln_s{k}", nxt=nxt)
        recs.append(dict(h=h, hm=hm, y=y, xhat=res[1], rstd=res[2], **core))
        h = res[0]
        hm = res[3] if nxt is not None else None

    dout, lparts = loss_and_grad(lay, h, tgt, "loss")
    loss = lax.psum(jnp.sum(lparts), ("x", "y", "c"))

    gbuf = lax.empty((N_CHIP, 12 * ns, D), F32)
    dln = {}
    dms = {}
    mixg = {}

    def ffn_core_bwd(dy, r, l, f, gbuf):
        tag = f"l{l}f{f}"
        dg, du = ffn_bwd_da(lay, dy, wbuf, fq(l, f, 2), r["g"], r["u"], ns, f"ffn_da_{tag}")
        gbuf = slab_tn(lay, r["a"], dy, gbuf, fq(l, f, 2), ns, f"ffn_dwd_{tag}")
        gbuf = slab_tn(lay, dg, r["hm"], gbuf, fq(l, f, 0), ns, f"ffn_dwg_{tag}")
        gbuf = slab_tn(lay, du, r["hm"], gbuf, fq(l, f, 1), ns, f"ffn_dwu_{tag}")
        return slab_nn_acc(lay, [dg, du], wbuf, [fq(l, f, 0), fq(l, f, 1)], ns, f"ffn_dh_{tag}"), gbuf

    def mixc_core_bwd(dy, r):
        do_c = mm_nt(dy, lru_out, "mixc_out_dx")
        mixg["lru_out"] = mm_tn(r["o"], dy, "mixc_out_dw")
        dgate, dyg = lru_gate_bwd(lay, r["p"], r["s"], do_c, "lru_gate_b")
        da_c, db_c = lru_scan_bwd(lay, r["a"], r["s"], dyg, "lru_scan_b")
        duc, mixg["wa"], mixg["wx"], mixg["vec"] = lru_coeffs_bwd(lay, r["uc"], lru_wa[0], lru_wx[0], lru_vec, da_c, db_c,
                                                                  "lru_coef_b")
        du_c, mixg["cw"], mixg["cb"] = conv_bwd(lay, r["p"], D, conv_w_f, duc, "conv_b")
        dp_c = jnp.concatenate([dgate, du_c], axis=1).astype(BF16)
        mixg["lru_in_t"] = mm_tn(dp_c, r["hm"], "mixc_in_dw")
        return mm_nn(dp_c, lru_in_t, "mixc_in_dx")

    def mixa_core_bwd(dy, r):
        dcat = mm_nt(dy, w_out_ext, "mixa_out_dx")
        mixg["out_ext"] = mm_tn(r["cat"], dy, "mixa_out_dw")
        dqr, dkr, dv, mixg["sink"] = attn_bwd(lay, r["qr"], r["kr"], r["vb"], sink_tab, r["lse"], dcat, "attn_b")
        du_a, mixg["pw"], mixg["ps"] = pool_bwd(lay, r["u"], dcat, pool_w[0], pscale, "pool_b")
        dp_a = rope_bwd(lay, dqr, dkr, dv, du_a, cos_t, sin_t, "rope_b")
        mixg["ext_t"] = mm_tn(dp_a, r["hm"], "mixa_in_dw")
        return mm_nn(dp_a, w_ext_t, "mixa_in_dx")

    l, k0, coef, j = subs[5]
    dy, dres, s1 = ln_bwd(lay, dout, recs[5]["xhat"], recs[5]["rstd"], recs[5]["y"], modtab[l], k0 + 2, coef, lnv(l, j),
                          "lnb_s5")
    for k in range(5, -1, -1):
        l, k0, coef, j = subs[k]
        r = recs[k]
        if k0 == 3:
            dhm = mixa_core_bwd(dy, r) if l == 0 else mixc_core_bwd(dy, r)
        else:
            dhm, gbuf = ffn_core_bwd(dy, r, l, k0 // 6, gbuf)
        dln[(l, j)] = block_sums(lay, s1, f"bs_ln_s{k}")
        if k > 0:
            lp, k0p, coefp, jp = subs[k - 1]
            rp = recs[k - 1]
            dy, dres, s1, s2 = modb_lnb(lay, dres, dhm, r["h"], modtab[l], k0 + 1, rp["xhat"], rp["rstd"], rp["y"],
                                        modtab[lp], k0p + 2, coefp, lnv(lp, jp), f"modb_lnb_s{k}")
        else:
            gx, s2 = mod_bwd(lay, dres, dhm, r["h"], modtab[l], k0 + 1, "modb_s0")
        dms[(l, k0)] = block_sums(lay, s2, f"bs_mod_s{k}")
    grad_x = gx.reshape(2, n_lat, D)
    g_lru_out, g_wa, g_wx, g_vec, g_cw, g_cb = (mixg[n] for n in ("lru_out", "wa", "wx", "vec", "cw", "cb"))
    g_lru_in_t, g_out_ext, g_sink, g_pw, g_ps, g_ext_t = (mixg[n] for n in ("lru_in_t", "out_ext", "sink", "pw", "ps", "ext_t"))

    rows = []
    for l in range(DEPTH):
        per_k = []
        for k0, j in ((0, 0), (3, 1), (6, 2)):
            per_k += [dms[(l, k0)][:3, 0], dms[(l, k0)][:3, 1], dln[(l, j)][:3, 2]]
        rows.append(jnp.stack(per_k, axis=1).reshape(3, N_MOD * D))
    dmod_loc = jnp.concatenate(rows + [jnp.zeros((2, N_MOD * D), F32)], axis=0)
    dmod_all, g_b_mod = mod_grad_rows(all_gather8(dmod_loc, "ag8_dmod"), "dmod_rows")
    dcol = lax.dynamic_slice(dmod_all, (0, 0, chip * wm), (DEPTH, 32, wm))
    g_w_mod = wmod_dw(sc, dcol, "wmod_dw")
    g_cctx = cctx_grad(cctx_dx(dcol[:, 16:32], w_mod, "cctx_dx"), c_ctx[None, :], "cctx_grad")

    gq = _heads(g_ext_t[Q0:K0], N_HEADS)
    gqr = _heads(g_ext_t[QR0:KR0], N_HEADS)
    g_q = [None] * N_HEADS
    for i, h in enumerate(HEAD_PERM):
        g_q[h] = gq[i] + _unrot_rows(gqr[i])
    gk = [a + _unrot_rows(b) for a, b in zip(_heads(g_ext_t[K0:V0], N_KV), _heads(g_ext_t[KR0:PEXT], N_KV))]
    g_ab_in_t = jnp.concatenate(g_q + gk + [g_ext_t[V0:QR0]], axis=0)
    go = _heads(g_out_ext[0:ATT_W], N_HEADS)
    g_o = [None] * N_HEADS
    for i, h in enumerate(HEAD_PERM):
        g_o[h] = go[i]
    g_ab_out = jnp.concatenate(g_o + [g_out_ext[ATT_W:]], axis=0)
    mix_g = jnp.concatenate([g_lru_in_t.reshape(N_CHIP, n_li, D), g_ab_in_t.reshape(N_CHIP, n_ai, D),
                             g_ab_out.reshape(N_CHIP, n_ao, D), g_lru_out.reshape(N_CHIP, n_lo, D)], axis=1)

    g_ln_g = jnp.stack([jnp.stack([dln[(l, j)][3, 1] for j in range(3)]) for l in range(DEPTH)])
    g_ln_b = jnp.stack([jnp.stack([dln[(l, j)][3, 0] for j in range(3)]) for l in range(DEPTH)])
    sink_row = jnp.sum(g_sink, axis=0)[:4]
    g_sink8 = jnp.concatenate([sink_row[:, 0], sink_row[:, HEAD_DIM]])
    misc = jnp.concatenate([g_sink8, jnp.sum(g_ps, axis=0).reshape(POOL_W), jnp.zeros((D - 8 - POOL_W,), F32)])
    small_g = jnp.concatenate([
        g_ln_g.reshape(6, D), g_ln_b.reshape(6, D), jnp.sum(g_cw, axis=0), jnp.sum(g_cb, axis=0), g_vec,
        misc[None, :], jnp.sum(g_pw, axis=0).reshape(64, D), g_wa.reshape(256, D), g_wx.reshape(256, D), g_cctx,
        jnp.zeros((39, D), F32)], axis=0)
    n_small = small_g.shape[0] // N_CHIP
    mix_buf = jnp.concatenate([mix_g, small_g.reshape(N_CHIP, n_small, D)], axis=1)
    n_mb = n_mix + n_small

    ffn_red = reduce_scatter_chips(gbuf.reshape(N_CHIP, 2, 6 * ns, D), "ffn", wire=BF16).reshape(12 * ns, D)
    mix_red = reduce_scatter_chips(mix_buf.reshape(N_CHIP, 2, n_mb // 2, D), "mix").reshape(n_mb, D)
    small_red = all_gather_chips(mix_red[n_mix:].reshape(2, n_small // 2, D), "ag_smallg").reshape(N_CHIP * n_small, D)

    ffn_kind = dict(ffn_w_gate=0, ffn_w_up=1, ffn_w_down=2)

    def cols(a):
        return lax.dynamic_slice_in_dim(a, chip * dsh, dsh, axis=a.ndim - 1)

    sr = small_red
    grads = dict(
        c_ctx=sr[600], w_mod=g_w_mod, b_mod=g_b_mod,
        ln_g=cols(sr[0:6]).reshape(2, 3, dsh), ln_b=cols(sr[6:12]).reshape(2, 3, dsh),
        mix_ab_w_in=mix_red[o1:o2].T[None], attn_sink=sr[23, 0:8][None], pool_w=sr[24:88].reshape(1, 4, 128, 128),
        pool_scale=sr[23, 8:8 + POOL_W][None], mix_ab_w_out=mix_red[o2:o3][None], lru_w_in=mix_red[0:o1].T[None],
        lru_conv_w=cols(sr[12:16])[None], lru_conv_b=cols(sr[16:17]), lru_wa=sr[88:344].reshape(1, 2, 8, 128, 128),
        lru_ba=cols(sr[17:19])[None], lru_wx=sr[344:600].reshape(1, 2, 8, 128, 128), lru_bx=cols(sr[19:21])[None],
        lru_lambda=cols(sr[21:23])[None], lru_w_out=mix_red[o3:n_mix][None])
    params = dict(c_ctx=(c_ctx, m_c_ctx, v_c_ctx), w_mod=(w_mod, m_w_mod, v_w_mod), b_mod=(b_mod, m_b_mod, v_b_mod),
                  ln_g=(ln_g, m_ln_g, v_ln_g), ln_b=(ln_b, m_ln_b, v_ln_b),
                  ffn_w_gate=(ffn_w_gate, m_ffn_w_gate, v_ffn_w_gate), ffn_w_up=(ffn_w_up, m_ffn_w_up, v_ffn_w_up),
                  ffn_w_down=(ffn_w_down, m_ffn_w_down, v_ffn_w_down),
                  mix_ab_w_in=(mix_ab_w_in, m_mix_ab_w_in, v_mix_ab_w_in), attn_sink=(attn_sink, m_attn_sink, v_attn_sink),
                  pool_w=(pool_w, m_pool_w, v_pool_w), pool_scale=(pool_scale, m_pool_scale, v_pool_scale),
                  mix_ab_w_out=(mix_ab_w_out, m_mix_ab_w_out, v_mix_ab_w_out), lru_w_in=(lru_w_in, m_lru_w_in, v_lru_w_in),
                  lru_conv_w=(lru_conv_w, m_lru_conv_w, v_lru_conv_w), lru_conv_b=(lru_conv_b, m_lru_conv_b, v_lru_conv_b),
                  lru_wa=(lru_wa, m_lru_wa, v_lru_wa), lru_ba=(lru_ba, m_lru_ba, v_lru_ba), lru_wx=(lru_wx, m_lru_wx, v_lru_wx),
                  lru_bx=(lru_bx, m_lru_bx, v_lru_bx), lru_lambda=(lru_lambda, m_lru_lambda, v_lru_lambda),
                  lru_w_out=(lru_w_out, m_lru_w_out, v_lru_w_out))
    gl, dl, ml, vl = [], [], [], []
    for name, (w, m, v) in params.items():
        if name in ffn_kind:
            g, d, mn, vn = adamw_ffn(w, m, v, ffn_red, ffn_kind[name], ns, f"adamw_{name}")
        else:
            g = grads[name].reshape(w.shape)
            d, mn, vn = adamw(w, g, m, v, f"adamw_{name}")
        gl.append(g)
        dl.append(d)
        ml.append(mn)
        vl.append(vn)
    return (loss, grad_x, *gl, *dl, *ml, *vl)
```

```python
import functools
import math

import jax
import jax.numpy as jnp
from jax import lax
from jax.experimental import pallas as pl
from jax.experimental.pallas import tpu as pltpu

F32, BF16 = jnp.float32, jnp.bfloat16
MESH = pl.DeviceIdType.MESH
ANY = pl.BlockSpec(memory_space=pl.ANY)
VMEM_SPEC = pl.BlockSpec(memory_space=pltpu.VMEM)

D = 1024
N_CHIP = 4
HEAD_DIM, N_HEADS, N_KV = 64, 8, 2
ATT_W, KV_W, POOL_W = 512, 128, 512
POOL_WINDOWS = (2, 4, 8, 16)
BLK = 128
ATT_SCALE = HEAD_DIM ** -0.5
ROPE_THETA = 10000.0
GRID_W = 64
LRU_C = 8.0
LN_EPS = 1e-5
NEG_INF = -1e30
DEPTH = 2
ALPHA = (2 * DEPTH) ** 0.25
N_MOD = 9
ADAM_LR, ADAM_B1, ADAM_B2, ADAM_EPS, ADAM_WD, ADAM_STEP = 0.001, 0.9, 0.999, 1e-08, 0.01, 10
VMEM_BIG = 48 * 1024 * 1024


def _cp(sem=None, vmem=None):
    kw = {}
    if sem is not None:
        kw["dimension_semantics"] = sem
    if vmem is not None:
        kw["vmem_limit_bytes"] = vmem
    return pltpu.CompilerParams(**kw)


def _sds(shape, dtype):
    return jax.ShapeDtypeStruct(tuple(shape), dtype)


def _pick(n, cands):
    for c in cands:
        if n % c == 0:
            return c
    return n


def _dot(a, b, dims):
    return lax.dot_general(a, b, (dims, ((), ())), preferred_element_type=F32)


def _nn(a, b):
    return _dot(a, b, ((1,), (0,)))


def _nt(a, b):
    return _dot(a, b, ((1,), (1,)))


def _tn(a, b):
    return _dot(a, b, ((0,), (0,)))


def _sigmoid(x):
    return 1.0 / (1.0 + jnp.exp(-x))


def _me():
    return lax.axis_index("x"), lax.axis_index("y"), lax.axis_index("c")


def _rcopy(src, dst, ssem, rsem, dev):
    return pltpu.make_async_remote_copy(src_ref=src, dst_ref=dst, send_sem=ssem, recv_sem=rsem,
                                        device_id=dev, device_id_type=MESH)


def all_gather8(x, name):
    def body(x_ref, o_ref, ssem, rsem, lsem):
        mx, my, mc = _me()
        me = 4 * mx + 2 * my + mc
        loc = pltpu.make_async_copy(x_ref, o_ref.at[me], lsem)
        loc.start()
        peers = []
        for m in range(1, 8):
            px = 1 - mx if (m >> 2) & 1 else mx
            py = 1 - my if (m >> 1) & 1 else my
            pc = 1 - mc if m & 1 else mc
            peers.append((px, py, pc))
        sends = [_rcopy(x_ref, o_ref.at[me], ssem.at[k], rsem.at[k], p) for k, p in enumerate(peers)]
        for cp in sends:
            cp.start()
        for k, (px, py, pc) in enumerate(peers):
            _rcopy(x_ref, o_ref.at[4 * px + 2 * py + pc], ssem.at[k], rsem.at[k], (px, py, pc)).wait_recv()
        for cp in sends:
            cp.wait_send()
        loc.wait()

    return pl.pallas_call(
        body, name=name, out_shape=_sds((8,) + x.shape, x.dtype),
        in_specs=[VMEM_SPEC], out_specs=VMEM_SPEC,
        scratch_shapes=[pltpu.SemaphoreType.DMA((7,)), pltpu.SemaphoreType.DMA((7,)), pltpu.SemaphoreType.DMA],
    )(x)


_ROW_BLOCKS = (512, 384, 352, 256, 224, 128)


def _idx(v):
    return jnp.reshape(v, (1,)).astype(jnp.int32)


def place_slab(shard, name):
    _, h, w = shard.shape
    th = _pick(h, _ROW_BLOCKS)

    def body(s_ref, x_ref, o_ref):
        del s_ref
        o_ref[...] = x_ref[...]

    return pl.pallas_call(
        body, name=name, out_shape=_sds((N_CHIP,) + shard.shape, shard.dtype),
        grid_spec=pltpu.PrefetchScalarGridSpec(
            num_scalar_prefetch=1, grid=(2, h // th),
            in_specs=[pl.BlockSpec((None, th, w), lambda k, r, s: (k, r, 0))],
            out_specs=pl.BlockSpec((None, None, th, w), lambda k, r, s: (s[0], k, r, 0))),
    )(_idx(2 * lax.axis_index("x") + lax.axis_index("y")), shard)


def all_gather_chips(shard, name):
    def body(x_ref, o_ref, ssem, rsem):
        del x_ref
        mx, my, mc = _me()
        s = 2 * mx + my
        sib = (mx, my, 1 - mc)
        chips = [(1 - mx, my), (mx, 1 - my), (1 - mx, 1 - my)]
        first = [_rcopy(o_ref.at[s, mc], o_ref.at[s, mc], ssem.at[j], rsem.at[j], (px, py, mc))
                 for j, (px, py) in enumerate(chips)]
        for cp in first:
            cp.start()
        passed = []
        for j, (px, py) in enumerate(chips):
            ps = 2 * px + py
            _rcopy(o_ref.at[ps, mc], o_ref.at[ps, mc], ssem.at[j], rsem.at[j], (px, py, mc)).wait_recv()
            fw = _rcopy(o_ref.at[ps, mc], o_ref.at[ps, mc], ssem.at[3 + j], rsem.at[3 + j], sib)
            fw.start()
            passed.append(fw)
        for j, (px, py) in enumerate(chips):
            ps = 2 * px + py
            _rcopy(o_ref.at[ps, 1 - mc], o_ref.at[ps, 1 - mc], ssem.at[3 + j], rsem.at[3 + j], sib).wait_recv()
        for cp in first + passed:
            cp.wait_send()

    full = place_slab(shard, name + "_place")
    return pl.pallas_call(
        body, name=name, out_shape=_sds(full.shape, full.dtype), in_specs=[ANY], out_specs=ANY,
        input_output_aliases={0: 0},
        scratch_shapes=[pltpu.SemaphoreType.DMA((6,)), pltpu.SemaphoreType.DMA((6,))],
    )(full)


def sibling_send_other_half(buf, name):
    def body(x_ref, o_ref, ssem, rsem):
        mx, my, mc = _me()
        sib = (mx, my, 1 - mc)
        cps = [_rcopy(x_ref.at[k, 1 - mc], o_ref.at[k], ssem.at[k], rsem.at[k], sib) for k in range(N_CHIP)]
        for cp in cps:
            cp.start()
        for cp in cps:
            cp.wait_recv()
        for cp in cps:
            cp.wait_send()

    n, _, h, w = buf.shape
    return pl.pallas_call(
        body, name=name, out_shape=_sds((n, h, w), buf.dtype), in_specs=[ANY], out_specs=ANY,
        scratch_shapes=[pltpu.SemaphoreType.DMA((N_CHIP,)), pltpu.SemaphoreType.DMA((N_CHIP,))],
    )(buf)


def chips_all_to_all(q, name):
    def body(x_ref, o_ref, ssem, rsem):
        mx, my, mc = _me()
        s = 2 * mx + my
        chips = [(1 - mx, my), (mx, 1 - my), (1 - mx, 1 - my)]
        cps = [_rcopy(x_ref.at[2 * px + py], o_ref.at[s], ssem.at[j], rsem.at[j], (px, py, mc))
               for j, (px, py) in enumerate(chips)]
        for cp in cps:
            cp.start()
        for j, (px, py) in enumerate(chips):
            ps = 2 * px + py
            _rcopy(x_ref.at[ps], o_ref.at[ps], ssem.at[j], rsem.at[j], (px, py, mc)).wait_recv()
        for cp in cps:
            cp.wait_send()

    return pl.pallas_call(
        body, name=name, out_shape=_sds(q.shape, q.dtype), in_specs=[ANY], out_specs=ANY,
        scratch_shapes=[pltpu.SemaphoreType.DMA((3,)), pltpu.SemaphoreType.DMA((3,))],
    )(q)


def sibling_join_halves(both, name):
    def body(x_ref, o_ref, ssem, rsem):
        del x_ref
        mx, my, mc = _me()
        sib = (mx, my, 1 - mc)
        cp = _rcopy(o_ref.at[mc], o_ref.at[mc], ssem, rsem, sib)
        cp.start()
        _rcopy(o_ref.at[1 - mc], o_ref.at[1 - mc], ssem, rsem, sib).wait_recv()
        cp.wait_send()

    return pl.pallas_call(
        body, name=name, out_shape=_sds(both.shape, both.dtype), in_specs=[ANY], out_specs=ANY,
        input_output_aliases={0: 0}, scratch_shapes=[pltpu.SemaphoreType.DMA, pltpu.SemaphoreType.DMA],
    )(both)


def add_own_half(buf, recv, wire, name):
    n, _, h, w = buf.shape
    th = _pick(h, _ROW_BLOCKS)

    def body(c_ref, a_ref, b_ref, o_ref):
        del c_ref
        o_ref[...] = (a_ref[...] + b_ref[...]).astype(o_ref.dtype)

    return pl.pallas_call(
        body, name=name, out_shape=_sds((n, h, w), wire),
        grid_spec=pltpu.PrefetchScalarGridSpec(
            num_scalar_prefetch=1, grid=(n, h // th),
            in_specs=[pl.BlockSpec((None, None, th, w), lambda k, r, c: (k, c[0], r, 0)),
                      pl.BlockSpec((None, th, w), lambda k, r, c: (k, r, 0))],
            out_specs=pl.BlockSpec((None, th, w), lambda k, r, c: (k, r, 0))),
    )(_idx(lax.axis_index("c")), buf, recv)


def sum_slots(q, r, name):
    n, h, w = r.shape
    th = _pick(h, _ROW_BLOCKS)

    def body(i_ref, q_ref, r1, r2, r3, o_ref):
        del i_ref
        o_ref[...] = ((q_ref[...].astype(F32) + r1[...].astype(F32)) + r2[...].astype(F32)) + r3[...].astype(F32)

    def slot(d):
        return lambda i, ix: ((ix[0] + d) % N_CHIP, i, 0)

    return pl.pallas_call(
        body, name=name, out_shape=_sds((2, h, w), F32),
        grid_spec=pltpu.PrefetchScalarGridSpec(
            num_scalar_prefetch=1, grid=(h // th,),
            in_specs=[pl.BlockSpec((None, th, w), slot(d)) for d in (0, 1, 2, 3)],
            out_specs=pl.BlockSpec((None, th, w), lambda i, ix: (ix[1], i, 0))),
    )(jnp.stack([2 * lax.axis_index("x") + lax.axis_index("y"), lax.axis_index("c")]).astype(jnp.int32), q, r, r, r)


def reduce_scatter_chips(buf, tag, wire=F32):
    recv = sibling_send_other_half(buf, f"rs_sib_{tag}")
    q = add_own_half(buf, recv, wire, f"rs_add2_{tag}")
    r = chips_all_to_all(q, f"rs_a2a_{tag}")
    red = sum_slots(q, r, f"rs_add4_{tag}")
    return sibling_join_halves(red, f"rs_join_{tag}")


class Layout:
    def __init__(self, n_ctx, n_lat):
        self.C, self.L = n_ctx, n_lat
        self.PS = n_ctx + n_lat
        self.T = 2 * self.PS
        self.tr = _pick(math.gcd(n_ctx, n_lat), (256, 128))
        self.bps = self.PS // self.tr
        self.cb = n_ctx // self.tr
        self.nblk = self.T // self.tr
        self.tm = _pick(self.T, (512, 256, 128))

    def seg(self, i):
        return jnp.where(i % self.bps < self.cb, 2, i // self.bps)


def rowwise(lay, name, fn, rows, segs=(), vecs=(), outs=(), sums=()):
    tr, nblk = lay.tr, lay.nblk
    n_r, n_s, n_v, n_o = len(rows), len(segs), len(vecs), len(outs)

    def body(*refs):
        ins = refs[:n_r + n_s + n_v]
        ors = refs[n_r + n_s + n_v:]
        vals = [r[...] for r in ins[:n_r]] + [r[0] for r in ins[n_r:n_r + n_s]] + [r[...] for r in ins[n_r + n_s:]]
        res = fn(*vals)
        for k in range(n_o):
            ors[k][...] = res[k].astype(ors[k].dtype)
        for k in range(len(sums)):
            ors[n_o + k][0] = res[n_o + k]

    def all_rows(i):
        return (i, 0)

    def lat_rows(i):
        return ((i // lay.bps) * (lay.bps - lay.cb) + jnp.maximum(i % lay.bps - lay.cb, 0), 0)

    in_specs = [pl.BlockSpec((tr, a.shape[1]), all_rows if a.shape[0] == lay.T else lat_rows) for a in rows]
    in_specs += [pl.BlockSpec((1,) + a.shape[1:], lambda i: (lay.seg(i), 0, 0)) for a in segs]
    in_specs += [pl.BlockSpec(a.shape, lambda i: (0, 0)) for a in vecs]
    out_shape = [_sds((2 * lay.L if o[2:] else lay.T, o[0]), o[1]) for o in outs]
    out_shape += [_sds((nblk, r, w), F32) for r, w in sums]
    out_specs = [pl.BlockSpec((tr, o[0]), lat_rows if o[2:] else all_rows) for o in outs]
    out_specs += [pl.BlockSpec((1, r, w), lambda i: (i, 0, 0)) for r, w in sums]
    sem = "arbitrary" if any(o[2:] for o in outs) else "parallel"
    return pl.pallas_call(body, name=name, out_shape=out_shape, grid=(nblk,), in_specs=in_specs,
                          out_specs=out_specs, compiler_params=_cp((sem,)))(*rows, *segs, *vecs)


def modulate(lay, h, mod, k_shift, k_scale, name):
    def fn(hb, m):
        return (hb * (1.0 + m[k_scale:k_scale + 1]) + m[k_shift:k_shift + 1],)
    return rowwise(lay, name, fn, [h], segs=[mod], outs=[(D, BF16)])[0]


def resid_ln(lay, h, y, mod, k_gate, coef, lnv, name, nxt=None):
    def fn(hb, yb, m, *rest):
        ln = rest[-1]
        z = ALPHA * hb + (coef * m[k_gate:k_gate + 1]) * yb
        mu = jnp.mean(z, axis=-1, keepdims=True)
        zc = z - mu
        var = jnp.mean(zc * zc, axis=-1, keepdims=True)
        rstd = lax.rsqrt(var + LN_EPS)
        xhat = zc * rstd
        out = xhat * ln[0:1] + ln[1:2]
        if nxt is None:
            return out, xhat, rstd
        mn = rest[0]
        return out, xhat, rstd, out * (1.0 + mn[nxt[2]:nxt[2] + 1]) + mn[nxt[1]:nxt[1] + 1]
    segs = [mod] if nxt is None else [mod, nxt[0]]
    outs = [(D, F32), (D, F32), (1, F32)] + ([] if nxt is None else [(D, BF16)])
    return rowwise(lay, name, fn, [h, y], segs=segs, vecs=[lnv], outs=outs)


def _ln_bwd_math(do, xh, rs, yb, gate, coef, ln):
    dxh = do * ln[0:1]
    m1 = jnp.mean(dxh, axis=-1, keepdims=True)
    m2 = jnp.mean(dxh * xh, axis=-1, keepdims=True)
    dz = rs * (dxh - m1 - xh * m2)
    s = jnp.concatenate([jnp.sum(do, axis=0, keepdims=True), jnp.sum(do * xh, axis=0, keepdims=True),
                         jnp.sum(coef * dz * yb, axis=0, keepdims=True)], axis=0)
    return (coef * gate) * dz, ALPHA * dz, s


def _mod_bwd_math(dr, dm, hb, scale):
    s = jnp.concatenate([jnp.sum(dm, axis=0, keepdims=True), jnp.sum(dm * hb, axis=0, keepdims=True)], axis=0)
    return dr + dm * (1.0 + scale), s


def ln_bwd(lay, dout, xhat, rstd, y, mod, k_gate, coef, lnv, name):
    def fn(do, xh, rs, yb, m, ln):
        return _ln_bwd_math(do, xh, rs, yb, m[k_gate:k_gate + 1], coef, ln)
    return rowwise(lay, name, fn, [dout, xhat, rstd, y], segs=[mod], vecs=[lnv],
                   outs=[(D, BF16), (D, F32)], sums=[(3, D)])


def mod_bwd(lay, dres, dhm, h, mod, k_scale, name):
    def fn(dr, dm, hb, m):
        return _mod_bwd_math(dr, dm, hb, m[k_scale:k_scale + 1])
    return rowwise(lay, name, fn, [dres, dhm, h], segs=[mod], outs=[(D, F32, "lat")], sums=[(2, D)])


def modb_lnb(lay, dres, dhm, h, mod, k_scale, xhat, rstd, y, mod_p, k_gate, coef, lnv, name):
    def fn(dr, dm, hb, xh, rs, yb, m, mp, ln):
        dh, s2 = _mod_bwd_math(dr, dm, hb, m[k_scale:k_scale + 1])
        dy, dres_p, s1 = _ln_bwd_math(dh, xh, rs, yb, mp[k_gate:k_gate + 1], coef, ln)
        return dy, dres_p, s1, s2
    return rowwise(lay, name, fn, [dres, dhm, h, xhat, rstd, y], segs=[mod, mod_p], vecs=[lnv],
                   outs=[(D, BF16), (D, F32)], sums=[(3, D), (2, D)])


def block_sums(lay, parts, name):
    nblk, r, w = parts.shape

    def body(p_ref, o_ref):
        acc = [None, None, None]
        for i in range(nblk):
            sg = 2 if i % lay.bps < lay.cb else i // lay.bps
            acc[sg] = p_ref[i] if acc[sg] is None else acc[sg] + p_ref[i]
        for k in range(3):
            o_ref[k] = acc[k]
        o_ref[3] = (acc[0] + acc[1]) + acc[2]

    return pl.pallas_call(body, name=name, out_shape=_sds((4, r, w), F32), in_specs=[VMEM_SPEC],
                          out_specs=VMEM_SPEC)(parts)


def mm_nn(a, b, name, out_dtype=F32, bias=None):
    m, k = a.shape
    n = b.shape[1]
    tm = _pick(m, (512, 256, 128, 64, 32, 16, 8))
    tn = _pick(n, (1024, 768, 640, 512, 384, 256, 128))

    def body(*refs):
        if bias is None:
            a_ref, b_ref, o_ref = refs
            o_ref[...] = _nn(a_ref[...].astype(BF16), b_ref[...].astype(BF16)).astype(o_ref.dtype)
        else:
            a_ref, b_ref, c_ref, o_ref = refs
            o_ref[...] = (_nn(a_ref[...].astype(BF16), b_ref[...].astype(BF16)) + c_ref[...]).astype(o_ref.dtype)

    in_specs = [pl.BlockSpec((tm, k), lambda i, j: (i, 0)), pl.BlockSpec((k, tn), lambda i, j: (0, j))]
    ops = [a, b]
    if bias is not None:
        in_specs.append(pl.BlockSpec((1, tn), lambda i, j: (0, j)))
        ops.append(bias)
    return pl.pallas_call(body, name=name, out_shape=_sds((m, n), out_dtype), grid=(m // tm, n // tn),
                          in_specs=in_specs, out_specs=pl.BlockSpec((tm, tn), lambda i, j: (i, j)),
                          compiler_params=_cp(("parallel", "parallel"), VMEM_BIG))(*ops)


def mm_nt(a, b, name, out_dtype=F32):
    m, k = a.shape
    n = b.shape[0]
    tm = _pick(m, (512, 256, 128, 64, 32, 16, 8))
    tn = _pick(n, (1024, 768, 640, 512, 384, 256, 128))

    def body(a_ref, b_ref, o_ref):
        o_ref[...] = _nt(a_ref[...].astype(BF16), b_ref[...].astype(BF16)).astype(o_ref.dtype)

    return pl.pallas_call(body, name=name, out_shape=_sds((m, n), out_dtype), grid=(m // tm, n // tn),
                          in_specs=[pl.BlockSpec((tm, k), lambda i, j: (i, 0)), pl.BlockSpec((tn, k), lambda i, j: (j, 0))],
                          out_specs=pl.BlockSpec((tm, tn), lambda i, j: (i, j)),
                          compiler_params=_cp(("parallel", "parallel"), VMEM_BIG))(a, b)


def mm_tn(a, b, name):
    t, m = a.shape
    n = b.shape[1]
    tk = _pick(t, (512, 256, 128, 64, 32, 16))
    tm = _pick(m, (512, 384, 256, 128))

    def body(a_ref, b_ref, o_ref):
        @pl.when(pl.program_id(1) == 0)
        def _():
            o_ref[...] = jnp.zeros_like(o_ref)
        o_ref[...] += _tn(a_ref[...].astype(BF16), b_ref[...].astype(BF16))

    return pl.pallas_call(body, name=name, out_shape=_sds((m, n), F32), grid=(m // tm, t // tk),
                          in_specs=[pl.BlockSpec((tk, tm), lambda i, k: (k, i)), pl.BlockSpec((tk, n), lambda i, k: (k, 0))],
                          out_specs=pl.BlockSpec((tm, n), lambda i, k: (i, 0)),
                          compiler_params=_cp(("parallel", "arbitrary"), VMEM_BIG))(a, b)


class Rider:
    def __init__(self, ins, outs, aliases, nsem, start, wait):
        self.ins, self.outs, self.aliases, self.nsem, self.start, self.wait = ins, outs, aliases, nsem, start, wait


def _chips_of(mx, my):
    return [(1 - mx, my), (mx, 1 - my), (1 - mx, 1 - my)]


def rider_gather_ici(buf):
    def start(ins, outs, ssem, rsem):
        o = outs[0]
        mx, my, mc = _me()
        s = 2 * mx + my
        for j, (px, py) in enumerate(_chips_of(mx, my)):
            _rcopy(o.at[s, mc], o.at[s, mc], ssem.at[j], rsem.at[j], (px, py, mc)).start()

    def wait(ins, outs, ssem, rsem):
        o = outs[0]
        mx, my, mc = _me()
        s = 2 * mx + my
        for j, (px, py) in enumerate(_chips_of(mx, my)):
            ps = 2 * px + py
            _rcopy(o.at[ps, mc], o.at[ps, mc], ssem.at[j], rsem.at[j], (px, py, mc)).wait_recv()
        for j, (px, py) in enumerate(_chips_of(mx, my)):
            _rcopy(o.at[s, mc], o.at[s, mc], ssem.at[j], rsem.at[j], (px, py, mc)).wait_send()

    return Rider([buf], [_sds(buf.shape, buf.dtype)], {0: 0}, 3, start, wait)


def rider_gather_d2d(buf):
    def start(ins, outs, ssem, rsem):
        o = outs[0]
        mx, my, mc = _me()
        for j, (px, py) in enumerate(_chips_of(mx, my)):
            ps = 2 * px + py
            _rcopy(o.at[ps, mc], o.at[ps, mc], ssem.at[j], rsem.at[j], (mx, my, 1 - mc)).start()

    def wait(ins, outs, ssem, rsem):
        o = outs[0]
        mx, my, mc = _me()
        sib = (mx, my, 1 - mc)
        for j, (px, py) in enumerate(_chips_of(mx, my)):
            ps = 2 * px + py
            _rcopy(o.at[ps, 1 - mc], o.at[ps, 1 - mc], ssem.at[j], rsem.at[j], sib).wait_recv()
        for j, (px, py) in enumerate(_chips_of(mx, my)):
            ps = 2 * px + py
            _rcopy(o.at[ps, mc], o.at[ps, mc], ssem.at[j], rsem.at[j], sib).wait_send()

    return Rider([buf], [_sds(buf.shape, buf.dtype)], {0: 0}, 3, start, wait)


def rider_reduce_sib(buf):
    n, _, h, w = buf.shape

    def start(ins, outs, ssem, rsem):
        mx, my, mc = _me()
        for k in range(N_CHIP):
            _rcopy(ins[0].at[k, 1 - mc], outs[0].at[k], ssem.at[k], rsem.at[k], (mx, my, 1 - mc)).start()

    def wait(ins, outs, ssem, rsem):
        mx, my, mc = _me()
        for k in range(N_CHIP):
            _rcopy(ins[0].at[k, 1 - mc], outs[0].at[k], ssem.at[k], rsem.at[k], (mx, my, 1 - mc)).wait_recv()
        for k in range(N_CHIP):
            _rcopy(ins[0].at[k, 1 - mc], outs[0].at[k], ssem.at[k], rsem.at[k], (mx, my, 1 - mc)).wait_send()

    return Rider([buf], [_sds((n, h, w), buf.dtype)], {}, N_CHIP, start, wait)


def rider_reduce_a2a(q):
    def start(ins, outs, ssem, rsem):
        mx, my, mc = _me()
        s = 2 * mx + my
        for j, (px, py) in enumerate(_chips_of(mx, my)):
            _rcopy(ins[0].at[2 * px + py], outs[0].at[s], ssem.at[j], rsem.at[j], (px, py, mc)).start()

    def wait(ins, outs, ssem, rsem):
        mx, my, mc = _me()
        s = 2 * mx + my
        for j, (px, py) in enumerate(_chips_of(mx, my)):
            ps = 2 * px + py
            _rcopy(ins[0].at[ps], outs[0].at[ps], ssem.at[j], rsem.at[j], (px, py, mc)).wait_recv()
        for j, (px, py) in enumerate(_chips_of(mx, my)):
            _rcopy(ins[0].at[2 * px + py], outs[0].at[s], ssem.at[j], rsem.at[j], (px, py, mc)).wait_send()

    return Rider([q], [_sds(q.shape, q.dtype)], {}, 3, start, wait)


def _host_call(body, rider, name, grid, in_specs, out_specs, out_shape, operands, sem, n_in, n_out):
    if rider is None:
        return pl.pallas_call(body, name=name, out_shape=out_shape, grid=grid, in_specs=in_specs, out_specs=out_specs,
                              compiler_params=_cp(sem, VMEM_BIG))(*operands)
    n_ri, n_ro = len(rider.ins), len(rider.outs)

    def hosted(*refs):
        ins, r_in = refs[:n_in], refs[n_in:n_in + n_ri]
        outs, r_out = refs[n_in + n_ri:n_in + n_ri + n_out], refs[n_in + n_ri + n_out:n_in + n_ri + n_out + n_ro]
        ssem, rsem = refs[-2], refs[-1]
        first = functools.reduce(lambda a, b: a & b, [pl.program_id(k) == 0 for k in range(len(grid))])
        last = functools.reduce(lambda a, b: a & b, [pl.program_id(k) == grid[k] - 1 for k in range(len(grid))])

        @pl.when(first)
        def _():
            rider.start(r_in, r_out, ssem, rsem)
        body(*ins, *outs)

        @pl.when(last)
        def _():
            rider.wait(r_in, r_out, ssem, rsem)

    return pl.pallas_call(
        hosted, name=name, out_shape=list(out_shape) + list(rider.outs), grid=grid,
        in_specs=list(in_specs) + [ANY] * n_ri, out_specs=list(out_specs) + [ANY] * n_ro,
        input_output_aliases={n_in + a: n_out + b for a, b in rider.aliases.items()},
        scratch_shapes=[pltpu.SemaphoreType.DMA((rider.nsem,)), pltpu.SemaphoreType.DMA((rider.nsem,))],
        compiler_params=_cp(("arbitrary",) * len(grid), VMEM_BIG))(*operands, *rider.ins)


def ffn_up(lay, hm, wbuf, ig, iu, ns, name, rider=None):
    tm = lay.tm

    def body(h_ref, wg_ref, wu_ref, g_ref, u_ref, a_ref):
        hb = h_ref[...]
        g = _nt(hb, wg_ref[0])
        u = _nt(hb, wu_ref[0])
        g_ref[0] = g.astype(BF16)
        u_ref[0] = u.astype(BF16)
        a_ref[0] = (g * _sigmoid(g) * u).astype(BF16)

    spec_o = pl.BlockSpec((1, tm, ns), lambda s, i: (s, i, 0))
    return _host_call(
        body, rider, name, (N_CHIP, lay.T // tm),
        [pl.BlockSpec((tm, D), lambda s, i: (i, 0)), pl.BlockSpec((1, ns, D), lambda s, i: (s, ig, 0)),
         pl.BlockSpec((1, ns, D), lambda s, i: (s, iu, 0))],
        [spec_o] * 3, [_sds((N_CHIP, lay.T, ns), BF16)] * 3, (hm, wbuf, wbuf), ("parallel", "parallel"), 3, 3)


def slab_nn_acc(lay, zs, wbuf, idxs, ns, name, rider=None):
    tm = lay.tm
    npair = len(zs)

    def body(*refs):
        o_ref = refs[-1]

        @pl.when(pl.program_id(1) == 0)
        def _():
            o_ref[...] = jnp.zeros_like(o_ref)
        acc = _nn(refs[0][0], refs[npair][0])
        for p in range(1, npair):
            acc += _nn(refs[p][0], refs[npair + p][0])
        o_ref[...] += acc

    in_specs = [pl.BlockSpec((1, tm, ns), lambda i, s: (s, i, 0)) for _ in zs]
    in_specs += [pl.BlockSpec((1, ns, D), functools.partial(lambda i, s, q: (s, q, 0), q=q)) for q in idxs]
    return _host_call(body, rider, name, (lay.T // tm, N_CHIP), in_specs, [pl.BlockSpec((tm, D), lambda i, s: (i, 0))],
                      [_sds((lay.T, D), F32)], (*zs, *([wbuf] * npair)), ("parallel", "arbitrary"), 2 * npair, 1)


def ffn_bwd_da(lay, dy, wbuf, idn, g, u, ns, name, rider=None):
    tm = lay.tm

    def body(dy_ref, wd_ref, g_ref, u_ref, dg_ref, du_ref):
        da = _nt(dy_ref[...], wd_ref[0])
        gv = g_ref[0].astype(F32)
        uv = u_ref[0].astype(F32)
        sg = _sigmoid(gv)
        dg_ref[0] = (da * uv * (sg * (1.0 + gv * (1.0 - sg)))).astype(BF16)
        du_ref[0] = (da * (gv * sg)).astype(BF16)

    spec_z = pl.BlockSpec((1, tm, ns), lambda s, i: (s, i, 0))
    return _host_call(
        body, rider, name, (N_CHIP, lay.T // tm),
        [pl.BlockSpec((tm, D), lambda s, i: (i, 0)), pl.BlockSpec((1, ns, D), lambda s, i: (s, idn, 0)), spec_z, spec_z],
        [spec_z] * 2, [_sds((N_CHIP, lay.T, ns), BF16)] * 2, (dy, wbuf, g, u), ("parallel", "parallel"), 4, 2)


def slab_tn(lay, z, x, gbuf, idx, ns, name):
    tk = lay.tm

    def body(z_ref, x_ref, g_in, o_ref):
        del g_in

        @pl.when(pl.program_id(1) == 0)
        def _():
            o_ref[...] = jnp.zeros_like(o_ref)
        o_ref[0] += _tn(z_ref[0], x_ref[...])

    return pl.pallas_call(
        body, name=name, out_shape=_sds(gbuf.shape, F32), grid=(N_CHIP, lay.T // tk),
        in_specs=[pl.BlockSpec((1, tk, ns), lambda s, k: (s, k, 0)), pl.BlockSpec((tk, D), lambda s, k: (k, 0)), ANY],
        out_specs=pl.BlockSpec((1, ns, D), lambda s, k: (s, idx, 0)),
        input_output_aliases={2: 0}, compiler_params=_cp(("parallel", "arbitrary"), VMEM_BIG))(z, x, gbuf)


Q0, K0, V0, U0, QR0, KR0, PEXT = 0, 512, 640, 768, 1280, 1792, 1920


def rope_fwd(lay, p, cos, sin, name):
    def fn(pb, cs, sn):
        cs4 = jnp.concatenate([cs] * 4, axis=1)
        sn4 = jnp.concatenate([sn] * 4, axis=1)
        qr = pb[:, Q0:K0] * cs4 + pb[:, QR0:KR0] * sn4
        kr = pb[:, K0:V0] * cs + pb[:, KR0:PEXT] * sn
        return qr, kr, pb[:, V0:U0], pb[:, U0:QR0]
    return rowwise(lay, name, fn, [p, cos, sin], outs=[(ATT_W, BF16), (KV_W, BF16), (KV_W, BF16), (POOL_W, F32)])


def rope_bwd(lay, dqr, dkr, dv, du, cos, sin, name):
    def fn(dq, dk, dvb, dub, cs, sn):
        cs4 = jnp.concatenate([cs] * 4, axis=1)
        sn4 = jnp.concatenate([sn] * 4, axis=1)
        return (jnp.concatenate([dq * cs4, dk * cs, dvb, dub, dq * sn4, dk * sn], axis=1),)
    return rowwise(lay, name, fn, [dqr, dkr, dv, du, cos, sin], outs=[(PEXT, BF16)])[0]


def _attn_specs(lay):
    nbs, cbk, lbk = lay.PS // BLK, lay.C // BLK, lay.L // BLK

    def kv_map(j):
        return lambda s, n: (s * nbs + cbk + jnp.clip(n - cbk + j - 1, 0, lbk - 1), 0)

    win = [pl.BlockSpec((BLK, KV_W), kv_map(j)) for j in range(3)]
    ctx = pl.BlockSpec((lay.C, KV_W), lambda s, n: (s * (lay.PS // lay.C), 0))
    return nbs, cbk, lbk, win, ctx


def _attn_masks(n, cbk, lbk):
    row = lax.broadcasted_iota(jnp.int32, (BLK, BLK), 0)
    col = lax.broadcasted_iota(jnp.int32, (BLK, BLK), 1)
    m = n - cbk
    lat = n >= cbk
    valid = [lat & (m >= 1) & (col >= row), lat & (col >= 0), lat & (m <= lbk - 2) & (col <= row)]
    lane_lo = lax.broadcasted_iota(jnp.int32, (BLK, 2 * HEAD_DIM), 1) < HEAD_DIM
    return valid, lane_lo


def attn_fwd(lay, qr, kr, vb, sink_tab, name):
    nbs, cbk, lbk, win, ctx = _attn_specs(lay)

    def body(q_ref, k0, k1, k2, kc_ref, v0, v1, v2, vc_ref, sk_ref, o_ref, l_ref):
        n = pl.program_id(1)
        valid, lane_lo = _attn_masks(n, cbk, lbk)
        ks = [k0[...], k1[...], k2[...]]
        vs = [v0[...], v1[...], v2[...]]
        kc, vc = kc_ref[...], vc_ref[...]
        for p in range(4):
            q2 = q_ref[:, p * 128:(p + 1) * 128]
            outs, lses = [], []
            for hh in range(2):
                qm = jnp.where(lane_lo == (hh == 0), q2, jnp.zeros_like(q2))
                sk = sk_ref[p:p + 1, hh * HEAD_DIM:hh * HEAD_DIM + 1]
                sw = [jnp.where(valid[j], _nt(qm, ks[j]) * ATT_SCALE, NEG_INF) for j in range(3)]
                sc = _nt(qm, kc) * ATT_SCALE
                mx = jnp.maximum(jnp.maximum(jnp.maximum(sw[0].max(-1, keepdims=True), sw[1].max(-1, keepdims=True)),
                                             jnp.maximum(sw[2].max(-1, keepdims=True), sc.max(-1, keepdims=True))), sk)
                ew = [jnp.exp(s - mx) for s in sw]
                ec = jnp.exp(sc - mx)
                den = ew[0].sum(-1, keepdims=True) + ew[1].sum(-1, keepdims=True) + ew[2].sum(-1, keepdims=True)
                den = den + ec.sum(-1, keepdims=True) + jnp.exp(sk - mx)
                o = _nn((ec / den).astype(BF16), vc)
                for j in range(3):
                    o += _nn((ew[j] / den).astype(BF16), vs[j])
                outs.append(o)
                lses.append(jnp.broadcast_to(mx + jnp.log(den), (BLK, 128)))
            o_ref[:, p * 128:(p + 1) * 128] = jnp.where(lane_lo, outs[0], outs[1]).astype(o_ref.dtype)
            l_ref[:, p * 128:(p + 1) * 128] = jnp.where(lane_lo, lses[0], lses[1])

    qspec = pl.BlockSpec((BLK, ATT_W), lambda s, n: (s * nbs + n, 0))
    return pl.pallas_call(
        body, name=name, out_shape=[_sds((lay.T, ATT_W), BF16), _sds((lay.T, ATT_W), F32)], grid=(2, nbs),
        in_specs=[qspec] + win + [ctx] + win + [ctx] + [pl.BlockSpec((8, 128), lambda s, n: (0, 0))],
        out_specs=[qspec, qspec], compiler_params=_cp(("parallel", "parallel")))(qr, kr, kr, kr, kr, vb, vb, vb, vb, sink_tab)


def attn_bwd(lay, qr, kr, vb, sink_tab, lse, datt, name):
    nbs, cbk, lbk, win, ctx = _attn_specs(lay)
    C, PS = lay.C, lay.PS

    def body(q_ref, k0, k1, k2, kc_ref, v0, v1, v2, vc_ref, sk_ref, l_ref, do_ref, dq_ref, dk_ref, dv_ref, ds_ref):
        n = pl.program_id(1)
        valid, lane_lo = _attn_masks(n, cbk, lbk)

        @pl.when(n == 0)
        def _():
            dk_ref[...] = jnp.zeros_like(dk_ref)
            dv_ref[...] = jnp.zeros_like(dv_ref)
            ds_ref[...] = jnp.zeros_like(ds_ref)

        ks = [k0[...], k1[...], k2[...], kc_ref[...]]
        vs = [v0[...], v1[...], v2[...], vc_ref[...]]
        dks = [jnp.zeros((BLK, KV_W), F32)] * 3 + [jnp.zeros((C, KV_W), F32)]
        dvs = list(dks)
        for p in range(4):
            sl = slice(p * 128, (p + 1) * 128)
            q2 = q_ref[:, sl]
            do2 = do_ref[:, sl].astype(BF16)
            lse2 = l_ref[:, sl]
            dq_h, dd_h = [], []
            for hh in range(2):
                sel = lane_lo == (hh == 0)
                qm = jnp.where(sel, q2, jnp.zeros_like(q2))
                dom = jnp.where(sel, do2, jnp.zeros_like(do2))
                lse_h = lse2[:, hh * HEAD_DIM:hh * HEAD_DIM + 1]
                ps, dps = [], []
                for j in range(4):
                    s = _nt(qm, ks[j]) * ATT_SCALE
                    if j < 3:
                        s = jnp.where(valid[j], s, NEG_INF)
                    ps.append(jnp.exp(s - lse_h))
                    dps.append(_nt(dom, vs[j]))
                dd = (ps[0] * dps[0]).sum(-1, keepdims=True) + (ps[1] * dps[1]).sum(-1, keepdims=True)
                dd = dd + (ps[2] * dps[2]).sum(-1, keepdims=True) + (ps[3] * dps[3]).sum(-1, keepdims=True)
                dq = jnp.zeros((BLK, 128), F32)
                for j in range(4):
                    dsb = (ps[j] * (dps[j] - dd) * ATT_SCALE).astype(BF16)
                    dq += _nn(dsb, ks[j])
                    dks[j] = dks[j] + _tn(dsb, qm)
                    dvs[j] = dvs[j] + _tn(ps[j].astype(BF16), dom)
                dq_h.append(dq)
                dd_h.append(jnp.broadcast_to(dd, (BLK, 128)))
            dq_ref[:, sl] = jnp.where(lane_lo, dq_h[0], dq_h[1])
            dd2 = jnp.where(lane_lo, dd_h[0], dd_h[1])
            psink = jnp.exp(sk_ref[p:p + 1, :] - lse2)
            ds_ref[0, p:p + 1, :] += -jnp.sum(psink * dd2, axis=0, keepdims=True)
        dk_ref[0:C, :] += dks[3]
        dv_ref[0:C, :] += dvs[3]
        for j in range(3):
            r0 = pl.multiple_of((cbk + jnp.clip(n - cbk + j - 1, 0, lbk - 1)) * BLK, BLK)
            dk_ref[pl.ds(r0, BLK), :] += dks[j]
            dv_ref[pl.ds(r0, BLK), :] += dvs[j]

    qspec = pl.BlockSpec((BLK, ATT_W), lambda s, n: (s * nbs + n, 0))
    kvout = pl.BlockSpec((PS, KV_W), lambda s, n: (s, 0))
    return pl.pallas_call(
        body, name=name,
        out_shape=[_sds((lay.T, ATT_W), F32), _sds((lay.T, KV_W), F32), _sds((lay.T, KV_W), F32), _sds((2, 8, 128), F32)],
        grid=(2, nbs),
        in_specs=[qspec] + win + [ctx] + win + [ctx] + [pl.BlockSpec((8, 128), lambda s, n: (0, 0)), qspec, qspec],
        out_specs=[qspec, kvout, kvout, pl.BlockSpec((1, 8, 128), lambda s, n: (s, 0, 0))],
        compiler_params=_cp(("parallel", "arbitrary")))(qr, kr, kr, kr, kr, vb, vb, vb, vb, sink_tab, lse, datt)


def _winsum(x, r):
    n = x.shape[0]
    t = lax.broadcasted_iota(jnp.int32, x.shape, 0)
    acc = x
    for o in range(1, r + 1):
        acc = acc + jnp.where(t >= o, pltpu.roll(x, o, 0), 0.0) + jnp.where(t < n - o, pltpu.roll(x, n - o, 0), 0.0)
    return acc


def _wincount(n, r):
    t = lax.broadcasted_iota(jnp.int32, (n, 128), 0)
    return (jnp.minimum(t + r, n - 1) - jnp.maximum(t - r, 0) + 1).astype(F32)


def pool_fwd(lay, u, w_pool, scale, name):
    segs = [(0, lay.C), (lay.C, lay.L)]

    def body(u_ref, w_ref, s_ref, o_ref):
        for r0, n in segs:
            for g, wd in enumerate(POOL_WINDOWS):
                sl = slice(g * 128, (g + 1) * 128)
                x = u_ref[r0:r0 + n, sl]
                d = _winsum(x, wd // 2) / _wincount(n, wd // 2) - x
                y = _nn(d.astype(BF16), w_ref[g].astype(BF16)) * s_ref[:, sl]
                o_ref[r0:r0 + n, sl] = y.astype(o_ref.dtype)

    spec = pl.BlockSpec((lay.PS, POOL_W), lambda s: (s, 0))
    return pl.pallas_call(
        body, name=name, out_shape=_sds((lay.T, POOL_W), BF16), grid=(2,),
        in_specs=[spec, pl.BlockSpec(w_pool.shape, lambda s: (0, 0, 0)), pl.BlockSpec((1, POOL_W), lambda s: (0, 0))],
        out_specs=spec, compiler_params=_cp(("parallel",), VMEM_BIG))(u, w_pool, scale)


def pool_bwd(lay, u, dcat, w_pool, scale, name):
    segs = [(0, lay.C), (lay.C, lay.L)]

    def body(u_ref, dp_ref, w_ref, s_ref, du_ref, dw_ref, dsc_ref):
        for g, wd in enumerate(POOL_WINDOWS):
            sl = slice(g * 128, (g + 1) * 128)
            wb = w_ref[g].astype(BF16)
            dw = jnp.zeros((128, 128), F32)
            dsc = jnp.zeros((1, 128), F32)
            for r0, n in segs:
                x = u_ref[r0:r0 + n, sl]
                cnt = _wincount(n, wd // 2)
                d = (_winsum(x, wd // 2) / cnt - x).astype(BF16)
                dp = dp_ref[r0:r0 + n, sl]
                dsc += jnp.sum(_nn(d, wb) * dp, axis=0, keepdims=True)
                dyp = (dp * s_ref[:, sl]).astype(BF16)
                dw += _tn(d, dyp)
                dd = _nt(dyp, wb)
                du_ref[r0:r0 + n, sl] = _winsum(dd / cnt, wd // 2) - dd
            dw_ref[0, g] = dw
            dsc_ref[0, :, sl] = dsc

    spec = pl.BlockSpec((lay.PS, POOL_W), lambda s: (s, 0))
    return pl.pallas_call(
        body, name=name,
        out_shape=[_sds((lay.T, POOL_W), F32), _sds((2, 4, 128, 128), F32), _sds((2, 1, POOL_W), F32)], grid=(2,),
        in_specs=[spec, pl.BlockSpec((lay.PS, POOL_W), lambda s: (s, 1)), pl.BlockSpec(w_pool.shape, lambda s: (0, 0, 0)),
                  pl.BlockSpec((1, POOL_W), lambda s: (0, 0))],
        out_specs=[spec, pl.BlockSpec((1, 4, 128, 128), lambda s: (s, 0, 0, 0)), pl.BlockSpec((1, 1, POOL_W), lambda s: (s, 0, 0))],
        compiler_params=_cp(("parallel",), VMEM_BIG))(u, dcat, w_pool, scale)


CONV_OFFS = (-1, 0, 1, 2)
CW = 256


def _shift_rows(x, o):
    if o == 0:
        return x
    n = x.shape[0]
    t = lax.broadcasted_iota(jnp.int32, x.shape, 0)
    if o < 0:
        return jnp.where(t >= -o, pltpu.roll(x, -o, 0), 0.0)
    return jnp.where(t < n - o, pltpu.roll(x, n - o, 0), 0.0)


def conv_fwd(lay, p, col0, w, b, name):
    segs = [(0, lay.C), (lay.C, lay.L)]
    cb0 = col0 // CW

    def body(x_ref, w_ref, b_ref, o_ref):
        for r0, n in segs:
            x = x_ref[r0:r0 + n, :]
            y = jnp.broadcast_to(b_ref[...], x.shape)
            for k, o in enumerate(CONV_OFFS):
                y = y + _shift_rows(x, o) * w_ref[k:k + 1, :]
            o_ref[r0:r0 + n, :] = y

    return pl.pallas_call(
        body, name=name, out_shape=_sds((lay.T, D), F32), grid=(2, D // CW),
        in_specs=[pl.BlockSpec((lay.PS, CW), lambda s, j: (s, cb0 + j)), pl.BlockSpec((4, CW), lambda s, j: (0, j)),
                  pl.BlockSpec((1, CW), lambda s, j: (0, j))],
        out_specs=pl.BlockSpec((lay.PS, CW), lambda s, j: (s, j)),
        compiler_params=_cp(("parallel", "parallel")))(p, w, b)


def conv_bwd(lay, p, col0, w, duc, name):
    segs = [(0, lay.C), (lay.C, lay.L)]
    cb0 = col0 // CW

    def body(x_ref, w_ref, g_ref, du_ref, dw_ref, db_ref):
        dws = [jnp.zeros((1, CW), F32)] * 4
        db = jnp.zeros((1, CW), F32)
        for r0, n in segs:
            x = x_ref[r0:r0 + n, :]
            g = g_ref[r0:r0 + n, :]
            du = jnp.zeros_like(g)
            for k, o in enumerate(CONV_OFFS):
                du = du + _shift_rows(g, -o) * w_ref[k:k + 1, :]
                dws[k] = dws[k] + jnp.sum(g * _shift_rows(x, o), axis=0, keepdims=True)
            db = db + jnp.sum(g, axis=0, keepdims=True)
            du_ref[r0:r0 + n, :] = du
        dw_ref[0] = jnp.concatenate(dws, axis=0)
        db_ref[0] = db

    return pl.pallas_call(
        body, name=name, out_shape=[_sds((lay.T, D), F32), _sds((2, 4, D), F32), _sds((2, 1, D), F32)], grid=(2, D // CW),
        in_specs=[pl.BlockSpec((lay.PS, CW), lambda s, j: (s, cb0 + j)), pl.BlockSpec((4, CW), lambda s, j: (0, j)),
                  pl.BlockSpec((lay.PS, CW), lambda s, j: (s, j))],
        out_specs=[pl.BlockSpec((lay.PS, CW), lambda s, j: (s, j)), pl.BlockSpec((1, 4, CW), lambda s, j: (s, 0, j)),
                   pl.BlockSpec((1, 1, CW), lambda s, j: (s, 0, j))],
        compiler_params=_cp(("parallel", "parallel")))(p, w, duc)


def _softplus_neg(lam):
    z = -lam
    w = jnp.exp(-jnp.abs(z))
    log1p = jnp.where(w < 1e-2, w * (1.0 - w * (0.5 - w / 3.0)), jnp.log(1.0 + w))
    return jnp.maximum(z, 0.0) + log1p, -_sigmoid(z)


def _neg_expm1(x):
    series = -x * (1.0 + x * (0.5 + x * (1.0 / 6.0 + x * (1.0 / 24.0 + x * (1.0 / 120.0)))))
    return jnp.where(x > -0.05, series, 1.0 - jnp.exp(x))


def _lru_gates(x, xb, wa, wx, ba, bx, lam):
    r = _sigmoid(_nn(xb, wa.astype(BF16)) + ba)
    gi = _sigmoid(_nn(xb, wx.astype(BF16)) + bx)
    sp, dsp = _softplus_neg(lam)
    la = -LRU_C * r * sp
    a = jnp.exp(la)
    sq = jnp.sqrt(_neg_expm1(2.0 * la))
    return r, gi, sp, dsp, a, sq


def lru_coeffs(lay, uc, wa, wx, vec, name):
    tr = lay.tr

    def body(x_ref, wa_ref, wx_ref, v_ref, a_ref, b_ref):
        for h in range(8):
            sl = slice(h * 128, (h + 1) * 128)
            x = x_ref[:, sl]
            xb = x.astype(BF16)
            for d in range(2):
                _, gi, _, _, a, sq = _lru_gates(x, xb, wa_ref[d, h], wx_ref[d, h], v_ref[d:d + 1, sl],
                                                v_ref[2 + d:3 + d, sl], v_ref[4 + d:5 + d, sl])
                a_ref[d, h] = a
                b_ref[d, h] = sq * (gi * x)

    wspec = pl.BlockSpec((2, 8, 128, 128), lambda i: (0, 0, 0, 0))
    ospec = pl.BlockSpec((2, 8, tr, 128), lambda i: (0, 0, i, 0))
    return pl.pallas_call(
        body, name=name, out_shape=[_sds((2, 8, lay.T, 128), F32)] * 2, grid=(lay.nblk,),
        in_specs=[pl.BlockSpec((tr, D), lambda i: (i, 0)), wspec, wspec, pl.BlockSpec((6, D), lambda i: (0, 0))],
        out_specs=[ospec, ospec], compiler_params=_cp(("parallel",)))(uc, wa, wx, vec)


def lru_coeffs_bwd(lay, uc, wa, wx, vec, da, db, name):
    tr = lay.tr

    def body(x_ref, wa_ref, wx_ref, v_ref, da_ref, db_ref, dx_ref, dwa_ref, dwx_ref, dv_ref):
        @pl.when(pl.program_id(0) == 0)
        def _():
            dwa_ref[...] = jnp.zeros_like(dwa_ref)
            dwx_ref[...] = jnp.zeros_like(dwx_ref)
            dv_ref[...] = jnp.zeros_like(dv_ref)

        for h in range(8):
            sl = slice(h * 128, (h + 1) * 128)
            x = x_ref[:, sl]
            xb = x.astype(BF16)
            dx = jnp.zeros_like(x)
            for d in range(2):
                wab, wxb = wa_ref[d, h].astype(BF16), wx_ref[d, h].astype(BF16)
                r, gi, sp, dsp, a, sq = _lru_gates(x, xb, wa_ref[d, h], wx_ref[d, h], v_ref[d:d + 1, sl],
                                                   v_ref[2 + d:3 + d, sl], v_ref[4 + d:5 + d, sl])
                dbv, dav = db_ref[d, h], da_ref[d, h]
                t1 = dbv * sq
                dgi = t1 * x
                dx = dx + t1 * gi
                dla = dav * a - (dbv * gi * x) * (a * a) / sq
                dr = dla * (-LRU_C * sp)
                dlam = jnp.sum(dla * (-LRU_C * r), axis=0, keepdims=True) * dsp
                dpa = dr * r * (1.0 - r)
                dpx = dgi * gi * (1.0 - gi)
                dpab, dpxb = dpa.astype(BF16), dpx.astype(BF16)
                dwa_ref[d, h] += _tn(xb, dpab)
                dwx_ref[d, h] += _tn(xb, dpxb)
                dx = dx + _nt(dpab, wab) + _nt(dpxb, wxb)
                dv_ref[d:d + 1, sl] += jnp.sum(dpa, axis=0, keepdims=True)
                dv_ref[2 + d:3 + d, sl] += jnp.sum(dpx, axis=0, keepdims=True)
                dv_ref[4 + d:5 + d, sl] += dlam
            dx_ref[:, sl] = dx

    wspec = pl.BlockSpec((2, 8, 128, 128), lambda i: (0, 0, 0, 0))
    gspec = pl.BlockSpec((2, 8, tr, 128), lambda i: (0, 0, i, 0))
    vspec = pl.BlockSpec((6, D), lambda i: (0, 0))
    xspec = pl.BlockSpec((tr, D), lambda i: (i, 0))
    return pl.pallas_call(
        body, name=name,
        out_shape=[_sds((lay.T, D), F32), _sds((2, 8, 128, 128), F32), _sds((2, 8, 128, 128), F32), _sds((6, D), F32)],
        grid=(lay.nblk,), in_specs=[xspec, wspec, wspec, vspec, gspec, gspec],
        out_specs=[xspec, wspec, wspec, vspec], compiler_params=_cp(("arbitrary",)))(uc, wa, wx, vec, da, db)


GB = 2
SCAN_UNROLL = 4


def _tile_scan(a, b, up):
    t = lax.broadcasted_iota(jnp.int32, a.shape, 0)
    for d in (1, 2, 4):
        sh = 8 - d if up else d
        m = (t < 8 - d) if up else (t >= d)
        a_prev, b_prev = pltpu.roll(a, sh, 0), pltpu.roll(b, sh, 0)
        b = jnp.where(m, a * b_prev + b, b)
        a = jnp.where(m, a * a_prev, a)
    return a, b


def lru_scan(lay, a, b, name):
    segs = [(0, lay.C), (lay.C, lay.L)]

    def body(a_ref, b_ref, s_ref):
        for d in range(2):
            rev = d == 1
            state = tuple(jnp.zeros((1, 128), F32) for _ in range(GB))
            for base, n in segs:
                nt = n // 8

                def step(j, c, base=base, nt=nt, rev=rev, d=d):
                    c = list(c)
                    for u in range(SCAN_UNROLL):
                        jj = j * SCAN_UNROLL + u
                        r0 = pl.multiple_of(base + 8 * ((nt - 1 - jj) if rev else jj), 8)
                        for g in range(GB):
                            at, bt = _tile_scan(a_ref[d, g, pl.ds(r0, 8), :], b_ref[d, g, pl.ds(r0, 8), :], rev)
                            h = at * c[g] + bt
                            s_ref[d, g, pl.ds(r0, 8), :] = h
                            c[g] = h[0:1] if rev else h[7:8]
                    return tuple(c)

                state = lax.fori_loop(0, nt // SCAN_UNROLL, step, state)

    spec = pl.BlockSpec((2, GB, lay.PS, 128), lambda s, hb: (0, hb, s, 0))
    return pl.pallas_call(
        body, name=name, out_shape=_sds((2, 8, lay.T, 128), F32), grid=(2, 8 // GB),
        in_specs=[spec, spec], out_specs=spec, compiler_params=_cp(("parallel", "parallel"), VMEM_BIG))(a, b)


def lru_scan_bwd(lay, a, s, dy, name):
    segs = [(0, lay.C), (lay.C, lay.L)]
    C, PS = lay.C, lay.PS

    def body(a_ref, s_ref, g_ref, da_ref, db_ref):
        t = lax.broadcasted_iota(jnp.int32, (8, 128), 0)
        for d in range(2):
            rev = d == 1
            carry = tuple(jnp.zeros((1, 128), F32) for _ in range(GB))
            for si in (1, 0):
                base, n = segs[si]
                nt = n // 8

                def step(j, c, base=base, nt=nt, rev=rev, d=d):
                    c = list(c)
                    for u in range(SCAN_UNROLL):
                        jj = j * SCAN_UNROLL + u
                        r0 = pl.multiple_of(base + 8 * (jj if rev else (nt - 1 - jj)), 8)
                        if rev:
                            rn = pl.multiple_of(jnp.where(r0 == PS - 8, 0, r0 + 8), 8)
                            nb_zero = r0 == C - 8
                        else:
                            rn = pl.multiple_of(jnp.maximum(r0 - 8, 0), 8)
                            nb_zero = r0 == 0
                        for g in range(GB):
                            av = a_ref[d, g, pl.ds(r0, 8), :]
                            gv = g_ref[g, pl.ds(r0, 8), :]
                            sv = s_ref[d, g, pl.ds(r0, 8), :]
                            nbt = s_ref[d, g, pl.ds(rn, 8), :]
                            at, bt = _tile_scan(av, av * gv, not rev)
                            m = at * c[g] + bt
                            if rev:
                                m_next = jnp.where(t >= 1, pltpu.roll(m, 1, 0), c[g])
                                nb = jnp.where(nb_zero, 0.0, nbt[0:1])
                                h_prev = jnp.where(t < 7, pltpu.roll(sv, 7, 0), nb)
                                c[g] = m[7:8]
                            else:
                                m_next = jnp.where(t < 7, pltpu.roll(m, 7, 0), c[g])
                                nb = jnp.where(nb_zero, 0.0, nbt[7:8])
                                h_prev = jnp.where(t >= 1, pltpu.roll(sv, 1, 0), nb)
                                c[g] = m[0:1]
                            lam = gv + m_next
                            db_ref[d, g, pl.ds(r0, 8), :] = lam
                            da_ref[d, g, pl.ds(r0, 8), :] = lam * h_prev
                    return tuple(c)

                carry = lax.fori_loop(0, nt // SCAN_UNROLL, step, carry)

    spec = pl.BlockSpec((2, GB, lay.PS, 128), lambda s, hb: (0, hb, s, 0))
    return pl.pallas_call(
        body, name=name, out_shape=[_sds((2, 8, lay.T, 128), F32)] * 2, grid=(2, 8 // GB),
        in_specs=[spec, spec, pl.BlockSpec((GB, lay.PS, 128), lambda s, hb: (hb, s, 0))],
        out_specs=[spec, spec], compiler_params=_cp(("parallel", "parallel"), VMEM_BIG))(a, s, dy)


def _gelu(x):
    k = math.sqrt(2.0 / math.pi)
    t = jnp.tanh(k * (x + 0.044715 * x * x * x))
    return 0.5 * x * (1.0 + t), 0.5 * (1.0 + t) + 0.5 * x * (1.0 - t * t) * k * (1.0 + 3 * 0.044715 * x * x)


def lru_gate(lay, p, s, name):
    tr = lay.tr

    def body(g_ref, s_ref, o_ref):
        for h in range(8):
            sl = slice(h * 128, (h + 1) * 128)
            o_ref[:, sl] = (_gelu(g_ref[:, sl])[0] * (s_ref[0, h] + s_ref[1, h])).astype(o_ref.dtype)

    return pl.pallas_call(
        body, name=name, out_shape=_sds((lay.T, D), BF16), grid=(lay.nblk,),
        in_specs=[pl.BlockSpec((tr, D), lambda i: (i, 0)), pl.BlockSpec((2, 8, tr, 128), lambda i: (0, 0, i, 0))],
        out_specs=pl.BlockSpec((tr, D), lambda i: (i, 0)), compiler_params=_cp(("parallel",)))(p, s)


def lru_gate_bwd(lay, p, s, do, name):
    tr = lay.tr

    def body(g_ref, s_ref, do_ref, dg_ref, dy_ref):
        for h in range(8):
            sl = slice(h * 128, (h + 1) * 128)
            ge, dge = _gelu(g_ref[:, sl])
            dov = do_ref[:, sl]
            dg_ref[:, sl] = dov * (s_ref[0, h] + s_ref[1, h]) * dge
            dy_ref[h] = dov * ge

    xspec = pl.BlockSpec((tr, D), lambda i: (i, 0))
    return pl.pallas_call(
        body, name=name, out_shape=[_sds((lay.T, D), F32), _sds((8, lay.T, 128), F32)], grid=(lay.nblk,),
        in_specs=[xspec, pl.BlockSpec((2, 8, tr, 128), lambda i: (0, 0, i, 0)), xspec],
        out_specs=[xspec, pl.BlockSpec((8, tr, 128), lambda i: (0, i, 0))],
        compiler_params=_cp(("parallel",)))(p, s, do)


def silu_rows(x, name):
    def body(x_ref, o_ref):
        v = x_ref[...]
        o_ref[...] = (v * _sigmoid(v)).astype(o_ref.dtype)
    return pl.pallas_call(body, name=name, out_shape=_sds(x.shape, BF16), in_specs=[VMEM_SPEC], out_specs=VMEM_SPEC)(x)


def mod_grad_rows(gath, name):
    w = gath.shape[-1]

    def body(g_ref, dm_ref, db_ref):
        dm_ref[...] = jnp.zeros_like(dm_ref)
        for l in range(2):
            ctx = g_ref[0, 3 * l + 2:3 * l + 3, :]
            tot = g_ref[0, 3 * l:3 * l + 1, :] + g_ref[0, 3 * l + 1:3 * l + 2, :]
            for k in range(8):
                dm_ref[l, 2 * k:2 * k + 2, :] = g_ref[k, 3 * l:3 * l + 2, :]
                if k:
                    ctx = ctx + g_ref[k, 3 * l + 2:3 * l + 3, :]
                    tot = tot + (g_ref[k, 3 * l:3 * l + 1, :] + g_ref[k, 3 * l + 1:3 * l + 2, :])
            dm_ref[l, 16:17, :] = ctx
            db_ref[l:l + 1, :] = tot + ctx

    return pl.pallas_call(body, name=name, out_shape=[_sds((2, 32, w), F32), _sds((2, w), F32)],
                          in_specs=[VMEM_SPEC], out_specs=[VMEM_SPEC, VMEM_SPEC])(gath)


def cctx_grad(p, c_ctx, name):
    def body(a_ref, c_ref, o_ref):
        cv = c_ref[...]
        sg = _sigmoid(cv)
        o_ref[...] = 0.5 * (a_ref[0, 0:1, :] + a_ref[1, 0:1, :]) * (sg * (1.0 + cv * (1.0 - sg)))
    return pl.pallas_call(body, name=name, out_shape=_sds((1, D), F32), in_specs=[VMEM_SPEC] * 2,
                          out_specs=VMEM_SPEC)(p, c_ctx)


def loss_and_grad(lay, h, tgt, name):
    def fn(hb, tb):
        lat = (pl.program_id(0) % lay.bps) >= lay.cb
        e = jnp.where(lat, hb - tb, 0.0)
        return e * (1.0 / D), jnp.sum(e * e, axis=0, keepdims=True) * (0.5 / D)
    return rowwise(lay, name, fn, [h, tgt], outs=[(D, F32)], sums=[(1, D)])


def adamw(w, g, m, v, name):
    shape = w.shape
    w2, g2, m2, v2 = (t.reshape(-1, shape[-1]) for t in (w, g, m, v))
    rows, width = w2.shape
    tr = 256 if rows % 256 == 0 else rows
    c1 = 1.0 - ADAM_B1 ** ADAM_STEP
    c2 = 1.0 - ADAM_B2 ** ADAM_STEP

    def body(w_ref, g_ref, m_ref, v_ref, d_ref, mo_ref, vo_ref):
        gv = g_ref[...]
        mn = ADAM_B1 * m_ref[...] + (1.0 - ADAM_B1) * gv
        vn = ADAM_B2 * v_ref[...] + (1.0 - ADAM_B2) * (gv * gv)
        d_ref[...] = -ADAM_LR * ((mn / c1) / (jnp.sqrt(vn / c2) + ADAM_EPS) + ADAM_WD * w_ref[...])
        mo_ref[...] = mn
        vo_ref[...] = vn

    spec = pl.BlockSpec((tr, width), lambda i: (i, 0))
    d, mn, vn = pl.pallas_call(body, name=name, out_shape=[_sds((rows, width), F32)] * 3, grid=(rows // tr,),
                               in_specs=[spec] * 4, out_specs=[spec] * 3, compiler_params=_cp(("parallel",)))(w2, g2, m2, v2)
    return d.reshape(shape), mn.reshape(shape), vn.reshape(shape)


def adamw_ffn(w, m, v, red, kind, ns, name):
    shape = w.shape
    w2, m2, v2 = (t.reshape(-1, shape[-1]) for t in (w, m, v))
    rows, width = w2.shape
    c1 = 1.0 - ADAM_B1 ** ADAM_STEP
    c2 = 1.0 - ADAM_B2 ** ADAM_STEP
    if kind < 2:
        tr, nb = 256, D // 256
        gspec = pl.BlockSpec((ns, tr), lambda i: ((i // nb) * 3 + kind, i % nb))
    else:
        tr, nb = ns // 2, 2
        gspec = pl.BlockSpec((tr, D), lambda i: (((i // nb) * 3 + kind) * nb + i % nb, 0))

    def body(w_ref, g_ref, m_ref, v_ref, go_ref, d_ref, mo_ref, vo_ref):
        gv = g_ref[...].T if kind < 2 else g_ref[...]
        mn = ADAM_B1 * m_ref[...] + (1.0 - ADAM_B1) * gv
        vn = ADAM_B2 * v_ref[...] + (1.0 - ADAM_B2) * (gv * gv)
        go_ref[...] = gv
        d_ref[...] = -ADAM_LR * ((mn / c1) / (jnp.sqrt(vn / c2) + ADAM_EPS) + ADAM_WD * w_ref[...])
        mo_ref[...] = mn
        vo_ref[...] = vn

    spec = pl.BlockSpec((tr, width), lambda i: (i, 0))
    outs = pl.pallas_call(body, name=name, out_shape=[_sds((rows, width), F32)] * 4, grid=(rows // tr,),
                          in_specs=[spec, gspec, spec, spec], out_specs=[spec] * 4,
                          compiler_params=_cp(("parallel",)))(w2, red, m2, v2)
    return tuple(t.reshape(shape) for t in outs)


def mod_mm(sc, w_mod, bias, name):
    wm = w_mod.shape[-1]
    tn = _pick(wm, (768, 512, 384, 256, 128))

    def body(a_ref, b_ref, c_ref, o_ref):
        o_ref[...] = _nn(a_ref[...], b_ref[...].astype(BF16)) + c_ref[...]

    return pl.pallas_call(
        body, name=name, out_shape=_sds((DEPTH, 32, wm), F32), grid=(DEPTH, wm // tn),
        in_specs=[pl.BlockSpec((32, D), lambda l, j: (0, 0)), pl.BlockSpec((None, D, tn), lambda l, j: (l, 0, j)),
                  pl.BlockSpec((None, 1, tn), lambda l, j: (l, 0, j))],
        out_specs=pl.BlockSpec((None, 32, tn), lambda l, j: (l, 0, j)),
        compiler_params=_cp(("parallel", "parallel")))(sc, w_mod, bias)


def wmod_dw(sc, dcol, name):
    wm = dcol.shape[-1]
    tm = 256

    def body(a_ref, b_ref, o_ref):
        o_ref[...] = _tn(a_ref[...], b_ref[...].astype(BF16))

    return pl.pallas_call(
        body, name=name, out_shape=_sds((DEPTH, D, wm), F32), grid=(DEPTH, D // tm),
        in_specs=[pl.BlockSpec((32, tm), lambda l, i: (0, i)), pl.BlockSpec((None, 32, wm), lambda l, i: (l, 0, 0))],
        out_specs=pl.BlockSpec((None, tm, wm), lambda l, i: (l, i, 0)),
        compiler_params=_cp(("parallel", "parallel")))(sc, dcol)


def cctx_dx(drow, w_mod, name):
    wm = w_mod.shape[-1]

    def body(a_ref, b_ref, o_ref):
        o_ref[...] = _nt(a_ref[...].astype(BF16), b_ref[...].astype(BF16))

    return pl.pallas_call(
        body, name=name, out_shape=_sds((DEPTH, 16, D), F32), grid=(DEPTH,),
        in_specs=[pl.BlockSpec((None, 16, wm), lambda l: (l, 0, 0)), pl.BlockSpec((None, D, wm), lambda l: (l, 0, 0))],
        out_specs=pl.BlockSpec((None, 16, D), lambda l: (l, 0, 0)), compiler_params=_cp(("parallel",), VMEM_BIG))(drow, w_mod)


HEAD_PERM = (0, 4, 1, 5, 2, 6, 3, 7)


def _rot_rows(wt):
    return jnp.concatenate([-wt[32:64], wt[0:32]], axis=0)


def _unrot_rows(g):
    return jnp.concatenate([g[32:64], -g[0:32]], axis=0)


def _heads(a, n):
    return [a[64 * i:64 * (i + 1)] for i in range(n)]


def kernel(x, c, ctx, c_ctx, w_mod, b_mod, ln_g, ln_b, ffn_w_gate, ffn_w_up, ffn_w_down, mix_ab_w_in, attn_sink, pool_w, pool_scale, mix_ab_w_out, lru_w_in, lru_conv_w, lru_conv_b, lru_wa, lru_ba, lru_wx, lru_bx, lru_lambda, lru_w_out, loss_target, m_c_ctx, m_w_mod, m_b_mod, m_ln_g, m_ln_b, m_ffn_w_gate, m_ffn_w_up, m_ffn_w_down, m_mix_ab_w_in, m_attn_sink, m_pool_w, m_pool_scale, m_mix_ab_w_out, m_lru_w_in, m_lru_conv_w, m_lru_conv_b, m_lru_wa, m_lru_ba, m_lru_wx, m_lru_bx, m_lru_lambda, m_lru_w_out, v_c_ctx, v_w_mod, v_b_mod, v_ln_g, v_ln_b, v_ffn_w_gate, v_ffn_w_up, v_ffn_w_down, v_mix_ab_w_in, v_attn_sink, v_pool_w, v_pool_scale, v_mix_ab_w_out, v_lru_w_in, v_lru_conv_w, v_lru_conv_b, v_lru_wa, v_lru_ba, v_lru_wx, v_lru_bx, v_lru_lambda, v_lru_w_out):
    n_lat, n_ctx = x.shape[1], ctx.shape[1]
    lay = Layout(n_ctx, n_lat)
    T = lay.T
    ns = ffn_w_gate.shape[-1]
    n_li, n_ai = lru_w_in.shape[-1], mix_ab_w_in.shape[-1]
    n_ao, n_lo = mix_ab_w_out.shape[1], lru_w_out.shape[1]
    wm = w_mod.shape[-1]
    dsh = ln_g.shape[-1]
    mx, my, mc = lax.axis_index("x"), lax.axis_index("y"), lax.axis_index("c")
    chip = 2 * mx + my
    me = 2 * chip + mc

    c_all = all_gather8(c, "ag8_c").reshape(16, D)
    cc = jnp.concatenate([c_all, c_ctx[None, :], jnp.zeros((15, D), F32)], axis=0)
    sc = silu_rows(cc, "silu_c")
    bias = lax.dynamic_slice(b_mod, (0, chip * wm), (DEPTH, wm)).reshape(DEPTH, 1, wm)
    modg = all_gather_chips(mod_mm(sc, w_mod, bias, "mod_mm"), "ag_mod")
    modtab = []
    for l in range(DEPTH):
        full = jnp.transpose(modg[:, l], (1, 0, 2)).reshape(32, N_CHIP * wm)
        mine = lax.dynamic_slice(full, (2 * me, 0), (2, N_CHIP * wm))
        modtab.append(jnp.concatenate([mine, full[16:17]], axis=0).reshape(3, N_MOD, D))

    small = jnp.concatenate([ln_g.reshape(6, dsh), ln_b.reshape(6, dsh), lru_conv_w[0], lru_conv_b, lru_ba[0],
                             lru_bx[0], lru_lambda[0], jnp.zeros((9, dsh), F32)], axis=0)
    small = all_gather_chips(small.reshape(2, 16, dsh), "ag_small").reshape(N_CHIP, 32, dsh)
    small = jnp.transpose(small, (1, 0, 2)).reshape(32, D)
    ln_g_f, ln_b_f = small[0:6].reshape(2, 3, D), small[6:12].reshape(2, 3, D)
    conv_w_f, conv_b_f = small[12:16], small[16:17]
    lru_vec = small[17:23]

    ffn_sh = jnp.stack([jnp.swapaxes(ffn_w_gate, -1, -2), jnp.swapaxes(ffn_w_up, -1, -2), ffn_w_down], axis=2)
    ffn_sh = ffn_sh.astype(BF16)
    hh = 3 * ns // 2
    wb = [all_gather_chips(ffn_sh[0, 0].reshape(2, hh, D), "ag_ffn0"), None, None, None]
    placed = [None] + [place_slab(ffn_sh[g // 2, g % 2].reshape(2, hh, D), f"ag_ffn{g}_place") for g in (1, 2, 3)]
    mix_sh = jnp.concatenate([lru_w_in[0].T, mix_ab_w_in[0].T, mix_ab_w_out[0], lru_w_out[0]], axis=0).astype(BF16)
    n_mix = n_li + n_ai + n_ao + n_lo
    mixw = all_gather_chips(mix_sh.reshape(2, n_mix // 2, D), "ag_mix").reshape(N_CHIP, n_mix, D)
    o1, o2, o3 = n_li, n_li + n_ai, n_li + n_ai + n_ao
    lru_in_t = mixw[:, 0:o1].reshape(N_CHIP * n_li, D)
    ab_in_t = mixw[:, o1:o2].reshape(N_CHIP * n_ai, D)
    ab_out = mixw[:, o2:o3].reshape(N_CHIP * n_ao, D)
    lru_out = mixw[:, o3:].reshape(N_CHIP * n_lo, D)
    qh, kh = _heads(ab_in_t[Q0:K0], N_HEADS), _heads(ab_in_t[K0:V0], N_KV)
    w_ext_t = jnp.concatenate([qh[h] for h in HEAD_PERM] + [ab_in_t[K0:QR0]]
                              + [_rot_rows(qh[h]) for h in HEAD_PERM] + [_rot_rows(t) for t in kh], axis=0)
    oh = _heads(ab_out[0:ATT_W], N_HEADS)
    w_out_ext = jnp.concatenate([oh[h] for h in HEAD_PERM] + [ab_out[ATT_W:]], axis=0)

    t = jnp.arange(n_lat)
    inv = ROPE_THETA ** (-jnp.arange(16, dtype=F32) / 16.0)
    ang = jnp.concatenate([(t // GRID_W).astype(F32)[:, None] * inv, (t % GRID_W).astype(F32)[:, None] * inv], axis=-1)
    cos1 = jnp.concatenate([jnp.ones((n_ctx, 32), F32), jnp.cos(ang)], axis=0)
    sin1 = jnp.concatenate([jnp.zeros((n_ctx, 32), F32), jnp.sin(ang)], axis=0)
    cos_t = jnp.tile(cos1, (2, 4))
    sin_t = jnp.tile(sin1, (2, 4))
    sk = attn_sink[0]
    sink_tab = jnp.concatenate([jnp.repeat(jnp.stack([sk[:4], sk[4:]], axis=1), HEAD_DIM, axis=1),
                                jnp.zeros((4, 128), F32)], axis=0)
    pscale = pool_scale.reshape(1, POOL_W)

    h0 = jnp.concatenate([ctx, x], axis=1).reshape(T, D)
    tgt = loss_target.reshape(2 * n_lat, D)

    def lnv(l, j):
        return jnp.stack([ln_g_f[l, j], ln_b_f[l, j]])

    def fq(l, f, kind):
        return (l * 2 + f) * 3 + kind

    subs = [(0, 0, 0.5, 0), (0, 3, 1.0, 1), (0, 6, 0.5, 2), (1, 0, 0.5, 0), (1, 3, 1.0, 1), (1, 6, 0.5, 2)]

    def ffn_core(hm, l, f):
        tag = f"l{l}f{f}"
        gi = 2 * l + f
        w = wb[gi].reshape(N_CHIP, 3 * ns, D)
        if gi == 3:
            g, u, a = ffn_up(lay, hm, w, 0, 1, ns, f"ffn_up_{tag}")
            (y,) = slab_nn_acc(lay, [a], w, [2], ns, f"ffn_down_{tag}")
        else:
            g, u, a, nbuf = ffn_up(lay, hm, w, 0, 1, ns, f"ffn_up_{tag}", rider=rider_gather_ici(placed[gi + 1]))
            y, wb[gi + 1] = slab_nn_acc(lay, [a], w, [2], ns, f"ffn_down_{tag}", rider=rider_gather_d2d(nbuf))
        return y, dict(g=g, u=u, a=a)

    def mixa_core(hm):
        p = mm_nt(hm, w_ext_t, "mixa_in")
        qr, kr, vb, u = rope_fwd(lay, p, cos_t, sin_t, "rope")
        att, lse = attn_fwd(lay, qr, kr, vb, sink_tab, "attn")
        pool = pool_fwd(lay, u, pool_w[0], pscale, "pool")
        cat = jnp.concatenate([att, pool], axis=1)
        return mm_nn(cat, w_out_ext, "mixa_out"), dict(qr=qr, kr=kr, vb=vb, u=u, lse=lse, cat=cat)

    def mixc_core(hm):
        p = mm_nt(hm, lru_in_t, "mixc_in")
        uc = conv_fwd(lay, p, D, conv_w_f, conv_b_f, "conv")
        a, b = lru_coeffs(lay, uc, lru_wa[0], lru_wx[0], lru_vec, "lru_coef")
        s = lru_scan(lay, a, b, "lru_scan")
        o = lru_gate(lay, p, s, "lru_gate")
        return mm_nn(o, lru_out, "mixc_out"), dict(p=p, uc=uc, a=a, s=s, o=o)

    recs = []
    h = h0
    hm = modulate(lay, h0, modtab[0], 0, 1, "mod_first")
    for k, (l, k0, coef, j) in enumerate(subs):
        if k0 == 3:
            y, core = mixa_core(hm) if l == 0 else mixc_core(hm)
        else:
            y, core = ffn_core(hm, l, k0 // 6)
        nxt = None if k == 5 else (modtab[subs[k + 1][0]], subs[k + 1][1], subs[k + 1][1] + 1)
        res = resid_ln(lay, h, y, modtab[l], k0 + 2, coef, lnv(l, j), f"ln_s{k}", nxt=nxt)
        recs.append(dict(h=h, hm=hm, y=y, xhat=res[1], rstd=res[2], **core))
        h = res[0]
        hm = res[3] if nxt is not None else None

    dout, lparts = loss_and_grad(lay, h, tgt, "loss")
    loss = lax.psum(jnp.sum(lparts), ("x", "y", "c"))

    dln = {}
    dms = {}
    mixg = {}
    ffn_red = [None] * 4
    pending = []

    def ffn_core_bwd(dy, r, l, f):
        tag = f"l{l}f{f}"
        gi = 2 * l + f
        w = wb[gi].reshape(N_CHIP, 3 * ns, D)
        prev = pending.pop() if pending else None
        if prev is None:
            dg, du = ffn_bwd_da(lay, dy, w, 2, r["g"], r["u"], ns, f"ffn_da_{tag}")
        else:
            dg, du, recv = ffn_bwd_da(lay, dy, w, 2, r["g"], r["u"], ns, f"ffn_da_{tag}", rider=rider_reduce_sib(prev[1]))
        gb = lax.empty((N_CHIP, 3 * ns, D), F32)
        gb = slab_tn(lay, r["a"], dy, gb, 2, ns, f"ffn_dwd_{tag}")
        gb = slab_tn(lay, dg, r["hm"], gb, 0, ns, f"ffn_dwg_{tag}")
        gb = slab_tn(lay, du, r["hm"], gb, 1, ns, f"ffn_dwu_{tag}")
        if prev is None:
            (dhm,) = slab_nn_acc(lay, [dg, du], w, [0, 1], ns, f"ffn_dh_{tag}")
        else:
            q = add_own_half(prev[1], recv, BF16, f"rs_add2_ffn{prev[0]}")
            dhm, arr = slab_nn_acc(lay, [dg, du], w, [0, 1], ns, f"ffn_dh_{tag}", rider=rider_reduce_a2a(q))
            red = sibling_join_halves(sum_slots(q, arr, f"rs_add4_ffn{prev[0]}"), f"rs_join_ffn{prev[0]}")
            ffn_red[prev[0]] = red.reshape(3 * ns, D)
        pending.append((gi, gb.reshape(N_CHIP, 2, hh, D)))
        return dhm

    def mixc_core_bwd(dy, r):
        do_c = mm_nt(dy, lru_out, "mixc_out_dx")
        mixg["lru_out"] = mm_tn(r["o"], dy, "mixc_out_dw")
        dgate, dyg = lru_gate_bwd(lay, r["p"], r["s"], do_c, "lru_gate_b")
        da_c, db_c = lru_scan_bwd(lay, r["a"], r["s"], dyg, "lru_scan_b")
        duc, mixg["wa"], mixg["wx"], mixg["vec"] = lru_coeffs_bwd(lay, r["uc"], lru_wa[0], lru_wx[0], lru_vec, da_c, db_c,
                                                                  "lru_coef_b")
        du_c, mixg["cw"], mixg["cb"] = conv_bwd(lay, r["p"], D, conv_w_f, duc, "conv_b")
        dp_c = jnp.concatenate([dgate, du_c], axis=1).astype(BF16)
        mixg["lru_in_t"] = mm_tn(dp_c, r["hm"], "mixc_in_dw")
        return mm_nn(dp_c, lru_in_t, "mixc_in_dx")

    def mixa_core_bwd(dy, r):
        dcat = mm_nt(dy, w_out_ext, "mixa_out_dx")
        mixg["out_ext"] = mm_tn(r["cat"], dy, "mixa_out_dw")
        dqr, dkr, dv, mixg["sink"] = attn_bwd(lay, r["qr"], r["kr"], r["vb"], sink_tab, r["lse"], dcat, "attn_b")
        du_a, mixg["pw"], mixg["ps"] = pool_bwd(lay, r["u"], dcat, pool_w[0], pscale, "pool_b")
        dp_a = rope_bwd(lay, dqr, dkr, dv, du_a, cos_t, sin_t, "rope_b")
        mixg["ext_t"] = mm_tn(dp_a, r["hm"], "mixa_in_dw")
        return mm_nn(dp_a, w_ext_t, "mixa_in_dx")

    l, k0, coef, j = subs[5]
    dy, dres, s1 = ln_bwd(lay, dout, recs[5]["xhat"], recs[5]["rstd"], recs[5]["y"], modtab[l], k0 + 2, coef, lnv(l, j),
                          "lnb_s5")
    for k in range(5, -1, -1):
        l, k0, coef, j = subs[k]
        r = recs[k]
        if k0 == 3:
            dhm = mixa_core_bwd(dy, r) if l == 0 else mixc_core_bwd(dy, r)
        else:
            dhm = ffn_core_bwd(dy, r, l, k0 // 6)
        dln[(l, j)] = block_sums(lay, s1, f"bs_ln_s{k}")
        if k > 0:
            lp, k0p, coefp, jp = subs[k - 1]
            rp = recs[k - 1]
            dy, dres, s1, s2 = modb_lnb(lay, dres, dhm, r["h"], modtab[l], k0 + 1, rp["xhat"], rp["rstd"], rp["y"],
                                        modtab[lp], k0p + 2, coefp, lnv(lp, jp), f"modb_lnb_s{k}")
        else:
            gx, s2 = mod_bwd(lay, dres, dhm, r["h"], modtab[l], k0 + 1, "modb_s0")
        dms[(l, k0)] = block_sums(lay, s2, f"bs_mod_s{k}")
    grad_x = gx.reshape(2, n_lat, D)
    g_lru_out, g_wa, g_wx, g_vec, g_cw, g_cb = (mixg[n] for n in ("lru_out", "wa", "wx", "vec", "cw", "cb"))
    g_lru_in_t, g_out_ext, g_sink, g_pw, g_ps, g_ext_t = (mixg[n] for n in ("lru_in_t", "out_ext", "sink", "pw", "ps", "ext_t"))

    rows = []
    for l in range(DEPTH):
        per_k = []
        for k0, j in ((0, 0), (3, 1), (6, 2)):
            per_k += [dms[(l, k0)][:3, 0], dms[(l, k0)][:3, 1], dln[(l, j)][:3, 2]]
        rows.append(jnp.stack(per_k, axis=1).reshape(3, N_MOD * D))
    dmod_loc = jnp.concatenate(rows + [jnp.zeros((2, N_MOD * D), F32)], axis=0)
    dmod_all, g_b_mod = mod_grad_rows(all_gather8(dmod_loc, "ag8_dmod"), "dmod_rows")
    dcol = lax.dynamic_slice(dmod_all, (0, 0, chip * wm), (DEPTH, 32, wm))
    g_w_mod = wmod_dw(sc, dcol, "wmod_dw")
    g_cctx = cctx_grad(cctx_dx(dcol[:, 16:32], w_mod, "cctx_dx"), c_ctx[None, :], "cctx_grad")

    gq = _heads(g_ext_t[Q0:K0], N_HEADS)
    gqr = _heads(g_ext_t[QR0:KR0], N_HEADS)
    g_q = [None] * N_HEADS
    for i, h in enumerate(HEAD_PERM):
        g_q[h] = gq[i] + _unrot_rows(gqr[i])
    gk = [a + _unrot_rows(b) for a, b in zip(_heads(g_ext_t[K0:V0], N_KV), _heads(g_ext_t[KR0:PEXT], N_KV))]
    g_ab_in_t = jnp.concatenate(g_q + gk + [g_ext_t[V0:QR0]], axis=0)
    go = _heads(g_out_ext[0:ATT_W], N_HEADS)
    g_o = [None] * N_HEADS
    for i, h in enumerate(HEAD_PERM):
        g_o[h] = go[i]
    g_ab_out = jnp.concatenate(g_o + [g_out_ext[ATT_W:]], axis=0)
    mix_g = jnp.concatenate([g_lru_in_t.reshape(N_CHIP, n_li, D), g_ab_in_t.reshape(N_CHIP, n_ai, D),
                             g_ab_out.reshape(N_CHIP, n_ao, D), g_lru_out.reshape(N_CHIP, n_lo, D)], axis=1)

    g_ln_g = jnp.stack([jnp.stack([dln[(l, j)][3, 1] for j in range(3)]) for l in range(DEPTH)])
    g_ln_b = jnp.stack([jnp.stack([dln[(l, j)][3, 0] for j in range(3)]) for l in range(DEPTH)])
    sink_row = jnp.sum(g_sink, axis=0)[:4]
    g_sink8 = jnp.concatenate([sink_row[:, 0], sink_row[:, HEAD_DIM]])
    misc = jnp.concatenate([g_sink8, jnp.sum(g_ps, axis=0).reshape(POOL_W), jnp.zeros((D - 8 - POOL_W,), F32)])
    small_g = jnp.concatenate([
        g_ln_g.reshape(6, D), g_ln_b.reshape(6, D), jnp.sum(g_cw, axis=0), jnp.sum(g_cb, axis=0), g_vec,
        misc[None, :], jnp.sum(g_pw, axis=0).reshape(64, D), g_wa.reshape(256, D), g_wx.reshape(256, D), g_cctx,
        jnp.zeros((39, D), F32)], axis=0)
    n_small = small_g.shape[0] // N_CHIP
    mix_buf = jnp.concatenate([mix_g, small_g.reshape(N_CHIP, n_small, D)], axis=1)
    n_mb = n_mix + n_small

    last_g, last_buf = pending.pop()
    ffn_red[last_g] = reduce_scatter_chips(last_buf, f"ffn{last_g}", wire=BF16).reshape(3 * ns, D)
    ffn_red = jnp.concatenate(ffn_red, axis=0)
    mix_red = reduce_scatter_chips(mix_buf.reshape(N_CHIP, 2, n_mb // 2, D), "mix").reshape(n_mb, D)
    small_red = all_gather_chips(mix_red[n_mix:].reshape(2, n_small // 2, D), "ag_smallg").reshape(N_CHIP * n_small, D)

    ffn_kind = dict(ffn_w_gate=0, ffn_w_up=1, ffn_w_down=2)

    def cols(a):
        return lax.dynamic_slice_in_dim(a, chip * dsh, dsh, axis=a.ndim - 1)

    sr = small_red
    grads = dict(
        c_ctx=sr[600], w_mod=g_w_mod, b_mod=g_b_mod,
        ln_g=cols(sr[0:6]).reshape(2, 3, dsh), ln_b=cols(sr[6:12]).reshape(2, 3, dsh),
        mix_ab_w_in=mix_red[o1:o2].T[None], attn_sink=sr[23, 0:8][None], pool_w=sr[24:88].reshape(1, 4, 128, 128),
        pool_scale=sr[23, 8:8 + POOL_W][None], mix_ab_w_out=mix_red[o2:o3][None], lru_w_in=mix_red[0:o1].T[None],
        lru_conv_w=cols(sr[12:16])[None], lru_conv_b=cols(sr[16:17]), lru_wa=sr[88:344].reshape(1, 2, 8, 128, 128),
        lru_ba=cols(sr[17:19])[None], lru_wx=sr[344:600].reshape(1, 2, 8, 128, 128), lru_bx=cols(sr[19:21])[None],
        lru_lambda=cols(sr[21:23])[None], lru_w_out=mix_red[o3:n_mix][None])
    params = dict(c_ctx=(c_ctx, m_c_ctx, v_c_ctx), w_mod=(w_mod, m_w_mod, v_w_mod), b_mod=(b_mod, m_b_mod, v_b_mod),
                  ln_g=(ln_g, m_ln_g, v_ln_g), ln_b=(ln_b, m_ln_b, v_ln_b),
                  ffn_w_gate=(ffn_w_gate, m_ffn_w_gate, v_ffn_w_gate), ffn_w_up=(ffn_w_up, m_ffn_w_up, v_ffn_w_up),
                  ffn_w_down=(ffn_w_down, m_ffn_w_down, v_ffn_w_down),
                  mix_ab_w_in=(mix_ab_w_in, m_mix_ab_w_in, v_mix_ab_w_in), attn_sink=(attn_sink, m_attn_sink, v_attn_sink),
                  pool_w=(pool_w, m_pool_w, v_pool_w), pool_scale=(pool_scale, m_pool_scale, v_pool_scale),
                  mix_ab_w_out=(mix_ab_w_out, m_mix_ab_w_out, v_mix_ab_w_out), lru_w_in=(lru_w_in, m_lru_w_in, v_lru_w_in),
                  lru_conv_w=(lru_conv_w, m_lru_conv_w, v_lru_conv_w), lru_conv_b=(lru_conv_b, m_lru_conv_b, v_lru_conv_b),
                  lru_wa=(lru_wa, m_lru_wa, v_lru_wa), lru_ba=(lru_ba, m_lru_ba, v_lru_ba), lru_wx=(lru_wx, m_lru_wx, v_lru_wx),
                  lru_bx=(lru_bx, m_lru_bx, v_lru_bx), lru_lambda=(lru_lambda, m_lru_lambda, v_lru_lambda),
                  lru_w_out=(lru_w_out, m_lru_w_out, v_lru_w_out))
    gl, dl, ml, vl = [], [], [], []
    for name, (w, m, v) in params.items():
        if name in ffn_kind:
            g, d, mn, vn = adamw_ffn(w, m, v, ffn_red, ffn_kind[name], ns, f"adamw_{name}")
        else:
            g = grads[name].reshape(w.shape)
            d, mn, vn = adamw(w, g, m, v, f"adamw_{name}")
        gl.append(g)
        dl.append(d)
        ml.append(mn)
        vl.append(vn)
    return (loss, grad_x, *gl, *dl, *ml, *vl)
```

```python
import functools
import math

import jax
import jax.numpy as jnp
from jax import lax
from jax.experimental import pallas as pl
from jax.experimental.pallas import tpu as pltpu

F32, BF16 = jnp.float32, jnp.bfloat16
MESH = pl.DeviceIdType.MESH
ANY = pl.BlockSpec(memory_space=pl.ANY)
VMEM_SPEC = pl.BlockSpec(memory_space=pltpu.VMEM)

D = 1024
N_CHIP = 4
HEAD_DIM, N_HEADS, N_KV = 64, 8, 2
ATT_W, KV_W, POOL_W = 512, 128, 512
POOL_WINDOWS = (2, 4, 8, 16)
BLK = 128
ATT_SCALE = HEAD_DIM ** -0.5
ROPE_THETA = 10000.0
GRID_W = 64
LRU_C = 8.0
LN_EPS = 1e-5
NEG_INF = -1e30
DEPTH = 2
ALPHA = (2 * DEPTH) ** 0.25
N_MOD = 9
ADAM_LR, ADAM_B1, ADAM_B2, ADAM_EPS, ADAM_WD, ADAM_STEP = 0.001, 0.9, 0.999, 1e-08, 0.01, 10
VMEM_BIG = 48 * 1024 * 1024


def _cp(sem=None, vmem=None):
    kw = {}
    if sem is not None:
        kw["dimension_semantics"] = sem
    if vmem is not None:
        kw["vmem_limit_bytes"] = vmem
    return pltpu.CompilerParams(**kw)


def _sds(shape, dtype):
    return jax.ShapeDtypeStruct(tuple(shape), dtype)


def _pick(n, cands):
    for c in cands:
        if n % c == 0:
            return c
    return n


def _dot(a, b, dims):
    return lax.dot_general(a, b, (dims, ((), ())), preferred_element_type=F32)


def _nn(a, b):
    return _dot(a, b, ((1,), (0,)))


def _nt(a, b):
    return _dot(a, b, ((1,), (1,)))


def _tn(a, b):
    return _dot(a, b, ((0,), (0,)))


def _sigmoid(x):
    return 0.5 * jnp.tanh(0.5 * x) + 0.5


def _me():
    return lax.axis_index("x"), lax.axis_index("y"), lax.axis_index("c")


def _rcopy(src, dst, ssem, rsem, dev):
    return pltpu.make_async_remote_copy(src_ref=src, dst_ref=dst, send_sem=ssem, recv_sem=rsem,
                                        device_id=dev, device_id_type=MESH)


def all_gather8(x, name):
    def body(x_ref, o_ref, ssem, rsem, lsem):
        mx, my, mc = _me()
        me = 4 * mx + 2 * my + mc
        loc = pltpu.make_async_copy(x_ref, o_ref.at[me], lsem)
        loc.start()
        peers = []
        for m in range(1, 8):
            px = 1 - mx if (m >> 2) & 1 else mx
            py = 1 - my if (m >> 1) & 1 else my
            pc = 1 - mc if m & 1 else mc
            peers.append((px, py, pc))
        sends = [_rcopy(x_ref, o_ref.at[me], ssem.at[k], rsem.at[k], p) for k, p in enumerate(peers)]
        for cp in sends:
            cp.start()
        for k, (px, py, pc) in enumerate(peers):
            _rcopy(x_ref, o_ref.at[4 * px + 2 * py + pc], ssem.at[k], rsem.at[k], (px, py, pc)).wait_recv()
        for cp in sends:
            cp.wait_send()
        loc.wait()

    return pl.pallas_call(
        body, name=name, out_shape=_sds((8,) + x.shape, x.dtype),
        in_specs=[VMEM_SPEC], out_specs=VMEM_SPEC,
        scratch_shapes=[pltpu.SemaphoreType.DMA((7,)), pltpu.SemaphoreType.DMA((7,)), pltpu.SemaphoreType.DMA],
    )(x)


_ROW_BLOCKS = (512, 384, 352, 256, 224, 128)


def _idx(v):
    return jnp.reshape(v, (1,)).astype(jnp.int32)


def place_slab(shard, name):
    _, h, w = shard.shape
    th = _pick(h, _ROW_BLOCKS)

    def body(s_ref, x_ref, o_ref):
        del s_ref
        o_ref[...] = x_ref[...]

    return pl.pallas_call(
        body, name=name, out_shape=_sds((N_CHIP,) + shard.shape, shard.dtype),
        grid_spec=pltpu.PrefetchScalarGridSpec(
            num_scalar_prefetch=1, grid=(2, h // th),
            in_specs=[pl.BlockSpec((None, th, w), lambda k, r, s: (k, r, 0))],
            out_specs=pl.BlockSpec((None, None, th, w), lambda k, r, s: (s[0], k, r, 0))),
    )(_idx(2 * lax.axis_index("x") + lax.axis_index("y")), shard)


def ffn_place(w_gate, w_up, w_down, l, f, name):
    ns = w_gate.shape[-1]
    tc = 256

    def body(s_ref, g_ref, u_ref, d_ref, o_ref):
        del s_ref
        k = pl.program_id(0)

        @pl.when(k == 0)
        def _():
            o_ref[...] = g_ref[...].T.astype(BF16)

        @pl.when(k == 1)
        def _():
            o_ref[...] = u_ref[...].T.astype(BF16)

        @pl.when(k == 2)
        def _():
            o_ref[...] = d_ref[...].astype(BF16)

    nat = pl.BlockSpec((None, None, tc, ns), lambda k, j, s: (l, f, j, 0))
    out = pl.pallas_call(
        body, name=name, out_shape=_sds((N_CHIP, 3 * ns, D), BF16),
        grid_spec=pltpu.PrefetchScalarGridSpec(
            num_scalar_prefetch=1, grid=(3, D // tc),
            in_specs=[nat, nat, pl.BlockSpec((None, None, ns, tc), lambda k, j, s: (l, f, 0, j))],
            out_specs=pl.BlockSpec((None, ns, tc), lambda k, j, s: (s[0], k, j))),
    )(_idx(2 * lax.axis_index("x") + lax.axis_index("y")), w_gate, w_up, w_down)
    return out.reshape(N_CHIP, 2, 3 * ns // 2, D)


def all_gather_chips(shard, name):
    return gather_placed(place_slab(shard, name + "_place"), name)


def gather_placed(full, name):
    def body(x_ref, o_ref, ssem, rsem):
        del x_ref
        mx, my, mc = _me()
        s = 2 * mx + my
        sib = (mx, my, 1 - mc)
        chips = [(1 - mx, my), (mx, 1 - my), (1 - mx, 1 - my)]
        first = [_rcopy(o_ref.at[s, mc], o_ref.at[s, mc], ssem.at[j], rsem.at[j], (px, py, mc))
                 for j, (px, py) in enumerate(chips)]
        for cp in first:
            cp.start()
        passed = []
        for j, (px, py) in enumerate(chips):
            ps = 2 * px + py
            _rcopy(o_ref.at[ps, mc], o_ref.at[ps, mc], ssem.at[j], rsem.at[j], (px, py, mc)).wait_recv()
            fw = _rcopy(o_ref.at[ps, mc], o_ref.at[ps, mc], ssem.at[3 + j], rsem.at[3 + j], sib)
            fw.start()
            passed.append(fw)
        for j, (px, py) in enumerate(chips):
            ps = 2 * px + py
            _rcopy(o_ref.at[ps, 1 - mc], o_ref.at[ps, 1 - mc], ssem.at[3 + j], rsem.at[3 + j], sib).wait_recv()
        for cp in first + passed:
            cp.wait_send()

    return pl.pallas_call(
        body, name=name, out_shape=_sds(full.shape, full.dtype), in_specs=[ANY], out_specs=ANY,
        input_output_aliases={0: 0},
        scratch_shapes=[pltpu.SemaphoreType.DMA((6,)), pltpu.SemaphoreType.DMA((6,))],
    )(full)


def sibling_send_other_half(buf, name):
    def body(x_ref, o_ref, ssem, rsem):
        mx, my, mc = _me()
        sib = (mx, my, 1 - mc)
        cps = [_rcopy(x_ref.at[k, 1 - mc], o_ref.at[k], ssem.at[k], rsem.at[k], sib) for k in range(N_CHIP)]
        for cp in cps:
            cp.start()
        for cp in cps:
            cp.wait_recv()
        for cp in cps:
            cp.wait_send()

    n, _, h, w = buf.shape
    return pl.pallas_call(
        body, name=name, out_shape=_sds((n, h, w), buf.dtype), in_specs=[ANY], out_specs=ANY,
        scratch_shapes=[pltpu.SemaphoreType.DMA((N_CHIP,)), pltpu.SemaphoreType.DMA((N_CHIP,))],
    )(buf)


def chips_all_to_all(q, name):
    def body(x_ref, o_ref, ssem, rsem):
        mx, my, mc = _me()
        s = 2 * mx + my
        chips = [(1 - mx, my), (mx, 1 - my), (1 - mx, 1 - my)]
        cps = [_rcopy(x_ref.at[2 * px + py], o_ref.at[s], ssem.at[j], rsem.at[j], (px, py, mc))
               for j, (px, py) in enumerate(chips)]
        for cp in cps:
            cp.start()
        for j, (px, py) in enumerate(chips):
            ps = 2 * px + py
            _rcopy(x_ref.at[ps], o_ref.at[ps], ssem.at[j], rsem.at[j], (px, py, mc)).wait_recv()
        for cp in cps:
            cp.wait_send()

    return pl.pallas_call(
        body, name=name, out_shape=_sds(q.shape, q.dtype), in_specs=[ANY], out_specs=ANY,
        scratch_shapes=[pltpu.SemaphoreType.DMA((3,)), pltpu.SemaphoreType.DMA((3,))],
    )(q)


def sibling_join_halves(both, name, g=None):
    def body(x_ref, o_ref, ssem, rsem):
        del x_ref
        mx, my, mc = _me()
        sib = (mx, my, 1 - mc)
        o = o_ref if g is None else o_ref.at[g]
        cp = _rcopy(o.at[mc], o.at[mc], ssem, rsem, sib)
        cp.start()
        _rcopy(o.at[1 - mc], o.at[1 - mc], ssem, rsem, sib).wait_recv()
        cp.wait_send()

    return pl.pallas_call(
        body, name=name, out_shape=_sds(both.shape, both.dtype), in_specs=[ANY], out_specs=ANY,
        input_output_aliases={0: 0}, scratch_shapes=[pltpu.SemaphoreType.DMA, pltpu.SemaphoreType.DMA],
    )(both)


def add_own_half(buf, recv, wire, name):
    n, _, h, w = buf.shape
    th = _pick(h, _ROW_BLOCKS)

    def body(c_ref, a_ref, b_ref, o_ref):
        del c_ref
        o_ref[...] = (a_ref[...] + b_ref[...]).astype(o_ref.dtype)

    return pl.pallas_call(
        body, name=name, out_shape=_sds((n, h, w), wire),
        grid_spec=pltpu.PrefetchScalarGridSpec(
            num_scalar_prefetch=1, grid=(n, h // th),
            in_specs=[pl.BlockSpec((None, None, th, w), lambda k, r, c: (k, c[0], r, 0)),
                      pl.BlockSpec((None, th, w), lambda k, r, c: (k, r, 0))],
            out_specs=pl.BlockSpec((None, th, w), lambda k, r, c: (k, r, 0))),
    )(_idx(lax.axis_index("c")), buf, recv)


def sum_slots(q, r, name, dst=None, g=None):
    n, h, w = r.shape
    th = _pick(h, _ROW_BLOCKS)

    def body(i_ref, q_ref, r1, r2, r3, *rest):
        del i_ref
        rest[-1][...] = ((q_ref[...].astype(F32) + r1[...].astype(F32)) + r2[...].astype(F32)) + r3[...].astype(F32)

    def slot(d):
        return lambda i, ix: ((ix[0] + d) % N_CHIP, i, 0)

    idx = jnp.stack([2 * lax.axis_index("x") + lax.axis_index("y"), lax.axis_index("c")]).astype(jnp.int32)
    in_specs = [pl.BlockSpec((None, th, w), slot(d)) for d in (0, 1, 2, 3)]
    if dst is None:
        return pl.pallas_call(
            body, name=name, out_shape=_sds((2, h, w), F32),
            grid_spec=pltpu.PrefetchScalarGridSpec(
                num_scalar_prefetch=1, grid=(h // th,), in_specs=in_specs,
                out_specs=pl.BlockSpec((None, th, w), lambda i, ix: (ix[1], i, 0))),
        )(idx, q, r, r, r)
    return pl.pallas_call(
        body, name=name, out_shape=_sds(dst.shape, F32),
        grid_spec=pltpu.PrefetchScalarGridSpec(
            num_scalar_prefetch=1, grid=(h // th,), in_specs=in_specs + [ANY],
            out_specs=pl.BlockSpec((None, None, th, w), lambda i, ix: (g, ix[1], i, 0))),
        input_output_aliases={5: 0},
    )(idx, q, r, r, r, dst)


def reduce_scatter_chips(buf, tag, wire=F32, dst=None, g=None):
    recv = sibling_send_other_half(buf, f"rs_sib_{tag}")
    q = add_own_half(buf, recv, wire, f"rs_add2_{tag}")
    r = chips_all_to_all(q, f"rs_a2a_{tag}")
    red = sum_slots(q, r, f"rs_add4_{tag}", dst=dst, g=g)
    return sibling_join_halves(red, f"rs_join_{tag}", g=g)


class Layout:
    def __init__(self, n_ctx, n_lat):
        self.C, self.L = n_ctx, n_lat
        self.PS = n_ctx + n_lat
        self.T = 2 * self.PS
        self.tr = _pick(math.gcd(n_ctx, n_lat), (256, 128))
        self.bps = self.PS // self.tr
        self.cb = n_ctx // self.tr
        self.nblk = self.T // self.tr
        self.tm = _pick(self.T, (1152, 768, 512, 256, 128))

    def seg(self, i):
        return jnp.where(i % self.bps < self.cb, 2, i // self.bps)


def rowwise(lay, name, fn, rows, segs=(), vecs=(), outs=(), sums=()):
    tr, nblk = lay.tr, lay.nblk
    n_r, n_s, n_v, n_o = len(rows), len(segs), len(vecs), len(outs)

    def body(*refs):
        ins = refs[:n_r + n_s + n_v]
        ors = refs[n_r + n_s + n_v:]
        vals = [r[...] for r in ins[:n_r]] + [r[0] for r in ins[n_r:n_r + n_s]] + [r[...] for r in ins[n_r + n_s:]]
        res = fn(*vals)
        for k in range(n_o):
            ors[k][...] = res[k].astype(ors[k].dtype)
        for k in range(len(sums)):
            ors[n_o + k][0] = res[n_o + k]

    def all_rows(i):
        return (i, 0)

    def lat_rows(i):
        return ((i // lay.bps) * (lay.bps - lay.cb) + jnp.maximum(i % lay.bps - lay.cb, 0), 0)

    in_specs = [pl.BlockSpec((tr, a.shape[1]), all_rows if a.shape[0] == lay.T else lat_rows) for a in rows]
    in_specs += [pl.BlockSpec((1,) + a.shape[1:], lambda i: (lay.seg(i), 0, 0)) for a in segs]
    in_specs += [pl.BlockSpec(a.shape, lambda i: (0, 0)) for a in vecs]
    out_shape = [_sds((2 * lay.L if o[2:] else lay.T, o[0]), o[1]) for o in outs]
    out_shape += [_sds((nblk, r, w), F32) for r, w in sums]
    out_specs = [pl.BlockSpec((tr, o[0]), lat_rows if o[2:] else all_rows) for o in outs]
    out_specs += [pl.BlockSpec((1, r, w), lambda i: (i, 0, 0)) for r, w in sums]
    sem = "arbitrary" if any(o[2:] for o in outs) else "parallel"
    return pl.pallas_call(body, name=name, out_shape=out_shape, grid=(nblk,), in_specs=in_specs,
                          out_specs=out_specs, compiler_params=_cp((sem,)))(*rows, *segs, *vecs)


def modulate(lay, h, mod, k_shift, k_scale, name):
    def fn(hb, m):
        return (hb * (1.0 + m[k_scale:k_scale + 1]) + m[k_shift:k_shift + 1],)
    return rowwise(lay, name, fn, [h], segs=[mod], outs=[(D, BF16)])[0]


def resid_ln(lay, h, y, mod, k_gate, coef, lnv, name, nxt=None):
    def fn(hb, yb, m, *rest):
        ln = rest[-1]
        z = ALPHA * hb + (coef * m[k_gate:k_gate + 1]) * yb
        mu = jnp.mean(z, axis=-1, keepdims=True)
        zc = z - mu
        var = jnp.mean(zc * zc, axis=-1, keepdims=True)
        rstd = lax.rsqrt(var + LN_EPS)
        xhat = zc * rstd
        out = xhat * ln[0:1] + ln[1:2]
        if nxt is None:
            return out, xhat, rstd
        mn = rest[0]
        return out, xhat, rstd, out * (1.0 + mn[nxt[2]:nxt[2] + 1]) + mn[nxt[1]:nxt[1] + 1]
    segs = [mod] if nxt is None else [mod, nxt[0]]
    outs = [(D, F32), (D, F32), (1, F32)] + ([] if nxt is None else [(D, BF16)])
    return rowwise(lay, name, fn, [h, y], segs=segs, vecs=[lnv], outs=outs)


def _ln_bwd_math(do, xh, rs, yb, gate, coef, ln):
    dxh = do * ln[0:1]
    m1 = jnp.mean(dxh, axis=-1, keepdims=True)
    m2 = jnp.mean(dxh * xh, axis=-1, keepdims=True)
    dz = rs * (dxh - m1 - xh * m2)
    s = jnp.concatenate([jnp.sum(do, axis=0, keepdims=True), jnp.sum(do * xh, axis=0, keepdims=True),
                         jnp.sum(coef * dz * yb, axis=0, keepdims=True)], axis=0)
    return (coef * gate) * dz, ALPHA * dz, s


def _mod_bwd_math(dr, dm, hb, scale):
    s = jnp.concatenate([jnp.sum(dm, axis=0, keepdims=True), jnp.sum(dm * hb, axis=0, keepdims=True)], axis=0)
    return dr + dm * (1.0 + scale), s


def ln_bwd(lay, dout, xhat, rstd, y, mod, k_gate, coef, lnv, name):
    def fn(do, xh, rs, yb, m, ln):
        return _ln_bwd_math(do, xh, rs, yb, m[k_gate:k_gate + 1], coef, ln)
    return rowwise(lay, name, fn, [dout, xhat, rstd, y], segs=[mod], vecs=[lnv],
                   outs=[(D, BF16), (D, F32)], sums=[(3, D)])


def mod_bwd(lay, dres, dhm, h, mod, k_scale, name):
    def fn(dr, dm, hb, m):
        return _mod_bwd_math(dr, dm, hb, m[k_scale:k_scale + 1])
    return rowwise(lay, name, fn, [dres, dhm, h], segs=[mod], outs=[(D, F32, "lat")], sums=[(2, D)])


def modb_lnb(lay, dres, dhm, h, mod, k_scale, xhat, rstd, y, mod_p, k_gate, coef, lnv, name):
    def fn(dr, dm, hb, xh, rs, yb, m, mp, ln):
        dh, s2 = _mod_bwd_math(dr, dm, hb, m[k_scale:k_scale + 1])
        dy, dres_p, s1 = _ln_bwd_math(dh, xh, rs, yb, mp[k_gate:k_gate + 1], coef, ln)
        return dy, dres_p, s1, s2
    return rowwise(lay, name, fn, [dres, dhm, h, xhat, rstd, y], segs=[mod, mod_p], vecs=[lnv],
                   outs=[(D, BF16), (D, F32)], sums=[(3, D), (2, D)])


def block_sums(lay, parts, name):
    nblk, r, w = parts.shape

    def body(p_ref, o_ref):
        acc = [None, None, None]
        for i in range(nblk):
            sg = 2 if i % lay.bps < lay.cb else i // lay.bps
            acc[sg] = p_ref[i] if acc[sg] is None else acc[sg] + p_ref[i]
        for k in range(3):
            o_ref[k] = acc[k]
        o_ref[3] = (acc[0] + acc[1]) + acc[2]

    return pl.pallas_call(body, name=name, out_shape=_sds((4, r, w), F32), in_specs=[VMEM_SPEC],
                          out_specs=VMEM_SPEC)(parts)


def mm_nn(a, b, name, out_dtype=F32, bias=None):
    m, k = a.shape
    n = b.shape[1]
    tm = _pick(m, (512, 256, 128, 64, 32, 16, 8))
    tn = _pick(n, (1024, 768, 640, 512, 384, 256, 128))

    def body(*refs):
        if bias is None:
            a_ref, b_ref, o_ref = refs
            o_ref[...] = _nn(a_ref[...].astype(BF16), b_ref[...].astype(BF16)).astype(o_ref.dtype)
        else:
            a_ref, b_ref, c_ref, o_ref = refs
            o_ref[...] = (_nn(a_ref[...].astype(BF16), b_ref[...].astype(BF16)) + c_ref[...]).astype(o_ref.dtype)

    in_specs = [pl.BlockSpec((tm, k), lambda i, j: (i, 0)), pl.BlockSpec((k, tn), lambda i, j: (0, j))]
    ops = [a, b]
    if bias is not None:
        in_specs.append(pl.BlockSpec((1, tn), lambda i, j: (0, j)))
        ops.append(bias)
    return pl.pallas_call(body, name=name, out_shape=_sds((m, n), out_dtype), grid=(m // tm, n // tn),
                          in_specs=in_specs, out_specs=pl.BlockSpec((tm, tn), lambda i, j: (i, j)),
                          compiler_params=_cp(("parallel", "parallel"), VMEM_BIG))(*ops)


def mm_nt(a, b, name, out_dtype=F32):
    m, k = a.shape
    n = b.shape[0]
    tm = _pick(m, (512, 256, 128, 64, 32, 16, 8))
    tn = _pick(n, (1024, 768, 640, 512, 384, 256, 128))

    def body(a_ref, b_ref, o_ref):
        o_ref[...] = _nt(a_ref[...].astype(BF16), b_ref[...].astype(BF16)).astype(o_ref.dtype)

    return pl.pallas_call(body, name=name, out_shape=_sds((m, n), out_dtype), grid=(m // tm, n // tn),
                          in_specs=[pl.BlockSpec((tm, k), lambda i, j: (i, 0)), pl.BlockSpec((tn, k), lambda i, j: (j, 0))],
                          out_specs=pl.BlockSpec((tm, tn), lambda i, j: (i, j)),
                          compiler_params=_cp(("parallel", "parallel"), VMEM_BIG))(a, b)


def mm_tn(a, b, name):
    t, m = a.shape
    n = b.shape[1]
    tk = _pick(t, (512, 256, 128, 64, 32, 16))
    tm = _pick(m, (512, 384, 256, 128))

    def body(a_ref, b_ref, o_ref):
        @pl.when(pl.program_id(1) == 0)
        def _():
            o_ref[...] = jnp.zeros_like(o_ref)
        o_ref[...] += _tn(a_ref[...].astype(BF16), b_ref[...].astype(BF16))

    return pl.pallas_call(body, name=name, out_shape=_sds((m, n), F32), grid=(m // tm, t // tk),
                          in_specs=[pl.BlockSpec((tk, tm), lambda i, k: (k, i)), pl.BlockSpec((tk, n), lambda i, k: (k, 0))],
                          out_specs=pl.BlockSpec((tm, n), lambda i, k: (i, 0)),
                          compiler_params=_cp(("parallel", "arbitrary"), VMEM_BIG))(a, b)


class Rider:
    def __init__(self, ins, outs, aliases, nsem, start, wait):
        self.ins, self.outs, self.aliases, self.nsem, self.start, self.wait = ins, outs, aliases, nsem, start, wait


def _chips_of(mx, my):
    return [(1 - mx, my), (mx, 1 - my), (1 - mx, 1 - my)]


def rider_gather_ici(buf):
    def start(ins, outs, ssem, rsem):
        o = outs[0]
        mx, my, mc = _me()
        s = 2 * mx + my
        for j, (px, py) in enumerate(_chips_of(mx, my)):
            _rcopy(o.at[s, mc], o.at[s, mc], ssem.at[j], rsem.at[j], (px, py, mc)).start()

    def wait(ins, outs, ssem, rsem):
        o = outs[0]
        mx, my, mc = _me()
        s = 2 * mx + my
        for j, (px, py) in enumerate(_chips_of(mx, my)):
            ps = 2 * px + py
            _rcopy(o.at[ps, mc], o.at[ps, mc], ssem.at[j], rsem.at[j], (px, py, mc)).wait_recv()
        for j, (px, py) in enumerate(_chips_of(mx, my)):
            _rcopy(o.at[s, mc], o.at[s, mc], ssem.at[j], rsem.at[j], (px, py, mc)).wait_send()

    return Rider([buf], [_sds(buf.shape, buf.dtype)], {0: 0}, 3, start, wait)


def rider_gather_d2d(buf):
    def start(ins, outs, ssem, rsem):
        o = outs[0]
        mx, my, mc = _me()
        for j, (px, py) in enumerate(_chips_of(mx, my)):
            ps = 2 * px + py
            _rcopy(o.at[ps, mc], o.at[ps, mc], ssem.at[j], rsem.at[j], (mx, my, 1 - mc)).start()

    def wait(ins, outs, ssem, rsem):
        o = outs[0]
        mx, my, mc = _me()
        sib = (mx, my, 1 - mc)
        for j, (px, py) in enumerate(_chips_of(mx, my)):
            ps = 2 * px + py
            _rcopy(o.at[ps, 1 - mc], o.at[ps, 1 - mc], ssem.at[j], rsem.at[j], sib).wait_recv()
        for j, (px, py) in enumerate(_chips_of(mx, my)):
            ps = 2 * px + py
            _rcopy(o.at[ps, mc], o.at[ps, mc], ssem.at[j], rsem.at[j], sib).wait_send()

    return Rider([buf], [_sds(buf.shape, buf.dtype)], {0: 0}, 3, start, wait)


def rider_reduce_sib(buf):
    n, _, h, w = buf.shape

    def start(ins, outs, ssem, rsem):
        mx, my, mc = _me()
        for k in range(N_CHIP):
            _rcopy(ins[0].at[k, 1 - mc], outs[0].at[k], ssem.at[k], rsem.at[k], (mx, my, 1 - mc)).start()

    def wait(ins, outs, ssem, rsem):
        mx, my, mc = _me()
        for k in range(N_CHIP):
            _rcopy(ins[0].at[k, 1 - mc], outs[0].at[k], ssem.at[k], rsem.at[k], (mx, my, 1 - mc)).wait_recv()
        for k in range(N_CHIP):
            _rcopy(ins[0].at[k, 1 - mc], outs[0].at[k], ssem.at[k], rsem.at[k], (mx, my, 1 - mc)).wait_send()

    return Rider([buf], [_sds((n, h, w), buf.dtype)], {}, N_CHIP, start, wait)


def rider_reduce_a2a(q):
    def start(ins, outs, ssem, rsem):
        mx, my, mc = _me()
        s = 2 * mx + my
        for j, (px, py) in enumerate(_chips_of(mx, my)):
            _rcopy(ins[0].at[2 * px + py], outs[0].at[s], ssem.at[j], rsem.at[j], (px, py, mc)).start()

    def wait(ins, outs, ssem, rsem):
        mx, my, mc = _me()
        s = 2 * mx + my
        for j, (px, py) in enumerate(_chips_of(mx, my)):
            ps = 2 * px + py
            _rcopy(ins[0].at[ps], outs[0].at[ps], ssem.at[j], rsem.at[j], (px, py, mc)).wait_recv()
        for j, (px, py) in enumerate(_chips_of(mx, my)):
            _rcopy(ins[0].at[2 * px + py], outs[0].at[s], ssem.at[j], rsem.at[j], (px, py, mc)).wait_send()

    return Rider([q], [_sds(q.shape, q.dtype)], {}, 3, start, wait)


def _host_call(body, rider, name, grid, in_specs, out_specs, out_shape, operands, sem, n_in, n_out):
    if rider is None:
        return pl.pallas_call(body, name=name, out_shape=out_shape, grid=grid, in_specs=in_specs, out_specs=out_specs,
                              compiler_params=_cp(sem, VMEM_BIG))(*operands)
    n_ri, n_ro = len(rider.ins), len(rider.outs)

    def hosted(*refs):
        ins, r_in = refs[:n_in], refs[n_in:n_in + n_ri]
        outs, r_out = refs[n_in + n_ri:n_in + n_ri + n_out], refs[n_in + n_ri + n_out:n_in + n_ri + n_out + n_ro]
        ssem, rsem = refs[-2], refs[-1]
        first = functools.reduce(lambda a, b: a & b, [pl.program_id(k) == 0 for k in range(len(grid))])
        last = functools.reduce(lambda a, b: a & b, [pl.program_id(k) == grid[k] - 1 for k in range(len(grid))])

        @pl.when(first)
        def _():
            rider.start(r_in, r_out, ssem, rsem)
        body(*ins, *outs)

        @pl.when(last)
        def _():
            rider.wait(r_in, r_out, ssem, rsem)

    return pl.pallas_call(
        hosted, name=name, out_shape=list(out_shape) + list(rider.outs), grid=grid,
        in_specs=list(in_specs) + [ANY] * n_ri, out_specs=list(out_specs) + [ANY] * n_ro,
        input_output_aliases={n_in + a: n_out + b for a, b in rider.aliases.items()},
        scratch_shapes=[pltpu.SemaphoreType.DMA((rider.nsem,)), pltpu.SemaphoreType.DMA((rider.nsem,))],
        compiler_params=_cp(("arbitrary",) * len(grid), VMEM_BIG))(*operands, *rider.ins)


def ffn_up(lay, hm, wbuf, ig, iu, ns, name, rider=None):
    tm = lay.tm

    def body(h_ref, wg_ref, wu_ref, g_ref, u_ref, a_ref):
        hb = h_ref[...]
        g = _nt(hb, wg_ref[0])
        u = _nt(hb, wu_ref[0])
        g_ref[0] = g.astype(BF16)
        u_ref[0] = u.astype(BF16)
        a_ref[0] = (g * _sigmoid(g) * u).astype(BF16)

    spec_o = pl.BlockSpec((1, tm, ns), lambda s, i: (s, i, 0))
    return _host_call(
        body, rider, name, (N_CHIP, lay.T // tm),
        [pl.BlockSpec((tm, D), lambda s, i: (i, 0)), pl.BlockSpec((1, ns, D), lambda s, i: (s, ig, 0)),
         pl.BlockSpec((1, ns, D), lambda s, i: (s, iu, 0))],
        [spec_o] * 3, [_sds((N_CHIP, lay.T, ns), BF16)] * 3, (hm, wbuf, wbuf), ("parallel", "parallel"), 3, 3)


def slab_nn_acc(lay, zs, wbuf, idxs, ns, name, rider=None):
    tm = lay.tm
    npair = len(zs)

    def body(*refs):
        o_ref = refs[-1]

        @pl.when(pl.program_id(1) == 0)
        def _():
            o_ref[...] = jnp.zeros_like(o_ref)
        acc = _nn(refs[0][0], refs[npair][0])
        for p in range(1, npair):
            acc += _nn(refs[p][0], refs[npair + p][0])
        o_ref[...] += acc

    in_specs = [pl.BlockSpec((1, tm, ns), lambda i, s: (s, i, 0)) for _ in zs]
    in_specs += [pl.BlockSpec((1, ns, D), functools.partial(lambda i, s, q: (s, q, 0), q=q)) for q in idxs]
    return _host_call(body, rider, name, (lay.T // tm, N_CHIP), in_specs, [pl.BlockSpec((tm, D), lambda i, s: (i, 0))],
                      [_sds((lay.T, D), F32)], (*zs, *([wbuf] * npair)), ("parallel", "arbitrary"), 2 * npair, 1)


def ffn_bwd_da(lay, dy, wbuf, idn, g, u, ns, name, rider=None):
    tm = lay.tm

    def body(dy_ref, wd_ref, g_ref, u_ref, dg_ref, du_ref):
        da = _nt(dy_ref[...], wd_ref[0])
        gv = g_ref[0].astype(F32)
        uv = u_ref[0].astype(F32)
        sg = _sigmoid(gv)
        dg_ref[0] = (da * uv * (sg * (1.0 + gv * (1.0 - sg)))).astype(BF16)
        du_ref[0] = (da * (gv * sg)).astype(BF16)

    spec_z = pl.BlockSpec((1, tm, ns), lambda s, i: (s, i, 0))
    return _host_call(
        body, rider, name, (N_CHIP, lay.T // tm),
        [pl.BlockSpec((tm, D), lambda s, i: (i, 0)), pl.BlockSpec((1, ns, D), lambda s, i: (s, idn, 0)), spec_z, spec_z],
        [spec_z] * 2, [_sds((N_CHIP, lay.T, ns), BF16)] * 2, (dy, wbuf, g, u), ("parallel", "parallel"), 4, 2)


def slab_tn(lay, z, x, gbuf, idx, ns, name):
    tk = lay.tm

    def body(z_ref, x_ref, g_in, o_ref):
        del g_in

        @pl.when(pl.program_id(1) == 0)
        def _():
            o_ref[...] = jnp.zeros_like(o_ref)
        o_ref[0] += _tn(z_ref[0], x_ref[...])

    return pl.pallas_call(
        body, name=name, out_shape=_sds(gbuf.shape, F32), grid=(N_CHIP, lay.T // tk),
        in_specs=[pl.BlockSpec((1, tk, ns), lambda s, k: (s, k, 0)), pl.BlockSpec((tk, D), lambda s, k: (k, 0)), ANY],
        out_specs=pl.BlockSpec((1, ns, D), lambda s, k: (s, idx, 0)),
        input_output_aliases={2: 0}, compiler_params=_cp(("parallel", "arbitrary"), VMEM_BIG))(z, x, gbuf)


Q0, K0, V0, U0, QR0, KR0, PEXT = 0, 512, 640, 768, 1280, 1792, 1920


def rope_fwd(lay, p, cos, sin, name):
    def fn(pb, cs, sn):
        cs4 = jnp.concatenate([cs] * 4, axis=1)
        sn4 = jnp.concatenate([sn] * 4, axis=1)
        qr = pb[:, Q0:K0] * cs4 + pb[:, QR0:KR0] * sn4
        kr = pb[:, K0:V0] * cs + pb[:, KR0:PEXT] * sn
        return qr, kr, pb[:, V0:U0], pb[:, U0:QR0]
    return rowwise(lay, name, fn, [p, cos, sin], outs=[(ATT_W, BF16), (KV_W, BF16), (KV_W, BF16), (POOL_W, F32)])


def rope_bwd(lay, dqr, dkr, dv, du, cos, sin, name):
    def fn(dq, dk, dvb, dub, cs, sn):
        cs4 = jnp.concatenate([cs] * 4, axis=1)
        sn4 = jnp.concatenate([sn] * 4, axis=1)
        return (jnp.concatenate([dq * cs4, dk * cs, dvb, dub, dq * sn4, dk * sn], axis=1),)
    return rowwise(lay, name, fn, [dqr, dkr, dv, du, cos, sin], outs=[(PEXT, BF16)])[0]


def _attn_specs(lay):
    nbs, cbk, lbk = lay.PS // BLK, lay.C // BLK, lay.L // BLK

    def kv_map(j):
        return lambda s, n: (s * nbs + cbk + jnp.clip(n - cbk + j - 1, 0, lbk - 1), 0)

    win = [pl.BlockSpec((BLK, KV_W), kv_map(j)) for j in range(3)]
    ctx = pl.BlockSpec((lay.C, KV_W), lambda s, n: (s * (lay.PS // lay.C), 0))
    return nbs, cbk, lbk, win, ctx


def _attn_masks(n, cbk, lbk):
    row = lax.broadcasted_iota(jnp.int32, (BLK, BLK), 0)
    col = lax.broadcasted_iota(jnp.int32, (BLK, BLK), 1)
    m = n - cbk
    lat = n >= cbk
    valid = [lat & (m >= 1) & (col >= row), lat & (col >= 0), lat & (m <= lbk - 2) & (col <= row)]
    lane_lo = lax.broadcasted_iota(jnp.int32, (BLK, 2 * HEAD_DIM), 1) < HEAD_DIM
    return valid, lane_lo


def attn_fwd(lay, qr, kr, vb, sink_tab, name):
    nbs, cbk, lbk, win, ctx = _attn_specs(lay)

    def body(q_ref, k0, k1, k2, kc_ref, v0, v1, v2, vc_ref, sk_ref, o_ref, l_ref):
        n = pl.program_id(1)
        valid, lane_lo = _attn_masks(n, cbk, lbk)
        ks = [k0[...], k1[...], k2[...]]
        vs = [v0[...], v1[...], v2[...]]
        kc, vc = kc_ref[...], vc_ref[...]
        for p in range(4):
            q2 = q_ref[:, p * 128:(p + 1) * 128]
            outs, lses = [], []
            for hh in range(2):
                qm = jnp.where(lane_lo == (hh == 0), q2, jnp.zeros_like(q2))
                sk = sk_ref[p:p + 1, hh * HEAD_DIM:hh * HEAD_DIM + 1]
                sw = [jnp.where(valid[j], _nt(qm, ks[j]) * ATT_SCALE, NEG_INF) for j in range(3)]
                sc = _nt(qm, kc) * ATT_SCALE
                mx = jnp.maximum(jnp.maximum(jnp.maximum(sw[0].max(-1, keepdims=True), sw[1].max(-1, keepdims=True)),
                                             jnp.maximum(sw[2].max(-1, keepdims=True), sc.max(-1, keepdims=True))), sk)
                ew = [jnp.exp(s - mx) for s in sw]
                ec = jnp.exp(sc - mx)
                den = ew[0].sum(-1, keepdims=True) + ew[1].sum(-1, keepdims=True) + ew[2].sum(-1, keepdims=True)
                den = den + ec.sum(-1, keepdims=True) + jnp.exp(sk - mx)
                o = _nn((ec / den).astype(BF16), vc)
                for j in range(3):
                    o += _nn((ew[j] / den).astype(BF16), vs[j])
                outs.append(o)
                lses.append(jnp.broadcast_to(mx + jnp.log(den), (BLK, 128)))
            o_ref[:, p * 128:(p + 1) * 128] = jnp.where(lane_lo, outs[0], outs[1]).astype(o_ref.dtype)
            l_ref[:, p * 128:(p + 1) * 128] = jnp.where(lane_lo, lses[0], lses[1])

    qspec = pl.BlockSpec((BLK, ATT_W), lambda s, n: (s * nbs + n, 0))
    return pl.pallas_call(
        body, name=name, out_shape=[_sds((lay.T, ATT_W), BF16), _sds((lay.T, ATT_W), F32)], grid=(2, nbs),
        in_specs=[qspec] + win + [ctx] + win + [ctx] + [pl.BlockSpec((8, 128), lambda s, n: (0, 0))],
        out_specs=[qspec, qspec], compiler_params=_cp(("parallel", "parallel")))(qr, kr, kr, kr, kr, vb, vb, vb, vb, sink_tab)


def attn_bwd(lay, qr, kr, vb, sink_tab, lse, datt, name):
    nbs, cbk, lbk, win, ctx = _attn_specs(lay)
    C, PS = lay.C, lay.PS

    def body(q_ref, k0, k1, k2, kc_ref, v0, v1, v2, vc_ref, sk_ref, l_ref, do_ref, dq_ref, dk_ref, dv_ref, ds_ref):
        n = pl.program_id(1)
        valid, lane_lo = _attn_masks(n, cbk, lbk)

        @pl.when(n == 0)
        def _():
            dk_ref[...] = jnp.zeros_like(dk_ref)
            dv_ref[...] = jnp.zeros_like(dv_ref)
            ds_ref[...] = jnp.zeros_like(ds_ref)

        ks = [k0[...], k1[...], k2[...], kc_ref[...]]
        vs = [v0[...], v1[...], v2[...], vc_ref[...]]
        dks = [jnp.zeros((BLK, KV_W), F32)] * 3 + [jnp.zeros((C, KV_W), F32)]
        dvs = list(dks)
        for p in range(4):
            sl = slice(p * 128, (p + 1) * 128)
            q2 = q_ref[:, sl]
            do2 = do_ref[:, sl].astype(BF16)
            lse2 = l_ref[:, sl]
            dq_h, dd_h = [], []
            for hh in range(2):
                sel = lane_lo == (hh == 0)
                qm = jnp.where(sel, q2, jnp.zeros_like(q2))
                dom = jnp.where(sel, do2, jnp.zeros_like(do2))
                lse_h = lse2[:, hh * HEAD_DIM:hh * HEAD_DIM + 1]
                ps, dps = [], []
                for j in range(4):
                    s = _nt(qm, ks[j]) * ATT_SCALE
                    if j < 3:
                        s = jnp.where(valid[j], s, NEG_INF)
                    ps.append(jnp.exp(s - lse_h))
                    dps.append(_nt(dom, vs[j]))
                dd = (ps[0] * dps[0]).sum(-1, keepdims=True) + (ps[1] * dps[1]).sum(-1, keepdims=True)
                dd = dd + (ps[2] * dps[2]).sum(-1, keepdims=True) + (ps[3] * dps[3]).sum(-1, keepdims=True)
                dq = jnp.zeros((BLK, 128), F32)
                for j in range(4):
                    dsb = (ps[j] * (dps[j] - dd) * ATT_SCALE).astype(BF16)
                    dq += _nn(dsb, ks[j])
                    dks[j] = dks[j] + _tn(dsb, qm)
                    dvs[j] = dvs[j] + _tn(ps[j].astype(BF16), dom)
                dq_h.append(dq)
                dd_h.append(jnp.broadcast_to(dd, (BLK, 128)))
            dq_ref[:, sl] = jnp.where(lane_lo, dq_h[0], dq_h[1])
            dd2 = jnp.where(lane_lo, dd_h[0], dd_h[1])
            psink = jnp.exp(sk_ref[p:p + 1, :] - lse2)
            ds_ref[0, p:p + 1, :] += -jnp.sum(psink * dd2, axis=0, keepdims=True)
        dk_ref[0:C, :] += dks[3]
        dv_ref[0:C, :] += dvs[3]
        for j in range(3):
            r0 = pl.multiple_of((cbk + jnp.clip(n - cbk + j - 1, 0, lbk - 1)) * BLK, BLK)
            dk_ref[pl.ds(r0, BLK), :] += dks[j]
            dv_ref[pl.ds(r0, BLK), :] += dvs[j]

    qspec = pl.BlockSpec((BLK, ATT_W), lambda s, n: (s * nbs + n, 0))
    kvout = pl.BlockSpec((PS, KV_W), lambda s, n: (s, 0))
    return pl.pallas_call(
        body, name=name,
        out_shape=[_sds((lay.T, ATT_W), F32), _sds((lay.T, KV_W), F32), _sds((lay.T, KV_W), F32), _sds((2, 8, 128), F32)],
        grid=(2, nbs),
        in_specs=[qspec] + win + [ctx] + win + [ctx] + [pl.BlockSpec((8, 128), lambda s, n: (0, 0)), qspec, qspec],
        out_specs=[qspec, kvout, kvout, pl.BlockSpec((1, 8, 128), lambda s, n: (s, 0, 0))],
        compiler_params=_cp(("parallel", "arbitrary")))(qr, kr, kr, kr, kr, vb, vb, vb, vb, sink_tab, lse, datt)


def _winsum(x, r):
    n = x.shape[0]
    t = lax.broadcasted_iota(jnp.int32, x.shape, 0)
    acc = x
    for o in range(1, r + 1):
        acc = acc + jnp.where(t >= o, pltpu.roll(x, o, 0), 0.0) + jnp.where(t < n - o, pltpu.roll(x, n - o, 0), 0.0)
    return acc


def _wincount(n, r):
    t = lax.broadcasted_iota(jnp.int32, (n, 128), 0)
    return (jnp.minimum(t + r, n - 1) - jnp.maximum(t - r, 0) + 1).astype(F32)


def pool_fwd(lay, u, w_pool, scale, name):
    segs = [(0, lay.C), (lay.C, lay.L)]

    def body(u_ref, w_ref, s_ref, o_ref):
        for r0, n in segs:
            for g, wd in enumerate(POOL_WINDOWS):
                sl = slice(g * 128, (g + 1) * 128)
                x = u_ref[r0:r0 + n, sl]
                d = _winsum(x, wd // 2) / _wincount(n, wd // 2) - x
                y = _nn(d.astype(BF16), w_ref[g].astype(BF16)) * s_ref[:, sl]
                o_ref[r0:r0 + n, sl] = y.astype(o_ref.dtype)

    spec = pl.BlockSpec((lay.PS, POOL_W), lambda s: (s, 0))
    return pl.pallas_call(
        body, name=name, out_shape=_sds((lay.T, POOL_W), BF16), grid=(2,),
        in_specs=[spec, pl.BlockSpec(w_pool.shape, lambda s: (0, 0, 0)), pl.BlockSpec((1, POOL_W), lambda s: (0, 0))],
        out_specs=spec, compiler_params=_cp(("parallel",), VMEM_BIG))(u, w_pool, scale)


def pool_bwd(lay, u, dcat, w_pool, scale, name):
    segs = [(0, lay.C), (lay.C, lay.L)]

    def body(u_ref, dp_ref, w_ref, s_ref, du_ref, dw_ref, dsc_ref):
        for g, wd in enumerate(POOL_WINDOWS):
            sl = slice(g * 128, (g + 1) * 128)
            wb = w_ref[g].astype(BF16)
            dw = jnp.zeros((128, 128), F32)
            dsc = jnp.zeros((1, 128), F32)
            for r0, n in segs:
                x = u_ref[r0:r0 + n, sl]
                cnt = _wincount(n, wd // 2)
                d = (_winsum(x, wd // 2) / cnt - x).astype(BF16)
                dp = dp_ref[r0:r0 + n, sl]
                dsc += jnp.sum(_nn(d, wb) * dp, axis=0, keepdims=True)
                dyp = (dp * s_ref[:, sl]).astype(BF16)
                dw += _tn(d, dyp)
                dd = _nt(dyp, wb)
                du_ref[r0:r0 + n, sl] = _winsum(dd / cnt, wd // 2) - dd
            dw_ref[0, g] = dw
            dsc_ref[0, :, sl] = dsc

    spec = pl.BlockSpec((lay.PS, POOL_W), lambda s: (s, 0))
    return pl.pallas_call(
        body, name=name,
        out_shape=[_sds((lay.T, POOL_W), F32), _sds((2, 4, 128, 128), F32), _sds((2, 1, POOL_W), F32)], grid=(2,),
        in_specs=[spec, pl.BlockSpec((lay.PS, POOL_W), lambda s: (s, 1)), pl.BlockSpec(w_pool.shape, lambda s: (0, 0, 0)),
                  pl.BlockSpec((1, POOL_W), lambda s: (0, 0))],
        out_specs=[spec, pl.BlockSpec((1, 4, 128, 128), lambda s: (s, 0, 0, 0)), pl.BlockSpec((1, 1, POOL_W), lambda s: (s, 0, 0))],
        compiler_params=_cp(("parallel",), VMEM_BIG))(u, dcat, w_pool, scale)


CONV_OFFS = (-1, 0, 1, 2)
CW = 256


def _shift_rows(x, o):
    if o == 0:
        return x
    n = x.shape[0]
    t = lax.broadcasted_iota(jnp.int32, x.shape, 0)
    if o < 0:
        return jnp.where(t >= -o, pltpu.roll(x, -o, 0), 0.0)
    return jnp.where(t < n - o, pltpu.roll(x, n - o, 0), 0.0)


def conv_fwd(lay, p, col0, w, b, name):
    segs = [(0, lay.C), (lay.C, lay.L)]
    cb0 = col0 // CW

    def body(x_ref, w_ref, b_ref, o_ref):
        for r0, n in segs:
            x = x_ref[r0:r0 + n, :]
            y = jnp.broadcast_to(b_ref[...], x.shape)
            for k, o in enumerate(CONV_OFFS):
                y = y + _shift_rows(x, o) * w_ref[k:k + 1, :]
            o_ref[r0:r0 + n, :] = y

    return pl.pallas_call(
        body, name=name, out_shape=_sds((lay.T, D), F32), grid=(2, D // CW),
        in_specs=[pl.BlockSpec((lay.PS, CW), lambda s, j: (s, cb0 + j)), pl.BlockSpec((4, CW), lambda s, j: (0, j)),
                  pl.BlockSpec((1, CW), lambda s, j: (0, j))],
        out_specs=pl.BlockSpec((lay.PS, CW), lambda s, j: (s, j)),
        compiler_params=_cp(("parallel", "parallel")))(p, w, b)


def conv_bwd(lay, p, col0, w, duc, name):
    segs = [(0, lay.C), (lay.C, lay.L)]
    cb0 = col0 // CW

    def body(x_ref, w_ref, g_ref, du_ref, dw_ref, db_ref):
        dws = [jnp.zeros((1, CW), F32)] * 4
        db = jnp.zeros((1, CW), F32)
        for r0, n in segs:
            x = x_ref[r0:r0 + n, :]
            g = g_ref[r0:r0 + n, :]
            du = jnp.zeros_like(g)
            for k, o in enumerate(CONV_OFFS):
                du = du + _shift_rows(g, -o) * w_ref[k:k + 1, :]
                dws[k] = dws[k] + jnp.sum(g * _shift_rows(x, o), axis=0, keepdims=True)
            db = db + jnp.sum(g, axis=0, keepdims=True)
            du_ref[r0:r0 + n, :] = du
        dw_ref[0] = jnp.concatenate(dws, axis=0)
        db_ref[0] = db

    return pl.pallas_call(
        body, name=name, out_shape=[_sds((lay.T, D), F32), _sds((2, 4, D), F32), _sds((2, 1, D), F32)], grid=(2, D // CW),
        in_specs=[pl.BlockSpec((lay.PS, CW), lambda s, j: (s, cb0 + j)), pl.BlockSpec((4, CW), lambda s, j: (0, j)),
                  pl.BlockSpec((lay.PS, CW), lambda s, j: (s, j))],
        out_specs=[pl.BlockSpec((lay.PS, CW), lambda s, j: (s, j)), pl.BlockSpec((1, 4, CW), lambda s, j: (s, 0, j)),
                   pl.BlockSpec((1, 1, CW), lambda s, j: (s, 0, j))],
        compiler_params=_cp(("parallel", "parallel")))(p, w, duc)


def _softplus_neg(lam):
    z = -lam
    w = jnp.exp(-jnp.abs(z))
    log1p = jnp.where(w < 1e-2, w * (1.0 - w * (0.5 - w / 3.0)), jnp.log(1.0 + w))
    return jnp.maximum(z, 0.0) + log1p, -_sigmoid(z)


def _neg_expm1(x):
    series = -x * (1.0 + x * (0.5 + x * (1.0 / 6.0 + x * (1.0 / 24.0 + x * (1.0 / 120.0)))))
    return jnp.where(x > -0.05, series, 1.0 - jnp.exp(x))


def _lru_gates(x, xb, wa, wx, ba, bx, lam):
    r = _sigmoid(_nn(xb, wa.astype(BF16)) + ba)
    gi = _sigmoid(_nn(xb, wx.astype(BF16)) + bx)
    sp, dsp = _softplus_neg(lam)
    la = -LRU_C * r * sp
    a = jnp.exp(la)
    sq = jnp.sqrt(_neg_expm1(2.0 * la))
    return r, gi, sp, dsp, a, sq


def lru_coeffs(lay, uc, wa, wx, vec, name):
    tr = lay.tr

    def body(x_ref, wa_ref, wx_ref, v_ref, a_ref, b_ref):
        for h in range(8):
            sl = slice(h * 128, (h + 1) * 128)
            x = x_ref[:, sl]
            xb = x.astype(BF16)
            for d in range(2):
                _, gi, _, _, a, sq = _lru_gates(x, xb, wa_ref[d, h], wx_ref[d, h], v_ref[d:d + 1, sl],
                                                v_ref[2 + d:3 + d, sl], v_ref[4 + d:5 + d, sl])
                a_ref[d, h] = a
                b_ref[d, h] = sq * (gi * x)

    wspec = pl.BlockSpec((2, 8, 128, 128), lambda i: (0, 0, 0, 0))
    ospec = pl.BlockSpec((2, 8, tr, 128), lambda i: (0, 0, i, 0))
    return pl.pallas_call(
        body, name=name, out_shape=[_sds((2, 8, lay.T, 128), F32)] * 2, grid=(lay.nblk,),
        in_specs=[pl.BlockSpec((tr, D), lambda i: (i, 0)), wspec, wspec, pl.BlockSpec((6, D), lambda i: (0, 0))],
        out_specs=[ospec, ospec], compiler_params=_cp(("parallel",)))(uc, wa, wx, vec)


def lru_coeffs_bwd(lay, uc, wa, wx, vec, da, db, name):
    tr = lay.tr

    def body(x_ref, wa_ref, wx_ref, v_ref, da_ref, db_ref, dx_ref, dwa_ref, dwx_ref, dv_ref):
        @pl.when(pl.program_id(0) == 0)
        def _():
            dwa_ref[...] = jnp.zeros_like(dwa_ref)
            dwx_ref[...] = jnp.zeros_like(dwx_ref)
            dv_ref[...] = jnp.zeros_like(dv_ref)

        for h in range(8):
            sl = slice(h * 128, (h + 1) * 128)
            x = x_ref[:, sl]
            xb = x.astype(BF16)
            dx = jnp.zeros_like(x)
            for d in range(2):
                wab, wxb = wa_ref[d, h].astype(BF16), wx_ref[d, h].astype(BF16)
                r, gi, sp, dsp, a, sq = _lru_gates(x, xb, wa_ref[d, h], wx_ref[d, h], v_ref[d:d + 1, sl],
                                                   v_ref[2 + d:3 + d, sl], v_ref[4 + d:5 + d, sl])
                dbv, dav = db_ref[d, h], da_ref[d, h]
                t1 = dbv * sq
                dgi = t1 * x
                dx = dx + t1 * gi
                dla = dav * a - (dbv * gi * x) * (a * a) / sq
                dr = dla * (-LRU_C * sp)
                dlam = jnp.sum(dla * (-LRU_C * r), axis=0, keepdims=True) * dsp
                dpa = dr * r * (1.0 - r)
                dpx = dgi * gi * (1.0 - gi)
                dpab, dpxb = dpa.astype(BF16), dpx.astype(BF16)
                dwa_ref[d, h] += _tn(xb, dpab)
                dwx_ref[d, h] += _tn(xb, dpxb)
                dx = dx + _nt(dpab, wab) + _nt(dpxb, wxb)
                dv_ref[d:d + 1, sl] += jnp.sum(dpa, axis=0, keepdims=True)
                dv_ref[2 + d:3 + d, sl] += jnp.sum(dpx, axis=0, keepdims=True)
                dv_ref[4 + d:5 + d, sl] += dlam
            dx_ref[:, sl] = dx

    wspec = pl.BlockSpec((2, 8, 128, 128), lambda i: (0, 0, 0, 0))
    gspec = pl.BlockSpec((2, 8, tr, 128), lambda i: (0, 0, i, 0))
    vspec = pl.BlockSpec((6, D), lambda i: (0, 0))
    xspec = pl.BlockSpec((tr, D), lambda i: (i, 0))
    return pl.pallas_call(
        body, name=name,
        out_shape=[_sds((lay.T, D), F32), _sds((2, 8, 128, 128), F32), _sds((2, 8, 128, 128), F32), _sds((6, D), F32)],
        grid=(lay.nblk,), in_specs=[xspec, wspec, wspec, vspec, gspec, gspec],
        out_specs=[xspec, wspec, wspec, vspec], compiler_params=_cp(("arbitrary",)))(uc, wa, wx, vec, da, db)


GB = 2
SCAN_UNROLL = 4


def _tile_scan(a, b, up):
    t = lax.broadcasted_iota(jnp.int32, a.shape, 0)
    for d in (1, 2, 4):
        sh = 8 - d if up else d
        m = (t < 8 - d) if up else (t >= d)
        a_prev, b_prev = pltpu.roll(a, sh, 0), pltpu.roll(b, sh, 0)
        b = jnp.where(m, a * b_prev + b, b)
        a = jnp.where(m, a * a_prev, a)
    return a, b


def lru_scan(lay, a, b, name):
    segs = [(0, lay.C), (lay.C, lay.L)]

    def body(a_ref, b_ref, s_ref):
        for d in range(2):
            rev = d == 1
            state = tuple(jnp.zeros((1, 128), F32) for _ in range(GB))
            for base, n in segs:
                nt = n // 8

                def step(j, c, base=base, nt=nt, rev=rev, d=d):
                    c = list(c)
                    for u in range(SCAN_UNROLL):
                        jj = j * SCAN_UNROLL + u
                        r0 = pl.multiple_of(base + 8 * ((nt - 1 - jj) if rev else jj), 8)
                        for g in range(GB):
                            at, bt = _tile_scan(a_ref[d, g, pl.ds(r0, 8), :], b_ref[d, g, pl.ds(r0, 8), :], rev)
                            h = at * c[g] + bt
                            s_ref[d, g, pl.ds(r0, 8), :] = h
                            c[g] = h[0:1] if rev else h[7:8]
                    return tuple(c)

                state = lax.fori_loop(0, nt // SCAN_UNROLL, step, state)

    spec = pl.BlockSpec((2, GB, lay.PS, 128), lambda s, hb: (0, hb, s, 0))
    return pl.pallas_call(
        body, name=name, out_shape=_sds((2, 8, lay.T, 128), F32), grid=(2, 8 // GB),
        in_specs=[spec, spec], out_specs=spec, compiler_params=_cp(("parallel", "parallel"), VMEM_BIG))(a, b)


def lru_scan_bwd(lay, a, s, dy, name):
    segs = [(0, lay.C), (lay.C, lay.L)]
    C, PS = lay.C, lay.PS

    def body(a_ref, s_ref, g_ref, da_ref, db_ref):
        t = lax.broadcasted_iota(jnp.int32, (8, 128), 0)
        for d in range(2):
            rev = d == 1
            carry = tuple(jnp.zeros((1, 128), F32) for _ in range(GB))
            for si in (1, 0):
                base, n = segs[si]
                nt = n // 8

                def step(j, c, base=base, nt=nt, rev=rev, d=d):
                    c = list(c)
                    for u in range(SCAN_UNROLL):
                        jj = j * SCAN_UNROLL + u
                        r0 = pl.multiple_of(base + 8 * (jj if rev else (nt - 1 - jj)), 8)
                        if rev:
                            rn = pl.multiple_of(jnp.where(r0 == PS - 8, 0, r0 + 8), 8)
                            nb_zero = r0 == C - 8
                        else:
                            rn = pl.multiple_of(jnp.maximum(r0 - 8, 0), 8)
                            nb_zero = r0 == 0
                        for g in range(GB):
                            av = a_ref[d, g, pl.ds(r0, 8), :]
                            gv = g_ref[g, pl.ds(r0, 8), :]
                            sv = s_ref[d, g, pl.ds(r0, 8), :]
                            nbt = s_ref[d, g, pl.ds(rn, 8), :]
                            at, bt = _tile_scan(av, av * gv, not rev)
                            m = at * c[g] + bt
                            if rev:
                                m_next = jnp.where(t >= 1, pltpu.roll(m, 1, 0), c[g])
                                nb = jnp.where(nb_zero, 0.0, nbt[0:1])
                                h_prev = jnp.where(t < 7, pltpu.roll(sv, 7, 0), nb)
                                c[g] = m[7:8]
                            else:
                                m_next = jnp.where(t < 7, pltpu.roll(m, 7, 0), c[g])
                                nb = jnp.where(nb_zero, 0.0, nbt[7:8])
                                h_prev = jnp.where(t >= 1, pltpu.roll(sv, 1, 0), nb)
                                c[g] = m[0:1]
                            lam = gv + m_next
                            db_ref[d, g, pl.ds(r0, 8), :] = lam
                            da_ref[d, g, pl.ds(r0, 8), :] = lam * h_prev
                    return tuple(c)

                carry = lax.fori_loop(0, nt // SCAN_UNROLL, step, carry)

    spec = pl.BlockSpec((2, GB, lay.PS, 128), lambda s, hb: (0, hb, s, 0))
    return pl.pallas_call(
        body, name=name, out_shape=[_sds((2, 8, lay.T, 128), F32)] * 2, grid=(2, 8 // GB),
        in_specs=[spec, spec, pl.BlockSpec((GB, lay.PS, 128), lambda s, hb: (hb, s, 0))],
        out_specs=[spec, spec], compiler_params=_cp(("parallel", "parallel"), VMEM_BIG))(a, s, dy)


def _gelu(x):
    k = math.sqrt(2.0 / math.pi)
    t = jnp.tanh(k * (x + 0.044715 * x * x * x))
    return 0.5 * x * (1.0 + t), 0.5 * (1.0 + t) + 0.5 * x * (1.0 - t * t) * k * (1.0 + 3 * 0.044715 * x * x)


def lru_gate(lay, p, s, name):
    tr = lay.tr

    def body(g_ref, s_ref, o_ref):
        for h in range(8):
            sl = slice(h * 128, (h + 1) * 128)
            o_ref[:, sl] = (_gelu(g_ref[:, sl])[0] * (s_ref[0, h] + s_ref[1, h])).astype(o_ref.dtype)

    return pl.pallas_call(
        body, name=name, out_shape=_sds((lay.T, D), BF16), grid=(lay.nblk,),
        in_specs=[pl.BlockSpec((tr, D), lambda i: (i, 0)), pl.BlockSpec((2, 8, tr, 128), lambda i: (0, 0, i, 0))],
        out_specs=pl.BlockSpec((tr, D), lambda i: (i, 0)), compiler_params=_cp(("parallel",)))(p, s)


def lru_gate_bwd(lay, p, s, do, name):
    tr = lay.tr

    def body(g_ref, s_ref, do_ref, dg_ref, dy_ref):
        for h in range(8):
            sl = slice(h * 128, (h + 1) * 128)
            ge, dge = _gelu(g_ref[:, sl])
            dov = do_ref[:, sl]
            dg_ref[:, sl] = dov * (s_ref[0, h] + s_ref[1, h]) * dge
            dy_ref[h] = dov * ge

    xspec = pl.BlockSpec((tr, D), lambda i: (i, 0))
    return pl.pallas_call(
        body, name=name, out_shape=[_sds((lay.T, D), F32), _sds((8, lay.T, 128), F32)], grid=(lay.nblk,),
        in_specs=[xspec, pl.BlockSpec((2, 8, tr, 128), lambda i: (0, 0, i, 0)), xspec],
        out_specs=[xspec, pl.BlockSpec((8, tr, 128), lambda i: (0, i, 0))],
        compiler_params=_cp(("parallel",)))(p, s, do)


def silu_rows(x, name):
    def body(x_ref, o_ref):
        v = x_ref[...]
        o_ref[...] = (v * _sigmoid(v)).astype(o_ref.dtype)
    return pl.pallas_call(body, name=name, out_shape=_sds(x.shape, BF16), in_specs=[VMEM_SPEC], out_specs=VMEM_SPEC)(x)


def mod_grad_rows(gath, name):
    w = gath.shape[-1]

    def body(g_ref, dm_ref, db_ref):
        dm_ref[...] = jnp.zeros_like(dm_ref)
        for l in range(2):
            ctx = g_ref[0, 3 * l + 2:3 * l + 3, :]
            tot = g_ref[0, 3 * l:3 * l + 1, :] + g_ref[0, 3 * l + 1:3 * l + 2, :]
            for k in range(8):
                dm_ref[l, 2 * k:2 * k + 2, :] = g_ref[k, 3 * l:3 * l + 2, :]
                if k:
                    ctx = ctx + g_ref[k, 3 * l + 2:3 * l + 3, :]
                    tot = tot + (g_ref[k, 3 * l:3 * l + 1, :] + g_ref[k, 3 * l + 1:3 * l + 2, :])
            dm_ref[l, 16:17, :] = ctx
            db_ref[l:l + 1, :] = tot + ctx

    return pl.pallas_call(body, name=name, out_shape=[_sds((2, 32, w), F32), _sds((2, w), F32)],
                          in_specs=[VMEM_SPEC], out_specs=[VMEM_SPEC, VMEM_SPEC])(gath)


def cctx_grad(p, c_ctx, name):
    def body(a_ref, c_ref, o_ref):
        cv = c_ref[...]
        sg = _sigmoid(cv)
        o_ref[...] = 0.5 * (a_ref[0, 0:1, :] + a_ref[1, 0:1, :]) * (sg * (1.0 + cv * (1.0 - sg)))
    return pl.pallas_call(body, name=name, out_shape=_sds((1, D), F32), in_specs=[VMEM_SPEC] * 2,
                          out_specs=VMEM_SPEC)(p, c_ctx)


def loss_and_grad(lay, h, tgt, name):
    def fn(hb, tb):
        lat = (pl.program_id(0) % lay.bps) >= lay.cb
        e = jnp.where(lat, hb - tb, 0.0)
        return e * (1.0 / D), jnp.sum(e * e, axis=0, keepdims=True) * (0.5 / D)
    return rowwise(lay, name, fn, [h, tgt], outs=[(D, F32)], sums=[(1, D)])


def adamw(w, g, m, v, name):
    shape = w.shape
    w2, g2, m2, v2 = (t.reshape(-1, shape[-1]) for t in (w, g, m, v))
    rows, width = w2.shape
    tr = 256 if rows % 256 == 0 else rows
    c1 = 1.0 - ADAM_B1 ** ADAM_STEP
    c2 = 1.0 - ADAM_B2 ** ADAM_STEP

    def body(w_ref, g_ref, m_ref, v_ref, d_ref, mo_ref, vo_ref):
        gv = g_ref[...]
        mn = ADAM_B1 * m_ref[...] + (1.0 - ADAM_B1) * gv
        vn = ADAM_B2 * v_ref[...] + (1.0 - ADAM_B2) * (gv * gv)
        d_ref[...] = -ADAM_LR * ((mn / c1) / (jnp.sqrt(vn / c2) + ADAM_EPS) + ADAM_WD * w_ref[...])
        mo_ref[...] = mn
        vo_ref[...] = vn

    spec = pl.BlockSpec((tr, width), lambda i: (i, 0))
    d, mn, vn = pl.pallas_call(body, name=name, out_shape=[_sds((rows, width), F32)] * 3, grid=(rows // tr,),
                               in_specs=[spec] * 4, out_specs=[spec] * 3, compiler_params=_cp(("parallel",)))(w2, g2, m2, v2)
    return d.reshape(shape), mn.reshape(shape), vn.reshape(shape)


def adamw_ffn(w, m, v, red, kind, ns, name):
    shape = w.shape
    w2, m2, v2 = (t.reshape(-1, shape[-1]) for t in (w, m, v))
    rows, width = w2.shape
    c1 = 1.0 - ADAM_B1 ** ADAM_STEP
    c2 = 1.0 - ADAM_B2 ** ADAM_STEP
    if kind < 2:
        tr, nb = 256, D // 256
        gspec = pl.BlockSpec((ns, tr), lambda i: ((i // nb) * 3 + kind, i % nb))
    else:
        tr, nb = ns // 2, 2
        gspec = pl.BlockSpec((tr, D), lambda i: (((i // nb) * 3 + kind) * nb + i % nb, 0))

    def body(w_ref, g_ref, m_ref, v_ref, go_ref, d_ref, mo_ref, vo_ref):
        gv = g_ref[...].T if kind < 2 else g_ref[...]
        mn = ADAM_B1 * m_ref[...] + (1.0 - ADAM_B1) * gv
        vn = ADAM_B2 * v_ref[...] + (1.0 - ADAM_B2) * (gv * gv)
        go_ref[...] = gv
        d_ref[...] = -ADAM_LR * ((mn / c1) / (jnp.sqrt(vn / c2) + ADAM_EPS) + ADAM_WD * w_ref[...])
        mo_ref[...] = mn
        vo_ref[...] = vn

    spec = pl.BlockSpec((tr, width), lambda i: (i, 0))
    outs = pl.pallas_call(body, name=name, out_shape=[_sds((rows, width), F32)] * 4, grid=(rows // tr,),
                          in_specs=[spec, gspec, spec, spec], out_specs=[spec] * 4,
                          compiler_params=_cp(("parallel",)))(w2, red, m2, v2)
    return tuple(t.reshape(shape) for t in outs)


def mod_mm(sc, w_mod, bias, name):
    wm = w_mod.shape[-1]
    tn = _pick(wm, (768, 512, 384, 256, 128))

    def body(a_ref, b_ref, c_ref, o_ref):
        o_ref[...] = _nn(a_ref[...], b_ref[...].astype(BF16)) + c_ref[...]

    return pl.pallas_call(
        body, name=name, out_shape=_sds((DEPTH, 32, wm), F32), grid=(DEPTH, wm // tn),
        in_specs=[pl.BlockSpec((32, D), lambda l, j: (0, 0)), pl.BlockSpec((None, D, tn), lambda l, j: (l, 0, j)),
                  pl.BlockSpec((None, 1, tn), lambda l, j: (l, 0, j))],
        out_specs=pl.BlockSpec((None, 32, tn), lambda l, j: (l, 0, j)),
        compiler_params=_cp(("parallel", "parallel")))(sc, w_mod, bias)


def wmod_dw(sc, dcol, name):
    wm = dcol.shape[-1]
    tm = 256

    def body(a_ref, b_ref, o_ref):
        o_ref[...] = _tn(a_ref[...], b_ref[...].astype(BF16))

    return pl.pallas_call(
        body, name=name, out_shape=_sds((DEPTH, D, wm), F32), grid=(DEPTH, D // tm),
        in_specs=[pl.BlockSpec((32, tm), lambda l, i: (0, i)), pl.BlockSpec((None, 32, wm), lambda l, i: (l, 0, 0))],
        out_specs=pl.BlockSpec((None, tm, wm), lambda l, i: (l, i, 0)),
        compiler_params=_cp(("parallel", "parallel")))(sc, dcol)


def cctx_dx(drow, w_mod, name):
    wm = w_mod.shape[-1]

    def body(a_ref, b_ref, o_ref):
        o_ref[...] = _nt(a_ref[...].astype(BF16), b_ref[...].astype(BF16))

    return pl.pallas_call(
        body, name=name, out_shape=_sds((DEPTH, 16, D), F32), grid=(DEPTH,),
        in_specs=[pl.BlockSpec((None, 16, wm), lambda l: (l, 0, 0)), pl.BlockSpec((None, D, wm), lambda l: (l, 0, 0))],
        out_specs=pl.BlockSpec((None, 16, D), lambda l: (l, 0, 0)), compiler_params=_cp(("parallel",), VMEM_BIG))(drow, w_mod)


HEAD_PERM = (0, 4, 1, 5, 2, 6, 3, 7)


def _rot_rows(wt):
    return jnp.concatenate([-wt[32:64], wt[0:32]], axis=0)


def _unrot_rows(g):
    return jnp.concatenate([g[32:64], -g[0:32]], axis=0)


def _heads(a, n):
    return [a[64 * i:64 * (i + 1)] for i in range(n)]


def kernel(x, c, ctx, c_ctx, w_mod, b_mod, ln_g, ln_b, ffn_w_gate, ffn_w_up, ffn_w_down, mix_ab_w_in, attn_sink, pool_w, pool_scale, mix_ab_w_out, lru_w_in, lru_conv_w, lru_conv_b, lru_wa, lru_ba, lru_wx, lru_bx, lru_lambda, lru_w_out, loss_target, m_c_ctx, m_w_mod, m_b_mod, m_ln_g, m_ln_b, m_ffn_w_gate, m_ffn_w_up, m_ffn_w_down, m_mix_ab_w_in, m_attn_sink, m_pool_w, m_pool_scale, m_mix_ab_w_out, m_lru_w_in, m_lru_conv_w, m_lru_conv_b, m_lru_wa, m_lru_ba, m_lru_wx, m_lru_bx, m_lru_lambda, m_lru_w_out, v_c_ctx, v_w_mod, v_b_mod, v_ln_g, v_ln_b, v_ffn_w_gate, v_ffn_w_up, v_ffn_w_down, v_mix_ab_w_in, v_attn_sink, v_pool_w, v_pool_scale, v_mix_ab_w_out, v_lru_w_in, v_lru_conv_w, v_lru_conv_b, v_lru_wa, v_lru_ba, v_lru_wx, v_lru_bx, v_lru_lambda, v_lru_w_out):
    n_lat, n_ctx = x.shape[1], ctx.shape[1]
    lay = Layout(n_ctx, n_lat)
    T = lay.T
    ns = ffn_w_gate.shape[-1]
    n_li, n_ai = lru_w_in.shape[-1], mix_ab_w_in.shape[-1]
    n_ao, n_lo = mix_ab_w_out.shape[1], lru_w_out.shape[1]
    wm = w_mod.shape[-1]
    dsh = ln_g.shape[-1]
    mx, my, mc = lax.axis_index("x"), lax.axis_index("y"), lax.axis_index("c")
    chip = 2 * mx + my
    me = 2 * chip + mc

    c_all = all_gather8(c, "ag8_c").reshape(16, D)
    cc = jnp.concatenate([c_all, c_ctx[None, :], jnp.zeros((15, D), F32)], axis=0)
    sc = silu_rows(cc, "silu_c")
    bias = lax.dynamic_slice(b_mod, (0, chip * wm), (DEPTH, wm)).reshape(DEPTH, 1, wm)
    modg = all_gather_chips(mod_mm(sc, w_mod, bias, "mod_mm"), "ag_mod")
    modtab = []
    for l in range(DEPTH):
        full = jnp.transpose(modg[:, l], (1, 0, 2)).reshape(32, N_CHIP * wm)
        mine = lax.dynamic_slice(full, (2 * me, 0), (2, N_CHIP * wm))
        modtab.append(jnp.concatenate([mine, full[16:17]], axis=0).reshape(3, N_MOD, D))

    small = jnp.concatenate([ln_g.reshape(6, dsh), ln_b.reshape(6, dsh), lru_conv_w[0], lru_conv_b, lru_ba[0],
                             lru_bx[0], lru_lambda[0], jnp.zeros((9, dsh), F32)], axis=0)
    small = all_gather_chips(small.reshape(2, 16, dsh), "ag_small").reshape(N_CHIP, 32, dsh)
    small = jnp.transpose(small, (1, 0, 2)).reshape(32, D)
    ln_g_f, ln_b_f = small[0:6].reshape(2, 3, D), small[6:12].reshape(2, 3, D)
    conv_w_f, conv_b_f = small[12:16], small[16:17]
    lru_vec = small[17:23]

    hh = 3 * ns // 2
    placed = [ffn_place(ffn_w_gate, ffn_w_up, ffn_w_down, g // 2, g % 2, f"ag_ffn{g}_place") for g in range(4)]
    wb = [gather_placed(placed[0], "ag_ffn0"), None, None, None]
    mix_sh = jnp.concatenate([lru_w_in[0].T, mix_ab_w_in[0].T, mix_ab_w_out[0], lru_w_out[0]], axis=0).astype(BF16)
    n_mix = n_li + n_ai + n_ao + n_lo
    mixw = all_gather_chips(mix_sh.reshape(2, n_mix // 2, D), "ag_mix").reshape(N_CHIP, n_mix, D)
    o1, o2, o3 = n_li, n_li + n_ai, n_li + n_ai + n_ao
    lru_in_t = mixw[:, 0:o1].reshape(N_CHIP * n_li, D)
    ab_in_t = mixw[:, o1:o2].reshape(N_CHIP * n_ai, D)
    ab_out = mixw[:, o2:o3].reshape(N_CHIP * n_ao, D)
    lru_out = mixw[:, o3:].reshape(N_CHIP * n_lo, D)
    qh, kh = _heads(ab_in_t[Q0:K0], N_HEADS), _heads(ab_in_t[K0:V0], N_KV)
    w_ext_t = jnp.concatenate([qh[h] for h in HEAD_PERM] + [ab_in_t[K0:QR0]]
                              + [_rot_rows(qh[h]) for h in HEAD_PERM] + [_rot_rows(t) for t in kh], axis=0)
    oh = _heads(ab_out[0:ATT_W], N_HEADS)
    w_out_ext = jnp.concatenate([oh[h] for h in HEAD_PERM] + [ab_out[ATT_W:]], axis=0)

    t = jnp.arange(n_lat)
    inv = ROPE_THETA ** (-jnp.arange(16, dtype=F32) / 16.0)
    ang = jnp.concatenate([(t // GRID_W).astype(F32)[:, None] * inv, (t % GRID_W).astype(F32)[:, None] * inv], axis=-1)
    cos1 = jnp.concatenate([jnp.ones((n_ctx, 32), F32), jnp.cos(ang)], axis=0)
    sin1 = jnp.concatenate([jnp.zeros((n_ctx, 32), F32), jnp.sin(ang)], axis=0)
    cos_t = jnp.tile(cos1, (2, 4))
    sin_t = jnp.tile(sin1, (2, 4))
    sk = attn_sink[0]
    sink_tab = jnp.concatenate([jnp.repeat(jnp.stack([sk[:4], sk[4:]], axis=1), HEAD_DIM, axis=1),
                                jnp.zeros((4, 128), F32)], axis=0)
    pscale = pool_scale.reshape(1, POOL_W)

    h0 = jnp.concatenate([ctx, x], axis=1).reshape(T, D)
    tgt = loss_target.reshape(2 * n_lat, D)

    def lnv(l, j):
        return jnp.stack([ln_g_f[l, j], ln_b_f[l, j]])

    subs = [(0, 0, 0.5, 0), (0, 3, 1.0, 1), (0, 6, 0.5, 2), (1, 0, 0.5, 0), (1, 3, 1.0, 1), (1, 6, 0.5, 2)]

    def ffn_core(hm, l, f):
        tag = f"l{l}f{f}"
        gi = 2 * l + f
        w = wb[gi].reshape(N_CHIP, 3 * ns, D)
        if gi == 3:
            g, u, a = ffn_up(lay, hm, w, 0, 1, ns, f"ffn_up_{tag}")
            (y,) = slab_nn_acc(lay, [a], w, [2], ns, f"ffn_down_{tag}")
        else:
            g, u, a, nbuf = ffn_up(lay, hm, w, 0, 1, ns, f"ffn_up_{tag}", rider=rider_gather_ici(placed[gi + 1]))
            y, wb[gi + 1] = slab_nn_acc(lay, [a], w, [2], ns, f"ffn_down_{tag}", rider=rider_gather_d2d(nbuf))
        return y, dict(g=g, u=u, a=a)

    def mixa_core(hm):
        p = mm_nt(hm, w_ext_t, "mixa_in")
        qr, kr, vb, u = rope_fwd(lay, p, cos_t, sin_t, "rope")
        att, lse = attn_fwd(lay, qr, kr, vb, sink_tab, "attn")
        pool = pool_fwd(lay, u, pool_w[0], pscale, "pool")
        cat = jnp.concatenate([att, pool], axis=1)
        return mm_nn(cat, w_out_ext, "mixa_out"), dict(qr=qr, kr=kr, vb=vb, u=u, lse=lse, cat=cat)

    def mixc_core(hm):
        p = mm_nt(hm, lru_in_t, "mixc_in")
        uc = conv_fwd(lay, p, D, conv_w_f, conv_b_f, "conv")
        a, b = lru_coeffs(lay, uc, lru_wa[0], lru_wx[0], lru_vec, "lru_coef")
        s = lru_scan(lay, a, b, "lru_scan")
        o = lru_gate(lay, p, s, "lru_gate")
        return mm_nn(o, lru_out, "mixc_out"), dict(p=p, uc=uc, a=a, s=s, o=o)

    recs = []
    h = h0
    hm = modulate(lay, h0, modtab[0], 0, 1, "mod_first")
    for k, (l, k0, coef, j) in enumerate(subs):
        if k0 == 3:
            y, core = mixa_core(hm) if l == 0 else mixc_core(hm)
        else:
            y, core = ffn_core(hm, l, k0 // 6)
        nxt = None if k == 5 else (modtab[subs[k + 1][0]], subs[k + 1][1], subs[k + 1][1] + 1)
        res = resid_ln(lay, h, y, modtab[l], k0 + 2, coef, lnv(l, j), f"ln_s{k}", nxt=nxt)
        recs.append(dict(h=h, hm=hm, y=y, xhat=res[1], rstd=res[2], **core))
        h = res[0]
        hm = res[3] if nxt is not None else None

    dout, lparts = loss_and_grad(lay, h, tgt, "loss")
    loss = lax.psum(jnp.sum(lparts), ("x", "y", "c"))

    dln = {}
    dms = {}
    mixg = {}
    ffn_red = [lax.empty((4, 2, hh, D), F32)]
    pending = []

    def ffn_core_bwd(dy, r, l, f):
        tag = f"l{l}f{f}"
        gi = 2 * l + f
        w = wb[gi].reshape(N_CHIP, 3 * ns, D)
        prev = pending.pop() if pending else None
        if prev is None:
            dg, du = ffn_bwd_da(lay, dy, w, 2, r["g"], r["u"], ns, f"ffn_da_{tag}")
        else:
            dg, du, recv = ffn_bwd_da(lay, dy, w, 2, r["g"], r["u"], ns, f"ffn_da_{tag}", rider=rider_reduce_sib(prev[1]))
        gb = lax.empty((N_CHIP, 3 * ns, D), F32)
        gb = slab_tn(lay, r["a"], dy, gb, 2, ns, f"ffn_dwd_{tag}")
        gb = slab_tn(lay, dg, r["hm"], gb, 0, ns, f"ffn_dwg_{tag}")
        gb = slab_tn(lay, du, r["hm"], gb, 1, ns, f"ffn_dwu_{tag}")
        if prev is None:
            (dhm,) = slab_nn_acc(lay, [dg, du], w, [0, 1], ns, f"ffn_dh_{tag}")
        else:
            q = add_own_half(prev[1], recv, BF16, f"rs_add2_ffn{prev[0]}")
            dhm, arr = slab_nn_acc(lay, [dg, du], w, [0, 1], ns, f"ffn_dh_{tag}", rider=rider_reduce_a2a(q))
            red = sum_slots(q, arr, f"rs_add4_ffn{prev[0]}", dst=ffn_red[0], g=prev[0])
            ffn_red[0] = sibling_join_halves(red, f"rs_join_ffn{prev[0]}", g=prev[0])
        pending.append((gi, gb.reshape(N_CHIP, 2, hh, D)))
        return dhm

    def mixc_core_bwd(dy, r):
        do_c = mm_nt(dy, lru_out, "mixc_out_dx")
        mixg["lru_out"] = mm_tn(r["o"], dy, "mixc_out_dw")
        dgate, dyg = lru_gate_bwd(lay, r["p"], r["s"], do_c, "lru_gate_b")
        da_c, db_c = lru_scan_bwd(lay, r["a"], r["s"], dyg, "lru_scan_b")
        duc, mixg["wa"], mixg["wx"], mixg["vec"] = lru_coeffs_bwd(lay, r["uc"], lru_wa[0], lru_wx[0], lru_vec, da_c, db_c,
                                                                  "lru_coef_b")
        du_c, mixg["cw"], mixg["cb"] = conv_bwd(lay, r["p"], D, conv_w_f, duc, "conv_b")
        dp_c = jnp.concatenate([dgate, du_c], axis=1).astype(BF16)
        mixg["lru_in_t"] = mm_tn(dp_c, r["hm"], "mixc_in_dw")
        return mm_nn(dp_c, lru_in_t, "mixc_in_dx")

    def mixa_core_bwd(dy, r):
        dcat = mm_nt(dy, w_out_ext, "mixa_out_dx")
        mixg["out_ext"] = mm_tn(r["cat"], dy, "mixa_out_dw")
        dqr, dkr, dv, mixg["sink"] = attn_bwd(lay, r["qr"], r["kr"], r["vb"], sink_tab, r["lse"], dcat, "attn_b")
        du_a, mixg["pw"], mixg["ps"] = pool_bwd(lay, r["u"], dcat, pool_w[0], pscale, "pool_b")
        dp_a = rope_bwd(lay, dqr, dkr, dv, du_a, cos_t, sin_t, "rope_b")
        mixg["ext_t"] = mm_tn(dp_a, r["hm"], "mixa_in_dw")
        return mm_nn(dp_a, w_ext_t, "mixa_in_dx")

    l, k0, coef, j = subs[5]
    dy, dres, s1 = ln_bwd(lay, dout, recs[5]["xhat"], recs[5]["rstd"], recs[5]["y"], modtab[l], k0 + 2, coef, lnv(l, j),
                          "lnb_s5")
    for k in range(5, -1, -1):
        l, k0, coef, j = subs[k]
        r = recs[k]
        if k0 == 3:
            dhm = mixa_core_bwd(dy, r) if l == 0 else mixc_core_bwd(dy, r)
        else:
            dhm = ffn_core_bwd(dy, r, l, k0 // 6)
        dln[(l, j)] = block_sums(lay, s1, f"bs_ln_s{k}")
        if k > 0:
            lp, k0p, coefp, jp = subs[k - 1]
            rp = recs[k - 1]
            dy, dres, s1, s2 = modb_lnb(lay, dres, dhm, r["h"], modtab[l], k0 + 1, rp["xhat"], rp["rstd"], rp["y"],
                                        modtab[lp], k0p + 2, coefp, lnv(lp, jp), f"modb_lnb_s{k}")
        else:
            gx, s2 = mod_bwd(lay, dres, dhm, r["h"], modtab[l], k0 + 1, "modb_s0")
        dms[(l, k0)] = block_sums(lay, s2, f"bs_mod_s{k}")
    grad_x = gx.reshape(2, n_lat, D)
    g_lru_out, g_wa, g_wx, g_vec, g_cw, g_cb = (mixg[n] for n in ("lru_out", "wa", "wx", "vec", "cw", "cb"))
    g_lru_in_t, g_out_ext, g_sink, g_pw, g_ps, g_ext_t = (mixg[n] for n in ("lru_in_t", "out_ext", "sink", "pw", "ps", "ext_t"))

    rows = []
    for l in range(DEPTH):
        per_k = []
        for k0, j in ((0, 0), (3, 1), (6, 2)):
            per_k += [dms[(l, k0)][:3, 0], dms[(l, k0)][:3, 1], dln[(l, j)][:3, 2]]
        rows.append(jnp.stack(per_k, axis=1).reshape(3, N_MOD * D))
    dmod_loc = jnp.concatenate(rows + [jnp.zeros((2, N_MOD * D), F32)], axis=0)
    dmod_all, g_b_mod = mod_grad_rows(all_gather8(dmod_loc, "ag8_dmod"), "dmod_rows")
    dcol = lax.dynamic_slice(dmod_all, (0, 0, chip * wm), (DEPTH, 32, wm))
    g_w_mod = wmod_dw(sc, dcol, "wmod_dw")
    g_cctx = cctx_grad(cctx_dx(dcol[:, 16:32], w_mod, "cctx_dx"), c_ctx[None, :], "cctx_grad")

    gq = _heads(g_ext_t[Q0:K0], N_HEADS)
    gqr = _heads(g_ext_t[QR0:KR0], N_HEADS)
    g_q = [None] * N_HEADS
    for i, h in enumerate(HEAD_PERM):
        g_q[h] = gq[i] + _unrot_rows(gqr[i])
    gk = [a + _unrot_rows(b) for a, b in zip(_heads(g_ext_t[K0:V0], N_KV), _heads(g_ext_t[KR0:PEXT], N_KV))]
    g_ab_in_t = jnp.concatenate(g_q + gk + [g_ext_t[V0:QR0]], axis=0)
    go = _heads(g_out_ext[0:ATT_W], N_HEADS)
    g_o = [None] * N_HEADS
    for i, h in enumerate(HEAD_PERM):
        g_o[h] = go[i]
    g_ab_out = jnp.concatenate(g_o + [g_out_ext[ATT_W:]], axis=0)
    mix_g = jnp.concatenate([g_lru_in_t.reshape(N_CHIP, n_li, D), g_ab_in_t.reshape(N_CHIP, n_ai, D),
                             g_ab_out.reshape(N_CHIP, n_ao, D), g_lru_out.reshape(N_CHIP, n_lo, D)], axis=1)

    g_ln_g = jnp.stack([jnp.stack([dln[(l, j)][3, 1] for j in range(3)]) for l in range(DEPTH)])
    g_ln_b = jnp.stack([jnp.stack([dln[(l, j)][3, 0] for j in range(3)]) for l in range(DEPTH)])
    sink_row = jnp.sum(g_sink, axis=0)[:4]
    g_sink8 = jnp.concatenate([sink_row[:, 0], sink_row[:, HEAD_DIM]])
    misc = jnp.concatenate([g_sink8, jnp.sum(g_ps, axis=0).reshape(POOL_W), jnp.zeros((D - 8 - POOL_W,), F32)])
    small_g = jnp.concatenate([
        g_ln_g.reshape(6, D), g_ln_b.reshape(6, D), jnp.sum(g_cw, axis=0), jnp.sum(g_cb, axis=0), g_vec,
        misc[None, :], jnp.sum(g_pw, axis=0).reshape(64, D), g_wa.reshape(256, D), g_wx.reshape(256, D), g_cctx,
        jnp.zeros((39, D), F32)], axis=0)
    n_small = small_g.shape[0] // N_CHIP
    mix_buf = jnp.concatenate([mix_g, small_g.reshape(N_CHIP, n_small, D)], axis=1)
    n_mb = n_mix + n_small

    last_g, last_buf = pending.pop()
    ffn_red = reduce_scatter_chips(last_buf, f"ffn{last_g}", wire=BF16, dst=ffn_red[0], g=last_g).reshape(12 * ns, D)
    mix_red = reduce_scatter_chips(mix_buf.reshape(N_CHIP, 2, n_mb // 2, D), "mix").reshape(n_mb, D)
    small_red = all_gather_chips(mix_red[n_mix:].reshape(2, n_small // 2, D), "ag_smallg").reshape(N_CHIP * n_small, D)

    ffn_kind = dict(ffn_w_gate=0, ffn_w_up=1, ffn_w_down=2)

    def cols(a):
        return lax.dynamic_slice_in_dim(a, chip * dsh, dsh, axis=a.ndim - 1)

    sr = small_red
    grads = dict(
        c_ctx=sr[600], w_mod=g_w_mod, b_mod=g_b_mod,
        ln_g=cols(sr[0:6]).reshape(2, 3, dsh), ln_b=cols(sr[6:12]).reshape(2, 3, dsh),
        mix_ab_w_in=mix_red[o1:o2].T[None], attn_sink=sr[23, 0:8][None], pool_w=sr[24:88].reshape(1, 4, 128, 128),
        pool_scale=sr[23, 8:8 + POOL_W][None], mix_ab_w_out=mix_red[o2:o3][None], lru_w_in=mix_red[0:o1].T[None],
        lru_conv_w=cols(sr[12:16])[None], lru_conv_b=cols(sr[16:17]), lru_wa=sr[88:344].reshape(1, 2, 8, 128, 128),
        lru_ba=cols(sr[17:19])[None], lru_wx=sr[344:600].reshape(1, 2, 8, 128, 128), lru_bx=cols(sr[19:21])[None],
        lru_lambda=cols(sr[21:23])[None], lru_w_out=mix_red[o3:n_mix][None])
    params = dict(c_ctx=(c_ctx, m_c_ctx, v_c_ctx), w_mod=(w_mod, m_w_mod, v_w_mod), b_mod=(b_mod, m_b_mod, v_b_mod),
                  ln_g=(ln_g, m_ln_g, v_ln_g), ln_b=(ln_b, m_ln_b, v_ln_b),
                  ffn_w_gate=(ffn_w_gate, m_ffn_w_gate, v_ffn_w_gate), ffn_w_up=(ffn_w_up, m_ffn_w_up, v_ffn_w_up),
                  ffn_w_down=(ffn_w_down, m_ffn_w_down, v_ffn_w_down),
                  mix_ab_w_in=(mix_ab_w_in, m_mix_ab_w_in, v_mix_ab_w_in), attn_sink=(attn_sink, m_attn_sink, v_attn_sink),
                  pool_w=(pool_w, m_pool_w, v_pool_w), pool_scale=(pool_scale, m_pool_scale, v_pool_scale),
                  mix_ab_w_out=(mix_ab_w_out, m_mix_ab_w_out, v_mix_ab_w_out), lru_w_in=(lru_w_in, m_lru_w_in, v_lru_w_in),
                  lru_conv_w=(lru_conv_w, m_lru_conv_w, v_lru_conv_w), lru_conv_b=(lru_conv_b, m_lru_conv_b, v_lru_conv_b),
                  lru_wa=(lru_wa, m_lru_wa, v_lru_wa), lru_ba=(lru_ba, m_lru_ba, v_lru_ba), lru_wx=(lru_wx, m_lru_wx, v_lru_wx),
                  lru_bx=(lru_bx, m_lru_bx, v_lru_bx), lru_lambda=(lru_lambda, m_lru_lambda, v_lru_lambda),
                  lru_w_out=(lru_w_out, m_lru_w_out, v_lru_w_out))
    gl, dl, ml, vl = [], [], [], []
    for name, (w, m, v) in params.items():
        if name in ffn_kind:
            g, d, mn, vn = adamw_ffn(w, m, v, ffn_red, ffn_kind[name], ns, f"adamw_{name}")
        else:
            g = grads[name].reshape(w.shape)
            d, mn, vn = adamw(w, g, m, v, f"adamw_{name}")
        gl.append(g)
        dl.append(d)
        ml.append(mn)
        vl.append(vn)
    return (loss, grad_x, *gl, *dl, *ml, *vl)
```

```python
import functools
import math

import jax
import jax.numpy as jnp
from jax import lax
from jax.experimental import pallas as pl
from jax.experimental.pallas import tpu as pltpu

F32, BF16 = jnp.float32, jnp.bfloat16
MESH = pl.DeviceIdType.MESH
ANY = pl.BlockSpec(memory_space=pl.ANY)
VMEM_SPEC = pl.BlockSpec(memory_space=pltpu.VMEM)

D = 1024
N_CHIP = 4
HEAD_DIM, N_HEADS, N_KV = 64, 8, 2
ATT_W, KV_W, POOL_W = 512, 128, 512
POOL_WINDOWS = (2, 4, 8, 16)
BLK = 128
ATT_SCALE = HEAD_DIM ** -0.5
ROPE_THETA = 10000.0
GRID_W = 64
LRU_C = 8.0
LN_EPS = 1e-5
NEG_INF = -1e30
DEPTH = 2
ALPHA = (2 * DEPTH) ** 0.25
N_MOD = 9
ADAM_LR, ADAM_B1, ADAM_B2, ADAM_EPS, ADAM_WD, ADAM_STEP = 0.001, 0.9, 0.999, 1e-08, 0.01, 10
VMEM_BIG = 48 * 1024 * 1024


def _cp(sem=None, vmem=None):
    kw = {}
    if sem is not None:
        kw["dimension_semantics"] = sem
    if vmem is not None:
        kw["vmem_limit_bytes"] = vmem
    return pltpu.CompilerParams(**kw)


def _sds(shape, dtype):
    return jax.ShapeDtypeStruct(tuple(shape), dtype)


def _pick(n, cands):
    for c in cands:
        if n % c == 0:
            return c
    return n


def _dot(a, b, dims):
    return lax.dot_general(a, b, (dims, ((), ())), preferred_element_type=F32)


def _nn(a, b):
    return _dot(a, b, ((1,), (0,)))


def _nt(a, b):
    return _dot(a, b, ((1,), (1,)))


def _tn(a, b):
    return _dot(a, b, ((0,), (0,)))


def _sigmoid(x):
    return 0.5 * jnp.tanh(0.5 * x) + 0.5


def _me():
    return lax.axis_index("x"), lax.axis_index("y"), lax.axis_index("c")


def _rcopy(src, dst, ssem, rsem, dev):
    return pltpu.make_async_remote_copy(src_ref=src, dst_ref=dst, send_sem=ssem, recv_sem=rsem,
                                        device_id=dev, device_id_type=MESH)


def all_gather8(x, name):
    def body(x_ref, o_ref, ssem, rsem, lsem):
        mx, my, mc = _me()
        me = 4 * mx + 2 * my + mc
        loc = pltpu.make_async_copy(x_ref, o_ref.at[me], lsem)
        loc.start()
        peers = []
        for m in range(1, 8):
            px = 1 - mx if (m >> 2) & 1 else mx
            py = 1 - my if (m >> 1) & 1 else my
            pc = 1 - mc if m & 1 else mc
            peers.append((px, py, pc))
        sends = [_rcopy(x_ref, o_ref.at[me], ssem.at[k], rsem.at[k], p) for k, p in enumerate(peers)]
        for cp in sends:
            cp.start()
        for k, (px, py, pc) in enumerate(peers):
            _rcopy(x_ref, o_ref.at[4 * px + 2 * py + pc], ssem.at[k], rsem.at[k], (px, py, pc)).wait_recv()
        for cp in sends:
            cp.wait_send()
        loc.wait()

    return pl.pallas_call(
        body, name=name, out_shape=_sds((8,) + x.shape, x.dtype),
        in_specs=[VMEM_SPEC], out_specs=VMEM_SPEC,
        scratch_shapes=[pltpu.SemaphoreType.DMA((7,)), pltpu.SemaphoreType.DMA((7,)), pltpu.SemaphoreType.DMA],
    )(x)


_ROW_BLOCKS = (512, 384, 352, 256, 224, 128)


def _idx(v):
    return jnp.reshape(v, (1,)).astype(jnp.int32)


def place_slab(shard, name):
    _, h, w = shard.shape
    th = _pick(h, _ROW_BLOCKS)

    def body(s_ref, x_ref, o_ref):
        del s_ref
        o_ref[...] = x_ref[...]

    return pl.pallas_call(
        body, name=name, out_shape=_sds((N_CHIP,) + shard.shape, shard.dtype),
        grid_spec=pltpu.PrefetchScalarGridSpec(
            num_scalar_prefetch=1, grid=(2, h // th),
            in_specs=[pl.BlockSpec((None, th, w), lambda k, r, s: (k, r, 0))],
            out_specs=pl.BlockSpec((None, None, th, w), lambda k, r, s: (s[0], k, r, 0))),
    )(_idx(2 * lax.axis_index("x") + lax.axis_index("y")), shard)


def ffn_place(w_gate, w_up, w_down, l, f, name):
    ns = w_gate.shape[-1]
    tc = 256

    def body(s_ref, g_ref, u_ref, d_ref, o_ref):
        del s_ref
        k = pl.program_id(0)

        @pl.when(k == 0)
        def _():
            o_ref[...] = g_ref[...].T.astype(BF16)

        @pl.when(k == 1)
        def _():
            o_ref[...] = u_ref[...].T.astype(BF16)

        @pl.when(k == 2)
        def _():
            o_ref[...] = d_ref[...].astype(BF16)

    nat = pl.BlockSpec((None, None, tc, ns), lambda k, j, s: (l, f, j, 0))
    out = pl.pallas_call(
        body, name=name, out_shape=_sds((N_CHIP, 3 * ns, D), BF16),
        grid_spec=pltpu.PrefetchScalarGridSpec(
            num_scalar_prefetch=1, grid=(3, D // tc),
            in_specs=[nat, nat, pl.BlockSpec((None, None, ns, tc), lambda k, j, s: (l, f, 0, j))],
            out_specs=pl.BlockSpec((None, ns, tc), lambda k, j, s: (s[0], k, j))),
    )(_idx(2 * lax.axis_index("x") + lax.axis_index("y")), w_gate, w_up, w_down)
    return out.reshape(N_CHIP, 2, 3 * ns // 2, D)


def all_gather_chips(shard, name):
    return gather_placed(place_slab(shard, name + "_place"), name)


def gather_placed(full, name):
    def body(x_ref, o_ref, ssem, rsem):
        del x_ref
        mx, my, mc = _me()
        s = 2 * mx + my
        sib = (mx, my, 1 - mc)
        chips = [(1 - mx, my), (mx, 1 - my), (1 - mx, 1 - my)]
        first = [_rcopy(o_ref.at[s, mc], o_ref.at[s, mc], ssem.at[j], rsem.at[j], (px, py, mc))
                 for j, (px, py) in enumerate(chips)]
        for cp in first:
            cp.start()
        passed = []
        for j, (px, py) in enumerate(chips):
            ps = 2 * px + py
            _rcopy(o_ref.at[ps, mc], o_ref.at[ps, mc], ssem.at[j], rsem.at[j], (px, py, mc)).wait_recv()
            fw = _rcopy(o_ref.at[ps, mc], o_ref.at[ps, mc], ssem.at[3 + j], rsem.at[3 + j], sib)
            fw.start()
            passed.append(fw)
        for j, (px, py) in enumerate(chips):
            ps = 2 * px + py
            _rcopy(o_ref.at[ps, 1 - mc], o_ref.at[ps, 1 - mc], ssem.at[3 + j], rsem.at[3 + j], sib).wait_recv()
        for cp in first + passed:
            cp.wait_send()

    return pl.pallas_call(
        body, name=name, out_shape=_sds(full.shape, full.dtype), in_specs=[ANY], out_specs=ANY,
        input_output_aliases={0: 0},
        scratch_shapes=[pltpu.SemaphoreType.DMA((6,)), pltpu.SemaphoreType.DMA((6,))],
    )(full)


def sibling_send_other_half(buf, name):
    def body(x_ref, o_ref, ssem, rsem):
        mx, my, mc = _me()
        sib = (mx, my, 1 - mc)
        cps = [_rcopy(x_ref.at[k, 1 - mc], o_ref.at[k], ssem.at[k], rsem.at[k], sib) for k in range(N_CHIP)]
        for cp in cps:
            cp.start()
        for cp in cps:
            cp.wait_recv()
        for cp in cps:
            cp.wait_send()

    n, _, h, w = buf.shape
    return pl.pallas_call(
        body, name=name, out_shape=_sds((n, h, w), buf.dtype), in_specs=[ANY], out_specs=ANY,
        scratch_shapes=[pltpu.SemaphoreType.DMA((N_CHIP,)), pltpu.SemaphoreType.DMA((N_CHIP,))],
    )(buf)


def chips_all_to_all(q, name):
    def body(x_ref, o_ref, ssem, rsem):
        mx, my, mc = _me()
        s = 2 * mx + my
        chips = [(1 - mx, my), (mx, 1 - my), (1 - mx, 1 - my)]
        cps = [_rcopy(x_ref.at[2 * px + py], o_ref.at[s], ssem.at[j], rsem.at[j], (px, py, mc))
               for j, (px, py) in enumerate(chips)]
        for cp in cps:
            cp.start()
        for j, (px, py) in enumerate(chips):
            ps = 2 * px + py
            _rcopy(x_ref.at[ps], o_ref.at[ps], ssem.at[j], rsem.at[j], (px, py, mc)).wait_recv()
        for cp in cps:
            cp.wait_send()

    return pl.pallas_call(
        body, name=name, out_shape=_sds(q.shape, q.dtype), in_specs=[ANY], out_specs=ANY,
        scratch_shapes=[pltpu.SemaphoreType.DMA((3,)), pltpu.SemaphoreType.DMA((3,))],
    )(q)


def sibling_join_halves(both, name, g=None):
    def body(x_ref, o_ref, ssem, rsem):
        del x_ref
        mx, my, mc = _me()
        sib = (mx, my, 1 - mc)
        o = o_ref if g is None else o_ref.at[g]
        cp = _rcopy(o.at[mc], o.at[mc], ssem, rsem, sib)
        cp.start()
        _rcopy(o.at[1 - mc], o.at[1 - mc], ssem, rsem, sib).wait_recv()
        cp.wait_send()

    return pl.pallas_call(
        body, name=name, out_shape=_sds(both.shape, both.dtype), in_specs=[ANY], out_specs=ANY,
        input_output_aliases={0: 0}, scratch_shapes=[pltpu.SemaphoreType.DMA, pltpu.SemaphoreType.DMA],
    )(both)


def add_own_half(buf, recv, wire, name):
    n, _, h, w = buf.shape
    th = _pick(h, _ROW_BLOCKS)

    def body(c_ref, a_ref, b_ref, o_ref):
        del c_ref
        o_ref[...] = (a_ref[...] + b_ref[...]).astype(o_ref.dtype)

    return pl.pallas_call(
        body, name=name, out_shape=_sds((n, h, w), wire),
        grid_spec=pltpu.PrefetchScalarGridSpec(
            num_scalar_prefetch=1, grid=(n, h // th),
            in_specs=[pl.BlockSpec((None, None, th, w), lambda k, r, c: (k, c[0], r, 0)),
                      pl.BlockSpec((None, th, w), lambda k, r, c: (k, r, 0))],
            out_specs=pl.BlockSpec((None, th, w), lambda k, r, c: (k, r, 0))),
    )(_idx(lax.axis_index("c")), buf, recv)


def sum_slots(q, r, name, dst=None, g=None):
    n, h, w = r.shape
    th = _pick(h, _ROW_BLOCKS)

    def body(i_ref, q_ref, r1, r2, r3, *rest):
        del i_ref
        rest[-1][...] = ((q_ref[...].astype(F32) + r1[...].astype(F32)) + r2[...].astype(F32)) + r3[...].astype(F32)

    def slot(d):
        return lambda i, ix: ((ix[0] + d) % N_CHIP, i, 0)

    idx = jnp.stack([2 * lax.axis_index("x") + lax.axis_index("y"), lax.axis_index("c")]).astype(jnp.int32)
    in_specs = [pl.BlockSpec((None, th, w), slot(d)) for d in (0, 1, 2, 3)]
    if dst is None:
        return pl.pallas_call(
            body, name=name, out_shape=_sds((2, h, w), F32),
            grid_spec=pltpu.PrefetchScalarGridSpec(
                num_scalar_prefetch=1, grid=(h // th,), in_specs=in_specs,
                out_specs=pl.BlockSpec((None, th, w), lambda i, ix: (ix[1], i, 0))),
        )(idx, q, r, r, r)
    return pl.pallas_call(
        body, name=name, out_shape=_sds(dst.shape, F32),
        grid_spec=pltpu.PrefetchScalarGridSpec(
            num_scalar_prefetch=1, grid=(h // th,), in_specs=in_specs + [ANY],
            out_specs=pl.BlockSpec((None, None, th, w), lambda i, ix: (g, ix[1], i, 0))),
        input_output_aliases={5: 0},
    )(idx, q, r, r, r, dst)


def reduce_scatter_chips(buf, tag, wire=F32, dst=None, g=None):
    recv = sibling_send_other_half(buf, f"rs_sib_{tag}")
    q = add_own_half(buf, recv, wire, f"rs_add2_{tag}")
    r = chips_all_to_all(q, f"rs_a2a_{tag}")
    red = sum_slots(q, r, f"rs_add4_{tag}", dst=dst, g=g)
    return sibling_join_halves(red, f"rs_join_{tag}", g=g)


class Layout:
    def __init__(self, n_ctx, n_lat):
        self.C, self.L = n_ctx, n_lat
        self.PS = n_ctx + n_lat
        self.T = 2 * self.PS
        self.tr = _pick(math.gcd(n_ctx, n_lat), (256, 128))
        self.bps = self.PS // self.tr
        self.cb = n_ctx // self.tr
        self.nblk = self.T // self.tr
        self.tm = _pick(self.T, (1152, 768, 512, 256, 128))

    def seg(self, i):
        return jnp.where(i % self.bps < self.cb, 2, i // self.bps)


def rowwise(lay, name, fn, rows, segs=(), vecs=(), outs=(), sums=(), rider=None):
    tr, nblk = lay.tr, lay.nblk
    n_r, n_s, n_v, n_o = len(rows), len(segs), len(vecs), len(outs)

    def body(*refs):
        ins = refs[:n_r + n_s + n_v]
        ors = refs[n_r + n_s + n_v:]
        vals = [r[...] for r in ins[:n_r]] + [r[0] for r in ins[n_r:n_r + n_s]] + [r[...] for r in ins[n_r + n_s:]]
        res = fn(*vals)
        for k in range(n_o):
            ors[k][...] = res[k].astype(ors[k].dtype)
        for k in range(len(sums)):
            ors[n_o + k][0] = res[n_o + k]

    def all_rows(i):
        return (i, 0)

    def lat_rows(i):
        return ((i // lay.bps) * (lay.bps - lay.cb) + jnp.maximum(i % lay.bps - lay.cb, 0), 0)

    in_specs = [pl.BlockSpec((tr, a.shape[1]), all_rows if a.shape[0] == lay.T else lat_rows) for a in rows]
    in_specs += [pl.BlockSpec((1,) + a.shape[1:], lambda i: (lay.seg(i), 0, 0)) for a in segs]
    in_specs += [pl.BlockSpec(a.shape, lambda i: (0, 0)) for a in vecs]
    out_shape = [_sds((2 * lay.L if o[2:] else lay.T, o[0]), o[1]) for o in outs]
    out_shape += [_sds((nblk, r, w), F32) for r, w in sums]
    out_specs = [pl.BlockSpec((tr, o[0]), lat_rows if o[2:] else all_rows) for o in outs]
    out_specs += [pl.BlockSpec((1, r, w), lambda i: (i, 0, 0)) for r, w in sums]
    sem = "arbitrary" if any(o[2:] for o in outs) else "parallel"
    if rider is None:
        return pl.pallas_call(body, name=name, out_shape=out_shape, grid=(nblk,), in_specs=in_specs,
                              out_specs=out_specs, compiler_params=_cp((sem,)))(*rows, *segs, *vecs)
    return _host_call(body, rider, name, (nblk,), in_specs, out_specs, out_shape, (*rows, *segs, *vecs), (sem,),
                      n_r + n_s + n_v, n_o + len(sums))


def modulate(lay, h, mod, k_shift, k_scale, name):
    def fn(hb, m):
        return (hb * (1.0 + m[k_scale:k_scale + 1]) + m[k_shift:k_shift + 1],)
    return rowwise(lay, name, fn, [h], segs=[mod], outs=[(D, BF16)])[0]


def resid_ln(lay, h, y, mod, k_gate, coef, lnv, name, nxt=None, rider=None):
    def fn(hb, yb, m, *rest):
        ln = rest[-1]
        z = ALPHA * hb + (coef * m[k_gate:k_gate + 1]) * yb
        mu = jnp.mean(z, axis=-1, keepdims=True)
        zc = z - mu
        var = jnp.mean(zc * zc, axis=-1, keepdims=True)
        rstd = lax.rsqrt(var + LN_EPS)
        xhat = zc * rstd
        out = xhat * ln[0:1] + ln[1:2]
        if nxt is None:
            return out, xhat, rstd
        mn = rest[0]
        return out, xhat, rstd, out * (1.0 + mn[nxt[2]:nxt[2] + 1]) + mn[nxt[1]:nxt[1] + 1]
    segs = [mod] if nxt is None else [mod, nxt[0]]
    outs = [(D, F32), (D, F32), (1, F32)] + ([] if nxt is None else [(D, BF16)])
    return rowwise(lay, name, fn, [h, y], segs=segs, vecs=[lnv], outs=outs, rider=rider)


def _ln_bwd_math(do, xh, rs, yb, gate, coef, ln):
    dxh = do * ln[0:1]
    m1 = jnp.mean(dxh, axis=-1, keepdims=True)
    m2 = jnp.mean(dxh * xh, axis=-1, keepdims=True)
    dz = rs * (dxh - m1 - xh * m2)
    s = jnp.concatenate([jnp.sum(do, axis=0, keepdims=True), jnp.sum(do * xh, axis=0, keepdims=True),
                         jnp.sum(coef * dz * yb, axis=0, keepdims=True)], axis=0)
    return (coef * gate) * dz, ALPHA * dz, s


def _mod_bwd_math(dr, dm, hb, scale):
    s = jnp.concatenate([jnp.sum(dm, axis=0, keepdims=True), jnp.sum(dm * hb, axis=0, keepdims=True)], axis=0)
    return dr + dm * (1.0 + scale), s


def ln_bwd(lay, dout, xhat, rstd, y, mod, k_gate, coef, lnv, name):
    def fn(do, xh, rs, yb, m, ln):
        return _ln_bwd_math(do, xh, rs, yb, m[k_gate:k_gate + 1], coef, ln)
    return rowwise(lay, name, fn, [dout, xhat, rstd, y], segs=[mod], vecs=[lnv],
                   outs=[(D, BF16), (D, F32)], sums=[(3, D)])


def mod_bwd(lay, dres, dhm, h, mod, k_scale, name):
    def fn(dr, dm, hb, m):
        return _mod_bwd_math(dr, dm, hb, m[k_scale:k_scale + 1])
    return rowwise(lay, name, fn, [dres, dhm, h], segs=[mod], outs=[(D, F32, "lat")], sums=[(2, D)])


def modb_lnb(lay, dres, dhm, h, mod, k_scale, xhat, rstd, y, mod_p, k_gate, coef, lnv, name):
    def fn(dr, dm, hb, xh, rs, yb, m, mp, ln):
        dh, s2 = _mod_bwd_math(dr, dm, hb, m[k_scale:k_scale + 1])
        dy, dres_p, s1 = _ln_bwd_math(dh, xh, rs, yb, mp[k_gate:k_gate + 1], coef, ln)
        return dy, dres_p, s1, s2
    return rowwise(lay, name, fn, [dres, dhm, h, xhat, rstd, y], segs=[mod, mod_p], vecs=[lnv],
                   outs=[(D, BF16), (D, F32)], sums=[(3, D), (2, D)])


def block_sums(lay, parts, name):
    nblk, r, w = parts.shape

    def body(p_ref, o_ref):
        acc = [None, None, None]
        for i in range(nblk):
            sg = 2 if i % lay.bps < lay.cb else i // lay.bps
            acc[sg] = p_ref[i] if acc[sg] is None else acc[sg] + p_ref[i]
        for k in range(3):
            o_ref[k] = acc[k]
        o_ref[3] = (acc[0] + acc[1]) + acc[2]

    return pl.pallas_call(body, name=name, out_shape=_sds((4, r, w), F32), in_specs=[VMEM_SPEC],
                          out_specs=VMEM_SPEC)(parts)


def mm_nn(a, b, name, out_dtype=F32, bias=None):
    m, k = a.shape
    n = b.shape[1]
    tm = _pick(m, (512, 256, 128, 64, 32, 16, 8))
    tn = _pick(n, (1024, 768, 640, 512, 384, 256, 128))

    def body(*refs):
        if bias is None:
            a_ref, b_ref, o_ref = refs
            o_ref[...] = _nn(a_ref[...].astype(BF16), b_ref[...].astype(BF16)).astype(o_ref.dtype)
        else:
            a_ref, b_ref, c_ref, o_ref = refs
            o_ref[...] = (_nn(a_ref[...].astype(BF16), b_ref[...].astype(BF16)) + c_ref[...]).astype(o_ref.dtype)

    in_specs = [pl.BlockSpec((tm, k), lambda i, j: (i, 0)), pl.BlockSpec((k, tn), lambda i, j: (0, j))]
    ops = [a, b]
    if bias is not None:
        in_specs.append(pl.BlockSpec((1, tn), lambda i, j: (0, j)))
        ops.append(bias)
    return pl.pallas_call(body, name=name, out_shape=_sds((m, n), out_dtype), grid=(m // tm, n // tn),
                          in_specs=in_specs, out_specs=pl.BlockSpec((tm, tn), lambda i, j: (i, j)),
                          compiler_params=_cp(("parallel", "parallel"), VMEM_BIG))(*ops)


def mm_nt(a, b, name, out_dtype=F32):
    m, k = a.shape
    n = b.shape[0]
    tm = _pick(m, (512, 256, 128, 64, 32, 16, 8))
    tn = _pick(n, (1024, 768, 640, 512, 384, 256, 128))

    def body(a_ref, b_ref, o_ref):
        o_ref[...] = _nt(a_ref[...].astype(BF16), b_ref[...].astype(BF16)).astype(o_ref.dtype)

    return pl.pallas_call(body, name=name, out_shape=_sds((m, n), out_dtype), grid=(m // tm, n // tn),
                          in_specs=[pl.BlockSpec((tm, k), lambda i, j: (i, 0)), pl.BlockSpec((tn, k), lambda i, j: (j, 0))],
                          out_specs=pl.BlockSpec((tm, tn), lambda i, j: (i, j)),
                          compiler_params=_cp(("parallel", "parallel"), VMEM_BIG))(a, b)


def mm_tn(a, b, name):
    t, m = a.shape
    n = b.shape[1]
    tk = _pick(t, (512, 256, 128, 64, 32, 16))
    tm = _pick(m, (512, 384, 256, 128))

    def body(a_ref, b_ref, o_ref):
        @pl.when(pl.program_id(1) == 0)
        def _():
            o_ref[...] = jnp.zeros_like(o_ref)
        o_ref[...] += _tn(a_ref[...].astype(BF16), b_ref[...].astype(BF16))

    return pl.pallas_call(body, name=name, out_shape=_sds((m, n), F32), grid=(m // tm, t // tk),
                          in_specs=[pl.BlockSpec((tk, tm), lambda i, k: (k, i)), pl.BlockSpec((tk, n), lambda i, k: (k, 0))],
                          out_specs=pl.BlockSpec((tm, n), lambda i, k: (i, 0)),
                          compiler_params=_cp(("parallel", "arbitrary"), VMEM_BIG))(a, b)


class Rider:
    def __init__(self, ins, outs, aliases, nsem, start, wait):
        self.ins, self.outs, self.aliases, self.nsem, self.start, self.wait = ins, outs, aliases, nsem, start, wait


def _chips_of(mx, my):
    return [(1 - mx, my), (mx, 1 - my), (1 - mx, 1 - my)]


def rider_gather_d2d(buf):
    def start(ins, outs, ssem, rsem):
        o = outs[0]
        mx, my, mc = _me()
        for j, (px, py) in enumerate(_chips_of(mx, my)):
            ps = 2 * px + py
            _rcopy(o.at[ps, mc], o.at[ps, mc], ssem.at[j], rsem.at[j], (mx, my, 1 - mc)).start()

    def wait(ins, outs, ssem, rsem):
        o = outs[0]
        mx, my, mc = _me()
        sib = (mx, my, 1 - mc)
        for j, (px, py) in enumerate(_chips_of(mx, my)):
            ps = 2 * px + py
            _rcopy(o.at[ps, 1 - mc], o.at[ps, 1 - mc], ssem.at[j], rsem.at[j], sib).wait_recv()
        for j, (px, py) in enumerate(_chips_of(mx, my)):
            ps = 2 * px + py
            _rcopy(o.at[ps, mc], o.at[ps, mc], ssem.at[j], rsem.at[j], sib).wait_send()

    return Rider([buf], [_sds(buf.shape, buf.dtype)], {0: 0}, 3, start, wait)


def rider_reduce_sib(buf):
    n, _, h, w = buf.shape

    def start(ins, outs, ssem, rsem):
        mx, my, mc = _me()
        for k in range(N_CHIP):
            _rcopy(ins[0].at[k, 1 - mc], outs[0].at[k], ssem.at[k], rsem.at[k], (mx, my, 1 - mc)).start()

    def wait(ins, outs, ssem, rsem):
        mx, my, mc = _me()
        for k in range(N_CHIP):
            _rcopy(ins[0].at[k, 1 - mc], outs[0].at[k], ssem.at[k], rsem.at[k], (mx, my, 1 - mc)).wait_recv()
        for k in range(N_CHIP):
            _rcopy(ins[0].at[k, 1 - mc], outs[0].at[k], ssem.at[k], rsem.at[k], (mx, my, 1 - mc)).wait_send()

    return Rider([buf], [_sds((n, h, w), buf.dtype)], {}, N_CHIP, start, wait)


def rider_gather_xy(buf):
    def peers():
        mx, my, mc = _me()
        return 2 * mx + my, mc, [(1 - mx, my), (mx, 1 - my)]

    def start(ins, outs, ssem, rsem):
        o = outs[0]
        s, mc, nb = peers()
        for j, (px, py) in enumerate(nb):
            _rcopy(o.at[s, mc], o.at[s, mc], ssem.at[j], rsem.at[j], (px, py, mc)).start()

    def wait(ins, outs, ssem, rsem):
        o = outs[0]
        s, mc, nb = peers()
        for j, (px, py) in enumerate(nb):
            _rcopy(o.at[2 * px + py, mc], o.at[2 * px + py, mc], ssem.at[j], rsem.at[j], (px, py, mc)).wait_recv()
        for j, (px, py) in enumerate(nb):
            _rcopy(o.at[s, mc], o.at[s, mc], ssem.at[j], rsem.at[j], (px, py, mc)).wait_send()

    return Rider([buf], [_sds(buf.shape, buf.dtype)], {0: 0}, 2, start, wait)


def rider_gather_fwd(buf):
    def start(ins, outs, ssem, rsem):
        o = outs[0]
        mx, my, mc = _me()
        xs = 2 * (1 - mx) + my
        _rcopy(o.at[xs, mc], o.at[xs, mc], ssem.at[0], rsem.at[0], (mx, 1 - my, mc)).start()

    def wait(ins, outs, ssem, rsem):
        o = outs[0]
        mx, my, mc = _me()
        xs, ds = 2 * (1 - mx) + my, 2 * (1 - mx) + (1 - my)
        _rcopy(o.at[ds, mc], o.at[ds, mc], ssem.at[0], rsem.at[0], (mx, 1 - my, mc)).wait_recv()
        _rcopy(o.at[xs, mc], o.at[xs, mc], ssem.at[0], rsem.at[0], (mx, 1 - my, mc)).wait_send()

    return Rider([buf], [_sds(buf.shape, buf.dtype)], {0: 0}, 1, start, wait)


def rider_reduce_copy(q, j, r=None):
    def peer():
        mx, my, mc = _me()
        px, py = _chips_of(mx, my)[j]
        return 2 * mx + my, 2 * px + py, (px, py, mc)

    def start(ins, outs, ssem, rsem):
        s, ps, dev = peer()
        _rcopy(ins[0].at[ps], outs[0].at[s], ssem.at[0], rsem.at[0], dev).start()

    def wait(ins, outs, ssem, rsem):
        s, ps, dev = peer()
        _rcopy(ins[0].at[ps], outs[0].at[ps], ssem.at[0], rsem.at[0], dev).wait_recv()
        _rcopy(ins[0].at[ps], outs[0].at[s], ssem.at[0], rsem.at[0], dev).wait_send()

    if r is None:
        return Rider([q], [_sds(q.shape, q.dtype)], {}, 1, start, wait)
    return Rider([q, r], [_sds(q.shape, q.dtype)], {1: 0}, 1, start, wait)


def rider_join(buf, g):
    def start(ins, outs, ssem, rsem):
        o = outs[0].at[g]
        mx, my, mc = _me()
        _rcopy(o.at[mc], o.at[mc], ssem.at[0], rsem.at[0], (mx, my, 1 - mc)).start()

    def wait(ins, outs, ssem, rsem):
        o = outs[0].at[g]
        mx, my, mc = _me()
        _rcopy(o.at[1 - mc], o.at[1 - mc], ssem.at[0], rsem.at[0], (mx, my, 1 - mc)).wait_recv()
        _rcopy(o.at[mc], o.at[mc], ssem.at[0], rsem.at[0], (mx, my, 1 - mc)).wait_send()

    return Rider([buf], [_sds(buf.shape, buf.dtype)], {0: 0}, 1, start, wait)


def _host_call(body, rider, name, grid, in_specs, out_specs, out_shape, operands, sem, n_in, n_out, aliases=None):
    aliases = dict(aliases or {})
    if rider is None:
        return pl.pallas_call(body, name=name, out_shape=out_shape, grid=grid, in_specs=in_specs, out_specs=out_specs,
                              input_output_aliases=aliases, compiler_params=_cp(sem, VMEM_BIG))(*operands)
    n_ri, n_ro = len(rider.ins), len(rider.outs)
    aliases.update({n_in + a: n_out + b for a, b in rider.aliases.items()})

    def hosted(*refs):
        ins, r_in = refs[:n_in], refs[n_in:n_in + n_ri]
        outs, r_out = refs[n_in + n_ri:n_in + n_ri + n_out], refs[n_in + n_ri + n_out:n_in + n_ri + n_out + n_ro]
        ssem, rsem = refs[-2], refs[-1]
        first = functools.reduce(lambda a, b: a & b, [pl.program_id(k) == 0 for k in range(len(grid))])
        last = functools.reduce(lambda a, b: a & b, [pl.program_id(k) == grid[k] - 1 for k in range(len(grid))])

        @pl.when(first)
        def _():
            rider.start(r_in, r_out, ssem, rsem)
        body(*ins, *outs)

        @pl.when(last)
        def _():
            rider.wait(r_in, r_out, ssem, rsem)

    return pl.pallas_call(
        hosted, name=name, out_shape=list(out_shape) + list(rider.outs), grid=grid,
        in_specs=list(in_specs) + [ANY] * n_ri, out_specs=list(out_specs) + [ANY] * n_ro,
        input_output_aliases=aliases,
        scratch_shapes=[pltpu.SemaphoreType.DMA((rider.nsem,)), pltpu.SemaphoreType.DMA((rider.nsem,))],
        compiler_params=_cp(("arbitrary",) * len(grid), VMEM_BIG))(*operands, *rider.ins)


def ffn_up(lay, hm, wbuf, ig, iu, ns, name, rider=None):
    tm = lay.tm

    def body(h_ref, wg_ref, wu_ref, g_ref, u_ref, a_ref):
        hb = h_ref[...]
        g = _nt(hb, wg_ref[0])
        u = _nt(hb, wu_ref[0])
        g_ref[0] = g.astype(BF16)
        u_ref[0] = u.astype(BF16)
        a_ref[0] = (g * _sigmoid(g) * u).astype(BF16)

    spec_o = pl.BlockSpec((1, tm, ns), lambda s, i: (s, i, 0))
    return _host_call(
        body, rider, name, (N_CHIP, lay.T // tm),
        [pl.BlockSpec((tm, D), lambda s, i: (i, 0)), pl.BlockSpec((1, ns, D), lambda s, i: (s, ig, 0)),
         pl.BlockSpec((1, ns, D), lambda s, i: (s, iu, 0))],
        [spec_o] * 3, [_sds((N_CHIP, lay.T, ns), BF16)] * 3, (hm, wbuf, wbuf), ("parallel", "parallel"), 3, 3)


def slab_nn_acc(lay, zs, wbuf, idxs, ns, name, rider=None):
    tm = lay.tm
    npair = len(zs)

    def body(*refs):
        o_ref = refs[-1]

        @pl.when(pl.program_id(1) == 0)
        def _():
            o_ref[...] = jnp.zeros_like(o_ref)
        acc = _nn(refs[0][0], refs[npair][0])
        for p in range(1, npair):
            acc += _nn(refs[p][0], refs[npair + p][0])
        o_ref[...] += acc

    in_specs = [pl.BlockSpec((1, tm, ns), lambda i, s: (s, i, 0)) for _ in zs]
    in_specs += [pl.BlockSpec((1, ns, D), functools.partial(lambda i, s, q: (s, q, 0), q=q)) for q in idxs]
    return _host_call(body, rider, name, (lay.T // tm, N_CHIP), in_specs, [pl.BlockSpec((tm, D), lambda i, s: (i, 0))],
                      [_sds((lay.T, D), F32)], (*zs, *([wbuf] * npair)), ("parallel", "arbitrary"), 2 * npair, 1)


def ffn_bwd_da(lay, dy, wbuf, idn, g, u, ns, name, rider=None):
    tm = lay.tm

    def body(dy_ref, wd_ref, g_ref, u_ref, dg_ref, du_ref):
        da = _nt(dy_ref[...], wd_ref[0])
        gv = g_ref[0].astype(F32)
        uv = u_ref[0].astype(F32)
        sg = _sigmoid(gv)
        dg_ref[0] = (da * uv * (sg * (1.0 + gv * (1.0 - sg)))).astype(BF16)
        du_ref[0] = (da * (gv * sg)).astype(BF16)

    spec_z = pl.BlockSpec((1, tm, ns), lambda s, i: (s, i, 0))
    return _host_call(
        body, rider, name, (N_CHIP, lay.T // tm),
        [pl.BlockSpec((tm, D), lambda s, i: (i, 0)), pl.BlockSpec((1, ns, D), lambda s, i: (s, idn, 0)), spec_z, spec_z],
        [spec_z] * 2, [_sds((N_CHIP, lay.T, ns), BF16)] * 2, (dy, wbuf, g, u), ("parallel", "parallel"), 4, 2)


def slab_tn(lay, z, x, gbuf, idx, ns, name, rider=None):
    tk = lay.tm

    def body(z_ref, x_ref, g_in, o_ref):
        del g_in

        @pl.when(pl.program_id(1) == 0)
        def _():
            o_ref[...] = jnp.zeros_like(o_ref)
        o_ref[0] += _tn(z_ref[0], x_ref[...])

    return _host_call(
        body, rider, name, (N_CHIP, lay.T // tk),
        [pl.BlockSpec((1, tk, ns), lambda s, k: (s, k, 0)), pl.BlockSpec((tk, D), lambda s, k: (k, 0)), ANY],
        [pl.BlockSpec((1, ns, D), lambda s, k: (s, idx, 0))], [_sds(gbuf.shape, F32)], (z, x, gbuf),
        ("parallel", "arbitrary"), 3, 1, aliases={2: 0})


Q0, K0, V0, U0, QR0, KR0, PEXT = 0, 512, 640, 768, 1280, 1792, 1920


def rope_fwd(lay, p, cos, sin, name):
    def fn(pb, cs, sn):
        cs4 = jnp.concatenate([cs] * 4, axis=1)
        sn4 = jnp.concatenate([sn] * 4, axis=1)
        qr = pb[:, Q0:K0] * cs4 + pb[:, QR0:KR0] * sn4
        kr = pb[:, K0:V0] * cs + pb[:, KR0:PEXT] * sn
        return qr, kr, pb[:, V0:U0], pb[:, U0:QR0]
    return rowwise(lay, name, fn, [p, cos, sin], outs=[(ATT_W, BF16), (KV_W, BF16), (KV_W, BF16), (POOL_W, F32)])


def rope_bwd(lay, dqr, dkr, dv, du, cos, sin, name):
    def fn(dq, dk, dvb, dub, cs, sn):
        cs4 = jnp.concatenate([cs] * 4, axis=1)
        sn4 = jnp.concatenate([sn] * 4, axis=1)
        return (jnp.concatenate([dq * cs4, dk * cs, dvb, dub, dq * sn4, dk * sn], axis=1),)
    return rowwise(lay, name, fn, [dqr, dkr, dv, du, cos, sin], outs=[(PEXT, BF16)])[0]


def _attn_specs(lay):
    nbs, cbk, lbk = lay.PS // BLK, lay.C // BLK, lay.L // BLK

    def kv_map(j):
        return lambda s, n: (s * nbs + cbk + jnp.clip(n - cbk + j - 1, 0, lbk - 1), 0)

    win = [pl.BlockSpec((BLK, KV_W), kv_map(j)) for j in range(3)]
    ctx = pl.BlockSpec((lay.C, KV_W), lambda s, n: (s * (lay.PS // lay.C), 0))
    return nbs, cbk, lbk, win, ctx


def _attn_masks(n, cbk, lbk):
    row = lax.broadcasted_iota(jnp.int32, (BLK, BLK), 0)
    col = lax.broadcasted_iota(jnp.int32, (BLK, BLK), 1)
    m = n - cbk
    lat = n >= cbk
    valid = [lat & (m >= 1) & (col >= row), lat & (col >= 0), lat & (m <= lbk - 2) & (col <= row)]
    lane_lo = lax.broadcasted_iota(jnp.int32, (BLK, 2 * HEAD_DIM), 1) < HEAD_DIM
    return valid, lane_lo


def attn_fwd(lay, qr, kr, vb, sink_tab, name):
    nbs, cbk, lbk, win, ctx = _attn_specs(lay)

    def body(q_ref, k0, k1, k2, kc_ref, v0, v1, v2, vc_ref, sk_ref, o_ref, l_ref):
        n = pl.program_id(1)
        valid, lane_lo = _attn_masks(n, cbk, lbk)
        ks = [k0[...], k1[...], k2[...]]
        vs = [v0[...], v1[...], v2[...]]
        kc, vc = kc_ref[...], vc_ref[...]
        for p in range(4):
            q2 = q_ref[:, p * 128:(p + 1) * 128]
            outs, lses = [], []
            for hh in range(2):
                qm = jnp.where(lane_lo == (hh == 0), q2, jnp.zeros_like(q2))
                sk = sk_ref[p:p + 1, hh * HEAD_DIM:hh * HEAD_DIM + 1]
                sw = [jnp.where(valid[j], _nt(qm, ks[j]) * ATT_SCALE, NEG_INF) for j in range(3)]
                sc = _nt(qm, kc) * ATT_SCALE
                mx = jnp.maximum(jnp.maximum(jnp.maximum(sw[0].max(-1, keepdims=True), sw[1].max(-1, keepdims=True)),
                                             jnp.maximum(sw[2].max(-1, keepdims=True), sc.max(-1, keepdims=True))), sk)
                ew = [jnp.exp(s - mx) for s in sw]
                ec = jnp.exp(sc - mx)
                den = ew[0].sum(-1, keepdims=True) + ew[1].sum(-1, keepdims=True) + ew[2].sum(-1, keepdims=True)
                den = den + ec.sum(-1, keepdims=True) + jnp.exp(sk - mx)
                o = _nn((ec / den).astype(BF16), vc)
                for j in range(3):
                    o += _nn((ew[j] / den).astype(BF16), vs[j])
                outs.append(o)
                lses.append(jnp.broadcast_to(mx + jnp.log(den), (BLK, 128)))
            o_ref[:, p * 128:(p + 1) * 128] = jnp.where(lane_lo, outs[0], outs[1]).astype(o_ref.dtype)
            l_ref[:, p * 128:(p + 1) * 128] = jnp.where(lane_lo, lses[0], lses[1])

    qspec = pl.BlockSpec((BLK, ATT_W), lambda s, n: (s * nbs + n, 0))
    return pl.pallas_call(
        body, name=name, out_shape=[_sds((lay.T, ATT_W), BF16), _sds((lay.T, ATT_W), F32)], grid=(2, nbs),
        in_specs=[qspec] + win + [ctx] + win + [ctx] + [pl.BlockSpec((8, 128), lambda s, n: (0, 0))],
        out_specs=[qspec, qspec], compiler_params=_cp(("parallel", "parallel")))(qr, kr, kr, kr, kr, vb, vb, vb, vb, sink_tab)


def attn_bwd(lay, qr, kr, vb, sink_tab, lse, datt, name):
    nbs, cbk, lbk, win, ctx = _attn_specs(lay)
    C, PS = lay.C, lay.PS

    def body(q_ref, k0, k1, k2, kc_ref, v0, v1, v2, vc_ref, sk_ref, l_ref, do_ref, dq_ref, dk_ref, dv_ref, ds_ref):
        n = pl.program_id(1)
        valid, lane_lo = _attn_masks(n, cbk, lbk)

        @pl.when(n == 0)
        def _():
            dk_ref[...] = jnp.zeros_like(dk_ref)
            dv_ref[...] = jnp.zeros_like(dv_ref)
            ds_ref[...] = jnp.zeros_like(ds_ref)

        ks = [k0[...], k1[...], k2[...], kc_ref[...]]
        vs = [v0[...], v1[...], v2[...], vc_ref[...]]
        dks = [jnp.zeros((BLK, KV_W), F32)] * 3 + [jnp.zeros((C, KV_W), F32)]
        dvs = list(dks)
        for p in range(4):
            sl = slice(p * 128, (p + 1) * 128)
            q2 = q_ref[:, sl]
            do2 = do_ref[:, sl].astype(BF16)
            lse2 = l_ref[:, sl]
            dq_h, dd_h = [], []
            for hh in range(2):
                sel = lane_lo == (hh == 0)
                qm = jnp.where(sel, q2, jnp.zeros_like(q2))
                dom = jnp.where(sel, do2, jnp.zeros_like(do2))
                lse_h = lse2[:, hh * HEAD_DIM:hh * HEAD_DIM + 1]
                ps, dps = [], []
                for j in range(4):
                    s = _nt(qm, ks[j]) * ATT_SCALE
                    if j < 3:
                        s = jnp.where(valid[j], s, NEG_INF)
                    ps.append(jnp.exp(s - lse_h))
                    dps.append(_nt(dom, vs[j]))
                dd = (ps[0] * dps[0]).sum(-1, keepdims=True) + (ps[1] * dps[1]).sum(-1, keepdims=True)
                dd = dd + (ps[2] * dps[2]).sum(-1, keepdims=True) + (ps[3] * dps[3]).sum(-1, keepdims=True)
                dq = jnp.zeros((BLK, 128), F32)
                for j in range(4):
                    dsb = (ps[j] * (dps[j] - dd) * ATT_SCALE).astype(BF16)
                    dq += _nn(dsb, ks[j])
                    dks[j] = dks[j] + _tn(dsb, qm)
                    dvs[j] = dvs[j] + _tn(ps[j].astype(BF16), dom)
                dq_h.append(dq)
                dd_h.append(jnp.broadcast_to(dd, (BLK, 128)))
            dq_ref[:, sl] = jnp.where(lane_lo, dq_h[0], dq_h[1])
            dd2 = jnp.where(lane_lo, dd_h[0], dd_h[1])
            psink = jnp.exp(sk_ref[p:p + 1, :] - lse2)
            ds_ref[0, p:p + 1, :] += -jnp.sum(psink * dd2, axis=0, keepdims=True)
        dk_ref[0:C, :] += dks[3]
        dv_ref[0:C, :] += dvs[3]
        for j in range(3):
            r0 = pl.multiple_of((cbk + jnp.clip(n - cbk + j - 1, 0, lbk - 1)) * BLK, BLK)
            dk_ref[pl.ds(r0, BLK), :] += dks[j]
            dv_ref[pl.ds(r0, BLK), :] += dvs[j]

    qspec = pl.BlockSpec((BLK, ATT_W), lambda s, n: (s * nbs + n, 0))
    kvout = pl.BlockSpec((PS, KV_W), lambda s, n: (s, 0))
    return pl.pallas_call(
        body, name=name,
        out_shape=[_sds((lay.T, ATT_W), F32), _sds((lay.T, KV_W), F32), _sds((lay.T, KV_W), F32), _sds((2, 8, 128), F32)],
        grid=(2, nbs),
        in_specs=[qspec] + win + [ctx] + win + [ctx] + [pl.BlockSpec((8, 128), lambda s, n: (0, 0)), qspec, qspec],
        out_specs=[qspec, kvout, kvout, pl.BlockSpec((1, 8, 128), lambda s, n: (s, 0, 0))],
        compiler_params=_cp(("parallel", "arbitrary")))(qr, kr, kr, kr, kr, vb, vb, vb, vb, sink_tab, lse, datt)


def _winsum(x, r):
    n = x.shape[0]
    t = lax.broadcasted_iota(jnp.int32, x.shape, 0)
    acc = x
    for o in range(1, r + 1):
        acc = acc + jnp.where(t >= o, pltpu.roll(x, o, 0), 0.0) + jnp.where(t < n - o, pltpu.roll(x, n - o, 0), 0.0)
    return acc


def _wincount(n, r):
    t = lax.broadcasted_iota(jnp.int32, (n, 128), 0)
    return (jnp.minimum(t + r, n - 1) - jnp.maximum(t - r, 0) + 1).astype(F32)


def pool_fwd(lay, u, w_pool, scale, name):
    segs = [(0, lay.C), (lay.C, lay.L)]

    def body(u_ref, w_ref, s_ref, o_ref):
        for r0, n in segs:
            for g, wd in enumerate(POOL_WINDOWS):
                sl = slice(g * 128, (g + 1) * 128)
                x = u_ref[r0:r0 + n, sl]
                d = _winsum(x, wd // 2) / _wincount(n, wd // 2) - x
                y = _nn(d.astype(BF16), w_ref[g].astype(BF16)) * s_ref[:, sl]
                o_ref[r0:r0 + n, sl] = y.astype(o_ref.dtype)

    spec = pl.BlockSpec((lay.PS, POOL_W), lambda s: (s, 0))
    return pl.pallas_call(
        body, name=name, out_shape=_sds((lay.T, POOL_W), BF16), grid=(2,),
        in_specs=[spec, pl.BlockSpec(w_pool.shape, lambda s: (0, 0, 0)), pl.BlockSpec((1, POOL_W), lambda s: (0, 0))],
        out_specs=spec, compiler_params=_cp(("parallel",), VMEM_BIG))(u, w_pool, scale)


def pool_bwd(lay, u, dcat, w_pool, scale, name):
    segs = [(0, lay.C), (lay.C, lay.L)]

    def body(u_ref, dp_ref, w_ref, s_ref, du_ref, dw_ref, dsc_ref):
        for g, wd in enumerate(POOL_WINDOWS):
            sl = slice(g * 128, (g + 1) * 128)
            wb = w_ref[g].astype(BF16)
            dw = jnp.zeros((128, 128), F32)
            dsc = jnp.zeros((1, 128), F32)
            for r0, n in segs:
                x = u_ref[r0:r0 + n, sl]
                cnt = _wincount(n, wd // 2)
                d = (_winsum(x, wd // 2) / cnt - x).astype(BF16)
                dp = dp_ref[r0:r0 + n, sl]
                dsc += jnp.sum(_nn(d, wb) * dp, axis=0, keepdims=True)
                dyp = (dp * s_ref[:, sl]).astype(BF16)
                dw += _tn(d, dyp)
                dd = _nt(dyp, wb)
                du_ref[r0:r0 + n, sl] = _winsum(dd / cnt, wd // 2) - dd
            dw_ref[0, g] = dw
            dsc_ref[0, :, sl] = dsc

    spec = pl.BlockSpec((lay.PS, POOL_W), lambda s: (s, 0))
    return pl.pallas_call(
        body, name=name,
        out_shape=[_sds((lay.T, POOL_W), F32), _sds((2, 4, 128, 128), F32), _sds((2, 1, POOL_W), F32)], grid=(2,),
        in_specs=[spec, pl.BlockSpec((lay.PS, POOL_W), lambda s: (s, 1)), pl.BlockSpec(w_pool.shape, lambda s: (0, 0, 0)),
                  pl.BlockSpec((1, POOL_W), lambda s: (0, 0))],
        out_specs=[spec, pl.BlockSpec((1, 4, 128, 128), lambda s: (s, 0, 0, 0)), pl.BlockSpec((1, 1, POOL_W), lambda s: (s, 0, 0))],
        compiler_params=_cp(("parallel",), VMEM_BIG))(u, dcat, w_pool, scale)


CONV_OFFS = (-1, 0, 1, 2)
CW = 256


def _shift_rows(x, o):
    if o == 0:
        return x
    n = x.shape[0]
    t = lax.broadcasted_iota(jnp.int32, x.shape, 0)
    if o < 0:
        return jnp.where(t >= -o, pltpu.roll(x, -o, 0), 0.0)
    return jnp.where(t < n - o, pltpu.roll(x, n - o, 0), 0.0)


def conv_fwd(lay, p, col0, w, b, name):
    segs = [(0, lay.C), (lay.C, lay.L)]
    cb0 = col0 // CW

    def body(x_ref, w_ref, b_ref, o_ref):
        for r0, n in segs:
            x = x_ref[r0:r0 + n, :]
            y = jnp.broadcast_to(b_ref[...], x.shape)
            for k, o in enumerate(CONV_OFFS):
                y = y + _shift_rows(x, o) * w_ref[k:k + 1, :]
            o_ref[r0:r0 + n, :] = y

    return pl.pallas_call(
        body, name=name, out_shape=_sds((lay.T, D), F32), grid=(2, D // CW),
        in_specs=[pl.BlockSpec((lay.PS, CW), lambda s, j: (s, cb0 + j)), pl.BlockSpec((4, CW), lambda s, j: (0, j)),
                  pl.BlockSpec((1, CW), lambda s, j: (0, j))],
        out_specs=pl.BlockSpec((lay.PS, CW), lambda s, j: (s, j)),
        compiler_params=_cp(("parallel", "parallel")))(p, w, b)


def conv_bwd(lay, p, col0, w, duc, name):
    segs = [(0, lay.C), (lay.C, lay.L)]
    cb0 = col0 // CW

    def body(x_ref, w_ref, g_ref, du_ref, dw_ref, db_ref):
        dws = [jnp.zeros((1, CW), F32)] * 4
        db = jnp.zeros((1, CW), F32)
        for r0, n in segs:
            x = x_ref[r0:r0 + n, :]
            g = g_ref[r0:r0 + n, :]
            du = jnp.zeros_like(g)
            for k, o in enumerate(CONV_OFFS):
                du = du + _shift_rows(g, -o) * w_ref[k:k + 1, :]
                dws[k] = dws[k] + jnp.sum(g * _shift_rows(x, o), axis=0, keepdims=True)
            db = db + jnp.sum(g, axis=0, keepdims=True)
            du_ref[r0:r0 + n, :] = du
        dw_ref[0] = jnp.concatenate(dws, axis=0)
        db_ref[0] = db

    return pl.pallas_call(
        body, name=name, out_shape=[_sds((lay.T, D), F32), _sds((2, 4, D), F32), _sds((2, 1, D), F32)], grid=(2, D // CW),
        in_specs=[pl.BlockSpec((lay.PS, CW), lambda s, j: (s, cb0 + j)), pl.BlockSpec((4, CW), lambda s, j: (0, j)),
                  pl.BlockSpec((lay.PS, CW), lambda s, j: (s, j))],
        out_specs=[pl.BlockSpec((lay.PS, CW), lambda s, j: (s, j)), pl.BlockSpec((1, 4, CW), lambda s, j: (s, 0, j)),
                   pl.BlockSpec((1, 1, CW), lambda s, j: (s, 0, j))],
        compiler_params=_cp(("parallel", "parallel")))(p, w, duc)


def _softplus_neg(lam):
    z = -lam
    w = jnp.exp(-jnp.abs(z))
    log1p = jnp.where(w < 1e-2, w * (1.0 - w * (0.5 - w / 3.0)), jnp.log(1.0 + w))
    return jnp.maximum(z, 0.0) + log1p, -_sigmoid(z)


def _neg_expm1(x):
    series = -x * (1.0 + x * (0.5 + x * (1.0 / 6.0 + x * (1.0 / 24.0 + x * (1.0 / 120.0)))))
    return jnp.where(x > -0.05, series, 1.0 - jnp.exp(x))


def _lru_gates(x, xb, wa, wx, ba, bx, lam):
    r = _sigmoid(_nn(xb, wa.astype(BF16)) + ba)
    gi = _sigmoid(_nn(xb, wx.astype(BF16)) + bx)
    sp, dsp = _softplus_neg(lam)
    la = -LRU_C * r * sp
    a = jnp.exp(la)
    sq = jnp.sqrt(_neg_expm1(2.0 * la))
    return r, gi, sp, dsp, a, sq


def lru_coeffs(lay, uc, wa, wx, vec, name):
    tr = lay.tr

    def body(x_ref, wa_ref, wx_ref, v_ref, a_ref, b_ref):
        for h in range(8):
            sl = slice(h * 128, (h + 1) * 128)
            x = x_ref[:, sl]
            xb = x.astype(BF16)
            for d in range(2):
                _, gi, _, _, a, sq = _lru_gates(x, xb, wa_ref[d, h], wx_ref[d, h], v_ref[d:d + 1, sl],
                                                v_ref[2 + d:3 + d, sl], v_ref[4 + d:5 + d, sl])
                a_ref[d, h] = a
                b_ref[d, h] = sq * (gi * x)

    wspec = pl.BlockSpec((2, 8, 128, 128), lambda i: (0, 0, 0, 0))
    ospec = pl.BlockSpec((2, 8, tr, 128), lambda i: (0, 0, i, 0))
    return pl.pallas_call(
        body, name=name, out_shape=[_sds((2, 8, lay.T, 128), F32)] * 2, grid=(lay.nblk,),
        in_specs=[pl.BlockSpec((tr, D), lambda i: (i, 0)), wspec, wspec, pl.BlockSpec((6, D), lambda i: (0, 0))],
        out_specs=[ospec, ospec], compiler_params=_cp(("parallel",)))(uc, wa, wx, vec)


def lru_coeffs_bwd(lay, uc, wa, wx, vec, da, db, name):
    tr = lay.tr

    def body(x_ref, wa_ref, wx_ref, v_ref, da_ref, db_ref, dx_ref, dwa_ref, dwx_ref, dv_ref):
        @pl.when(pl.program_id(0) == 0)
        def _():
            dwa_ref[...] = jnp.zeros_like(dwa_ref)
            dwx_ref[...] = jnp.zeros_like(dwx_ref)
            dv_ref[...] = jnp.zeros_like(dv_ref)

        for h in range(8):
            sl = slice(h * 128, (h + 1) * 128)
            x = x_ref[:, sl]
            xb = x.astype(BF16)
            dx = jnp.zeros_like(x)
            for d in range(2):
                wab, wxb = wa_ref[d, h].astype(BF16), wx_ref[d, h].astype(BF16)
                r, gi, sp, dsp, a, sq = _lru_gates(x, xb, wa_ref[d, h], wx_ref[d, h], v_ref[d:d + 1, sl],
                                                   v_ref[2 + d:3 + d, sl], v_ref[4 + d:5 + d, sl])
                dbv, dav = db_ref[d, h], da_ref[d, h]
                t1 = dbv * sq
                dgi = t1 * x
                dx = dx + t1 * gi
                dla = dav * a - (dbv * gi * x) * (a * a) / sq
                dr = dla * (-LRU_C * sp)
                dlam = jnp.sum(dla * (-LRU_C * r), axis=0, keepdims=True) * dsp
                dpa = dr * r * (1.0 - r)
                dpx = dgi * gi * (1.0 - gi)
                dpab, dpxb = dpa.astype(BF16), dpx.astype(BF16)
                dwa_ref[d, h] += _tn(xb, dpab)
                dwx_ref[d, h] += _tn(xb, dpxb)
                dx = dx + _nt(dpab, wab) + _nt(dpxb, wxb)
                dv_ref[d:d + 1, sl] += jnp.sum(dpa, axis=0, keepdims=True)
                dv_ref[2 + d:3 + d, sl] += jnp.sum(dpx, axis=0, keepdims=True)
                dv_ref[4 + d:5 + d, sl] += dlam
            dx_ref[:, sl] = dx

    wspec = pl.BlockSpec((2, 8, 128, 128), lambda i: (0, 0, 0, 0))
    gspec = pl.BlockSpec((2, 8, tr, 128), lambda i: (0, 0, i, 0))
    vspec = pl.BlockSpec((6, D), lambda i: (0, 0))
    xspec = pl.BlockSpec((tr, D), lambda i: (i, 0))
    return pl.pallas_call(
        body, name=name,
        out_shape=[_sds((lay.T, D), F32), _sds((2, 8, 128, 128), F32), _sds((2, 8, 128, 128), F32), _sds((6, D), F32)],
        grid=(lay.nblk,), in_specs=[xspec, wspec, wspec, vspec, gspec, gspec],
        out_specs=[xspec, wspec, wspec, vspec], compiler_params=_cp(("arbitrary",)))(uc, wa, wx, vec, da, db)


GB = 2
SCAN_UNROLL = 4


def _tile_scan(a, b, up):
    t = lax.broadcasted_iota(jnp.int32, a.shape, 0)
    for d in (1, 2, 4):
        sh = 8 - d if up else d
        m = (t < 8 - d) if up else (t >= d)
        a_prev, b_prev = pltpu.roll(a, sh, 0), pltpu.roll(b, sh, 0)
        b = jnp.where(m, a * b_prev + b, b)
        a = jnp.where(m, a * a_prev, a)
    return a, b


def lru_scan(lay, a, b, name):
    segs = [(0, lay.C), (lay.C, lay.L)]

    def body(a_ref, b_ref, s_ref):
        for d in range(2):
            rev = d == 1
            state = tuple(jnp.zeros((1, 128), F32) for _ in range(GB))
            for base, n in segs:
                nt = n // 8

                def step(j, c, base=base, nt=nt, rev=rev, d=d):
                    c = list(c)
                    for u in range(SCAN_UNROLL):
                        jj = j * SCAN_UNROLL + u
                        r0 = pl.multiple_of(base + 8 * ((nt - 1 - jj) if rev else jj), 8)
                        for g in range(GB):
                            at, bt = _tile_scan(a_ref[d, g, pl.ds(r0, 8), :], b_ref[d, g, pl.ds(r0, 8), :], rev)
                            h = at * c[g] + bt
                            s_ref[d, g, pl.ds(r0, 8), :] = h
                            c[g] = h[0:1] if rev else h[7:8]
                    return tuple(c)

                state = lax.fori_loop(0, nt // SCAN_UNROLL, step, state)

    spec = pl.BlockSpec((2, GB, lay.PS, 128), lambda s, hb: (0, hb, s, 0))
    return pl.pallas_call(
        body, name=name, out_shape=_sds((2, 8, lay.T, 128), F32), grid=(2, 8 // GB),
        in_specs=[spec, spec], out_specs=spec, compiler_params=_cp(("parallel", "parallel"), VMEM_BIG))(a, b)


def lru_scan_bwd(lay, a, s, dy, name):
    segs = [(0, lay.C), (lay.C, lay.L)]
    C, PS = lay.C, lay.PS

    def body(a_ref, s_ref, g_ref, da_ref, db_ref):
        t = lax.broadcasted_iota(jnp.int32, (8, 128), 0)
        for d in range(2):
            rev = d == 1
            carry = tuple(jnp.zeros((1, 128), F32) for _ in range(GB))
            for si in (1, 0):
                base, n = segs[si]
                nt = n // 8

                def step(j, c, base=base, nt=nt, rev=rev, d=d):
                    c = list(c)
                    for u in range(SCAN_UNROLL):
                        jj = j * SCAN_UNROLL + u
                        r0 = pl.multiple_of(base + 8 * (jj if rev else (nt - 1 - jj)), 8)
                        if rev:
                            rn = pl.multiple_of(jnp.where(r0 == PS - 8, 0, r0 + 8), 8)
                            nb_zero = r0 == C - 8
                        else:
                            rn = pl.multiple_of(jnp.maximum(r0 - 8, 0), 8)
                            nb_zero = r0 == 0
                        for g in range(GB):
                            av = a_ref[d, g, pl.ds(r0, 8), :]
                            gv = g_ref[g, pl.ds(r0, 8), :]
                            sv = s_ref[d, g, pl.ds(r0, 8), :]
                            nbt = s_ref[d, g, pl.ds(rn, 8), :]
                            at, bt = _tile_scan(av, av * gv, not rev)
                            m = at * c[g] + bt
                            if rev:
                                m_next = jnp.where(t >= 1, pltpu.roll(m, 1, 0), c[g])
                                nb = jnp.where(nb_zero, 0.0, nbt[0:1])
                                h_prev = jnp.where(t < 7, pltpu.roll(sv, 7, 0), nb)
                                c[g] = m[7:8]
                            else:
                                m_next = jnp.where(t < 7, pltpu.roll(m, 7, 0), c[g])
                                nb = jnp.where(nb_zero, 0.0, nbt[7:8])
                                h_prev = jnp.where(t >= 1, pltpu.roll(sv, 1, 0), nb)
                                c[g] = m[0:1]
                            lam = gv + m_next
                            db_ref[d, g, pl.ds(r0, 8), :] = lam
                            da_ref[d, g, pl.ds(r0, 8), :] = lam * h_prev
                    return tuple(c)

                carry = lax.fori_loop(0, nt // SCAN_UNROLL, step, carry)

    spec = pl.BlockSpec((2, GB, lay.PS, 128), lambda s, hb: (0, hb, s, 0))
    return pl.pallas_call(
        body, name=name, out_shape=[_sds((2, 8, lay.T, 128), F32)] * 2, grid=(2, 8 // GB),
        in_specs=[spec, spec, pl.BlockSpec((GB, lay.PS, 128), lambda s, hb: (hb, s, 0))],
        out_specs=[spec, spec], compiler_params=_cp(("parallel", "parallel"), VMEM_BIG))(a, s, dy)


def _gelu(x):
    k = math.sqrt(2.0 / math.pi)
    t = jnp.tanh(k * (x + 0.044715 * x * x * x))
    return 0.5 * x * (1.0 + t), 0.5 * (1.0 + t) + 0.5 * x * (1.0 - t * t) * k * (1.0 + 3 * 0.044715 * x * x)


def lru_gate(lay, p, s, name):
    tr = lay.tr

    def body(g_ref, s_ref, o_ref):
        for h in range(8):
            sl = slice(h * 128, (h + 1) * 128)
            o_ref[:, sl] = (_gelu(g_ref[:, sl])[0] * (s_ref[0, h] + s_ref[1, h])).astype(o_ref.dtype)

    return pl.pallas_call(
        body, name=name, out_shape=_sds((lay.T, D), BF16), grid=(lay.nblk,),
        in_specs=[pl.BlockSpec((tr, D), lambda i: (i, 0)), pl.BlockSpec((2, 8, tr, 128), lambda i: (0, 0, i, 0))],
        out_specs=pl.BlockSpec((tr, D), lambda i: (i, 0)), compiler_params=_cp(("parallel",)))(p, s)


def lru_gate_bwd(lay, p, s, do, name):
    tr = lay.tr

    def body(g_ref, s_ref, do_ref, dg_ref, dy_ref):
        for h in range(8):
            sl = slice(h * 128, (h + 1) * 128)
            ge, dge = _gelu(g_ref[:, sl])
            dov = do_ref[:, sl]
            dg_ref[:, sl] = dov * (s_ref[0, h] + s_ref[1, h]) * dge
            dy_ref[h] = dov * ge

    xspec = pl.BlockSpec((tr, D), lambda i: (i, 0))
    return pl.pallas_call(
        body, name=name, out_shape=[_sds((lay.T, D), F32), _sds((8, lay.T, 128), F32)], grid=(lay.nblk,),
        in_specs=[xspec, pl.BlockSpec((2, 8, tr, 128), lambda i: (0, 0, i, 0)), xspec],
        out_specs=[xspec, pl.BlockSpec((8, tr, 128), lambda i: (0, i, 0))],
        compiler_params=_cp(("parallel",)))(p, s, do)


def silu_rows(x, name):
    def body(x_ref, o_ref):
        v = x_ref[...]
        o_ref[...] = (v * _sigmoid(v)).astype(o_ref.dtype)
    return pl.pallas_call(body, name=name, out_shape=_sds(x.shape, BF16), in_specs=[VMEM_SPEC], out_specs=VMEM_SPEC)(x)


def mod_grad_rows(gath, name):
    w = gath.shape[-1]

    def body(g_ref, dm_ref, db_ref):
        dm_ref[...] = jnp.zeros_like(dm_ref)
        for l in range(2):
            ctx = g_ref[0, 3 * l + 2:3 * l + 3, :]
            tot = g_ref[0, 3 * l:3 * l + 1, :] + g_ref[0, 3 * l + 1:3 * l + 2, :]
            for k in range(8):
                dm_ref[l, 2 * k:2 * k + 2, :] = g_ref[k, 3 * l:3 * l + 2, :]
                if k:
                    ctx = ctx + g_ref[k, 3 * l + 2:3 * l + 3, :]
                    tot = tot + (g_ref[k, 3 * l:3 * l + 1, :] + g_ref[k, 3 * l + 1:3 * l + 2, :])
            dm_ref[l, 16:17, :] = ctx
            db_ref[l:l + 1, :] = tot + ctx

    return pl.pallas_call(body, name=name, out_shape=[_sds((2, 32, w), F32), _sds((2, w), F32)],
                          in_specs=[VMEM_SPEC], out_specs=[VMEM_SPEC, VMEM_SPEC])(gath)


def cctx_grad(p, c_ctx, name):
    def body(a_ref, c_ref, o_ref):
        cv = c_ref[...]
        sg = _sigmoid(cv)
        o_ref[...] = 0.5 * (a_ref[0, 0:1, :] + a_ref[1, 0:1, :]) * (sg * (1.0 + cv * (1.0 - sg)))
    return pl.pallas_call(body, name=name, out_shape=_sds((1, D), F32), in_specs=[VMEM_SPEC] * 2,
                          out_specs=VMEM_SPEC)(p, c_ctx)


def loss_and_grad(lay, h, tgt, name):
    def fn(hb, tb):
        lat = (pl.program_id(0) % lay.bps) >= lay.cb
        e = jnp.where(lat, hb - tb, 0.0)
        return e * (1.0 / D), jnp.sum(e * e, axis=0, keepdims=True) * (0.5 / D)
    return rowwise(lay, name, fn, [h, tgt], outs=[(D, F32)], sums=[(1, D)])


def adamw(w, g, m, v, name):
    shape = w.shape
    w2, g2, m2, v2 = (t.reshape(-1, shape[-1]) for t in (w, g, m, v))
    rows, width = w2.shape
    tr = 256 if rows % 256 == 0 else rows
    c1 = 1.0 - ADAM_B1 ** ADAM_STEP
    c2 = 1.0 - ADAM_B2 ** ADAM_STEP

    def body(w_ref, g_ref, m_ref, v_ref, d_ref, mo_ref, vo_ref):
        gv = g_ref[...]
        mn = ADAM_B1 * m_ref[...] + (1.0 - ADAM_B1) * gv
        vn = ADAM_B2 * v_ref[...] + (1.0 - ADAM_B2) * (gv * gv)
        d_ref[...] = -ADAM_LR * ((mn / c1) / (jnp.sqrt(vn / c2) + ADAM_EPS) + ADAM_WD * w_ref[...])
        mo_ref[...] = mn
        vo_ref[...] = vn

    spec = pl.BlockSpec((tr, width), lambda i: (i, 0))
    d, mn, vn = pl.pallas_call(body, name=name, out_shape=[_sds((rows, width), F32)] * 3, grid=(rows // tr,),
                               in_specs=[spec] * 4, out_specs=[spec] * 3, compiler_params=_cp(("parallel",)))(w2, g2, m2, v2)
    return d.reshape(shape), mn.reshape(shape), vn.reshape(shape)


def adamw_ffn(w, m, v, red, kind, ns, name):
    shape = w.shape
    w2, m2, v2 = (t.reshape(-1, shape[-1]) for t in (w, m, v))
    rows, width = w2.shape
    c1 = 1.0 - ADAM_B1 ** ADAM_STEP
    c2 = 1.0 - ADAM_B2 ** ADAM_STEP
    if kind < 2:
        tr, nb = 256, D // 256
        gspec = pl.BlockSpec((ns, tr), lambda i: ((i // nb) * 3 + kind, i % nb))
    else:
        tr, nb = ns // 2, 2
        gspec = pl.BlockSpec((tr, D), lambda i: (((i // nb) * 3 + kind) * nb + i % nb, 0))

    def body(w_ref, g_ref, m_ref, v_ref, go_ref, d_ref, mo_ref, vo_ref):
        gv = g_ref[...].T if kind < 2 else g_ref[...]
        mn = ADAM_B1 * m_ref[...] + (1.0 - ADAM_B1) * gv
        vn = ADAM_B2 * v_ref[...] + (1.0 - ADAM_B2) * (gv * gv)
        go_ref[...] = gv
        d_ref[...] = -ADAM_LR * ((mn / c1) / (jnp.sqrt(vn / c2) + ADAM_EPS) + ADAM_WD * w_ref[...])
        mo_ref[...] = mn
        vo_ref[...] = vn

    spec = pl.BlockSpec((tr, width), lambda i: (i, 0))
    outs = pl.pallas_call(body, name=name, out_shape=[_sds((rows, width), F32)] * 4, grid=(rows // tr,),
                          in_specs=[spec, gspec, spec, spec], out_specs=[spec] * 4,
                          compiler_params=_cp(("parallel",)))(w2, red, m2, v2)
    return tuple(t.reshape(shape) for t in outs)


def mod_mm(sc, w_mod, bias, name):
    wm = w_mod.shape[-1]
    tn = _pick(wm, (768, 512, 384, 256, 128))

    def body(a_ref, b_ref, c_ref, o_ref):
        o_ref[...] = _nn(a_ref[...], b_ref[...].astype(BF16)) + c_ref[...]

    return pl.pallas_call(
        body, name=name, out_shape=_sds((DEPTH, 32, wm), F32), grid=(DEPTH, wm // tn),
        in_specs=[pl.BlockSpec((32, D), lambda l, j: (0, 0)), pl.BlockSpec((None, D, tn), lambda l, j: (l, 0, j)),
                  pl.BlockSpec((None, 1, tn), lambda l, j: (l, 0, j))],
        out_specs=pl.BlockSpec((None, 32, tn), lambda l, j: (l, 0, j)),
        compiler_params=_cp(("parallel", "parallel")))(sc, w_mod, bias)


def wmod_dw(sc, dcol, name):
    wm = dcol.shape[-1]
    tm = 256

    def body(a_ref, b_ref, o_ref):
        o_ref[...] = _tn(a_ref[...], b_ref[...].astype(BF16))

    return pl.pallas_call(
        body, name=name, out_shape=_sds((DEPTH, D, wm), F32), grid=(DEPTH, D // tm),
        in_specs=[pl.BlockSpec((32, tm), lambda l, i: (0, i)), pl.BlockSpec((None, 32, wm), lambda l, i: (l, 0, 0))],
        out_specs=pl.BlockSpec((None, tm, wm), lambda l, i: (l, i, 0)),
        compiler_params=_cp(("parallel", "parallel")))(sc, dcol)


def cctx_dx(drow, w_mod, name):
    wm = w_mod.shape[-1]

    def body(a_ref, b_ref, o_ref):
        o_ref[...] = _nt(a_ref[...].astype(BF16), b_ref[...].astype(BF16))

    return pl.pallas_call(
        body, name=name, out_shape=_sds((DEPTH, 16, D), F32), grid=(DEPTH,),
        in_specs=[pl.BlockSpec((None, 16, wm), lambda l: (l, 0, 0)), pl.BlockSpec((None, D, wm), lambda l: (l, 0, 0))],
        out_specs=pl.BlockSpec((None, 16, D), lambda l: (l, 0, 0)), compiler_params=_cp(("parallel",), VMEM_BIG))(drow, w_mod)


HEAD_PERM = (0, 4, 1, 5, 2, 6, 3, 7)


def _rot_rows(wt):
    return jnp.concatenate([-wt[32:64], wt[0:32]], axis=0)


def _unrot_rows(g):
    return jnp.concatenate([g[32:64], -g[0:32]], axis=0)


def _heads(a, n):
    return [a[64 * i:64 * (i + 1)] for i in range(n)]


def kernel(x, c, ctx, c_ctx, w_mod, b_mod, ln_g, ln_b, ffn_w_gate, ffn_w_up, ffn_w_down, mix_ab_w_in, attn_sink, pool_w, pool_scale, mix_ab_w_out, lru_w_in, lru_conv_w, lru_conv_b, lru_wa, lru_ba, lru_wx, lru_bx, lru_lambda, lru_w_out, loss_target, m_c_ctx, m_w_mod, m_b_mod, m_ln_g, m_ln_b, m_ffn_w_gate, m_ffn_w_up, m_ffn_w_down, m_mix_ab_w_in, m_attn_sink, m_pool_w, m_pool_scale, m_mix_ab_w_out, m_lru_w_in, m_lru_conv_w, m_lru_conv_b, m_lru_wa, m_lru_ba, m_lru_wx, m_lru_bx, m_lru_lambda, m_lru_w_out, v_c_ctx, v_w_mod, v_b_mod, v_ln_g, v_ln_b, v_ffn_w_gate, v_ffn_w_up, v_ffn_w_down, v_mix_ab_w_in, v_attn_sink, v_pool_w, v_pool_scale, v_mix_ab_w_out, v_lru_w_in, v_lru_conv_w, v_lru_conv_b, v_lru_wa, v_lru_ba, v_lru_wx, v_lru_bx, v_lru_lambda, v_lru_w_out):
    n_lat, n_ctx = x.shape[1], ctx.shape[1]
    lay = Layout(n_ctx, n_lat)
    T = lay.T
    ns = ffn_w_gate.shape[-1]
    n_li, n_ai = lru_w_in.shape[-1], mix_ab_w_in.shape[-1]
    n_ao, n_lo = mix_ab_w_out.shape[1], lru_w_out.shape[1]
    wm = w_mod.shape[-1]
    dsh = ln_g.shape[-1]
    mx, my, mc = lax.axis_index("x"), lax.axis_index("y"), lax.axis_index("c")
    chip = 2 * mx + my
    me = 2 * chip + mc

    c_all = all_gather8(c, "ag8_c").reshape(16, D)
    cc = jnp.concatenate([c_all, c_ctx[None, :], jnp.zeros((15, D), F32)], axis=0)
    sc = silu_rows(cc, "silu_c")
    bias = lax.dynamic_slice(b_mod, (0, chip * wm), (DEPTH, wm)).reshape(DEPTH, 1, wm)
    modg = all_gather_chips(mod_mm(sc, w_mod, bias, "mod_mm"), "ag_mod")
    modtab = []
    for l in range(DEPTH):
        full = jnp.transpose(modg[:, l], (1, 0, 2)).reshape(32, N_CHIP * wm)
        mine = lax.dynamic_slice(full, (2 * me, 0), (2, N_CHIP * wm))
        modtab.append(jnp.concatenate([mine, full[16:17]], axis=0).reshape(3, N_MOD, D))

    small = jnp.concatenate([ln_g.reshape(6, dsh), ln_b.reshape(6, dsh), lru_conv_w[0], lru_conv_b, lru_ba[0],
                             lru_bx[0], lru_lambda[0], jnp.zeros((9, dsh), F32)], axis=0)
    small = all_gather_chips(small.reshape(2, 16, dsh), "ag_small").reshape(N_CHIP, 32, dsh)
    small = jnp.transpose(small, (1, 0, 2)).reshape(32, D)
    ln_g_f, ln_b_f = small[0:6].reshape(2, 3, D), small[6:12].reshape(2, 3, D)
    conv_w_f, conv_b_f = small[12:16], small[16:17]
    lru_vec = small[17:23]

    hh = 3 * ns // 2
    placed = [ffn_place(ffn_w_gate, ffn_w_up, ffn_w_down, g // 2, g % 2, f"ag_ffn{g}_place") for g in range(4)]
    wb = [gather_placed(placed[0], "ag_ffn0"), None, None, None]
    mix_sh = jnp.concatenate([lru_w_in[0].T, mix_ab_w_in[0].T, mix_ab_w_out[0], lru_w_out[0]], axis=0).astype(BF16)
    n_mix = n_li + n_ai + n_ao + n_lo
    mixw = all_gather_chips(mix_sh.reshape(2, n_mix // 2, D), "ag_mix").reshape(N_CHIP, n_mix, D)
    o1, o2, o3 = n_li, n_li + n_ai, n_li + n_ai + n_ao
    lru_in_t = mixw[:, 0:o1].reshape(N_CHIP * n_li, D)
    ab_in_t = mixw[:, o1:o2].reshape(N_CHIP * n_ai, D)
    ab_out = mixw[:, o2:o3].reshape(N_CHIP * n_ao, D)
    lru_out = mixw[:, o3:].reshape(N_CHIP * n_lo, D)
    qh, kh = _heads(ab_in_t[Q0:K0], N_HEADS), _heads(ab_in_t[K0:V0], N_KV)
    w_ext_t = jnp.concatenate([qh[h] for h in HEAD_PERM] + [ab_in_t[K0:QR0]]
                              + [_rot_rows(qh[h]) for h in HEAD_PERM] + [_rot_rows(t) for t in kh], axis=0)
    oh = _heads(ab_out[0:ATT_W], N_HEADS)
    w_out_ext = jnp.concatenate([oh[h] for h in HEAD_PERM] + [ab_out[ATT_W:]], axis=0)

    t = jnp.arange(n_lat)
    inv = ROPE_THETA ** (-jnp.arange(16, dtype=F32) / 16.0)
    ang = jnp.concatenate([(t // GRID_W).astype(F32)[:, None] * inv, (t % GRID_W).astype(F32)[:, None] * inv], axis=-1)
    cos1 = jnp.concatenate([jnp.ones((n_ctx, 32), F32), jnp.cos(ang)], axis=0)
    sin1 = jnp.concatenate([jnp.zeros((n_ctx, 32), F32), jnp.sin(ang)], axis=0)
    cos_t = jnp.tile(cos1, (2, 4))
    sin_t = jnp.tile(sin1, (2, 4))
    sk = attn_sink[0]
    sink_tab = jnp.concatenate([jnp.repeat(jnp.stack([sk[:4], sk[4:]], axis=1), HEAD_DIM, axis=1),
                                jnp.zeros((4, 128), F32)], axis=0)
    pscale = pool_scale.reshape(1, POOL_W)

    h0 = jnp.concatenate([ctx, x], axis=1).reshape(T, D)
    tgt = loss_target.reshape(2 * n_lat, D)

    def lnv(l, j):
        return jnp.stack([ln_g_f[l, j], ln_b_f[l, j]])

    subs = [(0, 0, 0.5, 0), (0, 3, 1.0, 1), (0, 6, 0.5, 2), (1, 0, 0.5, 0), (1, 3, 1.0, 1), (1, 6, 0.5, 2)]

    def ffn_core(hm, l, f):
        tag = f"l{l}f{f}"
        gi = 2 * l + f
        w = wb[gi].reshape(N_CHIP, 3 * ns, D)
        if gi == 3:
            g, u, a = ffn_up(lay, hm, w, 0, 1, ns, f"ffn_up_{tag}")
            (y,) = slab_nn_acc(lay, [a], w, [2], ns, f"ffn_down_{tag}")
            return y, dict(g=g, u=u, a=a, nbuf=None)
        g, u, a, nbuf = ffn_up(lay, hm, w, 0, 1, ns, f"ffn_up_{tag}", rider=rider_gather_xy(placed[gi + 1]))
        y, nbuf = slab_nn_acc(lay, [a], w, [2], ns, f"ffn_down_{tag}", rider=rider_gather_fwd(nbuf))
        return y, dict(g=g, u=u, a=a, nbuf=nbuf)

    def mixa_core(hm):
        p = mm_nt(hm, w_ext_t, "mixa_in")
        qr, kr, vb, u = rope_fwd(lay, p, cos_t, sin_t, "rope")
        att, lse = attn_fwd(lay, qr, kr, vb, sink_tab, "attn")
        pool = pool_fwd(lay, u, pool_w[0], pscale, "pool")
        cat = jnp.concatenate([att, pool], axis=1)
        return mm_nn(cat, w_out_ext, "mixa_out"), dict(qr=qr, kr=kr, vb=vb, u=u, lse=lse, cat=cat)

    def mixc_core(hm):
        p = mm_nt(hm, lru_in_t, "mixc_in")
        uc = conv_fwd(lay, p, D, conv_w_f, conv_b_f, "conv")
        a, b = lru_coeffs(lay, uc, lru_wa[0], lru_wx[0], lru_vec, "lru_coef")
        s = lru_scan(lay, a, b, "lru_scan")
        o = lru_gate(lay, p, s, "lru_gate")
        return mm_nn(o, lru_out, "mixc_out"), dict(p=p, uc=uc, a=a, s=s, o=o)

    recs = []
    h = h0
    hm = modulate(lay, h0, modtab[0], 0, 1, "mod_first")
    for k, (l, k0, coef, j) in enumerate(subs):
        if k0 == 3:
            y, core = mixa_core(hm) if l == 0 else mixc_core(hm)
        else:
            y, core = ffn_core(hm, l, k0 // 6)
        nxt = None if k == 5 else (modtab[subs[k + 1][0]], subs[k + 1][1], subs[k + 1][1] + 1)
        nbuf = core.pop("nbuf", None)
        res = resid_ln(lay, h, y, modtab[l], k0 + 2, coef, lnv(l, j), f"ln_s{k}", nxt=nxt,
                       rider=None if nbuf is None else rider_gather_d2d(nbuf))
        if nbuf is not None:
            wb[2 * l + k0 // 6 + 1] = res[-1]
        recs.append(dict(h=h, hm=hm, y=y, xhat=res[1], rstd=res[2], **core))
        h = res[0]
        hm = res[3] if nxt is not None else None

    dout, lparts = loss_and_grad(lay, h, tgt, "loss")
    loss = lax.psum(jnp.sum(lparts), ("x", "y", "c"))

    dln = {}
    dms = {}
    mixg = {}
    ffn_red = [lax.empty((4, 2, hh, D), F32)]
    pending = []

    def ffn_core_bwd(dy, r, l, f):
        tag = f"l{l}f{f}"
        gi = 2 * l + f
        w = wb[gi].reshape(N_CHIP, 3 * ns, D)
        prev = pending.pop() if pending else None
        if prev is None:
            dg, du = ffn_bwd_da(lay, dy, w, 2, r["g"], r["u"], ns, f"ffn_da_{tag}")
        else:
            dg, du, recv = ffn_bwd_da(lay, dy, w, 2, r["g"], r["u"], ns, f"ffn_da_{tag}", rider=rider_reduce_sib(prev[1]))
        gb = lax.empty((N_CHIP, 3 * ns, D), F32)
        if prev is None:
            (gb,) = slab_tn(lay, r["a"], dy, gb, 2, ns, f"ffn_dwd_{tag}")
            (gb,) = slab_tn(lay, dg, r["hm"], gb, 0, ns, f"ffn_dwg_{tag}")
            (gb,) = slab_tn(lay, du, r["hm"], gb, 1, ns, f"ffn_dwu_{tag}")
            (dhm,) = slab_nn_acc(lay, [dg, du], w, [0, 1], ns, f"ffn_dh_{tag}")
        else:
            q = add_own_half(prev[1], recv, BF16, f"rs_add2_ffn{prev[0]}")
            gb, arr = slab_tn(lay, r["a"], dy, gb, 2, ns, f"ffn_dwd_{tag}", rider=rider_reduce_copy(q, 0))
            gb, arr = slab_tn(lay, dg, r["hm"], gb, 0, ns, f"ffn_dwg_{tag}", rider=rider_reduce_copy(q, 1, arr))
            gb, arr = slab_tn(lay, du, r["hm"], gb, 1, ns, f"ffn_dwu_{tag}", rider=rider_reduce_copy(q, 2, arr))
            red = sum_slots(q, arr, f"rs_add4_ffn{prev[0]}", dst=ffn_red[0], g=prev[0])
            dhm, ffn_red[0] = slab_nn_acc(lay, [dg, du], w, [0, 1], ns, f"ffn_dh_{tag}", rider=rider_join(red, prev[0]))
        pending.append((gi, gb.reshape(N_CHIP, 2, hh, D)))
        return dhm

    def mixc_core_bwd(dy, r):
        do_c = mm_nt(dy, lru_out, "mixc_out_dx")
        mixg["lru_out"] = mm_tn(r["o"], dy, "mixc_out_dw")
        dgate, dyg = lru_gate_bwd(lay, r["p"], r["s"], do_c, "lru_gate_b")
        da_c, db_c = lru_scan_bwd(lay, r["a"], r["s"], dyg, "lru_scan_b")
        duc, mixg["wa"], mixg["wx"], mixg["vec"] = lru_coeffs_bwd(lay, r["uc"], lru_wa[0], lru_wx[0], lru_vec, da_c, db_c,
                                                                  "lru_coef_b")
        du_c, mixg["cw"], mixg["cb"] = conv_bwd(lay, r["p"], D, conv_w_f, duc, "conv_b")
        dp_c = jnp.concatenate([dgate, du_c], axis=1).astype(BF16)
        mixg["lru_in_t"] = mm_tn(dp_c, r["hm"], "mixc_in_dw")
        return mm_nn(dp_c, lru_in_t, "mixc_in_dx")

    def mixa_core_bwd(dy, r):
        dcat = mm_nt(dy, w_out_ext, "mixa_out_dx")
        mixg["out_ext"] = mm_tn(r["cat"], dy, "mixa_out_dw")
        dqr, dkr, dv, mixg["sink"] = attn_bwd(lay, r["qr"], r["kr"], r["vb"], sink_tab, r["lse"], dcat, "attn_b")
        du_a, mixg["pw"], mixg["ps"] = pool_bwd(lay, r["u"], dcat, pool_w[0], pscale, "pool_b")
        dp_a = rope_bwd(lay, dqr, dkr, dv, du_a, cos_t, sin_t, "rope_b")
        mixg["ext_t"] = mm_tn(dp_a, r["hm"], "mixa_in_dw")
        return mm_nn(dp_a, w_ext_t, "mixa_in_dx")

    l, k0, coef, j = subs[5]
    dy, dres, s1 = ln_bwd(lay, dout, recs[5]["xhat"], recs[5]["rstd"], recs[5]["y"], modtab[l], k0 + 2, coef, lnv(l, j),
                          "lnb_s5")
    for k in range(5, -1, -1):
        l, k0, coef, j = subs[k]
        r = recs[k]
        if k0 == 3:
            dhm = mixa_core_bwd(dy, r) if l == 0 else mixc_core_bwd(dy, r)
        else:
            dhm = ffn_core_bwd(dy, r, l, k0 // 6)
        dln[(l, j)] = block_sums(lay, s1, f"bs_ln_s{k}")
        if k > 0:
            lp, k0p, coefp, jp = subs[k - 1]
            rp = recs[k - 1]
            dy, dres, s1, s2 = modb_lnb(lay, dres, dhm, r["h"], modtab[l], k0 + 1, rp["xhat"], rp["rstd"], rp["y"],
                                        modtab[lp], k0p + 2, coefp, lnv(lp, jp), f"modb_lnb_s{k}")
        else:
            gx, s2 = mod_bwd(lay, dres, dhm, r["h"], modtab[l], k0 + 1, "modb_s0")
        dms[(l, k0)] = block_sums(lay, s2, f"bs_mod_s{k}")
    grad_x = gx.reshape(2, n_lat, D)
    g_lru_out, g_wa, g_wx, g_vec, g_cw, g_cb = (mixg[n] for n in ("lru_out", "wa", "wx", "vec", "cw", "cb"))
    g_lru_in_t, g_out_ext, g_sink, g_pw, g_ps, g_ext_t = (mixg[n] for n in ("lru_in_t", "out_ext", "sink", "pw", "ps", "ext_t"))

    rows = []
    for l in range(DEPTH):
        per_k = []
        for k0, j in ((0, 0), (3, 1), (6, 2)):
            per_k += [dms[(l, k0)][:3, 0], dms[(l, k0)][:3, 1], dln[(l, j)][:3, 2]]
        rows.append(jnp.stack(per_k, axis=1).reshape(3, N_MOD * D))
    dmod_loc = jnp.concatenate(rows + [jnp.zeros((2, N_MOD * D), F32)], axis=0)
    dmod_all, g_b_mod = mod_grad_rows(all_gather8(dmod_loc, "ag8_dmod"), "dmod_rows")
    dcol = lax.dynamic_slice(dmod_all, (0, 0, chip * wm), (DEPTH, 32, wm))
    g_w_mod = wmod_dw(sc, dcol, "wmod_dw")
    g_cctx = cctx_grad(cctx_dx(dcol[:, 16:32], w_mod, "cctx_dx"), c_ctx[None, :], "cctx_grad")

    gq = _heads(g_ext_t[Q0:K0], N_HEADS)
    gqr = _heads(g_ext_t[QR0:KR0], N_HEADS)
    g_q = [None] * N_HEADS
    for i, h in enumerate(HEAD_PERM):
        g_q[h] = gq[i] + _unrot_rows(gqr[i])
    gk = [a + _unrot_rows(b) for a, b in zip(_heads(g_ext_t[K0:V0], N_KV), _heads(g_ext_t[KR0:PEXT], N_KV))]
    g_ab_in_t = jnp.concatenate(g_q + gk + [g_ext_t[V0:QR0]], axis=0)
    go = _heads(g_out_ext[0:ATT_W], N_HEADS)
    g_o = [None] * N_HEADS
    for i, h in enumerate(HEAD_PERM):
        g_o[h] = go[i]
    g_ab_out = jnp.concatenate(g_o + [g_out_ext[ATT_W:]], axis=0)
    mix_g = jnp.concatenate([g_lru_in_t.reshape(N_CHIP, n_li, D), g_ab_in_t.reshape(N_CHIP, n_ai, D),
                             g_ab_out.reshape(N_CHIP, n_ao, D), g_lru_out.reshape(N_CHIP, n_lo, D)], axis=1)

    g_ln_g = jnp.stack([jnp.stack([dln[(l, j)][3, 1] for j in range(3)]) for l in range(DEPTH)])
    g_ln_b = jnp.stack([jnp.stack([dln[(l, j)][3, 0] for j in range(3)]) for l in range(DEPTH)])
    sink_row = jnp.sum(g_sink, axis=0)[:4]
    g_sink8 = jnp.concatenate([sink_row[:, 0], sink_row[:, HEAD_DIM]])
    misc = jnp.concatenate([g_sink8, jnp.sum(g_ps, axis=0).reshape(POOL_W), jnp.zeros((D - 8 - POOL_W,), F32)])
    small_g = jnp.concatenate([
        g_ln_g.reshape(6, D), g_ln_b.reshape(6, D), jnp.sum(g_cw, axis=0), jnp.sum(g_cb, axis=0), g_vec,
        misc[None, :], jnp.sum(g_pw, axis=0).reshape(64, D), g_wa.reshape(256, D), g_wx.reshape(256, D), g_cctx,
        jnp.zeros((39, D), F32)], axis=0)
    n_small = small_g.shape[0] // N_CHIP
    mix_buf = jnp.concatenate([mix_g, small_g.reshape(N_CHIP, n_small, D)], axis=1)
    n_mb = n_mix + n_small

    last_g, last_buf = pending.pop()
    ffn_red = reduce_scatter_chips(last_buf, f"ffn{last_g}", wire=BF16, dst=ffn_red[0], g=last_g).reshape(12 * ns, D)
    mix_red = reduce_scatter_chips(mix_buf.reshape(N_CHIP, 2, n_mb // 2, D), "mix").reshape(n_mb, D)
    small_red = all_gather_chips(mix_red[n_mix:].reshape(2, n_small // 2, D), "ag_smallg").reshape(N_CHIP * n_small, D)

    ffn_kind = dict(ffn_w_gate=0, ffn_w_up=1, ffn_w_down=2)

    def cols(a):
        return lax.dynamic_slice_in_dim(a, chip * dsh, dsh, axis=a.ndim - 1)

    sr = small_red
    grads = dict(
        c_ctx=sr[600], w_mod=g_w_mod, b_mod=g_b_mod,
        ln_g=cols(sr[0:6]).reshape(2, 3, dsh), ln_b=cols(sr[6:12]).reshape(2, 3, dsh),
        mix_ab_w_in=mix_red[o1:o2].T[None], attn_sink=sr[23, 0:8][None], pool_w=sr[24:88].reshape(1, 4, 128, 128),
        pool_scale=sr[23, 8:8 + POOL_W][None], mix_ab_w_out=mix_red[o2:o3][None], lru_w_in=mix_red[0:o1].T[None],
        lru_conv_w=cols(sr[12:16])[None], lru_conv_b=cols(sr[16:17]), lru_wa=sr[88:344].reshape(1, 2, 8, 128, 128),
        lru_ba=cols(sr[17:19])[None], lru_wx=sr[344:600].reshape(1, 2, 8, 128, 128), lru_bx=cols(sr[19:21])[None],
        lru_lambda=cols(sr[21:23])[None], lru_w_out=mix_red[o3:n_mix][None])
    params = dict(c_ctx=(c_ctx, m_c_ctx, v_c_ctx), w_mod=(w_mod, m_w_mod, v_w_mod), b_mod=(b_mod, m_b_mod, v_b_mod),
                  ln_g=(ln_g, m_ln_g, v_ln_g), ln_b=(ln_b, m_ln_b, v_ln_b),
                  ffn_w_gate=(ffn_w_gate, m_ffn_w_gate, v_ffn_w_gate), ffn_w_up=(ffn_w_up, m_ffn_w_up, v_ffn_w_up),
                  ffn_w_down=(ffn_w_down, m_ffn_w_down, v_ffn_w_down),
                  mix_ab_w_in=(mix_ab_w_in, m_mix_ab_w_in, v_mix_ab_w_in), attn_sink=(attn_sink, m_attn_sink, v_attn_sink),
                  pool_w=(pool_w, m_pool_w, v_pool_w), pool_scale=(pool_scale, m_pool_scale, v_pool_scale),
                  mix_ab_w_out=(mix_ab_w_out, m_mix_ab_w_out, v_mix_ab_w_out), lru_w_in=(lru_w_in, m_lru_w_in, v_lru_w_in),
                  lru_conv_w=(lru_conv_w, m_lru_conv_w, v_lru_conv_w), lru_conv_b=(lru_conv_b, m_lru_conv_b, v_lru_conv_b),
                  lru_wa=(lru_wa, m_lru_wa, v_lru_wa), lru_ba=(lru_ba, m_lru_ba, v_lru_ba), lru_wx=(lru_wx, m_lru_wx, v_lru_wx),
                  lru_bx=(lru_bx, m_lru_bx, v_lru_bx), lru_lambda=(lru_lambda, m_lru_lambda, v_lru_lambda),
                  lru_w_out=(lru_w_out, m_lru_w_out, v_lru_w_out))
    gl, dl, ml, vl = [], [], [], []
    for name, (w, m, v) in params.items():
        if name in ffn_kind:
            g, d, mn, vn = adamw_ffn(w, m, v, ffn_red, ffn_kind[name], ns, f"adamw_{name}")
        else:
            g = grads[name].reshape(w.shape)
            d, mn, vn = adamw(w, g, m, v, f"adamw_{name}")
        gl.append(g)
        dl.append(d)
        ml.append(mn)
        vl.append(vn)
    return (loss, grad_x, *gl, *dl, *ml, *vl)
```

```python
import functools
import math

import jax
import jax.numpy as jnp
from jax import lax
from jax.experimental import pallas as pl
from jax.experimental.pallas import tpu as pltpu

F32, BF16 = jnp.float32, jnp.bfloat16
MESH = pl.DeviceIdType.MESH
ANY = pl.BlockSpec(memory_space=pl.ANY)
VMEM_SPEC = pl.BlockSpec(memory_space=pltpu.VMEM)

D = 1024
N_CHIP = 4
HEAD_DIM, N_HEADS, N_KV = 64, 8, 2
ATT_W, KV_W, POOL_W = 512, 128, 512
POOL_WINDOWS = (2, 4, 8, 16)
BLK = 128
ATT_SCALE = HEAD_DIM ** -0.5
ROPE_THETA = 10000.0
GRID_W = 64
LRU_C = 8.0
LN_EPS = 1e-5
NEG_INF = -1e30
DEPTH = 2
ALPHA = (2 * DEPTH) ** 0.25
N_MOD = 9
ADAM_LR, ADAM_B1, ADAM_B2, ADAM_EPS, ADAM_WD, ADAM_STEP = 0.001, 0.9, 0.999, 1e-08, 0.01, 10
VMEM_BIG = 48 * 1024 * 1024


def _cp(sem=None, vmem=None):
    kw = {}
    if sem is not None:
        kw["dimension_semantics"] = sem
    if vmem is not None:
        kw["vmem_limit_bytes"] = vmem
    return pltpu.CompilerParams(**kw)


def _sds(shape, dtype):
    return jax.ShapeDtypeStruct(tuple(shape), dtype)


def _pick(n, cands):
    for c in cands:
        if n % c == 0:
            return c
    return n


def _dot(a, b, dims):
    return lax.dot_general(a, b, (dims, ((), ())), preferred_element_type=F32)


def _nn(a, b):
    return _dot(a, b, ((1,), (0,)))


def _nt(a, b):
    return _dot(a, b, ((1,), (1,)))


def _tn(a, b):
    return _dot(a, b, ((0,), (0,)))


def _sigmoid(x):
    return 0.5 * jnp.tanh(0.5 * x) + 0.5


def _me():
    return lax.axis_index("x"), lax.axis_index("y"), lax.axis_index("c")


def _rcopy(src, dst, ssem, rsem, dev):
    return pltpu.make_async_remote_copy(src_ref=src, dst_ref=dst, send_sem=ssem, recv_sem=rsem,
                                        device_id=dev, device_id_type=MESH)


def all_gather8(x, name):
    def body(x_ref, o_ref, ssem, rsem, lsem):
        mx, my, mc = _me()
        me = 4 * mx + 2 * my + mc
        loc = pltpu.make_async_copy(x_ref, o_ref.at[me], lsem)
        loc.start()
        peers = []
        for m in range(1, 8):
            px = 1 - mx if (m >> 2) & 1 else mx
            py = 1 - my if (m >> 1) & 1 else my
            pc = 1 - mc if m & 1 else mc
            peers.append((px, py, pc))
        sends = [_rcopy(x_ref, o_ref.at[me], ssem.at[k], rsem.at[k], p) for k, p in enumerate(peers)]
        for cp in sends:
            cp.start()
        for k, (px, py, pc) in enumerate(peers):
            _rcopy(x_ref, o_ref.at[4 * px + 2 * py + pc], ssem.at[k], rsem.at[k], (px, py, pc)).wait_recv()
        for cp in sends:
            cp.wait_send()
        loc.wait()

    return pl.pallas_call(
        body, name=name, out_shape=_sds((8,) + x.shape, x.dtype),
        in_specs=[VMEM_SPEC], out_specs=VMEM_SPEC,
        scratch_shapes=[pltpu.SemaphoreType.DMA((7,)), pltpu.SemaphoreType.DMA((7,)), pltpu.SemaphoreType.DMA],
    )(x)


_ROW_BLOCKS = (512, 384, 352, 256, 224, 128)


def _idx(v):
    return jnp.reshape(v, (1,)).astype(jnp.int32)


def place_slab(shard, name):
    _, h, w = shard.shape
    th = _pick(h, _ROW_BLOCKS)

    def body(s_ref, x_ref, o_ref):
        del s_ref
        o_ref[...] = x_ref[...]

    return pl.pallas_call(
        body, name=name, out_shape=_sds((N_CHIP,) + shard.shape, shard.dtype),
        grid_spec=pltpu.PrefetchScalarGridSpec(
            num_scalar_prefetch=1, grid=(2, h // th),
            in_specs=[pl.BlockSpec((None, th, w), lambda k, r, s: (k, r, 0))],
            out_specs=pl.BlockSpec((None, None, th, w), lambda k, r, s: (s[0], k, r, 0))),
    )(_idx(2 * lax.axis_index("x") + lax.axis_index("y")), shard)


def ffn_place(w_gate, w_up, w_down, l, f, name):
    ns = w_gate.shape[-1]
    tc = 256

    def body(s_ref, g_ref, u_ref, d_ref, o_ref):
        del s_ref
        k = pl.program_id(0)

        @pl.when(k == 0)
        def _():
            o_ref[...] = g_ref[...].T.astype(BF16)

        @pl.when(k == 1)
        def _():
            o_ref[...] = u_ref[...].T.astype(BF16)

        @pl.when(k == 2)
        def _():
            o_ref[...] = d_ref[...].astype(BF16)

    nat = pl.BlockSpec((None, None, tc, ns), lambda k, j, s: (l, f, j, 0))
    out = pl.pallas_call(
        body, name=name, out_shape=_sds((N_CHIP, 3 * ns, D), BF16),
        grid_spec=pltpu.PrefetchScalarGridSpec(
            num_scalar_prefetch=1, grid=(3, D // tc),
            in_specs=[nat, nat, pl.BlockSpec((None, None, ns, tc), lambda k, j, s: (l, f, 0, j))],
            out_specs=pl.BlockSpec((None, ns, tc), lambda k, j, s: (s[0], k, j))),
    )(_idx(2 * lax.axis_index("x") + lax.axis_index("y")), w_gate, w_up, w_down)
    return out.reshape(N_CHIP, 2, 3 * ns // 2, D)


def all_gather_chips(shard, name):
    return gather_placed(place_slab(shard, name + "_place"), name)


def gather_placed(full, name):
    def body(x_ref, o_ref, ssem, rsem):
        del x_ref
        mx, my, mc = _me()
        s = 2 * mx + my
        sib = (mx, my, 1 - mc)
        chips = [(1 - mx, my), (mx, 1 - my), (1 - mx, 1 - my)]
        first = [_rcopy(o_ref.at[s, mc], o_ref.at[s, mc], ssem.at[j], rsem.at[j], (px, py, mc))
                 for j, (px, py) in enumerate(chips)]
        for cp in first:
            cp.start()
        passed = []
        for j, (px, py) in enumerate(chips):
            ps = 2 * px + py
            _rcopy(o_ref.at[ps, mc], o_ref.at[ps, mc], ssem.at[j], rsem.at[j], (px, py, mc)).wait_recv()
            fw = _rcopy(o_ref.at[ps, mc], o_ref.at[ps, mc], ssem.at[3 + j], rsem.at[3 + j], sib)
            fw.start()
            passed.append(fw)
        for j, (px, py) in enumerate(chips):
            ps = 2 * px + py
            _rcopy(o_ref.at[ps, 1 - mc], o_ref.at[ps, 1 - mc], ssem.at[3 + j], rsem.at[3 + j], sib).wait_recv()
        for cp in first + passed:
            cp.wait_send()

    return pl.pallas_call(
        body, name=name, out_shape=_sds(full.shape, full.dtype), in_specs=[ANY], out_specs=ANY,
        input_output_aliases={0: 0},
        scratch_shapes=[pltpu.SemaphoreType.DMA((6,)), pltpu.SemaphoreType.DMA((6,))],
    )(full)


def sibling_send_other_half(buf, name):
    def body(x_ref, o_ref, ssem, rsem):
        mx, my, mc = _me()
        sib = (mx, my, 1 - mc)
        cps = [_rcopy(x_ref.at[k, 1 - mc], o_ref.at[k], ssem.at[k], rsem.at[k], sib) for k in range(N_CHIP)]
        for cp in cps:
            cp.start()
        for cp in cps:
            cp.wait_recv()
        for cp in cps:
            cp.wait_send()

    n, _, h, w = buf.shape
    return pl.pallas_call(
        body, name=name, out_shape=_sds((n, h, w), buf.dtype), in_specs=[ANY], out_specs=ANY,
        scratch_shapes=[pltpu.SemaphoreType.DMA((N_CHIP,)), pltpu.SemaphoreType.DMA((N_CHIP,))],
    )(buf)


def chips_all_to_all(q, name):
    def body(x_ref, o_ref, ssem, rsem):
        mx, my, mc = _me()
        s = 2 * mx + my
        chips = [(1 - mx, my), (mx, 1 - my), (1 - mx, 1 - my)]
        cps = [_rcopy(x_ref.at[2 * px + py], o_ref.at[s], ssem.at[j], rsem.at[j], (px, py, mc))
               for j, (px, py) in enumerate(chips)]
        for cp in cps:
            cp.start()
        for j, (px, py) in enumerate(chips):
            ps = 2 * px + py
            _rcopy(x_ref.at[ps], o_ref.at[ps], ssem.at[j], rsem.at[j], (px, py, mc)).wait_recv()
        for cp in cps:
            cp.wait_send()

    return pl.pallas_call(
        body, name=name, out_shape=_sds(q.shape, q.dtype), in_specs=[ANY], out_specs=ANY,
        scratch_shapes=[pltpu.SemaphoreType.DMA((3,)), pltpu.SemaphoreType.DMA((3,))],
    )(q)


def sibling_join_halves(both, name, g=None):
    def body(x_ref, o_ref, ssem, rsem):
        del x_ref
        mx, my, mc = _me()
        sib = (mx, my, 1 - mc)
        o = o_ref if g is None else o_ref.at[g]
        cp = _rcopy(o.at[mc], o.at[mc], ssem, rsem, sib)
        cp.start()
        _rcopy(o.at[1 - mc], o.at[1 - mc], ssem, rsem, sib).wait_recv()
        cp.wait_send()

    return pl.pallas_call(
        body, name=name, out_shape=_sds(both.shape, both.dtype), in_specs=[ANY], out_specs=ANY,
        input_output_aliases={0: 0}, scratch_shapes=[pltpu.SemaphoreType.DMA, pltpu.SemaphoreType.DMA],
    )(both)


def add_own_half(buf, recv, wire, name):
    n, _, h, w = buf.shape
    th = _pick(h, _ROW_BLOCKS)

    def body(c_ref, a_ref, b_ref, o_ref):
        del c_ref
        o_ref[...] = (a_ref[...] + b_ref[...]).astype(o_ref.dtype)

    return pl.pallas_call(
        body, name=name, out_shape=_sds((n, h, w), wire),
        grid_spec=pltpu.PrefetchScalarGridSpec(
            num_scalar_prefetch=1, grid=(n, h // th),
            in_specs=[pl.BlockSpec((None, None, th, w), lambda k, r, c: (k, c[0], r, 0)),
                      pl.BlockSpec((None, th, w), lambda k, r, c: (k, r, 0))],
            out_specs=pl.BlockSpec((None, th, w), lambda k, r, c: (k, r, 0))),
    )(_idx(lax.axis_index("c")), buf, recv)


def sum_slots(q, r, name, dst=None, g=None):
    n, h, w = r.shape
    th = _pick(h, _ROW_BLOCKS)

    def body(i_ref, q_ref, r1, r2, r3, *rest):
        del i_ref
        rest[-1][...] = ((q_ref[...].astype(F32) + r1[...].astype(F32)) + r2[...].astype(F32)) + r3[...].astype(F32)

    def slot(d):
        return lambda i, ix: ((ix[0] + d) % N_CHIP, i, 0)

    idx = jnp.stack([2 * lax.axis_index("x") + lax.axis_index("y"), lax.axis_index("c")]).astype(jnp.int32)
    in_specs = [pl.BlockSpec((None, th, w), slot(d)) for d in (0, 1, 2, 3)]
    if dst is None:
        return pl.pallas_call(
            body, name=name, out_shape=_sds((2, h, w), F32),
            grid_spec=pltpu.PrefetchScalarGridSpec(
                num_scalar_prefetch=1, grid=(h // th,), in_specs=in_specs,
                out_specs=pl.BlockSpec((None, th, w), lambda i, ix: (ix[1], i, 0))),
        )(idx, q, r, r, r)
    return pl.pallas_call(
        body, name=name, out_shape=_sds(dst.shape, F32),
        grid_spec=pltpu.PrefetchScalarGridSpec(
            num_scalar_prefetch=1, grid=(h // th,), in_specs=in_specs + [ANY],
            out_specs=pl.BlockSpec((None, None, th, w), lambda i, ix: (g, ix[1], i, 0))),
        input_output_aliases={5: 0},
    )(idx, q, r, r, r, dst)


def reduce_scatter_chips(buf, tag, wire=F32, dst=None, g=None):
    recv = sibling_send_other_half(buf, f"rs_sib_{tag}")
    q = add_own_half(buf, recv, wire, f"rs_add2_{tag}")
    r = chips_all_to_all(q, f"rs_a2a_{tag}")
    red = sum_slots(q, r, f"rs_add4_{tag}", dst=dst, g=g)
    return sibling_join_halves(red, f"rs_join_{tag}", g=g)


class Layout:
    def __init__(self, n_ctx, n_lat):
        self.C, self.L = n_ctx, n_lat
        self.PS = n_ctx + n_lat
        self.T = 2 * self.PS
        self.tr = _pick(math.gcd(n_ctx, n_lat), (256, 128))
        self.bps = self.PS // self.tr
        self.cb = n_ctx // self.tr
        self.nblk = self.T // self.tr
        self.tm = _pick(self.T, (1152, 768, 512, 256, 128))
        self.tc = _pick(self.T, (512, 256, 128))

    def sub_rows(self, parts=4):
        ts = self.tm // parts
        return [slice(k * ts, (k + 1) * ts) for k in range(parts)]

    def seg(self, i):
        return jnp.where(i % self.bps < self.cb, 2, i // self.bps)


def rowwise(lay, name, fn, rows, segs=(), vecs=(), outs=(), sums=(), rider=None):
    tr, nblk = lay.tr, lay.nblk
    n_r, n_s, n_v, n_o = len(rows), len(segs), len(vecs), len(outs)

    def body(*refs):
        ins = refs[:n_r + n_s + n_v]
        ors = refs[n_r + n_s + n_v:]
        vals = [r[...] for r in ins[:n_r]] + [r[0] for r in ins[n_r:n_r + n_s]] + [r[...] for r in ins[n_r + n_s:]]
        res = fn(*vals)
        for k in range(n_o):
            ors[k][...] = res[k].astype(ors[k].dtype)
        for k in range(len(sums)):
            ors[n_o + k][0] = res[n_o + k]

    def all_rows(i):
        return (i, 0)

    def lat_rows(i):
        return ((i // lay.bps) * (lay.bps - lay.cb) + jnp.maximum(i % lay.bps - lay.cb, 0), 0)

    in_specs = [pl.BlockSpec((tr, a.shape[1]), all_rows if a.shape[0] == lay.T else lat_rows) for a in rows]
    in_specs += [pl.BlockSpec((1,) + a.shape[1:], lambda i: (lay.seg(i), 0, 0)) for a in segs]
    in_specs += [pl.BlockSpec(a.shape, lambda i: (0, 0)) for a in vecs]
    out_shape = [_sds((2 * lay.L if o[2:] else lay.T, o[0]), o[1]) for o in outs]
    out_shape += [_sds((nblk, r, w), F32) for r, w in sums]
    out_specs = [pl.BlockSpec((tr, o[0]), lat_rows if o[2:] else all_rows) for o in outs]
    out_specs += [pl.BlockSpec((1, r, w), lambda i: (i, 0, 0)) for r, w in sums]
    sem = "arbitrary" if any(o[2:] for o in outs) else "parallel"
    if rider is None:
        return pl.pallas_call(body, name=name, out_shape=out_shape, grid=(nblk,), in_specs=in_specs,
                              out_specs=out_specs, compiler_params=_cp((sem,)))(*rows, *segs, *vecs)
    return _host_call(body, rider, name, (nblk,), in_specs, out_specs, out_shape, (*rows, *segs, *vecs), (sem,),
                      n_r + n_s + n_v, n_o + len(sums))


def modulate(lay, h, mod, k_shift, k_scale, name):
    def fn(hb, m):
        return (hb * (1.0 + m[k_scale:k_scale + 1]) + m[k_shift:k_shift + 1],)
    return rowwise(lay, name, fn, [h], segs=[mod], outs=[(D, BF16)])[0]


def resid_ln(lay, h, y, mod, k_gate, coef, lnv, name, nxt=None, rider=None):
    def fn(hb, yb, m, *rest):
        ln = rest[-1]
        z = ALPHA * hb + (coef * m[k_gate:k_gate + 1]) * yb
        mu = jnp.mean(z, axis=-1, keepdims=True)
        zc = z - mu
        var = jnp.mean(zc * zc, axis=-1, keepdims=True)
        rstd = lax.rsqrt(var + LN_EPS)
        xhat = zc * rstd
        out = xhat * ln[0:1] + ln[1:2]
        if nxt is None:
            return out, xhat, rstd
        mn = rest[0]
        return out, xhat, rstd, out * (1.0 + mn[nxt[2]:nxt[2] + 1]) + mn[nxt[1]:nxt[1] + 1]
    segs = [mod] if nxt is None else [mod, nxt[0]]
    outs = [(D, F32), (D, F32), (1, F32)] + ([] if nxt is None else [(D, BF16)])
    return rowwise(lay, name, fn, [h, y], segs=segs, vecs=[lnv], outs=outs, rider=rider)


def _ln_bwd_math(do, xh, rs, yb, gate, coef, ln):
    dxh = do * ln[0:1]
    m1 = jnp.mean(dxh, axis=-1, keepdims=True)
    m2 = jnp.mean(dxh * xh, axis=-1, keepdims=True)
    dz = rs * (dxh - m1 - xh * m2)
    s = jnp.concatenate([jnp.sum(do, axis=0, keepdims=True), jnp.sum(do * xh, axis=0, keepdims=True),
                         jnp.sum(coef * dz * yb, axis=0, keepdims=True)], axis=0)
    return (coef * gate) * dz, ALPHA * dz, s


def _mod_bwd_math(dr, dm, hb, scale):
    s = jnp.concatenate([jnp.sum(dm, axis=0, keepdims=True), jnp.sum(dm * hb, axis=0, keepdims=True)], axis=0)
    return dr + dm * (1.0 + scale), s


def ln_bwd(lay, dout, xhat, rstd, y, mod, k_gate, coef, lnv, name):
    def fn(do, xh, rs, yb, m, ln):
        return _ln_bwd_math(do, xh, rs, yb, m[k_gate:k_gate + 1], coef, ln)
    return rowwise(lay, name, fn, [dout, xhat, rstd, y], segs=[mod], vecs=[lnv],
                   outs=[(D, BF16), (D, F32)], sums=[(3, D)])


def mod_bwd(lay, dres, dhm, h, mod, k_scale, name):
    def fn(dr, dm, hb, m):
        return _mod_bwd_math(dr, dm, hb, m[k_scale:k_scale + 1])
    return rowwise(lay, name, fn, [dres, dhm, h], segs=[mod], outs=[(D, F32, "lat")], sums=[(2, D)])


def modb_lnb(lay, dres, dhm, h, mod, k_scale, xhat, rstd, y, mod_p, k_gate, coef, lnv, name):
    def fn(dr, dm, hb, xh, rs, yb, m, mp, ln):
        dh, s2 = _mod_bwd_math(dr, dm, hb, m[k_scale:k_scale + 1])
        dy, dres_p, s1 = _ln_bwd_math(dh, xh, rs, yb, mp[k_gate:k_gate + 1], coef, ln)
        return dy, dres_p, s1, s2
    return rowwise(lay, name, fn, [dres, dhm, h, xhat, rstd, y], segs=[mod, mod_p], vecs=[lnv],
                   outs=[(D, BF16), (D, F32)], sums=[(3, D), (2, D)])


def block_sums(lay, parts, name):
    nblk, r, w = parts.shape

    def body(p_ref, o_ref):
        acc = [None, None, None]
        for i in range(nblk):
            sg = 2 if i % lay.bps < lay.cb else i // lay.bps
            acc[sg] = p_ref[i] if acc[sg] is None else acc[sg] + p_ref[i]
        for k in range(3):
            o_ref[k] = acc[k]
        o_ref[3] = (acc[0] + acc[1]) + acc[2]

    return pl.pallas_call(body, name=name, out_shape=_sds((4, r, w), F32), in_specs=[VMEM_SPEC],
                          out_specs=VMEM_SPEC)(parts)


def mm_nn(a, b, name, out_dtype=F32, bias=None):
    m, k = a.shape
    n = b.shape[1]
    tm = _pick(m, (512, 256, 128, 64, 32, 16, 8))
    tn = _pick(n, (1024, 768, 640, 512, 384, 256, 128))

    def body(*refs):
        if bias is None:
            a_ref, b_ref, o_ref = refs
            o_ref[...] = _nn(a_ref[...].astype(BF16), b_ref[...].astype(BF16)).astype(o_ref.dtype)
        else:
            a_ref, b_ref, c_ref, o_ref = refs
            o_ref[...] = (_nn(a_ref[...].astype(BF16), b_ref[...].astype(BF16)) + c_ref[...]).astype(o_ref.dtype)

    in_specs = [pl.BlockSpec((tm, k), lambda i, j: (i, 0)), pl.BlockSpec((k, tn), lambda i, j: (0, j))]
    ops = [a, b]
    if bias is not None:
        in_specs.append(pl.BlockSpec((1, tn), lambda i, j: (0, j)))
        ops.append(bias)
    return pl.pallas_call(body, name=name, out_shape=_sds((m, n), out_dtype), grid=(m // tm, n // tn),
                          in_specs=in_specs, out_specs=pl.BlockSpec((tm, tn), lambda i, j: (i, j)),
                          compiler_params=_cp(("parallel", "parallel"), VMEM_BIG))(*ops)


def mm_nt(a, b, name, out_dtype=F32):
    m, k = a.shape
    n = b.shape[0]
    tm = _pick(m, (512, 256, 128, 64, 32, 16, 8))
    tn = _pick(n, (1024, 768, 640, 512, 384, 256, 128))

    def body(a_ref, b_ref, o_ref):
        o_ref[...] = _nt(a_ref[...].astype(BF16), b_ref[...].astype(BF16)).astype(o_ref.dtype)

    return pl.pallas_call(body, name=name, out_shape=_sds((m, n), out_dtype), grid=(m // tm, n // tn),
                          in_specs=[pl.BlockSpec((tm, k), lambda i, j: (i, 0)), pl.BlockSpec((tn, k), lambda i, j: (j, 0))],
                          out_specs=pl.BlockSpec((tm, tn), lambda i, j: (i, j)),
                          compiler_params=_cp(("parallel", "parallel"), VMEM_BIG))(a, b)


def mm_tn(a, b, name):
    t, m = a.shape
    n = b.shape[1]
    tk = _pick(t, (512, 256, 128, 64, 32, 16))
    tm = _pick(m, (512, 384, 256, 128))

    def body(a_ref, b_ref, o_ref):
        @pl.when(pl.program_id(1) == 0)
        def _():
            o_ref[...] = jnp.zeros_like(o_ref)
        o_ref[...] += _tn(a_ref[...].astype(BF16), b_ref[...].astype(BF16))

    return pl.pallas_call(body, name=name, out_shape=_sds((m, n), F32), grid=(m // tm, t // tk),
                          in_specs=[pl.BlockSpec((tk, tm), lambda i, k: (k, i)), pl.BlockSpec((tk, n), lambda i, k: (k, 0))],
                          out_specs=pl.BlockSpec((tm, n), lambda i, k: (i, 0)),
                          compiler_params=_cp(("parallel", "arbitrary"), VMEM_BIG))(a, b)


class Rider:
    def __init__(self, ins, outs, aliases, nsem, start, wait):
        self.ins, self.outs, self.aliases, self.nsem, self.start, self.wait = ins, outs, aliases, nsem, start, wait


def _chips_of(mx, my):
    return [(1 - mx, my), (mx, 1 - my), (1 - mx, 1 - my)]


def rider_gather_d2d(buf):
    def start(ins, outs, ssem, rsem):
        o = outs[0]
        mx, my, mc = _me()
        for j, (px, py) in enumerate(_chips_of(mx, my)):
            ps = 2 * px + py
            _rcopy(o.at[ps, mc], o.at[ps, mc], ssem.at[j], rsem.at[j], (mx, my, 1 - mc)).start()

    def wait(ins, outs, ssem, rsem):
        o = outs[0]
        mx, my, mc = _me()
        sib = (mx, my, 1 - mc)
        for j, (px, py) in enumerate(_chips_of(mx, my)):
            ps = 2 * px + py
            _rcopy(o.at[ps, 1 - mc], o.at[ps, 1 - mc], ssem.at[j], rsem.at[j], sib).wait_recv()
        for j, (px, py) in enumerate(_chips_of(mx, my)):
            ps = 2 * px + py
            _rcopy(o.at[ps, mc], o.at[ps, mc], ssem.at[j], rsem.at[j], sib).wait_send()

    return Rider([buf], [_sds(buf.shape, buf.dtype)], {0: 0}, 3, start, wait)


def rider_reduce_sib(buf):
    n, _, h, w = buf.shape

    def start(ins, outs, ssem, rsem):
        mx, my, mc = _me()
        for k in range(N_CHIP):
            _rcopy(ins[0].at[k, 1 - mc], outs[0].at[k], ssem.at[k], rsem.at[k], (mx, my, 1 - mc)).start()

    def wait(ins, outs, ssem, rsem):
        mx, my, mc = _me()
        for k in range(N_CHIP):
            _rcopy(ins[0].at[k, 1 - mc], outs[0].at[k], ssem.at[k], rsem.at[k], (mx, my, 1 - mc)).wait_recv()
        for k in range(N_CHIP):
            _rcopy(ins[0].at[k, 1 - mc], outs[0].at[k], ssem.at[k], rsem.at[k], (mx, my, 1 - mc)).wait_send()

    return Rider([buf], [_sds((n, h, w), buf.dtype)], {}, N_CHIP, start, wait)


def rider_gather_xy(buf):
    def peers():
        mx, my, mc = _me()
        return 2 * mx + my, mc, [(1 - mx, my), (mx, 1 - my)]

    def start(ins, outs, ssem, rsem):
        o = outs[0]
        s, mc, nb = peers()
        for j, (px, py) in enumerate(nb):
            _rcopy(o.at[s, mc], o.at[s, mc], ssem.at[j], rsem.at[j], (px, py, mc)).start()

    def wait(ins, outs, ssem, rsem):
        o = outs[0]
        s, mc, nb = peers()
        for j, (px, py) in enumerate(nb):
            _rcopy(o.at[2 * px + py, mc], o.at[2 * px + py, mc], ssem.at[j], rsem.at[j], (px, py, mc)).wait_recv()
        for j, (px, py) in enumerate(nb):
            _rcopy(o.at[s, mc], o.at[s, mc], ssem.at[j], rsem.at[j], (px, py, mc)).wait_send()

    return Rider([buf], [_sds(buf.shape, buf.dtype)], {0: 0}, 2, start, wait)


def rider_gather_fwd(buf):
    def start(ins, outs, ssem, rsem):
        o = outs[0]
        mx, my, mc = _me()
        xs = 2 * (1 - mx) + my
        _rcopy(o.at[xs, mc], o.at[xs, mc], ssem.at[0], rsem.at[0], (mx, 1 - my, mc)).start()

    def wait(ins, outs, ssem, rsem):
        o = outs[0]
        mx, my, mc = _me()
        xs, ds = 2 * (1 - mx) + my, 2 * (1 - mx) + (1 - my)
        _rcopy(o.at[ds, mc], o.at[ds, mc], ssem.at[0], rsem.at[0], (mx, 1 - my, mc)).wait_recv()
        _rcopy(o.at[xs, mc], o.at[xs, mc], ssem.at[0], rsem.at[0], (mx, 1 - my, mc)).wait_send()

    return Rider([buf], [_sds(buf.shape, buf.dtype)], {0: 0}, 1, start, wait)


def rider_reduce_copy(q, j, r=None):
    def peer():
        mx, my, mc = _me()
        px, py = _chips_of(mx, my)[j]
        return 2 * mx + my, 2 * px + py, (px, py, mc)

    def start(ins, outs, ssem, rsem):
        s, ps, dev = peer()
        _rcopy(ins[0].at[ps], outs[0].at[s], ssem.at[0], rsem.at[0], dev).start()

    def wait(ins, outs, ssem, rsem):
        s, ps, dev = peer()
        _rcopy(ins[0].at[ps], outs[0].at[ps], ssem.at[0], rsem.at[0], dev).wait_recv()
        _rcopy(ins[0].at[ps], outs[0].at[s], ssem.at[0], rsem.at[0], dev).wait_send()

    if r is None:
        return Rider([q], [_sds(q.shape, q.dtype)], {}, 1, start, wait)
    return Rider([q, r], [_sds(q.shape, q.dtype)], {1: 0}, 1, start, wait)


def rider_join(buf, g):
    def start(ins, outs, ssem, rsem):
        o = outs[0].at[g]
        mx, my, mc = _me()
        _rcopy(o.at[mc], o.at[mc], ssem.at[0], rsem.at[0], (mx, my, 1 - mc)).start()

    def wait(ins, outs, ssem, rsem):
        o = outs[0].at[g]
        mx, my, mc = _me()
        _rcopy(o.at[1 - mc], o.at[1 - mc], ssem.at[0], rsem.at[0], (mx, my, 1 - mc)).wait_recv()
        _rcopy(o.at[mc], o.at[mc], ssem.at[0], rsem.at[0], (mx, my, 1 - mc)).wait_send()

    return Rider([buf], [_sds(buf.shape, buf.dtype)], {0: 0}, 1, start, wait)


def _host_call(body, rider, name, grid, in_specs, out_specs, out_shape, operands, sem, n_in, n_out, aliases=None):
    aliases = dict(aliases or {})
    if rider is None:
        return pl.pallas_call(body, name=name, out_shape=out_shape, grid=grid, in_specs=in_specs, out_specs=out_specs,
                              input_output_aliases=aliases, compiler_params=_cp(sem, VMEM_BIG))(*operands)
    n_ri, n_ro = len(rider.ins), len(rider.outs)
    aliases.update({n_in + a: n_out + b for a, b in rider.aliases.items()})

    def hosted(*refs):
        ins, r_in = refs[:n_in], refs[n_in:n_in + n_ri]
        outs, r_out = refs[n_in + n_ri:n_in + n_ri + n_out], refs[n_in + n_ri + n_out:n_in + n_ri + n_out + n_ro]
        ssem, rsem = refs[-2], refs[-1]
        first = functools.reduce(lambda a, b: a & b, [pl.program_id(k) == 0 for k in range(len(grid))])
        last = functools.reduce(lambda a, b: a & b, [pl.program_id(k) == grid[k] - 1 for k in range(len(grid))])

        @pl.when(first)
        def _():
            rider.start(r_in, r_out, ssem, rsem)
        body(*ins, *outs)

        @pl.when(last)
        def _():
            rider.wait(r_in, r_out, ssem, rsem)

    return pl.pallas_call(
        hosted, name=name, out_shape=list(out_shape) + list(rider.outs), grid=grid,
        in_specs=list(in_specs) + [ANY] * n_ri, out_specs=list(out_specs) + [ANY] * n_ro,
        input_output_aliases=aliases,
        scratch_shapes=[pltpu.SemaphoreType.DMA((rider.nsem,)), pltpu.SemaphoreType.DMA((rider.nsem,))],
        compiler_params=_cp(("arbitrary",) * len(grid), VMEM_BIG))(*operands, *rider.ins)


def ffn_up(lay, hm, wbuf, ig, iu, ns, name, rider=None):
    tm = lay.tm

    def body(h_ref, wg_ref, wu_ref, g_ref, u_ref, a_ref):
        for rs in lay.sub_rows():
            hb = h_ref[rs, :]
            g = _nt(hb, wg_ref[0])
            u = _nt(hb, wu_ref[0])
            g_ref[0, rs, :] = g.astype(BF16)
            u_ref[0, rs, :] = u.astype(BF16)
            a_ref[0, rs, :] = (g * _sigmoid(g) * u).astype(BF16)

    spec_o = pl.BlockSpec((1, tm, ns), lambda s, i: (s, i, 0))
    return _host_call(
        body, rider, name, (N_CHIP, lay.T // tm),
        [pl.BlockSpec((tm, D), lambda s, i: (i, 0)), pl.BlockSpec((1, ns, D), lambda s, i: (s, ig, 0)),
         pl.BlockSpec((1, ns, D), lambda s, i: (s, iu, 0))],
        [spec_o] * 3, [_sds((N_CHIP, lay.T, ns), BF16)] * 3, (hm, wbuf, wbuf), ("parallel", "parallel"), 3, 3)


def slab_nn_acc(lay, zs, wbuf, idxs, ns, name, rider=None):
    tm = lay.tm
    npair = len(zs)

    def body(*refs):
        o_ref = refs[-1]

        @pl.when(pl.program_id(1) == 0)
        def _():
            o_ref[...] = jnp.zeros_like(o_ref)
        acc = _nn(refs[0][0], refs[npair][0])
        for p in range(1, npair):
            acc += _nn(refs[p][0], refs[npair + p][0])
        o_ref[...] += acc

    in_specs = [pl.BlockSpec((1, tm, ns), lambda i, s: (s, i, 0)) for _ in zs]
    in_specs += [pl.BlockSpec((1, ns, D), functools.partial(lambda i, s, q: (s, q, 0), q=q)) for q in idxs]
    return _host_call(body, rider, name, (lay.T // tm, N_CHIP), in_specs, [pl.BlockSpec((tm, D), lambda i, s: (i, 0))],
                      [_sds((lay.T, D), F32)], (*zs, *([wbuf] * npair)), ("parallel", "arbitrary"), 2 * npair, 1)


def ffn_bwd_da(lay, dy, wbuf, idn, g, u, ns, name, rider=None):
    tm = lay.tm

    def body(dy_ref, wd_ref, g_ref, u_ref, dg_ref, du_ref):
        for rs in lay.sub_rows():
            da = _nt(dy_ref[rs, :], wd_ref[0])
            gv = g_ref[0, rs, :].astype(F32)
            uv = u_ref[0, rs, :].astype(F32)
            sg = _sigmoid(gv)
            dg_ref[0, rs, :] = (da * uv * (sg * (1.0 + gv * (1.0 - sg)))).astype(BF16)
            du_ref[0, rs, :] = (da * (gv * sg)).astype(BF16)

    spec_z = pl.BlockSpec((1, tm, ns), lambda s, i: (s, i, 0))
    return _host_call(
        body, rider, name, (N_CHIP, lay.T // tm),
        [pl.BlockSpec((tm, D), lambda s, i: (i, 0)), pl.BlockSpec((1, ns, D), lambda s, i: (s, idn, 0)), spec_z, spec_z],
        [spec_z] * 2, [_sds((N_CHIP, lay.T, ns), BF16)] * 2, (dy, wbuf, g, u), ("parallel", "parallel"), 4, 2)


def slab_tn(lay, z, x, gbuf, idx, ns, name, rider=None):
    tk = lay.tm

    def body(z_ref, x_ref, g_in, o_ref):
        del g_in

        @pl.when(pl.program_id(1) == 0)
        def _():
            o_ref[...] = jnp.zeros_like(o_ref)
        o_ref[0] += _tn(z_ref[0], x_ref[...])

    return _host_call(
        body, rider, name, (N_CHIP, lay.T // tk),
        [pl.BlockSpec((1, tk, ns), lambda s, k: (s, k, 0)), pl.BlockSpec((tk, D), lambda s, k: (k, 0)), ANY],
        [pl.BlockSpec((1, ns, D), lambda s, k: (s, idx, 0))], [_sds(gbuf.shape, F32)], (z, x, gbuf),
        ("parallel", "arbitrary"), 3, 1, aliases={2: 0})


Q0, K0, V0, U0, QR0, KR0, PEXT = 0, 512, 640, 768, 1280, 1792, 1920


def rope_fwd(lay, p, cos, sin, name):
    def fn(pb, cs, sn):
        cs4 = jnp.concatenate([cs] * 4, axis=1)
        sn4 = jnp.concatenate([sn] * 4, axis=1)
        qr = pb[:, Q0:K0] * cs4 + pb[:, QR0:KR0] * sn4
        kr = pb[:, K0:V0] * cs + pb[:, KR0:PEXT] * sn
        return qr, kr, pb[:, V0:U0], pb[:, U0:QR0]
    return rowwise(lay, name, fn, [p, cos, sin], outs=[(ATT_W, BF16), (KV_W, BF16), (KV_W, BF16), (POOL_W, F32)])


def rope_bwd(lay, dqr, dkr, dv, du, cos, sin, name):
    def fn(dq, dk, dvb, dub, cs, sn):
        cs4 = jnp.concatenate([cs] * 4, axis=1)
        sn4 = jnp.concatenate([sn] * 4, axis=1)
        return (jnp.concatenate([dq * cs4, dk * cs, dvb, dub, dq * sn4, dk * sn], axis=1),)
    return rowwise(lay, name, fn, [dqr, dkr, dv, du, cos, sin], outs=[(PEXT, BF16)])[0]


def _attn_specs(lay):
    nbs, cbk, lbk = lay.PS // BLK, lay.C // BLK, lay.L // BLK

    def kv_map(j):
        return lambda s, n: (s * nbs + cbk + jnp.clip(n - cbk + j - 1, 0, lbk - 1), 0)

    win = [pl.BlockSpec((BLK, KV_W), kv_map(j)) for j in range(3)]
    ctx = pl.BlockSpec((lay.C, KV_W), lambda s, n: (s * (lay.PS // lay.C), 0))
    return nbs, cbk, lbk, win, ctx


def _attn_masks(n, cbk, lbk):
    row = lax.broadcasted_iota(jnp.int32, (BLK, BLK), 0)
    col = lax.broadcasted_iota(jnp.int32, (BLK, BLK), 1)
    m = n - cbk
    lat = n >= cbk
    valid = [lat & (m >= 1) & (col >= row), lat & (col >= 0), lat & (m <= lbk - 2) & (col <= row)]
    lane_lo = lax.broadcasted_iota(jnp.int32, (BLK, 2 * HEAD_DIM), 1) < HEAD_DIM
    return valid, lane_lo


def attn_fwd(lay, qr, kr, vb, sink_tab, name):
    nbs, cbk, lbk, win, ctx = _attn_specs(lay)

    def body(q_ref, k0, k1, k2, kc_ref, v0, v1, v2, vc_ref, sk_ref, o_ref, l_ref):
        n = pl.program_id(1)
        valid, lane_lo = _attn_masks(n, cbk, lbk)
        valid4 = [jnp.concatenate([v] * 4, axis=0) for v in valid]
        ks = [k0[...], k1[...], k2[...]]
        vs = [v0[...], v1[...], v2[...]]
        kc, vc = kc_ref[...], vc_ref[...]
        q2s = [q_ref[:, p * 128:(p + 1) * 128] for p in range(4)]
        outs, lses = [], []
        for hh in range(2):
            sel = lane_lo == (hh == 0)
            qm = jnp.concatenate([jnp.where(sel, q2, jnp.zeros_like(q2)) for q2 in q2s], axis=0)
            sk = jnp.concatenate([jnp.broadcast_to(sk_ref[p:p + 1, hh * HEAD_DIM:hh * HEAD_DIM + 1], (BLK, 1))
                                  for p in range(4)], axis=0)
            sw = [jnp.where(valid4[j], _nt(qm, ks[j]) * ATT_SCALE, NEG_INF) for j in range(3)]
            sc = _nt(qm, kc) * ATT_SCALE
            mx = jnp.maximum(jnp.maximum(jnp.maximum(sw[0].max(-1, keepdims=True), sw[1].max(-1, keepdims=True)),
                                         jnp.maximum(sw[2].max(-1, keepdims=True), sc.max(-1, keepdims=True))), sk)
            ew = [jnp.exp(s - mx) for s in sw]
            ec = jnp.exp(sc - mx)
            den = ew[0].sum(-1, keepdims=True) + ew[1].sum(-1, keepdims=True) + ew[2].sum(-1, keepdims=True)
            den = den + ec.sum(-1, keepdims=True) + jnp.exp(sk - mx)
            o = _nn((ec / den).astype(BF16), vc)
            for j in range(3):
                o += _nn((ew[j] / den).astype(BF16), vs[j])
            outs.append(o)
            lses.append(mx + jnp.log(den))
        for p in range(4):
            rows = slice(p * BLK, (p + 1) * BLK)
            o_ref[:, p * 128:(p + 1) * 128] = jnp.where(lane_lo, outs[0][rows], outs[1][rows]).astype(o_ref.dtype)
            l_ref[:, p * 128:(p + 1) * 128] = jnp.where(lane_lo, jnp.broadcast_to(lses[0][rows], (BLK, 128)),
                                                        jnp.broadcast_to(lses[1][rows], (BLK, 128)))

    qspec = pl.BlockSpec((BLK, ATT_W), lambda s, n: (s * nbs + n, 0))
    return pl.pallas_call(
        body, name=name, out_shape=[_sds((lay.T, ATT_W), BF16), _sds((lay.T, ATT_W), F32)], grid=(2, nbs),
        in_specs=[qspec] + win + [ctx] + win + [ctx] + [pl.BlockSpec((8, 128), lambda s, n: (0, 0))],
        out_specs=[qspec, qspec], compiler_params=_cp(("parallel", "parallel")))(qr, kr, kr, kr, kr, vb, vb, vb, vb, sink_tab)


def attn_bwd(lay, qr, kr, vb, sink_tab, lse, datt, name):
    nbs, cbk, lbk, win, ctx = _attn_specs(lay)
    C, PS = lay.C, lay.PS

    def body(q_ref, k0, k1, k2, kc_ref, v0, v1, v2, vc_ref, sk_ref, l_ref, do_ref, dq_ref, dk_ref, dv_ref, ds_ref):
        n = pl.program_id(1)
        valid, lane_lo = _attn_masks(n, cbk, lbk)

        @pl.when(n == 0)
        def _():
            dk_ref[...] = jnp.zeros_like(dk_ref)
            dv_ref[...] = jnp.zeros_like(dv_ref)
            ds_ref[...] = jnp.zeros_like(ds_ref)

        ks = [k0[...], k1[...], k2[...], kc_ref[...]]
        vs = [v0[...], v1[...], v2[...], vc_ref[...]]
        valid4 = [jnp.concatenate([v] * 4, axis=0) for v in valid]
        dks = [jnp.zeros((BLK, KV_W), F32)] * 3 + [jnp.zeros((C, KV_W), F32)]
        dvs = list(dks)
        q2s = [q_ref[:, p * 128:(p + 1) * 128] for p in range(4)]
        do2s = [do_ref[:, p * 128:(p + 1) * 128].astype(BF16) for p in range(4)]
        lse2s = [l_ref[:, p * 128:(p + 1) * 128] for p in range(4)]
        dq_h, dd_h = [], []
        for hh in range(2):
            sel = lane_lo == (hh == 0)
            qm = jnp.concatenate([jnp.where(sel, q2, jnp.zeros_like(q2)) for q2 in q2s], axis=0)
            dom = jnp.concatenate([jnp.where(sel, d2, jnp.zeros_like(d2)) for d2 in do2s], axis=0)
            lse_h = jnp.concatenate([l2[:, hh * HEAD_DIM:hh * HEAD_DIM + 1] for l2 in lse2s], axis=0)
            ps, dps = [], []
            for j in range(4):
                s = _nt(qm, ks[j]) * ATT_SCALE
                if j < 3:
                    s = jnp.where(valid4[j], s, NEG_INF)
                ps.append(jnp.exp(s - lse_h))
                dps.append(_nt(dom, vs[j]))
            dd = (ps[0] * dps[0]).sum(-1, keepdims=True) + (ps[1] * dps[1]).sum(-1, keepdims=True)
            dd = dd + (ps[2] * dps[2]).sum(-1, keepdims=True) + (ps[3] * dps[3]).sum(-1, keepdims=True)
            dq = jnp.zeros((4 * BLK, 128), F32)
            for j in range(4):
                dsb = (ps[j] * (dps[j] - dd) * ATT_SCALE).astype(BF16)
                dq += _nn(dsb, ks[j])
                dks[j] = dks[j] + _tn(dsb, qm)
                dvs[j] = dvs[j] + _tn(ps[j].astype(BF16), dom)
            dq_h.append(dq)
            dd_h.append(dd)
        for p in range(4):
            sl = slice(p * 128, (p + 1) * 128)
            rows = slice(p * BLK, (p + 1) * BLK)
            dq_ref[:, sl] = jnp.where(lane_lo, dq_h[0][rows], dq_h[1][rows])
            dd2 = jnp.where(lane_lo, jnp.broadcast_to(dd_h[0][rows], (BLK, 128)), jnp.broadcast_to(dd_h[1][rows], (BLK, 128)))
            psink = jnp.exp(sk_ref[p:p + 1, :] - lse2s[p])
            ds_ref[0, p:p + 1, :] += -jnp.sum(psink * dd2, axis=0, keepdims=True)
        dk_ref[0:C, :] += dks[3]
        dv_ref[0:C, :] += dvs[3]
        for j in range(3):
            r0 = pl.multiple_of((cbk + jnp.clip(n - cbk + j - 1, 0, lbk - 1)) * BLK, BLK)
            dk_ref[pl.ds(r0, BLK), :] += dks[j]
            dv_ref[pl.ds(r0, BLK), :] += dvs[j]

    qspec = pl.BlockSpec((BLK, ATT_W), lambda s, n: (s * nbs + n, 0))
    kvout = pl.BlockSpec((PS, KV_W), lambda s, n: (s, 0))
    return pl.pallas_call(
        body, name=name,
        out_shape=[_sds((lay.T, ATT_W), F32), _sds((lay.T, KV_W), F32), _sds((lay.T, KV_W), F32), _sds((2, 8, 128), F32)],
        grid=(2, nbs),
        in_specs=[qspec] + win + [ctx] + win + [ctx] + [pl.BlockSpec((8, 128), lambda s, n: (0, 0)), qspec, qspec],
        out_specs=[qspec, kvout, kvout, pl.BlockSpec((1, 8, 128), lambda s, n: (s, 0, 0))],
        compiler_params=_cp(("parallel", "arbitrary")))(qr, kr, kr, kr, kr, vb, vb, vb, vb, sink_tab, lse, datt)


def _winsum(x, r):
    n = x.shape[0]
    t = lax.broadcasted_iota(jnp.int32, x.shape, 0)
    acc = x
    for o in range(1, r + 1):
        acc = acc + jnp.where(t >= o, pltpu.roll(x, o, 0), 0.0) + jnp.where(t < n - o, pltpu.roll(x, n - o, 0), 0.0)
    return acc


def _wincount(n, r):
    t = lax.broadcasted_iota(jnp.int32, (n, 128), 0)
    return (jnp.minimum(t + r, n - 1) - jnp.maximum(t - r, 0) + 1).astype(F32)


def pool_fwd(lay, u, w_pool, scale, name):
    segs = [(0, lay.C), (lay.C, lay.L)]

    def body(u_ref, w_ref, s_ref, o_ref):
        for r0, n in segs:
            for g, wd in enumerate(POOL_WINDOWS):
                sl = slice(g * 128, (g + 1) * 128)
                x = u_ref[r0:r0 + n, sl]
                d = _winsum(x, wd // 2) / _wincount(n, wd // 2) - x
                y = _nn(d.astype(BF16), w_ref[g].astype(BF16)) * s_ref[:, sl]
                o_ref[r0:r0 + n, sl] = y.astype(o_ref.dtype)

    spec = pl.BlockSpec((lay.PS, POOL_W), lambda s: (s, 0))
    return pl.pallas_call(
        body, name=name, out_shape=_sds((lay.T, POOL_W), BF16), grid=(2,),
        in_specs=[spec, pl.BlockSpec(w_pool.shape, lambda s: (0, 0, 0)), pl.BlockSpec((1, POOL_W), lambda s: (0, 0))],
        out_specs=spec, compiler_params=_cp(("parallel",), VMEM_BIG))(u, w_pool, scale)


def pool_bwd(lay, u, dcat, w_pool, scale, name):
    segs = [(0, lay.C), (lay.C, lay.L)]

    def body(u_ref, dp_ref, w_ref, s_ref, du_ref, dw_ref, dsc_ref):
        for g, wd in enumerate(POOL_WINDOWS):
            sl = slice(g * 128, (g + 1) * 128)
            wb = w_ref[g].astype(BF16)
            dw = jnp.zeros((128, 128), F32)
            dsc = jnp.zeros((1, 128), F32)
            for r0, n in segs:
                x = u_ref[r0:r0 + n, sl]
                cnt = _wincount(n, wd // 2)
                d = (_winsum(x, wd // 2) / cnt - x).astype(BF16)
                dp = dp_ref[r0:r0 + n, sl]
                dsc += jnp.sum(_nn(d, wb) * dp, axis=0, keepdims=True)
                dyp = (dp * s_ref[:, sl]).astype(BF16)
                dw += _tn(d, dyp)
                dd = _nt(dyp, wb)
                du_ref[r0:r0 + n, sl] = _winsum(dd / cnt, wd // 2) - dd
            dw_ref[0, g] = dw
            dsc_ref[0, :, sl] = dsc

    spec = pl.BlockSpec((lay.PS, POOL_W), lambda s: (s, 0))
    return pl.pallas_call(
        body, name=name,
        out_shape=[_sds((lay.T, POOL_W), F32), _sds((2, 4, 128, 128), F32), _sds((2, 1, POOL_W), F32)], grid=(2,),
        in_specs=[spec, pl.BlockSpec((lay.PS, POOL_W), lambda s: (s, 1)), pl.BlockSpec(w_pool.shape, lambda s: (0, 0, 0)),
                  pl.BlockSpec((1, POOL_W), lambda s: (0, 0))],
        out_specs=[spec, pl.BlockSpec((1, 4, 128, 128), lambda s: (s, 0, 0, 0)), pl.BlockSpec((1, 1, POOL_W), lambda s: (s, 0, 0))],
        compiler_params=_cp(("parallel",), VMEM_BIG))(u, dcat, w_pool, scale)


CONV_OFFS = (-1, 0, 1, 2)
CW = 256


def _shift_rows(x, o):
    if o == 0:
        return x
    n = x.shape[0]
    t = lax.broadcasted_iota(jnp.int32, x.shape, 0)
    if o < 0:
        return jnp.where(t >= -o, pltpu.roll(x, -o, 0), 0.0)
    return jnp.where(t < n - o, pltpu.roll(x, n - o, 0), 0.0)


def conv_fwd(lay, p, col0, w, b, name):
    segs = [(0, lay.C), (lay.C, lay.L)]
    cb0 = col0 // CW

    def body(x_ref, w_ref, b_ref, o_ref):
        for r0, n in segs:
            x = x_ref[r0:r0 + n, :]
            y = jnp.broadcast_to(b_ref[...], x.shape)
            for k, o in enumerate(CONV_OFFS):
                y = y + _shift_rows(x, o) * w_ref[k:k + 1, :]
            o_ref[r0:r0 + n, :] = y

    return pl.pallas_call(
        body, name=name, out_shape=_sds((lay.T, D), F32), grid=(2, D // CW),
        in_specs=[pl.BlockSpec((lay.PS, CW), lambda s, j: (s, cb0 + j)), pl.BlockSpec((4, CW), lambda s, j: (0, j)),
                  pl.BlockSpec((1, CW), lambda s, j: (0, j))],
        out_specs=pl.BlockSpec((lay.PS, CW), lambda s, j: (s, j)),
        compiler_params=_cp(("parallel", "parallel")))(p, w, b)


def conv_bwd(lay, p, col0, w, duc, name):
    segs = [(0, lay.C), (lay.C, lay.L)]
    cb0 = col0 // CW

    def body(x_ref, w_ref, g_ref, du_ref, dw_ref, db_ref):
        dws = [jnp.zeros((1, CW), F32)] * 4
        db = jnp.zeros((1, CW), F32)
        for r0, n in segs:
            x = x_ref[r0:r0 + n, :]
            g = g_ref[r0:r0 + n, :]
            du = jnp.zeros_like(g)
            for k, o in enumerate(CONV_OFFS):
                du = du + _shift_rows(g, -o) * w_ref[k:k + 1, :]
                dws[k] = dws[k] + jnp.sum(g * _shift_rows(x, o), axis=0, keepdims=True)
            db = db + jnp.sum(g, axis=0, keepdims=True)
            du_ref[r0:r0 + n, :] = du
        dw_ref[0] = jnp.concatenate(dws, axis=0)
        db_ref[0] = db

    return pl.pallas_call(
        body, name=name, out_shape=[_sds((lay.T, D), F32), _sds((2, 4, D), F32), _sds((2, 1, D), F32)], grid=(2, D // CW),
        in_specs=[pl.BlockSpec((lay.PS, CW), lambda s, j: (s, cb0 + j)), pl.BlockSpec((4, CW), lambda s, j: (0, j)),
                  pl.BlockSpec((lay.PS, CW), lambda s, j: (s, j))],
        out_specs=[pl.BlockSpec((lay.PS, CW), lambda s, j: (s, j)), pl.BlockSpec((1, 4, CW), lambda s, j: (s, 0, j)),
                   pl.BlockSpec((1, 1, CW), lambda s, j: (s, 0, j))],
        compiler_params=_cp(("parallel", "parallel")))(p, w, duc)


def _softplus_neg(lam):
    z = -lam
    w = jnp.exp(-jnp.abs(z))
    log1p = jnp.where(w < 1e-2, w * (1.0 - w * (0.5 - w / 3.0)), jnp.log(1.0 + w))
    return jnp.maximum(z, 0.0) + log1p, -_sigmoid(z)


def _neg_expm1(x):
    series = -x * (1.0 + x * (0.5 + x * (1.0 / 6.0 + x * (1.0 / 24.0 + x * (1.0 / 120.0)))))
    return jnp.where(x > -0.05, series, 1.0 - jnp.exp(x))


def _lru_gates(x, xb, wa, wx, ba, bx, lam):
    r = _sigmoid(_nn(xb, wa.astype(BF16)) + ba)
    gi = _sigmoid(_nn(xb, wx.astype(BF16)) + bx)
    sp, dsp = _softplus_neg(lam)
    la = -LRU_C * r * sp
    a = jnp.exp(la)
    sq = jnp.sqrt(_neg_expm1(2.0 * la))
    return r, gi, sp, dsp, a, sq


def lru_coeffs(lay, uc, wa, wx, vec, name):
    tr = lay.tc

    def body(x_ref, wa_ref, wx_ref, v_ref, a_ref, b_ref):
        for h in range(8):
            sl = slice(h * 128, (h + 1) * 128)
            x = x_ref[:, sl]
            xb = x.astype(BF16)
            for d in range(2):
                _, gi, _, _, a, sq = _lru_gates(x, xb, wa_ref[d, h], wx_ref[d, h], v_ref[d:d + 1, sl],
                                                v_ref[2 + d:3 + d, sl], v_ref[4 + d:5 + d, sl])
                a_ref[d, h] = a
                b_ref[d, h] = sq * (gi * x)

    wspec = pl.BlockSpec((2, 8, 128, 128), lambda i: (0, 0, 0, 0))
    ospec = pl.BlockSpec((2, 8, tr, 128), lambda i: (0, 0, i, 0))
    return pl.pallas_call(
        body, name=name, out_shape=[_sds((2, 8, lay.T, 128), F32)] * 2, grid=(lay.T // tr,),
        in_specs=[pl.BlockSpec((tr, D), lambda i: (i, 0)), wspec, wspec, pl.BlockSpec((6, D), lambda i: (0, 0))],
        out_specs=[ospec, ospec], compiler_params=_cp(("parallel",), VMEM_BIG))(uc, wa, wx, vec)


def lru_coeffs_bwd(lay, uc, wa, wx, vec, da, db, name):
    tr = lay.tc

    def body(x_ref, wa_ref, wx_ref, v_ref, da_ref, db_ref, dx_ref, dwa_ref, dwx_ref, dv_ref):
        @pl.when(pl.program_id(0) == 0)
        def _():
            dwa_ref[...] = jnp.zeros_like(dwa_ref)
            dwx_ref[...] = jnp.zeros_like(dwx_ref)
            dv_ref[...] = jnp.zeros_like(dv_ref)

        for h in range(8):
            sl = slice(h * 128, (h + 1) * 128)
            x = x_ref[:, sl]
            xb = x.astype(BF16)
            dx = jnp.zeros_like(x)
            for d in range(2):
                wab, wxb = wa_ref[d, h].astype(BF16), wx_ref[d, h].astype(BF16)
                r, gi, sp, dsp, a, sq = _lru_gates(x, xb, wa_ref[d, h], wx_ref[d, h], v_ref[d:d + 1, sl],
                                                   v_ref[2 + d:3 + d, sl], v_ref[4 + d:5 + d, sl])
                dbv, dav = db_ref[d, h], da_ref[d, h]
                t1 = dbv * sq
                dgi = t1 * x
                dx = dx + t1 * gi
                dla = dav * a - (dbv * gi * x) * (a * a) / sq
                dr = dla * (-LRU_C * sp)
                dlam = jnp.sum(dla * (-LRU_C * r), axis=0, keepdims=True) * dsp
                dpa = dr * r * (1.0 - r)
                dpx = dgi * gi * (1.0 - gi)
                dpab, dpxb = dpa.astype(BF16), dpx.astype(BF16)
                dwa_ref[d, h] += _tn(xb, dpab)
                dwx_ref[d, h] += _tn(xb, dpxb)
                dx = dx + _nt(dpab, wab) + _nt(dpxb, wxb)
                dv_ref[d:d + 1, sl] += jnp.sum(dpa, axis=0, keepdims=True)
                dv_ref[2 + d:3 + d, sl] += jnp.sum(dpx, axis=0, keepdims=True)
                dv_ref[4 + d:5 + d, sl] += dlam
            dx_ref[:, sl] = dx

    wspec = pl.BlockSpec((2, 8, 128, 128), lambda i: (0, 0, 0, 0))
    gspec = pl.BlockSpec((2, 8, tr, 128), lambda i: (0, 0, i, 0))
    vspec = pl.BlockSpec((6, D), lambda i: (0, 0))
    xspec = pl.BlockSpec((tr, D), lambda i: (i, 0))
    return pl.pallas_call(
        body, name=name,
        out_shape=[_sds((lay.T, D), F32), _sds((2, 8, 128, 128), F32), _sds((2, 8, 128, 128), F32), _sds((6, D), F32)],
        grid=(lay.T // tr,), in_specs=[xspec, wspec, wspec, vspec, gspec, gspec],
        out_specs=[xspec, wspec, wspec, vspec], compiler_params=_cp(("arbitrary",), VMEM_BIG))(uc, wa, wx, vec, da, db)


GB = 2
SCAN_UNROLL = 4


def _tile_scan(a, b, up):
    t = lax.broadcasted_iota(jnp.int32, a.shape, 0)
    for d in (1, 2, 4):
        sh = 8 - d if up else d
        m = (t < 8 - d) if up else (t >= d)
        a_prev, b_prev = pltpu.roll(a, sh, 0), pltpu.roll(b, sh, 0)
        b = jnp.where(m, a * b_prev + b, b)
        a = jnp.where(m, a * a_prev, a)
    return a, b


def lru_scan(lay, a, b, name):
    segs = [(0, lay.C), (lay.C, lay.L)]

    def body(a_ref, b_ref, s_ref):
        for d in range(2):
            rev = d == 1
            state = tuple(jnp.zeros((1, 128), F32) for _ in range(GB))
            for base, n in segs:
                nt = n // 8

                def step(j, c, base=base, nt=nt, rev=rev, d=d):
                    c = list(c)
                    for u in range(SCAN_UNROLL):
                        jj = j * SCAN_UNROLL + u
                        r0 = pl.multiple_of(base + 8 * ((nt - 1 - jj) if rev else jj), 8)
                        for g in range(GB):
                            at, bt = _tile_scan(a_ref[d, g, pl.ds(r0, 8), :], b_ref[d, g, pl.ds(r0, 8), :], rev)
                            h = at * c[g] + bt
                            s_ref[d, g, pl.ds(r0, 8), :] = h
                            c[g] = h[0:1] if rev else h[7:8]
                    return tuple(c)

                state = lax.fori_loop(0, nt // SCAN_UNROLL, step, state)

    spec = pl.BlockSpec((2, GB, lay.PS, 128), lambda s, hb: (0, hb, s, 0))
    return pl.pallas_call(
        body, name=name, out_shape=_sds((2, 8, lay.T, 128), F32), grid=(2, 8 // GB),
        in_specs=[spec, spec], out_specs=spec, compiler_params=_cp(("parallel", "parallel"), VMEM_BIG))(a, b)


def lru_scan_bwd(lay, a, s, dy, name):
    segs = [(0, lay.C), (lay.C, lay.L)]
    C, PS = lay.C, lay.PS

    def body(a_ref, s_ref, g_ref, da_ref, db_ref):
        t = lax.broadcasted_iota(jnp.int32, (8, 128), 0)
        for d in range(2):
            rev = d == 1
            carry = tuple(jnp.zeros((1, 128), F32) for _ in range(GB))
            for si in (1, 0):
                base, n = segs[si]
                nt = n // 8

                def step(j, c, base=base, nt=nt, rev=rev, d=d):
                    c = list(c)
                    for u in range(SCAN_UNROLL):
                        jj = j * SCAN_UNROLL + u
                        r0 = pl.multiple_of(base + 8 * (jj if rev else (nt - 1 - jj)), 8)
                        if rev:
                            rn = pl.multiple_of(jnp.where(r0 == PS - 8, 0, r0 + 8), 8)
                            nb_zero = r0 == C - 8
                        else:
                            rn = pl.multiple_of(jnp.maximum(r0 - 8, 0), 8)
                            nb_zero = r0 == 0
                        for g in range(GB):
                            av = a_ref[d, g, pl.ds(r0, 8), :]
                            gv = g_ref[g, pl.ds(r0, 8), :]
                            sv = s_ref[d, g, pl.ds(r0, 8), :]
                            nbt = s_ref[d, g, pl.ds(rn, 8), :]
                            at, bt = _tile_scan(av, av * gv, not rev)
                            m = at * c[g] + bt
                            if rev:
                                m_next = jnp.where(t >= 1, pltpu.roll(m, 1, 0), c[g])
                                nb = jnp.where(nb_zero, 0.0, nbt[0:1])
                                h_prev = jnp.where(t < 7, pltpu.roll(sv, 7, 0), nb)
                                c[g] = m[7:8]
                            else:
                                m_next = jnp.where(t < 7, pltpu.roll(m, 7, 0), c[g])
                                nb = jnp.where(nb_zero, 0.0, nbt[7:8])
                                h_prev = jnp.where(t >= 1, pltpu.roll(sv, 1, 0), nb)
                                c[g] = m[0:1]
                            lam = gv + m_next
                            db_ref[d, g, pl.ds(r0, 8), :] = lam
                            da_ref[d, g, pl.ds(r0, 8), :] = lam * h_prev
                    return tuple(c)

                carry = lax.fori_loop(0, nt // SCAN_UNROLL, step, carry)

    spec = pl.BlockSpec((2, GB, lay.PS, 128), lambda s, hb: (0, hb, s, 0))
    return pl.pallas_call(
        body, name=name, out_shape=[_sds((2, 8, lay.T, 128), F32)] * 2, grid=(2, 8 // GB),
        in_specs=[spec, spec, pl.BlockSpec((GB, lay.PS, 128), lambda s, hb: (hb, s, 0))],
        out_specs=[spec, spec], compiler_params=_cp(("parallel", "parallel"), VMEM_BIG))(a, s, dy)


def _gelu(x):
    k = math.sqrt(2.0 / math.pi)
    t = jnp.tanh(k * (x + 0.044715 * x * x * x))
    return 0.5 * x * (1.0 + t), 0.5 * (1.0 + t) + 0.5 * x * (1.0 - t * t) * k * (1.0 + 3 * 0.044715 * x * x)


def lru_gate(lay, p, s, name):
    tr = lay.tr

    def body(g_ref, s_ref, o_ref):
        for h in range(8):
            sl = slice(h * 128, (h + 1) * 128)
            o_ref[:, sl] = (_gelu(g_ref[:, sl])[0] * (s_ref[0, h] + s_ref[1, h])).astype(o_ref.dtype)

    return pl.pallas_call(
        body, name=name, out_shape=_sds((lay.T, D), BF16), grid=(lay.nblk,),
        in_specs=[pl.BlockSpec((tr, D), lambda i: (i, 0)), pl.BlockSpec((2, 8, tr, 128), lambda i: (0, 0, i, 0))],
        out_specs=pl.BlockSpec((tr, D), lambda i: (i, 0)), compiler_params=_cp(("parallel",)))(p, s)


def lru_gate_bwd(lay, p, s, do, name):
    tr = lay.tr

    def body(g_ref, s_ref, do_ref, dg_ref, dy_ref):
        for h in range(8):
            sl = slice(h * 128, (h + 1) * 128)
            ge, dge = _gelu(g_ref[:, sl])
            dov = do_ref[:, sl]
            dg_ref[:, sl] = dov * (s_ref[0, h] + s_ref[1, h]) * dge
            dy_ref[h] = dov * ge

    xspec = pl.BlockSpec((tr, D), lambda i: (i, 0))
    return pl.pallas_call(
        body, name=name, out_shape=[_sds((lay.T, D), F32), _sds((8, lay.T, 128), F32)], grid=(lay.nblk,),
        in_specs=[xspec, pl.BlockSpec((2, 8, tr, 128), lambda i: (0, 0, i, 0)), xspec],
        out_specs=[xspec, pl.BlockSpec((8, tr, 128), lambda i: (0, i, 0))],
        compiler_params=_cp(("parallel",)))(p, s, do)


def silu_rows(x, name):
    def body(x_ref, o_ref):
        v = x_ref[...]
        o_ref[...] = (v * _sigmoid(v)).astype(o_ref.dtype)
    return pl.pallas_call(body, name=name, out_shape=_sds(x.shape, BF16), in_specs=[VMEM_SPEC], out_specs=VMEM_SPEC)(x)


def mod_grad_rows(gath, name):
    w = gath.shape[-1]

    def body(g_ref, dm_ref, db_ref):
        dm_ref[...] = jnp.zeros_like(dm_ref)
        for l in range(2):
            ctx = g_ref[0, 3 * l + 2:3 * l + 3, :]
            tot = g_ref[0, 3 * l:3 * l + 1, :] + g_ref[0, 3 * l + 1:3 * l + 2, :]
            for k in range(8):
                dm_ref[l, 2 * k:2 * k + 2, :] = g_ref[k, 3 * l:3 * l + 2, :]
                if k:
                    ctx = ctx + g_ref[k, 3 * l + 2:3 * l + 3, :]
                    tot = tot + (g_ref[k, 3 * l:3 * l + 1, :] + g_ref[k, 3 * l + 1:3 * l + 2, :])
            dm_ref[l, 16:17, :] = ctx
            db_ref[l:l + 1, :] = tot + ctx

    return pl.pallas_call(body, name=name, out_shape=[_sds((2, 32, w), F32), _sds((2, w), F32)],
                          in_specs=[VMEM_SPEC], out_specs=[VMEM_SPEC, VMEM_SPEC])(gath)


def cctx_grad(p, c_ctx, name):
    def body(a_ref, c_ref, o_ref):
        cv = c_ref[...]
        sg = _sigmoid(cv)
        o_ref[...] = 0.5 * (a_ref[0, 0:1, :] + a_ref[1, 0:1, :]) * (sg * (1.0 + cv * (1.0 - sg)))
    return pl.pallas_call(body, name=name, out_shape=_sds((1, D), F32), in_specs=[VMEM_SPEC] * 2,
                          out_specs=VMEM_SPEC)(p, c_ctx)


def loss_and_grad(lay, h, tgt, name):
    def fn(hb, tb):
        lat = (pl.program_id(0) % lay.bps) >= lay.cb
        e = jnp.where(lat, hb - tb, 0.0)
        return e * (1.0 / D), jnp.sum(e * e, axis=0, keepdims=True) * (0.5 / D)
    return rowwise(lay, name, fn, [h, tgt], outs=[(D, F32)], sums=[(1, D)])


def adamw(w, g, m, v, name):
    shape = w.shape
    w2, g2, m2, v2 = (t.reshape(-1, shape[-1]) for t in (w, g, m, v))
    rows, width = w2.shape
    tr = 256 if rows % 256 == 0 else rows
    c1 = 1.0 - ADAM_B1 ** ADAM_STEP
    c2 = 1.0 - ADAM_B2 ** ADAM_STEP

    def body(w_ref, g_ref, m_ref, v_ref, d_ref, mo_ref, vo_ref):
        gv = g_ref[...]
        mn = ADAM_B1 * m_ref[...] + (1.0 - ADAM_B1) * gv
        vn = ADAM_B2 * v_ref[...] + (1.0 - ADAM_B2) * (gv * gv)
        d_ref[...] = -ADAM_LR * ((mn / c1) / (jnp.sqrt(vn / c2) + ADAM_EPS) + ADAM_WD * w_ref[...])
        mo_ref[...] = mn
        vo_ref[...] = vn

    spec = pl.BlockSpec((tr, width), lambda i: (i, 0))
    d, mn, vn = pl.pallas_call(body, name=name, out_shape=[_sds((rows, width), F32)] * 3, grid=(rows // tr,),
                               in_specs=[spec] * 4, out_specs=[spec] * 3, compiler_params=_cp(("parallel",)))(w2, g2, m2, v2)
    return d.reshape(shape), mn.reshape(shape), vn.reshape(shape)


def adamw_ffn(w, m, v, red, kind, ns, name):
    shape = w.shape
    w2, m2, v2 = (t.reshape(-1, shape[-1]) for t in (w, m, v))
    rows, width = w2.shape
    c1 = 1.0 - ADAM_B1 ** ADAM_STEP
    c2 = 1.0 - ADAM_B2 ** ADAM_STEP
    if kind < 2:
        tr, nb = 256, D // 256
        gspec = pl.BlockSpec((ns, tr), lambda i: ((i // nb) * 3 + kind, i % nb))
    else:
        tr, nb = ns // 2, 2
        gspec = pl.BlockSpec((tr, D), lambda i: (((i // nb) * 3 + kind) * nb + i % nb, 0))

    def body(w_ref, g_ref, m_ref, v_ref, go_ref, d_ref, mo_ref, vo_ref):
        gv = g_ref[...].T if kind < 2 else g_ref[...]
        mn = ADAM_B1 * m_ref[...] + (1.0 - ADAM_B1) * gv
        vn = ADAM_B2 * v_ref[...] + (1.0 - ADAM_B2) * (gv * gv)
        go_ref[...] = gv
        d_ref[...] = -ADAM_LR * ((mn / c1) / (jnp.sqrt(vn / c2) + ADAM_EPS) + ADAM_WD * w_ref[...])
        mo_ref[...] = mn
        vo_ref[...] = vn

    spec = pl.BlockSpec((tr, width), lambda i: (i, 0))
    outs = pl.pallas_call(body, name=name, out_shape=[_sds((rows, width), F32)] * 4, grid=(rows // tr,),
                          in_specs=[spec, gspec, spec, spec], out_specs=[spec] * 4,
                          compiler_params=_cp(("parallel",)))(w2, red, m2, v2)
    return tuple(t.reshape(shape) for t in outs)


def mod_mm(sc, w_mod, bias, name):
    wm = w_mod.shape[-1]
    tn = _pick(wm, (768, 512, 384, 256, 128))

    def body(a_ref, b_ref, c_ref, o_ref):
        o_ref[...] = _nn(a_ref[...], b_ref[...].astype(BF16)) + c_ref[...]

    return pl.pallas_call(
        body, name=name, out_shape=_sds((DEPTH, 32, wm), F32), grid=(DEPTH, wm // tn),
        in_specs=[pl.BlockSpec((32, D), lambda l, j: (0, 0)), pl.BlockSpec((None, D, tn), lambda l, j: (l, 0, j)),
                  pl.BlockSpec((None, 1, tn), lambda l, j: (l, 0, j))],
        out_specs=pl.BlockSpec((None, 32, tn), lambda l, j: (l, 0, j)),
        compiler_params=_cp(("parallel", "parallel")))(sc, w_mod, bias)


def wmod_dw(sc, dcol, name):
    wm = dcol.shape[-1]
    tm = 256

    def body(a_ref, b_ref, o_ref):
        o_ref[...] = _tn(a_ref[...], b_ref[...].astype(BF16))

    return pl.pallas_call(
        body, name=name, out_shape=_sds((DEPTH, D, wm), F32), grid=(DEPTH, D // tm),
        in_specs=[pl.BlockSpec((32, tm), lambda l, i: (0, i)), pl.BlockSpec((None, 32, wm), lambda l, i: (l, 0, 0))],
        out_specs=pl.BlockSpec((None, tm, wm), lambda l, i: (l, i, 0)),
        compiler_params=_cp(("parallel", "parallel")))(sc, dcol)


def cctx_dx(drow, w_mod, name):
    wm = w_mod.shape[-1]

    def body(a_ref, b_ref, o_ref):
        o_ref[...] = _nt(a_ref[...].astype(BF16), b_ref[...].astype(BF16))

    return pl.pallas_call(
        body, name=name, out_shape=_sds((DEPTH, 16, D), F32), grid=(DEPTH,),
        in_specs=[pl.BlockSpec((None, 16, wm), lambda l: (l, 0, 0)), pl.BlockSpec((None, D, wm), lambda l: (l, 0, 0))],
        out_specs=pl.BlockSpec((None, 16, D), lambda l: (l, 0, 0)), compiler_params=_cp(("parallel",), VMEM_BIG))(drow, w_mod)


HEAD_PERM = (0, 4, 1, 5, 2, 6, 3, 7)


def _rot_rows(wt):
    return jnp.concatenate([-wt[32:64], wt[0:32]], axis=0)


def _unrot_rows(g):
    return jnp.concatenate([g[32:64], -g[0:32]], axis=0)


def _heads(a, n):
    return [a[64 * i:64 * (i + 1)] for i in range(n)]


def kernel(x, c, ctx, c_ctx, w_mod, b_mod, ln_g, ln_b, ffn_w_gate, ffn_w_up, ffn_w_down, mix_ab_w_in, attn_sink, pool_w, pool_scale, mix_ab_w_out, lru_w_in, lru_conv_w, lru_conv_b, lru_wa, lru_ba, lru_wx, lru_bx, lru_lambda, lru_w_out, loss_target, m_c_ctx, m_w_mod, m_b_mod, m_ln_g, m_ln_b, m_ffn_w_gate, m_ffn_w_up, m_ffn_w_down, m_mix_ab_w_in, m_attn_sink, m_pool_w, m_pool_scale, m_mix_ab_w_out, m_lru_w_in, m_lru_conv_w, m_lru_conv_b, m_lru_wa, m_lru_ba, m_lru_wx, m_lru_bx, m_lru_lambda, m_lru_w_out, v_c_ctx, v_w_mod, v_b_mod, v_ln_g, v_ln_b, v_ffn_w_gate, v_ffn_w_up, v_ffn_w_down, v_mix_ab_w_in, v_attn_sink, v_pool_w, v_pool_scale, v_mix_ab_w_out, v_lru_w_in, v_lru_conv_w, v_lru_conv_b, v_lru_wa, v_lru_ba, v_lru_wx, v_lru_bx, v_lru_lambda, v_lru_w_out):
    n_lat, n_ctx = x.shape[1], ctx.shape[1]
    lay = Layout(n_ctx, n_lat)
    T = lay.T
    ns = ffn_w_gate.shape[-1]
    n_li, n_ai = lru_w_in.shape[-1], mix_ab_w_in.shape[-1]
    n_ao, n_lo = mix_ab_w_out.shape[1], lru_w_out.shape[1]
    wm = w_mod.shape[-1]
    dsh = ln_g.shape[-1]
    mx, my, mc = lax.axis_index("x"), lax.axis_index("y"), lax.axis_index("c")
    chip = 2 * mx + my
    me = 2 * chip + mc

    c_all = all_gather8(c, "ag8_c").reshape(16, D)
    cc = jnp.concatenate([c_all, c_ctx[None, :], jnp.zeros((15, D), F32)], axis=0)
    sc = silu_rows(cc, "silu_c")
    bias = lax.dynamic_slice(b_mod, (0, chip * wm), (DEPTH, wm)).reshape(DEPTH, 1, wm)
    modg = all_gather_chips(mod_mm(sc, w_mod, bias, "mod_mm"), "ag_mod")
    modtab = []
    for l in range(DEPTH):
        full = jnp.transpose(modg[:, l], (1, 0, 2)).reshape(32, N_CHIP * wm)
        mine = lax.dynamic_slice(full, (2 * me, 0), (2, N_CHIP * wm))
        modtab.append(jnp.concatenate([mine, full[16:17]], axis=0).reshape(3, N_MOD, D))

    small = jnp.concatenate([ln_g.reshape(6, dsh), ln_b.reshape(6, dsh), lru_conv_w[0], lru_conv_b, lru_ba[0],
                             lru_bx[0], lru_lambda[0], jnp.zeros((9, dsh), F32)], axis=0)
    small = all_gather_chips(small.reshape(2, 16, dsh), "ag_small").reshape(N_CHIP, 32, dsh)
    small = jnp.transpose(small, (1, 0, 2)).reshape(32, D)
    ln_g_f, ln_b_f = small[0:6].reshape(2, 3, D), small[6:12].reshape(2, 3, D)
    conv_w_f, conv_b_f = small[12:16], small[16:17]
    lru_vec = small[17:23]

    hh = 3 * ns // 2
    placed = [ffn_place(ffn_w_gate, ffn_w_up, ffn_w_down, g // 2, g % 2, f"ag_ffn{g}_place") for g in range(4)]
    wb = [gather_placed(placed[0], "ag_ffn0"), None, None, None]
    mix_sh = jnp.concatenate([lru_w_in[0].T, mix_ab_w_in[0].T, mix_ab_w_out[0], lru_w_out[0]], axis=0).astype(BF16)
    n_mix = n_li + n_ai + n_ao + n_lo
    mixw = all_gather_chips(mix_sh.reshape(2, n_mix // 2, D), "ag_mix").reshape(N_CHIP, n_mix, D)
    o1, o2, o3 = n_li, n_li + n_ai, n_li + n_ai + n_ao
    lru_in_t = mixw[:, 0:o1].reshape(N_CHIP * n_li, D)
    ab_in_t = mixw[:, o1:o2].reshape(N_CHIP * n_ai, D)
    ab_out = mixw[:, o2:o3].reshape(N_CHIP * n_ao, D)
    lru_out = mixw[:, o3:].reshape(N_CHIP * n_lo, D)
    qh, kh = _heads(ab_in_t[Q0:K0], N_HEADS), _heads(ab_in_t[K0:V0], N_KV)
    w_ext_t = jnp.concatenate([qh[h] for h in HEAD_PERM] + [ab_in_t[K0:QR0]]
                              + [_rot_rows(qh[h]) for h in HEAD_PERM] + [_rot_rows(t) for t in kh], axis=0)
    oh = _heads(ab_out[0:ATT_W], N_HEADS)
    w_out_ext = jnp.concatenate([oh[h] for h in HEAD_PERM] + [ab_out[ATT_W:]], axis=0)

    t = jnp.arange(n_lat)
    inv = ROPE_THETA ** (-jnp.arange(16, dtype=F32) / 16.0)
    ang = jnp.concatenate([(t // GRID_W).astype(F32)[:, None] * inv, (t % GRID_W).astype(F32)[:, None] * inv], axis=-1)
    cos1 = jnp.concatenate([jnp.ones((n_ctx, 32), F32), jnp.cos(ang)], axis=0)
    sin1 = jnp.concatenate([jnp.zeros((n_ctx, 32), F32), jnp.sin(ang)], axis=0)
    cos_t = jnp.tile(cos1, (2, 4))
    sin_t = jnp.tile(sin1, (2, 4))
    sk = attn_sink[0]
    sink_tab = jnp.concatenate([jnp.repeat(jnp.stack([sk[:4], sk[4:]], axis=1), HEAD_DIM, axis=1),
                                jnp.zeros((4, 128), F32)], axis=0)
    pscale = pool_scale.reshape(1, POOL_W)

    h0 = jnp.concatenate([ctx, x], axis=1).reshape(T, D)
    tgt = loss_target.reshape(2 * n_lat, D)

    def lnv(l, j):
        return jnp.stack([ln_g_f[l, j], ln_b_f[l, j]])

    subs = [(0, 0, 0.5, 0), (0, 3, 1.0, 1), (0, 6, 0.5, 2), (1, 0, 0.5, 0), (1, 3, 1.0, 1), (1, 6, 0.5, 2)]

    def ffn_core(hm, l, f):
        tag = f"l{l}f{f}"
        gi = 2 * l + f
        w = wb[gi].reshape(N_CHIP, 3 * ns, D)
        if gi == 3:
            g, u, a = ffn_up(lay, hm, w, 0, 1, ns, f"ffn_up_{tag}")
            (y,) = slab_nn_acc(lay, [a], w, [2], ns, f"ffn_down_{tag}")
            return y, dict(g=g, u=u, a=a, nbuf=None)
        g, u, a, nbuf = ffn_up(lay, hm, w, 0, 1, ns, f"ffn_up_{tag}", rider=rider_gather_xy(placed[gi + 1]))
        y, nbuf = slab_nn_acc(lay, [a], w, [2], ns, f"ffn_down_{tag}", rider=rider_gather_fwd(nbuf))
        return y, dict(g=g, u=u, a=a, nbuf=nbuf)

    def mixa_core(hm):
        p = mm_nt(hm, w_ext_t, "mixa_in")
        qr, kr, vb, u = rope_fwd(lay, p, cos_t, sin_t, "rope")
        att, lse = attn_fwd(lay, qr, kr, vb, sink_tab, "attn")
        pool = pool_fwd(lay, u, pool_w[0], pscale, "pool")
        cat = jnp.concatenate([att, pool], axis=1)
        return mm_nn(cat, w_out_ext, "mixa_out"), dict(qr=qr, kr=kr, vb=vb, u=u, lse=lse, cat=cat)

    def mixc_core(hm):
        p = mm_nt(hm, lru_in_t, "mixc_in")
        uc = conv_fwd(lay, p, D, conv_w_f, conv_b_f, "conv")
        a, b = lru_coeffs(lay, uc, lru_wa[0], lru_wx[0], lru_vec, "lru_coef")
        s = lru_scan(lay, a, b, "lru_scan")
        o = lru_gate(lay, p, s, "lru_gate")
        return mm_nn(o, lru_out, "mixc_out"), dict(p=p, uc=uc, a=a, s=s, o=o)

    recs = []
    h = h0
    hm = modulate(lay, h0, modtab[0], 0, 1, "mod_first")
    for k, (l, k0, coef, j) in enumerate(subs):
        if k0 == 3:
            y, core = mixa_core(hm) if l == 0 else mixc_core(hm)
        else:
            y, core = ffn_core(hm, l, k0 // 6)
        nxt = None if k == 5 else (modtab[subs[k + 1][0]], subs[k + 1][1], subs[k + 1][1] + 1)
        nbuf = core.pop("nbuf", None)
        res = resid_ln(lay, h, y, modtab[l], k0 + 2, coef, lnv(l, j), f"ln_s{k}", nxt=nxt,
                       rider=None if nbuf is None else rider_gather_d2d(nbuf))
        if nbuf is not None:
            wb[2 * l + k0 // 6 + 1] = res[-1]
        recs.append(dict(h=h, hm=hm, y=y, xhat=res[1], rstd=res[2], **core))
        h = res[0]
        hm = res[3] if nxt is not None else None

    dout, lparts = loss_and_grad(lay, h, tgt, "loss")
    loss = lax.psum(jnp.sum(lparts), ("x", "y", "c"))

    dln = {}
    dms = {}
    mixg = {}
    ffn_red = [lax.empty((4, 2, hh, D), F32)]
    pending = []

    def ffn_core_bwd(dy, r, l, f):
        tag = f"l{l}f{f}"
        gi = 2 * l + f
        w = wb[gi].reshape(N_CHIP, 3 * ns, D)
        prev = pending.pop() if pending else None
        if prev is None:
            dg, du = ffn_bwd_da(lay, dy, w, 2, r["g"], r["u"], ns, f"ffn_da_{tag}")
        else:
            dg, du, recv = ffn_bwd_da(lay, dy, w, 2, r["g"], r["u"], ns, f"ffn_da_{tag}", rider=rider_reduce_sib(prev[1]))
        gb = lax.empty((N_CHIP, 3 * ns, D), F32)
        if prev is None:
            (gb,) = slab_tn(lay, r["a"], dy, gb, 2, ns, f"ffn_dwd_{tag}")
            (gb,) = slab_tn(lay, dg, r["hm"], gb, 0, ns, f"ffn_dwg_{tag}")
            (gb,) = slab_tn(lay, du, r["hm"], gb, 1, ns, f"ffn_dwu_{tag}")
            (dhm,) = slab_nn_acc(lay, [dg, du], w, [0, 1], ns, f"ffn_dh_{tag}")
        else:
            q = add_own_half(prev[1], recv, BF16, f"rs_add2_ffn{prev[0]}")
            gb, arr = slab_tn(lay, r["a"], dy, gb, 2, ns, f"ffn_dwd_{tag}", rider=rider_reduce_copy(q, 0))
            gb, arr = slab_tn(lay, dg, r["hm"], gb, 0, ns, f"ffn_dwg_{tag}", rider=rider_reduce_copy(q, 1, arr))
            gb, arr = slab_tn(lay, du, r["hm"], gb, 1, ns, f"ffn_dwu_{tag}", rider=rider_reduce_copy(q, 2, arr))
            red = sum_slots(q, arr, f"rs_add4_ffn{prev[0]}", dst=ffn_red[0], g=prev[0])
            dhm, ffn_red[0] = slab_nn_acc(lay, [dg, du], w, [0, 1], ns, f"ffn_dh_{tag}", rider=rider_join(red, prev[0]))
        pending.append((gi, gb.reshape(N_CHIP, 2, hh, D)))
        return dhm

    def mixc_core_bwd(dy, r):
        do_c = mm_nt(dy, lru_out, "mixc_out_dx")
        mixg["lru_out"] = mm_tn(r["o"], dy, "mixc_out_dw")
        dgate, dyg = lru_gate_bwd(lay, r["p"], r["s"], do_c, "lru_gate_b")
        da_c, db_c = lru_scan_bwd(lay, r["a"], r["s"], dyg, "lru_scan_b")
        duc, mixg["wa"], mixg["wx"], mixg["vec"] = lru_coeffs_bwd(lay, r["uc"], lru_wa[0], lru_wx[0], lru_vec, da_c, db_c,
                                                                  "lru_coef_b")
        du_c, mixg["cw"], mixg["cb"] = conv_bwd(lay, r["p"], D, conv_w_f, duc, "conv_b")
        dp_c = jnp.concatenate([dgate, du_c], axis=1).astype(BF16)
        mixg["lru_in_t"] = mm_tn(dp_c, r["hm"], "mixc_in_dw")
        return mm_nn(dp_c, lru_in_t, "mixc_in_dx")

    def mixa_core_bwd(dy, r):
        dcat = mm_nt(dy, w_out_ext, "mixa_out_dx")
        mixg["out_ext"] = mm_tn(r["cat"], dy, "mixa_out_dw")
        dqr, dkr, dv, mixg["sink"] = attn_bwd(lay, r["qr"], r["kr"], r["vb"], sink_tab, r["lse"], dcat, "attn_b")
        du_a, mixg["pw"], mixg["ps"] = pool_bwd(lay, r["u"], dcat, pool_w[0], pscale, "pool_b")
        dp_a = rope_bwd(lay, dqr, dkr, dv, du_a, cos_t, sin_t, "rope_b")
        mixg["ext_t"] = mm_tn(dp_a, r["hm"], "mixa_in_dw")
        return mm_nn(dp_a, w_ext_t, "mixa_in_dx")

    l, k0, coef, j = subs[5]
    dy, dres, s1 = ln_bwd(lay, dout, recs[5]["xhat"], recs[5]["rstd"], recs[5]["y"], modtab[l], k0 + 2, coef, lnv(l, j),
                          "lnb_s5")
    for k in range(5, -1, -1):
        l, k0, coef, j = subs[k]
        r = recs[k]
        if k0 == 3:
            dhm = mixa_core_bwd(dy, r) if l == 0 else mixc_core_bwd(dy, r)
        else:
            dhm = ffn_core_bwd(dy, r, l, k0 // 6)
        dln[(l, j)] = block_sums(lay, s1, f"bs_ln_s{k}")
        if k > 0:
            lp, k0p, coefp, jp = subs[k - 1]
            rp = recs[k - 1]
            dy, dres, s1, s2 = modb_lnb(lay, dres, dhm, r["h"], modtab[l], k0 + 1, rp["xhat"], rp["rstd"], rp["y"],
                                        modtab[lp], k0p + 2, coefp, lnv(lp, jp), f"modb_lnb_s{k}")
        else:
            gx, s2 = mod_bwd(lay, dres, dhm, r["h"], modtab[l], k0 + 1, "modb_s0")
        dms[(l, k0)] = block_sums(lay, s2, f"bs_mod_s{k}")
    grad_x = gx.reshape(2, n_lat, D)
    g_lru_out, g_wa, g_wx, g_vec, g_cw, g_cb = (mixg[n] for n in ("lru_out", "wa", "wx", "vec", "cw", "cb"))
    g_lru_in_t, g_out_ext, g_sink, g_pw, g_ps, g_ext_t = (mixg[n] for n in ("lru_in_t", "out_ext", "sink", "pw", "ps", "ext_t"))

    rows = []
    for l in range(DEPTH):
        per_k = []
        for k0, j in ((0, 0), (3, 1), (6, 2)):
            per_k += [dms[(l, k0)][:3, 0], dms[(l, k0)][:3, 1], dln[(l, j)][:3, 2]]
        rows.append(jnp.stack(per_k, axis=1).reshape(3, N_MOD * D))
    dmod_loc = jnp.concatenate(rows + [jnp.zeros((2, N_MOD * D), F32)], axis=0)
    dmod_all, g_b_mod = mod_grad_rows(all_gather8(dmod_loc, "ag8_dmod"), "dmod_rows")
    dcol = lax.dynamic_slice(dmod_all, (0, 0, chip * wm), (DEPTH, 32, wm))
    g_w_mod = wmod_dw(sc, dcol, "wmod_dw")
    g_cctx = cctx_grad(cctx_dx(dcol[:, 16:32], w_mod, "cctx_dx"), c_ctx[None, :], "cctx_grad")

    gq = _heads(g_ext_t[Q0:K0], N_HEADS)
    gqr = _heads(g_ext_t[QR0:KR0], N_HEADS)
    g_q = [None] * N_HEADS
    for i, h in enumerate(HEAD_PERM):
        g_q[h] = gq[i] + _unrot_rows(gqr[i])
    gk = [a + _unrot_rows(b) for a, b in zip(_heads(g_ext_t[K0:V0], N_KV), _heads(g_ext_t[KR0:PEXT], N_KV))]
    g_ab_in_t = jnp.concatenate(g_q + gk + [g_ext_t[V0:QR0]], axis=0)
    go = _heads(g_out_ext[0:ATT_W], N_HEADS)
    g_o = [None] * N_HEADS
    for i, h in enumerate(HEAD_PERM):
        g_o[h] = go[i]
    g_ab_out = jnp.concatenate(g_o + [g_out_ext[ATT_W:]], axis=0)
    mix_g = jnp.concatenate([g_lru_in_t.reshape(N_CHIP, n_li, D), g_ab_in_t.reshape(N_CHIP, n_ai, D),
                             g_ab_out.reshape(N_CHIP, n_ao, D), g_lru_out.reshape(N_CHIP, n_lo, D)], axis=1)

    g_ln_g = jnp.stack([jnp.stack([dln[(l, j)][3, 1] for j in range(3)]) for l in range(DEPTH)])
    g_ln_b = jnp.stack([jnp.stack([dln[(l, j)][3, 0] for j in range(3)]) for l in range(DEPTH)])
    sink_row = jnp.sum(g_sink, axis=0)[:4]
    g_sink8 = jnp.concatenate([sink_row[:, 0], sink_row[:, HEAD_DIM]])
    misc = jnp.concatenate([g_sink8, jnp.sum(g_ps, axis=0).reshape(POOL_W), jnp.zeros((D - 8 - POOL_W,), F32)])
    small_g = jnp.concatenate([
        g_ln_g.reshape(6, D), g_ln_b.reshape(6, D), jnp.sum(g_cw, axis=0), jnp.sum(g_cb, axis=0), g_vec,
        misc[None, :], jnp.sum(g_pw, axis=0).reshape(64, D), g_wa.reshape(256, D), g_wx.reshape(256, D), g_cctx,
        jnp.zeros((39, D), F32)], axis=0)
    n_small = small_g.shape[0] // N_CHIP
    mix_buf = jnp.concatenate([mix_g, small_g.reshape(N_CHIP, n_small, D)], axis=1)
    n_mb = n_mix + n_small

    last_g, last_buf = pending.pop()
    ffn_red = reduce_scatter_chips(last_buf, f"ffn{last_g}", wire=BF16, dst=ffn_red[0], g=last_g).reshape(12 * ns, D)
    mix_red = reduce_scatter_chips(mix_buf.reshape(N_CHIP, 2, n_mb // 2, D), "mix").reshape(n_mb, D)
    small_red = all_gather_chips(mix_red[n_mix:].reshape(2, n_small // 2, D), "ag_smallg").reshape(N_CHIP * n_small, D)

    ffn_kind = dict(ffn_w_gate=0, ffn_w_up=1, ffn_w_down=2)

    def cols(a):
        return lax.dynamic_slice_in_dim(a, chip * dsh, dsh, axis=a.ndim - 1)

    sr = small_red
    grads = dict(
        c_ctx=sr[600], w_mod=g_w_mod, b_mod=g_b_mod,
        ln_g=cols(sr[0:6]).reshape(2, 3, dsh), ln_b=cols(sr[6:12]).reshape(2, 3, dsh),
        mix_ab_w_in=mix_red[o1:o2].T[None], attn_sink=sr[23, 0:8][None], pool_w=sr[24:88].reshape(1, 4, 128, 128),
        pool_scale=sr[23, 8:8 + POOL_W][None], mix_ab_w_out=mix_red[o2:o3][None], lru_w_in=mix_red[0:o1].T[None],
        lru_conv_w=cols(sr[12:16])[None], lru_conv_b=cols(sr[16:17]), lru_wa=sr[88:344].reshape(1, 2, 8, 128, 128),
        lru_ba=cols(sr[17:19])[None], lru_wx=sr[344:600].reshape(1, 2, 8, 128, 128), lru_bx=cols(sr[19:21])[None],
        lru_lambda=cols(sr[21:23])[None], lru_w_out=mix_red[o3:n_mix][None])
    params = dict(c_ctx=(c_ctx, m_c_ctx, v_c_ctx), w_mod=(w_mod, m_w_mod, v_w_mod), b_mod=(b_mod, m_b_mod, v_b_mod),
                  ln_g=(ln_g, m_ln_g, v_ln_g), ln_b=(ln_b, m_ln_b, v_ln_b),
                  ffn_w_gate=(ffn_w_gate, m_ffn_w_gate, v_ffn_w_gate), ffn_w_up=(ffn_w_up, m_ffn_w_up, v_ffn_w_up),
                  ffn_w_down=(ffn_w_down, m_ffn_w_down, v_ffn_w_down),
                  mix_ab_w_in=(mix_ab_w_in, m_mix_ab_w_in, v_mix_ab_w_in), attn_sink=(attn_sink, m_attn_sink, v_attn_sink),
                  pool_w=(pool_w, m_pool_w, v_pool_w), pool_scale=(pool_scale, m_pool_scale, v_pool_scale),
                  mix_ab_w_out=(mix_ab_w_out, m_mix_ab_w_out, v_mix_ab_w_out), lru_w_in=(lru_w_in, m_lru_w_in, v_lru_w_in),
                  lru_conv_w=(lru_conv_w, m_lru_conv_w, v_lru_conv_w), lru_conv_b=(lru_conv_b, m_lru_conv_b, v_lru_conv_b),
                  lru_wa=(lru_wa, m_lru_wa, v_lru_wa), lru_ba=(lru_ba, m_lru_ba, v_lru_ba), lru_wx=(lru_wx, m_lru_wx, v_lru_wx),
                  lru_bx=(lru_bx, m_lru_bx, v_lru_bx), lru_lambda=(lru_lambda, m_lru_lambda, v_lru_lambda),
                  lru_w_out=(lru_w_out, m_lru_w_out, v_lru_w_out))
    gl, dl, ml, vl = [], [], [], []
    for name, (w, m, v) in params.items():
        if name in ffn_kind:
            g, d, mn, vn = adamw_ffn(w, m, v, ffn_red, ffn_kind[name], ns, f"adamw_{name}")
        else:
            g = grads[name].reshape(w.shape)
            d, mn, vn = adamw(w, g, m, v, f"adamw_{name}")
        gl.append(g)
        dl.append(d)
        ml.append(mn)
        vl.append(vn)
    return (loss, grad_x, *gl, *dl, *ml, *vl)
```

```python
import functools
import math

import jax
import jax.numpy as jnp
from jax import lax
from jax.experimental import pallas as pl
from jax.experimental.pallas import tpu as pltpu

F32, BF16 = jnp.float32, jnp.bfloat16
MESH = pl.DeviceIdType.MESH
ANY = pl.BlockSpec(memory_space=pl.ANY)
VMEM_SPEC = pl.BlockSpec(memory_space=pltpu.VMEM)

D = 1024
N_CHIP = 4
HEAD_DIM, N_HEADS, N_KV = 64, 8, 2
ATT_W, KV_W, POOL_W = 512, 128, 512
POOL_WINDOWS = (2, 4, 8, 16)
BLK = 128
ATT_SCALE = HEAD_DIM ** -0.5
ROPE_THETA = 10000.0
GRID_W = 64
LRU_C = 8.0
LN_EPS = 1e-5
NEG_INF = -1e30
DEPTH = 2
ALPHA = (2 * DEPTH) ** 0.25
N_MOD = 9
ADAM_LR, ADAM_B1, ADAM_B2, ADAM_EPS, ADAM_WD, ADAM_STEP = 0.001, 0.9, 0.999, 1e-08, 0.01, 10
VMEM_BIG = 48 * 1024 * 1024


def _cp(sem=None, vmem=None):
    kw = {}
    if sem is not None:
        kw["dimension_semantics"] = sem
    if vmem is not None:
        kw["vmem_limit_bytes"] = vmem
    return pltpu.CompilerParams(**kw)


def _sds(shape, dtype):
    return jax.ShapeDtypeStruct(tuple(shape), dtype)


def _pick(n, cands):
    for c in cands:
        if n % c == 0:
            return c
    return n


def _dot(a, b, dims):
    return lax.dot_general(a, b, (dims, ((), ())), preferred_element_type=F32)


def _nn(a, b):
    return _dot(a, b, ((1,), (0,)))


def _nt(a, b):
    return _dot(a, b, ((1,), (1,)))


def _tn(a, b):
    return _dot(a, b, ((0,), (0,)))


def _sigmoid(x):
    return 0.5 * jnp.tanh(0.5 * x) + 0.5


def _me():
    return lax.axis_index("x"), lax.axis_index("y"), lax.axis_index("c")


def _rcopy(src, dst, ssem, rsem, dev):
    return pltpu.make_async_remote_copy(src_ref=src, dst_ref=dst, send_sem=ssem, recv_sem=rsem,
                                        device_id=dev, device_id_type=MESH)


def all_gather8(x, name):
    def body(x_ref, o_ref, ssem, rsem, lsem):
        mx, my, mc = _me()
        me = 4 * mx + 2 * my + mc
        loc = pltpu.make_async_copy(x_ref, o_ref.at[me], lsem)
        loc.start()
        peers = []
        for m in range(1, 8):
            px = 1 - mx if (m >> 2) & 1 else mx
            py = 1 - my if (m >> 1) & 1 else my
            pc = 1 - mc if m & 1 else mc
            peers.append((px, py, pc))
        sends = [_rcopy(x_ref, o_ref.at[me], ssem.at[k], rsem.at[k], p) for k, p in enumerate(peers)]
        for cp in sends:
            cp.start()
        for k, (px, py, pc) in enumerate(peers):
            _rcopy(x_ref, o_ref.at[4 * px + 2 * py + pc], ssem.at[k], rsem.at[k], (px, py, pc)).wait_recv()
        for cp in sends:
            cp.wait_send()
        loc.wait()

    return pl.pallas_call(
        body, name=name, out_shape=_sds((8,) + x.shape, x.dtype),
        in_specs=[VMEM_SPEC], out_specs=VMEM_SPEC,
        scratch_shapes=[pltpu.SemaphoreType.DMA((7,)), pltpu.SemaphoreType.DMA((7,)), pltpu.SemaphoreType.DMA],
    )(x)


_ROW_BLOCKS = (512, 384, 352, 256, 224, 128)


def _idx(v):
    return jnp.reshape(v, (1,)).astype(jnp.int32)


def place_slab(shard, name):
    _, h, w = shard.shape
    th = _pick(h, _ROW_BLOCKS)

    def body(s_ref, x_ref, o_ref):
        del s_ref
        o_ref[...] = x_ref[...]

    return pl.pallas_call(
        body, name=name, out_shape=_sds((N_CHIP,) + shard.shape, shard.dtype),
        grid_spec=pltpu.PrefetchScalarGridSpec(
            num_scalar_prefetch=1, grid=(2, h // th),
            in_specs=[pl.BlockSpec((None, th, w), lambda k, r, s: (k, r, 0))],
            out_specs=pl.BlockSpec((None, None, th, w), lambda k, r, s: (s[0], k, r, 0))),
    )(_idx(2 * lax.axis_index("x") + lax.axis_index("y")), shard)


def ffn_place(w_gate_t, w_up_t, w_down, l, f, name):
    ns = w_down.shape[-2]
    tc = 256

    def body(s_ref, g_ref, u_ref, d_ref, o_ref):
        del s_ref
        k = pl.program_id(0)

        @pl.when(k == 0)
        def _():
            o_ref[...] = g_ref[...].astype(BF16)

        @pl.when(k == 1)
        def _():
            o_ref[...] = u_ref[...].astype(BF16)

        @pl.when(k == 2)
        def _():
            o_ref[...] = d_ref[...].astype(BF16)

    spec = pl.BlockSpec((None, None, ns, tc), lambda k, j, s: (l, f, 0, j))
    out = pl.pallas_call(
        body, name=name, out_shape=_sds((N_CHIP, 3 * ns, D), BF16),
        grid_spec=pltpu.PrefetchScalarGridSpec(
            num_scalar_prefetch=1, grid=(3, D // tc), in_specs=[spec, spec, spec],
            out_specs=pl.BlockSpec((None, ns, tc), lambda k, j, s: (s[0], k, j))),
    )(_idx(2 * lax.axis_index("x") + lax.axis_index("y")), w_gate_t, w_up_t, w_down)
    return out.reshape(N_CHIP, 2, 3 * ns // 2, D)


def all_gather_chips(shard, name):
    return gather_placed(place_slab(shard, name + "_place"), name)


def gather_placed(full, name):
    def body(x_ref, o_ref, ssem, rsem):
        del x_ref
        mx, my, mc = _me()
        s = 2 * mx + my
        sib = (mx, my, 1 - mc)
        chips = [(1 - mx, my), (mx, 1 - my), (1 - mx, 1 - my)]
        first = [_rcopy(o_ref.at[s, mc], o_ref.at[s, mc], ssem.at[j], rsem.at[j], (px, py, mc))
                 for j, (px, py) in enumerate(chips)]
        for cp in first:
            cp.start()
        passed = []
        for j, (px, py) in enumerate(chips):
            ps = 2 * px + py
            _rcopy(o_ref.at[ps, mc], o_ref.at[ps, mc], ssem.at[j], rsem.at[j], (px, py, mc)).wait_recv()
            fw = _rcopy(o_ref.at[ps, mc], o_ref.at[ps, mc], ssem.at[3 + j], rsem.at[3 + j], sib)
            fw.start()
            passed.append(fw)
        for j, (px, py) in enumerate(chips):
            ps = 2 * px + py
            _rcopy(o_ref.at[ps, 1 - mc], o_ref.at[ps, 1 - mc], ssem.at[3 + j], rsem.at[3 + j], sib).wait_recv()
        for cp in first + passed:
            cp.wait_send()

    return pl.pallas_call(
        body, name=name, out_shape=_sds(full.shape, full.dtype), in_specs=[ANY], out_specs=ANY,
        input_output_aliases={0: 0},
        scratch_shapes=[pltpu.SemaphoreType.DMA((6,)), pltpu.SemaphoreType.DMA((6,))],
    )(full)


def sibling_send_other_half(buf, name):
    def body(x_ref, o_ref, ssem, rsem):
        mx, my, mc = _me()
        sib = (mx, my, 1 - mc)
        cps = [_rcopy(x_ref.at[k, 1 - mc], o_ref.at[k], ssem.at[k], rsem.at[k], sib) for k in range(N_CHIP)]
        for cp in cps:
            cp.start()
        for cp in cps:
            cp.wait_recv()
        for cp in cps:
            cp.wait_send()

    n, _, h, w = buf.shape
    return pl.pallas_call(
        body, name=name, out_shape=_sds((n, h, w), buf.dtype), in_specs=[ANY], out_specs=ANY,
        scratch_shapes=[pltpu.SemaphoreType.DMA((N_CHIP,)), pltpu.SemaphoreType.DMA((N_CHIP,))],
    )(buf)


def chips_all_to_all(q, name):
    def body(x_ref, o_ref, ssem, rsem):
        mx, my, mc = _me()
        s = 2 * mx + my
        chips = [(1 - mx, my), (mx, 1 - my), (1 - mx, 1 - my)]
        cps = [_rcopy(x_ref.at[2 * px + py], o_ref.at[s], ssem.at[j], rsem.at[j], (px, py, mc))
               for j, (px, py) in enumerate(chips)]
        for cp in cps:
            cp.start()
        for j, (px, py) in enumerate(chips):
            ps = 2 * px + py
            _rcopy(x_ref.at[ps], o_ref.at[ps], ssem.at[j], rsem.at[j], (px, py, mc)).wait_recv()
        for cp in cps:
            cp.wait_send()

    return pl.pallas_call(
        body, name=name, out_shape=_sds(q.shape, q.dtype), in_specs=[ANY], out_specs=ANY,
        scratch_shapes=[pltpu.SemaphoreType.DMA((3,)), pltpu.SemaphoreType.DMA((3,))],
    )(q)


def sibling_join_halves(both, name, g=None):
    def body(x_ref, o_ref, ssem, rsem):
        del x_ref
        mx, my, mc = _me()
        sib = (mx, my, 1 - mc)
        o = o_ref if g is None else o_ref.at[g]
        cp = _rcopy(o.at[mc], o.at[mc], ssem, rsem, sib)
        cp.start()
        _rcopy(o.at[1 - mc], o.at[1 - mc], ssem, rsem, sib).wait_recv()
        cp.wait_send()

    return pl.pallas_call(
        body, name=name, out_shape=_sds(both.shape, both.dtype), in_specs=[ANY], out_specs=ANY,
        input_output_aliases={0: 0}, scratch_shapes=[pltpu.SemaphoreType.DMA, pltpu.SemaphoreType.DMA],
    )(both)


def add_own_half(buf, recv, wire, name):
    n, _, h, w = buf.shape
    th = _pick(h, _ROW_BLOCKS)

    def body(c_ref, a_ref, b_ref, o_ref):
        del c_ref
        o_ref[...] = (a_ref[...] + b_ref[...]).astype(o_ref.dtype)

    return pl.pallas_call(
        body, name=name, out_shape=_sds((n, h, w), wire),
        grid_spec=pltpu.PrefetchScalarGridSpec(
            num_scalar_prefetch=1, grid=(n, h // th),
            in_specs=[pl.BlockSpec((None, None, th, w), lambda k, r, c: (k, c[0], r, 0)),
                      pl.BlockSpec((None, th, w), lambda k, r, c: (k, r, 0))],
            out_specs=pl.BlockSpec((None, th, w), lambda k, r, c: (k, r, 0))),
    )(_idx(lax.axis_index("c")), buf, recv)


def sum_slots(q, r, name, dst=None, g=None):
    n, h, w = r.shape
    th = _pick(h, _ROW_BLOCKS)

    def body(i_ref, q_ref, r1, r2, r3, *rest):
        del i_ref
        rest[-1][...] = ((q_ref[...].astype(F32) + r1[...].astype(F32)) + r2[...].astype(F32)) + r3[...].astype(F32)

    def slot(d):
        return lambda i, ix: ((ix[0] + d) % N_CHIP, i, 0)

    idx = jnp.stack([2 * lax.axis_index("x") + lax.axis_index("y"), lax.axis_index("c")]).astype(jnp.int32)
    in_specs = [pl.BlockSpec((None, th, w), slot(d)) for d in (0, 1, 2, 3)]
    if dst is None:
        return pl.pallas_call(
            body, name=name, out_shape=_sds((2, h, w), F32),
            grid_spec=pltpu.PrefetchScalarGridSpec(
                num_scalar_prefetch=1, grid=(h // th,), in_specs=in_specs,
                out_specs=pl.BlockSpec((None, th, w), lambda i, ix: (ix[1], i, 0))),
        )(idx, q, r, r, r)
    return pl.pallas_call(
        body, name=name, out_shape=_sds(dst.shape, F32),
        grid_spec=pltpu.PrefetchScalarGridSpec(
            num_scalar_prefetch=1, grid=(h // th,), in_specs=in_specs + [ANY],
            out_specs=pl.BlockSpec((None, None, th, w), lambda i, ix: (g, ix[1], i, 0))),
        input_output_aliases={5: 0},
    )(idx, q, r, r, r, dst)


def reduce_scatter_chips(buf, tag, wire=F32, dst=None, g=None):
    recv = sibling_send_other_half(buf, f"rs_sib_{tag}")
    q = add_own_half(buf, recv, wire, f"rs_add2_{tag}")
    r = chips_all_to_all(q, f"rs_a2a_{tag}")
    red = sum_slots(q, r, f"rs_add4_{tag}", dst=dst, g=g)
    return sibling_join_halves(red, f"rs_join_{tag}", g=g)


class Layout:
    def __init__(self, n_ctx, n_lat):
        self.C, self.L = n_ctx, n_lat
        self.PS = n_ctx + n_lat
        self.T = 2 * self.PS
        self.tr = _pick(math.gcd(n_ctx, n_lat), (256, 128))
        self.bps = self.PS // self.tr
        self.cb = n_ctx // self.tr
        self.nblk = self.T // self.tr
        self.tm = _pick(self.T, (1152, 768, 512, 256, 128))
        self.tc = _pick(self.T, (512, 256, 128))

    def seg(self, i):
        return jnp.where(i % self.bps < self.cb, 2, i // self.bps)


def rowwise(lay, name, fn, rows, segs=(), vecs=(), outs=(), sums=(), rider=None):
    tr, nblk = lay.tr, lay.nblk
    n_r, n_s, n_v, n_o = len(rows), len(segs), len(vecs), len(outs)

    def body(*refs):
        ins = refs[:n_r + n_s + n_v]
        ors = refs[n_r + n_s + n_v:]
        vals = [r[...] for r in ins[:n_r]] + [r[0] for r in ins[n_r:n_r + n_s]] + [r[...] for r in ins[n_r + n_s:]]
        res = fn(*vals)
        for k in range(n_o):
            ors[k][...] = res[k].astype(ors[k].dtype)
        for k in range(len(sums)):
            ors[n_o + k][0] = res[n_o + k]

    def all_rows(i):
        return (i, 0)

    def lat_rows(i):
        return ((i // lay.bps) * (lay.bps - lay.cb) + jnp.maximum(i % lay.bps - lay.cb, 0), 0)

    in_specs = [pl.BlockSpec((tr, a.shape[1]), all_rows if a.shape[0] == lay.T else lat_rows) for a in rows]
    in_specs += [pl.BlockSpec((1,) + a.shape[1:], lambda i: (lay.seg(i), 0, 0)) for a in segs]
    in_specs += [pl.BlockSpec(a.shape, lambda i: (0, 0)) for a in vecs]
    out_shape = [_sds((2 * lay.L if o[2:] else lay.T, o[0]), o[1]) for o in outs]
    out_shape += [_sds((nblk, r, w), F32) for r, w in sums]
    out_specs = [pl.BlockSpec((tr, o[0]), lat_rows if o[2:] else all_rows) for o in outs]
    out_specs += [pl.BlockSpec((1, r, w), lambda i: (i, 0, 0)) for r, w in sums]
    sem = "arbitrary" if any(o[2:] for o in outs) else "parallel"
    if rider is None:
        return pl.pallas_call(body, name=name, out_shape=out_shape, grid=(nblk,), in_specs=in_specs,
                              out_specs=out_specs, compiler_params=_cp((sem,)))(*rows, *segs, *vecs)
    return _host_call(body, rider, name, (nblk,), in_specs, out_specs, out_shape, (*rows, *segs, *vecs), (sem,),
                      n_r + n_s + n_v, n_o + len(sums))


def modulate(lay, h, mod, k_shift, k_scale, name):
    def fn(hb, m):
        return (hb * (1.0 + m[k_scale:k_scale + 1]) + m[k_shift:k_shift + 1],)
    return rowwise(lay, name, fn, [h], segs=[mod], outs=[(D, BF16)])[0]


def resid_ln(lay, h, y, mod, k_gate, coef, lnv, name, nxt=None, rider=None):
    def fn(hb, yb, m, *rest):
        ln = rest[-1]
        z = ALPHA * hb + (coef * m[k_gate:k_gate + 1]) * yb
        mu = jnp.mean(z, axis=-1, keepdims=True)
        zc = z - mu
        var = jnp.mean(zc * zc, axis=-1, keepdims=True)
        rstd = lax.rsqrt(var + LN_EPS)
        xhat = zc * rstd
        out = xhat * ln[0:1] + ln[1:2]
        if nxt is None:
            return out, xhat, rstd
        mn = rest[0]
        return out, xhat, rstd, out * (1.0 + mn[nxt[2]:nxt[2] + 1]) + mn[nxt[1]:nxt[1] + 1]
    segs = [mod] if nxt is None else [mod, nxt[0]]
    outs = [(D, F32), (D, F32), (1, F32)] + ([] if nxt is None else [(D, BF16)])
    return rowwise(lay, name, fn, [h, y], segs=segs, vecs=[lnv], outs=outs, rider=rider)


def _ln_bwd_math(do, xh, rs, yb, gate, coef, ln):
    dxh = do * ln[0:1]
    m1 = jnp.mean(dxh, axis=-1, keepdims=True)
    m2 = jnp.mean(dxh * xh, axis=-1, keepdims=True)
    dz = rs * (dxh - m1 - xh * m2)
    s = jnp.concatenate([jnp.sum(do, axis=0, keepdims=True), jnp.sum(do * xh, axis=0, keepdims=True),
                         jnp.sum(coef * dz * yb, axis=0, keepdims=True)], axis=0)
    return (coef * gate) * dz, ALPHA * dz, s


def _mod_bwd_math(dr, dm, hb, scale):
    s = jnp.concatenate([jnp.sum(dm, axis=0, keepdims=True), jnp.sum(dm * hb, axis=0, keepdims=True)], axis=0)
    return dr + dm * (1.0 + scale), s


def ln_bwd(lay, dout, xhat, rstd, y, mod, k_gate, coef, lnv, name):
    def fn(do, xh, rs, yb, m, ln):
        return _ln_bwd_math(do, xh, rs, yb, m[k_gate:k_gate + 1], coef, ln)
    return rowwise(lay, name, fn, [dout, xhat, rstd, y], segs=[mod], vecs=[lnv],
                   outs=[(D, BF16), (D, F32)], sums=[(3, D)])


def mod_bwd(lay, dres, dhm, h, mod, k_scale, name):
    def fn(dr, dm, hb, m):
        return _mod_bwd_math(dr, dm, hb, m[k_scale:k_scale + 1])
    return rowwise(lay, name, fn, [dres, dhm, h], segs=[mod], outs=[(D, F32, "lat")], sums=[(2, D)])


def modb_lnb(lay, dres, dhm, h, mod, k_scale, xhat, rstd, y, mod_p, k_gate, coef, lnv, name):
    def fn(dr, dm, hb, xh, rs, yb, m, mp, ln):
        dh, s2 = _mod_bwd_math(dr, dm, hb, m[k_scale:k_scale + 1])
        dy, dres_p, s1 = _ln_bwd_math(dh, xh, rs, yb, mp[k_gate:k_gate + 1], coef, ln)
        return dy, dres_p, s1, s2
    return rowwise(lay, name, fn, [dres, dhm, h, xhat, rstd, y], segs=[mod, mod_p], vecs=[lnv],
                   outs=[(D, BF16), (D, F32)], sums=[(3, D), (2, D)])


def block_sums(lay, parts, name):
    nblk, r, w = parts.shape

    def body(p_ref, o_ref):
        acc = [None, None, None]
        for i in range(nblk):
            sg = 2 if i % lay.bps < lay.cb else i // lay.bps
            acc[sg] = p_ref[i] if acc[sg] is None else acc[sg] + p_ref[i]
        for k in range(3):
            o_ref[k] = acc[k]
        o_ref[3] = (acc[0] + acc[1]) + acc[2]

    return pl.pallas_call(body, name=name, out_shape=_sds((4, r, w), F32), in_specs=[VMEM_SPEC],
                          out_specs=VMEM_SPEC)(parts)


def mm_nn(a, b, name, out_dtype=F32, bias=None):
    m, k = a.shape
    n = b.shape[1]
    tm = _pick(m, (512, 256, 128, 64, 32, 16, 8))
    tn = _pick(n, (1024, 768, 640, 512, 384, 256, 128))

    def body(*refs):
        if bias is None:
            a_ref, b_ref, o_ref = refs
            o_ref[...] = _nn(a_ref[...].astype(BF16), b_ref[...].astype(BF16)).astype(o_ref.dtype)
        else:
            a_ref, b_ref, c_ref, o_ref = refs
            o_ref[...] = (_nn(a_ref[...].astype(BF16), b_ref[...].astype(BF16)) + c_ref[...]).astype(o_ref.dtype)

    in_specs = [pl.BlockSpec((tm, k), lambda i, j: (i, 0)), pl.BlockSpec((k, tn), lambda i, j: (0, j))]
    ops = [a, b]
    if bias is not None:
        in_specs.append(pl.BlockSpec((1, tn), lambda i, j: (0, j)))
        ops.append(bias)
    return pl.pallas_call(body, name=name, out_shape=_sds((m, n), out_dtype), grid=(m // tm, n // tn),
                          in_specs=in_specs, out_specs=pl.BlockSpec((tm, tn), lambda i, j: (i, j)),
                          compiler_params=_cp(("parallel", "parallel"), VMEM_BIG))(*ops)


def mm_nt(a, b, name, out_dtype=F32):
    m, k = a.shape
    n = b.shape[0]
    tm = _pick(m, (512, 256, 128, 64, 32, 16, 8))
    tn = _pick(n, (1024, 768, 640, 512, 384, 256, 128))

    def body(a_ref, b_ref, o_ref):
        o_ref[...] = _nt(a_ref[...].astype(BF16), b_ref[...].astype(BF16)).astype(o_ref.dtype)

    return pl.pallas_call(body, name=name, out_shape=_sds((m, n), out_dtype), grid=(m // tm, n // tn),
                          in_specs=[pl.BlockSpec((tm, k), lambda i, j: (i, 0)), pl.BlockSpec((tn, k), lambda i, j: (j, 0))],
                          out_specs=pl.BlockSpec((tm, tn), lambda i, j: (i, j)),
                          compiler_params=_cp(("parallel", "parallel"), VMEM_BIG))(a, b)


def mm_tn(a, b, name):
    t, m = a.shape
    n = b.shape[1]
    tk = _pick(t, (512, 256, 128, 64, 32, 16))
    tm = _pick(m, (512, 384, 256, 128))

    def body(a_ref, b_ref, o_ref):
        @pl.when(pl.program_id(1) == 0)
        def _():
            o_ref[...] = jnp.zeros_like(o_ref)
        o_ref[...] += _tn(a_ref[...].astype(BF16), b_ref[...].astype(BF16))

    return pl.pallas_call(body, name=name, out_shape=_sds((m, n), F32), grid=(m // tm, t // tk),
                          in_specs=[pl.BlockSpec((tk, tm), lambda i, k: (k, i)), pl.BlockSpec((tk, n), lambda i, k: (k, 0))],
                          out_specs=pl.BlockSpec((tm, n), lambda i, k: (i, 0)),
                          compiler_params=_cp(("parallel", "arbitrary"), VMEM_BIG))(a, b)


class Rider:
    def __init__(self, ins, outs, aliases, nsem, start, wait):
        self.ins, self.outs, self.aliases, self.nsem, self.start, self.wait = ins, outs, aliases, nsem, start, wait


def _chips_of(mx, my):
    return [(1 - mx, my), (mx, 1 - my), (1 - mx, 1 - my)]


def rider_gather_d2d(buf):
    def start(ins, outs, ssem, rsem):
        o = outs[0]
        mx, my, mc = _me()
        for j, (px, py) in enumerate(_chips_of(mx, my)):
            ps = 2 * px + py
            _rcopy(o.at[ps, mc], o.at[ps, mc], ssem.at[j], rsem.at[j], (mx, my, 1 - mc)).start()

    def wait(ins, outs, ssem, rsem):
        o = outs[0]
        mx, my, mc = _me()
        sib = (mx, my, 1 - mc)
        for j, (px, py) in enumerate(_chips_of(mx, my)):
            ps = 2 * px + py
            _rcopy(o.at[ps, 1 - mc], o.at[ps, 1 - mc], ssem.at[j], rsem.at[j], sib).wait_recv()
        for j, (px, py) in enumerate(_chips_of(mx, my)):
            ps = 2 * px + py
            _rcopy(o.at[ps, mc], o.at[ps, mc], ssem.at[j], rsem.at[j], sib).wait_send()

    return Rider([buf], [_sds(buf.shape, buf.dtype)], {0: 0}, 3, start, wait)


def rider_reduce_sib(buf):
    n, _, h, w = buf.shape

    def start(ins, outs, ssem, rsem):
        mx, my, mc = _me()
        for k in range(N_CHIP):
            _rcopy(ins[0].at[k, 1 - mc], outs[0].at[k], ssem.at[k], rsem.at[k], (mx, my, 1 - mc)).start()

    def wait(ins, outs, ssem, rsem):
        mx, my, mc = _me()
        for k in range(N_CHIP):
            _rcopy(ins[0].at[k, 1 - mc], outs[0].at[k], ssem.at[k], rsem.at[k], (mx, my, 1 - mc)).wait_recv()
        for k in range(N_CHIP):
            _rcopy(ins[0].at[k, 1 - mc], outs[0].at[k], ssem.at[k], rsem.at[k], (mx, my, 1 - mc)).wait_send()

    return Rider([buf], [_sds((n, h, w), buf.dtype)], {}, N_CHIP, start, wait)


def rider_gather_xy(buf):
    def peers():
        mx, my, mc = _me()
        return 2 * mx + my, mc, [(1 - mx, my), (mx, 1 - my)]

    def start(ins, outs, ssem, rsem):
        o = outs[0]
        s, mc, nb = peers()
        for j, (px, py) in enumerate(nb):
            _rcopy(o.at[s, mc], o.at[s, mc], ssem.at[j], rsem.at[j], (px, py, mc)).start()

    def wait(ins, outs, ssem, rsem):
        o = outs[0]
        s, mc, nb = peers()
        for j, (px, py) in enumerate(nb):
            _rcopy(o.at[2 * px + py, mc], o.at[2 * px + py, mc], ssem.at[j], rsem.at[j], (px, py, mc)).wait_recv()
        for j, (px, py) in enumerate(nb):
            _rcopy(o.at[s, mc], o.at[s, mc], ssem.at[j], rsem.at[j], (px, py, mc)).wait_send()

    return Rider([buf], [_sds(buf.shape, buf.dtype)], {0: 0}, 2, start, wait)


def rider_gather_fwd(buf):
    def start(ins, outs, ssem, rsem):
        o = outs[0]
        mx, my, mc = _me()
        xs = 2 * (1 - mx) + my
        _rcopy(o.at[xs, mc], o.at[xs, mc], ssem.at[0], rsem.at[0], (mx, 1 - my, mc)).start()

    def wait(ins, outs, ssem, rsem):
        o = outs[0]
        mx, my, mc = _me()
        xs, ds = 2 * (1 - mx) + my, 2 * (1 - mx) + (1 - my)
        _rcopy(o.at[ds, mc], o.at[ds, mc], ssem.at[0], rsem.at[0], (mx, 1 - my, mc)).wait_recv()
        _rcopy(o.at[xs, mc], o.at[xs, mc], ssem.at[0], rsem.at[0], (mx, 1 - my, mc)).wait_send()

    return Rider([buf], [_sds(buf.shape, buf.dtype)], {0: 0}, 1, start, wait)


def rider_reduce_copy(q, j, r=None):
    def peer():
        mx, my, mc = _me()
        px, py = _chips_of(mx, my)[j]
        return 2 * mx + my, 2 * px + py, (px, py, mc)

    def start(ins, outs, ssem, rsem):
        s, ps, dev = peer()
        _rcopy(ins[0].at[ps], outs[0].at[s], ssem.at[0], rsem.at[0], dev).start()

    def wait(ins, outs, ssem, rsem):
        s, ps, dev = peer()
        _rcopy(ins[0].at[ps], outs[0].at[ps], ssem.at[0], rsem.at[0], dev).wait_recv()
        _rcopy(ins[0].at[ps], outs[0].at[s], ssem.at[0], rsem.at[0], dev).wait_send()

    if r is None:
        return Rider([q], [_sds(q.shape, q.dtype)], {}, 1, start, wait)
    return Rider([q, r], [_sds(q.shape, q.dtype)], {1: 0}, 1, start, wait)


def rider_join(buf, g):
    def start(ins, outs, ssem, rsem):
        o = outs[0].at[g]
        mx, my, mc = _me()
        _rcopy(o.at[mc], o.at[mc], ssem.at[0], rsem.at[0], (mx, my, 1 - mc)).start()

    def wait(ins, outs, ssem, rsem):
        o = outs[0].at[g]
        mx, my, mc = _me()
        _rcopy(o.at[1 - mc], o.at[1 - mc], ssem.at[0], rsem.at[0], (mx, my, 1 - mc)).wait_recv()
        _rcopy(o.at[mc], o.at[mc], ssem.at[0], rsem.at[0], (mx, my, 1 - mc)).wait_send()

    return Rider([buf], [_sds(buf.shape, buf.dtype)], {0: 0}, 1, start, wait)


def _host_call(body, rider, name, grid, in_specs, out_specs, out_shape, operands, sem, n_in, n_out, aliases=None):
    aliases = dict(aliases or {})
    if rider is None:
        return pl.pallas_call(body, name=name, out_shape=out_shape, grid=grid, in_specs=in_specs, out_specs=out_specs,
                              input_output_aliases=aliases, compiler_params=_cp(sem, VMEM_BIG))(*operands)
    n_ri, n_ro = len(rider.ins), len(rider.outs)
    aliases.update({n_in + a: n_out + b for a, b in rider.aliases.items()})

    def hosted(*refs):
        ins, r_in = refs[:n_in], refs[n_in:n_in + n_ri]
        outs, r_out = refs[n_in + n_ri:n_in + n_ri + n_out], refs[n_in + n_ri + n_out:n_in + n_ri + n_out + n_ro]
        ssem, rsem = refs[-2], refs[-1]
        first = functools.reduce(lambda a, b: a & b, [pl.program_id(k) == 0 for k in range(len(grid))])
        last = functools.reduce(lambda a, b: a & b, [pl.program_id(k) == grid[k] - 1 for k in range(len(grid))])

        @pl.when(first)
        def _():
            rider.start(r_in, r_out, ssem, rsem)
        body(*ins, *outs)

        @pl.when(last)
        def _():
            rider.wait(r_in, r_out, ssem, rsem)

    return pl.pallas_call(
        hosted, name=name, out_shape=list(out_shape) + list(rider.outs), grid=grid,
        in_specs=list(in_specs) + [ANY] * n_ri, out_specs=list(out_specs) + [ANY] * n_ro,
        input_output_aliases=aliases,
        scratch_shapes=[pltpu.SemaphoreType.DMA((rider.nsem,)), pltpu.SemaphoreType.DMA((rider.nsem,))],
        compiler_params=_cp(("arbitrary",) * len(grid), VMEM_BIG))(*operands, *rider.ins)


def ffn_up(lay, hm, wbuf, ig, iu, ns, name, rider=None):
    tm = lay.tm

    def body(h_ref, wg_ref, wu_ref, g_ref, u_ref, a_ref):
        hb = h_ref[...]
        g = _nt(hb, wg_ref[0])
        u = _nt(hb, wu_ref[0])
        g_ref[0] = g.astype(BF16)
        u_ref[0] = u.astype(BF16)
        a_ref[0] = (g * _sigmoid(g) * u).astype(BF16)

    spec_o = pl.BlockSpec((1, tm, ns), lambda s, i: (s, i, 0))
    return _host_call(
        body, rider, name, (N_CHIP, lay.T // tm),
        [pl.BlockSpec((tm, D), lambda s, i: (i, 0)), pl.BlockSpec((1, ns, D), lambda s, i: (s, ig, 0)),
         pl.BlockSpec((1, ns, D), lambda s, i: (s, iu, 0))],
        [spec_o] * 3, [_sds((N_CHIP, lay.T, ns), BF16)] * 3, (hm, wbuf, wbuf), ("parallel", "parallel"), 3, 3)


def slab_nn_acc(lay, zs, wbuf, idxs, ns, name, rider=None):
    tm = lay.tm
    npair = len(zs)

    def body(*refs):
        o_ref = refs[-1]

        @pl.when(pl.program_id(1) == 0)
        def _():
            o_ref[...] = jnp.zeros_like(o_ref)
        acc = _nn(refs[0][0], refs[npair][0])
        for p in range(1, npair):
            acc += _nn(refs[p][0], refs[npair + p][0])
        o_ref[...] += acc

    in_specs = [pl.BlockSpec((1, tm, ns), lambda i, s: (s, i, 0)) for _ in zs]
    in_specs += [pl.BlockSpec((1, ns, D), functools.partial(lambda i, s, q: (s, q, 0), q=q)) for q in idxs]
    return _host_call(body, rider, name, (lay.T // tm, N_CHIP), in_specs, [pl.BlockSpec((tm, D), lambda i, s: (i, 0))],
                      [_sds((lay.T, D), F32)], (*zs, *([wbuf] * npair)), ("parallel", "arbitrary"), 2 * npair, 1)


def ffn_bwd_da(lay, dy, wbuf, idn, g, u, ns, name, rider=None):
    tm = lay.tm

    def body(dy_ref, wd_ref, g_ref, u_ref, dg_ref, du_ref):
        da = _nt(dy_ref[...], wd_ref[0])
        gv = g_ref[0].astype(F32)
        uv = u_ref[0].astype(F32)
        sg = _sigmoid(gv)
        dg_ref[0] = (da * uv * (sg * (1.0 + gv * (1.0 - sg)))).astype(BF16)
        du_ref[0] = (da * (gv * sg)).astype(BF16)

    spec_z = pl.BlockSpec((1, tm, ns), lambda s, i: (s, i, 0))
    return _host_call(
        body, rider, name, (N_CHIP, lay.T // tm),
        [pl.BlockSpec((tm, D), lambda s, i: (i, 0)), pl.BlockSpec((1, ns, D), lambda s, i: (s, idn, 0)), spec_z, spec_z],
        [spec_z] * 2, [_sds((N_CHIP, lay.T, ns), BF16)] * 2, (dy, wbuf, g, u), ("parallel", "parallel"), 4, 2)


def slab_tn(lay, z, x, gbuf, idx, ns, name, rider=None):
    tk = lay.tm

    def body(z_ref, x_ref, g_in, o_ref):
        del g_in

        @pl.when(pl.program_id(1) == 0)
        def _():
            o_ref[...] = jnp.zeros_like(o_ref)
        o_ref[0] += _tn(z_ref[0], x_ref[...])

    return _host_call(
        body, rider, name, (N_CHIP, lay.T // tk),
        [pl.BlockSpec((1, tk, ns), lambda s, k: (s, k, 0)), pl.BlockSpec((tk, D), lambda s, k: (k, 0)), ANY],
        [pl.BlockSpec((1, ns, D), lambda s, k: (s, idx, 0))], [_sds(gbuf.shape, F32)], (z, x, gbuf),
        ("parallel", "arbitrary"), 3, 1, aliases={2: 0})


Q0, K0, V0, U0, QR0, KR0, PEXT = 0, 512, 640, 768, 1280, 1792, 1920


def rope_fwd(lay, p, cos, sin, name):
    def fn(pb, cs, sn):
        cs4 = jnp.concatenate([cs] * 4, axis=1)
        sn4 = jnp.concatenate([sn] * 4, axis=1)
        qr = pb[:, Q0:K0] * cs4 + pb[:, QR0:KR0] * sn4
        kr = pb[:, K0:V0] * cs + pb[:, KR0:PEXT] * sn
        return qr, kr, pb[:, V0:U0], pb[:, U0:QR0]
    return rowwise(lay, name, fn, [p, cos, sin], outs=[(ATT_W, BF16), (KV_W, BF16), (KV_W, BF16), (POOL_W, F32)])


def rope_bwd(lay, dqr, dkr, dv, du, cos, sin, name):
    def fn(dq, dk, dvb, dub, cs, sn):
        cs4 = jnp.concatenate([cs] * 4, axis=1)
        sn4 = jnp.concatenate([sn] * 4, axis=1)
        return (jnp.concatenate([dq * cs4, dk * cs, dvb, dub, dq * sn4, dk * sn], axis=1),)
    return rowwise(lay, name, fn, [dqr, dkr, dv, du, cos, sin], outs=[(PEXT, BF16)])[0]


def _attn_specs(lay):
    nbs, cbk, lbk = lay.PS // BLK, lay.C // BLK, lay.L // BLK

    def kv_map(j):
        return lambda s, n: (s * nbs + cbk + jnp.clip(n - cbk + j - 1, 0, lbk - 1), 0)

    win = [pl.BlockSpec((BLK, KV_W), kv_map(j)) for j in range(3)]
    ctx = pl.BlockSpec((lay.C, KV_W), lambda s, n: (s * (lay.PS // lay.C), 0))
    return nbs, cbk, lbk, win, ctx


def _attn_masks(n, cbk, lbk):
    row = lax.broadcasted_iota(jnp.int32, (BLK, BLK), 0)
    col = lax.broadcasted_iota(jnp.int32, (BLK, BLK), 1)
    m = n - cbk
    lat = n >= cbk
    valid = [lat & (m >= 1) & (col >= row), lat & (col >= 0), lat & (m <= lbk - 2) & (col <= row)]
    lane_lo = lax.broadcasted_iota(jnp.int32, (BLK, 2 * HEAD_DIM), 1) < HEAD_DIM
    return valid, lane_lo


def attn_fwd(lay, qr, kr, vb, sink_tab, name):
    nbs, cbk, lbk, win, ctx = _attn_specs(lay)

    def body(q_ref, k0, k1, k2, kc_ref, v0, v1, v2, vc_ref, sk_ref, o_ref, l_ref):
        n = pl.program_id(1)
        valid, lane_lo = _attn_masks(n, cbk, lbk)
        valid4 = [jnp.concatenate([v] * 4, axis=0) for v in valid]
        ks = [k0[...], k1[...], k2[...]]
        vs = [v0[...], v1[...], v2[...]]
        kc, vc = kc_ref[...], vc_ref[...]
        q2s = [q_ref[:, p * 128:(p + 1) * 128] for p in range(4)]
        outs, lses = [], []
        for hh in range(2):
            sel = lane_lo == (hh == 0)
            qm = jnp.concatenate([jnp.where(sel, q2, jnp.zeros_like(q2)) for q2 in q2s], axis=0)
            sk = jnp.concatenate([jnp.broadcast_to(sk_ref[p:p + 1, hh * HEAD_DIM:hh * HEAD_DIM + 1], (BLK, 1))
                                  for p in range(4)], axis=0)
            sw = [jnp.where(valid4[j], _nt(qm, ks[j]) * ATT_SCALE, NEG_INF) for j in range(3)]
            sc = _nt(qm, kc) * ATT_SCALE
            mx = jnp.maximum(jnp.maximum(jnp.maximum(sw[0].max(-1, keepdims=True), sw[1].max(-1, keepdims=True)),
                                         jnp.maximum(sw[2].max(-1, keepdims=True), sc.max(-1, keepdims=True))), sk)
            ew = [jnp.exp(s - mx) for s in sw]
            ec = jnp.exp(sc - mx)
            den = ew[0].sum(-1, keepdims=True) + ew[1].sum(-1, keepdims=True) + ew[2].sum(-1, keepdims=True)
            den = den + ec.sum(-1, keepdims=True) + jnp.exp(sk - mx)
            o = _nn((ec / den).astype(BF16), vc)
            for j in range(3):
                o += _nn((ew[j] / den).astype(BF16), vs[j])
            outs.append(o)
            lses.append(mx + jnp.log(den))
        for p in range(4):
            rows = slice(p * BLK, (p + 1) * BLK)
            o_ref[:, p * 128:(p + 1) * 128] = jnp.where(lane_lo, outs[0][rows], outs[1][rows]).astype(o_ref.dtype)
            l_ref[:, p * 128:(p + 1) * 128] = jnp.where(lane_lo, jnp.broadcast_to(lses[0][rows], (BLK, 128)),
                                                        jnp.broadcast_to(lses[1][rows], (BLK, 128)))

    qspec = pl.BlockSpec((BLK, ATT_W), lambda s, n: (s * nbs + n, 0))
    return pl.pallas_call(
        body, name=name, out_shape=[_sds((lay.T, ATT_W), BF16), _sds((lay.T, ATT_W), F32)], grid=(2, nbs),
        in_specs=[qspec] + win + [ctx] + win + [ctx] + [pl.BlockSpec((8, 128), lambda s, n: (0, 0))],
        out_specs=[qspec, qspec], compiler_params=_cp(("parallel", "parallel")))(qr, kr, kr, kr, kr, vb, vb, vb, vb, sink_tab)


def attn_bwd(lay, qr, kr, vb, sink_tab, lse, datt, name):
    nbs, cbk, lbk, win, ctx = _attn_specs(lay)
    C, PS = lay.C, lay.PS

    def body(q_ref, k0, k1, k2, kc_ref, v0, v1, v2, vc_ref, sk_ref, l_ref, do_ref, dq_ref, dk_ref, dv_ref, ds_ref):
        n = pl.program_id(1)
        valid, lane_lo = _attn_masks(n, cbk, lbk)

        @pl.when(n == 0)
        def _():
            dk_ref[...] = jnp.zeros_like(dk_ref)
            dv_ref[...] = jnp.zeros_like(dv_ref)
            ds_ref[...] = jnp.zeros_like(ds_ref)

        ks = [k0[...], k1[...], k2[...], kc_ref[...]]
        vs = [v0[...], v1[...], v2[...], vc_ref[...]]
        valid4 = [jnp.concatenate([v] * 4, axis=0) for v in valid]
        dks = [jnp.zeros((BLK, KV_W), F32)] * 3 + [jnp.zeros((C, KV_W), F32)]
        dvs = list(dks)
        q2s = [q_ref[:, p * 128:(p + 1) * 128] for p in range(4)]
        do2s = [do_ref[:, p * 128:(p + 1) * 128].astype(BF16) for p in range(4)]
        lse2s = [l_ref[:, p * 128:(p + 1) * 128] for p in range(4)]
        dq_h, dd_h = [], []
        for hh in range(2):
            sel = lane_lo == (hh == 0)
            qm = jnp.concatenate([jnp.where(sel, q2, jnp.zeros_like(q2)) for q2 in q2s], axis=0)
            dom = jnp.concatenate([jnp.where(sel, d2, jnp.zeros_like(d2)) for d2 in do2s], axis=0)
            lse_h = jnp.concatenate([l2[:, hh * HEAD_DIM:hh * HEAD_DIM + 1] for l2 in lse2s], axis=0)
            ps, dps = [], []
            for j in range(4):
                s = _nt(qm, ks[j]) * ATT_SCALE
                if j < 3:
                    s = jnp.where(valid4[j], s, NEG_INF)
                ps.append(jnp.exp(s - lse_h))
                dps.append(_nt(dom, vs[j]))
            dd = (ps[0] * dps[0]).sum(-1, keepdims=True) + (ps[1] * dps[1]).sum(-1, keepdims=True)
            dd = dd + (ps[2] * dps[2]).sum(-1, keepdims=True) + (ps[3] * dps[3]).sum(-1, keepdims=True)
            dq = jnp.zeros((4 * BLK, 128), F32)
            for j in range(4):
                dsb = (ps[j] * (dps[j] - dd) * ATT_SCALE).astype(BF16)
                dq += _nn(dsb, ks[j])
                dks[j] = dks[j] + _tn(dsb, qm)
                dvs[j] = dvs[j] + _tn(ps[j].astype(BF16), dom)
            dq_h.append(dq)
            dd_h.append(dd)
        for p in range(4):
            sl = slice(p * 128, (p + 1) * 128)
            rows = slice(p * BLK, (p + 1) * BLK)
            dq_ref[:, sl] = jnp.where(lane_lo, dq_h[0][rows], dq_h[1][rows])
            dd2 = jnp.where(lane_lo, jnp.broadcast_to(dd_h[0][rows], (BLK, 128)), jnp.broadcast_to(dd_h[1][rows], (BLK, 128)))
            psink = jnp.exp(sk_ref[p:p + 1, :] - lse2s[p])
            ds_ref[0, p:p + 1, :] += -jnp.sum(psink * dd2, axis=0, keepdims=True)
        dk_ref[0:C, :] += dks[3]
        dv_ref[0:C, :] += dvs[3]
        for j in range(3):
            r0 = pl.multiple_of((cbk + jnp.clip(n - cbk + j - 1, 0, lbk - 1)) * BLK, BLK)
            dk_ref[pl.ds(r0, BLK), :] += dks[j]
            dv_ref[pl.ds(r0, BLK), :] += dvs[j]

    qspec = pl.BlockSpec((BLK, ATT_W), lambda s, n: (s * nbs + n, 0))
    kvout = pl.BlockSpec((PS, KV_W), lambda s, n: (s, 0))
    return pl.pallas_call(
        body, name=name,
        out_shape=[_sds((lay.T, ATT_W), F32), _sds((lay.T, KV_W), F32), _sds((lay.T, KV_W), F32), _sds((2, 8, 128), F32)],
        grid=(2, nbs),
        in_specs=[qspec] + win + [ctx] + win + [ctx] + [pl.BlockSpec((8, 128), lambda s, n: (0, 0)), qspec, qspec],
        out_specs=[qspec, kvout, kvout, pl.BlockSpec((1, 8, 128), lambda s, n: (s, 0, 0))],
        compiler_params=_cp(("parallel", "arbitrary")))(qr, kr, kr, kr, kr, vb, vb, vb, vb, sink_tab, lse, datt)


def _winsum(x, r):
    n = x.shape[0]
    t = lax.broadcasted_iota(jnp.int32, x.shape, 0)
    acc = x
    for o in range(1, r + 1):
        acc = acc + jnp.where(t >= o, pltpu.roll(x, o, 0), 0.0) + jnp.where(t < n - o, pltpu.roll(x, n - o, 0), 0.0)
    return acc


def _wincount(n, r):
    t = lax.broadcasted_iota(jnp.int32, (n, 128), 0)
    return (jnp.minimum(t + r, n - 1) - jnp.maximum(t - r, 0) + 1).astype(F32)


def pool_fwd(lay, u, w_pool, scale, name):
    segs = [(0, lay.C), (lay.C, lay.L)]

    def body(u_ref, w_ref, s_ref, o_ref):
        for r0, n in segs:
            for g, wd in enumerate(POOL_WINDOWS):
                sl = slice(g * 128, (g + 1) * 128)
                x = u_ref[r0:r0 + n, sl]
                d = _winsum(x, wd // 2) / _wincount(n, wd // 2) - x
                y = _nn(d.astype(BF16), w_ref[g].astype(BF16)) * s_ref[:, sl]
                o_ref[r0:r0 + n, sl] = y.astype(o_ref.dtype)

    spec = pl.BlockSpec((lay.PS, POOL_W), lambda s: (s, 0))
    return pl.pallas_call(
        body, name=name, out_shape=_sds((lay.T, POOL_W), BF16), grid=(2,),
        in_specs=[spec, pl.BlockSpec(w_pool.shape, lambda s: (0, 0, 0)), pl.BlockSpec((1, POOL_W), lambda s: (0, 0))],
        out_specs=spec, compiler_params=_cp(("parallel",), VMEM_BIG))(u, w_pool, scale)


def pool_bwd(lay, u, dcat, w_pool, scale, name):
    segs = [(0, lay.C), (lay.C, lay.L)]

    def body(u_ref, dp_ref, w_ref, s_ref, du_ref, dw_ref, dsc_ref):
        for g, wd in enumerate(POOL_WINDOWS):
            sl = slice(g * 128, (g + 1) * 128)
            wb = w_ref[g].astype(BF16)
            dw = jnp.zeros((128, 128), F32)
            dsc = jnp.zeros((1, 128), F32)
            for r0, n in segs:
                x = u_ref[r0:r0 + n, sl]
                cnt = _wincount(n, wd // 2)
                d = (_winsum(x, wd // 2) / cnt - x).astype(BF16)
                dp = dp_ref[r0:r0 + n, sl]
                dsc += jnp.sum(_nn(d, wb) * dp, axis=0, keepdims=True)
                dyp = (dp * s_ref[:, sl]).astype(BF16)
                dw += _tn(d, dyp)
                dd = _nt(dyp, wb)
                du_ref[r0:r0 + n, sl] = _winsum(dd / cnt, wd // 2) - dd
            dw_ref[0, g] = dw
            dsc_ref[0, :, sl] = dsc

    spec = pl.BlockSpec((lay.PS, POOL_W), lambda s: (s, 0))
    return pl.pallas_call(
        body, name=name,
        out_shape=[_sds((lay.T, POOL_W), F32), _sds((2, 4, 128, 128), F32), _sds((2, 1, POOL_W), F32)], grid=(2,),
        in_specs=[spec, pl.BlockSpec((lay.PS, POOL_W), lambda s: (s, 1)), pl.BlockSpec(w_pool.shape, lambda s: (0, 0, 0)),
                  pl.BlockSpec((1, POOL_W), lambda s: (0, 0))],
        out_specs=[spec, pl.BlockSpec((1, 4, 128, 128), lambda s: (s, 0, 0, 0)), pl.BlockSpec((1, 1, POOL_W), lambda s: (s, 0, 0))],
        compiler_params=_cp(("parallel",), VMEM_BIG))(u, dcat, w_pool, scale)


CONV_OFFS = (-1, 0, 1, 2)
CW = 256


def _shift_rows(x, o):
    if o == 0:
        return x
    n = x.shape[0]
    t = lax.broadcasted_iota(jnp.int32, x.shape, 0)
    if o < 0:
        return jnp.where(t >= -o, pltpu.roll(x, -o, 0), 0.0)
    return jnp.where(t < n - o, pltpu.roll(x, n - o, 0), 0.0)


def conv_fwd(lay, p, col0, w, b, name):
    segs = [(0, lay.C), (lay.C, lay.L)]
    cb0 = col0 // CW

    def body(x_ref, w_ref, b_ref, o_ref):
        for r0, n in segs:
            x = x_ref[r0:r0 + n, :]
            y = jnp.broadcast_to(b_ref[...], x.shape)
            for k, o in enumerate(CONV_OFFS):
                y = y + _shift_rows(x, o) * w_ref[k:k + 1, :]
            o_ref[r0:r0 + n, :] = y

    return pl.pallas_call(
        body, name=name, out_shape=_sds((lay.T, D), F32), grid=(2, D // CW),
        in_specs=[pl.BlockSpec((lay.PS, CW), lambda s, j: (s, cb0 + j)), pl.BlockSpec((4, CW), lambda s, j: (0, j)),
                  pl.BlockSpec((1, CW), lambda s, j: (0, j))],
        out_specs=pl.BlockSpec((lay.PS, CW), lambda s, j: (s, j)),
        compiler_params=_cp(("parallel", "parallel")))(p, w, b)


def conv_bwd(lay, p, col0, w, duc, name):
    segs = [(0, lay.C), (lay.C, lay.L)]
    cb0 = col0 // CW

    def body(x_ref, w_ref, g_ref, du_ref, dw_ref, db_ref):
        dws = [jnp.zeros((1, CW), F32)] * 4
        db = jnp.zeros((1, CW), F32)
        for r0, n in segs:
            x = x_ref[r0:r0 + n, :]
            g = g_ref[r0:r0 + n, :]
            du = jnp.zeros_like(g)
            for k, o in enumerate(CONV_OFFS):
                du = du + _shift_rows(g, -o) * w_ref[k:k + 1, :]
                dws[k] = dws[k] + jnp.sum(g * _shift_rows(x, o), axis=0, keepdims=True)
            db = db + jnp.sum(g, axis=0, keepdims=True)
            du_ref[r0:r0 + n, :] = du.astype(du_ref.dtype)
        dw_ref[0] = jnp.concatenate(dws, axis=0)
        db_ref[0] = db

    return pl.pallas_call(
        body, name=name, out_shape=[_sds((lay.T, D), BF16), _sds((2, 4, D), F32), _sds((2, 1, D), F32)], grid=(2, D // CW),
        in_specs=[pl.BlockSpec((lay.PS, CW), lambda s, j: (s, cb0 + j)), pl.BlockSpec((4, CW), lambda s, j: (0, j)),
                  pl.BlockSpec((lay.PS, CW), lambda s, j: (s, j))],
        out_specs=[pl.BlockSpec((lay.PS, CW), lambda s, j: (s, j)), pl.BlockSpec((1, 4, CW), lambda s, j: (s, 0, j)),
                   pl.BlockSpec((1, 1, CW), lambda s, j: (s, 0, j))],
        compiler_params=_cp(("parallel", "parallel")))(p, w, duc)


def _softplus_neg(lam):
    z = -lam
    w = jnp.exp(-jnp.abs(z))
    log1p = jnp.where(w < 1e-2, w * (1.0 - w * (0.5 - w / 3.0)), jnp.log(1.0 + w))
    return jnp.maximum(z, 0.0) + log1p, -_sigmoid(z)


def _neg_expm1(x):
    series = -x * (1.0 + x * (0.5 + x * (1.0 / 6.0 + x * (1.0 / 24.0 + x * (1.0 / 120.0)))))
    return jnp.where(x > -0.05, series, 1.0 - jnp.exp(x))


def _lru_gates(x, xb, wa, wx, ba, bx, lam):
    r = _sigmoid(_nn(xb, wa.astype(BF16)) + ba)
    gi = _sigmoid(_nn(xb, wx.astype(BF16)) + bx)
    sp, dsp = _softplus_neg(lam)
    la = -LRU_C * r * sp
    a = jnp.exp(la)
    sq = jnp.sqrt(_neg_expm1(2.0 * la))
    return r, gi, sp, dsp, a, sq


def lru_coeffs(lay, uc, wa, wx, vec, name):
    tr = lay.tc

    def body(x_ref, wa_ref, wx_ref, v_ref, a_ref, b_ref):
        for h in range(8):
            sl = slice(h * 128, (h + 1) * 128)
            x = x_ref[:, sl]
            xb = x.astype(BF16)
            for d in range(2):
                _, gi, _, _, a, sq = _lru_gates(x, xb, wa_ref[d, h], wx_ref[d, h], v_ref[d:d + 1, sl],
                                                v_ref[2 + d:3 + d, sl], v_ref[4 + d:5 + d, sl])
                a_ref[d, h] = a
                b_ref[d, h] = sq * (gi * x)

    wspec = pl.BlockSpec((2, 8, 128, 128), lambda i: (0, 0, 0, 0))
    ospec = pl.BlockSpec((2, 8, tr, 128), lambda i: (0, 0, i, 0))
    return pl.pallas_call(
        body, name=name, out_shape=[_sds((2, 8, lay.T, 128), F32)] * 2, grid=(lay.T // tr,),
        in_specs=[pl.BlockSpec((tr, D), lambda i: (i, 0)), wspec, wspec, pl.BlockSpec((6, D), lambda i: (0, 0))],
        out_specs=[ospec, ospec], compiler_params=_cp(("parallel",), VMEM_BIG))(uc, wa, wx, vec)


def lru_coeffs_bwd(lay, uc, wa, wx, vec, da, db, name):
    tr = lay.tc

    def body(x_ref, wa_ref, wx_ref, v_ref, da_ref, db_ref, dx_ref, dwa_ref, dwx_ref, dv_ref):
        @pl.when(pl.program_id(0) == 0)
        def _():
            dwa_ref[...] = jnp.zeros_like(dwa_ref)
            dwx_ref[...] = jnp.zeros_like(dwx_ref)
            dv_ref[...] = jnp.zeros_like(dv_ref)

        for h in range(8):
            sl = slice(h * 128, (h + 1) * 128)
            x = x_ref[:, sl]
            xb = x.astype(BF16)
            dx = jnp.zeros_like(x)
            for d in range(2):
                wab, wxb = wa_ref[d, h].astype(BF16), wx_ref[d, h].astype(BF16)
                r, gi, sp, dsp, a, sq = _lru_gates(x, xb, wa_ref[d, h], wx_ref[d, h], v_ref[d:d + 1, sl],
                                                   v_ref[2 + d:3 + d, sl], v_ref[4 + d:5 + d, sl])
                dbv, dav = db_ref[d, h], da_ref[d, h]
                t1 = dbv * sq
                dgi = t1 * x
                dx = dx + t1 * gi
                dla = dav * a - (dbv * gi * x) * (a * a) / sq
                dr = dla * (-LRU_C * sp)
                dlam = jnp.sum(dla * (-LRU_C * r), axis=0, keepdims=True) * dsp
                dpa = dr * r * (1.0 - r)
                dpx = dgi * gi * (1.0 - gi)
                dpab, dpxb = dpa.astype(BF16), dpx.astype(BF16)
                dwa_ref[d, h] += _tn(xb, dpab)
                dwx_ref[d, h] += _tn(xb, dpxb)
                dx = dx + _nt(dpab, wab) + _nt(dpxb, wxb)
                dv_ref[d:d + 1, sl] += jnp.sum(dpa, axis=0, keepdims=True)
                dv_ref[2 + d:3 + d, sl] += jnp.sum(dpx, axis=0, keepdims=True)
                dv_ref[4 + d:5 + d, sl] += dlam
            dx_ref[:, sl] = dx

    wspec = pl.BlockSpec((2, 8, 128, 128), lambda i: (0, 0, 0, 0))
    gspec = pl.BlockSpec((2, 8, tr, 128), lambda i: (0, 0, i, 0))
    vspec = pl.BlockSpec((6, D), lambda i: (0, 0))
    xspec = pl.BlockSpec((tr, D), lambda i: (i, 0))
    return pl.pallas_call(
        body, name=name,
        out_shape=[_sds((lay.T, D), F32), _sds((2, 8, 128, 128), F32), _sds((2, 8, 128, 128), F32), _sds((6, D), F32)],
        grid=(lay.T // tr,), in_specs=[xspec, wspec, wspec, vspec, gspec, gspec],
        out_specs=[xspec, wspec, wspec, vspec], compiler_params=_cp(("arbitrary",), VMEM_BIG))(uc, wa, wx, vec, da, db)


GB = 2
SCAN_UNROLL = 4


def _tile_scan(a, b, up):
    t = lax.broadcasted_iota(jnp.int32, a.shape, 0)
    for d in (1, 2, 4):
        sh = 8 - d if up else d
        m = (t < 8 - d) if up else (t >= d)
        a_prev, b_prev = pltpu.roll(a, sh, 0), pltpu.roll(b, sh, 0)
        b = jnp.where(m, a * b_prev + b, b)
        a = jnp.where(m, a * a_prev, a)
    return a, b


def lru_scan(lay, a, b, name):
    segs = [(0, lay.C), (lay.C, lay.L)]

    def body(a_ref, b_ref, s_ref):
        for d in range(2):
            rev = d == 1
            state = tuple(jnp.zeros((1, 128), F32) for _ in range(GB))
            for base, n in segs:
                nt = n // 8

                def step(j, c, base=base, nt=nt, rev=rev, d=d):
                    c = list(c)
                    for u in range(SCAN_UNROLL):
                        jj = j * SCAN_UNROLL + u
                        r0 = pl.multiple_of(base + 8 * ((nt - 1 - jj) if rev else jj), 8)
                        for g in range(GB):
                            at, bt = _tile_scan(a_ref[d, g, pl.ds(r0, 8), :], b_ref[d, g, pl.ds(r0, 8), :], rev)
                            h = at * c[g] + bt
                            s_ref[d, g, pl.ds(r0, 8), :] = h
                            c[g] = h[0:1] if rev else h[7:8]
                    return tuple(c)

                state = lax.fori_loop(0, nt // SCAN_UNROLL, step, state)

    spec = pl.BlockSpec((2, GB, lay.PS, 128), lambda s, hb: (0, hb, s, 0))
    return pl.pallas_call(
        body, name=name, out_shape=_sds((2, 8, lay.T, 128), F32), grid=(2, 8 // GB),
        in_specs=[spec, spec], out_specs=spec, compiler_params=_cp(("parallel", "parallel"), VMEM_BIG))(a, b)


def lru_scan_bwd(lay, a, s, dy, name):
    segs = [(0, lay.C), (lay.C, lay.L)]
    C, PS = lay.C, lay.PS

    def body(a_ref, s_ref, g_ref, da_ref, db_ref):
        t = lax.broadcasted_iota(jnp.int32, (8, 128), 0)
        for d in range(2):
            rev = d == 1
            carry = tuple(jnp.zeros((1, 128), F32) for _ in range(GB))
            for si in (1, 0):
                base, n = segs[si]
                nt = n // 8

                def step(j, c, base=base, nt=nt, rev=rev, d=d):
                    c = list(c)
                    for u in range(SCAN_UNROLL):
                        jj = j * SCAN_UNROLL + u
                        r0 = pl.multiple_of(base + 8 * (jj if rev else (nt - 1 - jj)), 8)
                        if rev:
                            rn = pl.multiple_of(jnp.where(r0 == PS - 8, 0, r0 + 8), 8)
                            nb_zero = r0 == C - 8
                        else:
                            rn = pl.multiple_of(jnp.maximum(r0 - 8, 0), 8)
                            nb_zero = r0 == 0
                        for g in range(GB):
                            av = a_ref[d, g, pl.ds(r0, 8), :]
                            gv = g_ref[g, pl.ds(r0, 8), :]
                            sv = s_ref[d, g, pl.ds(r0, 8), :]
                            nbt = s_ref[d, g, pl.ds(rn, 8), :]
                            at, bt = _tile_scan(av, av * gv, not rev)
                            m = at * c[g] + bt
                            if rev:
                                m_next = jnp.where(t >= 1, pltpu.roll(m, 1, 0), c[g])
                                nb = jnp.where(nb_zero, 0.0, nbt[0:1])
                                h_prev = jnp.where(t < 7, pltpu.roll(sv, 7, 0), nb)
                                c[g] = m[7:8]
                            else:
                                m_next = jnp.where(t < 7, pltpu.roll(m, 7, 0), c[g])
                                nb = jnp.where(nb_zero, 0.0, nbt[7:8])
                                h_prev = jnp.where(t >= 1, pltpu.roll(sv, 1, 0), nb)
                                c[g] = m[0:1]
                            lam = gv + m_next
                            db_ref[d, g, pl.ds(r0, 8), :] = lam
                            da_ref[d, g, pl.ds(r0, 8), :] = lam * h_prev
                    return tuple(c)

                carry = lax.fori_loop(0, nt // SCAN_UNROLL, step, carry)

    spec = pl.BlockSpec((2, GB, lay.PS, 128), lambda s, hb: (0, hb, s, 0))
    return pl.pallas_call(
        body, name=name, out_shape=[_sds((2, 8, lay.T, 128), F32)] * 2, grid=(2, 8 // GB),
        in_specs=[spec, spec, pl.BlockSpec((GB, lay.PS, 128), lambda s, hb: (hb, s, 0))],
        out_specs=[spec, spec], compiler_params=_cp(("parallel", "parallel"), VMEM_BIG))(a, s, dy)


def _gelu(x):
    k = math.sqrt(2.0 / math.pi)
    t = jnp.tanh(k * (x + 0.044715 * x * x * x))
    return 0.5 * x * (1.0 + t), 0.5 * (1.0 + t) + 0.5 * x * (1.0 - t * t) * k * (1.0 + 3 * 0.044715 * x * x)


def lru_gate(lay, p, s, name):
    tr = lay.tr

    def body(g_ref, s_ref, o_ref):
        for h in range(8):
            sl = slice(h * 128, (h + 1) * 128)
            o_ref[:, sl] = (_gelu(g_ref[:, sl])[0] * (s_ref[0, h] + s_ref[1, h])).astype(o_ref.dtype)

    return pl.pallas_call(
        body, name=name, out_shape=_sds((lay.T, D), BF16), grid=(lay.nblk,),
        in_specs=[pl.BlockSpec((tr, D), lambda i: (i, 0)), pl.BlockSpec((2, 8, tr, 128), lambda i: (0, 0, i, 0))],
        out_specs=pl.BlockSpec((tr, D), lambda i: (i, 0)), compiler_params=_cp(("parallel",)))(p, s)


def lru_gate_bwd(lay, p, s, do, name):
    tr = lay.tr

    def body(g_ref, s_ref, do_ref, dg_ref, dy_ref):
        for h in range(8):
            sl = slice(h * 128, (h + 1) * 128)
            ge, dge = _gelu(g_ref[:, sl])
            dov = do_ref[:, sl]
            dg_ref[:, sl] = (dov * (s_ref[0, h] + s_ref[1, h]) * dge).astype(dg_ref.dtype)
            dy_ref[h] = dov * ge

    xspec = pl.BlockSpec((tr, D), lambda i: (i, 0))
    return pl.pallas_call(
        body, name=name, out_shape=[_sds((lay.T, D), BF16), _sds((8, lay.T, 128), F32)], grid=(lay.nblk,),
        in_specs=[xspec, pl.BlockSpec((2, 8, tr, 128), lambda i: (0, 0, i, 0)), xspec],
        out_specs=[xspec, pl.BlockSpec((8, tr, 128), lambda i: (0, i, 0))],
        compiler_params=_cp(("parallel",)))(p, s, do)


def silu_rows(x, name):
    def body(x_ref, o_ref):
        v = x_ref[...]
        o_ref[...] = (v * _sigmoid(v)).astype(o_ref.dtype)
    return pl.pallas_call(body, name=name, out_shape=_sds(x.shape, BF16), in_specs=[VMEM_SPEC], out_specs=VMEM_SPEC)(x)


def mod_grad_rows(gath, name):
    w = gath.shape[-1]

    def body(g_ref, dm_ref, db_ref):
        dm_ref[...] = jnp.zeros_like(dm_ref)
        for l in range(2):
            ctx = g_ref[0, 3 * l + 2:3 * l + 3, :]
            tot = g_ref[0, 3 * l:3 * l + 1, :] + g_ref[0, 3 * l + 1:3 * l + 2, :]
            for k in range(8):
                dm_ref[l, 2 * k:2 * k + 2, :] = g_ref[k, 3 * l:3 * l + 2, :]
                if k:
                    ctx = ctx + g_ref[k, 3 * l + 2:3 * l + 3, :]
                    tot = tot + (g_ref[k, 3 * l:3 * l + 1, :] + g_ref[k, 3 * l + 1:3 * l + 2, :])
            dm_ref[l, 16:17, :] = ctx
            db_ref[l:l + 1, :] = tot + ctx

    return pl.pallas_call(body, name=name, out_shape=[_sds((2, 32, w), F32), _sds((2, w), F32)],
                          in_specs=[VMEM_SPEC], out_specs=[VMEM_SPEC, VMEM_SPEC])(gath)


def cctx_grad(p, c_ctx, name):
    def body(a_ref, c_ref, o_ref):
        cv = c_ref[...]
        sg = _sigmoid(cv)
        o_ref[...] = 0.5 * (a_ref[0, 0:1, :] + a_ref[1, 0:1, :]) * (sg * (1.0 + cv * (1.0 - sg)))
    return pl.pallas_call(body, name=name, out_shape=_sds((1, D), F32), in_specs=[VMEM_SPEC] * 2,
                          out_specs=VMEM_SPEC)(p, c_ctx)


def loss_and_grad(lay, h, tgt, name):
    def fn(hb, tb):
        lat = (pl.program_id(0) % lay.bps) >= lay.cb
        e = jnp.where(lat, hb - tb, 0.0)
        return e * (1.0 / D), jnp.sum(e * e, axis=0, keepdims=True) * (0.5 / D)
    return rowwise(lay, name, fn, [h, tgt], outs=[(D, F32)], sums=[(1, D)])


def adamw(w, g, m, v, name):
    shape = w.shape
    w2, g2, m2, v2 = (t.reshape(-1, shape[-1]) for t in (w, g, m, v))
    rows, width = w2.shape
    tr = 256 if rows % 256 == 0 else rows
    c1 = 1.0 - ADAM_B1 ** ADAM_STEP
    c2 = 1.0 - ADAM_B2 ** ADAM_STEP

    def body(w_ref, g_ref, m_ref, v_ref, d_ref, mo_ref, vo_ref):
        gv = g_ref[...]
        mn = ADAM_B1 * m_ref[...] + (1.0 - ADAM_B1) * gv
        vn = ADAM_B2 * v_ref[...] + (1.0 - ADAM_B2) * (gv * gv)
        d_ref[...] = -ADAM_LR * ((mn / c1) / (jnp.sqrt(vn / c2) + ADAM_EPS) + ADAM_WD * w_ref[...])
        mo_ref[...] = mn
        vo_ref[...] = vn

    spec = pl.BlockSpec((tr, width), lambda i: (i, 0))
    d, mn, vn = pl.pallas_call(body, name=name, out_shape=[_sds((rows, width), F32)] * 3, grid=(rows // tr,),
                               in_specs=[spec] * 4, out_specs=[spec] * 3, compiler_params=_cp(("parallel",)))(w2, g2, m2, v2)
    return d.reshape(shape), mn.reshape(shape), vn.reshape(shape)


def adamw_ffn(w, m, v, red, kind, ns, name):
    shape = w.shape
    w2, m2, v2 = (t.reshape(-1, shape[-1]) for t in (w, m, v))
    rows, width = w2.shape
    c1 = 1.0 - ADAM_B1 ** ADAM_STEP
    c2 = 1.0 - ADAM_B2 ** ADAM_STEP
    tr, nb = ns // 2, 2
    gspec = pl.BlockSpec((tr, D), lambda i: (((i // nb) * 3 + kind) * nb + i % nb, 0))

    def body(w_ref, g_ref, m_ref, v_ref, go_ref, d_ref, mo_ref, vo_ref):
        gv = g_ref[...]
        mn = ADAM_B1 * m_ref[...] + (1.0 - ADAM_B1) * gv
        vn = ADAM_B2 * v_ref[...] + (1.0 - ADAM_B2) * (gv * gv)
        go_ref[...] = gv
        d_ref[...] = -ADAM_LR * ((mn / c1) / (jnp.sqrt(vn / c2) + ADAM_EPS) + ADAM_WD * w_ref[...])
        mo_ref[...] = mn
        vo_ref[...] = vn

    spec = pl.BlockSpec((tr, width), lambda i: (i, 0))
    outs = pl.pallas_call(body, name=name, out_shape=[_sds((rows, width), F32)] * 4, grid=(rows // tr,),
                          in_specs=[spec, gspec, spec, spec], out_specs=[spec] * 4,
                          compiler_params=_cp(("parallel",)))(w2, red, m2, v2)
    return tuple(t.reshape(shape) for t in outs)


def mod_mm(sc, w_mod, bias, name):
    wm = w_mod.shape[-1]
    tn = _pick(wm, (768, 512, 384, 256, 128))

    def body(a_ref, b_ref, c_ref, o_ref):
        o_ref[...] = _nn(a_ref[...], b_ref[...].astype(BF16)) + c_ref[...]

    return pl.pallas_call(
        body, name=name, out_shape=_sds((DEPTH, 32, wm), F32), grid=(DEPTH, wm // tn),
        in_specs=[pl.BlockSpec((32, D), lambda l, j: (0, 0)), pl.BlockSpec((None, D, tn), lambda l, j: (l, 0, j)),
                  pl.BlockSpec((None, 1, tn), lambda l, j: (l, 0, j))],
        out_specs=pl.BlockSpec((None, 32, tn), lambda l, j: (l, 0, j)),
        compiler_params=_cp(("parallel", "parallel")))(sc, w_mod, bias)


def wmod_dw(sc, dcol, name):
    wm = dcol.shape[-1]
    tm = 256

    def body(a_ref, b_ref, o_ref):
        o_ref[...] = _tn(a_ref[...], b_ref[...].astype(BF16))

    return pl.pallas_call(
        body, name=name, out_shape=_sds((DEPTH, D, wm), F32), grid=(DEPTH, D // tm),
        in_specs=[pl.BlockSpec((32, tm), lambda l, i: (0, i)), pl.BlockSpec((None, 32, wm), lambda l, i: (l, 0, 0))],
        out_specs=pl.BlockSpec((None, tm, wm), lambda l, i: (l, i, 0)),
        compiler_params=_cp(("parallel", "parallel")))(sc, dcol)


def cctx_dx(drow, w_mod, name):
    wm = w_mod.shape[-1]

    def body(a_ref, b_ref, o_ref):
        o_ref[...] = _nt(a_ref[...].astype(BF16), b_ref[...].astype(BF16))

    return pl.pallas_call(
        body, name=name, out_shape=_sds((DEPTH, 16, D), F32), grid=(DEPTH,),
        in_specs=[pl.BlockSpec((None, 16, wm), lambda l: (l, 0, 0)), pl.BlockSpec((None, D, wm), lambda l: (l, 0, 0))],
        out_specs=pl.BlockSpec((None, 16, D), lambda l: (l, 0, 0)), compiler_params=_cp(("parallel",), VMEM_BIG))(drow, w_mod)


HEAD_PERM = (0, 4, 1, 5, 2, 6, 3, 7)


def _rot_rows(wt):
    return jnp.concatenate([-wt[32:64], wt[0:32]], axis=0)


def _unrot_rows(g):
    return jnp.concatenate([g[32:64], -g[0:32]], axis=0)


def _heads(a, n):
    return [a[64 * i:64 * (i + 1)] for i in range(n)]


def kernel(x, c, ctx, c_ctx, w_mod, b_mod, ln_g, ln_b, ffn_w_gate, ffn_w_up, ffn_w_down, mix_ab_w_in, attn_sink, pool_w, pool_scale, mix_ab_w_out, lru_w_in, lru_conv_w, lru_conv_b, lru_wa, lru_ba, lru_wx, lru_bx, lru_lambda, lru_w_out, loss_target, m_c_ctx, m_w_mod, m_b_mod, m_ln_g, m_ln_b, m_ffn_w_gate, m_ffn_w_up, m_ffn_w_down, m_mix_ab_w_in, m_attn_sink, m_pool_w, m_pool_scale, m_mix_ab_w_out, m_lru_w_in, m_lru_conv_w, m_lru_conv_b, m_lru_wa, m_lru_ba, m_lru_wx, m_lru_bx, m_lru_lambda, m_lru_w_out, v_c_ctx, v_w_mod, v_b_mod, v_ln_g, v_ln_b, v_ffn_w_gate, v_ffn_w_up, v_ffn_w_down, v_mix_ab_w_in, v_attn_sink, v_pool_w, v_pool_scale, v_mix_ab_w_out, v_lru_w_in, v_lru_conv_w, v_lru_conv_b, v_lru_wa, v_lru_ba, v_lru_wx, v_lru_bx, v_lru_lambda, v_lru_w_out):
    n_lat, n_ctx = x.shape[1], ctx.shape[1]
    lay = Layout(n_ctx, n_lat)
    T = lay.T
    ns = ffn_w_gate.shape[-1]
    n_li, n_ai = lru_w_in.shape[-1], mix_ab_w_in.shape[-1]
    n_ao, n_lo = mix_ab_w_out.shape[1], lru_w_out.shape[1]
    wm = w_mod.shape[-1]
    dsh = ln_g.shape[-1]
    mx, my, mc = lax.axis_index("x"), lax.axis_index("y"), lax.axis_index("c")
    chip = 2 * mx + my
    me = 2 * chip + mc

    c_all = all_gather8(c, "ag8_c").reshape(16, D)
    cc = jnp.concatenate([c_all, c_ctx[None, :], jnp.zeros((15, D), F32)], axis=0)
    sc = silu_rows(cc, "silu_c")
    bias = lax.dynamic_slice(b_mod, (0, chip * wm), (DEPTH, wm)).reshape(DEPTH, 1, wm)
    modg = all_gather_chips(mod_mm(sc, w_mod, bias, "mod_mm"), "ag_mod")
    modtab = []
    for l in range(DEPTH):
        full = jnp.transpose(modg[:, l], (1, 0, 2)).reshape(32, N_CHIP * wm)
        mine = lax.dynamic_slice(full, (2 * me, 0), (2, N_CHIP * wm))
        modtab.append(jnp.concatenate([mine, full[16:17]], axis=0).reshape(3, N_MOD, D))

    small = jnp.concatenate([ln_g.reshape(6, dsh), ln_b.reshape(6, dsh), lru_conv_w[0], lru_conv_b, lru_ba[0],
                             lru_bx[0], lru_lambda[0], jnp.zeros((9, dsh), F32)], axis=0)
    small = all_gather_chips(small.reshape(2, 16, dsh), "ag_small").reshape(N_CHIP, 32, dsh)
    small = jnp.transpose(small, (1, 0, 2)).reshape(32, D)
    ln_g_f, ln_b_f = small[0:6].reshape(2, 3, D), small[6:12].reshape(2, 3, D)
    conv_w_f, conv_b_f = small[12:16], small[16:17]
    lru_vec = small[17:23]

    hh = 3 * ns // 2
    gate_t, up_t = jnp.swapaxes(ffn_w_gate, -1, -2), jnp.swapaxes(ffn_w_up, -1, -2)
    placed = [ffn_place(gate_t, up_t, ffn_w_down, g // 2, g % 2, f"ag_ffn{g}_place") for g in range(4)]
    wb = [gather_placed(placed[0], "ag_ffn0"), None, None, None]
    mix_sh = jnp.concatenate([lru_w_in[0].T, mix_ab_w_in[0].T, mix_ab_w_out[0], lru_w_out[0]], axis=0).astype(BF16)
    n_mix = n_li + n_ai + n_ao + n_lo
    mixw = all_gather_chips(mix_sh.reshape(2, n_mix // 2, D), "ag_mix").reshape(N_CHIP, n_mix, D)
    o1, o2, o3 = n_li, n_li + n_ai, n_li + n_ai + n_ao
    lru_in_t = mixw[:, 0:o1].reshape(N_CHIP * n_li, D)
    ab_in_t = mixw[:, o1:o2].reshape(N_CHIP * n_ai, D)
    ab_out = mixw[:, o2:o3].reshape(N_CHIP * n_ao, D)
    lru_out = mixw[:, o3:].reshape(N_CHIP * n_lo, D)
    qh, kh = _heads(ab_in_t[Q0:K0], N_HEADS), _heads(ab_in_t[K0:V0], N_KV)
    w_ext_t = jnp.concatenate([qh[h] for h in HEAD_PERM] + [ab_in_t[K0:QR0]]
                              + [_rot_rows(qh[h]) for h in HEAD_PERM] + [_rot_rows(t) for t in kh], axis=0)
    oh = _heads(ab_out[0:ATT_W], N_HEADS)
    w_out_ext = jnp.concatenate([oh[h] for h in HEAD_PERM] + [ab_out[ATT_W:]], axis=0)

    t = jnp.arange(n_lat)
    inv = ROPE_THETA ** (-jnp.arange(16, dtype=F32) / 16.0)
    ang = jnp.concatenate([(t // GRID_W).astype(F32)[:, None] * inv, (t % GRID_W).astype(F32)[:, None] * inv], axis=-1)
    cos1 = jnp.concatenate([jnp.ones((n_ctx, 32), F32), jnp.cos(ang)], axis=0)
    sin1 = jnp.concatenate([jnp.zeros((n_ctx, 32), F32), jnp.sin(ang)], axis=0)
    cos_t = jnp.tile(cos1, (2, 4))
    sin_t = jnp.tile(sin1, (2, 4))
    sk = attn_sink[0]
    sink_tab = jnp.concatenate([jnp.repeat(jnp.stack([sk[:4], sk[4:]], axis=1), HEAD_DIM, axis=1),
                                jnp.zeros((4, 128), F32)], axis=0)
    pscale = pool_scale.reshape(1, POOL_W)

    h0 = jnp.concatenate([ctx, x], axis=1).reshape(T, D)
    tgt = loss_target.reshape(2 * n_lat, D)

    def lnv(l, j):
        return jnp.stack([ln_g_f[l, j], ln_b_f[l, j]])

    subs = [(0, 0, 0.5, 0), (0, 3, 1.0, 1), (0, 6, 0.5, 2), (1, 0, 0.5, 0), (1, 3, 1.0, 1), (1, 6, 0.5, 2)]

    def ffn_core(hm, l, f):
        tag = f"l{l}f{f}"
        gi = 2 * l + f
        w = wb[gi].reshape(N_CHIP, 3 * ns, D)
        if gi == 3:
            g, u, a = ffn_up(lay, hm, w, 0, 1, ns, f"ffn_up_{tag}")
            (y,) = slab_nn_acc(lay, [a], w, [2], ns, f"ffn_down_{tag}")
            return y, dict(g=g, u=u, a=a, nbuf=None)
        g, u, a, nbuf = ffn_up(lay, hm, w, 0, 1, ns, f"ffn_up_{tag}", rider=rider_gather_xy(placed[gi + 1]))
        y, nbuf = slab_nn_acc(lay, [a], w, [2], ns, f"ffn_down_{tag}", rider=rider_gather_fwd(nbuf))
        return y, dict(g=g, u=u, a=a, nbuf=nbuf)

    def mixa_core(hm):
        p = mm_nt(hm, w_ext_t, "mixa_in")
        qr, kr, vb, u = rope_fwd(lay, p, cos_t, sin_t, "rope")
        att, lse = attn_fwd(lay, qr, kr, vb, sink_tab, "attn")
        pool = pool_fwd(lay, u, pool_w[0], pscale, "pool")
        cat = jnp.concatenate([att, pool], axis=1)
        return mm_nn(cat, w_out_ext, "mixa_out"), dict(qr=qr, kr=kr, vb=vb, u=u, lse=lse, cat=cat)

    def mixc_core(hm):
        p = mm_nt(hm, lru_in_t, "mixc_in")
        uc = conv_fwd(lay, p, D, conv_w_f, conv_b_f, "conv")
        a, b = lru_coeffs(lay, uc, lru_wa[0], lru_wx[0], lru_vec, "lru_coef")
        s = lru_scan(lay, a, b, "lru_scan")
        o = lru_gate(lay, p, s, "lru_gate")
        return mm_nn(o, lru_out, "mixc_out"), dict(p=p, uc=uc, a=a, s=s, o=o)

    recs = []
    h = h0
    hm = modulate(lay, h0, modtab[0], 0, 1, "mod_first")
    for k, (l, k0, coef, j) in enumerate(subs):
        if k0 == 3:
            y, core = mixa_core(hm) if l == 0 else mixc_core(hm)
        else:
            y, core = ffn_core(hm, l, k0 // 6)
        nxt = None if k == 5 else (modtab[subs[k + 1][0]], subs[k + 1][1], subs[k + 1][1] + 1)
        nbuf = core.pop("nbuf", None)
        res = resid_ln(lay, h, y, modtab[l], k0 + 2, coef, lnv(l, j), f"ln_s{k}", nxt=nxt,
                       rider=None if nbuf is None else rider_gather_d2d(nbuf))
        if nbuf is not None:
            wb[2 * l + k0 // 6 + 1] = res[-1]
        recs.append(dict(h=h, hm=hm, y=y, xhat=res[1], rstd=res[2], **core))
        h = res[0]
        hm = res[3] if nxt is not None else None

    dout, lparts = loss_and_grad(lay, h, tgt, "loss")
    loss = lax.psum(jnp.sum(lparts), ("x", "y", "c"))

    dln = {}
    dms = {}
    mixg = {}
    ffn_red = [lax.empty((4, 2, hh, D), F32)]
    pending = []

    def ffn_core_bwd(dy, r, l, f):
        tag = f"l{l}f{f}"
        gi = 2 * l + f
        w = wb[gi].reshape(N_CHIP, 3 * ns, D)
        prev = pending.pop() if pending else None
        if prev is None:
            dg, du = ffn_bwd_da(lay, dy, w, 2, r["g"], r["u"], ns, f"ffn_da_{tag}")
        else:
            dg, du, recv = ffn_bwd_da(lay, dy, w, 2, r["g"], r["u"], ns, f"ffn_da_{tag}", rider=rider_reduce_sib(prev[1]))
        gb = lax.empty((N_CHIP, 3 * ns, D), F32)
        if prev is None:
            (gb,) = slab_tn(lay, r["a"], dy, gb, 2, ns, f"ffn_dwd_{tag}")
            (gb,) = slab_tn(lay, dg, r["hm"], gb, 0, ns, f"ffn_dwg_{tag}")
            (gb,) = slab_tn(lay, du, r["hm"], gb, 1, ns, f"ffn_dwu_{tag}")
            (dhm,) = slab_nn_acc(lay, [dg, du], w, [0, 1], ns, f"ffn_dh_{tag}")
        else:
            q = add_own_half(prev[1], recv, BF16, f"rs_add2_ffn{prev[0]}")
            gb, arr = slab_tn(lay, r["a"], dy, gb, 2, ns, f"ffn_dwd_{tag}", rider=rider_reduce_copy(q, 0))
            gb, arr = slab_tn(lay, dg, r["hm"], gb, 0, ns, f"ffn_dwg_{tag}", rider=rider_reduce_copy(q, 1, arr))
            gb, arr = slab_tn(lay, du, r["hm"], gb, 1, ns, f"ffn_dwu_{tag}", rider=rider_reduce_copy(q, 2, arr))
            red = sum_slots(q, arr, f"rs_add4_ffn{prev[0]}", dst=ffn_red[0], g=prev[0])
            dhm, ffn_red[0] = slab_nn_acc(lay, [dg, du], w, [0, 1], ns, f"ffn_dh_{tag}", rider=rider_join(red, prev[0]))
        pending.append((gi, gb.reshape(N_CHIP, 2, hh, D)))
        return dhm

    def mixc_core_bwd(dy, r):
        do_c = mm_nt(dy, lru_out, "mixc_out_dx")
        mixg["lru_out"] = mm_tn(r["o"], dy, "mixc_out_dw")
        dgate, dyg = lru_gate_bwd(lay, r["p"], r["s"], do_c, "lru_gate_b")
        da_c, db_c = lru_scan_bwd(lay, r["a"], r["s"], dyg, "lru_scan_b")
        duc, mixg["wa"], mixg["wx"], mixg["vec"] = lru_coeffs_bwd(lay, r["uc"], lru_wa[0], lru_wx[0], lru_vec, da_c, db_c,
                                                                  "lru_coef_b")
        du_c, mixg["cw"], mixg["cb"] = conv_bwd(lay, r["p"], D, conv_w_f, duc, "conv_b")
        dp_c = jnp.concatenate([dgate, du_c], axis=1)
        mixg["lru_in_t"] = mm_tn(dp_c, r["hm"], "mixc_in_dw")
        return mm_nn(dp_c, lru_in_t, "mixc_in_dx")

    def mixa_core_bwd(dy, r):
        dcat = mm_nt(dy, w_out_ext, "mixa_out_dx")
        mixg["out_ext"] = mm_tn(r["cat"], dy, "mixa_out_dw")
        dqr, dkr, dv, mixg["sink"] = attn_bwd(lay, r["qr"], r["kr"], r["vb"], sink_tab, r["lse"], dcat, "attn_b")
        du_a, mixg["pw"], mixg["ps"] = pool_bwd(lay, r["u"], dcat, pool_w[0], pscale, "pool_b")
        dp_a = rope_bwd(lay, dqr, dkr, dv, du_a, cos_t, sin_t, "rope_b")
        mixg["ext_t"] = mm_tn(dp_a, r["hm"], "mixa_in_dw")
        return mm_nn(dp_a, w_ext_t, "mixa_in_dx")

    l, k0, coef, j = subs[5]
    dy, dres, s1 = ln_bwd(lay, dout, recs[5]["xhat"], recs[5]["rstd"], recs[5]["y"], modtab[l], k0 + 2, coef, lnv(l, j),
                          "lnb_s5")
    for k in range(5, -1, -1):
        l, k0, coef, j = subs[k]
        r = recs[k]
        if k0 == 3:
            dhm = mixa_core_bwd(dy, r) if l == 0 else mixc_core_bwd(dy, r)
        else:
            dhm = ffn_core_bwd(dy, r, l, k0 // 6)
        dln[(l, j)] = block_sums(lay, s1, f"bs_ln_s{k}")
        if k > 0:
            lp, k0p, coefp, jp = subs[k - 1]
            rp = recs[k - 1]
            dy, dres, s1, s2 = modb_lnb(lay, dres, dhm, r["h"], modtab[l], k0 + 1, rp["xhat"], rp["rstd"], rp["y"],
                                        modtab[lp], k0p + 2, coefp, lnv(lp, jp), f"modb_lnb_s{k}")
        else:
            gx, s2 = mod_bwd(lay, dres, dhm, r["h"], modtab[l], k0 + 1, "modb_s0")
        dms[(l, k0)] = block_sums(lay, s2, f"bs_mod_s{k}")
    grad_x = gx.reshape(2, n_lat, D)
    g_lru_out, g_wa, g_wx, g_vec, g_cw, g_cb = (mixg[n] for n in ("lru_out", "wa", "wx", "vec", "cw", "cb"))
    g_lru_in_t, g_out_ext, g_sink, g_pw, g_ps, g_ext_t = (mixg[n] for n in ("lru_in_t", "out_ext", "sink", "pw", "ps", "ext_t"))

    rows = []
    for l in range(DEPTH):
        per_k = []
        for k0, j in ((0, 0), (3, 1), (6, 2)):
            per_k += [dms[(l, k0)][:3, 0], dms[(l, k0)][:3, 1], dln[(l, j)][:3, 2]]
        rows.append(jnp.stack(per_k, axis=1).reshape(3, N_MOD * D))
    dmod_loc = jnp.concatenate(rows + [jnp.zeros((2, N_MOD * D), F32)], axis=0)
    dmod_all, g_b_mod = mod_grad_rows(all_gather8(dmod_loc, "ag8_dmod"), "dmod_rows")
    dcol = lax.dynamic_slice(dmod_all, (0, 0, chip * wm), (DEPTH, 32, wm))
    g_w_mod = wmod_dw(sc, dcol, "wmod_dw")
    g_cctx = cctx_grad(cctx_dx(dcol[:, 16:32], w_mod, "cctx_dx"), c_ctx[None, :], "cctx_grad")

    gq = _heads(g_ext_t[Q0:K0], N_HEADS)
    gqr = _heads(g_ext_t[QR0:KR0], N_HEADS)
    g_q = [None] * N_HEADS
    for i, h in enumerate(HEAD_PERM):
        g_q[h] = gq[i] + _unrot_rows(gqr[i])
    gk = [a + _unrot_rows(b) for a, b in zip(_heads(g_ext_t[K0:V0], N_KV), _heads(g_ext_t[KR0:PEXT], N_KV))]
    g_ab_in_t = jnp.concatenate(g_q + gk + [g_ext_t[V0:QR0]], axis=0)
    go = _heads(g_out_ext[0:ATT_W], N_HEADS)
    g_o = [None] * N_HEADS
    for i, h in enumerate(HEAD_PERM):
        g_o[h] = go[i]
    g_ab_out = jnp.concatenate(g_o + [g_out_ext[ATT_W:]], axis=0)
    mix_g = jnp.concatenate([g_lru_in_t.reshape(N_CHIP, n_li, D), g_ab_in_t.reshape(N_CHIP, n_ai, D),
                             g_ab_out.reshape(N_CHIP, n_ao, D), g_lru_out.reshape(N_CHIP, n_lo, D)], axis=1)

    g_ln_g = jnp.stack([jnp.stack([dln[(l, j)][3, 1] for j in range(3)]) for l in range(DEPTH)])
    g_ln_b = jnp.stack([jnp.stack([dln[(l, j)][3, 0] for j in range(3)]) for l in range(DEPTH)])
    sink_row = jnp.sum(g_sink, axis=0)[:4]
    g_sink8 = jnp.concatenate([sink_row[:, 0], sink_row[:, HEAD_DIM]])
    misc = jnp.concatenate([g_sink8, jnp.sum(g_ps, axis=0).reshape(POOL_W), jnp.zeros((D - 8 - POOL_W,), F32)])
    small_g = jnp.concatenate([
        g_ln_g.reshape(6, D), g_ln_b.reshape(6, D), jnp.sum(g_cw, axis=0), jnp.sum(g_cb, axis=0), g_vec,
        misc[None, :], jnp.sum(g_pw, axis=0).reshape(64, D), g_wa.reshape(256, D), g_wx.reshape(256, D), g_cctx,
        jnp.zeros((39, D), F32)], axis=0)
    n_small = small_g.shape[0] // N_CHIP
    mix_buf = jnp.concatenate([mix_g, small_g.reshape(N_CHIP, n_small, D)], axis=1)
    n_mb = n_mix + n_small

    last_g, last_buf = pending.pop()
    ffn_red = reduce_scatter_chips(last_buf, f"ffn{last_g}", wire=BF16, dst=ffn_red[0], g=last_g).reshape(12 * ns, D)
    mix_red = reduce_scatter_chips(mix_buf.reshape(N_CHIP, 2, n_mb // 2, D), "mix").reshape(n_mb, D)
    small_red = all_gather_chips(mix_red[n_mix:].reshape(2, n_small // 2, D), "ag_smallg").reshape(N_CHIP * n_small, D)

    ffn_kind = dict(ffn_w_gate=0, ffn_w_up=1, ffn_w_down=2)

    def cols(a):
        return lax.dynamic_slice_in_dim(a, chip * dsh, dsh, axis=a.ndim - 1)

    sr = small_red
    grads = dict(
        c_ctx=sr[600], w_mod=g_w_mod, b_mod=g_b_mod,
        ln_g=cols(sr[0:6]).reshape(2, 3, dsh), ln_b=cols(sr[6:12]).reshape(2, 3, dsh),
        mix_ab_w_in=mix_red[o1:o2][None], attn_sink=sr[23, 0:8][None], pool_w=sr[24:88].reshape(1, 4, 128, 128),
        pool_scale=sr[23, 8:8 + POOL_W][None], mix_ab_w_out=mix_red[o2:o3][None], lru_w_in=mix_red[0:o1].T[None],
        lru_conv_w=cols(sr[12:16])[None], lru_conv_b=cols(sr[16:17]), lru_wa=sr[88:344].reshape(1, 2, 8, 128, 128),
        lru_ba=cols(sr[17:19])[None], lru_wx=sr[344:600].reshape(1, 2, 8, 128, 128), lru_bx=cols(sr[19:21])[None],
        lru_lambda=cols(sr[21:23])[None], lru_w_out=mix_red[o3:n_mix][None])
    params = dict(c_ctx=(c_ctx, m_c_ctx, v_c_ctx), w_mod=(w_mod, m_w_mod, v_w_mod), b_mod=(b_mod, m_b_mod, v_b_mod),
                  ln_g=(ln_g, m_ln_g, v_ln_g), ln_b=(ln_b, m_ln_b, v_ln_b),
                  ffn_w_gate=(ffn_w_gate, m_ffn_w_gate, v_ffn_w_gate), ffn_w_up=(ffn_w_up, m_ffn_w_up, v_ffn_w_up),
                  ffn_w_down=(ffn_w_down, m_ffn_w_down, v_ffn_w_down),
                  mix_ab_w_in=(mix_ab_w_in, m_mix_ab_w_in, v_mix_ab_w_in), attn_sink=(attn_sink, m_attn_sink, v_attn_sink),
                  pool_w=(pool_w, m_pool_w, v_pool_w), pool_scale=(pool_scale, m_pool_scale, v_pool_scale),
                  mix_ab_w_out=(mix_ab_w_out, m_mix_ab_w_out, v_mix_ab_w_out), lru_w_in=(lru_w_in, m_lru_w_in, v_lru_w_in),
                  lru_conv_w=(lru_conv_w, m_lru_conv_w, v_lru_conv_w), lru_conv_b=(lru_conv_b, m_lru_conv_b, v_lru_conv_b),
                  lru_wa=(lru_wa, m_lru_wa, v_lru_wa), lru_ba=(lru_ba, m_lru_ba, v_lru_ba), lru_wx=(lru_wx, m_lru_wx, v_lru_wx),
                  lru_bx=(lru_bx, m_lru_bx, v_lru_bx), lru_lambda=(lru_lambda, m_lru_lambda, v_lru_lambda),
                  lru_w_out=(lru_w_out, m_lru_w_out, v_lru_w_out))
    gl, dl, ml, vl = [], [], [], []
    transposed = ("ffn_w_gate", "ffn_w_up", "mix_ab_w_in")
    for name, (w, m, v) in params.items():
        if name in transposed:
            w, m, v = (jnp.swapaxes(t, -1, -2) for t in (w, m, v))
        if name in ffn_kind:
            g, d, mn, vn = adamw_ffn(w, m, v, ffn_red, ffn_kind[name], ns, f"adamw_{name}")
        else:
            g = grads[name].reshape(w.shape)
            d, mn, vn = adamw(w, g, m, v, f"adamw_{name}")
        if name in transposed:
            g, d, mn, vn = (jnp.swapaxes(t, -1, -2) for t in (g, d, mn, vn))
        gl.append(g)
        dl.append(d)
        ml.append(mn)
        vl.append(vn)
    return (loss, grad_x, *gl, *dl, *ml, *vl)
```

```python
import functools
import math

import jax
import jax.numpy as jnp
from jax import lax
from jax.experimental import pallas as pl
from jax.experimental.pallas import tpu as pltpu

F32, BF16 = jnp.float32, jnp.bfloat16
MESH = pl.DeviceIdType.MESH
ANY = pl.BlockSpec(memory_space=pl.ANY)
VMEM_SPEC = pl.BlockSpec(memory_space=pltpu.VMEM)

D = 1024
N_CHIP = 4
HEAD_DIM, N_HEADS, N_KV = 64, 8, 2
ATT_W, KV_W, POOL_W = 512, 128, 512
POOL_WINDOWS = (2, 4, 8, 16)
BLK = 128
ATT_SCALE = HEAD_DIM ** -0.5
ROPE_THETA = 10000.0
GRID_W = 64
LRU_C = 8.0
LN_EPS = 1e-5
NEG_INF = -1e30
DEPTH = 2
ALPHA = (2 * DEPTH) ** 0.25
N_MOD = 9
ADAM_LR, ADAM_B1, ADAM_B2, ADAM_EPS, ADAM_WD, ADAM_STEP = 0.001, 0.9, 0.999, 1e-08, 0.01, 10
VMEM_BIG = 48 * 1024 * 1024


def _cp(sem=None, vmem=None):
    kw = {}
    if sem is not None:
        kw["dimension_semantics"] = sem
    if vmem is not None:
        kw["vmem_limit_bytes"] = vmem
    return pltpu.CompilerParams(**kw)


def _sds(shape, dtype):
    return jax.ShapeDtypeStruct(tuple(shape), dtype)


def _pick(n, cands):
    for c in cands:
        if n % c == 0:
            return c
    return n


def _dot(a, b, dims):
    return lax.dot_general(a, b, (dims, ((), ())), preferred_element_type=F32)


def _nn(a, b):
    return _dot(a, b, ((1,), (0,)))


def _nt(a, b):
    return _dot(a, b, ((1,), (1,)))


def _tn(a, b):
    return _dot(a, b, ((0,), (0,)))


def _sigmoid(x):
    return 0.5 * jnp.tanh(0.5 * x) + 0.5


def _me():
    return lax.axis_index("x"), lax.axis_index("y"), lax.axis_index("c")


def _rcopy(src, dst, ssem, rsem, dev):
    return pltpu.make_async_remote_copy(src_ref=src, dst_ref=dst, send_sem=ssem, recv_sem=rsem,
                                        device_id=dev, device_id_type=MESH)


def all_gather8(x, name):
    def body(x_ref, o_ref, ssem, rsem, lsem):
        mx, my, mc = _me()
        me = 4 * mx + 2 * my + mc
        loc = pltpu.make_async_copy(x_ref, o_ref.at[me], lsem)
        loc.start()
        peers = []
        for m in range(1, 8):
            px = 1 - mx if (m >> 2) & 1 else mx
            py = 1 - my if (m >> 1) & 1 else my
            pc = 1 - mc if m & 1 else mc
            peers.append((px, py, pc))
        sends = [_rcopy(x_ref, o_ref.at[me], ssem.at[k], rsem.at[k], p) for k, p in enumerate(peers)]
        for cp in sends:
            cp.start()
        for k, (px, py, pc) in enumerate(peers):
            _rcopy(x_ref, o_ref.at[4 * px + 2 * py + pc], ssem.at[k], rsem.at[k], (px, py, pc)).wait_recv()
        for cp in sends:
            cp.wait_send()
        loc.wait()

    return pl.pallas_call(
        body, name=name, out_shape=_sds((8,) + x.shape, x.dtype),
        in_specs=[VMEM_SPEC], out_specs=VMEM_SPEC,
        scratch_shapes=[pltpu.SemaphoreType.DMA((7,)), pltpu.SemaphoreType.DMA((7,)), pltpu.SemaphoreType.DMA],
    )(x)


_ROW_BLOCKS = (512, 384, 352, 256, 224, 128)


def _idx(v):
    return jnp.reshape(v, (1,)).astype(jnp.int32)


def place_slab(shard, name):
    _, h, w = shard.shape
    th = _pick(h, _ROW_BLOCKS)

    def body(s_ref, x_ref, o_ref):
        del s_ref
        o_ref[...] = x_ref[...]

    return pl.pallas_call(
        body, name=name, out_shape=_sds((N_CHIP,) + shard.shape, shard.dtype),
        grid_spec=pltpu.PrefetchScalarGridSpec(
            num_scalar_prefetch=1, grid=(2, h // th),
            in_specs=[pl.BlockSpec((None, th, w), lambda k, r, s: (k, r, 0))],
            out_specs=pl.BlockSpec((None, None, th, w), lambda k, r, s: (s[0], k, r, 0))),
    )(_idx(2 * lax.axis_index("x") + lax.axis_index("y")), shard)


def ffn_place(w_gate_t, w_up_t, w_down, l, f, name):
    ns = w_down.shape[-2]
    tc = 256

    def body(s_ref, g_ref, u_ref, d_ref, o_ref):
        del s_ref
        k = pl.program_id(0)

        @pl.when(k == 0)
        def _():
            o_ref[...] = g_ref[...].astype(BF16)

        @pl.when(k == 1)
        def _():
            o_ref[...] = u_ref[...].astype(BF16)

        @pl.when(k == 2)
        def _():
            o_ref[...] = d_ref[...].astype(BF16)

    spec = pl.BlockSpec((None, None, ns, tc), lambda k, j, s: (l, f, 0, j))
    out = pl.pallas_call(
        body, name=name, out_shape=_sds((N_CHIP, 3 * ns, D), BF16),
        grid_spec=pltpu.PrefetchScalarGridSpec(
            num_scalar_prefetch=1, grid=(3, D // tc), in_specs=[spec, spec, spec],
            out_specs=pl.BlockSpec((None, ns, tc), lambda k, j, s: (s[0], k, j))),
    )(_idx(2 * lax.axis_index("x") + lax.axis_index("y")), w_gate_t, w_up_t, w_down)
    return out.reshape(N_CHIP, 2, 3 * ns // 2, D)


def all_gather_chips(shard, name):
    return gather_placed(place_slab(shard, name + "_place"), name)


def gather_placed(full, name):
    def body(x_ref, o_ref, ssem, rsem):
        del x_ref
        mx, my, mc = _me()
        s = 2 * mx + my
        sib = (mx, my, 1 - mc)
        chips = [(1 - mx, my), (mx, 1 - my), (1 - mx, 1 - my)]
        first = [_rcopy(o_ref.at[s, mc], o_ref.at[s, mc], ssem.at[j], rsem.at[j], (px, py, mc))
                 for j, (px, py) in enumerate(chips)]
        for cp in first:
            cp.start()
        passed = []
        for j, (px, py) in enumerate(chips):
            ps = 2 * px + py
            _rcopy(o_ref.at[ps, mc], o_ref.at[ps, mc], ssem.at[j], rsem.at[j], (px, py, mc)).wait_recv()
            fw = _rcopy(o_ref.at[ps, mc], o_ref.at[ps, mc], ssem.at[3 + j], rsem.at[3 + j], sib)
            fw.start()
            passed.append(fw)
        for j, (px, py) in enumerate(chips):
            ps = 2 * px + py
            _rcopy(o_ref.at[ps, 1 - mc], o_ref.at[ps, 1 - mc], ssem.at[3 + j], rsem.at[3 + j], sib).wait_recv()
        for cp in first + passed:
            cp.wait_send()

    return pl.pallas_call(
        body, name=name, out_shape=_sds(full.shape, full.dtype), in_specs=[ANY], out_specs=ANY,
        input_output_aliases={0: 0},
        scratch_shapes=[pltpu.SemaphoreType.DMA((6,)), pltpu.SemaphoreType.DMA((6,))],
    )(full)


def sibling_send_other_half(buf, name):
    def body(x_ref, o_ref, ssem, rsem):
        mx, my, mc = _me()
        sib = (mx, my, 1 - mc)
        cps = [_rcopy(x_ref.at[k, 1 - mc], o_ref.at[k], ssem.at[k], rsem.at[k], sib) for k in range(N_CHIP)]
        for cp in cps:
            cp.start()
        for cp in cps:
            cp.wait_recv()
        for cp in cps:
            cp.wait_send()

    n, _, h, w = buf.shape
    return pl.pallas_call(
        body, name=name, out_shape=_sds((n, h, w), buf.dtype), in_specs=[ANY], out_specs=ANY,
        scratch_shapes=[pltpu.SemaphoreType.DMA((N_CHIP,)), pltpu.SemaphoreType.DMA((N_CHIP,))],
    )(buf)


def chips_all_to_all(q, name):
    def body(x_ref, o_ref, ssem, rsem):
        mx, my, mc = _me()
        s = 2 * mx + my
        chips = [(1 - mx, my), (mx, 1 - my), (1 - mx, 1 - my)]
        cps = [_rcopy(x_ref.at[2 * px + py], o_ref.at[s], ssem.at[j], rsem.at[j], (px, py, mc))
               for j, (px, py) in enumerate(chips)]
        for cp in cps:
            cp.start()
        for j, (px, py) in enumerate(chips):
            ps = 2 * px + py
            _rcopy(x_ref.at[ps], o_ref.at[ps], ssem.at[j], rsem.at[j], (px, py, mc)).wait_recv()
        for cp in cps:
            cp.wait_send()

    return pl.pallas_call(
        body, name=name, out_shape=_sds(q.shape, q.dtype), in_specs=[ANY], out_specs=ANY,
        scratch_shapes=[pltpu.SemaphoreType.DMA((3,)), pltpu.SemaphoreType.DMA((3,))],
    )(q)


def sibling_join_halves(both, name, g=None):
    def body(x_ref, o_ref, ssem, rsem):
        del x_ref
        mx, my, mc = _me()
        sib = (mx, my, 1 - mc)
        o = o_ref if g is None else o_ref.at[g]
        cp = _rcopy(o.at[mc], o.at[mc], ssem, rsem, sib)
        cp.start()
        _rcopy(o.at[1 - mc], o.at[1 - mc], ssem, rsem, sib).wait_recv()
        cp.wait_send()

    return pl.pallas_call(
        body, name=name, out_shape=_sds(both.shape, both.dtype), in_specs=[ANY], out_specs=ANY,
        input_output_aliases={0: 0}, scratch_shapes=[pltpu.SemaphoreType.DMA, pltpu.SemaphoreType.DMA],
    )(both)


def add_own_half(buf, recv, wire, name):
    n, _, h, w = buf.shape
    th = _pick(h, _ROW_BLOCKS)

    def body(c_ref, a_ref, b_ref, o_ref):
        del c_ref
        o_ref[...] = (a_ref[...] + b_ref[...]).astype(o_ref.dtype)

    return pl.pallas_call(
        body, name=name, out_shape=_sds((n, h, w), wire),
        grid_spec=pltpu.PrefetchScalarGridSpec(
            num_scalar_prefetch=1, grid=(n, h // th),
            in_specs=[pl.BlockSpec((None, None, th, w), lambda k, r, c: (k, c[0], r, 0)),
                      pl.BlockSpec((None, th, w), lambda k, r, c: (k, r, 0))],
            out_specs=pl.BlockSpec((None, th, w), lambda k, r, c: (k, r, 0))),
    )(_idx(lax.axis_index("c")), buf, recv)


def sum_slots(q, r, name, dst=None, g=None):
    n, h, w = r.shape
    th = _pick(h, _ROW_BLOCKS)

    def body(i_ref, q_ref, r1, r2, r3, *rest):
        del i_ref
        rest[-1][...] = ((q_ref[...].astype(F32) + r1[...].astype(F32)) + r2[...].astype(F32)) + r3[...].astype(F32)

    def slot(d):
        return lambda i, ix: ((ix[0] + d) % N_CHIP, i, 0)

    idx = jnp.stack([2 * lax.axis_index("x") + lax.axis_index("y"), lax.axis_index("c")]).astype(jnp.int32)
    in_specs = [pl.BlockSpec((None, th, w), slot(d)) for d in (0, 1, 2, 3)]
    if dst is None:
        return pl.pallas_call(
            body, name=name, out_shape=_sds((2, h, w), F32),
            grid_spec=pltpu.PrefetchScalarGridSpec(
                num_scalar_prefetch=1, grid=(h // th,), in_specs=in_specs,
                out_specs=pl.BlockSpec((None, th, w), lambda i, ix: (ix[1], i, 0))),
        )(idx, q, r, r, r)
    return pl.pallas_call(
        body, name=name, out_shape=_sds(dst.shape, F32),
        grid_spec=pltpu.PrefetchScalarGridSpec(
            num_scalar_prefetch=1, grid=(h // th,), in_specs=in_specs + [ANY],
            out_specs=pl.BlockSpec((None, None, th, w), lambda i, ix: (g, ix[1], i, 0))),
        input_output_aliases={5: 0},
    )(idx, q, r, r, r, dst)


def reduce_scatter_chips(buf, tag, wire=F32, dst=None, g=None):
    recv = sibling_send_other_half(buf, f"rs_sib_{tag}")
    q = add_own_half(buf, recv, wire, f"rs_add2_{tag}")
    r = chips_all_to_all(q, f"rs_a2a_{tag}")
    red = sum_slots(q, r, f"rs_add4_{tag}", dst=dst, g=g)
    return sibling_join_halves(red, f"rs_join_{tag}", g=g)


class Layout:
    def __init__(self, n_ctx, n_lat):
        self.C, self.L = n_ctx, n_lat
        self.PS = n_ctx + n_lat
        self.T = 2 * self.PS
        self.tr = _pick(math.gcd(n_ctx, n_lat), (256, 128))
        self.bps = self.PS // self.tr
        self.cb = n_ctx // self.tr
        self.nblk = self.T // self.tr
        self.tm = _pick(self.T, (1152, 768, 512, 256, 128))
        self.tc = _pick(self.T, (512, 256, 128))

    def seg(self, i):
        return jnp.where(i % self.bps < self.cb, 2, i // self.bps)


def rowwise(lay, name, fn, rows, segs=(), vecs=(), outs=(), sums=(), rider=None):
    tr, nblk = lay.tr, lay.nblk
    n_r, n_s, n_v, n_o = len(rows), len(segs), len(vecs), len(outs)

    def body(*refs):
        ins = refs[:n_r + n_s + n_v]
        ors = refs[n_r + n_s + n_v:]
        vals = [r[...] for r in ins[:n_r]] + [r[0] for r in ins[n_r:n_r + n_s]] + [r[...] for r in ins[n_r + n_s:]]
        res = fn(*vals)
        for k in range(n_o):
            ors[k][...] = res[k].astype(ors[k].dtype)
        for k in range(len(sums)):
            ors[n_o + k][0] = res[n_o + k]

    def all_rows(i):
        return (i, 0)

    def lat_rows(i):
        return ((i // lay.bps) * (lay.bps - lay.cb) + jnp.maximum(i % lay.bps - lay.cb, 0), 0)

    in_specs = [pl.BlockSpec((tr, a.shape[1]), all_rows if a.shape[0] == lay.T else lat_rows) for a in rows]
    in_specs += [pl.BlockSpec((1,) + a.shape[1:], lambda i: (lay.seg(i), 0, 0)) for a in segs]
    in_specs += [pl.BlockSpec(a.shape, lambda i: (0, 0)) for a in vecs]
    out_shape = [_sds((2 * lay.L if o[2:] else lay.T, o[0]), o[1]) for o in outs]
    out_shape += [_sds((nblk, r, w), F32) for r, w in sums]
    out_specs = [pl.BlockSpec((tr, o[0]), lat_rows if o[2:] else all_rows) for o in outs]
    out_specs += [pl.BlockSpec((1, r, w), lambda i: (i, 0, 0)) for r, w in sums]
    sem = "arbitrary" if any(o[2:] for o in outs) else "parallel"
    if rider is None:
        return pl.pallas_call(body, name=name, out_shape=out_shape, grid=(nblk,), in_specs=in_specs,
                              out_specs=out_specs, compiler_params=_cp((sem,)))(*rows, *segs, *vecs)
    return _host_call(body, rider, name, (nblk,), in_specs, out_specs, out_shape, (*rows, *segs, *vecs), (sem,),
                      n_r + n_s + n_v, n_o + len(sums))


def modulate(lay, h, mod, k_shift, k_scale, name):
    def fn(hb, m):
        return (hb * (1.0 + m[k_scale:k_scale + 1]) + m[k_shift:k_shift + 1],)
    return rowwise(lay, name, fn, [h], segs=[mod], outs=[(D, BF16)])[0]


def resid_ln(lay, h, y, mod, k_gate, coef, lnv, name, nxt=None, rider=None):
    def fn(hb, yb, m, *rest):
        ln = rest[-1]
        z = ALPHA * hb + (coef * m[k_gate:k_gate + 1]) * yb
        mu = jnp.mean(z, axis=-1, keepdims=True)
        zc = z - mu
        var = jnp.mean(zc * zc, axis=-1, keepdims=True)
        rstd = lax.rsqrt(var + LN_EPS)
        xhat = zc * rstd
        out = xhat * ln[0:1] + ln[1:2]
        if nxt is None:
            return out, xhat, rstd
        mn = rest[0]
        return out, xhat, rstd, out * (1.0 + mn[nxt[2]:nxt[2] + 1]) + mn[nxt[1]:nxt[1] + 1]
    segs = [mod] if nxt is None else [mod, nxt[0]]
    outs = [(D, F32), (D, F32), (1, F32)] + ([] if nxt is None else [(D, BF16)])
    return rowwise(lay, name, fn, [h, y], segs=segs, vecs=[lnv], outs=outs, rider=rider)


def _ln_bwd_math(do, xh, rs, yb, gate, coef, ln):
    dxh = do * ln[0:1]
    m1 = jnp.mean(dxh, axis=-1, keepdims=True)
    m2 = jnp.mean(dxh * xh, axis=-1, keepdims=True)
    dz = rs * (dxh - m1 - xh * m2)
    s = jnp.concatenate([jnp.sum(do, axis=0, keepdims=True), jnp.sum(do * xh, axis=0, keepdims=True),
                         jnp.sum(coef * dz * yb, axis=0, keepdims=True)], axis=0)
    return (coef * gate) * dz, ALPHA * dz, s


def _mod_bwd_math(dr, dm, hb, scale):
    s = jnp.concatenate([jnp.sum(dm, axis=0, keepdims=True), jnp.sum(dm * hb, axis=0, keepdims=True)], axis=0)
    return dr + dm * (1.0 + scale), s


def ln_bwd(lay, dout, xhat, rstd, y, mod, k_gate, coef, lnv, name):
    def fn(do, xh, rs, yb, m, ln):
        return _ln_bwd_math(do, xh, rs, yb, m[k_gate:k_gate + 1], coef, ln)
    return rowwise(lay, name, fn, [dout, xhat, rstd, y], segs=[mod], vecs=[lnv],
                   outs=[(D, BF16), (D, F32)], sums=[(3, D)])


def mod_bwd(lay, dres, dhm, h, mod, k_scale, name):
    def fn(dr, dm, hb, m):
        return _mod_bwd_math(dr, dm, hb, m[k_scale:k_scale + 1])
    return rowwise(lay, name, fn, [dres, dhm, h], segs=[mod], outs=[(D, F32, "lat")], sums=[(2, D)])


def modb_lnb(lay, dres, dhm, h, mod, k_scale, xhat, rstd, y, mod_p, k_gate, coef, lnv, name):
    def fn(dr, dm, hb, xh, rs, yb, m, mp, ln):
        dh, s2 = _mod_bwd_math(dr, dm, hb, m[k_scale:k_scale + 1])
        dy, dres_p, s1 = _ln_bwd_math(dh, xh, rs, yb, mp[k_gate:k_gate + 1], coef, ln)
        return dy, dres_p, s1, s2
    return rowwise(lay, name, fn, [dres, dhm, h, xhat, rstd, y], segs=[mod, mod_p], vecs=[lnv],
                   outs=[(D, BF16), (D, F32)], sums=[(3, D), (2, D)])


def block_sums(lay, parts, name):
    nblk, r, w = parts.shape

    def body(p_ref, o_ref):
        acc = [None, None, None]
        for i in range(nblk):
            sg = 2 if i % lay.bps < lay.cb else i // lay.bps
            acc[sg] = p_ref[i] if acc[sg] is None else acc[sg] + p_ref[i]
        for k in range(3):
            o_ref[k] = acc[k]
        o_ref[3] = (acc[0] + acc[1]) + acc[2]

    return pl.pallas_call(body, name=name, out_shape=_sds((4, r, w), F32), in_specs=[VMEM_SPEC],
                          out_specs=VMEM_SPEC)(parts)


def mm_nn(a, b, name, out_dtype=F32, bias=None):
    m, k = a.shape
    n = b.shape[1]
    tm = _pick(m, (1152, 768, 512, 256, 128, 64, 32, 16, 8))
    tn = _pick(n, (1024, 768, 640, 512, 384, 256, 128))

    def body(*refs):
        if bias is None:
            a_ref, b_ref, o_ref = refs
            o_ref[...] = _nn(a_ref[...].astype(BF16), b_ref[...].astype(BF16)).astype(o_ref.dtype)
        else:
            a_ref, b_ref, c_ref, o_ref = refs
            o_ref[...] = (_nn(a_ref[...].astype(BF16), b_ref[...].astype(BF16)) + c_ref[...]).astype(o_ref.dtype)

    in_specs = [pl.BlockSpec((tm, k), lambda i, j: (i, 0)), pl.BlockSpec((k, tn), lambda i, j: (0, j))]
    ops = [a, b]
    if bias is not None:
        in_specs.append(pl.BlockSpec((1, tn), lambda i, j: (0, j)))
        ops.append(bias)
    return pl.pallas_call(body, name=name, out_shape=_sds((m, n), out_dtype), grid=(m // tm, n // tn),
                          in_specs=in_specs, out_specs=pl.BlockSpec((tm, tn), lambda i, j: (i, j)),
                          compiler_params=_cp(("parallel", "parallel"), VMEM_BIG))(*ops)


def mm_nt(a, b, name, out_dtype=F32):
    m, k = a.shape
    n = b.shape[0]
    tm = _pick(m, (1152, 768, 512, 256, 128, 64, 32, 16, 8))
    tn = _pick(n, (1024, 768, 640, 512, 384, 256, 128))

    def body(a_ref, b_ref, o_ref):
        o_ref[...] = _nt(a_ref[...].astype(BF16), b_ref[...].astype(BF16)).astype(o_ref.dtype)

    return pl.pallas_call(body, name=name, out_shape=_sds((m, n), out_dtype), grid=(m // tm, n // tn),
                          in_specs=[pl.BlockSpec((tm, k), lambda i, j: (i, 0)), pl.BlockSpec((tn, k), lambda i, j: (j, 0))],
                          out_specs=pl.BlockSpec((tm, tn), lambda i, j: (i, j)),
                          compiler_params=_cp(("parallel", "parallel"), VMEM_BIG))(a, b)


def mm_tn(a, b, name):
    t, m = a.shape
    n = b.shape[1]
    tk = _pick(t, (1152, 768, 512, 256, 128, 64, 32, 16))
    tm = _pick(m, (512, 384, 256, 128))

    def body(a_ref, b_ref, o_ref):
        @pl.when(pl.program_id(1) == 0)
        def _():
            o_ref[...] = jnp.zeros_like(o_ref)
        o_ref[...] += _tn(a_ref[...].astype(BF16), b_ref[...].astype(BF16))

    return pl.pallas_call(body, name=name, out_shape=_sds((m, n), F32), grid=(m // tm, t // tk),
                          in_specs=[pl.BlockSpec((tk, tm), lambda i, k: (k, i)), pl.BlockSpec((tk, n), lambda i, k: (k, 0))],
                          out_specs=pl.BlockSpec((tm, n), lambda i, k: (i, 0)),
                          compiler_params=_cp(("parallel", "arbitrary"), VMEM_BIG))(a, b)


class Rider:
    def __init__(self, ins, outs, aliases, nsem, start, wait):
        self.ins, self.outs, self.aliases, self.nsem, self.start, self.wait = ins, outs, aliases, nsem, start, wait


def _chips_of(mx, my):
    return [(1 - mx, my), (mx, 1 - my), (1 - mx, 1 - my)]


def rider_gather_d2d(buf):
    def start(ins, outs, ssem, rsem):
        o = outs[0]
        mx, my, mc = _me()
        for j, (px, py) in enumerate(_chips_of(mx, my)):
            ps = 2 * px + py
            _rcopy(o.at[ps, mc], o.at[ps, mc], ssem.at[j], rsem.at[j], (mx, my, 1 - mc)).start()

    def wait(ins, outs, ssem, rsem):
        o = outs[0]
        mx, my, mc = _me()
        sib = (mx, my, 1 - mc)
        for j, (px, py) in enumerate(_chips_of(mx, my)):
            ps = 2 * px + py
            _rcopy(o.at[ps, 1 - mc], o.at[ps, 1 - mc], ssem.at[j], rsem.at[j], sib).wait_recv()
        for j, (px, py) in enumerate(_chips_of(mx, my)):
            ps = 2 * px + py
            _rcopy(o.at[ps, mc], o.at[ps, mc], ssem.at[j], rsem.at[j], sib).wait_send()

    return Rider([buf], [_sds(buf.shape, buf.dtype)], {0: 0}, 3, start, wait)


def rider_reduce_sib(buf):
    n, _, h, w = buf.shape

    def start(ins, outs, ssem, rsem):
        mx, my, mc = _me()
        for k in range(N_CHIP):
            _rcopy(ins[0].at[k, 1 - mc], outs[0].at[k], ssem.at[k], rsem.at[k], (mx, my, 1 - mc)).start()

    def wait(ins, outs, ssem, rsem):
        mx, my, mc = _me()
        for k in range(N_CHIP):
            _rcopy(ins[0].at[k, 1 - mc], outs[0].at[k], ssem.at[k], rsem.at[k], (mx, my, 1 - mc)).wait_recv()
        for k in range(N_CHIP):
            _rcopy(ins[0].at[k, 1 - mc], outs[0].at[k], ssem.at[k], rsem.at[k], (mx, my, 1 - mc)).wait_send()

    return Rider([buf], [_sds((n, h, w), buf.dtype)], {}, N_CHIP, start, wait)


def rider_gather_xy(buf):
    def peers():
        mx, my, mc = _me()
        return 2 * mx + my, mc, [(1 - mx, my), (mx, 1 - my)]

    def start(ins, outs, ssem, rsem):
        o = outs[0]
        s, mc, nb = peers()
        for j, (px, py) in enumerate(nb):
            _rcopy(o.at[s, mc], o.at[s, mc], ssem.at[j], rsem.at[j], (px, py, mc)).start()

    def wait(ins, outs, ssem, rsem):
        o = outs[0]
        s, mc, nb = peers()
        for j, (px, py) in enumerate(nb):
            _rcopy(o.at[2 * px + py, mc], o.at[2 * px + py, mc], ssem.at[j], rsem.at[j], (px, py, mc)).wait_recv()
        for j, (px, py) in enumerate(nb):
            _rcopy(o.at[s, mc], o.at[s, mc], ssem.at[j], rsem.at[j], (px, py, mc)).wait_send()

    return Rider([buf], [_sds(buf.shape, buf.dtype)], {0: 0}, 2, start, wait)


def rider_gather_fwd(buf):
    def start(ins, outs, ssem, rsem):
        o = outs[0]
        mx, my, mc = _me()
        xs = 2 * (1 - mx) + my
        _rcopy(o.at[xs, mc], o.at[xs, mc], ssem.at[0], rsem.at[0], (mx, 1 - my, mc)).start()

    def wait(ins, outs, ssem, rsem):
        o = outs[0]
        mx, my, mc = _me()
        xs, ds = 2 * (1 - mx) + my, 2 * (1 - mx) + (1 - my)
        _rcopy(o.at[ds, mc], o.at[ds, mc], ssem.at[0], rsem.at[0], (mx, 1 - my, mc)).wait_recv()
        _rcopy(o.at[xs, mc], o.at[xs, mc], ssem.at[0], rsem.at[0], (mx, 1 - my, mc)).wait_send()

    return Rider([buf], [_sds(buf.shape, buf.dtype)], {0: 0}, 1, start, wait)


def rider_reduce_copy(q, j, r=None):
    def peer():
        mx, my, mc = _me()
        px, py = _chips_of(mx, my)[j]
        return 2 * mx + my, 2 * px + py, (px, py, mc)

    def start(ins, outs, ssem, rsem):
        s, ps, dev = peer()
        _rcopy(ins[0].at[ps], outs[0].at[s], ssem.at[0], rsem.at[0], dev).start()

    def wait(ins, outs, ssem, rsem):
        s, ps, dev = peer()
        _rcopy(ins[0].at[ps], outs[0].at[ps], ssem.at[0], rsem.at[0], dev).wait_recv()
        _rcopy(ins[0].at[ps], outs[0].at[s], ssem.at[0], rsem.at[0], dev).wait_send()

    if r is None:
        return Rider([q], [_sds(q.shape, q.dtype)], {}, 1, start, wait)
    return Rider([q, r], [_sds(q.shape, q.dtype)], {1: 0}, 1, start, wait)


def rider_join(buf, g):
    def start(ins, outs, ssem, rsem):
        o = outs[0].at[g]
        mx, my, mc = _me()
        _rcopy(o.at[mc], o.at[mc], ssem.at[0], rsem.at[0], (mx, my, 1 - mc)).start()

    def wait(ins, outs, ssem, rsem):
        o = outs[0].at[g]
        mx, my, mc = _me()
        _rcopy(o.at[1 - mc], o.at[1 - mc], ssem.at[0], rsem.at[0], (mx, my, 1 - mc)).wait_recv()
        _rcopy(o.at[mc], o.at[mc], ssem.at[0], rsem.at[0], (mx, my, 1 - mc)).wait_send()

    return Rider([buf], [_sds(buf.shape, buf.dtype)], {0: 0}, 1, start, wait)


def _host_call(body, rider, name, grid, in_specs, out_specs, out_shape, operands, sem, n_in, n_out, aliases=None):
    aliases = dict(aliases or {})
    if rider is None:
        return pl.pallas_call(body, name=name, out_shape=out_shape, grid=grid, in_specs=in_specs, out_specs=out_specs,
                              input_output_aliases=aliases, compiler_params=_cp(sem, VMEM_BIG))(*operands)
    n_ri, n_ro = len(rider.ins), len(rider.outs)
    aliases.update({n_in + a: n_out + b for a, b in rider.aliases.items()})

    def hosted(*refs):
        ins, r_in = refs[:n_in], refs[n_in:n_in + n_ri]
        outs, r_out = refs[n_in + n_ri:n_in + n_ri + n_out], refs[n_in + n_ri + n_out:n_in + n_ri + n_out + n_ro]
        ssem, rsem = refs[-2], refs[-1]
        first = functools.reduce(lambda a, b: a & b, [pl.program_id(k) == 0 for k in range(len(grid))])
        last = functools.reduce(lambda a, b: a & b, [pl.program_id(k) == grid[k] - 1 for k in range(len(grid))])

        @pl.when(first)
        def _():
            rider.start(r_in, r_out, ssem, rsem)
        body(*ins, *outs)

        @pl.when(last)
        def _():
            rider.wait(r_in, r_out, ssem, rsem)

    return pl.pallas_call(
        hosted, name=name, out_shape=list(out_shape) + list(rider.outs), grid=grid,
        in_specs=list(in_specs) + [ANY] * n_ri, out_specs=list(out_specs) + [ANY] * n_ro,
        input_output_aliases=aliases,
        scratch_shapes=[pltpu.SemaphoreType.DMA((rider.nsem,)), pltpu.SemaphoreType.DMA((rider.nsem,))],
        compiler_params=_cp(("arbitrary",) * len(grid), VMEM_BIG))(*operands, *rider.ins)


def ffn_up(lay, hm, wbuf, ig, iu, ns, name, rider=None):
    tm = lay.tm

    def body(h_ref, wg_ref, wu_ref, sp_ref, sl_ref, u_ref, a_ref):
        hb = h_ref[...]
        g = _nt(hb, wg_ref[0])
        u = _nt(hb, wu_ref[0])
        sg = _sigmoid(g)
        sl = g * sg
        sp_ref[0] = (sg + sl * (1.0 - sg)).astype(BF16)
        sl_ref[0] = sl.astype(BF16)
        u_ref[0] = u.astype(BF16)
        a_ref[0] = (sl * u).astype(BF16)

    spec_o = pl.BlockSpec((1, tm, ns), lambda s, i: (s, i, 0))
    return _host_call(
        body, rider, name, (N_CHIP, lay.T // tm),
        [pl.BlockSpec((tm, D), lambda s, i: (i, 0)), pl.BlockSpec((1, ns, D), lambda s, i: (s, ig, 0)),
         pl.BlockSpec((1, ns, D), lambda s, i: (s, iu, 0))],
        [spec_o] * 4, [_sds((N_CHIP, lay.T, ns), BF16)] * 4, (hm, wbuf, wbuf), ("parallel", "parallel"), 3, 4)


def slab_nn_acc(lay, zs, wbuf, idxs, ns, name, rider=None):
    tm = lay.tm
    npair = len(zs)

    def body(*refs):
        o_ref = refs[-1]

        @pl.when(pl.program_id(1) == 0)
        def _():
            o_ref[...] = jnp.zeros_like(o_ref)
        acc = _nn(refs[0][0], refs[npair][0])
        for p in range(1, npair):
            acc += _nn(refs[p][0], refs[npair + p][0])
        o_ref[...] += acc

    in_specs = [pl.BlockSpec((1, tm, ns), lambda i, s: (s, i, 0)) for _ in zs]
    in_specs += [pl.BlockSpec((1, ns, D), functools.partial(lambda i, s, q: (s, q, 0), q=q)) for q in idxs]
    return _host_call(body, rider, name, (lay.T // tm, N_CHIP), in_specs, [pl.BlockSpec((tm, D), lambda i, s: (i, 0))],
                      [_sds((lay.T, D), F32)], (*zs, *([wbuf] * npair)), ("parallel", "arbitrary"), 2 * npair, 1)


def ffn_bwd_da(lay, dy, wbuf, idn, sp, sl, u, ns, name, rider=None):
    tm = lay.tm

    def body(dy_ref, wd_ref, sp_ref, sl_ref, u_ref, dg_ref, du_ref):
        da = _nt(dy_ref[...], wd_ref[0])
        dg_ref[0] = (da * u_ref[0].astype(F32) * sp_ref[0].astype(F32)).astype(BF16)
        du_ref[0] = (da * sl_ref[0].astype(F32)).astype(BF16)

    spec_z = pl.BlockSpec((1, tm, ns), lambda s, i: (s, i, 0))
    return _host_call(
        body, rider, name, (N_CHIP, lay.T // tm),
        [pl.BlockSpec((tm, D), lambda s, i: (i, 0)), pl.BlockSpec((1, ns, D), lambda s, i: (s, idn, 0)),
         spec_z, spec_z, spec_z],
        [spec_z] * 2, [_sds((N_CHIP, lay.T, ns), BF16)] * 2, (dy, wbuf, sp, sl, u), ("parallel", "parallel"), 5, 2)


def slab_tn(lay, z, x, gbuf, idx, ns, name, rider=None):
    tk = lay.tm

    def body(z_ref, x_ref, g_in, o_ref):
        del g_in

        @pl.when(pl.program_id(1) == 0)
        def _():
            o_ref[...] = jnp.zeros_like(o_ref)
        o_ref[0] += _tn(z_ref[0], x_ref[...])

    return _host_call(
        body, rider, name, (N_CHIP, lay.T // tk),
        [pl.BlockSpec((1, tk, ns), lambda s, k: (s, k, 0)), pl.BlockSpec((tk, D), lambda s, k: (k, 0)), ANY],
        [pl.BlockSpec((1, ns, D), lambda s, k: (s, idx, 0))], [_sds(gbuf.shape, F32)], (z, x, gbuf),
        ("parallel", "arbitrary"), 3, 1, aliases={2: 0})


Q0, K0, V0, U0, QR0, KR0, PEXT = 0, 512, 640, 768, 1280, 1792, 1920


def rope_fwd(lay, p, cos, sin, name):
    def fn(pb, cs, sn):
        cs4 = jnp.concatenate([cs] * 4, axis=1)
        sn4 = jnp.concatenate([sn] * 4, axis=1)
        qr = pb[:, Q0:K0] * cs4 + pb[:, QR0:KR0] * sn4
        kr = pb[:, K0:V0] * cs + pb[:, KR0:PEXT] * sn
        return qr, kr, pb[:, V0:U0], pb[:, U0:QR0]
    return rowwise(lay, name, fn, [p, cos, sin], outs=[(ATT_W, BF16), (KV_W, BF16), (KV_W, BF16), (POOL_W, F32)])


def rope_bwd(lay, dqr, dkr, dv, du, cos, sin, name):
    def fn(dq, dk, dvb, dub, cs, sn):
        cs4 = jnp.concatenate([cs] * 4, axis=1)
        sn4 = jnp.concatenate([sn] * 4, axis=1)
        return (jnp.concatenate([dq * cs4, dk * cs, dvb, dub, dq * sn4, dk * sn], axis=1),)
    return rowwise(lay, name, fn, [dqr, dkr, dv, du, cos, sin], outs=[(PEXT, BF16)])[0]


def _attn_specs(lay):
    nbs, cbk, lbk = lay.PS // BLK, lay.C // BLK, lay.L // BLK

    def kv_map(j):
        return lambda s, n: (s * nbs + cbk + jnp.clip(n - cbk + j - 1, 0, lbk - 1), 0)

    win = [pl.BlockSpec((BLK, KV_W), kv_map(j)) for j in range(3)]
    ctx = pl.BlockSpec((lay.C, KV_W), lambda s, n: (s * (lay.PS // lay.C), 0))
    return nbs, cbk, lbk, win, ctx


def _attn_masks(n, cbk, lbk):
    row = lax.broadcasted_iota(jnp.int32, (BLK, BLK), 0)
    col = lax.broadcasted_iota(jnp.int32, (BLK, BLK), 1)
    m = n - cbk
    lat = n >= cbk
    valid = [lat & (m >= 1) & (col >= row), lat & (col >= 0), lat & (m <= lbk - 2) & (col <= row)]
    lane_lo = lax.broadcasted_iota(jnp.int32, (BLK, 2 * HEAD_DIM), 1) < HEAD_DIM
    return valid, lane_lo


def attn_fwd(lay, qr, kr, vb, sink_tab, name):
    nbs, cbk, lbk, win, ctx = _attn_specs(lay)

    def body(q_ref, k0, k1, k2, kc_ref, v0, v1, v2, vc_ref, sk_ref, o_ref, l_ref):
        n = pl.program_id(1)
        valid, lane_lo = _attn_masks(n, cbk, lbk)
        valid4 = [jnp.concatenate([v] * 4, axis=0) for v in valid]
        ks = [k0[...], k1[...], k2[...]]
        vs = [v0[...], v1[...], v2[...]]
        kc, vc = kc_ref[...], vc_ref[...]
        q2s = [q_ref[:, p * 128:(p + 1) * 128] for p in range(4)]
        outs, lses = [], []
        for hh in range(2):
            sel = lane_lo == (hh == 0)
            qm = jnp.concatenate([jnp.where(sel, q2, jnp.zeros_like(q2)) for q2 in q2s], axis=0)
            sk = jnp.concatenate([jnp.broadcast_to(sk_ref[p:p + 1, hh * HEAD_DIM:hh * HEAD_DIM + 1], (BLK, 1))
                                  for p in range(4)], axis=0)
            sw = [jnp.where(valid4[j], _nt(qm, ks[j]) * ATT_SCALE, NEG_INF) for j in range(3)]
            sc = _nt(qm, kc) * ATT_SCALE
            mx = jnp.maximum(jnp.maximum(jnp.maximum(sw[0].max(-1, keepdims=True), sw[1].max(-1, keepdims=True)),
                                         jnp.maximum(sw[2].max(-1, keepdims=True), sc.max(-1, keepdims=True))), sk)
            ew = [jnp.exp(s - mx) for s in sw]
            ec = jnp.exp(sc - mx)
            den = ew[0].sum(-1, keepdims=True) + ew[1].sum(-1, keepdims=True) + ew[2].sum(-1, keepdims=True)
            den = den + ec.sum(-1, keepdims=True) + jnp.exp(sk - mx)
            o = _nn((ec / den).astype(BF16), vc)
            for j in range(3):
                o += _nn((ew[j] / den).astype(BF16), vs[j])
            outs.append(o)
            lses.append(mx + jnp.log(den))
        for p in range(4):
            rows = slice(p * BLK, (p + 1) * BLK)
            o_ref[:, p * 128:(p + 1) * 128] = jnp.where(lane_lo, outs[0][rows], outs[1][rows]).astype(o_ref.dtype)
            l_ref[:, p * 128:(p + 1) * 128] = jnp.where(lane_lo, jnp.broadcast_to(lses[0][rows], (BLK, 128)),
                                                        jnp.broadcast_to(lses[1][rows], (BLK, 128)))

    qspec = pl.BlockSpec((BLK, ATT_W), lambda s, n: (s * nbs + n, 0))
    return pl.pallas_call(
        body, name=name, out_shape=[_sds((lay.T, ATT_W), BF16), _sds((lay.T, ATT_W), F32)], grid=(2, nbs),
        in_specs=[qspec] + win + [ctx] + win + [ctx] + [pl.BlockSpec((8, 128), lambda s, n: (0, 0))],
        out_specs=[qspec, qspec], compiler_params=_cp(("parallel", "parallel")))(qr, kr, kr, kr, kr, vb, vb, vb, vb, sink_tab)


def attn_bwd(lay, qr, kr, vb, sink_tab, lse, datt, name):
    nbs, cbk, lbk, win, ctx = _attn_specs(lay)
    C, PS = lay.C, lay.PS

    def body(q_ref, k0, k1, k2, kc_ref, v0, v1, v2, vc_ref, sk_ref, l_ref, do_ref, dq_ref, dk_ref, dv_ref, ds_ref):
        n = pl.program_id(1)
        valid, lane_lo = _attn_masks(n, cbk, lbk)

        @pl.when(n == 0)
        def _():
            dk_ref[...] = jnp.zeros_like(dk_ref)
            dv_ref[...] = jnp.zeros_like(dv_ref)
            ds_ref[...] = jnp.zeros_like(ds_ref)

        ks = [k0[...], k1[...], k2[...], kc_ref[...]]
        vs = [v0[...], v1[...], v2[...], vc_ref[...]]
        valid4 = [jnp.concatenate([v] * 4, axis=0) for v in valid]
        dks = [jnp.zeros((BLK, KV_W), F32)] * 3 + [jnp.zeros((C, KV_W), F32)]
        dvs = list(dks)
        q2s = [q_ref[:, p * 128:(p + 1) * 128] for p in range(4)]
        do2s = [do_ref[:, p * 128:(p + 1) * 128].astype(BF16) for p in range(4)]
        lse2s = [l_ref[:, p * 128:(p + 1) * 128] for p in range(4)]
        dq_h, dd_h = [], []
        for hh in range(2):
            sel = lane_lo == (hh == 0)
            qm = jnp.concatenate([jnp.where(sel, q2, jnp.zeros_like(q2)) for q2 in q2s], axis=0)
            dom = jnp.concatenate([jnp.where(sel, d2, jnp.zeros_like(d2)) for d2 in do2s], axis=0)
            lse_h = jnp.concatenate([l2[:, hh * HEAD_DIM:hh * HEAD_DIM + 1] for l2 in lse2s], axis=0)
            ps, dps = [], []
            for j in range(4):
                s = _nt(qm, ks[j]) * ATT_SCALE
                if j < 3:
                    s = jnp.where(valid4[j], s, NEG_INF)
                ps.append(jnp.exp(s - lse_h))
                dps.append(_nt(dom, vs[j]))
            dd = (ps[0] * dps[0]).sum(-1, keepdims=True) + (ps[1] * dps[1]).sum(-1, keepdims=True)
            dd = dd + (ps[2] * dps[2]).sum(-1, keepdims=True) + (ps[3] * dps[3]).sum(-1, keepdims=True)
            dq = jnp.zeros((4 * BLK, 128), F32)
            for j in range(4):
                dsb = (ps[j] * (dps[j] - dd) * ATT_SCALE).astype(BF16)
                dq += _nn(dsb, ks[j])
                dks[j] = dks[j] + _tn(dsb, qm)
                dvs[j] = dvs[j] + _tn(ps[j].astype(BF16), dom)
            dq_h.append(dq)
            dd_h.append(dd)
        for p in range(4):
            sl = slice(p * 128, (p + 1) * 128)
            rows = slice(p * BLK, (p + 1) * BLK)
            dq_ref[:, sl] = jnp.where(lane_lo, dq_h[0][rows], dq_h[1][rows])
            dd2 = jnp.where(lane_lo, jnp.broadcast_to(dd_h[0][rows], (BLK, 128)), jnp.broadcast_to(dd_h[1][rows], (BLK, 128)))
            psink = jnp.exp(sk_ref[p:p + 1, :] - lse2s[p])
            ds_ref[0, p:p + 1, :] += -jnp.sum(psink * dd2, axis=0, keepdims=True)
        dk_ref[0:C, :] += dks[3]
        dv_ref[0:C, :] += dvs[3]
        for j in range(3):
            r0 = pl.multiple_of((cbk + jnp.clip(n - cbk + j - 1, 0, lbk - 1)) * BLK, BLK)
            dk_ref[pl.ds(r0, BLK), :] += dks[j]
            dv_ref[pl.ds(r0, BLK), :] += dvs[j]

    qspec = pl.BlockSpec((BLK, ATT_W), lambda s, n: (s * nbs + n, 0))
    kvout = pl.BlockSpec((PS, KV_W), lambda s, n: (s, 0))
    return pl.pallas_call(
        body, name=name,
        out_shape=[_sds((lay.T, ATT_W), F32), _sds((lay.T, KV_W), F32), _sds((lay.T, KV_W), F32), _sds((2, 8, 128), F32)],
        grid=(2, nbs),
        in_specs=[qspec] + win + [ctx] + win + [ctx] + [pl.BlockSpec((8, 128), lambda s, n: (0, 0)), qspec, qspec],
        out_specs=[qspec, kvout, kvout, pl.BlockSpec((1, 8, 128), lambda s, n: (s, 0, 0))],
        compiler_params=_cp(("parallel", "arbitrary")))(qr, kr, kr, kr, kr, vb, vb, vb, vb, sink_tab, lse, datt)


def _winsum(x, r):
    n = x.shape[0]
    t = lax.broadcasted_iota(jnp.int32, x.shape, 0)
    acc = x
    for o in range(1, r + 1):
        acc = acc + jnp.where(t >= o, pltpu.roll(x, o, 0), 0.0) + jnp.where(t < n - o, pltpu.roll(x, n - o, 0), 0.0)
    return acc


def _wincount(n, r):
    t = lax.broadcasted_iota(jnp.int32, (n, 128), 0)
    return (jnp.minimum(t + r, n - 1) - jnp.maximum(t - r, 0) + 1).astype(F32)


def pool_fwd(lay, u, w_pool, scale, name):
    segs = [(0, lay.C), (lay.C, lay.L)]

    def body(u_ref, w_ref, s_ref, o_ref):
        for r0, n in segs:
            for g, wd in enumerate(POOL_WINDOWS):
                sl = slice(g * 128, (g + 1) * 128)
                x = u_ref[r0:r0 + n, sl]
                d = _winsum(x, wd // 2) / _wincount(n, wd // 2) - x
                y = _nn(d.astype(BF16), w_ref[g].astype(BF16)) * s_ref[:, sl]
                o_ref[r0:r0 + n, sl] = y.astype(o_ref.dtype)

    spec = pl.BlockSpec((lay.PS, POOL_W), lambda s: (s, 0))
    return pl.pallas_call(
        body, name=name, out_shape=_sds((lay.T, POOL_W), BF16), grid=(2,),
        in_specs=[spec, pl.BlockSpec(w_pool.shape, lambda s: (0, 0, 0)), pl.BlockSpec((1, POOL_W), lambda s: (0, 0))],
        out_specs=spec, compiler_params=_cp(("parallel",), VMEM_BIG))(u, w_pool, scale)


def pool_bwd(lay, u, dcat, w_pool, scale, name):
    segs = [(0, lay.C), (lay.C, lay.L)]

    def body(u_ref, dp_ref, w_ref, s_ref, du_ref, dw_ref, dsc_ref):
        for g, wd in enumerate(POOL_WINDOWS):
            sl = slice(g * 128, (g + 1) * 128)
            wb = w_ref[g].astype(BF16)
            dw = jnp.zeros((128, 128), F32)
            dsc = jnp.zeros((1, 128), F32)
            for r0, n in segs:
                x = u_ref[r0:r0 + n, sl]
                cnt = _wincount(n, wd // 2)
                d = (_winsum(x, wd // 2) / cnt - x).astype(BF16)
                dp = dp_ref[r0:r0 + n, sl]
                dsc += jnp.sum(_nn(d, wb) * dp, axis=0, keepdims=True)
                dyp = (dp * s_ref[:, sl]).astype(BF16)
                dw += _tn(d, dyp)
                dd = _nt(dyp, wb)
                du_ref[r0:r0 + n, sl] = _winsum(dd / cnt, wd // 2) - dd
            dw_ref[0, g] = dw
            dsc_ref[0, :, sl] = dsc

    spec = pl.BlockSpec((lay.PS, POOL_W), lambda s: (s, 0))
    return pl.pallas_call(
        body, name=name,
        out_shape=[_sds((lay.T, POOL_W), F32), _sds((2, 4, 128, 128), F32), _sds((2, 1, POOL_W), F32)], grid=(2,),
        in_specs=[spec, pl.BlockSpec((lay.PS, POOL_W), lambda s: (s, 1)), pl.BlockSpec(w_pool.shape, lambda s: (0, 0, 0)),
                  pl.BlockSpec((1, POOL_W), lambda s: (0, 0))],
        out_specs=[spec, pl.BlockSpec((1, 4, 128, 128), lambda s: (s, 0, 0, 0)), pl.BlockSpec((1, 1, POOL_W), lambda s: (s, 0, 0))],
        compiler_params=_cp(("parallel",), VMEM_BIG))(u, dcat, w_pool, scale)


CONV_OFFS = (-1, 0, 1, 2)
CW = 256


def _shift_rows(x, o):
    if o == 0:
        return x
    n = x.shape[0]
    t = lax.broadcasted_iota(jnp.int32, x.shape, 0)
    if o < 0:
        return jnp.where(t >= -o, pltpu.roll(x, -o, 0), 0.0)
    return jnp.where(t < n - o, pltpu.roll(x, n - o, 0), 0.0)


def conv_fwd(lay, p, col0, w, b, name):
    segs = [(0, lay.C), (lay.C, lay.L)]
    cb0 = col0 // CW

    def body(x_ref, w_ref, b_ref, o_ref):
        for r0, n in segs:
            x = x_ref[r0:r0 + n, :]
            y = jnp.broadcast_to(b_ref[...], x.shape)
            for k, o in enumerate(CONV_OFFS):
                y = y + _shift_rows(x, o) * w_ref[k:k + 1, :]
            o_ref[r0:r0 + n, :] = y

    return pl.pallas_call(
        body, name=name, out_shape=_sds((lay.T, D), F32), grid=(2, D // CW),
        in_specs=[pl.BlockSpec((lay.PS, CW), lambda s, j: (s, cb0 + j)), pl.BlockSpec((4, CW), lambda s, j: (0, j)),
                  pl.BlockSpec((1, CW), lambda s, j: (0, j))],
        out_specs=pl.BlockSpec((lay.PS, CW), lambda s, j: (s, j)),
        compiler_params=_cp(("parallel", "parallel")))(p, w, b)


def conv_bwd(lay, p, col0, w, duc, name):
    segs = [(0, lay.C), (lay.C, lay.L)]
    cb0 = col0 // CW

    def body(x_ref, w_ref, g_ref, du_ref, dw_ref, db_ref):
        dws = [jnp.zeros((1, CW), F32)] * 4
        db = jnp.zeros((1, CW), F32)
        for r0, n in segs:
            x = x_ref[r0:r0 + n, :]
            g = g_ref[r0:r0 + n, :]
            du = jnp.zeros_like(g)
            for k, o in enumerate(CONV_OFFS):
                du = du + _shift_rows(g, -o) * w_ref[k:k + 1, :]
                dws[k] = dws[k] + jnp.sum(g * _shift_rows(x, o), axis=0, keepdims=True)
            db = db + jnp.sum(g, axis=0, keepdims=True)
            du_ref[r0:r0 + n, :] = du.astype(du_ref.dtype)
        dw_ref[0] = jnp.concatenate(dws, axis=0)
        db_ref[0] = db

    return pl.pallas_call(
        body, name=name, out_shape=[_sds((lay.T, D), BF16), _sds((2, 4, D), F32), _sds((2, 1, D), F32)], grid=(2, D // CW),
        in_specs=[pl.BlockSpec((lay.PS, CW), lambda s, j: (s, cb0 + j)), pl.BlockSpec((4, CW), lambda s, j: (0, j)),
                  pl.BlockSpec((lay.PS, CW), lambda s, j: (s, j))],
        out_specs=[pl.BlockSpec((lay.PS, CW), lambda s, j: (s, j)), pl.BlockSpec((1, 4, CW), lambda s, j: (s, 0, j)),
                   pl.BlockSpec((1, 1, CW), lambda s, j: (s, 0, j))],
        compiler_params=_cp(("parallel", "parallel")))(p, w, duc)


def _softplus_neg(lam):
    z = -lam
    w = jnp.exp(-jnp.abs(z))
    log1p = jnp.where(w < 1e-2, w * (1.0 - w * (0.5 - w / 3.0)), jnp.log(1.0 + w))
    return jnp.maximum(z, 0.0) + log1p, -_sigmoid(z)


def _neg_expm1(x):
    series = -x * (1.0 + x * (0.5 + x * (1.0 / 6.0 + x * (1.0 / 24.0 + x * (1.0 / 120.0)))))
    return jnp.where(x > -0.05, series, 1.0 - jnp.exp(x))


def _lru_gates(x, xb, wa, wx, ba, bx, lam):
    r = _sigmoid(_nn(xb, wa.astype(BF16)) + ba)
    gi = _sigmoid(_nn(xb, wx.astype(BF16)) + bx)
    sp, dsp = _softplus_neg(lam)
    la = -LRU_C * r * sp
    a = jnp.exp(la)
    sq = jnp.sqrt(_neg_expm1(2.0 * la))
    return r, gi, sp, dsp, a, sq


def lru_coeffs(lay, uc, wa, wx, vec, name):
    tr = lay.tc

    def body(x_ref, wa_ref, wx_ref, v_ref, a_ref, b_ref):
        for h in range(8):
            sl = slice(h * 128, (h + 1) * 128)
            x = x_ref[:, sl]
            xb = x.astype(BF16)
            for d in range(2):
                _, gi, _, _, a, sq = _lru_gates(x, xb, wa_ref[d, h], wx_ref[d, h], v_ref[d:d + 1, sl],
                                                v_ref[2 + d:3 + d, sl], v_ref[4 + d:5 + d, sl])
                a_ref[d, h] = a
                b_ref[d, h] = sq * (gi * x)

    wspec = pl.BlockSpec((2, 8, 128, 128), lambda i: (0, 0, 0, 0))
    ospec = pl.BlockSpec((2, 8, tr, 128), lambda i: (0, 0, i, 0))
    return pl.pallas_call(
        body, name=name, out_shape=[_sds((2, 8, lay.T, 128), F32)] * 2, grid=(lay.T // tr,),
        in_specs=[pl.BlockSpec((tr, D), lambda i: (i, 0)), wspec, wspec, pl.BlockSpec((6, D), lambda i: (0, 0))],
        out_specs=[ospec, ospec], compiler_params=_cp(("parallel",), VMEM_BIG))(uc, wa, wx, vec)


def lru_coeffs_bwd(lay, uc, wa, wx, vec, da, db, name):
    tr = lay.tc

    def body(x_ref, wa_ref, wx_ref, v_ref, da_ref, db_ref, dx_ref, dwa_ref, dwx_ref, dv_ref):
        @pl.when(pl.program_id(0) == 0)
        def _():
            dwa_ref[...] = jnp.zeros_like(dwa_ref)
            dwx_ref[...] = jnp.zeros_like(dwx_ref)
            dv_ref[...] = jnp.zeros_like(dv_ref)

        for h in range(8):
            sl = slice(h * 128, (h + 1) * 128)
            x = x_ref[:, sl]
            xb = x.astype(BF16)
            dx = jnp.zeros_like(x)
            for d in range(2):
                wab, wxb = wa_ref[d, h].astype(BF16), wx_ref[d, h].astype(BF16)
                r, gi, sp, dsp, a, sq = _lru_gates(x, xb, wa_ref[d, h], wx_ref[d, h], v_ref[d:d + 1, sl],
                                                   v_ref[2 + d:3 + d, sl], v_ref[4 + d:5 + d, sl])
                dbv, dav = db_ref[d, h], da_ref[d, h]
                t1 = dbv * sq
                dgi = t1 * x
                dx = dx + t1 * gi
                dla = dav * a - (dbv * gi * x) * (a * a) / sq
                dr = dla * (-LRU_C * sp)
                dlam = jnp.sum(dla * (-LRU_C * r), axis=0, keepdims=True) * dsp
                dpa = dr * r * (1.0 - r)
                dpx = dgi * gi * (1.0 - gi)
                dpab, dpxb = dpa.astype(BF16), dpx.astype(BF16)
                dwa_ref[d, h] += _tn(xb, dpab)
                dwx_ref[d, h] += _tn(xb, dpxb)
                dx = dx + _nt(dpab, wab) + _nt(dpxb, wxb)
                dv_ref[d:d + 1, sl] += jnp.sum(dpa, axis=0, keepdims=True)
                dv_ref[2 + d:3 + d, sl] += jnp.sum(dpx, axis=0, keepdims=True)
                dv_ref[4 + d:5 + d, sl] += dlam
            dx_ref[:, sl] = dx

    wspec = pl.BlockSpec((2, 8, 128, 128), lambda i: (0, 0, 0, 0))
    gspec = pl.BlockSpec((2, 8, tr, 128), lambda i: (0, 0, i, 0))
    vspec = pl.BlockSpec((6, D), lambda i: (0, 0))
    xspec = pl.BlockSpec((tr, D), lambda i: (i, 0))
    return pl.pallas_call(
        body, name=name,
        out_shape=[_sds((lay.T, D), F32), _sds((2, 8, 128, 128), F32), _sds((2, 8, 128, 128), F32), _sds((6, D), F32)],
        grid=(lay.T // tr,), in_specs=[xspec, wspec, wspec, vspec, gspec, gspec],
        out_specs=[xspec, wspec, wspec, vspec], compiler_params=_cp(("arbitrary",), VMEM_BIG))(uc, wa, wx, vec, da, db)


GB = 2
SCAN_UNROLL = 4


def _tile_scan(a, b, up):
    t = lax.broadcasted_iota(jnp.int32, a.shape, 0)
    for d in (1, 2, 4):
        sh = 8 - d if up else d
        m = (t < 8 - d) if up else (t >= d)
        a_prev, b_prev = pltpu.roll(a, sh, 0), pltpu.roll(b, sh, 0)
        b = jnp.where(m, a * b_prev + b, b)
        a = jnp.where(m, a * a_prev, a)
    return a, b


def lru_scan(lay, a, b, name):
    segs = [(0, lay.C), (lay.C, lay.L)]

    def body(a_ref, b_ref, s_ref):
        for d in range(2):
            rev = d == 1
            state = tuple(jnp.zeros((1, 128), F32) for _ in range(GB))
            for base, n in segs:
                nt = n // 8

                def step(j, c, base=base, nt=nt, rev=rev, d=d):
                    c = list(c)
                    for u in range(SCAN_UNROLL):
                        jj = j * SCAN_UNROLL + u
                        r0 = pl.multiple_of(base + 8 * ((nt - 1 - jj) if rev else jj), 8)
                        for g in range(GB):
                            at, bt = _tile_scan(a_ref[d, g, pl.ds(r0, 8), :], b_ref[d, g, pl.ds(r0, 8), :], rev)
                            h = at * c[g] + bt
                            s_ref[d, g, pl.ds(r0, 8), :] = h
                            c[g] = h[0:1] if rev else h[7:8]
                    return tuple(c)

                state = lax.fori_loop(0, nt // SCAN_UNROLL, step, state)

    spec = pl.BlockSpec((2, GB, lay.PS, 128), lambda s, hb: (0, hb, s, 0))
    return pl.pallas_call(
        body, name=name, out_shape=_sds((2, 8, lay.T, 128), F32), grid=(2, 8 // GB),
        in_specs=[spec, spec], out_specs=spec, compiler_params=_cp(("parallel", "parallel"), VMEM_BIG))(a, b)


def lru_scan_bwd(lay, a, s, dy, name):
    segs = [(0, lay.C), (lay.C, lay.L)]
    C, PS = lay.C, lay.PS

    def body(a_ref, s_ref, g_ref, da_ref, db_ref):
        t = lax.broadcasted_iota(jnp.int32, (8, 128), 0)
        for d in range(2):
            rev = d == 1
            carry = tuple(jnp.zeros((1, 128), F32) for _ in range(GB))
            for si in (1, 0):
                base, n = segs[si]
                nt = n // 8

                def step(j, c, base=base, nt=nt, rev=rev, d=d):
                    c = list(c)
                    for u in range(SCAN_UNROLL):
                        jj = j * SCAN_UNROLL + u
                        r0 = pl.multiple_of(base + 8 * (jj if rev else (nt - 1 - jj)), 8)
                        if rev:
                            rn = pl.multiple_of(jnp.where(r0 == PS - 8, 0, r0 + 8), 8)
                            nb_zero = r0 == C - 8
                        else:
                            rn = pl.multiple_of(jnp.maximum(r0 - 8, 0), 8)
                            nb_zero = r0 == 0
                        for g in range(GB):
                            av = a_ref[d, g, pl.ds(r0, 8), :]
                            gv = g_ref[g, pl.ds(r0, 8), :]
                            sv = s_ref[d, g, pl.ds(r0, 8), :]
                            nbt = s_ref[d, g, pl.ds(rn, 8), :]
                            at, bt = _tile_scan(av, av * gv, not rev)
                            m = at * c[g] + bt
                            if rev:
                                m_next = jnp.where(t >= 1, pltpu.roll(m, 1, 0), c[g])
                                nb = jnp.where(nb_zero, 0.0, nbt[0:1])
                                h_prev = jnp.where(t < 7, pltpu.roll(sv, 7, 0), nb)
                                c[g] = m[7:8]
                            else:
                                m_next = jnp.where(t < 7, pltpu.roll(m, 7, 0), c[g])
                                nb = jnp.where(nb_zero, 0.0, nbt[7:8])
                                h_prev = jnp.where(t >= 1, pltpu.roll(sv, 1, 0), nb)
                                c[g] = m[0:1]
                            lam = gv + m_next
                            db_ref[d, g, pl.ds(r0, 8), :] = lam
                            da_ref[d, g, pl.ds(r0, 8), :] = lam * h_prev
                    return tuple(c)

                carry = lax.fori_loop(0, nt // SCAN_UNROLL, step, carry)

    spec = pl.BlockSpec((2, GB, lay.PS, 128), lambda s, hb: (0, hb, s, 0))
    return pl.pallas_call(
        body, name=name, out_shape=[_sds((2, 8, lay.T, 128), F32)] * 2, grid=(2, 8 // GB),
        in_specs=[spec, spec, pl.BlockSpec((GB, lay.PS, 128), lambda s, hb: (hb, s, 0))],
        out_specs=[spec, spec], compiler_params=_cp(("parallel", "parallel"), VMEM_BIG))(a, s, dy)


def _gelu(x):
    k = math.sqrt(2.0 / math.pi)
    t = jnp.tanh(k * (x + 0.044715 * x * x * x))
    return 0.5 * x * (1.0 + t), 0.5 * (1.0 + t) + 0.5 * x * (1.0 - t * t) * k * (1.0 + 3 * 0.044715 * x * x)


def lru_gate(lay, p, s, name):
    tr = lay.tr

    def body(g_ref, s_ref, o_ref):
        for h in range(8):
            sl = slice(h * 128, (h + 1) * 128)
            o_ref[:, sl] = (_gelu(g_ref[:, sl])[0] * (s_ref[0, h] + s_ref[1, h])).astype(o_ref.dtype)

    return pl.pallas_call(
        body, name=name, out_shape=_sds((lay.T, D), BF16), grid=(lay.nblk,),
        in_specs=[pl.BlockSpec((tr, D), lambda i: (i, 0)), pl.BlockSpec((2, 8, tr, 128), lambda i: (0, 0, i, 0))],
        out_specs=pl.BlockSpec((tr, D), lambda i: (i, 0)), compiler_params=_cp(("parallel",)))(p, s)


def lru_gate_bwd(lay, p, s, do, name):
    tr = lay.tr

    def body(g_ref, s_ref, do_ref, dg_ref, dy_ref):
        for h in range(8):
            sl = slice(h * 128, (h + 1) * 128)
            ge, dge = _gelu(g_ref[:, sl])
            dov = do_ref[:, sl]
            dg_ref[:, sl] = (dov * (s_ref[0, h] + s_ref[1, h]) * dge).astype(dg_ref.dtype)
            dy_ref[h] = dov * ge

    xspec = pl.BlockSpec((tr, D), lambda i: (i, 0))
    return pl.pallas_call(
        body, name=name, out_shape=[_sds((lay.T, D), BF16), _sds((8, lay.T, 128), F32)], grid=(lay.nblk,),
        in_specs=[xspec, pl.BlockSpec((2, 8, tr, 128), lambda i: (0, 0, i, 0)), xspec],
        out_specs=[xspec, pl.BlockSpec((8, tr, 128), lambda i: (0, i, 0))],
        compiler_params=_cp(("parallel",)))(p, s, do)


def silu_rows(x, name):
    def body(x_ref, o_ref):
        v = x_ref[...]
        o_ref[...] = (v * _sigmoid(v)).astype(o_ref.dtype)
    return pl.pallas_call(body, name=name, out_shape=_sds(x.shape, BF16), in_specs=[VMEM_SPEC], out_specs=VMEM_SPEC)(x)


def mod_grad_rows(gath, name):
    w = gath.shape[-1]

    def body(g_ref, dm_ref, db_ref):
        dm_ref[...] = jnp.zeros_like(dm_ref)
        for l in range(2):
            ctx = g_ref[0, 3 * l + 2:3 * l + 3, :]
            tot = g_ref[0, 3 * l:3 * l + 1, :] + g_ref[0, 3 * l + 1:3 * l + 2, :]
            for k in range(8):
                dm_ref[l, 2 * k:2 * k + 2, :] = g_ref[k, 3 * l:3 * l + 2, :]
                if k:
                    ctx = ctx + g_ref[k, 3 * l + 2:3 * l + 3, :]
                    tot = tot + (g_ref[k, 3 * l:3 * l + 1, :] + g_ref[k, 3 * l + 1:3 * l + 2, :])
            dm_ref[l, 16:17, :] = ctx
            db_ref[l:l + 1, :] = tot + ctx

    return pl.pallas_call(body, name=name, out_shape=[_sds((2, 32, w), F32), _sds((2, w), F32)],
                          in_specs=[VMEM_SPEC], out_specs=[VMEM_SPEC, VMEM_SPEC])(gath)


def cctx_grad(p, c_ctx, name):
    def body(a_ref, c_ref, o_ref):
        cv = c_ref[...]
        sg = _sigmoid(cv)
        o_ref[...] = 0.5 * (a_ref[0, 0:1, :] + a_ref[1, 0:1, :]) * (sg * (1.0 + cv * (1.0 - sg)))
    return pl.pallas_call(body, name=name, out_shape=_sds((1, D), F32), in_specs=[VMEM_SPEC] * 2,
                          out_specs=VMEM_SPEC)(p, c_ctx)


def loss_and_grad(lay, h, tgt, name):
    def fn(hb, tb):
        lat = (pl.program_id(0) % lay.bps) >= lay.cb
        e = jnp.where(lat, hb - tb, 0.0)
        return e * (1.0 / D), jnp.sum(e * e, axis=0, keepdims=True) * (0.5 / D)
    return rowwise(lay, name, fn, [h, tgt], outs=[(D, F32)], sums=[(1, D)])


def adamw(w, g, m, v, name):
    shape = w.shape
    w2, g2, m2, v2 = (t.reshape(-1, shape[-1]) for t in (w, g, m, v))
    rows, width = w2.shape
    tr = 256 if rows % 256 == 0 else rows
    c1 = 1.0 - ADAM_B1 ** ADAM_STEP
    c2 = 1.0 - ADAM_B2 ** ADAM_STEP

    def body(w_ref, g_ref, m_ref, v_ref, d_ref, mo_ref, vo_ref):
        gv = g_ref[...]
        mn = ADAM_B1 * m_ref[...] + (1.0 - ADAM_B1) * gv
        vn = ADAM_B2 * v_ref[...] + (1.0 - ADAM_B2) * (gv * gv)
        d_ref[...] = -ADAM_LR * ((mn / c1) / (jnp.sqrt(vn / c2) + ADAM_EPS) + ADAM_WD * w_ref[...])
        mo_ref[...] = mn
        vo_ref[...] = vn

    spec = pl.BlockSpec((tr, width), lambda i: (i, 0))
    d, mn, vn = pl.pallas_call(body, name=name, out_shape=[_sds((rows, width), F32)] * 3, grid=(rows // tr,),
                               in_specs=[spec] * 4, out_specs=[spec] * 3, compiler_params=_cp(("parallel",)))(w2, g2, m2, v2)
    return d.reshape(shape), mn.reshape(shape), vn.reshape(shape)


def adamw_ffn(w, m, v, red, kind, ns, name):
    shape = w.shape
    w2, m2, v2 = (t.reshape(-1, shape[-1]) for t in (w, m, v))
    rows, width = w2.shape
    c1 = 1.0 - ADAM_B1 ** ADAM_STEP
    c2 = 1.0 - ADAM_B2 ** ADAM_STEP
    tr, nb = ns // 2, 2
    gspec = pl.BlockSpec((tr, D), lambda i: (((i // nb) * 3 + kind) * nb + i % nb, 0))

    def body(w_ref, g_ref, m_ref, v_ref, go_ref, d_ref, mo_ref, vo_ref):
        gv = g_ref[...]
        mn = ADAM_B1 * m_ref[...] + (1.0 - ADAM_B1) * gv
        vn = ADAM_B2 * v_ref[...] + (1.0 - ADAM_B2) * (gv * gv)
        go_ref[...] = gv
        d_ref[...] = -ADAM_LR * ((mn / c1) / (jnp.sqrt(vn / c2) + ADAM_EPS) + ADAM_WD * w_ref[...])
        mo_ref[...] = mn
        vo_ref[...] = vn

    spec = pl.BlockSpec((tr, width), lambda i: (i, 0))
    outs = pl.pallas_call(body, name=name, out_shape=[_sds((rows, width), F32)] * 4, grid=(rows // tr,),
                          in_specs=[spec, gspec, spec, spec], out_specs=[spec] * 4,
                          compiler_params=_cp(("parallel",)))(w2, red, m2, v2)
    return tuple(t.reshape(shape) for t in outs)


def mod_mm(sc, w_mod, bias, name):
    wm = w_mod.shape[-1]
    tn = _pick(wm, (768, 512, 384, 256, 128))

    def body(a_ref, b_ref, c_ref, o_ref):
        o_ref[...] = _nn(a_ref[...], b_ref[...].astype(BF16)) + c_ref[...]

    return pl.pallas_call(
        body, name=name, out_shape=_sds((DEPTH, 32, wm), F32), grid=(DEPTH, wm // tn),
        in_specs=[pl.BlockSpec((32, D), lambda l, j: (0, 0)), pl.BlockSpec((None, D, tn), lambda l, j: (l, 0, j)),
                  pl.BlockSpec((None, 1, tn), lambda l, j: (l, 0, j))],
        out_specs=pl.BlockSpec((None, 32, tn), lambda l, j: (l, 0, j)),
        compiler_params=_cp(("parallel", "parallel")))(sc, w_mod, bias)


def wmod_dw(sc, dcol, name):
    wm = dcol.shape[-1]
    tm = 256

    def body(a_ref, b_ref, o_ref):
        o_ref[...] = _tn(a_ref[...], b_ref[...].astype(BF16))

    return pl.pallas_call(
        body, name=name, out_shape=_sds((DEPTH, D, wm), F32), grid=(DEPTH, D // tm),
        in_specs=[pl.BlockSpec((32, tm), lambda l, i: (0, i)), pl.BlockSpec((None, 32, wm), lambda l, i: (l, 0, 0))],
        out_specs=pl.BlockSpec((None, tm, wm), lambda l, i: (l, i, 0)),
        compiler_params=_cp(("parallel", "parallel")))(sc, dcol)


def cctx_dx(drow, w_mod, name):
    wm = w_mod.shape[-1]

    def body(a_ref, b_ref, o_ref):
        o_ref[...] = _nt(a_ref[...].astype(BF16), b_ref[...].astype(BF16))

    return pl.pallas_call(
        body, name=name, out_shape=_sds((DEPTH, 16, D), F32), grid=(DEPTH,),
        in_specs=[pl.BlockSpec((None, 16, wm), lambda l: (l, 0, 0)), pl.BlockSpec((None, D, wm), lambda l: (l, 0, 0))],
        out_specs=pl.BlockSpec((None, 16, D), lambda l: (l, 0, 0)), compiler_params=_cp(("parallel",), VMEM_BIG))(drow, w_mod)


HEAD_PERM = (0, 4, 1, 5, 2, 6, 3, 7)


def _rot_rows(wt):
    return jnp.concatenate([-wt[32:64], wt[0:32]], axis=0)


def _unrot_rows(g):
    return jnp.concatenate([g[32:64], -g[0:32]], axis=0)


def _heads(a, n):
    return [a[64 * i:64 * (i + 1)] for i in range(n)]


def kernel(x, c, ctx, c_ctx, w_mod, b_mod, ln_g, ln_b, ffn_w_gate, ffn_w_up, ffn_w_down, mix_ab_w_in, attn_sink, pool_w, pool_scale, mix_ab_w_out, lru_w_in, lru_conv_w, lru_conv_b, lru_wa, lru_ba, lru_wx, lru_bx, lru_lambda, lru_w_out, loss_target, m_c_ctx, m_w_mod, m_b_mod, m_ln_g, m_ln_b, m_ffn_w_gate, m_ffn_w_up, m_ffn_w_down, m_mix_ab_w_in, m_attn_sink, m_pool_w, m_pool_scale, m_mix_ab_w_out, m_lru_w_in, m_lru_conv_w, m_lru_conv_b, m_lru_wa, m_lru_ba, m_lru_wx, m_lru_bx, m_lru_lambda, m_lru_w_out, v_c_ctx, v_w_mod, v_b_mod, v_ln_g, v_ln_b, v_ffn_w_gate, v_ffn_w_up, v_ffn_w_down, v_mix_ab_w_in, v_attn_sink, v_pool_w, v_pool_scale, v_mix_ab_w_out, v_lru_w_in, v_lru_conv_w, v_lru_conv_b, v_lru_wa, v_lru_ba, v_lru_wx, v_lru_bx, v_lru_lambda, v_lru_w_out):
    n_lat, n_ctx = x.shape[1], ctx.shape[1]
    lay = Layout(n_ctx, n_lat)
    T = lay.T
    ns = ffn_w_gate.shape[-1]
    n_li, n_ai = lru_w_in.shape[-1], mix_ab_w_in.shape[-1]
    n_ao, n_lo = mix_ab_w_out.shape[1], lru_w_out.shape[1]
    wm = w_mod.shape[-1]
    dsh = ln_g.shape[-1]
    mx, my, mc = lax.axis_index("x"), lax.axis_index("y"), lax.axis_index("c")
    chip = 2 * mx + my
    me = 2 * chip + mc

    c_all = all_gather8(c, "ag8_c").reshape(16, D)
    cc = jnp.concatenate([c_all, c_ctx[None, :], jnp.zeros((15, D), F32)], axis=0)
    sc = silu_rows(cc, "silu_c")
    bias = lax.dynamic_slice(b_mod, (0, chip * wm), (DEPTH, wm)).reshape(DEPTH, 1, wm)
    modg = all_gather_chips(mod_mm(sc, w_mod, bias, "mod_mm"), "ag_mod")
    modtab = []
    for l in range(DEPTH):
        full = jnp.transpose(modg[:, l], (1, 0, 2)).reshape(32, N_CHIP * wm)
        mine = lax.dynamic_slice(full, (2 * me, 0), (2, N_CHIP * wm))
        modtab.append(jnp.concatenate([mine, full[16:17]], axis=0).reshape(3, N_MOD, D))

    small = jnp.concatenate([ln_g.reshape(6, dsh), ln_b.reshape(6, dsh), lru_conv_w[0], lru_conv_b, lru_ba[0],
                             lru_bx[0], lru_lambda[0], jnp.zeros((9, dsh), F32)], axis=0)
    small = all_gather_chips(small.reshape(2, 16, dsh), "ag_small").reshape(N_CHIP, 32, dsh)
    small = jnp.transpose(small, (1, 0, 2)).reshape(32, D)
    ln_g_f, ln_b_f = small[0:6].reshape(2, 3, D), small[6:12].reshape(2, 3, D)
    conv_w_f, conv_b_f = small[12:16], small[16:17]
    lru_vec = small[17:23]

    hh = 3 * ns // 2
    gate_t, up_t = jnp.swapaxes(ffn_w_gate, -1, -2), jnp.swapaxes(ffn_w_up, -1, -2)
    placed = [ffn_place(gate_t, up_t, ffn_w_down, g // 2, g % 2, f"ag_ffn{g}_place") for g in range(4)]
    wb = [gather_placed(placed[0], "ag_ffn0"), None, None, None]
    mix_sh = jnp.concatenate([lru_w_in[0].T, mix_ab_w_in[0].T, mix_ab_w_out[0], lru_w_out[0]], axis=0).astype(BF16)
    n_mix = n_li + n_ai + n_ao + n_lo
    mixw = all_gather_chips(mix_sh.reshape(2, n_mix // 2, D), "ag_mix").reshape(N_CHIP, n_mix, D)
    o1, o2, o3 = n_li, n_li + n_ai, n_li + n_ai + n_ao
    lru_in_t = mixw[:, 0:o1].reshape(N_CHIP * n_li, D)
    ab_in_t = mixw[:, o1:o2].reshape(N_CHIP * n_ai, D)
    ab_out = mixw[:, o2:o3].reshape(N_CHIP * n_ao, D)
    lru_out = mixw[:, o3:].reshape(N_CHIP * n_lo, D)
    qh, kh = _heads(ab_in_t[Q0:K0], N_HEADS), _heads(ab_in_t[K0:V0], N_KV)
    w_ext_t = jnp.concatenate([qh[h] for h in HEAD_PERM] + [ab_in_t[K0:QR0]]
                              + [_rot_rows(qh[h]) for h in HEAD_PERM] + [_rot_rows(t) for t in kh], axis=0)
    oh = _heads(ab_out[0:ATT_W], N_HEADS)
    w_out_ext = jnp.concatenate([oh[h] for h in HEAD_PERM] + [ab_out[ATT_W:]], axis=0)

    t = jnp.arange(n_lat)
    inv = ROPE_THETA ** (-jnp.arange(16, dtype=F32) / 16.0)
    ang = jnp.concatenate([(t // GRID_W).astype(F32)[:, None] * inv, (t % GRID_W).astype(F32)[:, None] * inv], axis=-1)
    cos1 = jnp.concatenate([jnp.ones((n_ctx, 32), F32), jnp.cos(ang)], axis=0)
    sin1 = jnp.concatenate([jnp.zeros((n_ctx, 32), F32), jnp.sin(ang)], axis=0)
    cos_t = jnp.tile(cos1, (2, 4))
    sin_t = jnp.tile(sin1, (2, 4))
    sk = attn_sink[0]
    sink_tab = jnp.concatenate([jnp.repeat(jnp.stack([sk[:4], sk[4:]], axis=1), HEAD_DIM, axis=1),
                                jnp.zeros((4, 128), F32)], axis=0)
    pscale = pool_scale.reshape(1, POOL_W)

    h0 = jnp.concatenate([ctx, x], axis=1).reshape(T, D)
    tgt = loss_target.reshape(2 * n_lat, D)

    def lnv(l, j):
        return jnp.stack([ln_g_f[l, j], ln_b_f[l, j]])

    subs = [(0, 0, 0.5, 0), (0, 3, 1.0, 1), (0, 6, 0.5, 2), (1, 0, 0.5, 0), (1, 3, 1.0, 1), (1, 6, 0.5, 2)]

    def ffn_core(hm, l, f):
        tag = f"l{l}f{f}"
        gi = 2 * l + f
        w = wb[gi].reshape(N_CHIP, 3 * ns, D)
        if gi == 3:
            sp, sl, u, a = ffn_up(lay, hm, w, 0, 1, ns, f"ffn_up_{tag}")
            (y,) = slab_nn_acc(lay, [a], w, [2], ns, f"ffn_down_{tag}")
            return y, dict(sp=sp, sl=sl, u=u, a=a, nbuf=None)
        sp, sl, u, a, nbuf = ffn_up(lay, hm, w, 0, 1, ns, f"ffn_up_{tag}", rider=rider_gather_xy(placed[gi + 1]))
        y, nbuf = slab_nn_acc(lay, [a], w, [2], ns, f"ffn_down_{tag}", rider=rider_gather_fwd(nbuf))
        return y, dict(sp=sp, sl=sl, u=u, a=a, nbuf=nbuf)

    def mixa_core(hm):
        p = mm_nt(hm, w_ext_t, "mixa_in")
        qr, kr, vb, u = rope_fwd(lay, p, cos_t, sin_t, "rope")
        att, lse = attn_fwd(lay, qr, kr, vb, sink_tab, "attn")
        pool = pool_fwd(lay, u, pool_w[0], pscale, "pool")
        cat = jnp.concatenate([att, pool], axis=1)
        return mm_nn(cat, w_out_ext, "mixa_out"), dict(qr=qr, kr=kr, vb=vb, u=u, lse=lse, cat=cat)

    def mixc_core(hm):
        p = mm_nt(hm, lru_in_t, "mixc_in")
        uc = conv_fwd(lay, p, D, conv_w_f, conv_b_f, "conv")
        a, b = lru_coeffs(lay, uc, lru_wa[0], lru_wx[0], lru_vec, "lru_coef")
        s = lru_scan(lay, a, b, "lru_scan")
        o = lru_gate(lay, p, s, "lru_gate")
        return mm_nn(o, lru_out, "mixc_out"), dict(p=p, uc=uc, a=a, s=s, o=o)

    recs = []
    h = h0
    hm = modulate(lay, h0, modtab[0], 0, 1, "mod_first")
    for k, (l, k0, coef, j) in enumerate(subs):
        if k0 == 3:
            y, core = mixa_core(hm) if l == 0 else mixc_core(hm)
        else:
            y, core = ffn_core(hm, l, k0 // 6)
        nxt = None if k == 5 else (modtab[subs[k + 1][0]], subs[k + 1][1], subs[k + 1][1] + 1)
        nbuf = core.pop("nbuf", None)
        res = resid_ln(lay, h, y, modtab[l], k0 + 2, coef, lnv(l, j), f"ln_s{k}", nxt=nxt,
                       rider=None if nbuf is None else rider_gather_d2d(nbuf))
        if nbuf is not None:
            wb[2 * l + k0 // 6 + 1] = res[-1]
        recs.append(dict(h=h, hm=hm, y=y, xhat=res[1], rstd=res[2], **core))
        h = res[0]
        hm = res[3] if nxt is not None else None

    dout, lparts = loss_and_grad(lay, h, tgt, "loss")
    loss = lax.psum(jnp.sum(lparts), ("x", "y", "c"))

    dln = {}
    dms = {}
    mixg = {}
    ffn_red = [lax.empty((4, 2, hh, D), F32)]
    pending = []

    def ffn_core_bwd(dy, r, l, f):
        tag = f"l{l}f{f}"
        gi = 2 * l + f
        w = wb[gi].reshape(N_CHIP, 3 * ns, D)
        prev = pending.pop() if pending else None
        if prev is None:
            dg, du = ffn_bwd_da(lay, dy, w, 2, r["sp"], r["sl"], r["u"], ns, f"ffn_da_{tag}")
        else:
            dg, du, recv = ffn_bwd_da(lay, dy, w, 2, r["sp"], r["sl"], r["u"], ns, f"ffn_da_{tag}",
                                      rider=rider_reduce_sib(prev[1]))
        gb = lax.empty((N_CHIP, 3 * ns, D), F32)
        if prev is None:
            (gb,) = slab_tn(lay, r["a"], dy, gb, 2, ns, f"ffn_dwd_{tag}")
            (gb,) = slab_tn(lay, dg, r["hm"], gb, 0, ns, f"ffn_dwg_{tag}")
            (gb,) = slab_tn(lay, du, r["hm"], gb, 1, ns, f"ffn_dwu_{tag}")
            (dhm,) = slab_nn_acc(lay, [dg, du], w, [0, 1], ns, f"ffn_dh_{tag}")
        else:
            q = add_own_half(prev[1], recv, BF16, f"rs_add2_ffn{prev[0]}")
            gb, arr = slab_tn(lay, r["a"], dy, gb, 2, ns, f"ffn_dwd_{tag}", rider=rider_reduce_copy(q, 0))
            gb, arr = slab_tn(lay, dg, r["hm"], gb, 0, ns, f"ffn_dwg_{tag}", rider=rider_reduce_copy(q, 1, arr))
            gb, arr = slab_tn(lay, du, r["hm"], gb, 1, ns, f"ffn_dwu_{tag}", rider=rider_reduce_copy(q, 2, arr))
            red = sum_slots(q, arr, f"rs_add4_ffn{prev[0]}", dst=ffn_red[0], g=prev[0])
            dhm, ffn_red[0] = slab_nn_acc(lay, [dg, du], w, [0, 1], ns, f"ffn_dh_{tag}", rider=rider_join(red, prev[0]))
        pending.append((gi, gb.reshape(N_CHIP, 2, hh, D)))
        return dhm

    def mixc_core_bwd(dy, r):
        do_c = mm_nt(dy, lru_out, "mixc_out_dx")
        mixg["lru_out"] = mm_tn(r["o"], dy, "mixc_out_dw")
        dgate, dyg = lru_gate_bwd(lay, r["p"], r["s"], do_c, "lru_gate_b")
        da_c, db_c = lru_scan_bwd(lay, r["a"], r["s"], dyg, "lru_scan_b")
        duc, mixg["wa"], mixg["wx"], mixg["vec"] = lru_coeffs_bwd(lay, r["uc"], lru_wa[0], lru_wx[0], lru_vec, da_c, db_c,
                                                                  "lru_coef_b")
        du_c, mixg["cw"], mixg["cb"] = conv_bwd(lay, r["p"], D, conv_w_f, duc, "conv_b")
        dp_c = jnp.concatenate([dgate, du_c], axis=1)
        mixg["lru_in_t"] = mm_tn(dp_c, r["hm"], "mixc_in_dw")
        return mm_nn(dp_c, lru_in_t, "mixc_in_dx")

    def mixa_core_bwd(dy, r):
        dcat = mm_nt(dy, w_out_ext, "mixa_out_dx")
        mixg["out_ext"] = mm_tn(r["cat"], dy, "mixa_out_dw")
        dqr, dkr, dv, mixg["sink"] = attn_bwd(lay, r["qr"], r["kr"], r["vb"], sink_tab, r["lse"], dcat, "attn_b")
        du_a, mixg["pw"], mixg["ps"] = pool_bwd(lay, r["u"], dcat, pool_w[0], pscale, "pool_b")
        dp_a = rope_bwd(lay, dqr, dkr, dv, du_a, cos_t, sin_t, "rope_b")
        mixg["ext_t"] = mm_tn(dp_a, r["hm"], "mixa_in_dw")
        return mm_nn(dp_a, w_ext_t, "mixa_in_dx")

    l, k0, coef, j = subs[5]
    dy, dres, s1 = ln_bwd(lay, dout, recs[5]["xhat"], recs[5]["rstd"], recs[5]["y"], modtab[l], k0 + 2, coef, lnv(l, j),
                          "lnb_s5")
    for k in range(5, -1, -1):
        l, k0, coef, j = subs[k]
        r = recs[k]
        if k0 == 3:
            dhm = mixa_core_bwd(dy, r) if l == 0 else mixc_core_bwd(dy, r)
        else:
            dhm = ffn_core_bwd(dy, r, l, k0 // 6)
        dln[(l, j)] = block_sums(lay, s1, f"bs_ln_s{k}")
        if k > 0:
            lp, k0p, coefp, jp = subs[k - 1]
            rp = recs[k - 1]
            dy, dres, s1, s2 = modb_lnb(lay, dres, dhm, r["h"], modtab[l], k0 + 1, rp["xhat"], rp["rstd"], rp["y"],
                                        modtab[lp], k0p + 2, coefp, lnv(lp, jp), f"modb_lnb_s{k}")
        else:
            gx, s2 = mod_bwd(lay, dres, dhm, r["h"], modtab[l], k0 + 1, "modb_s0")
        dms[(l, k0)] = block_sums(lay, s2, f"bs_mod_s{k}")
    grad_x = gx.reshape(2, n_lat, D)
    g_lru_out, g_wa, g_wx, g_vec, g_cw, g_cb = (mixg[n] for n in ("lru_out", "wa", "wx", "vec", "cw", "cb"))
    g_lru_in_t, g_out_ext, g_sink, g_pw, g_ps, g_ext_t = (mixg[n] for n in ("lru_in_t", "out_ext", "sink", "pw", "ps", "ext_t"))

    rows = []
    for l in range(DEPTH):
        per_k = []
        for k0, j in ((0, 0), (3, 1), (6, 2)):
            per_k += [dms[(l, k0)][:3, 0], dms[(l, k0)][:3, 1], dln[(l, j)][:3, 2]]
        rows.append(jnp.stack(per_k, axis=1).reshape(3, N_MOD * D))
    dmod_loc = jnp.concatenate(rows + [jnp.zeros((2, N_MOD * D), F32)], axis=0)
    dmod_all, g_b_mod = mod_grad_rows(all_gather8(dmod_loc, "ag8_dmod"), "dmod_rows")
    dcol = lax.dynamic_slice(dmod_all, (0, 0, chip * wm), (DEPTH, 32, wm))
    g_w_mod = wmod_dw(sc, dcol, "wmod_dw")
    g_cctx = cctx_grad(cctx_dx(dcol[:, 16:32], w_mod, "cctx_dx"), c_ctx[None, :], "cctx_grad")

    gq = _heads(g_ext_t[Q0:K0], N_HEADS)
    gqr = _heads(g_ext_t[QR0:KR0], N_HEADS)
    g_q = [None] * N_HEADS
    for i, h in enumerate(HEAD_PERM):
        g_q[h] = gq[i] + _unrot_rows(gqr[i])
    gk = [a + _unrot_rows(b) for a, b in zip(_heads(g_ext_t[K0:V0], N_KV), _heads(g_ext_t[KR0:PEXT], N_KV))]
    g_ab_in_t = jnp.concatenate(g_q + gk + [g_ext_t[V0:QR0]], axis=0)
    go = _heads(g_out_ext[0:ATT_W], N_HEADS)
    g_o = [None] * N_HEADS
    for i, h in enumerate(HEAD_PERM):
        g_o[h] = go[i]
    g_ab_out = jnp.concatenate(g_o + [g_out_ext[ATT_W:]], axis=0)
    mix_g = jnp.concatenate([g_lru_in_t.reshape(N_CHIP, n_li, D), g_ab_in_t.reshape(N_CHIP, n_ai, D),
                             g_ab_out.reshape(N_CHIP, n_ao, D), g_lru_out.reshape(N_CHIP, n_lo, D)], axis=1)

    g_ln_g = jnp.stack([jnp.stack([dln[(l, j)][3, 1] for j in range(3)]) for l in range(DEPTH)])
    g_ln_b = jnp.stack([jnp.stack([dln[(l, j)][3, 0] for j in range(3)]) for l in range(DEPTH)])
    sink_row = jnp.sum(g_sink, axis=0)[:4]
    g_sink8 = jnp.concatenate([sink_row[:, 0], sink_row[:, HEAD_DIM]])
    misc = jnp.concatenate([g_sink8, jnp.sum(g_ps, axis=0).reshape(POOL_W), jnp.zeros((D - 8 - POOL_W,), F32)])
    small_g = jnp.concatenate([
        g_ln_g.reshape(6, D), g_ln_b.reshape(6, D), jnp.sum(g_cw, axis=0), jnp.sum(g_cb, axis=0), g_vec,
        misc[None, :], jnp.sum(g_pw, axis=0).reshape(64, D), g_wa.reshape(256, D), g_wx.reshape(256, D), g_cctx,
        jnp.zeros((39, D), F32)], axis=0)
    n_small = small_g.shape[0] // N_CHIP
    mix_buf = jnp.concatenate([mix_g, small_g.reshape(N_CHIP, n_small, D)], axis=1)
    n_mb = n_mix + n_small

    last_g, last_buf = pending.pop()
    ffn_red = reduce_scatter_chips(last_buf, f"ffn{last_g}", wire=BF16, dst=ffn_red[0], g=last_g).reshape(12 * ns, D)
    mix_red = reduce_scatter_chips(mix_buf.reshape(N_CHIP, 2, n_mb // 2, D), "mix").reshape(n_mb, D)
    small_red = all_gather_chips(mix_red[n_mix:].reshape(2, n_small // 2, D), "ag_smallg").reshape(N_CHIP * n_small, D)

    ffn_kind = dict(ffn_w_gate=0, ffn_w_up=1, ffn_w_down=2)

    def cols(a):
        return lax.dynamic_slice_in_dim(a, chip * dsh, dsh, axis=a.ndim - 1)

    sr = small_red
    grads = dict(
        c_ctx=sr[600], w_mod=g_w_mod, b_mod=g_b_mod,
        ln_g=cols(sr[0:6]).reshape(2, 3, dsh), ln_b=cols(sr[6:12]).reshape(2, 3, dsh),
        mix_ab_w_in=mix_red[o1:o2][None], attn_sink=sr[23, 0:8][None], pool_w=sr[24:88].reshape(1, 4, 128, 128),
        pool_scale=sr[23, 8:8 + POOL_W][None], mix_ab_w_out=mix_red[o2:o3][None], lru_w_in=mix_red[0:o1].T[None],
        lru_conv_w=cols(sr[12:16])[None], lru_conv_b=cols(sr[16:17]), lru_wa=sr[88:344].reshape(1, 2, 8, 128, 128),
        lru_ba=cols(sr[17:19])[None], lru_wx=sr[344:600].reshape(1, 2, 8, 128, 128), lru_bx=cols(sr[19:21])[None],
        lru_lambda=cols(sr[21:23])[None], lru_w_out=mix_red[o3:n_mix][None])
    params = dict(c_ctx=(c_ctx, m_c_ctx, v_c_ctx), w_mod=(w_mod, m_w_mod, v_w_mod), b_mod=(b_mod, m_b_mod, v_b_mod),
                  ln_g=(ln_g, m_ln_g, v_ln_g), ln_b=(ln_b, m_ln_b, v_ln_b),
                  ffn_w_gate=(ffn_w_gate, m_ffn_w_gate, v_ffn_w_gate), ffn_w_up=(ffn_w_up, m_ffn_w_up, v_ffn_w_up),
                  ffn_w_down=(ffn_w_down, m_ffn_w_down, v_ffn_w_down),
                  mix_ab_w_in=(mix_ab_w_in, m_mix_ab_w_in, v_mix_ab_w_in), attn_sink=(attn_sink, m_attn_sink, v_attn_sink),
                  pool_w=(pool_w, m_pool_w, v_pool_w), pool_scale=(pool_scale, m_pool_scale, v_pool_scale),
                  mix_ab_w_out=(mix_ab_w_out, m_mix_ab_w_out, v_mix_ab_w_out), lru_w_in=(lru_w_in, m_lru_w_in, v_lru_w_in),
                  lru_conv_w=(lru_conv_w, m_lru_conv_w, v_lru_conv_w), lru_conv_b=(lru_conv_b, m_lru_conv_b, v_lru_conv_b),
                  lru_wa=(lru_wa, m_lru_wa, v_lru_wa), lru_ba=(lru_ba, m_lru_ba, v_lru_ba), lru_wx=(lru_wx, m_lru_wx, v_lru_wx),
                  lru_bx=(lru_bx, m_lru_bx, v_lru_bx), lru_lambda=(lru_lambda, m_lru_lambda, v_lru_lambda),
                  lru_w_out=(lru_w_out, m_lru_w_out, v_lru_w_out))
    gl, dl, ml, vl = [], [], [], []
    transposed = ("ffn_w_gate", "ffn_w_up", "mix_ab_w_in")
    for name, (w, m, v) in params.items():
        if name in transposed:
            w, m, v = (jnp.swapaxes(t, -1, -2) for t in (w, m, v))
        if name in ffn_kind:
            g, d, mn, vn = adamw_ffn(w, m, v, ffn_red, ffn_kind[name], ns, f"adamw_{name}")
        else:
            g = grads[name].reshape(w.shape)
            d, mn, vn = adamw(w, g, m, v, f"adamw_{name}")
        if name in transposed:
            g, d, mn, vn = (jnp.swapaxes(t, -1, -2) for t in (g, d, mn, vn))
        gl.append(g)
        dl.append(d)
        ml.append(mn)
        vl.append(vn)
    return (loss, grad_x, *gl, *dl, *ml, *vl)
```

```python
import functools
import math

import jax
import jax.numpy as jnp
from jax import lax
from jax.experimental import pallas as pl
from jax.experimental.pallas import tpu as pltpu

F32, BF16 = jnp.float32, jnp.bfloat16
MESH = pl.DeviceIdType.MESH
ANY = pl.BlockSpec(memory_space=pl.ANY)
VMEM_SPEC = pl.BlockSpec(memory_space=pltpu.VMEM)

D = 1024
N_CHIP = 4
HEAD_DIM, N_HEADS, N_KV = 64, 8, 2
ATT_W, KV_W, POOL_W = 512, 128, 512
POOL_WINDOWS = (2, 4, 8, 16)
BLK = 128
ATT_SCALE = HEAD_DIM ** -0.5
ROPE_THETA = 10000.0
GRID_W = 64
LRU_C = 8.0
LN_EPS = 1e-5
NEG_INF = -1e30
DEPTH = 2
ALPHA = (2 * DEPTH) ** 0.25
N_MOD = 9
ADAM_LR, ADAM_B1, ADAM_B2, ADAM_EPS, ADAM_WD, ADAM_STEP = 0.001, 0.9, 0.999, 1e-08, 0.01, 10
VMEM_BIG = 48 * 1024 * 1024


def _cp(sem=None, vmem=None):
    kw = {}
    if sem is not None:
        kw["dimension_semantics"] = sem
    if vmem is not None:
        kw["vmem_limit_bytes"] = vmem
    return pltpu.CompilerParams(**kw)


def _sds(shape, dtype):
    return jax.ShapeDtypeStruct(tuple(shape), dtype)


def _pick(n, cands):
    for c in cands:
        if n % c == 0:
            return c
    return n


def _dot(a, b, dims):
    return lax.dot_general(a, b, (dims, ((), ())), preferred_element_type=F32)


def _nn(a, b):
    return _dot(a, b, ((1,), (0,)))


def _nt(a, b):
    return _dot(a, b, ((1,), (1,)))


def _tn(a, b):
    return _dot(a, b, ((0,), (0,)))


def _sigmoid(x):
    return 0.5 * jnp.tanh(0.5 * x) + 0.5


def _me():
    return lax.axis_index("x"), lax.axis_index("y"), lax.axis_index("c")


def _rcopy(src, dst, ssem, rsem, dev):
    return pltpu.make_async_remote_copy(src_ref=src, dst_ref=dst, send_sem=ssem, recv_sem=rsem,
                                        device_id=dev, device_id_type=MESH)


def all_gather8(x, name):
    def body(x_ref, o_ref, ssem, rsem, lsem):
        mx, my, mc = _me()
        me = 4 * mx + 2 * my + mc
        loc = pltpu.make_async_copy(x_ref, o_ref.at[me], lsem)
        loc.start()
        peers = []
        for m in range(1, 8):
            px = 1 - mx if (m >> 2) & 1 else mx
            py = 1 - my if (m >> 1) & 1 else my
            pc = 1 - mc if m & 1 else mc
            peers.append((px, py, pc))
        sends = [_rcopy(x_ref, o_ref.at[me], ssem.at[k], rsem.at[k], p) for k, p in enumerate(peers)]
        for cp in sends:
            cp.start()
        for k, (px, py, pc) in enumerate(peers):
            _rcopy(x_ref, o_ref.at[4 * px + 2 * py + pc], ssem.at[k], rsem.at[k], (px, py, pc)).wait_recv()
        for cp in sends:
            cp.wait_send()
        loc.wait()

    return pl.pallas_call(
        body, name=name, out_shape=_sds((8,) + x.shape, x.dtype),
        in_specs=[VMEM_SPEC], out_specs=VMEM_SPEC,
        scratch_shapes=[pltpu.SemaphoreType.DMA((7,)), pltpu.SemaphoreType.DMA((7,)), pltpu.SemaphoreType.DMA],
    )(x)


_ROW_BLOCKS = (512, 384, 352, 256, 224, 128)


def _idx(v):
    return jnp.reshape(v, (1,)).astype(jnp.int32)


def place_slab(shard, name):
    _, h, w = shard.shape
    th = _pick(h, _ROW_BLOCKS)

    def body(s_ref, x_ref, o_ref):
        del s_ref
        o_ref[...] = x_ref[...]

    return pl.pallas_call(
        body, name=name, out_shape=_sds((N_CHIP,) + shard.shape, shard.dtype),
        grid_spec=pltpu.PrefetchScalarGridSpec(
            num_scalar_prefetch=1, grid=(2, h // th),
            in_specs=[pl.BlockSpec((None, th, w), lambda k, r, s: (k, r, 0))],
            out_specs=pl.BlockSpec((None, None, th, w), lambda k, r, s: (s[0], k, r, 0))),
    )(_idx(2 * lax.axis_index("x") + lax.axis_index("y")), shard)


def ffn_place(w_gate_t, w_up_t, w_down, l, f, name):
    ns = w_down.shape[-2]
    tc = 256

    def body(s_ref, g_ref, u_ref, d_ref, o_ref):
        del s_ref
        k = pl.program_id(0)

        @pl.when(k == 0)
        def _():
            o_ref[...] = g_ref[...].astype(BF16)

        @pl.when(k == 1)
        def _():
            o_ref[...] = u_ref[...].astype(BF16)

        @pl.when(k == 2)
        def _():
            o_ref[...] = d_ref[...].astype(BF16)

    spec = pl.BlockSpec((None, None, ns, tc), lambda k, j, s: (l, f, 0, j))
    out = pl.pallas_call(
        body, name=name, out_shape=_sds((N_CHIP, 3 * ns, D), BF16),
        grid_spec=pltpu.PrefetchScalarGridSpec(
            num_scalar_prefetch=1, grid=(3, D // tc), in_specs=[spec, spec, spec],
            out_specs=pl.BlockSpec((None, ns, tc), lambda k, j, s: (s[0], k, j))),
    )(_idx(2 * lax.axis_index("x") + lax.axis_index("y")), w_gate_t, w_up_t, w_down)
    return out.reshape(N_CHIP, 2, 3 * ns // 2, D)


def all_gather_chips(shard, name):
    return gather_placed(place_slab(shard, name + "_place"), name)


def gather_placed(full, name):
    def body(x_ref, o_ref, ssem, rsem):
        del x_ref
        mx, my, mc = _me()
        s = 2 * mx + my
        sib = (mx, my, 1 - mc)
        chips = [(1 - mx, my), (mx, 1 - my), (1 - mx, 1 - my)]
        first = [_rcopy(o_ref.at[s, mc], o_ref.at[s, mc], ssem.at[j], rsem.at[j], (px, py, mc))
                 for j, (px, py) in enumerate(chips)]
        for cp in first:
            cp.start()
        passed = []
        for j, (px, py) in enumerate(chips):
            ps = 2 * px + py
            _rcopy(o_ref.at[ps, mc], o_ref.at[ps, mc], ssem.at[j], rsem.at[j], (px, py, mc)).wait_recv()
            fw = _rcopy(o_ref.at[ps, mc], o_ref.at[ps, mc], ssem.at[3 + j], rsem.at[3 + j], sib)
            fw.start()
            passed.append(fw)
        for j, (px, py) in enumerate(chips):
            ps = 2 * px + py
            _rcopy(o_ref.at[ps, 1 - mc], o_ref.at[ps, 1 - mc], ssem.at[3 + j], rsem.at[3 + j], sib).wait_recv()
        for cp in first + passed:
            cp.wait_send()

    return pl.pallas_call(
        body, name=name, out_shape=_sds(full.shape, full.dtype), in_specs=[ANY], out_specs=ANY,
        input_output_aliases={0: 0},
        scratch_shapes=[pltpu.SemaphoreType.DMA((6,)), pltpu.SemaphoreType.DMA((6,))],
    )(full)


def sibling_send_other_half(buf, name):
    def body(x_ref, o_ref, ssem, rsem):
        mx, my, mc = _me()
        sib = (mx, my, 1 - mc)
        cps = [_rcopy(x_ref.at[k, 1 - mc], o_ref.at[k], ssem.at[k], rsem.at[k], sib) for k in range(N_CHIP)]
        for cp in cps:
            cp.start()
        for cp in cps:
            cp.wait_recv()
        for cp in cps:
            cp.wait_send()

    n, _, h, w = buf.shape
    return pl.pallas_call(
        body, name=name, out_shape=_sds((n, h, w), buf.dtype), in_specs=[ANY], out_specs=ANY,
        scratch_shapes=[pltpu.SemaphoreType.DMA((N_CHIP,)), pltpu.SemaphoreType.DMA((N_CHIP,))],
    )(buf)


def chips_all_to_all(q, name):
    def body(x_ref, o_ref, ssem, rsem):
        mx, my, mc = _me()
        s = 2 * mx + my
        chips = [(1 - mx, my), (mx, 1 - my), (1 - mx, 1 - my)]
        cps = [_rcopy(x_ref.at[2 * px + py], o_ref.at[s], ssem.at[j], rsem.at[j], (px, py, mc))
               for j, (px, py) in enumerate(chips)]
        for cp in cps:
            cp.start()
        for j, (px, py) in enumerate(chips):
            ps = 2 * px + py
            _rcopy(x_ref.at[ps], o_ref.at[ps], ssem.at[j], rsem.at[j], (px, py, mc)).wait_recv()
        for cp in cps:
            cp.wait_send()

    return pl.pallas_call(
        body, name=name, out_shape=_sds(q.shape, q.dtype), in_specs=[ANY], out_specs=ANY,
        scratch_shapes=[pltpu.SemaphoreType.DMA((3,)), pltpu.SemaphoreType.DMA((3,))],
    )(q)


def sibling_join_halves(both, name, g=None):
    def body(x_ref, o_ref, ssem, rsem):
        del x_ref
        mx, my, mc = _me()
        sib = (mx, my, 1 - mc)
        o = o_ref if g is None else o_ref.at[g]
        cp = _rcopy(o.at[mc], o.at[mc], ssem, rsem, sib)
        cp.start()
        _rcopy(o.at[1 - mc], o.at[1 - mc], ssem, rsem, sib).wait_recv()
        cp.wait_send()

    return pl.pallas_call(
        body, name=name, out_shape=_sds(both.shape, both.dtype), in_specs=[ANY], out_specs=ANY,
        input_output_aliases={0: 0}, scratch_shapes=[pltpu.SemaphoreType.DMA, pltpu.SemaphoreType.DMA],
    )(both)


def add_own_half(buf, recv, wire, name):
    n, _, h, w = buf.shape
    th = _pick(h, _ROW_BLOCKS)

    def body(c_ref, a_ref, b_ref, o_ref):
        del c_ref
        o_ref[...] = (a_ref[...] + b_ref[...]).astype(o_ref.dtype)

    return pl.pallas_call(
        body, name=name, out_shape=_sds((n, h, w), wire),
        grid_spec=pltpu.PrefetchScalarGridSpec(
            num_scalar_prefetch=1, grid=(n, h // th),
            in_specs=[pl.BlockSpec((None, None, th, w), lambda k, r, c: (k, c[0], r, 0)),
                      pl.BlockSpec((None, th, w), lambda k, r, c: (k, r, 0))],
            out_specs=pl.BlockSpec((None, th, w), lambda k, r, c: (k, r, 0))),
    )(_idx(lax.axis_index("c")), buf, recv)


def sum_slots(q, r, name, dst=None, g=None):
    n, h, w = r.shape
    th = _pick(h, _ROW_BLOCKS)

    def body(i_ref, q_ref, r1, r2, r3, *rest):
        del i_ref
        rest[-1][...] = ((q_ref[...].astype(F32) + r1[...].astype(F32)) + r2[...].astype(F32)) + r3[...].astype(F32)

    def slot(d):
        return lambda i, ix: ((ix[0] + d) % N_CHIP, i, 0)

    idx = jnp.stack([2 * lax.axis_index("x") + lax.axis_index("y"), lax.axis_index("c")]).astype(jnp.int32)
    in_specs = [pl.BlockSpec((None, th, w), slot(d)) for d in (0, 1, 2, 3)]
    if dst is None:
        return pl.pallas_call(
            body, name=name, out_shape=_sds((2, h, w), F32),
            grid_spec=pltpu.PrefetchScalarGridSpec(
                num_scalar_prefetch=1, grid=(h // th,), in_specs=in_specs,
                out_specs=pl.BlockSpec((None, th, w), lambda i, ix: (ix[1], i, 0))),
        )(idx, q, r, r, r)
    return pl.pallas_call(
        body, name=name, out_shape=_sds(dst.shape, F32),
        grid_spec=pltpu.PrefetchScalarGridSpec(
            num_scalar_prefetch=1, grid=(h // th,), in_specs=in_specs + [ANY],
            out_specs=pl.BlockSpec((None, None, th, w), lambda i, ix: (g, ix[1], i, 0))),
        input_output_aliases={5: 0},
    )(idx, q, r, r, r, dst)


def reduce_scatter_chips(buf, tag, wire=F32, dst=None, g=None):
    recv = sibling_send_other_half(buf, f"rs_sib_{tag}")
    q = add_own_half(buf, recv, wire, f"rs_add2_{tag}")
    r = chips_all_to_all(q, f"rs_a2a_{tag}")
    red = sum_slots(q, r, f"rs_add4_{tag}", dst=dst, g=g)
    return sibling_join_halves(red, f"rs_join_{tag}", g=g)


class Layout:
    def __init__(self, n_ctx, n_lat):
        self.C, self.L = n_ctx, n_lat
        self.PS = n_ctx + n_lat
        self.T = 2 * self.PS
        self.tr = _pick(math.gcd(n_ctx, n_lat), (256, 128))
        self.bps = self.PS // self.tr
        self.cb = n_ctx // self.tr
        self.nblk = self.T // self.tr
        self.tm = _pick(self.T, (1152, 768, 512, 256, 128))
        self.tc = _pick(self.T, (512, 256, 128))

    def seg(self, i):
        return jnp.where(i % self.bps < self.cb, 2, i // self.bps)


def rowwise(lay, name, fn, rows, segs=(), vecs=(), outs=(), sums=(), rider=None):
    tr, nblk = lay.tr, lay.nblk
    n_r, n_s, n_v, n_o = len(rows), len(segs), len(vecs), len(outs)

    def body(*refs):
        ins = refs[:n_r + n_s + n_v]
        ors = refs[n_r + n_s + n_v:]
        vals = [r[...] for r in ins[:n_r]] + [r[0] for r in ins[n_r:n_r + n_s]] + [r[...] for r in ins[n_r + n_s:]]
        res = fn(*vals)
        for k in range(n_o):
            ors[k][...] = res[k].astype(ors[k].dtype)
        for k in range(len(sums)):
            ors[n_o + k][0] = res[n_o + k]

    def all_rows(i):
        return (i, 0)

    def lat_rows(i):
        return ((i // lay.bps) * (lay.bps - lay.cb) + jnp.maximum(i % lay.bps - lay.cb, 0), 0)

    in_specs = [pl.BlockSpec((tr, a.shape[1]), all_rows if a.shape[0] == lay.T else lat_rows) for a in rows]
    in_specs += [pl.BlockSpec((1,) + a.shape[1:], lambda i: (lay.seg(i), 0, 0)) for a in segs]
    in_specs += [pl.BlockSpec(a.shape, lambda i: (0, 0)) for a in vecs]
    out_shape = [_sds((2 * lay.L if o[2:] else lay.T, o[0]), o[1]) for o in outs]
    out_shape += [_sds((nblk, r, w), F32) for r, w in sums]
    out_specs = [pl.BlockSpec((tr, o[0]), lat_rows if o[2:] else all_rows) for o in outs]
    out_specs += [pl.BlockSpec((1, r, w), lambda i: (i, 0, 0)) for r, w in sums]
    sem = "arbitrary" if any(o[2:] for o in outs) else "parallel"
    if rider is None:
        return pl.pallas_call(body, name=name, out_shape=out_shape, grid=(nblk,), in_specs=in_specs,
                              out_specs=out_specs, compiler_params=_cp((sem,)))(*rows, *segs, *vecs)
    return _host_call(body, rider, name, (nblk,), in_specs, out_specs, out_shape, (*rows, *segs, *vecs), (sem,),
                      n_r + n_s + n_v, n_o + len(sums))


def modulate(lay, h, mod, k_shift, k_scale, name):
    def fn(hb, m):
        return (hb * (1.0 + m[k_scale:k_scale + 1]) + m[k_shift:k_shift + 1],)
    return rowwise(lay, name, fn, [h], segs=[mod], outs=[(D, BF16)])[0]


def resid_ln(lay, h, y, mod, k_gate, coef, lnv, name, nxt=None, rider=None):
    def fn(hb, yb, m, *rest):
        ln = rest[-1]
        z = ALPHA * hb + (coef * m[k_gate:k_gate + 1]) * yb
        mu = jnp.mean(z, axis=-1, keepdims=True)
        zc = z - mu
        var = jnp.mean(zc * zc, axis=-1, keepdims=True)
        rstd = lax.rsqrt(var + LN_EPS)
        xhat = zc * rstd
        out = xhat * ln[0:1] + ln[1:2]
        if nxt is None:
            return out, xhat, rstd
        mn = rest[0]
        return out, xhat, rstd, out * (1.0 + mn[nxt[2]:nxt[2] + 1]) + mn[nxt[1]:nxt[1] + 1]
    segs = [mod] if nxt is None else [mod, nxt[0]]
    outs = [(D, F32), (D, F32), (1, F32)] + ([] if nxt is None else [(D, BF16)])
    return rowwise(lay, name, fn, [h, y], segs=segs, vecs=[lnv], outs=outs, rider=rider)


def _ln_bwd_math(do, xh, rs, yb, gate, coef, ln):
    dxh = do * ln[0:1]
    m1 = jnp.mean(dxh, axis=-1, keepdims=True)
    m2 = jnp.mean(dxh * xh, axis=-1, keepdims=True)
    dz = rs * (dxh - m1 - xh * m2)
    s = jnp.concatenate([jnp.sum(do, axis=0, keepdims=True), jnp.sum(do * xh, axis=0, keepdims=True),
                         jnp.sum(coef * dz * yb, axis=0, keepdims=True)], axis=0)
    return (coef * gate) * dz, ALPHA * dz, s


def _mod_bwd_math(dr, dm, hb, scale):
    s = jnp.concatenate([jnp.sum(dm, axis=0, keepdims=True), jnp.sum(dm * hb, axis=0, keepdims=True)], axis=0)
    return dr + dm * (1.0 + scale), s


def ln_bwd(lay, dout, xhat, rstd, y, mod, k_gate, coef, lnv, name):
    def fn(do, xh, rs, yb, m, ln):
        return _ln_bwd_math(do, xh, rs, yb, m[k_gate:k_gate + 1], coef, ln)
    return rowwise(lay, name, fn, [dout, xhat, rstd, y], segs=[mod], vecs=[lnv],
                   outs=[(D, BF16), (D, F32)], sums=[(3, D)])


def mod_bwd(lay, dres, dhm, h, mod, k_scale, name, rider=None):
    def fn(dr, dm, hb, m):
        return _mod_bwd_math(dr, dm, hb, m[k_scale:k_scale + 1])
    return rowwise(lay, name, fn, [dres, dhm, h], segs=[mod], outs=[(D, F32, "lat")], sums=[(2, D)], rider=rider)


def modb_lnb(lay, dres, dhm, h, mod, k_scale, xhat, rstd, y, mod_p, k_gate, coef, lnv, name, rider=None):
    def fn(dr, dm, hb, xh, rs, yb, m, mp, ln):
        dh, s2 = _mod_bwd_math(dr, dm, hb, m[k_scale:k_scale + 1])
        dy, dres_p, s1 = _ln_bwd_math(dh, xh, rs, yb, mp[k_gate:k_gate + 1], coef, ln)
        return dy, dres_p, s1, s2
    return rowwise(lay, name, fn, [dres, dhm, h, xhat, rstd, y], segs=[mod, mod_p], vecs=[lnv],
                   outs=[(D, BF16), (D, F32)], sums=[(3, D), (2, D)], rider=rider)


def block_sums(lay, parts, name):
    nblk, r, w = parts.shape

    def body(p_ref, o_ref):
        acc = [None, None, None]
        for i in range(nblk):
            sg = 2 if i % lay.bps < lay.cb else i // lay.bps
            acc[sg] = p_ref[i] if acc[sg] is None else acc[sg] + p_ref[i]
        for k in range(3):
            o_ref[k] = acc[k]
        o_ref[3] = (acc[0] + acc[1]) + acc[2]

    return pl.pallas_call(body, name=name, out_shape=_sds((4, r, w), F32), in_specs=[VMEM_SPEC],
                          out_specs=VMEM_SPEC)(parts)


def mm_nn(a, b, name, out_dtype=F32, bias=None):
    m, k = a.shape
    n = b.shape[1]
    tm = _pick(m, (1152, 768, 512, 256, 128, 64, 32, 16, 8))
    tn = _pick(n, (1024, 768, 640, 512, 384, 256, 128))

    def body(*refs):
        if bias is None:
            a_ref, b_ref, o_ref = refs
            o_ref[...] = _nn(a_ref[...].astype(BF16), b_ref[...].astype(BF16)).astype(o_ref.dtype)
        else:
            a_ref, b_ref, c_ref, o_ref = refs
            o_ref[...] = (_nn(a_ref[...].astype(BF16), b_ref[...].astype(BF16)) + c_ref[...]).astype(o_ref.dtype)

    in_specs = [pl.BlockSpec((tm, k), lambda i, j: (i, 0)), pl.BlockSpec((k, tn), lambda i, j: (0, j))]
    ops = [a, b]
    if bias is not None:
        in_specs.append(pl.BlockSpec((1, tn), lambda i, j: (0, j)))
        ops.append(bias)
    return pl.pallas_call(body, name=name, out_shape=_sds((m, n), out_dtype), grid=(m // tm, n // tn),
                          in_specs=in_specs, out_specs=pl.BlockSpec((tm, tn), lambda i, j: (i, j)),
                          compiler_params=_cp(("parallel", "parallel"), VMEM_BIG))(*ops)


def mm_nt(a, b, name, out_dtype=F32):
    m, k = a.shape
    n = b.shape[0]
    tm = _pick(m, (1152, 768, 512, 256, 128, 64, 32, 16, 8))
    tn = _pick(n, (1024, 768, 640, 512, 384, 256, 128))

    def body(a_ref, b_ref, o_ref):
        o_ref[...] = _nt(a_ref[...].astype(BF16), b_ref[...].astype(BF16)).astype(o_ref.dtype)

    return pl.pallas_call(body, name=name, out_shape=_sds((m, n), out_dtype), grid=(m // tm, n // tn),
                          in_specs=[pl.BlockSpec((tm, k), lambda i, j: (i, 0)), pl.BlockSpec((tn, k), lambda i, j: (j, 0))],
                          out_specs=pl.BlockSpec((tm, tn), lambda i, j: (i, j)),
                          compiler_params=_cp(("parallel", "parallel"), VMEM_BIG))(a, b)


def mm_tn(a, b, name):
    t, m = a.shape
    n = b.shape[1]
    tk = _pick(t, (1152, 768, 512, 256, 128, 64, 32, 16))
    tm = _pick(m, (512, 384, 256, 128))

    def body(a_ref, b_ref, o_ref):
        @pl.when(pl.program_id(1) == 0)
        def _():
            o_ref[...] = jnp.zeros_like(o_ref)
        o_ref[...] += _tn(a_ref[...].astype(BF16), b_ref[...].astype(BF16))

    return pl.pallas_call(body, name=name, out_shape=_sds((m, n), F32), grid=(m // tm, t // tk),
                          in_specs=[pl.BlockSpec((tk, tm), lambda i, k: (k, i)), pl.BlockSpec((tk, n), lambda i, k: (k, 0))],
                          out_specs=pl.BlockSpec((tm, n), lambda i, k: (i, 0)),
                          compiler_params=_cp(("parallel", "arbitrary"), VMEM_BIG))(a, b)


class Rider:
    def __init__(self, ins, outs, aliases, nsem, start, wait):
        self.ins, self.outs, self.aliases, self.nsem, self.start, self.wait = ins, outs, aliases, nsem, start, wait


def _chips_of(mx, my):
    return [(1 - mx, my), (mx, 1 - my), (1 - mx, 1 - my)]


def rider_gather_d2d(buf):
    def start(ins, outs, ssem, rsem):
        o = outs[0]
        mx, my, mc = _me()
        for j, (px, py) in enumerate(_chips_of(mx, my)):
            ps = 2 * px + py
            _rcopy(o.at[ps, mc], o.at[ps, mc], ssem.at[j], rsem.at[j], (mx, my, 1 - mc)).start()

    def wait(ins, outs, ssem, rsem):
        o = outs[0]
        mx, my, mc = _me()
        sib = (mx, my, 1 - mc)
        for j, (px, py) in enumerate(_chips_of(mx, my)):
            ps = 2 * px + py
            _rcopy(o.at[ps, 1 - mc], o.at[ps, 1 - mc], ssem.at[j], rsem.at[j], sib).wait_recv()
        for j, (px, py) in enumerate(_chips_of(mx, my)):
            ps = 2 * px + py
            _rcopy(o.at[ps, mc], o.at[ps, mc], ssem.at[j], rsem.at[j], sib).wait_send()

    return Rider([buf], [_sds(buf.shape, buf.dtype)], {0: 0}, 3, start, wait)


def rider_reduce_sib(buf):
    n, _, h, w = buf.shape

    def start(ins, outs, ssem, rsem):
        mx, my, mc = _me()
        for k in range(N_CHIP):
            _rcopy(ins[0].at[k, 1 - mc], outs[0].at[k], ssem.at[k], rsem.at[k], (mx, my, 1 - mc)).start()

    def wait(ins, outs, ssem, rsem):
        mx, my, mc = _me()
        for k in range(N_CHIP):
            _rcopy(ins[0].at[k, 1 - mc], outs[0].at[k], ssem.at[k], rsem.at[k], (mx, my, 1 - mc)).wait_recv()
        for k in range(N_CHIP):
            _rcopy(ins[0].at[k, 1 - mc], outs[0].at[k], ssem.at[k], rsem.at[k], (mx, my, 1 - mc)).wait_send()

    return Rider([buf], [_sds((n, h, w), buf.dtype)], {}, N_CHIP, start, wait)


def rider_gather_xy(buf):
    def peers():
        mx, my, mc = _me()
        return 2 * mx + my, mc, [(1 - mx, my), (mx, 1 - my)]

    def start(ins, outs, ssem, rsem):
        o = outs[0]
        s, mc, nb = peers()
        for j, (px, py) in enumerate(nb):
            _rcopy(o.at[s, mc], o.at[s, mc], ssem.at[j], rsem.at[j], (px, py, mc)).start()

    def wait(ins, outs, ssem, rsem):
        o = outs[0]
        s, mc, nb = peers()
        for j, (px, py) in enumerate(nb):
            _rcopy(o.at[2 * px + py, mc], o.at[2 * px + py, mc], ssem.at[j], rsem.at[j], (px, py, mc)).wait_recv()
        for j, (px, py) in enumerate(nb):
            _rcopy(o.at[s, mc], o.at[s, mc], ssem.at[j], rsem.at[j], (px, py, mc)).wait_send()

    return Rider([buf], [_sds(buf.shape, buf.dtype)], {0: 0}, 2, start, wait)


def rider_gather_fwd(buf):
    def start(ins, outs, ssem, rsem):
        o = outs[0]
        mx, my, mc = _me()
        xs = 2 * (1 - mx) + my
        _rcopy(o.at[xs, mc], o.at[xs, mc], ssem.at[0], rsem.at[0], (mx, 1 - my, mc)).start()

    def wait(ins, outs, ssem, rsem):
        o = outs[0]
        mx, my, mc = _me()
        xs, ds = 2 * (1 - mx) + my, 2 * (1 - mx) + (1 - my)
        _rcopy(o.at[ds, mc], o.at[ds, mc], ssem.at[0], rsem.at[0], (mx, 1 - my, mc)).wait_recv()
        _rcopy(o.at[xs, mc], o.at[xs, mc], ssem.at[0], rsem.at[0], (mx, 1 - my, mc)).wait_send()

    return Rider([buf], [_sds(buf.shape, buf.dtype)], {0: 0}, 1, start, wait)


def rider_reduce_copy(q, j, r=None):
    def peer():
        mx, my, mc = _me()
        px, py = _chips_of(mx, my)[j]
        return 2 * mx + my, 2 * px + py, (px, py, mc)

    def start(ins, outs, ssem, rsem):
        s, ps, dev = peer()
        _rcopy(ins[0].at[ps], outs[0].at[s], ssem.at[0], rsem.at[0], dev).start()

    def wait(ins, outs, ssem, rsem):
        s, ps, dev = peer()
        _rcopy(ins[0].at[ps], outs[0].at[ps], ssem.at[0], rsem.at[0], dev).wait_recv()
        _rcopy(ins[0].at[ps], outs[0].at[s], ssem.at[0], rsem.at[0], dev).wait_send()

    if r is None:
        return Rider([q], [_sds(q.shape, q.dtype)], {}, 1, start, wait)
    return Rider([q, r], [_sds(q.shape, q.dtype)], {1: 0}, 1, start, wait)


def rider_join(buf, g):
    def start(ins, outs, ssem, rsem):
        o = outs[0].at[g]
        mx, my, mc = _me()
        _rcopy(o.at[mc], o.at[mc], ssem.at[0], rsem.at[0], (mx, my, 1 - mc)).start()

    def wait(ins, outs, ssem, rsem):
        o = outs[0].at[g]
        mx, my, mc = _me()
        _rcopy(o.at[1 - mc], o.at[1 - mc], ssem.at[0], rsem.at[0], (mx, my, 1 - mc)).wait_recv()
        _rcopy(o.at[mc], o.at[mc], ssem.at[0], rsem.at[0], (mx, my, 1 - mc)).wait_send()

    return Rider([buf], [_sds(buf.shape, buf.dtype)], {0: 0}, 1, start, wait)


def _host_call(body, rider, name, grid, in_specs, out_specs, out_shape, operands, sem, n_in, n_out, aliases=None):
    aliases = dict(aliases or {})
    if rider is None:
        return pl.pallas_call(body, name=name, out_shape=out_shape, grid=grid, in_specs=in_specs, out_specs=out_specs,
                              input_output_aliases=aliases, compiler_params=_cp(sem, VMEM_BIG))(*operands)
    n_ri, n_ro = len(rider.ins), len(rider.outs)
    aliases.update({n_in + a: n_out + b for a, b in rider.aliases.items()})

    def hosted(*refs):
        ins, r_in = refs[:n_in], refs[n_in:n_in + n_ri]
        outs, r_out = refs[n_in + n_ri:n_in + n_ri + n_out], refs[n_in + n_ri + n_out:n_in + n_ri + n_out + n_ro]
        ssem, rsem = refs[-2], refs[-1]
        first = functools.reduce(lambda a, b: a & b, [pl.program_id(k) == 0 for k in range(len(grid))])
        last = functools.reduce(lambda a, b: a & b, [pl.program_id(k) == grid[k] - 1 for k in range(len(grid))])

        @pl.when(first)
        def _():
            rider.start(r_in, r_out, ssem, rsem)
        body(*ins, *outs)

        @pl.when(last)
        def _():
            rider.wait(r_in, r_out, ssem, rsem)

    return pl.pallas_call(
        hosted, name=name, out_shape=list(out_shape) + list(rider.outs), grid=grid,
        in_specs=list(in_specs) + [ANY] * n_ri, out_specs=list(out_specs) + [ANY] * n_ro,
        input_output_aliases=aliases,
        scratch_shapes=[pltpu.SemaphoreType.DMA((rider.nsem,)), pltpu.SemaphoreType.DMA((rider.nsem,))],
        compiler_params=_cp(("arbitrary",) * len(grid), VMEM_BIG))(*operands, *rider.ins)


def ffn_up(lay, hm, wbuf, ig, iu, ns, name, rider=None):
    tm = lay.tm

    def body(h_ref, wg_ref, wu_ref, sp_ref, sl_ref, u_ref, a_ref):
        hb = h_ref[...]
        g = _nt(hb, wg_ref[0])
        u = _nt(hb, wu_ref[0])
        sg = _sigmoid(g)
        sl = g * sg
        sp_ref[0] = (sg + sl * (1.0 - sg)).astype(BF16)
        sl_ref[0] = sl.astype(BF16)
        u_ref[0] = u.astype(BF16)
        a_ref[0] = (sl * u).astype(BF16)

    spec_o = pl.BlockSpec((1, tm, ns), lambda s, i: (s, i, 0))
    return _host_call(
        body, rider, name, (N_CHIP, lay.T // tm),
        [pl.BlockSpec((tm, D), lambda s, i: (i, 0)), pl.BlockSpec((1, ns, D), lambda s, i: (s, ig, 0)),
         pl.BlockSpec((1, ns, D), lambda s, i: (s, iu, 0))],
        [spec_o] * 4, [_sds((N_CHIP, lay.T, ns), BF16)] * 4, (hm, wbuf, wbuf), ("parallel", "parallel"), 3, 4)


def slab_nn_acc(lay, zs, wbuf, idxs, ns, name, rider=None):
    tm = lay.tm
    npair = len(zs)

    def body(*refs):
        o_ref = refs[-1]

        @pl.when(pl.program_id(1) == 0)
        def _():
            o_ref[...] = jnp.zeros_like(o_ref)
        acc = _nn(refs[0][0], refs[npair][0])
        for p in range(1, npair):
            acc += _nn(refs[p][0], refs[npair + p][0])
        o_ref[...] += acc

    in_specs = [pl.BlockSpec((1, tm, ns), lambda i, s: (s, i, 0)) for _ in zs]
    in_specs += [pl.BlockSpec((1, ns, D), functools.partial(lambda i, s, q: (s, q, 0), q=q)) for q in idxs]
    return _host_call(body, rider, name, (lay.T // tm, N_CHIP), in_specs, [pl.BlockSpec((tm, D), lambda i, s: (i, 0))],
                      [_sds((lay.T, D), F32)], (*zs, *([wbuf] * npair)), ("parallel", "arbitrary"), 2 * npair, 1)


def ffn_bwd_dx(lay, dy, wbuf, sp, sl, u, ns, name, rider=None):
    tm = _pick(lay.T, (768, 512, 256, 128))

    def body(dy_ref, wd_ref, wg_ref, wu_ref, sp_ref, sl_ref, u_ref, dg_ref, du_ref, o_ref):
        @pl.when(pl.program_id(1) == 0)
        def _():
            o_ref[...] = jnp.zeros_like(o_ref)
        da = _nt(dy_ref[...], wd_ref[0])
        dg = (da * u_ref[0].astype(F32) * sp_ref[0].astype(F32)).astype(BF16)
        du = (da * sl_ref[0].astype(F32)).astype(BF16)
        dg_ref[0] = dg
        du_ref[0] = du
        o_ref[...] += _nn(dg, wg_ref[0]) + _nn(du, wu_ref[0])

    def wspec(q):
        return pl.BlockSpec((1, ns, D), lambda i, s: (s, q, 0))

    spec_z = pl.BlockSpec((1, tm, ns), lambda i, s: (s, i, 0))
    return _host_call(
        body, rider, name, (lay.T // tm, N_CHIP),
        [pl.BlockSpec((tm, D), lambda i, s: (i, 0)), wspec(2), wspec(0), wspec(1), spec_z, spec_z, spec_z],
        [spec_z, spec_z, pl.BlockSpec((tm, D), lambda i, s: (i, 0))],
        [_sds((N_CHIP, lay.T, ns), BF16)] * 2 + [_sds((lay.T, D), F32)], (dy, wbuf, wbuf, wbuf, sp, sl, u),
        ("parallel", "arbitrary"), 7, 3)


def slab_tn(lay, z, x, gbuf, idx, ns, name, rider=None):
    tk = lay.tm

    def body(z_ref, x_ref, g_in, o_ref):
        del g_in

        @pl.when(pl.program_id(1) == 0)
        def _():
            o_ref[...] = jnp.zeros_like(o_ref)
        o_ref[0] += _tn(z_ref[0], x_ref[...])

    return _host_call(
        body, rider, name, (N_CHIP, lay.T // tk),
        [pl.BlockSpec((1, tk, ns), lambda s, k: (s, k, 0)), pl.BlockSpec((tk, D), lambda s, k: (k, 0)), ANY],
        [pl.BlockSpec((1, ns, D), lambda s, k: (s, idx, 0))], [_sds(gbuf.shape, F32)], (z, x, gbuf),
        ("parallel", "arbitrary"), 3, 1, aliases={2: 0})


Q0, K0, V0, U0, QR0, KR0, PEXT = 0, 512, 640, 768, 1280, 1792, 1920


def rope_fwd(lay, p, cos, sin, name):
    def fn(pb, cs, sn):
        cs4 = jnp.concatenate([cs] * 4, axis=1)
        sn4 = jnp.concatenate([sn] * 4, axis=1)
        qr = pb[:, Q0:K0] * cs4 + pb[:, QR0:KR0] * sn4
        kr = pb[:, K0:V0] * cs + pb[:, KR0:PEXT] * sn
        return qr, kr, pb[:, V0:U0], pb[:, U0:QR0]
    return rowwise(lay, name, fn, [p, cos, sin], outs=[(ATT_W, BF16), (KV_W, BF16), (KV_W, BF16), (POOL_W, F32)])


def rope_bwd(lay, dqr, dkr, dv, du, cos, sin, name):
    def fn(dq, dk, dvb, dub, cs, sn):
        cs4 = jnp.concatenate([cs] * 4, axis=1)
        sn4 = jnp.concatenate([sn] * 4, axis=1)
        return (jnp.concatenate([dq * cs4, dk * cs, dvb, dub, dq * sn4, dk * sn], axis=1),)
    return rowwise(lay, name, fn, [dqr, dkr, dv, du, cos, sin], outs=[(PEXT, BF16)])[0]


def _attn_specs(lay):
    nbs, cbk, lbk = lay.PS // BLK, lay.C // BLK, lay.L // BLK

    def kv_map(j):
        return lambda s, n: (s * nbs + cbk + jnp.clip(n - cbk + j - 1, 0, lbk - 1), 0)

    win = [pl.BlockSpec((BLK, KV_W), kv_map(j)) for j in range(3)]
    ctx = pl.BlockSpec((lay.C, KV_W), lambda s, n: (s * (lay.PS // lay.C), 0))
    return nbs, cbk, lbk, win, ctx


def _attn_masks(n, cbk, lbk):
    row = lax.broadcasted_iota(jnp.int32, (BLK, BLK), 0)
    col = lax.broadcasted_iota(jnp.int32, (BLK, BLK), 1)
    m = n - cbk
    lat = n >= cbk
    valid = [lat & (m >= 1) & (col >= row), lat & (col >= 0), lat & (m <= lbk - 2) & (col <= row)]
    lane_lo = lax.broadcasted_iota(jnp.int32, (BLK, 2 * HEAD_DIM), 1) < HEAD_DIM
    return valid, lane_lo


def attn_fwd(lay, qr, kr, vb, sink_tab, name):
    nbs, cbk, lbk, win, ctx = _attn_specs(lay)

    def body(q_ref, k0, k1, k2, kc_ref, v0, v1, v2, vc_ref, sk_ref, o_ref, l_ref):
        n = pl.program_id(1)
        valid, lane_lo = _attn_masks(n, cbk, lbk)
        valid4 = [jnp.concatenate([v] * 4, axis=0) for v in valid]
        ks = [k0[...], k1[...], k2[...]]
        vs = [v0[...], v1[...], v2[...]]
        kc, vc = kc_ref[...], vc_ref[...]
        q2s = [q_ref[:, p * 128:(p + 1) * 128] for p in range(4)]
        outs, lses = [], []
        for hh in range(2):
            sel = lane_lo == (hh == 0)
            qm = jnp.concatenate([jnp.where(sel, q2, jnp.zeros_like(q2)) for q2 in q2s], axis=0)
            sk = jnp.concatenate([jnp.broadcast_to(sk_ref[p:p + 1, hh * HEAD_DIM:hh * HEAD_DIM + 1], (BLK, 1))
                                  for p in range(4)], axis=0)
            sw = [jnp.where(valid4[j], _nt(qm, ks[j]) * ATT_SCALE, NEG_INF) for j in range(3)]
            sc = _nt(qm, kc) * ATT_SCALE
            mx = jnp.maximum(jnp.maximum(jnp.maximum(sw[0].max(-1, keepdims=True), sw[1].max(-1, keepdims=True)),
                                         jnp.maximum(sw[2].max(-1, keepdims=True), sc.max(-1, keepdims=True))), sk)
            ew = [jnp.exp(s - mx) for s in sw]
            ec = jnp.exp(sc - mx)
            den = ew[0].sum(-1, keepdims=True) + ew[1].sum(-1, keepdims=True) + ew[2].sum(-1, keepdims=True)
            den = den + ec.sum(-1, keepdims=True) + jnp.exp(sk - mx)
            o = _nn((ec / den).astype(BF16), vc)
            for j in range(3):
                o += _nn((ew[j] / den).astype(BF16), vs[j])
            outs.append(o)
            lses.append(mx + jnp.log(den))
        for p in range(4):
            rows = slice(p * BLK, (p + 1) * BLK)
            o_ref[:, p * 128:(p + 1) * 128] = jnp.where(lane_lo, outs[0][rows], outs[1][rows]).astype(o_ref.dtype)
            l_ref[:, p * 128:(p + 1) * 128] = jnp.where(lane_lo, jnp.broadcast_to(lses[0][rows], (BLK, 128)),
                                                        jnp.broadcast_to(lses[1][rows], (BLK, 128)))

    qspec = pl.BlockSpec((BLK, ATT_W), lambda s, n: (s * nbs + n, 0))
    return pl.pallas_call(
        body, name=name, out_shape=[_sds((lay.T, ATT_W), BF16), _sds((lay.T, ATT_W), F32)], grid=(2, nbs),
        in_specs=[qspec] + win + [ctx] + win + [ctx] + [pl.BlockSpec((8, 128), lambda s, n: (0, 0))],
        out_specs=[qspec, qspec], compiler_params=_cp(("parallel", "parallel")))(qr, kr, kr, kr, kr, vb, vb, vb, vb, sink_tab)


def attn_bwd(lay, qr, kr, vb, sink_tab, lse, datt, name):
    nbs, cbk, lbk, win, ctx = _attn_specs(lay)
    C, PS = lay.C, lay.PS

    def body(q_ref, k0, k1, k2, kc_ref, v0, v1, v2, vc_ref, sk_ref, l_ref, do_ref, dq_ref, dk_ref, dv_ref, ds_ref):
        n = pl.program_id(1)
        valid, lane_lo = _attn_masks(n, cbk, lbk)

        @pl.when(n == 0)
        def _():
            dk_ref[...] = jnp.zeros_like(dk_ref)
            dv_ref[...] = jnp.zeros_like(dv_ref)
            ds_ref[...] = jnp.zeros_like(ds_ref)

        ks = [k0[...], k1[...], k2[...], kc_ref[...]]
        vs = [v0[...], v1[...], v2[...], vc_ref[...]]
        valid4 = [jnp.concatenate([v] * 4, axis=0) for v in valid]
        dks = [jnp.zeros((BLK, KV_W), F32)] * 3 + [jnp.zeros((C, KV_W), F32)]
        dvs = list(dks)
        q2s = [q_ref[:, p * 128:(p + 1) * 128] for p in range(4)]
        do2s = [do_ref[:, p * 128:(p + 1) * 128].astype(BF16) for p in range(4)]
        lse2s = [l_ref[:, p * 128:(p + 1) * 128] for p in range(4)]
        dq_h, dd_h = [], []
        for hh in range(2):
            sel = lane_lo == (hh == 0)
            qm = jnp.concatenate([jnp.where(sel, q2, jnp.zeros_like(q2)) for q2 in q2s], axis=0)
            dom = jnp.concatenate([jnp.where(sel, d2, jnp.zeros_like(d2)) for d2 in do2s], axis=0)
            lse_h = jnp.concatenate([l2[:, hh * HEAD_DIM:hh * HEAD_DIM + 1] for l2 in lse2s], axis=0)
            ps, dps = [], []
            for j in range(4):
                s = _nt(qm, ks[j]) * ATT_SCALE
                if j < 3:
                    s = jnp.where(valid4[j], s, NEG_INF)
                ps.append(jnp.exp(s - lse_h))
                dps.append(_nt(dom, vs[j]))
            dd = (ps[0] * dps[0]).sum(-1, keepdims=True) + (ps[1] * dps[1]).sum(-1, keepdims=True)
            dd = dd + (ps[2] * dps[2]).sum(-1, keepdims=True) + (ps[3] * dps[3]).sum(-1, keepdims=True)
            dq = jnp.zeros((4 * BLK, 128), F32)
            for j in range(4):
                dsb = (ps[j] * (dps[j] - dd) * ATT_SCALE).astype(BF16)
                dq += _nn(dsb, ks[j])
                dks[j] = dks[j] + _tn(dsb, qm)
                dvs[j] = dvs[j] + _tn(ps[j].astype(BF16), dom)
            dq_h.append(dq)
            dd_h.append(dd)
        for p in range(4):
            sl = slice(p * 128, (p + 1) * 128)
            rows = slice(p * BLK, (p + 1) * BLK)
            dq_ref[:, sl] = jnp.where(lane_lo, dq_h[0][rows], dq_h[1][rows])
            dd2 = jnp.where(lane_lo, jnp.broadcast_to(dd_h[0][rows], (BLK, 128)), jnp.broadcast_to(dd_h[1][rows], (BLK, 128)))
            psink = jnp.exp(sk_ref[p:p + 1, :] - lse2s[p])
            ds_ref[0, p:p + 1, :] += -jnp.sum(psink * dd2, axis=0, keepdims=True)
        dk_ref[0:C, :] += dks[3]
        dv_ref[0:C, :] += dvs[3]
        for j in range(3):
            r0 = pl.multiple_of((cbk + jnp.clip(n - cbk + j - 1, 0, lbk - 1)) * BLK, BLK)
            dk_ref[pl.ds(r0, BLK), :] += dks[j]
            dv_ref[pl.ds(r0, BLK), :] += dvs[j]

    qspec = pl.BlockSpec((BLK, ATT_W), lambda s, n: (s * nbs + n, 0))
    kvout = pl.BlockSpec((PS, KV_W), lambda s, n: (s, 0))
    return pl.pallas_call(
        body, name=name,
        out_shape=[_sds((lay.T, ATT_W), F32), _sds((lay.T, KV_W), F32), _sds((lay.T, KV_W), F32), _sds((2, 8, 128), F32)],
        grid=(2, nbs),
        in_specs=[qspec] + win + [ctx] + win + [ctx] + [pl.BlockSpec((8, 128), lambda s, n: (0, 0)), qspec, qspec],
        out_specs=[qspec, kvout, kvout, pl.BlockSpec((1, 8, 128), lambda s, n: (s, 0, 0))],
        compiler_params=_cp(("parallel", "arbitrary")))(qr, kr, kr, kr, kr, vb, vb, vb, vb, sink_tab, lse, datt)


def _winsum(x, r):
    n = x.shape[0]
    t = lax.broadcasted_iota(jnp.int32, x.shape, 0)
    acc = x
    for o in range(1, r + 1):
        acc = acc + jnp.where(t >= o, pltpu.roll(x, o, 0), 0.0) + jnp.where(t < n - o, pltpu.roll(x, n - o, 0), 0.0)
    return acc


def _wincount(n, r):
    t = lax.broadcasted_iota(jnp.int32, (n, 128), 0)
    return (jnp.minimum(t + r, n - 1) - jnp.maximum(t - r, 0) + 1).astype(F32)


def pool_fwd(lay, u, w_pool, scale, name):
    segs = [(0, lay.C), (lay.C, lay.L)]

    def body(u_ref, w_ref, s_ref, o_ref):
        for r0, n in segs:
            for g, wd in enumerate(POOL_WINDOWS):
                sl = slice(g * 128, (g + 1) * 128)
                x = u_ref[r0:r0 + n, sl]
                d = _winsum(x, wd // 2) / _wincount(n, wd // 2) - x
                y = _nn(d.astype(BF16), w_ref[g].astype(BF16)) * s_ref[:, sl]
                o_ref[r0:r0 + n, sl] = y.astype(o_ref.dtype)

    spec = pl.BlockSpec((lay.PS, POOL_W), lambda s: (s, 0))
    return pl.pallas_call(
        body, name=name, out_shape=_sds((lay.T, POOL_W), BF16), grid=(2,),
        in_specs=[spec, pl.BlockSpec(w_pool.shape, lambda s: (0, 0, 0)), pl.BlockSpec((1, POOL_W), lambda s: (0, 0))],
        out_specs=spec, compiler_params=_cp(("parallel",), VMEM_BIG))(u, w_pool, scale)


def pool_bwd(lay, u, dcat, w_pool, scale, name):
    segs = [(0, lay.C), (lay.C, lay.L)]

    def body(u_ref, dp_ref, w_ref, s_ref, du_ref, dw_ref, dsc_ref):
        for g, wd in enumerate(POOL_WINDOWS):
            sl = slice(g * 128, (g + 1) * 128)
            wb = w_ref[g].astype(BF16)
            dw = jnp.zeros((128, 128), F32)
            dsc = jnp.zeros((1, 128), F32)
            for r0, n in segs:
                x = u_ref[r0:r0 + n, sl]
                cnt = _wincount(n, wd // 2)
                d = (_winsum(x, wd // 2) / cnt - x).astype(BF16)
                dp = dp_ref[r0:r0 + n, sl]
                dsc += jnp.sum(_nn(d, wb) * dp, axis=0, keepdims=True)
                dyp = (dp * s_ref[:, sl]).astype(BF16)
                dw += _tn(d, dyp)
                dd = _nt(dyp, wb)
                du_ref[r0:r0 + n, sl] = _winsum(dd / cnt, wd // 2) - dd
            dw_ref[0, g] = dw
            dsc_ref[0, :, sl] = dsc

    spec = pl.BlockSpec((lay.PS, POOL_W), lambda s: (s, 0))
    return pl.pallas_call(
        body, name=name,
        out_shape=[_sds((lay.T, POOL_W), F32), _sds((2, 4, 128, 128), F32), _sds((2, 1, POOL_W), F32)], grid=(2,),
        in_specs=[spec, pl.BlockSpec((lay.PS, POOL_W), lambda s: (s, 1)), pl.BlockSpec(w_pool.shape, lambda s: (0, 0, 0)),
                  pl.BlockSpec((1, POOL_W), lambda s: (0, 0))],
        out_specs=[spec, pl.BlockSpec((1, 4, 128, 128), lambda s: (s, 0, 0, 0)), pl.BlockSpec((1, 1, POOL_W), lambda s: (s, 0, 0))],
        compiler_params=_cp(("parallel",), VMEM_BIG))(u, dcat, w_pool, scale)


CONV_OFFS = (-1, 0, 1, 2)
CW = 256


def _shift_rows(x, o):
    if o == 0:
        return x
    n = x.shape[0]
    t = lax.broadcasted_iota(jnp.int32, x.shape, 0)
    if o < 0:
        return jnp.where(t >= -o, pltpu.roll(x, -o, 0), 0.0)
    return jnp.where(t < n - o, pltpu.roll(x, n - o, 0), 0.0)


def conv_fwd(lay, p, col0, w, b, name):
    segs = [(0, lay.C), (lay.C, lay.L)]
    cb0 = col0 // CW

    def body(x_ref, w_ref, b_ref, o_ref):
        for r0, n in segs:
            x = x_ref[r0:r0 + n, :]
            y = jnp.broadcast_to(b_ref[...], x.shape)
            for k, o in enumerate(CONV_OFFS):
                y = y + _shift_rows(x, o) * w_ref[k:k + 1, :]
            o_ref[r0:r0 + n, :] = y

    return pl.pallas_call(
        body, name=name, out_shape=_sds((lay.T, D), F32), grid=(2, D // CW),
        in_specs=[pl.BlockSpec((lay.PS, CW), lambda s, j: (s, cb0 + j)), pl.BlockSpec((4, CW), lambda s, j: (0, j)),
                  pl.BlockSpec((1, CW), lambda s, j: (0, j))],
        out_specs=pl.BlockSpec((lay.PS, CW), lambda s, j: (s, j)),
        compiler_params=_cp(("parallel", "parallel")))(p, w, b)


def conv_bwd(lay, p, col0, w, duc, name):
    segs = [(0, lay.C), (lay.C, lay.L)]
    cb0 = col0 // CW

    def body(x_ref, w_ref, g_ref, du_ref, dw_ref, db_ref):
        dws = [jnp.zeros((1, CW), F32)] * 4
        db = jnp.zeros((1, CW), F32)
        for r0, n in segs:
            x = x_ref[r0:r0 + n, :]
            g = g_ref[r0:r0 + n, :]
            du = jnp.zeros_like(g)
            for k, o in enumerate(CONV_OFFS):
                du = du + _shift_rows(g, -o) * w_ref[k:k + 1, :]
                dws[k] = dws[k] + jnp.sum(g * _shift_rows(x, o), axis=0, keepdims=True)
            db = db + jnp.sum(g, axis=0, keepdims=True)
            du_ref[r0:r0 + n, :] = du.astype(du_ref.dtype)
        dw_ref[0] = jnp.concatenate(dws, axis=0)
        db_ref[0] = db

    return pl.pallas_call(
        body, name=name, out_shape=[_sds((lay.T, D), BF16), _sds((2, 4, D), F32), _sds((2, 1, D), F32)], grid=(2, D // CW),
        in_specs=[pl.BlockSpec((lay.PS, CW), lambda s, j: (s, cb0 + j)), pl.BlockSpec((4, CW), lambda s, j: (0, j)),
                  pl.BlockSpec((lay.PS, CW), lambda s, j: (s, j))],
        out_specs=[pl.BlockSpec((lay.PS, CW), lambda s, j: (s, j)), pl.BlockSpec((1, 4, CW), lambda s, j: (s, 0, j)),
                   pl.BlockSpec((1, 1, CW), lambda s, j: (s, 0, j))],
        compiler_params=_cp(("parallel", "parallel")))(p, w, duc)


def _softplus_neg(lam):
    z = -lam
    w = jnp.exp(-jnp.abs(z))
    log1p = jnp.where(w < 1e-2, w * (1.0 - w * (0.5 - w / 3.0)), jnp.log(1.0 + w))
    return jnp.maximum(z, 0.0) + log1p, -_sigmoid(z)


def _neg_expm1(x):
    series = -x * (1.0 + x * (0.5 + x * (1.0 / 6.0 + x * (1.0 / 24.0 + x * (1.0 / 120.0)))))
    return jnp.where(x > -0.05, series, 1.0 - jnp.exp(x))


def _lru_gates(x, xb, wa, wx, ba, bx, lam):
    r = _sigmoid(_nn(xb, wa.astype(BF16)) + ba)
    gi = _sigmoid(_nn(xb, wx.astype(BF16)) + bx)
    sp, dsp = _softplus_neg(lam)
    la = -LRU_C * r * sp
    a = jnp.exp(la)
    sq = jnp.sqrt(_neg_expm1(2.0 * la))
    return r, gi, sp, dsp, a, sq


def lru_coeffs(lay, uc, wa, wx, vec, name):
    tr = lay.tc

    def body(x_ref, wa_ref, wx_ref, v_ref, a_ref, b_ref):
        for h in range(8):
            sl = slice(h * 128, (h + 1) * 128)
            x = x_ref[:, sl]
            xb = x.astype(BF16)
            for d in range(2):
                _, gi, _, _, a, sq = _lru_gates(x, xb, wa_ref[d, h], wx_ref[d, h], v_ref[d:d + 1, sl],
                                                v_ref[2 + d:3 + d, sl], v_ref[4 + d:5 + d, sl])
                a_ref[d, h] = a
                b_ref[d, h] = sq * (gi * x)

    wspec = pl.BlockSpec((2, 8, 128, 128), lambda i: (0, 0, 0, 0))
    ospec = pl.BlockSpec((2, 8, tr, 128), lambda i: (0, 0, i, 0))
    return pl.pallas_call(
        body, name=name, out_shape=[_sds((2, 8, lay.T, 128), F32)] * 2, grid=(lay.T // tr,),
        in_specs=[pl.BlockSpec((tr, D), lambda i: (i, 0)), wspec, wspec, pl.BlockSpec((6, D), lambda i: (0, 0))],
        out_specs=[ospec, ospec], compiler_params=_cp(("parallel",), VMEM_BIG))(uc, wa, wx, vec)


def lru_coeffs_bwd(lay, uc, wa, wx, vec, da, db, name):
    tr = lay.tc

    def body(x_ref, wa_ref, wx_ref, v_ref, da_ref, db_ref, dx_ref, dwa_ref, dwx_ref, dv_ref):
        @pl.when(pl.program_id(0) == 0)
        def _():
            dwa_ref[...] = jnp.zeros_like(dwa_ref)
            dwx_ref[...] = jnp.zeros_like(dwx_ref)
            dv_ref[...] = jnp.zeros_like(dv_ref)

        for h in range(8):
            sl = slice(h * 128, (h + 1) * 128)
            x = x_ref[:, sl]
            xb = x.astype(BF16)
            dx = jnp.zeros_like(x)
            for d in range(2):
                wab, wxb = wa_ref[d, h].astype(BF16), wx_ref[d, h].astype(BF16)
                r, gi, sp, dsp, a, sq = _lru_gates(x, xb, wa_ref[d, h], wx_ref[d, h], v_ref[d:d + 1, sl],
                                                   v_ref[2 + d:3 + d, sl], v_ref[4 + d:5 + d, sl])
                dbv, dav = db_ref[d, h], da_ref[d, h]
                t1 = dbv * sq
                dgi = t1 * x
                dx = dx + t1 * gi
                dla = dav * a - (dbv * gi * x) * (a * a) / sq
                dr = dla * (-LRU_C * sp)
                dlam = jnp.sum(dla * (-LRU_C * r), axis=0, keepdims=True) * dsp
                dpa = dr * r * (1.0 - r)
                dpx = dgi * gi * (1.0 - gi)
                dpab, dpxb = dpa.astype(BF16), dpx.astype(BF16)
                dwa_ref[d, h] += _tn(xb, dpab)
                dwx_ref[d, h] += _tn(xb, dpxb)
                dx = dx + _nt(dpab, wab) + _nt(dpxb, wxb)
                dv_ref[d:d + 1, sl] += jnp.sum(dpa, axis=0, keepdims=True)
                dv_ref[2 + d:3 + d, sl] += jnp.sum(dpx, axis=0, keepdims=True)
                dv_ref[4 + d:5 + d, sl] += dlam
            dx_ref[:, sl] = dx

    wspec = pl.BlockSpec((2, 8, 128, 128), lambda i: (0, 0, 0, 0))
    gspec = pl.BlockSpec((2, 8, tr, 128), lambda i: (0, 0, i, 0))
    vspec = pl.BlockSpec((6, D), lambda i: (0, 0))
    xspec = pl.BlockSpec((tr, D), lambda i: (i, 0))
    return pl.pallas_call(
        body, name=name,
        out_shape=[_sds((lay.T, D), F32), _sds((2, 8, 128, 128), F32), _sds((2, 8, 128, 128), F32), _sds((6, D), F32)],
        grid=(lay.T // tr,), in_specs=[xspec, wspec, wspec, vspec, gspec, gspec],
        out_specs=[xspec, wspec, wspec, vspec], compiler_params=_cp(("arbitrary",), VMEM_BIG))(uc, wa, wx, vec, da, db)


GB = 2
SCAN_UNROLL = 4


def _tile_scan(a, b, up):
    t = lax.broadcasted_iota(jnp.int32, a.shape, 0)
    for d in (1, 2, 4):
        sh = 8 - d if up else d
        m = (t < 8 - d) if up else (t >= d)
        a_prev, b_prev = pltpu.roll(a, sh, 0), pltpu.roll(b, sh, 0)
        b = jnp.where(m, a * b_prev + b, b)
        a = jnp.where(m, a * a_prev, a)
    return a, b


def lru_scan(lay, a, b, name):
    segs = [(0, lay.C), (lay.C, lay.L)]

    def body(a_ref, b_ref, s_ref):
        for d in range(2):
            rev = d == 1
            state = tuple(jnp.zeros((1, 128), F32) for _ in range(GB))
            for base, n in segs:
                nt = n // 8

                def step(j, c, base=base, nt=nt, rev=rev, d=d):
                    c = list(c)
                    for u in range(SCAN_UNROLL):
                        jj = j * SCAN_UNROLL + u
                        r0 = pl.multiple_of(base + 8 * ((nt - 1 - jj) if rev else jj), 8)
                        for g in range(GB):
                            at, bt = _tile_scan(a_ref[d, g, pl.ds(r0, 8), :], b_ref[d, g, pl.ds(r0, 8), :], rev)
                            h = at * c[g] + bt
                            s_ref[d, g, pl.ds(r0, 8), :] = h
                            c[g] = h[0:1] if rev else h[7:8]
                    return tuple(c)

                state = lax.fori_loop(0, nt // SCAN_UNROLL, step, state)

    spec = pl.BlockSpec((2, GB, lay.PS, 128), lambda s, hb: (0, hb, s, 0))
    return pl.pallas_call(
        body, name=name, out_shape=_sds((2, 8, lay.T, 128), F32), grid=(2, 8 // GB),
        in_specs=[spec, spec], out_specs=spec, compiler_params=_cp(("parallel", "parallel"), VMEM_BIG))(a, b)


def lru_scan_bwd(lay, a, s, dy, name):
    segs = [(0, lay.C), (lay.C, lay.L)]
    C, PS = lay.C, lay.PS

    def body(a_ref, s_ref, g_ref, da_ref, db_ref):
        t = lax.broadcasted_iota(jnp.int32, (8, 128), 0)
        for d in range(2):
            rev = d == 1
            carry = tuple(jnp.zeros((1, 128), F32) for _ in range(GB))
            for si in (1, 0):
                base, n = segs[si]
                nt = n // 8

                def step(j, c, base=base, nt=nt, rev=rev, d=d):
                    c = list(c)
                    for u in range(SCAN_UNROLL):
                        jj = j * SCAN_UNROLL + u
                        r0 = pl.multiple_of(base + 8 * (jj if rev else (nt - 1 - jj)), 8)
                        if rev:
                            rn = pl.multiple_of(jnp.where(r0 == PS - 8, 0, r0 + 8), 8)
                            nb_zero = r0 == C - 8
                        else:
                            rn = pl.multiple_of(jnp.maximum(r0 - 8, 0), 8)
                            nb_zero = r0 == 0
                        for g in range(GB):
                            av = a_ref[d, g, pl.ds(r0, 8), :]
                            gv = g_ref[g, pl.ds(r0, 8), :]
                            sv = s_ref[d, g, pl.ds(r0, 8), :]
                            nbt = s_ref[d, g, pl.ds(rn, 8), :]
                            at, bt = _tile_scan(av, av * gv, not rev)
                            m = at * c[g] + bt
                            if rev:
                                m_next = jnp.where(t >= 1, pltpu.roll(m, 1, 0), c[g])
                                nb = jnp.where(nb_zero, 0.0, nbt[0:1])
                                h_prev = jnp.where(t < 7, pltpu.roll(sv, 7, 0), nb)
                                c[g] = m[7:8]
                            else:
                                m_next = jnp.where(t < 7, pltpu.roll(m, 7, 0), c[g])
                                nb = jnp.where(nb_zero, 0.0, nbt[7:8])
                                h_prev = jnp.where(t >= 1, pltpu.roll(sv, 1, 0), nb)
                                c[g] = m[0:1]
                            lam = gv + m_next
                            db_ref[d, g, pl.ds(r0, 8), :] = lam
                            da_ref[d, g, pl.ds(r0, 8), :] = lam * h_prev
                    return tuple(c)

                carry = lax.fori_loop(0, nt // SCAN_UNROLL, step, carry)

    spec = pl.BlockSpec((2, GB, lay.PS, 128), lambda s, hb: (0, hb, s, 0))
    return pl.pallas_call(
        body, name=name, out_shape=[_sds((2, 8, lay.T, 128), F32)] * 2, grid=(2, 8 // GB),
        in_specs=[spec, spec, pl.BlockSpec((GB, lay.PS, 128), lambda s, hb: (hb, s, 0))],
        out_specs=[spec, spec], compiler_params=_cp(("parallel", "parallel"), VMEM_BIG))(a, s, dy)


def _gelu(x):
    k = math.sqrt(2.0 / math.pi)
    t = jnp.tanh(k * (x + 0.044715 * x * x * x))
    return 0.5 * x * (1.0 + t), 0.5 * (1.0 + t) + 0.5 * x * (1.0 - t * t) * k * (1.0 + 3 * 0.044715 * x * x)


def lru_gate(lay, p, s, name):
    tr = lay.tr

    def body(g_ref, s_ref, o_ref):
        for h in range(8):
            sl = slice(h * 128, (h + 1) * 128)
            o_ref[:, sl] = (_gelu(g_ref[:, sl])[0] * (s_ref[0, h] + s_ref[1, h])).astype(o_ref.dtype)

    return pl.pallas_call(
        body, name=name, out_shape=_sds((lay.T, D), BF16), grid=(lay.nblk,),
        in_specs=[pl.BlockSpec((tr, D), lambda i: (i, 0)), pl.BlockSpec((2, 8, tr, 128), lambda i: (0, 0, i, 0))],
        out_specs=pl.BlockSpec((tr, D), lambda i: (i, 0)), compiler_params=_cp(("parallel",)))(p, s)


def lru_gate_bwd(lay, p, s, do, name):
    tr = lay.tr

    def body(g_ref, s_ref, do_ref, dg_ref, dy_ref):
        for h in range(8):
            sl = slice(h * 128, (h + 1) * 128)
            ge, dge = _gelu(g_ref[:, sl])
            dov = do_ref[:, sl]
            dg_ref[:, sl] = (dov * (s_ref[0, h] + s_ref[1, h]) * dge).astype(dg_ref.dtype)
            dy_ref[h] = dov * ge

    xspec = pl.BlockSpec((tr, D), lambda i: (i, 0))
    return pl.pallas_call(
        body, name=name, out_shape=[_sds((lay.T, D), BF16), _sds((8, lay.T, 128), F32)], grid=(lay.nblk,),
        in_specs=[xspec, pl.BlockSpec((2, 8, tr, 128), lambda i: (0, 0, i, 0)), xspec],
        out_specs=[xspec, pl.BlockSpec((8, tr, 128), lambda i: (0, i, 0))],
        compiler_params=_cp(("parallel",)))(p, s, do)


def silu_rows(x, name):
    def body(x_ref, o_ref):
        v = x_ref[...]
        o_ref[...] = (v * _sigmoid(v)).astype(o_ref.dtype)
    return pl.pallas_call(body, name=name, out_shape=_sds(x.shape, BF16), in_specs=[VMEM_SPEC], out_specs=VMEM_SPEC)(x)


def mod_grad_rows(gath, name):
    w = gath.shape[-1]

    def body(g_ref, dm_ref, db_ref):
        dm_ref[...] = jnp.zeros_like(dm_ref)
        for l in range(2):
            ctx = g_ref[0, 3 * l + 2:3 * l + 3, :]
            tot = g_ref[0, 3 * l:3 * l + 1, :] + g_ref[0, 3 * l + 1:3 * l + 2, :]
            for k in range(8):
                dm_ref[l, 2 * k:2 * k + 2, :] = g_ref[k, 3 * l:3 * l + 2, :]
                if k:
                    ctx = ctx + g_ref[k, 3 * l + 2:3 * l + 3, :]
                    tot = tot + (g_ref[k, 3 * l:3 * l + 1, :] + g_ref[k, 3 * l + 1:3 * l + 2, :])
            dm_ref[l, 16:17, :] = ctx
            db_ref[l:l + 1, :] = tot + ctx

    return pl.pallas_call(body, name=name, out_shape=[_sds((2, 32, w), F32), _sds((2, w), F32)],
                          in_specs=[VMEM_SPEC], out_specs=[VMEM_SPEC, VMEM_SPEC])(gath)


def cctx_grad(p, c_ctx, name):
    def body(a_ref, c_ref, o_ref):
        cv = c_ref[...]
        sg = _sigmoid(cv)
        o_ref[...] = 0.5 * (a_ref[0, 0:1, :] + a_ref[1, 0:1, :]) * (sg * (1.0 + cv * (1.0 - sg)))
    return pl.pallas_call(body, name=name, out_shape=_sds((1, D), F32), in_specs=[VMEM_SPEC] * 2,
                          out_specs=VMEM_SPEC)(p, c_ctx)


def loss_and_grad(lay, h, tgt, name):
    def fn(hb, tb):
        lat = (pl.program_id(0) % lay.bps) >= lay.cb
        e = jnp.where(lat, hb - tb, 0.0)
        return e * (1.0 / D), jnp.sum(e * e, axis=0, keepdims=True) * (0.5 / D)
    return rowwise(lay, name, fn, [h, tgt], outs=[(D, F32)], sums=[(1, D)])


def adamw(w, g, m, v, name):
    shape = w.shape
    w2, g2, m2, v2 = (t.reshape(-1, shape[-1]) for t in (w, g, m, v))
    rows, width = w2.shape
    tr = 256 if rows % 256 == 0 else rows
    c1 = 1.0 - ADAM_B1 ** ADAM_STEP
    c2 = 1.0 - ADAM_B2 ** ADAM_STEP

    def body(w_ref, g_ref, m_ref, v_ref, d_ref, mo_ref, vo_ref):
        gv = g_ref[...]
        mn = ADAM_B1 * m_ref[...] + (1.0 - ADAM_B1) * gv
        vn = ADAM_B2 * v_ref[...] + (1.0 - ADAM_B2) * (gv * gv)
        d_ref[...] = -ADAM_LR * ((mn / c1) / (jnp.sqrt(vn / c2) + ADAM_EPS) + ADAM_WD * w_ref[...])
        mo_ref[...] = mn
        vo_ref[...] = vn

    spec = pl.BlockSpec((tr, width), lambda i: (i, 0))
    d, mn, vn = pl.pallas_call(body, name=name, out_shape=[_sds((rows, width), F32)] * 3, grid=(rows // tr,),
                               in_specs=[spec] * 4, out_specs=[spec] * 3, compiler_params=_cp(("parallel",)))(w2, g2, m2, v2)
    return d.reshape(shape), mn.reshape(shape), vn.reshape(shape)


def adamw_ffn(w, m, v, red, kind, ns, name):
    shape = w.shape
    w2, m2, v2 = (t.reshape(-1, shape[-1]) for t in (w, m, v))
    rows, width = w2.shape
    c1 = 1.0 - ADAM_B1 ** ADAM_STEP
    c2 = 1.0 - ADAM_B2 ** ADAM_STEP
    tr, nb = ns // 2, 2
    gspec = pl.BlockSpec((tr, D), lambda i: (((i // nb) * 3 + kind) * nb + i % nb, 0))

    def body(w_ref, g_ref, m_ref, v_ref, go_ref, d_ref, mo_ref, vo_ref):
        gv = g_ref[...]
        mn = ADAM_B1 * m_ref[...] + (1.0 - ADAM_B1) * gv
        vn = ADAM_B2 * v_ref[...] + (1.0 - ADAM_B2) * (gv * gv)
        go_ref[...] = gv
        d_ref[...] = -ADAM_LR * ((mn / c1) / (jnp.sqrt(vn / c2) + ADAM_EPS) + ADAM_WD * w_ref[...])
        mo_ref[...] = mn
        vo_ref[...] = vn

    spec = pl.BlockSpec((tr, width), lambda i: (i, 0))
    outs = pl.pallas_call(body, name=name, out_shape=[_sds((rows, width), F32)] * 4, grid=(rows // tr,),
                          in_specs=[spec, gspec, spec, spec], out_specs=[spec] * 4,
                          compiler_params=_cp(("parallel",)))(w2, red, m2, v2)
    return tuple(t.reshape(shape) for t in outs)


def mod_mm(sc, w_mod, bias, name):
    wm = w_mod.shape[-1]
    tn = _pick(wm, (768, 512, 384, 256, 128))

    def body(a_ref, b_ref, c_ref, o_ref):
        o_ref[...] = _nn(a_ref[...], b_ref[...].astype(BF16)) + c_ref[...]

    return pl.pallas_call(
        body, name=name, out_shape=_sds((DEPTH, 32, wm), F32), grid=(DEPTH, wm // tn),
        in_specs=[pl.BlockSpec((32, D), lambda l, j: (0, 0)), pl.BlockSpec((None, D, tn), lambda l, j: (l, 0, j)),
                  pl.BlockSpec((None, 1, tn), lambda l, j: (l, 0, j))],
        out_specs=pl.BlockSpec((None, 32, tn), lambda l, j: (l, 0, j)),
        compiler_params=_cp(("parallel", "parallel")))(sc, w_mod, bias)


def wmod_dw(sc, dcol, name):
    wm = dcol.shape[-1]
    tm = 256

    def body(a_ref, b_ref, o_ref):
        o_ref[...] = _tn(a_ref[...], b_ref[...].astype(BF16))

    return pl.pallas_call(
        body, name=name, out_shape=_sds((DEPTH, D, wm), F32), grid=(DEPTH, D // tm),
        in_specs=[pl.BlockSpec((32, tm), lambda l, i: (0, i)), pl.BlockSpec((None, 32, wm), lambda l, i: (l, 0, 0))],
        out_specs=pl.BlockSpec((None, tm, wm), lambda l, i: (l, i, 0)),
        compiler_params=_cp(("parallel", "parallel")))(sc, dcol)


def cctx_dx(drow, w_mod, name):
    wm = w_mod.shape[-1]

    def body(a_ref, b_ref, o_ref):
        o_ref[...] = _nt(a_ref[...].astype(BF16), b_ref[...].astype(BF16))

    return pl.pallas_call(
        body, name=name, out_shape=_sds((DEPTH, 16, D), F32), grid=(DEPTH,),
        in_specs=[pl.BlockSpec((None, 16, wm), lambda l: (l, 0, 0)), pl.BlockSpec((None, D, wm), lambda l: (l, 0, 0))],
        out_specs=pl.BlockSpec((None, 16, D), lambda l: (l, 0, 0)), compiler_params=_cp(("parallel",), VMEM_BIG))(drow, w_mod)


HEAD_PERM = (0, 4, 1, 5, 2, 6, 3, 7)


def _rot_rows(wt):
    return jnp.concatenate([-wt[32:64], wt[0:32]], axis=0)


def _unrot_rows(g):
    return jnp.concatenate([g[32:64], -g[0:32]], axis=0)


def _heads(a, n):
    return [a[64 * i:64 * (i + 1)] for i in range(n)]


def kernel(x, c, ctx, c_ctx, w_mod, b_mod, ln_g, ln_b, ffn_w_gate, ffn_w_up, ffn_w_down, mix_ab_w_in, attn_sink, pool_w, pool_scale, mix_ab_w_out, lru_w_in, lru_conv_w, lru_conv_b, lru_wa, lru_ba, lru_wx, lru_bx, lru_lambda, lru_w_out, loss_target, m_c_ctx, m_w_mod, m_b_mod, m_ln_g, m_ln_b, m_ffn_w_gate, m_ffn_w_up, m_ffn_w_down, m_mix_ab_w_in, m_attn_sink, m_pool_w, m_pool_scale, m_mix_ab_w_out, m_lru_w_in, m_lru_conv_w, m_lru_conv_b, m_lru_wa, m_lru_ba, m_lru_wx, m_lru_bx, m_lru_lambda, m_lru_w_out, v_c_ctx, v_w_mod, v_b_mod, v_ln_g, v_ln_b, v_ffn_w_gate, v_ffn_w_up, v_ffn_w_down, v_mix_ab_w_in, v_attn_sink, v_pool_w, v_pool_scale, v_mix_ab_w_out, v_lru_w_in, v_lru_conv_w, v_lru_conv_b, v_lru_wa, v_lru_ba, v_lru_wx, v_lru_bx, v_lru_lambda, v_lru_w_out):
    n_lat, n_ctx = x.shape[1], ctx.shape[1]
    lay = Layout(n_ctx, n_lat)
    T = lay.T
    ns = ffn_w_gate.shape[-1]
    n_li, n_ai = lru_w_in.shape[-1], mix_ab_w_in.shape[-1]
    n_ao, n_lo = mix_ab_w_out.shape[1], lru_w_out.shape[1]
    wm = w_mod.shape[-1]
    dsh = ln_g.shape[-1]
    mx, my, mc = lax.axis_index("x"), lax.axis_index("y"), lax.axis_index("c")
    chip = 2 * mx + my
    me = 2 * chip + mc

    c_all = all_gather8(c, "ag8_c").reshape(16, D)
    cc = jnp.concatenate([c_all, c_ctx[None, :], jnp.zeros((15, D), F32)], axis=0)
    sc = silu_rows(cc, "silu_c")
    bias = lax.dynamic_slice(b_mod, (0, chip * wm), (DEPTH, wm)).reshape(DEPTH, 1, wm)
    modg = all_gather_chips(mod_mm(sc, w_mod, bias, "mod_mm"), "ag_mod")
    modtab = []
    for l in range(DEPTH):
        full = jnp.transpose(modg[:, l], (1, 0, 2)).reshape(32, N_CHIP * wm)
        mine = lax.dynamic_slice(full, (2 * me, 0), (2, N_CHIP * wm))
        modtab.append(jnp.concatenate([mine, full[16:17]], axis=0).reshape(3, N_MOD, D))

    small = jnp.concatenate([ln_g.reshape(6, dsh), ln_b.reshape(6, dsh), lru_conv_w[0], lru_conv_b, lru_ba[0],
                             lru_bx[0], lru_lambda[0], jnp.zeros((9, dsh), F32)], axis=0)
    small = all_gather_chips(small.reshape(2, 16, dsh), "ag_small").reshape(N_CHIP, 32, dsh)
    small = jnp.transpose(small, (1, 0, 2)).reshape(32, D)
    ln_g_f, ln_b_f = small[0:6].reshape(2, 3, D), small[6:12].reshape(2, 3, D)
    conv_w_f, conv_b_f = small[12:16], small[16:17]
    lru_vec = small[17:23]

    hh = 3 * ns // 2
    gate_t, up_t = jnp.swapaxes(ffn_w_gate, -1, -2), jnp.swapaxes(ffn_w_up, -1, -2)
    placed = [ffn_place(gate_t, up_t, ffn_w_down, g // 2, g % 2, f"ag_ffn{g}_place") for g in range(4)]
    wb = [gather_placed(placed[0], "ag_ffn0"), None, None, None]
    mix_sh = jnp.concatenate([lru_w_in[0].T, mix_ab_w_in[0].T, mix_ab_w_out[0], lru_w_out[0]], axis=0).astype(BF16)
    n_mix = n_li + n_ai + n_ao + n_lo
    mixw = all_gather_chips(mix_sh.reshape(2, n_mix // 2, D), "ag_mix").reshape(N_CHIP, n_mix, D)
    o1, o2, o3 = n_li, n_li + n_ai, n_li + n_ai + n_ao
    lru_in_t = mixw[:, 0:o1].reshape(N_CHIP * n_li, D)
    ab_in_t = mixw[:, o1:o2].reshape(N_CHIP * n_ai, D)
    ab_out = mixw[:, o2:o3].reshape(N_CHIP * n_ao, D)
    lru_out = mixw[:, o3:].reshape(N_CHIP * n_lo, D)
    qh, kh = _heads(ab_in_t[Q0:K0], N_HEADS), _heads(ab_in_t[K0:V0], N_KV)
    w_ext_t = jnp.concatenate([qh[h] for h in HEAD_PERM] + [ab_in_t[K0:QR0]]
                              + [_rot_rows(qh[h]) for h in HEAD_PERM] + [_rot_rows(t) for t in kh], axis=0)
    oh = _heads(ab_out[0:ATT_W], N_HEADS)
    w_out_ext = jnp.concatenate([oh[h] for h in HEAD_PERM] + [ab_out[ATT_W:]], axis=0)

    t = jnp.arange(n_lat)
    inv = ROPE_THETA ** (-jnp.arange(16, dtype=F32) / 16.0)
    ang = jnp.concatenate([(t // GRID_W).astype(F32)[:, None] * inv, (t % GRID_W).astype(F32)[:, None] * inv], axis=-1)
    cos1 = jnp.concatenate([jnp.ones((n_ctx, 32), F32), jnp.cos(ang)], axis=0)
    sin1 = jnp.concatenate([jnp.zeros((n_ctx, 32), F32), jnp.sin(ang)], axis=0)
    cos_t = jnp.tile(cos1, (2, 4))
    sin_t = jnp.tile(sin1, (2, 4))
    sk = attn_sink[0]
    sink_tab = jnp.concatenate([jnp.repeat(jnp.stack([sk[:4], sk[4:]], axis=1), HEAD_DIM, axis=1),
                                jnp.zeros((4, 128), F32)], axis=0)
    pscale = pool_scale.reshape(1, POOL_W)

    h0 = jnp.concatenate([ctx, x], axis=1).reshape(T, D)
    tgt = loss_target.reshape(2 * n_lat, D)

    def lnv(l, j):
        return jnp.stack([ln_g_f[l, j], ln_b_f[l, j]])

    subs = [(0, 0, 0.5, 0), (0, 3, 1.0, 1), (0, 6, 0.5, 2), (1, 0, 0.5, 0), (1, 3, 1.0, 1), (1, 6, 0.5, 2)]

    def ffn_core(hm, l, f):
        tag = f"l{l}f{f}"
        gi = 2 * l + f
        w = wb[gi].reshape(N_CHIP, 3 * ns, D)
        if gi == 3:
            sp, sl, u, a = ffn_up(lay, hm, w, 0, 1, ns, f"ffn_up_{tag}")
            (y,) = slab_nn_acc(lay, [a], w, [2], ns, f"ffn_down_{tag}")
            return y, dict(sp=sp, sl=sl, u=u, a=a, nbuf=None)
        sp, sl, u, a, nbuf = ffn_up(lay, hm, w, 0, 1, ns, f"ffn_up_{tag}", rider=rider_gather_xy(placed[gi + 1]))
        y, nbuf = slab_nn_acc(lay, [a], w, [2], ns, f"ffn_down_{tag}", rider=rider_gather_fwd(nbuf))
        return y, dict(sp=sp, sl=sl, u=u, a=a, nbuf=nbuf)

    def mixa_core(hm):
        p = mm_nt(hm, w_ext_t, "mixa_in")
        qr, kr, vb, u = rope_fwd(lay, p, cos_t, sin_t, "rope")
        att, lse = attn_fwd(lay, qr, kr, vb, sink_tab, "attn")
        pool = pool_fwd(lay, u, pool_w[0], pscale, "pool")
        cat = jnp.concatenate([att, pool], axis=1)
        return mm_nn(cat, w_out_ext, "mixa_out"), dict(qr=qr, kr=kr, vb=vb, u=u, lse=lse, cat=cat)

    def mixc_core(hm):
        p = mm_nt(hm, lru_in_t, "mixc_in")
        uc = conv_fwd(lay, p, D, conv_w_f, conv_b_f, "conv")
        a, b = lru_coeffs(lay, uc, lru_wa[0], lru_wx[0], lru_vec, "lru_coef")
        s = lru_scan(lay, a, b, "lru_scan")
        o = lru_gate(lay, p, s, "lru_gate")
        return mm_nn(o, lru_out, "mixc_out"), dict(p=p, uc=uc, a=a, s=s, o=o)

    recs = []
    h = h0
    hm = modulate(lay, h0, modtab[0], 0, 1, "mod_first")
    for k, (l, k0, coef, j) in enumerate(subs):
        if k0 == 3:
            y, core = mixa_core(hm) if l == 0 else mixc_core(hm)
        else:
            y, core = ffn_core(hm, l, k0 // 6)
        nxt = None if k == 5 else (modtab[subs[k + 1][0]], subs[k + 1][1], subs[k + 1][1] + 1)
        nbuf = core.pop("nbuf", None)
        res = resid_ln(lay, h, y, modtab[l], k0 + 2, coef, lnv(l, j), f"ln_s{k}", nxt=nxt,
                       rider=None if nbuf is None else rider_gather_d2d(nbuf))
        if nbuf is not None:
            wb[2 * l + k0 // 6 + 1] = res[-1]
        recs.append(dict(h=h, hm=hm, y=y, xhat=res[1], rstd=res[2], **core))
        h = res[0]
        hm = res[3] if nxt is not None else None

    dout, lparts = loss_and_grad(lay, h, tgt, "loss")
    loss = lax.psum(jnp.sum(lparts), ("x", "y", "c"))

    dln = {}
    dms = {}
    mixg = {}
    ffn_red = [lax.empty((4, 2, hh, D), F32)]
    pending = []

    def ffn_core_bwd(dy, r, l, f):
        tag = f"l{l}f{f}"
        gi = 2 * l + f
        w = wb[gi].reshape(N_CHIP, 3 * ns, D)
        prev = pending.pop() if pending else None
        gb = lax.empty((N_CHIP, 3 * ns, D), F32)
        join = None
        if prev is None:
            dg, du, dhm = ffn_bwd_dx(lay, dy, w, r["sp"], r["sl"], r["u"], ns, f"ffn_dx_{tag}")
            (gb,) = slab_tn(lay, r["a"], dy, gb, 2, ns, f"ffn_dwd_{tag}")
            (gb,) = slab_tn(lay, dg, r["hm"], gb, 0, ns, f"ffn_dwg_{tag}")
            (gb,) = slab_tn(lay, du, r["hm"], gb, 1, ns, f"ffn_dwu_{tag}")
        else:
            dg, du, dhm, recv = ffn_bwd_dx(lay, dy, w, r["sp"], r["sl"], r["u"], ns, f"ffn_dx_{tag}",
                                           rider=rider_reduce_sib(prev[1]))
            q = add_own_half(prev[1], recv, BF16, f"rs_add2_ffn{prev[0]}")
            gb, arr = slab_tn(lay, r["a"], dy, gb, 2, ns, f"ffn_dwd_{tag}", rider=rider_reduce_copy(q, 0))
            gb, arr = slab_tn(lay, dg, r["hm"], gb, 0, ns, f"ffn_dwg_{tag}", rider=rider_reduce_copy(q, 1, arr))
            gb, arr = slab_tn(lay, du, r["hm"], gb, 1, ns, f"ffn_dwu_{tag}", rider=rider_reduce_copy(q, 2, arr))
            join = rider_join(sum_slots(q, arr, f"rs_add4_ffn{prev[0]}", dst=ffn_red[0], g=prev[0]), prev[0])
        pending.append((gi, gb.reshape(N_CHIP, 2, hh, D)))
        return dhm, join

    def mixc_core_bwd(dy, r):
        do_c = mm_nt(dy, lru_out, "mixc_out_dx")
        mixg["lru_out"] = mm_tn(r["o"], dy, "mixc_out_dw")
        dgate, dyg = lru_gate_bwd(lay, r["p"], r["s"], do_c, "lru_gate_b")
        da_c, db_c = lru_scan_bwd(lay, r["a"], r["s"], dyg, "lru_scan_b")
        duc, mixg["wa"], mixg["wx"], mixg["vec"] = lru_coeffs_bwd(lay, r["uc"], lru_wa[0], lru_wx[0], lru_vec, da_c, db_c,
                                                                  "lru_coef_b")
        du_c, mixg["cw"], mixg["cb"] = conv_bwd(lay, r["p"], D, conv_w_f, duc, "conv_b")
        dp_c = jnp.concatenate([dgate, du_c], axis=1)
        mixg["lru_in_t"] = mm_tn(dp_c, r["hm"], "mixc_in_dw")
        return mm_nn(dp_c, lru_in_t, "mixc_in_dx")

    def mixa_core_bwd(dy, r):
        dcat = mm_nt(dy, w_out_ext, "mixa_out_dx")
        mixg["out_ext"] = mm_tn(r["cat"], dy, "mixa_out_dw")
        dqr, dkr, dv, mixg["sink"] = attn_bwd(lay, r["qr"], r["kr"], r["vb"], sink_tab, r["lse"], dcat, "attn_b")
        du_a, mixg["pw"], mixg["ps"] = pool_bwd(lay, r["u"], dcat, pool_w[0], pscale, "pool_b")
        dp_a = rope_bwd(lay, dqr, dkr, dv, du_a, cos_t, sin_t, "rope_b")
        mixg["ext_t"] = mm_tn(dp_a, r["hm"], "mixa_in_dw")
        return mm_nn(dp_a, w_ext_t, "mixa_in_dx")

    l, k0, coef, j = subs[5]
    dy, dres, s1 = ln_bwd(lay, dout, recs[5]["xhat"], recs[5]["rstd"], recs[5]["y"], modtab[l], k0 + 2, coef, lnv(l, j),
                          "lnb_s5")
    for k in range(5, -1, -1):
        l, k0, coef, j = subs[k]
        r = recs[k]
        join = None
        if k0 == 3:
            dhm = mixa_core_bwd(dy, r) if l == 0 else mixc_core_bwd(dy, r)
        else:
            dhm, join = ffn_core_bwd(dy, r, l, k0 // 6)
        dln[(l, j)] = block_sums(lay, s1, f"bs_ln_s{k}")
        if k > 0:
            lp, k0p, coefp, jp = subs[k - 1]
            rp = recs[k - 1]
            res = modb_lnb(lay, dres, dhm, r["h"], modtab[l], k0 + 1, rp["xhat"], rp["rstd"], rp["y"],
                           modtab[lp], k0p + 2, coefp, lnv(lp, jp), f"modb_lnb_s{k}", rider=join)
            dy, dres, s1, s2 = res[:4]
        else:
            res = mod_bwd(lay, dres, dhm, r["h"], modtab[l], k0 + 1, "modb_s0", rider=join)
            gx, s2 = res[:2]
        if join is not None:
            ffn_red[0] = res[-1]
        dms[(l, k0)] = block_sums(lay, s2, f"bs_mod_s{k}")
    grad_x = gx.reshape(2, n_lat, D)
    g_lru_out, g_wa, g_wx, g_vec, g_cw, g_cb = (mixg[n] for n in ("lru_out", "wa", "wx", "vec", "cw", "cb"))
    g_lru_in_t, g_out_ext, g_sink, g_pw, g_ps, g_ext_t = (mixg[n] for n in ("lru_in_t", "out_ext", "sink", "pw", "ps", "ext_t"))

    rows = []
    for l in range(DEPTH):
        per_k = []
        for k0, j in ((0, 0), (3, 1), (6, 2)):
            per_k += [dms[(l, k0)][:3, 0], dms[(l, k0)][:3, 1], dln[(l, j)][:3, 2]]
        rows.append(jnp.stack(per_k, axis=1).reshape(3, N_MOD * D))
    dmod_loc = jnp.concatenate(rows + [jnp.zeros((2, N_MOD * D), F32)], axis=0)
    dmod_all, g_b_mod = mod_grad_rows(all_gather8(dmod_loc, "ag8_dmod"), "dmod_rows")
    dcol = lax.dynamic_slice(dmod_all, (0, 0, chip * wm), (DEPTH, 32, wm))
    g_w_mod = wmod_dw(sc, dcol, "wmod_dw")
    g_cctx = cctx_grad(cctx_dx(dcol[:, 16:32], w_mod, "cctx_dx"), c_ctx[None, :], "cctx_grad")

    gq = _heads(g_ext_t[Q0:K0], N_HEADS)
    gqr = _heads(g_ext_t[QR0:KR0], N_HEADS)
    g_q = [None] * N_HEADS
    for i, h in enumerate(HEAD_PERM):
        g_q[h] = gq[i] + _unrot_rows(gqr[i])
    gk = [a + _unrot_rows(b) for a, b in zip(_heads(g_ext_t[K0:V0], N_KV), _heads(g_ext_t[KR0:PEXT], N_KV))]
    g_ab_in_t = jnp.concatenate(g_q + gk + [g_ext_t[V0:QR0]], axis=0)
    go = _heads(g_out_ext[0:ATT_W], N_HEADS)
    g_o = [None] * N_HEADS
    for i, h in enumerate(HEAD_PERM):
        g_o[h] = go[i]
    g_ab_out = jnp.concatenate(g_o + [g_out_ext[ATT_W:]], axis=0)
    mix_g = jnp.concatenate([g_lru_in_t.reshape(N_CHIP, n_li, D), g_ab_in_t.reshape(N_CHIP, n_ai, D),
                             g_ab_out.reshape(N_CHIP, n_ao, D), g_lru_out.reshape(N_CHIP, n_lo, D)], axis=1)

    g_ln_g = jnp.stack([jnp.stack([dln[(l, j)][3, 1] for j in range(3)]) for l in range(DEPTH)])
    g_ln_b = jnp.stack([jnp.stack([dln[(l, j)][3, 0] for j in range(3)]) for l in range(DEPTH)])
    sink_row = jnp.sum(g_sink, axis=0)[:4]
    g_sink8 = jnp.concatenate([sink_row[:, 0], sink_row[:, HEAD_DIM]])
    misc = jnp.concatenate([g_sink8, jnp.sum(g_ps, axis=0).reshape(POOL_W), jnp.zeros((D - 8 - POOL_W,), F32)])
    small_g = jnp.concatenate([
        g_ln_g.reshape(6, D), g_ln_b.reshape(6, D), jnp.sum(g_cw, axis=0), jnp.sum(g_cb, axis=0), g_vec,
        misc[None, :], jnp.sum(g_pw, axis=0).reshape(64, D), g_wa.reshape(256, D), g_wx.reshape(256, D), g_cctx,
        jnp.zeros((39, D), F32)], axis=0)
    n_small = small_g.shape[0] // N_CHIP
    mix_buf = jnp.concatenate([mix_g, small_g.reshape(N_CHIP, n_small, D)], axis=1)
    n_mb = n_mix + n_small

    last_g, last_buf = pending.pop()
    ffn_red = reduce_scatter_chips(last_buf, f"ffn{last_g}", wire=BF16, dst=ffn_red[0], g=last_g).reshape(12 * ns, D)
    mix_red = reduce_scatter_chips(mix_buf.reshape(N_CHIP, 2, n_mb // 2, D), "mix").reshape(n_mb, D)
    small_red = all_gather_chips(mix_red[n_mix:].reshape(2, n_small // 2, D), "ag_smallg").reshape(N_CHIP * n_small, D)

    ffn_kind = dict(ffn_w_gate=0, ffn_w_up=1, ffn_w_down=2)

    def cols(a):
        return lax.dynamic_slice_in_dim(a, chip * dsh, dsh, axis=a.ndim - 1)

    sr = small_red
    grads = dict(
        c_ctx=sr[600], w_mod=g_w_mod, b_mod=g_b_mod,
        ln_g=cols(sr[0:6]).reshape(2, 3, dsh), ln_b=cols(sr[6:12]).reshape(2, 3, dsh),
        mix_ab_w_in=mix_red[o1:o2][None], attn_sink=sr[23, 0:8][None], pool_w=sr[24:88].reshape(1, 4, 128, 128),
        pool_scale=sr[23, 8:8 + POOL_W][None], mix_ab_w_out=mix_red[o2:o3][None], lru_w_in=mix_red[0:o1].T[None],
        lru_conv_w=cols(sr[12:16])[None], lru_conv_b=cols(sr[16:17]), lru_wa=sr[88:344].reshape(1, 2, 8, 128, 128),
        lru_ba=cols(sr[17:19])[None], lru_wx=sr[344:600].reshape(1, 2, 8, 128, 128), lru_bx=cols(sr[19:21])[None],
        lru_lambda=cols(sr[21:23])[None], lru_w_out=mix_red[o3:n_mix][None])
    params = dict(c_ctx=(c_ctx, m_c_ctx, v_c_ctx), w_mod=(w_mod, m_w_mod, v_w_mod), b_mod=(b_mod, m_b_mod, v_b_mod),
                  ln_g=(ln_g, m_ln_g, v_ln_g), ln_b=(ln_b, m_ln_b, v_ln_b),
                  ffn_w_gate=(ffn_w_gate, m_ffn_w_gate, v_ffn_w_gate), ffn_w_up=(ffn_w_up, m_ffn_w_up, v_ffn_w_up),
                  ffn_w_down=(ffn_w_down, m_ffn_w_down, v_ffn_w_down),
                  mix_ab_w_in=(mix_ab_w_in, m_mix_ab_w_in, v_mix_ab_w_in), attn_sink=(attn_sink, m_attn_sink, v_attn_sink),
                  pool_w=(pool_w, m_pool_w, v_pool_w), pool_scale=(pool_scale, m_pool_scale, v_pool_scale),
                  mix_ab_w_out=(mix_ab_w_out, m_mix_ab_w_out, v_mix_ab_w_out), lru_w_in=(lru_w_in, m_lru_w_in, v_lru_w_in),
                  lru_conv_w=(lru_conv_w, m_lru_conv_w, v_lru_conv_w), lru_conv_b=(lru_conv_b, m_lru_conv_b, v_lru_conv_b),
                  lru_wa=(lru_wa, m_lru_wa, v_lru_wa), lru_ba=(lru_ba, m_lru_ba, v_lru_ba), lru_wx=(lru_wx, m_lru_wx, v_lru_wx),
                  lru_bx=(lru_bx, m_lru_bx, v_lru_bx), lru_lambda=(lru_lambda, m_lru_lambda, v_lru_lambda),
                  lru_w_out=(lru_w_out, m_lru_w_out, v_lru_w_out))
    gl, dl, ml, vl = [], [], [], []
    transposed = ("ffn_w_gate", "ffn_w_up", "mix_ab_w_in")
    for name, (w, m, v) in params.items():
        if name in transposed:
            w, m, v = (jnp.swapaxes(t, -1, -2) for t in (w, m, v))
        if name in ffn_kind:
            g, d, mn, vn = adamw_ffn(w, m, v, ffn_red, ffn_kind[name], ns, f"adamw_{name}")
        else:
            g = grads[name].reshape(w.shape)
            d, mn, vn = adamw(w, g, m, v, f"adamw_{name}")
        if name in transposed:
            g, d, mn, vn = (jnp.swapaxes(t, -1, -2) for t in (g, d, mn, vn))
        gl.append(g)
        dl.append(d)
        ml.append(mn)
        vl.append(vn)
    return (loss, grad_x, *gl, *dl, *ml, *vl)
```

```python
import functools
import math

import jax
import jax.numpy as jnp
from jax import lax
from jax.experimental import pallas as pl
from jax.experimental.pallas import tpu as pltpu

F32, BF16 = jnp.float32, jnp.bfloat16
MESH = pl.DeviceIdType.MESH
ANY = pl.BlockSpec(memory_space=pl.ANY)
VMEM_SPEC = pl.BlockSpec(memory_space=pltpu.VMEM)

D = 1024
N_CHIP = 4
HEAD_DIM, N_HEADS, N_KV = 64, 8, 2
ATT_W, KV_W, POOL_W = 512, 128, 512
POOL_WINDOWS = (2, 4, 8, 16)
BLK = 128
ATT_SCALE = HEAD_DIM ** -0.5
ROPE_THETA = 10000.0
GRID_W = 64
LRU_C = 8.0
LN_EPS = 1e-5
NEG_INF = -1e30
DEPTH = 2
ALPHA = (2 * DEPTH) ** 0.25
N_MOD = 9
ADAM_LR, ADAM_B1, ADAM_B2, ADAM_EPS, ADAM_WD, ADAM_STEP = 0.001, 0.9, 0.999, 1e-08, 0.01, 10
VMEM_BIG = 48 * 1024 * 1024


def _cp(sem=None, vmem=None):
    kw = {}
    if sem is not None:
        kw["dimension_semantics"] = sem
    if vmem is not None:
        kw["vmem_limit_bytes"] = vmem
    return pltpu.CompilerParams(**kw)


def _sds(shape, dtype):
    return jax.ShapeDtypeStruct(tuple(shape), dtype)


def _pick(n, cands):
    for c in cands:
        if n % c == 0:
            return c
    return n


def _dot(a, b, dims):
    return lax.dot_general(a, b, (dims, ((), ())), preferred_element_type=F32)


def _nn(a, b):
    return _dot(a, b, ((1,), (0,)))


def _nt(a, b):
    return _dot(a, b, ((1,), (1,)))


def _tn(a, b):
    return _dot(a, b, ((0,), (0,)))


def _sigmoid(x):
    return 0.5 * jnp.tanh(0.5 * x) + 0.5


def _me():
    return lax.axis_index("x"), lax.axis_index("y"), lax.axis_index("c")


def _rcopy(src, dst, ssem, rsem, dev):
    return pltpu.make_async_remote_copy(src_ref=src, dst_ref=dst, send_sem=ssem, recv_sem=rsem,
                                        device_id=dev, device_id_type=MESH)


def all_gather8(x, name):
    def body(x_ref, o_ref, ssem, rsem, lsem):
        mx, my, mc = _me()
        me = 4 * mx + 2 * my + mc
        loc = pltpu.make_async_copy(x_ref, o_ref.at[me], lsem)
        loc.start()
        peers = []
        for m in range(1, 8):
            px = 1 - mx if (m >> 2) & 1 else mx
            py = 1 - my if (m >> 1) & 1 else my
            pc = 1 - mc if m & 1 else mc
            peers.append((px, py, pc))
        sends = [_rcopy(x_ref, o_ref.at[me], ssem.at[k], rsem.at[k], p) for k, p in enumerate(peers)]
        for cp in sends:
            cp.start()
        for k, (px, py, pc) in enumerate(peers):
            _rcopy(x_ref, o_ref.at[4 * px + 2 * py + pc], ssem.at[k], rsem.at[k], (px, py, pc)).wait_recv()
        for cp in sends:
            cp.wait_send()
        loc.wait()

    return pl.pallas_call(
        body, name=name, out_shape=_sds((8,) + x.shape, x.dtype),
        in_specs=[VMEM_SPEC], out_specs=VMEM_SPEC,
        scratch_shapes=[pltpu.SemaphoreType.DMA((7,)), pltpu.SemaphoreType.DMA((7,)), pltpu.SemaphoreType.DMA],
    )(x)


_ROW_BLOCKS = (512, 384, 352, 256, 224, 128)


def _idx(v):
    return jnp.reshape(v, (1,)).astype(jnp.int32)


def place_slab(shard, name):
    _, h, w = shard.shape
    th = _pick(h, _ROW_BLOCKS)

    def body(s_ref, x_ref, o_ref):
        del s_ref
        o_ref[...] = x_ref[...]

    return pl.pallas_call(
        body, name=name, out_shape=_sds((N_CHIP,) + shard.shape, shard.dtype),
        grid_spec=pltpu.PrefetchScalarGridSpec(
            num_scalar_prefetch=1, grid=(2, h // th),
            in_specs=[pl.BlockSpec((None, th, w), lambda k, r, s: (k, r, 0))],
            out_specs=pl.BlockSpec((None, None, th, w), lambda k, r, s: (s[0], k, r, 0))),
    )(_idx(2 * lax.axis_index("x") + lax.axis_index("y")), shard)


def ffn_place(w_gate_t, w_up_t, w_down, l, f, name):
    ns = w_down.shape[-2]
    tc = 256

    def body(s_ref, g_ref, u_ref, d_ref, o_ref):
        del s_ref
        k = pl.program_id(0)

        @pl.when(k == 0)
        def _():
            o_ref[...] = g_ref[...].astype(BF16)

        @pl.when(k == 1)
        def _():
            o_ref[...] = u_ref[...].astype(BF16)

        @pl.when(k == 2)
        def _():
            o_ref[...] = d_ref[...].astype(BF16)

    spec = pl.BlockSpec((None, None, ns, tc), lambda k, j, s: (l, f, 0, j))
    out = pl.pallas_call(
        body, name=name, out_shape=_sds((N_CHIP, 3 * ns, D), BF16),
        grid_spec=pltpu.PrefetchScalarGridSpec(
            num_scalar_prefetch=1, grid=(3, D // tc), in_specs=[spec, spec, spec],
            out_specs=pl.BlockSpec((None, ns, tc), lambda k, j, s: (s[0], k, j))),
    )(_idx(2 * lax.axis_index("x") + lax.axis_index("y")), w_gate_t, w_up_t, w_down)
    return out.reshape(N_CHIP, 2, 3 * ns // 2, D)


def all_gather_chips(shard, name):
    return gather_placed(place_slab(shard, name + "_place"), name)


def gather_placed(full, name):
    def body(x_ref, o_ref, ssem, rsem):
        del x_ref
        mx, my, mc = _me()
        s = 2 * mx + my
        sib = (mx, my, 1 - mc)
        chips = [(1 - mx, my), (mx, 1 - my), (1 - mx, 1 - my)]
        first = [_rcopy(o_ref.at[s, mc], o_ref.at[s, mc], ssem.at[j], rsem.at[j], (px, py, mc))
                 for j, (px, py) in enumerate(chips)]
        for cp in first:
            cp.start()
        passed = []
        for j, (px, py) in enumerate(chips):
            ps = 2 * px + py
            _rcopy(o_ref.at[ps, mc], o_ref.at[ps, mc], ssem.at[j], rsem.at[j], (px, py, mc)).wait_recv()
            fw = _rcopy(o_ref.at[ps, mc], o_ref.at[ps, mc], ssem.at[3 + j], rsem.at[3 + j], sib)
            fw.start()
            passed.append(fw)
        for j, (px, py) in enumerate(chips):
            ps = 2 * px + py
            _rcopy(o_ref.at[ps, 1 - mc], o_ref.at[ps, 1 - mc], ssem.at[3 + j], rsem.at[3 + j], sib).wait_recv()
        for cp in first + passed:
            cp.wait_send()

    return pl.pallas_call(
        body, name=name, out_shape=_sds(full.shape, full.dtype), in_specs=[ANY], out_specs=ANY,
        input_output_aliases={0: 0},
        scratch_shapes=[pltpu.SemaphoreType.DMA((6,)), pltpu.SemaphoreType.DMA((6,))],
    )(full)


def sibling_send_other_half(buf, name):
    def body(x_ref, o_ref, ssem, rsem):
        mx, my, mc = _me()
        sib = (mx, my, 1 - mc)
        cps = [_rcopy(x_ref.at[k, 1 - mc], o_ref.at[k], ssem.at[k], rsem.at[k], sib) for k in range(N_CHIP)]
        for cp in cps:
            cp.start()
        for cp in cps:
            cp.wait_recv()
        for cp in cps:
            cp.wait_send()

    n, _, h, w = buf.shape
    return pl.pallas_call(
        body, name=name, out_shape=_sds((n, h, w), buf.dtype), in_specs=[ANY], out_specs=ANY,
        scratch_shapes=[pltpu.SemaphoreType.DMA((N_CHIP,)), pltpu.SemaphoreType.DMA((N_CHIP,))],
    )(buf)


def chips_all_to_all(q, name):
    def body(x_ref, o_ref, ssem, rsem):
        mx, my, mc = _me()
        s = 2 * mx + my
        chips = [(1 - mx, my), (mx, 1 - my), (1 - mx, 1 - my)]
        cps = [_rcopy(x_ref.at[2 * px + py], o_ref.at[s], ssem.at[j], rsem.at[j], (px, py, mc))
               for j, (px, py) in enumerate(chips)]
        for cp in cps:
            cp.start()
        for j, (px, py) in enumerate(chips):
            ps = 2 * px + py
            _rcopy(x_ref.at[ps], o_ref.at[ps], ssem.at[j], rsem.at[j], (px, py, mc)).wait_recv()
        for cp in cps:
            cp.wait_send()

    return pl.pallas_call(
        body, name=name, out_shape=_sds(q.shape, q.dtype), in_specs=[ANY], out_specs=ANY,
        scratch_shapes=[pltpu.SemaphoreType.DMA((3,)), pltpu.SemaphoreType.DMA((3,))],
    )(q)


def sibling_join_halves(both, name, g=None):
    def body(x_ref, o_ref, ssem, rsem):
        del x_ref
        mx, my, mc = _me()
        sib = (mx, my, 1 - mc)
        o = o_ref if g is None else o_ref.at[g]
        cp = _rcopy(o.at[mc], o.at[mc], ssem, rsem, sib)
        cp.start()
        _rcopy(o.at[1 - mc], o.at[1 - mc], ssem, rsem, sib).wait_recv()
        cp.wait_send()

    return pl.pallas_call(
        body, name=name, out_shape=_sds(both.shape, both.dtype), in_specs=[ANY], out_specs=ANY,
        input_output_aliases={0: 0}, scratch_shapes=[pltpu.SemaphoreType.DMA, pltpu.SemaphoreType.DMA],
    )(both)


def add_own_half(buf, recv, wire, name):
    n, _, h, w = buf.shape
    th = _pick(h, _ROW_BLOCKS)

    def body(c_ref, a_ref, b_ref, o_ref):
        del c_ref
        o_ref[...] = (a_ref[...] + b_ref[...]).astype(o_ref.dtype)

    return pl.pallas_call(
        body, name=name, out_shape=_sds((n, h, w), wire),
        grid_spec=pltpu.PrefetchScalarGridSpec(
            num_scalar_prefetch=1, grid=(n, h // th),
            in_specs=[pl.BlockSpec((None, None, th, w), lambda k, r, c: (k, c[0], r, 0)),
                      pl.BlockSpec((None, th, w), lambda k, r, c: (k, r, 0))],
            out_specs=pl.BlockSpec((None, th, w), lambda k, r, c: (k, r, 0))),
    )(_idx(lax.axis_index("c")), buf, recv)


def sum_slots(q, r, name, dst=None, g=None):
    n, h, w = r.shape
    th = _pick(h, _ROW_BLOCKS)

    def body(i_ref, q_ref, r1, r2, r3, *rest):
        del i_ref
        rest[-1][...] = ((q_ref[...].astype(F32) + r1[...].astype(F32)) + r2[...].astype(F32)) + r3[...].astype(F32)

    def slot(d):
        return lambda i, ix: ((ix[0] + d) % N_CHIP, i, 0)

    idx = jnp.stack([2 * lax.axis_index("x") + lax.axis_index("y"), lax.axis_index("c")]).astype(jnp.int32)
    in_specs = [pl.BlockSpec((None, th, w), slot(d)) for d in (0, 1, 2, 3)]
    if dst is None:
        return pl.pallas_call(
            body, name=name, out_shape=_sds((2, h, w), F32),
            grid_spec=pltpu.PrefetchScalarGridSpec(
                num_scalar_prefetch=1, grid=(h // th,), in_specs=in_specs,
                out_specs=pl.BlockSpec((None, th, w), lambda i, ix: (ix[1], i, 0))),
        )(idx, q, r, r, r)
    return pl.pallas_call(
        body, name=name, out_shape=_sds(dst.shape, F32),
        grid_spec=pltpu.PrefetchScalarGridSpec(
            num_scalar_prefetch=1, grid=(h // th,), in_specs=in_specs + [ANY],
            out_specs=pl.BlockSpec((None, None, th, w), lambda i, ix: (g, ix[1], i, 0))),
        input_output_aliases={5: 0},
    )(idx, q, r, r, r, dst)


def reduce_scatter_chips(buf, tag, wire=F32, dst=None, g=None):
    recv = sibling_send_other_half(buf, f"rs_sib_{tag}")
    q = add_own_half(buf, recv, wire, f"rs_add2_{tag}")
    r = chips_all_to_all(q, f"rs_a2a_{tag}")
    red = sum_slots(q, r, f"rs_add4_{tag}", dst=dst, g=g)
    return sibling_join_halves(red, f"rs_join_{tag}", g=g)


class Layout:
    def __init__(self, n_ctx, n_lat):
        self.C, self.L = n_ctx, n_lat
        self.PS = n_ctx + n_lat
        self.T = 2 * self.PS
        self.tr = _pick(math.gcd(n_ctx, n_lat), (256, 128))
        self.bps = self.PS // self.tr
        self.cb = n_ctx // self.tr
        self.nblk = self.T // self.tr
        self.tm = _pick(self.T, (1152, 768, 512, 256, 128))
        self.tc = _pick(self.T, (512, 256, 128))

    def seg(self, i):
        return jnp.where(i % self.bps < self.cb, 2, i // self.bps)


def rowwise(lay, name, fn, rows, segs=(), vecs=(), outs=(), sums=(), rider=None):
    tr, nblk = lay.tr, lay.nblk
    n_r, n_s, n_v, n_o = len(rows), len(segs), len(vecs), len(outs)
    lat_only = any(o[2:] for o in outs) or any(a.shape[0] != lay.T for a in rows)
    nsub = 1 if lat_only or nblk % 2 else 2
    tb = tr * nsub

    def body(*refs):
        ins = refs[:n_r + n_s + n_v]
        ors = refs[n_r + n_s + n_v:]
        for sub in range(nsub):
            rs = slice(sub * tr, (sub + 1) * tr)
            seg = lay.seg(pl.program_id(0) * nsub + sub)
            vals = [r[rs, :] for r in ins[:n_r]] + [r[seg] for r in ins[n_r:n_r + n_s]] + [r[...] for r in ins[n_r + n_s:]]
            res = fn(*vals)
            for k in range(n_o):
                ors[k][rs, :] = res[k].astype(ors[k].dtype)
            for k in range(len(sums)):
                ors[n_o + k][sub] = res[n_o + k]

    def all_rows(i):
        return (i, 0)

    def lat_rows(i):
        return ((i // lay.bps) * (lay.bps - lay.cb) + jnp.maximum(i % lay.bps - lay.cb, 0), 0)

    in_specs = [pl.BlockSpec((tb, a.shape[1]), all_rows if a.shape[0] == lay.T else lat_rows) for a in rows]
    in_specs += [pl.BlockSpec(a.shape, lambda i: (0, 0, 0)) for a in segs]
    in_specs += [pl.BlockSpec(a.shape, lambda i: (0, 0)) for a in vecs]
    out_shape = [_sds((2 * lay.L if o[2:] else lay.T, o[0]), o[1]) for o in outs]
    out_shape += [_sds((nblk, r, w), F32) for r, w in sums]
    out_specs = [pl.BlockSpec((tb, o[0]), lat_rows if o[2:] else all_rows) for o in outs]
    out_specs += [pl.BlockSpec((nsub, r, w), lambda i: (i, 0, 0)) for r, w in sums]
    sem = "arbitrary" if any(o[2:] for o in outs) else "parallel"
    if rider is None:
        return pl.pallas_call(body, name=name, out_shape=out_shape, grid=(nblk // nsub,), in_specs=in_specs,
                              out_specs=out_specs, compiler_params=_cp((sem,), VMEM_BIG))(*rows, *segs, *vecs)
    return _host_call(body, rider, name, (nblk // nsub,), in_specs, out_specs, out_shape, (*rows, *segs, *vecs), (sem,),
                      n_r + n_s + n_v, n_o + len(sums))


def modulate(lay, h, mod, k_shift, k_scale, name):
    def fn(hb, m):
        return (hb * (1.0 + m[k_scale:k_scale + 1]) + m[k_shift:k_shift + 1],)
    return rowwise(lay, name, fn, [h], segs=[mod], outs=[(D, BF16)])[0]


def resid_ln(lay, h, y, mod, k_gate, coef, lnv, name, nxt=None, rider=None):
    def fn(hb, yb, m, *rest):
        ln = rest[-1]
        z = ALPHA * hb + (coef * m[k_gate:k_gate + 1]) * yb
        mu = jnp.mean(z, axis=-1, keepdims=True)
        zc = z - mu
        var = jnp.mean(zc * zc, axis=-1, keepdims=True)
        rstd = lax.rsqrt(var + LN_EPS)
        xhat = zc * rstd
        out = xhat * ln[0:1] + ln[1:2]
        if nxt is None:
            return out, xhat, rstd
        mn = rest[0]
        return out, xhat, rstd, out * (1.0 + mn[nxt[2]:nxt[2] + 1]) + mn[nxt[1]:nxt[1] + 1]
    segs = [mod] if nxt is None else [mod, nxt[0]]
    outs = [(D, F32), (D, F32), (1, F32)] + ([] if nxt is None else [(D, BF16)])
    return rowwise(lay, name, fn, [h, y], segs=segs, vecs=[lnv], outs=outs, rider=rider)


def _ln_bwd_math(do, xh, rs, yb, gate, coef, ln):
    dxh = do * ln[0:1]
    m1 = jnp.mean(dxh, axis=-1, keepdims=True)
    m2 = jnp.mean(dxh * xh, axis=-1, keepdims=True)
    dz = rs * (dxh - m1 - xh * m2)
    s = jnp.concatenate([jnp.sum(do, axis=0, keepdims=True), jnp.sum(do * xh, axis=0, keepdims=True),
                         jnp.sum(coef * dz * yb, axis=0, keepdims=True)], axis=0)
    return (coef * gate) * dz, ALPHA * dz, s


def _mod_bwd_math(dr, dm, hb, scale):
    s = jnp.concatenate([jnp.sum(dm, axis=0, keepdims=True), jnp.sum(dm * hb, axis=0, keepdims=True)], axis=0)
    return dr + dm * (1.0 + scale), s


def ln_bwd(lay, dout, xhat, rstd, y, mod, k_gate, coef, lnv, name):
    def fn(do, xh, rs, yb, m, ln):
        return _ln_bwd_math(do, xh, rs, yb, m[k_gate:k_gate + 1], coef, ln)
    return rowwise(lay, name, fn, [dout, xhat, rstd, y], segs=[mod], vecs=[lnv],
                   outs=[(D, BF16), (D, F32)], sums=[(3, D)])


def mod_bwd(lay, dres, dhm, h, mod, k_scale, name, rider=None):
    def fn(dr, dm, hb, m):
        return _mod_bwd_math(dr, dm, hb, m[k_scale:k_scale + 1])
    return rowwise(lay, name, fn, [dres, dhm, h], segs=[mod], outs=[(D, F32, "lat")], sums=[(2, D)], rider=rider)


def modb_lnb(lay, dres, dhm, h, mod, k_scale, xhat, rstd, y, mod_p, k_gate, coef, lnv, name, rider=None):
    def fn(dr, dm, hb, xh, rs, yb, m, mp, ln):
        dh, s2 = _mod_bwd_math(dr, dm, hb, m[k_scale:k_scale + 1])
        dy, dres_p, s1 = _ln_bwd_math(dh, xh, rs, yb, mp[k_gate:k_gate + 1], coef, ln)
        return dy, dres_p, s1, s2
    return rowwise(lay, name, fn, [dres, dhm, h, xhat, rstd, y], segs=[mod, mod_p], vecs=[lnv],
                   outs=[(D, BF16), (D, F32)], sums=[(3, D), (2, D)], rider=rider)


def block_sums(lay, parts, name):
    nblk, r, w = parts.shape

    def body(p_ref, o_ref):
        acc = [None, None, None]
        for i in range(nblk):
            sg = 2 if i % lay.bps < lay.cb else i // lay.bps
            acc[sg] = p_ref[i] if acc[sg] is None else acc[sg] + p_ref[i]
        for k in range(3):
            o_ref[k] = acc[k]
        o_ref[3] = (acc[0] + acc[1]) + acc[2]

    return pl.pallas_call(body, name=name, out_shape=_sds((4, r, w), F32), in_specs=[VMEM_SPEC],
                          out_specs=VMEM_SPEC)(parts)


def mm_nn(a, b, name, out_dtype=F32, bias=None):
    m, k = a.shape
    n = b.shape[1]
    tm = _pick(m, (1152, 768, 512, 256, 128, 64, 32, 16, 8))
    tn = _pick(n, (1024, 768, 640, 512, 384, 256, 128))

    def body(*refs):
        if bias is None:
            a_ref, b_ref, o_ref = refs
            o_ref[...] = _nn(a_ref[...].astype(BF16), b_ref[...].astype(BF16)).astype(o_ref.dtype)
        else:
            a_ref, b_ref, c_ref, o_ref = refs
            o_ref[...] = (_nn(a_ref[...].astype(BF16), b_ref[...].astype(BF16)) + c_ref[...]).astype(o_ref.dtype)

    in_specs = [pl.BlockSpec((tm, k), lambda i, j: (i, 0)), pl.BlockSpec((k, tn), lambda i, j: (0, j))]
    ops = [a, b]
    if bias is not None:
        in_specs.append(pl.BlockSpec((1, tn), lambda i, j: (0, j)))
        ops.append(bias)
    return pl.pallas_call(body, name=name, out_shape=_sds((m, n), out_dtype), grid=(m // tm, n // tn),
                          in_specs=in_specs, out_specs=pl.BlockSpec((tm, tn), lambda i, j: (i, j)),
                          compiler_params=_cp(("parallel", "parallel"), VMEM_BIG))(*ops)


def mm_nt(a, b, name, out_dtype=F32):
    m, k = a.shape
    n = b.shape[0]
    tm = _pick(m, (1152, 768, 512, 256, 128, 64, 32, 16, 8))
    tn = _pick(n, (1024, 768, 640, 512, 384, 256, 128))

    def body(a_ref, b_ref, o_ref):
        o_ref[...] = _nt(a_ref[...].astype(BF16), b_ref[...].astype(BF16)).astype(o_ref.dtype)

    return pl.pallas_call(body, name=name, out_shape=_sds((m, n), out_dtype), grid=(m // tm, n // tn),
                          in_specs=[pl.BlockSpec((tm, k), lambda i, j: (i, 0)), pl.BlockSpec((tn, k), lambda i, j: (j, 0))],
                          out_specs=pl.BlockSpec((tm, tn), lambda i, j: (i, j)),
                          compiler_params=_cp(("parallel", "parallel"), VMEM_BIG))(a, b)


def mm_tn(a, b, name):
    t, m = a.shape
    n = b.shape[1]
    tk = _pick(t, (1152, 768, 512, 256, 128, 64, 32, 16))
    tm = _pick(m, (512, 384, 256, 128))

    def body(a_ref, b_ref, o_ref):
        @pl.when(pl.program_id(1) == 0)
        def _():
            o_ref[...] = jnp.zeros_like(o_ref)
        o_ref[...] += _tn(a_ref[...].astype(BF16), b_ref[...].astype(BF16))

    return pl.pallas_call(body, name=name, out_shape=_sds((m, n), F32), grid=(m // tm, t // tk),
                          in_specs=[pl.BlockSpec((tk, tm), lambda i, k: (k, i)), pl.BlockSpec((tk, n), lambda i, k: (k, 0))],
                          out_specs=pl.BlockSpec((tm, n), lambda i, k: (i, 0)),
                          compiler_params=_cp(("parallel", "arbitrary"), VMEM_BIG))(a, b)


class Rider:
    def __init__(self, ins, outs, aliases, nsem, start, wait):
        self.ins, self.outs, self.aliases, self.nsem, self.start, self.wait = ins, outs, aliases, nsem, start, wait


def _chips_of(mx, my):
    return [(1 - mx, my), (mx, 1 - my), (1 - mx, 1 - my)]


def rider_gather_d2d(buf):
    def start(ins, outs, ssem, rsem):
        o = outs[0]
        mx, my, mc = _me()
        for j, (px, py) in enumerate(_chips_of(mx, my)):
            ps = 2 * px + py
            _rcopy(o.at[ps, mc], o.at[ps, mc], ssem.at[j], rsem.at[j], (mx, my, 1 - mc)).start()

    def wait(ins, outs, ssem, rsem):
        o = outs[0]
        mx, my, mc = _me()
        sib = (mx, my, 1 - mc)
        for j, (px, py) in enumerate(_chips_of(mx, my)):
            ps = 2 * px + py
            _rcopy(o.at[ps, 1 - mc], o.at[ps, 1 - mc], ssem.at[j], rsem.at[j], sib).wait_recv()
        for j, (px, py) in enumerate(_chips_of(mx, my)):
            ps = 2 * px + py
            _rcopy(o.at[ps, mc], o.at[ps, mc], ssem.at[j], rsem.at[j], sib).wait_send()

    return Rider([buf], [_sds(buf.shape, buf.dtype)], {0: 0}, 3, start, wait)


def rider_reduce_sib(buf):
    n, _, h, w = buf.shape

    def start(ins, outs, ssem, rsem):
        mx, my, mc = _me()
        for k in range(N_CHIP):
            _rcopy(ins[0].at[k, 1 - mc], outs[0].at[k], ssem.at[k], rsem.at[k], (mx, my, 1 - mc)).start()

    def wait(ins, outs, ssem, rsem):
        mx, my, mc = _me()
        for k in range(N_CHIP):
            _rcopy(ins[0].at[k, 1 - mc], outs[0].at[k], ssem.at[k], rsem.at[k], (mx, my, 1 - mc)).wait_recv()
        for k in range(N_CHIP):
            _rcopy(ins[0].at[k, 1 - mc], outs[0].at[k], ssem.at[k], rsem.at[k], (mx, my, 1 - mc)).wait_send()

    return Rider([buf], [_sds((n, h, w), buf.dtype)], {}, N_CHIP, start, wait)


def rider_gather_xy(buf):
    def peers():
        mx, my, mc = _me()
        return 2 * mx + my, mc, [(1 - mx, my), (mx, 1 - my)]

    def start(ins, outs, ssem, rsem):
        o = outs[0]
        s, mc, nb = peers()
        for j, (px, py) in enumerate(nb):
            _rcopy(o.at[s, mc], o.at[s, mc], ssem.at[j], rsem.at[j], (px, py, mc)).start()

    def wait(ins, outs, ssem, rsem):
        o = outs[0]
        s, mc, nb = peers()
        for j, (px, py) in enumerate(nb):
            _rcopy(o.at[2 * px + py, mc], o.at[2 * px + py, mc], ssem.at[j], rsem.at[j], (px, py, mc)).wait_recv()
        for j, (px, py) in enumerate(nb):
            _rcopy(o.at[s, mc], o.at[s, mc], ssem.at[j], rsem.at[j], (px, py, mc)).wait_send()

    return Rider([buf], [_sds(buf.shape, buf.dtype)], {0: 0}, 2, start, wait)


def rider_gather_fwd(buf):
    def start(ins, outs, ssem, rsem):
        o = outs[0]
        mx, my, mc = _me()
        xs = 2 * (1 - mx) + my
        _rcopy(o.at[xs, mc], o.at[xs, mc], ssem.at[0], rsem.at[0], (mx, 1 - my, mc)).start()

    def wait(ins, outs, ssem, rsem):
        o = outs[0]
        mx, my, mc = _me()
        xs, ds = 2 * (1 - mx) + my, 2 * (1 - mx) + (1 - my)
        _rcopy(o.at[ds, mc], o.at[ds, mc], ssem.at[0], rsem.at[0], (mx, 1 - my, mc)).wait_recv()
        _rcopy(o.at[xs, mc], o.at[xs, mc], ssem.at[0], rsem.at[0], (mx, 1 - my, mc)).wait_send()

    return Rider([buf], [_sds(buf.shape, buf.dtype)], {0: 0}, 1, start, wait)


def rider_reduce_copy(q, j, r=None):
    def peer():
        mx, my, mc = _me()
        px, py = _chips_of(mx, my)[j]
        return 2 * mx + my, 2 * px + py, (px, py, mc)

    def start(ins, outs, ssem, rsem):
        s, ps, dev = peer()
        _rcopy(ins[0].at[ps], outs[0].at[s], ssem.at[0], rsem.at[0], dev).start()

    def wait(ins, outs, ssem, rsem):
        s, ps, dev = peer()
        _rcopy(ins[0].at[ps], outs[0].at[ps], ssem.at[0], rsem.at[0], dev).wait_recv()
        _rcopy(ins[0].at[ps], outs[0].at[s], ssem.at[0], rsem.at[0], dev).wait_send()

    if r is None:
        return Rider([q], [_sds(q.shape, q.dtype)], {}, 1, start, wait)
    return Rider([q, r], [_sds(q.shape, q.dtype)], {1: 0}, 1, start, wait)


def rider_join(buf, g):
    def start(ins, outs, ssem, rsem):
        o = outs[0].at[g]
        mx, my, mc = _me()
        _rcopy(o.at[mc], o.at[mc], ssem.at[0], rsem.at[0], (mx, my, 1 - mc)).start()

    def wait(ins, outs, ssem, rsem):
        o = outs[0].at[g]
        mx, my, mc = _me()
        _rcopy(o.at[1 - mc], o.at[1 - mc], ssem.at[0], rsem.at[0], (mx, my, 1 - mc)).wait_recv()
        _rcopy(o.at[mc], o.at[mc], ssem.at[0], rsem.at[0], (mx, my, 1 - mc)).wait_send()

    return Rider([buf], [_sds(buf.shape, buf.dtype)], {0: 0}, 1, start, wait)


def _host_call(body, rider, name, grid, in_specs, out_specs, out_shape, operands, sem, n_in, n_out, aliases=None):
    aliases = dict(aliases or {})
    if rider is None:
        return pl.pallas_call(body, name=name, out_shape=out_shape, grid=grid, in_specs=in_specs, out_specs=out_specs,
                              input_output_aliases=aliases, compiler_params=_cp(sem, VMEM_BIG))(*operands)
    n_ri, n_ro = len(rider.ins), len(rider.outs)
    aliases.update({n_in + a: n_out + b for a, b in rider.aliases.items()})

    def hosted(*refs):
        ins, r_in = refs[:n_in], refs[n_in:n_in + n_ri]
        outs, r_out = refs[n_in + n_ri:n_in + n_ri + n_out], refs[n_in + n_ri + n_out:n_in + n_ri + n_out + n_ro]
        ssem, rsem = refs[-2], refs[-1]
        first = functools.reduce(lambda a, b: a & b, [pl.program_id(k) == 0 for k in range(len(grid))])
        last = functools.reduce(lambda a, b: a & b, [pl.program_id(k) == grid[k] - 1 for k in range(len(grid))])

        @pl.when(first)
        def _():
            rider.start(r_in, r_out, ssem, rsem)
        body(*ins, *outs)

        @pl.when(last)
        def _():
            rider.wait(r_in, r_out, ssem, rsem)

    return pl.pallas_call(
        hosted, name=name, out_shape=list(out_shape) + list(rider.outs), grid=grid,
        in_specs=list(in_specs) + [ANY] * n_ri, out_specs=list(out_specs) + [ANY] * n_ro,
        input_output_aliases=aliases,
        scratch_shapes=[pltpu.SemaphoreType.DMA((rider.nsem,)), pltpu.SemaphoreType.DMA((rider.nsem,))],
        compiler_params=_cp(("arbitrary",) * len(grid), VMEM_BIG))(*operands, *rider.ins)


def ffn_up(lay, hm, wbuf, ig, iu, ns, name, rider=None):
    tm = lay.tm

    def body(h_ref, wg_ref, wu_ref, sp_ref, sl_ref, u_ref, a_ref):
        hb = h_ref[...]
        g = _nt(hb, wg_ref[0])
        u = _nt(hb, wu_ref[0])
        sg = _sigmoid(g)
        sl = g * sg
        sp_ref[0] = (sg + sl * (1.0 - sg)).astype(BF16)
        sl_ref[0] = sl.astype(BF16)
        u_ref[0] = u.astype(BF16)
        a_ref[0] = (sl * u).astype(BF16)

    spec_o = pl.BlockSpec((1, tm, ns), lambda s, i: (s, i, 0))
    return _host_call(
        body, rider, name, (N_CHIP, lay.T // tm),
        [pl.BlockSpec((tm, D), lambda s, i: (i, 0)), pl.BlockSpec((1, ns, D), lambda s, i: (s, ig, 0)),
         pl.BlockSpec((1, ns, D), lambda s, i: (s, iu, 0))],
        [spec_o] * 4, [_sds((N_CHIP, lay.T, ns), BF16)] * 4, (hm, wbuf, wbuf), ("parallel", "parallel"), 3, 4)


def slab_nn_acc(lay, zs, wbuf, idxs, ns, name, rider=None):
    tm = lay.tm
    npair = len(zs)

    def body(*refs):
        o_ref = refs[-1]

        @pl.when(pl.program_id(1) == 0)
        def _():
            o_ref[...] = jnp.zeros_like(o_ref)
        acc = _nn(refs[0][0], refs[npair][0])
        for p in range(1, npair):
            acc += _nn(refs[p][0], refs[npair + p][0])
        o_ref[...] += acc

    in_specs = [pl.BlockSpec((1, tm, ns), lambda i, s: (s, i, 0)) for _ in zs]
    in_specs += [pl.BlockSpec((1, ns, D), functools.partial(lambda i, s, q: (s, q, 0), q=q)) for q in idxs]
    return _host_call(body, rider, name, (lay.T // tm, N_CHIP), in_specs, [pl.BlockSpec((tm, D), lambda i, s: (i, 0))],
                      [_sds((lay.T, D), F32)], (*zs, *([wbuf] * npair)), ("parallel", "arbitrary"), 2 * npair, 1)


def ffn_bwd_da(lay, dy, wbuf, idn, sp, sl, u, ns, name, rider=None):
    tm = lay.tm

    def body(dy_ref, wd_ref, sp_ref, sl_ref, u_ref, dg_ref, du_ref):
        da = _nt(dy_ref[...], wd_ref[0])
        dg_ref[0] = (da * u_ref[0].astype(F32) * sp_ref[0].astype(F32)).astype(BF16)
        du_ref[0] = (da * sl_ref[0].astype(F32)).astype(BF16)

    spec_z = pl.BlockSpec((1, tm, ns), lambda s, i: (s, i, 0))
    return _host_call(
        body, rider, name, (N_CHIP, lay.T // tm),
        [pl.BlockSpec((tm, D), lambda s, i: (i, 0)), pl.BlockSpec((1, ns, D), lambda s, i: (s, idn, 0)),
         spec_z, spec_z, spec_z],
        [spec_z] * 2, [_sds((N_CHIP, lay.T, ns), BF16)] * 2, (dy, wbuf, sp, sl, u), ("parallel", "parallel"), 5, 2)


def slab_tn(lay, z, x, gbuf, idx, ns, name, rider=None):
    tk = lay.tm

    def body(z_ref, x_ref, g_in, o_ref):
        del g_in

        @pl.when(pl.program_id(1) == 0)
        def _():
            o_ref[...] = jnp.zeros_like(o_ref)
        o_ref[0] += _tn(z_ref[0], x_ref[...])

    return _host_call(
        body, rider, name, (N_CHIP, lay.T // tk),
        [pl.BlockSpec((1, tk, ns), lambda s, k: (s, k, 0)), pl.BlockSpec((tk, D), lambda s, k: (k, 0)), ANY],
        [pl.BlockSpec((1, ns, D), lambda s, k: (s, idx, 0))], [_sds(gbuf.shape, F32)], (z, x, gbuf),
        ("parallel", "arbitrary"), 3, 1, aliases={2: 0})


Q0, K0, V0, U0, QR0, KR0, PEXT = 0, 512, 640, 768, 1280, 1792, 1920


def rope_fwd(lay, p, cos, sin, name):
    def fn(pb, cs, sn):
        cs4 = jnp.concatenate([cs] * 4, axis=1)
        sn4 = jnp.concatenate([sn] * 4, axis=1)
        qr = pb[:, Q0:K0] * cs4 + pb[:, QR0:KR0] * sn4
        kr = pb[:, K0:V0] * cs + pb[:, KR0:PEXT] * sn
        return qr, kr, pb[:, V0:U0], pb[:, U0:QR0]
    return rowwise(lay, name, fn, [p, cos, sin], outs=[(ATT_W, BF16), (KV_W, BF16), (KV_W, BF16), (POOL_W, F32)])


def rope_bwd(lay, dqr, dkr, dv, du, cos, sin, name):
    def fn(dq, dk, dvb, dub, cs, sn):
        cs4 = jnp.concatenate([cs] * 4, axis=1)
        sn4 = jnp.concatenate([sn] * 4, axis=1)
        return (jnp.concatenate([dq * cs4, dk * cs, dvb, dub, dq * sn4, dk * sn], axis=1),)
    return rowwise(lay, name, fn, [dqr, dkr, dv, du, cos, sin], outs=[(PEXT, BF16)])[0]


def _attn_specs(lay):
    nbs, cbk, lbk = lay.PS // BLK, lay.C // BLK, lay.L // BLK

    def kv_map(j):
        return lambda s, n: (s * nbs + cbk + jnp.clip(n - cbk + j - 1, 0, lbk - 1), 0)

    win = [pl.BlockSpec((BLK, KV_W), kv_map(j)) for j in range(3)]
    ctx = pl.BlockSpec((lay.C, KV_W), lambda s, n: (s * (lay.PS // lay.C), 0))
    return nbs, cbk, lbk, win, ctx


def _attn_masks(n, cbk, lbk):
    row = lax.broadcasted_iota(jnp.int32, (BLK, BLK), 0)
    col = lax.broadcasted_iota(jnp.int32, (BLK, BLK), 1)
    m = n - cbk
    lat = n >= cbk
    valid = [lat & (m >= 1) & (col >= row), lat & (col >= 0), lat & (m <= lbk - 2) & (col <= row)]
    lane_lo = lax.broadcasted_iota(jnp.int32, (BLK, 2 * HEAD_DIM), 1) < HEAD_DIM
    return valid, lane_lo


def attn_fwd(lay, qr, kr, vb, sink_tab, name):
    nbs, cbk, lbk, win, ctx = _attn_specs(lay)

    def body(q_ref, k0, k1, k2, kc_ref, v0, v1, v2, vc_ref, sk_ref, o_ref, l_ref):
        n = pl.program_id(1)
        valid, lane_lo = _attn_masks(n, cbk, lbk)
        valid4 = [jnp.concatenate([v] * 4, axis=0) for v in valid]
        ks = [k0[...], k1[...], k2[...]]
        vs = [v0[...], v1[...], v2[...]]
        kc, vc = kc_ref[...], vc_ref[...]
        q2s = [q_ref[:, p * 128:(p + 1) * 128] for p in range(4)]
        outs, lses = [], []
        for hh in range(2):
            sel = lane_lo == (hh == 0)
            qm = jnp.concatenate([jnp.where(sel, q2, jnp.zeros_like(q2)) for q2 in q2s], axis=0)
            sk = jnp.concatenate([jnp.broadcast_to(sk_ref[p:p + 1, hh * HEAD_DIM:hh * HEAD_DIM + 1], (BLK, 1))
                                  for p in range(4)], axis=0)
            sw = [jnp.where(valid4[j], _nt(qm, ks[j]) * ATT_SCALE, NEG_INF) for j in range(3)]
            sc = _nt(qm, kc) * ATT_SCALE
            mx = jnp.maximum(jnp.maximum(jnp.maximum(sw[0].max(-1, keepdims=True), sw[1].max(-1, keepdims=True)),
                                         jnp.maximum(sw[2].max(-1, keepdims=True), sc.max(-1, keepdims=True))), sk)
            ew = [jnp.exp(s - mx) for s in sw]
            ec = jnp.exp(sc - mx)
            den = ew[0].sum(-1, keepdims=True) + ew[1].sum(-1, keepdims=True) + ew[2].sum(-1, keepdims=True)
            den = den + ec.sum(-1, keepdims=True) + jnp.exp(sk - mx)
            o = _nn((ec / den).astype(BF16), vc)
            for j in range(3):
                o += _nn((ew[j] / den).astype(BF16), vs[j])
            outs.append(o)
            lses.append(mx + jnp.log(den))
        for p in range(4):
            rows = slice(p * BLK, (p + 1) * BLK)
            o_ref[:, p * 128:(p + 1) * 128] = jnp.where(lane_lo, outs[0][rows], outs[1][rows]).astype(o_ref.dtype)
            l_ref[:, p * 128:(p + 1) * 128] = jnp.where(lane_lo, jnp.broadcast_to(lses[0][rows], (BLK, 128)),
                                                        jnp.broadcast_to(lses[1][rows], (BLK, 128)))

    qspec = pl.BlockSpec((BLK, ATT_W), lambda s, n: (s * nbs + n, 0))
    return pl.pallas_call(
        body, name=name, out_shape=[_sds((lay.T, ATT_W), BF16), _sds((lay.T, ATT_W), F32)], grid=(2, nbs),
        in_specs=[qspec] + win + [ctx] + win + [ctx] + [pl.BlockSpec((8, 128), lambda s, n: (0, 0))],
        out_specs=[qspec, qspec], compiler_params=_cp(("parallel", "parallel")))(qr, kr, kr, kr, kr, vb, vb, vb, vb, sink_tab)


def attn_bwd(lay, qr, kr, vb, sink_tab, lse, datt, name):
    nbs, cbk, lbk, win, ctx = _attn_specs(lay)
    C, PS = lay.C, lay.PS

    def body(q_ref, k0, k1, k2, kc_ref, v0, v1, v2, vc_ref, sk_ref, l_ref, do_ref, dq_ref, dk_ref, dv_ref, ds_ref):
        n = pl.program_id(1)
        valid, lane_lo = _attn_masks(n, cbk, lbk)

        @pl.when(n == 0)
        def _():
            dk_ref[...] = jnp.zeros_like(dk_ref)
            dv_ref[...] = jnp.zeros_like(dv_ref)
            ds_ref[...] = jnp.zeros_like(ds_ref)

        ks = [k0[...], k1[...], k2[...], kc_ref[...]]
        vs = [v0[...], v1[...], v2[...], vc_ref[...]]
        valid4 = [jnp.concatenate([v] * 4, axis=0) for v in valid]
        dks = [jnp.zeros((BLK, KV_W), F32)] * 3 + [jnp.zeros((C, KV_W), F32)]
        dvs = list(dks)
        q2s = [q_ref[:, p * 128:(p + 1) * 128] for p in range(4)]
        do2s = [do_ref[:, p * 128:(p + 1) * 128].astype(BF16) for p in range(4)]
        lse2s = [l_ref[:, p * 128:(p + 1) * 128] for p in range(4)]
        dq_h, dd_h = [], []
        for hh in range(2):
            sel = lane_lo == (hh == 0)
            qm = jnp.concatenate([jnp.where(sel, q2, jnp.zeros_like(q2)) for q2 in q2s], axis=0)
            dom = jnp.concatenate([jnp.where(sel, d2, jnp.zeros_like(d2)) for d2 in do2s], axis=0)
            lse_h = jnp.concatenate([l2[:, hh * HEAD_DIM:hh * HEAD_DIM + 1] for l2 in lse2s], axis=0)
            ps, dps = [], []
            for j in range(4):
                s = _nt(qm, ks[j]) * ATT_SCALE
                if j < 3:
                    s = jnp.where(valid4[j], s, NEG_INF)
                ps.append(jnp.exp(s - lse_h))
                dps.append(_nt(dom, vs[j]))
            dd = (ps[0] * dps[0]).sum(-1, keepdims=True) + (ps[1] * dps[1]).sum(-1, keepdims=True)
            dd = dd + (ps[2] * dps[2]).sum(-1, keepdims=True) + (ps[3] * dps[3]).sum(-1, keepdims=True)
            dq = jnp.zeros((4 * BLK, 128), F32)
            for j in range(4):
                dsb = (ps[j] * (dps[j] - dd) * ATT_SCALE).astype(BF16)
                dq += _nn(dsb, ks[j])
                dks[j] = dks[j] + _tn(dsb, qm)
                dvs[j] = dvs[j] + _tn(ps[j].astype(BF16), dom)
            dq_h.append(dq)
            dd_h.append(dd)
        for p in range(4):
            sl = slice(p * 128, (p + 1) * 128)
            rows = slice(p * BLK, (p + 1) * BLK)
            dq_ref[:, sl] = jnp.where(lane_lo, dq_h[0][rows], dq_h[1][rows])
            dd2 = jnp.where(lane_lo, jnp.broadcast_to(dd_h[0][rows], (BLK, 128)), jnp.broadcast_to(dd_h[1][rows], (BLK, 128)))
            psink = jnp.exp(sk_ref[p:p + 1, :] - lse2s[p])
            ds_ref[0, p:p + 1, :] += -jnp.sum(psink * dd2, axis=0, keepdims=True)
        dk_ref[0:C, :] += dks[3]
        dv_ref[0:C, :] += dvs[3]
        for j in range(3):
            r0 = pl.multiple_of((cbk + jnp.clip(n - cbk + j - 1, 0, lbk - 1)) * BLK, BLK)
            dk_ref[pl.ds(r0, BLK), :] += dks[j]
            dv_ref[pl.ds(r0, BLK), :] += dvs[j]

    qspec = pl.BlockSpec((BLK, ATT_W), lambda s, n: (s * nbs + n, 0))
    kvout = pl.BlockSpec((PS, KV_W), lambda s, n: (s, 0))
    return pl.pallas_call(
        body, name=name,
        out_shape=[_sds((lay.T, ATT_W), F32), _sds((lay.T, KV_W), F32), _sds((lay.T, KV_W), F32), _sds((2, 8, 128), F32)],
        grid=(2, nbs),
        in_specs=[qspec] + win + [ctx] + win + [ctx] + [pl.BlockSpec((8, 128), lambda s, n: (0, 0)), qspec, qspec],
        out_specs=[qspec, kvout, kvout, pl.BlockSpec((1, 8, 128), lambda s, n: (s, 0, 0))],
        compiler_params=_cp(("parallel", "arbitrary")))(qr, kr, kr, kr, kr, vb, vb, vb, vb, sink_tab, lse, datt)


def _winsum(x, r):
    n = x.shape[0]
    t = lax.broadcasted_iota(jnp.int32, x.shape, 0)
    acc = x
    for o in range(1, r + 1):
        acc = acc + jnp.where(t >= o, pltpu.roll(x, o, 0), 0.0) + jnp.where(t < n - o, pltpu.roll(x, n - o, 0), 0.0)
    return acc


def _wincount(n, r):
    t = lax.broadcasted_iota(jnp.int32, (n, 128), 0)
    return (jnp.minimum(t + r, n - 1) - jnp.maximum(t - r, 0) + 1).astype(F32)


def pool_fwd(lay, u, w_pool, scale, name):
    segs = [(0, lay.C), (lay.C, lay.L)]

    def body(u_ref, w_ref, s_ref, o_ref):
        for r0, n in segs:
            for g, wd in enumerate(POOL_WINDOWS):
                sl = slice(g * 128, (g + 1) * 128)
                x = u_ref[r0:r0 + n, sl]
                d = _winsum(x, wd // 2) / _wincount(n, wd // 2) - x
                y = _nn(d.astype(BF16), w_ref[g].astype(BF16)) * s_ref[:, sl]
                o_ref[r0:r0 + n, sl] = y.astype(o_ref.dtype)

    spec = pl.BlockSpec((lay.PS, POOL_W), lambda s: (s, 0))
    return pl.pallas_call(
        body, name=name, out_shape=_sds((lay.T, POOL_W), BF16), grid=(2,),
        in_specs=[spec, pl.BlockSpec(w_pool.shape, lambda s: (0, 0, 0)), pl.BlockSpec((1, POOL_W), lambda s: (0, 0))],
        out_specs=spec, compiler_params=_cp(("parallel",), VMEM_BIG))(u, w_pool, scale)


def pool_bwd(lay, u, dcat, w_pool, scale, name):
    segs = [(0, lay.C), (lay.C, lay.L)]

    def body(u_ref, dp_ref, w_ref, s_ref, du_ref, dw_ref, dsc_ref):
        for g, wd in enumerate(POOL_WINDOWS):
            sl = slice(g * 128, (g + 1) * 128)
            wb = w_ref[g].astype(BF16)
            dw = jnp.zeros((128, 128), F32)
            dsc = jnp.zeros((1, 128), F32)
            for r0, n in segs:
                x = u_ref[r0:r0 + n, sl]
                cnt = _wincount(n, wd // 2)
                d = (_winsum(x, wd // 2) / cnt - x).astype(BF16)
                dp = dp_ref[r0:r0 + n, sl]
                dsc += jnp.sum(_nn(d, wb) * dp, axis=0, keepdims=True)
                dyp = (dp * s_ref[:, sl]).astype(BF16)
                dw += _tn(d, dyp)
                dd = _nt(dyp, wb)
                du_ref[r0:r0 + n, sl] = _winsum(dd / cnt, wd // 2) - dd
            dw_ref[0, g] = dw
            dsc_ref[0, :, sl] = dsc

    spec = pl.BlockSpec((lay.PS, POOL_W), lambda s: (s, 0))
    return pl.pallas_call(
        body, name=name,
        out_shape=[_sds((lay.T, POOL_W), F32), _sds((2, 4, 128, 128), F32), _sds((2, 1, POOL_W), F32)], grid=(2,),
        in_specs=[spec, pl.BlockSpec((lay.PS, POOL_W), lambda s: (s, 1)), pl.BlockSpec(w_pool.shape, lambda s: (0, 0, 0)),
                  pl.BlockSpec((1, POOL_W), lambda s: (0, 0))],
        out_specs=[spec, pl.BlockSpec((1, 4, 128, 128), lambda s: (s, 0, 0, 0)), pl.BlockSpec((1, 1, POOL_W), lambda s: (s, 0, 0))],
        compiler_params=_cp(("parallel",), VMEM_BIG))(u, dcat, w_pool, scale)


CONV_OFFS = (-1, 0, 1, 2)
CW = 256


def _shift_rows(x, o):
    if o == 0:
        return x
    n = x.shape[0]
    t = lax.broadcasted_iota(jnp.int32, x.shape, 0)
    if o < 0:
        return jnp.where(t >= -o, pltpu.roll(x, -o, 0), 0.0)
    return jnp.where(t < n - o, pltpu.roll(x, n - o, 0), 0.0)


def conv_fwd(lay, p, col0, w, b, name):
    segs = [(0, lay.C), (lay.C, lay.L)]
    cb0 = col0 // CW

    def body(x_ref, w_ref, b_ref, o_ref):
        for r0, n in segs:
            x = x_ref[r0:r0 + n, :]
            y = jnp.broadcast_to(b_ref[...], x.shape)
            for k, o in enumerate(CONV_OFFS):
                y = y + _shift_rows(x, o) * w_ref[k:k + 1, :]
            o_ref[r0:r0 + n, :] = y

    return pl.pallas_call(
        body, name=name, out_shape=_sds((lay.T, D), F32), grid=(2, D // CW),
        in_specs=[pl.BlockSpec((lay.PS, CW), lambda s, j: (s, cb0 + j)), pl.BlockSpec((4, CW), lambda s, j: (0, j)),
                  pl.BlockSpec((1, CW), lambda s, j: (0, j))],
        out_specs=pl.BlockSpec((lay.PS, CW), lambda s, j: (s, j)),
        compiler_params=_cp(("parallel", "parallel")))(p, w, b)


def conv_bwd(lay, p, col0, w, duc, name):
    segs = [(0, lay.C), (lay.C, lay.L)]
    cb0 = col0 // CW

    def body(x_ref, w_ref, g_ref, du_ref, dw_ref, db_ref):
        dws = [jnp.zeros((1, CW), F32)] * 4
        db = jnp.zeros((1, CW), F32)
        for r0, n in segs:
            x = x_ref[r0:r0 + n, :]
            g = g_ref[r0:r0 + n, :]
            du = jnp.zeros_like(g)
            for k, o in enumerate(CONV_OFFS):
                du = du + _shift_rows(g, -o) * w_ref[k:k + 1, :]
                dws[k] = dws[k] + jnp.sum(g * _shift_rows(x, o), axis=0, keepdims=True)
            db = db + jnp.sum(g, axis=0, keepdims=True)
            du_ref[r0:r0 + n, :] = du.astype(du_ref.dtype)
        dw_ref[0] = jnp.concatenate(dws, axis=0)
        db_ref[0] = db

    return pl.pallas_call(
        body, name=name, out_shape=[_sds((lay.T, D), BF16), _sds((2, 4, D), F32), _sds((2, 1, D), F32)], grid=(2, D // CW),
        in_specs=[pl.BlockSpec((lay.PS, CW), lambda s, j: (s, cb0 + j)), pl.BlockSpec((4, CW), lambda s, j: (0, j)),
                  pl.BlockSpec((lay.PS, CW), lambda s, j: (s, j))],
        out_specs=[pl.BlockSpec((lay.PS, CW), lambda s, j: (s, j)), pl.BlockSpec((1, 4, CW), lambda s, j: (s, 0, j)),
                   pl.BlockSpec((1, 1, CW), lambda s, j: (s, 0, j))],
        compiler_params=_cp(("parallel", "parallel")))(p, w, duc)


def _softplus_neg(lam):
    z = -lam
    w = jnp.exp(-jnp.abs(z))
    log1p = jnp.where(w < 1e-2, w * (1.0 - w * (0.5 - w / 3.0)), jnp.log(1.0 + w))
    return jnp.maximum(z, 0.0) + log1p, -_sigmoid(z)


def _neg_expm1(x):
    series = -x * (1.0 + x * (0.5 + x * (1.0 / 6.0 + x * (1.0 / 24.0 + x * (1.0 / 120.0)))))
    return jnp.where(x > -0.05, series, 1.0 - jnp.exp(x))


def _lru_gates(x, xb, wa, wx, ba, bx, lam):
    r = _sigmoid(_nn(xb, wa.astype(BF16)) + ba)
    gi = _sigmoid(_nn(xb, wx.astype(BF16)) + bx)
    sp, dsp = _softplus_neg(lam)
    la = -LRU_C * r * sp
    a = jnp.exp(la)
    sq = jnp.sqrt(_neg_expm1(2.0 * la))
    return r, gi, sp, dsp, a, sq


def lru_coeffs(lay, uc, wa, wx, vec, name):
    tr = lay.tc

    def body(x_ref, wa_ref, wx_ref, v_ref, a_ref, b_ref):
        for h in range(8):
            sl = slice(h * 128, (h + 1) * 128)
            x = x_ref[:, sl]
            xb = x.astype(BF16)
            for d in range(2):
                _, gi, _, _, a, sq = _lru_gates(x, xb, wa_ref[d, h], wx_ref[d, h], v_ref[d:d + 1, sl],
                                                v_ref[2 + d:3 + d, sl], v_ref[4 + d:5 + d, sl])
                a_ref[d, h] = a
                b_ref[d, h] = sq * (gi * x)

    wspec = pl.BlockSpec((2, 8, 128, 128), lambda i: (0, 0, 0, 0))
    ospec = pl.BlockSpec((2, 8, tr, 128), lambda i: (0, 0, i, 0))
    return pl.pallas_call(
        body, name=name, out_shape=[_sds((2, 8, lay.T, 128), F32)] * 2, grid=(lay.T // tr,),
        in_specs=[pl.BlockSpec((tr, D), lambda i: (i, 0)), wspec, wspec, pl.BlockSpec((6, D), lambda i: (0, 0))],
        out_specs=[ospec, ospec], compiler_params=_cp(("parallel",), VMEM_BIG))(uc, wa, wx, vec)


def lru_coeffs_bwd(lay, uc, wa, wx, vec, da, db, name):
    tr = lay.tc

    def body(x_ref, wa_ref, wx_ref, v_ref, da_ref, db_ref, dx_ref, dwa_ref, dwx_ref, dv_ref):
        @pl.when(pl.program_id(0) == 0)
        def _():
            dwa_ref[...] = jnp.zeros_like(dwa_ref)
            dwx_ref[...] = jnp.zeros_like(dwx_ref)
            dv_ref[...] = jnp.zeros_like(dv_ref)

        for h in range(8):
            sl = slice(h * 128, (h + 1) * 128)
            x = x_ref[:, sl]
            xb = x.astype(BF16)
            dx = jnp.zeros_like(x)
            for d in range(2):
                wab, wxb = wa_ref[d, h].astype(BF16), wx_ref[d, h].astype(BF16)
                r, gi, sp, dsp, a, sq = _lru_gates(x, xb, wa_ref[d, h], wx_ref[d, h], v_ref[d:d + 1, sl],
                                                   v_ref[2 + d:3 + d, sl], v_ref[4 + d:5 + d, sl])
                dbv, dav = db_ref[d, h], da_ref[d, h]
                t1 = dbv * sq
                dgi = t1 * x
                dx = dx + t1 * gi
                dla = dav * a - (dbv * gi * x) * (a * a) / sq
                dr = dla * (-LRU_C * sp)
                dlam = jnp.sum(dla * (-LRU_C * r), axis=0, keepdims=True) * dsp
                dpa = dr * r * (1.0 - r)
                dpx = dgi * gi * (1.0 - gi)
                dpab, dpxb = dpa.astype(BF16), dpx.astype(BF16)
                dwa_ref[d, h] += _tn(xb, dpab)
                dwx_ref[d, h] += _tn(xb, dpxb)
                dx = dx + _nt(dpab, wab) + _nt(dpxb, wxb)
                dv_ref[d:d + 1, sl] += jnp.sum(dpa, axis=0, keepdims=True)
                dv_ref[2 + d:3 + d, sl] += jnp.sum(dpx, axis=0, keepdims=True)
                dv_ref[4 + d:5 + d, sl] += dlam
            dx_ref[:, sl] = dx

    wspec = pl.BlockSpec((2, 8, 128, 128), lambda i: (0, 0, 0, 0))
    gspec = pl.BlockSpec((2, 8, tr, 128), lambda i: (0, 0, i, 0))
    vspec = pl.BlockSpec((6, D), lambda i: (0, 0))
    xspec = pl.BlockSpec((tr, D), lambda i: (i, 0))
    return pl.pallas_call(
        body, name=name,
        out_shape=[_sds((lay.T, D), F32), _sds((2, 8, 128, 128), F32), _sds((2, 8, 128, 128), F32), _sds((6, D), F32)],
        grid=(lay.T // tr,), in_specs=[xspec, wspec, wspec, vspec, gspec, gspec],
        out_specs=[xspec, wspec, wspec, vspec], compiler_params=_cp(("arbitrary",), VMEM_BIG))(uc, wa, wx, vec, da, db)


GB = 2
SCAN_UNROLL = 4


def _tile_scan(a, b, up):
    t = lax.broadcasted_iota(jnp.int32, a.shape, 0)
    for d in (1, 2, 4):
        sh = 8 - d if up else d
        m = (t < 8 - d) if up else (t >= d)
        a_prev, b_prev = pltpu.roll(a, sh, 0), pltpu.roll(b, sh, 0)
        b = jnp.where(m, a * b_prev + b, b)
        a = jnp.where(m, a * a_prev, a)
    return a, b


def lru_scan(lay, a, b, name):
    segs = [(0, lay.C), (lay.C, lay.L)]

    def body(a_ref, b_ref, s_ref):
        for d in range(2):
            rev = d == 1
            state = tuple(jnp.zeros((1, 128), F32) for _ in range(GB))
            for base, n in segs:
                nt = n // 8

                def step(j, c, base=base, nt=nt, rev=rev, d=d):
                    c = list(c)
                    for u in range(SCAN_UNROLL):
                        jj = j * SCAN_UNROLL + u
                        r0 = pl.multiple_of(base + 8 * ((nt - 1 - jj) if rev else jj), 8)
                        for g in range(GB):
                            at, bt = _tile_scan(a_ref[d, g, pl.ds(r0, 8), :], b_ref[d, g, pl.ds(r0, 8), :], rev)
                            h = at * c[g] + bt
                            s_ref[d, g, pl.ds(r0, 8), :] = h
                            c[g] = h[0:1] if rev else h[7:8]
                    return tuple(c)

                state = lax.fori_loop(0, nt // SCAN_UNROLL, step, state)

    spec = pl.BlockSpec((2, GB, lay.PS, 128), lambda s, hb: (0, hb, s, 0))
    return pl.pallas_call(
        body, name=name, out_shape=_sds((2, 8, lay.T, 128), F32), grid=(2, 8 // GB),
        in_specs=[spec, spec], out_specs=spec, compiler_params=_cp(("parallel", "parallel"), VMEM_BIG))(a, b)


def lru_scan_bwd(lay, a, s, dy, name):
    segs = [(0, lay.C), (lay.C, lay.L)]
    C, PS = lay.C, lay.PS

    def body(a_ref, s_ref, g_ref, da_ref, db_ref):
        t = lax.broadcasted_iota(jnp.int32, (8, 128), 0)
        for d in range(2):
            rev = d == 1
            carry = tuple(jnp.zeros((1, 128), F32) for _ in range(GB))
            for si in (1, 0):
                base, n = segs[si]
                nt = n // 8

                def step(j, c, base=base, nt=nt, rev=rev, d=d):
                    c = list(c)
                    for u in range(SCAN_UNROLL):
                        jj = j * SCAN_UNROLL + u
                        r0 = pl.multiple_of(base + 8 * (jj if rev else (nt - 1 - jj)), 8)
                        if rev:
                            rn = pl.multiple_of(jnp.where(r0 == PS - 8, 0, r0 + 8), 8)
                            nb_zero = r0 == C - 8
                        else:
                            rn = pl.multiple_of(jnp.maximum(r0 - 8, 0), 8)
                            nb_zero = r0 == 0
                        for g in range(GB):
                            av = a_ref[d, g, pl.ds(r0, 8), :]
                            gv = g_ref[g, pl.ds(r0, 8), :]
                            sv = s_ref[d, g, pl.ds(r0, 8), :]
                            nbt = s_ref[d, g, pl.ds(rn, 8), :]
                            at, bt = _tile_scan(av, av * gv, not rev)
                            m = at * c[g] + bt
                            if rev:
                                m_next = jnp.where(t >= 1, pltpu.roll(m, 1, 0), c[g])
                                nb = jnp.where(nb_zero, 0.0, nbt[0:1])
                                h_prev = jnp.where(t < 7, pltpu.roll(sv, 7, 0), nb)
                                c[g] = m[7:8]
                            else:
                                m_next = jnp.where(t < 7, pltpu.roll(m, 7, 0), c[g])
                                nb = jnp.where(nb_zero, 0.0, nbt[7:8])
                                h_prev = jnp.where(t >= 1, pltpu.roll(sv, 1, 0), nb)
                                c[g] = m[0:1]
                            lam = gv + m_next
                            db_ref[d, g, pl.ds(r0, 8), :] = lam
                            da_ref[d, g, pl.ds(r0, 8), :] = lam * h_prev
                    return tuple(c)

                carry = lax.fori_loop(0, nt // SCAN_UNROLL, step, carry)

    spec = pl.BlockSpec((2, GB, lay.PS, 128), lambda s, hb: (0, hb, s, 0))
    return pl.pallas_call(
        body, name=name, out_shape=[_sds((2, 8, lay.T, 128), F32)] * 2, grid=(2, 8 // GB),
        in_specs=[spec, spec, pl.BlockSpec((GB, lay.PS, 128), lambda s, hb: (hb, s, 0))],
        out_specs=[spec, spec], compiler_params=_cp(("parallel", "parallel"), VMEM_BIG))(a, s, dy)


def _gelu(x):
    k = math.sqrt(2.0 / math.pi)
    t = jnp.tanh(k * (x + 0.044715 * x * x * x))
    return 0.5 * x * (1.0 + t), 0.5 * (1.0 + t) + 0.5 * x * (1.0 - t * t) * k * (1.0 + 3 * 0.044715 * x * x)


def lru_gate(lay, p, s, name):
    tr = lay.tr

    def body(g_ref, s_ref, o_ref):
        for h in range(8):
            sl = slice(h * 128, (h + 1) * 128)
            o_ref[:, sl] = (_gelu(g_ref[:, sl])[0] * (s_ref[0, h] + s_ref[1, h])).astype(o_ref.dtype)

    return pl.pallas_call(
        body, name=name, out_shape=_sds((lay.T, D), BF16), grid=(lay.nblk,),
        in_specs=[pl.BlockSpec((tr, D), lambda i: (i, 0)), pl.BlockSpec((2, 8, tr, 128), lambda i: (0, 0, i, 0))],
        out_specs=pl.BlockSpec((tr, D), lambda i: (i, 0)), compiler_params=_cp(("parallel",)))(p, s)


def lru_gate_bwd(lay, p, s, do, name):
    tr = lay.tr

    def body(g_ref, s_ref, do_ref, dg_ref, dy_ref):
        for h in range(8):
            sl = slice(h * 128, (h + 1) * 128)
            ge, dge = _gelu(g_ref[:, sl])
            dov = do_ref[:, sl]
            dg_ref[:, sl] = (dov * (s_ref[0, h] + s_ref[1, h]) * dge).astype(dg_ref.dtype)
            dy_ref[h] = dov * ge

    xspec = pl.BlockSpec((tr, D), lambda i: (i, 0))
    return pl.pallas_call(
        body, name=name, out_shape=[_sds((lay.T, D), BF16), _sds((8, lay.T, 128), F32)], grid=(lay.nblk,),
        in_specs=[xspec, pl.BlockSpec((2, 8, tr, 128), lambda i: (0, 0, i, 0)), xspec],
        out_specs=[xspec, pl.BlockSpec((8, tr, 128), lambda i: (0, i, 0))],
        compiler_params=_cp(("parallel",)))(p, s, do)


def silu_rows(x, name):
    def body(x_ref, o_ref):
        v = x_ref[...]
        o_ref[...] = (v * _sigmoid(v)).astype(o_ref.dtype)
    return pl.pallas_call(body, name=name, out_shape=_sds(x.shape, BF16), in_specs=[VMEM_SPEC], out_specs=VMEM_SPEC)(x)


def mod_grad_rows(gath, name):
    w = gath.shape[-1]

    def body(g_ref, dm_ref, db_ref):
        dm_ref[...] = jnp.zeros_like(dm_ref)
        for l in range(2):
            ctx = g_ref[0, 3 * l + 2:3 * l + 3, :]
            tot = g_ref[0, 3 * l:3 * l + 1, :] + g_ref[0, 3 * l + 1:3 * l + 2, :]
            for k in range(8):
                dm_ref[l, 2 * k:2 * k + 2, :] = g_ref[k, 3 * l:3 * l + 2, :]
                if k:
                    ctx = ctx + g_ref[k, 3 * l + 2:3 * l + 3, :]
                    tot = tot + (g_ref[k, 3 * l:3 * l + 1, :] + g_ref[k, 3 * l + 1:3 * l + 2, :])
            dm_ref[l, 16:17, :] = ctx
            db_ref[l:l + 1, :] = tot + ctx

    return pl.pallas_call(body, name=name, out_shape=[_sds((2, 32, w), F32), _sds((2, w), F32)],
                          in_specs=[VMEM_SPEC], out_specs=[VMEM_SPEC, VMEM_SPEC])(gath)


def cctx_grad(p, c_ctx, name):
    def body(a_ref, c_ref, o_ref):
        cv = c_ref[...]
        sg = _sigmoid(cv)
        o_ref[...] = 0.5 * (a_ref[0, 0:1, :] + a_ref[1, 0:1, :]) * (sg * (1.0 + cv * (1.0 - sg)))
    return pl.pallas_call(body, name=name, out_shape=_sds((1, D), F32), in_specs=[VMEM_SPEC] * 2,
                          out_specs=VMEM_SPEC)(p, c_ctx)


def loss_and_grad(lay, h, tgt, name):
    def fn(hb, tb):
        lat = (pl.program_id(0) % lay.bps) >= lay.cb
        e = jnp.where(lat, hb - tb, 0.0)
        return e * (1.0 / D), jnp.sum(e * e, axis=0, keepdims=True) * (0.5 / D)
    return rowwise(lay, name, fn, [h, tgt], outs=[(D, F32)], sums=[(1, D)])


def adamw(w, g, m, v, name):
    shape = w.shape
    w2, g2, m2, v2 = (t.reshape(-1, shape[-1]) for t in (w, g, m, v))
    rows, width = w2.shape
    tr = 256 if rows % 256 == 0 else rows
    c1 = 1.0 - ADAM_B1 ** ADAM_STEP
    c2 = 1.0 - ADAM_B2 ** ADAM_STEP

    def body(w_ref, g_ref, m_ref, v_ref, d_ref, mo_ref, vo_ref):
        gv = g_ref[...]
        mn = ADAM_B1 * m_ref[...] + (1.0 - ADAM_B1) * gv
        vn = ADAM_B2 * v_ref[...] + (1.0 - ADAM_B2) * (gv * gv)
        d_ref[...] = -ADAM_LR * ((mn / c1) / (jnp.sqrt(vn / c2) + ADAM_EPS) + ADAM_WD * w_ref[...])
        mo_ref[...] = mn
        vo_ref[...] = vn

    spec = pl.BlockSpec((tr, width), lambda i: (i, 0))
    d, mn, vn = pl.pallas_call(body, name=name, out_shape=[_sds((rows, width), F32)] * 3, grid=(rows // tr,),
                               in_specs=[spec] * 4, out_specs=[spec] * 3, compiler_params=_cp(("parallel",)))(w2, g2, m2, v2)
    return d.reshape(shape), mn.reshape(shape), vn.reshape(shape)


def adamw_ffn(w, m, v, red, kind, ns, name):
    shape = w.shape
    w2, m2, v2 = (t.reshape(-1, shape[-1]) for t in (w, m, v))
    rows, width = w2.shape
    c1 = 1.0 - ADAM_B1 ** ADAM_STEP
    c2 = 1.0 - ADAM_B2 ** ADAM_STEP
    tr, nb = ns // 2, 2
    gspec = pl.BlockSpec((tr, D), lambda i: (((i // nb) * 3 + kind) * nb + i % nb, 0))

    def body(w_ref, g_ref, m_ref, v_ref, go_ref, d_ref, mo_ref, vo_ref):
        gv = g_ref[...]
        mn = ADAM_B1 * m_ref[...] + (1.0 - ADAM_B1) * gv
        vn = ADAM_B2 * v_ref[...] + (1.0 - ADAM_B2) * (gv * gv)
        go_ref[...] = gv
        d_ref[...] = -ADAM_LR * ((mn / c1) / (jnp.sqrt(vn / c2) + ADAM_EPS) + ADAM_WD * w_ref[...])
        mo_ref[...] = mn
        vo_ref[...] = vn

    spec = pl.BlockSpec((tr, width), lambda i: (i, 0))
    outs = pl.pallas_call(body, name=name, out_shape=[_sds((rows, width), F32)] * 4, grid=(rows // tr,),
                          in_specs=[spec, gspec, spec, spec], out_specs=[spec] * 4,
                          compiler_params=_cp(("parallel",)))(w2, red, m2, v2)
    return tuple(t.reshape(shape) for t in outs)


def mod_mm(sc, w_mod, bias, name):
    wm = w_mod.shape[-1]
    tn = _pick(wm, (768, 512, 384, 256, 128))

    def body(a_ref, b_ref, c_ref, o_ref):
        o_ref[...] = _nn(a_ref[...], b_ref[...].astype(BF16)) + c_ref[...]

    return pl.pallas_call(
        body, name=name, out_shape=_sds((DEPTH, 32, wm), F32), grid=(DEPTH, wm // tn),
        in_specs=[pl.BlockSpec((32, D), lambda l, j: (0, 0)), pl.BlockSpec((None, D, tn), lambda l, j: (l, 0, j)),
                  pl.BlockSpec((None, 1, tn), lambda l, j: (l, 0, j))],
        out_specs=pl.BlockSpec((None, 32, tn), lambda l, j: (l, 0, j)),
        compiler_params=_cp(("parallel", "parallel")))(sc, w_mod, bias)


def wmod_dw(sc, dcol, name):
    wm = dcol.shape[-1]
    tm = 256

    def body(a_ref, b_ref, o_ref):
        o_ref[...] = _tn(a_ref[...], b_ref[...].astype(BF16))

    return pl.pallas_call(
        body, name=name, out_shape=_sds((DEPTH, D, wm), F32), grid=(DEPTH, D // tm),
        in_specs=[pl.BlockSpec((32, tm), lambda l, i: (0, i)), pl.BlockSpec((None, 32, wm), lambda l, i: (l, 0, 0))],
        out_specs=pl.BlockSpec((None, tm, wm), lambda l, i: (l, i, 0)),
        compiler_params=_cp(("parallel", "parallel")))(sc, dcol)


def cctx_dx(drow, w_mod, name):
    wm = w_mod.shape[-1]

    def body(a_ref, b_ref, o_ref):
        o_ref[...] = _nt(a_ref[...].astype(BF16), b_ref[...].astype(BF16))

    return pl.pallas_call(
        body, name=name, out_shape=_sds((DEPTH, 16, D), F32), grid=(DEPTH,),
        in_specs=[pl.BlockSpec((None, 16, wm), lambda l: (l, 0, 0)), pl.BlockSpec((None, D, wm), lambda l: (l, 0, 0))],
        out_specs=pl.BlockSpec((None, 16, D), lambda l: (l, 0, 0)), compiler_params=_cp(("parallel",), VMEM_BIG))(drow, w_mod)


HEAD_PERM = (0, 4, 1, 5, 2, 6, 3, 7)


def _rot_rows(wt):
    return jnp.concatenate([-wt[32:64], wt[0:32]], axis=0)


def _unrot_rows(g):
    return jnp.concatenate([g[32:64], -g[0:32]], axis=0)


def _heads(a, n):
    return [a[64 * i:64 * (i + 1)] for i in range(n)]


def kernel(x, c, ctx, c_ctx, w_mod, b_mod, ln_g, ln_b, ffn_w_gate, ffn_w_up, ffn_w_down, mix_ab_w_in, attn_sink, pool_w, pool_scale, mix_ab_w_out, lru_w_in, lru_conv_w, lru_conv_b, lru_wa, lru_ba, lru_wx, lru_bx, lru_lambda, lru_w_out, loss_target, m_c_ctx, m_w_mod, m_b_mod, m_ln_g, m_ln_b, m_ffn_w_gate, m_ffn_w_up, m_ffn_w_down, m_mix_ab_w_in, m_attn_sink, m_pool_w, m_pool_scale, m_mix_ab_w_out, m_lru_w_in, m_lru_conv_w, m_lru_conv_b, m_lru_wa, m_lru_ba, m_lru_wx, m_lru_bx, m_lru_lambda, m_lru_w_out, v_c_ctx, v_w_mod, v_b_mod, v_ln_g, v_ln_b, v_ffn_w_gate, v_ffn_w_up, v_ffn_w_down, v_mix_ab_w_in, v_attn_sink, v_pool_w, v_pool_scale, v_mix_ab_w_out, v_lru_w_in, v_lru_conv_w, v_lru_conv_b, v_lru_wa, v_lru_ba, v_lru_wx, v_lru_bx, v_lru_lambda, v_lru_w_out):
    n_lat, n_ctx = x.shape[1], ctx.shape[1]
    lay = Layout(n_ctx, n_lat)
    T = lay.T
    ns = ffn_w_gate.shape[-1]
    n_li, n_ai = lru_w_in.shape[-1], mix_ab_w_in.shape[-1]
    n_ao, n_lo = mix_ab_w_out.shape[1], lru_w_out.shape[1]
    wm = w_mod.shape[-1]
    dsh = ln_g.shape[-1]
    mx, my, mc = lax.axis_index("x"), lax.axis_index("y"), lax.axis_index("c")
    chip = 2 * mx + my
    me = 2 * chip + mc

    c_all = all_gather8(c, "ag8_c").reshape(16, D)
    cc = jnp.concatenate([c_all, c_ctx[None, :], jnp.zeros((15, D), F32)], axis=0)
    sc = silu_rows(cc, "silu_c")
    bias = lax.dynamic_slice(b_mod, (0, chip * wm), (DEPTH, wm)).reshape(DEPTH, 1, wm)
    modg = all_gather_chips(mod_mm(sc, w_mod, bias, "mod_mm"), "ag_mod")
    modtab = []
    for l in range(DEPTH):
        full = jnp.transpose(modg[:, l], (1, 0, 2)).reshape(32, N_CHIP * wm)
        mine = lax.dynamic_slice(full, (2 * me, 0), (2, N_CHIP * wm))
        modtab.append(jnp.concatenate([mine, full[16:17]], axis=0).reshape(3, N_MOD, D))

    small = jnp.concatenate([ln_g.reshape(6, dsh), ln_b.reshape(6, dsh), lru_conv_w[0], lru_conv_b, lru_ba[0],
                             lru_bx[0], lru_lambda[0], jnp.zeros((9, dsh), F32)], axis=0)
    small = all_gather_chips(small.reshape(2, 16, dsh), "ag_small").reshape(N_CHIP, 32, dsh)
    small = jnp.transpose(small, (1, 0, 2)).reshape(32, D)
    ln_g_f, ln_b_f = small[0:6].reshape(2, 3, D), small[6:12].reshape(2, 3, D)
    conv_w_f, conv_b_f = small[12:16], small[16:17]
    lru_vec = small[17:23]

    hh = 3 * ns // 2
    gate_t, up_t = jnp.swapaxes(ffn_w_gate, -1, -2), jnp.swapaxes(ffn_w_up, -1, -2)
    placed = [ffn_place(gate_t, up_t, ffn_w_down, g // 2, g % 2, f"ag_ffn{g}_place") for g in range(4)]
    wb = [gather_placed(placed[0], "ag_ffn0"), None, None, None]
    mix_sh = jnp.concatenate([lru_w_in[0].T, mix_ab_w_in[0].T, mix_ab_w_out[0], lru_w_out[0]], axis=0).astype(BF16)
    n_mix = n_li + n_ai + n_ao + n_lo
    mixw = all_gather_chips(mix_sh.reshape(2, n_mix // 2, D), "ag_mix").reshape(N_CHIP, n_mix, D)
    o1, o2, o3 = n_li, n_li + n_ai, n_li + n_ai + n_ao
    lru_in_t = mixw[:, 0:o1].reshape(N_CHIP * n_li, D)
    ab_in_t = mixw[:, o1:o2].reshape(N_CHIP * n_ai, D)
    ab_out = mixw[:, o2:o3].reshape(N_CHIP * n_ao, D)
    lru_out = mixw[:, o3:].reshape(N_CHIP * n_lo, D)
    qh, kh = _heads(ab_in_t[Q0:K0], N_HEADS), _heads(ab_in_t[K0:V0], N_KV)
    w_ext_t = jnp.concatenate([qh[h] for h in HEAD_PERM] + [ab_in_t[K0:QR0]]
                              + [_rot_rows(qh[h]) for h in HEAD_PERM] + [_rot_rows(t) for t in kh], axis=0)
    oh = _heads(ab_out[0:ATT_W], N_HEADS)
    w_out_ext = jnp.concatenate([oh[h] for h in HEAD_PERM] + [ab_out[ATT_W:]], axis=0)

    t = jnp.arange(n_lat)
    inv = ROPE_THETA ** (-jnp.arange(16, dtype=F32) / 16.0)
    ang = jnp.concatenate([(t // GRID_W).astype(F32)[:, None] * inv, (t % GRID_W).astype(F32)[:, None] * inv], axis=-1)
    cos1 = jnp.concatenate([jnp.ones((n_ctx, 32), F32), jnp.cos(ang)], axis=0)
    sin1 = jnp.concatenate([jnp.zeros((n_ctx, 32), F32), jnp.sin(ang)], axis=0)
    cos_t = jnp.tile(cos1, (2, 4))
    sin_t = jnp.tile(sin1, (2, 4))
    sk = attn_sink[0]
    sink_tab = jnp.concatenate([jnp.repeat(jnp.stack([sk[:4], sk[4:]], axis=1), HEAD_DIM, axis=1),
                                jnp.zeros((4, 128), F32)], axis=0)
    pscale = pool_scale.reshape(1, POOL_W)

    h0 = jnp.concatenate([ctx, x], axis=1).reshape(T, D)
    tgt = loss_target.reshape(2 * n_lat, D)

    def lnv(l, j):
        return jnp.stack([ln_g_f[l, j], ln_b_f[l, j]])

    subs = [(0, 0, 0.5, 0), (0, 3, 1.0, 1), (0, 6, 0.5, 2), (1, 0, 0.5, 0), (1, 3, 1.0, 1), (1, 6, 0.5, 2)]

    def ffn_core(hm, l, f):
        tag = f"l{l}f{f}"
        gi = 2 * l + f
        w = wb[gi].reshape(N_CHIP, 3 * ns, D)
        if gi == 3:
            sp, sl, u, a = ffn_up(lay, hm, w, 0, 1, ns, f"ffn_up_{tag}")
            (y,) = slab_nn_acc(lay, [a], w, [2], ns, f"ffn_down_{tag}")
            return y, dict(sp=sp, sl=sl, u=u, a=a, nbuf=None)
        sp, sl, u, a, nbuf = ffn_up(lay, hm, w, 0, 1, ns, f"ffn_up_{tag}", rider=rider_gather_xy(placed[gi + 1]))
        y, nbuf = slab_nn_acc(lay, [a], w, [2], ns, f"ffn_down_{tag}", rider=rider_gather_fwd(nbuf))
        return y, dict(sp=sp, sl=sl, u=u, a=a, nbuf=nbuf)

    def mixa_core(hm):
        p = mm_nt(hm, w_ext_t, "mixa_in")
        qr, kr, vb, u = rope_fwd(lay, p, cos_t, sin_t, "rope")
        att, lse = attn_fwd(lay, qr, kr, vb, sink_tab, "attn")
        pool = pool_fwd(lay, u, pool_w[0], pscale, "pool")
        cat = jnp.concatenate([att, pool], axis=1)
        return mm_nn(cat, w_out_ext, "mixa_out"), dict(qr=qr, kr=kr, vb=vb, u=u, lse=lse, cat=cat)

    def mixc_core(hm):
        p = mm_nt(hm, lru_in_t, "mixc_in")
        uc = conv_fwd(lay, p, D, conv_w_f, conv_b_f, "conv")
        a, b = lru_coeffs(lay, uc, lru_wa[0], lru_wx[0], lru_vec, "lru_coef")
        s = lru_scan(lay, a, b, "lru_scan")
        o = lru_gate(lay, p, s, "lru_gate")
        return mm_nn(o, lru_out, "mixc_out"), dict(p=p, uc=uc, a=a, s=s, o=o)

    recs = []
    h = h0
    hm = modulate(lay, h0, modtab[0], 0, 1, "mod_first")
    for k, (l, k0, coef, j) in enumerate(subs):
        if k0 == 3:
            y, core = mixa_core(hm) if l == 0 else mixc_core(hm)
        else:
            y, core = ffn_core(hm, l, k0 // 6)
        nxt = None if k == 5 else (modtab[subs[k + 1][0]], subs[k + 1][1], subs[k + 1][1] + 1)
        nbuf = core.pop("nbuf", None)
        res = resid_ln(lay, h, y, modtab[l], k0 + 2, coef, lnv(l, j), f"ln_s{k}", nxt=nxt,
                       rider=None if nbuf is None else rider_gather_d2d(nbuf))
        if nbuf is not None:
            wb[2 * l + k0 // 6 + 1] = res[-1]
        recs.append(dict(h=h, hm=hm, y=y, xhat=res[1], rstd=res[2], **core))
        h = res[0]
        hm = res[3] if nxt is not None else None

    dout, lparts = loss_and_grad(lay, h, tgt, "loss")
    loss = lax.psum(jnp.sum(lparts), ("x", "y", "c"))

    dln = {}
    dms = {}
    mixg = {}
    ffn_red = [lax.empty((4, 2, hh, D), F32)]
    pending = []

    def ffn_core_bwd(dy, r, l, f):
        tag = f"l{l}f{f}"
        gi = 2 * l + f
        w = wb[gi].reshape(N_CHIP, 3 * ns, D)
        prev = pending.pop() if pending else None
        gb = lax.empty((N_CHIP, 3 * ns, D), F32)
        join = None
        if prev is None:
            dg, du = ffn_bwd_da(lay, dy, w, 2, r["sp"], r["sl"], r["u"], ns, f"ffn_da_{tag}")
            (gb,) = slab_tn(lay, r["a"], dy, gb, 2, ns, f"ffn_dwd_{tag}")
            (gb,) = slab_tn(lay, dg, r["hm"], gb, 0, ns, f"ffn_dwg_{tag}")
            (gb,) = slab_tn(lay, du, r["hm"], gb, 1, ns, f"ffn_dwu_{tag}")
            (dhm,) = slab_nn_acc(lay, [dg, du], w, [0, 1], ns, f"ffn_dh_{tag}")
        else:
            dg, du, recv = ffn_bwd_da(lay, dy, w, 2, r["sp"], r["sl"], r["u"], ns, f"ffn_da_{tag}",
                                      rider=rider_reduce_sib(prev[1]))
            q = add_own_half(prev[1], recv, BF16, f"rs_add2_ffn{prev[0]}")
            gb, arr = slab_tn(lay, r["a"], dy, gb, 2, ns, f"ffn_dwd_{tag}", rider=rider_reduce_copy(q, 0))
            gb, arr = slab_tn(lay, dg, r["hm"], gb, 0, ns, f"ffn_dwg_{tag}", rider=rider_reduce_copy(q, 1, arr))
            gb, arr = slab_tn(lay, du, r["hm"], gb, 1, ns, f"ffn_dwu_{tag}", rider=rider_reduce_copy(q, 2, arr))
            red = sum_slots(q, arr, f"rs_add4_ffn{prev[0]}", dst=ffn_red[0], g=prev[0])
            dhm, ffn_red[0] = slab_nn_acc(lay, [dg, du], w, [0, 1], ns, f"ffn_dh_{tag}", rider=rider_join(red, prev[0]))
        pending.append((gi, gb.reshape(N_CHIP, 2, hh, D)))
        return dhm, join

    def mixc_core_bwd(dy, r):
        do_c = mm_nt(dy, lru_out, "mixc_out_dx")
        mixg["lru_out"] = mm_tn(r["o"], dy, "mixc_out_dw")
        dgate, dyg = lru_gate_bwd(lay, r["p"], r["s"], do_c, "lru_gate_b")
        da_c, db_c = lru_scan_bwd(lay, r["a"], r["s"], dyg, "lru_scan_b")
        duc, mixg["wa"], mixg["wx"], mixg["vec"] = lru_coeffs_bwd(lay, r["uc"], lru_wa[0], lru_wx[0], lru_vec, da_c, db_c,
                                                                  "lru_coef_b")
        du_c, mixg["cw"], mixg["cb"] = conv_bwd(lay, r["p"], D, conv_w_f, duc, "conv_b")
        dp_c = jnp.concatenate([dgate, du_c], axis=1)
        mixg["lru_in_t"] = mm_tn(dp_c, r["hm"], "mixc_in_dw")
        return mm_nn(dp_c, lru_in_t, "mixc_in_dx")

    def mixa_core_bwd(dy, r):
        dcat = mm_nt(dy, w_out_ext, "mixa_out_dx")
        mixg["out_ext"] = mm_tn(r["cat"], dy, "mixa_out_dw")
        dqr, dkr, dv, mixg["sink"] = attn_bwd(lay, r["qr"], r["kr"], r["vb"], sink_tab, r["lse"], dcat, "attn_b")
        du_a, mixg["pw"], mixg["ps"] = pool_bwd(lay, r["u"], dcat, pool_w[0], pscale, "pool_b")
        dp_a = rope_bwd(lay, dqr, dkr, dv, du_a, cos_t, sin_t, "rope_b")
        mixg["ext_t"] = mm_tn(dp_a, r["hm"], "mixa_in_dw")
        return mm_nn(dp_a, w_ext_t, "mixa_in_dx")

    l, k0, coef, j = subs[5]
    dy, dres, s1 = ln_bwd(lay, dout, recs[5]["xhat"], recs[5]["rstd"], recs[5]["y"], modtab[l], k0 + 2, coef, lnv(l, j),
                          "lnb_s5")
    for k in range(5, -1, -1):
        l, k0, coef, j = subs[k]
        r = recs[k]
        join = None
        if k0 == 3:
            dhm = mixa_core_bwd(dy, r) if l == 0 else mixc_core_bwd(dy, r)
        else:
            dhm, join = ffn_core_bwd(dy, r, l, k0 // 6)
        dln[(l, j)] = block_sums(lay, s1, f"bs_ln_s{k}")
        if k > 0:
            lp, k0p, coefp, jp = subs[k - 1]
            rp = recs[k - 1]
            res = modb_lnb(lay, dres, dhm, r["h"], modtab[l], k0 + 1, rp["xhat"], rp["rstd"], rp["y"],
                           modtab[lp], k0p + 2, coefp, lnv(lp, jp), f"modb_lnb_s{k}", rider=join)
            dy, dres, s1, s2 = res[:4]
        else:
            res = mod_bwd(lay, dres, dhm, r["h"], modtab[l], k0 + 1, "modb_s0", rider=join)
            gx, s2 = res[:2]
        if join is not None:
            ffn_red[0] = res[-1]
        dms[(l, k0)] = block_sums(lay, s2, f"bs_mod_s{k}")
    grad_x = gx.reshape(2, n_lat, D)
    g_lru_out, g_wa, g_wx, g_vec, g_cw, g_cb = (mixg[n] for n in ("lru_out", "wa", "wx", "vec", "cw", "cb"))
    g_lru_in_t, g_out_ext, g_sink, g_pw, g_ps, g_ext_t = (mixg[n] for n in ("lru_in_t", "out_ext", "sink", "pw", "ps", "ext_t"))

    rows = []
    for l in range(DEPTH):
        per_k = []
        for k0, j in ((0, 0), (3, 1), (6, 2)):
            per_k += [dms[(l, k0)][:3, 0], dms[(l, k0)][:3, 1], dln[(l, j)][:3, 2]]
        rows.append(jnp.stack(per_k, axis=1).reshape(3, N_MOD * D))
    dmod_loc = jnp.concatenate(rows + [jnp.zeros((2, N_MOD * D), F32)], axis=0)
    dmod_all, g_b_mod = mod_grad_rows(all_gather8(dmod_loc, "ag8_dmod"), "dmod_rows")
    dcol = lax.dynamic_slice(dmod_all, (0, 0, chip * wm), (DEPTH, 32, wm))
    g_w_mod = wmod_dw(sc, dcol, "wmod_dw")
    g_cctx = cctx_grad(cctx_dx(dcol[:, 16:32], w_mod, "cctx_dx"), c_ctx[None, :], "cctx_grad")

    gq = _heads(g_ext_t[Q0:K0], N_HEADS)
    gqr = _heads(g_ext_t[QR0:KR0], N_HEADS)
    g_q = [None] * N_HEADS
    for i, h in enumerate(HEAD_PERM):
        g_q[h] = gq[i] + _unrot_rows(gqr[i])
    gk = [a + _unrot_rows(b) for a, b in zip(_heads(g_ext_t[K0:V0], N_KV), _heads(g_ext_t[KR0:PEXT], N_KV))]
    g_ab_in_t = jnp.concatenate(g_q + gk + [g_ext_t[V0:QR0]], axis=0)
    go = _heads(g_out_ext[0:ATT_W], N_HEADS)
    g_o = [None] * N_HEADS
    for i, h in enumerate(HEAD_PERM):
        g_o[h] = go[i]
    g_ab_out = jnp.concatenate(g_o + [g_out_ext[ATT_W:]], axis=0)
    mix_g = jnp.concatenate([g_lru_in_t.reshape(N_CHIP, n_li, D), g_ab_in_t.reshape(N_CHIP, n_ai, D),
                             g_ab_out.reshape(N_CHIP, n_ao, D), g_lru_out.reshape(N_CHIP, n_lo, D)], axis=1)

    g_ln_g = jnp.stack([jnp.stack([dln[(l, j)][3, 1] for j in range(3)]) for l in range(DEPTH)])
    g_ln_b = jnp.stack([jnp.stack([dln[(l, j)][3, 0] for j in range(3)]) for l in range(DEPTH)])
    sink_row = jnp.sum(g_sink, axis=0)[:4]
    g_sink8 = jnp.concatenate([sink_row[:, 0], sink_row[:, HEAD_DIM]])
    misc = jnp.concatenate([g_sink8, jnp.sum(g_ps, axis=0).reshape(POOL_W), jnp.zeros((D - 8 - POOL_W,), F32)])
    small_g = jnp.concatenate([
        g_ln_g.reshape(6, D), g_ln_b.reshape(6, D), jnp.sum(g_cw, axis=0), jnp.sum(g_cb, axis=0), g_vec,
        misc[None, :], jnp.sum(g_pw, axis=0).reshape(64, D), g_wa.reshape(256, D), g_wx.reshape(256, D), g_cctx,
        jnp.zeros((39, D), F32)], axis=0)
    n_small = small_g.shape[0] // N_CHIP
    mix_buf = jnp.concatenate([mix_g, small_g.reshape(N_CHIP, n_small, D)], axis=1)
    n_mb = n_mix + n_small

    last_g, last_buf = pending.pop()
    ffn_red = reduce_scatter_chips(last_buf, f"ffn{last_g}", wire=BF16, dst=ffn_red[0], g=last_g).reshape(12 * ns, D)
    mix_red = reduce_scatter_chips(mix_buf.reshape(N_CHIP, 2, n_mb // 2, D), "mix").reshape(n_mb, D)
    small_red = all_gather_chips(mix_red[n_mix:].reshape(2, n_small // 2, D), "ag_smallg").reshape(N_CHIP * n_small, D)

    ffn_kind = dict(ffn_w_gate=0, ffn_w_up=1, ffn_w_down=2)

    def cols(a):
        return lax.dynamic_slice_in_dim(a, chip * dsh, dsh, axis=a.ndim - 1)

    sr = small_red
    grads = dict(
        c_ctx=sr[600], w_mod=g_w_mod, b_mod=g_b_mod,
        ln_g=cols(sr[0:6]).reshape(2, 3, dsh), ln_b=cols(sr[6:12]).reshape(2, 3, dsh),
        mix_ab_w_in=mix_red[o1:o2][None], attn_sink=sr[23, 0:8][None], pool_w=sr[24:88].reshape(1, 4, 128, 128),
        pool_scale=sr[23, 8:8 + POOL_W][None], mix_ab_w_out=mix_red[o2:o3][None], lru_w_in=mix_red[0:o1].T[None],
        lru_conv_w=cols(sr[12:16])[None], lru_conv_b=cols(sr[16:17]), lru_wa=sr[88:344].reshape(1, 2, 8, 128, 128),
        lru_ba=cols(sr[17:19])[None], lru_wx=sr[344:600].reshape(1, 2, 8, 128, 128), lru_bx=cols(sr[19:21])[None],
        lru_lambda=cols(sr[21:23])[None], lru_w_out=mix_red[o3:n_mix][None])
    params = dict(c_ctx=(c_ctx, m_c_ctx, v_c_ctx), w_mod=(w_mod, m_w_mod, v_w_mod), b_mod=(b_mod, m_b_mod, v_b_mod),
                  ln_g=(ln_g, m_ln_g, v_ln_g), ln_b=(ln_b, m_ln_b, v_ln_b),
                  ffn_w_gate=(ffn_w_gate, m_ffn_w_gate, v_ffn_w_gate), ffn_w_up=(ffn_w_up, m_ffn_w_up, v_ffn_w_up),
                  ffn_w_down=(ffn_w_down, m_ffn_w_down, v_ffn_w_down),
                  mix_ab_w_in=(mix_ab_w_in, m_mix_ab_w_in, v_mix_ab_w_in), attn_sink=(attn_sink, m_attn_sink, v_attn_sink),
                  pool_w=(pool_w, m_pool_w, v_pool_w), pool_scale=(pool_scale, m_pool_scale, v_pool_scale),
                  mix_ab_w_out=(mix_ab_w_out, m_mix_ab_w_out, v_mix_ab_w_out), lru_w_in=(lru_w_in, m_lru_w_in, v_lru_w_in),
                  lru_conv_w=(lru_conv_w, m_lru_conv_w, v_lru_conv_w), lru_conv_b=(lru_conv_b, m_lru_conv_b, v_lru_conv_b),
                  lru_wa=(lru_wa, m_lru_wa, v_lru_wa), lru_ba=(lru_ba, m_lru_ba, v_lru_ba), lru_wx=(lru_wx, m_lru_wx, v_lru_wx),
                  lru_bx=(lru_bx, m_lru_bx, v_lru_bx), lru_lambda=(lru_lambda, m_lru_lambda, v_lru_lambda),
                  lru_w_out=(lru_w_out, m_lru_w_out, v_lru_w_out))
    gl, dl, ml, vl = [], [], [], []
    transposed = ("ffn_w_gate", "ffn_w_up", "mix_ab_w_in")
    for name, (w, m, v) in params.items():
        if name in transposed:
            w, m, v = (jnp.swapaxes(t, -1, -2) for t in (w, m, v))
        if name in ffn_kind:
            g, d, mn, vn = adamw_ffn(w, m, v, ffn_red, ffn_kind[name], ns, f"adamw_{name}")
        else:
            g = grads[name].reshape(w.shape)
            d, mn, vn = adamw(w, g, m, v, f"adamw_{name}")
        if name in transposed:
            g, d, mn, vn = (jnp.swapaxes(t, -1, -2) for t in (g, d, mn, vn))
        gl.append(g)
        dl.append(d)
        ml.append(mn)
        vl.append(vn)
    return (loss, grad_x, *gl, *dl, *ml, *vl)
```

```python
import functools
import math

import jax
import jax.numpy as jnp
from jax import lax
from jax.experimental import pallas as pl
from jax.experimental.pallas import tpu as pltpu

F32, BF16 = jnp.float32, jnp.bfloat16
MESH = pl.DeviceIdType.MESH
ANY = pl.BlockSpec(memory_space=pl.ANY)
VMEM_SPEC = pl.BlockSpec(memory_space=pltpu.VMEM)

D = 1024
N_CHIP = 4
HEAD_DIM, N_HEADS, N_KV = 64, 8, 2
ATT_W, KV_W, POOL_W = 512, 128, 512
POOL_WINDOWS = (2, 4, 8, 16)
BLK = 128
ATT_SCALE = HEAD_DIM ** -0.5
ROPE_THETA = 10000.0
GRID_W = 64
LRU_C = 8.0
LN_EPS = 1e-5
NEG_INF = -1e30
DEPTH = 2
ALPHA = (2 * DEPTH) ** 0.25
N_MOD = 9
ADAM_LR, ADAM_B1, ADAM_B2, ADAM_EPS, ADAM_WD, ADAM_STEP = 0.001, 0.9, 0.999, 1e-08, 0.01, 10
VMEM_BIG = 48 * 1024 * 1024


def _cp(sem=None, vmem=None):
    kw = {}
    if sem is not None:
        kw["dimension_semantics"] = sem
    if vmem is not None:
        kw["vmem_limit_bytes"] = vmem
    return pltpu.CompilerParams(**kw)


def _sds(shape, dtype):
    return jax.ShapeDtypeStruct(tuple(shape), dtype)


def _pick(n, cands):
    for c in cands:
        if n % c == 0:
            return c
    return n


def _dot(a, b, dims):
    return lax.dot_general(a, b, (dims, ((), ())), preferred_element_type=F32)


def _nn(a, b):
    return _dot(a, b, ((1,), (0,)))


def _nt(a, b):
    return _dot(a, b, ((1,), (1,)))


def _tn(a, b):
    return _dot(a, b, ((0,), (0,)))


def _sigmoid(x):
    return 0.5 * jnp.tanh(0.5 * x) + 0.5


def _me():
    return lax.axis_index("x"), lax.axis_index("y"), lax.axis_index("c")


def _rcopy(src, dst, ssem, rsem, dev):
    return pltpu.make_async_remote_copy(src_ref=src, dst_ref=dst, send_sem=ssem, recv_sem=rsem,
                                        device_id=dev, device_id_type=MESH)


def all_gather8(x, name):
    def body(x_ref, o_ref, ssem, rsem, lsem):
        mx, my, mc = _me()
        me = 4 * mx + 2 * my + mc
        loc = pltpu.make_async_copy(x_ref, o_ref.at[me], lsem)
        loc.start()
        peers = []
        for m in range(1, 8):
            px = 1 - mx if (m >> 2) & 1 else mx
            py = 1 - my if (m >> 1) & 1 else my
            pc = 1 - mc if m & 1 else mc
            peers.append((px, py, pc))
        sends = [_rcopy(x_ref, o_ref.at[me], ssem.at[k], rsem.at[k], p) for k, p in enumerate(peers)]
        for cp in sends:
            cp.start()
        for k, (px, py, pc) in enumerate(peers):
            _rcopy(x_ref, o_ref.at[4 * px + 2 * py + pc], ssem.at[k], rsem.at[k], (px, py, pc)).wait_recv()
        for cp in sends:
            cp.wait_send()
        loc.wait()

    return pl.pallas_call(
        body, name=name, out_shape=_sds((8,) + x.shape, x.dtype),
        in_specs=[VMEM_SPEC], out_specs=VMEM_SPEC,
        scratch_shapes=[pltpu.SemaphoreType.DMA((7,)), pltpu.SemaphoreType.DMA((7,)), pltpu.SemaphoreType.DMA],
    )(x)


_ROW_BLOCKS = (512, 384, 352, 256, 224, 128)


def _idx(v):
    return jnp.reshape(v, (1,)).astype(jnp.int32)


def place_slab(shard, name):
    _, h, w = shard.shape
    th = _pick(h, _ROW_BLOCKS)

    def body(s_ref, x_ref, o_ref):
        del s_ref
        o_ref[...] = x_ref[...]

    return pl.pallas_call(
        body, name=name, out_shape=_sds((N_CHIP,) + shard.shape, shard.dtype),
        grid_spec=pltpu.PrefetchScalarGridSpec(
            num_scalar_prefetch=1, grid=(2, h // th),
            in_specs=[pl.BlockSpec((None, th, w), lambda k, r, s: (k, r, 0))],
            out_specs=pl.BlockSpec((None, None, th, w), lambda k, r, s: (s[0], k, r, 0))),
    )(_idx(2 * lax.axis_index("x") + lax.axis_index("y")), shard)


def place_rows(buf, rows, r0, name):
    e, w = rows.shape
    tb = 64

    def body(s_ref, x_ref, b_ref, o_ref):
        del s_ref, b_ref
        o_ref[...] = x_ref[...]

    return pl.pallas_call(
        body, name=name, out_shape=_sds(buf.shape, buf.dtype),
        grid_spec=pltpu.PrefetchScalarGridSpec(
            num_scalar_prefetch=1, grid=(e // tb,),
            in_specs=[pl.BlockSpec((tb, w), lambda j, s: (j, 0)), ANY],
            out_specs=pl.BlockSpec((None, tb, w), lambda j, s: (s[0], r0 // tb + j, 0))),
        input_output_aliases={2: 0},
    )(_idx(2 * lax.axis_index("x") + lax.axis_index("y")), rows, buf)


def ffn_place(w_gate_t, w_up_t, w_down, l, f, name, extra=0):
    ns = w_down.shape[-2]
    tc = 256

    def body(s_ref, g_ref, u_ref, d_ref, o_ref):
        del s_ref
        k = pl.program_id(0)

        @pl.when(k == 0)
        def _():
            o_ref[...] = g_ref[...].astype(BF16)

        @pl.when(k == 1)
        def _():
            o_ref[...] = u_ref[...].astype(BF16)

        @pl.when(k == 2)
        def _():
            o_ref[...] = d_ref[...].astype(BF16)

    spec = pl.BlockSpec((None, None, ns, tc), lambda k, j, s: (l, f, 0, j))
    return pl.pallas_call(
        body, name=name, out_shape=_sds((N_CHIP, 3 * ns + extra, D), BF16),
        grid_spec=pltpu.PrefetchScalarGridSpec(
            num_scalar_prefetch=1, grid=(3, D // tc), in_specs=[spec, spec, spec],
            out_specs=pl.BlockSpec((None, ns, tc), lambda k, j, s: (s[0], k, j))),
    )(_idx(2 * lax.axis_index("x") + lax.axis_index("y")), w_gate_t, w_up_t, w_down)


def all_gather_chips(shard, name):
    return gather_placed(place_slab(shard, name + "_place"), name)


def gather_placed(full, name):
    def body(x_ref, o_ref, ssem, rsem):
        del x_ref
        mx, my, mc = _me()
        s = 2 * mx + my
        sib = (mx, my, 1 - mc)
        chips = [(1 - mx, my), (mx, 1 - my), (1 - mx, 1 - my)]
        first = [_rcopy(o_ref.at[s, mc], o_ref.at[s, mc], ssem.at[j], rsem.at[j], (px, py, mc))
                 for j, (px, py) in enumerate(chips)]
        for cp in first:
            cp.start()
        passed = []
        for j, (px, py) in enumerate(chips):
            ps = 2 * px + py
            _rcopy(o_ref.at[ps, mc], o_ref.at[ps, mc], ssem.at[j], rsem.at[j], (px, py, mc)).wait_recv()
            fw = _rcopy(o_ref.at[ps, mc], o_ref.at[ps, mc], ssem.at[3 + j], rsem.at[3 + j], sib)
            fw.start()
            passed.append(fw)
        for j, (px, py) in enumerate(chips):
            ps = 2 * px + py
            _rcopy(o_ref.at[ps, 1 - mc], o_ref.at[ps, 1 - mc], ssem.at[3 + j], rsem.at[3 + j], sib).wait_recv()
        for cp in first + passed:
            cp.wait_send()

    return pl.pallas_call(
        body, name=name, out_shape=_sds(full.shape, full.dtype), in_specs=[ANY], out_specs=ANY,
        input_output_aliases={0: 0},
        scratch_shapes=[pltpu.SemaphoreType.DMA((6,)), pltpu.SemaphoreType.DMA((6,))],
    )(full)


def sibling_send_other_half(buf, name):
    def body(x_ref, o_ref, ssem, rsem):
        mx, my, mc = _me()
        sib = (mx, my, 1 - mc)
        cps = [_rcopy(x_ref.at[k, 1 - mc], o_ref.at[k], ssem.at[k], rsem.at[k], sib) for k in range(N_CHIP)]
        for cp in cps:
            cp.start()
        for cp in cps:
            cp.wait_recv()
        for cp in cps:
            cp.wait_send()

    n, _, h, w = buf.shape
    return pl.pallas_call(
        body, name=name, out_shape=_sds((n, h, w), buf.dtype), in_specs=[ANY], out_specs=ANY,
        scratch_shapes=[pltpu.SemaphoreType.DMA((N_CHIP,)), pltpu.SemaphoreType.DMA((N_CHIP,))],
    )(buf)


def chips_all_to_all(q, name):
    def body(x_ref, o_ref, ssem, rsem):
        mx, my, mc = _me()
        s = 2 * mx + my
        chips = [(1 - mx, my), (mx, 1 - my), (1 - mx, 1 - my)]
        cps = [_rcopy(x_ref.at[2 * px + py], o_ref.at[s], ssem.at[j], rsem.at[j], (px, py, mc))
               for j, (px, py) in enumerate(chips)]
        for cp in cps:
            cp.start()
        for j, (px, py) in enumerate(chips):
            ps = 2 * px + py
            _rcopy(x_ref.at[ps], o_ref.at[ps], ssem.at[j], rsem.at[j], (px, py, mc)).wait_recv()
        for cp in cps:
            cp.wait_send()

    return pl.pallas_call(
        body, name=name, out_shape=_sds(q.shape, q.dtype), in_specs=[ANY], out_specs=ANY,
        scratch_shapes=[pltpu.SemaphoreType.DMA((3,)), pltpu.SemaphoreType.DMA((3,))],
    )(q)


def sibling_join_halves(both, name, g=None):
    def body(x_ref, o_ref, ssem, rsem):
        del x_ref
        mx, my, mc = _me()
        sib = (mx, my, 1 - mc)
        o = o_ref if g is None else o_ref.at[g]
        cp = _rcopy(o.at[mc], o.at[mc], ssem, rsem, sib)
        cp.start()
        _rcopy(o.at[1 - mc], o.at[1 - mc], ssem, rsem, sib).wait_recv()
        cp.wait_send()

    return pl.pallas_call(
        body, name=name, out_shape=_sds(both.shape, both.dtype), in_specs=[ANY], out_specs=ANY,
        input_output_aliases={0: 0}, scratch_shapes=[pltpu.SemaphoreType.DMA, pltpu.SemaphoreType.DMA],
    )(both)


def add_own_half(buf, recv, wire, name):
    n, _, h, w = buf.shape
    th = _pick(h, _ROW_BLOCKS)

    def body(c_ref, a_ref, b_ref, o_ref):
        del c_ref
        o_ref[...] = (a_ref[...] + b_ref[...]).astype(o_ref.dtype)

    return pl.pallas_call(
        body, name=name, out_shape=_sds((n, h, w), wire),
        grid_spec=pltpu.PrefetchScalarGridSpec(
            num_scalar_prefetch=1, grid=(n, h // th),
            in_specs=[pl.BlockSpec((None, None, th, w), lambda k, r, c: (k, c[0], r, 0)),
                      pl.BlockSpec((None, th, w), lambda k, r, c: (k, r, 0))],
            out_specs=pl.BlockSpec((None, th, w), lambda k, r, c: (k, r, 0))),
    )(_idx(lax.axis_index("c")), buf, recv)


def sum_slots(q, r, name, dst=None, g=None):
    n, h, w = r.shape
    th = _pick(h, _ROW_BLOCKS)

    def body(i_ref, q_ref, r1, r2, r3, *rest):
        del i_ref
        rest[-1][...] = ((q_ref[...].astype(F32) + r1[...].astype(F32)) + r2[...].astype(F32)) + r3[...].astype(F32)

    def slot(d):
        return lambda i, ix: ((ix[0] + d) % N_CHIP, i, 0)

    idx = jnp.stack([2 * lax.axis_index("x") + lax.axis_index("y"), lax.axis_index("c")]).astype(jnp.int32)
    in_specs = [pl.BlockSpec((None, th, w), slot(d)) for d in (0, 1, 2, 3)]
    if dst is None:
        return pl.pallas_call(
            body, name=name, out_shape=_sds((2, h, w), F32),
            grid_spec=pltpu.PrefetchScalarGridSpec(
                num_scalar_prefetch=1, grid=(h // th,), in_specs=in_specs,
                out_specs=pl.BlockSpec((None, th, w), lambda i, ix: (ix[1], i, 0))),
        )(idx, q, r, r, r)
    return pl.pallas_call(
        body, name=name, out_shape=_sds(dst.shape, F32),
        grid_spec=pltpu.PrefetchScalarGridSpec(
            num_scalar_prefetch=1, grid=(h // th,), in_specs=in_specs + [ANY],
            out_specs=pl.BlockSpec((None, None, th, w), lambda i, ix: (g, ix[1], i, 0))),
        input_output_aliases={5: 0},
    )(idx, q, r, r, r, dst)


def reduce_scatter_chips(buf, tag, wire=F32, dst=None, g=None):
    recv = sibling_send_other_half(buf, f"rs_sib_{tag}")
    q = add_own_half(buf, recv, wire, f"rs_add2_{tag}")
    r = chips_all_to_all(q, f"rs_a2a_{tag}")
    red = sum_slots(q, r, f"rs_add4_{tag}", dst=dst, g=g)
    return sibling_join_halves(red, f"rs_join_{tag}", g=g)


class Layout:
    def __init__(self, n_ctx, n_lat):
        self.C, self.L = n_ctx, n_lat
        self.PS = n_ctx + n_lat
        self.T = 2 * self.PS
        self.tr = _pick(math.gcd(n_ctx, n_lat), (256, 128))
        self.bps = self.PS // self.tr
        self.cb = n_ctx // self.tr
        self.nblk = self.T // self.tr
        self.tm = _pick(self.T, (1152, 768, 512, 256, 128))
        self.tc = _pick(self.T, (512, 256, 128))

    def seg(self, i):
        return jnp.where(i % self.bps < self.cb, 2, i // self.bps)


def rowwise(lay, name, fn, rows, segs=(), vecs=(), outs=(), sums=(), rider=None):
    tr, nblk = lay.tr, lay.nblk
    n_r, n_s, n_v, n_o = len(rows), len(segs), len(vecs), len(outs)
    lat_only = any(o[2:] for o in outs) or any(a.shape[0] != lay.T for a in rows)
    nsub = 1 if lat_only or nblk % 2 else 2
    tb = tr * nsub

    def body(*refs):
        ins = refs[:n_r + n_s + n_v]
        ors = refs[n_r + n_s + n_v:]
        for sub in range(nsub):
            rs = slice(sub * tr, (sub + 1) * tr)
            seg = lay.seg(pl.program_id(0) * nsub + sub)
            vals = [r[rs, :] for r in ins[:n_r]] + [r[seg] for r in ins[n_r:n_r + n_s]] + [r[...] for r in ins[n_r + n_s:]]
            res = fn(*vals)
            for k in range(n_o):
                ors[k][rs, :] = res[k].astype(ors[k].dtype)
            for k in range(len(sums)):
                ors[n_o + k][sub] = res[n_o + k]

    def all_rows(i):
        return (i, 0)

    def lat_rows(i):
        return ((i // lay.bps) * (lay.bps - lay.cb) + jnp.maximum(i % lay.bps - lay.cb, 0), 0)

    in_specs = [pl.BlockSpec((tb, a.shape[1]), all_rows if a.shape[0] == lay.T else lat_rows) for a in rows]
    in_specs += [pl.BlockSpec(a.shape, lambda i: (0, 0, 0)) for a in segs]
    in_specs += [pl.BlockSpec(a.shape, lambda i: (0, 0)) for a in vecs]
    out_shape = [_sds((2 * lay.L if o[2:] else lay.T, o[0]), o[1]) for o in outs]
    out_shape += [_sds((nblk, r, w), F32) for r, w in sums]
    out_specs = [pl.BlockSpec((tb, o[0]), lat_rows if o[2:] else all_rows) for o in outs]
    out_specs += [pl.BlockSpec((nsub, r, w), lambda i: (i, 0, 0)) for r, w in sums]
    sem = "arbitrary" if any(o[2:] for o in outs) else "parallel"
    if rider is None:
        return pl.pallas_call(body, name=name, out_shape=out_shape, grid=(nblk // nsub,), in_specs=in_specs,
                              out_specs=out_specs, compiler_params=_cp((sem,), VMEM_BIG))(*rows, *segs, *vecs)
    return _host_call(body, rider, name, (nblk // nsub,), in_specs, out_specs, out_shape, (*rows, *segs, *vecs), (sem,),
                      n_r + n_s + n_v, n_o + len(sums))


def modulate(lay, h, mod, k_shift, k_scale, name):
    def fn(hb, m):
        return (hb * (1.0 + m[k_scale:k_scale + 1]) + m[k_shift:k_shift + 1],)
    return rowwise(lay, name, fn, [h], segs=[mod], outs=[(D, BF16)])[0]


def resid_ln(lay, h, y, mod, k_gate, coef, lnv, name, nxt=None, rider=None):
    def fn(hb, yb, m, *rest):
        ln = rest[-1]
        z = ALPHA * hb + (coef * m[k_gate:k_gate + 1]) * yb
        mu = jnp.mean(z, axis=-1, keepdims=True)
        zc = z - mu
        var = jnp.mean(zc * zc, axis=-1, keepdims=True)
        rstd = lax.rsqrt(var + LN_EPS)
        xhat = zc * rstd
        out = xhat * ln[0:1] + ln[1:2]
        if nxt is None:
            return out, xhat, rstd
        mn = rest[0]
        return out, xhat, rstd, out * (1.0 + mn[nxt[2]:nxt[2] + 1]) + mn[nxt[1]:nxt[1] + 1]
    segs = [mod] if nxt is None else [mod, nxt[0]]
    outs = [(D, F32), (D, F32), (1, F32)] + ([] if nxt is None else [(D, BF16)])
    return rowwise(lay, name, fn, [h, y], segs=segs, vecs=[lnv], outs=outs, rider=rider)


def _ln_bwd_math(do, xh, rs, yb, gate, coef, ln):
    dxh = do * ln[0:1]
    m1 = jnp.mean(dxh, axis=-1, keepdims=True)
    m2 = jnp.mean(dxh * xh, axis=-1, keepdims=True)
    dz = rs * (dxh - m1 - xh * m2)
    s = jnp.concatenate([jnp.sum(do, axis=0, keepdims=True), jnp.sum(do * xh, axis=0, keepdims=True),
                         jnp.sum(coef * dz * yb, axis=0, keepdims=True)], axis=0)
    return (coef * gate) * dz, ALPHA * dz, s


def _mod_bwd_math(dr, dm, hb, scale):
    s = jnp.concatenate([jnp.sum(dm, axis=0, keepdims=True), jnp.sum(dm * hb, axis=0, keepdims=True)], axis=0)
    return dr + dm * (1.0 + scale), s


def ln_bwd(lay, dout, xhat, rstd, y, mod, k_gate, coef, lnv, name):
    def fn(do, xh, rs, yb, m, ln):
        return _ln_bwd_math(do, xh, rs, yb, m[k_gate:k_gate + 1], coef, ln)
    return rowwise(lay, name, fn, [dout, xhat, rstd, y], segs=[mod], vecs=[lnv],
                   outs=[(D, BF16), (D, F32)], sums=[(3, D)])


def mod_bwd(lay, dres, dhm, h, mod, k_scale, name, rider=None):
    def fn(dr, dm, hb, m):
        return _mod_bwd_math(dr, dm, hb, m[k_scale:k_scale + 1])
    return rowwise(lay, name, fn, [dres, dhm, h], segs=[mod], outs=[(D, F32, "lat")], sums=[(2, D)], rider=rider)


def modb_lnb(lay, dres, dhm, h, mod, k_scale, xhat, rstd, y, mod_p, k_gate, coef, lnv, name, rider=None):
    def fn(dr, dm, hb, xh, rs, yb, m, mp, ln):
        dh, s2 = _mod_bwd_math(dr, dm, hb, m[k_scale:k_scale + 1])
        dy, dres_p, s1 = _ln_bwd_math(dh, xh, rs, yb, mp[k_gate:k_gate + 1], coef, ln)
        return dy, dres_p, s1, s2
    return rowwise(lay, name, fn, [dres, dhm, h, xhat, rstd, y], segs=[mod, mod_p], vecs=[lnv],
                   outs=[(D, BF16), (D, F32)], sums=[(3, D), (2, D)], rider=rider)


def block_sums(lay, parts, name):
    nblk, r, w = parts.shape

    def body(p_ref, o_ref):
        acc = [None, None, None]
        for i in range(nblk):
            sg = 2 if i % lay.bps < lay.cb else i // lay.bps
            acc[sg] = p_ref[i] if acc[sg] is None else acc[sg] + p_ref[i]
        for k in range(3):
            o_ref[k] = acc[k]
        o_ref[3] = (acc[0] + acc[1]) + acc[2]

    return pl.pallas_call(body, name=name, out_shape=_sds((4, r, w), F32), in_specs=[VMEM_SPEC],
                          out_specs=VMEM_SPEC)(parts)


def mm_nn(a, b, name, out_dtype=F32, bias=None):
    m, k = a.shape
    n = b.shape[1]
    tm = _pick(m, (1152, 768, 512, 256, 128, 64, 32, 16, 8))
    tn = _pick(n, (1024, 768, 640, 512, 384, 256, 128))

    def body(*refs):
        if bias is None:
            a_ref, b_ref, o_ref = refs
            o_ref[...] = _nn(a_ref[...].astype(BF16), b_ref[...].astype(BF16)).astype(o_ref.dtype)
        else:
            a_ref, b_ref, c_ref, o_ref = refs
            o_ref[...] = (_nn(a_ref[...].astype(BF16), b_ref[...].astype(BF16)) + c_ref[...]).astype(o_ref.dtype)

    in_specs = [pl.BlockSpec((tm, k), lambda i, j: (i, 0)), pl.BlockSpec((k, tn), lambda i, j: (0, j))]
    ops = [a, b]
    if bias is not None:
        in_specs.append(pl.BlockSpec((1, tn), lambda i, j: (0, j)))
        ops.append(bias)
    return pl.pallas_call(body, name=name, out_shape=_sds((m, n), out_dtype), grid=(m // tm, n // tn),
                          in_specs=in_specs, out_specs=pl.BlockSpec((tm, tn), lambda i, j: (i, j)),
                          compiler_params=_cp(("parallel", "parallel"), VMEM_BIG))(*ops)


def mm_nt(a, b, name, out_dtype=F32):
    m, k = a.shape
    n = b.shape[0]
    tm = _pick(m, (1152, 768, 512, 256, 128, 64, 32, 16, 8))
    tn = _pick(n, (1024, 768, 640, 512, 384, 256, 128))

    def body(a_ref, b_ref, o_ref):
        o_ref[...] = _nt(a_ref[...].astype(BF16), b_ref[...].astype(BF16)).astype(o_ref.dtype)

    return pl.pallas_call(body, name=name, out_shape=_sds((m, n), out_dtype), grid=(m // tm, n // tn),
                          in_specs=[pl.BlockSpec((tm, k), lambda i, j: (i, 0)), pl.BlockSpec((tn, k), lambda i, j: (j, 0))],
                          out_specs=pl.BlockSpec((tm, tn), lambda i, j: (i, j)),
                          compiler_params=_cp(("parallel", "parallel"), VMEM_BIG))(a, b)


def mm_tn(a, b, name):
    t, m = a.shape
    n = b.shape[1]
    tk = _pick(t, (1152, 768, 512, 256, 128, 64, 32, 16))
    tm = _pick(m, (512, 384, 256, 128))

    def body(a_ref, b_ref, o_ref):
        @pl.when(pl.program_id(1) == 0)
        def _():
            o_ref[...] = jnp.zeros_like(o_ref)
        o_ref[...] += _tn(a_ref[...].astype(BF16), b_ref[...].astype(BF16))

    return pl.pallas_call(body, name=name, out_shape=_sds((m, n), F32), grid=(m // tm, t // tk),
                          in_specs=[pl.BlockSpec((tk, tm), lambda i, k: (k, i)), pl.BlockSpec((tk, n), lambda i, k: (k, 0))],
                          out_specs=pl.BlockSpec((tm, n), lambda i, k: (i, 0)),
                          compiler_params=_cp(("parallel", "arbitrary"), VMEM_BIG))(a, b)


class Rider:
    def __init__(self, ins, outs, aliases, nsem, start, wait):
        self.ins, self.outs, self.aliases, self.nsem, self.start, self.wait = ins, outs, aliases, nsem, start, wait


def _chips_of(mx, my):
    return [(1 - mx, my), (mx, 1 - my), (1 - mx, 1 - my)]


def rider_gather_d2d(buf):
    def start(ins, outs, ssem, rsem):
        o = outs[0]
        mx, my, mc = _me()
        for j, (px, py) in enumerate(_chips_of(mx, my)):
            ps = 2 * px + py
            _rcopy(o.at[ps, mc], o.at[ps, mc], ssem.at[j], rsem.at[j], (mx, my, 1 - mc)).start()

    def wait(ins, outs, ssem, rsem):
        o = outs[0]
        mx, my, mc = _me()
        sib = (mx, my, 1 - mc)
        for j, (px, py) in enumerate(_chips_of(mx, my)):
            ps = 2 * px + py
            _rcopy(o.at[ps, 1 - mc], o.at[ps, 1 - mc], ssem.at[j], rsem.at[j], sib).wait_recv()
        for j, (px, py) in enumerate(_chips_of(mx, my)):
            ps = 2 * px + py
            _rcopy(o.at[ps, mc], o.at[ps, mc], ssem.at[j], rsem.at[j], sib).wait_send()

    return Rider([buf], [_sds(buf.shape, buf.dtype)], {0: 0}, 3, start, wait)


def rider_reduce_sib(buf):
    n, _, h, w = buf.shape

    def start(ins, outs, ssem, rsem):
        mx, my, mc = _me()
        for k in range(N_CHIP):
            _rcopy(ins[0].at[k, 1 - mc], outs[0].at[k], ssem.at[k], rsem.at[k], (mx, my, 1 - mc)).start()

    def wait(ins, outs, ssem, rsem):
        mx, my, mc = _me()
        for k in range(N_CHIP):
            _rcopy(ins[0].at[k, 1 - mc], outs[0].at[k], ssem.at[k], rsem.at[k], (mx, my, 1 - mc)).wait_recv()
        for k in range(N_CHIP):
            _rcopy(ins[0].at[k, 1 - mc], outs[0].at[k], ssem.at[k], rsem.at[k], (mx, my, 1 - mc)).wait_send()

    return Rider([buf], [_sds((n, h, w), buf.dtype)], {}, N_CHIP, start, wait)


def rider_gather_xy(buf):
    def peers():
        mx, my, mc = _me()
        return 2 * mx + my, mc, [(1 - mx, my), (mx, 1 - my)]

    def start(ins, outs, ssem, rsem):
        o = outs[0]
        s, mc, nb = peers()
        for j, (px, py) in enumerate(nb):
            _rcopy(o.at[s, mc], o.at[s, mc], ssem.at[j], rsem.at[j], (px, py, mc)).start()

    def wait(ins, outs, ssem, rsem):
        o = outs[0]
        s, mc, nb = peers()
        for j, (px, py) in enumerate(nb):
            _rcopy(o.at[2 * px + py, mc], o.at[2 * px + py, mc], ssem.at[j], rsem.at[j], (px, py, mc)).wait_recv()
        for j, (px, py) in enumerate(nb):
            _rcopy(o.at[s, mc], o.at[s, mc], ssem.at[j], rsem.at[j], (px, py, mc)).wait_send()

    return Rider([buf], [_sds(buf.shape, buf.dtype)], {0: 0}, 2, start, wait)


def rider_gather_fwd(buf):
    def start(ins, outs, ssem, rsem):
        o = outs[0]
        mx, my, mc = _me()
        xs = 2 * (1 - mx) + my
        _rcopy(o.at[xs, mc], o.at[xs, mc], ssem.at[0], rsem.at[0], (mx, 1 - my, mc)).start()

    def wait(ins, outs, ssem, rsem):
        o = outs[0]
        mx, my, mc = _me()
        xs, ds = 2 * (1 - mx) + my, 2 * (1 - mx) + (1 - my)
        _rcopy(o.at[ds, mc], o.at[ds, mc], ssem.at[0], rsem.at[0], (mx, 1 - my, mc)).wait_recv()
        _rcopy(o.at[xs, mc], o.at[xs, mc], ssem.at[0], rsem.at[0], (mx, 1 - my, mc)).wait_send()

    return Rider([buf], [_sds(buf.shape, buf.dtype)], {0: 0}, 1, start, wait)


def rider_reduce_copy(q, j, r=None):
    def peer():
        mx, my, mc = _me()
        px, py = _chips_of(mx, my)[j]
        return 2 * mx + my, 2 * px + py, (px, py, mc)

    def start(ins, outs, ssem, rsem):
        s, ps, dev = peer()
        _rcopy(ins[0].at[ps], outs[0].at[s], ssem.at[0], rsem.at[0], dev).start()

    def wait(ins, outs, ssem, rsem):
        s, ps, dev = peer()
        _rcopy(ins[0].at[ps], outs[0].at[ps], ssem.at[0], rsem.at[0], dev).wait_recv()
        _rcopy(ins[0].at[ps], outs[0].at[s], ssem.at[0], rsem.at[0], dev).wait_send()

    if r is None:
        return Rider([q], [_sds(q.shape, q.dtype)], {}, 1, start, wait)
    return Rider([q, r], [_sds(q.shape, q.dtype)], {1: 0}, 1, start, wait)


def rider_join(buf, g):
    def start(ins, outs, ssem, rsem):
        o = outs[0].at[g]
        mx, my, mc = _me()
        _rcopy(o.at[mc], o.at[mc], ssem.at[0], rsem.at[0], (mx, my, 1 - mc)).start()

    def wait(ins, outs, ssem, rsem):
        o = outs[0].at[g]
        mx, my, mc = _me()
        _rcopy(o.at[1 - mc], o.at[1 - mc], ssem.at[0], rsem.at[0], (mx, my, 1 - mc)).wait_recv()
        _rcopy(o.at[mc], o.at[mc], ssem.at[0], rsem.at[0], (mx, my, 1 - mc)).wait_send()

    return Rider([buf], [_sds(buf.shape, buf.dtype)], {0: 0}, 1, start, wait)


def _host_call(body, rider, name, grid, in_specs, out_specs, out_shape, operands, sem, n_in, n_out, aliases=None):
    aliases = dict(aliases or {})
    if rider is None:
        return pl.pallas_call(body, name=name, out_shape=out_shape, grid=grid, in_specs=in_specs, out_specs=out_specs,
                              input_output_aliases=aliases, compiler_params=_cp(sem, VMEM_BIG))(*operands)
    n_ri, n_ro = len(rider.ins), len(rider.outs)
    aliases.update({n_in + a: n_out + b for a, b in rider.aliases.items()})

    def hosted(*refs):
        ins, r_in = refs[:n_in], refs[n_in:n_in + n_ri]
        outs, r_out = refs[n_in + n_ri:n_in + n_ri + n_out], refs[n_in + n_ri + n_out:n_in + n_ri + n_out + n_ro]
        ssem, rsem = refs[-2], refs[-1]
        first = functools.reduce(lambda a, b: a & b, [pl.program_id(k) == 0 for k in range(len(grid))])
        last = functools.reduce(lambda a, b: a & b, [pl.program_id(k) == grid[k] - 1 for k in range(len(grid))])

        @pl.when(first)
        def _():
            rider.start(r_in, r_out, ssem, rsem)
        body(*ins, *outs)

        @pl.when(last)
        def _():
            rider.wait(r_in, r_out, ssem, rsem)

    return pl.pallas_call(
        hosted, name=name, out_shape=list(out_shape) + list(rider.outs), grid=grid,
        in_specs=list(in_specs) + [ANY] * n_ri, out_specs=list(out_specs) + [ANY] * n_ro,
        input_output_aliases=aliases,
        scratch_shapes=[pltpu.SemaphoreType.DMA((rider.nsem,)), pltpu.SemaphoreType.DMA((rider.nsem,))],
        compiler_params=_cp(("arbitrary",) * len(grid), VMEM_BIG))(*operands, *rider.ins)


def ffn_up(lay, hm, wbuf, ig, iu, ns, name, rider=None):
    tm = lay.tm

    def body(h_ref, wg_ref, wu_ref, sp_ref, sl_ref, u_ref, a_ref):
        hb = h_ref[...]
        g = _nt(hb, wg_ref[0])
        u = _nt(hb, wu_ref[0])
        sg = _sigmoid(g)
        sl = g * sg
        sp_ref[0] = (sg + sl * (1.0 - sg)).astype(BF16)
        sl_ref[0] = sl.astype(BF16)
        u_ref[0] = u.astype(BF16)
        a_ref[0] = (sl * u).astype(BF16)

    spec_o = pl.BlockSpec((1, tm, ns), lambda s, i: (s, i, 0))
    return _host_call(
        body, rider, name, (N_CHIP, lay.T // tm),
        [pl.BlockSpec((tm, D), lambda s, i: (i, 0)), pl.BlockSpec((1, ns, D), lambda s, i: (s, ig, 0)),
         pl.BlockSpec((1, ns, D), lambda s, i: (s, iu, 0))],
        [spec_o] * 4, [_sds((N_CHIP, lay.T, ns), BF16)] * 4, (hm, wbuf, wbuf), ("parallel", "parallel"), 3, 4)


def slab_nn_acc(lay, zs, wbuf, idxs, ns, name, rider=None):
    tm = lay.tm
    npair = len(zs)

    def body(*refs):
        o_ref = refs[-1]

        @pl.when(pl.program_id(1) == 0)
        def _():
            o_ref[...] = jnp.zeros_like(o_ref)
        acc = _nn(refs[0][0], refs[npair][0])
        for p in range(1, npair):
            acc += _nn(refs[p][0], refs[npair + p][0])
        o_ref[...] += acc

    in_specs = [pl.BlockSpec((1, tm, ns), lambda i, s: (s, i, 0)) for _ in zs]
    in_specs += [pl.BlockSpec((1, ns, D), functools.partial(lambda i, s, q: (s, q, 0), q=q)) for q in idxs]
    return _host_call(body, rider, name, (lay.T // tm, N_CHIP), in_specs, [pl.BlockSpec((tm, D), lambda i, s: (i, 0))],
                      [_sds((lay.T, D), F32)], (*zs, *([wbuf] * npair)), ("parallel", "arbitrary"), 2 * npair, 1)


def ffn_bwd_da(lay, dy, wbuf, idn, sp, sl, u, ns, name, rider=None):
    tm = lay.tm

    def body(dy_ref, wd_ref, sp_ref, sl_ref, u_ref, dg_ref, du_ref):
        da = _nt(dy_ref[...], wd_ref[0])
        dg_ref[0] = (da * u_ref[0].astype(F32) * sp_ref[0].astype(F32)).astype(BF16)
        du_ref[0] = (da * sl_ref[0].astype(F32)).astype(BF16)

    spec_z = pl.BlockSpec((1, tm, ns), lambda s, i: (s, i, 0))
    return _host_call(
        body, rider, name, (N_CHIP, lay.T // tm),
        [pl.BlockSpec((tm, D), lambda s, i: (i, 0)), pl.BlockSpec((1, ns, D), lambda s, i: (s, idn, 0)),
         spec_z, spec_z, spec_z],
        [spec_z] * 2, [_sds((N_CHIP, lay.T, ns), BF16)] * 2, (dy, wbuf, sp, sl, u), ("parallel", "parallel"), 5, 2)


def slab_tn(lay, z, x, gbuf, idx, ns, name, rider=None):
    tk = lay.tm

    def body(z_ref, x_ref, g_in, o_ref):
        del g_in

        @pl.when(pl.program_id(1) == 0)
        def _():
            o_ref[...] = jnp.zeros_like(o_ref)
        o_ref[0] += _tn(z_ref[0], x_ref[...])

    return _host_call(
        body, rider, name, (N_CHIP, lay.T // tk),
        [pl.BlockSpec((1, tk, ns), lambda s, k: (s, k, 0)), pl.BlockSpec((tk, D), lambda s, k: (k, 0)), ANY],
        [pl.BlockSpec((1, ns, D), lambda s, k: (s, idx, 0))], [_sds(gbuf.shape, F32)], (z, x, gbuf),
        ("parallel", "arbitrary"), 3, 1, aliases={2: 0})


Q0, K0, V0, U0, QR0, KR0, PEXT = 0, 512, 640, 768, 1280, 1792, 1920


def rope_fwd(lay, p, cos, sin, name):
    def fn(pb, cs, sn):
        cs4 = jnp.concatenate([cs] * 4, axis=1)
        sn4 = jnp.concatenate([sn] * 4, axis=1)
        qr = pb[:, Q0:K0] * cs4 + pb[:, QR0:KR0] * sn4
        kr = pb[:, K0:V0] * cs + pb[:, KR0:PEXT] * sn
        return qr, kr, pb[:, V0:U0], pb[:, U0:QR0]
    return rowwise(lay, name, fn, [p, cos, sin], outs=[(ATT_W, BF16), (KV_W, BF16), (KV_W, BF16), (POOL_W, F32)])


def rope_bwd(lay, dqr, dkr, dv, du, cos, sin, name):
    def fn(dq, dk, dvb, dub, cs, sn):
        cs4 = jnp.concatenate([cs] * 4, axis=1)
        sn4 = jnp.concatenate([sn] * 4, axis=1)
        return (jnp.concatenate([dq * cs4, dk * cs, dvb, dub, dq * sn4, dk * sn], axis=1),)
    return rowwise(lay, name, fn, [dqr, dkr, dv, du, cos, sin], outs=[(PEXT, BF16)])[0]


def _attn_specs(lay):
    nbs, cbk, lbk = lay.PS // BLK, lay.C // BLK, lay.L // BLK

    def kv_map(j):
        return lambda s, n: (s * nbs + cbk + jnp.clip(n - cbk + j - 1, 0, lbk - 1), 0)

    win = [pl.BlockSpec((BLK, KV_W), kv_map(j)) for j in range(3)]
    ctx = pl.BlockSpec((lay.C, KV_W), lambda s, n: (s * (lay.PS // lay.C), 0))
    return nbs, cbk, lbk, win, ctx


def _attn_masks(n, cbk, lbk):
    row = lax.broadcasted_iota(jnp.int32, (BLK, BLK), 0)
    col = lax.broadcasted_iota(jnp.int32, (BLK, BLK), 1)
    m = n - cbk
    lat = n >= cbk
    valid = [lat & (m >= 1) & (col >= row), lat & (col >= 0), lat & (m <= lbk - 2) & (col <= row)]
    lane_lo = lax.broadcasted_iota(jnp.int32, (BLK, 2 * HEAD_DIM), 1) < HEAD_DIM
    return valid, lane_lo


def attn_fwd(lay, qr, kr, vb, sink_tab, name):
    nbs, cbk, lbk, win, ctx = _attn_specs(lay)

    def body(q_ref, k0, k1, k2, kc_ref, v0, v1, v2, vc_ref, sk_ref, o_ref, l_ref):
        n = pl.program_id(1)
        valid, lane_lo = _attn_masks(n, cbk, lbk)
        valid4 = [jnp.concatenate([v] * 4, axis=0) for v in valid]
        ks = [k0[...], k1[...], k2[...]]
        vs = [v0[...], v1[...], v2[...]]
        kc, vc = kc_ref[...], vc_ref[...]
        q2s = [q_ref[:, p * 128:(p + 1) * 128] for p in range(4)]
        outs, lses = [], []
        for hh in range(2):
            sel = lane_lo == (hh == 0)
            qm = jnp.concatenate([jnp.where(sel, q2, jnp.zeros_like(q2)) for q2 in q2s], axis=0)
            sk = jnp.concatenate([jnp.broadcast_to(sk_ref[p:p + 1, hh * HEAD_DIM:hh * HEAD_DIM + 1], (BLK, 1))
                                  for p in range(4)], axis=0)
            sw = [jnp.where(valid4[j], _nt(qm, ks[j]) * ATT_SCALE, NEG_INF) for j in range(3)]
            sc = _nt(qm, kc) * ATT_SCALE
            mx = jnp.maximum(jnp.maximum(jnp.maximum(sw[0].max(-1, keepdims=True), sw[1].max(-1, keepdims=True)),
                                         jnp.maximum(sw[2].max(-1, keepdims=True), sc.max(-1, keepdims=True))), sk)
            ew = [jnp.exp(s - mx) for s in sw]
            ec = jnp.exp(sc - mx)
            den = ew[0].sum(-1, keepdims=True) + ew[1].sum(-1, keepdims=True) + ew[2].sum(-1, keepdims=True)
            den = den + ec.sum(-1, keepdims=True) + jnp.exp(sk - mx)
            o = _nn((ec / den).astype(BF16), vc)
            for j in range(3):
                o += _nn((ew[j] / den).astype(BF16), vs[j])
            outs.append(o)
            lses.append(mx + jnp.log(den))
        for p in range(4):
            rows = slice(p * BLK, (p + 1) * BLK)
            o_ref[:, p * 128:(p + 1) * 128] = jnp.where(lane_lo, outs[0][rows], outs[1][rows]).astype(o_ref.dtype)
            l_ref[:, p * 128:(p + 1) * 128] = jnp.where(lane_lo, jnp.broadcast_to(lses[0][rows], (BLK, 128)),
                                                        jnp.broadcast_to(lses[1][rows], (BLK, 128)))

    qspec = pl.BlockSpec((BLK, ATT_W), lambda s, n: (s * nbs + n, 0))
    return pl.pallas_call(
        body, name=name, out_shape=[_sds((lay.T, ATT_W), BF16), _sds((lay.T, ATT_W), F32)], grid=(2, nbs),
        in_specs=[qspec] + win + [ctx] + win + [ctx] + [pl.BlockSpec((8, 128), lambda s, n: (0, 0))],
        out_specs=[qspec, qspec], compiler_params=_cp(("parallel", "parallel")))(qr, kr, kr, kr, kr, vb, vb, vb, vb, sink_tab)


def attn_bwd(lay, qr, kr, vb, sink_tab, lse, datt, name):
    nbs, cbk, lbk, win, ctx = _attn_specs(lay)
    C, PS = lay.C, lay.PS

    def body(q_ref, k0, k1, k2, kc_ref, v0, v1, v2, vc_ref, sk_ref, l_ref, do_ref, dq_ref, dk_ref, dv_ref, ds_ref):
        n = pl.program_id(1)
        valid, lane_lo = _attn_masks(n, cbk, lbk)

        @pl.when(n == 0)
        def _():
            dk_ref[...] = jnp.zeros_like(dk_ref)
            dv_ref[...] = jnp.zeros_like(dv_ref)
            ds_ref[...] = jnp.zeros_like(ds_ref)

        ks = [k0[...], k1[...], k2[...], kc_ref[...]]
        vs = [v0[...], v1[...], v2[...], vc_ref[...]]
        valid4 = [jnp.concatenate([v] * 4, axis=0) for v in valid]
        dks = [jnp.zeros((BLK, KV_W), F32)] * 3 + [jnp.zeros((C, KV_W), F32)]
        dvs = list(dks)
        q2s = [q_ref[:, p * 128:(p + 1) * 128] for p in range(4)]
        do2s = [do_ref[:, p * 128:(p + 1) * 128].astype(BF16) for p in range(4)]
        lse2s = [l_ref[:, p * 128:(p + 1) * 128] for p in range(4)]
        dq_h, dd_h = [], []
        for hh in range(2):
            sel = lane_lo == (hh == 0)
            qm = jnp.concatenate([jnp.where(sel, q2, jnp.zeros_like(q2)) for q2 in q2s], axis=0)
            dom = jnp.concatenate([jnp.where(sel, d2, jnp.zeros_like(d2)) for d2 in do2s], axis=0)
            lse_h = jnp.concatenate([l2[:, hh * HEAD_DIM:hh * HEAD_DIM + 1] for l2 in lse2s], axis=0)
            ps, dps = [], []
            for j in range(4):
                s = _nt(qm, ks[j]) * ATT_SCALE
                if j < 3:
                    s = jnp.where(valid4[j], s, NEG_INF)
                ps.append(jnp.exp(s - lse_h))
                dps.append(_nt(dom, vs[j]))
            dd = (ps[0] * dps[0]).sum(-1, keepdims=True) + (ps[1] * dps[1]).sum(-1, keepdims=True)
            dd = dd + (ps[2] * dps[2]).sum(-1, keepdims=True) + (ps[3] * dps[3]).sum(-1, keepdims=True)
            dq = jnp.zeros((4 * BLK, 128), F32)
            for j in range(4):
                dsb = (ps[j] * (dps[j] - dd) * ATT_SCALE).astype(BF16)
                dq += _nn(dsb, ks[j])
                dks[j] = dks[j] + _tn(dsb, qm)
                dvs[j] = dvs[j] + _tn(ps[j].astype(BF16), dom)
            dq_h.append(dq)
            dd_h.append(dd)
        for p in range(4):
            sl = slice(p * 128, (p + 1) * 128)
            rows = slice(p * BLK, (p + 1) * BLK)
            dq_ref[:, sl] = jnp.where(lane_lo, dq_h[0][rows], dq_h[1][rows])
            dd2 = jnp.where(lane_lo, jnp.broadcast_to(dd_h[0][rows], (BLK, 128)), jnp.broadcast_to(dd_h[1][rows], (BLK, 128)))
            psink = jnp.exp(sk_ref[p:p + 1, :] - lse2s[p])
            ds_ref[0, p:p + 1, :] += -jnp.sum(psink * dd2, axis=0, keepdims=True)
        dk_ref[0:C, :] += dks[3]
        dv_ref[0:C, :] += dvs[3]
        for j in range(3):
            r0 = pl.multiple_of((cbk + jnp.clip(n - cbk + j - 1, 0, lbk - 1)) * BLK, BLK)
            dk_ref[pl.ds(r0, BLK), :] += dks[j]
            dv_ref[pl.ds(r0, BLK), :] += dvs[j]

    qspec = pl.BlockSpec((BLK, ATT_W), lambda s, n: (s * nbs + n, 0))
    kvout = pl.BlockSpec((PS, KV_W), lambda s, n: (s, 0))
    return pl.pallas_call(
        body, name=name,
        out_shape=[_sds((lay.T, ATT_W), F32), _sds((lay.T, KV_W), F32), _sds((lay.T, KV_W), F32), _sds((2, 8, 128), F32)],
        grid=(2, nbs),
        in_specs=[qspec] + win + [ctx] + win + [ctx] + [pl.BlockSpec((8, 128), lambda s, n: (0, 0)), qspec, qspec],
        out_specs=[qspec, kvout, kvout, pl.BlockSpec((1, 8, 128), lambda s, n: (s, 0, 0))],
        compiler_params=_cp(("parallel", "arbitrary")))(qr, kr, kr, kr, kr, vb, vb, vb, vb, sink_tab, lse, datt)


def _winsum(x, r):
    n = x.shape[0]
    t = lax.broadcasted_iota(jnp.int32, x.shape, 0)
    acc = x
    for o in range(1, r + 1):
        acc = acc + jnp.where(t >= o, pltpu.roll(x, o, 0), 0.0) + jnp.where(t < n - o, pltpu.roll(x, n - o, 0), 0.0)
    return acc


def _wincount(n, r):
    t = lax.broadcasted_iota(jnp.int32, (n, 128), 0)
    return (jnp.minimum(t + r, n - 1) - jnp.maximum(t - r, 0) + 1).astype(F32)


def pool_fwd(lay, u, w_pool, scale, name):
    segs = [(0, lay.C), (lay.C, lay.L)]

    def body(u_ref, w_ref, s_ref, o_ref):
        for r0, n in segs:
            for g, wd in enumerate(POOL_WINDOWS):
                sl = slice(g * 128, (g + 1) * 128)
                x = u_ref[r0:r0 + n, sl]
                d = _winsum(x, wd // 2) / _wincount(n, wd // 2) - x
                y = _nn(d.astype(BF16), w_ref[g].astype(BF16)) * s_ref[:, sl]
                o_ref[r0:r0 + n, sl] = y.astype(o_ref.dtype)

    spec = pl.BlockSpec((lay.PS, POOL_W), lambda s: (s, 0))
    return pl.pallas_call(
        body, name=name, out_shape=_sds((lay.T, POOL_W), BF16), grid=(2,),
        in_specs=[spec, pl.BlockSpec(w_pool.shape, lambda s: (0, 0, 0)), pl.BlockSpec((1, POOL_W), lambda s: (0, 0))],
        out_specs=spec, compiler_params=_cp(("parallel",), VMEM_BIG))(u, w_pool, scale)


def pool_bwd(lay, u, dcat, w_pool, scale, name):
    segs = [(0, lay.C), (lay.C, lay.L)]

    def body(u_ref, dp_ref, w_ref, s_ref, du_ref, dw_ref, dsc_ref):
        for g, wd in enumerate(POOL_WINDOWS):
            sl = slice(g * 128, (g + 1) * 128)
            wb = w_ref[g].astype(BF16)
            dw = jnp.zeros((128, 128), F32)
            dsc = jnp.zeros((1, 128), F32)
            for r0, n in segs:
                x = u_ref[r0:r0 + n, sl]
                cnt = _wincount(n, wd // 2)
                d = (_winsum(x, wd // 2) / cnt - x).astype(BF16)
                dp = dp_ref[r0:r0 + n, sl]
                dsc += jnp.sum(_nn(d, wb) * dp, axis=0, keepdims=True)
                dyp = (dp * s_ref[:, sl]).astype(BF16)
                dw += _tn(d, dyp)
                dd = _nt(dyp, wb)
                du_ref[r0:r0 + n, sl] = _winsum(dd / cnt, wd // 2) - dd
            dw_ref[0, g] = dw
            dsc_ref[0, :, sl] = dsc

    spec = pl.BlockSpec((lay.PS, POOL_W), lambda s: (s, 0))
    return pl.pallas_call(
        body, name=name,
        out_shape=[_sds((lay.T, POOL_W), F32), _sds((2, 4, 128, 128), F32), _sds((2, 1, POOL_W), F32)], grid=(2,),
        in_specs=[spec, pl.BlockSpec((lay.PS, POOL_W), lambda s: (s, 1)), pl.BlockSpec(w_pool.shape, lambda s: (0, 0, 0)),
                  pl.BlockSpec((1, POOL_W), lambda s: (0, 0))],
        out_specs=[spec, pl.BlockSpec((1, 4, 128, 128), lambda s: (s, 0, 0, 0)), pl.BlockSpec((1, 1, POOL_W), lambda s: (s, 0, 0))],
        compiler_params=_cp(("parallel",), VMEM_BIG))(u, dcat, w_pool, scale)


CONV_OFFS = (-1, 0, 1, 2)
CW = 256


def _shift_rows(x, o):
    if o == 0:
        return x
    n = x.shape[0]
    t = lax.broadcasted_iota(jnp.int32, x.shape, 0)
    if o < 0:
        return jnp.where(t >= -o, pltpu.roll(x, -o, 0), 0.0)
    return jnp.where(t < n - o, pltpu.roll(x, n - o, 0), 0.0)


def conv_fwd(lay, p, col0, w, b, name):
    segs = [(0, lay.C), (lay.C, lay.L)]
    cb0 = col0 // CW

    def body(x_ref, w_ref, b_ref, o_ref):
        for r0, n in segs:
            x = x_ref[r0:r0 + n, :]
            y = jnp.broadcast_to(b_ref[...], x.shape)
            for k, o in enumerate(CONV_OFFS):
                y = y + _shift_rows(x, o) * w_ref[k:k + 1, :]
            o_ref[r0:r0 + n, :] = y

    return pl.pallas_call(
        body, name=name, out_shape=_sds((lay.T, D), F32), grid=(2, D // CW),
        in_specs=[pl.BlockSpec((lay.PS, CW), lambda s, j: (s, cb0 + j)), pl.BlockSpec((4, CW), lambda s, j: (0, j)),
                  pl.BlockSpec((1, CW), lambda s, j: (0, j))],
        out_specs=pl.BlockSpec((lay.PS, CW), lambda s, j: (s, j)),
        compiler_params=_cp(("parallel", "parallel")))(p, w, b)


def conv_bwd(lay, p, col0, w, duc, name):
    segs = [(0, lay.C), (lay.C, lay.L)]
    cb0 = col0 // CW

    def body(x_ref, w_ref, g_ref, du_ref, dw_ref, db_ref):
        dws = [jnp.zeros((1, CW), F32)] * 4
        db = jnp.zeros((1, CW), F32)
        for r0, n in segs:
            x = x_ref[r0:r0 + n, :]
            g = g_ref[r0:r0 + n, :]
            du = jnp.zeros_like(g)
            for k, o in enumerate(CONV_OFFS):
                du = du + _shift_rows(g, -o) * w_ref[k:k + 1, :]
                dws[k] = dws[k] + jnp.sum(g * _shift_rows(x, o), axis=0, keepdims=True)
            db = db + jnp.sum(g, axis=0, keepdims=True)
            du_ref[r0:r0 + n, :] = du.astype(du_ref.dtype)
        dw_ref[0] = jnp.concatenate(dws, axis=0)
        db_ref[0] = db

    return pl.pallas_call(
        body, name=name, out_shape=[_sds((lay.T, D), BF16), _sds((2, 4, D), F32), _sds((2, 1, D), F32)], grid=(2, D // CW),
        in_specs=[pl.BlockSpec((lay.PS, CW), lambda s, j: (s, cb0 + j)), pl.BlockSpec((4, CW), lambda s, j: (0, j)),
                  pl.BlockSpec((lay.PS, CW), lambda s, j: (s, j))],
        out_specs=[pl.BlockSpec((lay.PS, CW), lambda s, j: (s, j)), pl.BlockSpec((1, 4, CW), lambda s, j: (s, 0, j)),
                   pl.BlockSpec((1, 1, CW), lambda s, j: (s, 0, j))],
        compiler_params=_cp(("parallel", "parallel")))(p, w, duc)


def _softplus_neg(lam):
    z = -lam
    w = jnp.exp(-jnp.abs(z))
    log1p = jnp.where(w < 1e-2, w * (1.0 - w * (0.5 - w / 3.0)), jnp.log(1.0 + w))
    return jnp.maximum(z, 0.0) + log1p, -_sigmoid(z)


def _neg_expm1(x):
    series = -x * (1.0 + x * (0.5 + x * (1.0 / 6.0 + x * (1.0 / 24.0 + x * (1.0 / 120.0)))))
    return jnp.where(x > -0.05, series, 1.0 - jnp.exp(x))


def _lru_gates(x, xb, wa, wx, ba, bx, lam):
    r = _sigmoid(_nn(xb, wa.astype(BF16)) + ba)
    gi = _sigmoid(_nn(xb, wx.astype(BF16)) + bx)
    sp, dsp = _softplus_neg(lam)
    la = -LRU_C * r * sp
    a = jnp.exp(la)
    sq = jnp.sqrt(_neg_expm1(2.0 * la))
    return r, gi, sp, dsp, a, sq


def lru_coeffs(lay, uc, wa, wx, vec, name):
    tr = lay.tc

    def body(x_ref, wa_ref, wx_ref, v_ref, a_ref, b_ref):
        for h in range(8):
            sl = slice(h * 128, (h + 1) * 128)
            x = x_ref[:, sl]
            xb = x.astype(BF16)
            for d in range(2):
                _, gi, _, _, a, sq = _lru_gates(x, xb, wa_ref[d, h], wx_ref[d, h], v_ref[d:d + 1, sl],
                                                v_ref[2 + d:3 + d, sl], v_ref[4 + d:5 + d, sl])
                a_ref[d, h] = a
                b_ref[d, h] = sq * (gi * x)

    wspec = pl.BlockSpec((2, 8, 128, 128), lambda i: (0, 0, 0, 0))
    ospec = pl.BlockSpec((2, 8, tr, 128), lambda i: (0, 0, i, 0))
    return pl.pallas_call(
        body, name=name, out_shape=[_sds((2, 8, lay.T, 128), F32)] * 2, grid=(lay.T // tr,),
        in_specs=[pl.BlockSpec((tr, D), lambda i: (i, 0)), wspec, wspec, pl.BlockSpec((6, D), lambda i: (0, 0))],
        out_specs=[ospec, ospec], compiler_params=_cp(("parallel",), VMEM_BIG))(uc, wa, wx, vec)


def lru_coeffs_bwd(lay, uc, wa, wx, vec, da, db, name):
    tr = lay.tc

    def body(x_ref, wa_ref, wx_ref, v_ref, da_ref, db_ref, dx_ref, dwa_ref, dwx_ref, dv_ref):
        @pl.when(pl.program_id(0) == 0)
        def _():
            dwa_ref[...] = jnp.zeros_like(dwa_ref)
            dwx_ref[...] = jnp.zeros_like(dwx_ref)
            dv_ref[...] = jnp.zeros_like(dv_ref)

        for h in range(8):
            sl = slice(h * 128, (h + 1) * 128)
            x = x_ref[:, sl]
            xb = x.astype(BF16)
            dx = jnp.zeros_like(x)
            for d in range(2):
                wab, wxb = wa_ref[d, h].astype(BF16), wx_ref[d, h].astype(BF16)
                r, gi, sp, dsp, a, sq = _lru_gates(x, xb, wa_ref[d, h], wx_ref[d, h], v_ref[d:d + 1, sl],
                                                   v_ref[2 + d:3 + d, sl], v_ref[4 + d:5 + d, sl])
                dbv, dav = db_ref[d, h], da_ref[d, h]
                t1 = dbv * sq
                dgi = t1 * x
                dx = dx + t1 * gi
                dla = dav * a - (dbv * gi * x) * (a * a) / sq
                dr = dla * (-LRU_C * sp)
                dlam = jnp.sum(dla * (-LRU_C * r), axis=0, keepdims=True) * dsp
                dpa = dr * r * (1.0 - r)
                dpx = dgi * gi * (1.0 - gi)
                dpab, dpxb = dpa.astype(BF16), dpx.astype(BF16)
                dwa_ref[d, h] += _tn(xb, dpab)
                dwx_ref[d, h] += _tn(xb, dpxb)
                dx = dx + _nt(dpab, wab) + _nt(dpxb, wxb)
                dv_ref[d:d + 1, sl] += jnp.sum(dpa, axis=0, keepdims=True)
                dv_ref[2 + d:3 + d, sl] += jnp.sum(dpx, axis=0, keepdims=True)
                dv_ref[4 + d:5 + d, sl] += dlam
            dx_ref[:, sl] = dx

    wspec = pl.BlockSpec((2, 8, 128, 128), lambda i: (0, 0, 0, 0))
    gspec = pl.BlockSpec((2, 8, tr, 128), lambda i: (0, 0, i, 0))
    vspec = pl.BlockSpec((6, D), lambda i: (0, 0))
    xspec = pl.BlockSpec((tr, D), lambda i: (i, 0))
    return pl.pallas_call(
        body, name=name,
        out_shape=[_sds((lay.T, D), F32), _sds((2, 8, 128, 128), F32), _sds((2, 8, 128, 128), F32), _sds((6, D), F32)],
        grid=(lay.T // tr,), in_specs=[xspec, wspec, wspec, vspec, gspec, gspec],
        out_specs=[xspec, wspec, wspec, vspec], compiler_params=_cp(("arbitrary",), VMEM_BIG))(uc, wa, wx, vec, da, db)


GB = 2
SCAN_UNROLL = 4


def _tile_scan(a, b, up):
    t = lax.broadcasted_iota(jnp.int32, a.shape, 0)
    for d in (1, 2, 4):
        sh = 8 - d if up else d
        m = (t < 8 - d) if up else (t >= d)
        a_prev, b_prev = pltpu.roll(a, sh, 0), pltpu.roll(b, sh, 0)
        b = jnp.where(m, a * b_prev + b, b)
        a = jnp.where(m, a * a_prev, a)
    return a, b


def lru_scan(lay, a, b, name):
    segs = [(0, lay.C), (lay.C, lay.L)]

    def body(a_ref, b_ref, s_ref):
        for d in range(2):
            rev = d == 1
            state = tuple(jnp.zeros((1, 128), F32) for _ in range(GB))
            for base, n in segs:
                nt = n // 8

                def step(j, c, base=base, nt=nt, rev=rev, d=d):
                    c = list(c)
                    for u in range(SCAN_UNROLL):
                        jj = j * SCAN_UNROLL + u
                        r0 = pl.multiple_of(base + 8 * ((nt - 1 - jj) if rev else jj), 8)
                        for g in range(GB):
                            at, bt = _tile_scan(a_ref[d, g, pl.ds(r0, 8), :], b_ref[d, g, pl.ds(r0, 8), :], rev)
                            h = at * c[g] + bt
                            s_ref[d, g, pl.ds(r0, 8), :] = h
                            c[g] = h[0:1] if rev else h[7:8]
                    return tuple(c)

                state = lax.fori_loop(0, nt // SCAN_UNROLL, step, state)

    spec = pl.BlockSpec((2, GB, lay.PS, 128), lambda s, hb: (0, hb, s, 0))
    return pl.pallas_call(
        body, name=name, out_shape=_sds((2, 8, lay.T, 128), F32), grid=(2, 8 // GB),
        in_specs=[spec, spec], out_specs=spec, compiler_params=_cp(("parallel", "parallel"), VMEM_BIG))(a, b)


def lru_scan_bwd(lay, a, s, dy, name):
    segs = [(0, lay.C), (lay.C, lay.L)]
    C, PS = lay.C, lay.PS

    def body(a_ref, s_ref, g_ref, da_ref, db_ref):
        t = lax.broadcasted_iota(jnp.int32, (8, 128), 0)
        for d in range(2):
            rev = d == 1
            carry = tuple(jnp.zeros((1, 128), F32) for _ in range(GB))
            for si in (1, 0):
                base, n = segs[si]
                nt = n // 8

                def step(j, c, base=base, nt=nt, rev=rev, d=d):
                    c = list(c)
                    for u in range(SCAN_UNROLL):
                        jj = j * SCAN_UNROLL + u
                        r0 = pl.multiple_of(base + 8 * (jj if rev else (nt - 1 - jj)), 8)
                        if rev:
                            rn = pl.multiple_of(jnp.where(r0 == PS - 8, 0, r0 + 8), 8)
                            nb_zero = r0 == C - 8
                        else:
                            rn = pl.multiple_of(jnp.maximum(r0 - 8, 0), 8)
                            nb_zero = r0 == 0
                        for g in range(GB):
                            av = a_ref[d, g, pl.ds(r0, 8), :]
                            gv = g_ref[g, pl.ds(r0, 8), :]
                            sv = s_ref[d, g, pl.ds(r0, 8), :]
                            nbt = s_ref[d, g, pl.ds(rn, 8), :]
                            at, bt = _tile_scan(av, av * gv, not rev)
                            m = at * c[g] + bt
                            if rev:
                                m_next = jnp.where(t >= 1, pltpu.roll(m, 1, 0), c[g])
                                nb = jnp.where(nb_zero, 0.0, nbt[0:1])
                                h_prev = jnp.where(t < 7, pltpu.roll(sv, 7, 0), nb)
                                c[g] = m[7:8]
                            else:
                                m_next = jnp.where(t < 7, pltpu.roll(m, 7, 0), c[g])
                                nb = jnp.where(nb_zero, 0.0, nbt[7:8])
                                h_prev = jnp.where(t >= 1, pltpu.roll(sv, 1, 0), nb)
                                c[g] = m[0:1]
                            lam = gv + m_next
                            db_ref[d, g, pl.ds(r0, 8), :] = lam
                            da_ref[d, g, pl.ds(r0, 8), :] = lam * h_prev
                    return tuple(c)

                carry = lax.fori_loop(0, nt // SCAN_UNROLL, step, carry)

    spec = pl.BlockSpec((2, GB, lay.PS, 128), lambda s, hb: (0, hb, s, 0))
    return pl.pallas_call(
        body, name=name, out_shape=[_sds((2, 8, lay.T, 128), F32)] * 2, grid=(2, 8 // GB),
        in_specs=[spec, spec, pl.BlockSpec((GB, lay.PS, 128), lambda s, hb: (hb, s, 0))],
        out_specs=[spec, spec], compiler_params=_cp(("parallel", "parallel"), VMEM_BIG))(a, s, dy)


def _gelu(x):
    k = math.sqrt(2.0 / math.pi)
    t = jnp.tanh(k * (x + 0.044715 * x * x * x))
    return 0.5 * x * (1.0 + t), 0.5 * (1.0 + t) + 0.5 * x * (1.0 - t * t) * k * (1.0 + 3 * 0.044715 * x * x)


def lru_gate(lay, p, s, name):
    tr = lay.tr

    def body(g_ref, s_ref, o_ref):
        for h in range(8):
            sl = slice(h * 128, (h + 1) * 128)
            o_ref[:, sl] = (_gelu(g_ref[:, sl])[0] * (s_ref[0, h] + s_ref[1, h])).astype(o_ref.dtype)

    return pl.pallas_call(
        body, name=name, out_shape=_sds((lay.T, D), BF16), grid=(lay.nblk,),
        in_specs=[pl.BlockSpec((tr, D), lambda i: (i, 0)), pl.BlockSpec((2, 8, tr, 128), lambda i: (0, 0, i, 0))],
        out_specs=pl.BlockSpec((tr, D), lambda i: (i, 0)), compiler_params=_cp(("parallel",)))(p, s)


def lru_gate_bwd(lay, p, s, do, name):
    tr = lay.tr

    def body(g_ref, s_ref, do_ref, dg_ref, dy_ref):
        for h in range(8):
            sl = slice(h * 128, (h + 1) * 128)
            ge, dge = _gelu(g_ref[:, sl])
            dov = do_ref[:, sl]
            dg_ref[:, sl] = (dov * (s_ref[0, h] + s_ref[1, h]) * dge).astype(dg_ref.dtype)
            dy_ref[h] = dov * ge

    xspec = pl.BlockSpec((tr, D), lambda i: (i, 0))
    return pl.pallas_call(
        body, name=name, out_shape=[_sds((lay.T, D), BF16), _sds((8, lay.T, 128), F32)], grid=(lay.nblk,),
        in_specs=[xspec, pl.BlockSpec((2, 8, tr, 128), lambda i: (0, 0, i, 0)), xspec],
        out_specs=[xspec, pl.BlockSpec((8, tr, 128), lambda i: (0, i, 0))],
        compiler_params=_cp(("parallel",)))(p, s, do)


def silu_rows(x, name):
    def body(x_ref, o_ref):
        v = x_ref[...]
        o_ref[...] = (v * _sigmoid(v)).astype(o_ref.dtype)
    return pl.pallas_call(body, name=name, out_shape=_sds(x.shape, BF16), in_specs=[VMEM_SPEC], out_specs=VMEM_SPEC)(x)


def mod_grad_rows(gath, name):
    w = gath.shape[-1]

    def body(g_ref, dm_ref, db_ref):
        dm_ref[...] = jnp.zeros_like(dm_ref)
        for l in range(2):
            ctx = g_ref[0, 3 * l + 2:3 * l + 3, :]
            tot = g_ref[0, 3 * l:3 * l + 1, :] + g_ref[0, 3 * l + 1:3 * l + 2, :]
            for k in range(8):
                dm_ref[l, 2 * k:2 * k + 2, :] = g_ref[k, 3 * l:3 * l + 2, :]
                if k:
                    ctx = ctx + g_ref[k, 3 * l + 2:3 * l + 3, :]
                    tot = tot + (g_ref[k, 3 * l:3 * l + 1, :] + g_ref[k, 3 * l + 1:3 * l + 2, :])
            dm_ref[l, 16:17, :] = ctx
            db_ref[l:l + 1, :] = tot + ctx

    return pl.pallas_call(body, name=name, out_shape=[_sds((2, 32, w), F32), _sds((2, w), F32)],
                          in_specs=[VMEM_SPEC], out_specs=[VMEM_SPEC, VMEM_SPEC])(gath)


def cctx_grad(p, c_ctx, name):
    def body(a_ref, c_ref, o_ref):
        cv = c_ref[...]
        sg = _sigmoid(cv)
        o_ref[...] = 0.5 * (a_ref[0, 0:1, :] + a_ref[1, 0:1, :]) * (sg * (1.0 + cv * (1.0 - sg)))
    return pl.pallas_call(body, name=name, out_shape=_sds((1, D), F32), in_specs=[VMEM_SPEC] * 2,
                          out_specs=VMEM_SPEC)(p, c_ctx)


def loss_and_grad(lay, h, tgt, name):
    def fn(hb, tb):
        lat = (pl.program_id(0) % lay.bps) >= lay.cb
        e = jnp.where(lat, hb - tb, 0.0)
        return e * (1.0 / D), jnp.sum(e * e, axis=0, keepdims=True) * (0.5 / D)
    return rowwise(lay, name, fn, [h, tgt], outs=[(D, F32)], sums=[(1, D)])


def adamw(w, g, m, v, name):
    shape = w.shape
    w2, g2, m2, v2 = (t.reshape(-1, shape[-1]) for t in (w, g, m, v))
    rows, width = w2.shape
    tr = 256 if rows % 256 == 0 else rows
    c1 = 1.0 - ADAM_B1 ** ADAM_STEP
    c2 = 1.0 - ADAM_B2 ** ADAM_STEP

    def body(w_ref, g_ref, m_ref, v_ref, d_ref, mo_ref, vo_ref):
        gv = g_ref[...]
        mn = ADAM_B1 * m_ref[...] + (1.0 - ADAM_B1) * gv
        vn = ADAM_B2 * v_ref[...] + (1.0 - ADAM_B2) * (gv * gv)
        d_ref[...] = -ADAM_LR * ((mn / c1) / (jnp.sqrt(vn / c2) + ADAM_EPS) + ADAM_WD * w_ref[...])
        mo_ref[...] = mn
        vo_ref[...] = vn

    spec = pl.BlockSpec((tr, width), lambda i: (i, 0))
    d, mn, vn = pl.pallas_call(body, name=name, out_shape=[_sds((rows, width), F32)] * 3, grid=(rows // tr,),
                               in_specs=[spec] * 4, out_specs=[spec] * 3, compiler_params=_cp(("parallel",)))(w2, g2, m2, v2)
    return d.reshape(shape), mn.reshape(shape), vn.reshape(shape)


def adamw_ffn(w, m, v, red, kind, ns, name):
    shape = w.shape
    w2, m2, v2 = (t.reshape(-1, shape[-1]) for t in (w, m, v))
    rows, width = w2.shape
    c1 = 1.0 - ADAM_B1 ** ADAM_STEP
    c2 = 1.0 - ADAM_B2 ** ADAM_STEP
    tr, nb = ns // 2, 2
    gspec = pl.BlockSpec((tr, D), lambda i: (((i // nb) * 3 + kind) * nb + i % nb, 0))

    def body(w_ref, g_ref, m_ref, v_ref, go_ref, d_ref, mo_ref, vo_ref):
        gv = g_ref[...]
        mn = ADAM_B1 * m_ref[...] + (1.0 - ADAM_B1) * gv
        vn = ADAM_B2 * v_ref[...] + (1.0 - ADAM_B2) * (gv * gv)
        go_ref[...] = gv
        d_ref[...] = -ADAM_LR * ((mn / c1) / (jnp.sqrt(vn / c2) + ADAM_EPS) + ADAM_WD * w_ref[...])
        mo_ref[...] = mn
        vo_ref[...] = vn

    spec = pl.BlockSpec((tr, width), lambda i: (i, 0))
    outs = pl.pallas_call(body, name=name, out_shape=[_sds((rows, width), F32)] * 4, grid=(rows // tr,),
                          in_specs=[spec, gspec, spec, spec], out_specs=[spec] * 4,
                          compiler_params=_cp(("parallel",)))(w2, red, m2, v2)
    return tuple(t.reshape(shape) for t in outs)


def mod_mm(sc, w_mod, bias, name):
    wm = w_mod.shape[-1]
    tn = _pick(wm, (768, 512, 384, 256, 128))

    def body(a_ref, b_ref, c_ref, o_ref):
        o_ref[...] = _nn(a_ref[...], b_ref[...].astype(BF16)) + c_ref[...]

    return pl.pallas_call(
        body, name=name, out_shape=_sds((DEPTH, 32, wm), F32), grid=(DEPTH, wm // tn),
        in_specs=[pl.BlockSpec((32, D), lambda l, j: (0, 0)), pl.BlockSpec((None, D, tn), lambda l, j: (l, 0, j)),
                  pl.BlockSpec((None, 1, tn), lambda l, j: (l, 0, j))],
        out_specs=pl.BlockSpec((None, 32, tn), lambda l, j: (l, 0, j)),
        compiler_params=_cp(("parallel", "parallel")))(sc, w_mod, bias)


def wmod_dw(sc, dcol, name):
    wm = dcol.shape[-1]
    tm = 256

    def body(a_ref, b_ref, o_ref):
        o_ref[...] = _tn(a_ref[...], b_ref[...].astype(BF16))

    return pl.pallas_call(
        body, name=name, out_shape=_sds((DEPTH, D, wm), F32), grid=(DEPTH, D // tm),
        in_specs=[pl.BlockSpec((32, tm), lambda l, i: (0, i)), pl.BlockSpec((None, 32, wm), lambda l, i: (l, 0, 0))],
        out_specs=pl.BlockSpec((None, tm, wm), lambda l, i: (l, i, 0)),
        compiler_params=_cp(("parallel", "parallel")))(sc, dcol)


def cctx_dx(drow, w_mod, name):
    wm = w_mod.shape[-1]

    def body(a_ref, b_ref, o_ref):
        o_ref[...] = _nt(a_ref[...].astype(BF16), b_ref[...].astype(BF16))

    return pl.pallas_call(
        body, name=name, out_shape=_sds((DEPTH, 16, D), F32), grid=(DEPTH,),
        in_specs=[pl.BlockSpec((None, 16, wm), lambda l: (l, 0, 0)), pl.BlockSpec((None, D, wm), lambda l: (l, 0, 0))],
        out_specs=pl.BlockSpec((None, 16, D), lambda l: (l, 0, 0)), compiler_params=_cp(("parallel",), VMEM_BIG))(drow, w_mod)


HEAD_PERM = (0, 4, 1, 5, 2, 6, 3, 7)


def _rot_rows(wt):
    return jnp.concatenate([-wt[32:64], wt[0:32]], axis=0)


def _unrot_rows(g):
    return jnp.concatenate([g[32:64], -g[0:32]], axis=0)


def _heads(a, n):
    return [a[64 * i:64 * (i + 1)] for i in range(n)]


def kernel(x, c, ctx, c_ctx, w_mod, b_mod, ln_g, ln_b, ffn_w_gate, ffn_w_up, ffn_w_down, mix_ab_w_in, attn_sink, pool_w, pool_scale, mix_ab_w_out, lru_w_in, lru_conv_w, lru_conv_b, lru_wa, lru_ba, lru_wx, lru_bx, lru_lambda, lru_w_out, loss_target, m_c_ctx, m_w_mod, m_b_mod, m_ln_g, m_ln_b, m_ffn_w_gate, m_ffn_w_up, m_ffn_w_down, m_mix_ab_w_in, m_attn_sink, m_pool_w, m_pool_scale, m_mix_ab_w_out, m_lru_w_in, m_lru_conv_w, m_lru_conv_b, m_lru_wa, m_lru_ba, m_lru_wx, m_lru_bx, m_lru_lambda, m_lru_w_out, v_c_ctx, v_w_mod, v_b_mod, v_ln_g, v_ln_b, v_ffn_w_gate, v_ffn_w_up, v_ffn_w_down, v_mix_ab_w_in, v_attn_sink, v_pool_w, v_pool_scale, v_mix_ab_w_out, v_lru_w_in, v_lru_conv_w, v_lru_conv_b, v_lru_wa, v_lru_ba, v_lru_wx, v_lru_bx, v_lru_lambda, v_lru_w_out):
    n_lat, n_ctx = x.shape[1], ctx.shape[1]
    lay = Layout(n_ctx, n_lat)
    T = lay.T
    ns = ffn_w_gate.shape[-1]
    n_li, n_ai = lru_w_in.shape[-1], mix_ab_w_in.shape[-1]
    n_ao, n_lo = mix_ab_w_out.shape[1], lru_w_out.shape[1]
    wm = w_mod.shape[-1]
    dsh = ln_g.shape[-1]
    mx, my, mc = lax.axis_index("x"), lax.axis_index("y"), lax.axis_index("c")
    chip = 2 * mx + my
    me = 2 * chip + mc

    c_all = all_gather8(c, "ag8_c").reshape(16, D)
    cc = jnp.concatenate([c_all, c_ctx[None, :], jnp.zeros((15, D), F32)], axis=0)
    sc = silu_rows(cc, "silu_c")
    bias = lax.dynamic_slice(b_mod, (0, chip * wm), (DEPTH, wm)).reshape(DEPTH, 1, wm)
    modg = all_gather_chips(mod_mm(sc, w_mod, bias, "mod_mm"), "ag_mod")
    modtab = []
    for l in range(DEPTH):
        full = jnp.transpose(modg[:, l], (1, 0, 2)).reshape(32, N_CHIP * wm)
        mine = lax.dynamic_slice(full, (2 * me, 0), (2, N_CHIP * wm))
        modtab.append(jnp.concatenate([mine, full[16:17]], axis=0).reshape(3, N_MOD, D))

    small = jnp.concatenate([ln_g.reshape(6, dsh), ln_b.reshape(6, dsh), lru_conv_w[0], lru_conv_b, lru_ba[0],
                             lru_bx[0], lru_lambda[0], jnp.zeros((9, dsh), F32)], axis=0)
    small = all_gather_chips(small.reshape(2, 16, dsh), "ag_small").reshape(N_CHIP, 32, dsh)
    small = jnp.transpose(small, (1, 0, 2)).reshape(32, D)
    ln_g_f, ln_b_f = small[0:6].reshape(2, 3, D), small[6:12].reshape(2, 3, D)
    conv_w_f, conv_b_f = small[12:16], small[16:17]
    lru_vec = small[17:23]

    hh = 3 * ns // 2
    gate_t, up_t = jnp.swapaxes(ffn_w_gate, -1, -2), jnp.swapaxes(ffn_w_up, -1, -2)
    extra = [0, n_ai + n_ao, n_li + n_lo, 0]
    placed = [ffn_place(gate_t, up_t, ffn_w_down, g // 2, g % 2, f"ag_ffn{g}_place", extra[g]) for g in range(4)]
    placed[1] = place_rows(placed[1], jnp.concatenate([mix_ab_w_in[0].T, mix_ab_w_out[0]], axis=0).astype(BF16), 3 * ns,
                           "ag_mixa_place")
    placed[2] = place_rows(placed[2], jnp.concatenate([lru_w_in[0].T, lru_w_out[0]], axis=0).astype(BF16), 3 * ns,
                           "ag_mixc_place")
    placed = [p.reshape(N_CHIP, 2, p.shape[1] // 2, D) for p in placed]
    wb = [gather_placed(placed[0], "ag_ffn0"), None, None, None]
    n_mix = n_li + n_ai + n_ao + n_lo
    o1, o2, o3 = n_li, n_li + n_ai, n_li + n_ai + n_ao
    mixw = {}

    def mixa_w():
        if "a" not in mixw:
            full = wb[1].reshape(N_CHIP, -1, D)
            ab_in_t = full[:, 3 * ns:3 * ns + n_ai].reshape(N_CHIP * n_ai, D)
            ab_out = full[:, 3 * ns + n_ai:].reshape(N_CHIP * n_ao, D)
            qh, kh = _heads(ab_in_t[Q0:K0], N_HEADS), _heads(ab_in_t[K0:V0], N_KV)
            w_ext_t = jnp.concatenate([qh[h] for h in HEAD_PERM] + [ab_in_t[K0:QR0]]
                                      + [_rot_rows(qh[h]) for h in HEAD_PERM] + [_rot_rows(t) for t in kh], axis=0)
            oh = _heads(ab_out[0:ATT_W], N_HEADS)
            mixw["a"] = (w_ext_t, jnp.concatenate([oh[h] for h in HEAD_PERM] + [ab_out[ATT_W:]], axis=0))
        return mixw["a"]

    def mixc_w():
        if "c" not in mixw:
            full = wb[2].reshape(N_CHIP, -1, D)
            mixw["c"] = (full[:, 3 * ns:3 * ns + n_li].reshape(N_CHIP * n_li, D),
                         full[:, 3 * ns + n_li:].reshape(N_CHIP * n_lo, D))
        return mixw["c"]

    t = jnp.arange(n_lat)
    inv = ROPE_THETA ** (-jnp.arange(16, dtype=F32) / 16.0)
    ang = jnp.concatenate([(t // GRID_W).astype(F32)[:, None] * inv, (t % GRID_W).astype(F32)[:, None] * inv], axis=-1)
    cos1 = jnp.concatenate([jnp.ones((n_ctx, 32), F32), jnp.cos(ang)], axis=0)
    sin1 = jnp.concatenate([jnp.zeros((n_ctx, 32), F32), jnp.sin(ang)], axis=0)
    cos_t = jnp.tile(cos1, (2, 4))
    sin_t = jnp.tile(sin1, (2, 4))
    sk = attn_sink[0]
    sink_tab = jnp.concatenate([jnp.repeat(jnp.stack([sk[:4], sk[4:]], axis=1), HEAD_DIM, axis=1),
                                jnp.zeros((4, 128), F32)], axis=0)
    pscale = pool_scale.reshape(1, POOL_W)

    h0 = jnp.concatenate([ctx, x], axis=1).reshape(T, D)
    tgt = loss_target.reshape(2 * n_lat, D)

    def lnv(l, j):
        return jnp.stack([ln_g_f[l, j], ln_b_f[l, j]])

    subs = [(0, 0, 0.5, 0), (0, 3, 1.0, 1), (0, 6, 0.5, 2), (1, 0, 0.5, 0), (1, 3, 1.0, 1), (1, 6, 0.5, 2)]

    def ffn_core(hm, l, f):
        tag = f"l{l}f{f}"
        gi = 2 * l + f
        w = wb[gi].reshape(N_CHIP, -1, D)
        if gi == 3:
            sp, sl, u, a = ffn_up(lay, hm, w, 0, 1, ns, f"ffn_up_{tag}")
            (y,) = slab_nn_acc(lay, [a], w, [2], ns, f"ffn_down_{tag}")
            return y, dict(sp=sp, sl=sl, u=u, a=a, nbuf=None)
        sp, sl, u, a, nbuf = ffn_up(lay, hm, w, 0, 1, ns, f"ffn_up_{tag}", rider=rider_gather_xy(placed[gi + 1]))
        y, nbuf = slab_nn_acc(lay, [a], w, [2], ns, f"ffn_down_{tag}", rider=rider_gather_fwd(nbuf))
        return y, dict(sp=sp, sl=sl, u=u, a=a, nbuf=nbuf)

    def mixa_core(hm):
        p = mm_nt(hm, mixa_w()[0], "mixa_in")
        qr, kr, vb, u = rope_fwd(lay, p, cos_t, sin_t, "rope")
        att, lse = attn_fwd(lay, qr, kr, vb, sink_tab, "attn")
        pool = pool_fwd(lay, u, pool_w[0], pscale, "pool")
        cat = jnp.concatenate([att, pool], axis=1)
        return mm_nn(cat, mixa_w()[1], "mixa_out"), dict(qr=qr, kr=kr, vb=vb, u=u, lse=lse, cat=cat)

    def mixc_core(hm):
        p = mm_nt(hm, mixc_w()[0], "mixc_in")
        uc = conv_fwd(lay, p, D, conv_w_f, conv_b_f, "conv")
        a, b = lru_coeffs(lay, uc, lru_wa[0], lru_wx[0], lru_vec, "lru_coef")
        s = lru_scan(lay, a, b, "lru_scan")
        o = lru_gate(lay, p, s, "lru_gate")
        return mm_nn(o, mixc_w()[1], "mixc_out"), dict(p=p, uc=uc, a=a, s=s, o=o)

    recs = []
    h = h0
    hm = modulate(lay, h0, modtab[0], 0, 1, "mod_first")
    for k, (l, k0, coef, j) in enumerate(subs):
        if k0 == 3:
            y, core = mixa_core(hm) if l == 0 else mixc_core(hm)
        else:
            y, core = ffn_core(hm, l, k0 // 6)
        nxt = None if k == 5 else (modtab[subs[k + 1][0]], subs[k + 1][1], subs[k + 1][1] + 1)
        nbuf = core.pop("nbuf", None)
        res = resid_ln(lay, h, y, modtab[l], k0 + 2, coef, lnv(l, j), f"ln_s{k}", nxt=nxt,
                       rider=None if nbuf is None else rider_gather_d2d(nbuf))
        if nbuf is not None:
            wb[2 * l + k0 // 6 + 1] = res[-1]
        recs.append(dict(h=h, hm=hm, y=y, xhat=res[1], rstd=res[2], **core))
        h = res[0]
        hm = res[3] if nxt is not None else None

    dout, lparts = loss_and_grad(lay, h, tgt, "loss")
    loss = lax.psum(jnp.sum(lparts), ("x", "y", "c"))

    dln = {}
    dms = {}
    mixg = {}
    ffn_red = [lax.empty((4, 2, hh, D), F32)]
    pending = []

    def ffn_core_bwd(dy, r, l, f):
        tag = f"l{l}f{f}"
        gi = 2 * l + f
        w = wb[gi].reshape(N_CHIP, -1, D)
        prev = pending.pop() if pending else None
        gb = lax.empty((N_CHIP, 3 * ns, D), F32)
        join = None
        if prev is None:
            dg, du = ffn_bwd_da(lay, dy, w, 2, r["sp"], r["sl"], r["u"], ns, f"ffn_da_{tag}")
            (gb,) = slab_tn(lay, r["a"], dy, gb, 2, ns, f"ffn_dwd_{tag}")
            (gb,) = slab_tn(lay, dg, r["hm"], gb, 0, ns, f"ffn_dwg_{tag}")
            (gb,) = slab_tn(lay, du, r["hm"], gb, 1, ns, f"ffn_dwu_{tag}")
            (dhm,) = slab_nn_acc(lay, [dg, du], w, [0, 1], ns, f"ffn_dh_{tag}")
        else:
            dg, du, recv = ffn_bwd_da(lay, dy, w, 2, r["sp"], r["sl"], r["u"], ns, f"ffn_da_{tag}",
                                      rider=rider_reduce_sib(prev[1]))
            q = add_own_half(prev[1], recv, BF16, f"rs_add2_ffn{prev[0]}")
            gb, arr = slab_tn(lay, r["a"], dy, gb, 2, ns, f"ffn_dwd_{tag}", rider=rider_reduce_copy(q, 0))
            gb, arr = slab_tn(lay, dg, r["hm"], gb, 0, ns, f"ffn_dwg_{tag}", rider=rider_reduce_copy(q, 1, arr))
            gb, arr = slab_tn(lay, du, r["hm"], gb, 1, ns, f"ffn_dwu_{tag}", rider=rider_reduce_copy(q, 2, arr))
            red = sum_slots(q, arr, f"rs_add4_ffn{prev[0]}", dst=ffn_red[0], g=prev[0])
            dhm, ffn_red[0] = slab_nn_acc(lay, [dg, du], w, [0, 1], ns, f"ffn_dh_{tag}", rider=rider_join(red, prev[0]))
        pending.append((gi, gb.reshape(N_CHIP, 2, hh, D)))
        return dhm, join

    def mixc_core_bwd(dy, r):
        do_c = mm_nt(dy, mixc_w()[1], "mixc_out_dx")
        mixg["lru_out"] = mm_tn(r["o"], dy, "mixc_out_dw")
        dgate, dyg = lru_gate_bwd(lay, r["p"], r["s"], do_c, "lru_gate_b")
        da_c, db_c = lru_scan_bwd(lay, r["a"], r["s"], dyg, "lru_scan_b")
        duc, mixg["wa"], mixg["wx"], mixg["vec"] = lru_coeffs_bwd(lay, r["uc"], lru_wa[0], lru_wx[0], lru_vec, da_c, db_c,
                                                                  "lru_coef_b")
        du_c, mixg["cw"], mixg["cb"] = conv_bwd(lay, r["p"], D, conv_w_f, duc, "conv_b")
        dp_c = jnp.concatenate([dgate, du_c], axis=1)
        mixg["lru_in_t"] = mm_tn(dp_c, r["hm"], "mixc_in_dw")
        return mm_nn(dp_c, mixc_w()[0], "mixc_in_dx")

    def mixa_core_bwd(dy, r):
        dcat = mm_nt(dy, mixa_w()[1], "mixa_out_dx")
        mixg["out_ext"] = mm_tn(r["cat"], dy, "mixa_out_dw")
        dqr, dkr, dv, mixg["sink"] = attn_bwd(lay, r["qr"], r["kr"], r["vb"], sink_tab, r["lse"], dcat, "attn_b")
        du_a, mixg["pw"], mixg["ps"] = pool_bwd(lay, r["u"], dcat, pool_w[0], pscale, "pool_b")
        dp_a = rope_bwd(lay, dqr, dkr, dv, du_a, cos_t, sin_t, "rope_b")
        mixg["ext_t"] = mm_tn(dp_a, r["hm"], "mixa_in_dw")
        return mm_nn(dp_a, mixa_w()[0], "mixa_in_dx")

    l, k0, coef, j = subs[5]
    dy, dres, s1 = ln_bwd(lay, dout, recs[5]["xhat"], recs[5]["rstd"], recs[5]["y"], modtab[l], k0 + 2, coef, lnv(l, j),
                          "lnb_s5")
    for k in range(5, -1, -1):
        l, k0, coef, j = subs[k]
        r = recs[k]
        join = None
        if k0 == 3:
            dhm = mixa_core_bwd(dy, r) if l == 0 else mixc_core_bwd(dy, r)
        else:
            dhm, join = ffn_core_bwd(dy, r, l, k0 // 6)
        dln[(l, j)] = block_sums(lay, s1, f"bs_ln_s{k}")
        if k > 0:
            lp, k0p, coefp, jp = subs[k - 1]
            rp = recs[k - 1]
            res = modb_lnb(lay, dres, dhm, r["h"], modtab[l], k0 + 1, rp["xhat"], rp["rstd"], rp["y"],
                           modtab[lp], k0p + 2, coefp, lnv(lp, jp), f"modb_lnb_s{k}", rider=join)
            dy, dres, s1, s2 = res[:4]
        else:
            res = mod_bwd(lay, dres, dhm, r["h"], modtab[l], k0 + 1, "modb_s0", rider=join)
            gx, s2 = res[:2]
        if join is not None:
            ffn_red[0] = res[-1]
        dms[(l, k0)] = block_sums(lay, s2, f"bs_mod_s{k}")
    grad_x = gx.reshape(2, n_lat, D)
    g_lru_out, g_wa, g_wx, g_vec, g_cw, g_cb = (mixg[n] for n in ("lru_out", "wa", "wx", "vec", "cw", "cb"))
    g_lru_in_t, g_out_ext, g_sink, g_pw, g_ps, g_ext_t = (mixg[n] for n in ("lru_in_t", "out_ext", "sink", "pw", "ps", "ext_t"))

    rows = []
    for l in range(DEPTH):
        per_k = []
        for k0, j in ((0, 0), (3, 1), (6, 2)):
            per_k += [dms[(l, k0)][:3, 0], dms[(l, k0)][:3, 1], dln[(l, j)][:3, 2]]
        rows.append(jnp.stack(per_k, axis=1).reshape(3, N_MOD * D))
    dmod_loc = jnp.concatenate(rows + [jnp.zeros((2, N_MOD * D), F32)], axis=0)
    dmod_all, g_b_mod = mod_grad_rows(all_gather8(dmod_loc, "ag8_dmod"), "dmod_rows")
    dcol = lax.dynamic_slice(dmod_all, (0, 0, chip * wm), (DEPTH, 32, wm))
    g_w_mod = wmod_dw(sc, dcol, "wmod_dw")
    g_cctx = cctx_grad(cctx_dx(dcol[:, 16:32], w_mod, "cctx_dx"), c_ctx[None, :], "cctx_grad")

    gq = _heads(g_ext_t[Q0:K0], N_HEADS)
    gqr = _heads(g_ext_t[QR0:KR0], N_HEADS)
    g_q = [None] * N_HEADS
    for i, h in enumerate(HEAD_PERM):
        g_q[h] = gq[i] + _unrot_rows(gqr[i])
    gk = [a + _unrot_rows(b) for a, b in zip(_heads(g_ext_t[K0:V0], N_KV), _heads(g_ext_t[KR0:PEXT], N_KV))]
    g_ab_in_t = jnp.concatenate(g_q + gk + [g_ext_t[V0:QR0]], axis=0)
    go = _heads(g_out_ext[0:ATT_W], N_HEADS)
    g_o = [None] * N_HEADS
    for i, h in enumerate(HEAD_PERM):
        g_o[h] = go[i]
    g_ab_out = jnp.concatenate(g_o + [g_out_ext[ATT_W:]], axis=0)
    mix_g = jnp.concatenate([g_lru_in_t.reshape(N_CHIP, n_li, D), g_ab_in_t.reshape(N_CHIP, n_ai, D),
                             g_ab_out.reshape(N_CHIP, n_ao, D), g_lru_out.reshape(N_CHIP, n_lo, D)], axis=1)

    g_ln_g = jnp.stack([jnp.stack([dln[(l, j)][3, 1] for j in range(3)]) for l in range(DEPTH)])
    g_ln_b = jnp.stack([jnp.stack([dln[(l, j)][3, 0] for j in range(3)]) for l in range(DEPTH)])
    sink_row = jnp.sum(g_sink, axis=0)[:4]
    g_sink8 = jnp.concatenate([sink_row[:, 0], sink_row[:, HEAD_DIM]])
    misc = jnp.concatenate([g_sink8, jnp.sum(g_ps, axis=0).reshape(POOL_W), jnp.zeros((D - 8 - POOL_W,), F32)])
    small_g = jnp.concatenate([
        g_ln_g.reshape(6, D), g_ln_b.reshape(6, D), jnp.sum(g_cw, axis=0), jnp.sum(g_cb, axis=0), g_vec,
        misc[None, :], jnp.sum(g_pw, axis=0).reshape(64, D), g_wa.reshape(256, D), g_wx.reshape(256, D), g_cctx,
        jnp.zeros((39, D), F32)], axis=0)
    n_small = small_g.shape[0] // N_CHIP
    mix_buf = jnp.concatenate([mix_g, small_g.reshape(N_CHIP, n_small, D)], axis=1)
    n_mb = n_mix + n_small

    last_g, last_buf = pending.pop()
    ffn_red = reduce_scatter_chips(last_buf, f"ffn{last_g}", wire=BF16, dst=ffn_red[0], g=last_g).reshape(12 * ns, D)
    mix_red = reduce_scatter_chips(mix_buf.reshape(N_CHIP, 2, n_mb // 2, D), "mix").reshape(n_mb, D)
    small_red = all_gather_chips(mix_red[n_mix:].reshape(2, n_small // 2, D), "ag_smallg").reshape(N_CHIP * n_small, D)

    ffn_kind = dict(ffn_w_gate=0, ffn_w_up=1, ffn_w_down=2)

    def cols(a):
        return lax.dynamic_slice_in_dim(a, chip * dsh, dsh, axis=a.ndim - 1)

    sr = small_red
    grads = dict(
        c_ctx=sr[600], w_mod=g_w_mod, b_mod=g_b_mod,
        ln_g=cols(sr[0:6]).reshape(2, 3, dsh), ln_b=cols(sr[6:12]).reshape(2, 3, dsh),
        mix_ab_w_in=mix_red[o1:o2][None], attn_sink=sr[23, 0:8][None], pool_w=sr[24:88].reshape(1, 4, 128, 128),
        pool_scale=sr[23, 8:8 + POOL_W][None], mix_ab_w_out=mix_red[o2:o3][None], lru_w_in=mix_red[0:o1].T[None],
        lru_conv_w=cols(sr[12:16])[None], lru_conv_b=cols(sr[16:17]), lru_wa=sr[88:344].reshape(1, 2, 8, 128, 128),
        lru_ba=cols(sr[17:19])[None], lru_wx=sr[344:600].reshape(1, 2, 8, 128, 128), lru_bx=cols(sr[19:21])[None],
        lru_lambda=cols(sr[21:23])[None], lru_w_out=mix_red[o3:n_mix][None])
    params = dict(c_ctx=(c_ctx, m_c_ctx, v_c_ctx), w_mod=(w_mod, m_w_mod, v_w_mod), b_mod=(b_mod, m_b_mod, v_b_mod),
                  ln_g=(ln_g, m_ln_g, v_ln_g), ln_b=(ln_b, m_ln_b, v_ln_b),
                  ffn_w_gate=(ffn_w_gate, m_ffn_w_gate, v_ffn_w_gate), ffn_w_up=(ffn_w_up, m_ffn_w_up, v_ffn_w_up),
                  ffn_w_down=(ffn_w_down, m_ffn_w_down, v_ffn_w_down),
                  mix_ab_w_in=(mix_ab_w_in, m_mix_ab_w_in, v_mix_ab_w_in), attn_sink=(attn_sink, m_attn_sink, v_attn_sink),
                  pool_w=(pool_w, m_pool_w, v_pool_w), pool_scale=(pool_scale, m_pool_scale, v_pool_scale),
                  mix_ab_w_out=(mix_ab_w_out, m_mix_ab_w_out, v_mix_ab_w_out), lru_w_in=(lru_w_in, m_lru_w_in, v_lru_w_in),
                  lru_conv_w=(lru_conv_w, m_lru_conv_w, v_lru_conv_w), lru_conv_b=(lru_conv_b, m_lru_conv_b, v_lru_conv_b),
                  lru_wa=(lru_wa, m_lru_wa, v_lru_wa), lru_ba=(lru_ba, m_lru_ba, v_lru_ba), lru_wx=(lru_wx, m_lru_wx, v_lru_wx),
                  lru_bx=(lru_bx, m_lru_bx, v_lru_bx), lru_lambda=(lru_lambda, m_lru_lambda, v_lru_lambda),
                  lru_w_out=(lru_w_out, m_lru_w_out, v_lru_w_out))
    gl, dl, ml, vl = [], [], [], []
    transposed = ("ffn_w_gate", "ffn_w_up", "mix_ab_w_in")
    for name, (w, m, v) in params.items():
        if name in transposed:
            w, m, v = (jnp.swapaxes(t, -1, -2) for t in (w, m, v))
        if name in ffn_kind:
            g, d, mn, vn = adamw_ffn(w, m, v, ffn_red, ffn_kind[name], ns, f"adamw_{name}")
        else:
            g = grads[name].reshape(w.shape)
            d, mn, vn = adamw(w, g, m, v, f"adamw_{name}")
        if name in transposed:
            g, d, mn, vn = (jnp.swapaxes(t, -1, -2) for t in (g, d, mn, vn))
        gl.append(g)
        dl.append(d)
        ml.append(mn)
        vl.append(vn)
    return (loss, grad_x, *gl, *dl, *ml, *vl)
```

```python
import functools
import math

import jax
import jax.numpy as jnp
from jax import lax
from jax.experimental import pallas as pl
from jax.experimental.pallas import tpu as pltpu

F32, BF16 = jnp.float32, jnp.bfloat16
MESH = pl.DeviceIdType.MESH
ANY = pl.BlockSpec(memory_space=pl.ANY)
VMEM_SPEC = pl.BlockSpec(memory_space=pltpu.VMEM)

D = 1024
N_CHIP = 4
HEAD_DIM, N_HEADS, N_KV = 64, 8, 2
ATT_W, KV_W, POOL_W = 512, 128, 512
POOL_WINDOWS = (2, 4, 8, 16)
BLK = 128
ATT_SCALE = HEAD_DIM ** -0.5
ROPE_THETA = 10000.0
GRID_W = 64
LRU_C = 8.0
LN_EPS = 1e-5
NEG_INF = -1e30
DEPTH = 2
ALPHA = (2 * DEPTH) ** 0.25
N_MOD = 9
ADAM_LR, ADAM_B1, ADAM_B2, ADAM_EPS, ADAM_WD, ADAM_STEP = 0.001, 0.9, 0.999, 1e-08, 0.01, 10
VMEM_BIG = 48 * 1024 * 1024


def _cp(sem=None, vmem=None):
    kw = {}
    if sem is not None:
        kw["dimension_semantics"] = sem
    if vmem is not None:
        kw["vmem_limit_bytes"] = vmem
    return pltpu.CompilerParams(**kw)


def _sds(shape, dtype):
    return jax.ShapeDtypeStruct(tuple(shape), dtype)


def _pick(n, cands):
    for c in cands:
        if n % c == 0:
            return c
    return n


def _dot(a, b, dims):
    return lax.dot_general(a, b, (dims, ((), ())), preferred_element_type=F32)


def _nn(a, b):
    return _dot(a, b, ((1,), (0,)))


def _nt(a, b):
    return _dot(a, b, ((1,), (1,)))


def _tn(a, b):
    return _dot(a, b, ((0,), (0,)))


def _sigmoid(x):
    return 0.5 * jnp.tanh(0.5 * x) + 0.5


def _me():
    return lax.axis_index("x"), lax.axis_index("y"), lax.axis_index("c")


def _rcopy(src, dst, ssem, rsem, dev):
    return pltpu.make_async_remote_copy(src_ref=src, dst_ref=dst, send_sem=ssem, recv_sem=rsem,
                                        device_id=dev, device_id_type=MESH)


def all_gather8(x, name):
    def body(x_ref, o_ref, ssem, rsem, lsem):
        mx, my, mc = _me()
        me = 4 * mx + 2 * my + mc
        loc = pltpu.make_async_copy(x_ref, o_ref.at[me], lsem)
        loc.start()
        peers = []
        for m in range(1, 8):
            px = 1 - mx if (m >> 2) & 1 else mx
            py = 1 - my if (m >> 1) & 1 else my
            pc = 1 - mc if m & 1 else mc
            peers.append((px, py, pc))
        sends = [_rcopy(x_ref, o_ref.at[me], ssem.at[k], rsem.at[k], p) for k, p in enumerate(peers)]
        for cp in sends:
            cp.start()
        for k, (px, py, pc) in enumerate(peers):
            _rcopy(x_ref, o_ref.at[4 * px + 2 * py + pc], ssem.at[k], rsem.at[k], (px, py, pc)).wait_recv()
        for cp in sends:
            cp.wait_send()
        loc.wait()

    return pl.pallas_call(
        body, name=name, out_shape=_sds((8,) + x.shape, x.dtype),
        in_specs=[VMEM_SPEC], out_specs=VMEM_SPEC,
        scratch_shapes=[pltpu.SemaphoreType.DMA((7,)), pltpu.SemaphoreType.DMA((7,)), pltpu.SemaphoreType.DMA],
    )(x)


_ROW_BLOCKS = (512, 384, 352, 256, 224, 128)


def _idx(v):
    return jnp.reshape(v, (1,)).astype(jnp.int32)


def place_slab(shard, name):
    _, h, w = shard.shape
    th = _pick(h, _ROW_BLOCKS)

    def body(s_ref, x_ref, o_ref):
        del s_ref
        o_ref[...] = x_ref[...]

    return pl.pallas_call(
        body, name=name, out_shape=_sds((N_CHIP,) + shard.shape, shard.dtype),
        grid_spec=pltpu.PrefetchScalarGridSpec(
            num_scalar_prefetch=1, grid=(2, h // th),
            in_specs=[pl.BlockSpec((None, th, w), lambda k, r, s: (k, r, 0))],
            out_specs=pl.BlockSpec((None, None, th, w), lambda k, r, s: (s[0], k, r, 0))),
    )(_idx(2 * lax.axis_index("x") + lax.axis_index("y")), shard)


def place_rows(buf, rows, r0, name):
    e, w = rows.shape
    tb = 64

    def body(s_ref, x_ref, b_ref, o_ref):
        del s_ref, b_ref
        o_ref[...] = x_ref[...]

    return pl.pallas_call(
        body, name=name, out_shape=_sds(buf.shape, buf.dtype),
        grid_spec=pltpu.PrefetchScalarGridSpec(
            num_scalar_prefetch=1, grid=(e // tb,),
            in_specs=[pl.BlockSpec((tb, w), lambda j, s: (j, 0)), ANY],
            out_specs=pl.BlockSpec((None, tb, w), lambda j, s: (s[0], r0 // tb + j, 0))),
        input_output_aliases={2: 0},
    )(_idx(2 * lax.axis_index("x") + lax.axis_index("y")), rows, buf)


def ffn_place(w_gate_t, w_up_t, w_down, l, f, name, extra=0):
    ns = w_down.shape[-2]
    tc = 256

    def body(s_ref, g_ref, u_ref, d_ref, o_ref):
        del s_ref
        k = pl.program_id(0)

        @pl.when(k == 0)
        def _():
            o_ref[...] = g_ref[...].astype(BF16)

        @pl.when(k == 1)
        def _():
            o_ref[...] = u_ref[...].astype(BF16)

        @pl.when(k == 2)
        def _():
            o_ref[...] = d_ref[...].astype(BF16)

    spec = pl.BlockSpec((None, None, ns, tc), lambda k, j, s: (l, f, 0, j))
    return pl.pallas_call(
        body, name=name, out_shape=_sds((N_CHIP, 3 * ns + extra, D), BF16),
        grid_spec=pltpu.PrefetchScalarGridSpec(
            num_scalar_prefetch=1, grid=(3, D // tc), in_specs=[spec, spec, spec],
            out_specs=pl.BlockSpec((None, ns, tc), lambda k, j, s: (s[0], k, j))),
    )(_idx(2 * lax.axis_index("x") + lax.axis_index("y")), w_gate_t, w_up_t, w_down)


def all_gather_chips(shard, name):
    return gather_placed(place_slab(shard, name + "_place"), name)


def gather_placed(full, name):
    def body(x_ref, o_ref, ssem, rsem):
        del x_ref
        mx, my, mc = _me()
        s = 2 * mx + my
        sib = (mx, my, 1 - mc)
        chips = [(1 - mx, my), (mx, 1 - my), (1 - mx, 1 - my)]
        first = [_rcopy(o_ref.at[s, mc], o_ref.at[s, mc], ssem.at[j], rsem.at[j], (px, py, mc))
                 for j, (px, py) in enumerate(chips)]
        for cp in first:
            cp.start()
        passed = []
        for j, (px, py) in enumerate(chips):
            ps = 2 * px + py
            _rcopy(o_ref.at[ps, mc], o_ref.at[ps, mc], ssem.at[j], rsem.at[j], (px, py, mc)).wait_recv()
            fw = _rcopy(o_ref.at[ps, mc], o_ref.at[ps, mc], ssem.at[3 + j], rsem.at[3 + j], sib)
            fw.start()
            passed.append(fw)
        for j, (px, py) in enumerate(chips):
            ps = 2 * px + py
            _rcopy(o_ref.at[ps, 1 - mc], o_ref.at[ps, 1 - mc], ssem.at[3 + j], rsem.at[3 + j], sib).wait_recv()
        for cp in first + passed:
            cp.wait_send()

    return pl.pallas_call(
        body, name=name, out_shape=_sds(full.shape, full.dtype), in_specs=[ANY], out_specs=ANY,
        input_output_aliases={0: 0},
        scratch_shapes=[pltpu.SemaphoreType.DMA((6,)), pltpu.SemaphoreType.DMA((6,))],
    )(full)


def sibling_send_other_half(buf, name):
    def body(x_ref, o_ref, ssem, rsem):
        mx, my, mc = _me()
        sib = (mx, my, 1 - mc)
        cps = [_rcopy(x_ref.at[k, 1 - mc], o_ref.at[k], ssem.at[k], rsem.at[k], sib) for k in range(N_CHIP)]
        for cp in cps:
            cp.start()
        for cp in cps:
            cp.wait_recv()
        for cp in cps:
            cp.wait_send()

    n, _, h, w = buf.shape
    return pl.pallas_call(
        body, name=name, out_shape=_sds((n, h, w), buf.dtype), in_specs=[ANY], out_specs=ANY,
        scratch_shapes=[pltpu.SemaphoreType.DMA((N_CHIP,)), pltpu.SemaphoreType.DMA((N_CHIP,))],
    )(buf)


def chips_all_to_all(q, name):
    def body(x_ref, o_ref, ssem, rsem):
        mx, my, mc = _me()
        s = 2 * mx + my
        chips = [(1 - mx, my), (mx, 1 - my), (1 - mx, 1 - my)]
        cps = [_rcopy(x_ref.at[2 * px + py], o_ref.at[s], ssem.at[j], rsem.at[j], (px, py, mc))
               for j, (px, py) in enumerate(chips)]
        for cp in cps:
            cp.start()
        for j, (px, py) in enumerate(chips):
            ps = 2 * px + py
            _rcopy(x_ref.at[ps], o_ref.at[ps], ssem.at[j], rsem.at[j], (px, py, mc)).wait_recv()
        for cp in cps:
            cp.wait_send()

    return pl.pallas_call(
        body, name=name, out_shape=_sds(q.shape, q.dtype), in_specs=[ANY], out_specs=ANY,
        scratch_shapes=[pltpu.SemaphoreType.DMA((3,)), pltpu.SemaphoreType.DMA((3,))],
    )(q)


def sibling_join_halves(both, name, g=None):
    def body(x_ref, o_ref, ssem, rsem):
        del x_ref
        mx, my, mc = _me()
        sib = (mx, my, 1 - mc)
        o = o_ref if g is None else o_ref.at[g]
        cp = _rcopy(o.at[mc], o.at[mc], ssem, rsem, sib)
        cp.start()
        _rcopy(o.at[1 - mc], o.at[1 - mc], ssem, rsem, sib).wait_recv()
        cp.wait_send()

    return pl.pallas_call(
        body, name=name, out_shape=_sds(both.shape, both.dtype), in_specs=[ANY], out_specs=ANY,
        input_output_aliases={0: 0}, scratch_shapes=[pltpu.SemaphoreType.DMA, pltpu.SemaphoreType.DMA],
    )(both)


def add_own_half(buf, recv, wire, name):
    n, _, h, w = buf.shape
    th = _pick(h, _ROW_BLOCKS)

    def body(c_ref, a_ref, b_ref, o_ref):
        del c_ref
        o_ref[...] = (a_ref[...] + b_ref[...]).astype(o_ref.dtype)

    return pl.pallas_call(
        body, name=name, out_shape=_sds((n, h, w), wire),
        grid_spec=pltpu.PrefetchScalarGridSpec(
            num_scalar_prefetch=1, grid=(n, h // th),
            in_specs=[pl.BlockSpec((None, None, th, w), lambda k, r, c: (k, c[0], r, 0)),
                      pl.BlockSpec((None, th, w), lambda k, r, c: (k, r, 0))],
            out_specs=pl.BlockSpec((None, th, w), lambda k, r, c: (k, r, 0))),
    )(_idx(lax.axis_index("c")), buf, recv)


def sum_slots(q, r, name, dst=None, g=None):
    n, h, w = r.shape
    th = _pick(h, _ROW_BLOCKS)

    def body(i_ref, q_ref, r1, r2, r3, *rest):
        del i_ref
        rest[-1][...] = ((q_ref[...].astype(F32) + r1[...].astype(F32)) + r2[...].astype(F32)) + r3[...].astype(F32)

    def slot(d):
        return lambda i, ix: ((ix[0] + d) % N_CHIP, i, 0)

    idx = jnp.stack([2 * lax.axis_index("x") + lax.axis_index("y"), lax.axis_index("c")]).astype(jnp.int32)
    in_specs = [pl.BlockSpec((None, th, w), slot(d)) for d in (0, 1, 2, 3)]
    if dst is None:
        return pl.pallas_call(
            body, name=name, out_shape=_sds((2, h, w), F32),
            grid_spec=pltpu.PrefetchScalarGridSpec(
                num_scalar_prefetch=1, grid=(h // th,), in_specs=in_specs,
                out_specs=pl.BlockSpec((None, th, w), lambda i, ix: (ix[1], i, 0))),
        )(idx, q, r, r, r)
    return pl.pallas_call(
        body, name=name, out_shape=_sds(dst.shape, F32),
        grid_spec=pltpu.PrefetchScalarGridSpec(
            num_scalar_prefetch=1, grid=(h // th,), in_specs=in_specs + [ANY],
            out_specs=pl.BlockSpec((None, None, th, w), lambda i, ix: (g, ix[1], i, 0))),
        input_output_aliases={5: 0},
    )(idx, q, r, r, r, dst)


def reduce_scatter_chips(buf, tag, wire=F32, dst=None, g=None):
    recv = sibling_send_other_half(buf, f"rs_sib_{tag}")
    q = add_own_half(buf, recv, wire, f"rs_add2_{tag}")
    r = chips_all_to_all(q, f"rs_a2a_{tag}")
    red = sum_slots(q, r, f"rs_add4_{tag}", dst=dst, g=g)
    return sibling_join_halves(red, f"rs_join_{tag}", g=g)


class Layout:
    def __init__(self, n_ctx, n_lat):
        self.C, self.L = n_ctx, n_lat
        self.PS = n_ctx + n_lat
        self.T = 2 * self.PS
        self.tr = _pick(math.gcd(n_ctx, n_lat), (256, 128))
        self.bps = self.PS // self.tr
        self.cb = n_ctx // self.tr
        self.nblk = self.T // self.tr
        self.tm = _pick(self.T, (1152, 768, 512, 256, 128))
        self.tc = _pick(self.T, (512, 256, 128))

    def seg(self, i):
        return jnp.where(i % self.bps < self.cb, 2, i // self.bps)


def rowwise(lay, name, fn, rows, segs=(), vecs=(), outs=(), sums=(), rider=None):
    tr, nblk = lay.tr, lay.nblk
    n_r, n_s, n_v, n_o = len(rows), len(segs), len(vecs), len(outs)
    lat_only = any(o[2:] for o in outs) or any(a.shape[0] != lay.T for a in rows)
    nsub = 1 if lat_only or nblk % 2 else 2
    tb = tr * nsub

    def body(*refs):
        ins = refs[:n_r + n_s + n_v]
        ors = refs[n_r + n_s + n_v:]
        for sub in range(nsub):
            rs = slice(sub * tr, (sub + 1) * tr)
            seg = lay.seg(pl.program_id(0) * nsub + sub)
            vals = [r[rs, :] for r in ins[:n_r]] + [r[seg] for r in ins[n_r:n_r + n_s]] + [r[...] for r in ins[n_r + n_s:]]
            res = fn(*vals)
            for k in range(n_o):
                ors[k][rs, :] = res[k].astype(ors[k].dtype)
            for k in range(len(sums)):
                ors[n_o + k][sub] = res[n_o + k]

    def all_rows(i):
        return (i, 0)

    def lat_rows(i):
        return ((i // lay.bps) * (lay.bps - lay.cb) + jnp.maximum(i % lay.bps - lay.cb, 0), 0)

    in_specs = [pl.BlockSpec((tb, a.shape[1]), all_rows if a.shape[0] == lay.T else lat_rows) for a in rows]
    in_specs += [pl.BlockSpec(a.shape, lambda i: (0, 0, 0)) for a in segs]
    in_specs += [pl.BlockSpec(a.shape, lambda i: (0, 0)) for a in vecs]
    out_shape = [_sds((2 * lay.L if o[2:] else lay.T, o[0]), o[1]) for o in outs]
    out_shape += [_sds((nblk, r, w), F32) for r, w in sums]
    out_specs = [pl.BlockSpec((tb, o[0]), lat_rows if o[2:] else all_rows) for o in outs]
    out_specs += [pl.BlockSpec((nsub, r, w), lambda i: (i, 0, 0)) for r, w in sums]
    sem = "arbitrary" if any(o[2:] for o in outs) else "parallel"
    if rider is None:
        return pl.pallas_call(body, name=name, out_shape=out_shape, grid=(nblk // nsub,), in_specs=in_specs,
                              out_specs=out_specs, compiler_params=_cp((sem,), VMEM_BIG))(*rows, *segs, *vecs)
    return _host_call(body, rider, name, (nblk // nsub,), in_specs, out_specs, out_shape, (*rows, *segs, *vecs), (sem,),
                      n_r + n_s + n_v, n_o + len(sums))


def modulate(lay, h, mod, k_shift, k_scale, name):
    def fn(hb, m):
        return (hb * (1.0 + m[k_scale:k_scale + 1]) + m[k_shift:k_shift + 1],)
    return rowwise(lay, name, fn, [h], segs=[mod], outs=[(D, BF16)])[0]


def resid_ln(lay, h, y, mod, k_gate, coef, lnv, name, nxt=None, rider=None):
    def fn(hb, yb, m, *rest):
        ln = rest[-1]
        z = ALPHA * hb + (coef * m[k_gate:k_gate + 1]) * yb
        mu = jnp.mean(z, axis=-1, keepdims=True)
        zc = z - mu
        var = jnp.mean(zc * zc, axis=-1, keepdims=True)
        rstd = lax.rsqrt(var + LN_EPS)
        xhat = zc * rstd
        out = xhat * ln[0:1] + ln[1:2]
        if nxt is None:
            return out, xhat, rstd
        mn = rest[0]
        return out, xhat, rstd, out * (1.0 + mn[nxt[2]:nxt[2] + 1]) + mn[nxt[1]:nxt[1] + 1]
    segs = [mod] if nxt is None else [mod, nxt[0]]
    outs = [(D, F32), (D, F32), (1, F32)] + ([] if nxt is None else [(D, BF16)])
    return rowwise(lay, name, fn, [h, y], segs=segs, vecs=[lnv], outs=outs, rider=rider)


def _ln_bwd_math(do, xh, rs, yb, gate, coef, ln):
    dxh = do * ln[0:1]
    m1 = jnp.mean(dxh, axis=-1, keepdims=True)
    m2 = jnp.mean(dxh * xh, axis=-1, keepdims=True)
    dz = rs * (dxh - m1 - xh * m2)
    s = jnp.concatenate([jnp.sum(do, axis=0, keepdims=True), jnp.sum(do * xh, axis=0, keepdims=True),
                         jnp.sum(coef * dz * yb, axis=0, keepdims=True)], axis=0)
    return (coef * gate) * dz, ALPHA * dz, s


def _mod_bwd_math(dr, dm, hb, scale):
    s = jnp.concatenate([jnp.sum(dm, axis=0, keepdims=True), jnp.sum(dm * hb, axis=0, keepdims=True)], axis=0)
    return dr + dm * (1.0 + scale), s


def ln_bwd(lay, dout, xhat, rstd, y, mod, k_gate, coef, lnv, name):
    def fn(do, xh, rs, yb, m, ln):
        return _ln_bwd_math(do, xh, rs, yb, m[k_gate:k_gate + 1], coef, ln)
    return rowwise(lay, name, fn, [dout, xhat, rstd, y], segs=[mod], vecs=[lnv],
                   outs=[(D, BF16), (D, F32)], sums=[(3, D)])


def mod_bwd(lay, dres, dhm, h, mod, k_scale, name, rider=None):
    def fn(dr, dm, hb, m):
        return _mod_bwd_math(dr, dm, hb, m[k_scale:k_scale + 1])
    return rowwise(lay, name, fn, [dres, dhm, h], segs=[mod], outs=[(D, F32, "lat")], sums=[(2, D)], rider=rider)


def modb_lnb(lay, dres, dhm, h, mod, k_scale, xhat, rstd, y, mod_p, k_gate, coef, lnv, name, rider=None):
    def fn(dr, dm, hb, xh, rs, yb, m, mp, ln):
        dh, s2 = _mod_bwd_math(dr, dm, hb, m[k_scale:k_scale + 1])
        dy, dres_p, s1 = _ln_bwd_math(dh, xh, rs, yb, mp[k_gate:k_gate + 1], coef, ln)
        return dy, dres_p, s1, s2
    return rowwise(lay, name, fn, [dres, dhm, h, xhat, rstd, y], segs=[mod, mod_p], vecs=[lnv],
                   outs=[(D, BF16), (D, F32)], sums=[(3, D), (2, D)], rider=rider)


def block_sums(lay, parts, name):
    nblk, r, w = parts.shape

    def body(p_ref, o_ref):
        acc = [None, None, None]
        for i in range(nblk):
            sg = 2 if i % lay.bps < lay.cb else i // lay.bps
            acc[sg] = p_ref[i] if acc[sg] is None else acc[sg] + p_ref[i]
        for k in range(3):
            o_ref[k] = acc[k]
        o_ref[3] = (acc[0] + acc[1]) + acc[2]

    return pl.pallas_call(body, name=name, out_shape=_sds((4, r, w), F32), in_specs=[VMEM_SPEC],
                          out_specs=VMEM_SPEC)(parts)


def mm_nn(a, b, name, out_dtype=F32, bias=None):
    m, k = a.shape
    n = b.shape[1]
    tm = _pick(m, (1152, 768, 512, 256, 128, 64, 32, 16, 8))
    tn = _pick(n, (1024, 768, 640, 512, 384, 256, 128))

    def body(*refs):
        if bias is None:
            a_ref, b_ref, o_ref = refs
            o_ref[...] = _nn(a_ref[...].astype(BF16), b_ref[...].astype(BF16)).astype(o_ref.dtype)
        else:
            a_ref, b_ref, c_ref, o_ref = refs
            o_ref[...] = (_nn(a_ref[...].astype(BF16), b_ref[...].astype(BF16)) + c_ref[...]).astype(o_ref.dtype)

    in_specs = [pl.BlockSpec((tm, k), lambda i, j: (i, 0)), pl.BlockSpec((k, tn), lambda i, j: (0, j))]
    ops = [a, b]
    if bias is not None:
        in_specs.append(pl.BlockSpec((1, tn), lambda i, j: (0, j)))
        ops.append(bias)
    return pl.pallas_call(body, name=name, out_shape=_sds((m, n), out_dtype), grid=(m // tm, n // tn),
                          in_specs=in_specs, out_specs=pl.BlockSpec((tm, tn), lambda i, j: (i, j)),
                          compiler_params=_cp(("parallel", "parallel"), VMEM_BIG))(*ops)


def mm_nt(a, b, name, out_dtype=F32, rider=None):
    m, k = a.shape
    n = b.shape[0]
    tm = _pick(m, (1152, 768, 512, 256, 128, 64, 32, 16, 8))
    tn = _pick(n, (1024, 768, 640, 512, 384, 256, 128))

    def body(a_ref, b_ref, o_ref):
        o_ref[...] = _nt(a_ref[...].astype(BF16), b_ref[...].astype(BF16)).astype(o_ref.dtype)

    res = _host_call(body, rider, name, (m // tm, n // tn),
                     [pl.BlockSpec((tm, k), lambda i, j: (i, 0)), pl.BlockSpec((tn, k), lambda i, j: (j, 0))],
                     [pl.BlockSpec((tm, tn), lambda i, j: (i, j))], [_sds((m, n), out_dtype)], (a, b),
                     ("parallel", "parallel"), 2, 1)
    return res[0] if rider is None else res


def mm_tn(a, b, name, rider=None):
    t, m = a.shape
    n = b.shape[1]
    tk = _pick(t, (1152, 768, 512, 256, 128, 64, 32, 16))
    tm = _pick(m, (512, 384, 256, 128))

    def body(a_ref, b_ref, o_ref):
        @pl.when(pl.program_id(1) == 0)
        def _():
            o_ref[...] = jnp.zeros_like(o_ref)
        o_ref[...] += _tn(a_ref[...].astype(BF16), b_ref[...].astype(BF16))

    res = _host_call(body, rider, name, (m // tm, t // tk),
                     [pl.BlockSpec((tk, tm), lambda i, k: (k, i)), pl.BlockSpec((tk, n), lambda i, k: (k, 0))],
                     [pl.BlockSpec((tm, n), lambda i, k: (i, 0))], [_sds((m, n), F32)], (a, b),
                     ("parallel", "arbitrary"), 2, 1)
    return res[0] if rider is None else res


class Rider:
    def __init__(self, ins, outs, aliases, nsem, start, wait):
        self.ins, self.outs, self.aliases, self.nsem, self.start, self.wait = ins, outs, aliases, nsem, start, wait


def _chips_of(mx, my):
    return [(1 - mx, my), (mx, 1 - my), (1 - mx, 1 - my)]


def rider_gather_d2d(buf):
    def start(ins, outs, ssem, rsem):
        o = outs[0]
        mx, my, mc = _me()
        for j, (px, py) in enumerate(_chips_of(mx, my)):
            ps = 2 * px + py
            _rcopy(o.at[ps, mc], o.at[ps, mc], ssem.at[j], rsem.at[j], (mx, my, 1 - mc)).start()

    def wait(ins, outs, ssem, rsem):
        o = outs[0]
        mx, my, mc = _me()
        sib = (mx, my, 1 - mc)
        for j, (px, py) in enumerate(_chips_of(mx, my)):
            ps = 2 * px + py
            _rcopy(o.at[ps, 1 - mc], o.at[ps, 1 - mc], ssem.at[j], rsem.at[j], sib).wait_recv()
        for j, (px, py) in enumerate(_chips_of(mx, my)):
            ps = 2 * px + py
            _rcopy(o.at[ps, mc], o.at[ps, mc], ssem.at[j], rsem.at[j], sib).wait_send()

    return Rider([buf], [_sds(buf.shape, buf.dtype)], {0: 0}, 3, start, wait)


def rider_reduce_sib(buf):
    n, _, h, w = buf.shape

    def start(ins, outs, ssem, rsem):
        mx, my, mc = _me()
        for k in range(N_CHIP):
            _rcopy(ins[0].at[k, 1 - mc], outs[0].at[k], ssem.at[k], rsem.at[k], (mx, my, 1 - mc)).start()

    def wait(ins, outs, ssem, rsem):
        mx, my, mc = _me()
        for k in range(N_CHIP):
            _rcopy(ins[0].at[k, 1 - mc], outs[0].at[k], ssem.at[k], rsem.at[k], (mx, my, 1 - mc)).wait_recv()
        for k in range(N_CHIP):
            _rcopy(ins[0].at[k, 1 - mc], outs[0].at[k], ssem.at[k], rsem.at[k], (mx, my, 1 - mc)).wait_send()

    return Rider([buf], [_sds((n, h, w), buf.dtype)], {}, N_CHIP, start, wait)


def rider_gather_xy(buf):
    def peers():
        mx, my, mc = _me()
        return 2 * mx + my, mc, [(1 - mx, my), (mx, 1 - my)]

    def start(ins, outs, ssem, rsem):
        o = outs[0]
        s, mc, nb = peers()
        for j, (px, py) in enumerate(nb):
            _rcopy(o.at[s, mc], o.at[s, mc], ssem.at[j], rsem.at[j], (px, py, mc)).start()

    def wait(ins, outs, ssem, rsem):
        o = outs[0]
        s, mc, nb = peers()
        for j, (px, py) in enumerate(nb):
            _rcopy(o.at[2 * px + py, mc], o.at[2 * px + py, mc], ssem.at[j], rsem.at[j], (px, py, mc)).wait_recv()
        for j, (px, py) in enumerate(nb):
            _rcopy(o.at[s, mc], o.at[s, mc], ssem.at[j], rsem.at[j], (px, py, mc)).wait_send()

    return Rider([buf], [_sds(buf.shape, buf.dtype)], {0: 0}, 2, start, wait)


def rider_gather_fwd(buf):
    def start(ins, outs, ssem, rsem):
        o = outs[0]
        mx, my, mc = _me()
        xs = 2 * (1 - mx) + my
        _rcopy(o.at[xs, mc], o.at[xs, mc], ssem.at[0], rsem.at[0], (mx, 1 - my, mc)).start()

    def wait(ins, outs, ssem, rsem):
        o = outs[0]
        mx, my, mc = _me()
        xs, ds = 2 * (1 - mx) + my, 2 * (1 - mx) + (1 - my)
        _rcopy(o.at[ds, mc], o.at[ds, mc], ssem.at[0], rsem.at[0], (mx, 1 - my, mc)).wait_recv()
        _rcopy(o.at[xs, mc], o.at[xs, mc], ssem.at[0], rsem.at[0], (mx, 1 - my, mc)).wait_send()

    return Rider([buf], [_sds(buf.shape, buf.dtype)], {0: 0}, 1, start, wait)


def rider_reduce_copy(q, j, r=None):
    def peer():
        mx, my, mc = _me()
        px, py = _chips_of(mx, my)[j]
        return 2 * mx + my, 2 * px + py, (px, py, mc)

    def start(ins, outs, ssem, rsem):
        s, ps, dev = peer()
        _rcopy(ins[0].at[ps], outs[0].at[s], ssem.at[0], rsem.at[0], dev).start()

    def wait(ins, outs, ssem, rsem):
        s, ps, dev = peer()
        _rcopy(ins[0].at[ps], outs[0].at[ps], ssem.at[0], rsem.at[0], dev).wait_recv()
        _rcopy(ins[0].at[ps], outs[0].at[s], ssem.at[0], rsem.at[0], dev).wait_send()

    if r is None:
        return Rider([q], [_sds(q.shape, q.dtype)], {}, 1, start, wait)
    return Rider([q, r], [_sds(q.shape, q.dtype)], {1: 0}, 1, start, wait)


def rider_reduce_copies(q):
    def start(ins, outs, ssem, rsem):
        mx, my, mc = _me()
        s = 2 * mx + my
        for j, (px, py) in enumerate(_chips_of(mx, my)):
            _rcopy(ins[0].at[2 * px + py], outs[0].at[s], ssem.at[j], rsem.at[j], (px, py, mc)).start()

    def wait(ins, outs, ssem, rsem):
        mx, my, mc = _me()
        s = 2 * mx + my
        for j, (px, py) in enumerate(_chips_of(mx, my)):
            ps = 2 * px + py
            _rcopy(ins[0].at[ps], outs[0].at[ps], ssem.at[j], rsem.at[j], (px, py, mc)).wait_recv()
        for j, (px, py) in enumerate(_chips_of(mx, my)):
            _rcopy(ins[0].at[2 * px + py], outs[0].at[s], ssem.at[j], rsem.at[j], (px, py, mc)).wait_send()

    return Rider([q], [_sds(q.shape, q.dtype)], {}, 3, start, wait)


def rider_join(buf, g=None):
    def start(ins, outs, ssem, rsem):
        o = outs[0] if g is None else outs[0].at[g]
        mx, my, mc = _me()
        _rcopy(o.at[mc], o.at[mc], ssem.at[0], rsem.at[0], (mx, my, 1 - mc)).start()

    def wait(ins, outs, ssem, rsem):
        o = outs[0] if g is None else outs[0].at[g]
        mx, my, mc = _me()
        _rcopy(o.at[1 - mc], o.at[1 - mc], ssem.at[0], rsem.at[0], (mx, my, 1 - mc)).wait_recv()
        _rcopy(o.at[mc], o.at[mc], ssem.at[0], rsem.at[0], (mx, my, 1 - mc)).wait_send()

    return Rider([buf], [_sds(buf.shape, buf.dtype)], {0: 0}, 1, start, wait)


def _host_call(body, rider, name, grid, in_specs, out_specs, out_shape, operands, sem, n_in, n_out, aliases=None):
    aliases = dict(aliases or {})
    if rider is None:
        return pl.pallas_call(body, name=name, out_shape=out_shape, grid=grid, in_specs=in_specs, out_specs=out_specs,
                              input_output_aliases=aliases, compiler_params=_cp(sem, VMEM_BIG))(*operands)
    n_ri, n_ro = len(rider.ins), len(rider.outs)
    aliases.update({n_in + a: n_out + b for a, b in rider.aliases.items()})

    def hosted(*refs):
        ins, r_in = refs[:n_in], refs[n_in:n_in + n_ri]
        outs, r_out = refs[n_in + n_ri:n_in + n_ri + n_out], refs[n_in + n_ri + n_out:n_in + n_ri + n_out + n_ro]
        ssem, rsem = refs[-2], refs[-1]
        first = functools.reduce(lambda a, b: a & b, [pl.program_id(k) == 0 for k in range(len(grid))])
        last = functools.reduce(lambda a, b: a & b, [pl.program_id(k) == grid[k] - 1 for k in range(len(grid))])

        @pl.when(first)
        def _():
            rider.start(r_in, r_out, ssem, rsem)
        body(*ins, *outs)

        @pl.when(last)
        def _():
            rider.wait(r_in, r_out, ssem, rsem)

    return pl.pallas_call(
        hosted, name=name, out_shape=list(out_shape) + list(rider.outs), grid=grid,
        in_specs=list(in_specs) + [ANY] * n_ri, out_specs=list(out_specs) + [ANY] * n_ro,
        input_output_aliases=aliases,
        scratch_shapes=[pltpu.SemaphoreType.DMA((rider.nsem,)), pltpu.SemaphoreType.DMA((rider.nsem,))],
        compiler_params=_cp(("arbitrary",) * len(grid), VMEM_BIG))(*operands, *rider.ins)


def ffn_up(lay, hm, wbuf, ig, iu, ns, name, rider=None):
    tm = lay.tm

    def body(h_ref, wg_ref, wu_ref, sp_ref, sl_ref, u_ref, a_ref):
        hb = h_ref[...]
        g = _nt(hb, wg_ref[0])
        u = _nt(hb, wu_ref[0])
        sg = _sigmoid(g)
        sl = g * sg
        sp_ref[0] = (sg + sl * (1.0 - sg)).astype(BF16)
        sl_ref[0] = sl.astype(BF16)
        u_ref[0] = u.astype(BF16)
        a_ref[0] = (sl * u).astype(BF16)

    spec_o = pl.BlockSpec((1, tm, ns), lambda s, i: (s, i, 0))
    return _host_call(
        body, rider, name, (N_CHIP, lay.T // tm),
        [pl.BlockSpec((tm, D), lambda s, i: (i, 0)), pl.BlockSpec((1, ns, D), lambda s, i: (s, ig, 0)),
         pl.BlockSpec((1, ns, D), lambda s, i: (s, iu, 0))],
        [spec_o] * 4, [_sds((N_CHIP, lay.T, ns), BF16)] * 4, (hm, wbuf, wbuf), ("parallel", "parallel"), 3, 4)


def slab_nn_acc(lay, zs, wbuf, idxs, ns, name, rider=None):
    tm = lay.tm
    npair = len(zs)

    def body(*refs):
        o_ref = refs[-1]

        @pl.when(pl.program_id(1) == 0)
        def _():
            o_ref[...] = jnp.zeros_like(o_ref)
        acc = _nn(refs[0][0], refs[npair][0])
        for p in range(1, npair):
            acc += _nn(refs[p][0], refs[npair + p][0])
        o_ref[...] += acc

    in_specs = [pl.BlockSpec((1, tm, ns), lambda i, s: (s, i, 0)) for _ in zs]
    in_specs += [pl.BlockSpec((1, ns, D), functools.partial(lambda i, s, q: (s, q, 0), q=q)) for q in idxs]
    return _host_call(body, rider, name, (lay.T // tm, N_CHIP), in_specs, [pl.BlockSpec((tm, D), lambda i, s: (i, 0))],
                      [_sds((lay.T, D), F32)], (*zs, *([wbuf] * npair)), ("parallel", "arbitrary"), 2 * npair, 1)


def ffn_bwd_da(lay, dy, wbuf, idn, sp, sl, u, ns, name, rider=None):
    tm = lay.tm

    def body(dy_ref, wd_ref, sp_ref, sl_ref, u_ref, dg_ref, du_ref):
        da = _nt(dy_ref[...], wd_ref[0])
        dg_ref[0] = (da * u_ref[0].astype(F32) * sp_ref[0].astype(F32)).astype(BF16)
        du_ref[0] = (da * sl_ref[0].astype(F32)).astype(BF16)

    spec_z = pl.BlockSpec((1, tm, ns), lambda s, i: (s, i, 0))
    return _host_call(
        body, rider, name, (N_CHIP, lay.T // tm),
        [pl.BlockSpec((tm, D), lambda s, i: (i, 0)), pl.BlockSpec((1, ns, D), lambda s, i: (s, idn, 0)),
         spec_z, spec_z, spec_z],
        [spec_z] * 2, [_sds((N_CHIP, lay.T, ns), BF16)] * 2, (dy, wbuf, sp, sl, u), ("parallel", "parallel"), 5, 2)


def slab_tn(lay, z, x, gbuf, idx, ns, name, rider=None):
    tk = lay.tm

    def body(z_ref, x_ref, g_in, o_ref):
        del g_in

        @pl.when(pl.program_id(1) == 0)
        def _():
            o_ref[...] = jnp.zeros_like(o_ref)
        o_ref[0] += _tn(z_ref[0], x_ref[...])

    return _host_call(
        body, rider, name, (N_CHIP, lay.T // tk),
        [pl.BlockSpec((1, tk, ns), lambda s, k: (s, k, 0)), pl.BlockSpec((tk, D), lambda s, k: (k, 0)), ANY],
        [pl.BlockSpec((1, ns, D), lambda s, k: (s, idx, 0))], [_sds(gbuf.shape, F32)], (z, x, gbuf),
        ("parallel", "arbitrary"), 3, 1, aliases={2: 0})


Q0, K0, V0, U0, QR0, KR0, PEXT = 0, 512, 640, 768, 1280, 1792, 1920


def rope_fwd(lay, p, cos, sin, name):
    def fn(pb, cs, sn):
        cs4 = jnp.concatenate([cs] * 4, axis=1)
        sn4 = jnp.concatenate([sn] * 4, axis=1)
        qr = pb[:, Q0:K0] * cs4 + pb[:, QR0:KR0] * sn4
        kr = pb[:, K0:V0] * cs + pb[:, KR0:PEXT] * sn
        return qr, kr, pb[:, V0:U0], pb[:, U0:QR0]
    return rowwise(lay, name, fn, [p, cos, sin], outs=[(ATT_W, BF16), (KV_W, BF16), (KV_W, BF16), (POOL_W, F32)])


def rope_bwd(lay, dqr, dkr, dv, du, cos, sin, name):
    def fn(dq, dk, dvb, dub, cs, sn):
        cs4 = jnp.concatenate([cs] * 4, axis=1)
        sn4 = jnp.concatenate([sn] * 4, axis=1)
        return (jnp.concatenate([dq * cs4, dk * cs, dvb, dub, dq * sn4, dk * sn], axis=1),)
    return rowwise(lay, name, fn, [dqr, dkr, dv, du, cos, sin], outs=[(PEXT, BF16)])[0]


def _attn_specs(lay):
    nbs, cbk, lbk = lay.PS // BLK, lay.C // BLK, lay.L // BLK

    def kv_map(j):
        return lambda s, n: (s * nbs + cbk + jnp.clip(n - cbk + j - 1, 0, lbk - 1), 0)

    win = [pl.BlockSpec((BLK, KV_W), kv_map(j)) for j in range(3)]
    ctx = pl.BlockSpec((lay.C, KV_W), lambda s, n: (s * (lay.PS // lay.C), 0))
    return nbs, cbk, lbk, win, ctx


def _attn_masks(n, cbk, lbk):
    row = lax.broadcasted_iota(jnp.int32, (BLK, BLK), 0)
    col = lax.broadcasted_iota(jnp.int32, (BLK, BLK), 1)
    m = n - cbk
    lat = n >= cbk
    valid = [lat & (m >= 1) & (col >= row), lat & (col >= 0), lat & (m <= lbk - 2) & (col <= row)]
    lane_lo = lax.broadcasted_iota(jnp.int32, (BLK, 2 * HEAD_DIM), 1) < HEAD_DIM
    return valid, lane_lo


def attn_fwd(lay, qr, kr, vb, sink_tab, name):
    nbs, cbk, lbk, win, ctx = _attn_specs(lay)

    def body(q_ref, k0, k1, k2, kc_ref, v0, v1, v2, vc_ref, sk_ref, o_ref, l_ref):
        n = pl.program_id(1)
        valid, lane_lo = _attn_masks(n, cbk, lbk)
        valid4 = [jnp.concatenate([v] * 4, axis=0) for v in valid]
        ks = [k0[...], k1[...], k2[...]]
        vs = [v0[...], v1[...], v2[...]]
        kc, vc = kc_ref[...], vc_ref[...]
        q2s = [q_ref[:, p * 128:(p + 1) * 128] for p in range(4)]
        outs, lses = [], []
        for hh in range(2):
            sel = lane_lo == (hh == 0)
            qm = jnp.concatenate([jnp.where(sel, q2, jnp.zeros_like(q2)) for q2 in q2s], axis=0)
            sk = jnp.concatenate([jnp.broadcast_to(sk_ref[p:p + 1, hh * HEAD_DIM:hh * HEAD_DIM + 1], (BLK, 1))
                                  for p in range(4)], axis=0)
            sw = [jnp.where(valid4[j], _nt(qm, ks[j]) * ATT_SCALE, NEG_INF) for j in range(3)]
            sc = _nt(qm, kc) * ATT_SCALE
            mx = jnp.maximum(jnp.maximum(jnp.maximum(sw[0].max(-1, keepdims=True), sw[1].max(-1, keepdims=True)),
                                         jnp.maximum(sw[2].max(-1, keepdims=True), sc.max(-1, keepdims=True))), sk)
            ew = [jnp.exp(s - mx) for s in sw]
            ec = jnp.exp(sc - mx)
            den = ew[0].sum(-1, keepdims=True) + ew[1].sum(-1, keepdims=True) + ew[2].sum(-1, keepdims=True)
            den = den + ec.sum(-1, keepdims=True) + jnp.exp(sk - mx)
            o = _nn((ec / den).astype(BF16), vc)
            for j in range(3):
                o += _nn((ew[j] / den).astype(BF16), vs[j])
            outs.append(o)
            lses.append(mx + jnp.log(den))
        for p in range(4):
            rows = slice(p * BLK, (p + 1) * BLK)
            o_ref[:, p * 128:(p + 1) * 128] = jnp.where(lane_lo, outs[0][rows], outs[1][rows]).astype(o_ref.dtype)
            l_ref[:, p * 128:(p + 1) * 128] = jnp.where(lane_lo, jnp.broadcast_to(lses[0][rows], (BLK, 128)),
                                                        jnp.broadcast_to(lses[1][rows], (BLK, 128)))

    qspec = pl.BlockSpec((BLK, ATT_W), lambda s, n: (s * nbs + n, 0))
    return pl.pallas_call(
        body, name=name, out_shape=[_sds((lay.T, ATT_W), BF16), _sds((lay.T, ATT_W), F32)], grid=(2, nbs),
        in_specs=[qspec] + win + [ctx] + win + [ctx] + [pl.BlockSpec((8, 128), lambda s, n: (0, 0))],
        out_specs=[qspec, qspec], compiler_params=_cp(("parallel", "parallel")))(qr, kr, kr, kr, kr, vb, vb, vb, vb, sink_tab)


def attn_bwd(lay, qr, kr, vb, sink_tab, lse, datt, name, rider=None):
    nbs, cbk, lbk, win, ctx = _attn_specs(lay)
    C, PS = lay.C, lay.PS

    def body(q_ref, k0, k1, k2, kc_ref, v0, v1, v2, vc_ref, sk_ref, l_ref, do_ref, dq_ref, dk_ref, dv_ref, ds_ref):
        n = pl.program_id(1)
        valid, lane_lo = _attn_masks(n, cbk, lbk)

        @pl.when(n == 0)
        def _():
            dk_ref[...] = jnp.zeros_like(dk_ref)
            dv_ref[...] = jnp.zeros_like(dv_ref)
            ds_ref[...] = jnp.zeros_like(ds_ref)

        ks = [k0[...], k1[...], k2[...], kc_ref[...]]
        vs = [v0[...], v1[...], v2[...], vc_ref[...]]
        valid4 = [jnp.concatenate([v] * 4, axis=0) for v in valid]
        dks = [jnp.zeros((BLK, KV_W), F32)] * 3 + [jnp.zeros((C, KV_W), F32)]
        dvs = list(dks)
        q2s = [q_ref[:, p * 128:(p + 1) * 128] for p in range(4)]
        do2s = [do_ref[:, p * 128:(p + 1) * 128].astype(BF16) for p in range(4)]
        lse2s = [l_ref[:, p * 128:(p + 1) * 128] for p in range(4)]
        dq_h, dd_h = [], []
        for hh in range(2):
            sel = lane_lo == (hh == 0)
            qm = jnp.concatenate([jnp.where(sel, q2, jnp.zeros_like(q2)) for q2 in q2s], axis=0)
            dom = jnp.concatenate([jnp.where(sel, d2, jnp.zeros_like(d2)) for d2 in do2s], axis=0)
            lse_h = jnp.concatenate([l2[:, hh * HEAD_DIM:hh * HEAD_DIM + 1] for l2 in lse2s], axis=0)
            ps, dps = [], []
            for j in range(4):
                s = _nt(qm, ks[j]) * ATT_SCALE
                if j < 3:
                    s = jnp.where(valid4[j], s, NEG_INF)
                ps.append(jnp.exp(s - lse_h))
                dps.append(_nt(dom, vs[j]))
            dd = (ps[0] * dps[0]).sum(-1, keepdims=True) + (ps[1] * dps[1]).sum(-1, keepdims=True)
            dd = dd + (ps[2] * dps[2]).sum(-1, keepdims=True) + (ps[3] * dps[3]).sum(-1, keepdims=True)
            dq = jnp.zeros((4 * BLK, 128), F32)
            for j in range(4):
                dsb = (ps[j] * (dps[j] - dd) * ATT_SCALE).astype(BF16)
                dq += _nn(dsb, ks[j])
                dks[j] = dks[j] + _tn(dsb, qm)
                dvs[j] = dvs[j] + _tn(ps[j].astype(BF16), dom)
            dq_h.append(dq)
            dd_h.append(dd)
        for p in range(4):
            sl = slice(p * 128, (p + 1) * 128)
            rows = slice(p * BLK, (p + 1) * BLK)
            dq_ref[:, sl] = jnp.where(lane_lo, dq_h[0][rows], dq_h[1][rows])
            dd2 = jnp.where(lane_lo, jnp.broadcast_to(dd_h[0][rows], (BLK, 128)), jnp.broadcast_to(dd_h[1][rows], (BLK, 128)))
            psink = jnp.exp(sk_ref[p:p + 1, :] - lse2s[p])
            ds_ref[0, p:p + 1, :] += -jnp.sum(psink * dd2, axis=0, keepdims=True)
        dk_ref[0:C, :] += dks[3]
        dv_ref[0:C, :] += dvs[3]
        for j in range(3):
            r0 = pl.multiple_of((cbk + jnp.clip(n - cbk + j - 1, 0, lbk - 1)) * BLK, BLK)
            dk_ref[pl.ds(r0, BLK), :] += dks[j]
            dv_ref[pl.ds(r0, BLK), :] += dvs[j]

    qspec = pl.BlockSpec((BLK, ATT_W), lambda s, n: (s * nbs + n, 0))
    kvout = pl.BlockSpec((PS, KV_W), lambda s, n: (s, 0))
    return _host_call(
        body, rider, name, (2, nbs),
        [qspec] + win + [ctx] + win + [ctx] + [pl.BlockSpec((8, 128), lambda s, n: (0, 0)), qspec, qspec],
        [qspec, kvout, kvout, pl.BlockSpec((1, 8, 128), lambda s, n: (s, 0, 0))],
        [_sds((lay.T, ATT_W), F32), _sds((lay.T, KV_W), F32), _sds((lay.T, KV_W), F32), _sds((2, 8, 128), F32)],
        (qr, kr, kr, kr, kr, vb, vb, vb, vb, sink_tab, lse, datt), ("parallel", "arbitrary"), 12, 4)


def _winsum(x, r):
    n = x.shape[0]
    t = lax.broadcasted_iota(jnp.int32, x.shape, 0)
    acc = x
    for o in range(1, r + 1):
        acc = acc + jnp.where(t >= o, pltpu.roll(x, o, 0), 0.0) + jnp.where(t < n - o, pltpu.roll(x, n - o, 0), 0.0)
    return acc


def _wincount(n, r):
    t = lax.broadcasted_iota(jnp.int32, (n, 128), 0)
    return (jnp.minimum(t + r, n - 1) - jnp.maximum(t - r, 0) + 1).astype(F32)


def pool_fwd(lay, u, w_pool, scale, name):
    segs = [(0, lay.C), (lay.C, lay.L)]

    def body(u_ref, w_ref, s_ref, o_ref):
        for r0, n in segs:
            for g, wd in enumerate(POOL_WINDOWS):
                sl = slice(g * 128, (g + 1) * 128)
                x = u_ref[r0:r0 + n, sl]
                d = _winsum(x, wd // 2) / _wincount(n, wd // 2) - x
                y = _nn(d.astype(BF16), w_ref[g].astype(BF16)) * s_ref[:, sl]
                o_ref[r0:r0 + n, sl] = y.astype(o_ref.dtype)

    spec = pl.BlockSpec((lay.PS, POOL_W), lambda s: (s, 0))
    return pl.pallas_call(
        body, name=name, out_shape=_sds((lay.T, POOL_W), BF16), grid=(2,),
        in_specs=[spec, pl.BlockSpec(w_pool.shape, lambda s: (0, 0, 0)), pl.BlockSpec((1, POOL_W), lambda s: (0, 0))],
        out_specs=spec, compiler_params=_cp(("parallel",), VMEM_BIG))(u, w_pool, scale)


def pool_bwd(lay, u, dcat, w_pool, scale, name):
    segs = [(0, lay.C), (lay.C, lay.L)]

    def body(u_ref, dp_ref, w_ref, s_ref, du_ref, dw_ref, dsc_ref):
        for g, wd in enumerate(POOL_WINDOWS):
            sl = slice(g * 128, (g + 1) * 128)
            wb = w_ref[g].astype(BF16)
            dw = jnp.zeros((128, 128), F32)
            dsc = jnp.zeros((1, 128), F32)
            for r0, n in segs:
                x = u_ref[r0:r0 + n, sl]
                cnt = _wincount(n, wd // 2)
                d = (_winsum(x, wd // 2) / cnt - x).astype(BF16)
                dp = dp_ref[r0:r0 + n, sl]
                dsc += jnp.sum(_nn(d, wb) * dp, axis=0, keepdims=True)
                dyp = (dp * s_ref[:, sl]).astype(BF16)
                dw += _tn(d, dyp)
                dd = _nt(dyp, wb)
                du_ref[r0:r0 + n, sl] = _winsum(dd / cnt, wd // 2) - dd
            dw_ref[0, g] = dw
            dsc_ref[0, :, sl] = dsc

    spec = pl.BlockSpec((lay.PS, POOL_W), lambda s: (s, 0))
    return pl.pallas_call(
        body, name=name,
        out_shape=[_sds((lay.T, POOL_W), F32), _sds((2, 4, 128, 128), F32), _sds((2, 1, POOL_W), F32)], grid=(2,),
        in_specs=[spec, pl.BlockSpec((lay.PS, POOL_W), lambda s: (s, 1)), pl.BlockSpec(w_pool.shape, lambda s: (0, 0, 0)),
                  pl.BlockSpec((1, POOL_W), lambda s: (0, 0))],
        out_specs=[spec, pl.BlockSpec((1, 4, 128, 128), lambda s: (s, 0, 0, 0)), pl.BlockSpec((1, 1, POOL_W), lambda s: (s, 0, 0))],
        compiler_params=_cp(("parallel",), VMEM_BIG))(u, dcat, w_pool, scale)


CONV_OFFS = (-1, 0, 1, 2)
CW = 256


def _shift_rows(x, o):
    if o == 0:
        return x
    n = x.shape[0]
    t = lax.broadcasted_iota(jnp.int32, x.shape, 0)
    if o < 0:
        return jnp.where(t >= -o, pltpu.roll(x, -o, 0), 0.0)
    return jnp.where(t < n - o, pltpu.roll(x, n - o, 0), 0.0)


def conv_fwd(lay, p, col0, w, b, name):
    segs = [(0, lay.C), (lay.C, lay.L)]
    cb0 = col0 // CW

    def body(x_ref, w_ref, b_ref, o_ref):
        for r0, n in segs:
            x = x_ref[r0:r0 + n, :]
            y = jnp.broadcast_to(b_ref[...], x.shape)
            for k, o in enumerate(CONV_OFFS):
                y = y + _shift_rows(x, o) * w_ref[k:k + 1, :]
            o_ref[r0:r0 + n, :] = y

    return pl.pallas_call(
        body, name=name, out_shape=_sds((lay.T, D), F32), grid=(2, D // CW),
        in_specs=[pl.BlockSpec((lay.PS, CW), lambda s, j: (s, cb0 + j)), pl.BlockSpec((4, CW), lambda s, j: (0, j)),
                  pl.BlockSpec((1, CW), lambda s, j: (0, j))],
        out_specs=pl.BlockSpec((lay.PS, CW), lambda s, j: (s, j)),
        compiler_params=_cp(("parallel", "parallel")))(p, w, b)


def conv_bwd(lay, p, col0, w, duc, name):
    segs = [(0, lay.C), (lay.C, lay.L)]
    cb0 = col0 // CW

    def body(x_ref, w_ref, g_ref, du_ref, dw_ref, db_ref):
        dws = [jnp.zeros((1, CW), F32)] * 4
        db = jnp.zeros((1, CW), F32)
        for r0, n in segs:
            x = x_ref[r0:r0 + n, :]
            g = g_ref[r0:r0 + n, :]
            du = jnp.zeros_like(g)
            for k, o in enumerate(CONV_OFFS):
                du = du + _shift_rows(g, -o) * w_ref[k:k + 1, :]
                dws[k] = dws[k] + jnp.sum(g * _shift_rows(x, o), axis=0, keepdims=True)
            db = db + jnp.sum(g, axis=0, keepdims=True)
            du_ref[r0:r0 + n, :] = du.astype(du_ref.dtype)
        dw_ref[0] = jnp.concatenate(dws, axis=0)
        db_ref[0] = db

    return pl.pallas_call(
        body, name=name, out_shape=[_sds((lay.T, D), BF16), _sds((2, 4, D), F32), _sds((2, 1, D), F32)], grid=(2, D // CW),
        in_specs=[pl.BlockSpec((lay.PS, CW), lambda s, j: (s, cb0 + j)), pl.BlockSpec((4, CW), lambda s, j: (0, j)),
                  pl.BlockSpec((lay.PS, CW), lambda s, j: (s, j))],
        out_specs=[pl.BlockSpec((lay.PS, CW), lambda s, j: (s, j)), pl.BlockSpec((1, 4, CW), lambda s, j: (s, 0, j)),
                   pl.BlockSpec((1, 1, CW), lambda s, j: (s, 0, j))],
        compiler_params=_cp(("parallel", "parallel")))(p, w, duc)


def _softplus_neg(lam):
    z = -lam
    w = jnp.exp(-jnp.abs(z))
    log1p = jnp.where(w < 1e-2, w * (1.0 - w * (0.5 - w / 3.0)), jnp.log(1.0 + w))
    return jnp.maximum(z, 0.0) + log1p, -_sigmoid(z)


def _neg_expm1(x):
    series = -x * (1.0 + x * (0.5 + x * (1.0 / 6.0 + x * (1.0 / 24.0 + x * (1.0 / 120.0)))))
    return jnp.where(x > -0.05, series, 1.0 - jnp.exp(x))


def _lru_gates(x, xb, wa, wx, ba, bx, lam):
    r = _sigmoid(_nn(xb, wa.astype(BF16)) + ba)
    gi = _sigmoid(_nn(xb, wx.astype(BF16)) + bx)
    sp, dsp = _softplus_neg(lam)
    la = -LRU_C * r * sp
    a = jnp.exp(la)
    sq = jnp.sqrt(_neg_expm1(2.0 * la))
    return r, gi, sp, dsp, a, sq


def lru_coeffs(lay, uc, wa, wx, vec, name):
    tr = lay.tc

    def body(x_ref, wa_ref, wx_ref, v_ref, a_ref, b_ref):
        for h in range(8):
            sl = slice(h * 128, (h + 1) * 128)
            x = x_ref[:, sl]
            xb = x.astype(BF16)
            for d in range(2):
                _, gi, _, _, a, sq = _lru_gates(x, xb, wa_ref[d, h], wx_ref[d, h], v_ref[d:d + 1, sl],
                                                v_ref[2 + d:3 + d, sl], v_ref[4 + d:5 + d, sl])
                a_ref[d, h] = a
                b_ref[d, h] = sq * (gi * x)

    wspec = pl.BlockSpec((2, 8, 128, 128), lambda i: (0, 0, 0, 0))
    ospec = pl.BlockSpec((2, 8, tr, 128), lambda i: (0, 0, i, 0))
    return pl.pallas_call(
        body, name=name, out_shape=[_sds((2, 8, lay.T, 128), F32)] * 2, grid=(lay.T // tr,),
        in_specs=[pl.BlockSpec((tr, D), lambda i: (i, 0)), wspec, wspec, pl.BlockSpec((6, D), lambda i: (0, 0))],
        out_specs=[ospec, ospec], compiler_params=_cp(("parallel",), VMEM_BIG))(uc, wa, wx, vec)


def lru_coeffs_bwd(lay, uc, wa, wx, vec, da, db, name, rider=None):
    tr = lay.tc

    def body(x_ref, wa_ref, wx_ref, v_ref, da_ref, db_ref, dx_ref, dwa_ref, dwx_ref, dv_ref):
        @pl.when(pl.program_id(0) == 0)
        def _():
            dwa_ref[...] = jnp.zeros_like(dwa_ref)
            dwx_ref[...] = jnp.zeros_like(dwx_ref)
            dv_ref[...] = jnp.zeros_like(dv_ref)

        for h in range(8):
            sl = slice(h * 128, (h + 1) * 128)
            x = x_ref[:, sl]
            xb = x.astype(BF16)
            dx = jnp.zeros_like(x)
            for d in range(2):
                wab, wxb = wa_ref[d, h].astype(BF16), wx_ref[d, h].astype(BF16)
                r, gi, sp, dsp, a, sq = _lru_gates(x, xb, wa_ref[d, h], wx_ref[d, h], v_ref[d:d + 1, sl],
                                                   v_ref[2 + d:3 + d, sl], v_ref[4 + d:5 + d, sl])
                dbv, dav = db_ref[d, h], da_ref[d, h]
                t1 = dbv * sq
                dgi = t1 * x
                dx = dx + t1 * gi
                dla = dav * a - (dbv * gi * x) * (a * a) / sq
                dr = dla * (-LRU_C * sp)
                dlam = jnp.sum(dla * (-LRU_C * r), axis=0, keepdims=True) * dsp
                dpa = dr * r * (1.0 - r)
                dpx = dgi * gi * (1.0 - gi)
                dpab, dpxb = dpa.astype(BF16), dpx.astype(BF16)
                dwa_ref[d, h] += _tn(xb, dpab)
                dwx_ref[d, h] += _tn(xb, dpxb)
                dx = dx + _nt(dpab, wab) + _nt(dpxb, wxb)
                dv_ref[d:d + 1, sl] += jnp.sum(dpa, axis=0, keepdims=True)
                dv_ref[2 + d:3 + d, sl] += jnp.sum(dpx, axis=0, keepdims=True)
                dv_ref[4 + d:5 + d, sl] += dlam
            dx_ref[:, sl] = dx

    wspec = pl.BlockSpec((2, 8, 128, 128), lambda i: (0, 0, 0, 0))
    gspec = pl.BlockSpec((2, 8, tr, 128), lambda i: (0, 0, i, 0))
    vspec = pl.BlockSpec((6, D), lambda i: (0, 0))
    xspec = pl.BlockSpec((tr, D), lambda i: (i, 0))
    return _host_call(
        body, rider, name, (lay.T // tr,), [xspec, wspec, wspec, vspec, gspec, gspec], [xspec, wspec, wspec, vspec],
        [_sds((lay.T, D), F32), _sds((2, 8, 128, 128), F32), _sds((2, 8, 128, 128), F32), _sds((6, D), F32)],
        (uc, wa, wx, vec, da, db), ("arbitrary",), 6, 4)


GB = 2
SCAN_UNROLL = 4


def _tile_scan(a, b, up):
    t = lax.broadcasted_iota(jnp.int32, a.shape, 0)
    for d in (1, 2, 4):
        sh = 8 - d if up else d
        m = (t < 8 - d) if up else (t >= d)
        a_prev, b_prev = pltpu.roll(a, sh, 0), pltpu.roll(b, sh, 0)
        b = jnp.where(m, a * b_prev + b, b)
        a = jnp.where(m, a * a_prev, a)
    return a, b


def lru_scan(lay, a, b, name):
    segs = [(0, lay.C), (lay.C, lay.L)]

    def body(a_ref, b_ref, s_ref):
        for d in range(2):
            rev = d == 1
            state = tuple(jnp.zeros((1, 128), F32) for _ in range(GB))
            for base, n in segs:
                nt = n // 8

                def step(j, c, base=base, nt=nt, rev=rev, d=d):
                    c = list(c)
                    for u in range(SCAN_UNROLL):
                        jj = j * SCAN_UNROLL + u
                        r0 = pl.multiple_of(base + 8 * ((nt - 1 - jj) if rev else jj), 8)
                        for g in range(GB):
                            at, bt = _tile_scan(a_ref[d, g, pl.ds(r0, 8), :], b_ref[d, g, pl.ds(r0, 8), :], rev)
                            h = at * c[g] + bt
                            s_ref[d, g, pl.ds(r0, 8), :] = h
                            c[g] = h[0:1] if rev else h[7:8]
                    return tuple(c)

                state = lax.fori_loop(0, nt // SCAN_UNROLL, step, state)

    spec = pl.BlockSpec((2, GB, lay.PS, 128), lambda s, hb: (0, hb, s, 0))
    return pl.pallas_call(
        body, name=name, out_shape=_sds((2, 8, lay.T, 128), F32), grid=(2, 8 // GB),
        in_specs=[spec, spec], out_specs=spec, compiler_params=_cp(("parallel", "parallel"), VMEM_BIG))(a, b)


def lru_scan_bwd(lay, a, s, dy, name):
    segs = [(0, lay.C), (lay.C, lay.L)]
    C, PS = lay.C, lay.PS

    def body(a_ref, s_ref, g_ref, da_ref, db_ref):
        t = lax.broadcasted_iota(jnp.int32, (8, 128), 0)
        for d in range(2):
            rev = d == 1
            carry = tuple(jnp.zeros((1, 128), F32) for _ in range(GB))
            for si in (1, 0):
                base, n = segs[si]
                nt = n // 8

                def step(j, c, base=base, nt=nt, rev=rev, d=d):
                    c = list(c)
                    for u in range(SCAN_UNROLL):
                        jj = j * SCAN_UNROLL + u
                        r0 = pl.multiple_of(base + 8 * (jj if rev else (nt - 1 - jj)), 8)
                        if rev:
                            rn = pl.multiple_of(jnp.where(r0 == PS - 8, 0, r0 + 8), 8)
                            nb_zero = r0 == C - 8
                        else:
                            rn = pl.multiple_of(jnp.maximum(r0 - 8, 0), 8)
                            nb_zero = r0 == 0
                        for g in range(GB):
                            av = a_ref[d, g, pl.ds(r0, 8), :]
                            gv = g_ref[g, pl.ds(r0, 8), :]
                            sv = s_ref[d, g, pl.ds(r0, 8), :]
                            nbt = s_ref[d, g, pl.ds(rn, 8), :]
                            at, bt = _tile_scan(av, av * gv, not rev)
                            m = at * c[g] + bt
                            if rev:
                                m_next = jnp.where(t >= 1, pltpu.roll(m, 1, 0), c[g])
                                nb = jnp.where(nb_zero, 0.0, nbt[0:1])
                                h_prev = jnp.where(t < 7, pltpu.roll(sv, 7, 0), nb)
                                c[g] = m[7:8]
                            else:
                                m_next = jnp.where(t < 7, pltpu.roll(m, 7, 0), c[g])
                                nb = jnp.where(nb_zero, 0.0, nbt[7:8])
                                h_prev = jnp.where(t >= 1, pltpu.roll(sv, 1, 0), nb)
                                c[g] = m[0:1]
                            lam = gv + m_next
                            db_ref[d, g, pl.ds(r0, 8), :] = lam
                            da_ref[d, g, pl.ds(r0, 8), :] = lam * h_prev
                    return tuple(c)

                carry = lax.fori_loop(0, nt // SCAN_UNROLL, step, carry)

    spec = pl.BlockSpec((2, GB, lay.PS, 128), lambda s, hb: (0, hb, s, 0))
    return pl.pallas_call(
        body, name=name, out_shape=[_sds((2, 8, lay.T, 128), F32)] * 2, grid=(2, 8 // GB),
        in_specs=[spec, spec, pl.BlockSpec((GB, lay.PS, 128), lambda s, hb: (hb, s, 0))],
        out_specs=[spec, spec], compiler_params=_cp(("parallel", "parallel"), VMEM_BIG))(a, s, dy)


def _gelu(x):
    k = math.sqrt(2.0 / math.pi)
    t = jnp.tanh(k * (x + 0.044715 * x * x * x))
    return 0.5 * x * (1.0 + t), 0.5 * (1.0 + t) + 0.5 * x * (1.0 - t * t) * k * (1.0 + 3 * 0.044715 * x * x)


def lru_gate(lay, p, s, name):
    tr = lay.tr

    def body(g_ref, s_ref, o_ref):
        for h in range(8):
            sl = slice(h * 128, (h + 1) * 128)
            o_ref[:, sl] = (_gelu(g_ref[:, sl])[0] * (s_ref[0, h] + s_ref[1, h])).astype(o_ref.dtype)

    return pl.pallas_call(
        body, name=name, out_shape=_sds((lay.T, D), BF16), grid=(lay.nblk,),
        in_specs=[pl.BlockSpec((tr, D), lambda i: (i, 0)), pl.BlockSpec((2, 8, tr, 128), lambda i: (0, 0, i, 0))],
        out_specs=pl.BlockSpec((tr, D), lambda i: (i, 0)), compiler_params=_cp(("parallel",)))(p, s)


def lru_gate_bwd(lay, p, s, do, name):
    tr = lay.tr

    def body(g_ref, s_ref, do_ref, dg_ref, dy_ref):
        for h in range(8):
            sl = slice(h * 128, (h + 1) * 128)
            ge, dge = _gelu(g_ref[:, sl])
            dov = do_ref[:, sl]
            dg_ref[:, sl] = (dov * (s_ref[0, h] + s_ref[1, h]) * dge).astype(dg_ref.dtype)
            dy_ref[h] = dov * ge

    xspec = pl.BlockSpec((tr, D), lambda i: (i, 0))
    return pl.pallas_call(
        body, name=name, out_shape=[_sds((lay.T, D), BF16), _sds((8, lay.T, 128), F32)], grid=(lay.nblk,),
        in_specs=[xspec, pl.BlockSpec((2, 8, tr, 128), lambda i: (0, 0, i, 0)), xspec],
        out_specs=[xspec, pl.BlockSpec((8, tr, 128), lambda i: (0, i, 0))],
        compiler_params=_cp(("parallel",)))(p, s, do)


def silu_rows(x, name):
    def body(x_ref, o_ref):
        v = x_ref[...]
        o_ref[...] = (v * _sigmoid(v)).astype(o_ref.dtype)
    return pl.pallas_call(body, name=name, out_shape=_sds(x.shape, BF16), in_specs=[VMEM_SPEC], out_specs=VMEM_SPEC)(x)


def mod_grad_rows(gath, name):
    w = gath.shape[-1]

    def body(g_ref, dm_ref, db_ref):
        dm_ref[...] = jnp.zeros_like(dm_ref)
        for l in range(2):
            ctx = g_ref[0, 3 * l + 2:3 * l + 3, :]
            tot = g_ref[0, 3 * l:3 * l + 1, :] + g_ref[0, 3 * l + 1:3 * l + 2, :]
            for k in range(8):
                dm_ref[l, 2 * k:2 * k + 2, :] = g_ref[k, 3 * l:3 * l + 2, :]
                if k:
                    ctx = ctx + g_ref[k, 3 * l + 2:3 * l + 3, :]
                    tot = tot + (g_ref[k, 3 * l:3 * l + 1, :] + g_ref[k, 3 * l + 1:3 * l + 2, :])
            dm_ref[l, 16:17, :] = ctx
            db_ref[l:l + 1, :] = tot + ctx

    return pl.pallas_call(body, name=name, out_shape=[_sds((2, 32, w), F32), _sds((2, w), F32)],
                          in_specs=[VMEM_SPEC], out_specs=[VMEM_SPEC, VMEM_SPEC])(gath)


def cctx_grad(p, c_ctx, name):
    def body(a_ref, c_ref, o_ref):
        cv = c_ref[...]
        sg = _sigmoid(cv)
        o_ref[...] = 0.5 * (a_ref[0, 0:1, :] + a_ref[1, 0:1, :]) * (sg * (1.0 + cv * (1.0 - sg)))
    return pl.pallas_call(body, name=name, out_shape=_sds((1, D), F32), in_specs=[VMEM_SPEC] * 2,
                          out_specs=VMEM_SPEC)(p, c_ctx)


def loss_and_grad(lay, h, tgt, name):
    def fn(hb, tb):
        lat = (pl.program_id(0) % lay.bps) >= lay.cb
        e = jnp.where(lat, hb - tb, 0.0)
        return e * (1.0 / D), jnp.sum(e * e, axis=0, keepdims=True) * (0.5 / D)
    return rowwise(lay, name, fn, [h, tgt], outs=[(D, F32)], sums=[(1, D)])


def adamw(w, g, m, v, name):
    shape = w.shape
    w2, g2, m2, v2 = (t.reshape(-1, shape[-1]) for t in (w, g, m, v))
    rows, width = w2.shape
    tr = 256 if rows % 256 == 0 else rows
    c1 = 1.0 - ADAM_B1 ** ADAM_STEP
    c2 = 1.0 - ADAM_B2 ** ADAM_STEP

    def body(w_ref, g_ref, m_ref, v_ref, d_ref, mo_ref, vo_ref):
        gv = g_ref[...]
        mn = ADAM_B1 * m_ref[...] + (1.0 - ADAM_B1) * gv
        vn = ADAM_B2 * v_ref[...] + (1.0 - ADAM_B2) * (gv * gv)
        d_ref[...] = -ADAM_LR * ((mn / c1) / (jnp.sqrt(vn / c2) + ADAM_EPS) + ADAM_WD * w_ref[...])
        mo_ref[...] = mn
        vo_ref[...] = vn

    spec = pl.BlockSpec((tr, width), lambda i: (i, 0))
    d, mn, vn = pl.pallas_call(body, name=name, out_shape=[_sds((rows, width), F32)] * 3, grid=(rows // tr,),
                               in_specs=[spec] * 4, out_specs=[spec] * 3, compiler_params=_cp(("parallel",)))(w2, g2, m2, v2)
    return d.reshape(shape), mn.reshape(shape), vn.reshape(shape)


def adamw_ffn(w, m, v, red, kind, ns, name):
    shape = w.shape
    w2, m2, v2 = (t.reshape(-1, shape[-1]) for t in (w, m, v))
    rows, width = w2.shape
    c1 = 1.0 - ADAM_B1 ** ADAM_STEP
    c2 = 1.0 - ADAM_B2 ** ADAM_STEP
    tr, nb = ns // 2, 2
    gspec = pl.BlockSpec((tr, D), lambda i: (((i // nb) * 3 + kind) * nb + i % nb, 0))

    def body(w_ref, g_ref, m_ref, v_ref, go_ref, d_ref, mo_ref, vo_ref):
        gv = g_ref[...]
        mn = ADAM_B1 * m_ref[...] + (1.0 - ADAM_B1) * gv
        vn = ADAM_B2 * v_ref[...] + (1.0 - ADAM_B2) * (gv * gv)
        go_ref[...] = gv
        d_ref[...] = -ADAM_LR * ((mn / c1) / (jnp.sqrt(vn / c2) + ADAM_EPS) + ADAM_WD * w_ref[...])
        mo_ref[...] = mn
        vo_ref[...] = vn

    spec = pl.BlockSpec((tr, width), lambda i: (i, 0))
    outs = pl.pallas_call(body, name=name, out_shape=[_sds((rows, width), F32)] * 4, grid=(rows // tr,),
                          in_specs=[spec, gspec, spec, spec], out_specs=[spec] * 4,
                          compiler_params=_cp(("parallel",)))(w2, red, m2, v2)
    return tuple(t.reshape(shape) for t in outs)


def mod_mm(sc, w_mod, bias, name):
    wm = w_mod.shape[-1]
    tn = _pick(wm, (768, 512, 384, 256, 128))

    def body(a_ref, b_ref, c_ref, o_ref):
        o_ref[...] = _nn(a_ref[...], b_ref[...].astype(BF16)) + c_ref[...]

    return pl.pallas_call(
        body, name=name, out_shape=_sds((DEPTH, 32, wm), F32), grid=(DEPTH, wm // tn),
        in_specs=[pl.BlockSpec((32, D), lambda l, j: (0, 0)), pl.BlockSpec((None, D, tn), lambda l, j: (l, 0, j)),
                  pl.BlockSpec((None, 1, tn), lambda l, j: (l, 0, j))],
        out_specs=pl.BlockSpec((None, 32, tn), lambda l, j: (l, 0, j)),
        compiler_params=_cp(("parallel", "parallel")))(sc, w_mod, bias)


def wmod_dw(sc, dcol, name):
    wm = dcol.shape[-1]
    tm = 256

    def body(a_ref, b_ref, o_ref):
        o_ref[...] = _tn(a_ref[...], b_ref[...].astype(BF16))

    return pl.pallas_call(
        body, name=name, out_shape=_sds((DEPTH, D, wm), F32), grid=(DEPTH, D // tm),
        in_specs=[pl.BlockSpec((32, tm), lambda l, i: (0, i)), pl.BlockSpec((None, 32, wm), lambda l, i: (l, 0, 0))],
        out_specs=pl.BlockSpec((None, tm, wm), lambda l, i: (l, i, 0)),
        compiler_params=_cp(("parallel", "parallel")))(sc, dcol)


def cctx_dx(drow, w_mod, name):
    wm = w_mod.shape[-1]

    def body(a_ref, b_ref, o_ref):
        o_ref[...] = _nt(a_ref[...].astype(BF16), b_ref[...].astype(BF16))

    return pl.pallas_call(
        body, name=name, out_shape=_sds((DEPTH, 16, D), F32), grid=(DEPTH,),
        in_specs=[pl.BlockSpec((None, 16, wm), lambda l: (l, 0, 0)), pl.BlockSpec((None, D, wm), lambda l: (l, 0, 0))],
        out_specs=pl.BlockSpec((None, 16, D), lambda l: (l, 0, 0)), compiler_params=_cp(("parallel",), VMEM_BIG))(drow, w_mod)


HEAD_PERM = (0, 4, 1, 5, 2, 6, 3, 7)


def _rot_rows(wt):
    return jnp.concatenate([-wt[32:64], wt[0:32]], axis=0)


def _unrot_rows(g):
    return jnp.concatenate([g[32:64], -g[0:32]], axis=0)


def _heads(a, n):
    return [a[64 * i:64 * (i + 1)] for i in range(n)]


def kernel(x, c, ctx, c_ctx, w_mod, b_mod, ln_g, ln_b, ffn_w_gate, ffn_w_up, ffn_w_down, mix_ab_w_in, attn_sink, pool_w, pool_scale, mix_ab_w_out, lru_w_in, lru_conv_w, lru_conv_b, lru_wa, lru_ba, lru_wx, lru_bx, lru_lambda, lru_w_out, loss_target, m_c_ctx, m_w_mod, m_b_mod, m_ln_g, m_ln_b, m_ffn_w_gate, m_ffn_w_up, m_ffn_w_down, m_mix_ab_w_in, m_attn_sink, m_pool_w, m_pool_scale, m_mix_ab_w_out, m_lru_w_in, m_lru_conv_w, m_lru_conv_b, m_lru_wa, m_lru_ba, m_lru_wx, m_lru_bx, m_lru_lambda, m_lru_w_out, v_c_ctx, v_w_mod, v_b_mod, v_ln_g, v_ln_b, v_ffn_w_gate, v_ffn_w_up, v_ffn_w_down, v_mix_ab_w_in, v_attn_sink, v_pool_w, v_pool_scale, v_mix_ab_w_out, v_lru_w_in, v_lru_conv_w, v_lru_conv_b, v_lru_wa, v_lru_ba, v_lru_wx, v_lru_bx, v_lru_lambda, v_lru_w_out):
    n_lat, n_ctx = x.shape[1], ctx.shape[1]
    lay = Layout(n_ctx, n_lat)
    T = lay.T
    ns = ffn_w_gate.shape[-1]
    n_li, n_ai = lru_w_in.shape[-1], mix_ab_w_in.shape[-1]
    n_ao, n_lo = mix_ab_w_out.shape[1], lru_w_out.shape[1]
    wm = w_mod.shape[-1]
    dsh = ln_g.shape[-1]
    mx, my, mc = lax.axis_index("x"), lax.axis_index("y"), lax.axis_index("c")
    chip = 2 * mx + my
    me = 2 * chip + mc

    c_all = all_gather8(c, "ag8_c").reshape(16, D)
    cc = jnp.concatenate([c_all, c_ctx[None, :], jnp.zeros((15, D), F32)], axis=0)
    sc = silu_rows(cc, "silu_c")
    bias = lax.dynamic_slice(b_mod, (0, chip * wm), (DEPTH, wm)).reshape(DEPTH, 1, wm)
    modg = all_gather_chips(mod_mm(sc, w_mod, bias, "mod_mm"), "ag_mod")
    modtab = []
    for l in range(DEPTH):
        full = jnp.transpose(modg[:, l], (1, 0, 2)).reshape(32, N_CHIP * wm)
        mine = lax.dynamic_slice(full, (2 * me, 0), (2, N_CHIP * wm))
        modtab.append(jnp.concatenate([mine, full[16:17]], axis=0).reshape(3, N_MOD, D))

    small = jnp.concatenate([ln_g.reshape(6, dsh), ln_b.reshape(6, dsh), lru_conv_w[0], lru_conv_b, lru_ba[0],
                             lru_bx[0], lru_lambda[0], jnp.zeros((9, dsh), F32)], axis=0)
    small = all_gather_chips(small.reshape(2, 16, dsh), "ag_small").reshape(N_CHIP, 32, dsh)
    small = jnp.transpose(small, (1, 0, 2)).reshape(32, D)
    ln_g_f, ln_b_f = small[0:6].reshape(2, 3, D), small[6:12].reshape(2, 3, D)
    conv_w_f, conv_b_f = small[12:16], small[16:17]
    lru_vec = small[17:23]

    hh = 3 * ns // 2
    gate_t, up_t = jnp.swapaxes(ffn_w_gate, -1, -2), jnp.swapaxes(ffn_w_up, -1, -2)
    extra = [0, n_ai + n_ao, n_li + n_lo, 0]
    placed = [ffn_place(gate_t, up_t, ffn_w_down, g // 2, g % 2, f"ag_ffn{g}_place", extra[g]) for g in range(4)]
    placed[1] = place_rows(placed[1], jnp.concatenate([mix_ab_w_in[0].T, mix_ab_w_out[0]], axis=0).astype(BF16), 3 * ns,
                           "ag_mixa_place")
    placed[2] = place_rows(placed[2], jnp.concatenate([lru_w_in[0].T, lru_w_out[0]], axis=0).astype(BF16), 3 * ns,
                           "ag_mixc_place")
    placed = [p.reshape(N_CHIP, 2, p.shape[1] // 2, D) for p in placed]
    wb = [gather_placed(placed[0], "ag_ffn0"), None, None, None]
    mixw = {}

    def mixa_w():
        if "a" not in mixw:
            full = wb[1].reshape(N_CHIP, -1, D)
            ab_in_t = full[:, 3 * ns:3 * ns + n_ai].reshape(N_CHIP * n_ai, D)
            ab_out = full[:, 3 * ns + n_ai:].reshape(N_CHIP * n_ao, D)
            qh, kh = _heads(ab_in_t[Q0:K0], N_HEADS), _heads(ab_in_t[K0:V0], N_KV)
            w_ext_t = jnp.concatenate([qh[h] for h in HEAD_PERM] + [ab_in_t[K0:QR0]]
                                      + [_rot_rows(qh[h]) for h in HEAD_PERM] + [_rot_rows(t) for t in kh], axis=0)
            oh = _heads(ab_out[0:ATT_W], N_HEADS)
            mixw["a"] = (w_ext_t, jnp.concatenate([oh[h] for h in HEAD_PERM] + [ab_out[ATT_W:]], axis=0))
        return mixw["a"]

    def mixc_w():
        if "c" not in mixw:
            full = wb[2].reshape(N_CHIP, -1, D)
            mixw["c"] = (full[:, 3 * ns:3 * ns + n_li].reshape(N_CHIP * n_li, D),
                         full[:, 3 * ns + n_li:].reshape(N_CHIP * n_lo, D))
        return mixw["c"]

    t = jnp.arange(n_lat)
    inv = ROPE_THETA ** (-jnp.arange(16, dtype=F32) / 16.0)
    ang = jnp.concatenate([(t // GRID_W).astype(F32)[:, None] * inv, (t % GRID_W).astype(F32)[:, None] * inv], axis=-1)
    cos1 = jnp.concatenate([jnp.ones((n_ctx, 32), F32), jnp.cos(ang)], axis=0)
    sin1 = jnp.concatenate([jnp.zeros((n_ctx, 32), F32), jnp.sin(ang)], axis=0)
    cos_t = jnp.tile(cos1, (2, 4))
    sin_t = jnp.tile(sin1, (2, 4))
    sk = attn_sink[0]
    sink_tab = jnp.concatenate([jnp.repeat(jnp.stack([sk[:4], sk[4:]], axis=1), HEAD_DIM, axis=1),
                                jnp.zeros((4, 128), F32)], axis=0)
    pscale = pool_scale.reshape(1, POOL_W)

    h0 = jnp.concatenate([ctx, x], axis=1).reshape(T, D)
    tgt = loss_target.reshape(2 * n_lat, D)

    def lnv(l, j):
        return jnp.stack([ln_g_f[l, j], ln_b_f[l, j]])

    subs = [(0, 0, 0.5, 0), (0, 3, 1.0, 1), (0, 6, 0.5, 2), (1, 0, 0.5, 0), (1, 3, 1.0, 1), (1, 6, 0.5, 2)]

    def ffn_core(hm, l, f):
        tag = f"l{l}f{f}"
        gi = 2 * l + f
        w = wb[gi].reshape(N_CHIP, -1, D)
        if gi == 3:
            sp, sl, u, a = ffn_up(lay, hm, w, 0, 1, ns, f"ffn_up_{tag}")
            (y,) = slab_nn_acc(lay, [a], w, [2], ns, f"ffn_down_{tag}")
            return y, dict(sp=sp, sl=sl, u=u, a=a, nbuf=None)
        sp, sl, u, a, nbuf = ffn_up(lay, hm, w, 0, 1, ns, f"ffn_up_{tag}", rider=rider_gather_xy(placed[gi + 1]))
        y, nbuf = slab_nn_acc(lay, [a], w, [2], ns, f"ffn_down_{tag}", rider=rider_gather_fwd(nbuf))
        return y, dict(sp=sp, sl=sl, u=u, a=a, nbuf=nbuf)

    def mixa_core(hm):
        p = mm_nt(hm, mixa_w()[0], "mixa_in")
        qr, kr, vb, u = rope_fwd(lay, p, cos_t, sin_t, "rope")
        att, lse = attn_fwd(lay, qr, kr, vb, sink_tab, "attn")
        pool = pool_fwd(lay, u, pool_w[0], pscale, "pool")
        cat = jnp.concatenate([att, pool], axis=1)
        return mm_nn(cat, mixa_w()[1], "mixa_out"), dict(qr=qr, kr=kr, vb=vb, u=u, lse=lse, cat=cat)

    def mixc_core(hm):
        p = mm_nt(hm, mixc_w()[0], "mixc_in")
        uc = conv_fwd(lay, p, D, conv_w_f, conv_b_f, "conv")
        a, b = lru_coeffs(lay, uc, lru_wa[0], lru_wx[0], lru_vec, "lru_coef")
        s = lru_scan(lay, a, b, "lru_scan")
        o = lru_gate(lay, p, s, "lru_gate")
        return mm_nn(o, mixc_w()[1], "mixc_out"), dict(p=p, uc=uc, a=a, s=s, o=o)

    recs = []
    h = h0
    hm = modulate(lay, h0, modtab[0], 0, 1, "mod_first")
    for k, (l, k0, coef, j) in enumerate(subs):
        if k0 == 3:
            y, core = mixa_core(hm) if l == 0 else mixc_core(hm)
        else:
            y, core = ffn_core(hm, l, k0 // 6)
        nxt = None if k == 5 else (modtab[subs[k + 1][0]], subs[k + 1][1], subs[k + 1][1] + 1)
        nbuf = core.pop("nbuf", None)
        res = resid_ln(lay, h, y, modtab[l], k0 + 2, coef, lnv(l, j), f"ln_s{k}", nxt=nxt,
                       rider=None if nbuf is None else rider_gather_d2d(nbuf))
        if nbuf is not None:
            wb[2 * l + k0 // 6 + 1] = res[-1]
        recs.append(dict(h=h, hm=hm, y=y, xhat=res[1], rstd=res[2], **core))
        h = res[0]
        hm = res[3] if nxt is not None else None

    dout, lparts = loss_and_grad(lay, h, tgt, "loss")
    loss = lax.psum(jnp.sum(lparts), ("x", "y", "c"))

    dln = {}
    dms = {}
    mixg = {}
    ffn_red = [lax.empty((4, 2, hh, D), F32)]
    mix_red = {}
    pending = []

    def rs_sib(p):
        return None if p is None else rider_reduce_sib(p["buf"])

    def rs_add2(p, recv):
        p["q"] = add_own_half(p["buf"], recv, BF16, f"rs_add2_{p['key']}")

    def rs_join(p, arr):
        if isinstance(p["key"], int):
            return rider_join(sum_slots(p["q"], arr, f"rs_add4_{p['key']}", dst=ffn_red[0], g=p["key"]), p["key"])
        return rider_join(sum_slots(p["q"], arr, f"rs_add4_{p['key']}"))

    def rs_done(p, joined):
        if isinstance(p["key"], int):
            ffn_red[0] = joined
        else:
            mix_red[p["key"]] = joined.reshape(-1, D)

    def ffn_core_bwd(dy, r, l, f):
        tag = f"l{l}f{f}"
        gi = 2 * l + f
        w = wb[gi].reshape(N_CHIP, -1, D)
        p = pending.pop() if pending else None
        gb = lax.empty((N_CHIP, 3 * ns, D), F32)
        if p is None:
            dg, du = ffn_bwd_da(lay, dy, w, 2, r["sp"], r["sl"], r["u"], ns, f"ffn_da_{tag}")
            (gb,) = slab_tn(lay, r["a"], dy, gb, 2, ns, f"ffn_dwd_{tag}")
            (gb,) = slab_tn(lay, dg, r["hm"], gb, 0, ns, f"ffn_dwg_{tag}")
            (gb,) = slab_tn(lay, du, r["hm"], gb, 1, ns, f"ffn_dwu_{tag}")
            (dhm,) = slab_nn_acc(lay, [dg, du], w, [0, 1], ns, f"ffn_dh_{tag}")
        else:
            dg, du, recv = ffn_bwd_da(lay, dy, w, 2, r["sp"], r["sl"], r["u"], ns, f"ffn_da_{tag}", rider=rs_sib(p))
            rs_add2(p, recv)
            gb, arr = slab_tn(lay, r["a"], dy, gb, 2, ns, f"ffn_dwd_{tag}", rider=rider_reduce_copy(p["q"], 0))
            gb, arr = slab_tn(lay, dg, r["hm"], gb, 0, ns, f"ffn_dwg_{tag}", rider=rider_reduce_copy(p["q"], 1, arr))
            gb, arr = slab_tn(lay, du, r["hm"], gb, 1, ns, f"ffn_dwu_{tag}", rider=rider_reduce_copy(p["q"], 2, arr))
            dhm, joined = slab_nn_acc(lay, [dg, du], w, [0, 1], ns, f"ffn_dh_{tag}", rider=rs_join(p, arr))
            rs_done(p, joined)
        pending.append(dict(buf=gb.reshape(N_CHIP, 2, hh, D), key=gi))
        return dhm

    def mixc_core_bwd(dy, r):
        p = pending.pop() if pending else None
        w_in_t, w_out = mixc_w()
        if p is None:
            do_c = mm_nt(dy, w_out, "mixc_out_dx")
        else:
            do_c, recv = mm_nt(dy, w_out, "mixc_out_dx", rider=rs_sib(p))
            rs_add2(p, recv)
        g_out = mm_tn(r["o"], dy, "mixc_out_dw")
        dgate, dyg = lru_gate_bwd(lay, r["p"], r["s"], do_c, "lru_gate_b")
        da_c, db_c = lru_scan_bwd(lay, r["a"], r["s"], dyg, "lru_scan_b")
        res = lru_coeffs_bwd(lay, r["uc"], lru_wa[0], lru_wx[0], lru_vec, da_c, db_c, "lru_coef_b",
                             rider=None if p is None else rider_reduce_copies(p["q"]))
        duc, mixg["wa"], mixg["wx"], mixg["vec"] = res[:4]
        du_c, mixg["cw"], mixg["cb"] = conv_bwd(lay, r["p"], D, conv_w_f, duc, "conv_b")
        dp_c = jnp.concatenate([dgate, du_c], axis=1)
        if p is None:
            g_in_t = mm_tn(dp_c, r["hm"], "mixc_in_dw")
        else:
            g_in_t, joined = mm_tn(dp_c, r["hm"], "mixc_in_dw", rider=rs_join(p, res[4]))
            rs_done(p, joined)
        buf = jnp.concatenate([g_in_t.reshape(N_CHIP, n_li, D), g_out.reshape(N_CHIP, n_lo, D)], axis=1)
        pending.append(dict(buf=buf.reshape(N_CHIP, 2, (n_li + n_lo) // 2, D), key="c"))
        return mm_nn(dp_c, w_in_t, "mixc_in_dx")

    def mixa_core_bwd(dy, r):
        p = pending.pop() if pending else None
        w_ext_t, w_out_ext = mixa_w()
        if p is None:
            dcat = mm_nt(dy, w_out_ext, "mixa_out_dx")
        else:
            dcat, recv = mm_nt(dy, w_out_ext, "mixa_out_dx", rider=rs_sib(p))
            rs_add2(p, recv)
        g_out_ext = mm_tn(r["cat"], dy, "mixa_out_dw")
        res = attn_bwd(lay, r["qr"], r["kr"], r["vb"], sink_tab, r["lse"], dcat, "attn_b",
                       rider=None if p is None else rider_reduce_copies(p["q"]))
        dqr, dkr, dv, mixg["sink"] = res[:4]
        du_a, mixg["pw"], mixg["ps"] = pool_bwd(lay, r["u"], dcat, pool_w[0], pscale, "pool_b")
        dp_a = rope_bwd(lay, dqr, dkr, dv, du_a, cos_t, sin_t, "rope_b")
        if p is None:
            g_ext_t = mm_tn(dp_a, r["hm"], "mixa_in_dw")
        else:
            g_ext_t, joined = mm_tn(dp_a, r["hm"], "mixa_in_dw", rider=rs_join(p, res[4]))
            rs_done(p, joined)
        gq, gqr = _heads(g_ext_t[Q0:K0], N_HEADS), _heads(g_ext_t[QR0:KR0], N_HEADS)
        g_q = [None] * N_HEADS
        for i, h in enumerate(HEAD_PERM):
            g_q[h] = gq[i] + _unrot_rows(gqr[i])
        gk = [a + _unrot_rows(b) for a, b in zip(_heads(g_ext_t[K0:V0], N_KV), _heads(g_ext_t[KR0:PEXT], N_KV))]
        g_ab_in_t = jnp.concatenate(g_q + gk + [g_ext_t[V0:QR0]], axis=0)
        go = _heads(g_out_ext[0:ATT_W], N_HEADS)
        g_o = [None] * N_HEADS
        for i, h in enumerate(HEAD_PERM):
            g_o[h] = go[i]
        g_ab_out = jnp.concatenate(g_o + [g_out_ext[ATT_W:]], axis=0)
        buf = jnp.concatenate([g_ab_in_t.reshape(N_CHIP, n_ai, D), g_ab_out.reshape(N_CHIP, n_ao, D)], axis=1)
        pending.append(dict(buf=buf.reshape(N_CHIP, 2, (n_ai + n_ao) // 2, D), key="a"))
        return mm_nn(dp_a, w_ext_t, "mixa_in_dx")

    l, k0, coef, j = subs[5]
    dy, dres, s1 = ln_bwd(lay, dout, recs[5]["xhat"], recs[5]["rstd"], recs[5]["y"], modtab[l], k0 + 2, coef, lnv(l, j),
                          "lnb_s5")
    for k in range(5, -1, -1):
        l, k0, coef, j = subs[k]
        r = recs[k]
        if k0 == 3:
            dhm = mixa_core_bwd(dy, r) if l == 0 else mixc_core_bwd(dy, r)
        else:
            dhm = ffn_core_bwd(dy, r, l, k0 // 6)
        dln[(l, j)] = block_sums(lay, s1, f"bs_ln_s{k}")
        if k > 0:
            lp, k0p, coefp, jp = subs[k - 1]
            rp = recs[k - 1]
            dy, dres, s1, s2 = modb_lnb(lay, dres, dhm, r["h"], modtab[l], k0 + 1, rp["xhat"], rp["rstd"], rp["y"],
                                        modtab[lp], k0p + 2, coefp, lnv(lp, jp), f"modb_lnb_s{k}")
        else:
            gx, s2 = mod_bwd(lay, dres, dhm, r["h"], modtab[l], k0 + 1, "modb_s0")
        dms[(l, k0)] = block_sums(lay, s2, f"bs_mod_s{k}")
    grad_x = gx.reshape(2, n_lat, D)
    g_wa, g_wx, g_vec, g_cw, g_cb, g_sink, g_pw, g_ps = (mixg[n] for n in ("wa", "wx", "vec", "cw", "cb", "sink", "pw", "ps"))

    rows = []
    for l in range(DEPTH):
        per_k = []
        for k0, j in ((0, 0), (3, 1), (6, 2)):
            per_k += [dms[(l, k0)][:3, 0], dms[(l, k0)][:3, 1], dln[(l, j)][:3, 2]]
        rows.append(jnp.stack(per_k, axis=1).reshape(3, N_MOD * D))
    dmod_loc = jnp.concatenate(rows + [jnp.zeros((2, N_MOD * D), F32)], axis=0)
    dmod_all, g_b_mod = mod_grad_rows(all_gather8(dmod_loc, "ag8_dmod"), "dmod_rows")
    dcol = lax.dynamic_slice(dmod_all, (0, 0, chip * wm), (DEPTH, 32, wm))
    g_w_mod = wmod_dw(sc, dcol, "wmod_dw")
    g_cctx = cctx_grad(cctx_dx(dcol[:, 16:32], w_mod, "cctx_dx"), c_ctx[None, :], "cctx_grad")

    g_ln_g =jnp.stack([jnp.stack([dln[(l, j)][3, 1] for j in range(3)]) for l in range(DEPTH)])
    g_ln_b = jnp.stack([jnp.stack([dln[(l, j)][3, 0] for j in range(3)]) for l in range(DEPTH)])
    sink_row = jnp.sum(g_sink, axis=0)[:4]
    g_sink8 = jnp.concatenate([sink_row[:, 0], sink_row[:, HEAD_DIM]])
    misc = jnp.concatenate([g_sink8, jnp.sum(g_ps, axis=0).reshape(POOL_W), jnp.zeros((D - 8 - POOL_W,), F32)])
    small_g = jnp.concatenate([
        g_ln_g.reshape(6, D), g_ln_b.reshape(6, D), jnp.sum(g_cw, axis=0), jnp.sum(g_cb, axis=0), g_vec,
        misc[None, :], jnp.sum(g_pw, axis=0).reshape(64, D), g_wa.reshape(256, D), g_wx.reshape(256, D), g_cctx,
        jnp.zeros((39, D), F32)], axis=0)
    n_small = small_g.shape[0] // N_CHIP
    last = pending.pop()
    ffn_red = reduce_scatter_chips(last["buf"], f"ffn{last['key']}", wire=BF16, dst=ffn_red[0],
                                   g=last["key"]).reshape(12 * ns, D)
    small_red = reduce_scatter_chips(small_g.reshape(N_CHIP, 2, n_small // 2, D), "small")
    small_red = all_gather_chips(small_red, "ag_smallg").reshape(N_CHIP * n_small, D)

    ffn_kind = dict(ffn_w_gate=0, ffn_w_up=1, ffn_w_down=2)

    def cols(a):
        return lax.dynamic_slice_in_dim(a, chip * dsh, dsh, axis=a.ndim - 1)

    sr = small_red
    grads = dict(
        c_ctx=sr[600], w_mod=g_w_mod, b_mod=g_b_mod,
        ln_g=cols(sr[0:6]).reshape(2, 3, dsh), ln_b=cols(sr[6:12]).reshape(2, 3, dsh),
        mix_ab_w_in=mix_red["a"][0:n_ai][None], attn_sink=sr[23, 0:8][None], pool_w=sr[24:88].reshape(1, 4, 128, 128),
        pool_scale=sr[23, 8:8 + POOL_W][None], mix_ab_w_out=mix_red["a"][n_ai:][None],
        lru_w_in=mix_red["c"][0:n_li].T[None],
        lru_conv_w=cols(sr[12:16])[None], lru_conv_b=cols(sr[16:17]), lru_wa=sr[88:344].reshape(1, 2, 8, 128, 128),
        lru_ba=cols(sr[17:19])[None], lru_wx=sr[344:600].reshape(1, 2, 8, 128, 128), lru_bx=cols(sr[19:21])[None],
        lru_lambda=cols(sr[21:23])[None], lru_w_out=mix_red["c"][n_li:][None])
    params = dict(c_ctx=(c_ctx, m_c_ctx, v_c_ctx), w_mod=(w_mod, m_w_mod, v_w_mod), b_mod=(b_mod, m_b_mod, v_b_mod),
                  ln_g=(ln_g, m_ln_g, v_ln_g), ln_b=(ln_b, m_ln_b, v_ln_b),
                  ffn_w_gate=(ffn_w_gate, m_ffn_w_gate, v_ffn_w_gate), ffn_w_up=(ffn_w_up, m_ffn_w_up, v_ffn_w_up),
                  ffn_w_down=(ffn_w_down, m_ffn_w_down, v_ffn_w_down),
                  mix_ab_w_in=(mix_ab_w_in, m_mix_ab_w_in, v_mix_ab_w_in), attn_sink=(attn_sink, m_attn_sink, v_attn_sink),
                  pool_w=(pool_w, m_pool_w, v_pool_w), pool_scale=(pool_scale, m_pool_scale, v_pool_scale),
                  mix_ab_w_out=(mix_ab_w_out, m_mix_ab_w_out, v_mix_ab_w_out), lru_w_in=(lru_w_in, m_lru_w_in, v_lru_w_in),
                  lru_conv_w=(lru_conv_w, m_lru_conv_w, v_lru_conv_w), lru_conv_b=(lru_conv_b, m_lru_conv_b, v_lru_conv_b),
                  lru_wa=(lru_wa, m_lru_wa, v_lru_wa), lru_ba=(lru_ba, m_lru_ba, v_lru_ba), lru_wx=(lru_wx, m_lru_wx, v_lru_wx),
                  lru_bx=(lru_bx, m_lru_bx, v_lru_bx), lru_lambda=(lru_lambda, m_lru_lambda, v_lru_lambda),
                  lru_w_out=(lru_w_out, m_lru_w_out, v_lru_w_out))
    gl, dl, ml, vl = [], [], [], []
    transposed = ("ffn_w_gate", "ffn_w_up", "mix_ab_w_in")
    for name, (w, m, v) in params.items():
        if name in transposed:
            w, m, v = (jnp.swapaxes(t, -1, -2) for t in (w, m, v))
        if name in ffn_kind:
            g, d, mn, vn = adamw_ffn(w, m, v, ffn_red, ffn_kind[name], ns, f"adamw_{name}")
        else:
            g = grads[name].reshape(w.shape)
            d, mn, vn = adamw(w, g, m, v, f"adamw_{name}")
        if name in transposed:
            g, d, mn, vn = (jnp.swapaxes(t, -1, -2) for t in (g, d, mn, vn))
        gl.append(g)
        dl.append(d)
        ml.append(mn)
        vl.append(vn)
    return (loss, grad_x, *gl, *dl, *ml, *vl)
```

```python
import functools
import math

import jax
import jax.numpy as jnp
from jax import lax
from jax.experimental import pallas as pl
from jax.experimental.pallas import tpu as pltpu

F32, BF16 = jnp.float32, jnp.bfloat16
MESH = pl.DeviceIdType.MESH
ANY = pl.BlockSpec(memory_space=pl.ANY)
VMEM_SPEC = pl.BlockSpec(memory_space=pltpu.VMEM)

D = 1024
N_CHIP = 4
HEAD_DIM, N_HEADS, N_KV = 64, 8, 2
ATT_W, KV_W, POOL_W = 512, 128, 512
POOL_WINDOWS = (2, 4, 8, 16)
BLK = 128
ATT_SCALE = HEAD_DIM ** -0.5
ROPE_THETA = 10000.0
GRID_W = 64
LRU_C = 8.0
LN_EPS = 1e-5
NEG_INF = -1e30
DEPTH = 2
ALPHA = (2 * DEPTH) ** 0.25
N_MOD = 9
ADAM_LR, ADAM_B1, ADAM_B2, ADAM_EPS, ADAM_WD, ADAM_STEP = 0.001, 0.9, 0.999, 1e-08, 0.01, 10
VMEM_BIG = 48 * 1024 * 1024


def _cp(sem=None, vmem=None):
    kw = {}
    if sem is not None:
        kw["dimension_semantics"] = sem
    if vmem is not None:
        kw["vmem_limit_bytes"] = vmem
    return pltpu.CompilerParams(**kw)


def _sds(shape, dtype):
    return jax.ShapeDtypeStruct(tuple(shape), dtype)


def _pick(n, cands):
    for c in cands:
        if n % c == 0:
            return c
    return n


def _dot(a, b, dims):
    return lax.dot_general(a, b, (dims, ((), ())), preferred_element_type=F32)


def _nn(a, b):
    return _dot(a, b, ((1,), (0,)))


def _nt(a, b):
    return _dot(a, b, ((1,), (1,)))


def _tn(a, b):
    return _dot(a, b, ((0,), (0,)))


def _sigmoid(x):
    return 0.5 * jnp.tanh(0.5 * x) + 0.5


def _me():
    return lax.axis_index("x"), lax.axis_index("y"), lax.axis_index("c")


def _rcopy(src, dst, ssem, rsem, dev):
    return pltpu.make_async_remote_copy(src_ref=src, dst_ref=dst, send_sem=ssem, recv_sem=rsem,
                                        device_id=dev, device_id_type=MESH)


def all_gather8(x, name):
    def body(x_ref, o_ref, ssem, rsem, lsem):
        mx, my, mc = _me()
        me = 4 * mx + 2 * my + mc
        loc = pltpu.make_async_copy(x_ref, o_ref.at[me], lsem)
        loc.start()
        peers = []
        for m in range(1, 8):
            px = 1 - mx if (m >> 2) & 1 else mx
            py = 1 - my if (m >> 1) & 1 else my
            pc = 1 - mc if m & 1 else mc
            peers.append((px, py, pc))
        sends = [_rcopy(x_ref, o_ref.at[me], ssem.at[k], rsem.at[k], p) for k, p in enumerate(peers)]
        for cp in sends:
            cp.start()
        for k, (px, py, pc) in enumerate(peers):
            _rcopy(x_ref, o_ref.at[4 * px + 2 * py + pc], ssem.at[k], rsem.at[k], (px, py, pc)).wait_recv()
        for cp in sends:
            cp.wait_send()
        loc.wait()

    return pl.pallas_call(
        body, name=name, out_shape=_sds((8,) + x.shape, x.dtype),
        in_specs=[VMEM_SPEC], out_specs=VMEM_SPEC,
        scratch_shapes=[pltpu.SemaphoreType.DMA((7,)), pltpu.SemaphoreType.DMA((7,)), pltpu.SemaphoreType.DMA],
    )(x)


_ROW_BLOCKS = (512, 384, 352, 256, 224, 128)


def _idx(v):
    return jnp.reshape(v, (1,)).astype(jnp.int32)


def place_slab(shard, name):
    _, h, w = shard.shape
    th = _pick(h, _ROW_BLOCKS)

    def body(s_ref, x_ref, o_ref):
        del s_ref
        o_ref[...] = x_ref[...]

    return pl.pallas_call(
        body, name=name, out_shape=_sds((N_CHIP,) + shard.shape, shard.dtype),
        grid_spec=pltpu.PrefetchScalarGridSpec(
            num_scalar_prefetch=1, grid=(2, h // th),
            in_specs=[pl.BlockSpec((None, th, w), lambda k, r, s: (k, r, 0))],
            out_specs=pl.BlockSpec((None, None, th, w), lambda k, r, s: (s[0], k, r, 0))),
    )(_idx(2 * lax.axis_index("x") + lax.axis_index("y")), shard)


def place_rows(buf, rows, r0, name):
    e, w = rows.shape
    tb = 64

    def body(s_ref, x_ref, b_ref, o_ref):
        del s_ref, b_ref
        o_ref[...] = x_ref[...]

    return pl.pallas_call(
        body, name=name, out_shape=_sds(buf.shape, buf.dtype),
        grid_spec=pltpu.PrefetchScalarGridSpec(
            num_scalar_prefetch=1, grid=(e // tb,),
            in_specs=[pl.BlockSpec((tb, w), lambda j, s: (j, 0)), ANY],
            out_specs=pl.BlockSpec((None, tb, w), lambda j, s: (s[0], r0 // tb + j, 0))),
        input_output_aliases={2: 0},
    )(_idx(2 * lax.axis_index("x") + lax.axis_index("y")), rows, buf)


def ffn_place(w_gate_t, w_up_t, w_down, l, f, name, extra=0):
    ns = w_down.shape[-2]
    tc = 256

    def body(s_ref, g_ref, u_ref, d_ref, o_ref):
        del s_ref
        k = pl.program_id(0)

        @pl.when(k == 0)
        def _():
            o_ref[...] = g_ref[...].astype(BF16)

        @pl.when(k == 1)
        def _():
            o_ref[...] = u_ref[...].astype(BF16)

        @pl.when(k == 2)
        def _():
            o_ref[...] = d_ref[...].astype(BF16)

    spec = pl.BlockSpec((None, None, ns, tc), lambda k, j, s: (l, f, 0, j))
    return pl.pallas_call(
        body, name=name, out_shape=_sds((N_CHIP, 3 * ns + extra, D), BF16),
        grid_spec=pltpu.PrefetchScalarGridSpec(
            num_scalar_prefetch=1, grid=(3, D // tc), in_specs=[spec, spec, spec],
            out_specs=pl.BlockSpec((None, ns, tc), lambda k, j, s: (s[0], k, j))),
    )(_idx(2 * lax.axis_index("x") + lax.axis_index("y")), w_gate_t, w_up_t, w_down)


def all_gather_chips(shard, name):
    return gather_placed(place_slab(shard, name + "_place"), name)


def gather_placed(full, name):
    def body(x_ref, o_ref, ssem, rsem):
        del x_ref
        mx, my, mc = _me()
        s = 2 * mx + my
        sib = (mx, my, 1 - mc)
        chips = [(1 - mx, my), (mx, 1 - my), (1 - mx, 1 - my)]
        first = [_rcopy(o_ref.at[s, mc], o_ref.at[s, mc], ssem.at[j], rsem.at[j], (px, py, mc))
                 for j, (px, py) in enumerate(chips)]
        for cp in first:
            cp.start()
        passed = []
        for j, (px, py) in enumerate(chips):
            ps = 2 * px + py
            _rcopy(o_ref.at[ps, mc], o_ref.at[ps, mc], ssem.at[j], rsem.at[j], (px, py, mc)).wait_recv()
            fw = _rcopy(o_ref.at[ps, mc], o_ref.at[ps, mc], ssem.at[3 + j], rsem.at[3 + j], sib)
            fw.start()
            passed.append(fw)
        for j, (px, py) in enumerate(chips):
            ps = 2 * px + py
            _rcopy(o_ref.at[ps, 1 - mc], o_ref.at[ps, 1 - mc], ssem.at[3 + j], rsem.at[3 + j], sib).wait_recv()
        for cp in first + passed:
            cp.wait_send()

    return pl.pallas_call(
        body, name=name, out_shape=_sds(full.shape, full.dtype), in_specs=[ANY], out_specs=ANY,
        input_output_aliases={0: 0},
        scratch_shapes=[pltpu.SemaphoreType.DMA((6,)), pltpu.SemaphoreType.DMA((6,))],
    )(full)


def sibling_send_other_half(buf, name):
    def body(x_ref, o_ref, ssem, rsem):
        mx, my, mc = _me()
        sib = (mx, my, 1 - mc)
        cps = [_rcopy(x_ref.at[k, 1 - mc], o_ref.at[k], ssem.at[k], rsem.at[k], sib) for k in range(N_CHIP)]
        for cp in cps:
            cp.start()
        for cp in cps:
            cp.wait_recv()
        for cp in cps:
            cp.wait_send()

    n, _, h, w = buf.shape
    return pl.pallas_call(
        body, name=name, out_shape=_sds((n, h, w), buf.dtype), in_specs=[ANY], out_specs=ANY,
        scratch_shapes=[pltpu.SemaphoreType.DMA((N_CHIP,)), pltpu.SemaphoreType.DMA((N_CHIP,))],
    )(buf)


def chips_all_to_all(q, name):
    def body(x_ref, o_ref, ssem, rsem):
        mx, my, mc = _me()
        s = 2 * mx + my
        chips = [(1 - mx, my), (mx, 1 - my), (1 - mx, 1 - my)]
        cps = [_rcopy(x_ref.at[2 * px + py], o_ref.at[s], ssem.at[j], rsem.at[j], (px, py, mc))
               for j, (px, py) in enumerate(chips)]
        for cp in cps:
            cp.start()
        for j, (px, py) in enumerate(chips):
            ps = 2 * px + py
            _rcopy(x_ref.at[ps], o_ref.at[ps], ssem.at[j], rsem.at[j], (px, py, mc)).wait_recv()
        for cp in cps:
            cp.wait_send()

    return pl.pallas_call(
        body, name=name, out_shape=_sds(q.shape, q.dtype), in_specs=[ANY], out_specs=ANY,
        scratch_shapes=[pltpu.SemaphoreType.DMA((3,)), pltpu.SemaphoreType.DMA((3,))],
    )(q)


def sibling_join_halves(both, name, g=None):
    def body(x_ref, o_ref, ssem, rsem):
        del x_ref
        mx, my, mc = _me()
        sib = (mx, my, 1 - mc)
        o = o_ref if g is None else o_ref.at[g]
        cp = _rcopy(o.at[mc], o.at[mc], ssem, rsem, sib)
        cp.start()
        _rcopy(o.at[1 - mc], o.at[1 - mc], ssem, rsem, sib).wait_recv()
        cp.wait_send()

    return pl.pallas_call(
        body, name=name, out_shape=_sds(both.shape, both.dtype), in_specs=[ANY], out_specs=ANY,
        input_output_aliases={0: 0}, scratch_shapes=[pltpu.SemaphoreType.DMA, pltpu.SemaphoreType.DMA],
    )(both)


def add_own_half(buf, recv, wire, name):
    n, _, h, w = buf.shape
    th = _pick(h, _ROW_BLOCKS)

    def body(c_ref, a_ref, b_ref, o_ref):
        del c_ref
        o_ref[...] = (a_ref[...] + b_ref[...]).astype(o_ref.dtype)

    return pl.pallas_call(
        body, name=name, out_shape=_sds((n, h, w), wire),
        grid_spec=pltpu.PrefetchScalarGridSpec(
            num_scalar_prefetch=1, grid=(n, h // th),
            in_specs=[pl.BlockSpec((None, None, th, w), lambda k, r, c: (k, c[0], r, 0)),
                      pl.BlockSpec((None, th, w), lambda k, r, c: (k, r, 0))],
            out_specs=pl.BlockSpec((None, th, w), lambda k, r, c: (k, r, 0))),
    )(_idx(lax.axis_index("c")), buf, recv)


def sum_slots(q, r, name, dst=None, g=None):
    n, h, w = r.shape
    th = _pick(h, _ROW_BLOCKS)

    def body(i_ref, q_ref, r1, r2, r3, *rest):
        del i_ref
        rest[-1][...] = ((q_ref[...].astype(F32) + r1[...].astype(F32)) + r2[...].astype(F32)) + r3[...].astype(F32)

    def slot(d):
        return lambda i, ix: ((ix[0] + d) % N_CHIP, i, 0)

    idx = jnp.stack([2 * lax.axis_index("x") + lax.axis_index("y"), lax.axis_index("c")]).astype(jnp.int32)
    in_specs = [pl.BlockSpec((None, th, w), slot(d)) for d in (0, 1, 2, 3)]
    if dst is None:
        return pl.pallas_call(
            body, name=name, out_shape=_sds((2, h, w), F32),
            grid_spec=pltpu.PrefetchScalarGridSpec(
                num_scalar_prefetch=1, grid=(h // th,), in_specs=in_specs,
                out_specs=pl.BlockSpec((None, th, w), lambda i, ix: (ix[1], i, 0))),
        )(idx, q, r, r, r)
    return pl.pallas_call(
        body, name=name, out_shape=_sds(dst.shape, F32),
        grid_spec=pltpu.PrefetchScalarGridSpec(
            num_scalar_prefetch=1, grid=(h // th,), in_specs=in_specs + [ANY],
            out_specs=pl.BlockSpec((None, None, th, w), lambda i, ix: (g, ix[1], i, 0))),
        input_output_aliases={5: 0},
    )(idx, q, r, r, r, dst)


def reduce_scatter_chips(buf, tag, wire=F32, dst=None, g=None):
    recv = sibling_send_other_half(buf, f"rs_sib_{tag}")
    q = add_own_half(buf, recv, wire, f"rs_add2_{tag}")
    r = chips_all_to_all(q, f"rs_a2a_{tag}")
    red = sum_slots(q, r, f"rs_add4_{tag}", dst=dst, g=g)
    return sibling_join_halves(red, f"rs_join_{tag}", g=g)


class Layout:
    def __init__(self, n_ctx, n_lat):
        self.C, self.L = n_ctx, n_lat
        self.PS = n_ctx + n_lat
        self.T = 2 * self.PS
        self.tr = _pick(math.gcd(n_ctx, n_lat), (256, 128))
        self.bps = self.PS // self.tr
        self.cb = n_ctx // self.tr
        self.nblk = self.T // self.tr
        self.tm = _pick(self.T, (1152, 768, 512, 256, 128))
        self.tc = _pick(self.T, (512, 256, 128))

    def seg(self, i):
        return jnp.where(i % self.bps < self.cb, 2, i // self.bps)


def rowwise(lay, name, fn, rows, segs=(), vecs=(), outs=(), sums=(), rider=None):
    tr, nblk = lay.tr, lay.nblk
    n_r, n_s, n_v, n_o = len(rows), len(segs), len(vecs), len(outs)
    lat_only = any(o[2:] for o in outs) or any(a.shape[0] != lay.T for a in rows)
    nsub = 1 if lat_only or nblk % 2 else 2
    tb = tr * nsub

    def body(*refs):
        ins = refs[:n_r + n_s + n_v]
        ors = refs[n_r + n_s + n_v:]
        for sub in range(nsub):
            rs = slice(sub * tr, (sub + 1) * tr)
            seg = lay.seg(pl.program_id(0) * nsub + sub)
            vals = [r[rs, :] for r in ins[:n_r]] + [r[seg] for r in ins[n_r:n_r + n_s]] + [r[...] for r in ins[n_r + n_s:]]
            res = fn(*vals)
            for k in range(n_o):
                ors[k][rs, :] = res[k].astype(ors[k].dtype)
            for k in range(len(sums)):
                ors[n_o + k][sub] = res[n_o + k]

    def all_rows(i):
        return (i, 0)

    def lat_rows(i):
        return ((i // lay.bps) * (lay.bps - lay.cb) + jnp.maximum(i % lay.bps - lay.cb, 0), 0)

    in_specs = [pl.BlockSpec((tb, a.shape[1]), all_rows if a.shape[0] == lay.T else lat_rows) for a in rows]
    in_specs += [pl.BlockSpec(a.shape, lambda i: (0, 0, 0)) for a in segs]
    in_specs += [pl.BlockSpec(a.shape, lambda i: (0, 0)) for a in vecs]
    out_shape = [_sds((2 * lay.L if o[2:] else lay.T, o[0]), o[1]) for o in outs]
    out_shape += [_sds((nblk, r, w), F32) for r, w in sums]
    out_specs = [pl.BlockSpec((tb, o[0]), lat_rows if o[2:] else all_rows) for o in outs]
    out_specs += [pl.BlockSpec((nsub, r, w), lambda i: (i, 0, 0)) for r, w in sums]
    sem = "arbitrary" if any(o[2:] for o in outs) else "parallel"
    if rider is None:
        return pl.pallas_call(body, name=name, out_shape=out_shape, grid=(nblk // nsub,), in_specs=in_specs,
                              out_specs=out_specs, compiler_params=_cp((sem,), VMEM_BIG))(*rows, *segs, *vecs)
    return _host_call(body, rider, name, (nblk // nsub,), in_specs, out_specs, out_shape, (*rows, *segs, *vecs), (sem,),
                      n_r + n_s + n_v, n_o + len(sums))


def modulate(lay, h, mod, k_shift, k_scale, name):
    def fn(hb, m):
        return (hb * (1.0 + m[k_scale:k_scale + 1]) + m[k_shift:k_shift + 1],)
    return rowwise(lay, name, fn, [h], segs=[mod], outs=[(D, BF16)])[0]


def resid_ln(lay, h, y, mod, k_gate, coef, lnv, name, nxt=None, rider=None):
    def fn(hb, yb, m, *rest):
        ln = rest[-1]
        z = ALPHA * hb + (coef * m[k_gate:k_gate + 1]) * yb
        mu = jnp.mean(z, axis=-1, keepdims=True)
        zc = z - mu
        var = jnp.mean(zc * zc, axis=-1, keepdims=True)
        rstd = lax.rsqrt(var + LN_EPS)
        xhat = zc * rstd
        out = xhat * ln[0:1] + ln[1:2]
        if nxt is None:
            return out, xhat, rstd
        mn = rest[0]
        return out, xhat, rstd, out * (1.0 + mn[nxt[2]:nxt[2] + 1]) + mn[nxt[1]:nxt[1] + 1]
    segs = [mod] if nxt is None else [mod, nxt[0]]
    outs = [(D, F32), (D, F32), (1, F32)] + ([] if nxt is None else [(D, BF16)])
    return rowwise(lay, name, fn, [h, y], segs=segs, vecs=[lnv], outs=outs, rider=rider)


def _ln_bwd_math(do, xh, rs, yb, gate, coef, ln):
    dxh = do * ln[0:1]
    m1 = jnp.mean(dxh, axis=-1, keepdims=True)
    m2 = jnp.mean(dxh * xh, axis=-1, keepdims=True)
    dz = rs * (dxh - m1 - xh * m2)
    s = jnp.concatenate([jnp.sum(do, axis=0, keepdims=True), jnp.sum(do * xh, axis=0, keepdims=True),
                         jnp.sum(coef * dz * yb, axis=0, keepdims=True)], axis=0)
    return (coef * gate) * dz, ALPHA * dz, s


def _mod_bwd_math(dr, dm, hb, scale):
    s = jnp.concatenate([jnp.sum(dm, axis=0, keepdims=True), jnp.sum(dm * hb, axis=0, keepdims=True)], axis=0)
    return dr + dm * (1.0 + scale), s


def ln_bwd(lay, dout, xhat, rstd, y, mod, k_gate, coef, lnv, name):
    def fn(do, xh, rs, yb, m, ln):
        return _ln_bwd_math(do, xh, rs, yb, m[k_gate:k_gate + 1], coef, ln)
    return rowwise(lay, name, fn, [dout, xhat, rstd, y], segs=[mod], vecs=[lnv],
                   outs=[(D, BF16), (D, F32)], sums=[(3, D)])


def mod_bwd(lay, dres, dhm, h, mod, k_scale, name, rider=None):
    def fn(dr, dm, hb, m):
        return _mod_bwd_math(dr, dm, hb, m[k_scale:k_scale + 1])
    return rowwise(lay, name, fn, [dres, dhm, h], segs=[mod], outs=[(D, F32, "lat")], sums=[(2, D)], rider=rider)


def modb_lnb(lay, dres, dhm, h, mod, k_scale, xhat, rstd, y, mod_p, k_gate, coef, lnv, name, rider=None):
    def fn(dr, dm, hb, xh, rs, yb, m, mp, ln):
        dh, s2 = _mod_bwd_math(dr, dm, hb, m[k_scale:k_scale + 1])
        dy, dres_p, s1 = _ln_bwd_math(dh, xh, rs, yb, mp[k_gate:k_gate + 1], coef, ln)
        return dy, dres_p, s1, s2
    return rowwise(lay, name, fn, [dres, dhm, h, xhat, rstd, y], segs=[mod, mod_p], vecs=[lnv],
                   outs=[(D, BF16), (D, F32)], sums=[(3, D), (2, D)], rider=rider)


def block_sums(lay, parts, name):
    nblk, r, w = parts.shape

    def body(p_ref, o_ref):
        acc = [None, None, None]
        for i in range(nblk):
            sg = 2 if i % lay.bps < lay.cb else i // lay.bps
            acc[sg] = p_ref[i] if acc[sg] is None else acc[sg] + p_ref[i]
        for k in range(3):
            o_ref[k] = acc[k]
        o_ref[3] = (acc[0] + acc[1]) + acc[2]

    return pl.pallas_call(body, name=name, out_shape=_sds((4, r, w), F32), in_specs=[VMEM_SPEC],
                          out_specs=VMEM_SPEC)(parts)


def mm_nn(a, b, name, out_dtype=F32, bias=None):
    m, k = a.shape
    n = b.shape[1]
    tm = _pick(m, (1152, 768, 512, 256, 128, 64, 32, 16, 8))
    tn = _pick(n, (1024, 768, 640, 512, 384, 256, 128))

    def body(*refs):
        if bias is None:
            a_ref, b_ref, o_ref = refs
            o_ref[...] = _nn(a_ref[...].astype(BF16), b_ref[...].astype(BF16)).astype(o_ref.dtype)
        else:
            a_ref, b_ref, c_ref, o_ref = refs
            o_ref[...] = (_nn(a_ref[...].astype(BF16), b_ref[...].astype(BF16)) + c_ref[...]).astype(o_ref.dtype)

    in_specs = [pl.BlockSpec((tm, k), lambda i, j: (i, 0)), pl.BlockSpec((k, tn), lambda i, j: (0, j))]
    ops = [a, b]
    if bias is not None:
        in_specs.append(pl.BlockSpec((1, tn), lambda i, j: (0, j)))
        ops.append(bias)
    return pl.pallas_call(body, name=name, out_shape=_sds((m, n), out_dtype), grid=(m // tm, n // tn),
                          in_specs=in_specs, out_specs=pl.BlockSpec((tm, tn), lambda i, j: (i, j)),
                          compiler_params=_cp(("parallel", "parallel"), VMEM_BIG))(*ops)


def mm_nt(a, b, name, out_dtype=F32, rider=None):
    m, k = a.shape
    n = b.shape[0]
    tm = _pick(m, (1152, 768, 512, 256, 128, 64, 32, 16, 8))
    tn = _pick(n, (1024, 768, 640, 512, 384, 256, 128))

    def body(a_ref, b_ref, o_ref):
        o_ref[...] = _nt(a_ref[...].astype(BF16), b_ref[...].astype(BF16)).astype(o_ref.dtype)

    res = _host_call(body, rider, name, (m // tm, n // tn),
                     [pl.BlockSpec((tm, k), lambda i, j: (i, 0)), pl.BlockSpec((tn, k), lambda i, j: (j, 0))],
                     [pl.BlockSpec((tm, tn), lambda i, j: (i, j))], [_sds((m, n), out_dtype)], (a, b),
                     ("parallel", "parallel"), 2, 1)
    return res[0] if rider is None else res


def mm_tn(a, b, name, rider=None):
    t, m = a.shape
    n = b.shape[1]
    tk = _pick(t, (1152, 768, 512, 256, 128, 64, 32, 16))
    tm = _pick(m, (512, 384, 256, 128))

    def body(a_ref, b_ref, o_ref):
        @pl.when(pl.program_id(1) == 0)
        def _():
            o_ref[...] = jnp.zeros_like(o_ref)
        o_ref[...] += _tn(a_ref[...].astype(BF16), b_ref[...].astype(BF16))

    res = _host_call(body, rider, name, (m // tm, t // tk),
                     [pl.BlockSpec((tk, tm), lambda i, k: (k, i)), pl.BlockSpec((tk, n), lambda i, k: (k, 0))],
                     [pl.BlockSpec((tm, n), lambda i, k: (i, 0))], [_sds((m, n), F32)], (a, b),
                     ("parallel", "arbitrary"), 2, 1)
    return res[0] if rider is None else res


class Rider:
    def __init__(self, ins, outs, aliases, nsem, start, wait):
        self.ins, self.outs, self.aliases, self.nsem, self.start, self.wait = ins, outs, aliases, nsem, start, wait


def _chips_of(mx, my):
    return [(1 - mx, my), (mx, 1 - my), (1 - mx, 1 - my)]


def rider_gather_d2d(buf):
    def start(ins, outs, ssem, rsem):
        o = outs[0]
        mx, my, mc = _me()
        for j, (px, py) in enumerate(_chips_of(mx, my)):
            ps = 2 * px + py
            _rcopy(o.at[ps, mc], o.at[ps, mc], ssem.at[j], rsem.at[j], (mx, my, 1 - mc)).start()

    def wait(ins, outs, ssem, rsem):
        o = outs[0]
        mx, my, mc = _me()
        sib = (mx, my, 1 - mc)
        for j, (px, py) in enumerate(_chips_of(mx, my)):
            ps = 2 * px + py
            _rcopy(o.at[ps, 1 - mc], o.at[ps, 1 - mc], ssem.at[j], rsem.at[j], sib).wait_recv()
        for j, (px, py) in enumerate(_chips_of(mx, my)):
            ps = 2 * px + py
            _rcopy(o.at[ps, mc], o.at[ps, mc], ssem.at[j], rsem.at[j], sib).wait_send()

    return Rider([buf], [_sds(buf.shape, buf.dtype)], {0: 0}, 3, start, wait)


def rider_reduce_sib(buf):
    n, _, h, w = buf.shape

    def start(ins, outs, ssem, rsem):
        mx, my, mc = _me()
        for k in range(N_CHIP):
            _rcopy(ins[0].at[k, 1 - mc], outs[0].at[k], ssem.at[k], rsem.at[k], (mx, my, 1 - mc)).start()

    def wait(ins, outs, ssem, rsem):
        mx, my, mc = _me()
        for k in range(N_CHIP):
            _rcopy(ins[0].at[k, 1 - mc], outs[0].at[k], ssem.at[k], rsem.at[k], (mx, my, 1 - mc)).wait_recv()
        for k in range(N_CHIP):
            _rcopy(ins[0].at[k, 1 - mc], outs[0].at[k], ssem.at[k], rsem.at[k], (mx, my, 1 - mc)).wait_send()

    return Rider([buf], [_sds((n, h, w), buf.dtype)], {}, N_CHIP, start, wait)


def rider_gather_xy(buf):
    def peers():
        mx, my, mc = _me()
        return 2 * mx + my, mc, [(1 - mx, my), (mx, 1 - my)]

    def start(ins, outs, ssem, rsem):
        o = outs[0]
        s, mc, nb = peers()
        for j, (px, py) in enumerate(nb):
            _rcopy(o.at[s, mc], o.at[s, mc], ssem.at[j], rsem.at[j], (px, py, mc)).start()

    def wait(ins, outs, ssem, rsem):
        o = outs[0]
        s, mc, nb = peers()
        for j, (px, py) in enumerate(nb):
            _rcopy(o.at[2 * px + py, mc], o.at[2 * px + py, mc], ssem.at[j], rsem.at[j], (px, py, mc)).wait_recv()
        for j, (px, py) in enumerate(nb):
            _rcopy(o.at[s, mc], o.at[s, mc], ssem.at[j], rsem.at[j], (px, py, mc)).wait_send()

    return Rider([buf], [_sds(buf.shape, buf.dtype)], {0: 0}, 2, start, wait)


def rider_gather_fwd(buf):
    h2 = buf.shape[2] // 2
    lo, hi = pl.ds(0, h2), pl.ds(h2, buf.shape[2] - h2)

    def start(ins, outs, ssem, rsem):
        o = outs[0]
        mx, my, mc = _me()
        xs, ys = 2 * (1 - mx) + my, 2 * mx + (1 - my)
        _rcopy(o.at[xs, mc, lo], o.at[xs, mc, lo], ssem.at[0], rsem.at[0], (mx, 1 - my, mc)).start()
        _rcopy(o.at[ys, mc, hi], o.at[ys, mc, hi], ssem.at[1], rsem.at[1], (1 - mx, my, mc)).start()

    def wait(ins, outs, ssem, rsem):
        o = outs[0]
        mx, my, mc = _me()
        xs, ys, ds = 2 * (1 - mx) + my, 2 * mx + (1 - my), 2 * (1 - mx) + (1 - my)
        _rcopy(o.at[ds, mc, lo], o.at[ds, mc, lo], ssem.at[0], rsem.at[0], (mx, 1 - my, mc)).wait_recv()
        _rcopy(o.at[ds, mc, hi], o.at[ds, mc, hi], ssem.at[1], rsem.at[1], (1 - mx, my, mc)).wait_recv()
        _rcopy(o.at[xs, mc, lo], o.at[xs, mc, lo], ssem.at[0], rsem.at[0], (mx, 1 - my, mc)).wait_send()
        _rcopy(o.at[ys, mc, hi], o.at[ys, mc, hi], ssem.at[1], rsem.at[1], (1 - mx, my, mc)).wait_send()

    return Rider([buf], [_sds(buf.shape, buf.dtype)], {0: 0}, 2, start, wait)


def rider_reduce_copy(q, j, r=None):
    def peer():
        mx, my, mc = _me()
        px, py = _chips_of(mx, my)[j]
        return 2 * mx + my, 2 * px + py, (px, py, mc)

    def start(ins, outs, ssem, rsem):
        s, ps, dev = peer()
        _rcopy(ins[0].at[ps], outs[0].at[s], ssem.at[0], rsem.at[0], dev).start()

    def wait(ins, outs, ssem, rsem):
        s, ps, dev = peer()
        _rcopy(ins[0].at[ps], outs[0].at[ps], ssem.at[0], rsem.at[0], dev).wait_recv()
        _rcopy(ins[0].at[ps], outs[0].at[s], ssem.at[0], rsem.at[0], dev).wait_send()

    if r is None:
        return Rider([q], [_sds(q.shape, q.dtype)], {}, 1, start, wait)
    return Rider([q, r], [_sds(q.shape, q.dtype)], {1: 0}, 1, start, wait)


def rider_reduce_copies(q):
    def start(ins, outs, ssem, rsem):
        mx, my, mc = _me()
        s = 2 * mx + my
        for j, (px, py) in enumerate(_chips_of(mx, my)):
            _rcopy(ins[0].at[2 * px + py], outs[0].at[s], ssem.at[j], rsem.at[j], (px, py, mc)).start()

    def wait(ins, outs, ssem, rsem):
        mx, my, mc = _me()
        s = 2 * mx + my
        for j, (px, py) in enumerate(_chips_of(mx, my)):
            ps = 2 * px + py
            _rcopy(ins[0].at[ps], outs[0].at[ps], ssem.at[j], rsem.at[j], (px, py, mc)).wait_recv()
        for j, (px, py) in enumerate(_chips_of(mx, my)):
            _rcopy(ins[0].at[2 * px + py], outs[0].at[s], ssem.at[j], rsem.at[j], (px, py, mc)).wait_send()

    return Rider([q], [_sds(q.shape, q.dtype)], {}, 3, start, wait)


def rider_join(buf, g=None):
    def start(ins, outs, ssem, rsem):
        o = outs[0] if g is None else outs[0].at[g]
        mx, my, mc = _me()
        _rcopy(o.at[mc], o.at[mc], ssem.at[0], rsem.at[0], (mx, my, 1 - mc)).start()

    def wait(ins, outs, ssem, rsem):
        o = outs[0] if g is None else outs[0].at[g]
        mx, my, mc = _me()
        _rcopy(o.at[1 - mc], o.at[1 - mc], ssem.at[0], rsem.at[0], (mx, my, 1 - mc)).wait_recv()
        _rcopy(o.at[mc], o.at[mc], ssem.at[0], rsem.at[0], (mx, my, 1 - mc)).wait_send()

    return Rider([buf], [_sds(buf.shape, buf.dtype)], {0: 0}, 1, start, wait)


def _host_call(body, rider, name, grid, in_specs, out_specs, out_shape, operands, sem, n_in, n_out, aliases=None):
    aliases = dict(aliases or {})
    if rider is None:
        return pl.pallas_call(body, name=name, out_shape=out_shape, grid=grid, in_specs=in_specs, out_specs=out_specs,
                              input_output_aliases=aliases, compiler_params=_cp(sem, VMEM_BIG))(*operands)
    n_ri, n_ro = len(rider.ins), len(rider.outs)
    aliases.update({n_in + a: n_out + b for a, b in rider.aliases.items()})

    def hosted(*refs):
        ins, r_in = refs[:n_in], refs[n_in:n_in + n_ri]
        outs, r_out = refs[n_in + n_ri:n_in + n_ri + n_out], refs[n_in + n_ri + n_out:n_in + n_ri + n_out + n_ro]
        ssem, rsem = refs[-2], refs[-1]
        first = functools.reduce(lambda a, b: a & b, [pl.program_id(k) == 0 for k in range(len(grid))])
        last = functools.reduce(lambda a, b: a & b, [pl.program_id(k) == grid[k] - 1 for k in range(len(grid))])

        @pl.when(first)
        def _():
            rider.start(r_in, r_out, ssem, rsem)
        body(*ins, *outs)

        @pl.when(last)
        def _():
            rider.wait(r_in, r_out, ssem, rsem)

    return pl.pallas_call(
        hosted, name=name, out_shape=list(out_shape) + list(rider.outs), grid=grid,
        in_specs=list(in_specs) + [ANY] * n_ri, out_specs=list(out_specs) + [ANY] * n_ro,
        input_output_aliases=aliases,
        scratch_shapes=[pltpu.SemaphoreType.DMA((rider.nsem,)), pltpu.SemaphoreType.DMA((rider.nsem,))],
        compiler_params=_cp(("arbitrary",) * len(grid), VMEM_BIG))(*operands, *rider.ins)


def ffn_up(lay, hm, wbuf, ig, iu, ns, name, rider=None):
    tm = lay.tm

    def body(h_ref, wg_ref, wu_ref, sp_ref, sl_ref, u_ref, a_ref):
        hb = h_ref[...]
        g = _nt(hb, wg_ref[0])
        u = _nt(hb, wu_ref[0])
        sg = _sigmoid(g)
        sl = g * sg
        sp_ref[0] = (sg + sl * (1.0 - sg)).astype(BF16)
        sl_ref[0] = sl.astype(BF16)
        u_ref[0] = u.astype(BF16)
        a_ref[0] = (sl * u).astype(BF16)

    spec_o = pl.BlockSpec((1, tm, ns), lambda s, i: (s, i, 0))
    return _host_call(
        body, rider, name, (N_CHIP, lay.T // tm),
        [pl.BlockSpec((tm, D), lambda s, i: (i, 0)), pl.BlockSpec((1, ns, D), lambda s, i: (s, ig, 0)),
         pl.BlockSpec((1, ns, D), lambda s, i: (s, iu, 0))],
        [spec_o] * 4, [_sds((N_CHIP, lay.T, ns), BF16)] * 4, (hm, wbuf, wbuf), ("parallel", "parallel"), 3, 4)


def slab_nn_acc(lay, zs, wbuf, idxs, ns, name, rider=None):
    tm = lay.tm
    npair = len(zs)

    def body(*refs):
        o_ref = refs[-1]

        @pl.when(pl.program_id(1) == 0)
        def _():
            o_ref[...] = jnp.zeros_like(o_ref)
        acc = _nn(refs[0][0], refs[npair][0])
        for p in range(1, npair):
            acc += _nn(refs[p][0], refs[npair + p][0])
        o_ref[...] += acc

    in_specs = [pl.BlockSpec((1, tm, ns), lambda i, s: (s, i, 0)) for _ in zs]
    in_specs += [pl.BlockSpec((1, ns, D), functools.partial(lambda i, s, q: (s, q, 0), q=q)) for q in idxs]
    return _host_call(body, rider, name, (lay.T // tm, N_CHIP), in_specs, [pl.BlockSpec((tm, D), lambda i, s: (i, 0))],
                      [_sds((lay.T, D), F32)], (*zs, *([wbuf] * npair)), ("parallel", "arbitrary"), 2 * npair, 1)


def ffn_bwd_da(lay, dy, wbuf, idn, sp, sl, u, ns, name, rider=None):
    tm = lay.tm

    def body(dy_ref, wd_ref, sp_ref, sl_ref, u_ref, dg_ref, du_ref):
        da = _nt(dy_ref[...], wd_ref[0])
        dg_ref[0] = (da * u_ref[0].astype(F32) * sp_ref[0].astype(F32)).astype(BF16)
        du_ref[0] = (da * sl_ref[0].astype(F32)).astype(BF16)

    spec_z = pl.BlockSpec((1, tm, ns), lambda s, i: (s, i, 0))
    return _host_call(
        body, rider, name, (N_CHIP, lay.T // tm),
        [pl.BlockSpec((tm, D), lambda s, i: (i, 0)), pl.BlockSpec((1, ns, D), lambda s, i: (s, idn, 0)),
         spec_z, spec_z, spec_z],
        [spec_z] * 2, [_sds((N_CHIP, lay.T, ns), BF16)] * 2, (dy, wbuf, sp, sl, u), ("parallel", "parallel"), 5, 2)


def slab_tn(lay, z, x, gbuf, idx, ns, name, rider=None):
    tk = lay.tm

    def body(z_ref, x_ref, g_in, o_ref):
        del g_in

        @pl.when(pl.program_id(1) == 0)
        def _():
            o_ref[...] = jnp.zeros_like(o_ref)
        o_ref[0] += _tn(z_ref[0], x_ref[...])

    return _host_call(
        body, rider, name, (N_CHIP, lay.T // tk),
        [pl.BlockSpec((1, tk, ns), lambda s, k: (s, k, 0)), pl.BlockSpec((tk, D), lambda s, k: (k, 0)), ANY],
        [pl.BlockSpec((1, ns, D), lambda s, k: (s, idx, 0))], [_sds(gbuf.shape, F32)], (z, x, gbuf),
        ("parallel", "arbitrary"), 3, 1, aliases={2: 0})


Q0, K0, V0, U0, QR0, KR0, PEXT = 0, 512, 640, 768, 1280, 1792, 1920


def rope_fwd(lay, p, cos, sin, name):
    def fn(pb, cs, sn):
        cs4 = jnp.concatenate([cs] * 4, axis=1)
        sn4 = jnp.concatenate([sn] * 4, axis=1)
        qr = pb[:, Q0:K0] * cs4 + pb[:, QR0:KR0] * sn4
        kr = pb[:, K0:V0] * cs + pb[:, KR0:PEXT] * sn
        return qr, kr, pb[:, V0:U0], pb[:, U0:QR0]
    return rowwise(lay, name, fn, [p, cos, sin], outs=[(ATT_W, BF16), (KV_W, BF16), (KV_W, BF16), (POOL_W, F32)])


def rope_bwd(lay, dqr, dkr, dv, du, cos, sin, name):
    def fn(dq, dk, dvb, dub, cs, sn):
        cs4 = jnp.concatenate([cs] * 4, axis=1)
        sn4 = jnp.concatenate([sn] * 4, axis=1)
        return (jnp.concatenate([dq * cs4, dk * cs, dvb, dub, dq * sn4, dk * sn], axis=1),)
    return rowwise(lay, name, fn, [dqr, dkr, dv, du, cos, sin], outs=[(PEXT, BF16)])[0]


def _attn_specs(lay):
    nbs, cbk, lbk = lay.PS // BLK, lay.C // BLK, lay.L // BLK

    def kv_map(j):
        return lambda s, n: (s * nbs + cbk + jnp.clip(n - cbk + j - 1, 0, lbk - 1), 0)

    win = [pl.BlockSpec((BLK, KV_W), kv_map(j)) for j in range(3)]
    ctx = pl.BlockSpec((lay.C, KV_W), lambda s, n: (s * (lay.PS // lay.C), 0))
    return nbs, cbk, lbk, win, ctx


def _attn_masks(n, cbk, lbk):
    row = lax.broadcasted_iota(jnp.int32, (BLK, BLK), 0)
    col = lax.broadcasted_iota(jnp.int32, (BLK, BLK), 1)
    m = n - cbk
    lat = n >= cbk
    valid = [lat & (m >= 1) & (col >= row), lat & (col >= 0), lat & (m <= lbk - 2) & (col <= row)]
    lane_lo = lax.broadcasted_iota(jnp.int32, (BLK, 2 * HEAD_DIM), 1) < HEAD_DIM
    return valid, lane_lo


def attn_fwd(lay, qr, kr, vb, sink_tab, name):
    nbs, cbk, lbk, win, ctx = _attn_specs(lay)

    def body(q_ref, k0, k1, k2, kc_ref, v0, v1, v2, vc_ref, sk_ref, o_ref, l_ref):
        n = pl.program_id(1)
        valid, lane_lo = _attn_masks(n, cbk, lbk)
        valid4 = [jnp.concatenate([v] * 4, axis=0) for v in valid]
        ks = [k0[...], k1[...], k2[...]]
        vs = [v0[...], v1[...], v2[...]]
        kc, vc = kc_ref[...], vc_ref[...]
        q2s = [q_ref[:, p * 128:(p + 1) * 128] for p in range(4)]
        outs, lses = [], []
        for hh in range(2):
            sel = lane_lo == (hh == 0)
            qm = jnp.concatenate([jnp.where(sel, q2, jnp.zeros_like(q2)) for q2 in q2s], axis=0)
            sk = jnp.concatenate([jnp.broadcast_to(sk_ref[p:p + 1, hh * HEAD_DIM:hh * HEAD_DIM + 1], (BLK, 1))
                                  for p in range(4)], axis=0)
            sw = [jnp.where(valid4[j], _nt(qm, ks[j]) * ATT_SCALE, NEG_INF) for j in range(3)]
            sc = _nt(qm, kc) * ATT_SCALE
            mx = jnp.maximum(jnp.maximum(jnp.maximum(sw[0].max(-1, keepdims=True), sw[1].max(-1, keepdims=True)),
                                         jnp.maximum(sw[2].max(-1, keepdims=True), sc.max(-1, keepdims=True))), sk)
            ew = [jnp.exp(s - mx) for s in sw]
            ec = jnp.exp(sc - mx)
            den = ew[0].sum(-1, keepdims=True) + ew[1].sum(-1, keepdims=True) + ew[2].sum(-1, keepdims=True)
            den = den + ec.sum(-1, keepdims=True) + jnp.exp(sk - mx)
            o = _nn((ec / den).astype(BF16), vc)
            for j in range(3):
                o += _nn((ew[j] / den).astype(BF16), vs[j])
            outs.append(o)
            lses.append(mx + jnp.log(den))
        for p in range(4):
            rows = slice(p * BLK, (p + 1) * BLK)
            o_ref[:, p * 128:(p + 1) * 128] = jnp.where(lane_lo, outs[0][rows], outs[1][rows]).astype(o_ref.dtype)
            l_ref[:, p * 128:(p + 1) * 128] = jnp.where(lane_lo, jnp.broadcast_to(lses[0][rows], (BLK, 128)),
                                                        jnp.broadcast_to(lses[1][rows], (BLK, 128)))

    qspec = pl.BlockSpec((BLK, ATT_W), lambda s, n: (s * nbs + n, 0))
    return pl.pallas_call(
        body, name=name, out_shape=[_sds((lay.T, ATT_W), BF16), _sds((lay.T, ATT_W), F32)], grid=(2, nbs),
        in_specs=[qspec] + win + [ctx] + win + [ctx] + [pl.BlockSpec((8, 128), lambda s, n: (0, 0))],
        out_specs=[qspec, qspec], compiler_params=_cp(("parallel", "parallel")))(qr, kr, kr, kr, kr, vb, vb, vb, vb, sink_tab)


def attn_bwd(lay, qr, kr, vb, sink_tab, lse, datt, name, rider=None):
    nbs, cbk, lbk, win, ctx = _attn_specs(lay)
    C, PS = lay.C, lay.PS

    def body(q_ref, k0, k1, k2, kc_ref, v0, v1, v2, vc_ref, sk_ref, l_ref, do_ref, dq_ref, dk_ref, dv_ref, ds_ref):
        n = pl.program_id(1)
        valid, lane_lo = _attn_masks(n, cbk, lbk)

        @pl.when(n == 0)
        def _():
            dk_ref[...] = jnp.zeros_like(dk_ref)
            dv_ref[...] = jnp.zeros_like(dv_ref)
            ds_ref[...] = jnp.zeros_like(ds_ref)

        ks = [k0[...], k1[...], k2[...], kc_ref[...]]
        vs = [v0[...], v1[...], v2[...], vc_ref[...]]
        valid4 = [jnp.concatenate([v] * 4, axis=0) for v in valid]
        dks = [jnp.zeros((BLK, KV_W), F32)] * 3 + [jnp.zeros((C, KV_W), F32)]
        dvs = list(dks)
        q2s = [q_ref[:, p * 128:(p + 1) * 128] for p in range(4)]
        do2s = [do_ref[:, p * 128:(p + 1) * 128].astype(BF16) for p in range(4)]
        lse2s = [l_ref[:, p * 128:(p + 1) * 128] for p in range(4)]
        dq_h, dd_h = [], []
        for hh in range(2):
            sel = lane_lo == (hh == 0)
            qm = jnp.concatenate([jnp.where(sel, q2, jnp.zeros_like(q2)) for q2 in q2s], axis=0)
            dom = jnp.concatenate([jnp.where(sel, d2, jnp.zeros_like(d2)) for d2 in do2s], axis=0)
            lse_h = jnp.concatenate([l2[:, hh * HEAD_DIM:hh * HEAD_DIM + 1] for l2 in lse2s], axis=0)
            ps, dps = [], []
            for j in range(4):
                s = _nt(qm, ks[j]) * ATT_SCALE
                if j < 3:
                    s = jnp.where(valid4[j], s, NEG_INF)
                ps.append(jnp.exp(s - lse_h))
                dps.append(_nt(dom, vs[j]))
            dd = (ps[0] * dps[0]).sum(-1, keepdims=True) + (ps[1] * dps[1]).sum(-1, keepdims=True)
            dd = dd + (ps[2] * dps[2]).sum(-1, keepdims=True) + (ps[3] * dps[3]).sum(-1, keepdims=True)
            dq = jnp.zeros((4 * BLK, 128), F32)
            for j in range(4):
                dsb = (ps[j] * (dps[j] - dd) * ATT_SCALE).astype(BF16)
                dq += _nn(dsb, ks[j])
                dks[j] = dks[j] + _tn(dsb, qm)
                dvs[j] = dvs[j] + _tn(ps[j].astype(BF16), dom)
            dq_h.append(dq)
            dd_h.append(dd)
        for p in range(4):
            sl = slice(p * 128, (p + 1) * 128)
            rows = slice(p * BLK, (p + 1) * BLK)
            dq_ref[:, sl] = jnp.where(lane_lo, dq_h[0][rows], dq_h[1][rows])
            dd2 = jnp.where(lane_lo, jnp.broadcast_to(dd_h[0][rows], (BLK, 128)), jnp.broadcast_to(dd_h[1][rows], (BLK, 128)))
            psink = jnp.exp(sk_ref[p:p + 1, :] - lse2s[p])
            ds_ref[0, p:p + 1, :] += -jnp.sum(psink * dd2, axis=0, keepdims=True)
        dk_ref[0:C, :] += dks[3]
        dv_ref[0:C, :] += dvs[3]
        for j in range(3):
            r0 = pl.multiple_of((cbk + jnp.clip(n - cbk + j - 1, 0, lbk - 1)) * BLK, BLK)
            dk_ref[pl.ds(r0, BLK), :] += dks[j]
            dv_ref[pl.ds(r0, BLK), :] += dvs[j]

    qspec = pl.BlockSpec((BLK, ATT_W), lambda s, n: (s * nbs + n, 0))
    kvout = pl.BlockSpec((PS, KV_W), lambda s, n: (s, 0))
    return _host_call(
        body, rider, name, (2, nbs),
        [qspec] + win + [ctx] + win + [ctx] + [pl.BlockSpec((8, 128), lambda s, n: (0, 0)), qspec, qspec],
        [qspec, kvout, kvout, pl.BlockSpec((1, 8, 128), lambda s, n: (s, 0, 0))],
        [_sds((lay.T, ATT_W), F32), _sds((lay.T, KV_W), F32), _sds((lay.T, KV_W), F32), _sds((2, 8, 128), F32)],
        (qr, kr, kr, kr, kr, vb, vb, vb, vb, sink_tab, lse, datt), ("parallel", "arbitrary"), 12, 4)


def _winsum(x, r):
    n = x.shape[0]
    t = lax.broadcasted_iota(jnp.int32, x.shape, 0)
    acc = x
    for o in range(1, r + 1):
        acc = acc + jnp.where(t >= o, pltpu.roll(x, o, 0), 0.0) + jnp.where(t < n - o, pltpu.roll(x, n - o, 0), 0.0)
    return acc


def _wincount(n, r):
    t = lax.broadcasted_iota(jnp.int32, (n, 128), 0)
    return (jnp.minimum(t + r, n - 1) - jnp.maximum(t - r, 0) + 1).astype(F32)


def pool_fwd(lay, u, w_pool, scale, name):
    segs = [(0, lay.C), (lay.C, lay.L)]

    def body(u_ref, w_ref, s_ref, o_ref):
        for r0, n in segs:
            for g, wd in enumerate(POOL_WINDOWS):
                sl = slice(g * 128, (g + 1) * 128)
                x = u_ref[r0:r0 + n, sl]
                d = _winsum(x, wd // 2) / _wincount(n, wd // 2) - x
                y = _nn(d.astype(BF16), w_ref[g].astype(BF16)) * s_ref[:, sl]
                o_ref[r0:r0 + n, sl] = y.astype(o_ref.dtype)

    spec = pl.BlockSpec((lay.PS, POOL_W), lambda s: (s, 0))
    return pl.pallas_call(
        body, name=name, out_shape=_sds((lay.T, POOL_W), BF16), grid=(2,),
        in_specs=[spec, pl.BlockSpec(w_pool.shape, lambda s: (0, 0, 0)), pl.BlockSpec((1, POOL_W), lambda s: (0, 0))],
        out_specs=spec, compiler_params=_cp(("parallel",), VMEM_BIG))(u, w_pool, scale)


def pool_bwd(lay, u, dcat, w_pool, scale, name):
    segs = [(0, lay.C), (lay.C, lay.L)]

    def body(u_ref, dp_ref, w_ref, s_ref, du_ref, dw_ref, dsc_ref):
        for g, wd in enumerate(POOL_WINDOWS):
            sl = slice(g * 128, (g + 1) * 128)
            wb = w_ref[g].astype(BF16)
            dw = jnp.zeros((128, 128), F32)
            dsc = jnp.zeros((1, 128), F32)
            for r0, n in segs:
                x = u_ref[r0:r0 + n, sl]
                cnt = _wincount(n, wd // 2)
                d = (_winsum(x, wd // 2) / cnt - x).astype(BF16)
                dp = dp_ref[r0:r0 + n, sl]
                dsc += jnp.sum(_nn(d, wb) * dp, axis=0, keepdims=True)
                dyp = (dp * s_ref[:, sl]).astype(BF16)
                dw += _tn(d, dyp)
                dd = _nt(dyp, wb)
                du_ref[r0:r0 + n, sl] = _winsum(dd / cnt, wd // 2) - dd
            dw_ref[0, g] = dw
            dsc_ref[0, :, sl] = dsc

    spec = pl.BlockSpec((lay.PS, POOL_W), lambda s: (s, 0))
    return pl.pallas_call(
        body, name=name,
        out_shape=[_sds((lay.T, POOL_W), F32), _sds((2, 4, 128, 128), F32), _sds((2, 1, POOL_W), F32)], grid=(2,),
        in_specs=[spec, pl.BlockSpec((lay.PS, POOL_W), lambda s: (s, 1)), pl.BlockSpec(w_pool.shape, lambda s: (0, 0, 0)),
                  pl.BlockSpec((1, POOL_W), lambda s: (0, 0))],
        out_specs=[spec, pl.BlockSpec((1, 4, 128, 128), lambda s: (s, 0, 0, 0)), pl.BlockSpec((1, 1, POOL_W), lambda s: (s, 0, 0))],
        compiler_params=_cp(("parallel",), VMEM_BIG))(u, dcat, w_pool, scale)


CONV_OFFS = (-1, 0, 1, 2)
CW = 256


def _shift_rows(x, o):
    if o == 0:
        return x
    n = x.shape[0]
    t = lax.broadcasted_iota(jnp.int32, x.shape, 0)
    if o < 0:
        return jnp.where(t >= -o, pltpu.roll(x, -o, 0), 0.0)
    return jnp.where(t < n - o, pltpu.roll(x, n - o, 0), 0.0)


def conv_fwd(lay, p, col0, w, b, name):
    segs = [(0, lay.C), (lay.C, lay.L)]
    cb0 = col0 // CW

    def body(x_ref, w_ref, b_ref, o_ref):
        for r0, n in segs:
            x = x_ref[r0:r0 + n, :]
            y = jnp.broadcast_to(b_ref[...], x.shape)
            for k, o in enumerate(CONV_OFFS):
                y = y + _shift_rows(x, o) * w_ref[k:k + 1, :]
            o_ref[r0:r0 + n, :] = y

    return pl.pallas_call(
        body, name=name, out_shape=_sds((lay.T, D), F32), grid=(2, D // CW),
        in_specs=[pl.BlockSpec((lay.PS, CW), lambda s, j: (s, cb0 + j)), pl.BlockSpec((4, CW), lambda s, j: (0, j)),
                  pl.BlockSpec((1, CW), lambda s, j: (0, j))],
        out_specs=pl.BlockSpec((lay.PS, CW), lambda s, j: (s, j)),
        compiler_params=_cp(("parallel", "parallel")))(p, w, b)


def conv_bwd(lay, p, col0, w, duc, name):
    segs = [(0, lay.C), (lay.C, lay.L)]
    cb0 = col0 // CW

    def body(x_ref, w_ref, g_ref, du_ref, dw_ref, db_ref):
        dws = [jnp.zeros((1, CW), F32)] * 4
        db = jnp.zeros((1, CW), F32)
        for r0, n in segs:
            x = x_ref[r0:r0 + n, :]
            g = g_ref[r0:r0 + n, :]
            du = jnp.zeros_like(g)
            for k, o in enumerate(CONV_OFFS):
                du = du + _shift_rows(g, -o) * w_ref[k:k + 1, :]
                dws[k] = dws[k] + jnp.sum(g * _shift_rows(x, o), axis=0, keepdims=True)
            db = db + jnp.sum(g, axis=0, keepdims=True)
            du_ref[r0:r0 + n, :] = du.astype(du_ref.dtype)
        dw_ref[0] = jnp.concatenate(dws, axis=0)
        db_ref[0] = db

    return pl.pallas_call(
        body, name=name, out_shape=[_sds((lay.T, D), BF16), _sds((2, 4, D), F32), _sds((2, 1, D), F32)], grid=(2, D // CW),
        in_specs=[pl.BlockSpec((lay.PS, CW), lambda s, j: (s, cb0 + j)), pl.BlockSpec((4, CW), lambda s, j: (0, j)),
                  pl.BlockSpec((lay.PS, CW), lambda s, j: (s, j))],
        out_specs=[pl.BlockSpec((lay.PS, CW), lambda s, j: (s, j)), pl.BlockSpec((1, 4, CW), lambda s, j: (s, 0, j)),
                   pl.BlockSpec((1, 1, CW), lambda s, j: (s, 0, j))],
        compiler_params=_cp(("parallel", "parallel")))(p, w, duc)


def _softplus_neg(lam):
    z = -lam
    w = jnp.exp(-jnp.abs(z))
    log1p = jnp.where(w < 1e-2, w * (1.0 - w * (0.5 - w / 3.0)), jnp.log(1.0 + w))
    return jnp.maximum(z, 0.0) + log1p, -_sigmoid(z)


def _neg_expm1(x):
    series = -x * (1.0 + x * (0.5 + x * (1.0 / 6.0 + x * (1.0 / 24.0 + x * (1.0 / 120.0)))))
    return jnp.where(x > -0.05, series, 1.0 - jnp.exp(x))


def _lru_gates(x, xb, wa, wx, ba, bx, lam):
    r = _sigmoid(_nn(xb, wa.astype(BF16)) + ba)
    gi = _sigmoid(_nn(xb, wx.astype(BF16)) + bx)
    sp, dsp = _softplus_neg(lam)
    la = -LRU_C * r * sp
    a = jnp.exp(la)
    sq = jnp.sqrt(_neg_expm1(2.0 * la))
    return r, gi, sp, dsp, a, sq


def lru_coeffs(lay, uc, wa, wx, vec, name):
    tr = lay.tc

    def body(x_ref, wa_ref, wx_ref, v_ref, a_ref, b_ref):
        for h in range(8):
            sl = slice(h * 128, (h + 1) * 128)
            x = x_ref[:, sl]
            xb = x.astype(BF16)
            for d in range(2):
                _, gi, _, _, a, sq = _lru_gates(x, xb, wa_ref[d, h], wx_ref[d, h], v_ref[d:d + 1, sl],
                                                v_ref[2 + d:3 + d, sl], v_ref[4 + d:5 + d, sl])
                a_ref[d, h] = a
                b_ref[d, h] = sq * (gi * x)

    wspec = pl.BlockSpec((2, 8, 128, 128), lambda i: (0, 0, 0, 0))
    ospec = pl.BlockSpec((2, 8, tr, 128), lambda i: (0, 0, i, 0))
    return pl.pallas_call(
        body, name=name, out_shape=[_sds((2, 8, lay.T, 128), F32)] * 2, grid=(lay.T // tr,),
        in_specs=[pl.BlockSpec((tr, D), lambda i: (i, 0)), wspec, wspec, pl.BlockSpec((6, D), lambda i: (0, 0))],
        out_specs=[ospec, ospec], compiler_params=_cp(("parallel",), VMEM_BIG))(uc, wa, wx, vec)


def lru_coeffs_bwd(lay, uc, wa, wx, vec, da, db, name, rider=None):
    tr = lay.tc

    def body(x_ref, wa_ref, wx_ref, v_ref, da_ref, db_ref, dx_ref, dwa_ref, dwx_ref, dv_ref):
        @pl.when(pl.program_id(0) == 0)
        def _():
            dwa_ref[...] = jnp.zeros_like(dwa_ref)
            dwx_ref[...] = jnp.zeros_like(dwx_ref)
            dv_ref[...] = jnp.zeros_like(dv_ref)

        for h in range(8):
            sl = slice(h * 128, (h + 1) * 128)
            x = x_ref[:, sl]
            xb = x.astype(BF16)
            dx = jnp.zeros_like(x)
            for d in range(2):
                wab, wxb = wa_ref[d, h].astype(BF16), wx_ref[d, h].astype(BF16)
                r, gi, sp, dsp, a, sq = _lru_gates(x, xb, wa_ref[d, h], wx_ref[d, h], v_ref[d:d + 1, sl],
                                                   v_ref[2 + d:3 + d, sl], v_ref[4 + d:5 + d, sl])
                dbv, dav = db_ref[d, h], da_ref[d, h]
                t1 = dbv * sq
                dgi = t1 * x
                dx = dx + t1 * gi
                dla = dav * a - (dbv * gi * x) * (a * a) / sq
                dr = dla * (-LRU_C * sp)
                dlam = jnp.sum(dla * (-LRU_C * r), axis=0, keepdims=True) * dsp
                dpa = dr * r * (1.0 - r)
                dpx = dgi * gi * (1.0 - gi)
                dpab, dpxb = dpa.astype(BF16), dpx.astype(BF16)
                dwa_ref[d, h] += _tn(xb, dpab)
                dwx_ref[d, h] += _tn(xb, dpxb)
                dx = dx + _nt(dpab, wab) + _nt(dpxb, wxb)
                dv_ref[d:d + 1, sl] += jnp.sum(dpa, axis=0, keepdims=True)
                dv_ref[2 + d:3 + d, sl] += jnp.sum(dpx, axis=0, keepdims=True)
                dv_ref[4 + d:5 + d, sl] += dlam
            dx_ref[:, sl] = dx

    wspec = pl.BlockSpec((2, 8, 128, 128), lambda i: (0, 0, 0, 0))
    gspec = pl.BlockSpec((2, 8, tr, 128), lambda i: (0, 0, i, 0))
    vspec = pl.BlockSpec((6, D), lambda i: (0, 0))
    xspec = pl.BlockSpec((tr, D), lambda i: (i, 0))
    return _host_call(
        body, rider, name, (lay.T // tr,), [xspec, wspec, wspec, vspec, gspec, gspec], [xspec, wspec, wspec, vspec],
        [_sds((lay.T, D), F32), _sds((2, 8, 128, 128), F32), _sds((2, 8, 128, 128), F32), _sds((6, D), F32)],
        (uc, wa, wx, vec, da, db), ("arbitrary",), 6, 4)


GB = 2
SCAN_UNROLL = 4


def _tile_scan(a, b, up):
    t = lax.broadcasted_iota(jnp.int32, a.shape, 0)
    for d in (1, 2, 4):
        sh = 8 - d if up else d
        m = (t < 8 - d) if up else (t >= d)
        a_prev, b_prev = pltpu.roll(a, sh, 0), pltpu.roll(b, sh, 0)
        b = jnp.where(m, a * b_prev + b, b)
        a = jnp.where(m, a * a_prev, a)
    return a, b


def lru_scan(lay, a, b, name):
    segs = [(0, lay.C), (lay.C, lay.L)]

    def body(a_ref, b_ref, s_ref):
        for d in range(2):
            rev = d == 1
            state = tuple(jnp.zeros((1, 128), F32) for _ in range(GB))
            for base, n in segs:
                nt = n // 8

                def step(j, c, base=base, nt=nt, rev=rev, d=d):
                    c = list(c)
                    for u in range(SCAN_UNROLL):
                        jj = j * SCAN_UNROLL + u
                        r0 = pl.multiple_of(base + 8 * ((nt - 1 - jj) if rev else jj), 8)
                        for g in range(GB):
                            at, bt = _tile_scan(a_ref[d, g, pl.ds(r0, 8), :], b_ref[d, g, pl.ds(r0, 8), :], rev)
                            h = at * c[g] + bt
                            s_ref[d, g, pl.ds(r0, 8), :] = h
                            c[g] = h[0:1] if rev else h[7:8]
                    return tuple(c)

                state = lax.fori_loop(0, nt // SCAN_UNROLL, step, state)

    spec = pl.BlockSpec((2, GB, lay.PS, 128), lambda s, hb: (0, hb, s, 0))
    return pl.pallas_call(
        body, name=name, out_shape=_sds((2, 8, lay.T, 128), F32), grid=(2, 8 // GB),
        in_specs=[spec, spec], out_specs=spec, compiler_params=_cp(("parallel", "parallel"), VMEM_BIG))(a, b)


def lru_scan_bwd(lay, a, s, dy, name):
    segs = [(0, lay.C), (lay.C, lay.L)]
    C, PS = lay.C, lay.PS

    def body(a_ref, s_ref, g_ref, da_ref, db_ref):
        t = lax.broadcasted_iota(jnp.int32, (8, 128), 0)
        for d in range(2):
            rev = d == 1
            carry = tuple(jnp.zeros((1, 128), F32) for _ in range(GB))
            for si in (1, 0):
                base, n = segs[si]
                nt = n // 8

                def step(j, c, base=base, nt=nt, rev=rev, d=d):
                    c = list(c)
                    for u in range(SCAN_UNROLL):
                        jj = j * SCAN_UNROLL + u
                        r0 = pl.multiple_of(base + 8 * (jj if rev else (nt - 1 - jj)), 8)
                        if rev:
                            rn = pl.multiple_of(jnp.where(r0 == PS - 8, 0, r0 + 8), 8)
                            nb_zero = r0 == C - 8
                        else:
                            rn = pl.multiple_of(jnp.maximum(r0 - 8, 0), 8)
                            nb_zero = r0 == 0
                        for g in range(GB):
                            av = a_ref[d, g, pl.ds(r0, 8), :]
                            gv = g_ref[g, pl.ds(r0, 8), :]
                            sv = s_ref[d, g, pl.ds(r0, 8), :]
                            nbt = s_ref[d, g, pl.ds(rn, 8), :]
                            at, bt = _tile_scan(av, av * gv, not rev)
                            m = at * c[g] + bt
                            if rev:
                                m_next = jnp.where(t >= 1, pltpu.roll(m, 1, 0), c[g])
                                nb = jnp.where(nb_zero, 0.0, nbt[0:1])
                                h_prev = jnp.where(t < 7, pltpu.roll(sv, 7, 0), nb)
                                c[g] = m[7:8]
                            else:
                                m_next = jnp.where(t < 7, pltpu.roll(m, 7, 0), c[g])
                                nb = jnp.where(nb_zero, 0.0, nbt[7:8])
                                h_prev = jnp.where(t >= 1, pltpu.roll(sv, 1, 0), nb)
                                c[g] = m[0:1]
                            lam = gv + m_next
                            db_ref[d, g, pl.ds(r0, 8), :] = lam
                            da_ref[d, g, pl.ds(r0, 8), :] = lam * h_prev
                    return tuple(c)

                carry = lax.fori_loop(0, nt // SCAN_UNROLL, step, carry)

    spec = pl.BlockSpec((2, GB, lay.PS, 128), lambda s, hb: (0, hb, s, 0))
    return pl.pallas_call(
        body, name=name, out_shape=[_sds((2, 8, lay.T, 128), F32)] * 2, grid=(2, 8 // GB),
        in_specs=[spec, spec, pl.BlockSpec((GB, lay.PS, 128), lambda s, hb: (hb, s, 0))],
        out_specs=[spec, spec], compiler_params=_cp(("parallel", "parallel"), VMEM_BIG))(a, s, dy)


def _gelu(x):
    k = math.sqrt(2.0 / math.pi)
    t = jnp.tanh(k * (x + 0.044715 * x * x * x))
    return 0.5 * x * (1.0 + t), 0.5 * (1.0 + t) + 0.5 * x * (1.0 - t * t) * k * (1.0 + 3 * 0.044715 * x * x)


def lru_gate(lay, p, s, name):
    tr = lay.tr

    def body(g_ref, s_ref, o_ref):
        for h in range(8):
            sl = slice(h * 128, (h + 1) * 128)
            o_ref[:, sl] = (_gelu(g_ref[:, sl])[0] * (s_ref[0, h] + s_ref[1, h])).astype(o_ref.dtype)

    return pl.pallas_call(
        body, name=name, out_shape=_sds((lay.T, D), BF16), grid=(lay.nblk,),
        in_specs=[pl.BlockSpec((tr, D), lambda i: (i, 0)), pl.BlockSpec((2, 8, tr, 128), lambda i: (0, 0, i, 0))],
        out_specs=pl.BlockSpec((tr, D), lambda i: (i, 0)), compiler_params=_cp(("parallel",)))(p, s)


def lru_gate_bwd(lay, p, s, do, name):
    tr = lay.tr

    def body(g_ref, s_ref, do_ref, dg_ref, dy_ref):
        for h in range(8):
            sl = slice(h * 128, (h + 1) * 128)
            ge, dge = _gelu(g_ref[:, sl])
            dov = do_ref[:, sl]
            dg_ref[:, sl] = (dov * (s_ref[0, h] + s_ref[1, h]) * dge).astype(dg_ref.dtype)
            dy_ref[h] = dov * ge

    xspec = pl.BlockSpec((tr, D), lambda i: (i, 0))
    return pl.pallas_call(
        body, name=name, out_shape=[_sds((lay.T, D), BF16), _sds((8, lay.T, 128), F32)], grid=(lay.nblk,),
        in_specs=[xspec, pl.BlockSpec((2, 8, tr, 128), lambda i: (0, 0, i, 0)), xspec],
        out_specs=[xspec, pl.BlockSpec((8, tr, 128), lambda i: (0, i, 0))],
        compiler_params=_cp(("parallel",)))(p, s, do)


def silu_rows(x, name):
    def body(x_ref, o_ref):
        v = x_ref[...]
        o_ref[...] = (v * _sigmoid(v)).astype(o_ref.dtype)
    return pl.pallas_call(body, name=name, out_shape=_sds(x.shape, BF16), in_specs=[VMEM_SPEC], out_specs=VMEM_SPEC)(x)


def mod_grad_rows(gath, name):
    w = gath.shape[-1]

    def body(g_ref, dm_ref, db_ref):
        dm_ref[...] = jnp.zeros_like(dm_ref)
        for l in range(2):
            ctx = g_ref[0, 3 * l + 2:3 * l + 3, :]
            tot = g_ref[0, 3 * l:3 * l + 1, :] + g_ref[0, 3 * l + 1:3 * l + 2, :]
            for k in range(8):
                dm_ref[l, 2 * k:2 * k + 2, :] = g_ref[k, 3 * l:3 * l + 2, :]
                if k:
                    ctx = ctx + g_ref[k, 3 * l + 2:3 * l + 3, :]
                    tot = tot + (g_ref[k, 3 * l:3 * l + 1, :] + g_ref[k, 3 * l + 1:3 * l + 2, :])
            dm_ref[l, 16:17, :] = ctx
            db_ref[l:l + 1, :] = tot + ctx

    return pl.pallas_call(body, name=name, out_shape=[_sds((2, 32, w), F32), _sds((2, w), F32)],
                          in_specs=[VMEM_SPEC], out_specs=[VMEM_SPEC, VMEM_SPEC])(gath)


def cctx_grad(p, c_ctx, name):
    def body(a_ref, c_ref, o_ref):
        cv = c_ref[...]
        sg = _sigmoid(cv)
        o_ref[...] = 0.5 * (a_ref[0, 0:1, :] + a_ref[1, 0:1, :]) * (sg * (1.0 + cv * (1.0 - sg)))
    return pl.pallas_call(body, name=name, out_shape=_sds((1, D), F32), in_specs=[VMEM_SPEC] * 2,
                          out_specs=VMEM_SPEC)(p, c_ctx)


def loss_and_grad(lay, h, tgt, name):
    def fn(hb, tb):
        lat = (pl.program_id(0) % lay.bps) >= lay.cb
        e = jnp.where(lat, hb - tb, 0.0)
        return e * (1.0 / D), jnp.sum(e * e, axis=0, keepdims=True) * (0.5 / D)
    return rowwise(lay, name, fn, [h, tgt], outs=[(D, F32)], sums=[(1, D)])


def adamw(w, g, m, v, name):
    shape = w.shape
    w2, g2, m2, v2 = (t.reshape(-1, shape[-1]) for t in (w, g, m, v))
    rows, width = w2.shape
    tr = 256 if rows % 256 == 0 else rows
    c1 = 1.0 - ADAM_B1 ** ADAM_STEP
    c2 = 1.0 - ADAM_B2 ** ADAM_STEP

    def body(w_ref, g_ref, m_ref, v_ref, d_ref, mo_ref, vo_ref):
        gv = g_ref[...]
        mn = ADAM_B1 * m_ref[...] + (1.0 - ADAM_B1) * gv
        vn = ADAM_B2 * v_ref[...] + (1.0 - ADAM_B2) * (gv * gv)
        d_ref[...] = -ADAM_LR * ((mn / c1) / (jnp.sqrt(vn / c2) + ADAM_EPS) + ADAM_WD * w_ref[...])
        mo_ref[...] = mn
        vo_ref[...] = vn

    spec = pl.BlockSpec((tr, width), lambda i: (i, 0))
    d, mn, vn = pl.pallas_call(body, name=name, out_shape=[_sds((rows, width), F32)] * 3, grid=(rows // tr,),
                               in_specs=[spec] * 4, out_specs=[spec] * 3, compiler_params=_cp(("parallel",)))(w2, g2, m2, v2)
    return d.reshape(shape), mn.reshape(shape), vn.reshape(shape)


def adamw_ffn(w, m, v, red, kind, ns, name):
    shape = w.shape
    w2, m2, v2 = (t.reshape(-1, shape[-1]) for t in (w, m, v))
    rows, width = w2.shape
    c1 = 1.0 - ADAM_B1 ** ADAM_STEP
    c2 = 1.0 - ADAM_B2 ** ADAM_STEP
    tr, nb = ns // 2, 2
    gspec = pl.BlockSpec((tr, D), lambda i: (((i // nb) * 3 + kind) * nb + i % nb, 0))

    def body(w_ref, g_ref, m_ref, v_ref, go_ref, d_ref, mo_ref, vo_ref):
        gv = g_ref[...]
        mn = ADAM_B1 * m_ref[...] + (1.0 - ADAM_B1) * gv
        vn = ADAM_B2 * v_ref[...] + (1.0 - ADAM_B2) * (gv * gv)
        go_ref[...] = gv
        d_ref[...] = -ADAM_LR * ((mn / c1) / (jnp.sqrt(vn / c2) + ADAM_EPS) + ADAM_WD * w_ref[...])
        mo_ref[...] = mn
        vo_ref[...] = vn

    spec = pl.BlockSpec((tr, width), lambda i: (i, 0))
    outs = pl.pallas_call(body, name=name, out_shape=[_sds((rows, width), F32)] * 4, grid=(rows // tr,),
                          in_specs=[spec, gspec, spec, spec], out_specs=[spec] * 4,
                          compiler_params=_cp(("parallel",)))(w2, red, m2, v2)
    return tuple(t.reshape(shape) for t in outs)


def mod_mm(sc, w_mod, bias, name):
    wm = w_mod.shape[-1]
    tn = _pick(wm, (768, 512, 384, 256, 128))

    def body(a_ref, b_ref, c_ref, o_ref):
        o_ref[...] = _nn(a_ref[...], b_ref[...].astype(BF16)) + c_ref[...]

    return pl.pallas_call(
        body, name=name, out_shape=_sds((DEPTH, 32, wm), F32), grid=(DEPTH, wm // tn),
        in_specs=[pl.BlockSpec((32, D), lambda l, j: (0, 0)), pl.BlockSpec((None, D, tn), lambda l, j: (l, 0, j)),
                  pl.BlockSpec((None, 1, tn), lambda l, j: (l, 0, j))],
        out_specs=pl.BlockSpec((None, 32, tn), lambda l, j: (l, 0, j)),
        compiler_params=_cp(("parallel", "parallel")))(sc, w_mod, bias)


def wmod_dw(sc, dcol, name):
    wm = dcol.shape[-1]
    tm = 256

    def body(a_ref, b_ref, o_ref):
        o_ref[...] = _tn(a_ref[...], b_ref[...].astype(BF16))

    return pl.pallas_call(
        body, name=name, out_shape=_sds((DEPTH, D, wm), F32), grid=(DEPTH, D // tm),
        in_specs=[pl.BlockSpec((32, tm), lambda l, i: (0, i)), pl.BlockSpec((None, 32, wm), lambda l, i: (l, 0, 0))],
        out_specs=pl.BlockSpec((None, tm, wm), lambda l, i: (l, i, 0)),
        compiler_params=_cp(("parallel", "parallel")))(sc, dcol)


def cctx_dx(drow, w_mod, name):
    wm = w_mod.shape[-1]

    def body(a_ref, b_ref, o_ref):
        o_ref[...] = _nt(a_ref[...].astype(BF16), b_ref[...].astype(BF16))

    return pl.pallas_call(
        body, name=name, out_shape=_sds((DEPTH, 16, D), F32), grid=(DEPTH,),
        in_specs=[pl.BlockSpec((None, 16, wm), lambda l: (l, 0, 0)), pl.BlockSpec((None, D, wm), lambda l: (l, 0, 0))],
        out_specs=pl.BlockSpec((None, 16, D), lambda l: (l, 0, 0)), compiler_params=_cp(("parallel",), VMEM_BIG))(drow, w_mod)


HEAD_PERM = (0, 4, 1, 5, 2, 6, 3, 7)


def _rot_rows(wt):
    return jnp.concatenate([-wt[32:64], wt[0:32]], axis=0)


def _unrot_rows(g):
    return jnp.concatenate([g[32:64], -g[0:32]], axis=0)


def _heads(a, n):
    return [a[64 * i:64 * (i + 1)] for i in range(n)]


def kernel(x, c, ctx, c_ctx, w_mod, b_mod, ln_g, ln_b, ffn_w_gate, ffn_w_up, ffn_w_down, mix_ab_w_in, attn_sink, pool_w, pool_scale, mix_ab_w_out, lru_w_in, lru_conv_w, lru_conv_b, lru_wa, lru_ba, lru_wx, lru_bx, lru_lambda, lru_w_out, loss_target, m_c_ctx, m_w_mod, m_b_mod, m_ln_g, m_ln_b, m_ffn_w_gate, m_ffn_w_up, m_ffn_w_down, m_mix_ab_w_in, m_attn_sink, m_pool_w, m_pool_scale, m_mix_ab_w_out, m_lru_w_in, m_lru_conv_w, m_lru_conv_b, m_lru_wa, m_lru_ba, m_lru_wx, m_lru_bx, m_lru_lambda, m_lru_w_out, v_c_ctx, v_w_mod, v_b_mod, v_ln_g, v_ln_b, v_ffn_w_gate, v_ffn_w_up, v_ffn_w_down, v_mix_ab_w_in, v_attn_sink, v_pool_w, v_pool_scale, v_mix_ab_w_out, v_lru_w_in, v_lru_conv_w, v_lru_conv_b, v_lru_wa, v_lru_ba, v_lru_wx, v_lru_bx, v_lru_lambda, v_lru_w_out):
    n_lat, n_ctx = x.shape[1], ctx.shape[1]
    lay = Layout(n_ctx, n_lat)
    T = lay.T
    ns = ffn_w_gate.shape[-1]
    n_li, n_ai = lru_w_in.shape[-1], mix_ab_w_in.shape[-1]
    n_ao, n_lo = mix_ab_w_out.shape[1], lru_w_out.shape[1]
    wm = w_mod.shape[-1]
    dsh = ln_g.shape[-1]
    mx, my, mc = lax.axis_index("x"), lax.axis_index("y"), lax.axis_index("c")
    chip = 2 * mx + my
    me = 2 * chip + mc

    c_all = all_gather8(c, "ag8_c").reshape(16, D)
    cc = jnp.concatenate([c_all, c_ctx[None, :], jnp.zeros((15, D), F32)], axis=0)
    sc = silu_rows(cc, "silu_c")
    bias = lax.dynamic_slice(b_mod, (0, chip * wm), (DEPTH, wm)).reshape(DEPTH, 1, wm)
    modg = all_gather_chips(mod_mm(sc, w_mod, bias, "mod_mm"), "ag_mod")
    modtab = []
    for l in range(DEPTH):
        full = jnp.transpose(modg[:, l], (1, 0, 2)).reshape(32, N_CHIP * wm)
        mine = lax.dynamic_slice(full, (2 * me, 0), (2, N_CHIP * wm))
        modtab.append(jnp.concatenate([mine, full[16:17]], axis=0).reshape(3, N_MOD, D))

    small = jnp.concatenate([ln_g.reshape(6, dsh), ln_b.reshape(6, dsh), lru_conv_w[0], lru_conv_b, lru_ba[0],
                             lru_bx[0], lru_lambda[0], jnp.zeros((9, dsh), F32)], axis=0)
    small = all_gather_chips(small.reshape(2, 16, dsh), "ag_small").reshape(N_CHIP, 32, dsh)
    small = jnp.transpose(small, (1, 0, 2)).reshape(32, D)
    ln_g_f, ln_b_f = small[0:6].reshape(2, 3, D), small[6:12].reshape(2, 3, D)
    conv_w_f, conv_b_f = small[12:16], small[16:17]
    lru_vec = small[17:23]

    hh = 3 * ns // 2
    gate_t, up_t = jnp.swapaxes(ffn_w_gate, -1, -2), jnp.swapaxes(ffn_w_up, -1, -2)
    extra = [0, n_ai + n_ao, n_li + n_lo, 0]
    placed = [ffn_place(gate_t, up_t, ffn_w_down, g // 2, g % 2, f"ag_ffn{g}_place", extra[g]) for g in range(4)]
    placed[1] = place_rows(placed[1], jnp.concatenate([mix_ab_w_in[0].T, mix_ab_w_out[0]], axis=0).astype(BF16), 3 * ns,
                           "ag_mixa_place")
    placed[2] = place_rows(placed[2], jnp.concatenate([lru_w_in[0].T, lru_w_out[0]], axis=0).astype(BF16), 3 * ns,
                           "ag_mixc_place")
    placed = [p.reshape(N_CHIP, 2, p.shape[1] // 2, D) for p in placed]
    wb = [gather_placed(placed[0], "ag_ffn0"), None, None, None]
    mixw = {}

    def mixa_w():
        if "a" not in mixw:
            full = wb[1].reshape(N_CHIP, -1, D)
            ab_in_t = full[:, 3 * ns:3 * ns + n_ai].reshape(N_CHIP * n_ai, D)
            ab_out = full[:, 3 * ns + n_ai:].reshape(N_CHIP * n_ao, D)
            qh, kh = _heads(ab_in_t[Q0:K0], N_HEADS), _heads(ab_in_t[K0:V0], N_KV)
            w_ext_t = jnp.concatenate([qh[h] for h in HEAD_PERM] + [ab_in_t[K0:QR0]]
                                      + [_rot_rows(qh[h]) for h in HEAD_PERM] + [_rot_rows(t) for t in kh], axis=0)
            oh = _heads(ab_out[0:ATT_W], N_HEADS)
            mixw["a"] = (w_ext_t, jnp.concatenate([oh[h] for h in HEAD_PERM] + [ab_out[ATT_W:]], axis=0))
        return mixw["a"]

    def mixc_w():
        if "c" not in mixw:
            full = wb[2].reshape(N_CHIP, -1, D)
            mixw["c"] = (full[:, 3 * ns:3 * ns + n_li].reshape(N_CHIP * n_li, D),
                         full[:, 3 * ns + n_li:].reshape(N_CHIP * n_lo, D))
        return mixw["c"]

    t = jnp.arange(n_lat)
    inv = ROPE_THETA ** (-jnp.arange(16, dtype=F32) / 16.0)
    ang = jnp.concatenate([(t // GRID_W).astype(F32)[:, None] * inv, (t % GRID_W).astype(F32)[:, None] * inv], axis=-1)
    cos1 = jnp.concatenate([jnp.ones((n_ctx, 32), F32), jnp.cos(ang)], axis=0)
    sin1 = jnp.concatenate([jnp.zeros((n_ctx, 32), F32), jnp.sin(ang)], axis=0)
    cos_t = jnp.tile(cos1, (2, 4))
    sin_t = jnp.tile(sin1, (2, 4))
    sk = attn_sink[0]
    sink_tab = jnp.concatenate([jnp.repeat(jnp.stack([sk[:4], sk[4:]], axis=1), HEAD_DIM, axis=1),
                                jnp.zeros((4, 128), F32)], axis=0)
    pscale = pool_scale.reshape(1, POOL_W)

    h0 = jnp.concatenate([ctx, x], axis=1).reshape(T, D)
    tgt = loss_target.reshape(2 * n_lat, D)

    def lnv(l, j):
        return jnp.stack([ln_g_f[l, j], ln_b_f[l, j]])

    subs = [(0, 0, 0.5, 0), (0, 3, 1.0, 1), (0, 6, 0.5, 2), (1, 0, 0.5, 0), (1, 3, 1.0, 1), (1, 6, 0.5, 2)]

    def ffn_core(hm, l, f):
        tag = f"l{l}f{f}"
        gi = 2 * l + f
        w = wb[gi].reshape(N_CHIP, -1, D)
        if gi == 3:
            sp, sl, u, a = ffn_up(lay, hm, w, 0, 1, ns, f"ffn_up_{tag}")
            (y,) = slab_nn_acc(lay, [a], w, [2], ns, f"ffn_down_{tag}")
            return y, dict(sp=sp, sl=sl, u=u, a=a, nbuf=None)
        sp, sl, u, a, nbuf = ffn_up(lay, hm, w, 0, 1, ns, f"ffn_up_{tag}", rider=rider_gather_xy(placed[gi + 1]))
        y, nbuf = slab_nn_acc(lay, [a], w, [2], ns, f"ffn_down_{tag}", rider=rider_gather_fwd(nbuf))
        return y, dict(sp=sp, sl=sl, u=u, a=a, nbuf=nbuf)

    def mixa_core(hm):
        p = mm_nt(hm, mixa_w()[0], "mixa_in")
        qr, kr, vb, u = rope_fwd(lay, p, cos_t, sin_t, "rope")
        att, lse = attn_fwd(lay, qr, kr, vb, sink_tab, "attn")
        pool = pool_fwd(lay, u, pool_w[0], pscale, "pool")
        cat = jnp.concatenate([att, pool], axis=1)
        return mm_nn(cat, mixa_w()[1], "mixa_out"), dict(qr=qr, kr=kr, vb=vb, u=u, lse=lse, cat=cat)

    def mixc_core(hm):
        p = mm_nt(hm, mixc_w()[0], "mixc_in")
        uc = conv_fwd(lay, p, D, conv_w_f, conv_b_f, "conv")
        a, b = lru_coeffs(lay, uc, lru_wa[0], lru_wx[0], lru_vec, "lru_coef")
        s = lru_scan(lay, a, b, "lru_scan")
        o = lru_gate(lay, p, s, "lru_gate")
        return mm_nn(o, mixc_w()[1], "mixc_out"), dict(p=p, uc=uc, a=a, s=s, o=o)

    recs = []
    h = h0
    hm = modulate(lay, h0, modtab[0], 0, 1, "mod_first")
    for k, (l, k0, coef, j) in enumerate(subs):
        if k0 == 3:
            y, core = mixa_core(hm) if l == 0 else mixc_core(hm)
        else:
            y, core = ffn_core(hm, l, k0 // 6)
        nxt = None if k == 5 else (modtab[subs[k + 1][0]], subs[k + 1][1], subs[k + 1][1] + 1)
        nbuf = core.pop("nbuf", None)
        res = resid_ln(lay, h, y, modtab[l], k0 + 2, coef, lnv(l, j), f"ln_s{k}", nxt=nxt,
                       rider=None if nbuf is None else rider_gather_d2d(nbuf))
        if nbuf is not None:
            wb[2 * l + k0 // 6 + 1] = res[-1]
        recs.append(dict(h=h, hm=hm, y=y, xhat=res[1], rstd=res[2], **core))
        h = res[0]
        hm = res[3] if nxt is not None else None

    dout, lparts = loss_and_grad(lay, h, tgt, "loss")
    loss = lax.psum(jnp.sum(lparts), ("x", "y", "c"))

    dln = {}
    dms = {}
    mixg = {}
    ffn_red = [lax.empty((4, 2, hh, D), F32)]
    mix_red = {}
    pending = []

    def rs_sib(p):
        return None if p is None else rider_reduce_sib(p["buf"])

    def rs_add2(p, recv):
        p["q"] = add_own_half(p["buf"], recv, BF16, f"rs_add2_{p['key']}")

    def rs_join(p, arr):
        if isinstance(p["key"], int):
            return rider_join(sum_slots(p["q"], arr, f"rs_add4_{p['key']}", dst=ffn_red[0], g=p["key"]), p["key"])
        return rider_join(sum_slots(p["q"], arr, f"rs_add4_{p['key']}"))

    def rs_done(p, joined):
        if isinstance(p["key"], int):
            ffn_red[0] = joined
        else:
            mix_red[p["key"]] = joined.reshape(-1, D)

    def ffn_core_bwd(dy, r, l, f):
        tag = f"l{l}f{f}"
        gi = 2 * l + f
        w = wb[gi].reshape(N_CHIP, -1, D)
        p = pending.pop() if pending else None
        gb = lax.empty((N_CHIP, 3 * ns, D), F32)
        if p is None:
            dg, du = ffn_bwd_da(lay, dy, w, 2, r["sp"], r["sl"], r["u"], ns, f"ffn_da_{tag}")
            (gb,) = slab_tn(lay, r["a"], dy, gb, 2, ns, f"ffn_dwd_{tag}")
            (gb,) = slab_tn(lay, dg, r["hm"], gb, 0, ns, f"ffn_dwg_{tag}")
            (gb,) = slab_tn(lay, du, r["hm"], gb, 1, ns, f"ffn_dwu_{tag}")
            (dhm,) = slab_nn_acc(lay, [dg, du], w, [0, 1], ns, f"ffn_dh_{tag}")
        else:
            dg, du, recv = ffn_bwd_da(lay, dy, w, 2, r["sp"], r["sl"], r["u"], ns, f"ffn_da_{tag}", rider=rs_sib(p))
            rs_add2(p, recv)
            gb, arr = slab_tn(lay, r["a"], dy, gb, 2, ns, f"ffn_dwd_{tag}", rider=rider_reduce_copy(p["q"], 0))
            gb, arr = slab_tn(lay, dg, r["hm"], gb, 0, ns, f"ffn_dwg_{tag}", rider=rider_reduce_copy(p["q"], 1, arr))
            gb, arr = slab_tn(lay, du, r["hm"], gb, 1, ns, f"ffn_dwu_{tag}", rider=rider_reduce_copy(p["q"], 2, arr))
            dhm, joined = slab_nn_acc(lay, [dg, du], w, [0, 1], ns, f"ffn_dh_{tag}", rider=rs_join(p, arr))
            rs_done(p, joined)
        pending.append(dict(buf=gb.reshape(N_CHIP, 2, hh, D), key=gi))
        return dhm

    def mixc_core_bwd(dy, r):
        p = pending.pop() if pending else None
        w_in_t, w_out = mixc_w()
        if p is None:
            do_c = mm_nt(dy, w_out, "mixc_out_dx")
        else:
            do_c, recv = mm_nt(dy, w_out, "mixc_out_dx", rider=rs_sib(p))
            rs_add2(p, recv)
        g_out = mm_tn(r["o"], dy, "mixc_out_dw")
        dgate, dyg = lru_gate_bwd(lay, r["p"], r["s"], do_c, "lru_gate_b")
        da_c, db_c = lru_scan_bwd(lay, r["a"], r["s"], dyg, "lru_scan_b")
        res = lru_coeffs_bwd(lay, r["uc"], lru_wa[0], lru_wx[0], lru_vec, da_c, db_c, "lru_coef_b",
                             rider=None if p is None else rider_reduce_copies(p["q"]))
        duc, mixg["wa"], mixg["wx"], mixg["vec"] = res[:4]
        du_c, mixg["cw"], mixg["cb"] = conv_bwd(lay, r["p"], D, conv_w_f, duc, "conv_b")
        dp_c = jnp.concatenate([dgate, du_c], axis=1)
        if p is None:
            g_in_t = mm_tn(dp_c, r["hm"], "mixc_in_dw")
        else:
            g_in_t, joined = mm_tn(dp_c, r["hm"], "mixc_in_dw", rider=rs_join(p, res[4]))
            rs_done(p, joined)
        buf = jnp.concatenate([g_in_t.reshape(N_CHIP, n_li, D), g_out.reshape(N_CHIP, n_lo, D)], axis=1)
        pending.append(dict(buf=buf.reshape(N_CHIP, 2, (n_li + n_lo) // 2, D), key="c"))
        return mm_nn(dp_c, w_in_t, "mixc_in_dx")

    def mixa_core_bwd(dy, r):
        p = pending.pop() if pending else None
        w_ext_t, w_out_ext = mixa_w()
        if p is None:
            dcat = mm_nt(dy, w_out_ext, "mixa_out_dx")
        else:
            dcat, recv = mm_nt(dy, w_out_ext, "mixa_out_dx", rider=rs_sib(p))
            rs_add2(p, recv)
        g_out_ext = mm_tn(r["cat"], dy, "mixa_out_dw")
        res = attn_bwd(lay, r["qr"], r["kr"], r["vb"], sink_tab, r["lse"], dcat, "attn_b",
                       rider=None if p is None else rider_reduce_copies(p["q"]))
        dqr, dkr, dv, mixg["sink"] = res[:4]
        du_a, mixg["pw"], mixg["ps"] = pool_bwd(lay, r["u"], dcat, pool_w[0], pscale, "pool_b")
        dp_a = rope_bwd(lay, dqr, dkr, dv, du_a, cos_t, sin_t, "rope_b")
        if p is None:
            g_ext_t = mm_tn(dp_a, r["hm"], "mixa_in_dw")
        else:
            g_ext_t, joined = mm_tn(dp_a, r["hm"], "mixa_in_dw", rider=rs_join(p, res[4]))
            rs_done(p, joined)
        gq, gqr = _heads(g_ext_t[Q0:K0], N_HEADS), _heads(g_ext_t[QR0:KR0], N_HEADS)
        g_q = [None] * N_HEADS
        for i, h in enumerate(HEAD_PERM):
            g_q[h] = gq[i] + _unrot_rows(gqr[i])
        gk = [a + _unrot_rows(b) for a, b in zip(_heads(g_ext_t[K0:V0], N_KV), _heads(g_ext_t[KR0:PEXT], N_KV))]
        g_ab_in_t = jnp.concatenate(g_q + gk + [g_ext_t[V0:QR0]], axis=0)
        go = _heads(g_out_ext[0:ATT_W], N_HEADS)
        g_o = [None] * N_HEADS
        for i, h in enumerate(HEAD_PERM):
            g_o[h] = go[i]
        g_ab_out = jnp.concatenate(g_o + [g_out_ext[ATT_W:]], axis=0)
        buf = jnp.concatenate([g_ab_in_t.reshape(N_CHIP, n_ai, D), g_ab_out.reshape(N_CHIP, n_ao, D)], axis=1)
        pending.append(dict(buf=buf.reshape(N_CHIP, 2, (n_ai + n_ao) // 2, D), key="a"))
        return mm_nn(dp_a, w_ext_t, "mixa_in_dx")

    l, k0, coef, j = subs[5]
    dy, dres, s1 = ln_bwd(lay, dout, recs[5]["xhat"], recs[5]["rstd"], recs[5]["y"], modtab[l], k0 + 2, coef, lnv(l, j),
                          "lnb_s5")
    for k in range(5, -1, -1):
        l, k0, coef, j = subs[k]
        r = recs[k]
        if k0 == 3:
            dhm = mixa_core_bwd(dy, r) if l == 0 else mixc_core_bwd(dy, r)
        else:
            dhm = ffn_core_bwd(dy, r, l, k0 // 6)
        dln[(l, j)] = block_sums(lay, s1, f"bs_ln_s{k}")
        if k > 0:
            lp, k0p, coefp, jp = subs[k - 1]
            rp = recs[k - 1]
            dy, dres, s1, s2 = modb_lnb(lay, dres, dhm, r["h"], modtab[l], k0 + 1, rp["xhat"], rp["rstd"], rp["y"],
                                        modtab[lp], k0p + 2, coefp, lnv(lp, jp), f"modb_lnb_s{k}")
        else:
            gx, s2 = mod_bwd(lay, dres, dhm, r["h"], modtab[l], k0 + 1, "modb_s0")
        dms[(l, k0)] = block_sums(lay, s2, f"bs_mod_s{k}")
    grad_x = gx.reshape(2, n_lat, D)
    g_wa, g_wx, g_vec, g_cw, g_cb, g_sink, g_pw, g_ps = (mixg[n] for n in ("wa", "wx", "vec", "cw", "cb", "sink", "pw", "ps"))

    rows = []
    for l in range(DEPTH):
        per_k = []
        for k0, j in ((0, 0), (3, 1), (6, 2)):
            per_k += [dms[(l, k0)][:3, 0], dms[(l, k0)][:3, 1], dln[(l, j)][:3, 2]]
        rows.append(jnp.stack(per_k, axis=1).reshape(3, N_MOD * D))
    dmod_loc = jnp.concatenate(rows + [jnp.zeros((2, N_MOD * D), F32)], axis=0)
    dmod_all, g_b_mod = mod_grad_rows(all_gather8(dmod_loc, "ag8_dmod"), "dmod_rows")
    dcol = lax.dynamic_slice(dmod_all, (0, 0, chip * wm), (DEPTH, 32, wm))
    g_w_mod = wmod_dw(sc, dcol, "wmod_dw")
    g_cctx = cctx_grad(cctx_dx(dcol[:, 16:32], w_mod, "cctx_dx"), c_ctx[None, :], "cctx_grad")

    g_ln_g =jnp.stack([jnp.stack([dln[(l, j)][3, 1] for j in range(3)]) for l in range(DEPTH)])
    g_ln_b = jnp.stack([jnp.stack([dln[(l, j)][3, 0] for j in range(3)]) for l in range(DEPTH)])
    sink_row = jnp.sum(g_sink, axis=0)[:4]
    g_sink8 = jnp.concatenate([sink_row[:, 0], sink_row[:, HEAD_DIM]])
    misc = jnp.concatenate([g_sink8, jnp.sum(g_ps, axis=0).reshape(POOL_W), jnp.zeros((D - 8 - POOL_W,), F32)])
    small_g = jnp.concatenate([
        g_ln_g.reshape(6, D), g_ln_b.reshape(6, D), jnp.sum(g_cw, axis=0), jnp.sum(g_cb, axis=0), g_vec,
        misc[None, :], jnp.sum(g_pw, axis=0).reshape(64, D), g_wa.reshape(256, D), g_wx.reshape(256, D), g_cctx,
        jnp.zeros((39, D), F32)], axis=0)
    n_small = small_g.shape[0] // N_CHIP
    last = pending.pop()
    ffn_red = reduce_scatter_chips(last["buf"], f"ffn{last['key']}", wire=BF16, dst=ffn_red[0],
                                   g=last["key"]).reshape(12 * ns, D)
    small_red = reduce_scatter_chips(small_g.reshape(N_CHIP, 2, n_small // 2, D), "small")
    small_red = all_gather_chips(small_red, "ag_smallg").reshape(N_CHIP * n_small, D)

    ffn_kind = dict(ffn_w_gate=0, ffn_w_up=1, ffn_w_down=2)

    def cols(a):
        return lax.dynamic_slice_in_dim(a, chip * dsh, dsh, axis=a.ndim - 1)

    sr = small_red
    grads = dict(
        c_ctx=sr[600], w_mod=g_w_mod, b_mod=g_b_mod,
        ln_g=cols(sr[0:6]).reshape(2, 3, dsh), ln_b=cols(sr[6:12]).reshape(2, 3, dsh),
        mix_ab_w_in=mix_red["a"][0:n_ai][None], attn_sink=sr[23, 0:8][None], pool_w=sr[24:88].reshape(1, 4, 128, 128),
        pool_scale=sr[23, 8:8 + POOL_W][None], mix_ab_w_out=mix_red["a"][n_ai:][None],
        lru_w_in=mix_red["c"][0:n_li].T[None],
        lru_conv_w=cols(sr[12:16])[None], lru_conv_b=cols(sr[16:17]), lru_wa=sr[88:344].reshape(1, 2, 8, 128, 128),
        lru_ba=cols(sr[17:19])[None], lru_wx=sr[344:600].reshape(1, 2, 8, 128, 128), lru_bx=cols(sr[19:21])[None],
        lru_lambda=cols(sr[21:23])[None], lru_w_out=mix_red["c"][n_li:][None])
    params = dict(c_ctx=(c_ctx, m_c_ctx, v_c_ctx), w_mod=(w_mod, m_w_mod, v_w_mod), b_mod=(b_mod, m_b_mod, v_b_mod),
                  ln_g=(ln_g, m_ln_g, v_ln_g), ln_b=(ln_b, m_ln_b, v_ln_b),
                  ffn_w_gate=(ffn_w_gate, m_ffn_w_gate, v_ffn_w_gate), ffn_w_up=(ffn_w_up, m_ffn_w_up, v_ffn_w_up),
                  ffn_w_down=(ffn_w_down, m_ffn_w_down, v_ffn_w_down),
                  mix_ab_w_in=(mix_ab_w_in, m_mix_ab_w_in, v_mix_ab_w_in), attn_sink=(attn_sink, m_attn_sink, v_attn_sink),
                  pool_w=(pool_w, m_pool_w, v_pool_w), pool_scale=(pool_scale, m_pool_scale, v_pool_scale),
                  mix_ab_w_out=(mix_ab_w_out, m_mix_ab_w_out, v_mix_ab_w_out), lru_w_in=(lru_w_in, m_lru_w_in, v_lru_w_in),
                  lru_conv_w=(lru_conv_w, m_lru_conv_w, v_lru_conv_w), lru_conv_b=(lru_conv_b, m_lru_conv_b, v_lru_conv_b),
                  lru_wa=(lru_wa, m_lru_wa, v_lru_wa), lru_ba=(lru_ba, m_lru_ba, v_lru_ba), lru_wx=(lru_wx, m_lru_wx, v_lru_wx),
                  lru_bx=(lru_bx, m_lru_bx, v_lru_bx), lru_lambda=(lru_lambda, m_lru_lambda, v_lru_lambda),
                  lru_w_out=(lru_w_out, m_lru_w_out, v_lru_w_out))
    gl, dl, ml, vl = [], [], [], []
    transposed = ("ffn_w_gate", "ffn_w_up", "mix_ab_w_in")
    for name, (w, m, v) in params.items():
        if name in transposed:
            w, m, v = (jnp.swapaxes(t, -1, -2) for t in (w, m, v))
        if name in ffn_kind:
            g, d, mn, vn = adamw_ffn(w, m, v, ffn_red, ffn_kind[name], ns, f"adamw_{name}")
        else:
            g = grads[name].reshape(w.shape)
            d, mn, vn = adamw(w, g, m, v, f"adamw_{name}")
        if name in transposed:
            g, d, mn, vn = (jnp.swapaxes(t, -1, -2) for t in (g, d, mn, vn))
        gl.append(g)
        dl.append(d)
        ml.append(mn)
        vl.append(vn)
    return (loss, grad_x, *gl, *dl, *ml, *vl)
```

```python
import functools
import math

import jax
import jax.numpy as jnp
from jax import lax
from jax.experimental import pallas as pl
from jax.experimental.pallas import tpu as pltpu

F32, BF16 = jnp.float32, jnp.bfloat16
MESH = pl.DeviceIdType.MESH
ANY = pl.BlockSpec(memory_space=pl.ANY)
VMEM_SPEC = pl.BlockSpec(memory_space=pltpu.VMEM)

D = 1024
N_CHIP = 4
HEAD_DIM, N_HEADS, N_KV = 64, 8, 2
ATT_W, KV_W, POOL_W = 512, 128, 512
POOL_WINDOWS = (2, 4, 8, 16)
BLK = 128
ATT_SCALE = HEAD_DIM ** -0.5
ROPE_THETA = 10000.0
GRID_W = 64
LRU_C = 8.0
LN_EPS = 1e-5
NEG_INF = -1e30
DEPTH = 2
ALPHA = (2 * DEPTH) ** 0.25
N_MOD = 9
ADAM_LR, ADAM_B1, ADAM_B2, ADAM_EPS, ADAM_WD, ADAM_STEP = 0.001, 0.9, 0.999, 1e-08, 0.01, 10
VMEM_BIG = 48 * 1024 * 1024


def _cp(sem=None, vmem=None):
    kw = {}
    if sem is not None:
        kw["dimension_semantics"] = sem
    if vmem is not None:
        kw["vmem_limit_bytes"] = vmem
    return pltpu.CompilerParams(**kw)


def _sds(shape, dtype):
    return jax.ShapeDtypeStruct(tuple(shape), dtype)


def _pick(n, cands):
    for c in cands:
        if n % c == 0:
            return c
    return n


def _dot(a, b, dims):
    return lax.dot_general(a, b, (dims, ((), ())), preferred_element_type=F32)


def _nn(a, b):
    return _dot(a, b, ((1,), (0,)))


def _nt(a, b):
    return _dot(a, b, ((1,), (1,)))


def _tn(a, b):
    return _dot(a, b, ((0,), (0,)))


def _sigmoid(x):
    return 0.5 * jnp.tanh(0.5 * x) + 0.5


def _me():
    return lax.axis_index("x"), lax.axis_index("y"), lax.axis_index("c")


def _rcopy(src, dst, ssem, rsem, dev):
    return pltpu.make_async_remote_copy(src_ref=src, dst_ref=dst, send_sem=ssem, recv_sem=rsem,
                                        device_id=dev, device_id_type=MESH)


def all_gather8(x, name):
    def body(x_ref, o_ref, ssem, rsem, lsem):
        mx, my, mc = _me()
        me = 4 * mx + 2 * my + mc
        loc = pltpu.make_async_copy(x_ref, o_ref.at[me], lsem)
        loc.start()
        peers = []
        for m in range(1, 8):
            px = 1 - mx if (m >> 2) & 1 else mx
            py = 1 - my if (m >> 1) & 1 else my
            pc = 1 - mc if m & 1 else mc
            peers.append((px, py, pc))
        sends = [_rcopy(x_ref, o_ref.at[me], ssem.at[k], rsem.at[k], p) for k, p in enumerate(peers)]
        for cp in sends:
            cp.start()
        for k, (px, py, pc) in enumerate(peers):
            _rcopy(x_ref, o_ref.at[4 * px + 2 * py + pc], ssem.at[k], rsem.at[k], (px, py, pc)).wait_recv()
        for cp in sends:
            cp.wait_send()
        loc.wait()

    return pl.pallas_call(
        body, name=name, out_shape=_sds((8,) + x.shape, x.dtype),
        in_specs=[VMEM_SPEC], out_specs=VMEM_SPEC,
        scratch_shapes=[pltpu.SemaphoreType.DMA((7,)), pltpu.SemaphoreType.DMA((7,)), pltpu.SemaphoreType.DMA],
    )(x)


_ROW_BLOCKS = (512, 384, 352, 256, 224, 128)


def _idx(v):
    return jnp.reshape(v, (1,)).astype(jnp.int32)


def place_slab(shard, name):
    _, h, w = shard.shape
    th = _pick(h, _ROW_BLOCKS)

    def body(s_ref, x_ref, o_ref):
        del s_ref
        o_ref[...] = x_ref[...]

    return pl.pallas_call(
        body, name=name, out_shape=_sds((N_CHIP,) + shard.shape, shard.dtype),
        grid_spec=pltpu.PrefetchScalarGridSpec(
            num_scalar_prefetch=1, grid=(2, h // th),
            in_specs=[pl.BlockSpec((None, th, w), lambda k, r, s: (k, r, 0))],
            out_specs=pl.BlockSpec((None, None, th, w), lambda k, r, s: (s[0], k, r, 0))),
    )(_idx(2 * lax.axis_index("x") + lax.axis_index("y")), shard)


def place_rows(buf, rows, r0, name):
    e, w = rows.shape
    tb = 64

    def body(s_ref, x_ref, b_ref, o_ref):
        del s_ref, b_ref
        o_ref[...] = x_ref[...]

    return pl.pallas_call(
        body, name=name, out_shape=_sds(buf.shape, buf.dtype),
        grid_spec=pltpu.PrefetchScalarGridSpec(
            num_scalar_prefetch=1, grid=(e // tb,),
            in_specs=[pl.BlockSpec((tb, w), lambda j, s: (j, 0)), ANY],
            out_specs=pl.BlockSpec((None, tb, w), lambda j, s: (s[0], r0 // tb + j, 0))),
        input_output_aliases={2: 0},
    )(_idx(2 * lax.axis_index("x") + lax.axis_index("y")), rows, buf)


def ffn_place(w_gate_t, w_up_t, w_down, l, f, name, extra=0):
    ns = w_down.shape[-2]
    tr, nb = ns // 2, 2

    def body(s_ref, g_ref, u_ref, d_ref, o_ref):
        del s_ref
        k = pl.program_id(0)

        @pl.when(k == 0)
        def _():
            o_ref[...] = g_ref[...].astype(BF16)

        @pl.when(k == 1)
        def _():
            o_ref[...] = u_ref[...].astype(BF16)

        @pl.when(k == 2)
        def _():
            o_ref[...] = d_ref[...].astype(BF16)

    def spec(q):
        return pl.BlockSpec((None, None, tr, D), lambda k, j, s: (l, f, jnp.where(k == q, j, 0), 0))

    return pl.pallas_call(
        body, name=name, out_shape=_sds((N_CHIP, 3 * ns + extra, D), BF16),
        grid_spec=pltpu.PrefetchScalarGridSpec(
            num_scalar_prefetch=1, grid=(3, nb), in_specs=[spec(0), spec(1), spec(2)],
            out_specs=pl.BlockSpec((None, tr, D), lambda k, j, s: (s[0], k * nb + j, 0))),
    )(_idx(2 * lax.axis_index("x") + lax.axis_index("y")), w_gate_t, w_up_t, w_down)


def all_gather_chips(shard, name):
    return gather_placed(place_slab(shard, name + "_place"), name)


def gather_placed(full, name):
    def body(x_ref, o_ref, ssem, rsem):
        del x_ref
        mx, my, mc = _me()
        s = 2 * mx + my
        sib = (mx, my, 1 - mc)
        chips = [(1 - mx, my), (mx, 1 - my), (1 - mx, 1 - my)]
        first = [_rcopy(o_ref.at[s, mc], o_ref.at[s, mc], ssem.at[j], rsem.at[j], (px, py, mc))
                 for j, (px, py) in enumerate(chips)]
        for cp in first:
            cp.start()
        passed = []
        for j, (px, py) in enumerate(chips):
            ps = 2 * px + py
            _rcopy(o_ref.at[ps, mc], o_ref.at[ps, mc], ssem.at[j], rsem.at[j], (px, py, mc)).wait_recv()
            fw = _rcopy(o_ref.at[ps, mc], o_ref.at[ps, mc], ssem.at[3 + j], rsem.at[3 + j], sib)
            fw.start()
            passed.append(fw)
        for j, (px, py) in enumerate(chips):
            ps = 2 * px + py
            _rcopy(o_ref.at[ps, 1 - mc], o_ref.at[ps, 1 - mc], ssem.at[3 + j], rsem.at[3 + j], sib).wait_recv()
        for cp in first + passed:
            cp.wait_send()

    return pl.pallas_call(
        body, name=name, out_shape=_sds(full.shape, full.dtype), in_specs=[ANY], out_specs=ANY,
        input_output_aliases={0: 0},
        scratch_shapes=[pltpu.SemaphoreType.DMA((6,)), pltpu.SemaphoreType.DMA((6,))],
    )(full)


def sibling_send_other_half(buf, name):
    def body(x_ref, o_ref, ssem, rsem):
        mx, my, mc = _me()
        sib = (mx, my, 1 - mc)
        cps = [_rcopy(x_ref.at[k, 1 - mc], o_ref.at[k], ssem.at[k], rsem.at[k], sib) for k in range(N_CHIP)]
        for cp in cps:
            cp.start()
        for cp in cps:
            cp.wait_recv()
        for cp in cps:
            cp.wait_send()

    n, _, h, w = buf.shape
    return pl.pallas_call(
        body, name=name, out_shape=_sds((n, h, w), buf.dtype), in_specs=[ANY], out_specs=ANY,
        scratch_shapes=[pltpu.SemaphoreType.DMA((N_CHIP,)), pltpu.SemaphoreType.DMA((N_CHIP,))],
    )(buf)


def chips_all_to_all(q, name):
    def body(x_ref, o_ref, ssem, rsem):
        mx, my, mc = _me()
        s = 2 * mx + my
        chips = [(1 - mx, my), (mx, 1 - my), (1 - mx, 1 - my)]
        cps = [_rcopy(x_ref.at[2 * px + py], o_ref.at[s], ssem.at[j], rsem.at[j], (px, py, mc))
               for j, (px, py) in enumerate(chips)]
        for cp in cps:
            cp.start()
        for j, (px, py) in enumerate(chips):
            ps = 2 * px + py
            _rcopy(x_ref.at[ps], o_ref.at[ps], ssem.at[j], rsem.at[j], (px, py, mc)).wait_recv()
        for cp in cps:
            cp.wait_send()

    return pl.pallas_call(
        body, name=name, out_shape=_sds(q.shape, q.dtype), in_specs=[ANY], out_specs=ANY,
        scratch_shapes=[pltpu.SemaphoreType.DMA((3,)), pltpu.SemaphoreType.DMA((3,))],
    )(q)


def sibling_join_halves(both, name, g=None):
    def body(x_ref, o_ref, ssem, rsem):
        del x_ref
        mx, my, mc = _me()
        sib = (mx, my, 1 - mc)
        o = o_ref if g is None else o_ref.at[g]
        cp = _rcopy(o.at[mc], o.at[mc], ssem, rsem, sib)
        cp.start()
        _rcopy(o.at[1 - mc], o.at[1 - mc], ssem, rsem, sib).wait_recv()
        cp.wait_send()

    return pl.pallas_call(
        body, name=name, out_shape=_sds(both.shape, both.dtype), in_specs=[ANY], out_specs=ANY,
        input_output_aliases={0: 0}, scratch_shapes=[pltpu.SemaphoreType.DMA, pltpu.SemaphoreType.DMA],
    )(both)


def add_own_half(buf, recv, wire, name):
    n, _, h, w = buf.shape
    th = _pick(h, _ROW_BLOCKS)

    def body(c_ref, a_ref, b_ref, o_ref):
        del c_ref
        o_ref[...] = (a_ref[...] + b_ref[...]).astype(o_ref.dtype)

    return pl.pallas_call(
        body, name=name, out_shape=_sds((n, h, w), wire),
        grid_spec=pltpu.PrefetchScalarGridSpec(
            num_scalar_prefetch=1, grid=(n, h // th),
            in_specs=[pl.BlockSpec((None, None, th, w), lambda k, r, c: (k, c[0], r, 0)),
                      pl.BlockSpec((None, th, w), lambda k, r, c: (k, r, 0))],
            out_specs=pl.BlockSpec((None, th, w), lambda k, r, c: (k, r, 0))),
    )(_idx(lax.axis_index("c")), buf, recv)


def sum_slots(q, r, name, dst=None, g=None):
    n, h, w = r.shape
    th = _pick(h, _ROW_BLOCKS)

    def body(i_ref, q_ref, r1, r2, r3, *rest):
        del i_ref
        rest[-1][...] = ((q_ref[...].astype(F32) + r1[...].astype(F32)) + r2[...].astype(F32)) + r3[...].astype(F32)

    def slot(d):
        return lambda i, ix: ((ix[0] + d) % N_CHIP, i, 0)

    idx = jnp.stack([2 * lax.axis_index("x") + lax.axis_index("y"), lax.axis_index("c")]).astype(jnp.int32)
    in_specs = [pl.BlockSpec((None, th, w), slot(d)) for d in (0, 1, 2, 3)]
    if dst is None:
        return pl.pallas_call(
            body, name=name, out_shape=_sds((2, h, w), F32),
            grid_spec=pltpu.PrefetchScalarGridSpec(
                num_scalar_prefetch=1, grid=(h // th,), in_specs=in_specs,
                out_specs=pl.BlockSpec((None, th, w), lambda i, ix: (ix[1], i, 0))),
        )(idx, q, r, r, r)
    return pl.pallas_call(
        body, name=name, out_shape=_sds(dst.shape, F32),
        grid_spec=pltpu.PrefetchScalarGridSpec(
            num_scalar_prefetch=1, grid=(h // th,), in_specs=in_specs + [ANY],
            out_specs=pl.BlockSpec((None, None, th, w), lambda i, ix: (g, ix[1], i, 0))),
        input_output_aliases={5: 0},
    )(idx, q, r, r, r, dst)


def reduce_scatter_chips(buf, tag, wire=F32, dst=None, g=None):
    recv = sibling_send_other_half(buf, f"rs_sib_{tag}")
    q = add_own_half(buf, recv, wire, f"rs_add2_{tag}")
    r = chips_all_to_all(q, f"rs_a2a_{tag}")
    red = sum_slots(q, r, f"rs_add4_{tag}", dst=dst, g=g)
    return sibling_join_halves(red, f"rs_join_{tag}", g=g)


class Layout:
    def __init__(self, n_ctx, n_lat):
        self.C, self.L = n_ctx, n_lat
        self.PS = n_ctx + n_lat
        self.T = 2 * self.PS
        self.tr = _pick(math.gcd(n_ctx, n_lat), (256, 128))
        self.bps = self.PS // self.tr
        self.cb = n_ctx // self.tr
        self.nblk = self.T // self.tr
        self.tm = _pick(self.T, (1152, 768, 512, 256, 128))
        self.tm2 = _pick(self.T, (2304, 1152, 768, 512, 256, 128))
        self.tc = _pick(self.T, (512, 256, 128))

    def seg(self, i):
        return jnp.where(i % self.bps < self.cb, 2, i // self.bps)


def rowwise(lay, name, fn, rows, segs=(), vecs=(), outs=(), sums=(), rider=None):
    tr, nblk = lay.tr, lay.nblk
    n_r, n_s, n_v, n_o = len(rows), len(segs), len(vecs), len(outs)
    lat_only = any(o[2:] for o in outs) or any(a.shape[0] != lay.T for a in rows)
    nsub = 1 if lat_only or nblk % 2 else 2
    tb = tr * nsub

    def body(*refs):
        ins = refs[:n_r + n_s + n_v]
        ors = refs[n_r + n_s + n_v:]
        for sub in range(nsub):
            rs = slice(sub * tr, (sub + 1) * tr)
            seg = lay.seg(pl.program_id(0) * nsub + sub)
            vals = [r[rs, :] for r in ins[:n_r]] + [r[seg] for r in ins[n_r:n_r + n_s]] + [r[...] for r in ins[n_r + n_s:]]
            res = fn(*vals)
            for k in range(n_o):
                ors[k][rs, :] = res[k].astype(ors[k].dtype)
            for k in range(len(sums)):
                ors[n_o + k][sub] = res[n_o + k]

    def all_rows(i):
        return (i, 0)

    def lat_rows(i):
        return ((i // lay.bps) * (lay.bps - lay.cb) + jnp.maximum(i % lay.bps - lay.cb, 0), 0)

    in_specs = [pl.BlockSpec((tb, a.shape[1]), all_rows if a.shape[0] == lay.T else lat_rows) for a in rows]
    in_specs += [pl.BlockSpec(a.shape, lambda i: (0, 0, 0)) for a in segs]
    in_specs += [pl.BlockSpec(a.shape, lambda i: (0, 0)) for a in vecs]
    out_shape = [_sds((2 * lay.L if o[2:] else lay.T, o[0]), o[1]) for o in outs]
    out_shape += [_sds((nblk, r, w), F32) for r, w in sums]
    out_specs = [pl.BlockSpec((tb, o[0]), lat_rows if o[2:] else all_rows) for o in outs]
    out_specs += [pl.BlockSpec((nsub, r, w), lambda i: (i, 0, 0)) for r, w in sums]
    sem = "arbitrary" if any(o[2:] for o in outs) else "parallel"
    if rider is None:
        return pl.pallas_call(body, name=name, out_shape=out_shape, grid=(nblk // nsub,), in_specs=in_specs,
                              out_specs=out_specs, compiler_params=_cp((sem,), VMEM_BIG))(*rows, *segs, *vecs)
    return _host_call(body, rider, name, (nblk // nsub,), in_specs, out_specs, out_shape, (*rows, *segs, *vecs), (sem,),
                      n_r + n_s + n_v, n_o + len(sums))


def modulate(lay, h, mod, k_shift, k_scale, name):
    def fn(hb, m):
        return (hb * (1.0 + m[k_scale:k_scale + 1]) + m[k_shift:k_shift + 1],)
    return rowwise(lay, name, fn, [h], segs=[mod], outs=[(D, BF16)])[0]


def resid_ln(lay, h, y, mod, k_gate, coef, lnv, name, nxt=None, rider=None):
    def fn(hb, yb, m, *rest):
        ln = rest[-1]
        z = ALPHA * hb + (coef * m[k_gate:k_gate + 1]) * yb
        mu = jnp.mean(z, axis=-1, keepdims=True)
        zc = z - mu
        var = jnp.mean(zc * zc, axis=-1, keepdims=True)
        rstd = lax.rsqrt(var + LN_EPS)
        xhat = zc * rstd
        out = xhat * ln[0:1] + ln[1:2]
        if nxt is None:
            return out, xhat, rstd
        mn = rest[0]
        return out, xhat, rstd, out * (1.0 + mn[nxt[2]:nxt[2] + 1]) + mn[nxt[1]:nxt[1] + 1]
    segs = [mod] if nxt is None else [mod, nxt[0]]
    outs = [(D, F32), (D, F32), (1, F32)] + ([] if nxt is None else [(D, BF16)])
    return rowwise(lay, name, fn, [h, y], segs=segs, vecs=[lnv], outs=outs, rider=rider)


def _ln_bwd_math(do, xh, rs, yb, gate, coef, ln):
    dxh = do * ln[0:1]
    m1 = jnp.mean(dxh, axis=-1, keepdims=True)
    m2 = jnp.mean(dxh * xh, axis=-1, keepdims=True)
    dz = rs * (dxh - m1 - xh * m2)
    s = jnp.concatenate([jnp.sum(do, axis=0, keepdims=True), jnp.sum(do * xh, axis=0, keepdims=True),
                         jnp.sum(coef * dz * yb, axis=0, keepdims=True)], axis=0)
    return (coef * gate) * dz, ALPHA * dz, s


def _mod_bwd_math(dr, dm, hb, scale):
    s = jnp.concatenate([jnp.sum(dm, axis=0, keepdims=True), jnp.sum(dm * hb, axis=0, keepdims=True)], axis=0)
    return dr + dm * (1.0 + scale), s


def ln_bwd(lay, dout, xhat, rstd, y, mod, k_gate, coef, lnv, name):
    def fn(do, xh, rs, yb, m, ln):
        return _ln_bwd_math(do, xh, rs, yb, m[k_gate:k_gate + 1], coef, ln)
    return rowwise(lay, name, fn, [dout, xhat, rstd, y], segs=[mod], vecs=[lnv],
                   outs=[(D, BF16), (D, F32)], sums=[(3, D)])


def mod_bwd(lay, dres, dhm, h, mod, k_scale, name, rider=None):
    def fn(dr, dm, hb, m):
        return _mod_bwd_math(dr, dm, hb, m[k_scale:k_scale + 1])
    return rowwise(lay, name, fn, [dres, dhm, h], segs=[mod], outs=[(D, F32, "lat")], sums=[(2, D)], rider=rider)


def modb_lnb(lay, dres, dhm, h, mod, k_scale, xhat, rstd, y, mod_p, k_gate, coef, lnv, name, rider=None):
    def fn(dr, dm, hb, xh, rs, yb, m, mp, ln):
        dh, s2 = _mod_bwd_math(dr, dm, hb, m[k_scale:k_scale + 1])
        dy, dres_p, s1 = _ln_bwd_math(dh, xh, rs, yb, mp[k_gate:k_gate + 1], coef, ln)
        return dy, dres_p, s1, s2
    return rowwise(lay, name, fn, [dres, dhm, h, xhat, rstd, y], segs=[mod, mod_p], vecs=[lnv],
                   outs=[(D, BF16), (D, F32)], sums=[(3, D), (2, D)], rider=rider)


def block_sums(lay, parts, name):
    nblk, r, w = parts.shape

    def body(p_ref, o_ref):
        acc = [None, None, None]
        for i in range(nblk):
            sg = 2 if i % lay.bps < lay.cb else i // lay.bps
            acc[sg] = p_ref[i] if acc[sg] is None else acc[sg] + p_ref[i]
        for k in range(3):
            o_ref[k] = acc[k]
        o_ref[3] = (acc[0] + acc[1]) + acc[2]

    return pl.pallas_call(body, name=name, out_shape=_sds((4, r, w), F32), in_specs=[VMEM_SPEC],
                          out_specs=VMEM_SPEC)(parts)


def mm_nn(a, b, name, out_dtype=F32, bias=None):
    m, k = a.shape
    n = b.shape[1]
    tm = _pick(m, (1152, 768, 512, 256, 128, 64, 32, 16, 8))
    tn = _pick(n, (1024, 768, 640, 512, 384, 256, 128))

    def body(*refs):
        if bias is None:
            a_ref, b_ref, o_ref = refs
            o_ref[...] = _nn(a_ref[...].astype(BF16), b_ref[...].astype(BF16)).astype(o_ref.dtype)
        else:
            a_ref, b_ref, c_ref, o_ref = refs
            o_ref[...] = (_nn(a_ref[...].astype(BF16), b_ref[...].astype(BF16)) + c_ref[...]).astype(o_ref.dtype)

    in_specs = [pl.BlockSpec((tm, k), lambda i, j: (i, 0)), pl.BlockSpec((k, tn), lambda i, j: (0, j))]
    ops = [a, b]
    if bias is not None:
        in_specs.append(pl.BlockSpec((1, tn), lambda i, j: (0, j)))
        ops.append(bias)
    return pl.pallas_call(body, name=name, out_shape=_sds((m, n), out_dtype), grid=(m // tm, n // tn),
                          in_specs=in_specs, out_specs=pl.BlockSpec((tm, tn), lambda i, j: (i, j)),
                          compiler_params=_cp(("parallel", "parallel"), VMEM_BIG))(*ops)


def mm_nt(a, b, name, out_dtype=F32, rider=None):
    m, k = a.shape
    n = b.shape[0]
    tm = _pick(m, (1152, 768, 512, 256, 128, 64, 32, 16, 8))
    tn = _pick(n, (1024, 768, 640, 512, 384, 256, 128))

    def body(a_ref, b_ref, o_ref):
        o_ref[...] = _nt(a_ref[...].astype(BF16), b_ref[...].astype(BF16)).astype(o_ref.dtype)

    res = _host_call(body, rider, name, (m // tm, n // tn),
                     [pl.BlockSpec((tm, k), lambda i, j: (i, 0)), pl.BlockSpec((tn, k), lambda i, j: (j, 0))],
                     [pl.BlockSpec((tm, tn), lambda i, j: (i, j))], [_sds((m, n), out_dtype)], (a, b),
                     ("parallel", "parallel"), 2, 1)
    return res[0] if rider is None else res


def mm_tn(a, b, name, rider=None):
    t, m = a.shape
    n = b.shape[1]
    tk = _pick(t, (1152, 768, 512, 256, 128, 64, 32, 16))
    tm = _pick(m, (512, 384, 256, 128))

    def body(a_ref, b_ref, o_ref):
        @pl.when(pl.program_id(1) == 0)
        def _():
            o_ref[...] = jnp.zeros_like(o_ref)
        o_ref[...] += _tn(a_ref[...].astype(BF16), b_ref[...].astype(BF16))

    res = _host_call(body, rider, name, (m // tm, t // tk),
                     [pl.BlockSpec((tk, tm), lambda i, k: (k, i)), pl.BlockSpec((tk, n), lambda i, k: (k, 0))],
                     [pl.BlockSpec((tm, n), lambda i, k: (i, 0))], [_sds((m, n), F32)], (a, b),
                     ("parallel", "arbitrary"), 2, 1)
    return res[0] if rider is None else res


class Rider:
    def __init__(self, ins, outs, aliases, nsem, start, wait):
        self.ins, self.outs, self.aliases, self.nsem, self.start, self.wait = ins, outs, aliases, nsem, start, wait


def _chips_of(mx, my):
    return [(1 - mx, my), (mx, 1 - my), (1 - mx, 1 - my)]


def rider_gather_d2d(buf):
    def start(ins, outs, ssem, rsem):
        o = outs[0]
        mx, my, mc = _me()
        for j, (px, py) in enumerate(_chips_of(mx, my)):
            ps = 2 * px + py
            _rcopy(o.at[ps, mc], o.at[ps, mc], ssem.at[j], rsem.at[j], (mx, my, 1 - mc)).start()

    def wait(ins, outs, ssem, rsem):
        o = outs[0]
        mx, my, mc = _me()
        sib = (mx, my, 1 - mc)
        for j, (px, py) in enumerate(_chips_of(mx, my)):
            ps = 2 * px + py
            _rcopy(o.at[ps, 1 - mc], o.at[ps, 1 - mc], ssem.at[j], rsem.at[j], sib).wait_recv()
        for j, (px, py) in enumerate(_chips_of(mx, my)):
            ps = 2 * px + py
            _rcopy(o.at[ps, mc], o.at[ps, mc], ssem.at[j], rsem.at[j], sib).wait_send()

    return Rider([buf], [_sds(buf.shape, buf.dtype)], {0: 0}, 3, start, wait)


def rider_reduce_sib(buf):
    n, _, h, w = buf.shape

    def start(ins, outs, ssem, rsem):
        mx, my, mc = _me()
        for k in range(N_CHIP):
            _rcopy(ins[0].at[k, 1 - mc], outs[0].at[k], ssem.at[k], rsem.at[k], (mx, my, 1 - mc)).start()

    def wait(ins, outs, ssem, rsem):
        mx, my, mc = _me()
        for k in range(N_CHIP):
            _rcopy(ins[0].at[k, 1 - mc], outs[0].at[k], ssem.at[k], rsem.at[k], (mx, my, 1 - mc)).wait_recv()
        for k in range(N_CHIP):
            _rcopy(ins[0].at[k, 1 - mc], outs[0].at[k], ssem.at[k], rsem.at[k], (mx, my, 1 - mc)).wait_send()

    return Rider([buf], [_sds((n, h, w), buf.dtype)], {}, N_CHIP, start, wait)


def rider_gather_xy(buf):
    def peers():
        mx, my, mc = _me()
        return 2 * mx + my, mc, [(1 - mx, my), (mx, 1 - my)]

    def start(ins, outs, ssem, rsem):
        o = outs[0]
        s, mc, nb = peers()
        for j, (px, py) in enumerate(nb):
            _rcopy(o.at[s, mc], o.at[s, mc], ssem.at[j], rsem.at[j], (px, py, mc)).start()

    def wait(ins, outs, ssem, rsem):
        o = outs[0]
        s, mc, nb = peers()
        for j, (px, py) in enumerate(nb):
            _rcopy(o.at[2 * px + py, mc], o.at[2 * px + py, mc], ssem.at[j], rsem.at[j], (px, py, mc)).wait_recv()
        for j, (px, py) in enumerate(nb):
            _rcopy(o.at[s, mc], o.at[s, mc], ssem.at[j], rsem.at[j], (px, py, mc)).wait_send()

    return Rider([buf], [_sds(buf.shape, buf.dtype)], {0: 0}, 2, start, wait)


def rider_gather_fwd(buf):
    h2 = buf.shape[2] // 2
    lo, hi = pl.ds(0, h2), pl.ds(h2, buf.shape[2] - h2)

    def start(ins, outs, ssem, rsem):
        o = outs[0]
        mx, my, mc = _me()
        xs, ys = 2 * (1 - mx) + my, 2 * mx + (1 - my)
        _rcopy(o.at[xs, mc, lo], o.at[xs, mc, lo], ssem.at[0], rsem.at[0], (mx, 1 - my, mc)).start()
        _rcopy(o.at[ys, mc, hi], o.at[ys, mc, hi], ssem.at[1], rsem.at[1], (1 - mx, my, mc)).start()

    def wait(ins, outs, ssem, rsem):
        o = outs[0]
        mx, my, mc = _me()
        xs, ys, ds = 2 * (1 - mx) + my, 2 * mx + (1 - my), 2 * (1 - mx) + (1 - my)
        _rcopy(o.at[ds, mc, lo], o.at[ds, mc, lo], ssem.at[0], rsem.at[0], (mx, 1 - my, mc)).wait_recv()
        _rcopy(o.at[ds, mc, hi], o.at[ds, mc, hi], ssem.at[1], rsem.at[1], (1 - mx, my, mc)).wait_recv()
        _rcopy(o.at[xs, mc, lo], o.at[xs, mc, lo], ssem.at[0], rsem.at[0], (mx, 1 - my, mc)).wait_send()
        _rcopy(o.at[ys, mc, hi], o.at[ys, mc, hi], ssem.at[1], rsem.at[1], (1 - mx, my, mc)).wait_send()

    return Rider([buf], [_sds(buf.shape, buf.dtype)], {0: 0}, 2, start, wait)


def rider_reduce_copy(q, j, r=None):
    def peer():
        mx, my, mc = _me()
        px, py = _chips_of(mx, my)[j]
        return 2 * mx + my, 2 * px + py, (px, py, mc)

    def start(ins, outs, ssem, rsem):
        s, ps, dev = peer()
        _rcopy(ins[0].at[ps], outs[0].at[s], ssem.at[0], rsem.at[0], dev).start()

    def wait(ins, outs, ssem, rsem):
        s, ps, dev = peer()
        _rcopy(ins[0].at[ps], outs[0].at[ps], ssem.at[0], rsem.at[0], dev).wait_recv()
        _rcopy(ins[0].at[ps], outs[0].at[s], ssem.at[0], rsem.at[0], dev).wait_send()

    if r is None:
        return Rider([q], [_sds(q.shape, q.dtype)], {}, 1, start, wait)
    return Rider([q, r], [_sds(q.shape, q.dtype)], {1: 0}, 1, start, wait)


def rider_reduce_copies(q):
    def start(ins, outs, ssem, rsem):
        mx, my, mc = _me()
        s = 2 * mx + my
        for j, (px, py) in enumerate(_chips_of(mx, my)):
            _rcopy(ins[0].at[2 * px + py], outs[0].at[s], ssem.at[j], rsem.at[j], (px, py, mc)).start()

    def wait(ins, outs, ssem, rsem):
        mx, my, mc = _me()
        s = 2 * mx + my
        for j, (px, py) in enumerate(_chips_of(mx, my)):
            ps = 2 * px + py
            _rcopy(ins[0].at[ps], outs[0].at[ps], ssem.at[j], rsem.at[j], (px, py, mc)).wait_recv()
        for j, (px, py) in enumerate(_chips_of(mx, my)):
            _rcopy(ins[0].at[2 * px + py], outs[0].at[s], ssem.at[j], rsem.at[j], (px, py, mc)).wait_send()

    return Rider([q], [_sds(q.shape, q.dtype)], {}, 3, start, wait)


def rider_join(buf, g=None):
    def start(ins, outs, ssem, rsem):
        o = outs[0] if g is None else outs[0].at[g]
        mx, my, mc = _me()
        _rcopy(o.at[mc], o.at[mc], ssem.at[0], rsem.at[0], (mx, my, 1 - mc)).start()

    def wait(ins, outs, ssem, rsem):
        o = outs[0] if g is None else outs[0].at[g]
        mx, my, mc = _me()
        _rcopy(o.at[1 - mc], o.at[1 - mc], ssem.at[0], rsem.at[0], (mx, my, 1 - mc)).wait_recv()
        _rcopy(o.at[mc], o.at[mc], ssem.at[0], rsem.at[0], (mx, my, 1 - mc)).wait_send()

    return Rider([buf], [_sds(buf.shape, buf.dtype)], {0: 0}, 1, start, wait)


def _host_call(body, rider, name, grid, in_specs, out_specs, out_shape, operands, sem, n_in, n_out, aliases=None):
    aliases = dict(aliases or {})
    if rider is None:
        return pl.pallas_call(body, name=name, out_shape=out_shape, grid=grid, in_specs=in_specs, out_specs=out_specs,
                              input_output_aliases=aliases, compiler_params=_cp(sem, VMEM_BIG))(*operands)
    n_ri, n_ro = len(rider.ins), len(rider.outs)
    aliases.update({n_in + a: n_out + b for a, b in rider.aliases.items()})

    def hosted(*refs):
        ins, r_in = refs[:n_in], refs[n_in:n_in + n_ri]
        outs, r_out = refs[n_in + n_ri:n_in + n_ri + n_out], refs[n_in + n_ri + n_out:n_in + n_ri + n_out + n_ro]
        ssem, rsem = refs[-2], refs[-1]
        first = functools.reduce(lambda a, b: a & b, [pl.program_id(k) == 0 for k in range(len(grid))])
        last = functools.reduce(lambda a, b: a & b, [pl.program_id(k) == grid[k] - 1 for k in range(len(grid))])

        @pl.when(first)
        def _():
            rider.start(r_in, r_out, ssem, rsem)
        body(*ins, *outs)

        @pl.when(last)
        def _():
            rider.wait(r_in, r_out, ssem, rsem)

    return pl.pallas_call(
        hosted, name=name, out_shape=list(out_shape) + list(rider.outs), grid=grid,
        in_specs=list(in_specs) + [ANY] * n_ri, out_specs=list(out_specs) + [ANY] * n_ro,
        input_output_aliases=aliases,
        scratch_shapes=[pltpu.SemaphoreType.DMA((rider.nsem,)), pltpu.SemaphoreType.DMA((rider.nsem,))],
        compiler_params=_cp(("arbitrary",) * len(grid), VMEM_BIG))(*operands, *rider.ins)


def ffn_up(lay, hm, wbuf, ig, iu, ns, name, rider=None):
    tm = lay.tm

    def body(h_ref, wg_ref, wu_ref, sp_ref, sl_ref, u_ref, a_ref):
        hb = h_ref[...]
        g = _nt(hb, wg_ref[0])
        u = _nt(hb, wu_ref[0])
        sg = _sigmoid(g)
        sl = g * sg
        sp_ref[0] = (sg + sl * (1.0 - sg)).astype(BF16)
        sl_ref[0] = sl.astype(BF16)
        u_ref[0] = u.astype(BF16)
        a_ref[0] = (sl * u).astype(BF16)

    spec_o = pl.BlockSpec((1, tm, ns), lambda s, i: (s, i, 0))
    return _host_call(
        body, rider, name, (N_CHIP, lay.T // tm),
        [pl.BlockSpec((tm, D), lambda s, i: (i, 0)), pl.BlockSpec((1, ns, D), lambda s, i: (s, ig, 0)),
         pl.BlockSpec((1, ns, D), lambda s, i: (s, iu, 0))],
        [spec_o] * 4, [_sds((N_CHIP, lay.T, ns), BF16)] * 4, (hm, wbuf, wbuf), ("parallel", "parallel"), 3, 4)


def slab_nn_acc(lay, zs, wbuf, idxs, ns, name, rider=None):
    tm = lay.tm2
    npair = len(zs)

    def body(*refs):
        o_ref = refs[-1]

        @pl.when(pl.program_id(1) == 0)
        def _():
            o_ref[...] = jnp.zeros_like(o_ref)
        acc = _nn(refs[0][0], refs[npair][0])
        for p in range(1, npair):
            acc += _nn(refs[p][0], refs[npair + p][0])
        o_ref[...] += acc

    in_specs = [pl.BlockSpec((1, tm, ns), lambda i, s: (s, i, 0)) for _ in zs]
    in_specs += [pl.BlockSpec((1, ns, D), functools.partial(lambda i, s, q: (s, q, 0), q=q)) for q in idxs]
    return _host_call(body, rider, name, (lay.T // tm, N_CHIP), in_specs, [pl.BlockSpec((tm, D), lambda i, s: (i, 0))],
                      [_sds((lay.T, D), F32)], (*zs, *([wbuf] * npair)), ("parallel", "arbitrary"), 2 * npair, 1)


def ffn_bwd_da(lay, dy, wbuf, idn, sp, sl, u, ns, name, rider=None):
    tm = lay.tm

    def body(dy_ref, wd_ref, sp_ref, sl_ref, u_ref, dg_ref, du_ref):
        da = _nt(dy_ref[...], wd_ref[0])
        dg_ref[0] = (da * u_ref[0].astype(F32) * sp_ref[0].astype(F32)).astype(BF16)
        du_ref[0] = (da * sl_ref[0].astype(F32)).astype(BF16)

    spec_z = pl.BlockSpec((1, tm, ns), lambda s, i: (s, i, 0))
    return _host_call(
        body, rider, name, (N_CHIP, lay.T // tm),
        [pl.BlockSpec((tm, D), lambda s, i: (i, 0)), pl.BlockSpec((1, ns, D), lambda s, i: (s, idn, 0)),
         spec_z, spec_z, spec_z],
        [spec_z] * 2, [_sds((N_CHIP, lay.T, ns), BF16)] * 2, (dy, wbuf, sp, sl, u), ("parallel", "parallel"), 5, 2)


def slab_tn(lay, z, x, gbuf, idx, ns, name, rider=None):
    tk = lay.tm2

    def body(z_ref, x_ref, g_in, o_ref):
        del g_in

        @pl.when(pl.program_id(1) == 0)
        def _():
            o_ref[...] = jnp.zeros_like(o_ref)
        o_ref[0] += _tn(z_ref[0], x_ref[...])

    return _host_call(
        body, rider, name, (N_CHIP, lay.T // tk),
        [pl.BlockSpec((1, tk, ns), lambda s, k: (s, k, 0)), pl.BlockSpec((tk, D), lambda s, k: (k, 0)), ANY],
        [pl.BlockSpec((1, ns, D), lambda s, k: (s, idx, 0))], [_sds(gbuf.shape, F32)], (z, x, gbuf),
        ("parallel", "arbitrary"), 3, 1, aliases={2: 0})


Q0, K0, V0, U0, QR0, KR0, PEXT = 0, 512, 640, 768, 1280, 1792, 1920


def rope_fwd(lay, p, cos, sin, name):
    def fn(pb, cs, sn):
        cs4 = jnp.concatenate([cs] * 4, axis=1)
        sn4 = jnp.concatenate([sn] * 4, axis=1)
        qr = pb[:, Q0:K0] * cs4 + pb[:, QR0:KR0] * sn4
        kr = pb[:, K0:V0] * cs + pb[:, KR0:PEXT] * sn
        return qr, kr, pb[:, V0:U0], pb[:, U0:QR0]
    return rowwise(lay, name, fn, [p, cos, sin], outs=[(ATT_W, BF16), (KV_W, BF16), (KV_W, BF16), (POOL_W, F32)])


def rope_bwd(lay, dqr, dkr, dv, du, cos, sin, name):
    def fn(dq, dk, dvb, dub, cs, sn):
        cs4 = jnp.concatenate([cs] * 4, axis=1)
        sn4 = jnp.concatenate([sn] * 4, axis=1)
        return (jnp.concatenate([dq * cs4, dk * cs, dvb, dub, dq * sn4, dk * sn], axis=1),)
    return rowwise(lay, name, fn, [dqr, dkr, dv, du, cos, sin], outs=[(PEXT, BF16)])[0]


def _attn_specs(lay):
    nbs, cbk, lbk = lay.PS // BLK, lay.C // BLK, lay.L // BLK

    def kv_map(j):
        return lambda s, n: (s * nbs + cbk + jnp.clip(n - cbk + j - 1, 0, lbk - 1), 0)

    win = [pl.BlockSpec((BLK, KV_W), kv_map(j)) for j in range(3)]
    ctx = pl.BlockSpec((lay.C, KV_W), lambda s, n: (s * (lay.PS // lay.C), 0))
    return nbs, cbk, lbk, win, ctx


def _attn_masks(n, cbk, lbk):
    row = lax.broadcasted_iota(jnp.int32, (BLK, BLK), 0)
    col = lax.broadcasted_iota(jnp.int32, (BLK, BLK), 1)
    m = n - cbk
    lat = n >= cbk
    valid = [lat & (m >= 1) & (col >= row), lat & (col >= 0), lat & (m <= lbk - 2) & (col <= row)]
    lane_lo = lax.broadcasted_iota(jnp.int32, (BLK, 2 * HEAD_DIM), 1) < HEAD_DIM
    return valid, lane_lo


def attn_fwd(lay, qr, kr, vb, sink_tab, name):
    nbs, cbk, lbk, win, ctx = _attn_specs(lay)

    def body(q_ref, k0, k1, k2, kc_ref, v0, v1, v2, vc_ref, sk_ref, o_ref, l_ref):
        n = pl.program_id(1)
        valid, lane_lo = _attn_masks(n, cbk, lbk)
        valid4 = [jnp.concatenate([v] * 4, axis=0) for v in valid]
        ks = [k0[...], k1[...], k2[...]]
        vs = [v0[...], v1[...], v2[...]]
        kc, vc = kc_ref[...], vc_ref[...]
        q2s = [q_ref[:, p * 128:(p + 1) * 128] for p in range(4)]
        outs, lses = [], []
        for hh in range(2):
            sel = lane_lo == (hh == 0)
            qm = jnp.concatenate([jnp.where(sel, q2, jnp.zeros_like(q2)) for q2 in q2s], axis=0)
            sk = jnp.concatenate([jnp.broadcast_to(sk_ref[p:p + 1, hh * HEAD_DIM:hh * HEAD_DIM + 1], (BLK, 1))
                                  for p in range(4)], axis=0)
            sw = [jnp.where(valid4[j], _nt(qm, ks[j]) * ATT_SCALE, NEG_INF) for j in range(3)]
            sc = _nt(qm, kc) * ATT_SCALE
            mx = jnp.maximum(jnp.maximum(jnp.maximum(sw[0].max(-1, keepdims=True), sw[1].max(-1, keepdims=True)),
                                         jnp.maximum(sw[2].max(-1, keepdims=True), sc.max(-1, keepdims=True))), sk)
            ew = [jnp.exp(s - mx) for s in sw]
            ec = jnp.exp(sc - mx)
            den = ew[0].sum(-1, keepdims=True) + ew[1].sum(-1, keepdims=True) + ew[2].sum(-1, keepdims=True)
            den = den + ec.sum(-1, keepdims=True) + jnp.exp(sk - mx)
            o = _nn((ec / den).astype(BF16), vc)
            for j in range(3):
                o += _nn((ew[j] / den).astype(BF16), vs[j])
            outs.append(o)
            lses.append(mx + jnp.log(den))
        for p in range(4):
            rows = slice(p * BLK, (p + 1) * BLK)
            o_ref[:, p * 128:(p + 1) * 128] = jnp.where(lane_lo, outs[0][rows], outs[1][rows]).astype(o_ref.dtype)
            l_ref[:, p * 128:(p + 1) * 128] = jnp.where(lane_lo, jnp.broadcast_to(lses[0][rows], (BLK, 128)),
                                                        jnp.broadcast_to(lses[1][rows], (BLK, 128)))

    qspec = pl.BlockSpec((BLK, ATT_W), lambda s, n: (s * nbs + n, 0))
    return pl.pallas_call(
        body, name=name, out_shape=[_sds((lay.T, ATT_W), BF16), _sds((lay.T, ATT_W), F32)], grid=(2, nbs),
        in_specs=[qspec] + win + [ctx] + win + [ctx] + [pl.BlockSpec((8, 128), lambda s, n: (0, 0))],
        out_specs=[qspec, qspec], compiler_params=_cp(("parallel", "parallel")))(qr, kr, kr, kr, kr, vb, vb, vb, vb, sink_tab)


def attn_bwd(lay, qr, kr, vb, sink_tab, lse, datt, name, rider=None):
    nbs, cbk, lbk, win, ctx = _attn_specs(lay)
    C, PS = lay.C, lay.PS

    def body(q_ref, k0, k1, k2, kc_ref, v0, v1, v2, vc_ref, sk_ref, l_ref, do_ref, dq_ref, dk_ref, dv_ref, ds_ref):
        n = pl.program_id(1)
        valid, lane_lo = _attn_masks(n, cbk, lbk)

        @pl.when(n == 0)
        def _():
            dk_ref[...] = jnp.zeros_like(dk_ref)
            dv_ref[...] = jnp.zeros_like(dv_ref)
            ds_ref[...] = jnp.zeros_like(ds_ref)

        ks = [k0[...], k1[...], k2[...], kc_ref[...]]
        vs = [v0[...], v1[...], v2[...], vc_ref[...]]
        valid4 = [jnp.concatenate([v] * 4, axis=0) for v in valid]
        dks = [jnp.zeros((BLK, KV_W), F32)] * 3 + [jnp.zeros((C, KV_W), F32)]
        dvs = list(dks)
        q2s = [q_ref[:, p * 128:(p + 1) * 128] for p in range(4)]
        do2s = [do_ref[:, p * 128:(p + 1) * 128].astype(BF16) for p in range(4)]
        lse2s = [l_ref[:, p * 128:(p + 1) * 128] for p in range(4)]
        dq_h, dd_h = [], []
        for hh in range(2):
            sel = lane_lo == (hh == 0)
            qm = jnp.concatenate([jnp.where(sel, q2, jnp.zeros_like(q2)) for q2 in q2s], axis=0)
            dom = jnp.concatenate([jnp.where(sel, d2, jnp.zeros_like(d2)) for d2 in do2s], axis=0)
            lse_h = jnp.concatenate([l2[:, hh * HEAD_DIM:hh * HEAD_DIM + 1] for l2 in lse2s], axis=0)
            ps, dps = [], []
            for j in range(4):
                s = _nt(qm, ks[j]) * ATT_SCALE
                if j < 3:
                    s = jnp.where(valid4[j], s, NEG_INF)
                ps.append(jnp.exp(s - lse_h))
                dps.append(_nt(dom, vs[j]))
            dd = (ps[0] * dps[0]).sum(-1, keepdims=True) + (ps[1] * dps[1]).sum(-1, keepdims=True)
            dd = dd + (ps[2] * dps[2]).sum(-1, keepdims=True) + (ps[3] * dps[3]).sum(-1, keepdims=True)
            dq = jnp.zeros((4 * BLK, 128), F32)
            for j in range(4):
                dsb = (ps[j] * (dps[j] - dd) * ATT_SCALE).astype(BF16)
                dq += _nn(dsb, ks[j])
                dks[j] = dks[j] + _tn(dsb, qm)
                dvs[j] = dvs[j] + _tn(ps[j].astype(BF16), dom)
            dq_h.append(dq)
            dd_h.append(dd)
        for p in range(4):
            sl = slice(p * 128, (p + 1) * 128)
            rows = slice(p * BLK, (p + 1) * BLK)
            dq_ref[:, sl] = jnp.where(lane_lo, dq_h[0][rows], dq_h[1][rows])
            dd2 = jnp.where(lane_lo, jnp.broadcast_to(dd_h[0][rows], (BLK, 128)), jnp.broadcast_to(dd_h[1][rows], (BLK, 128)))
            psink = jnp.exp(sk_ref[p:p + 1, :] - lse2s[p])
            ds_ref[0, p:p + 1, :] += -jnp.sum(psink * dd2, axis=0, keepdims=True)
        dk_ref[0:C, :] += dks[3]
        dv_ref[0:C, :] += dvs[3]
        for j in range(3):
            r0 = pl.multiple_of((cbk + jnp.clip(n - cbk + j - 1, 0, lbk - 1)) * BLK, BLK)
            dk_ref[pl.ds(r0, BLK), :] += dks[j]
            dv_ref[pl.ds(r0, BLK), :] += dvs[j]

    qspec = pl.BlockSpec((BLK, ATT_W), lambda s, n: (s * nbs + n, 0))
    kvout = pl.BlockSpec((PS, KV_W), lambda s, n: (s, 0))
    return _host_call(
        body, rider, name, (2, nbs),
        [qspec] + win + [ctx] + win + [ctx] + [pl.BlockSpec((8, 128), lambda s, n: (0, 0)), qspec, qspec],
        [qspec, kvout, kvout, pl.BlockSpec((1, 8, 128), lambda s, n: (s, 0, 0))],
        [_sds((lay.T, ATT_W), F32), _sds((lay.T, KV_W), F32), _sds((lay.T, KV_W), F32), _sds((2, 8, 128), F32)],
        (qr, kr, kr, kr, kr, vb, vb, vb, vb, sink_tab, lse, datt), ("parallel", "arbitrary"), 12, 4)


def _winsum(x, r):
    n = x.shape[0]
    t = lax.broadcasted_iota(jnp.int32, x.shape, 0)
    acc = x
    for o in range(1, r + 1):
        acc = acc + jnp.where(t >= o, pltpu.roll(x, o, 0), 0.0) + jnp.where(t < n - o, pltpu.roll(x, n - o, 0), 0.0)
    return acc


def _wincount(n, r):
    t = lax.broadcasted_iota(jnp.int32, (n, 128), 0)
    return (jnp.minimum(t + r, n - 1) - jnp.maximum(t - r, 0) + 1).astype(F32)


def pool_fwd(lay, u, w_pool, scale, name):
    segs = [(0, lay.C), (lay.C, lay.L)]

    def body(u_ref, w_ref, s_ref, o_ref):
        for r0, n in segs:
            for g, wd in enumerate(POOL_WINDOWS):
                sl = slice(g * 128, (g + 1) * 128)
                x = u_ref[r0:r0 + n, sl]
                d = _winsum(x, wd // 2) / _wincount(n, wd // 2) - x
                y = _nn(d.astype(BF16), w_ref[g].astype(BF16)) * s_ref[:, sl]
                o_ref[r0:r0 + n, sl] = y.astype(o_ref.dtype)

    spec = pl.BlockSpec((lay.PS, POOL_W), lambda s: (s, 0))
    return pl.pallas_call(
        body, name=name, out_shape=_sds((lay.T, POOL_W), BF16), grid=(2,),
        in_specs=[spec, pl.BlockSpec(w_pool.shape, lambda s: (0, 0, 0)), pl.BlockSpec((1, POOL_W), lambda s: (0, 0))],
        out_specs=spec, compiler_params=_cp(("parallel",), VMEM_BIG))(u, w_pool, scale)


def pool_bwd(lay, u, dcat, w_pool, scale, name):
    segs = [(0, lay.C), (lay.C, lay.L)]

    def body(u_ref, dp_ref, w_ref, s_ref, du_ref, dw_ref, dsc_ref):
        for g, wd in enumerate(POOL_WINDOWS):
            sl = slice(g * 128, (g + 1) * 128)
            wb = w_ref[g].astype(BF16)
            dw = jnp.zeros((128, 128), F32)
            dsc = jnp.zeros((1, 128), F32)
            for r0, n in segs:
                x = u_ref[r0:r0 + n, sl]
                cnt = _wincount(n, wd // 2)
                d = (_winsum(x, wd // 2) / cnt - x).astype(BF16)
                dp = dp_ref[r0:r0 + n, sl]
                dsc += jnp.sum(_nn(d, wb) * dp, axis=0, keepdims=True)
                dyp = (dp * s_ref[:, sl]).astype(BF16)
                dw += _tn(d, dyp)
                dd = _nt(dyp, wb)
                du_ref[r0:r0 + n, sl] = _winsum(dd / cnt, wd // 2) - dd
            dw_ref[0, g] = dw
            dsc_ref[0, :, sl] = dsc

    spec = pl.BlockSpec((lay.PS, POOL_W), lambda s: (s, 0))
    return pl.pallas_call(
        body, name=name,
        out_shape=[_sds((lay.T, POOL_W), F32), _sds((2, 4, 128, 128), F32), _sds((2, 1, POOL_W), F32)], grid=(2,),
        in_specs=[spec, pl.BlockSpec((lay.PS, POOL_W), lambda s: (s, 1)), pl.BlockSpec(w_pool.shape, lambda s: (0, 0, 0)),
                  pl.BlockSpec((1, POOL_W), lambda s: (0, 0))],
        out_specs=[spec, pl.BlockSpec((1, 4, 128, 128), lambda s: (s, 0, 0, 0)), pl.BlockSpec((1, 1, POOL_W), lambda s: (s, 0, 0))],
        compiler_params=_cp(("parallel",), VMEM_BIG))(u, dcat, w_pool, scale)


CONV_OFFS = (-1, 0, 1, 2)
CW = 256


def _shift_rows(x, o):
    if o == 0:
        return x
    n = x.shape[0]
    t = lax.broadcasted_iota(jnp.int32, x.shape, 0)
    if o < 0:
        return jnp.where(t >= -o, pltpu.roll(x, -o, 0), 0.0)
    return jnp.where(t < n - o, pltpu.roll(x, n - o, 0), 0.0)


def conv_fwd(lay, p, col0, w, b, name):
    segs = [(0, lay.C), (lay.C, lay.L)]
    cb0 = col0 // CW

    def body(x_ref, w_ref, b_ref, o_ref):
        for r0, n in segs:
            x = x_ref[r0:r0 + n, :]
            y = jnp.broadcast_to(b_ref[...], x.shape)
            for k, o in enumerate(CONV_OFFS):
                y = y + _shift_rows(x, o) * w_ref[k:k + 1, :]
            o_ref[r0:r0 + n, :] = y

    return pl.pallas_call(
        body, name=name, out_shape=_sds((lay.T, D), F32), grid=(2, D // CW),
        in_specs=[pl.BlockSpec((lay.PS, CW), lambda s, j: (s, cb0 + j)), pl.BlockSpec((4, CW), lambda s, j: (0, j)),
                  pl.BlockSpec((1, CW), lambda s, j: (0, j))],
        out_specs=pl.BlockSpec((lay.PS, CW), lambda s, j: (s, j)),
        compiler_params=_cp(("parallel", "parallel")))(p, w, b)


def conv_bwd(lay, p, col0, w, duc, name):
    segs = [(0, lay.C), (lay.C, lay.L)]
    cb0 = col0 // CW

    def body(x_ref, w_ref, g_ref, du_ref, dw_ref, db_ref):
        dws = [jnp.zeros((1, CW), F32)] * 4
        db = jnp.zeros((1, CW), F32)
        for r0, n in segs:
            x = x_ref[r0:r0 + n, :]
            g = g_ref[r0:r0 + n, :]
            du = jnp.zeros_like(g)
            for k, o in enumerate(CONV_OFFS):
                du = du + _shift_rows(g, -o) * w_ref[k:k + 1, :]
                dws[k] = dws[k] + jnp.sum(g * _shift_rows(x, o), axis=0, keepdims=True)
            db = db + jnp.sum(g, axis=0, keepdims=True)
            du_ref[r0:r0 + n, :] = du.astype(du_ref.dtype)
        dw_ref[0] = jnp.concatenate(dws, axis=0)
        db_ref[0] = db

    return pl.pallas_call(
        body, name=name, out_shape=[_sds((lay.T, D), BF16), _sds((2, 4, D), F32), _sds((2, 1, D), F32)], grid=(2, D // CW),
        in_specs=[pl.BlockSpec((lay.PS, CW), lambda s, j: (s, cb0 + j)), pl.BlockSpec((4, CW), lambda s, j: (0, j)),
                  pl.BlockSpec((lay.PS, CW), lambda s, j: (s, j))],
        out_specs=[pl.BlockSpec((lay.PS, CW), lambda s, j: (s, j)), pl.BlockSpec((1, 4, CW), lambda s, j: (s, 0, j)),
                   pl.BlockSpec((1, 1, CW), lambda s, j: (s, 0, j))],
        compiler_params=_cp(("parallel", "parallel")))(p, w, duc)


def _softplus_neg(lam):
    z = -lam
    w = jnp.exp(-jnp.abs(z))
    log1p = jnp.where(w < 1e-2, w * (1.0 - w * (0.5 - w / 3.0)), jnp.log(1.0 + w))
    return jnp.maximum(z, 0.0) + log1p, -_sigmoid(z)


def _neg_expm1(x):
    series = -x * (1.0 + x * (0.5 + x * (1.0 / 6.0 + x * (1.0 / 24.0 + x * (1.0 / 120.0)))))
    return jnp.where(x > -0.05, series, 1.0 - jnp.exp(x))


def _lru_gates(x, xb, wa, wx, ba, bx, lam):
    r = _sigmoid(_nn(xb, wa.astype(BF16)) + ba)
    gi = _sigmoid(_nn(xb, wx.astype(BF16)) + bx)
    sp, dsp = _softplus_neg(lam)
    la = -LRU_C * r * sp
    a = jnp.exp(la)
    sq = jnp.sqrt(_neg_expm1(2.0 * la))
    return r, gi, sp, dsp, a, sq


def lru_coeffs(lay, uc, wa, wx, vec, name):
    tr = lay.tc

    def body(x_ref, wa_ref, wx_ref, v_ref, a_ref, b_ref):
        for h in range(8):
            sl = slice(h * 128, (h + 1) * 128)
            x = x_ref[:, sl]
            xb = x.astype(BF16)
            for d in range(2):
                _, gi, _, _, a, sq = _lru_gates(x, xb, wa_ref[d, h], wx_ref[d, h], v_ref[d:d + 1, sl],
                                                v_ref[2 + d:3 + d, sl], v_ref[4 + d:5 + d, sl])
                a_ref[d, h] = a
                b_ref[d, h] = sq * (gi * x)

    wspec = pl.BlockSpec((2, 8, 128, 128), lambda i: (0, 0, 0, 0))
    ospec = pl.BlockSpec((2, 8, tr, 128), lambda i: (0, 0, i, 0))
    return pl.pallas_call(
        body, name=name, out_shape=[_sds((2, 8, lay.T, 128), F32)] * 2, grid=(lay.T // tr,),
        in_specs=[pl.BlockSpec((tr, D), lambda i: (i, 0)), wspec, wspec, pl.BlockSpec((6, D), lambda i: (0, 0))],
        out_specs=[ospec, ospec], compiler_params=_cp(("parallel",), VMEM_BIG))(uc, wa, wx, vec)


def lru_coeffs_bwd(lay, uc, wa, wx, vec, da, db, name, rider=None):
    tr = lay.tc

    def body(x_ref, wa_ref, wx_ref, v_ref, da_ref, db_ref, dx_ref, dwa_ref, dwx_ref, dv_ref):
        @pl.when(pl.program_id(0) == 0)
        def _():
            dwa_ref[...] = jnp.zeros_like(dwa_ref)
            dwx_ref[...] = jnp.zeros_like(dwx_ref)
            dv_ref[...] = jnp.zeros_like(dv_ref)

        for h in range(8):
            sl = slice(h * 128, (h + 1) * 128)
            x = x_ref[:, sl]
            xb = x.astype(BF16)
            dx = jnp.zeros_like(x)
            for d in range(2):
                wab, wxb = wa_ref[d, h].astype(BF16), wx_ref[d, h].astype(BF16)
                r, gi, sp, dsp, a, sq = _lru_gates(x, xb, wa_ref[d, h], wx_ref[d, h], v_ref[d:d + 1, sl],
                                                   v_ref[2 + d:3 + d, sl], v_ref[4 + d:5 + d, sl])
                dbv, dav = db_ref[d, h], da_ref[d, h]
                t1 = dbv * sq
                dgi = t1 * x
                dx = dx + t1 * gi
                dla = dav * a - (dbv * gi * x) * (a * a) / sq
                dr = dla * (-LRU_C * sp)
                dlam = jnp.sum(dla * (-LRU_C * r), axis=0, keepdims=True) * dsp
                dpa = dr * r * (1.0 - r)
                dpx = dgi * gi * (1.0 - gi)
                dpab, dpxb = dpa.astype(BF16), dpx.astype(BF16)
                dwa_ref[d, h] += _tn(xb, dpab)
                dwx_ref[d, h] += _tn(xb, dpxb)
                dx = dx + _nt(dpab, wab) + _nt(dpxb, wxb)
                dv_ref[d:d + 1, sl] += jnp.sum(dpa, axis=0, keepdims=True)
                dv_ref[2 + d:3 + d, sl] += jnp.sum(dpx, axis=0, keepdims=True)
                dv_ref[4 + d:5 + d, sl] += dlam
            dx_ref[:, sl] = dx

    wspec = pl.BlockSpec((2, 8, 128, 128), lambda i: (0, 0, 0, 0))
    gspec = pl.BlockSpec((2, 8, tr, 128), lambda i: (0, 0, i, 0))
    vspec = pl.BlockSpec((6, D), lambda i: (0, 0))
    xspec = pl.BlockSpec((tr, D), lambda i: (i, 0))
    return _host_call(
        body, rider, name, (lay.T // tr,), [xspec, wspec, wspec, vspec, gspec, gspec], [xspec, wspec, wspec, vspec],
        [_sds((lay.T, D), F32), _sds((2, 8, 128, 128), F32), _sds((2, 8, 128, 128), F32), _sds((6, D), F32)],
        (uc, wa, wx, vec, da, db), ("arbitrary",), 6, 4)


GB = 2
SCAN_UNROLL = 4


def _tile_scan(a, b, up):
    t = lax.broadcasted_iota(jnp.int32, a.shape, 0)
    for d in (1, 2, 4):
        sh = 8 - d if up else d
        m = (t < 8 - d) if up else (t >= d)
        a_prev, b_prev = pltpu.roll(a, sh, 0), pltpu.roll(b, sh, 0)
        b = jnp.where(m, a * b_prev + b, b)
        a = jnp.where(m, a * a_prev, a)
    return a, b


def lru_scan(lay, a, b, name):
    segs = [(0, lay.C), (lay.C, lay.L)]

    def body(a_ref, b_ref, s_ref):
        for d in range(2):
            rev = d == 1
            state = tuple(jnp.zeros((1, 128), F32) for _ in range(GB))
            for base, n in segs:
                nt = n // 8

                def step(j, c, base=base, nt=nt, rev=rev, d=d):
                    c = list(c)
                    for u in range(SCAN_UNROLL):
                        jj = j * SCAN_UNROLL + u
                        r0 = pl.multiple_of(base + 8 * ((nt - 1 - jj) if rev else jj), 8)
                        for g in range(GB):
                            at, bt = _tile_scan(a_ref[d, g, pl.ds(r0, 8), :], b_ref[d, g, pl.ds(r0, 8), :], rev)
                            h = at * c[g] + bt
                            s_ref[d, g, pl.ds(r0, 8), :] = h
                            c[g] = h[0:1] if rev else h[7:8]
                    return tuple(c)

                state = lax.fori_loop(0, nt // SCAN_UNROLL, step, state)

    spec = pl.BlockSpec((2, GB, lay.PS, 128), lambda s, hb: (0, hb, s, 0))
    return pl.pallas_call(
        body, name=name, out_shape=_sds((2, 8, lay.T, 128), F32), grid=(2, 8 // GB),
        in_specs=[spec, spec], out_specs=spec, compiler_params=_cp(("parallel", "parallel"), VMEM_BIG))(a, b)


def lru_scan_bwd(lay, a, s, dy, name):
    segs = [(0, lay.C), (lay.C, lay.L)]
    C, PS = lay.C, lay.PS

    def body(a_ref, s_ref, g_ref, da_ref, db_ref):
        t = lax.broadcasted_iota(jnp.int32, (8, 128), 0)
        for d in range(2):
            rev = d == 1
            carry = tuple(jnp.zeros((1, 128), F32) for _ in range(GB))
            for si in (1, 0):
                base, n = segs[si]
                nt = n // 8

                def step(j, c, base=base, nt=nt, rev=rev, d=d):
                    c = list(c)
                    for u in range(SCAN_UNROLL):
                        jj = j * SCAN_UNROLL + u
                        r0 = pl.multiple_of(base + 8 * (jj if rev else (nt - 1 - jj)), 8)
                        if rev:
                            rn = pl.multiple_of(jnp.where(r0 == PS - 8, 0, r0 + 8), 8)
                            nb_zero = r0 == C - 8
                        else:
                            rn = pl.multiple_of(jnp.maximum(r0 - 8, 0), 8)
                            nb_zero = r0 == 0
                        for g in range(GB):
                            av = a_ref[d, g, pl.ds(r0, 8), :]
                            gv = g_ref[g, pl.ds(r0, 8), :]
                            sv = s_ref[d, g, pl.ds(r0, 8), :]
                            nbt = s_ref[d, g, pl.ds(rn, 8), :]
                            at, bt = _tile_scan(av, av * gv, not rev)
                            m = at * c[g] + bt
                            if rev:
                                m_next = jnp.where(t >= 1, pltpu.roll(m, 1, 0), c[g])
                                nb = jnp.where(nb_zero, 0.0, nbt[0:1])
                                h_prev = jnp.where(t < 7, pltpu.roll(sv, 7, 0), nb)
                                c[g] = m[7:8]
                            else:
                                m_next = jnp.where(t < 7, pltpu.roll(m, 7, 0), c[g])
                                nb = jnp.where(nb_zero, 0.0, nbt[7:8])
                                h_prev = jnp.where(t >= 1, pltpu.roll(sv, 1, 0), nb)
                                c[g] = m[0:1]
                            lam = gv + m_next
                            db_ref[d, g, pl.ds(r0, 8), :] = lam
                            da_ref[d, g, pl.ds(r0, 8), :] = lam * h_prev
                    return tuple(c)

                carry = lax.fori_loop(0, nt // SCAN_UNROLL, step, carry)

    spec = pl.BlockSpec((2, GB, lay.PS, 128), lambda s, hb: (0, hb, s, 0))
    return pl.pallas_call(
        body, name=name, out_shape=[_sds((2, 8, lay.T, 128), F32)] * 2, grid=(2, 8 // GB),
        in_specs=[spec, spec, pl.BlockSpec((GB, lay.PS, 128), lambda s, hb: (hb, s, 0))],
        out_specs=[spec, spec], compiler_params=_cp(("parallel", "parallel"), VMEM_BIG))(a, s, dy)


def _gelu(x):
    k = math.sqrt(2.0 / math.pi)
    t = jnp.tanh(k * (x + 0.044715 * x * x * x))
    return 0.5 * x * (1.0 + t), 0.5 * (1.0 + t) + 0.5 * x * (1.0 - t * t) * k * (1.0 + 3 * 0.044715 * x * x)


def lru_gate(lay, p, s, name):
    tr = lay.tr

    def body(g_ref, s_ref, o_ref):
        for h in range(8):
            sl = slice(h * 128, (h + 1) * 128)
            o_ref[:, sl] = (_gelu(g_ref[:, sl])[0] * (s_ref[0, h] + s_ref[1, h])).astype(o_ref.dtype)

    return pl.pallas_call(
        body, name=name, out_shape=_sds((lay.T, D), BF16), grid=(lay.nblk,),
        in_specs=[pl.BlockSpec((tr, D), lambda i: (i, 0)), pl.BlockSpec((2, 8, tr, 128), lambda i: (0, 0, i, 0))],
        out_specs=pl.BlockSpec((tr, D), lambda i: (i, 0)), compiler_params=_cp(("parallel",)))(p, s)


def lru_gate_bwd(lay, p, s, do, name):
    tr = lay.tr

    def body(g_ref, s_ref, do_ref, dg_ref, dy_ref):
        for h in range(8):
            sl = slice(h * 128, (h + 1) * 128)
            ge, dge = _gelu(g_ref[:, sl])
            dov = do_ref[:, sl]
            dg_ref[:, sl] = (dov * (s_ref[0, h] + s_ref[1, h]) * dge).astype(dg_ref.dtype)
            dy_ref[h] = dov * ge

    xspec = pl.BlockSpec((tr, D), lambda i: (i, 0))
    return pl.pallas_call(
        body, name=name, out_shape=[_sds((lay.T, D), BF16), _sds((8, lay.T, 128), F32)], grid=(lay.nblk,),
        in_specs=[xspec, pl.BlockSpec((2, 8, tr, 128), lambda i: (0, 0, i, 0)), xspec],
        out_specs=[xspec, pl.BlockSpec((8, tr, 128), lambda i: (0, i, 0))],
        compiler_params=_cp(("parallel",)))(p, s, do)


def silu_rows(x, name):
    def body(x_ref, o_ref):
        v = x_ref[...]
        o_ref[...] = (v * _sigmoid(v)).astype(o_ref.dtype)
    return pl.pallas_call(body, name=name, out_shape=_sds(x.shape, BF16), in_specs=[VMEM_SPEC], out_specs=VMEM_SPEC)(x)


def mod_grad_rows(gath, name):
    w = gath.shape[-1]

    def body(g_ref, dm_ref, db_ref):
        dm_ref[...] = jnp.zeros_like(dm_ref)
        for l in range(2):
            ctx = g_ref[0, 3 * l + 2:3 * l + 3, :]
            tot = g_ref[0, 3 * l:3 * l + 1, :] + g_ref[0, 3 * l + 1:3 * l + 2, :]
            for k in range(8):
                dm_ref[l, 2 * k:2 * k + 2, :] = g_ref[k, 3 * l:3 * l + 2, :]
                if k:
                    ctx = ctx + g_ref[k, 3 * l + 2:3 * l + 3, :]
                    tot = tot + (g_ref[k, 3 * l:3 * l + 1, :] + g_ref[k, 3 * l + 1:3 * l + 2, :])
            dm_ref[l, 16:17, :] = ctx
            db_ref[l:l + 1, :] = tot + ctx

    return pl.pallas_call(body, name=name, out_shape=[_sds((2, 32, w), F32), _sds((2, w), F32)],
                          in_specs=[VMEM_SPEC], out_specs=[VMEM_SPEC, VMEM_SPEC])(gath)


def cctx_grad(p, c_ctx, name):
    def body(a_ref, c_ref, o_ref):
        cv = c_ref[...]
        sg = _sigmoid(cv)
        o_ref[...] = 0.5 * (a_ref[0, 0:1, :] + a_ref[1, 0:1, :]) * (sg * (1.0 + cv * (1.0 - sg)))
    return pl.pallas_call(body, name=name, out_shape=_sds((1, D), F32), in_specs=[VMEM_SPEC] * 2,
                          out_specs=VMEM_SPEC)(p, c_ctx)


def loss_and_grad(lay, h, tgt, name):
    def fn(hb, tb):
        lat = (pl.program_id(0) % lay.bps) >= lay.cb
        e = jnp.where(lat, hb - tb, 0.0)
        return e * (1.0 / D), jnp.sum(e * e, axis=0, keepdims=True) * (0.5 / D)
    return rowwise(lay, name, fn, [h, tgt], outs=[(D, F32)], sums=[(1, D)])


def adamw(w, g, m, v, name):
    shape = w.shape
    w2, g2, m2, v2 = (t.reshape(-1, shape[-1]) for t in (w, g, m, v))
    rows, width = w2.shape
    tr = 256 if rows % 256 == 0 else rows
    c1 = 1.0 - ADAM_B1 ** ADAM_STEP
    c2 = 1.0 - ADAM_B2 ** ADAM_STEP

    def body(w_ref, g_ref, m_ref, v_ref, d_ref, mo_ref, vo_ref):
        gv = g_ref[...]
        mn = ADAM_B1 * m_ref[...] + (1.0 - ADAM_B1) * gv
        vn = ADAM_B2 * v_ref[...] + (1.0 - ADAM_B2) * (gv * gv)
        d_ref[...] = -ADAM_LR * ((mn / c1) / (jnp.sqrt(vn / c2) + ADAM_EPS) + ADAM_WD * w_ref[...])
        mo_ref[...] = mn
        vo_ref[...] = vn

    spec = pl.BlockSpec((tr, width), lambda i: (i, 0))
    d, mn, vn = pl.pallas_call(body, name=name, out_shape=[_sds((rows, width), F32)] * 3, grid=(rows // tr,),
                               in_specs=[spec] * 4, out_specs=[spec] * 3, compiler_params=_cp(("parallel",)))(w2, g2, m2, v2)
    return d.reshape(shape), mn.reshape(shape), vn.reshape(shape)


def adamw_ffn(w, m, v, red, kind, ns, name):
    shape = w.shape
    w2, m2, v2 = (t.reshape(-1, shape[-1]) for t in (w, m, v))
    rows, width = w2.shape
    c1 = 1.0 - ADAM_B1 ** ADAM_STEP
    c2 = 1.0 - ADAM_B2 ** ADAM_STEP
    tr, nb = ns // 2, 2
    gspec = pl.BlockSpec((tr, D), lambda i: (((i // nb) * 3 + kind) * nb + i % nb, 0))

    def body(w_ref, g_ref, m_ref, v_ref, go_ref, d_ref, mo_ref, vo_ref):
        gv = g_ref[...]
        mn = ADAM_B1 * m_ref[...] + (1.0 - ADAM_B1) * gv
        vn = ADAM_B2 * v_ref[...] + (1.0 - ADAM_B2) * (gv * gv)
        go_ref[...] = gv
        d_ref[...] = -ADAM_LR * ((mn / c1) / (jnp.sqrt(vn / c2) + ADAM_EPS) + ADAM_WD * w_ref[...])
        mo_ref[...] = mn
        vo_ref[...] = vn

    spec = pl.BlockSpec((tr, width), lambda i: (i, 0))
    outs = pl.pallas_call(body, name=name, out_shape=[_sds((rows, width), F32)] * 4, grid=(rows // tr,),
                          in_specs=[spec, gspec, spec, spec], out_specs=[spec] * 4,
                          compiler_params=_cp(("parallel",)))(w2, red, m2, v2)
    return tuple(t.reshape(shape) for t in outs)


def mod_mm(sc, w_mod, bias, name):
    wm = w_mod.shape[-1]
    tn = _pick(wm, (768, 512, 384, 256, 128))

    def body(a_ref, b_ref, c_ref, o_ref):
        o_ref[...] = _nn(a_ref[...], b_ref[...].astype(BF16)) + c_ref[...]

    return pl.pallas_call(
        body, name=name, out_shape=_sds((DEPTH, 32, wm), F32), grid=(DEPTH, wm // tn),
        in_specs=[pl.BlockSpec((32, D), lambda l, j: (0, 0)), pl.BlockSpec((None, D, tn), lambda l, j: (l, 0, j)),
                  pl.BlockSpec((None, 1, tn), lambda l, j: (l, 0, j))],
        out_specs=pl.BlockSpec((None, 32, tn), lambda l, j: (l, 0, j)),
        compiler_params=_cp(("parallel", "parallel")))(sc, w_mod, bias)


def wmod_dw(sc, dcol, name):
    wm = dcol.shape[-1]
    tm = 256

    def body(a_ref, b_ref, o_ref):
        o_ref[...] = _tn(a_ref[...], b_ref[...].astype(BF16))

    return pl.pallas_call(
        body, name=name, out_shape=_sds((DEPTH, D, wm), F32), grid=(DEPTH, D // tm),
        in_specs=[pl.BlockSpec((32, tm), lambda l, i: (0, i)), pl.BlockSpec((None, 32, wm), lambda l, i: (l, 0, 0))],
        out_specs=pl.BlockSpec((None, tm, wm), lambda l, i: (l, i, 0)),
        compiler_params=_cp(("parallel", "parallel")))(sc, dcol)


def cctx_dx(drow, w_mod, name):
    wm = w_mod.shape[-1]

    def body(a_ref, b_ref, o_ref):
        o_ref[...] = _nt(a_ref[...].astype(BF16), b_ref[...].astype(BF16))

    return pl.pallas_call(
        body, name=name, out_shape=_sds((DEPTH, 16, D), F32), grid=(DEPTH,),
        in_specs=[pl.BlockSpec((None, 16, wm), lambda l: (l, 0, 0)), pl.BlockSpec((None, D, wm), lambda l: (l, 0, 0))],
        out_specs=pl.BlockSpec((None, 16, D), lambda l: (l, 0, 0)), compiler_params=_cp(("parallel",), VMEM_BIG))(drow, w_mod)


HEAD_PERM = (0, 4, 1, 5, 2, 6, 3, 7)


def _rot_rows(wt):
    return jnp.concatenate([-wt[32:64], wt[0:32]], axis=0)


def _unrot_rows(g):
    return jnp.concatenate([g[32:64], -g[0:32]], axis=0)


def _heads(a, n):
    return [a[64 * i:64 * (i + 1)] for i in range(n)]


def kernel(x, c, ctx, c_ctx, w_mod, b_mod, ln_g, ln_b, ffn_w_gate, ffn_w_up, ffn_w_down, mix_ab_w_in, attn_sink, pool_w, pool_scale, mix_ab_w_out, lru_w_in, lru_conv_w, lru_conv_b, lru_wa, lru_ba, lru_wx, lru_bx, lru_lambda, lru_w_out, loss_target, m_c_ctx, m_w_mod, m_b_mod, m_ln_g, m_ln_b, m_ffn_w_gate, m_ffn_w_up, m_ffn_w_down, m_mix_ab_w_in, m_attn_sink, m_pool_w, m_pool_scale, m_mix_ab_w_out, m_lru_w_in, m_lru_conv_w, m_lru_conv_b, m_lru_wa, m_lru_ba, m_lru_wx, m_lru_bx, m_lru_lambda, m_lru_w_out, v_c_ctx, v_w_mod, v_b_mod, v_ln_g, v_ln_b, v_ffn_w_gate, v_ffn_w_up, v_ffn_w_down, v_mix_ab_w_in, v_attn_sink, v_pool_w, v_pool_scale, v_mix_ab_w_out, v_lru_w_in, v_lru_conv_w, v_lru_conv_b, v_lru_wa, v_lru_ba, v_lru_wx, v_lru_bx, v_lru_lambda, v_lru_w_out):
    n_lat, n_ctx = x.shape[1], ctx.shape[1]
    lay = Layout(n_ctx, n_lat)
    T = lay.T
    ns = ffn_w_gate.shape[-1]
    n_li, n_ai = lru_w_in.shape[-1], mix_ab_w_in.shape[-1]
    n_ao, n_lo = mix_ab_w_out.shape[1], lru_w_out.shape[1]
    wm = w_mod.shape[-1]
    dsh = ln_g.shape[-1]
    mx, my, mc = lax.axis_index("x"), lax.axis_index("y"), lax.axis_index("c")
    chip = 2 * mx + my
    me = 2 * chip + mc

    c_all = all_gather8(c, "ag8_c").reshape(16, D)
    cc = jnp.concatenate([c_all, c_ctx[None, :], jnp.zeros((15, D), F32)], axis=0)
    sc = silu_rows(cc, "silu_c")
    bias = lax.dynamic_slice(b_mod, (0, chip * wm), (DEPTH, wm)).reshape(DEPTH, 1, wm)
    modg = all_gather_chips(mod_mm(sc, w_mod, bias, "mod_mm"), "ag_mod")
    modtab = []
    for l in range(DEPTH):
        full = jnp.transpose(modg[:, l], (1, 0, 2)).reshape(32, N_CHIP * wm)
        mine = lax.dynamic_slice(full, (2 * me, 0), (2, N_CHIP * wm))
        modtab.append(jnp.concatenate([mine, full[16:17]], axis=0).reshape(3, N_MOD, D))

    small = jnp.concatenate([ln_g.reshape(6, dsh), ln_b.reshape(6, dsh), lru_conv_w[0], lru_conv_b, lru_ba[0],
                             lru_bx[0], lru_lambda[0], jnp.zeros((9, dsh), F32)], axis=0)
    small = all_gather_chips(small.reshape(2, 16, dsh), "ag_small").reshape(N_CHIP, 32, dsh)
    small = jnp.transpose(small, (1, 0, 2)).reshape(32, D)
    ln_g_f, ln_b_f = small[0:6].reshape(2, 3, D), small[6:12].reshape(2, 3, D)
    conv_w_f, conv_b_f = small[12:16], small[16:17]
    lru_vec = small[17:23]

    hh = 3 * ns // 2
    gate_t, up_t = jnp.swapaxes(ffn_w_gate, -1, -2), jnp.swapaxes(ffn_w_up, -1, -2)
    extra = [0, n_ai + n_ao, n_li + n_lo, 0]
    placed = [ffn_place(gate_t, up_t, ffn_w_down, g // 2, g % 2, f"ag_ffn{g}_place", extra[g]) for g in range(4)]
    placed[1] = place_rows(placed[1], jnp.concatenate([mix_ab_w_in[0].T, mix_ab_w_out[0]], axis=0).astype(BF16), 3 * ns,
                           "ag_mixa_place")
    placed[2] = place_rows(placed[2], jnp.concatenate([lru_w_in[0].T, lru_w_out[0]], axis=0).astype(BF16), 3 * ns,
                           "ag_mixc_place")
    placed = [p.reshape(N_CHIP, 2, p.shape[1] // 2, D) for p in placed]
    wb = [gather_placed(placed[0], "ag_ffn0"), None, None, None]
    mixw = {}

    def mixa_w():
        if "a" not in mixw:
            full = wb[1].reshape(N_CHIP, -1, D)
            ab_in_t = full[:, 3 * ns:3 * ns + n_ai].reshape(N_CHIP * n_ai, D)
            ab_out = full[:, 3 * ns + n_ai:].reshape(N_CHIP * n_ao, D)
            qh, kh = _heads(ab_in_t[Q0:K0], N_HEADS), _heads(ab_in_t[K0:V0], N_KV)
            w_ext_t = jnp.concatenate([qh[h] for h in HEAD_PERM] + [ab_in_t[K0:QR0]]
                                      + [_rot_rows(qh[h]) for h in HEAD_PERM] + [_rot_rows(t) for t in kh], axis=0)
            oh = _heads(ab_out[0:ATT_W], N_HEADS)
            mixw["a"] = (w_ext_t, jnp.concatenate([oh[h] for h in HEAD_PERM] + [ab_out[ATT_W:]], axis=0))
        return mixw["a"]

    def mixc_w():
        if "c" not in mixw:
            full = wb[2].reshape(N_CHIP, -1, D)
            mixw["c"] = (full[:, 3 * ns:3 * ns + n_li].reshape(N_CHIP * n_li, D),
                         full[:, 3 * ns + n_li:].reshape(N_CHIP * n_lo, D))
        return mixw["c"]

    t = jnp.arange(n_lat)
    inv = ROPE_THETA ** (-jnp.arange(16, dtype=F32) / 16.0)
    ang = jnp.concatenate([(t // GRID_W).astype(F32)[:, None] * inv, (t % GRID_W).astype(F32)[:, None] * inv], axis=-1)
    cos1 = jnp.concatenate([jnp.ones((n_ctx, 32), F32), jnp.cos(ang)], axis=0)
    sin1 = jnp.concatenate([jnp.zeros((n_ctx, 32), F32), jnp.sin(ang)], axis=0)
    cos_t = jnp.tile(cos1, (2, 4))
    sin_t = jnp.tile(sin1, (2, 4))
    sk = attn_sink[0]
    sink_tab = jnp.concatenate([jnp.repeat(jnp.stack([sk[:4], sk[4:]], axis=1), HEAD_DIM, axis=1),
                                jnp.zeros((4, 128), F32)], axis=0)
    pscale = pool_scale.reshape(1, POOL_W)

    h0 = jnp.concatenate([ctx, x], axis=1).reshape(T, D)
    tgt = loss_target.reshape(2 * n_lat, D)

    def lnv(l, j):
        return jnp.stack([ln_g_f[l, j], ln_b_f[l, j]])

    subs = [(0, 0, 0.5, 0), (0, 3, 1.0, 1), (0, 6, 0.5, 2), (1, 0, 0.5, 0), (1, 3, 1.0, 1), (1, 6, 0.5, 2)]

    def ffn_core(hm, l, f):
        tag = f"l{l}f{f}"
        gi = 2 * l + f
        w = wb[gi].reshape(N_CHIP, -1, D)
        if gi == 3:
            sp, sl, u, a = ffn_up(lay, hm, w, 0, 1, ns, f"ffn_up_{tag}")
            (y,) = slab_nn_acc(lay, [a], w, [2], ns, f"ffn_down_{tag}")
            return y, dict(sp=sp, sl=sl, u=u, a=a, nbuf=None)
        sp, sl, u, a, nbuf = ffn_up(lay, hm, w, 0, 1, ns, f"ffn_up_{tag}", rider=rider_gather_xy(placed[gi + 1]))
        y, nbuf = slab_nn_acc(lay, [a], w, [2], ns, f"ffn_down_{tag}", rider=rider_gather_fwd(nbuf))
        return y, dict(sp=sp, sl=sl, u=u, a=a, nbuf=nbuf)

    def mixa_core(hm):
        p = mm_nt(hm, mixa_w()[0], "mixa_in")
        qr, kr, vb, u = rope_fwd(lay, p, cos_t, sin_t, "rope")
        att, lse = attn_fwd(lay, qr, kr, vb, sink_tab, "attn")
        pool = pool_fwd(lay, u, pool_w[0], pscale, "pool")
        cat = jnp.concatenate([att, pool], axis=1)
        return mm_nn(cat, mixa_w()[1], "mixa_out"), dict(qr=qr, kr=kr, vb=vb, u=u, lse=lse, cat=cat)

    def mixc_core(hm):
        p = mm_nt(hm, mixc_w()[0], "mixc_in")
        uc = conv_fwd(lay, p, D, conv_w_f, conv_b_f, "conv")
        a, b = lru_coeffs(lay, uc, lru_wa[0], lru_wx[0], lru_vec, "lru_coef")
        s = lru_scan(lay, a, b, "lru_scan")
        o = lru_gate(lay, p, s, "lru_gate")
        return mm_nn(o, mixc_w()[1], "mixc_out"), dict(p=p, uc=uc, a=a, s=s, o=o)

    recs = []
    h = h0
    hm = modulate(lay, h0, modtab[0], 0, 1, "mod_first")
    for k, (l, k0, coef, j) in enumerate(subs):
        if k0 == 3:
            y, core = mixa_core(hm) if l == 0 else mixc_core(hm)
        else:
            y, core = ffn_core(hm, l, k0 // 6)
        nxt = None if k == 5 else (modtab[subs[k + 1][0]], subs[k + 1][1], subs[k + 1][1] + 1)
        nbuf = core.pop("nbuf", None)
        res = resid_ln(lay, h, y, modtab[l], k0 + 2, coef, lnv(l, j), f"ln_s{k}", nxt=nxt,
                       rider=None if nbuf is None else rider_gather_d2d(nbuf))
        if nbuf is not None:
            wb[2 * l + k0 // 6 + 1] = res[-1]
        recs.append(dict(h=h, hm=hm, y=y, xhat=res[1], rstd=res[2], **core))
        h = res[0]
        hm = res[3] if nxt is not None else None

    dout, lparts = loss_and_grad(lay, h, tgt, "loss")
    loss = lax.psum(jnp.sum(lparts), ("x", "y", "c"))

    dln = {}
    dms = {}
    mixg = {}
    ffn_red = [lax.empty((4, 2, hh, D), F32)]
    mix_red = {}
    pending = []

    def rs_sib(p):
        return None if p is None else rider_reduce_sib(p["buf"])

    def rs_add2(p, recv):
        p["q"] = add_own_half(p["buf"], recv, BF16, f"rs_add2_{p['key']}")

    def rs_join(p, arr):
        if isinstance(p["key"], int):
            return rider_join(sum_slots(p["q"], arr, f"rs_add4_{p['key']}", dst=ffn_red[0], g=p["key"]), p["key"])
        return rider_join(sum_slots(p["q"], arr, f"rs_add4_{p['key']}"))

    def rs_done(p, joined):
        if isinstance(p["key"], int):
            ffn_red[0] = joined
        else:
            mix_red[p["key"]] = joined.reshape(-1, D)

    def ffn_core_bwd(dy, r, l, f):
        tag = f"l{l}f{f}"
        gi = 2 * l + f
        w = wb[gi].reshape(N_CHIP, -1, D)
        p = pending.pop() if pending else None
        gb = lax.empty((N_CHIP, 3 * ns, D), F32)
        if p is None:
            dg, du = ffn_bwd_da(lay, dy, w, 2, r["sp"], r["sl"], r["u"], ns, f"ffn_da_{tag}")
            (gb,) = slab_tn(lay, r["a"], dy, gb, 2, ns, f"ffn_dwd_{tag}")
            (gb,) = slab_tn(lay, dg, r["hm"], gb, 0, ns, f"ffn_dwg_{tag}")
            (gb,) = slab_tn(lay, du, r["hm"], gb, 1, ns, f"ffn_dwu_{tag}")
            (dhm,) = slab_nn_acc(lay, [dg, du], w, [0, 1], ns, f"ffn_dh_{tag}")
        else:
            dg, du, recv = ffn_bwd_da(lay, dy, w, 2, r["sp"], r["sl"], r["u"], ns, f"ffn_da_{tag}", rider=rs_sib(p))
            rs_add2(p, recv)
            gb, arr = slab_tn(lay, r["a"], dy, gb, 2, ns, f"ffn_dwd_{tag}", rider=rider_reduce_copy(p["q"], 0))
            gb, arr = slab_tn(lay, dg, r["hm"], gb, 0, ns, f"ffn_dwg_{tag}", rider=rider_reduce_copy(p["q"], 1, arr))
            gb, arr = slab_tn(lay, du, r["hm"], gb, 1, ns, f"ffn_dwu_{tag}", rider=rider_reduce_copy(p["q"], 2, arr))
            dhm, joined = slab_nn_acc(lay, [dg, du], w, [0, 1], ns, f"ffn_dh_{tag}", rider=rs_join(p, arr))
            rs_done(p, joined)
        pending.append(dict(buf=gb.reshape(N_CHIP, 2, hh, D), key=gi))
        return dhm

    def mixc_core_bwd(dy, r):
        p = pending.pop() if pending else None
        w_in_t, w_out = mixc_w()
        if p is None:
            do_c = mm_nt(dy, w_out, "mixc_out_dx")
        else:
            do_c, recv = mm_nt(dy, w_out, "mixc_out_dx", rider=rs_sib(p))
            rs_add2(p, recv)
        g_out = mm_tn(r["o"], dy, "mixc_out_dw")
        dgate, dyg = lru_gate_bwd(lay, r["p"], r["s"], do_c, "lru_gate_b")
        da_c, db_c = lru_scan_bwd(lay, r["a"], r["s"], dyg, "lru_scan_b")
        res = lru_coeffs_bwd(lay, r["uc"], lru_wa[0], lru_wx[0], lru_vec, da_c, db_c, "lru_coef_b",
                             rider=None if p is None else rider_reduce_copies(p["q"]))
        duc, mixg["wa"], mixg["wx"], mixg["vec"] = res[:4]
        du_c, mixg["cw"], mixg["cb"] = conv_bwd(lay, r["p"], D, conv_w_f, duc, "conv_b")
        dp_c = jnp.concatenate([dgate, du_c], axis=1)
        if p is None:
            g_in_t = mm_tn(dp_c, r["hm"], "mixc_in_dw")
        else:
            g_in_t, joined = mm_tn(dp_c, r["hm"], "mixc_in_dw", rider=rs_join(p, res[4]))
            rs_done(p, joined)
        buf = jnp.concatenate([g_in_t.reshape(N_CHIP, n_li, D), g_out.reshape(N_CHIP, n_lo, D)], axis=1)
        pending.append(dict(buf=buf.reshape(N_CHIP, 2, (n_li + n_lo) // 2, D), key="c"))
        return mm_nn(dp_c, w_in_t, "mixc_in_dx")

    def mixa_core_bwd(dy, r):
        p = pending.pop() if pending else None
        w_ext_t, w_out_ext = mixa_w()
        if p is None:
            dcat = mm_nt(dy, w_out_ext, "mixa_out_dx")
        else:
            dcat, recv = mm_nt(dy, w_out_ext, "mixa_out_dx", rider=rs_sib(p))
            rs_add2(p, recv)
        g_out_ext = mm_tn(r["cat"], dy, "mixa_out_dw")
        res = attn_bwd(lay, r["qr"], r["kr"], r["vb"], sink_tab, r["lse"], dcat, "attn_b",
                       rider=None if p is None else rider_reduce_copies(p["q"]))
        dqr, dkr, dv, mixg["sink"] = res[:4]
        du_a, mixg["pw"], mixg["ps"] = pool_bwd(lay, r["u"], dcat, pool_w[0], pscale, "pool_b")
        dp_a = rope_bwd(lay, dqr, dkr, dv, du_a, cos_t, sin_t, "rope_b")
        if p is None:
            g_ext_t = mm_tn(dp_a, r["hm"], "mixa_in_dw")
        else:
            g_ext_t, joined = mm_tn(dp_a, r["hm"], "mixa_in_dw", rider=rs_join(p, res[4]))
            rs_done(p, joined)
        gq, gqr = _heads(g_ext_t[Q0:K0], N_HEADS), _heads(g_ext_t[QR0:KR0], N_HEADS)
        g_q = [None] * N_HEADS
        for i, h in enumerate(HEAD_PERM):
            g_q[h] = gq[i] + _unrot_rows(gqr[i])
        gk = [a + _unrot_rows(b) for a, b in zip(_heads(g_ext_t[K0:V0], N_KV), _heads(g_ext_t[KR0:PEXT], N_KV))]
        g_ab_in_t = jnp.concatenate(g_q + gk + [g_ext_t[V0:QR0]], axis=0)
        go = _heads(g_out_ext[0:ATT_W], N_HEADS)
        g_o = [None] * N_HEADS
        for i, h in enumerate(HEAD_PERM):
            g_o[h] = go[i]
        g_ab_out = jnp.concatenate(g_o + [g_out_ext[ATT_W:]], axis=0)
        buf = jnp.concatenate([g_ab_in_t.reshape(N_CHIP, n_ai, D), g_ab_out.reshape(N_CHIP, n_ao, D)], axis=1)
        pending.append(dict(buf=buf.reshape(N_CHIP, 2, (n_ai + n_ao) // 2, D), key="a"))
        return mm_nn(dp_a, w_ext_t, "mixa_in_dx")

    l, k0, coef, j = subs[5]
    dy, dres, s1 = ln_bwd(lay, dout, recs[5]["xhat"], recs[5]["rstd"], recs[5]["y"], modtab[l], k0 + 2, coef, lnv(l, j),
                          "lnb_s5")
    for k in range(5, -1, -1):
        l, k0, coef, j = subs[k]
        r = recs[k]
        if k0 == 3:
            dhm = mixa_core_bwd(dy, r) if l == 0 else mixc_core_bwd(dy, r)
        else:
            dhm = ffn_core_bwd(dy, r, l, k0 // 6)
        dln[(l, j)] = block_sums(lay, s1, f"bs_ln_s{k}")
        if k > 0:
            lp, k0p, coefp, jp = subs[k - 1]
            rp = recs[k - 1]
            dy, dres, s1, s2 = modb_lnb(lay, dres, dhm, r["h"], modtab[l], k0 + 1, rp["xhat"], rp["rstd"], rp["y"],
                                        modtab[lp], k0p + 2, coefp, lnv(lp, jp), f"modb_lnb_s{k}")
        else:
            gx, s2 = mod_bwd(lay, dres, dhm, r["h"], modtab[l], k0 + 1, "modb_s0")
        dms[(l, k0)] = block_sums(lay, s2, f"bs_mod_s{k}")
    grad_x = gx.reshape(2, n_lat, D)
    g_wa, g_wx, g_vec, g_cw, g_cb, g_sink, g_pw, g_ps = (mixg[n] for n in ("wa", "wx", "vec", "cw", "cb", "sink", "pw", "ps"))

    rows = []
    for l in range(DEPTH):
        per_k = []
        for k0, j in ((0, 0), (3, 1), (6, 2)):
            per_k += [dms[(l, k0)][:3, 0], dms[(l, k0)][:3, 1], dln[(l, j)][:3, 2]]
        rows.append(jnp.stack(per_k, axis=1).reshape(3, N_MOD * D))
    dmod_loc = jnp.concatenate(rows + [jnp.zeros((2, N_MOD * D), F32)], axis=0)
    dmod_all, g_b_mod = mod_grad_rows(all_gather8(dmod_loc, "ag8_dmod"), "dmod_rows")
    dcol = lax.dynamic_slice(dmod_all, (0, 0, chip * wm), (DEPTH, 32, wm))
    g_w_mod = wmod_dw(sc, dcol, "wmod_dw")
    g_cctx = cctx_grad(cctx_dx(dcol[:, 16:32], w_mod, "cctx_dx"), c_ctx[None, :], "cctx_grad")

    g_ln_g =jnp.stack([jnp.stack([dln[(l, j)][3, 1] for j in range(3)]) for l in range(DEPTH)])
    g_ln_b = jnp.stack([jnp.stack([dln[(l, j)][3, 0] for j in range(3)]) for l in range(DEPTH)])
    sink_row = jnp.sum(g_sink, axis=0)[:4]
    g_sink8 = jnp.concatenate([sink_row[:, 0], sink_row[:, HEAD_DIM]])
    misc = jnp.concatenate([g_sink8, jnp.sum(g_ps, axis=0).reshape(POOL_W), jnp.zeros((D - 8 - POOL_W,), F32)])
    small_g = jnp.concatenate([
        g_ln_g.reshape(6, D), g_ln_b.reshape(6, D), jnp.sum(g_cw, axis=0), jnp.sum(g_cb, axis=0), g_vec,
        misc[None, :], jnp.sum(g_pw, axis=0).reshape(64, D), g_wa.reshape(256, D), g_wx.reshape(256, D), g_cctx,
        jnp.zeros((39, D), F32)], axis=0)
    n_small = small_g.shape[0] // N_CHIP
    last = pending.pop()
    ffn_red = reduce_scatter_chips(last["buf"], f"ffn{last['key']}", wire=BF16, dst=ffn_red[0],
                                   g=last["key"]).reshape(12 * ns, D)
    small_red = reduce_scatter_chips(small_g.reshape(N_CHIP, 2, n_small // 2, D), "small")
    small_red = all_gather_chips(small_red, "ag_smallg").reshape(N_CHIP * n_small, D)

    ffn_kind = dict(ffn_w_gate=0, ffn_w_up=1, ffn_w_down=2)

    def cols(a):
        return lax.dynamic_slice_in_dim(a, chip * dsh, dsh, axis=a.ndim - 1)

    sr = small_red
    grads = dict(
        c_ctx=sr[600], w_mod=g_w_mod, b_mod=g_b_mod,
        ln_g=cols(sr[0:6]).reshape(2, 3, dsh), ln_b=cols(sr[6:12]).reshape(2, 3, dsh),
        mix_ab_w_in=mix_red["a"][0:n_ai][None], attn_sink=sr[23, 0:8][None], pool_w=sr[24:88].reshape(1, 4, 128, 128),
        pool_scale=sr[23, 8:8 + POOL_W][None], mix_ab_w_out=mix_red["a"][n_ai:][None],
        lru_w_in=mix_red["c"][0:n_li].T[None],
        lru_conv_w=cols(sr[12:16])[None], lru_conv_b=cols(sr[16:17]), lru_wa=sr[88:344].reshape(1, 2, 8, 128, 128),
        lru_ba=cols(sr[17:19])[None], lru_wx=sr[344:600].reshape(1, 2, 8, 128, 128), lru_bx=cols(sr[19:21])[None],
        lru_lambda=cols(sr[21:23])[None], lru_w_out=mix_red["c"][n_li:][None])
    params = dict(c_ctx=(c_ctx, m_c_ctx, v_c_ctx), w_mod=(w_mod, m_w_mod, v_w_mod), b_mod=(b_mod, m_b_mod, v_b_mod),
                  ln_g=(ln_g, m_ln_g, v_ln_g), ln_b=(ln_b, m_ln_b, v_ln_b),
                  ffn_w_gate=(ffn_w_gate, m_ffn_w_gate, v_ffn_w_gate), ffn_w_up=(ffn_w_up, m_ffn_w_up, v_ffn_w_up),
                  ffn_w_down=(ffn_w_down, m_ffn_w_down, v_ffn_w_down),
                  mix_ab_w_in=(mix_ab_w_in, m_mix_ab_w_in, v_mix_ab_w_in), attn_sink=(attn_sink, m_attn_sink, v_attn_sink),
                  pool_w=(pool_w, m_pool_w, v_pool_w), pool_scale=(pool_scale, m_pool_scale, v_pool_scale),
                  mix_ab_w_out=(mix_ab_w_out, m_mix_ab_w_out, v_mix_ab_w_out), lru_w_in=(lru_w_in, m_lru_w_in, v_lru_w_in),
                  lru_conv_w=(lru_conv_w, m_lru_conv_w, v_lru_conv_w), lru_conv_b=(lru_conv_b, m_lru_conv_b, v_lru_conv_b),
                  lru_wa=(lru_wa, m_lru_wa, v_lru_wa), lru_ba=(lru_ba, m_lru_ba, v_lru_ba), lru_wx=(lru_wx, m_lru_wx, v_lru_wx),
                  lru_bx=(lru_bx, m_lru_bx, v_lru_bx), lru_lambda=(lru_lambda, m_lru_lambda, v_lru_lambda),
                  lru_w_out=(lru_w_out, m_lru_w_out, v_lru_w_out))
    gl, dl, ml, vl = [], [], [], []
    transposed = ("ffn_w_gate", "ffn_w_up", "mix_ab_w_in")
    for name, (w, m, v) in params.items():
        if name in transposed:
            w, m, v = (jnp.swapaxes(t, -1, -2) for t in (w, m, v))
        if name in ffn_kind:
            g, d, mn, vn = adamw_ffn(w, m, v, ffn_red, ffn_kind[name], ns, f"adamw_{name}")
        else:
            g = grads[name].reshape(w.shape)
            d, mn, vn = adamw(w, g, m, v, f"adamw_{name}")
        if name in transposed:
            g, d, mn, vn = (jnp.swapaxes(t, -1, -2) for t in (g, d, mn, vn))
        gl.append(g)
        dl.append(d)
        ml.append(mn)
        vl.append(vn)
    return (loss, grad_x, *gl, *dl, *ml, *vl)
```

```python
import functools
import math

import jax
import jax.numpy as jnp
from jax import lax
from jax.experimental import pallas as pl
from jax.experimental.pallas import tpu as pltpu

F32, BF16 = jnp.float32, jnp.bfloat16
MESH = pl.DeviceIdType.MESH
ANY = pl.BlockSpec(memory_space=pl.ANY)
VMEM_SPEC = pl.BlockSpec(memory_space=pltpu.VMEM)

D = 1024
N_CHIP = 4
HEAD_DIM, N_HEADS, N_KV = 64, 8, 2
ATT_W, KV_W, POOL_W = 512, 128, 512
POOL_WINDOWS = (2, 4, 8, 16)
BLK = 128
ATT_SCALE = HEAD_DIM ** -0.5
ROPE_THETA = 10000.0
GRID_W = 64
LRU_C = 8.0
LN_EPS = 1e-5
NEG_INF = -1e30
DEPTH = 2
ALPHA = (2 * DEPTH) ** 0.25
N_MOD = 9
ADAM_LR, ADAM_B1, ADAM_B2, ADAM_EPS, ADAM_WD, ADAM_STEP = 0.001, 0.9, 0.999, 1e-08, 0.01, 10
VMEM_BIG = 48 * 1024 * 1024


def _cp(sem=None, vmem=None):
    kw = {}
    if sem is not None:
        kw["dimension_semantics"] = sem
    if vmem is not None:
        kw["vmem_limit_bytes"] = vmem
    return pltpu.CompilerParams(**kw)


def _sds(shape, dtype):
    return jax.ShapeDtypeStruct(tuple(shape), dtype)


def _pick(n, cands):
    for c in cands:
        if n % c == 0:
            return c
    return n


def _dot(a, b, dims):
    return lax.dot_general(a, b, (dims, ((), ())), preferred_element_type=F32)


def _nn(a, b):
    return _dot(a, b, ((1,), (0,)))


def _nt(a, b):
    return _dot(a, b, ((1,), (1,)))


def _tn(a, b):
    return _dot(a, b, ((0,), (0,)))


def _sigmoid(x):
    return 0.5 * jnp.tanh(0.5 * x) + 0.5


def _me():
    return lax.axis_index("x"), lax.axis_index("y"), lax.axis_index("c")


def _rcopy(src, dst, ssem, rsem, dev):
    return pltpu.make_async_remote_copy(src_ref=src, dst_ref=dst, send_sem=ssem, recv_sem=rsem,
                                        device_id=dev, device_id_type=MESH)


def all_gather8(x, name):
    def body(x_ref, o_ref, ssem, rsem, lsem):
        mx, my, mc = _me()
        me = 4 * mx + 2 * my + mc
        loc = pltpu.make_async_copy(x_ref, o_ref.at[me], lsem)
        loc.start()
        peers = []
        for m in range(1, 8):
            px = 1 - mx if (m >> 2) & 1 else mx
            py = 1 - my if (m >> 1) & 1 else my
            pc = 1 - mc if m & 1 else mc
            peers.append((px, py, pc))
        sends = [_rcopy(x_ref, o_ref.at[me], ssem.at[k], rsem.at[k], p) for k, p in enumerate(peers)]
        for cp in sends:
            cp.start()
        for k, (px, py, pc) in enumerate(peers):
            _rcopy(x_ref, o_ref.at[4 * px + 2 * py + pc], ssem.at[k], rsem.at[k], (px, py, pc)).wait_recv()
        for cp in sends:
            cp.wait_send()
        loc.wait()

    return pl.pallas_call(
        body, name=name, out_shape=_sds((8,) + x.shape, x.dtype),
        in_specs=[VMEM_SPEC], out_specs=VMEM_SPEC,
        scratch_shapes=[pltpu.SemaphoreType.DMA((7,)), pltpu.SemaphoreType.DMA((7,)), pltpu.SemaphoreType.DMA],
    )(x)


_ROW_BLOCKS = (512, 384, 352, 256, 224, 128)


def _idx(v):
    return jnp.reshape(v, (1,)).astype(jnp.int32)


def place_slab(shard, name):
    _, h, w = shard.shape
    th = _pick(h, _ROW_BLOCKS)

    def body(s_ref, x_ref, o_ref):
        del s_ref
        o_ref[...] = x_ref[...]

    return pl.pallas_call(
        body, name=name, out_shape=_sds((N_CHIP,) + shard.shape, shard.dtype),
        grid_spec=pltpu.PrefetchScalarGridSpec(
            num_scalar_prefetch=1, grid=(2, h // th),
            in_specs=[pl.BlockSpec((None, th, w), lambda k, r, s: (k, r, 0))],
            out_specs=pl.BlockSpec((None, None, th, w), lambda k, r, s: (s[0], k, r, 0))),
    )(_idx(2 * lax.axis_index("x") + lax.axis_index("y")), shard)


def place_rows(buf, rows, r0, name):
    e, w = rows.shape
    tb = 64

    def body(s_ref, x_ref, b_ref, o_ref):
        del s_ref, b_ref
        o_ref[...] = x_ref[...]

    return pl.pallas_call(
        body, name=name, out_shape=_sds(buf.shape, buf.dtype),
        grid_spec=pltpu.PrefetchScalarGridSpec(
            num_scalar_prefetch=1, grid=(e // tb,),
            in_specs=[pl.BlockSpec((tb, w), lambda j, s: (j, 0)), ANY],
            out_specs=pl.BlockSpec((None, tb, w), lambda j, s: (s[0], r0 // tb + j, 0))),
        input_output_aliases={2: 0},
    )(_idx(2 * lax.axis_index("x") + lax.axis_index("y")), rows, buf)


def ffn_place(w_gate_t, w_up_t, w_down, l, f, name, extra=0):
    ns = w_down.shape[-2]
    tr, nb = ns // 2, 2

    def body(s_ref, g_ref, u_ref, d_ref, o_ref):
        del s_ref
        k = pl.program_id(0)

        @pl.when(k == 0)
        def _():
            o_ref[...] = g_ref[...].astype(BF16)

        @pl.when(k == 1)
        def _():
            o_ref[...] = u_ref[...].astype(BF16)

        @pl.when(k == 2)
        def _():
            o_ref[...] = d_ref[...].astype(BF16)

    def spec(q):
        return pl.BlockSpec((None, None, tr, D), lambda k, j, s: (l, f, jnp.where(k == q, j, 0), 0))

    return pl.pallas_call(
        body, name=name, out_shape=_sds((N_CHIP, 3 * ns + extra, D), BF16),
        grid_spec=pltpu.PrefetchScalarGridSpec(
            num_scalar_prefetch=1, grid=(3, nb), in_specs=[spec(0), spec(1), spec(2)],
            out_specs=pl.BlockSpec((None, tr, D), lambda k, j, s: (s[0], k * nb + j, 0))),
    )(_idx(2 * lax.axis_index("x") + lax.axis_index("y")), w_gate_t, w_up_t, w_down)


def all_gather_chips(shard, name):
    return gather_placed(place_slab(shard, name + "_place"), name)


def gather_placed(full, name):
    h = full.shape[2]
    lo, hi = pl.ds(0, h // 2), pl.ds(h // 2, h - h // 2)

    def body(x_ref, o_ref, ssem, rsem):
        del x_ref
        mx, my, mc = _me()
        s, xs, ys, ds = 2 * mx + my, 2 * (1 - mx) + my, 2 * mx + (1 - my), 2 * (1 - mx) + (1 - my)
        xn, yn, sib = (1 - mx, my, mc), (mx, 1 - my, mc), (mx, my, 1 - mc)

        def cp(k, src, dst, dev):
            return _rcopy(src, dst, ssem.at[k], rsem.at[k], dev)

        own = o_ref.at[s, mc]
        sent = [cp(0, own, own, xn), cp(1, own, own, yn)]
        for c in sent:
            c.start()
        cp(0, own, o_ref.at[xs, mc], xn).wait_recv()
        sent += [cp(2, o_ref.at[xs, mc, lo], o_ref.at[xs, mc, lo], yn), cp(4, o_ref.at[xs, mc], o_ref.at[xs, mc], sib)]
        sent[-2].start()
        sent[-1].start()
        cp(1, own, o_ref.at[ys, mc], yn).wait_recv()
        sent += [cp(3, o_ref.at[ys, mc, hi], o_ref.at[ys, mc, hi], xn), cp(5, o_ref.at[ys, mc], o_ref.at[ys, mc], sib)]
        sent[-2].start()
        sent[-1].start()
        cp(2, own, o_ref.at[ds, mc, lo], yn).wait_recv()
        cp(3, own, o_ref.at[ds, mc, hi], xn).wait_recv()
        sent.append(cp(6, o_ref.at[ds, mc], o_ref.at[ds, mc], sib))
        sent[-1].start()
        for k, slot in ((4, xs), (5, ys), (6, ds)):
            cp(k, own, o_ref.at[slot, 1 - mc], sib).wait_recv()
        for c in sent:
            c.wait_send()

    return pl.pallas_call(
        body, name=name, out_shape=_sds(full.shape, full.dtype), in_specs=[ANY], out_specs=ANY,
        input_output_aliases={0: 0},
        scratch_shapes=[pltpu.SemaphoreType.DMA((7,)), pltpu.SemaphoreType.DMA((7,))],
    )(full)


def sibling_send_other_half(buf, name):
    def body(x_ref, o_ref, ssem, rsem):
        mx, my, mc = _me()
        sib = (mx, my, 1 - mc)
        cps = [_rcopy(x_ref.at[k, 1 - mc], o_ref.at[k], ssem.at[k], rsem.at[k], sib) for k in range(N_CHIP)]
        for cp in cps:
            cp.start()
        for cp in cps:
            cp.wait_recv()
        for cp in cps:
            cp.wait_send()

    n, _, h, w = buf.shape
    return pl.pallas_call(
        body, name=name, out_shape=_sds((n, h, w), buf.dtype), in_specs=[ANY], out_specs=ANY,
        scratch_shapes=[pltpu.SemaphoreType.DMA((N_CHIP,)), pltpu.SemaphoreType.DMA((N_CHIP,))],
    )(buf)


def chips_all_to_all(q, name):
    def body(x_ref, o_ref, ssem, rsem):
        mx, my, mc = _me()
        s = 2 * mx + my
        chips = [(1 - mx, my), (mx, 1 - my), (1 - mx, 1 - my)]
        cps = [_rcopy(x_ref.at[2 * px + py], o_ref.at[s], ssem.at[j], rsem.at[j], (px, py, mc))
               for j, (px, py) in enumerate(chips)]
        for cp in cps:
            cp.start()
        for j, (px, py) in enumerate(chips):
            ps = 2 * px + py
            _rcopy(x_ref.at[ps], o_ref.at[ps], ssem.at[j], rsem.at[j], (px, py, mc)).wait_recv()
        for cp in cps:
            cp.wait_send()

    return pl.pallas_call(
        body, name=name, out_shape=_sds(q.shape, q.dtype), in_specs=[ANY], out_specs=ANY,
        scratch_shapes=[pltpu.SemaphoreType.DMA((3,)), pltpu.SemaphoreType.DMA((3,))],
    )(q)


def sibling_join_halves(both, name, g=None):
    def body(x_ref, o_ref, ssem, rsem):
        del x_ref
        mx, my, mc = _me()
        sib = (mx, my, 1 - mc)
        o = o_ref if g is None else o_ref.at[g]
        cp = _rcopy(o.at[mc], o.at[mc], ssem, rsem, sib)
        cp.start()
        _rcopy(o.at[1 - mc], o.at[1 - mc], ssem, rsem, sib).wait_recv()
        cp.wait_send()

    return pl.pallas_call(
        body, name=name, out_shape=_sds(both.shape, both.dtype), in_specs=[ANY], out_specs=ANY,
        input_output_aliases={0: 0}, scratch_shapes=[pltpu.SemaphoreType.DMA, pltpu.SemaphoreType.DMA],
    )(both)


def add_own_half(buf, recv, wire, name):
    n, _, h, w = buf.shape
    th = _pick(h, _ROW_BLOCKS)

    def body(c_ref, a_ref, b_ref, o_ref):
        del c_ref
        o_ref[...] = (a_ref[...] + b_ref[...]).astype(o_ref.dtype)

    return pl.pallas_call(
        body, name=name, out_shape=_sds((n, h, w), wire),
        grid_spec=pltpu.PrefetchScalarGridSpec(
            num_scalar_prefetch=1, grid=(n, h // th),
            in_specs=[pl.BlockSpec((None, None, th, w), lambda k, r, c: (k, c[0], r, 0)),
                      pl.BlockSpec((None, th, w), lambda k, r, c: (k, r, 0))],
            out_specs=pl.BlockSpec((None, th, w), lambda k, r, c: (k, r, 0))),
    )(_idx(lax.axis_index("c")), buf, recv)


def sum_slots(q, r, name, dst=None, g=None):
    n, h, w = r.shape
    th = _pick(h, _ROW_BLOCKS)

    def body(i_ref, q_ref, r1, r2, r3, *rest):
        del i_ref
        rest[-1][...] = ((q_ref[...].astype(F32) + r1[...].astype(F32)) + r2[...].astype(F32)) + r3[...].astype(F32)

    def slot(d):
        return lambda i, ix: ((ix[0] + d) % N_CHIP, i, 0)

    idx = jnp.stack([2 * lax.axis_index("x") + lax.axis_index("y"), lax.axis_index("c")]).astype(jnp.int32)
    in_specs = [pl.BlockSpec((None, th, w), slot(d)) for d in (0, 1, 2, 3)]
    if dst is None:
        return pl.pallas_call(
            body, name=name, out_shape=_sds((2, h, w), F32),
            grid_spec=pltpu.PrefetchScalarGridSpec(
                num_scalar_prefetch=1, grid=(h // th,), in_specs=in_specs,
                out_specs=pl.BlockSpec((None, th, w), lambda i, ix: (ix[1], i, 0))),
        )(idx, q, r, r, r)
    return pl.pallas_call(
        body, name=name, out_shape=_sds(dst.shape, F32),
        grid_spec=pltpu.PrefetchScalarGridSpec(
            num_scalar_prefetch=1, grid=(h // th,), in_specs=in_specs + [ANY],
            out_specs=pl.BlockSpec((None, None, th, w), lambda i, ix: (g, ix[1], i, 0))),
        input_output_aliases={5: 0},
    )(idx, q, r, r, r, dst)


def reduce_scatter_chips(buf, tag, wire=F32, dst=None, g=None):
    recv = sibling_send_other_half(buf, f"rs_sib_{tag}")
    q = add_own_half(buf, recv, wire, f"rs_add2_{tag}")
    r = chips_all_to_all(q, f"rs_a2a_{tag}")
    red = sum_slots(q, r, f"rs_add4_{tag}", dst=dst, g=g)
    return sibling_join_halves(red, f"rs_join_{tag}", g=g)


class Layout:
    def __init__(self, n_ctx, n_lat):
        self.C, self.L = n_ctx, n_lat
        self.PS = n_ctx + n_lat
        self.T = 2 * self.PS
        self.tr = _pick(math.gcd(n_ctx, n_lat), (256, 128))
        self.bps = self.PS // self.tr
        self.cb = n_ctx // self.tr
        self.nblk = self.T // self.tr
        self.tm = _pick(self.T, (1152, 768, 512, 256, 128))
        self.tm2 = _pick(self.T, (2304, 1152, 768, 512, 256, 128))
        self.tc = _pick(self.T, (512, 256, 128))

    def seg(self, i):
        return jnp.where(i % self.bps < self.cb, 2, i // self.bps)


def rowwise(lay, name, fn, rows, segs=(), vecs=(), outs=(), sums=(), rider=None):
    tr, nblk = lay.tr, lay.nblk
    n_r, n_s, n_v, n_o = len(rows), len(segs), len(vecs), len(outs)
    lat_only = any(o[2:] for o in outs) or any(a.shape[0] != lay.T for a in rows)
    nsub = 1 if lat_only or nblk % 2 else 2
    tb = tr * nsub

    def body(*refs):
        ins = refs[:n_r + n_s + n_v]
        ors = refs[n_r + n_s + n_v:]
        for sub in range(nsub):
            rs = slice(sub * tr, (sub + 1) * tr)
            seg = lay.seg(pl.program_id(0) * nsub + sub)
            vals = [r[rs, :] for r in ins[:n_r]] + [r[seg] for r in ins[n_r:n_r + n_s]] + [r[...] for r in ins[n_r + n_s:]]
            res = fn(*vals)
            for k in range(n_o):
                ors[k][rs, :] = res[k].astype(ors[k].dtype)
            for k in range(len(sums)):
                ors[n_o + k][sub] = res[n_o + k]

    def all_rows(i):
        return (i, 0)

    def lat_rows(i):
        return ((i // lay.bps) * (lay.bps - lay.cb) + jnp.maximum(i % lay.bps - lay.cb, 0), 0)

    in_specs = [pl.BlockSpec((tb, a.shape[1]), all_rows if a.shape[0] == lay.T else lat_rows) for a in rows]
    in_specs += [pl.BlockSpec(a.shape, lambda i: (0, 0, 0)) for a in segs]
    in_specs += [pl.BlockSpec(a.shape, lambda i: (0, 0)) for a in vecs]
    out_shape = [_sds((2 * lay.L if o[2:] else lay.T, o[0]), o[1]) for o in outs]
    out_shape += [_sds((nblk, r, w), F32) for r, w in sums]
    out_specs = [pl.BlockSpec((tb, o[0]), lat_rows if o[2:] else all_rows) for o in outs]
    out_specs += [pl.BlockSpec((nsub, r, w), lambda i: (i, 0, 0)) for r, w in sums]
    sem = "arbitrary" if any(o[2:] for o in outs) else "parallel"
    if rider is None:
        return pl.pallas_call(body, name=name, out_shape=out_shape, grid=(nblk // nsub,), in_specs=in_specs,
                              out_specs=out_specs, compiler_params=_cp((sem,), VMEM_BIG))(*rows, *segs, *vecs)
    return _host_call(body, rider, name, (nblk // nsub,), in_specs, out_specs, out_shape, (*rows, *segs, *vecs), (sem,),
                      n_r + n_s + n_v, n_o + len(sums))


def modulate(lay, h, mod, k_shift, k_scale, name):
    def fn(hb, m):
        return (hb * (1.0 + m[k_scale:k_scale + 1]) + m[k_shift:k_shift + 1],)
    return rowwise(lay, name, fn, [h], segs=[mod], outs=[(D, BF16)])[0]


def resid_ln(lay, h, y, mod, k_gate, coef, lnv, name, nxt=None, rider=None):
    def fn(hb, yb, m, *rest):
        ln = rest[-1]
        z = ALPHA * hb + (coef * m[k_gate:k_gate + 1]) * yb
        mu = jnp.mean(z, axis=-1, keepdims=True)
        zc = z - mu
        var = jnp.mean(zc * zc, axis=-1, keepdims=True)
        rstd = lax.rsqrt(var + LN_EPS)
        xhat = zc * rstd
        out = xhat * ln[0:1] + ln[1:2]
        if nxt is None:
            return out, xhat, rstd
        mn = rest[0]
        return out, xhat, rstd, out * (1.0 + mn[nxt[2]:nxt[2] + 1]) + mn[nxt[1]:nxt[1] + 1]
    segs = [mod] if nxt is None else [mod, nxt[0]]
    outs = [(D, F32), (D, F32), (1, F32)] + ([] if nxt is None else [(D, BF16)])
    return rowwise(lay, name, fn, [h, y], segs=segs, vecs=[lnv], outs=outs, rider=rider)


def _ln_bwd_math(do, xh, rs, yb, gate, coef, ln):
    dxh = do * ln[0:1]
    m1 = jnp.mean(dxh, axis=-1, keepdims=True)
    m2 = jnp.mean(dxh * xh, axis=-1, keepdims=True)
    dz = rs * (dxh - m1 - xh * m2)
    s = jnp.concatenate([jnp.sum(do, axis=0, keepdims=True), jnp.sum(do * xh, axis=0, keepdims=True),
                         jnp.sum(coef * dz * yb, axis=0, keepdims=True)], axis=0)
    return (coef * gate) * dz, ALPHA * dz, s


def _mod_bwd_math(dr, dm, hb, scale):
    s = jnp.concatenate([jnp.sum(dm, axis=0, keepdims=True), jnp.sum(dm * hb, axis=0, keepdims=True)], axis=0)
    return dr + dm * (1.0 + scale), s


def ln_bwd(lay, dout, xhat, rstd, y, mod, k_gate, coef, lnv, name):
    def fn(do, xh, rs, yb, m, ln):
        return _ln_bwd_math(do, xh, rs, yb, m[k_gate:k_gate + 1], coef, ln)
    return rowwise(lay, name, fn, [dout, xhat, rstd, y], segs=[mod], vecs=[lnv],
                   outs=[(D, BF16), (D, F32)], sums=[(3, D)])


def mod_bwd(lay, dres, dhm, h, mod, k_scale, name, rider=None):
    def fn(dr, dm, hb, m):
        return _mod_bwd_math(dr, dm, hb, m[k_scale:k_scale + 1])
    return rowwise(lay, name, fn, [dres, dhm, h], segs=[mod], outs=[(D, F32, "lat")], sums=[(2, D)], rider=rider)


def modb_lnb(lay, dres, dhm, h, mod, k_scale, xhat, rstd, y, mod_p, k_gate, coef, lnv, name, rider=None):
    def fn(dr, dm, hb, xh, rs, yb, m, mp, ln):
        dh, s2 = _mod_bwd_math(dr, dm, hb, m[k_scale:k_scale + 1])
        dy, dres_p, s1 = _ln_bwd_math(dh, xh, rs, yb, mp[k_gate:k_gate + 1], coef, ln)
        return dy, dres_p, s1, s2
    return rowwise(lay, name, fn, [dres, dhm, h, xhat, rstd, y], segs=[mod, mod_p], vecs=[lnv],
                   outs=[(D, BF16), (D, F32)], sums=[(3, D), (2, D)], rider=rider)


def block_sums(lay, parts, name):
    nblk, r, w = parts.shape

    def body(p_ref, o_ref):
        acc = [None, None, None]
        for i in range(nblk):
            sg = 2 if i % lay.bps < lay.cb else i // lay.bps
            acc[sg] = p_ref[i] if acc[sg] is None else acc[sg] + p_ref[i]
        for k in range(3):
            o_ref[k] = acc[k]
        o_ref[3] = (acc[0] + acc[1]) + acc[2]

    return pl.pallas_call(body, name=name, out_shape=_sds((4, r, w), F32), in_specs=[VMEM_SPEC],
                          out_specs=VMEM_SPEC)(parts)


def mm_nn(a, b, name, out_dtype=F32, bias=None):
    m, k = a.shape
    n = b.shape[1]
    tm = _pick(m, (1152, 768, 512, 256, 128, 64, 32, 16, 8))
    tn = _pick(n, (1024, 768, 640, 512, 384, 256, 128))

    def body(*refs):
        if bias is None:
            a_ref, b_ref, o_ref = refs
            o_ref[...] = _nn(a_ref[...].astype(BF16), b_ref[...].astype(BF16)).astype(o_ref.dtype)
        else:
            a_ref, b_ref, c_ref, o_ref = refs
            o_ref[...] = (_nn(a_ref[...].astype(BF16), b_ref[...].astype(BF16)) + c_ref[...]).astype(o_ref.dtype)

    in_specs = [pl.BlockSpec((tm, k), lambda i, j: (i, 0)), pl.BlockSpec((k, tn), lambda i, j: (0, j))]
    ops = [a, b]
    if bias is not None:
        in_specs.append(pl.BlockSpec((1, tn), lambda i, j: (0, j)))
        ops.append(bias)
    return pl.pallas_call(body, name=name, out_shape=_sds((m, n), out_dtype), grid=(m // tm, n // tn),
                          in_specs=in_specs, out_specs=pl.BlockSpec((tm, tn), lambda i, j: (i, j)),
                          compiler_params=_cp(("parallel", "parallel"), VMEM_BIG))(*ops)


def mm_nt(a, b, name, out_dtype=F32, rider=None):
    m, k = a.shape
    n = b.shape[0]
    tm = _pick(m, (1152, 768, 512, 256, 128, 64, 32, 16, 8))
    tn = _pick(n, (1024, 768, 640, 512, 384, 256, 128))

    def body(a_ref, b_ref, o_ref):
        o_ref[...] = _nt(a_ref[...].astype(BF16), b_ref[...].astype(BF16)).astype(o_ref.dtype)

    res = _host_call(body, rider, name, (m // tm, n // tn),
                     [pl.BlockSpec((tm, k), lambda i, j: (i, 0)), pl.BlockSpec((tn, k), lambda i, j: (j, 0))],
                     [pl.BlockSpec((tm, tn), lambda i, j: (i, j))], [_sds((m, n), out_dtype)], (a, b),
                     ("parallel", "parallel"), 2, 1)
    return res[0] if rider is None else res


def mm_tn(a, b, name, rider=None):
    t, m = a.shape
    n = b.shape[1]
    tk = _pick(t, (1152, 768, 512, 256, 128, 64, 32, 16))
    tm = _pick(m, (512, 384, 256, 128))

    def body(a_ref, b_ref, o_ref):
        @pl.when(pl.program_id(1) == 0)
        def _():
            o_ref[...] = jnp.zeros_like(o_ref)
        o_ref[...] += _tn(a_ref[...].astype(BF16), b_ref[...].astype(BF16))

    res = _host_call(body, rider, name, (m // tm, t // tk),
                     [pl.BlockSpec((tk, tm), lambda i, k: (k, i)), pl.BlockSpec((tk, n), lambda i, k: (k, 0))],
                     [pl.BlockSpec((tm, n), lambda i, k: (i, 0))], [_sds((m, n), F32)], (a, b),
                     ("parallel", "arbitrary"), 2, 1)
    return res[0] if rider is None else res


class Rider:
    def __init__(self, ins, outs, aliases, nsem, start, wait):
        self.ins, self.outs, self.aliases, self.nsem, self.start, self.wait = ins, outs, aliases, nsem, start, wait


def _chips_of(mx, my):
    return [(1 - mx, my), (mx, 1 - my), (1 - mx, 1 - my)]


def rider_gather_d2d(buf):
    def start(ins, outs, ssem, rsem):
        o = outs[0]
        mx, my, mc = _me()
        for j, (px, py) in enumerate(_chips_of(mx, my)):
            ps = 2 * px + py
            _rcopy(o.at[ps, mc], o.at[ps, mc], ssem.at[j], rsem.at[j], (mx, my, 1 - mc)).start()

    def wait(ins, outs, ssem, rsem):
        o = outs[0]
        mx, my, mc = _me()
        sib = (mx, my, 1 - mc)
        for j, (px, py) in enumerate(_chips_of(mx, my)):
            ps = 2 * px + py
            _rcopy(o.at[ps, 1 - mc], o.at[ps, 1 - mc], ssem.at[j], rsem.at[j], sib).wait_recv()
        for j, (px, py) in enumerate(_chips_of(mx, my)):
            ps = 2 * px + py
            _rcopy(o.at[ps, mc], o.at[ps, mc], ssem.at[j], rsem.at[j], sib).wait_send()

    return Rider([buf], [_sds(buf.shape, buf.dtype)], {0: 0}, 3, start, wait)


def rider_reduce_sib(buf):
    n, _, h, w = buf.shape

    def start(ins, outs, ssem, rsem):
        mx, my, mc = _me()
        for k in range(N_CHIP):
            _rcopy(ins[0].at[k, 1 - mc], outs[0].at[k], ssem.at[k], rsem.at[k], (mx, my, 1 - mc)).start()

    def wait(ins, outs, ssem, rsem):
        mx, my, mc = _me()
        for k in range(N_CHIP):
            _rcopy(ins[0].at[k, 1 - mc], outs[0].at[k], ssem.at[k], rsem.at[k], (mx, my, 1 - mc)).wait_recv()
        for k in range(N_CHIP):
            _rcopy(ins[0].at[k, 1 - mc], outs[0].at[k], ssem.at[k], rsem.at[k], (mx, my, 1 - mc)).wait_send()

    return Rider([buf], [_sds((n, h, w), buf.dtype)], {}, N_CHIP, start, wait)


def rider_gather_xy(buf):
    def peers():
        mx, my, mc = _me()
        return 2 * mx + my, mc, [(1 - mx, my), (mx, 1 - my)]

    def start(ins, outs, ssem, rsem):
        o = outs[0]
        s, mc, nb = peers()
        for j, (px, py) in enumerate(nb):
            _rcopy(o.at[s, mc], o.at[s, mc], ssem.at[j], rsem.at[j], (px, py, mc)).start()

    def wait(ins, outs, ssem, rsem):
        o = outs[0]
        s, mc, nb = peers()
        for j, (px, py) in enumerate(nb):
            _rcopy(o.at[2 * px + py, mc], o.at[2 * px + py, mc], ssem.at[j], rsem.at[j], (px, py, mc)).wait_recv()
        for j, (px, py) in enumerate(nb):
            _rcopy(o.at[s, mc], o.at[s, mc], ssem.at[j], rsem.at[j], (px, py, mc)).wait_send()

    return Rider([buf], [_sds(buf.shape, buf.dtype)], {0: 0}, 2, start, wait)


def rider_gather_fwd(buf):
    h2 = buf.shape[2] // 2
    lo, hi = pl.ds(0, h2), pl.ds(h2, buf.shape[2] - h2)

    def start(ins, outs, ssem, rsem):
        o = outs[0]
        mx, my, mc = _me()
        xs, ys = 2 * (1 - mx) + my, 2 * mx + (1 - my)
        _rcopy(o.at[xs, mc, lo], o.at[xs, mc, lo], ssem.at[0], rsem.at[0], (mx, 1 - my, mc)).start()
        _rcopy(o.at[ys, mc, hi], o.at[ys, mc, hi], ssem.at[1], rsem.at[1], (1 - mx, my, mc)).start()

    def wait(ins, outs, ssem, rsem):
        o = outs[0]
        mx, my, mc = _me()
        xs, ys, ds = 2 * (1 - mx) + my, 2 * mx + (1 - my), 2 * (1 - mx) + (1 - my)
        _rcopy(o.at[ds, mc, lo], o.at[ds, mc, lo], ssem.at[0], rsem.at[0], (mx, 1 - my, mc)).wait_recv()
        _rcopy(o.at[ds, mc, hi], o.at[ds, mc, hi], ssem.at[1], rsem.at[1], (1 - mx, my, mc)).wait_recv()
        _rcopy(o.at[xs, mc, lo], o.at[xs, mc, lo], ssem.at[0], rsem.at[0], (mx, 1 - my, mc)).wait_send()
        _rcopy(o.at[ys, mc, hi], o.at[ys, mc, hi], ssem.at[1], rsem.at[1], (1 - mx, my, mc)).wait_send()

    return Rider([buf], [_sds(buf.shape, buf.dtype)], {0: 0}, 2, start, wait)


def rider_reduce_copy(q, j, r=None):
    def peer():
        mx, my, mc = _me()
        px, py = _chips_of(mx, my)[j]
        return 2 * mx + my, 2 * px + py, (px, py, mc)

    def start(ins, outs, ssem, rsem):
        s, ps, dev = peer()
        _rcopy(ins[0].at[ps], outs[0].at[s], ssem.at[0], rsem.at[0], dev).start()

    def wait(ins, outs, ssem, rsem):
        s, ps, dev = peer()
        _rcopy(ins[0].at[ps], outs[0].at[ps], ssem.at[0], rsem.at[0], dev).wait_recv()
        _rcopy(ins[0].at[ps], outs[0].at[s], ssem.at[0], rsem.at[0], dev).wait_send()

    if r is None:
        return Rider([q], [_sds(q.shape, q.dtype)], {}, 1, start, wait)
    return Rider([q, r], [_sds(q.shape, q.dtype)], {1: 0}, 1, start, wait)


def rider_reduce_copies(q):
    def start(ins, outs, ssem, rsem):
        mx, my, mc = _me()
        s = 2 * mx + my
        for j, (px, py) in enumerate(_chips_of(mx, my)):
            _rcopy(ins[0].at[2 * px + py], outs[0].at[s], ssem.at[j], rsem.at[j], (px, py, mc)).start()

    def wait(ins, outs, ssem, rsem):
        mx, my, mc = _me()
        s = 2 * mx + my
        for j, (px, py) in enumerate(_chips_of(mx, my)):
            ps = 2 * px + py
            _rcopy(ins[0].at[ps], outs[0].at[ps], ssem.at[j], rsem.at[j], (px, py, mc)).wait_recv()
        for j, (px, py) in enumerate(_chips_of(mx, my)):
            _rcopy(ins[0].at[2 * px + py], outs[0].at[s], ssem.at[j], rsem.at[j], (px, py, mc)).wait_send()

    return Rider([q], [_sds(q.shape, q.dtype)], {}, 3, start, wait)


def rider_join(buf, g=None):
    def start(ins, outs, ssem, rsem):
        o = outs[0] if g is None else outs[0].at[g]
        mx, my, mc = _me()
        _rcopy(o.at[mc], o.at[mc], ssem.at[0], rsem.at[0], (mx, my, 1 - mc)).start()

    def wait(ins, outs, ssem, rsem):
        o = outs[0] if g is None else outs[0].at[g]
        mx, my, mc = _me()
        _rcopy(o.at[1 - mc], o.at[1 - mc], ssem.at[0], rsem.at[0], (mx, my, 1 - mc)).wait_recv()
        _rcopy(o.at[mc], o.at[mc], ssem.at[0], rsem.at[0], (mx, my, 1 - mc)).wait_send()

    return Rider([buf], [_sds(buf.shape, buf.dtype)], {0: 0}, 1, start, wait)


def _host_call(body, rider, name, grid, in_specs, out_specs, out_shape, operands, sem, n_in, n_out, aliases=None):
    aliases = dict(aliases or {})
    if rider is None:
        return pl.pallas_call(body, name=name, out_shape=out_shape, grid=grid, in_specs=in_specs, out_specs=out_specs,
                              input_output_aliases=aliases, compiler_params=_cp(sem, VMEM_BIG))(*operands)
    n_ri, n_ro = len(rider.ins), len(rider.outs)
    aliases.update({n_in + a: n_out + b for a, b in rider.aliases.items()})

    def hosted(*refs):
        ins, r_in = refs[:n_in], refs[n_in:n_in + n_ri]
        outs, r_out = refs[n_in + n_ri:n_in + n_ri + n_out], refs[n_in + n_ri + n_out:n_in + n_ri + n_out + n_ro]
        ssem, rsem = refs[-2], refs[-1]
        first = functools.reduce(lambda a, b: a & b, [pl.program_id(k) == 0 for k in range(len(grid))])
        last = functools.reduce(lambda a, b: a & b, [pl.program_id(k) == grid[k] - 1 for k in range(len(grid))])

        @pl.when(first)
        def _():
            rider.start(r_in, r_out, ssem, rsem)
        body(*ins, *outs)

        @pl.when(last)
        def _():
            rider.wait(r_in, r_out, ssem, rsem)

    return pl.pallas_call(
        hosted, name=name, out_shape=list(out_shape) + list(rider.outs), grid=grid,
        in_specs=list(in_specs) + [ANY] * n_ri, out_specs=list(out_specs) + [ANY] * n_ro,
        input_output_aliases=aliases,
        scratch_shapes=[pltpu.SemaphoreType.DMA((rider.nsem,)), pltpu.SemaphoreType.DMA((rider.nsem,))],
        compiler_params=_cp(("arbitrary",) * len(grid), VMEM_BIG))(*operands, *rider.ins)


def ffn_up(lay, hm, wbuf, ig, iu, ns, name, rider=None):
    tm = lay.tm

    def body(h_ref, wg_ref, wu_ref, up_ref, sl_ref, a_ref):
        hb = h_ref[...]
        g = _nt(hb, wg_ref[0])
        u = _nt(hb, wu_ref[0])
        sg = _sigmoid(g)
        sl = g * sg
        up_ref[0] = (u * (sg + sl * (1.0 - sg))).astype(BF16)
        sl_ref[0] = sl.astype(BF16)
        a_ref[0] = (sl * u).astype(BF16)

    spec_o = pl.BlockSpec((1, tm, ns), lambda s, i: (s, i, 0))
    return _host_call(
        body, rider, name, (N_CHIP, lay.T // tm),
        [pl.BlockSpec((tm, D), lambda s, i: (i, 0)), pl.BlockSpec((1, ns, D), lambda s, i: (s, ig, 0)),
         pl.BlockSpec((1, ns, D), lambda s, i: (s, iu, 0))],
        [spec_o] * 3, [_sds((N_CHIP, lay.T, ns), BF16)] * 3, (hm, wbuf, wbuf), ("parallel", "parallel"), 3, 3)


def slab_nn_acc(lay, zs, wbuf, idxs, ns, name, rider=None):
    tm = lay.tm2
    npair = len(zs)

    def body(*refs):
        o_ref = refs[-1]

        @pl.when(pl.program_id(1) == 0)
        def _():
            o_ref[...] = jnp.zeros_like(o_ref)
        acc = _nn(refs[0][0], refs[npair][0])
        for p in range(1, npair):
            acc += _nn(refs[p][0], refs[npair + p][0])
        o_ref[...] += acc

    in_specs = [pl.BlockSpec((1, tm, ns), lambda i, s: (s, i, 0)) for _ in zs]
    in_specs += [pl.BlockSpec((1, ns, D), functools.partial(lambda i, s, q: (s, q, 0), q=q)) for q in idxs]
    return _host_call(body, rider, name, (lay.T // tm, N_CHIP), in_specs, [pl.BlockSpec((tm, D), lambda i, s: (i, 0))],
                      [_sds((lay.T, D), F32)], (*zs, *([wbuf] * npair)), ("parallel", "arbitrary"), 2 * npair, 1)


def ffn_bwd_da(lay, dy, wbuf, idn, up, sl, ns, name, rider=None):
    tm = lay.tm

    def body(dy_ref, wd_ref, up_ref, sl_ref, dg_ref, du_ref):
        da = _nt(dy_ref[...], wd_ref[0])
        dg_ref[0] = (da * up_ref[0].astype(F32)).astype(BF16)
        du_ref[0] = (da * sl_ref[0].astype(F32)).astype(BF16)

    spec_z = pl.BlockSpec((1, tm, ns), lambda s, i: (s, i, 0))
    return _host_call(
        body, rider, name, (N_CHIP, lay.T // tm),
        [pl.BlockSpec((tm, D), lambda s, i: (i, 0)), pl.BlockSpec((1, ns, D), lambda s, i: (s, idn, 0)), spec_z, spec_z],
        [spec_z] * 2, [_sds((N_CHIP, lay.T, ns), BF16)] * 2, (dy, wbuf, up, sl), ("parallel", "parallel"), 4, 2)


def slab_tn(lay, z, x, gbuf, idx, ns, name, rider=None):
    tk = lay.tm2

    def body(z_ref, x_ref, g_in, o_ref):
        del g_in

        @pl.when(pl.program_id(1) == 0)
        def _():
            o_ref[...] = jnp.zeros_like(o_ref)
        o_ref[0] += _tn(z_ref[0], x_ref[...])

    return _host_call(
        body, rider, name, (N_CHIP, lay.T // tk),
        [pl.BlockSpec((1, tk, ns), lambda s, k: (s, k, 0)), pl.BlockSpec((tk, D), lambda s, k: (k, 0)), ANY],
        [pl.BlockSpec((1, ns, D), lambda s, k: (s, idx, 0))], [_sds(gbuf.shape, F32)], (z, x, gbuf),
        ("parallel", "arbitrary"), 3, 1, aliases={2: 0})


Q0, K0, V0, U0, QR0, KR0, PEXT = 0, 512, 640, 768, 1280, 1792, 1920


def rope_fwd(lay, p, cos, sin, name):
    def fn(pb, cs, sn):
        cs4 = jnp.concatenate([cs] * 4, axis=1)
        sn4 = jnp.concatenate([sn] * 4, axis=1)
        qr = pb[:, Q0:K0] * cs4 + pb[:, QR0:KR0] * sn4
        kr = pb[:, K0:V0] * cs + pb[:, KR0:PEXT] * sn
        return qr, kr, pb[:, V0:U0], pb[:, U0:QR0]
    return rowwise(lay, name, fn, [p, cos, sin], outs=[(ATT_W, BF16), (KV_W, BF16), (KV_W, BF16), (POOL_W, F32)])


def rope_bwd(lay, dqr, dkr, dv, du, cos, sin, name):
    def fn(dq, dk, dvb, dub, cs, sn):
        cs4 = jnp.concatenate([cs] * 4, axis=1)
        sn4 = jnp.concatenate([sn] * 4, axis=1)
        return (jnp.concatenate([dq * cs4, dk * cs, dvb, dub, dq * sn4, dk * sn], axis=1),)
    return rowwise(lay, name, fn, [dqr, dkr, dv, du, cos, sin], outs=[(PEXT, BF16)])[0]


def _attn_specs(lay):
    nbs, cbk, lbk = lay.PS // BLK, lay.C // BLK, lay.L // BLK

    def kv_map(j):
        return lambda s, n: (s * nbs + cbk + jnp.clip(n - cbk + j - 1, 0, lbk - 1), 0)

    win = [pl.BlockSpec((BLK, KV_W), kv_map(j)) for j in range(3)]
    ctx = pl.BlockSpec((lay.C, KV_W), lambda s, n: (s * (lay.PS // lay.C), 0))
    return nbs, cbk, lbk, win, ctx


def _attn_masks(n, cbk, lbk):
    row = lax.broadcasted_iota(jnp.int32, (BLK, BLK), 0)
    col = lax.broadcasted_iota(jnp.int32, (BLK, BLK), 1)
    m = n - cbk
    lat = n >= cbk
    valid = [lat & (m >= 1) & (col >= row), lat & (col >= 0), lat & (m <= lbk - 2) & (col <= row)]
    lane_lo = lax.broadcasted_iota(jnp.int32, (BLK, 2 * HEAD_DIM), 1) < HEAD_DIM
    return valid, lane_lo


def attn_fwd(lay, qr, kr, vb, sink_tab, name):
    nbs, cbk, lbk, win, ctx = _attn_specs(lay)

    def body(q_ref, k0, k1, k2, kc_ref, v0, v1, v2, vc_ref, sk_ref, o_ref, l_ref):
        n = pl.program_id(1)
        valid, lane_lo = _attn_masks(n, cbk, lbk)
        valid4 = [jnp.concatenate([v] * 4, axis=0) for v in valid]
        ks = [k0[...], k1[...], k2[...]]
        vs = [v0[...], v1[...], v2[...]]
        kc, vc = kc_ref[...], vc_ref[...]
        q2s = [q_ref[:, p * 128:(p + 1) * 128] for p in range(4)]
        outs, lses = [], []
        for hh in range(2):
            sel = lane_lo == (hh == 0)
            qm = jnp.concatenate([jnp.where(sel, q2, jnp.zeros_like(q2)) for q2 in q2s], axis=0)
            sk = jnp.concatenate([jnp.broadcast_to(sk_ref[p:p + 1, hh * HEAD_DIM:hh * HEAD_DIM + 1], (BLK, 1))
                                  for p in range(4)], axis=0)
            sw = [jnp.where(valid4[j], _nt(qm, ks[j]) * ATT_SCALE, NEG_INF) for j in range(3)]
            sc = _nt(qm, kc) * ATT_SCALE
            mx = jnp.maximum(jnp.maximum(jnp.maximum(sw[0].max(-1, keepdims=True), sw[1].max(-1, keepdims=True)),
                                         jnp.maximum(sw[2].max(-1, keepdims=True), sc.max(-1, keepdims=True))), sk)
            ew = [jnp.exp(s - mx) for s in sw]
            ec = jnp.exp(sc - mx)
            den = ew[0].sum(-1, keepdims=True) + ew[1].sum(-1, keepdims=True) + ew[2].sum(-1, keepdims=True)
            den = den + ec.sum(-1, keepdims=True) + jnp.exp(sk - mx)
            o = _nn((ec / den).astype(BF16), vc)
            for j in range(3):
                o += _nn((ew[j] / den).astype(BF16), vs[j])
            outs.append(o)
            lses.append(mx + jnp.log(den))
        for p in range(4):
            rows = slice(p * BLK, (p + 1) * BLK)
            o_ref[:, p * 128:(p + 1) * 128] = jnp.where(lane_lo, outs[0][rows], outs[1][rows]).astype(o_ref.dtype)
            l_ref[:, p * 128:(p + 1) * 128] = jnp.where(lane_lo, jnp.broadcast_to(lses[0][rows], (BLK, 128)),
                                                        jnp.broadcast_to(lses[1][rows], (BLK, 128)))

    qspec = pl.BlockSpec((BLK, ATT_W), lambda s, n: (s * nbs + n, 0))
    return pl.pallas_call(
        body, name=name, out_shape=[_sds((lay.T, ATT_W), BF16), _sds((lay.T, ATT_W), F32)], grid=(2, nbs),
        in_specs=[qspec] + win + [ctx] + win + [ctx] + [pl.BlockSpec((8, 128), lambda s, n: (0, 0))],
        out_specs=[qspec, qspec], compiler_params=_cp(("parallel", "parallel")))(qr, kr, kr, kr, kr, vb, vb, vb, vb, sink_tab)


def attn_bwd(lay, qr, kr, vb, sink_tab, lse, datt, name, rider=None):
    nbs, cbk, lbk, win, ctx = _attn_specs(lay)
    C, PS = lay.C, lay.PS

    def body(q_ref, k0, k1, k2, kc_ref, v0, v1, v2, vc_ref, sk_ref, l_ref, do_ref, dq_ref, dk_ref, dv_ref, ds_ref):
        n = pl.program_id(1)
        valid, lane_lo = _attn_masks(n, cbk, lbk)

        @pl.when(n == 0)
        def _():
            dk_ref[...] = jnp.zeros_like(dk_ref)
            dv_ref[...] = jnp.zeros_like(dv_ref)
            ds_ref[...] = jnp.zeros_like(ds_ref)

        ks = [k0[...], k1[...], k2[...], kc_ref[...]]
        vs = [v0[...], v1[...], v2[...], vc_ref[...]]
        valid4 = [jnp.concatenate([v] * 4, axis=0) for v in valid]
        dks = [jnp.zeros((BLK, KV_W), F32)] * 3 + [jnp.zeros((C, KV_W), F32)]
        dvs = list(dks)
        q2s = [q_ref[:, p * 128:(p + 1) * 128] for p in range(4)]
        do2s = [do_ref[:, p * 128:(p + 1) * 128].astype(BF16) for p in range(4)]
        lse2s = [l_ref[:, p * 128:(p + 1) * 128] for p in range(4)]
        dq_h, dd_h = [], []
        for hh in range(2):
            sel = lane_lo == (hh == 0)
            qm = jnp.concatenate([jnp.where(sel, q2, jnp.zeros_like(q2)) for q2 in q2s], axis=0)
            dom = jnp.concatenate([jnp.where(sel, d2, jnp.zeros_like(d2)) for d2 in do2s], axis=0)
            lse_h = jnp.concatenate([l2[:, hh * HEAD_DIM:hh * HEAD_DIM + 1] for l2 in lse2s], axis=0)
            ps, dps = [], []
            for j in range(4):
                s = _nt(qm, ks[j]) * ATT_SCALE
                if j < 3:
                    s = jnp.where(valid4[j], s, NEG_INF)
                ps.append(jnp.exp(s - lse_h))
                dps.append(_nt(dom, vs[j]))
            dd = (ps[0] * dps[0]).sum(-1, keepdims=True) + (ps[1] * dps[1]).sum(-1, keepdims=True)
            dd = dd + (ps[2] * dps[2]).sum(-1, keepdims=True) + (ps[3] * dps[3]).sum(-1, keepdims=True)
            dq = jnp.zeros((4 * BLK, 128), F32)
            for j in range(4):
                dsb = (ps[j] * (dps[j] - dd) * ATT_SCALE).astype(BF16)
                dq += _nn(dsb, ks[j])
                dks[j] = dks[j] + _tn(dsb, qm)
                dvs[j] = dvs[j] + _tn(ps[j].astype(BF16), dom)
            dq_h.append(dq)
            dd_h.append(dd)
        for p in range(4):
            sl = slice(p * 128, (p + 1) * 128)
            rows = slice(p * BLK, (p + 1) * BLK)
            dq_ref[:, sl] = jnp.where(lane_lo, dq_h[0][rows], dq_h[1][rows])
            dd2 = jnp.where(lane_lo, jnp.broadcast_to(dd_h[0][rows], (BLK, 128)), jnp.broadcast_to(dd_h[1][rows], (BLK, 128)))
            psink = jnp.exp(sk_ref[p:p + 1, :] - lse2s[p])
            ds_ref[0, p:p + 1, :] += -jnp.sum(psink * dd2, axis=0, keepdims=True)
        dk_ref[0:C, :] += dks[3]
        dv_ref[0:C, :] += dvs[3]
        for j in range(3):
            r0 = pl.multiple_of((cbk + jnp.clip(n - cbk + j - 1, 0, lbk - 1)) * BLK, BLK)
            dk_ref[pl.ds(r0, BLK), :] += dks[j]
            dv_ref[pl.ds(r0, BLK), :] += dvs[j]

    qspec = pl.BlockSpec((BLK, ATT_W), lambda s, n: (s * nbs + n, 0))
    kvout = pl.BlockSpec((PS, KV_W), lambda s, n: (s, 0))
    return _host_call(
        body, rider, name, (2, nbs),
        [qspec] + win + [ctx] + win + [ctx] + [pl.BlockSpec((8, 128), lambda s, n: (0, 0)), qspec, qspec],
        [qspec, kvout, kvout, pl.BlockSpec((1, 8, 128), lambda s, n: (s, 0, 0))],
        [_sds((lay.T, ATT_W), F32), _sds((lay.T, KV_W), F32), _sds((lay.T, KV_W), F32), _sds((2, 8, 128), F32)],
        (qr, kr, kr, kr, kr, vb, vb, vb, vb, sink_tab, lse, datt), ("parallel", "arbitrary"), 12, 4)


def _winsum(x, r):
    n = x.shape[0]
    t = lax.broadcasted_iota(jnp.int32, x.shape, 0)
    acc = x
    for o in range(1, r + 1):
        acc = acc + jnp.where(t >= o, pltpu.roll(x, o, 0), 0.0) + jnp.where(t < n - o, pltpu.roll(x, n - o, 0), 0.0)
    return acc


def _wincount(n, r):
    t = lax.broadcasted_iota(jnp.int32, (n, 128), 0)
    return (jnp.minimum(t + r, n - 1) - jnp.maximum(t - r, 0) + 1).astype(F32)


def pool_fwd(lay, u, w_pool, scale, name):
    segs = [(0, lay.C), (lay.C, lay.L)]

    def body(u_ref, w_ref, s_ref, o_ref):
        for r0, n in segs:
            for g, wd in enumerate(POOL_WINDOWS):
                sl = slice(g * 128, (g + 1) * 128)
                x = u_ref[r0:r0 + n, sl]
                d = _winsum(x, wd // 2) / _wincount(n, wd // 2) - x
                y = _nn(d.astype(BF16), w_ref[g].astype(BF16)) * s_ref[:, sl]
                o_ref[r0:r0 + n, sl] = y.astype(o_ref.dtype)

    spec = pl.BlockSpec((lay.PS, POOL_W), lambda s: (s, 0))
    return pl.pallas_call(
        body, name=name, out_shape=_sds((lay.T, POOL_W), BF16), grid=(2,),
        in_specs=[spec, pl.BlockSpec(w_pool.shape, lambda s: (0, 0, 0)), pl.BlockSpec((1, POOL_W), lambda s: (0, 0))],
        out_specs=spec, compiler_params=_cp(("parallel",), VMEM_BIG))(u, w_pool, scale)


def pool_bwd(lay, u, dcat, w_pool, scale, name):
    segs = [(0, lay.C), (lay.C, lay.L)]

    def body(u_ref, dp_ref, w_ref, s_ref, du_ref, dw_ref, dsc_ref):
        for g, wd in enumerate(POOL_WINDOWS):
            sl = slice(g * 128, (g + 1) * 128)
            wb = w_ref[g].astype(BF16)
            dw = jnp.zeros((128, 128), F32)
            dsc = jnp.zeros((1, 128), F32)
            for r0, n in segs:
                x = u_ref[r0:r0 + n, sl]
                cnt = _wincount(n, wd // 2)
                d = (_winsum(x, wd // 2) / cnt - x).astype(BF16)
                dp = dp_ref[r0:r0 + n, sl]
                dsc += jnp.sum(_nn(d, wb) * dp, axis=0, keepdims=True)
                dyp = (dp * s_ref[:, sl]).astype(BF16)
                dw += _tn(d, dyp)
                dd = _nt(dyp, wb)
                du_ref[r0:r0 + n, sl] = _winsum(dd / cnt, wd // 2) - dd
            dw_ref[0, g] = dw
            dsc_ref[0, :, sl] = dsc

    spec = pl.BlockSpec((lay.PS, POOL_W), lambda s: (s, 0))
    return pl.pallas_call(
        body, name=name,
        out_shape=[_sds((lay.T, POOL_W), F32), _sds((2, 4, 128, 128), F32), _sds((2, 1, POOL_W), F32)], grid=(2,),
        in_specs=[spec, pl.BlockSpec((lay.PS, POOL_W), lambda s: (s, 1)), pl.BlockSpec(w_pool.shape, lambda s: (0, 0, 0)),
                  pl.BlockSpec((1, POOL_W), lambda s: (0, 0))],
        out_specs=[spec, pl.BlockSpec((1, 4, 128, 128), lambda s: (s, 0, 0, 0)), pl.BlockSpec((1, 1, POOL_W), lambda s: (s, 0, 0))],
        compiler_params=_cp(("parallel",), VMEM_BIG))(u, dcat, w_pool, scale)


CONV_OFFS = (-1, 0, 1, 2)
CW = 256


def _shift_rows(x, o):
    if o == 0:
        return x
    n = x.shape[0]
    t = lax.broadcasted_iota(jnp.int32, x.shape, 0)
    if o < 0:
        return jnp.where(t >= -o, pltpu.roll(x, -o, 0), 0.0)
    return jnp.where(t < n - o, pltpu.roll(x, n - o, 0), 0.0)


def conv_fwd(lay, p, col0, w, b, name):
    segs = [(0, lay.C), (lay.C, lay.L)]
    cb0 = col0 // CW

    def body(x_ref, w_ref, b_ref, o_ref):
        for r0, n in segs:
            x = x_ref[r0:r0 + n, :]
            y = jnp.broadcast_to(b_ref[...], x.shape)
            for k, o in enumerate(CONV_OFFS):
                y = y + _shift_rows(x, o) * w_ref[k:k + 1, :]
            o_ref[r0:r0 + n, :] = y

    return pl.pallas_call(
        body, name=name, out_shape=_sds((lay.T, D), F32), grid=(2, D // CW),
        in_specs=[pl.BlockSpec((lay.PS, CW), lambda s, j: (s, cb0 + j)), pl.BlockSpec((4, CW), lambda s, j: (0, j)),
                  pl.BlockSpec((1, CW), lambda s, j: (0, j))],
        out_specs=pl.BlockSpec((lay.PS, CW), lambda s, j: (s, j)),
        compiler_params=_cp(("parallel", "parallel")))(p, w, b)


def conv_bwd(lay, p, col0, w, duc, name):
    segs = [(0, lay.C), (lay.C, lay.L)]
    cb0 = col0 // CW

    def body(x_ref, w_ref, g_ref, du_ref, dw_ref, db_ref):
        dws = [jnp.zeros((1, CW), F32)] * 4
        db = jnp.zeros((1, CW), F32)
        for r0, n in segs:
            x = x_ref[r0:r0 + n, :]
            g = g_ref[r0:r0 + n, :]
            du = jnp.zeros_like(g)
            for k, o in enumerate(CONV_OFFS):
                du = du + _shift_rows(g, -o) * w_ref[k:k + 1, :]
                dws[k] = dws[k] + jnp.sum(g * _shift_rows(x, o), axis=0, keepdims=True)
            db = db + jnp.sum(g, axis=0, keepdims=True)
            du_ref[r0:r0 + n, :] = du.astype(du_ref.dtype)
        dw_ref[0] = jnp.concatenate(dws, axis=0)
        db_ref[0] = db

    return pl.pallas_call(
        body, name=name, out_shape=[_sds((lay.T, D), BF16), _sds((2, 4, D), F32), _sds((2, 1, D), F32)], grid=(2, D // CW),
        in_specs=[pl.BlockSpec((lay.PS, CW), lambda s, j: (s, cb0 + j)), pl.BlockSpec((4, CW), lambda s, j: (0, j)),
                  pl.BlockSpec((lay.PS, CW), lambda s, j: (s, j))],
        out_specs=[pl.BlockSpec((lay.PS, CW), lambda s, j: (s, j)), pl.BlockSpec((1, 4, CW), lambda s, j: (s, 0, j)),
                   pl.BlockSpec((1, 1, CW), lambda s, j: (s, 0, j))],
        compiler_params=_cp(("parallel", "parallel")))(p, w, duc)


def _softplus_neg(lam):
    z = -lam
    w = jnp.exp(-jnp.abs(z))
    log1p = jnp.where(w < 1e-2, w * (1.0 - w * (0.5 - w / 3.0)), jnp.log(1.0 + w))
    return jnp.maximum(z, 0.0) + log1p, -_sigmoid(z)


def _neg_expm1(x):
    series = -x * (1.0 + x * (0.5 + x * (1.0 / 6.0 + x * (1.0 / 24.0 + x * (1.0 / 120.0)))))
    return jnp.where(x > -0.05, series, 1.0 - jnp.exp(x))


def _lru_gates(x, xb, wa, wx, ba, bx, lam):
    r = _sigmoid(_nn(xb, wa.astype(BF16)) + ba)
    gi = _sigmoid(_nn(xb, wx.astype(BF16)) + bx)
    sp, dsp = _softplus_neg(lam)
    la = -LRU_C * r * sp
    a = jnp.exp(la)
    sq = jnp.sqrt(_neg_expm1(2.0 * la))
    return r, gi, sp, dsp, a, sq


def lru_coeffs(lay, uc, wa, wx, vec, name):
    tr = lay.tc

    def body(x_ref, wa_ref, wx_ref, v_ref, a_ref, b_ref):
        for h in range(8):
            sl = slice(h * 128, (h + 1) * 128)
            x = x_ref[:, sl]
            xb = x.astype(BF16)
            for d in range(2):
                _, gi, _, _, a, sq = _lru_gates(x, xb, wa_ref[d, h], wx_ref[d, h], v_ref[d:d + 1, sl],
                                                v_ref[2 + d:3 + d, sl], v_ref[4 + d:5 + d, sl])
                a_ref[d, h] = a
                b_ref[d, h] = sq * (gi * x)

    wspec = pl.BlockSpec((2, 8, 128, 128), lambda i: (0, 0, 0, 0))
    ospec = pl.BlockSpec((2, 8, tr, 128), lambda i: (0, 0, i, 0))
    return pl.pallas_call(
        body, name=name, out_shape=[_sds((2, 8, lay.T, 128), F32)] * 2, grid=(lay.T // tr,),
        in_specs=[pl.BlockSpec((tr, D), lambda i: (i, 0)), wspec, wspec, pl.BlockSpec((6, D), lambda i: (0, 0))],
        out_specs=[ospec, ospec], compiler_params=_cp(("parallel",), VMEM_BIG))(uc, wa, wx, vec)


def lru_coeffs_bwd(lay, uc, wa, wx, vec, da, db, name, rider=None):
    tr = lay.tc

    def body(x_ref, wa_ref, wx_ref, v_ref, da_ref, db_ref, dx_ref, dwa_ref, dwx_ref, dv_ref):
        @pl.when(pl.program_id(0) == 0)
        def _():
            dwa_ref[...] = jnp.zeros_like(dwa_ref)
            dwx_ref[...] = jnp.zeros_like(dwx_ref)
            dv_ref[...] = jnp.zeros_like(dv_ref)

        for h in range(8):
            sl = slice(h * 128, (h + 1) * 128)
            x = x_ref[:, sl]
            xb = x.astype(BF16)
            dx = jnp.zeros_like(x)
            for d in range(2):
                wab, wxb = wa_ref[d, h].astype(BF16), wx_ref[d, h].astype(BF16)
                r, gi, sp, dsp, a, sq = _lru_gates(x, xb, wa_ref[d, h], wx_ref[d, h], v_ref[d:d + 1, sl],
                                                   v_ref[2 + d:3 + d, sl], v_ref[4 + d:5 + d, sl])
                dbv, dav = db_ref[d, h], da_ref[d, h]
                t1 = dbv * sq
                dgi = t1 * x
                dx = dx + t1 * gi
                dla = dav * a - (dbv * gi * x) * (a * a) / sq
                dr = dla * (-LRU_C * sp)
                dlam = jnp.sum(dla * (-LRU_C * r), axis=0, keepdims=True) * dsp
                dpa = dr * r * (1.0 - r)
                dpx = dgi * gi * (1.0 - gi)
                dpab, dpxb = dpa.astype(BF16), dpx.astype(BF16)
                dwa_ref[d, h] += _tn(xb, dpab)
                dwx_ref[d, h] += _tn(xb, dpxb)
                dx = dx + _nt(dpab, wab) + _nt(dpxb, wxb)
                dv_ref[d:d + 1, sl] += jnp.sum(dpa, axis=0, keepdims=True)
                dv_ref[2 + d:3 + d, sl] += jnp.sum(dpx, axis=0, keepdims=True)
                dv_ref[4 + d:5 + d, sl] += dlam
            dx_ref[:, sl] = dx

    wspec = pl.BlockSpec((2, 8, 128, 128), lambda i: (0, 0, 0, 0))
    gspec = pl.BlockSpec((2, 8, tr, 128), lambda i: (0, 0, i, 0))
    vspec = pl.BlockSpec((6, D), lambda i: (0, 0))
    xspec = pl.BlockSpec((tr, D), lambda i: (i, 0))
    return _host_call(
        body, rider, name, (lay.T // tr,), [xspec, wspec, wspec, vspec, gspec, gspec], [xspec, wspec, wspec, vspec],
        [_sds((lay.T, D), F32), _sds((2, 8, 128, 128), F32), _sds((2, 8, 128, 128), F32), _sds((6, D), F32)],
        (uc, wa, wx, vec, da, db), ("arbitrary",), 6, 4)


GB = 2
SCAN_UNROLL = 4


def _tile_scan(a, b, up):
    t = lax.broadcasted_iota(jnp.int32, a.shape, 0)
    for d in (1, 2, 4):
        sh = 8 - d if up else d
        m = (t < 8 - d) if up else (t >= d)
        a_prev, b_prev = pltpu.roll(a, sh, 0), pltpu.roll(b, sh, 0)
        b = jnp.where(m, a * b_prev + b, b)
        a = jnp.where(m, a * a_prev, a)
    return a, b


def lru_scan(lay, a, b, name):
    segs = [(0, lay.C), (lay.C, lay.L)]

    def body(a_ref, b_ref, s_ref):
        for d in range(2):
            rev = d == 1
            state = tuple(jnp.zeros((1, 128), F32) for _ in range(GB))
            for base, n in segs:
                nt = n // 8

                def step(j, c, base=base, nt=nt, rev=rev, d=d):
                    c = list(c)
                    for u in range(SCAN_UNROLL):
                        jj = j * SCAN_UNROLL + u
                        r0 = pl.multiple_of(base + 8 * ((nt - 1 - jj) if rev else jj), 8)
                        for g in range(GB):
                            at, bt = _tile_scan(a_ref[d, g, pl.ds(r0, 8), :], b_ref[d, g, pl.ds(r0, 8), :], rev)
                            h = at * c[g] + bt
                            s_ref[d, g, pl.ds(r0, 8), :] = h
                            c[g] = h[0:1] if rev else h[7:8]
                    return tuple(c)

                state = lax.fori_loop(0, nt // SCAN_UNROLL, step, state)

    spec = pl.BlockSpec((2, GB, lay.PS, 128), lambda s, hb: (0, hb, s, 0))
    return pl.pallas_call(
        body, name=name, out_shape=_sds((2, 8, lay.T, 128), F32), grid=(2, 8 // GB),
        in_specs=[spec, spec], out_specs=spec, compiler_params=_cp(("parallel", "parallel"), VMEM_BIG))(a, b)


def lru_scan_bwd(lay, a, s, dy, name):
    segs = [(0, lay.C), (lay.C, lay.L)]
    C, PS = lay.C, lay.PS

    def body(a_ref, s_ref, g_ref, da_ref, db_ref):
        t = lax.broadcasted_iota(jnp.int32, (8, 128), 0)
        for d in range(2):
            rev = d == 1
            carry = tuple(jnp.zeros((1, 128), F32) for _ in range(GB))
            for si in (1, 0):
                base, n = segs[si]
                nt = n // 8

                def step(j, c, base=base, nt=nt, rev=rev, d=d):
                    c = list(c)
                    for u in range(SCAN_UNROLL):
                        jj = j * SCAN_UNROLL + u
                        r0 = pl.multiple_of(base + 8 * (jj if rev else (nt - 1 - jj)), 8)
                        if rev:
                            rn = pl.multiple_of(jnp.where(r0 == PS - 8, 0, r0 + 8), 8)
                            nb_zero = r0 == C - 8
                        else:
                            rn = pl.multiple_of(jnp.maximum(r0 - 8, 0), 8)
                            nb_zero = r0 == 0
                        for g in range(GB):
                            av = a_ref[d, g, pl.ds(r0, 8), :]
                            gv = g_ref[g, pl.ds(r0, 8), :]
                            sv = s_ref[d, g, pl.ds(r0, 8), :]
                            nbt = s_ref[d, g, pl.ds(rn, 8), :]
                            at, bt = _tile_scan(av, av * gv, not rev)
                            m = at * c[g] + bt
                            if rev:
                                m_next = jnp.where(t >= 1, pltpu.roll(m, 1, 0), c[g])
                                nb = jnp.where(nb_zero, 0.0, nbt[0:1])
                                h_prev = jnp.where(t < 7, pltpu.roll(sv, 7, 0), nb)
                                c[g] = m[7:8]
                            else:
                                m_next = jnp.where(t < 7, pltpu.roll(m, 7, 0), c[g])
                                nb = jnp.where(nb_zero, 0.0, nbt[7:8])
                                h_prev = jnp.where(t >= 1, pltpu.roll(sv, 1, 0), nb)
                                c[g] = m[0:1]
                            lam = gv + m_next
                            db_ref[d, g, pl.ds(r0, 8), :] = lam
                            da_ref[d, g, pl.ds(r0, 8), :] = lam * h_prev
                    return tuple(c)

                carry = lax.fori_loop(0, nt // SCAN_UNROLL, step, carry)

    spec = pl.BlockSpec((2, GB, lay.PS, 128), lambda s, hb: (0, hb, s, 0))
    return pl.pallas_call(
        body, name=name, out_shape=[_sds((2, 8, lay.T, 128), F32)] * 2, grid=(2, 8 // GB),
        in_specs=[spec, spec, pl.BlockSpec((GB, lay.PS, 128), lambda s, hb: (hb, s, 0))],
        out_specs=[spec, spec], compiler_params=_cp(("parallel", "parallel"), VMEM_BIG))(a, s, dy)


def _gelu(x):
    k = math.sqrt(2.0 / math.pi)
    t = jnp.tanh(k * (x + 0.044715 * x * x * x))
    return 0.5 * x * (1.0 + t), 0.5 * (1.0 + t) + 0.5 * x * (1.0 - t * t) * k * (1.0 + 3 * 0.044715 * x * x)


def lru_gate(lay, p, s, name):
    tr = lay.tr

    def body(g_ref, s_ref, o_ref):
        for h in range(8):
            sl = slice(h * 128, (h + 1) * 128)
            o_ref[:, sl] = (_gelu(g_ref[:, sl])[0] * (s_ref[0, h] + s_ref[1, h])).astype(o_ref.dtype)

    return pl.pallas_call(
        body, name=name, out_shape=_sds((lay.T, D), BF16), grid=(lay.nblk,),
        in_specs=[pl.BlockSpec((tr, D), lambda i: (i, 0)), pl.BlockSpec((2, 8, tr, 128), lambda i: (0, 0, i, 0))],
        out_specs=pl.BlockSpec((tr, D), lambda i: (i, 0)), compiler_params=_cp(("parallel",)))(p, s)


def lru_gate_bwd(lay, p, s, do, name):
    tr = lay.tr

    def body(g_ref, s_ref, do_ref, dg_ref, dy_ref):
        for h in range(8):
            sl = slice(h * 128, (h + 1) * 128)
            ge, dge = _gelu(g_ref[:, sl])
            dov = do_ref[:, sl]
            dg_ref[:, sl] = (dov * (s_ref[0, h] + s_ref[1, h]) * dge).astype(dg_ref.dtype)
            dy_ref[h] = dov * ge

    xspec = pl.BlockSpec((tr, D), lambda i: (i, 0))
    return pl.pallas_call(
        body, name=name, out_shape=[_sds((lay.T, D), BF16), _sds((8, lay.T, 128), F32)], grid=(lay.nblk,),
        in_specs=[xspec, pl.BlockSpec((2, 8, tr, 128), lambda i: (0, 0, i, 0)), xspec],
        out_specs=[xspec, pl.BlockSpec((8, tr, 128), lambda i: (0, i, 0))],
        compiler_params=_cp(("parallel",)))(p, s, do)


def silu_rows(x, name):
    def body(x_ref, o_ref):
        v = x_ref[...]
        o_ref[...] = (v * _sigmoid(v)).astype(o_ref.dtype)
    return pl.pallas_call(body, name=name, out_shape=_sds(x.shape, BF16), in_specs=[VMEM_SPEC], out_specs=VMEM_SPEC)(x)


def mod_grad_rows(gath, name):
    w = gath.shape[-1]

    def body(g_ref, dm_ref, db_ref):
        dm_ref[...] = jnp.zeros_like(dm_ref)
        for l in range(2):
            ctx = g_ref[0, 3 * l + 2:3 * l + 3, :]
            tot = g_ref[0, 3 * l:3 * l + 1, :] + g_ref[0, 3 * l + 1:3 * l + 2, :]
            for k in range(8):
                dm_ref[l, 2 * k:2 * k + 2, :] = g_ref[k, 3 * l:3 * l + 2, :]
                if k:
                    ctx = ctx + g_ref[k, 3 * l + 2:3 * l + 3, :]
                    tot = tot + (g_ref[k, 3 * l:3 * l + 1, :] + g_ref[k, 3 * l + 1:3 * l + 2, :])
            dm_ref[l, 16:17, :] = ctx
            db_ref[l:l + 1, :] = tot + ctx

    return pl.pallas_call(body, name=name, out_shape=[_sds((2, 32, w), F32), _sds((2, w), F32)],
                          in_specs=[VMEM_SPEC], out_specs=[VMEM_SPEC, VMEM_SPEC])(gath)


def cctx_grad(p, c_ctx, name):
    def body(a_ref, c_ref, o_ref):
        cv = c_ref[...]
        sg = _sigmoid(cv)
        o_ref[...] = 0.5 * (a_ref[0, 0:1, :] + a_ref[1, 0:1, :]) * (sg * (1.0 + cv * (1.0 - sg)))
    return pl.pallas_call(body, name=name, out_shape=_sds((1, D), F32), in_specs=[VMEM_SPEC] * 2,
                          out_specs=VMEM_SPEC)(p, c_ctx)


def loss_and_grad(lay, h, tgt, name):
    def fn(hb, tb):
        lat = (pl.program_id(0) % lay.bps) >= lay.cb
        e = jnp.where(lat, hb - tb, 0.0)
        return e * (1.0 / D), jnp.sum(e * e, axis=0, keepdims=True) * (0.5 / D)
    return rowwise(lay, name, fn, [h, tgt], outs=[(D, F32)], sums=[(1, D)])


def adamw(w, g, m, v, name):
    shape = w.shape
    w2, g2, m2, v2 = (t.reshape(-1, shape[-1]) for t in (w, g, m, v))
    rows, width = w2.shape
    tr = 256 if rows % 256 == 0 else rows
    c1 = 1.0 - ADAM_B1 ** ADAM_STEP
    c2 = 1.0 - ADAM_B2 ** ADAM_STEP

    def body(w_ref, g_ref, m_ref, v_ref, d_ref, mo_ref, vo_ref):
        gv = g_ref[...]
        mn = ADAM_B1 * m_ref[...] + (1.0 - ADAM_B1) * gv
        vn = ADAM_B2 * v_ref[...] + (1.0 - ADAM_B2) * (gv * gv)
        d_ref[...] = -ADAM_LR * ((mn / c1) / (jnp.sqrt(vn / c2) + ADAM_EPS) + ADAM_WD * w_ref[...])
        mo_ref[...] = mn
        vo_ref[...] = vn

    spec = pl.BlockSpec((tr, width), lambda i: (i, 0))
    d, mn, vn = pl.pallas_call(body, name=name, out_shape=[_sds((rows, width), F32)] * 3, grid=(rows // tr,),
                               in_specs=[spec] * 4, out_specs=[spec] * 3, compiler_params=_cp(("parallel",)))(w2, g2, m2, v2)
    return d.reshape(shape), mn.reshape(shape), vn.reshape(shape)


def adamw_ffn(w, m, v, red, kind, ns, name):
    shape = w.shape
    w2, m2, v2 = (t.reshape(-1, shape[-1]) for t in (w, m, v))
    rows, width = w2.shape
    c1 = 1.0 - ADAM_B1 ** ADAM_STEP
    c2 = 1.0 - ADAM_B2 ** ADAM_STEP
    tr, nb = ns // 2, 2
    gspec = pl.BlockSpec((tr, D), lambda i: (((i // nb) * 3 + kind) * nb + i % nb, 0))

    def body(w_ref, g_ref, m_ref, v_ref, go_ref, d_ref, mo_ref, vo_ref):
        gv = g_ref[...]
        mn = ADAM_B1 * m_ref[...] + (1.0 - ADAM_B1) * gv
        vn = ADAM_B2 * v_ref[...] + (1.0 - ADAM_B2) * (gv * gv)
        go_ref[...] = gv
        d_ref[...] = -ADAM_LR * ((mn / c1) / (jnp.sqrt(vn / c2) + ADAM_EPS) + ADAM_WD * w_ref[...])
        mo_ref[...] = mn
        vo_ref[...] = vn

    spec = pl.BlockSpec((tr, width), lambda i: (i, 0))
    outs = pl.pallas_call(body, name=name, out_shape=[_sds((rows, width), F32)] * 4, grid=(rows // tr,),
                          in_specs=[spec, gspec, spec, spec], out_specs=[spec] * 4,
                          compiler_params=_cp(("parallel",)))(w2, red, m2, v2)
    return tuple(t.reshape(shape) for t in outs)


def mod_mm(sc, w_mod, bias, name):
    wm = w_mod.shape[-1]
    tn = _pick(wm, (768, 512, 384, 256, 128))

    def body(a_ref, b_ref, c_ref, o_ref):
        o_ref[...] = _nn(a_ref[...], b_ref[...].astype(BF16)) + c_ref[...]

    return pl.pallas_call(
        body, name=name, out_shape=_sds((DEPTH, 32, wm), F32), grid=(DEPTH, wm // tn),
        in_specs=[pl.BlockSpec((32, D), lambda l, j: (0, 0)), pl.BlockSpec((None, D, tn), lambda l, j: (l, 0, j)),
                  pl.BlockSpec((None, 1, tn), lambda l, j: (l, 0, j))],
        out_specs=pl.BlockSpec((None, 32, tn), lambda l, j: (l, 0, j)),
        compiler_params=_cp(("parallel", "parallel")))(sc, w_mod, bias)


def wmod_dw(sc, dcol, name):
    wm = dcol.shape[-1]
    tm = 256

    def body(a_ref, b_ref, o_ref):
        o_ref[...] = _tn(a_ref[...], b_ref[...].astype(BF16))

    return pl.pallas_call(
        body, name=name, out_shape=_sds((DEPTH, D, wm), F32), grid=(DEPTH, D // tm),
        in_specs=[pl.BlockSpec((32, tm), lambda l, i: (0, i)), pl.BlockSpec((None, 32, wm), lambda l, i: (l, 0, 0))],
        out_specs=pl.BlockSpec((None, tm, wm), lambda l, i: (l, i, 0)),
        compiler_params=_cp(("parallel", "parallel")))(sc, dcol)


def cctx_dx(drow, w_mod, name):
    wm = w_mod.shape[-1]

    def body(a_ref, b_ref, o_ref):
        o_ref[...] = _nt(a_ref[...].astype(BF16), b_ref[...].astype(BF16))

    return pl.pallas_call(
        body, name=name, out_shape=_sds((DEPTH, 16, D), F32), grid=(DEPTH,),
        in_specs=[pl.BlockSpec((None, 16, wm), lambda l: (l, 0, 0)), pl.BlockSpec((None, D, wm), lambda l: (l, 0, 0))],
        out_specs=pl.BlockSpec((None, 16, D), lambda l: (l, 0, 0)), compiler_params=_cp(("parallel",), VMEM_BIG))(drow, w_mod)


HEAD_PERM = (0, 4, 1, 5, 2, 6, 3, 7)


def _rot_rows(wt):
    return jnp.concatenate([-wt[32:64], wt[0:32]], axis=0)


def _unrot_rows(g):
    return jnp.concatenate([g[32:64], -g[0:32]], axis=0)


def _heads(a, n):
    return [a[64 * i:64 * (i + 1)] for i in range(n)]


def kernel(x, c, ctx, c_ctx, w_mod, b_mod, ln_g, ln_b, ffn_w_gate, ffn_w_up, ffn_w_down, mix_ab_w_in, attn_sink, pool_w, pool_scale, mix_ab_w_out, lru_w_in, lru_conv_w, lru_conv_b, lru_wa, lru_ba, lru_wx, lru_bx, lru_lambda, lru_w_out, loss_target, m_c_ctx, m_w_mod, m_b_mod, m_ln_g, m_ln_b, m_ffn_w_gate, m_ffn_w_up, m_ffn_w_down, m_mix_ab_w_in, m_attn_sink, m_pool_w, m_pool_scale, m_mix_ab_w_out, m_lru_w_in, m_lru_conv_w, m_lru_conv_b, m_lru_wa, m_lru_ba, m_lru_wx, m_lru_bx, m_lru_lambda, m_lru_w_out, v_c_ctx, v_w_mod, v_b_mod, v_ln_g, v_ln_b, v_ffn_w_gate, v_ffn_w_up, v_ffn_w_down, v_mix_ab_w_in, v_attn_sink, v_pool_w, v_pool_scale, v_mix_ab_w_out, v_lru_w_in, v_lru_conv_w, v_lru_conv_b, v_lru_wa, v_lru_ba, v_lru_wx, v_lru_bx, v_lru_lambda, v_lru_w_out):
    n_lat, n_ctx = x.shape[1], ctx.shape[1]
    lay = Layout(n_ctx, n_lat)
    T = lay.T
    ns = ffn_w_gate.shape[-1]
    n_li, n_ai = lru_w_in.shape[-1], mix_ab_w_in.shape[-1]
    n_ao, n_lo = mix_ab_w_out.shape[1], lru_w_out.shape[1]
    wm = w_mod.shape[-1]
    dsh = ln_g.shape[-1]
    mx, my, mc = lax.axis_index("x"), lax.axis_index("y"), lax.axis_index("c")
    chip = 2 * mx + my
    me = 2 * chip + mc

    c_all = all_gather8(c, "ag8_c").reshape(16, D)
    cc = jnp.concatenate([c_all, c_ctx[None, :], jnp.zeros((15, D), F32)], axis=0)
    sc = silu_rows(cc, "silu_c")
    bias = lax.dynamic_slice(b_mod, (0, chip * wm), (DEPTH, wm)).reshape(DEPTH, 1, wm)
    modg = all_gather_chips(mod_mm(sc, w_mod, bias, "mod_mm"), "ag_mod")
    modtab = []
    for l in range(DEPTH):
        full = jnp.transpose(modg[:, l], (1, 0, 2)).reshape(32, N_CHIP * wm)
        mine = lax.dynamic_slice(full, (2 * me, 0), (2, N_CHIP * wm))
        modtab.append(jnp.concatenate([mine, full[16:17]], axis=0).reshape(3, N_MOD, D))

    small = jnp.concatenate([ln_g.reshape(6, dsh), ln_b.reshape(6, dsh), lru_conv_w[0], lru_conv_b, lru_ba[0],
                             lru_bx[0], lru_lambda[0], jnp.zeros((9, dsh), F32)], axis=0)
    small = all_gather_chips(small.reshape(2, 16, dsh), "ag_small").reshape(N_CHIP, 32, dsh)
    small = jnp.transpose(small, (1, 0, 2)).reshape(32, D)
    ln_g_f, ln_b_f = small[0:6].reshape(2, 3, D), small[6:12].reshape(2, 3, D)
    conv_w_f, conv_b_f = small[12:16], small[16:17]
    lru_vec = small[17:23]

    hh = 3 * ns // 2
    gate_t, up_t = jnp.swapaxes(ffn_w_gate, -1, -2), jnp.swapaxes(ffn_w_up, -1, -2)
    extra = [0, n_ai + n_ao, n_li + n_lo, 0]
    placed = [ffn_place(gate_t, up_t, ffn_w_down, g // 2, g % 2, f"ag_ffn{g}_place", extra[g]) for g in range(4)]
    placed[1] = place_rows(placed[1], jnp.concatenate([mix_ab_w_in[0].T, mix_ab_w_out[0]], axis=0).astype(BF16), 3 * ns,
                           "ag_mixa_place")
    placed[2] = place_rows(placed[2], jnp.concatenate([lru_w_in[0].T, lru_w_out[0]], axis=0).astype(BF16), 3 * ns,
                           "ag_mixc_place")
    placed = [p.reshape(N_CHIP, 2, p.shape[1] // 2, D) for p in placed]
    wb = [gather_placed(placed[0], "ag_ffn0"), None, None, None]
    mixw = {}

    def mixa_w():
        if "a" not in mixw:
            full = wb[1].reshape(N_CHIP, -1, D)
            ab_in_t = full[:, 3 * ns:3 * ns + n_ai].reshape(N_CHIP * n_ai, D)
            ab_out = full[:, 3 * ns + n_ai:].reshape(N_CHIP * n_ao, D)
            qh, kh = _heads(ab_in_t[Q0:K0], N_HEADS), _heads(ab_in_t[K0:V0], N_KV)
            w_ext_t = jnp.concatenate([qh[h] for h in HEAD_PERM] + [ab_in_t[K0:QR0]]
                                      + [_rot_rows(qh[h]) for h in HEAD_PERM] + [_rot_rows(t) for t in kh], axis=0)
            oh = _heads(ab_out[0:ATT_W], N_HEADS)
            mixw["a"] = (w_ext_t, jnp.concatenate([oh[h] for h in HEAD_PERM] + [ab_out[ATT_W:]], axis=0))
        return mixw["a"]

    def mixc_w():
        if "c" not in mixw:
            full = wb[2].reshape(N_CHIP, -1, D)
            mixw["c"] = (full[:, 3 * ns:3 * ns + n_li].reshape(N_CHIP * n_li, D),
                         full[:, 3 * ns + n_li:].reshape(N_CHIP * n_lo, D))
        return mixw["c"]

    t = jnp.arange(n_lat)
    inv = ROPE_THETA ** (-jnp.arange(16, dtype=F32) / 16.0)
    ang = jnp.concatenate([(t // GRID_W).astype(F32)[:, None] * inv, (t % GRID_W).astype(F32)[:, None] * inv], axis=-1)
    cos1 = jnp.concatenate([jnp.ones((n_ctx, 32), F32), jnp.cos(ang)], axis=0)
    sin1 = jnp.concatenate([jnp.zeros((n_ctx, 32), F32), jnp.sin(ang)], axis=0)
    cos_t = jnp.tile(cos1, (2, 4))
    sin_t = jnp.tile(sin1, (2, 4))
    sk = attn_sink[0]
    sink_tab = jnp.concatenate([jnp.repeat(jnp.stack([sk[:4], sk[4:]], axis=1), HEAD_DIM, axis=1),
                                jnp.zeros((4, 128), F32)], axis=0)
    pscale = pool_scale.reshape(1, POOL_W)

    h0 = jnp.concatenate([ctx, x], axis=1).reshape(T, D)
    tgt = loss_target.reshape(2 * n_lat, D)

    def lnv(l, j):
        return jnp.stack([ln_g_f[l, j], ln_b_f[l, j]])

    subs = [(0, 0, 0.5, 0), (0, 3, 1.0, 1), (0, 6, 0.5, 2), (1, 0, 0.5, 0), (1, 3, 1.0, 1), (1, 6, 0.5, 2)]

    def ffn_core(hm, l, f):
        tag = f"l{l}f{f}"
        gi = 2 * l + f
        w = wb[gi].reshape(N_CHIP, -1, D)
        if gi == 3:
            up, sl, a = ffn_up(lay, hm, w, 0, 1, ns, f"ffn_up_{tag}")
            (y,) = slab_nn_acc(lay, [a], w, [2], ns, f"ffn_down_{tag}")
            return y, dict(up=up, sl=sl, a=a, nbuf=None)
        up, sl, a, nbuf = ffn_up(lay, hm, w, 0, 1, ns, f"ffn_up_{tag}", rider=rider_gather_xy(placed[gi + 1]))
        y, nbuf = slab_nn_acc(lay, [a], w, [2], ns, f"ffn_down_{tag}", rider=rider_gather_fwd(nbuf))
        return y, dict(up=up, sl=sl, a=a, nbuf=nbuf)

    def mixa_core(hm):
        p = mm_nt(hm, mixa_w()[0], "mixa_in")
        qr, kr, vb, u = rope_fwd(lay, p, cos_t, sin_t, "rope")
        att, lse = attn_fwd(lay, qr, kr, vb, sink_tab, "attn")
        pool = pool_fwd(lay, u, pool_w[0], pscale, "pool")
        cat = jnp.concatenate([att, pool], axis=1)
        return mm_nn(cat, mixa_w()[1], "mixa_out"), dict(qr=qr, kr=kr, vb=vb, u=u, lse=lse, cat=cat)

    def mixc_core(hm):
        p = mm_nt(hm, mixc_w()[0], "mixc_in")
        uc = conv_fwd(lay, p, D, conv_w_f, conv_b_f, "conv")
        a, b = lru_coeffs(lay, uc, lru_wa[0], lru_wx[0], lru_vec, "lru_coef")
        s = lru_scan(lay, a, b, "lru_scan")
        o = lru_gate(lay, p, s, "lru_gate")
        return mm_nn(o, mixc_w()[1], "mixc_out"), dict(p=p, uc=uc, a=a, s=s, o=o)

    recs = []
    h = h0
    hm = modulate(lay, h0, modtab[0], 0, 1, "mod_first")
    for k, (l, k0, coef, j) in enumerate(subs):
        if k0 == 3:
            y, core = mixa_core(hm) if l == 0 else mixc_core(hm)
        else:
            y, core = ffn_core(hm, l, k0 // 6)
        nxt = None if k == 5 else (modtab[subs[k + 1][0]], subs[k + 1][1], subs[k + 1][1] + 1)
        nbuf = core.pop("nbuf", None)
        res = resid_ln(lay, h, y, modtab[l], k0 + 2, coef, lnv(l, j), f"ln_s{k}", nxt=nxt,
                       rider=None if nbuf is None else rider_gather_d2d(nbuf))
        if nbuf is not None:
            wb[2 * l + k0 // 6 + 1] = res[-1]
        recs.append(dict(h=h, hm=hm, y=y, xhat=res[1], rstd=res[2], **core))
        h = res[0]
        hm = res[3] if nxt is not None else None

    dout, lparts = loss_and_grad(lay, h, tgt, "loss")
    loss = lax.psum(jnp.sum(lparts), ("x", "y", "c"))

    dln = {}
    dms = {}
    mixg = {}
    ffn_red = [lax.empty((4, 2, hh, D), F32)]
    mix_red = {}
    pending = []

    def rs_sib(p):
        return None if p is None else rider_reduce_sib(p["buf"])

    def rs_add2(p, recv):
        p["q"] = add_own_half(p["buf"], recv, BF16, f"rs_add2_{p['key']}")

    def rs_join(p, arr):
        if isinstance(p["key"], int):
            return rider_join(sum_slots(p["q"], arr, f"rs_add4_{p['key']}", dst=ffn_red[0], g=p["key"]), p["key"])
        return rider_join(sum_slots(p["q"], arr, f"rs_add4_{p['key']}"))

    def rs_done(p, joined):
        if isinstance(p["key"], int):
            ffn_red[0] = joined
        else:
            mix_red[p["key"]] = joined.reshape(-1, D)

    def ffn_core_bwd(dy, r, l, f):
        tag = f"l{l}f{f}"
        gi = 2 * l + f
        w = wb[gi].reshape(N_CHIP, -1, D)
        p = pending.pop() if pending else None
        gb = lax.empty((N_CHIP, 3 * ns, D), F32)
        if p is None:
            dg, du = ffn_bwd_da(lay, dy, w, 2, r["up"], r["sl"], ns, f"ffn_da_{tag}")
            (gb,) = slab_tn(lay, r["a"], dy, gb, 2, ns, f"ffn_dwd_{tag}")
            (gb,) = slab_tn(lay, dg, r["hm"], gb, 0, ns, f"ffn_dwg_{tag}")
            (gb,) = slab_tn(lay, du, r["hm"], gb, 1, ns, f"ffn_dwu_{tag}")
            (dhm,) = slab_nn_acc(lay, [dg, du], w, [0, 1], ns, f"ffn_dh_{tag}")
        else:
            dg, du, recv = ffn_bwd_da(lay, dy, w, 2, r["up"], r["sl"], ns, f"ffn_da_{tag}", rider=rs_sib(p))
            rs_add2(p, recv)
            gb, arr = slab_tn(lay, r["a"], dy, gb, 2, ns, f"ffn_dwd_{tag}", rider=rider_reduce_copy(p["q"], 0))
            gb, arr = slab_tn(lay, dg, r["hm"], gb, 0, ns, f"ffn_dwg_{tag}", rider=rider_reduce_copy(p["q"], 1, arr))
            gb, arr = slab_tn(lay, du, r["hm"], gb, 1, ns, f"ffn_dwu_{tag}", rider=rider_reduce_copy(p["q"], 2, arr))
            dhm, joined = slab_nn_acc(lay, [dg, du], w, [0, 1], ns, f"ffn_dh_{tag}", rider=rs_join(p, arr))
            rs_done(p, joined)
        pending.append(dict(buf=gb.reshape(N_CHIP, 2, hh, D), key=gi))
        return dhm

    def mixc_core_bwd(dy, r):
        p = pending.pop() if pending else None
        w_in_t, w_out = mixc_w()
        if p is None:
            do_c = mm_nt(dy, w_out, "mixc_out_dx")
        else:
            do_c, recv = mm_nt(dy, w_out, "mixc_out_dx", rider=rs_sib(p))
            rs_add2(p, recv)
        g_out = mm_tn(r["o"], dy, "mixc_out_dw")
        dgate, dyg = lru_gate_bwd(lay, r["p"], r["s"], do_c, "lru_gate_b")
        da_c, db_c = lru_scan_bwd(lay, r["a"], r["s"], dyg, "lru_scan_b")
        res = lru_coeffs_bwd(lay, r["uc"], lru_wa[0], lru_wx[0], lru_vec, da_c, db_c, "lru_coef_b",
                             rider=None if p is None else rider_reduce_copies(p["q"]))
        duc, mixg["wa"], mixg["wx"], mixg["vec"] = res[:4]
        du_c, mixg["cw"], mixg["cb"] = conv_bwd(lay, r["p"], D, conv_w_f, duc, "conv_b")
        dp_c = jnp.concatenate([dgate, du_c], axis=1)
        if p is None:
            g_in_t = mm_tn(dp_c, r["hm"], "mixc_in_dw")
        else:
            g_in_t, joined = mm_tn(dp_c, r["hm"], "mixc_in_dw", rider=rs_join(p, res[4]))
            rs_done(p, joined)
        buf = jnp.concatenate([g_in_t.reshape(N_CHIP, n_li, D), g_out.reshape(N_CHIP, n_lo, D)], axis=1)
        pending.append(dict(buf=buf.reshape(N_CHIP, 2, (n_li + n_lo) // 2, D), key="c"))
        return mm_nn(dp_c, w_in_t, "mixc_in_dx")

    def mixa_core_bwd(dy, r):
        p = pending.pop() if pending else None
        w_ext_t, w_out_ext = mixa_w()
        if p is None:
            dcat = mm_nt(dy, w_out_ext, "mixa_out_dx")
        else:
            dcat, recv = mm_nt(dy, w_out_ext, "mixa_out_dx", rider=rs_sib(p))
            rs_add2(p, recv)
        g_out_ext = mm_tn(r["cat"], dy, "mixa_out_dw")
        res = attn_bwd(lay, r["qr"], r["kr"], r["vb"], sink_tab, r["lse"], dcat, "attn_b",
                       rider=None if p is None else rider_reduce_copies(p["q"]))
        dqr, dkr, dv, mixg["sink"] = res[:4]
        du_a, mixg["pw"], mixg["ps"] = pool_bwd(lay, r["u"], dcat, pool_w[0], pscale, "pool_b")
        dp_a = rope_bwd(lay, dqr, dkr, dv, du_a, cos_t, sin_t, "rope_b")
        if p is None:
            g_ext_t = mm_tn(dp_a, r["hm"], "mixa_in_dw")
        else:
            g_ext_t, joined = mm_tn(dp_a, r["hm"], "mixa_in_dw", rider=rs_join(p, res[4]))
            rs_done(p, joined)
        gq, gqr = _heads(g_ext_t[Q0:K0], N_HEADS), _heads(g_ext_t[QR0:KR0], N_HEADS)
        g_q = [None] * N_HEADS
        for i, h in enumerate(HEAD_PERM):
            g_q[h] = gq[i] + _unrot_rows(gqr[i])
        gk = [a + _unrot_rows(b) for a, b in zip(_heads(g_ext_t[K0:V0], N_KV), _heads(g_ext_t[KR0:PEXT], N_KV))]
        g_ab_in_t = jnp.concatenate(g_q + gk + [g_ext_t[V0:QR0]], axis=0)
        go = _heads(g_out_ext[0:ATT_W], N_HEADS)
        g_o = [None] * N_HEADS
        for i, h in enumerate(HEAD_PERM):
            g_o[h] = go[i]
        g_ab_out = jnp.concatenate(g_o + [g_out_ext[ATT_W:]], axis=0)
        buf = jnp.concatenate([g_ab_in_t.reshape(N_CHIP, n_ai, D), g_ab_out.reshape(N_CHIP, n_ao, D)], axis=1)
        pending.append(dict(buf=buf.reshape(N_CHIP, 2, (n_ai + n_ao) // 2, D), key="a"))
        return mm_nn(dp_a, w_ext_t, "mixa_in_dx")

    l, k0, coef, j = subs[5]
    dy, dres, s1 = ln_bwd(lay, dout, recs[5]["xhat"], recs[5]["rstd"], recs[5]["y"], modtab[l], k0 + 2, coef, lnv(l, j),
                          "lnb_s5")
    for k in range(5, -1, -1):
        l, k0, coef, j = subs[k]
        r = recs[k]
        if k0 == 3:
            dhm = mixa_core_bwd(dy, r) if l == 0 else mixc_core_bwd(dy, r)
        else:
            dhm = ffn_core_bwd(dy, r, l, k0 // 6)
        dln[(l, j)] = block_sums(lay, s1, f"bs_ln_s{k}")
        if k > 0:
            lp, k0p, coefp, jp = subs[k - 1]
            rp = recs[k - 1]
            dy, dres, s1, s2 = modb_lnb(lay, dres, dhm, r["h"], modtab[l], k0 + 1, rp["xhat"], rp["rstd"], rp["y"],
                                        modtab[lp], k0p + 2, coefp, lnv(lp, jp), f"modb_lnb_s{k}")
        else:
            gx, s2 = mod_bwd(lay, dres, dhm, r["h"], modtab[l], k0 + 1, "modb_s0")
        dms[(l, k0)] = block_sums(lay, s2, f"bs_mod_s{k}")
    grad_x = gx.reshape(2, n_lat, D)
    g_wa, g_wx, g_vec, g_cw, g_cb, g_sink, g_pw, g_ps = (mixg[n] for n in ("wa", "wx", "vec", "cw", "cb", "sink", "pw", "ps"))

    rows = []
    for l in range(DEPTH):
        per_k = []
        for k0, j in ((0, 0), (3, 1), (6, 2)):
            per_k += [dms[(l, k0)][:3, 0], dms[(l, k0)][:3, 1], dln[(l, j)][:3, 2]]
        rows.append(jnp.stack(per_k, axis=1).reshape(3, N_MOD * D))
    dmod_loc = jnp.concatenate(rows + [jnp.zeros((2, N_MOD * D), F32)], axis=0)
    dmod_all, g_b_mod = mod_grad_rows(all_gather8(dmod_loc, "ag8_dmod"), "dmod_rows")
    dcol = lax.dynamic_slice(dmod_all, (0, 0, chip * wm), (DEPTH, 32, wm))
    g_w_mod = wmod_dw(sc, dcol, "wmod_dw")
    g_cctx = cctx_grad(cctx_dx(dcol[:, 16:32], w_mod, "cctx_dx"), c_ctx[None, :], "cctx_grad")

    g_ln_g =jnp.stack([jnp.stack([dln[(l, j)][3, 1] for j in range(3)]) for l in range(DEPTH)])
    g_ln_b = jnp.stack([jnp.stack([dln[(l, j)][3, 0] for j in range(3)]) for l in range(DEPTH)])
    sink_row = jnp.sum(g_sink, axis=0)[:4]
    g_sink8 = jnp.concatenate([sink_row[:, 0], sink_row[:, HEAD_DIM]])
    misc = jnp.concatenate([g_sink8, jnp.sum(g_ps, axis=0).reshape(POOL_W), jnp.zeros((D - 8 - POOL_W,), F32)])
    small_g = jnp.concatenate([
        g_ln_g.reshape(6, D), g_ln_b.reshape(6, D), jnp.sum(g_cw, axis=0), jnp.sum(g_cb, axis=0), g_vec,
        misc[None, :], jnp.sum(g_pw, axis=0).reshape(64, D), g_wa.reshape(256, D), g_wx.reshape(256, D), g_cctx,
        jnp.zeros((39, D), F32)], axis=0)
    n_small = small_g.shape[0] // N_CHIP
    last = pending.pop()
    ffn_red = reduce_scatter_chips(last["buf"], f"ffn{last['key']}", wire=BF16, dst=ffn_red[0],
                                   g=last["key"]).reshape(12 * ns, D)
    small_red = reduce_scatter_chips(small_g.reshape(N_CHIP, 2, n_small // 2, D), "small")
    small_red = all_gather_chips(small_red, "ag_smallg").reshape(N_CHIP * n_small, D)

    ffn_kind = dict(ffn_w_gate=0, ffn_w_up=1, ffn_w_down=2)

    def cols(a):
        return lax.dynamic_slice_in_dim(a, chip * dsh, dsh, axis=a.ndim - 1)

    sr = small_red
    grads = dict(
        c_ctx=sr[600], w_mod=g_w_mod, b_mod=g_b_mod,
        ln_g=cols(sr[0:6]).reshape(2, 3, dsh), ln_b=cols(sr[6:12]).reshape(2, 3, dsh),
        mix_ab_w_in=mix_red["a"][0:n_ai][None], attn_sink=sr[23, 0:8][None], pool_w=sr[24:88].reshape(1, 4, 128, 128),
        pool_scale=sr[23, 8:8 + POOL_W][None], mix_ab_w_out=mix_red["a"][n_ai:][None],
        lru_w_in=mix_red["c"][0:n_li].T[None],
        lru_conv_w=cols(sr[12:16])[None], lru_conv_b=cols(sr[16:17]), lru_wa=sr[88:344].reshape(1, 2, 8, 128, 128),
        lru_ba=cols(sr[17:19])[None], lru_wx=sr[344:600].reshape(1, 2, 8, 128, 128), lru_bx=cols(sr[19:21])[None],
        lru_lambda=cols(sr[21:23])[None], lru_w_out=mix_red["c"][n_li:][None])
    params = dict(c_ctx=(c_ctx, m_c_ctx, v_c_ctx), w_mod=(w_mod, m_w_mod, v_w_mod), b_mod=(b_mod, m_b_mod, v_b_mod),
                  ln_g=(ln_g, m_ln_g, v_ln_g), ln_b=(ln_b, m_ln_b, v_ln_b),
                  ffn_w_gate=(ffn_w_gate, m_ffn_w_gate, v_ffn_w_gate), ffn_w_up=(ffn_w_up, m_ffn_w_up, v_ffn_w_up),
                  ffn_w_down=(ffn_w_down, m_ffn_w_down, v_ffn_w_down),
                  mix_ab_w_in=(mix_ab_w_in, m_mix_ab_w_in, v_mix_ab_w_in), attn_sink=(attn_sink, m_attn_sink, v_attn_sink),
                  pool_w=(pool_w, m_pool_w, v_pool_w), pool_scale=(pool_scale, m_pool_scale, v_pool_scale),
                  mix_ab_w_out=(mix_ab_w_out, m_mix_ab_w_out, v_mix_ab_w_out), lru_w_in=(lru_w_in, m_lru_w_in, v_lru_w_in),
                  lru_conv_w=(lru_conv_w, m_lru_conv_w, v_lru_conv_w), lru_conv_b=(lru_conv_b, m_lru_conv_b, v_lru_conv_b),
                  lru_wa=(lru_wa, m_lru_wa, v_lru_wa), lru_ba=(lru_ba, m_lru_ba, v_lru_ba), lru_wx=(lru_wx, m_lru_wx, v_lru_wx),
                  lru_bx=(lru_bx, m_lru_bx, v_lru_bx), lru_lambda=(lru_lambda, m_lru_lambda, v_lru_lambda),
                  lru_w_out=(lru_w_out, m_lru_w_out, v_lru_w_out))
    gl, dl, ml, vl = [], [], [], []
    transposed = ("ffn_w_gate", "ffn_w_up", "mix_ab_w_in")
    for name, (w, m, v) in params.items():
        if name in transposed:
            w, m, v = (jnp.swapaxes(t, -1, -2) for t in (w, m, v))
        if name in ffn_kind:
            g, d, mn, vn = adamw_ffn(w, m, v, ffn_red, ffn_kind[name], ns, f"adamw_{name}")
        else:
            g = grads[name].reshape(w.shape)
            d, mn, vn = adamw(w, g, m, v, f"adamw_{name}")
        if name in transposed:
            g, d, mn, vn = (jnp.swapaxes(t, -1, -2) for t in (g, d, mn, vn))
        gl.append(g)
        dl.append(d)
        ml.append(mn)
        vl.append(vn)
    return (loss, grad_x, *gl, *dl, *ml, *vl)
```

```python
import functools
import math

import jax
import jax.numpy as jnp
from jax import lax
from jax.experimental import pallas as pl
from jax.experimental.pallas import tpu as pltpu

F32, BF16 = jnp.float32, jnp.bfloat16
MESH = pl.DeviceIdType.MESH
ANY = pl.BlockSpec(memory_space=pl.ANY)
VMEM_SPEC = pl.BlockSpec(memory_space=pltpu.VMEM)

D = 1024
N_CHIP = 4
HEAD_DIM, N_HEADS, N_KV = 64, 8, 2
ATT_W, KV_W, POOL_W = 512, 128, 512
POOL_WINDOWS = (2, 4, 8, 16)
BLK = 128
ATT_SCALE = HEAD_DIM ** -0.5
ROPE_THETA = 10000.0
GRID_W = 64
LRU_C = 8.0
LN_EPS = 1e-5
NEG_INF = -1e30
DEPTH = 2
ALPHA = (2 * DEPTH) ** 0.25
N_MOD = 9
ADAM_LR, ADAM_B1, ADAM_B2, ADAM_EPS, ADAM_WD, ADAM_STEP = 0.001, 0.9, 0.999, 1e-08, 0.01, 10
VMEM_BIG = 48 * 1024 * 1024


def _cp(sem=None, vmem=None):
    kw = {}
    if sem is not None:
        kw["dimension_semantics"] = sem
    if vmem is not None:
        kw["vmem_limit_bytes"] = vmem
    return pltpu.CompilerParams(**kw)


def _sds(shape, dtype):
    return jax.ShapeDtypeStruct(tuple(shape), dtype)


def _pick(n, cands):
    for c in cands:
        if n % c == 0:
            return c
    return n


def _dot(a, b, dims):
    return lax.dot_general(a, b, (dims, ((), ())), preferred_element_type=F32)


def _nn(a, b):
    return _dot(a, b, ((1,), (0,)))


def _nt(a, b):
    return _dot(a, b, ((1,), (1,)))


def _tn(a, b):
    return _dot(a, b, ((0,), (0,)))


def _sigmoid(x):
    return 0.5 * jnp.tanh(0.5 * x) + 0.5


def _me():
    return lax.axis_index("x"), lax.axis_index("y"), lax.axis_index("c")


def _rcopy(src, dst, ssem, rsem, dev):
    return pltpu.make_async_remote_copy(src_ref=src, dst_ref=dst, send_sem=ssem, recv_sem=rsem,
                                        device_id=dev, device_id_type=MESH)


def all_gather8(x, name):
    def body(x_ref, o_ref, ssem, rsem, lsem):
        mx, my, mc = _me()
        me = 4 * mx + 2 * my + mc
        loc = pltpu.make_async_copy(x_ref, o_ref.at[me], lsem)
        loc.start()
        peers = []
        for m in range(1, 8):
            px = 1 - mx if (m >> 2) & 1 else mx
            py = 1 - my if (m >> 1) & 1 else my
            pc = 1 - mc if m & 1 else mc
            peers.append((px, py, pc))
        sends = [_rcopy(x_ref, o_ref.at[me], ssem.at[k], rsem.at[k], p) for k, p in enumerate(peers)]
        for cp in sends:
            cp.start()
        for k, (px, py, pc) in enumerate(peers):
            _rcopy(x_ref, o_ref.at[4 * px + 2 * py + pc], ssem.at[k], rsem.at[k], (px, py, pc)).wait_recv()
        for cp in sends:
            cp.wait_send()
        loc.wait()

    return pl.pallas_call(
        body, name=name, out_shape=_sds((8,) + x.shape, x.dtype),
        in_specs=[VMEM_SPEC], out_specs=VMEM_SPEC,
        scratch_shapes=[pltpu.SemaphoreType.DMA((7,)), pltpu.SemaphoreType.DMA((7,)), pltpu.SemaphoreType.DMA],
    )(x)


_ROW_BLOCKS = (512, 384, 352, 256, 224, 128)


def _idx(v):
    return jnp.reshape(v, (1,)).astype(jnp.int32)


def place_slab(shard, name):
    _, h, w = shard.shape
    th = _pick(h, _ROW_BLOCKS)

    def body(s_ref, x_ref, o_ref):
        del s_ref
        o_ref[...] = x_ref[...]

    return pl.pallas_call(
        body, name=name, out_shape=_sds((N_CHIP,) + shard.shape, shard.dtype),
        grid_spec=pltpu.PrefetchScalarGridSpec(
            num_scalar_prefetch=1, grid=(2, h // th),
            in_specs=[pl.BlockSpec((None, th, w), lambda k, r, s: (k, r, 0))],
            out_specs=pl.BlockSpec((None, None, th, w), lambda k, r, s: (s[0], k, r, 0))),
    )(_idx(2 * lax.axis_index("x") + lax.axis_index("y")), shard)


def place_rows(buf, rows, r0, name):
    e, w = rows.shape
    tb = 64

    def body(s_ref, x_ref, b_ref, o_ref):
        del s_ref, b_ref
        o_ref[...] = x_ref[...]

    return pl.pallas_call(
        body, name=name, out_shape=_sds(buf.shape, buf.dtype),
        grid_spec=pltpu.PrefetchScalarGridSpec(
            num_scalar_prefetch=1, grid=(e // tb,),
            in_specs=[pl.BlockSpec((tb, w), lambda j, s: (j, 0)), ANY],
            out_specs=pl.BlockSpec((None, tb, w), lambda j, s: (s[0], r0 // tb + j, 0))),
        input_output_aliases={2: 0},
    )(_idx(2 * lax.axis_index("x") + lax.axis_index("y")), rows, buf)


def ffn_place(w_gate_t, w_up_t, w_down, l, f, name, extra=0):
    ns = w_down.shape[-2]
    tr, nb = ns // 2, 2

    def body(s_ref, g_ref, u_ref, d_ref, o_ref):
        del s_ref
        k = pl.program_id(0)

        @pl.when(k == 0)
        def _():
            o_ref[...] = g_ref[...].astype(BF16)

        @pl.when(k == 1)
        def _():
            o_ref[...] = u_ref[...].astype(BF16)

        @pl.when(k == 2)
        def _():
            o_ref[...] = d_ref[...].astype(BF16)

    def spec(q):
        return pl.BlockSpec((None, None, tr, D), lambda k, j, s: (l, f, jnp.where(k == q, j, 0), 0))

    return pl.pallas_call(
        body, name=name, out_shape=_sds((N_CHIP, 3 * ns + extra, D), BF16),
        grid_spec=pltpu.PrefetchScalarGridSpec(
            num_scalar_prefetch=1, grid=(3, nb), in_specs=[spec(0), spec(1), spec(2)],
            out_specs=pl.BlockSpec((None, tr, D), lambda k, j, s: (s[0], k * nb + j, 0))),
    )(_idx(2 * lax.axis_index("x") + lax.axis_index("y")), w_gate_t, w_up_t, w_down)


def all_gather_chips(shard, name):
    return gather_placed(place_slab(shard, name + "_place"), name)


def gather_placed(full, name):
    h = full.shape[2]
    lo, hi = pl.ds(0, h // 2), pl.ds(h // 2, h - h // 2)

    def body(x_ref, o_ref, ssem, rsem):
        del x_ref
        mx, my, mc = _me()
        s, xs, ys, ds = 2 * mx + my, 2 * (1 - mx) + my, 2 * mx + (1 - my), 2 * (1 - mx) + (1 - my)
        xn, yn, sib = (1 - mx, my, mc), (mx, 1 - my, mc), (mx, my, 1 - mc)

        def cp(k, src, dst, dev):
            return _rcopy(src, dst, ssem.at[k], rsem.at[k], dev)

        own = o_ref.at[s, mc]
        sent = [cp(0, own, own, xn), cp(1, own, own, yn)]
        for c in sent:
            c.start()
        cp(0, own, o_ref.at[xs, mc], xn).wait_recv()
        sent += [cp(2, o_ref.at[xs, mc, lo], o_ref.at[xs, mc, lo], yn), cp(4, o_ref.at[xs, mc], o_ref.at[xs, mc], sib)]
        sent[-2].start()
        sent[-1].start()
        cp(1, own, o_ref.at[ys, mc], yn).wait_recv()
        sent += [cp(3, o_ref.at[ys, mc, hi], o_ref.at[ys, mc, hi], xn), cp(5, o_ref.at[ys, mc], o_ref.at[ys, mc], sib)]
        sent[-2].start()
        sent[-1].start()
        cp(2, own, o_ref.at[ds, mc, lo], yn).wait_recv()
        cp(3, own, o_ref.at[ds, mc, hi], xn).wait_recv()
        sent.append(cp(6, o_ref.at[ds, mc], o_ref.at[ds, mc], sib))
        sent[-1].start()
        for k, slot in ((4, xs), (5, ys), (6, ds)):
            cp(k, own, o_ref.at[slot, 1 - mc], sib).wait_recv()
        for c in sent:
            c.wait_send()

    return pl.pallas_call(
        body, name=name, out_shape=_sds(full.shape, full.dtype), in_specs=[ANY], out_specs=ANY,
        input_output_aliases={0: 0},
        scratch_shapes=[pltpu.SemaphoreType.DMA((7,)), pltpu.SemaphoreType.DMA((7,))],
    )(full)


def sibling_send_other_half(buf, name):
    def body(x_ref, o_ref, ssem, rsem):
        mx, my, mc = _me()
        sib = (mx, my, 1 - mc)
        cps = [_rcopy(x_ref.at[k, 1 - mc], o_ref.at[k], ssem.at[k], rsem.at[k], sib) for k in range(N_CHIP)]
        for cp in cps:
            cp.start()
        for cp in cps:
            cp.wait_recv()
        for cp in cps:
            cp.wait_send()

    n, _, h, w = buf.shape
    return pl.pallas_call(
        body, name=name, out_shape=_sds((n, h, w), buf.dtype), in_specs=[ANY], out_specs=ANY,
        scratch_shapes=[pltpu.SemaphoreType.DMA((N_CHIP,)), pltpu.SemaphoreType.DMA((N_CHIP,))],
    )(buf)


def chips_all_to_all(q, name):
    def body(x_ref, o_ref, ssem, rsem):
        mx, my, mc = _me()
        s = 2 * mx + my
        chips = [(1 - mx, my), (mx, 1 - my), (1 - mx, 1 - my)]
        cps = [_rcopy(x_ref.at[2 * px + py], o_ref.at[s], ssem.at[j], rsem.at[j], (px, py, mc))
               for j, (px, py) in enumerate(chips)]
        for cp in cps:
            cp.start()
        for j, (px, py) in enumerate(chips):
            ps = 2 * px + py
            _rcopy(x_ref.at[ps], o_ref.at[ps], ssem.at[j], rsem.at[j], (px, py, mc)).wait_recv()
        for cp in cps:
            cp.wait_send()

    return pl.pallas_call(
        body, name=name, out_shape=_sds(q.shape, q.dtype), in_specs=[ANY], out_specs=ANY,
        scratch_shapes=[pltpu.SemaphoreType.DMA((3,)), pltpu.SemaphoreType.DMA((3,))],
    )(q)


def sibling_join_halves(both, name, g=None):
    def body(x_ref, o_ref, ssem, rsem):
        del x_ref
        mx, my, mc = _me()
        sib = (mx, my, 1 - mc)
        o = o_ref if g is None else o_ref.at[g]
        cp = _rcopy(o.at[mc], o.at[mc], ssem, rsem, sib)
        cp.start()
        _rcopy(o.at[1 - mc], o.at[1 - mc], ssem, rsem, sib).wait_recv()
        cp.wait_send()

    return pl.pallas_call(
        body, name=name, out_shape=_sds(both.shape, both.dtype), in_specs=[ANY], out_specs=ANY,
        input_output_aliases={0: 0}, scratch_shapes=[pltpu.SemaphoreType.DMA, pltpu.SemaphoreType.DMA],
    )(both)


def add_own_half(buf, recv, wire, name):
    n, _, h, w = buf.shape
    th = _pick(h, _ROW_BLOCKS)

    def body(c_ref, a_ref, b_ref, o_ref):
        del c_ref
        o_ref[...] = (a_ref[...] + b_ref[...]).astype(o_ref.dtype)

    return pl.pallas_call(
        body, name=name, out_shape=_sds((n, h, w), wire),
        grid_spec=pltpu.PrefetchScalarGridSpec(
            num_scalar_prefetch=1, grid=(n, h // th),
            in_specs=[pl.BlockSpec((None, None, th, w), lambda k, r, c: (k, c[0], r, 0)),
                      pl.BlockSpec((None, th, w), lambda k, r, c: (k, r, 0))],
            out_specs=pl.BlockSpec((None, th, w), lambda k, r, c: (k, r, 0))),
    )(_idx(lax.axis_index("c")), buf, recv)


def sum_slots(q, r, name, dst=None, g=None):
    n, h, w = r.shape
    th = _pick(h, _ROW_BLOCKS)

    def body(i_ref, q_ref, r1, r2, r3, *rest):
        del i_ref
        rest[-1][...] = ((q_ref[...].astype(F32) + r1[...].astype(F32)) + r2[...].astype(F32)) + r3[...].astype(F32)

    def slot(d):
        return lambda i, ix: ((ix[0] + d) % N_CHIP, i, 0)

    idx = jnp.stack([2 * lax.axis_index("x") + lax.axis_index("y"), lax.axis_index("c")]).astype(jnp.int32)
    in_specs = [pl.BlockSpec((None, th, w), slot(d)) for d in (0, 1, 2, 3)]
    if dst is None:
        return pl.pallas_call(
            body, name=name, out_shape=_sds((2, h, w), F32),
            grid_spec=pltpu.PrefetchScalarGridSpec(
                num_scalar_prefetch=1, grid=(h // th,), in_specs=in_specs,
                out_specs=pl.BlockSpec((None, th, w), lambda i, ix: (ix[1], i, 0))),
        )(idx, q, r, r, r)
    return pl.pallas_call(
        body, name=name, out_shape=_sds(dst.shape, F32),
        grid_spec=pltpu.PrefetchScalarGridSpec(
            num_scalar_prefetch=1, grid=(h // th,), in_specs=in_specs + [ANY],
            out_specs=pl.BlockSpec((None, None, th, w), lambda i, ix: (g, ix[1], i, 0))),
        input_output_aliases={5: 0},
    )(idx, q, r, r, r, dst)


def reduce_scatter_chips(buf, tag, wire=F32, dst=None, g=None):
    recv = sibling_send_other_half(buf, f"rs_sib_{tag}")
    q = add_own_half(buf, recv, wire, f"rs_add2_{tag}")
    r = chips_all_to_all(q, f"rs_a2a_{tag}")
    red = sum_slots(q, r, f"rs_add4_{tag}", dst=dst, g=g)
    return sibling_join_halves(red, f"rs_join_{tag}", g=g)


class Layout:
    def __init__(self, n_ctx, n_lat):
        self.C, self.L = n_ctx, n_lat
        self.PS = n_ctx + n_lat
        self.T = 2 * self.PS
        self.tr = _pick(math.gcd(n_ctx, n_lat), (256, 128))
        self.bps = self.PS // self.tr
        self.cb = n_ctx // self.tr
        self.nblk = self.T // self.tr
        self.tm = _pick(self.T, (1152, 768, 512, 256, 128))
        self.tm2 = _pick(self.T, (2304, 1152, 768, 512, 256, 128))
        self.tc = _pick(self.T, (512, 256, 128))

    def seg(self, i):
        return jnp.where(i % self.bps < self.cb, 2, i // self.bps)


def rowwise(lay, name, fn, rows, segs=(), vecs=(), outs=(), sums=(), rider=None):
    tr, nblk = lay.tr, lay.nblk
    n_r, n_s, n_v, n_o = len(rows), len(segs), len(vecs), len(outs)
    lat_only = any(o[2:] for o in outs) or any(a.shape[0] != lay.T for a in rows)
    nsub = 1 if lat_only or nblk % 2 else 2
    tb = tr * nsub

    def body(*refs):
        ins = refs[:n_r + n_s + n_v]
        ors = refs[n_r + n_s + n_v:]
        for sub in range(nsub):
            rs = slice(sub * tr, (sub + 1) * tr)
            seg = lay.seg(pl.program_id(0) * nsub + sub)
            vals = [r[rs, :] for r in ins[:n_r]] + [r[seg] for r in ins[n_r:n_r + n_s]] + [r[...] for r in ins[n_r + n_s:]]
            res = fn(*vals)
            for k in range(n_o):
                ors[k][rs, :] = res[k].astype(ors[k].dtype)
            for k in range(len(sums)):
                ors[n_o + k][sub] = res[n_o + k]

    def all_rows(i):
        return (i, 0)

    def lat_rows(i):
        return ((i // lay.bps) * (lay.bps - lay.cb) + jnp.maximum(i % lay.bps - lay.cb, 0), 0)

    in_specs = [pl.BlockSpec((tb, a.shape[1]), all_rows if a.shape[0] == lay.T else lat_rows) for a in rows]
    in_specs += [pl.BlockSpec(a.shape, lambda i: (0, 0, 0)) for a in segs]
    in_specs += [pl.BlockSpec(a.shape, lambda i: (0, 0)) for a in vecs]
    out_shape = [_sds((2 * lay.L if o[2:] else lay.T, o[0]), o[1]) for o in outs]
    out_shape += [_sds((nblk, r, w), F32) for r, w in sums]
    out_specs = [pl.BlockSpec((tb, o[0]), lat_rows if o[2:] else all_rows) for o in outs]
    out_specs += [pl.BlockSpec((nsub, r, w), lambda i: (i, 0, 0)) for r, w in sums]
    sem = "arbitrary" if any(o[2:] for o in outs) else "parallel"
    if rider is None:
        return pl.pallas_call(body, name=name, out_shape=out_shape, grid=(nblk // nsub,), in_specs=in_specs,
                              out_specs=out_specs, compiler_params=_cp((sem,), VMEM_BIG))(*rows, *segs, *vecs)
    return _host_call(body, rider, name, (nblk // nsub,), in_specs, out_specs, out_shape, (*rows, *segs, *vecs), (sem,),
                      n_r + n_s + n_v, n_o + len(sums))


def modulate(lay, h, mod, k_shift, k_scale, name):
    def fn(hb, m):
        return (hb * (1.0 + m[k_scale:k_scale + 1]) + m[k_shift:k_shift + 1],)
    return rowwise(lay, name, fn, [h], segs=[mod], outs=[(D, BF16)])[0]


def resid_ln(lay, h, y, mod, k_gate, coef, lnv, name, nxt=None, rider=None):
    def fn(hb, yb, m, *rest):
        ln = rest[-1]
        z = ALPHA * hb + (coef * m[k_gate:k_gate + 1]) * yb
        mu = jnp.mean(z, axis=-1, keepdims=True)
        zc = z - mu
        var = jnp.mean(zc * zc, axis=-1, keepdims=True)
        rstd = lax.rsqrt(var + LN_EPS)
        xhat = zc * rstd
        out = xhat * ln[0:1] + ln[1:2]
        if nxt is None:
            return out, xhat, rstd
        mn = rest[0]
        return out, xhat, rstd, out * (1.0 + mn[nxt[2]:nxt[2] + 1]) + mn[nxt[1]:nxt[1] + 1]
    segs = [mod] if nxt is None else [mod, nxt[0]]
    outs = [(D, F32), (D, F32), (1, F32)] + ([] if nxt is None else [(D, BF16)])
    return rowwise(lay, name, fn, [h, y], segs=segs, vecs=[lnv], outs=outs, rider=rider)


def _ln_bwd_math(do, xh, rs, yb, gate, coef, ln):
    dxh = do * ln[0:1]
    m1 = jnp.mean(dxh, axis=-1, keepdims=True)
    m2 = jnp.mean(dxh * xh, axis=-1, keepdims=True)
    dz = rs * (dxh - m1 - xh * m2)
    s = jnp.concatenate([jnp.sum(do, axis=0, keepdims=True), jnp.sum(do * xh, axis=0, keepdims=True),
                         jnp.sum(coef * dz * yb, axis=0, keepdims=True)], axis=0)
    return (coef * gate) * dz, ALPHA * dz, s


def _mod_bwd_math(dr, dm, hb, scale):
    s = jnp.concatenate([jnp.sum(dm, axis=0, keepdims=True), jnp.sum(dm * hb, axis=0, keepdims=True)], axis=0)
    return dr + dm * (1.0 + scale), s


def ln_bwd(lay, dout, xhat, rstd, y, mod, k_gate, coef, lnv, name):
    def fn(do, xh, rs, yb, m, ln):
        return _ln_bwd_math(do, xh, rs, yb, m[k_gate:k_gate + 1], coef, ln)
    return rowwise(lay, name, fn, [dout, xhat, rstd, y], segs=[mod], vecs=[lnv],
                   outs=[(D, BF16), (D, F32)], sums=[(3, D)])


def mod_bwd(lay, dres, dhm, h, mod, k_scale, name, rider=None):
    def fn(dr, dm, hb, m):
        return _mod_bwd_math(dr, dm, hb, m[k_scale:k_scale + 1])
    return rowwise(lay, name, fn, [dres, dhm, h], segs=[mod], outs=[(D, F32, "lat")], sums=[(2, D)], rider=rider)


def modb_lnb(lay, dres, dhm, h, mod, k_scale, xhat, rstd, y, mod_p, k_gate, coef, lnv, name, rider=None):
    def fn(dr, dm, hb, xh, rs, yb, m, mp, ln):
        dh, s2 = _mod_bwd_math(dr, dm, hb, m[k_scale:k_scale + 1])
        dy, dres_p, s1 = _ln_bwd_math(dh, xh, rs, yb, mp[k_gate:k_gate + 1], coef, ln)
        return dy, dres_p, s1, s2
    return rowwise(lay, name, fn, [dres, dhm, h, xhat, rstd, y], segs=[mod, mod_p], vecs=[lnv],
                   outs=[(D, BF16), (D, F32)], sums=[(3, D), (2, D)], rider=rider)


def block_sums(lay, parts, name):
    nblk, r, w = parts.shape

    def body(p_ref, o_ref):
        acc = [None, None, None]
        for i in range(nblk):
            sg = 2 if i % lay.bps < lay.cb else i // lay.bps
            acc[sg] = p_ref[i] if acc[sg] is None else acc[sg] + p_ref[i]
        for k in range(3):
            o_ref[k] = acc[k]
        o_ref[3] = (acc[0] + acc[1]) + acc[2]

    return pl.pallas_call(body, name=name, out_shape=_sds((4, r, w), F32), in_specs=[VMEM_SPEC],
                          out_specs=VMEM_SPEC)(parts)


def mm_nn(a, b, name, out_dtype=F32, bias=None):
    m, k = a.shape
    n = b.shape[1]
    tm = _pick(m, (1152, 768, 512, 256, 128, 64, 32, 16, 8))
    tn = _pick(n, (1024, 768, 640, 512, 384, 256, 128))

    def body(*refs):
        if bias is None:
            a_ref, b_ref, o_ref = refs
            o_ref[...] = _nn(a_ref[...].astype(BF16), b_ref[...].astype(BF16)).astype(o_ref.dtype)
        else:
            a_ref, b_ref, c_ref, o_ref = refs
            o_ref[...] = (_nn(a_ref[...].astype(BF16), b_ref[...].astype(BF16)) + c_ref[...]).astype(o_ref.dtype)

    in_specs = [pl.BlockSpec((tm, k), lambda i, j: (i, 0)), pl.BlockSpec((k, tn), lambda i, j: (0, j))]
    ops = [a, b]
    if bias is not None:
        in_specs.append(pl.BlockSpec((1, tn), lambda i, j: (0, j)))
        ops.append(bias)
    return pl.pallas_call(body, name=name, out_shape=_sds((m, n), out_dtype), grid=(m // tm, n // tn),
                          in_specs=in_specs, out_specs=pl.BlockSpec((tm, tn), lambda i, j: (i, j)),
                          compiler_params=_cp(("parallel", "parallel"), VMEM_BIG))(*ops)


def mm_nt(a, b, name, out_dtype=F32, rider=None):
    m, k = a.shape
    n = b.shape[0]
    tm = _pick(m, (1152, 768, 512, 256, 128, 64, 32, 16, 8))
    tn = _pick(n, (1024, 768, 640, 512, 384, 256, 128))

    def body(a_ref, b_ref, o_ref):
        o_ref[...] = _nt(a_ref[...].astype(BF16), b_ref[...].astype(BF16)).astype(o_ref.dtype)

    res = _host_call(body, rider, name, (m // tm, n // tn),
                     [pl.BlockSpec((tm, k), lambda i, j: (i, 0)), pl.BlockSpec((tn, k), lambda i, j: (j, 0))],
                     [pl.BlockSpec((tm, tn), lambda i, j: (i, j))], [_sds((m, n), out_dtype)], (a, b),
                     ("parallel", "parallel"), 2, 1)
    return res[0] if rider is None else res


def mm_tn(a, b, name, rider=None):
    t, m = a.shape
    n = b.shape[1]
    tk = _pick(t, (1152, 768, 512, 256, 128, 64, 32, 16))
    tm = _pick(m, (512, 384, 256, 128))

    def body(a_ref, b_ref, o_ref):
        @pl.when(pl.program_id(1) == 0)
        def _():
            o_ref[...] = jnp.zeros_like(o_ref)
        o_ref[...] += _tn(a_ref[...].astype(BF16), b_ref[...].astype(BF16))

    res = _host_call(body, rider, name, (m // tm, t // tk),
                     [pl.BlockSpec((tk, tm), lambda i, k: (k, i)), pl.BlockSpec((tk, n), lambda i, k: (k, 0))],
                     [pl.BlockSpec((tm, n), lambda i, k: (i, 0))], [_sds((m, n), F32)], (a, b),
                     ("parallel", "arbitrary"), 2, 1)
    return res[0] if rider is None else res


class Rider:
    def __init__(self, ins, outs, aliases, nsem, start, wait):
        self.ins, self.outs, self.aliases, self.nsem, self.start, self.wait = ins, outs, aliases, nsem, start, wait


def _chips_of(mx, my):
    return [(1 - mx, my), (mx, 1 - my), (1 - mx, 1 - my)]


def rider_gather_d2d(buf):
    def start(ins, outs, ssem, rsem):
        o = outs[0]
        mx, my, mc = _me()
        for j, (px, py) in enumerate(_chips_of(mx, my)):
            ps = 2 * px + py
            _rcopy(o.at[ps, mc], o.at[ps, mc], ssem.at[j], rsem.at[j], (mx, my, 1 - mc)).start()

    def wait(ins, outs, ssem, rsem):
        o = outs[0]
        mx, my, mc = _me()
        sib = (mx, my, 1 - mc)
        for j, (px, py) in enumerate(_chips_of(mx, my)):
            ps = 2 * px + py
            _rcopy(o.at[ps, 1 - mc], o.at[ps, 1 - mc], ssem.at[j], rsem.at[j], sib).wait_recv()
        for j, (px, py) in enumerate(_chips_of(mx, my)):
            ps = 2 * px + py
            _rcopy(o.at[ps, mc], o.at[ps, mc], ssem.at[j], rsem.at[j], sib).wait_send()

    return Rider([buf], [_sds(buf.shape, buf.dtype)], {0: 0}, 3, start, wait)


def rider_reduce_sib(buf):
    n, _, h, w = buf.shape

    def start(ins, outs, ssem, rsem):
        mx, my, mc = _me()
        for k in range(N_CHIP):
            _rcopy(ins[0].at[k, 1 - mc], outs[0].at[k], ssem.at[k], rsem.at[k], (mx, my, 1 - mc)).start()

    def wait(ins, outs, ssem, rsem):
        mx, my, mc = _me()
        for k in range(N_CHIP):
            _rcopy(ins[0].at[k, 1 - mc], outs[0].at[k], ssem.at[k], rsem.at[k], (mx, my, 1 - mc)).wait_recv()
        for k in range(N_CHIP):
            _rcopy(ins[0].at[k, 1 - mc], outs[0].at[k], ssem.at[k], rsem.at[k], (mx, my, 1 - mc)).wait_send()

    return Rider([buf], [_sds((n, h, w), buf.dtype)], {}, N_CHIP, start, wait)


def rider_gather_xy(buf):
    def peers():
        mx, my, mc = _me()
        return 2 * mx + my, mc, [(1 - mx, my), (mx, 1 - my)]

    def start(ins, outs, ssem, rsem):
        o = outs[0]
        s, mc, nb = peers()
        for j, (px, py) in enumerate(nb):
            _rcopy(o.at[s, mc], o.at[s, mc], ssem.at[j], rsem.at[j], (px, py, mc)).start()

    def wait(ins, outs, ssem, rsem):
        o = outs[0]
        s, mc, nb = peers()
        for j, (px, py) in enumerate(nb):
            _rcopy(o.at[2 * px + py, mc], o.at[2 * px + py, mc], ssem.at[j], rsem.at[j], (px, py, mc)).wait_recv()
        for j, (px, py) in enumerate(nb):
            _rcopy(o.at[s, mc], o.at[s, mc], ssem.at[j], rsem.at[j], (px, py, mc)).wait_send()

    return Rider([buf], [_sds(buf.shape, buf.dtype)], {0: 0}, 2, start, wait)


def rider_gather_fwd(buf):
    h2 = buf.shape[2] // 2
    lo, hi = pl.ds(0, h2), pl.ds(h2, buf.shape[2] - h2)

    def start(ins, outs, ssem, rsem):
        o = outs[0]
        mx, my, mc = _me()
        xs, ys = 2 * (1 - mx) + my, 2 * mx + (1 - my)
        _rcopy(o.at[xs, mc, lo], o.at[xs, mc, lo], ssem.at[0], rsem.at[0], (mx, 1 - my, mc)).start()
        _rcopy(o.at[ys, mc, hi], o.at[ys, mc, hi], ssem.at[1], rsem.at[1], (1 - mx, my, mc)).start()

    def wait(ins, outs, ssem, rsem):
        o = outs[0]
        mx, my, mc = _me()
        xs, ys, ds = 2 * (1 - mx) + my, 2 * mx + (1 - my), 2 * (1 - mx) + (1 - my)
        _rcopy(o.at[ds, mc, lo], o.at[ds, mc, lo], ssem.at[0], rsem.at[0], (mx, 1 - my, mc)).wait_recv()
        _rcopy(o.at[ds, mc, hi], o.at[ds, mc, hi], ssem.at[1], rsem.at[1], (1 - mx, my, mc)).wait_recv()
        _rcopy(o.at[xs, mc, lo], o.at[xs, mc, lo], ssem.at[0], rsem.at[0], (mx, 1 - my, mc)).wait_send()
        _rcopy(o.at[ys, mc, hi], o.at[ys, mc, hi], ssem.at[1], rsem.at[1], (1 - mx, my, mc)).wait_send()

    return Rider([buf], [_sds(buf.shape, buf.dtype)], {0: 0}, 2, start, wait)


def rider_reduce_copy(q, j, r=None):
    def peer():
        mx, my, mc = _me()
        px, py = _chips_of(mx, my)[j]
        return 2 * mx + my, 2 * px + py, (px, py, mc)

    def start(ins, outs, ssem, rsem):
        s, ps, dev = peer()
        _rcopy(ins[0].at[ps], outs[0].at[s], ssem.at[0], rsem.at[0], dev).start()

    def wait(ins, outs, ssem, rsem):
        s, ps, dev = peer()
        _rcopy(ins[0].at[ps], outs[0].at[ps], ssem.at[0], rsem.at[0], dev).wait_recv()
        _rcopy(ins[0].at[ps], outs[0].at[s], ssem.at[0], rsem.at[0], dev).wait_send()

    if r is None:
        return Rider([q], [_sds(q.shape, q.dtype)], {}, 1, start, wait)
    return Rider([q, r], [_sds(q.shape, q.dtype)], {1: 0}, 1, start, wait)


def rider_reduce_copies(q):
    def start(ins, outs, ssem, rsem):
        mx, my, mc = _me()
        s = 2 * mx + my
        for j, (px, py) in enumerate(_chips_of(mx, my)):
            _rcopy(ins[0].at[2 * px + py], outs[0].at[s], ssem.at[j], rsem.at[j], (px, py, mc)).start()

    def wait(ins, outs, ssem, rsem):
        mx, my, mc = _me()
        s = 2 * mx + my
        for j, (px, py) in enumerate(_chips_of(mx, my)):
            ps = 2 * px + py
            _rcopy(ins[0].at[ps], outs[0].at[ps], ssem.at[j], rsem.at[j], (px, py, mc)).wait_recv()
        for j, (px, py) in enumerate(_chips_of(mx, my)):
            _rcopy(ins[0].at[2 * px + py], outs[0].at[s], ssem.at[j], rsem.at[j], (px, py, mc)).wait_send()

    return Rider([q], [_sds(q.shape, q.dtype)], {}, 3, start, wait)


def rider_join(buf, g=None):
    def start(ins, outs, ssem, rsem):
        o = outs[0] if g is None else outs[0].at[g]
        mx, my, mc = _me()
        _rcopy(o.at[mc], o.at[mc], ssem.at[0], rsem.at[0], (mx, my, 1 - mc)).start()

    def wait(ins, outs, ssem, rsem):
        o = outs[0] if g is None else outs[0].at[g]
        mx, my, mc = _me()
        _rcopy(o.at[1 - mc], o.at[1 - mc], ssem.at[0], rsem.at[0], (mx, my, 1 - mc)).wait_recv()
        _rcopy(o.at[mc], o.at[mc], ssem.at[0], rsem.at[0], (mx, my, 1 - mc)).wait_send()

    return Rider([buf], [_sds(buf.shape, buf.dtype)], {0: 0}, 1, start, wait)


def _host_call(body, rider, name, grid, in_specs, out_specs, out_shape, operands, sem, n_in, n_out, aliases=None):
    aliases = dict(aliases or {})
    if rider is None:
        return pl.pallas_call(body, name=name, out_shape=out_shape, grid=grid, in_specs=in_specs, out_specs=out_specs,
                              input_output_aliases=aliases, compiler_params=_cp(sem, VMEM_BIG))(*operands)
    n_ri, n_ro = len(rider.ins), len(rider.outs)
    aliases.update({n_in + a: n_out + b for a, b in rider.aliases.items()})

    def hosted(*refs):
        ins, r_in = refs[:n_in], refs[n_in:n_in + n_ri]
        outs, r_out = refs[n_in + n_ri:n_in + n_ri + n_out], refs[n_in + n_ri + n_out:n_in + n_ri + n_out + n_ro]
        ssem, rsem = refs[-2], refs[-1]
        first = functools.reduce(lambda a, b: a & b, [pl.program_id(k) == 0 for k in range(len(grid))])
        last = functools.reduce(lambda a, b: a & b, [pl.program_id(k) == grid[k] - 1 for k in range(len(grid))])

        @pl.when(first)
        def _():
            rider.start(r_in, r_out, ssem, rsem)
        body(*ins, *outs)

        @pl.when(last)
        def _():
            rider.wait(r_in, r_out, ssem, rsem)

    return pl.pallas_call(
        hosted, name=name, out_shape=list(out_shape) + list(rider.outs), grid=grid,
        in_specs=list(in_specs) + [ANY] * n_ri, out_specs=list(out_specs) + [ANY] * n_ro,
        input_output_aliases=aliases,
        scratch_shapes=[pltpu.SemaphoreType.DMA((rider.nsem,)), pltpu.SemaphoreType.DMA((rider.nsem,))],
        compiler_params=_cp(("arbitrary",) * len(grid), VMEM_BIG))(*operands, *rider.ins)


def ffn_up(lay, hm, wbuf, ig, iu, ns, name, rider=None):
    tm = lay.tm

    def body(h_ref, wg_ref, wu_ref, up_ref, sl_ref, a_ref):
        hb = h_ref[...]
        g = _nt(hb, wg_ref[0])
        u = _nt(hb, wu_ref[0])
        sg = _sigmoid(g)
        sl = g * sg
        up_ref[0] = (u * (sg + sl * (1.0 - sg))).astype(BF16)
        sl_ref[0] = sl.astype(BF16)
        a_ref[0] = (sl * u).astype(BF16)

    spec_o = pl.BlockSpec((1, tm, ns), lambda s, i: (s, i, 0))
    return _host_call(
        body, rider, name, (N_CHIP, lay.T // tm),
        [pl.BlockSpec((tm, D), lambda s, i: (i, 0)), pl.BlockSpec((1, ns, D), lambda s, i: (s, ig, 0)),
         pl.BlockSpec((1, ns, D), lambda s, i: (s, iu, 0))],
        [spec_o] * 3, [_sds((N_CHIP, lay.T, ns), BF16)] * 3, (hm, wbuf, wbuf), ("parallel", "parallel"), 3, 3)


def slab_nn_acc(lay, zs, wbuf, idxs, ns, name, rider=None):
    tm = lay.tm2
    npair = len(zs)

    def body(*refs):
        o_ref = refs[-1]

        @pl.when(pl.program_id(1) == 0)
        def _():
            o_ref[...] = jnp.zeros_like(o_ref)
        acc = _nn(refs[0][0], refs[npair][0])
        for p in range(1, npair):
            acc += _nn(refs[p][0], refs[npair + p][0])
        o_ref[...] += acc

    in_specs = [pl.BlockSpec((1, tm, ns), lambda i, s: (s, i, 0)) for _ in zs]
    in_specs += [pl.BlockSpec((1, ns, D), functools.partial(lambda i, s, q: (s, q, 0), q=q)) for q in idxs]
    return _host_call(body, rider, name, (lay.T // tm, N_CHIP), in_specs, [pl.BlockSpec((tm, D), lambda i, s: (i, 0))],
                      [_sds((lay.T, D), F32)], (*zs, *([wbuf] * npair)), ("parallel", "arbitrary"), 2 * npair, 1)


def ffn_bwd_da(lay, dy, wbuf, idn, up, sl, ns, name, rider=None):
    tm = lay.tm

    def body(dy_ref, wd_ref, up_ref, sl_ref, dg_ref, du_ref):
        da = _nt(dy_ref[...], wd_ref[0])
        dg_ref[0] = (da * up_ref[0].astype(F32)).astype(BF16)
        du_ref[0] = (da * sl_ref[0].astype(F32)).astype(BF16)

    spec_z = pl.BlockSpec((1, tm, ns), lambda s, i: (s, i, 0))
    return _host_call(
        body, rider, name, (N_CHIP, lay.T // tm),
        [pl.BlockSpec((tm, D), lambda s, i: (i, 0)), pl.BlockSpec((1, ns, D), lambda s, i: (s, idn, 0)), spec_z, spec_z],
        [spec_z] * 2, [_sds((N_CHIP, lay.T, ns), BF16)] * 2, (dy, wbuf, up, sl), ("parallel", "parallel"), 4, 2)


def slab_tn(lay, z, x, gbuf, idx, ns, name, rider=None):
    tk = _pick(lay.T, (768, 512, 256, 128))

    def body(z_ref, x_ref, g_in, o_ref):
        del g_in

        @pl.when(pl.program_id(0) == 0)
        def _():
            o_ref[...] = jnp.zeros_like(o_ref)
        xv = x_ref[...]
        for s in range(N_CHIP):
            o_ref[s] += _tn(z_ref[s], xv)

    return _host_call(
        body, rider, name, (lay.T // tk,),
        [pl.BlockSpec((N_CHIP, tk, ns), lambda k: (0, k, 0)), pl.BlockSpec((tk, D), lambda k: (k, 0)), ANY],
        [pl.BlockSpec((N_CHIP, ns, D), lambda k: (0, idx, 0))], [_sds(gbuf.shape, F32)], (z, x, gbuf),
        ("arbitrary",), 3, 1, aliases={2: 0})


Q0, K0, V0, U0, QR0, KR0, PEXT = 0, 512, 640, 768, 1280, 1792, 1920


def rope_fwd(lay, p, cos, sin, name):
    def fn(pb, cs, sn):
        cs4 = jnp.concatenate([cs] * 4, axis=1)
        sn4 = jnp.concatenate([sn] * 4, axis=1)
        qr = pb[:, Q0:K0] * cs4 + pb[:, QR0:KR0] * sn4
        kr = pb[:, K0:V0] * cs + pb[:, KR0:PEXT] * sn
        return qr, kr, pb[:, V0:U0], pb[:, U0:QR0]
    return rowwise(lay, name, fn, [p, cos, sin], outs=[(ATT_W, BF16), (KV_W, BF16), (KV_W, BF16), (POOL_W, F32)])


def rope_bwd(lay, dqr, dkr, dv, du, cos, sin, name):
    def fn(dq, dk, dvb, dub, cs, sn):
        cs4 = jnp.concatenate([cs] * 4, axis=1)
        sn4 = jnp.concatenate([sn] * 4, axis=1)
        return (jnp.concatenate([dq * cs4, dk * cs, dvb, dub, dq * sn4, dk * sn], axis=1),)
    return rowwise(lay, name, fn, [dqr, dkr, dv, du, cos, sin], outs=[(PEXT, BF16)])[0]


def _attn_specs(lay):
    nbs, cbk, lbk = lay.PS // BLK, lay.C // BLK, lay.L // BLK

    def kv_map(j):
        return lambda s, n: (s * nbs + cbk + jnp.clip(n - cbk + j - 1, 0, lbk - 1), 0)

    win = [pl.BlockSpec((BLK, KV_W), kv_map(j)) for j in range(3)]
    ctx = pl.BlockSpec((lay.C, KV_W), lambda s, n: (s * (lay.PS // lay.C), 0))
    return nbs, cbk, lbk, win, ctx


def _attn_masks(n, cbk, lbk):
    row = lax.broadcasted_iota(jnp.int32, (BLK, BLK), 0)
    col = lax.broadcasted_iota(jnp.int32, (BLK, BLK), 1)
    m = n - cbk
    lat = n >= cbk
    valid = [lat & (m >= 1) & (col >= row), lat & (col >= 0), lat & (m <= lbk - 2) & (col <= row)]
    lane_lo = lax.broadcasted_iota(jnp.int32, (BLK, 2 * HEAD_DIM), 1) < HEAD_DIM
    return valid, lane_lo


def attn_fwd(lay, qr, kr, vb, sink_tab, name):
    nbs, cbk, lbk, win, ctx = _attn_specs(lay)

    def body(q_ref, k0, k1, k2, kc_ref, v0, v1, v2, vc_ref, sk_ref, o_ref, l_ref):
        n = pl.program_id(1)
        valid, lane_lo = _attn_masks(n, cbk, lbk)
        valid4 = [jnp.concatenate([v] * 4, axis=0) for v in valid]
        ks = [k0[...], k1[...], k2[...]]
        vs = [v0[...], v1[...], v2[...]]
        kc, vc = kc_ref[...], vc_ref[...]
        q2s = [q_ref[:, p * 128:(p + 1) * 128] for p in range(4)]
        outs, lses = [], []
        for hh in range(2):
            sel = lane_lo == (hh == 0)
            qm = jnp.concatenate([jnp.where(sel, q2, jnp.zeros_like(q2)) for q2 in q2s], axis=0)
            sk = jnp.concatenate([jnp.broadcast_to(sk_ref[p:p + 1, hh * HEAD_DIM:hh * HEAD_DIM + 1], (BLK, 1))
                                  for p in range(4)], axis=0)
            sw = [jnp.where(valid4[j], _nt(qm, ks[j]) * ATT_SCALE, NEG_INF) for j in range(3)]
            sc = _nt(qm, kc) * ATT_SCALE
            mx = jnp.maximum(jnp.maximum(jnp.maximum(sw[0].max(-1, keepdims=True), sw[1].max(-1, keepdims=True)),
                                         jnp.maximum(sw[2].max(-1, keepdims=True), sc.max(-1, keepdims=True))), sk)
            ew = [jnp.exp(s - mx) for s in sw]
            ec = jnp.exp(sc - mx)
            den = ew[0].sum(-1, keepdims=True) + ew[1].sum(-1, keepdims=True) + ew[2].sum(-1, keepdims=True)
            den = den + ec.sum(-1, keepdims=True) + jnp.exp(sk - mx)
            o = _nn((ec / den).astype(BF16), vc)
            for j in range(3):
                o += _nn((ew[j] / den).astype(BF16), vs[j])
            outs.append(o)
            lses.append(mx + jnp.log(den))
        for p in range(4):
            rows = slice(p * BLK, (p + 1) * BLK)
            o_ref[:, p * 128:(p + 1) * 128] = jnp.where(lane_lo, outs[0][rows], outs[1][rows]).astype(o_ref.dtype)
            l_ref[:, p * 128:(p + 1) * 128] = jnp.where(lane_lo, jnp.broadcast_to(lses[0][rows], (BLK, 128)),
                                                        jnp.broadcast_to(lses[1][rows], (BLK, 128)))

    qspec = pl.BlockSpec((BLK, ATT_W), lambda s, n: (s * nbs + n, 0))
    return pl.pallas_call(
        body, name=name, out_shape=[_sds((lay.T, ATT_W), BF16), _sds((lay.T, ATT_W), F32)], grid=(2, nbs),
        in_specs=[qspec] + win + [ctx] + win + [ctx] + [pl.BlockSpec((8, 128), lambda s, n: (0, 0))],
        out_specs=[qspec, qspec], compiler_params=_cp(("parallel", "parallel")))(qr, kr, kr, kr, kr, vb, vb, vb, vb, sink_tab)


def attn_bwd(lay, qr, kr, vb, sink_tab, lse, datt, name, rider=None):
    nbs, cbk, lbk, win, ctx = _attn_specs(lay)
    C, PS = lay.C, lay.PS

    def body(q_ref, k0, k1, k2, kc_ref, v0, v1, v2, vc_ref, sk_ref, l_ref, do_ref, dq_ref, dk_ref, dv_ref, ds_ref):
        n = pl.program_id(1)
        valid, lane_lo = _attn_masks(n, cbk, lbk)

        @pl.when(n == 0)
        def _():
            dk_ref[...] = jnp.zeros_like(dk_ref)
            dv_ref[...] = jnp.zeros_like(dv_ref)
            ds_ref[...] = jnp.zeros_like(ds_ref)

        ks = [k0[...], k1[...], k2[...], kc_ref[...]]
        vs = [v0[...], v1[...], v2[...], vc_ref[...]]
        valid4 = [jnp.concatenate([v] * 4, axis=0) for v in valid]
        dks = [jnp.zeros((BLK, KV_W), F32)] * 3 + [jnp.zeros((C, KV_W), F32)]
        dvs = list(dks)
        q2s = [q_ref[:, p * 128:(p + 1) * 128] for p in range(4)]
        do2s = [do_ref[:, p * 128:(p + 1) * 128].astype(BF16) for p in range(4)]
        lse2s = [l_ref[:, p * 128:(p + 1) * 128] for p in range(4)]
        dq_h, dd_h = [], []
        for hh in range(2):
            sel = lane_lo == (hh == 0)
            qm = jnp.concatenate([jnp.where(sel, q2, jnp.zeros_like(q2)) for q2 in q2s], axis=0)
            dom = jnp.concatenate([jnp.where(sel, d2, jnp.zeros_like(d2)) for d2 in do2s], axis=0)
            lse_h = jnp.concatenate([l2[:, hh * HEAD_DIM:hh * HEAD_DIM + 1] for l2 in lse2s], axis=0)
            ps, dps = [], []
            for j in range(4):
                s = _nt(qm, ks[j]) * ATT_SCALE
                if j < 3:
                    s = jnp.where(valid4[j], s, NEG_INF)
                ps.append(jnp.exp(s - lse_h))
                dps.append(_nt(dom, vs[j]))
            dd = (ps[0] * dps[0]).sum(-1, keepdims=True) + (ps[1] * dps[1]).sum(-1, keepdims=True)
            dd = dd + (ps[2] * dps[2]).sum(-1, keepdims=True) + (ps[3] * dps[3]).sum(-1, keepdims=True)
            dq = jnp.zeros((4 * BLK, 128), F32)
            for j in range(4):
                dsb = (ps[j] * (dps[j] - dd) * ATT_SCALE).astype(BF16)
                dq += _nn(dsb, ks[j])
                dks[j] = dks[j] + _tn(dsb, qm)
                dvs[j] = dvs[j] + _tn(ps[j].astype(BF16), dom)
            dq_h.append(dq)
            dd_h.append(dd)
        for p in range(4):
            sl = slice(p * 128, (p + 1) * 128)
            rows = slice(p * BLK, (p + 1) * BLK)
            dq_ref[:, sl] = jnp.where(lane_lo, dq_h[0][rows], dq_h[1][rows])
            dd2 = jnp.where(lane_lo, jnp.broadcast_to(dd_h[0][rows], (BLK, 128)), jnp.broadcast_to(dd_h[1][rows], (BLK, 128)))
            psink = jnp.exp(sk_ref[p:p + 1, :] - lse2s[p])
            ds_ref[0, p:p + 1, :] += -jnp.sum(psink * dd2, axis=0, keepdims=True)
        dk_ref[0:C, :] += dks[3]
        dv_ref[0:C, :] += dvs[3]
        for j in range(3):
            r0 = pl.multiple_of((cbk + jnp.clip(n - cbk + j - 1, 0, lbk - 1)) * BLK, BLK)
            dk_ref[pl.ds(r0, BLK), :] += dks[j]
            dv_ref[pl.ds(r0, BLK), :] += dvs[j]

    qspec = pl.BlockSpec((BLK, ATT_W), lambda s, n: (s * nbs + n, 0))
    kvout = pl.BlockSpec((PS, KV_W), lambda s, n: (s, 0))
    return _host_call(
        body, rider, name, (2, nbs),
        [qspec] + win + [ctx] + win + [ctx] + [pl.BlockSpec((8, 128), lambda s, n: (0, 0)), qspec, qspec],
        [qspec, kvout, kvout, pl.BlockSpec((1, 8, 128), lambda s, n: (s, 0, 0))],
        [_sds((lay.T, ATT_W), F32), _sds((lay.T, KV_W), F32), _sds((lay.T, KV_W), F32), _sds((2, 8, 128), F32)],
        (qr, kr, kr, kr, kr, vb, vb, vb, vb, sink_tab, lse, datt), ("parallel", "arbitrary"), 12, 4)


def _winsum(x, r):
    n = x.shape[0]
    t = lax.broadcasted_iota(jnp.int32, x.shape, 0)
    acc = x
    for o in range(1, r + 1):
        acc = acc + jnp.where(t >= o, pltpu.roll(x, o, 0), 0.0) + jnp.where(t < n - o, pltpu.roll(x, n - o, 0), 0.0)
    return acc


def _wincount(n, r):
    t = lax.broadcasted_iota(jnp.int32, (n, 128), 0)
    return (jnp.minimum(t + r, n - 1) - jnp.maximum(t - r, 0) + 1).astype(F32)


def pool_fwd(lay, u, w_pool, scale, name):
    segs = [(0, lay.C), (lay.C, lay.L)]

    def body(u_ref, w_ref, s_ref, o_ref):
        for r0, n in segs:
            for g, wd in enumerate(POOL_WINDOWS):
                sl = slice(g * 128, (g + 1) * 128)
                x = u_ref[r0:r0 + n, sl]
                d = _winsum(x, wd // 2) / _wincount(n, wd // 2) - x
                y = _nn(d.astype(BF16), w_ref[g].astype(BF16)) * s_ref[:, sl]
                o_ref[r0:r0 + n, sl] = y.astype(o_ref.dtype)

    spec = pl.BlockSpec((lay.PS, POOL_W), lambda s: (s, 0))
    return pl.pallas_call(
        body, name=name, out_shape=_sds((lay.T, POOL_W), BF16), grid=(2,),
        in_specs=[spec, pl.BlockSpec(w_pool.shape, lambda s: (0, 0, 0)), pl.BlockSpec((1, POOL_W), lambda s: (0, 0))],
        out_specs=spec, compiler_params=_cp(("parallel",), VMEM_BIG))(u, w_pool, scale)


def pool_bwd(lay, u, dcat, w_pool, scale, name):
    segs = [(0, lay.C), (lay.C, lay.L)]

    def body(u_ref, dp_ref, w_ref, s_ref, du_ref, dw_ref, dsc_ref):
        for g, wd in enumerate(POOL_WINDOWS):
            sl = slice(g * 128, (g + 1) * 128)
            wb = w_ref[g].astype(BF16)
            dw = jnp.zeros((128, 128), F32)
            dsc = jnp.zeros((1, 128), F32)
            for r0, n in segs:
                x = u_ref[r0:r0 + n, sl]
                cnt = _wincount(n, wd // 2)
                d = (_winsum(x, wd // 2) / cnt - x).astype(BF16)
                dp = dp_ref[r0:r0 + n, sl]
                dsc += jnp.sum(_nn(d, wb) * dp, axis=0, keepdims=True)
                dyp = (dp * s_ref[:, sl]).astype(BF16)
                dw += _tn(d, dyp)
                dd = _nt(dyp, wb)
                du_ref[r0:r0 + n, sl] = _winsum(dd / cnt, wd // 2) - dd
            dw_ref[0, g] = dw
            dsc_ref[0, :, sl] = dsc

    spec = pl.BlockSpec((lay.PS, POOL_W), lambda s: (s, 0))
    return pl.pallas_call(
        body, name=name,
        out_shape=[_sds((lay.T, POOL_W), F32), _sds((2, 4, 128, 128), F32), _sds((2, 1, POOL_W), F32)], grid=(2,),
        in_specs=[spec, pl.BlockSpec((lay.PS, POOL_W), lambda s: (s, 1)), pl.BlockSpec(w_pool.shape, lambda s: (0, 0, 0)),
                  pl.BlockSpec((1, POOL_W), lambda s: (0, 0))],
        out_specs=[spec, pl.BlockSpec((1, 4, 128, 128), lambda s: (s, 0, 0, 0)), pl.BlockSpec((1, 1, POOL_W), lambda s: (s, 0, 0))],
        compiler_params=_cp(("parallel",), VMEM_BIG))(u, dcat, w_pool, scale)


CONV_OFFS = (-1, 0, 1, 2)
CW = 256


def _shift_rows(x, o):
    if o == 0:
        return x
    n = x.shape[0]
    t = lax.broadcasted_iota(jnp.int32, x.shape, 0)
    if o < 0:
        return jnp.where(t >= -o, pltpu.roll(x, -o, 0), 0.0)
    return jnp.where(t < n - o, pltpu.roll(x, n - o, 0), 0.0)


def conv_fwd(lay, p, col0, w, b, name):
    segs = [(0, lay.C), (lay.C, lay.L)]
    cb0 = col0 // CW

    def body(x_ref, w_ref, b_ref, o_ref):
        for r0, n in segs:
            x = x_ref[r0:r0 + n, :]
            y = jnp.broadcast_to(b_ref[...], x.shape)
            for k, o in enumerate(CONV_OFFS):
                y = y + _shift_rows(x, o) * w_ref[k:k + 1, :]
            o_ref[r0:r0 + n, :] = y

    return pl.pallas_call(
        body, name=name, out_shape=_sds((lay.T, D), F32), grid=(2, D // CW),
        in_specs=[pl.BlockSpec((lay.PS, CW), lambda s, j: (s, cb0 + j)), pl.BlockSpec((4, CW), lambda s, j: (0, j)),
                  pl.BlockSpec((1, CW), lambda s, j: (0, j))],
        out_specs=pl.BlockSpec((lay.PS, CW), lambda s, j: (s, j)),
        compiler_params=_cp(("parallel", "parallel")))(p, w, b)


def conv_bwd(lay, p, col0, w, duc, name):
    segs = [(0, lay.C), (lay.C, lay.L)]
    cb0 = col0 // CW

    def body(x_ref, w_ref, g_ref, du_ref, dw_ref, db_ref):
        dws = [jnp.zeros((1, CW), F32)] * 4
        db = jnp.zeros((1, CW), F32)
        for r0, n in segs:
            x = x_ref[r0:r0 + n, :]
            g = g_ref[r0:r0 + n, :]
            du = jnp.zeros_like(g)
            for k, o in enumerate(CONV_OFFS):
                du = du + _shift_rows(g, -o) * w_ref[k:k + 1, :]
                dws[k] = dws[k] + jnp.sum(g * _shift_rows(x, o), axis=0, keepdims=True)
            db = db + jnp.sum(g, axis=0, keepdims=True)
            du_ref[r0:r0 + n, :] = du.astype(du_ref.dtype)
        dw_ref[0] = jnp.concatenate(dws, axis=0)
        db_ref[0] = db

    return pl.pallas_call(
        body, name=name, out_shape=[_sds((lay.T, D), BF16), _sds((2, 4, D), F32), _sds((2, 1, D), F32)], grid=(2, D // CW),
        in_specs=[pl.BlockSpec((lay.PS, CW), lambda s, j: (s, cb0 + j)), pl.BlockSpec((4, CW), lambda s, j: (0, j)),
                  pl.BlockSpec((lay.PS, CW), lambda s, j: (s, j))],
        out_specs=[pl.BlockSpec((lay.PS, CW), lambda s, j: (s, j)), pl.BlockSpec((1, 4, CW), lambda s, j: (s, 0, j)),
                   pl.BlockSpec((1, 1, CW), lambda s, j: (s, 0, j))],
        compiler_params=_cp(("parallel", "parallel")))(p, w, duc)


def _softplus_neg(lam):
    z = -lam
    w = jnp.exp(-jnp.abs(z))
    log1p = jnp.where(w < 1e-2, w * (1.0 - w * (0.5 - w / 3.0)), jnp.log(1.0 + w))
    return jnp.maximum(z, 0.0) + log1p, -_sigmoid(z)


def _neg_expm1(x):
    series = -x * (1.0 + x * (0.5 + x * (1.0 / 6.0 + x * (1.0 / 24.0 + x * (1.0 / 120.0)))))
    return jnp.where(x > -0.05, series, 1.0 - jnp.exp(x))


def _lru_gates(x, xb, wa, wx, ba, bx, lam):
    r = _sigmoid(_nn(xb, wa.astype(BF16)) + ba)
    gi = _sigmoid(_nn(xb, wx.astype(BF16)) + bx)
    sp, dsp = _softplus_neg(lam)
    la = -LRU_C * r * sp
    a = jnp.exp(la)
    sq = jnp.sqrt(_neg_expm1(2.0 * la))
    return r, gi, sp, dsp, a, sq


def lru_coeffs(lay, uc, wa, wx, vec, name):
    tr = lay.tc

    def body(x_ref, wa_ref, wx_ref, v_ref, a_ref, b_ref):
        for h in range(8):
            sl = slice(h * 128, (h + 1) * 128)
            x = x_ref[:, sl]
            xb = x.astype(BF16)
            for d in range(2):
                _, gi, _, _, a, sq = _lru_gates(x, xb, wa_ref[d, h], wx_ref[d, h], v_ref[d:d + 1, sl],
                                                v_ref[2 + d:3 + d, sl], v_ref[4 + d:5 + d, sl])
                a_ref[d, h] = a
                b_ref[d, h] = sq * (gi * x)

    wspec = pl.BlockSpec((2, 8, 128, 128), lambda i: (0, 0, 0, 0))
    ospec = pl.BlockSpec((2, 8, tr, 128), lambda i: (0, 0, i, 0))
    return pl.pallas_call(
        body, name=name, out_shape=[_sds((2, 8, lay.T, 128), F32)] * 2, grid=(lay.T // tr,),
        in_specs=[pl.BlockSpec((tr, D), lambda i: (i, 0)), wspec, wspec, pl.BlockSpec((6, D), lambda i: (0, 0))],
        out_specs=[ospec, ospec], compiler_params=_cp(("parallel",), VMEM_BIG))(uc, wa, wx, vec)


def lru_coeffs_bwd(lay, uc, wa, wx, vec, da, db, name, rider=None):
    tr = lay.tc

    def body(x_ref, wa_ref, wx_ref, v_ref, da_ref, db_ref, dx_ref, dwa_ref, dwx_ref, dv_ref):
        @pl.when(pl.program_id(0) == 0)
        def _():
            dwa_ref[...] = jnp.zeros_like(dwa_ref)
            dwx_ref[...] = jnp.zeros_like(dwx_ref)
            dv_ref[...] = jnp.zeros_like(dv_ref)

        for h in range(8):
            sl = slice(h * 128, (h + 1) * 128)
            x = x_ref[:, sl]
            xb = x.astype(BF16)
            dx = jnp.zeros_like(x)
            for d in range(2):
                wab, wxb = wa_ref[d, h].astype(BF16), wx_ref[d, h].astype(BF16)
                r, gi, sp, dsp, a, sq = _lru_gates(x, xb, wa_ref[d, h], wx_ref[d, h], v_ref[d:d + 1, sl],
                                                   v_ref[2 + d:3 + d, sl], v_ref[4 + d:5 + d, sl])
                dbv, dav = db_ref[d, h], da_ref[d, h]
                t1 = dbv * sq
                dgi = t1 * x
                dx = dx + t1 * gi
                dla = dav * a - (dbv * gi * x) * (a * a) / sq
                dr = dla * (-LRU_C * sp)
                dlam = jnp.sum(dla * (-LRU_C * r), axis=0, keepdims=True) * dsp
                dpa = dr * r * (1.0 - r)
                dpx = dgi * gi * (1.0 - gi)
                dpab, dpxb = dpa.astype(BF16), dpx.astype(BF16)
                dwa_ref[d, h] += _tn(xb, dpab)
                dwx_ref[d, h] += _tn(xb, dpxb)
                dx = dx + _nt(dpab, wab) + _nt(dpxb, wxb)
                dv_ref[d:d + 1, sl] += jnp.sum(dpa, axis=0, keepdims=True)
                dv_ref[2 + d:3 + d, sl] += jnp.sum(dpx, axis=0, keepdims=True)
                dv_ref[4 + d:5 + d, sl] += dlam
            dx_ref[:, sl] = dx

    wspec = pl.BlockSpec((2, 8, 128, 128), lambda i: (0, 0, 0, 0))
    gspec = pl.BlockSpec((2, 8, tr, 128), lambda i: (0, 0, i, 0))
    vspec = pl.BlockSpec((6, D), lambda i: (0, 0))
    xspec = pl.BlockSpec((tr, D), lambda i: (i, 0))
    return _host_call(
        body, rider, name, (lay.T // tr,), [xspec, wspec, wspec, vspec, gspec, gspec], [xspec, wspec, wspec, vspec],
        [_sds((lay.T, D), F32), _sds((2, 8, 128, 128), F32), _sds((2, 8, 128, 128), F32), _sds((6, D), F32)],
        (uc, wa, wx, vec, da, db), ("arbitrary",), 6, 4)


GB = 2
SCAN_UNROLL = 4


def _tile_scan(a, b, up):
    t = lax.broadcasted_iota(jnp.int32, a.shape, 0)
    for d in (1, 2, 4):
        sh = 8 - d if up else d
        m = (t < 8 - d) if up else (t >= d)
        a_prev, b_prev = pltpu.roll(a, sh, 0), pltpu.roll(b, sh, 0)
        b = jnp.where(m, a * b_prev + b, b)
        a = jnp.where(m, a * a_prev, a)
    return a, b


def lru_scan(lay, a, b, name):
    segs = [(0, lay.C), (lay.C, lay.L)]

    def body(a_ref, b_ref, s_ref):
        for d in range(2):
            rev = d == 1
            state = tuple(jnp.zeros((1, 128), F32) for _ in range(GB))
            for base, n in segs:
                nt = n // 8

                def step(j, c, base=base, nt=nt, rev=rev, d=d):
                    c = list(c)
                    for u in range(SCAN_UNROLL):
                        jj = j * SCAN_UNROLL + u
                        r0 = pl.multiple_of(base + 8 * ((nt - 1 - jj) if rev else jj), 8)
                        for g in range(GB):
                            at, bt = _tile_scan(a_ref[d, g, pl.ds(r0, 8), :], b_ref[d, g, pl.ds(r0, 8), :], rev)
                            h = at * c[g] + bt
                            s_ref[d, g, pl.ds(r0, 8), :] = h
                            c[g] = h[0:1] if rev else h[7:8]
                    return tuple(c)

                state = lax.fori_loop(0, nt // SCAN_UNROLL, step, state)

    spec = pl.BlockSpec((2, GB, lay.PS, 128), lambda s, hb: (0, hb, s, 0))
    return pl.pallas_call(
        body, name=name, out_shape=_sds((2, 8, lay.T, 128), F32), grid=(2, 8 // GB),
        in_specs=[spec, spec], out_specs=spec, compiler_params=_cp(("parallel", "parallel"), VMEM_BIG))(a, b)


def lru_scan_bwd(lay, a, s, dy, name):
    segs = [(0, lay.C), (lay.C, lay.L)]
    C, PS = lay.C, lay.PS

    def body(a_ref, s_ref, g_ref, da_ref, db_ref):
        t = lax.broadcasted_iota(jnp.int32, (8, 128), 0)
        for d in range(2):
            rev = d == 1
            carry = tuple(jnp.zeros((1, 128), F32) for _ in range(GB))
            for si in (1, 0):
                base, n = segs[si]
                nt = n // 8

                def step(j, c, base=base, nt=nt, rev=rev, d=d):
                    c = list(c)
                    for u in range(SCAN_UNROLL):
                        jj = j * SCAN_UNROLL + u
                        r0 = pl.multiple_of(base + 8 * (jj if rev else (nt - 1 - jj)), 8)
                        if rev:
                            rn = pl.multiple_of(jnp.where(r0 == PS - 8, 0, r0 + 8), 8)
                            nb_zero = r0 == C - 8
                        else:
                            rn = pl.multiple_of(jnp.maximum(r0 - 8, 0), 8)
                            nb_zero = r0 == 0
                        for g in range(GB):
                            av = a_ref[d, g, pl.ds(r0, 8), :]
                            gv = g_ref[g, pl.ds(r0, 8), :]
                            sv = s_ref[d, g, pl.ds(r0, 8), :]
                            nbt = s_ref[d, g, pl.ds(rn, 8), :]
                            at, bt = _tile_scan(av, av * gv, not rev)
                            m = at * c[g] + bt
                            if rev:
                                m_next = jnp.where(t >= 1, pltpu.roll(m, 1, 0), c[g])
                                nb = jnp.where(nb_zero, 0.0, nbt[0:1])
                                h_prev = jnp.where(t < 7, pltpu.roll(sv, 7, 0), nb)
                                c[g] = m[7:8]
                            else:
                                m_next = jnp.where(t < 7, pltpu.roll(m, 7, 0), c[g])
                                nb = jnp.where(nb_zero, 0.0, nbt[7:8])
                                h_prev = jnp.where(t >= 1, pltpu.roll(sv, 1, 0), nb)
                                c[g] = m[0:1]
                            lam = gv + m_next
                            db_ref[d, g, pl.ds(r0, 8), :] = lam
                            da_ref[d, g, pl.ds(r0, 8), :] = lam * h_prev
                    return tuple(c)

                carry = lax.fori_loop(0, nt // SCAN_UNROLL, step, carry)

    spec = pl.BlockSpec((2, GB, lay.PS, 128), lambda s, hb: (0, hb, s, 0))
    return pl.pallas_call(
        body, name=name, out_shape=[_sds((2, 8, lay.T, 128), F32)] * 2, grid=(2, 8 // GB),
        in_specs=[spec, spec, pl.BlockSpec((GB, lay.PS, 128), lambda s, hb: (hb, s, 0))],
        out_specs=[spec, spec], compiler_params=_cp(("parallel", "parallel"), VMEM_BIG))(a, s, dy)


def _gelu(x):
    k = math.sqrt(2.0 / math.pi)
    t = jnp.tanh(k * (x + 0.044715 * x * x * x))
    return 0.5 * x * (1.0 + t), 0.5 * (1.0 + t) + 0.5 * x * (1.0 - t * t) * k * (1.0 + 3 * 0.044715 * x * x)


def lru_gate(lay, p, s, name):
    tr = lay.tr

    def body(g_ref, s_ref, o_ref):
        for h in range(8):
            sl = slice(h * 128, (h + 1) * 128)
            o_ref[:, sl] = (_gelu(g_ref[:, sl])[0] * (s_ref[0, h] + s_ref[1, h])).astype(o_ref.dtype)

    return pl.pallas_call(
        body, name=name, out_shape=_sds((lay.T, D), BF16), grid=(lay.nblk,),
        in_specs=[pl.BlockSpec((tr, D), lambda i: (i, 0)), pl.BlockSpec((2, 8, tr, 128), lambda i: (0, 0, i, 0))],
        out_specs=pl.BlockSpec((tr, D), lambda i: (i, 0)), compiler_params=_cp(("parallel",)))(p, s)


def lru_gate_bwd(lay, p, s, do, name):
    tr = lay.tr

    def body(g_ref, s_ref, do_ref, dg_ref, dy_ref):
        for h in range(8):
            sl = slice(h * 128, (h + 1) * 128)
            ge, dge = _gelu(g_ref[:, sl])
            dov = do_ref[:, sl]
            dg_ref[:, sl] = (dov * (s_ref[0, h] + s_ref[1, h]) * dge).astype(dg_ref.dtype)
            dy_ref[h] = dov * ge

    xspec = pl.BlockSpec((tr, D), lambda i: (i, 0))
    return pl.pallas_call(
        body, name=name, out_shape=[_sds((lay.T, D), BF16), _sds((8, lay.T, 128), F32)], grid=(lay.nblk,),
        in_specs=[xspec, pl.BlockSpec((2, 8, tr, 128), lambda i: (0, 0, i, 0)), xspec],
        out_specs=[xspec, pl.BlockSpec((8, tr, 128), lambda i: (0, i, 0))],
        compiler_params=_cp(("parallel",)))(p, s, do)


def silu_rows(x, name):
    def body(x_ref, o_ref):
        v = x_ref[...]
        o_ref[...] = (v * _sigmoid(v)).astype(o_ref.dtype)
    return pl.pallas_call(body, name=name, out_shape=_sds(x.shape, BF16), in_specs=[VMEM_SPEC], out_specs=VMEM_SPEC)(x)


def mod_grad_rows(gath, name):
    w = gath.shape[-1]

    def body(g_ref, dm_ref, db_ref):
        dm_ref[...] = jnp.zeros_like(dm_ref)
        for l in range(2):
            ctx = g_ref[0, 3 * l + 2:3 * l + 3, :]
            tot = g_ref[0, 3 * l:3 * l + 1, :] + g_ref[0, 3 * l + 1:3 * l + 2, :]
            for k in range(8):
                dm_ref[l, 2 * k:2 * k + 2, :] = g_ref[k, 3 * l:3 * l + 2, :]
                if k:
                    ctx = ctx + g_ref[k, 3 * l + 2:3 * l + 3, :]
                    tot = tot + (g_ref[k, 3 * l:3 * l + 1, :] + g_ref[k, 3 * l + 1:3 * l + 2, :])
            dm_ref[l, 16:17, :] = ctx
            db_ref[l:l + 1, :] = tot + ctx

    return pl.pallas_call(body, name=name, out_shape=[_sds((2, 32, w), F32), _sds((2, w), F32)],
                          in_specs=[VMEM_SPEC], out_specs=[VMEM_SPEC, VMEM_SPEC])(gath)


def cctx_grad(p, c_ctx, name):
    def body(a_ref, c_ref, o_ref):
        cv = c_ref[...]
        sg = _sigmoid(cv)
        o_ref[...] = 0.5 * (a_ref[0, 0:1, :] + a_ref[1, 0:1, :]) * (sg * (1.0 + cv * (1.0 - sg)))
    return pl.pallas_call(body, name=name, out_shape=_sds((1, D), F32), in_specs=[VMEM_SPEC] * 2,
                          out_specs=VMEM_SPEC)(p, c_ctx)


def loss_and_grad(lay, h, tgt, name):
    def fn(hb, tb):
        lat = (pl.program_id(0) % lay.bps) >= lay.cb
        e = jnp.where(lat, hb - tb, 0.0)
        return e * (1.0 / D), jnp.sum(e * e, axis=0, keepdims=True) * (0.5 / D)
    return rowwise(lay, name, fn, [h, tgt], outs=[(D, F32)], sums=[(1, D)])


def adamw(w, g, m, v, name):
    shape = w.shape
    w2, g2, m2, v2 = (t.reshape(-1, shape[-1]) for t in (w, g, m, v))
    rows, width = w2.shape
    tr = 256 if rows % 256 == 0 else rows
    c1 = 1.0 - ADAM_B1 ** ADAM_STEP
    c2 = 1.0 - ADAM_B2 ** ADAM_STEP

    def body(w_ref, g_ref, m_ref, v_ref, d_ref, mo_ref, vo_ref):
        gv = g_ref[...]
        mn = ADAM_B1 * m_ref[...] + (1.0 - ADAM_B1) * gv
        vn = ADAM_B2 * v_ref[...] + (1.0 - ADAM_B2) * (gv * gv)
        d_ref[...] = -ADAM_LR * ((mn / c1) / (jnp.sqrt(vn / c2) + ADAM_EPS) + ADAM_WD * w_ref[...])
        mo_ref[...] = mn
        vo_ref[...] = vn

    spec = pl.BlockSpec((tr, width), lambda i: (i, 0))
    d, mn, vn = pl.pallas_call(body, name=name, out_shape=[_sds((rows, width), F32)] * 3, grid=(rows // tr,),
                               in_specs=[spec] * 4, out_specs=[spec] * 3, compiler_params=_cp(("parallel",)))(w2, g2, m2, v2)
    return d.reshape(shape), mn.reshape(shape), vn.reshape(shape)


def adamw_ffn(w, m, v, red, kind, ns, name):
    shape = w.shape
    w2, m2, v2 = (t.reshape(-1, shape[-1]) for t in (w, m, v))
    rows, width = w2.shape
    c1 = 1.0 - ADAM_B1 ** ADAM_STEP
    c2 = 1.0 - ADAM_B2 ** ADAM_STEP
    tr, nb = ns // 2, 2
    gspec = pl.BlockSpec((tr, D), lambda i: (((i // nb) * 3 + kind) * nb + i % nb, 0))

    def body(w_ref, g_ref, m_ref, v_ref, go_ref, d_ref, mo_ref, vo_ref):
        gv = g_ref[...]
        mn = ADAM_B1 * m_ref[...] + (1.0 - ADAM_B1) * gv
        vn = ADAM_B2 * v_ref[...] + (1.0 - ADAM_B2) * (gv * gv)
        go_ref[...] = gv
        d_ref[...] = -ADAM_LR * ((mn / c1) / (jnp.sqrt(vn / c2) + ADAM_EPS) + ADAM_WD * w_ref[...])
        mo_ref[...] = mn
        vo_ref[...] = vn

    spec = pl.BlockSpec((tr, width), lambda i: (i, 0))
    outs = pl.pallas_call(body, name=name, out_shape=[_sds((rows, width), F32)] * 4, grid=(rows // tr,),
                          in_specs=[spec, gspec, spec, spec], out_specs=[spec] * 4,
                          compiler_params=_cp(("parallel",)))(w2, red, m2, v2)
    return tuple(t.reshape(shape) for t in outs)


def mod_mm(sc, w_mod, bias, name):
    wm = w_mod.shape[-1]
    tn = _pick(wm, (768, 512, 384, 256, 128))

    def body(a_ref, b_ref, c_ref, o_ref):
        o_ref[...] = _nn(a_ref[...], b_ref[...].astype(BF16)) + c_ref[...]

    return pl.pallas_call(
        body, name=name, out_shape=_sds((DEPTH, 32, wm), F32), grid=(DEPTH, wm // tn),
        in_specs=[pl.BlockSpec((32, D), lambda l, j: (0, 0)), pl.BlockSpec((None, D, tn), lambda l, j: (l, 0, j)),
                  pl.BlockSpec((None, 1, tn), lambda l, j: (l, 0, j))],
        out_specs=pl.BlockSpec((None, 32, tn), lambda l, j: (l, 0, j)),
        compiler_params=_cp(("parallel", "parallel")))(sc, w_mod, bias)


def wmod_dw(sc, dcol, name):
    wm = dcol.shape[-1]
    tm = 256

    def body(a_ref, b_ref, o_ref):
        o_ref[...] = _tn(a_ref[...], b_ref[...].astype(BF16))

    return pl.pallas_call(
        body, name=name, out_shape=_sds((DEPTH, D, wm), F32), grid=(DEPTH, D // tm),
        in_specs=[pl.BlockSpec((32, tm), lambda l, i: (0, i)), pl.BlockSpec((None, 32, wm), lambda l, i: (l, 0, 0))],
        out_specs=pl.BlockSpec((None, tm, wm), lambda l, i: (l, i, 0)),
        compiler_params=_cp(("parallel", "parallel")))(sc, dcol)


def cctx_dx(drow, w_mod, name):
    wm = w_mod.shape[-1]

    def body(a_ref, b_ref, o_ref):
        o_ref[...] = _nt(a_ref[...].astype(BF16), b_ref[...].astype(BF16))

    return pl.pallas_call(
        body, name=name, out_shape=_sds((DEPTH, 16, D), F32), grid=(DEPTH,),
        in_specs=[pl.BlockSpec((None, 16, wm), lambda l: (l, 0, 0)), pl.BlockSpec((None, D, wm), lambda l: (l, 0, 0))],
        out_specs=pl.BlockSpec((None, 16, D), lambda l: (l, 0, 0)), compiler_params=_cp(("parallel",), VMEM_BIG))(drow, w_mod)


HEAD_PERM = (0, 4, 1, 5, 2, 6, 3, 7)


def _rot_rows(wt):
    return jnp.concatenate([-wt[32:64], wt[0:32]], axis=0)


def _unrot_rows(g):
    return jnp.concatenate([g[32:64], -g[0:32]], axis=0)


def _heads(a, n):
    return [a[64 * i:64 * (i + 1)] for i in range(n)]


def kernel(x, c, ctx, c_ctx, w_mod, b_mod, ln_g, ln_b, ffn_w_gate, ffn_w_up, ffn_w_down, mix_ab_w_in, attn_sink, pool_w, pool_scale, mix_ab_w_out, lru_w_in, lru_conv_w, lru_conv_b, lru_wa, lru_ba, lru_wx, lru_bx, lru_lambda, lru_w_out, loss_target, m_c_ctx, m_w_mod, m_b_mod, m_ln_g, m_ln_b, m_ffn_w_gate, m_ffn_w_up, m_ffn_w_down, m_mix_ab_w_in, m_attn_sink, m_pool_w, m_pool_scale, m_mix_ab_w_out, m_lru_w_in, m_lru_conv_w, m_lru_conv_b, m_lru_wa, m_lru_ba, m_lru_wx, m_lru_bx, m_lru_lambda, m_lru_w_out, v_c_ctx, v_w_mod, v_b_mod, v_ln_g, v_ln_b, v_ffn_w_gate, v_ffn_w_up, v_ffn_w_down, v_mix_ab_w_in, v_attn_sink, v_pool_w, v_pool_scale, v_mix_ab_w_out, v_lru_w_in, v_lru_conv_w, v_lru_conv_b, v_lru_wa, v_lru_ba, v_lru_wx, v_lru_bx, v_lru_lambda, v_lru_w_out):
    n_lat, n_ctx = x.shape[1], ctx.shape[1]
    lay = Layout(n_ctx, n_lat)
    T = lay.T
    ns = ffn_w_gate.shape[-1]
    n_li, n_ai = lru_w_in.shape[-1], mix_ab_w_in.shape[-1]
    n_ao, n_lo = mix_ab_w_out.shape[1], lru_w_out.shape[1]
    wm = w_mod.shape[-1]
    dsh = ln_g.shape[-1]
    mx, my, mc = lax.axis_index("x"), lax.axis_index("y"), lax.axis_index("c")
    chip = 2 * mx + my
    me = 2 * chip + mc

    c_all = all_gather8(c, "ag8_c").reshape(16, D)
    cc = jnp.concatenate([c_all, c_ctx[None, :], jnp.zeros((15, D), F32)], axis=0)
    sc = silu_rows(cc, "silu_c")
    bias = lax.dynamic_slice(b_mod, (0, chip * wm), (DEPTH, wm)).reshape(DEPTH, 1, wm)
    modg = all_gather_chips(mod_mm(sc, w_mod, bias, "mod_mm"), "ag_mod")
    modtab = []
    for l in range(DEPTH):
        full = jnp.transpose(modg[:, l], (1, 0, 2)).reshape(32, N_CHIP * wm)
        mine = lax.dynamic_slice(full, (2 * me, 0), (2, N_CHIP * wm))
        modtab.append(jnp.concatenate([mine, full[16:17]], axis=0).reshape(3, N_MOD, D))

    small = jnp.concatenate([ln_g.reshape(6, dsh), ln_b.reshape(6, dsh), lru_conv_w[0], lru_conv_b, lru_ba[0],
                             lru_bx[0], lru_lambda[0], jnp.zeros((9, dsh), F32)], axis=0)
    small = all_gather_chips(small.reshape(2, 16, dsh), "ag_small").reshape(N_CHIP, 32, dsh)
    small = jnp.transpose(small, (1, 0, 2)).reshape(32, D)
    ln_g_f, ln_b_f = small[0:6].reshape(2, 3, D), small[6:12].reshape(2, 3, D)
    conv_w_f, conv_b_f = small[12:16], small[16:17]
    lru_vec = small[17:23]

    hh = 3 * ns // 2
    gate_t, up_t = jnp.swapaxes(ffn_w_gate, -1, -2), jnp.swapaxes(ffn_w_up, -1, -2)
    extra = [0, n_ai + n_ao, n_li + n_lo, 0]
    placed = [ffn_place(gate_t, up_t, ffn_w_down, g // 2, g % 2, f"ag_ffn{g}_place", extra[g]) for g in range(4)]
    placed[1] = place_rows(placed[1], jnp.concatenate([mix_ab_w_in[0].T, mix_ab_w_out[0]], axis=0).astype(BF16), 3 * ns,
                           "ag_mixa_place")
    placed[2] = place_rows(placed[2], jnp.concatenate([lru_w_in[0].T, lru_w_out[0]], axis=0).astype(BF16), 3 * ns,
                           "ag_mixc_place")
    placed = [p.reshape(N_CHIP, 2, p.shape[1] // 2, D) for p in placed]
    wb = [gather_placed(placed[0], "ag_ffn0"), None, None, None]
    mixw = {}

    def mixa_w():
        if "a" not in mixw:
            full = wb[1].reshape(N_CHIP, -1, D)
            ab_in_t = full[:, 3 * ns:3 * ns + n_ai].reshape(N_CHIP * n_ai, D)
            ab_out = full[:, 3 * ns + n_ai:].reshape(N_CHIP * n_ao, D)
            qh, kh = _heads(ab_in_t[Q0:K0], N_HEADS), _heads(ab_in_t[K0:V0], N_KV)
            w_ext_t = jnp.concatenate([qh[h] for h in HEAD_PERM] + [ab_in_t[K0:QR0]]
                                      + [_rot_rows(qh[h]) for h in HEAD_PERM] + [_rot_rows(t) for t in kh], axis=0)
            oh = _heads(ab_out[0:ATT_W], N_HEADS)
            mixw["a"] = (w_ext_t, jnp.concatenate([oh[h] for h in HEAD_PERM] + [ab_out[ATT_W:]], axis=0))
        return mixw["a"]

    def mixc_w():
        if "c" not in mixw:
            full = wb[2].reshape(N_CHIP, -1, D)
            mixw["c"] = (full[:, 3 * ns:3 * ns + n_li].reshape(N_CHIP * n_li, D),
                         full[:, 3 * ns + n_li:].reshape(N_CHIP * n_lo, D))
        return mixw["c"]

    t = jnp.arange(n_lat)
    inv = ROPE_THETA ** (-jnp.arange(16, dtype=F32) / 16.0)
    ang = jnp.concatenate([(t // GRID_W).astype(F32)[:, None] * inv, (t % GRID_W).astype(F32)[:, None] * inv], axis=-1)
    cos1 = jnp.concatenate([jnp.ones((n_ctx, 32), F32), jnp.cos(ang)], axis=0)
    sin1 = jnp.concatenate([jnp.zeros((n_ctx, 32), F32), jnp.sin(ang)], axis=0)
    cos_t = jnp.tile(cos1, (2, 4))
    sin_t = jnp.tile(sin1, (2, 4))
    sk = attn_sink[0]
    sink_tab = jnp.concatenate([jnp.repeat(jnp.stack([sk[:4], sk[4:]], axis=1), HEAD_DIM, axis=1),
                                jnp.zeros((4, 128), F32)], axis=0)
    pscale = pool_scale.reshape(1, POOL_W)

    h0 = jnp.concatenate([ctx, x], axis=1).reshape(T, D)
    tgt = loss_target.reshape(2 * n_lat, D)

    def lnv(l, j):
        return jnp.stack([ln_g_f[l, j], ln_b_f[l, j]])

    subs = [(0, 0, 0.5, 0), (0, 3, 1.0, 1), (0, 6, 0.5, 2), (1, 0, 0.5, 0), (1, 3, 1.0, 1), (1, 6, 0.5, 2)]

    def ffn_core(hm, l, f):
        tag = f"l{l}f{f}"
        gi = 2 * l + f
        w = wb[gi].reshape(N_CHIP, -1, D)
        if gi == 3:
            up, sl, a = ffn_up(lay, hm, w, 0, 1, ns, f"ffn_up_{tag}")
            (y,) = slab_nn_acc(lay, [a], w, [2], ns, f"ffn_down_{tag}")
            return y, dict(up=up, sl=sl, a=a, nbuf=None)
        up, sl, a, nbuf = ffn_up(lay, hm, w, 0, 1, ns, f"ffn_up_{tag}", rider=rider_gather_xy(placed[gi + 1]))
        y, nbuf = slab_nn_acc(lay, [a], w, [2], ns, f"ffn_down_{tag}", rider=rider_gather_fwd(nbuf))
        return y, dict(up=up, sl=sl, a=a, nbuf=nbuf)

    def mixa_core(hm):
        p = mm_nt(hm, mixa_w()[0], "mixa_in")
        qr, kr, vb, u = rope_fwd(lay, p, cos_t, sin_t, "rope")
        att, lse = attn_fwd(lay, qr, kr, vb, sink_tab, "attn")
        pool = pool_fwd(lay, u, pool_w[0], pscale, "pool")
        cat = jnp.concatenate([att, pool], axis=1)
        return mm_nn(cat, mixa_w()[1], "mixa_out"), dict(qr=qr, kr=kr, vb=vb, u=u, lse=lse, cat=cat)

    def mixc_core(hm):
        p = mm_nt(hm, mixc_w()[0], "mixc_in")
        uc = conv_fwd(lay, p, D, conv_w_f, conv_b_f, "conv")
        a, b = lru_coeffs(lay, uc, lru_wa[0], lru_wx[0], lru_vec, "lru_coef")
        s = lru_scan(lay, a, b, "lru_scan")
        o = lru_gate(lay, p, s, "lru_gate")
        return mm_nn(o, mixc_w()[1], "mixc_out"), dict(p=p, uc=uc, a=a, s=s, o=o)

    recs = []
    h = h0
    hm = modulate(lay, h0, modtab[0], 0, 1, "mod_first")
    for k, (l, k0, coef, j) in enumerate(subs):
        if k0 == 3:
            y, core = mixa_core(hm) if l == 0 else mixc_core(hm)
        else:
            y, core = ffn_core(hm, l, k0 // 6)
        nxt = None if k == 5 else (modtab[subs[k + 1][0]], subs[k + 1][1], subs[k + 1][1] + 1)
        nbuf = core.pop("nbuf", None)
        res = resid_ln(lay, h, y, modtab[l], k0 + 2, coef, lnv(l, j), f"ln_s{k}", nxt=nxt,
                       rider=None if nbuf is None else rider_gather_d2d(nbuf))
        if nbuf is not None:
            wb[2 * l + k0 // 6 + 1] = res[-1]
        recs.append(dict(h=h, hm=hm, y=y, xhat=res[1], rstd=res[2], **core))
        h = res[0]
        hm = res[3] if nxt is not None else None

    dout, lparts = loss_and_grad(lay, h, tgt, "loss")
    loss = lax.psum(jnp.sum(lparts), ("x", "y", "c"))

    dln = {}
    dms = {}
    mixg = {}
    ffn_red = [lax.empty((4, 2, hh, D), F32)]
    mix_red = {}
    pending = []

    def rs_sib(p):
        return None if p is None else rider_reduce_sib(p["buf"])

    def rs_add2(p, recv):
        p["q"] = add_own_half(p["buf"], recv, BF16, f"rs_add2_{p['key']}")

    def rs_join(p, arr):
        if isinstance(p["key"], int):
            return rider_join(sum_slots(p["q"], arr, f"rs_add4_{p['key']}", dst=ffn_red[0], g=p["key"]), p["key"])
        return rider_join(sum_slots(p["q"], arr, f"rs_add4_{p['key']}"))

    def rs_done(p, joined):
        if isinstance(p["key"], int):
            ffn_red[0] = joined
        else:
            mix_red[p["key"]] = joined.reshape(-1, D)

    def ffn_core_bwd(dy, r, l, f):
        tag = f"l{l}f{f}"
        gi = 2 * l + f
        w = wb[gi].reshape(N_CHIP, -1, D)
        p = pending.pop() if pending else None
        gb = lax.empty((N_CHIP, 3 * ns, D), F32)
        if p is None:
            dg, du = ffn_bwd_da(lay, dy, w, 2, r["up"], r["sl"], ns, f"ffn_da_{tag}")
            (gb,) = slab_tn(lay, r["a"], dy, gb, 2, ns, f"ffn_dwd_{tag}")
            (gb,) = slab_tn(lay, dg, r["hm"], gb, 0, ns, f"ffn_dwg_{tag}")
            (gb,) = slab_tn(lay, du, r["hm"], gb, 1, ns, f"ffn_dwu_{tag}")
            (dhm,) = slab_nn_acc(lay, [dg, du], w, [0, 1], ns, f"ffn_dh_{tag}")
        else:
            dg, du, recv = ffn_bwd_da(lay, dy, w, 2, r["up"], r["sl"], ns, f"ffn_da_{tag}", rider=rs_sib(p))
            rs_add2(p, recv)
            gb, arr = slab_tn(lay, r["a"], dy, gb, 2, ns, f"ffn_dwd_{tag}", rider=rider_reduce_copy(p["q"], 0))
            gb, arr = slab_tn(lay, dg, r["hm"], gb, 0, ns, f"ffn_dwg_{tag}", rider=rider_reduce_copy(p["q"], 1, arr))
            gb, arr = slab_tn(lay, du, r["hm"], gb, 1, ns, f"ffn_dwu_{tag}", rider=rider_reduce_copy(p["q"], 2, arr))
            dhm, joined = slab_nn_acc(lay, [dg, du], w, [0, 1], ns, f"ffn_dh_{tag}", rider=rs_join(p, arr))
            rs_done(p, joined)
        pending.append(dict(buf=gb.reshape(N_CHIP, 2, hh, D), key=gi))
        return dhm

    def mixc_core_bwd(dy, r):
        p = pending.pop() if pending else None
        w_in_t, w_out = mixc_w()
        if p is None:
            do_c = mm_nt(dy, w_out, "mixc_out_dx")
        else:
            do_c, recv = mm_nt(dy, w_out, "mixc_out_dx", rider=rs_sib(p))
            rs_add2(p, recv)
        g_out = mm_tn(r["o"], dy, "mixc_out_dw")
        dgate, dyg = lru_gate_bwd(lay, r["p"], r["s"], do_c, "lru_gate_b")
        da_c, db_c = lru_scan_bwd(lay, r["a"], r["s"], dyg, "lru_scan_b")
        res = lru_coeffs_bwd(lay, r["uc"], lru_wa[0], lru_wx[0], lru_vec, da_c, db_c, "lru_coef_b",
                             rider=None if p is None else rider_reduce_copies(p["q"]))
        duc, mixg["wa"], mixg["wx"], mixg["vec"] = res[:4]
        du_c, mixg["cw"], mixg["cb"] = conv_bwd(lay, r["p"], D, conv_w_f, duc, "conv_b")
        dp_c = jnp.concatenate([dgate, du_c], axis=1)
        if p is None:
            g_in_t = mm_tn(dp_c, r["hm"], "mixc_in_dw")
        else:
            g_in_t, joined = mm_tn(dp_c, r["hm"], "mixc_in_dw", rider=rs_join(p, res[4]))
            rs_done(p, joined)
        buf = jnp.concatenate([g_in_t.reshape(N_CHIP, n_li, D), g_out.reshape(N_CHIP, n_lo, D)], axis=1)
        pending.append(dict(buf=buf.reshape(N_CHIP, 2, (n_li + n_lo) // 2, D), key="c"))
        return mm_nn(dp_c, w_in_t, "mixc_in_dx")

    def mixa_core_bwd(dy, r):
        p = pending.pop() if pending else None
        w_ext_t, w_out_ext = mixa_w()
        if p is None:
            dcat = mm_nt(dy, w_out_ext, "mixa_out_dx")
        else:
            dcat, recv = mm_nt(dy, w_out_ext, "mixa_out_dx", rider=rs_sib(p))
            rs_add2(p, recv)
        g_out_ext = mm_tn(r["cat"], dy, "mixa_out_dw")
        res = attn_bwd(lay, r["qr"], r["kr"], r["vb"], sink_tab, r["lse"], dcat, "attn_b",
                       rider=None if p is None else rider_reduce_copies(p["q"]))
        dqr, dkr, dv, mixg["sink"] = res[:4]
        du_a, mixg["pw"], mixg["ps"] = pool_bwd(lay, r["u"], dcat, pool_w[0], pscale, "pool_b")
        dp_a = rope_bwd(lay, dqr, dkr, dv, du_a, cos_t, sin_t, "rope_b")
        if p is None:
            g_ext_t = mm_tn(dp_a, r["hm"], "mixa_in_dw")
        else:
            g_ext_t, joined = mm_tn(dp_a, r["hm"], "mixa_in_dw", rider=rs_join(p, res[4]))
            rs_done(p, joined)
        gq, gqr = _heads(g_ext_t[Q0:K0], N_HEADS), _heads(g_ext_t[QR0:KR0], N_HEADS)
        g_q = [None] * N_HEADS
        for i, h in enumerate(HEAD_PERM):
            g_q[h] = gq[i] + _unrot_rows(gqr[i])
        gk = [a + _unrot_rows(b) for a, b in zip(_heads(g_ext_t[K0:V0], N_KV), _heads(g_ext_t[KR0:PEXT], N_KV))]
        g_ab_in_t = jnp.concatenate(g_q + gk + [g_ext_t[V0:QR0]], axis=0)
        go = _heads(g_out_ext[0:ATT_W], N_HEADS)
        g_o = [None] * N_HEADS
        for i, h in enumerate(HEAD_PERM):
            g_o[h] = go[i]
        g_ab_out = jnp.concatenate(g_o + [g_out_ext[ATT_W:]], axis=0)
        buf = jnp.concatenate([g_ab_in_t.reshape(N_CHIP, n_ai, D), g_ab_out.reshape(N_CHIP, n_ao, D)], axis=1)
        pending.append(dict(buf=buf.reshape(N_CHIP, 2, (n_ai + n_ao) // 2, D), key="a"))
        return mm_nn(dp_a, w_ext_t, "mixa_in_dx")

    l, k0, coef, j = subs[5]
    dy, dres, s1 = ln_bwd(lay, dout, recs[5]["xhat"], recs[5]["rstd"], recs[5]["y"], modtab[l], k0 + 2, coef, lnv(l, j),
                          "lnb_s5")
    for k in range(5, -1, -1):
        l, k0, coef, j = subs[k]
        r = recs[k]
        if k0 == 3:
            dhm = mixa_core_bwd(dy, r) if l == 0 else mixc_core_bwd(dy, r)
        else:
            dhm = ffn_core_bwd(dy, r, l, k0 // 6)
        dln[(l, j)] = block_sums(lay, s1, f"bs_ln_s{k}")
        if k > 0:
            lp, k0p, coefp, jp = subs[k - 1]
            rp = recs[k - 1]
            dy, dres, s1, s2 = modb_lnb(lay, dres, dhm, r["h"], modtab[l], k0 + 1, rp["xhat"], rp["rstd"], rp["y"],
                                        modtab[lp], k0p + 2, coefp, lnv(lp, jp), f"modb_lnb_s{k}")
        else:
            gx, s2 = mod_bwd(lay, dres, dhm, r["h"], modtab[l], k0 + 1, "modb_s0")
        dms[(l, k0)] = block_sums(lay, s2, f"bs_mod_s{k}")
    grad_x = gx.reshape(2, n_lat, D)
    g_wa, g_wx, g_vec, g_cw, g_cb, g_sink, g_pw, g_ps = (mixg[n] for n in ("wa", "wx", "vec", "cw", "cb", "sink", "pw", "ps"))

    rows = []
    for l in range(DEPTH):
        per_k = []
        for k0, j in ((0, 0), (3, 1), (6, 2)):
            per_k += [dms[(l, k0)][:3, 0], dms[(l, k0)][:3, 1], dln[(l, j)][:3, 2]]
        rows.append(jnp.stack(per_k, axis=1).reshape(3, N_MOD * D))
    dmod_loc = jnp.concatenate(rows + [jnp.zeros((2, N_MOD * D), F32)], axis=0)
    dmod_all, g_b_mod = mod_grad_rows(all_gather8(dmod_loc, "ag8_dmod"), "dmod_rows")
    dcol = lax.dynamic_slice(dmod_all, (0, 0, chip * wm), (DEPTH, 32, wm))
    g_w_mod = wmod_dw(sc, dcol, "wmod_dw")
    g_cctx = cctx_grad(cctx_dx(dcol[:, 16:32], w_mod, "cctx_dx"), c_ctx[None, :], "cctx_grad")

    g_ln_g =jnp.stack([jnp.stack([dln[(l, j)][3, 1] for j in range(3)]) for l in range(DEPTH)])
    g_ln_b = jnp.stack([jnp.stack([dln[(l, j)][3, 0] for j in range(3)]) for l in range(DEPTH)])
    sink_row = jnp.sum(g_sink, axis=0)[:4]
    g_sink8 = jnp.concatenate([sink_row[:, 0], sink_row[:, HEAD_DIM]])
    misc = jnp.concatenate([g_sink8, jnp.sum(g_ps, axis=0).reshape(POOL_W), jnp.zeros((D - 8 - POOL_W,), F32)])
    small_g = jnp.concatenate([
        g_ln_g.reshape(6, D), g_ln_b.reshape(6, D), jnp.sum(g_cw, axis=0), jnp.sum(g_cb, axis=0), g_vec,
        misc[None, :], jnp.sum(g_pw, axis=0).reshape(64, D), g_wa.reshape(256, D), g_wx.reshape(256, D), g_cctx,
        jnp.zeros((39, D), F32)], axis=0)
    n_small = small_g.shape[0] // N_CHIP
    last = pending.pop()
    ffn_red = reduce_scatter_chips(last["buf"], f"ffn{last['key']}", wire=BF16, dst=ffn_red[0],
                                   g=last["key"]).reshape(12 * ns, D)
    small_red = reduce_scatter_chips(small_g.reshape(N_CHIP, 2, n_small // 2, D), "small")
    small_red = all_gather_chips(small_red, "ag_smallg").reshape(N_CHIP * n_small, D)

    ffn_kind = dict(ffn_w_gate=0, ffn_w_up=1, ffn_w_down=2)

    def cols(a):
        return lax.dynamic_slice_in_dim(a, chip * dsh, dsh, axis=a.ndim - 1)

    sr = small_red
    grads = dict(
        c_ctx=sr[600], w_mod=g_w_mod, b_mod=g_b_mod,
        ln_g=cols(sr[0:6]).reshape(2, 3, dsh), ln_b=cols(sr[6:12]).reshape(2, 3, dsh),
        mix_ab_w_in=mix_red["a"][0:n_ai][None], attn_sink=sr[23, 0:8][None], pool_w=sr[24:88].reshape(1, 4, 128, 128),
        pool_scale=sr[23, 8:8 + POOL_W][None], mix_ab_w_out=mix_red["a"][n_ai:][None],
        lru_w_in=mix_red["c"][0:n_li].T[None],
        lru_conv_w=cols(sr[12:16])[None], lru_conv_b=cols(sr[16:17]), lru_wa=sr[88:344].reshape(1, 2, 8, 128, 128),
        lru_ba=cols(sr[17:19])[None], lru_wx=sr[344:600].reshape(1, 2, 8, 128, 128), lru_bx=cols(sr[19:21])[None],
        lru_lambda=cols(sr[21:23])[None], lru_w_out=mix_red["c"][n_li:][None])
    params = dict(c_ctx=(c_ctx, m_c_ctx, v_c_ctx), w_mod=(w_mod, m_w_mod, v_w_mod), b_mod=(b_mod, m_b_mod, v_b_mod),
                  ln_g=(ln_g, m_ln_g, v_ln_g), ln_b=(ln_b, m_ln_b, v_ln_b),
                  ffn_w_gate=(ffn_w_gate, m_ffn_w_gate, v_ffn_w_gate), ffn_w_up=(ffn_w_up, m_ffn_w_up, v_ffn_w_up),
                  ffn_w_down=(ffn_w_down, m_ffn_w_down, v_ffn_w_down),
                  mix_ab_w_in=(mix_ab_w_in, m_mix_ab_w_in, v_mix_ab_w_in), attn_sink=(attn_sink, m_attn_sink, v_attn_sink),
                  pool_w=(pool_w, m_pool_w, v_pool_w), pool_scale=(pool_scale, m_pool_scale, v_pool_scale),
                  mix_ab_w_out=(mix_ab_w_out, m_mix_ab_w_out, v_mix_ab_w_out), lru_w_in=(lru_w_in, m_lru_w_in, v_lru_w_in),
                  lru_conv_w=(lru_conv_w, m_lru_conv_w, v_lru_conv_w), lru_conv_b=(lru_conv_b, m_lru_conv_b, v_lru_conv_b),
                  lru_wa=(lru_wa, m_lru_wa, v_lru_wa), lru_ba=(lru_ba, m_lru_ba, v_lru_ba), lru_wx=(lru_wx, m_lru_wx, v_lru_wx),
                  lru_bx=(lru_bx, m_lru_bx, v_lru_bx), lru_lambda=(lru_lambda, m_lru_lambda, v_lru_lambda),
                  lru_w_out=(lru_w_out, m_lru_w_out, v_lru_w_out))
    gl, dl, ml, vl = [], [], [], []
    transposed = ("ffn_w_gate", "ffn_w_up", "mix_ab_w_in")
    for name, (w, m, v) in params.items():
        if name in transposed:
            w, m, v = (jnp.swapaxes(t, -1, -2) for t in (w, m, v))
        if name in ffn_kind:
            g, d, mn, vn = adamw_ffn(w, m, v, ffn_red, ffn_kind[name], ns, f"adamw_{name}")
        else:
            g = grads[name].reshape(w.shape)
            d, mn, vn = adamw(w, g, m, v, f"adamw_{name}")
        if name in transposed:
            g, d, mn, vn = (jnp.swapaxes(t, -1, -2) for t in (g, d, mn, vn))
        gl.append(g)
        dl.append(d)
        ml.append(mn)
        vl.append(vn)
    return (loss, grad_x, *gl, *dl, *ml, *vl)
```

```python
import functools
import math

import jax
import jax.numpy as jnp
from jax import lax
from jax.experimental import pallas as pl
from jax.experimental.pallas import tpu as pltpu

F32, BF16 = jnp.float32, jnp.bfloat16
MESH = pl.DeviceIdType.MESH
ANY = pl.BlockSpec(memory_space=pl.ANY)
VMEM_SPEC = pl.BlockSpec(memory_space=pltpu.VMEM)

D = 1024
N_CHIP = 4
HEAD_DIM, N_HEADS, N_KV = 64, 8, 2
ATT_W, KV_W, POOL_W = 512, 128, 512
POOL_WINDOWS = (2, 4, 8, 16)
BLK = 128
ATT_SCALE = HEAD_DIM ** -0.5
ROPE_THETA = 10000.0
GRID_W = 64
LRU_C = 8.0
LN_EPS = 1e-5
NEG_INF = -1e30
DEPTH = 2
ALPHA = (2 * DEPTH) ** 0.25
N_MOD = 9
ADAM_LR, ADAM_B1, ADAM_B2, ADAM_EPS, ADAM_WD, ADAM_STEP = 0.001, 0.9, 0.999, 1e-08, 0.01, 10
VMEM_BIG = 48 * 1024 * 1024


def _cp(sem=None, vmem=None):
    kw = {}
    if sem is not None:
        kw["dimension_semantics"] = sem
    if vmem is not None:
        kw["vmem_limit_bytes"] = vmem
    return pltpu.CompilerParams(**kw)


def _sds(shape, dtype):
    return jax.ShapeDtypeStruct(tuple(shape), dtype)


def _pick(n, cands):
    for c in cands:
        if n % c == 0:
            return c
    return n


def _dot(a, b, dims):
    return lax.dot_general(a, b, (dims, ((), ())), preferred_element_type=F32)


def _nn(a, b):
    return _dot(a, b, ((1,), (0,)))


def _nt(a, b):
    return _dot(a, b, ((1,), (1,)))


def _tn(a, b):
    return _dot(a, b, ((0,), (0,)))


def _sigmoid(x):
    return 0.5 * jnp.tanh(0.5 * x) + 0.5


def _me():
    return lax.axis_index("x"), lax.axis_index("y"), lax.axis_index("c")


def _rcopy(src, dst, ssem, rsem, dev):
    return pltpu.make_async_remote_copy(src_ref=src, dst_ref=dst, send_sem=ssem, recv_sem=rsem,
                                        device_id=dev, device_id_type=MESH)


def all_gather8(x, name):
    def body(x_ref, o_ref, ssem, rsem, lsem):
        mx, my, mc = _me()
        me = 4 * mx + 2 * my + mc
        loc = pltpu.make_async_copy(x_ref, o_ref.at[me], lsem)
        loc.start()
        peers = []
        for m in range(1, 8):
            px = 1 - mx if (m >> 2) & 1 else mx
            py = 1 - my if (m >> 1) & 1 else my
            pc = 1 - mc if m & 1 else mc
            peers.append((px, py, pc))
        sends = [_rcopy(x_ref, o_ref.at[me], ssem.at[k], rsem.at[k], p) for k, p in enumerate(peers)]
        for cp in sends:
            cp.start()
        for k, (px, py, pc) in enumerate(peers):
            _rcopy(x_ref, o_ref.at[4 * px + 2 * py + pc], ssem.at[k], rsem.at[k], (px, py, pc)).wait_recv()
        for cp in sends:
            cp.wait_send()
        loc.wait()

    return pl.pallas_call(
        body, name=name, out_shape=_sds((8,) + x.shape, x.dtype),
        in_specs=[VMEM_SPEC], out_specs=VMEM_SPEC,
        scratch_shapes=[pltpu.SemaphoreType.DMA((7,)), pltpu.SemaphoreType.DMA((7,)), pltpu.SemaphoreType.DMA],
    )(x)


_ROW_BLOCKS = (512, 384, 352, 256, 224, 128)


def _idx(v):
    return jnp.reshape(v, (1,)).astype(jnp.int32)


def place_slab(shard, name):
    _, h, w = shard.shape
    th = _pick(h, _ROW_BLOCKS)

    def body(s_ref, x_ref, o_ref):
        del s_ref
        o_ref[...] = x_ref[...]

    return pl.pallas_call(
        body, name=name, out_shape=_sds((N_CHIP,) + shard.shape, shard.dtype),
        grid_spec=pltpu.PrefetchScalarGridSpec(
            num_scalar_prefetch=1, grid=(2, h // th),
            in_specs=[pl.BlockSpec((None, th, w), lambda k, r, s: (k, r, 0))],
            out_specs=pl.BlockSpec((None, None, th, w), lambda k, r, s: (s[0], k, r, 0))),
    )(_idx(2 * lax.axis_index("x") + lax.axis_index("y")), shard)


def place_rows(buf, rows, r0, name):
    e, w = rows.shape
    tb = 64

    def body(s_ref, x_ref, b_ref, o_ref):
        del s_ref, b_ref
        o_ref[...] = x_ref[...]

    return pl.pallas_call(
        body, name=name, out_shape=_sds(buf.shape, buf.dtype),
        grid_spec=pltpu.PrefetchScalarGridSpec(
            num_scalar_prefetch=1, grid=(e // tb,),
            in_specs=[pl.BlockSpec((tb, w), lambda j, s: (j, 0)), ANY],
            out_specs=pl.BlockSpec((None, tb, w), lambda j, s: (s[0], r0 // tb + j, 0))),
        input_output_aliases={2: 0},
    )(_idx(2 * lax.axis_index("x") + lax.axis_index("y")), rows, buf)


def ffn_place(w_gate_t, w_up_t, w_down, l, f, name, extra=0):
    ns = w_down.shape[-2]
    tr, nb = ns // 2, 2

    def body(s_ref, g_ref, u_ref, d_ref, o_ref):
        del s_ref
        k = pl.program_id(0)

        @pl.when(k == 0)
        def _():
            o_ref[...] = g_ref[...].astype(BF16)

        @pl.when(k == 1)
        def _():
            o_ref[...] = u_ref[...].astype(BF16)

        @pl.when(k == 2)
        def _():
            o_ref[...] = d_ref[...].astype(BF16)

    def spec(q):
        return pl.BlockSpec((None, None, tr, D), lambda k, j, s: (l, f, jnp.where(k == q, j, 0), 0))

    return pl.pallas_call(
        body, name=name, out_shape=_sds((N_CHIP, 3 * ns + extra, D), BF16),
        grid_spec=pltpu.PrefetchScalarGridSpec(
            num_scalar_prefetch=1, grid=(3, nb), in_specs=[spec(0), spec(1), spec(2)],
            out_specs=pl.BlockSpec((None, tr, D), lambda k, j, s: (s[0], k * nb + j, 0))),
    )(_idx(2 * lax.axis_index("x") + lax.axis_index("y")), w_gate_t, w_up_t, w_down)


def all_gather_chips(shard, name):
    return gather_placed(place_slab(shard, name + "_place"), name)


def gather_placed(full, name):
    h = full.shape[2]
    lo, hi = pl.ds(0, h // 2), pl.ds(h // 2, h - h // 2)

    def body(x_ref, o_ref, ssem, rsem):
        del x_ref
        mx, my, mc = _me()
        s, xs, ys, ds = 2 * mx + my, 2 * (1 - mx) + my, 2 * mx + (1 - my), 2 * (1 - mx) + (1 - my)
        xn, yn, sib = (1 - mx, my, mc), (mx, 1 - my, mc), (mx, my, 1 - mc)

        def cp(k, src, dst, dev):
            return _rcopy(src, dst, ssem.at[k], rsem.at[k], dev)

        own = o_ref.at[s, mc]
        sent = [cp(0, own, own, xn), cp(1, own, own, yn)]
        for c in sent:
            c.start()
        cp(0, own, o_ref.at[xs, mc], xn).wait_recv()
        sent += [cp(2, o_ref.at[xs, mc, lo], o_ref.at[xs, mc, lo], yn), cp(4, o_ref.at[xs, mc], o_ref.at[xs, mc], sib)]
        sent[-2].start()
        sent[-1].start()
        cp(1, own, o_ref.at[ys, mc], yn).wait_recv()
        sent += [cp(3, o_ref.at[ys, mc, hi], o_ref.at[ys, mc, hi], xn), cp(5, o_ref.at[ys, mc], o_ref.at[ys, mc], sib)]
        sent[-2].start()
        sent[-1].start()
        cp(2, own, o_ref.at[ds, mc, lo], yn).wait_recv()
        cp(3, own, o_ref.at[ds, mc, hi], xn).wait_recv()
        sent.append(cp(6, o_ref.at[ds, mc], o_ref.at[ds, mc], sib))
        sent[-1].start()
        for k, slot in ((4, xs), (5, ys), (6, ds)):
            cp(k, own, o_ref.at[slot, 1 - mc], sib).wait_recv()
        for c in sent:
            c.wait_send()

    return pl.pallas_call(
        body, name=name, out_shape=_sds(full.shape, full.dtype), in_specs=[ANY], out_specs=ANY,
        input_output_aliases={0: 0},
        scratch_shapes=[pltpu.SemaphoreType.DMA((7,)), pltpu.SemaphoreType.DMA((7,))],
    )(full)


def sibling_send_other_half(buf, name):
    def body(x_ref, o_ref, ssem, rsem):
        mx, my, mc = _me()
        sib = (mx, my, 1 - mc)
        cps = [_rcopy(x_ref.at[k, 1 - mc], o_ref.at[k], ssem.at[k], rsem.at[k], sib) for k in range(N_CHIP)]
        for cp in cps:
            cp.start()
        for cp in cps:
            cp.wait_recv()
        for cp in cps:
            cp.wait_send()

    n, _, h, w = buf.shape
    return pl.pallas_call(
        body, name=name, out_shape=_sds((n, h, w), buf.dtype), in_specs=[ANY], out_specs=ANY,
        scratch_shapes=[pltpu.SemaphoreType.DMA((N_CHIP,)), pltpu.SemaphoreType.DMA((N_CHIP,))],
    )(buf)


def chips_all_to_all(q, name):
    def body(x_ref, o_ref, ssem, rsem):
        mx, my, mc = _me()
        s = 2 * mx + my
        chips = [(1 - mx, my), (mx, 1 - my), (1 - mx, 1 - my)]
        cps = [_rcopy(x_ref.at[2 * px + py], o_ref.at[s], ssem.at[j], rsem.at[j], (px, py, mc))
               for j, (px, py) in enumerate(chips)]
        for cp in cps:
            cp.start()
        for j, (px, py) in enumerate(chips):
            ps = 2 * px + py
            _rcopy(x_ref.at[ps], o_ref.at[ps], ssem.at[j], rsem.at[j], (px, py, mc)).wait_recv()
        for cp in cps:
            cp.wait_send()

    return pl.pallas_call(
        body, name=name, out_shape=_sds(q.shape, q.dtype), in_specs=[ANY], out_specs=ANY,
        scratch_shapes=[pltpu.SemaphoreType.DMA((3,)), pltpu.SemaphoreType.DMA((3,))],
    )(q)


def sibling_join_halves(both, name, g=None):
    def body(x_ref, o_ref, ssem, rsem):
        del x_ref
        mx, my, mc = _me()
        sib = (mx, my, 1 - mc)
        o = o_ref if g is None else o_ref.at[g]
        cp = _rcopy(o.at[mc], o.at[mc], ssem, rsem, sib)
        cp.start()
        _rcopy(o.at[1 - mc], o.at[1 - mc], ssem, rsem, sib).wait_recv()
        cp.wait_send()

    return pl.pallas_call(
        body, name=name, out_shape=_sds(both.shape, both.dtype), in_specs=[ANY], out_specs=ANY,
        input_output_aliases={0: 0}, scratch_shapes=[pltpu.SemaphoreType.DMA, pltpu.SemaphoreType.DMA],
    )(both)


def add_own_half(buf, recv, wire, name):
    n, _, h, w = buf.shape
    th = _pick(h, _ROW_BLOCKS)

    def body(c_ref, a_ref, b_ref, o_ref):
        del c_ref
        o_ref[...] = (a_ref[...] + b_ref[...]).astype(o_ref.dtype)

    return pl.pallas_call(
        body, name=name, out_shape=_sds((n, h, w), wire),
        grid_spec=pltpu.PrefetchScalarGridSpec(
            num_scalar_prefetch=1, grid=(n, h // th),
            in_specs=[pl.BlockSpec((None, None, th, w), lambda k, r, c: (k, c[0], r, 0)),
                      pl.BlockSpec((None, th, w), lambda k, r, c: (k, r, 0))],
            out_specs=pl.BlockSpec((None, th, w), lambda k, r, c: (k, r, 0))),
    )(_idx(lax.axis_index("c")), buf, recv)


def sum_slots(q, r, name, dst=None, g=None):
    n, h, w = r.shape
    th = _pick(h, _ROW_BLOCKS)

    def body(i_ref, q_ref, r1, r2, r3, *rest):
        del i_ref
        rest[-1][...] = ((q_ref[...].astype(F32) + r1[...].astype(F32)) + r2[...].astype(F32)) + r3[...].astype(F32)

    def slot(d):
        return lambda i, ix: ((ix[0] + d) % N_CHIP, i, 0)

    idx = jnp.stack([2 * lax.axis_index("x") + lax.axis_index("y"), lax.axis_index("c")]).astype(jnp.int32)
    in_specs = [pl.BlockSpec((None, th, w), slot(d)) for d in (0, 1, 2, 3)]
    if dst is None:
        return pl.pallas_call(
            body, name=name, out_shape=_sds((2, h, w), F32),
            grid_spec=pltpu.PrefetchScalarGridSpec(
                num_scalar_prefetch=1, grid=(h // th,), in_specs=in_specs,
                out_specs=pl.BlockSpec((None, th, w), lambda i, ix: (ix[1], i, 0))),
        )(idx, q, r, r, r)
    return pl.pallas_call(
        body, name=name, out_shape=_sds(dst.shape, F32),
        grid_spec=pltpu.PrefetchScalarGridSpec(
            num_scalar_prefetch=1, grid=(h // th,), in_specs=in_specs + [ANY],
            out_specs=pl.BlockSpec((None, None, th, w), lambda i, ix: (g, ix[1], i, 0))),
        input_output_aliases={5: 0},
    )(idx, q, r, r, r, dst)


def reduce_scatter_chips(buf, tag, wire=F32, dst=None, g=None, recv=None):
    if recv is None:
        recv = sibling_send_other_half(buf, f"rs_sib_{tag}")
    q = add_own_half(buf, recv, wire, f"rs_add2_{tag}")
    r = chips_all_to_all(q, f"rs_a2a_{tag}")
    red = sum_slots(q, r, f"rs_add4_{tag}", dst=dst, g=g)
    return sibling_join_halves(red, f"rs_join_{tag}", g=g)


class Layout:
    def __init__(self, n_ctx, n_lat):
        self.C, self.L = n_ctx, n_lat
        self.PS = n_ctx + n_lat
        self.T = 2 * self.PS
        self.tr = _pick(math.gcd(n_ctx, n_lat), (256, 128))
        self.bps = self.PS // self.tr
        self.cb = n_ctx // self.tr
        self.nblk = self.T // self.tr
        self.tm = _pick(self.T, (1152, 768, 512, 256, 128))
        self.tm2 = _pick(self.T, (2304, 1152, 768, 512, 256, 128))
        self.tc = _pick(self.T, (512, 256, 128))

    def seg(self, i):
        return jnp.where(i % self.bps < self.cb, 2, i // self.bps)


def rowwise(lay, name, fn, rows, segs=(), vecs=(), outs=(), sums=(), rider=None):
    tr, nblk = lay.tr, lay.nblk
    n_r, n_s, n_v, n_o = len(rows), len(segs), len(vecs), len(outs)
    lat_only = any(o[2:] for o in outs) or any(a.shape[0] != lay.T for a in rows)
    nsub = 1 if lat_only or nblk % 2 else 2
    tb = tr * nsub

    def body(*refs):
        ins = refs[:n_r + n_s + n_v]
        ors = refs[n_r + n_s + n_v:]
        for sub in range(nsub):
            rs = slice(sub * tr, (sub + 1) * tr)
            seg = lay.seg(pl.program_id(0) * nsub + sub)
            vals = [r[rs, :] for r in ins[:n_r]] + [r[seg] for r in ins[n_r:n_r + n_s]] + [r[...] for r in ins[n_r + n_s:]]
            res = fn(*vals)
            for k in range(n_o):
                ors[k][rs, :] = res[k].astype(ors[k].dtype)
            for k in range(len(sums)):
                ors[n_o + k][sub] = res[n_o + k]

    def all_rows(i):
        return (i, 0)

    def lat_rows(i):
        return ((i // lay.bps) * (lay.bps - lay.cb) + jnp.maximum(i % lay.bps - lay.cb, 0), 0)

    in_specs = [pl.BlockSpec((tb, a.shape[1]), all_rows if a.shape[0] == lay.T else lat_rows) for a in rows]
    in_specs += [pl.BlockSpec(a.shape, lambda i: (0, 0, 0)) for a in segs]
    in_specs += [pl.BlockSpec(a.shape, lambda i: (0, 0)) for a in vecs]
    out_shape = [_sds((2 * lay.L if o[2:] else lay.T, o[0]), o[1]) for o in outs]
    out_shape += [_sds((nblk, r, w), F32) for r, w in sums]
    out_specs = [pl.BlockSpec((tb, o[0]), lat_rows if o[2:] else all_rows) for o in outs]
    out_specs += [pl.BlockSpec((nsub, r, w), lambda i: (i, 0, 0)) for r, w in sums]
    sem = "arbitrary" if any(o[2:] for o in outs) else "parallel"
    if rider is None:
        return pl.pallas_call(body, name=name, out_shape=out_shape, grid=(nblk // nsub,), in_specs=in_specs,
                              out_specs=out_specs, compiler_params=_cp((sem,), VMEM_BIG))(*rows, *segs, *vecs)
    return _host_call(body, rider, name, (nblk // nsub,), in_specs, out_specs, out_shape, (*rows, *segs, *vecs), (sem,),
                      n_r + n_s + n_v, n_o + len(sums))


def modulate(lay, h, mod, k_shift, k_scale, name):
    def fn(hb, m):
        return (hb * (1.0 + m[k_scale:k_scale + 1]) + m[k_shift:k_shift + 1],)
    return rowwise(lay, name, fn, [h], segs=[mod], outs=[(D, BF16)])[0]


def resid_ln(lay, h, y, mod, k_gate, coef, lnv, name, nxt=None, rider=None):
    def fn(hb, yb, m, *rest):
        ln = rest[-1]
        z = ALPHA * hb + (coef * m[k_gate:k_gate + 1]) * yb
        mu = jnp.mean(z, axis=-1, keepdims=True)
        zc = z - mu
        var = jnp.mean(zc * zc, axis=-1, keepdims=True)
        rstd = lax.rsqrt(var + LN_EPS)
        xhat = zc * rstd
        out = xhat * ln[0:1] + ln[1:2]
        if nxt is None:
            return out, xhat, rstd
        mn = rest[0]
        return out, xhat, rstd, out * (1.0 + mn[nxt[2]:nxt[2] + 1]) + mn[nxt[1]:nxt[1] + 1]
    segs = [mod] if nxt is None else [mod, nxt[0]]
    outs = [(D, F32), (D, F32), (1, F32)] + ([] if nxt is None else [(D, BF16)])
    return rowwise(lay, name, fn, [h, y], segs=segs, vecs=[lnv], outs=outs, rider=rider)


def _ln_bwd_math(do, xh, rs, yb, gate, coef, ln):
    dxh = do * ln[0:1]
    m1 = jnp.mean(dxh, axis=-1, keepdims=True)
    m2 = jnp.mean(dxh * xh, axis=-1, keepdims=True)
    dz = rs * (dxh - m1 - xh * m2)
    s = jnp.concatenate([jnp.sum(do, axis=0, keepdims=True), jnp.sum(do * xh, axis=0, keepdims=True),
                         jnp.sum(coef * dz * yb, axis=0, keepdims=True)], axis=0)
    return (coef * gate) * dz, ALPHA * dz, s


def _mod_bwd_math(dr, dm, hb, scale):
    s = jnp.concatenate([jnp.sum(dm, axis=0, keepdims=True), jnp.sum(dm * hb, axis=0, keepdims=True)], axis=0)
    return dr + dm * (1.0 + scale), s


def ln_bwd(lay, dout, xhat, rstd, y, mod, k_gate, coef, lnv, name):
    def fn(do, xh, rs, yb, m, ln):
        return _ln_bwd_math(do, xh, rs, yb, m[k_gate:k_gate + 1], coef, ln)
    return rowwise(lay, name, fn, [dout, xhat, rstd, y], segs=[mod], vecs=[lnv],
                   outs=[(D, BF16), (D, F32)], sums=[(3, D)])


def mod_bwd(lay, dres, dhm, h, mod, k_scale, name, rider=None):
    def fn(dr, dm, hb, m):
        return _mod_bwd_math(dr, dm, hb, m[k_scale:k_scale + 1])
    return rowwise(lay, name, fn, [dres, dhm, h], segs=[mod], outs=[(D, F32, "lat")], sums=[(2, D)], rider=rider)


def modb_lnb(lay, dres, dhm, h, mod, k_scale, xhat, rstd, y, mod_p, k_gate, coef, lnv, name, rider=None):
    def fn(dr, dm, hb, xh, rs, yb, m, mp, ln):
        dh, s2 = _mod_bwd_math(dr, dm, hb, m[k_scale:k_scale + 1])
        dy, dres_p, s1 = _ln_bwd_math(dh, xh, rs, yb, mp[k_gate:k_gate + 1], coef, ln)
        return dy, dres_p, s1, s2
    return rowwise(lay, name, fn, [dres, dhm, h, xhat, rstd, y], segs=[mod, mod_p], vecs=[lnv],
                   outs=[(D, BF16), (D, F32)], sums=[(3, D), (2, D)], rider=rider)


def block_sums(lay, parts_list, name):
    n = len(parts_list)

    def body(*refs):
        for p_ref, o_ref in zip(refs[:n], refs[n:]):
            acc = [None, None, None]
            for i in range(lay.nblk):
                sg = 2 if i % lay.bps < lay.cb else i // lay.bps
                acc[sg] = p_ref[i] if acc[sg] is None else acc[sg] + p_ref[i]
            for k in range(3):
                o_ref[k] = acc[k]
            o_ref[3] = (acc[0] + acc[1]) + acc[2]

    return pl.pallas_call(body, name=name, out_shape=[_sds((4,) + p.shape[1:], F32) for p in parts_list],
                          in_specs=[VMEM_SPEC] * n, out_specs=[VMEM_SPEC] * n)(*parts_list)


def mm_nn(a, b, name, out_dtype=F32, bias=None):
    m, k = a.shape
    n = b.shape[1]
    tm = _pick(m, (1152, 768, 512, 256, 128, 64, 32, 16, 8))
    tn = _pick(n, (1024, 768, 640, 512, 384, 256, 128))

    def body(*refs):
        if bias is None:
            a_ref, b_ref, o_ref = refs
            o_ref[...] = _nn(a_ref[...].astype(BF16), b_ref[...].astype(BF16)).astype(o_ref.dtype)
        else:
            a_ref, b_ref, c_ref, o_ref = refs
            o_ref[...] = (_nn(a_ref[...].astype(BF16), b_ref[...].astype(BF16)) + c_ref[...]).astype(o_ref.dtype)

    in_specs = [pl.BlockSpec((tm, k), lambda i, j: (i, 0)), pl.BlockSpec((k, tn), lambda i, j: (0, j))]
    ops = [a, b]
    if bias is not None:
        in_specs.append(pl.BlockSpec((1, tn), lambda i, j: (0, j)))
        ops.append(bias)
    return pl.pallas_call(body, name=name, out_shape=_sds((m, n), out_dtype), grid=(m // tm, n // tn),
                          in_specs=in_specs, out_specs=pl.BlockSpec((tm, tn), lambda i, j: (i, j)),
                          compiler_params=_cp(("parallel", "parallel"), VMEM_BIG))(*ops)


def mm_nt(a, b, name, out_dtype=F32, rider=None):
    m, k = a.shape
    n = b.shape[0]
    tm = _pick(m, (1152, 768, 512, 256, 128, 64, 32, 16, 8))
    tn = _pick(n, (1024, 768, 640, 512, 384, 256, 128))

    def body(a_ref, b_ref, o_ref):
        o_ref[...] = _nt(a_ref[...].astype(BF16), b_ref[...].astype(BF16)).astype(o_ref.dtype)

    res = _host_call(body, rider, name, (m // tm, n // tn),
                     [pl.BlockSpec((tm, k), lambda i, j: (i, 0)), pl.BlockSpec((tn, k), lambda i, j: (j, 0))],
                     [pl.BlockSpec((tm, tn), lambda i, j: (i, j))], [_sds((m, n), out_dtype)], (a, b),
                     ("parallel", "parallel"), 2, 1)
    return res[0] if rider is None else res


def mm_tn(a, b, name, rider=None):
    t, m = a.shape
    n = b.shape[1]
    tk = _pick(t, (1152, 768, 512, 256, 128, 64, 32, 16))
    tm = _pick(m, (512, 384, 256, 128))

    def body(a_ref, b_ref, o_ref):
        @pl.when(pl.program_id(1) == 0)
        def _():
            o_ref[...] = jnp.zeros_like(o_ref)
        o_ref[...] += _tn(a_ref[...].astype(BF16), b_ref[...].astype(BF16))

    res = _host_call(body, rider, name, (m // tm, t // tk),
                     [pl.BlockSpec((tk, tm), lambda i, k: (k, i)), pl.BlockSpec((tk, n), lambda i, k: (k, 0))],
                     [pl.BlockSpec((tm, n), lambda i, k: (i, 0))], [_sds((m, n), F32)], (a, b),
                     ("parallel", "arbitrary"), 2, 1)
    return res[0] if rider is None else res


class Rider:
    def __init__(self, ins, outs, aliases, nsem, start, wait):
        self.ins, self.outs, self.aliases, self.nsem, self.start, self.wait = ins, outs, aliases, nsem, start, wait


def _chips_of(mx, my):
    return [(1 - mx, my), (mx, 1 - my), (1 - mx, 1 - my)]


def rider_gather_d2d(buf):
    def start(ins, outs, ssem, rsem):
        o = outs[0]
        mx, my, mc = _me()
        for j, (px, py) in enumerate(_chips_of(mx, my)):
            ps = 2 * px + py
            _rcopy(o.at[ps, mc], o.at[ps, mc], ssem.at[j], rsem.at[j], (mx, my, 1 - mc)).start()

    def wait(ins, outs, ssem, rsem):
        o = outs[0]
        mx, my, mc = _me()
        sib = (mx, my, 1 - mc)
        for j, (px, py) in enumerate(_chips_of(mx, my)):
            ps = 2 * px + py
            _rcopy(o.at[ps, 1 - mc], o.at[ps, 1 - mc], ssem.at[j], rsem.at[j], sib).wait_recv()
        for j, (px, py) in enumerate(_chips_of(mx, my)):
            ps = 2 * px + py
            _rcopy(o.at[ps, mc], o.at[ps, mc], ssem.at[j], rsem.at[j], sib).wait_send()

    return Rider([buf], [_sds(buf.shape, buf.dtype)], {0: 0}, 3, start, wait)


def rider_reduce_sib(buf):
    n, _, h, w = buf.shape

    def start(ins, outs, ssem, rsem):
        mx, my, mc = _me()
        for k in range(N_CHIP):
            _rcopy(ins[0].at[k, 1 - mc], outs[0].at[k], ssem.at[k], rsem.at[k], (mx, my, 1 - mc)).start()

    def wait(ins, outs, ssem, rsem):
        mx, my, mc = _me()
        for k in range(N_CHIP):
            _rcopy(ins[0].at[k, 1 - mc], outs[0].at[k], ssem.at[k], rsem.at[k], (mx, my, 1 - mc)).wait_recv()
        for k in range(N_CHIP):
            _rcopy(ins[0].at[k, 1 - mc], outs[0].at[k], ssem.at[k], rsem.at[k], (mx, my, 1 - mc)).wait_send()

    return Rider([buf], [_sds((n, h, w), buf.dtype)], {}, N_CHIP, start, wait)


def rider_gather_xy(buf):
    def peers():
        mx, my, mc = _me()
        return 2 * mx + my, mc, [(1 - mx, my), (mx, 1 - my)]

    def start(ins, outs, ssem, rsem):
        o = outs[0]
        s, mc, nb = peers()
        for j, (px, py) in enumerate(nb):
            _rcopy(o.at[s, mc], o.at[s, mc], ssem.at[j], rsem.at[j], (px, py, mc)).start()

    def wait(ins, outs, ssem, rsem):
        o = outs[0]
        s, mc, nb = peers()
        for j, (px, py) in enumerate(nb):
            _rcopy(o.at[2 * px + py, mc], o.at[2 * px + py, mc], ssem.at[j], rsem.at[j], (px, py, mc)).wait_recv()
        for j, (px, py) in enumerate(nb):
            _rcopy(o.at[s, mc], o.at[s, mc], ssem.at[j], rsem.at[j], (px, py, mc)).wait_send()

    return Rider([buf], [_sds(buf.shape, buf.dtype)], {0: 0}, 2, start, wait)


def rider_gather_fwd(buf):
    h2 = buf.shape[2] // 2
    lo, hi = pl.ds(0, h2), pl.ds(h2, buf.shape[2] - h2)

    def start(ins, outs, ssem, rsem):
        o = outs[0]
        mx, my, mc = _me()
        xs, ys = 2 * (1 - mx) + my, 2 * mx + (1 - my)
        _rcopy(o.at[xs, mc, lo], o.at[xs, mc, lo], ssem.at[0], rsem.at[0], (mx, 1 - my, mc)).start()
        _rcopy(o.at[ys, mc, hi], o.at[ys, mc, hi], ssem.at[1], rsem.at[1], (1 - mx, my, mc)).start()

    def wait(ins, outs, ssem, rsem):
        o = outs[0]
        mx, my, mc = _me()
        xs, ys, ds = 2 * (1 - mx) + my, 2 * mx + (1 - my), 2 * (1 - mx) + (1 - my)
        _rcopy(o.at[ds, mc, lo], o.at[ds, mc, lo], ssem.at[0], rsem.at[0], (mx, 1 - my, mc)).wait_recv()
        _rcopy(o.at[ds, mc, hi], o.at[ds, mc, hi], ssem.at[1], rsem.at[1], (1 - mx, my, mc)).wait_recv()
        _rcopy(o.at[xs, mc, lo], o.at[xs, mc, lo], ssem.at[0], rsem.at[0], (mx, 1 - my, mc)).wait_send()
        _rcopy(o.at[ys, mc, hi], o.at[ys, mc, hi], ssem.at[1], rsem.at[1], (1 - mx, my, mc)).wait_send()

    return Rider([buf], [_sds(buf.shape, buf.dtype)], {0: 0}, 2, start, wait)


def rider_reduce_copy(q, j, r=None):
    def peer():
        mx, my, mc = _me()
        px, py = _chips_of(mx, my)[j]
        return 2 * mx + my, 2 * px + py, (px, py, mc)

    def start(ins, outs, ssem, rsem):
        s, ps, dev = peer()
        _rcopy(ins[0].at[ps], outs[0].at[s], ssem.at[0], rsem.at[0], dev).start()

    def wait(ins, outs, ssem, rsem):
        s, ps, dev = peer()
        _rcopy(ins[0].at[ps], outs[0].at[ps], ssem.at[0], rsem.at[0], dev).wait_recv()
        _rcopy(ins[0].at[ps], outs[0].at[s], ssem.at[0], rsem.at[0], dev).wait_send()

    if r is None:
        return Rider([q], [_sds(q.shape, q.dtype)], {}, 1, start, wait)
    return Rider([q, r], [_sds(q.shape, q.dtype)], {1: 0}, 1, start, wait)


def rider_reduce_copies(q):
    def start(ins, outs, ssem, rsem):
        mx, my, mc = _me()
        s = 2 * mx + my
        for j, (px, py) in enumerate(_chips_of(mx, my)):
            _rcopy(ins[0].at[2 * px + py], outs[0].at[s], ssem.at[j], rsem.at[j], (px, py, mc)).start()

    def wait(ins, outs, ssem, rsem):
        mx, my, mc = _me()
        s = 2 * mx + my
        for j, (px, py) in enumerate(_chips_of(mx, my)):
            ps = 2 * px + py
            _rcopy(ins[0].at[ps], outs[0].at[ps], ssem.at[j], rsem.at[j], (px, py, mc)).wait_recv()
        for j, (px, py) in enumerate(_chips_of(mx, my)):
            _rcopy(ins[0].at[2 * px + py], outs[0].at[s], ssem.at[j], rsem.at[j], (px, py, mc)).wait_send()

    return Rider([q], [_sds(q.shape, q.dtype)], {}, 3, start, wait)


def rider_join(buf, g=None):
    def start(ins, outs, ssem, rsem):
        o = outs[0] if g is None else outs[0].at[g]
        mx, my, mc = _me()
        _rcopy(o.at[mc], o.at[mc], ssem.at[0], rsem.at[0], (mx, my, 1 - mc)).start()

    def wait(ins, outs, ssem, rsem):
        o = outs[0] if g is None else outs[0].at[g]
        mx, my, mc = _me()
        _rcopy(o.at[1 - mc], o.at[1 - mc], ssem.at[0], rsem.at[0], (mx, my, 1 - mc)).wait_recv()
        _rcopy(o.at[mc], o.at[mc], ssem.at[0], rsem.at[0], (mx, my, 1 - mc)).wait_send()

    return Rider([buf], [_sds(buf.shape, buf.dtype)], {0: 0}, 1, start, wait)


def _host_call(body, rider, name, grid, in_specs, out_specs, out_shape, operands, sem, n_in, n_out, aliases=None):
    aliases = dict(aliases or {})
    if rider is None:
        return pl.pallas_call(body, name=name, out_shape=out_shape, grid=grid, in_specs=in_specs, out_specs=out_specs,
                              input_output_aliases=aliases, compiler_params=_cp(sem, VMEM_BIG))(*operands)
    n_ri, n_ro = len(rider.ins), len(rider.outs)
    aliases.update({n_in + a: n_out + b for a, b in rider.aliases.items()})

    def hosted(*refs):
        ins, r_in = refs[:n_in], refs[n_in:n_in + n_ri]
        outs, r_out = refs[n_in + n_ri:n_in + n_ri + n_out], refs[n_in + n_ri + n_out:n_in + n_ri + n_out + n_ro]
        ssem, rsem = refs[-2], refs[-1]
        first = functools.reduce(lambda a, b: a & b, [pl.program_id(k) == 0 for k in range(len(grid))])
        last = functools.reduce(lambda a, b: a & b, [pl.program_id(k) == grid[k] - 1 for k in range(len(grid))])

        @pl.when(first)
        def _():
            rider.start(r_in, r_out, ssem, rsem)
        body(*ins, *outs)

        @pl.when(last)
        def _():
            rider.wait(r_in, r_out, ssem, rsem)

    return pl.pallas_call(
        hosted, name=name, out_shape=list(out_shape) + list(rider.outs), grid=grid,
        in_specs=list(in_specs) + [ANY] * n_ri, out_specs=list(out_specs) + [ANY] * n_ro,
        input_output_aliases=aliases,
        scratch_shapes=[pltpu.SemaphoreType.DMA((rider.nsem,)), pltpu.SemaphoreType.DMA((rider.nsem,))],
        compiler_params=_cp(("arbitrary",) * len(grid), VMEM_BIG))(*operands, *rider.ins)


def ffn_up(lay, hm, wbuf, ig, iu, ns, name, rider=None):
    tm = lay.tm

    def body(h_ref, wg_ref, wu_ref, up_ref, sl_ref, a_ref):
        hb = h_ref[...]
        g = _nt(hb, wg_ref[0])
        u = _nt(hb, wu_ref[0])
        sg = _sigmoid(g)
        sl = g * sg
        up_ref[0] = (u * (sg + sl * (1.0 - sg))).astype(BF16)
        sl_ref[0] = sl.astype(BF16)
        a_ref[0] = (sl * u).astype(BF16)

    spec_o = pl.BlockSpec((1, tm, ns), lambda s, i: (s, i, 0))
    return _host_call(
        body, rider, name, (N_CHIP, lay.T // tm),
        [pl.BlockSpec((tm, D), lambda s, i: (i, 0)), pl.BlockSpec((1, ns, D), lambda s, i: (s, ig, 0)),
         pl.BlockSpec((1, ns, D), lambda s, i: (s, iu, 0))],
        [spec_o] * 3, [_sds((N_CHIP, lay.T, ns), BF16)] * 3, (hm, wbuf, wbuf), ("parallel", "parallel"), 3, 3)


def slab_nn_acc(lay, zs, wbuf, idxs, ns, name, rider=None):
    tm = lay.tm2
    npair = len(zs)

    def body(*refs):
        o_ref = refs[-1]

        @pl.when(pl.program_id(1) == 0)
        def _():
            o_ref[...] = jnp.zeros_like(o_ref)
        acc = _nn(refs[0][0], refs[npair][0])
        for p in range(1, npair):
            acc += _nn(refs[p][0], refs[npair + p][0])
        o_ref[...] += acc

    in_specs = [pl.BlockSpec((1, tm, ns), lambda i, s: (s, i, 0)) for _ in zs]
    in_specs += [pl.BlockSpec((1, ns, D), functools.partial(lambda i, s, q: (s, q, 0), q=q)) for q in idxs]
    return _host_call(body, rider, name, (lay.T // tm, N_CHIP), in_specs, [pl.BlockSpec((tm, D), lambda i, s: (i, 0))],
                      [_sds((lay.T, D), F32)], (*zs, *([wbuf] * npair)), ("parallel", "arbitrary"), 2 * npair, 1)


def ffn_bwd_da(lay, dy, wbuf, idn, up, sl, ns, name, rider=None):
    tm = lay.tm

    def body(dy_ref, wd_ref, up_ref, sl_ref, dg_ref, du_ref):
        da = _nt(dy_ref[...], wd_ref[0])
        dg_ref[0] = (da * up_ref[0].astype(F32)).astype(BF16)
        du_ref[0] = (da * sl_ref[0].astype(F32)).astype(BF16)

    spec_z = pl.BlockSpec((1, tm, ns), lambda s, i: (s, i, 0))
    return _host_call(
        body, rider, name, (N_CHIP, lay.T // tm),
        [pl.BlockSpec((tm, D), lambda s, i: (i, 0)), pl.BlockSpec((1, ns, D), lambda s, i: (s, idn, 0)), spec_z, spec_z],
        [spec_z] * 2, [_sds((N_CHIP, lay.T, ns), BF16)] * 2, (dy, wbuf, up, sl), ("parallel", "parallel"), 4, 2)


def slab_tn(lay, z, x, gbuf, idx, ns, name, rider=None):
    tk = _pick(lay.T, (768, 512, 256, 128))

    def body(z_ref, x_ref, g_in, o_ref):
        del g_in

        @pl.when(pl.program_id(0) == 0)
        def _():
            o_ref[...] = jnp.zeros_like(o_ref)
        xv = x_ref[...]
        for s in range(N_CHIP):
            o_ref[s] += _tn(z_ref[s], xv)

    return _host_call(
        body, rider, name, (lay.T // tk,),
        [pl.BlockSpec((N_CHIP, tk, ns), lambda k: (0, k, 0)), pl.BlockSpec((tk, D), lambda k: (k, 0)), ANY],
        [pl.BlockSpec((N_CHIP, ns, D), lambda k: (0, idx, 0))], [_sds(gbuf.shape, F32)], (z, x, gbuf),
        ("arbitrary",), 3, 1, aliases={2: 0})


Q0, K0, V0, U0, QR0, KR0, PEXT = 0, 512, 640, 768, 1280, 1792, 1920


def rope_fwd(lay, p, cos, sin, name):
    def fn(pb, cs, sn):
        cs4 = jnp.concatenate([cs] * 4, axis=1)
        sn4 = jnp.concatenate([sn] * 4, axis=1)
        qr = pb[:, Q0:K0] * cs4 + pb[:, QR0:KR0] * sn4
        kr = pb[:, K0:V0] * cs + pb[:, KR0:PEXT] * sn
        return qr, kr, pb[:, V0:U0], pb[:, U0:QR0]
    return rowwise(lay, name, fn, [p, cos, sin], outs=[(ATT_W, BF16), (KV_W, BF16), (KV_W, BF16), (POOL_W, F32)])


def rope_bwd(lay, dqr, dkr, dv, du, cos, sin, name):
    def fn(dq, dk, dvb, dub, cs, sn):
        cs4 = jnp.concatenate([cs] * 4, axis=1)
        sn4 = jnp.concatenate([sn] * 4, axis=1)
        return (jnp.concatenate([dq * cs4, dk * cs, dvb, dub, dq * sn4, dk * sn], axis=1),)
    return rowwise(lay, name, fn, [dqr, dkr, dv, du, cos, sin], outs=[(PEXT, BF16)])[0]


def _attn_specs(lay):
    nbs, cbk, lbk = lay.PS // BLK, lay.C // BLK, lay.L // BLK

    def kv_map(j):
        return lambda s, n: (s * nbs + cbk + jnp.clip(n - cbk + j - 1, 0, lbk - 1), 0)

    win = [pl.BlockSpec((BLK, KV_W), kv_map(j)) for j in range(3)]
    ctx = pl.BlockSpec((lay.C, KV_W), lambda s, n: (s * (lay.PS // lay.C), 0))
    return nbs, cbk, lbk, win, ctx


def _attn_masks(n, cbk, lbk):
    row = lax.broadcasted_iota(jnp.int32, (BLK, BLK), 0)
    col = lax.broadcasted_iota(jnp.int32, (BLK, BLK), 1)
    m = n - cbk
    lat = n >= cbk
    valid = [lat & (m >= 1) & (col >= row), lat & (col >= 0), lat & (m <= lbk - 2) & (col <= row)]
    lane_lo = lax.broadcasted_iota(jnp.int32, (BLK, 2 * HEAD_DIM), 1) < HEAD_DIM
    return valid, lane_lo


def attn_fwd(lay, qr, kr, vb, sink_tab, name):
    nbs, cbk, lbk, win, ctx = _attn_specs(lay)

    def body(q_ref, k0, k1, k2, kc_ref, v0, v1, v2, vc_ref, sk_ref, o_ref, l_ref):
        n = pl.program_id(1)
        valid, lane_lo = _attn_masks(n, cbk, lbk)
        valid4 = [jnp.concatenate([v] * 4, axis=0) for v in valid]
        ks = [k0[...], k1[...], k2[...]]
        vs = [v0[...], v1[...], v2[...]]
        kc, vc = kc_ref[...], vc_ref[...]
        q2s = [q_ref[:, p * 128:(p + 1) * 128] for p in range(4)]
        outs, lses = [], []
        for hh in range(2):
            sel = lane_lo == (hh == 0)
            qm = jnp.concatenate([jnp.where(sel, q2, jnp.zeros_like(q2)) for q2 in q2s], axis=0)
            sk = jnp.concatenate([jnp.broadcast_to(sk_ref[p:p + 1, hh * HEAD_DIM:hh * HEAD_DIM + 1], (BLK, 1))
                                  for p in range(4)], axis=0)
            sw = [jnp.where(valid4[j], _nt(qm, ks[j]) * ATT_SCALE, NEG_INF) for j in range(3)]
            sc = _nt(qm, kc) * ATT_SCALE
            mx = jnp.maximum(jnp.maximum(jnp.maximum(sw[0].max(-1, keepdims=True), sw[1].max(-1, keepdims=True)),
                                         jnp.maximum(sw[2].max(-1, keepdims=True), sc.max(-1, keepdims=True))), sk)
            ew = [jnp.exp(s - mx) for s in sw]
            ec = jnp.exp(sc - mx)
            den = ew[0].sum(-1, keepdims=True) + ew[1].sum(-1, keepdims=True) + ew[2].sum(-1, keepdims=True)
            den = den + ec.sum(-1, keepdims=True) + jnp.exp(sk - mx)
            o = _nn((ec / den).astype(BF16), vc)
            for j in range(3):
                o += _nn((ew[j] / den).astype(BF16), vs[j])
            outs.append(o)
            lses.append(mx + jnp.log(den))
        for p in range(4):
            rows = slice(p * BLK, (p + 1) * BLK)
            o_ref[:, p * 128:(p + 1) * 128] = jnp.where(lane_lo, outs[0][rows], outs[1][rows]).astype(o_ref.dtype)
            l_ref[:, p * 128:(p + 1) * 128] = jnp.where(lane_lo, jnp.broadcast_to(lses[0][rows], (BLK, 128)),
                                                        jnp.broadcast_to(lses[1][rows], (BLK, 128)))

    qspec = pl.BlockSpec((BLK, ATT_W), lambda s, n: (s * nbs + n, 0))
    return pl.pallas_call(
        body, name=name, out_shape=[_sds((lay.T, ATT_W), BF16), _sds((lay.T, ATT_W), F32)], grid=(2, nbs),
        in_specs=[qspec] + win + [ctx] + win + [ctx] + [pl.BlockSpec((8, 128), lambda s, n: (0, 0))],
        out_specs=[qspec, qspec], compiler_params=_cp(("parallel", "parallel")))(qr, kr, kr, kr, kr, vb, vb, vb, vb, sink_tab)


def attn_bwd(lay, qr, kr, vb, sink_tab, lse, datt, name, rider=None):
    nbs, cbk, lbk, win, ctx = _attn_specs(lay)
    C, PS = lay.C, lay.PS

    def body(q_ref, k0, k1, k2, kc_ref, v0, v1, v2, vc_ref, sk_ref, l_ref, do_ref, dq_ref, dk_ref, dv_ref, ds_ref):
        n = pl.program_id(1)
        valid, lane_lo = _attn_masks(n, cbk, lbk)

        @pl.when(n == 0)
        def _():
            dk_ref[...] = jnp.zeros_like(dk_ref)
            dv_ref[...] = jnp.zeros_like(dv_ref)
            ds_ref[...] = jnp.zeros_like(ds_ref)

        ks = [k0[...], k1[...], k2[...], kc_ref[...]]
        vs = [v0[...], v1[...], v2[...], vc_ref[...]]
        valid4 = [jnp.concatenate([v] * 4, axis=0) for v in valid]
        dks = [jnp.zeros((BLK, KV_W), F32)] * 3 + [jnp.zeros((C, KV_W), F32)]
        dvs = list(dks)
        q2s = [q_ref[:, p * 128:(p + 1) * 128] for p in range(4)]
        do2s = [do_ref[:, p * 128:(p + 1) * 128].astype(BF16) for p in range(4)]
        lse2s = [l_ref[:, p * 128:(p + 1) * 128] for p in range(4)]
        dq_h, dd_h = [], []
        for hh in range(2):
            sel = lane_lo == (hh == 0)
            qm = jnp.concatenate([jnp.where(sel, q2, jnp.zeros_like(q2)) for q2 in q2s], axis=0)
            dom = jnp.concatenate([jnp.where(sel, d2, jnp.zeros_like(d2)) for d2 in do2s], axis=0)
            lse_h = jnp.concatenate([l2[:, hh * HEAD_DIM:hh * HEAD_DIM + 1] for l2 in lse2s], axis=0)
            ps, dps = [], []
            for j in range(4):
                s = _nt(qm, ks[j]) * ATT_SCALE
                if j < 3:
                    s = jnp.where(valid4[j], s, NEG_INF)
                ps.append(jnp.exp(s - lse_h))
                dps.append(_nt(dom, vs[j]))
            dd = (ps[0] * dps[0]).sum(-1, keepdims=True) + (ps[1] * dps[1]).sum(-1, keepdims=True)
            dd = dd + (ps[2] * dps[2]).sum(-1, keepdims=True) + (ps[3] * dps[3]).sum(-1, keepdims=True)
            dq = jnp.zeros((4 * BLK, 128), F32)
            for j in range(4):
                dsb = (ps[j] * (dps[j] - dd) * ATT_SCALE).astype(BF16)
                dq += _nn(dsb, ks[j])
                dks[j] = dks[j] + _tn(dsb, qm)
                dvs[j] = dvs[j] + _tn(ps[j].astype(BF16), dom)
            dq_h.append(dq)
            dd_h.append(dd)
        for p in range(4):
            sl = slice(p * 128, (p + 1) * 128)
            rows = slice(p * BLK, (p + 1) * BLK)
            dq_ref[:, sl] = jnp.where(lane_lo, dq_h[0][rows], dq_h[1][rows])
            dd2 = jnp.where(lane_lo, jnp.broadcast_to(dd_h[0][rows], (BLK, 128)), jnp.broadcast_to(dd_h[1][rows], (BLK, 128)))
            psink = jnp.exp(sk_ref[p:p + 1, :] - lse2s[p])
            ds_ref[0, p:p + 1, :] += -jnp.sum(psink * dd2, axis=0, keepdims=True)
        dk_ref[0:C, :] += dks[3]
        dv_ref[0:C, :] += dvs[3]
        for j in range(3):
            r0 = pl.multiple_of((cbk + jnp.clip(n - cbk + j - 1, 0, lbk - 1)) * BLK, BLK)
            dk_ref[pl.ds(r0, BLK), :] += dks[j]
            dv_ref[pl.ds(r0, BLK), :] += dvs[j]

    qspec = pl.BlockSpec((BLK, ATT_W), lambda s, n: (s * nbs + n, 0))
    kvout = pl.BlockSpec((PS, KV_W), lambda s, n: (s, 0))
    return _host_call(
        body, rider, name, (2, nbs),
        [qspec] + win + [ctx] + win + [ctx] + [pl.BlockSpec((8, 128), lambda s, n: (0, 0)), qspec, qspec],
        [qspec, kvout, kvout, pl.BlockSpec((1, 8, 128), lambda s, n: (s, 0, 0))],
        [_sds((lay.T, ATT_W), F32), _sds((lay.T, KV_W), F32), _sds((lay.T, KV_W), F32), _sds((2, 8, 128), F32)],
        (qr, kr, kr, kr, kr, vb, vb, vb, vb, sink_tab, lse, datt), ("parallel", "arbitrary"), 12, 4)


def _winsum(x, r):
    n = x.shape[0]
    t = lax.broadcasted_iota(jnp.int32, x.shape, 0)
    acc = x
    for o in range(1, r + 1):
        acc = acc + jnp.where(t >= o, pltpu.roll(x, o, 0), 0.0) + jnp.where(t < n - o, pltpu.roll(x, n - o, 0), 0.0)
    return acc


def _wincount(n, r):
    t = lax.broadcasted_iota(jnp.int32, (n, 128), 0)
    return (jnp.minimum(t + r, n - 1) - jnp.maximum(t - r, 0) + 1).astype(F32)


def pool_fwd(lay, u, w_pool, scale, name):
    segs = [(0, lay.C), (lay.C, lay.L)]

    def body(u_ref, w_ref, s_ref, o_ref):
        for r0, n in segs:
            for g, wd in enumerate(POOL_WINDOWS):
                sl = slice(g * 128, (g + 1) * 128)
                x = u_ref[r0:r0 + n, sl]
                d = _winsum(x, wd // 2) / _wincount(n, wd // 2) - x
                y = _nn(d.astype(BF16), w_ref[g].astype(BF16)) * s_ref[:, sl]
                o_ref[r0:r0 + n, sl] = y.astype(o_ref.dtype)

    spec = pl.BlockSpec((lay.PS, POOL_W), lambda s: (s, 0))
    return pl.pallas_call(
        body, name=name, out_shape=_sds((lay.T, POOL_W), BF16), grid=(2,),
        in_specs=[spec, pl.BlockSpec(w_pool.shape, lambda s: (0, 0, 0)), pl.BlockSpec((1, POOL_W), lambda s: (0, 0))],
        out_specs=spec, compiler_params=_cp(("parallel",), VMEM_BIG))(u, w_pool, scale)


def pool_bwd(lay, u, dcat, w_pool, scale, name):
    segs = [(0, lay.C), (lay.C, lay.L)]

    def body(u_ref, dp_ref, w_ref, s_ref, du_ref, dw_ref, dsc_ref):
        for g, wd in enumerate(POOL_WINDOWS):
            sl = slice(g * 128, (g + 1) * 128)
            wb = w_ref[g].astype(BF16)
            dw = jnp.zeros((128, 128), F32)
            dsc = jnp.zeros((1, 128), F32)
            for r0, n in segs:
                x = u_ref[r0:r0 + n, sl]
                cnt = _wincount(n, wd // 2)
                d = (_winsum(x, wd // 2) / cnt - x).astype(BF16)
                dp = dp_ref[r0:r0 + n, sl]
                dsc += jnp.sum(_nn(d, wb) * dp, axis=0, keepdims=True)
                dyp = (dp * s_ref[:, sl]).astype(BF16)
                dw += _tn(d, dyp)
                dd = _nt(dyp, wb)
                du_ref[r0:r0 + n, sl] = _winsum(dd / cnt, wd // 2) - dd
            dw_ref[0, g] = dw
            dsc_ref[0, :, sl] = dsc

    spec = pl.BlockSpec((lay.PS, POOL_W), lambda s: (s, 0))
    return pl.pallas_call(
        body, name=name,
        out_shape=[_sds((lay.T, POOL_W), F32), _sds((2, 4, 128, 128), F32), _sds((2, 1, POOL_W), F32)], grid=(2,),
        in_specs=[spec, pl.BlockSpec((lay.PS, POOL_W), lambda s: (s, 1)), pl.BlockSpec(w_pool.shape, lambda s: (0, 0, 0)),
                  pl.BlockSpec((1, POOL_W), lambda s: (0, 0))],
        out_specs=[spec, pl.BlockSpec((1, 4, 128, 128), lambda s: (s, 0, 0, 0)), pl.BlockSpec((1, 1, POOL_W), lambda s: (s, 0, 0))],
        compiler_params=_cp(("parallel",), VMEM_BIG))(u, dcat, w_pool, scale)


CONV_OFFS = (-1, 0, 1, 2)
CW = 256


def _shift_rows(x, o):
    if o == 0:
        return x
    n = x.shape[0]
    t = lax.broadcasted_iota(jnp.int32, x.shape, 0)
    if o < 0:
        return jnp.where(t >= -o, pltpu.roll(x, -o, 0), 0.0)
    return jnp.where(t < n - o, pltpu.roll(x, n - o, 0), 0.0)


def conv_fwd(lay, p, col0, w, b, name):
    segs = [(0, lay.C), (lay.C, lay.L)]
    cb0 = col0 // CW

    def body(x_ref, w_ref, b_ref, o_ref):
        for r0, n in segs:
            x = x_ref[r0:r0 + n, :]
            y = jnp.broadcast_to(b_ref[...], x.shape)
            for k, o in enumerate(CONV_OFFS):
                y = y + _shift_rows(x, o) * w_ref[k:k + 1, :]
            o_ref[r0:r0 + n, :] = y

    return pl.pallas_call(
        body, name=name, out_shape=_sds((lay.T, D), F32), grid=(2, D // CW),
        in_specs=[pl.BlockSpec((lay.PS, CW), lambda s, j: (s, cb0 + j)), pl.BlockSpec((4, CW), lambda s, j: (0, j)),
                  pl.BlockSpec((1, CW), lambda s, j: (0, j))],
        out_specs=pl.BlockSpec((lay.PS, CW), lambda s, j: (s, j)),
        compiler_params=_cp(("parallel", "parallel")))(p, w, b)


def conv_bwd(lay, p, col0, w, duc, name):
    segs = [(0, lay.C), (lay.C, lay.L)]
    cb0 = col0 // CW

    def body(x_ref, w_ref, g_ref, du_ref, dw_ref, db_ref):
        dws = [jnp.zeros((1, CW), F32)] * 4
        db = jnp.zeros((1, CW), F32)
        for r0, n in segs:
            x = x_ref[r0:r0 + n, :]
            g = g_ref[r0:r0 + n, :]
            du = jnp.zeros_like(g)
            for k, o in enumerate(CONV_OFFS):
                du = du + _shift_rows(g, -o) * w_ref[k:k + 1, :]
                dws[k] = dws[k] + jnp.sum(g * _shift_rows(x, o), axis=0, keepdims=True)
            db = db + jnp.sum(g, axis=0, keepdims=True)
            du_ref[r0:r0 + n, :] = du.astype(du_ref.dtype)
        dw_ref[0] = jnp.concatenate(dws, axis=0)
        db_ref[0] = db

    return pl.pallas_call(
        body, name=name, out_shape=[_sds((lay.T, D), BF16), _sds((2, 4, D), F32), _sds((2, 1, D), F32)], grid=(2, D // CW),
        in_specs=[pl.BlockSpec((lay.PS, CW), lambda s, j: (s, cb0 + j)), pl.BlockSpec((4, CW), lambda s, j: (0, j)),
                  pl.BlockSpec((lay.PS, CW), lambda s, j: (s, j))],
        out_specs=[pl.BlockSpec((lay.PS, CW), lambda s, j: (s, j)), pl.BlockSpec((1, 4, CW), lambda s, j: (s, 0, j)),
                   pl.BlockSpec((1, 1, CW), lambda s, j: (s, 0, j))],
        compiler_params=_cp(("parallel", "parallel")))(p, w, duc)


def _softplus_neg(lam):
    z = -lam
    w = jnp.exp(-jnp.abs(z))
    log1p = jnp.where(w < 1e-2, w * (1.0 - w * (0.5 - w / 3.0)), jnp.log(1.0 + w))
    return jnp.maximum(z, 0.0) + log1p, -_sigmoid(z)


def _neg_expm1(x):
    series = -x * (1.0 + x * (0.5 + x * (1.0 / 6.0 + x * (1.0 / 24.0 + x * (1.0 / 120.0)))))
    return jnp.where(x > -0.05, series, 1.0 - jnp.exp(x))


def _lru_gates(x, xb, wa, wx, ba, bx, lam):
    r = _sigmoid(_nn(xb, wa.astype(BF16)) + ba)
    gi = _sigmoid(_nn(xb, wx.astype(BF16)) + bx)
    sp, dsp = _softplus_neg(lam)
    la = -LRU_C * r * sp
    a = jnp.exp(la)
    sq = jnp.sqrt(_neg_expm1(2.0 * la))
    return r, gi, sp, dsp, a, sq


def lru_coeffs(lay, uc, wa, wx, vec, name):
    tr = lay.tc

    def body(x_ref, wa_ref, wx_ref, v_ref, a_ref, b_ref):
        for h in range(8):
            sl = slice(h * 128, (h + 1) * 128)
            x = x_ref[:, sl]
            xb = x.astype(BF16)
            for d in range(2):
                _, gi, _, _, a, sq = _lru_gates(x, xb, wa_ref[d, h], wx_ref[d, h], v_ref[d:d + 1, sl],
                                                v_ref[2 + d:3 + d, sl], v_ref[4 + d:5 + d, sl])
                a_ref[d, h] = a
                b_ref[d, h] = sq * (gi * x)

    wspec = pl.BlockSpec((2, 8, 128, 128), lambda i: (0, 0, 0, 0))
    ospec = pl.BlockSpec((2, 8, tr, 128), lambda i: (0, 0, i, 0))
    return pl.pallas_call(
        body, name=name, out_shape=[_sds((2, 8, lay.T, 128), F32)] * 2, grid=(lay.T // tr,),
        in_specs=[pl.BlockSpec((tr, D), lambda i: (i, 0)), wspec, wspec, pl.BlockSpec((6, D), lambda i: (0, 0))],
        out_specs=[ospec, ospec], compiler_params=_cp(("parallel",), VMEM_BIG))(uc, wa, wx, vec)


def lru_coeffs_bwd(lay, uc, wa, wx, vec, da, db, name, rider=None):
    tr = lay.tc

    def body(x_ref, wa_ref, wx_ref, v_ref, da_ref, db_ref, dx_ref, dwa_ref, dwx_ref, dv_ref):
        @pl.when(pl.program_id(0) == 0)
        def _():
            dwa_ref[...] = jnp.zeros_like(dwa_ref)
            dwx_ref[...] = jnp.zeros_like(dwx_ref)
            dv_ref[...] = jnp.zeros_like(dv_ref)

        for h in range(8):
            sl = slice(h * 128, (h + 1) * 128)
            x = x_ref[:, sl]
            xb = x.astype(BF16)
            dx = jnp.zeros_like(x)
            for d in range(2):
                wab, wxb = wa_ref[d, h].astype(BF16), wx_ref[d, h].astype(BF16)
                r, gi, sp, dsp, a, sq = _lru_gates(x, xb, wa_ref[d, h], wx_ref[d, h], v_ref[d:d + 1, sl],
                                                   v_ref[2 + d:3 + d, sl], v_ref[4 + d:5 + d, sl])
                dbv, dav = db_ref[d, h], da_ref[d, h]
                t1 = dbv * sq
                dgi = t1 * x
                dx = dx + t1 * gi
                dla = dav * a - (dbv * gi * x) * (a * a) / sq
                dr = dla * (-LRU_C * sp)
                dlam = jnp.sum(dla * (-LRU_C * r), axis=0, keepdims=True) * dsp
                dpa = dr * r * (1.0 - r)
                dpx = dgi * gi * (1.0 - gi)
                dpab, dpxb = dpa.astype(BF16), dpx.astype(BF16)
                dwa_ref[d, h] += _tn(xb, dpab)
                dwx_ref[d, h] += _tn(xb, dpxb)
                dx = dx + _nt(dpab, wab) + _nt(dpxb, wxb)
                dv_ref[d:d + 1, sl] += jnp.sum(dpa, axis=0, keepdims=True)
                dv_ref[2 + d:3 + d, sl] += jnp.sum(dpx, axis=0, keepdims=True)
                dv_ref[4 + d:5 + d, sl] += dlam
            dx_ref[:, sl] = dx

    wspec = pl.BlockSpec((2, 8, 128, 128), lambda i: (0, 0, 0, 0))
    gspec = pl.BlockSpec((2, 8, tr, 128), lambda i: (0, 0, i, 0))
    vspec = pl.BlockSpec((6, D), lambda i: (0, 0))
    xspec = pl.BlockSpec((tr, D), lambda i: (i, 0))
    return _host_call(
        body, rider, name, (lay.T // tr,), [xspec, wspec, wspec, vspec, gspec, gspec], [xspec, wspec, wspec, vspec],
        [_sds((lay.T, D), F32), _sds((2, 8, 128, 128), F32), _sds((2, 8, 128, 128), F32), _sds((6, D), F32)],
        (uc, wa, wx, vec, da, db), ("arbitrary",), 6, 4)


GB = 2
SCAN_UNROLL = 8


def _tile_scan(a, b, up):
    t = lax.broadcasted_iota(jnp.int32, a.shape, 0)
    for d in (1, 2, 4):
        sh = 8 - d if up else d
        m = (t < 8 - d) if up else (t >= d)
        a_prev, b_prev = pltpu.roll(a, sh, 0), pltpu.roll(b, sh, 0)
        b = jnp.where(m, a * b_prev + b, b)
        a = jnp.where(m, a * a_prev, a)
    return a, b


def lru_scan(lay, a, b, name):
    segs = [(0, lay.C), (lay.C, lay.L)]

    def body(a_ref, b_ref, s_ref):
        for d in range(2):
            rev = d == 1
            state = tuple(jnp.zeros((1, 128), F32) for _ in range(GB))
            for base, n in segs:
                nt = n // 8

                def step(j, c, base=base, nt=nt, rev=rev, d=d):
                    c = list(c)
                    for u in range(SCAN_UNROLL):
                        jj = j * SCAN_UNROLL + u
                        r0 = pl.multiple_of(base + 8 * ((nt - 1 - jj) if rev else jj), 8)
                        for g in range(GB):
                            at, bt = _tile_scan(a_ref[d, g, pl.ds(r0, 8), :], b_ref[d, g, pl.ds(r0, 8), :], rev)
                            h = at * c[g] + bt
                            s_ref[d, g, pl.ds(r0, 8), :] = h
                            c[g] = h[0:1] if rev else h[7:8]
                    return tuple(c)

                state = lax.fori_loop(0, nt // SCAN_UNROLL, step, state)

    spec = pl.BlockSpec((2, GB, lay.PS, 128), lambda s, hb: (0, hb, s, 0))
    return pl.pallas_call(
        body, name=name, out_shape=_sds((2, 8, lay.T, 128), F32), grid=(2, 8 // GB),
        in_specs=[spec, spec], out_specs=spec, compiler_params=_cp(("parallel", "parallel"), VMEM_BIG))(a, b)


def lru_scan_bwd(lay, a, s, dy, name):
    segs = [(0, lay.C), (lay.C, lay.L)]
    C, PS = lay.C, lay.PS

    def body(a_ref, s_ref, g_ref, da_ref, db_ref):
        t = lax.broadcasted_iota(jnp.int32, (8, 128), 0)
        for d in range(2):
            rev = d == 1
            carry = tuple(jnp.zeros((1, 128), F32) for _ in range(GB))
            for si in (1, 0):
                base, n = segs[si]
                nt = n // 8

                def step(j, c, base=base, nt=nt, rev=rev, d=d):
                    c = list(c)
                    for u in range(SCAN_UNROLL):
                        jj = j * SCAN_UNROLL + u
                        r0 = pl.multiple_of(base + 8 * (jj if rev else (nt - 1 - jj)), 8)
                        if rev:
                            rn = pl.multiple_of(jnp.where(r0 == PS - 8, 0, r0 + 8), 8)
                            nb_zero = r0 == C - 8
                        else:
                            rn = pl.multiple_of(jnp.maximum(r0 - 8, 0), 8)
                            nb_zero = r0 == 0
                        for g in range(GB):
                            av = a_ref[d, g, pl.ds(r0, 8), :]
                            gv = g_ref[g, pl.ds(r0, 8), :]
                            sv = s_ref[d, g, pl.ds(r0, 8), :]
                            nbt = s_ref[d, g, pl.ds(rn, 8), :]
                            at, bt = _tile_scan(av, av * gv, not rev)
                            m = at * c[g] + bt
                            if rev:
                                m_next = jnp.where(t >= 1, pltpu.roll(m, 1, 0), c[g])
                                nb = jnp.where(nb_zero, 0.0, nbt[0:1])
                                h_prev = jnp.where(t < 7, pltpu.roll(sv, 7, 0), nb)
                                c[g] = m[7:8]
                            else:
                                m_next = jnp.where(t < 7, pltpu.roll(m, 7, 0), c[g])
                                nb = jnp.where(nb_zero, 0.0, nbt[7:8])
                                h_prev = jnp.where(t >= 1, pltpu.roll(sv, 1, 0), nb)
                                c[g] = m[0:1]
                            lam = gv + m_next
                            db_ref[d, g, pl.ds(r0, 8), :] = lam
                            da_ref[d, g, pl.ds(r0, 8), :] = lam * h_prev
                    return tuple(c)

                carry = lax.fori_loop(0, nt // SCAN_UNROLL, step, carry)

    spec = pl.BlockSpec((2, GB, lay.PS, 128), lambda s, hb: (0, hb, s, 0))
    return pl.pallas_call(
        body, name=name, out_shape=[_sds((2, 8, lay.T, 128), F32)] * 2, grid=(2, 8 // GB),
        in_specs=[spec, spec, pl.BlockSpec((GB, lay.PS, 128), lambda s, hb: (hb, s, 0))],
        out_specs=[spec, spec], compiler_params=_cp(("parallel", "parallel"), VMEM_BIG))(a, s, dy)


def _gelu(x):
    k = math.sqrt(2.0 / math.pi)
    t = jnp.tanh(k * (x + 0.044715 * x * x * x))
    return 0.5 * x * (1.0 + t), 0.5 * (1.0 + t) + 0.5 * x * (1.0 - t * t) * k * (1.0 + 3 * 0.044715 * x * x)


def lru_gate(lay, p, s, name):
    tr = lay.tr

    def body(g_ref, s_ref, o_ref):
        for h in range(8):
            sl = slice(h * 128, (h + 1) * 128)
            o_ref[:, sl] = (_gelu(g_ref[:, sl])[0] * (s_ref[0, h] + s_ref[1, h])).astype(o_ref.dtype)

    return pl.pallas_call(
        body, name=name, out_shape=_sds((lay.T, D), BF16), grid=(lay.nblk,),
        in_specs=[pl.BlockSpec((tr, D), lambda i: (i, 0)), pl.BlockSpec((2, 8, tr, 128), lambda i: (0, 0, i, 0))],
        out_specs=pl.BlockSpec((tr, D), lambda i: (i, 0)), compiler_params=_cp(("parallel",)))(p, s)


def lru_gate_bwd(lay, p, s, do, name):
    tr = lay.tr

    def body(g_ref, s_ref, do_ref, dg_ref, dy_ref):
        for h in range(8):
            sl = slice(h * 128, (h + 1) * 128)
            ge, dge = _gelu(g_ref[:, sl])
            dov = do_ref[:, sl]
            dg_ref[:, sl] = (dov * (s_ref[0, h] + s_ref[1, h]) * dge).astype(dg_ref.dtype)
            dy_ref[h] = dov * ge

    xspec = pl.BlockSpec((tr, D), lambda i: (i, 0))
    return pl.pallas_call(
        body, name=name, out_shape=[_sds((lay.T, D), BF16), _sds((8, lay.T, 128), F32)], grid=(lay.nblk,),
        in_specs=[xspec, pl.BlockSpec((2, 8, tr, 128), lambda i: (0, 0, i, 0)), xspec],
        out_specs=[xspec, pl.BlockSpec((8, tr, 128), lambda i: (0, i, 0))],
        compiler_params=_cp(("parallel",)))(p, s, do)


def silu_rows(x, name):
    def body(x_ref, o_ref):
        v = x_ref[...]
        o_ref[...] = (v * _sigmoid(v)).astype(o_ref.dtype)
    return pl.pallas_call(body, name=name, out_shape=_sds(x.shape, BF16), in_specs=[VMEM_SPEC], out_specs=VMEM_SPEC)(x)


def mod_grad_rows(gath, name):
    w = gath.shape[-1]

    def body(g_ref, dm_ref, db_ref):
        dm_ref[...] = jnp.zeros_like(dm_ref)
        for l in range(2):
            ctx = g_ref[0, 3 * l + 2:3 * l + 3, :]
            tot = g_ref[0, 3 * l:3 * l + 1, :] + g_ref[0, 3 * l + 1:3 * l + 2, :]
            for k in range(8):
                dm_ref[l, 2 * k:2 * k + 2, :] = g_ref[k, 3 * l:3 * l + 2, :]
                if k:
                    ctx = ctx + g_ref[k, 3 * l + 2:3 * l + 3, :]
                    tot = tot + (g_ref[k, 3 * l:3 * l + 1, :] + g_ref[k, 3 * l + 1:3 * l + 2, :])
            dm_ref[l, 16:17, :] = ctx
            db_ref[l:l + 1, :] = tot + ctx

    return pl.pallas_call(body, name=name, out_shape=[_sds((2, 32, w), F32), _sds((2, w), F32)],
                          in_specs=[VMEM_SPEC], out_specs=[VMEM_SPEC, VMEM_SPEC])(gath)


def cctx_grad(p, c_ctx, name):
    def body(a_ref, c_ref, o_ref):
        cv = c_ref[...]
        sg = _sigmoid(cv)
        o_ref[...] = 0.5 * (a_ref[0, 0:1, :] + a_ref[1, 0:1, :]) * (sg * (1.0 + cv * (1.0 - sg)))
    return pl.pallas_call(body, name=name, out_shape=_sds((1, D), F32), in_specs=[VMEM_SPEC] * 2,
                          out_specs=VMEM_SPEC)(p, c_ctx)


def loss_and_grad(lay, h, tgt, name):
    def fn(hb, tb):
        lat = (pl.program_id(0) % lay.bps) >= lay.cb
        e = jnp.where(lat, hb - tb, 0.0)
        return e * (1.0 / D), jnp.sum(e * e, axis=0, keepdims=True) * (0.5 / D)
    return rowwise(lay, name, fn, [h, tgt], outs=[(D, F32)], sums=[(1, D)])


def adamw(w, g, m, v, name):
    shape = w.shape
    w2, g2, m2, v2 = (t.reshape(-1, shape[-1]) for t in (w, g, m, v))
    rows, width = w2.shape
    tr = 256 if rows % 256 == 0 else rows
    c1 = 1.0 - ADAM_B1 ** ADAM_STEP
    c2 = 1.0 - ADAM_B2 ** ADAM_STEP

    def body(w_ref, g_ref, m_ref, v_ref, d_ref, mo_ref, vo_ref):
        gv = g_ref[...]
        mn = ADAM_B1 * m_ref[...] + (1.0 - ADAM_B1) * gv
        vn = ADAM_B2 * v_ref[...] + (1.0 - ADAM_B2) * (gv * gv)
        d_ref[...] = -ADAM_LR * ((mn / c1) / (jnp.sqrt(vn / c2) + ADAM_EPS) + ADAM_WD * w_ref[...])
        mo_ref[...] = mn
        vo_ref[...] = vn

    spec = pl.BlockSpec((tr, width), lambda i: (i, 0))
    d, mn, vn = pl.pallas_call(body, name=name, out_shape=[_sds((rows, width), F32)] * 3, grid=(rows // tr,),
                               in_specs=[spec] * 4, out_specs=[spec] * 3, compiler_params=_cp(("parallel",)))(w2, g2, m2, v2)
    return d.reshape(shape), mn.reshape(shape), vn.reshape(shape)


def adamw_ffn(w, m, v, red, kind, ns, name):
    shape = w.shape
    w2, m2, v2 = (t.reshape(-1, shape[-1]) for t in (w, m, v))
    rows, width = w2.shape
    c1 = 1.0 - ADAM_B1 ** ADAM_STEP
    c2 = 1.0 - ADAM_B2 ** ADAM_STEP
    tr, nb = ns // 2, 2
    gspec = pl.BlockSpec((tr, D), lambda i: (((i // nb) * 3 + kind) * nb + i % nb, 0))

    def body(w_ref, g_ref, m_ref, v_ref, go_ref, d_ref, mo_ref, vo_ref):
        gv = g_ref[...]
        mn = ADAM_B1 * m_ref[...] + (1.0 - ADAM_B1) * gv
        vn = ADAM_B2 * v_ref[...] + (1.0 - ADAM_B2) * (gv * gv)
        go_ref[...] = gv
        d_ref[...] = -ADAM_LR * ((mn / c1) / (jnp.sqrt(vn / c2) + ADAM_EPS) + ADAM_WD * w_ref[...])
        mo_ref[...] = mn
        vo_ref[...] = vn

    spec = pl.BlockSpec((tr, width), lambda i: (i, 0))
    outs = pl.pallas_call(body, name=name, out_shape=[_sds((rows, width), F32)] * 4, grid=(rows // tr,),
                          in_specs=[spec, gspec, spec, spec], out_specs=[spec] * 4,
                          compiler_params=_cp(("parallel",)))(w2, red, m2, v2)
    return tuple(t.reshape(shape) for t in outs)


def mod_mm(sc, w_mod, bias, name):
    wm = w_mod.shape[-1]
    tn = _pick(wm, (768, 512, 384, 256, 128))

    def body(a_ref, b_ref, c_ref, o_ref):
        o_ref[...] = _nn(a_ref[...], b_ref[...].astype(BF16)) + c_ref[...]

    return pl.pallas_call(
        body, name=name, out_shape=_sds((DEPTH, 32, wm), F32), grid=(DEPTH, wm // tn),
        in_specs=[pl.BlockSpec((32, D), lambda l, j: (0, 0)), pl.BlockSpec((None, D, tn), lambda l, j: (l, 0, j)),
                  pl.BlockSpec((None, 1, tn), lambda l, j: (l, 0, j))],
        out_specs=pl.BlockSpec((None, 32, tn), lambda l, j: (l, 0, j)),
        compiler_params=_cp(("parallel", "parallel")))(sc, w_mod, bias)


def wmod_dw(sc, dcol, name):
    wm = dcol.shape[-1]
    tm = 256

    def body(a_ref, b_ref, o_ref):
        o_ref[...] = _tn(a_ref[...], b_ref[...].astype(BF16))

    return pl.pallas_call(
        body, name=name, out_shape=_sds((DEPTH, D, wm), F32), grid=(DEPTH, D // tm),
        in_specs=[pl.BlockSpec((32, tm), lambda l, i: (0, i)), pl.BlockSpec((None, 32, wm), lambda l, i: (l, 0, 0))],
        out_specs=pl.BlockSpec((None, tm, wm), lambda l, i: (l, i, 0)),
        compiler_params=_cp(("parallel", "parallel")))(sc, dcol)


def cctx_dx(drow, w_mod, name):
    wm = w_mod.shape[-1]

    def body(a_ref, b_ref, o_ref):
        o_ref[...] = _nt(a_ref[...].astype(BF16), b_ref[...].astype(BF16))

    return pl.pallas_call(
        body, name=name, out_shape=_sds((DEPTH, 16, D), F32), grid=(DEPTH,),
        in_specs=[pl.BlockSpec((None, 16, wm), lambda l: (l, 0, 0)), pl.BlockSpec((None, D, wm), lambda l: (l, 0, 0))],
        out_specs=pl.BlockSpec((None, 16, D), lambda l: (l, 0, 0)), compiler_params=_cp(("parallel",), VMEM_BIG))(drow, w_mod)


HEAD_PERM = (0, 4, 1, 5, 2, 6, 3, 7)


def _rot_rows(wt):
    return jnp.concatenate([-wt[32:64], wt[0:32]], axis=0)


def _unrot_rows(g):
    return jnp.concatenate([g[32:64], -g[0:32]], axis=0)


def _heads(a, n):
    return [a[64 * i:64 * (i + 1)] for i in range(n)]


def kernel(x, c, ctx, c_ctx, w_mod, b_mod, ln_g, ln_b, ffn_w_gate, ffn_w_up, ffn_w_down, mix_ab_w_in, attn_sink, pool_w, pool_scale, mix_ab_w_out, lru_w_in, lru_conv_w, lru_conv_b, lru_wa, lru_ba, lru_wx, lru_bx, lru_lambda, lru_w_out, loss_target, m_c_ctx, m_w_mod, m_b_mod, m_ln_g, m_ln_b, m_ffn_w_gate, m_ffn_w_up, m_ffn_w_down, m_mix_ab_w_in, m_attn_sink, m_pool_w, m_pool_scale, m_mix_ab_w_out, m_lru_w_in, m_lru_conv_w, m_lru_conv_b, m_lru_wa, m_lru_ba, m_lru_wx, m_lru_bx, m_lru_lambda, m_lru_w_out, v_c_ctx, v_w_mod, v_b_mod, v_ln_g, v_ln_b, v_ffn_w_gate, v_ffn_w_up, v_ffn_w_down, v_mix_ab_w_in, v_attn_sink, v_pool_w, v_pool_scale, v_mix_ab_w_out, v_lru_w_in, v_lru_conv_w, v_lru_conv_b, v_lru_wa, v_lru_ba, v_lru_wx, v_lru_bx, v_lru_lambda, v_lru_w_out):
    n_lat, n_ctx = x.shape[1], ctx.shape[1]
    lay = Layout(n_ctx, n_lat)
    T = lay.T
    ns = ffn_w_gate.shape[-1]
    n_li, n_ai = lru_w_in.shape[-1], mix_ab_w_in.shape[-1]
    n_ao, n_lo = mix_ab_w_out.shape[1], lru_w_out.shape[1]
    wm = w_mod.shape[-1]
    dsh = ln_g.shape[-1]
    mx, my, mc = lax.axis_index("x"), lax.axis_index("y"), lax.axis_index("c")
    chip = 2 * mx + my
    me = 2 * chip + mc

    c_all = all_gather8(c, "ag8_c").reshape(16, D)
    cc = jnp.concatenate([c_all, c_ctx[None, :], jnp.zeros((15, D), F32)], axis=0)
    sc = silu_rows(cc, "silu_c")
    bias = lax.dynamic_slice(b_mod, (0, chip * wm), (DEPTH, wm)).reshape(DEPTH, 1, wm)
    modg = all_gather_chips(mod_mm(sc, w_mod, bias, "mod_mm"), "ag_mod")
    modtab = []
    for l in range(DEPTH):
        full = jnp.transpose(modg[:, l], (1, 0, 2)).reshape(32, N_CHIP * wm)
        mine = lax.dynamic_slice(full, (2 * me, 0), (2, N_CHIP * wm))
        modtab.append(jnp.concatenate([mine, full[16:17]], axis=0).reshape(3, N_MOD, D))

    small = jnp.concatenate([ln_g.reshape(6, dsh), ln_b.reshape(6, dsh), lru_conv_w[0], lru_conv_b, lru_ba[0],
                             lru_bx[0], lru_lambda[0], jnp.zeros((9, dsh), F32)], axis=0)
    small = all_gather_chips(small.reshape(2, 16, dsh), "ag_small").reshape(N_CHIP, 32, dsh)
    small = jnp.transpose(small, (1, 0, 2)).reshape(32, D)
    ln_g_f, ln_b_f = small[0:6].reshape(2, 3, D), small[6:12].reshape(2, 3, D)
    conv_w_f, conv_b_f = small[12:16], small[16:17]
    lru_vec = small[17:23]

    hh = 3 * ns // 2
    gate_t, up_t = jnp.swapaxes(ffn_w_gate, -1, -2), jnp.swapaxes(ffn_w_up, -1, -2)
    extra = [0, n_ai + n_ao, n_li + n_lo, 0]
    placed = [ffn_place(gate_t, up_t, ffn_w_down, g // 2, g % 2, f"ag_ffn{g}_place", extra[g]) for g in range(4)]
    placed[1] = place_rows(placed[1], jnp.concatenate([mix_ab_w_in[0].T, mix_ab_w_out[0]], axis=0).astype(BF16), 3 * ns,
                           "ag_mixa_place")
    placed[2] = place_rows(placed[2], jnp.concatenate([lru_w_in[0].T, lru_w_out[0]], axis=0).astype(BF16), 3 * ns,
                           "ag_mixc_place")
    placed = [p.reshape(N_CHIP, 2, p.shape[1] // 2, D) for p in placed]
    wb = [gather_placed(placed[0], "ag_ffn0"), None, None, None]
    mixw = {}

    def mixa_w():
        if "a" not in mixw:
            full = wb[1].reshape(N_CHIP, -1, D)
            ab_in_t = full[:, 3 * ns:3 * ns + n_ai].reshape(N_CHIP * n_ai, D)
            ab_out = full[:, 3 * ns + n_ai:].reshape(N_CHIP * n_ao, D)
            qh, kh = _heads(ab_in_t[Q0:K0], N_HEADS), _heads(ab_in_t[K0:V0], N_KV)
            w_ext_t = jnp.concatenate([qh[h] for h in HEAD_PERM] + [ab_in_t[K0:QR0]]
                                      + [_rot_rows(qh[h]) for h in HEAD_PERM] + [_rot_rows(t) for t in kh], axis=0)
            oh = _heads(ab_out[0:ATT_W], N_HEADS)
            mixw["a"] = (w_ext_t, jnp.concatenate([oh[h] for h in HEAD_PERM] + [ab_out[ATT_W:]], axis=0))
        return mixw["a"]

    def mixc_w():
        if "c" not in mixw:
            full = wb[2].reshape(N_CHIP, -1, D)
            mixw["c"] = (full[:, 3 * ns:3 * ns + n_li].reshape(N_CHIP * n_li, D),
                         full[:, 3 * ns + n_li:].reshape(N_CHIP * n_lo, D))
        return mixw["c"]

    t = jnp.arange(n_lat)
    inv = ROPE_THETA ** (-jnp.arange(16, dtype=F32) / 16.0)
    ang = jnp.concatenate([(t // GRID_W).astype(F32)[:, None] * inv, (t % GRID_W).astype(F32)[:, None] * inv], axis=-1)
    cos1 = jnp.concatenate([jnp.ones((n_ctx, 32), F32), jnp.cos(ang)], axis=0)
    sin1 = jnp.concatenate([jnp.zeros((n_ctx, 32), F32), jnp.sin(ang)], axis=0)
    cos_t = jnp.tile(cos1, (2, 4))
    sin_t = jnp.tile(sin1, (2, 4))
    sk = attn_sink[0]
    sink_tab = jnp.concatenate([jnp.repeat(jnp.stack([sk[:4], sk[4:]], axis=1), HEAD_DIM, axis=1),
                                jnp.zeros((4, 128), F32)], axis=0)
    pscale = pool_scale.reshape(1, POOL_W)

    h0 = jnp.concatenate([ctx, x], axis=1).reshape(T, D)
    tgt = loss_target.reshape(2 * n_lat, D)

    def lnv(l, j):
        return jnp.stack([ln_g_f[l, j], ln_b_f[l, j]])

    subs = [(0, 0, 0.5, 0), (0, 3, 1.0, 1), (0, 6, 0.5, 2), (1, 0, 0.5, 0), (1, 3, 1.0, 1), (1, 6, 0.5, 2)]

    def ffn_core(hm, l, f):
        tag = f"l{l}f{f}"
        gi = 2 * l + f
        w = wb[gi].reshape(N_CHIP, -1, D)
        if gi == 3:
            up, sl, a = ffn_up(lay, hm, w, 0, 1, ns, f"ffn_up_{tag}")
            (y,) = slab_nn_acc(lay, [a], w, [2], ns, f"ffn_down_{tag}")
            return y, dict(up=up, sl=sl, a=a, nbuf=None)
        up, sl, a, nbuf = ffn_up(lay, hm, w, 0, 1, ns, f"ffn_up_{tag}", rider=rider_gather_xy(placed[gi + 1]))
        y, nbuf = slab_nn_acc(lay, [a], w, [2], ns, f"ffn_down_{tag}", rider=rider_gather_fwd(nbuf))
        return y, dict(up=up, sl=sl, a=a, nbuf=nbuf)

    def mixa_core(hm):
        p = mm_nt(hm, mixa_w()[0], "mixa_in")
        qr, kr, vb, u = rope_fwd(lay, p, cos_t, sin_t, "rope")
        att, lse = attn_fwd(lay, qr, kr, vb, sink_tab, "attn")
        pool = pool_fwd(lay, u, pool_w[0], pscale, "pool")
        cat = jnp.concatenate([att, pool], axis=1)
        return mm_nn(cat, mixa_w()[1], "mixa_out"), dict(qr=qr, kr=kr, vb=vb, u=u, lse=lse, cat=cat)

    def mixc_core(hm):
        p = mm_nt(hm, mixc_w()[0], "mixc_in")
        uc = conv_fwd(lay, p, D, conv_w_f, conv_b_f, "conv")
        a, b = lru_coeffs(lay, uc, lru_wa[0], lru_wx[0], lru_vec, "lru_coef")
        s = lru_scan(lay, a, b, "lru_scan")
        o = lru_gate(lay, p, s, "lru_gate")
        return mm_nn(o, mixc_w()[1], "mixc_out"), dict(p=p, uc=uc, a=a, s=s, o=o)

    recs = []
    h = h0
    hm = modulate(lay, h0, modtab[0], 0, 1, "mod_first")
    for k, (l, k0, coef, j) in enumerate(subs):
        if k0 == 3:
            y, core = mixa_core(hm) if l == 0 else mixc_core(hm)
        else:
            y, core = ffn_core(hm, l, k0 // 6)
        nxt = None if k == 5 else (modtab[subs[k + 1][0]], subs[k + 1][1], subs[k + 1][1] + 1)
        nbuf = core.pop("nbuf", None)
        res = resid_ln(lay, h, y, modtab[l], k0 + 2, coef, lnv(l, j), f"ln_s{k}", nxt=nxt,
                       rider=None if nbuf is None else rider_gather_d2d(nbuf))
        if nbuf is not None:
            wb[2 * l + k0 // 6 + 1] = res[-1]
        recs.append(dict(h=h, hm=hm, y=y, xhat=res[1], rstd=res[2], **core))
        h = res[0]
        hm = res[3] if nxt is not None else None

    dout, lparts = loss_and_grad(lay, h, tgt, "loss")
    loss = lax.psum(jnp.sum(lparts), ("x", "y", "c"))

    dln = {}
    dms = {}
    mixg = {}
    ffn_red = [lax.empty((4, 2, hh, D), F32)]
    mix_red = {}
    pending = []

    def rs_sib(p):
        return None if p is None else rider_reduce_sib(p["buf"])

    def rs_add2(p, recv):
        p["q"] = add_own_half(p["buf"], recv, BF16, f"rs_add2_{p['key']}")

    def rs_join(p, arr):
        if isinstance(p["key"], int):
            return rider_join(sum_slots(p["q"], arr, f"rs_add4_{p['key']}", dst=ffn_red[0], g=p["key"]), p["key"])
        return rider_join(sum_slots(p["q"], arr, f"rs_add4_{p['key']}"))

    def rs_done(p, joined):
        if isinstance(p["key"], int):
            ffn_red[0] = joined
        else:
            mix_red[p["key"]] = joined.reshape(-1, D)

    def ffn_core_bwd(dy, r, l, f):
        tag = f"l{l}f{f}"
        gi = 2 * l + f
        w = wb[gi].reshape(N_CHIP, -1, D)
        p = pending.pop() if pending else None
        gb = lax.empty((N_CHIP, 3 * ns, D), F32)
        if p is None:
            dg, du = ffn_bwd_da(lay, dy, w, 2, r["up"], r["sl"], ns, f"ffn_da_{tag}")
            (gb,) = slab_tn(lay, r["a"], dy, gb, 2, ns, f"ffn_dwd_{tag}")
            (gb,) = slab_tn(lay, dg, r["hm"], gb, 0, ns, f"ffn_dwg_{tag}")
            (gb,) = slab_tn(lay, du, r["hm"], gb, 1, ns, f"ffn_dwu_{tag}")
            (dhm,) = slab_nn_acc(lay, [dg, du], w, [0, 1], ns, f"ffn_dh_{tag}")
        elif gi == 0:
            dg, du, recv = ffn_bwd_da(lay, dy, w, 2, r["up"], r["sl"], ns, f"ffn_da_{tag}", rider=rs_sib(p))
            rs_add2(p, recv)
            gb, arr = slab_tn(lay, r["a"], dy, gb, 2, ns, f"ffn_dwd_{tag}", rider=rider_reduce_copies(p["q"]))
            gb, joined = slab_tn(lay, dg, r["hm"], gb, 0, ns, f"ffn_dwg_{tag}", rider=rs_join(p, arr))
            rs_done(p, joined)
            (gb,) = slab_tn(lay, du, r["hm"], gb, 1, ns, f"ffn_dwu_{tag}")
            own = gb.reshape(N_CHIP, 2, hh, D)
            dhm, recv = slab_nn_acc(lay, [dg, du], w, [0, 1], ns, f"ffn_dh_{tag}", rider=rider_reduce_sib(own))
            pending.append(dict(buf=own, key=gi, recv=recv))
            return dhm
        else:
            dg, du, recv = ffn_bwd_da(lay, dy, w, 2, r["up"], r["sl"], ns, f"ffn_da_{tag}", rider=rs_sib(p))
            rs_add2(p, recv)
            gb, arr = slab_tn(lay, r["a"], dy, gb, 2, ns, f"ffn_dwd_{tag}", rider=rider_reduce_copy(p["q"], 0))
            gb, arr = slab_tn(lay, dg, r["hm"], gb, 0, ns, f"ffn_dwg_{tag}", rider=rider_reduce_copy(p["q"], 1, arr))
            gb, arr = slab_tn(lay, du, r["hm"], gb, 1, ns, f"ffn_dwu_{tag}", rider=rider_reduce_copy(p["q"], 2, arr))
            dhm, joined = slab_nn_acc(lay, [dg, du], w, [0, 1], ns, f"ffn_dh_{tag}", rider=rs_join(p, arr))
            rs_done(p, joined)
        pending.append(dict(buf=gb.reshape(N_CHIP, 2, hh, D), key=gi))
        return dhm

    def mixc_core_bwd(dy, r):
        p = pending.pop() if pending else None
        w_in_t, w_out = mixc_w()
        if p is None:
            do_c = mm_nt(dy, w_out, "mixc_out_dx")
        else:
            do_c, recv = mm_nt(dy, w_out, "mixc_out_dx", rider=rs_sib(p))
            rs_add2(p, recv)
        g_out = mm_tn(r["o"], dy, "mixc_out_dw")
        dgate, dyg = lru_gate_bwd(lay, r["p"], r["s"], do_c, "lru_gate_b")
        da_c, db_c = lru_scan_bwd(lay, r["a"], r["s"], dyg, "lru_scan_b")
        res = lru_coeffs_bwd(lay, r["uc"], lru_wa[0], lru_wx[0], lru_vec, da_c, db_c, "lru_coef_b",
                             rider=None if p is None else rider_reduce_copies(p["q"]))
        duc, mixg["wa"], mixg["wx"], mixg["vec"] = res[:4]
        du_c, mixg["cw"], mixg["cb"] = conv_bwd(lay, r["p"], D, conv_w_f, duc, "conv_b")
        dp_c = jnp.concatenate([dgate, du_c], axis=1)
        if p is None:
            g_in_t = mm_tn(dp_c, r["hm"], "mixc_in_dw")
        else:
            g_in_t, joined = mm_tn(dp_c, r["hm"], "mixc_in_dw", rider=rs_join(p, res[4]))
            rs_done(p, joined)
        buf = jnp.concatenate([g_in_t.reshape(N_CHIP, n_li, D), g_out.reshape(N_CHIP, n_lo, D)], axis=1)
        pending.append(dict(buf=buf.reshape(N_CHIP, 2, (n_li + n_lo) // 2, D), key="c"))
        return mm_nn(dp_c, w_in_t, "mixc_in_dx")

    def mixa_core_bwd(dy, r):
        p = pending.pop() if pending else None
        w_ext_t, w_out_ext = mixa_w()
        if p is None:
            dcat = mm_nt(dy, w_out_ext, "mixa_out_dx")
        else:
            dcat, recv = mm_nt(dy, w_out_ext, "mixa_out_dx", rider=rs_sib(p))
            rs_add2(p, recv)
        g_out_ext = mm_tn(r["cat"], dy, "mixa_out_dw")
        res = attn_bwd(lay, r["qr"], r["kr"], r["vb"], sink_tab, r["lse"], dcat, "attn_b",
                       rider=None if p is None else rider_reduce_copies(p["q"]))
        dqr, dkr, dv, mixg["sink"] = res[:4]
        du_a, mixg["pw"], mixg["ps"] = pool_bwd(lay, r["u"], dcat, pool_w[0], pscale, "pool_b")
        dp_a = rope_bwd(lay, dqr, dkr, dv, du_a, cos_t, sin_t, "rope_b")
        if p is None:
            g_ext_t = mm_tn(dp_a, r["hm"], "mixa_in_dw")
        else:
            g_ext_t, joined = mm_tn(dp_a, r["hm"], "mixa_in_dw", rider=rs_join(p, res[4]))
            rs_done(p, joined)
        gq, gqr = _heads(g_ext_t[Q0:K0], N_HEADS), _heads(g_ext_t[QR0:KR0], N_HEADS)
        g_q = [None] * N_HEADS
        for i, h in enumerate(HEAD_PERM):
            g_q[h] = gq[i] + _unrot_rows(gqr[i])
        gk = [a + _unrot_rows(b) for a, b in zip(_heads(g_ext_t[K0:V0], N_KV), _heads(g_ext_t[KR0:PEXT], N_KV))]
        g_ab_in_t = jnp.concatenate(g_q + gk + [g_ext_t[V0:QR0]], axis=0)
        go = _heads(g_out_ext[0:ATT_W], N_HEADS)
        g_o = [None] * N_HEADS
        for i, h in enumerate(HEAD_PERM):
            g_o[h] = go[i]
        g_ab_out = jnp.concatenate(g_o + [g_out_ext[ATT_W:]], axis=0)
        buf = jnp.concatenate([g_ab_in_t.reshape(N_CHIP, n_ai, D), g_ab_out.reshape(N_CHIP, n_ao, D)], axis=1)
        pending.append(dict(buf=buf.reshape(N_CHIP, 2, (n_ai + n_ao) // 2, D), key="a"))
        return mm_nn(dp_a, w_ext_t, "mixa_in_dx")

    l, k0, coef, j = subs[5]
    dy, dres, s1 = ln_bwd(lay, dout, recs[5]["xhat"], recs[5]["rstd"], recs[5]["y"], modtab[l], k0 + 2, coef, lnv(l, j),
                          "lnb_s5")
    for k in range(5, -1, -1):
        l, k0, coef, j = subs[k]
        r = recs[k]
        if k0 == 3:
            dhm = mixa_core_bwd(dy, r) if l == 0 else mixc_core_bwd(dy, r)
        else:
            dhm = ffn_core_bwd(dy, r, l, k0 // 6)
        dln[(l, j)] = s1
        if k > 0:
            lp, k0p, coefp, jp = subs[k - 1]
            rp = recs[k - 1]
            dy, dres, s1, s2 = modb_lnb(lay, dres, dhm, r["h"], modtab[l], k0 + 1, rp["xhat"], rp["rstd"], rp["y"],
                                        modtab[lp], k0p + 2, coefp, lnv(lp, jp), f"modb_lnb_s{k}")
        else:
            gx, s2 = mod_bwd(lay, dres, dhm, r["h"], modtab[l], k0 + 1, "modb_s0")
        dms[(l, k0)] = s2
    sums = block_sums(lay, list(dln.values()) + list(dms.values()), "block_sums")
    dln, dms = dict(zip(dln, sums[:len(dln)])), dict(zip(dms, sums[len(dln):]))
    grad_x = gx.reshape(2, n_lat, D)
    g_wa, g_wx, g_vec, g_cw, g_cb, g_sink, g_pw, g_ps = (mixg[n] for n in ("wa", "wx", "vec", "cw", "cb", "sink", "pw", "ps"))

    rows = []
    for l in range(DEPTH):
        per_k = []
        for k0, j in ((0, 0), (3, 1), (6, 2)):
            per_k += [dms[(l, k0)][:3, 0], dms[(l, k0)][:3, 1], dln[(l, j)][:3, 2]]
        rows.append(jnp.stack(per_k, axis=1).reshape(3, N_MOD * D))
    dmod_loc = jnp.concatenate(rows + [jnp.zeros((2, N_MOD * D), F32)], axis=0)
    dmod_all, g_b_mod = mod_grad_rows(all_gather8(dmod_loc, "ag8_dmod"), "dmod_rows")
    dcol = lax.dynamic_slice(dmod_all, (0, 0, chip * wm), (DEPTH, 32, wm))
    g_w_mod = wmod_dw(sc, dcol, "wmod_dw")
    g_cctx = cctx_grad(cctx_dx(dcol[:, 16:32], w_mod, "cctx_dx"), c_ctx[None, :], "cctx_grad")

    g_ln_g =jnp.stack([jnp.stack([dln[(l, j)][3, 1] for j in range(3)]) for l in range(DEPTH)])
    g_ln_b = jnp.stack([jnp.stack([dln[(l, j)][3, 0] for j in range(3)]) for l in range(DEPTH)])
    sink_row = jnp.sum(g_sink, axis=0)[:4]
    g_sink8 = jnp.concatenate([sink_row[:, 0], sink_row[:, HEAD_DIM]])
    misc = jnp.concatenate([g_sink8, jnp.sum(g_ps, axis=0).reshape(POOL_W), jnp.zeros((D - 8 - POOL_W,), F32)])
    small_g = jnp.concatenate([
        g_ln_g.reshape(6, D), g_ln_b.reshape(6, D), jnp.sum(g_cw, axis=0), jnp.sum(g_cb, axis=0), g_vec,
        misc[None, :], jnp.sum(g_pw, axis=0).reshape(64, D), g_wa.reshape(256, D), g_wx.reshape(256, D), g_cctx,
        jnp.zeros((39, D), F32)], axis=0)
    n_small = small_g.shape[0] // N_CHIP
    last = pending.pop()
    ffn_red = reduce_scatter_chips(last["buf"], f"ffn{last['key']}", wire=BF16, dst=ffn_red[0], g=last["key"],
                                   recv=last.get("recv")).reshape(12 * ns, D)
    small_red = reduce_scatter_chips(small_g.reshape(N_CHIP, 2, n_small // 2, D), "small")
    small_red = all_gather_chips(small_red, "ag_smallg").reshape(N_CHIP * n_small, D)

    ffn_kind = dict(ffn_w_gate=0, ffn_w_up=1, ffn_w_down=2)

    def cols(a):
        return lax.dynamic_slice_in_dim(a, chip * dsh, dsh, axis=a.ndim - 1)

    sr = small_red
    grads = dict(
        c_ctx=sr[600], w_mod=g_w_mod, b_mod=g_b_mod,
        ln_g=cols(sr[0:6]).reshape(2, 3, dsh), ln_b=cols(sr[6:12]).reshape(2, 3, dsh),
        mix_ab_w_in=mix_red["a"][0:n_ai][None], attn_sink=sr[23, 0:8][None], pool_w=sr[24:88].reshape(1, 4, 128, 128),
        pool_scale=sr[23, 8:8 + POOL_W][None], mix_ab_w_out=mix_red["a"][n_ai:][None],
        lru_w_in=mix_red["c"][0:n_li].T[None],
        lru_conv_w=cols(sr[12:16])[None], lru_conv_b=cols(sr[16:17]), lru_wa=sr[88:344].reshape(1, 2, 8, 128, 128),
        lru_ba=cols(sr[17:19])[None], lru_wx=sr[344:600].reshape(1, 2, 8, 128, 128), lru_bx=cols(sr[19:21])[None],
        lru_lambda=cols(sr[21:23])[None], lru_w_out=mix_red["c"][n_li:][None])
    params = dict(c_ctx=(c_ctx, m_c_ctx, v_c_ctx), w_mod=(w_mod, m_w_mod, v_w_mod), b_mod=(b_mod, m_b_mod, v_b_mod),
                  ln_g=(ln_g, m_ln_g, v_ln_g), ln_b=(ln_b, m_ln_b, v_ln_b),
                  ffn_w_gate=(ffn_w_gate, m_ffn_w_gate, v_ffn_w_gate), ffn_w_up=(ffn_w_up, m_ffn_w_up, v_ffn_w_up),
                  ffn_w_down=(ffn_w_down, m_ffn_w_down, v_ffn_w_down),
                  mix_ab_w_in=(mix_ab_w_in, m_mix_ab_w_in, v_mix_ab_w_in), attn_sink=(attn_sink, m_attn_sink, v_attn_sink),
                  pool_w=(pool_w, m_pool_w, v_pool_w), pool_scale=(pool_scale, m_pool_scale, v_pool_scale),
                  mix_ab_w_out=(mix_ab_w_out, m_mix_ab_w_out, v_mix_ab_w_out), lru_w_in=(lru_w_in, m_lru_w_in, v_lru_w_in),
                  lru_conv_w=(lru_conv_w, m_lru_conv_w, v_lru_conv_w), lru_conv_b=(lru_conv_b, m_lru_conv_b, v_lru_conv_b),
                  lru_wa=(lru_wa, m_lru_wa, v_lru_wa), lru_ba=(lru_ba, m_lru_ba, v_lru_ba), lru_wx=(lru_wx, m_lru_wx, v_lru_wx),
                  lru_bx=(lru_bx, m_lru_bx, v_lru_bx), lru_lambda=(lru_lambda, m_lru_lambda, v_lru_lambda),
                  lru_w_out=(lru_w_out, m_lru_w_out, v_lru_w_out))
    gl, dl, ml, vl = [], [], [], []
    transposed = ("ffn_w_gate", "ffn_w_up", "mix_ab_w_in")
    for name, (w, m, v) in params.items():
        if name in transposed:
            w, m, v = (jnp.swapaxes(t, -1, -2) for t in (w, m, v))
        if name in ffn_kind:
            g, d, mn, vn = adamw_ffn(w, m, v, ffn_red, ffn_kind[name], ns, f"adamw_{name}")
        else:
            g = grads[name].reshape(w.shape)
            d, mn, vn = adamw(w, g, m, v, f"adamw_{name}")
        if name in transposed:
            g, d, mn, vn = (jnp.swapaxes(t, -1, -2) for t in (g, d, mn, vn))
        gl.append(g)
        dl.append(d)
        ml.append(mn)
        vl.append(vn)
    return (loss, grad_x, *gl, *dl, *ml, *vl)
```

```python
import functools
import math

import jax
import jax.numpy as jnp
from jax import lax
from jax.experimental import pallas as pl
from jax.experimental.pallas import tpu as pltpu

F32, BF16 = jnp.float32, jnp.bfloat16
MESH = pl.DeviceIdType.MESH
ANY = pl.BlockSpec(memory_space=pl.ANY)
VMEM_SPEC = pl.BlockSpec(memory_space=pltpu.VMEM)

D = 1024
N_CHIP = 4
HEAD_DIM, N_HEADS, N_KV = 64, 8, 2
ATT_W, KV_W, POOL_W = 512, 128, 512
POOL_WINDOWS = (2, 4, 8, 16)
BLK = 128
ATT_SCALE = HEAD_DIM ** -0.5
ROPE_THETA = 10000.0
GRID_W = 64
LRU_C = 8.0
LN_EPS = 1e-5
NEG_INF = -1e30
DEPTH = 2
ALPHA = (2 * DEPTH) ** 0.25
N_MOD = 9
ADAM_LR, ADAM_B1, ADAM_B2, ADAM_EPS, ADAM_WD, ADAM_STEP = 0.001, 0.9, 0.999, 1e-08, 0.01, 10
VMEM_BIG = 48 * 1024 * 1024


def _cp(sem=None, vmem=None):
    kw = {}
    if sem is not None:
        kw["dimension_semantics"] = sem
    if vmem is not None:
        kw["vmem_limit_bytes"] = vmem
    return pltpu.CompilerParams(**kw)


def _sds(shape, dtype):
    return jax.ShapeDtypeStruct(tuple(shape), dtype)


def _pick(n, cands):
    for c in cands:
        if n % c == 0:
            return c
    return n


def _dot(a, b, dims):
    return lax.dot_general(a, b, (dims, ((), ())), preferred_element_type=F32)


def _nn(a, b):
    return _dot(a, b, ((1,), (0,)))


def _nt(a, b):
    return _dot(a, b, ((1,), (1,)))


def _tn(a, b):
    return _dot(a, b, ((0,), (0,)))


def _sigmoid(x):
    return 0.5 * jnp.tanh(0.5 * x) + 0.5


def _me():
    return lax.axis_index("x"), lax.axis_index("y"), lax.axis_index("c")


def _rcopy(src, dst, ssem, rsem, dev):
    return pltpu.make_async_remote_copy(src_ref=src, dst_ref=dst, send_sem=ssem, recv_sem=rsem,
                                        device_id=dev, device_id_type=MESH)


def all_gather8(x, name):
    def body(x_ref, o_ref, ssem, rsem, lsem):
        mx, my, mc = _me()
        me = 4 * mx + 2 * my + mc
        loc = pltpu.make_async_copy(x_ref, o_ref.at[me], lsem)
        loc.start()
        peers = []
        for m in range(1, 8):
            px = 1 - mx if (m >> 2) & 1 else mx
            py = 1 - my if (m >> 1) & 1 else my
            pc = 1 - mc if m & 1 else mc
            peers.append((px, py, pc))
        sends = [_rcopy(x_ref, o_ref.at[me], ssem.at[k], rsem.at[k], p) for k, p in enumerate(peers)]
        for cp in sends:
            cp.start()
        for k, (px, py, pc) in enumerate(peers):
            _rcopy(x_ref, o_ref.at[4 * px + 2 * py + pc], ssem.at[k], rsem.at[k], (px, py, pc)).wait_recv()
        for cp in sends:
            cp.wait_send()
        loc.wait()

    return pl.pallas_call(
        body, name=name, out_shape=_sds((8,) + x.shape, x.dtype),
        in_specs=[VMEM_SPEC], out_specs=VMEM_SPEC,
        scratch_shapes=[pltpu.SemaphoreType.DMA((7,)), pltpu.SemaphoreType.DMA((7,)), pltpu.SemaphoreType.DMA],
    )(x)


_ROW_BLOCKS = (512, 384, 352, 256, 224, 128)


def _idx(v):
    return jnp.reshape(v, (1,)).astype(jnp.int32)


def place_slab(shard, name):
    _, h, w = shard.shape
    th = _pick(h, _ROW_BLOCKS)

    def body(s_ref, x_ref, o_ref):
        del s_ref
        o_ref[...] = x_ref[...]

    return pl.pallas_call(
        body, name=name, out_shape=_sds((N_CHIP,) + shard.shape, shard.dtype),
        grid_spec=pltpu.PrefetchScalarGridSpec(
            num_scalar_prefetch=1, grid=(2, h // th),
            in_specs=[pl.BlockSpec((None, th, w), lambda k, r, s: (k, r, 0))],
            out_specs=pl.BlockSpec((None, None, th, w), lambda k, r, s: (s[0], k, r, 0))),
    )(_idx(2 * lax.axis_index("x") + lax.axis_index("y")), shard)


def place_rows(buf, rows, r0, name):
    e, w = rows.shape
    tb = 64

    def body(s_ref, x_ref, b_ref, o_ref):
        del s_ref, b_ref
        o_ref[...] = x_ref[...]

    return pl.pallas_call(
        body, name=name, out_shape=_sds(buf.shape, buf.dtype),
        grid_spec=pltpu.PrefetchScalarGridSpec(
            num_scalar_prefetch=1, grid=(e // tb,),
            in_specs=[pl.BlockSpec((tb, w), lambda j, s: (j, 0)), ANY],
            out_specs=pl.BlockSpec((None, tb, w), lambda j, s: (s[0], r0 // tb + j, 0))),
        input_output_aliases={2: 0},
    )(_idx(2 * lax.axis_index("x") + lax.axis_index("y")), rows, buf)


def ffn_place(w_gate_t, w_up_t, w_down, l, f, name, extra=0):
    ns = w_down.shape[-2]
    tr, nb = ns // 2, 2

    def body(s_ref, g_ref, u_ref, d_ref, o_ref):
        del s_ref
        k = pl.program_id(0)

        @pl.when(k == 0)
        def _():
            o_ref[...] = g_ref[...].astype(BF16)

        @pl.when(k == 1)
        def _():
            o_ref[...] = u_ref[...].astype(BF16)

        @pl.when(k == 2)
        def _():
            o_ref[...] = d_ref[...].astype(BF16)

    def spec(q):
        return pl.BlockSpec((None, None, tr, D), lambda k, j, s: (l, f, jnp.where(k == q, j, 0), 0))

    return pl.pallas_call(
        body, name=name, out_shape=_sds((N_CHIP, 3 * ns + extra, D), BF16),
        grid_spec=pltpu.PrefetchScalarGridSpec(
            num_scalar_prefetch=1, grid=(3, nb), in_specs=[spec(0), spec(1), spec(2)],
            out_specs=pl.BlockSpec((None, tr, D), lambda k, j, s: (s[0], k * nb + j, 0))),
    )(_idx(2 * lax.axis_index("x") + lax.axis_index("y")), w_gate_t, w_up_t, w_down)


def all_gather_chips(shard, name):
    return gather_placed(place_slab(shard, name + "_place"), name)


def gather_placed(full, name):
    h = full.shape[2]
    lo, hi = pl.ds(0, h // 2), pl.ds(h // 2, h - h // 2)

    def body(x_ref, o_ref, ssem, rsem):
        del x_ref
        mx, my, mc = _me()
        s, xs, ys, ds = 2 * mx + my, 2 * (1 - mx) + my, 2 * mx + (1 - my), 2 * (1 - mx) + (1 - my)
        xn, yn, sib = (1 - mx, my, mc), (mx, 1 - my, mc), (mx, my, 1 - mc)

        def cp(k, src, dst, dev):
            return _rcopy(src, dst, ssem.at[k], rsem.at[k], dev)

        own = o_ref.at[s, mc]
        sent = [cp(0, own, own, xn), cp(1, own, own, yn)]
        for c in sent:
            c.start()
        cp(0, own, o_ref.at[xs, mc], xn).wait_recv()
        sent += [cp(2, o_ref.at[xs, mc, lo], o_ref.at[xs, mc, lo], yn), cp(4, o_ref.at[xs, mc], o_ref.at[xs, mc], sib)]
        sent[-2].start()
        sent[-1].start()
        cp(1, own, o_ref.at[ys, mc], yn).wait_recv()
        sent += [cp(3, o_ref.at[ys, mc, hi], o_ref.at[ys, mc, hi], xn), cp(5, o_ref.at[ys, mc], o_ref.at[ys, mc], sib)]
        sent[-2].start()
        sent[-1].start()
        cp(2, own, o_ref.at[ds, mc, lo], yn).wait_recv()
        cp(3, own, o_ref.at[ds, mc, hi], xn).wait_recv()
        sent.append(cp(6, o_ref.at[ds, mc], o_ref.at[ds, mc], sib))
        sent[-1].start()
        for k, slot in ((4, xs), (5, ys), (6, ds)):
            cp(k, own, o_ref.at[slot, 1 - mc], sib).wait_recv()
        for c in sent:
            c.wait_send()

    return pl.pallas_call(
        body, name=name, out_shape=_sds(full.shape, full.dtype), in_specs=[ANY], out_specs=ANY,
        input_output_aliases={0: 0},
        scratch_shapes=[pltpu.SemaphoreType.DMA((7,)), pltpu.SemaphoreType.DMA((7,))],
    )(full)


def sibling_send_other_half(buf, name):
    def body(x_ref, o_ref, ssem, rsem):
        mx, my, mc = _me()
        sib = (mx, my, 1 - mc)
        cps = [_rcopy(x_ref.at[k, 1 - mc], o_ref.at[k], ssem.at[k], rsem.at[k], sib) for k in range(N_CHIP)]
        for cp in cps:
            cp.start()
        for cp in cps:
            cp.wait_recv()
        for cp in cps:
            cp.wait_send()

    n, _, h, w = buf.shape
    return pl.pallas_call(
        body, name=name, out_shape=_sds((n, h, w), buf.dtype), in_specs=[ANY], out_specs=ANY,
        scratch_shapes=[pltpu.SemaphoreType.DMA((N_CHIP,)), pltpu.SemaphoreType.DMA((N_CHIP,))],
    )(buf)


def chips_all_to_all(q, name):
    def body(x_ref, o_ref, ssem, rsem):
        mx, my, mc = _me()
        s = 2 * mx + my
        chips = [(1 - mx, my), (mx, 1 - my), (1 - mx, 1 - my)]
        cps = [_rcopy(x_ref.at[2 * px + py], o_ref.at[s], ssem.at[j], rsem.at[j], (px, py, mc))
               for j, (px, py) in enumerate(chips)]
        for cp in cps:
            cp.start()
        for j, (px, py) in enumerate(chips):
            ps = 2 * px + py
            _rcopy(x_ref.at[ps], o_ref.at[ps], ssem.at[j], rsem.at[j], (px, py, mc)).wait_recv()
        for cp in cps:
            cp.wait_send()

    return pl.pallas_call(
        body, name=name, out_shape=_sds(q.shape, q.dtype), in_specs=[ANY], out_specs=ANY,
        scratch_shapes=[pltpu.SemaphoreType.DMA((3,)), pltpu.SemaphoreType.DMA((3,))],
    )(q)


def sibling_join_halves(both, name, g=None):
    def body(x_ref, o_ref, ssem, rsem):
        del x_ref
        mx, my, mc = _me()
        sib = (mx, my, 1 - mc)
        o = o_ref if g is None else o_ref.at[g]
        cp = _rcopy(o.at[mc], o.at[mc], ssem, rsem, sib)
        cp.start()
        _rcopy(o.at[1 - mc], o.at[1 - mc], ssem, rsem, sib).wait_recv()
        cp.wait_send()

    return pl.pallas_call(
        body, name=name, out_shape=_sds(both.shape, both.dtype), in_specs=[ANY], out_specs=ANY,
        input_output_aliases={0: 0}, scratch_shapes=[pltpu.SemaphoreType.DMA, pltpu.SemaphoreType.DMA],
    )(both)


def add_own_half(buf, recv, wire, name):
    n, _, h, w = buf.shape
    th = _pick(h, _ROW_BLOCKS)

    def body(c_ref, a_ref, b_ref, o_ref):
        del c_ref
        o_ref[...] = (a_ref[...] + b_ref[...]).astype(o_ref.dtype)

    return pl.pallas_call(
        body, name=name, out_shape=_sds((n, h, w), wire),
        grid_spec=pltpu.PrefetchScalarGridSpec(
            num_scalar_prefetch=1, grid=(n, h // th),
            in_specs=[pl.BlockSpec((None, None, th, w), lambda k, r, c: (k, c[0], r, 0)),
                      pl.BlockSpec((None, th, w), lambda k, r, c: (k, r, 0))],
            out_specs=pl.BlockSpec((None, th, w), lambda k, r, c: (k, r, 0))),
    )(_idx(lax.axis_index("c")), buf, recv)


def sum_slots(q, r, name, dst=None, g=None):
    n, h, w = r.shape
    th = _pick(h, _ROW_BLOCKS)

    def body(i_ref, q_ref, r1, r2, r3, *rest):
        del i_ref
        rest[-1][...] = ((q_ref[...].astype(F32) + r1[...].astype(F32)) + r2[...].astype(F32)) + r3[...].astype(F32)

    def slot(d):
        return lambda i, ix: ((ix[0] + d) % N_CHIP, i, 0)

    idx = jnp.stack([2 * lax.axis_index("x") + lax.axis_index("y"), lax.axis_index("c")]).astype(jnp.int32)
    in_specs = [pl.BlockSpec((None, th, w), slot(d)) for d in (0, 1, 2, 3)]
    if dst is None:
        return pl.pallas_call(
            body, name=name, out_shape=_sds((2, h, w), F32),
            grid_spec=pltpu.PrefetchScalarGridSpec(
                num_scalar_prefetch=1, grid=(h // th,), in_specs=in_specs,
                out_specs=pl.BlockSpec((None, th, w), lambda i, ix: (ix[1], i, 0))),
        )(idx, q, r, r, r)
    return pl.pallas_call(
        body, name=name, out_shape=_sds(dst.shape, F32),
        grid_spec=pltpu.PrefetchScalarGridSpec(
            num_scalar_prefetch=1, grid=(h // th,), in_specs=in_specs + [ANY],
            out_specs=pl.BlockSpec((None, None, th, w), lambda i, ix: (g, ix[1], i, 0))),
        input_output_aliases={5: 0},
    )(idx, q, r, r, r, dst)


def reduce_scatter_chips(buf, tag, wire=F32, dst=None, g=None, recv=None):
    if recv is None:
        recv = sibling_send_other_half(buf, f"rs_sib_{tag}")
    q = add_own_half(buf, recv, wire, f"rs_add2_{tag}")
    r = chips_all_to_all(q, f"rs_a2a_{tag}")
    red = sum_slots(q, r, f"rs_add4_{tag}", dst=dst, g=g)
    return sibling_join_halves(red, f"rs_join_{tag}", g=g)


class Layout:
    def __init__(self, n_ctx, n_lat):
        self.C, self.L = n_ctx, n_lat
        self.PS = n_ctx + n_lat
        self.T = 2 * self.PS
        self.tr = _pick(math.gcd(n_ctx, n_lat), (256, 128))
        self.bps = self.PS // self.tr
        self.cb = n_ctx // self.tr
        self.nblk = self.T // self.tr
        self.tm = _pick(self.T, (1152, 768, 512, 256, 128))
        self.tm2 = _pick(self.T, (2304, 1152, 768, 512, 256, 128))
        self.tc = _pick(self.T, (512, 256, 128))

    def seg(self, i):
        return jnp.where(i % self.bps < self.cb, 2, i // self.bps)


def rowwise(lay, name, fn, rows, segs=(), vecs=(), outs=(), sums=(), rider=None):
    tr, nblk = lay.tr, lay.nblk
    n_r, n_s, n_v, n_o = len(rows), len(segs), len(vecs), len(outs)
    lat_only = any(o[2:] for o in outs) or any(a.shape[0] != lay.T for a in rows)
    nsub = 1 if lat_only else (3 if nblk % 3 == 0 else 2 if nblk % 2 == 0 else 1)
    tb = tr * nsub

    def body(*refs):
        ins = refs[:n_r + n_s + n_v]
        ors = refs[n_r + n_s + n_v:]
        for sub in range(nsub):
            rs = slice(sub * tr, (sub + 1) * tr)
            seg = lay.seg(pl.program_id(0) * nsub + sub)
            vals = [r[rs, :] for r in ins[:n_r]] + [r[seg] for r in ins[n_r:n_r + n_s]] + [r[...] for r in ins[n_r + n_s:]]
            res = fn(*vals)
            for k in range(n_o):
                ors[k][rs, :] = res[k].astype(ors[k].dtype)
            for k in range(len(sums)):
                ors[n_o + k][sub] = res[n_o + k]

    def all_rows(i):
        return (i, 0)

    def lat_rows(i):
        return ((i // lay.bps) * (lay.bps - lay.cb) + jnp.maximum(i % lay.bps - lay.cb, 0), 0)

    in_specs = [pl.BlockSpec((tb, a.shape[1]), all_rows if a.shape[0] == lay.T else lat_rows) for a in rows]
    in_specs += [pl.BlockSpec(a.shape, lambda i: (0, 0, 0)) for a in segs]
    in_specs += [pl.BlockSpec(a.shape, lambda i: (0, 0)) for a in vecs]
    out_shape = [_sds((2 * lay.L if o[2:] else lay.T, o[0]), o[1]) for o in outs]
    out_shape += [_sds((nblk, r, w), F32) for r, w in sums]
    out_specs = [pl.BlockSpec((tb, o[0]), lat_rows if o[2:] else all_rows) for o in outs]
    out_specs += [pl.BlockSpec((nsub, r, w), lambda i: (i, 0, 0)) for r, w in sums]
    sem = "arbitrary" if any(o[2:] for o in outs) else "parallel"
    if rider is None:
        return pl.pallas_call(body, name=name, out_shape=out_shape, grid=(nblk // nsub,), in_specs=in_specs,
                              out_specs=out_specs, compiler_params=_cp((sem,), VMEM_BIG))(*rows, *segs, *vecs)
    return _host_call(body, rider, name, (nblk // nsub,), in_specs, out_specs, out_shape, (*rows, *segs, *vecs), (sem,),
                      n_r + n_s + n_v, n_o + len(sums))


def modulate(lay, h, mod, k_shift, k_scale, name):
    def fn(hb, m):
        return (hb * (1.0 + m[k_scale:k_scale + 1]) + m[k_shift:k_shift + 1],)
    return rowwise(lay, name, fn, [h], segs=[mod], outs=[(D, BF16)])[0]


def resid_ln(lay, h, y, mod, k_gate, coef, lnv, name, nxt=None, rider=None):
    def fn(hb, yb, m, *rest):
        ln = rest[-1]
        z = ALPHA * hb + (coef * m[k_gate:k_gate + 1]) * yb
        mu = jnp.mean(z, axis=-1, keepdims=True)
        zc = z - mu
        var = jnp.mean(zc * zc, axis=-1, keepdims=True)
        rstd = lax.rsqrt(var + LN_EPS)
        xhat = zc * rstd
        out = xhat * ln[0:1] + ln[1:2]
        if nxt is None:
            return out, xhat, rstd
        mn = rest[0]
        return out, xhat, rstd, out * (1.0 + mn[nxt[2]:nxt[2] + 1]) + mn[nxt[1]:nxt[1] + 1]
    segs = [mod] if nxt is None else [mod, nxt[0]]
    outs = [(D, F32), (D, F32), (1, F32)] + ([] if nxt is None else [(D, BF16)])
    return rowwise(lay, name, fn, [h, y], segs=segs, vecs=[lnv], outs=outs, rider=rider)


def _ln_bwd_math(do, xh, rs, yb, gate, coef, ln):
    dxh = do * ln[0:1]
    m1 = jnp.mean(dxh, axis=-1, keepdims=True)
    m2 = jnp.mean(dxh * xh, axis=-1, keepdims=True)
    dz = rs * (dxh - m1 - xh * m2)
    s = jnp.concatenate([jnp.sum(do, axis=0, keepdims=True), jnp.sum(do * xh, axis=0, keepdims=True),
                         jnp.sum(coef * dz * yb, axis=0, keepdims=True)], axis=0)
    return (coef * gate) * dz, ALPHA * dz, s


def _mod_bwd_math(dr, dm, hb, scale):
    s = jnp.concatenate([jnp.sum(dm, axis=0, keepdims=True), jnp.sum(dm * hb, axis=0, keepdims=True)], axis=0)
    return dr + dm * (1.0 + scale), s


def ln_bwd(lay, dout, xhat, rstd, y, mod, k_gate, coef, lnv, name):
    def fn(do, xh, rs, yb, m, ln):
        return _ln_bwd_math(do, xh, rs, yb, m[k_gate:k_gate + 1], coef, ln)
    return rowwise(lay, name, fn, [dout, xhat, rstd, y], segs=[mod], vecs=[lnv],
                   outs=[(D, BF16), (D, F32)], sums=[(3, D)])


def mod_bwd(lay, dres, dhm, h, mod, k_scale, name, rider=None):
    def fn(dr, dm, hb, m):
        return _mod_bwd_math(dr, dm, hb, m[k_scale:k_scale + 1])
    return rowwise(lay, name, fn, [dres, dhm, h], segs=[mod], outs=[(D, F32, "lat")], sums=[(2, D)], rider=rider)


def modb_lnb(lay, dres, dhm, h, mod, k_scale, xhat, rstd, y, mod_p, k_gate, coef, lnv, name, rider=None):
    def fn(dr, dm, hb, xh, rs, yb, m, mp, ln):
        dh, s2 = _mod_bwd_math(dr, dm, hb, m[k_scale:k_scale + 1])
        dy, dres_p, s1 = _ln_bwd_math(dh, xh, rs, yb, mp[k_gate:k_gate + 1], coef, ln)
        return dy, dres_p, s1, s2
    return rowwise(lay, name, fn, [dres, dhm, h, xhat, rstd, y], segs=[mod, mod_p], vecs=[lnv],
                   outs=[(D, BF16), (D, F32)], sums=[(3, D), (2, D)], rider=rider)


def block_sums(lay, parts_list, name):
    n = len(parts_list)

    def body(*refs):
        for p_ref, o_ref in zip(refs[:n], refs[n:]):
            acc = [None, None, None]
            for i in range(lay.nblk):
                sg = 2 if i % lay.bps < lay.cb else i // lay.bps
                acc[sg] = p_ref[i] if acc[sg] is None else acc[sg] + p_ref[i]
            for k in range(3):
                o_ref[k] = acc[k]
            o_ref[3] = (acc[0] + acc[1]) + acc[2]

    return pl.pallas_call(body, name=name, out_shape=[_sds((4,) + p.shape[1:], F32) for p in parts_list],
                          in_specs=[VMEM_SPEC] * n, out_specs=[VMEM_SPEC] * n)(*parts_list)


def mm_nn(a, b, name, out_dtype=F32):
    m, k = a.shape
    n = b.shape[1]
    tm = _pick(m, (1152, 768, 512, 256, 128, 64, 32, 16, 8))
    tn = _pick(n, (1024, 768, 640, 512, 384, 256, 128))

    def body(a_ref, b_ref, o_ref):
        o_ref[...] = _nn(a_ref[...].astype(BF16), b_ref[...].astype(BF16)).astype(o_ref.dtype)

    return pl.pallas_call(body, name=name, out_shape=_sds((m, n), out_dtype), grid=(m // tm, n // tn),
                          in_specs=[pl.BlockSpec((tm, k), lambda i, j: (i, 0)), pl.BlockSpec((k, tn), lambda i, j: (0, j))],
                          out_specs=pl.BlockSpec((tm, tn), lambda i, j: (i, j)),
                          compiler_params=_cp(("parallel", "parallel"), VMEM_BIG))(a, b)


def mm_nt(a, b, name, out_dtype=F32, rider=None):
    m, k = a.shape
    n = b.shape[0]
    tm = _pick(m, (1152, 768, 512, 256, 128, 64, 32, 16, 8))
    tn = _pick(n, (1024, 768, 640, 512, 384, 256, 128))

    def body(a_ref, b_ref, o_ref):
        o_ref[...] = _nt(a_ref[...].astype(BF16), b_ref[...].astype(BF16)).astype(o_ref.dtype)

    res = _host_call(body, rider, name, (m // tm, n // tn),
                     [pl.BlockSpec((tm, k), lambda i, j: (i, 0)), pl.BlockSpec((tn, k), lambda i, j: (j, 0))],
                     [pl.BlockSpec((tm, tn), lambda i, j: (i, j))], [_sds((m, n), out_dtype)], (a, b),
                     ("parallel", "parallel"), 2, 1)
    return res[0] if rider is None else res


def mm_tn(a, b, name, rider=None):
    t, m = a.shape
    n = b.shape[1]
    tk = _pick(t, (1152, 768, 512, 256, 128, 64, 32, 16))
    tm = _pick(m, (512, 384, 256, 128))

    def body(a_ref, b_ref, o_ref):
        @pl.when(pl.program_id(1) == 0)
        def _():
            o_ref[...] = jnp.zeros_like(o_ref)
        o_ref[...] += _tn(a_ref[...].astype(BF16), b_ref[...].astype(BF16))

    res = _host_call(body, rider, name, (m // tm, t // tk),
                     [pl.BlockSpec((tk, tm), lambda i, k: (k, i)), pl.BlockSpec((tk, n), lambda i, k: (k, 0))],
                     [pl.BlockSpec((tm, n), lambda i, k: (i, 0))], [_sds((m, n), F32)], (a, b),
                     ("parallel", "arbitrary"), 2, 1)
    return res[0] if rider is None else res


class Rider:
    def __init__(self, ins, outs, aliases, nsem, start, wait):
        self.ins, self.outs, self.aliases, self.nsem, self.start, self.wait = ins, outs, aliases, nsem, start, wait


def _chips_of(mx, my):
    return [(1 - mx, my), (mx, 1 - my), (1 - mx, 1 - my)]


def rider_gather_d2d(buf):
    def start(ins, outs, ssem, rsem):
        o = outs[0]
        mx, my, mc = _me()
        for j, (px, py) in enumerate(_chips_of(mx, my)):
            ps = 2 * px + py
            _rcopy(o.at[ps, mc], o.at[ps, mc], ssem.at[j], rsem.at[j], (mx, my, 1 - mc)).start()

    def wait(ins, outs, ssem, rsem):
        o = outs[0]
        mx, my, mc = _me()
        sib = (mx, my, 1 - mc)
        for j, (px, py) in enumerate(_chips_of(mx, my)):
            ps = 2 * px + py
            _rcopy(o.at[ps, 1 - mc], o.at[ps, 1 - mc], ssem.at[j], rsem.at[j], sib).wait_recv()
        for j, (px, py) in enumerate(_chips_of(mx, my)):
            ps = 2 * px + py
            _rcopy(o.at[ps, mc], o.at[ps, mc], ssem.at[j], rsem.at[j], sib).wait_send()

    return Rider([buf], [_sds(buf.shape, buf.dtype)], {0: 0}, 3, start, wait)


def rider_reduce_sib(buf):
    n, _, h, w = buf.shape

    def start(ins, outs, ssem, rsem):
        mx, my, mc = _me()
        for k in range(N_CHIP):
            _rcopy(ins[0].at[k, 1 - mc], outs[0].at[k], ssem.at[k], rsem.at[k], (mx, my, 1 - mc)).start()

    def wait(ins, outs, ssem, rsem):
        mx, my, mc = _me()
        for k in range(N_CHIP):
            _rcopy(ins[0].at[k, 1 - mc], outs[0].at[k], ssem.at[k], rsem.at[k], (mx, my, 1 - mc)).wait_recv()
        for k in range(N_CHIP):
            _rcopy(ins[0].at[k, 1 - mc], outs[0].at[k], ssem.at[k], rsem.at[k], (mx, my, 1 - mc)).wait_send()

    return Rider([buf], [_sds((n, h, w), buf.dtype)], {}, N_CHIP, start, wait)


def rider_gather_xy(buf):
    def peers():
        mx, my, mc = _me()
        return 2 * mx + my, mc, [(1 - mx, my), (mx, 1 - my)]

    def start(ins, outs, ssem, rsem):
        o = outs[0]
        s, mc, nb = peers()
        for j, (px, py) in enumerate(nb):
            _rcopy(o.at[s, mc], o.at[s, mc], ssem.at[j], rsem.at[j], (px, py, mc)).start()

    def wait(ins, outs, ssem, rsem):
        o = outs[0]
        s, mc, nb = peers()
        for j, (px, py) in enumerate(nb):
            _rcopy(o.at[2 * px + py, mc], o.at[2 * px + py, mc], ssem.at[j], rsem.at[j], (px, py, mc)).wait_recv()
        for j, (px, py) in enumerate(nb):
            _rcopy(o.at[s, mc], o.at[s, mc], ssem.at[j], rsem.at[j], (px, py, mc)).wait_send()

    return Rider([buf], [_sds(buf.shape, buf.dtype)], {0: 0}, 2, start, wait)


def rider_gather_fwd(buf):
    h2 = buf.shape[2] // 2
    lo, hi = pl.ds(0, h2), pl.ds(h2, buf.shape[2] - h2)

    def start(ins, outs, ssem, rsem):
        o = outs[0]
        mx, my, mc = _me()
        xs, ys = 2 * (1 - mx) + my, 2 * mx + (1 - my)
        _rcopy(o.at[xs, mc, lo], o.at[xs, mc, lo], ssem.at[0], rsem.at[0], (mx, 1 - my, mc)).start()
        _rcopy(o.at[ys, mc, hi], o.at[ys, mc, hi], ssem.at[1], rsem.at[1], (1 - mx, my, mc)).start()

    def wait(ins, outs, ssem, rsem):
        o = outs[0]
        mx, my, mc = _me()
        xs, ys, ds = 2 * (1 - mx) + my, 2 * mx + (1 - my), 2 * (1 - mx) + (1 - my)
        _rcopy(o.at[ds, mc, lo], o.at[ds, mc, lo], ssem.at[0], rsem.at[0], (mx, 1 - my, mc)).wait_recv()
        _rcopy(o.at[ds, mc, hi], o.at[ds, mc, hi], ssem.at[1], rsem.at[1], (1 - mx, my, mc)).wait_recv()
        _rcopy(o.at[xs, mc, lo], o.at[xs, mc, lo], ssem.at[0], rsem.at[0], (mx, 1 - my, mc)).wait_send()
        _rcopy(o.at[ys, mc, hi], o.at[ys, mc, hi], ssem.at[1], rsem.at[1], (1 - mx, my, mc)).wait_send()

    return Rider([buf], [_sds(buf.shape, buf.dtype)], {0: 0}, 2, start, wait)


def rider_reduce_copy(q, j, r=None):
    def peer():
        mx, my, mc = _me()
        px, py = _chips_of(mx, my)[j]
        return 2 * mx + my, 2 * px + py, (px, py, mc)

    def start(ins, outs, ssem, rsem):
        s, ps, dev = peer()
        _rcopy(ins[0].at[ps], outs[0].at[s], ssem.at[0], rsem.at[0], dev).start()

    def wait(ins, outs, ssem, rsem):
        s, ps, dev = peer()
        _rcopy(ins[0].at[ps], outs[0].at[ps], ssem.at[0], rsem.at[0], dev).wait_recv()
        _rcopy(ins[0].at[ps], outs[0].at[s], ssem.at[0], rsem.at[0], dev).wait_send()

    if r is None:
        return Rider([q], [_sds(q.shape, q.dtype)], {}, 1, start, wait)
    return Rider([q, r], [_sds(q.shape, q.dtype)], {1: 0}, 1, start, wait)


def rider_reduce_copies(q):
    def start(ins, outs, ssem, rsem):
        mx, my, mc = _me()
        s = 2 * mx + my
        for j, (px, py) in enumerate(_chips_of(mx, my)):
            _rcopy(ins[0].at[2 * px + py], outs[0].at[s], ssem.at[j], rsem.at[j], (px, py, mc)).start()

    def wait(ins, outs, ssem, rsem):
        mx, my, mc = _me()
        s = 2 * mx + my
        for j, (px, py) in enumerate(_chips_of(mx, my)):
            ps = 2 * px + py
            _rcopy(ins[0].at[ps], outs[0].at[ps], ssem.at[j], rsem.at[j], (px, py, mc)).wait_recv()
        for j, (px, py) in enumerate(_chips_of(mx, my)):
            _rcopy(ins[0].at[2 * px + py], outs[0].at[s], ssem.at[j], rsem.at[j], (px, py, mc)).wait_send()

    return Rider([q], [_sds(q.shape, q.dtype)], {}, 3, start, wait)


def rider_join(buf, g=None):
    def start(ins, outs, ssem, rsem):
        o = outs[0] if g is None else outs[0].at[g]
        mx, my, mc = _me()
        _rcopy(o.at[mc], o.at[mc], ssem.at[0], rsem.at[0], (mx, my, 1 - mc)).start()

    def wait(ins, outs, ssem, rsem):
        o = outs[0] if g is None else outs[0].at[g]
        mx, my, mc = _me()
        _rcopy(o.at[1 - mc], o.at[1 - mc], ssem.at[0], rsem.at[0], (mx, my, 1 - mc)).wait_recv()
        _rcopy(o.at[mc], o.at[mc], ssem.at[0], rsem.at[0], (mx, my, 1 - mc)).wait_send()

    return Rider([buf], [_sds(buf.shape, buf.dtype)], {0: 0}, 1, start, wait)


def _host_call(body, rider, name, grid, in_specs, out_specs, out_shape, operands, sem, n_in, n_out, aliases=None):
    aliases = dict(aliases or {})
    if rider is None:
        return pl.pallas_call(body, name=name, out_shape=out_shape, grid=grid, in_specs=in_specs, out_specs=out_specs,
                              input_output_aliases=aliases, compiler_params=_cp(sem, VMEM_BIG))(*operands)
    n_ri, n_ro = len(rider.ins), len(rider.outs)
    aliases.update({n_in + a: n_out + b for a, b in rider.aliases.items()})

    def hosted(*refs):
        ins, r_in = refs[:n_in], refs[n_in:n_in + n_ri]
        outs, r_out = refs[n_in + n_ri:n_in + n_ri + n_out], refs[n_in + n_ri + n_out:n_in + n_ri + n_out + n_ro]
        ssem, rsem = refs[-2], refs[-1]
        first = functools.reduce(lambda a, b: a & b, [pl.program_id(k) == 0 for k in range(len(grid))])
        last = functools.reduce(lambda a, b: a & b, [pl.program_id(k) == grid[k] - 1 for k in range(len(grid))])

        @pl.when(first)
        def _():
            rider.start(r_in, r_out, ssem, rsem)
        body(*ins, *outs)

        @pl.when(last)
        def _():
            rider.wait(r_in, r_out, ssem, rsem)

    return pl.pallas_call(
        hosted, name=name, out_shape=list(out_shape) + list(rider.outs), grid=grid,
        in_specs=list(in_specs) + [ANY] * n_ri, out_specs=list(out_specs) + [ANY] * n_ro,
        input_output_aliases=aliases,
        scratch_shapes=[pltpu.SemaphoreType.DMA((rider.nsem,)), pltpu.SemaphoreType.DMA((rider.nsem,))],
        compiler_params=_cp(("arbitrary",) * len(grid), VMEM_BIG))(*operands, *rider.ins)


def ffn_up(lay, hm, wbuf, ig, iu, ns, name, rider=None):
    tm = lay.tm

    def body(h_ref, wg_ref, wu_ref, up_ref, sl_ref, a_ref):
        hb = h_ref[...]
        g = _nt(hb, wg_ref[0])
        u = _nt(hb, wu_ref[0])
        sg = _sigmoid(g)
        sl = g * sg
        up_ref[0] = (u * (sg + sl * (1.0 - sg))).astype(BF16)
        sl_ref[0] = sl.astype(BF16)
        a_ref[0] = (sl * u).astype(BF16)

    spec_o = pl.BlockSpec((1, tm, ns), lambda s, i: (s, i, 0))
    return _host_call(
        body, rider, name, (N_CHIP, lay.T // tm),
        [pl.BlockSpec((tm, D), lambda s, i: (i, 0)), pl.BlockSpec((1, ns, D), lambda s, i: (s, ig, 0)),
         pl.BlockSpec((1, ns, D), lambda s, i: (s, iu, 0))],
        [spec_o] * 3, [_sds((N_CHIP, lay.T, ns), BF16)] * 3, (hm, wbuf, wbuf), ("parallel", "parallel"), 3, 3)


def slab_nn_acc(lay, zs, wbuf, idxs, ns, name, rider=None):
    tm = lay.tm2
    npair = len(zs)

    def body(*refs):
        o_ref = refs[-1]

        @pl.when(pl.program_id(1) == 0)
        def _():
            o_ref[...] = jnp.zeros_like(o_ref)
        acc = _nn(refs[0][0], refs[npair][0])
        for p in range(1, npair):
            acc += _nn(refs[p][0], refs[npair + p][0])
        o_ref[...] += acc

    in_specs = [pl.BlockSpec((1, tm, ns), lambda i, s: (s, i, 0)) for _ in zs]
    in_specs += [pl.BlockSpec((1, ns, D), functools.partial(lambda i, s, q: (s, q, 0), q=q)) for q in idxs]
    return _host_call(body, rider, name, (lay.T // tm, N_CHIP), in_specs, [pl.BlockSpec((tm, D), lambda i, s: (i, 0))],
                      [_sds((lay.T, D), F32)], (*zs, *([wbuf] * npair)), ("parallel", "arbitrary"), 2 * npair, 1)


def ffn_bwd_da(lay, dy, wbuf, idn, up, sl, ns, name, rider=None):
    tm = lay.tm

    def body(dy_ref, wd_ref, up_ref, sl_ref, dg_ref, du_ref):
        da = _nt(dy_ref[...], wd_ref[0])
        dg_ref[0] = (da * up_ref[0].astype(F32)).astype(BF16)
        du_ref[0] = (da * sl_ref[0].astype(F32)).astype(BF16)

    spec_z = pl.BlockSpec((1, tm, ns), lambda s, i: (s, i, 0))
    return _host_call(
        body, rider, name, (N_CHIP, lay.T // tm),
        [pl.BlockSpec((tm, D), lambda s, i: (i, 0)), pl.BlockSpec((1, ns, D), lambda s, i: (s, idn, 0)), spec_z, spec_z],
        [spec_z] * 2, [_sds((N_CHIP, lay.T, ns), BF16)] * 2, (dy, wbuf, up, sl), ("parallel", "parallel"), 4, 2)


def slab_tn(lay, z, x, gbuf, idx, ns, name, rider=None):
    tk = _pick(lay.T, (768, 512, 256, 128))

    def body(z_ref, x_ref, g_in, o_ref):
        del g_in

        @pl.when(pl.program_id(0) == 0)
        def _():
            o_ref[...] = jnp.zeros_like(o_ref)
        xv = x_ref[...]
        for s in range(N_CHIP):
            o_ref[s] += _tn(z_ref[s], xv)

    return _host_call(
        body, rider, name, (lay.T // tk,),
        [pl.BlockSpec((N_CHIP, tk, ns), lambda k: (0, k, 0)), pl.BlockSpec((tk, D), lambda k: (k, 0)), ANY],
        [pl.BlockSpec((N_CHIP, ns, D), lambda k: (0, idx, 0))], [_sds(gbuf.shape, F32)], (z, x, gbuf),
        ("arbitrary",), 3, 1, aliases={2: 0})


Q0, K0, V0, U0, QR0, KR0, PEXT = 0, 512, 640, 768, 1280, 1792, 1920


def rope_fwd(lay, p, cos, sin, name):
    def fn(pb, cs, sn):
        cs4 = jnp.concatenate([cs] * 4, axis=1)
        sn4 = jnp.concatenate([sn] * 4, axis=1)
        qr = pb[:, Q0:K0] * cs4 + pb[:, QR0:KR0] * sn4
        kr = pb[:, K0:V0] * cs + pb[:, KR0:PEXT] * sn
        return qr, kr, pb[:, V0:U0], pb[:, U0:QR0]
    return rowwise(lay, name, fn, [p, cos, sin], outs=[(ATT_W, BF16), (KV_W, BF16), (KV_W, BF16), (POOL_W, F32)])


def rope_bwd(lay, dqr, dkr, dv, du, cos, sin, name):
    def fn(dq, dk, dvb, dub, cs, sn):
        cs4 = jnp.concatenate([cs] * 4, axis=1)
        sn4 = jnp.concatenate([sn] * 4, axis=1)
        return (jnp.concatenate([dq * cs4, dk * cs, dvb, dub, dq * sn4, dk * sn], axis=1),)
    return rowwise(lay, name, fn, [dqr, dkr, dv, du, cos, sin], outs=[(PEXT, BF16)])[0]


def _attn_specs(lay):
    nbs, cbk, lbk = lay.PS // BLK, lay.C // BLK, lay.L // BLK

    def kv_map(j):
        return lambda s, n: (s * nbs + cbk + jnp.clip(n - cbk + j - 1, 0, lbk - 1), 0)

    win = [pl.BlockSpec((BLK, KV_W), kv_map(j)) for j in range(3)]
    ctx = pl.BlockSpec((lay.C, KV_W), lambda s, n: (s * (lay.PS // lay.C), 0))
    return nbs, cbk, lbk, win, ctx


def _attn_masks(n, cbk, lbk):
    row = lax.broadcasted_iota(jnp.int32, (BLK, BLK), 0)
    col = lax.broadcasted_iota(jnp.int32, (BLK, BLK), 1)
    m = n - cbk
    lat = n >= cbk
    valid = [lat & (m >= 1) & (col >= row), lat & (col >= 0), lat & (m <= lbk - 2) & (col <= row)]
    lane_lo = lax.broadcasted_iota(jnp.int32, (BLK, 2 * HEAD_DIM), 1) < HEAD_DIM
    return valid, lane_lo


def attn_fwd(lay, qr, kr, vb, sink_tab, name):
    nbs, cbk, lbk, win, ctx = _attn_specs(lay)

    def body(q_ref, k0, k1, k2, kc_ref, v0, v1, v2, vc_ref, sk_ref, o_ref, l_ref):
        n = pl.program_id(1)
        valid, lane_lo = _attn_masks(n, cbk, lbk)
        valid4 = [jnp.concatenate([v] * 4, axis=0) for v in valid]
        ks = [k0[...], k1[...], k2[...]]
        vs = [v0[...], v1[...], v2[...]]
        kc, vc = kc_ref[...], vc_ref[...]
        q2s = [q_ref[:, p * 128:(p + 1) * 128] for p in range(4)]
        outs, lses = [], []
        for hh in range(2):
            sel = lane_lo == (hh == 0)
            qm = jnp.concatenate([jnp.where(sel, q2, jnp.zeros_like(q2)) for q2 in q2s], axis=0)
            sk = jnp.concatenate([jnp.broadcast_to(sk_ref[p:p + 1, hh * HEAD_DIM:hh * HEAD_DIM + 1], (BLK, 1))
                                  for p in range(4)], axis=0)
            sw = [jnp.where(valid4[j], _nt(qm, ks[j]) * ATT_SCALE, NEG_INF) for j in range(3)]
            sc = _nt(qm, kc) * ATT_SCALE
            mx = jnp.maximum(jnp.maximum(jnp.maximum(sw[0].max(-1, keepdims=True), sw[1].max(-1, keepdims=True)),
                                         jnp.maximum(sw[2].max(-1, keepdims=True), sc.max(-1, keepdims=True))), sk)
            ew = [jnp.exp(s - mx) for s in sw]
            ec = jnp.exp(sc - mx)
            den = ew[0].sum(-1, keepdims=True) + ew[1].sum(-1, keepdims=True) + ew[2].sum(-1, keepdims=True)
            den = den + ec.sum(-1, keepdims=True) + jnp.exp(sk - mx)
            o = _nn((ec / den).astype(BF16), vc)
            for j in range(3):
                o += _nn((ew[j] / den).astype(BF16), vs[j])
            outs.append(o)
            lses.append(mx + jnp.log(den))
        for p in range(4):
            rows = slice(p * BLK, (p + 1) * BLK)
            o_ref[:, p * 128:(p + 1) * 128] = jnp.where(lane_lo, outs[0][rows], outs[1][rows]).astype(o_ref.dtype)
            l_ref[:, p * 128:(p + 1) * 128] = jnp.where(lane_lo, jnp.broadcast_to(lses[0][rows], (BLK, 128)),
                                                        jnp.broadcast_to(lses[1][rows], (BLK, 128)))

    qspec = pl.BlockSpec((BLK, ATT_W), lambda s, n: (s * nbs + n, 0))
    return pl.pallas_call(
        body, name=name, out_shape=[_sds((lay.T, ATT_W), BF16), _sds((lay.T, ATT_W), F32)], grid=(2, nbs),
        in_specs=[qspec] + win + [ctx] + win + [ctx] + [pl.BlockSpec((8, 128), lambda s, n: (0, 0))],
        out_specs=[qspec, qspec], compiler_params=_cp(("parallel", "parallel")))(qr, kr, kr, kr, kr, vb, vb, vb, vb, sink_tab)


def attn_bwd(lay, qr, kr, vb, sink_tab, lse, datt, name, rider=None):
    nbs, cbk, lbk, win, ctx = _attn_specs(lay)
    C, PS = lay.C, lay.PS

    def body(q_ref, k0, k1, k2, kc_ref, v0, v1, v2, vc_ref, sk_ref, l_ref, do_ref, dq_ref, dk_ref, dv_ref, ds_ref):
        n = pl.program_id(1)
        valid, lane_lo = _attn_masks(n, cbk, lbk)

        @pl.when(n == 0)
        def _():
            dk_ref[...] = jnp.zeros_like(dk_ref)
            dv_ref[...] = jnp.zeros_like(dv_ref)
            ds_ref[...] = jnp.zeros_like(ds_ref)

        ks = [k0[...], k1[...], k2[...], kc_ref[...]]
        vs = [v0[...], v1[...], v2[...], vc_ref[...]]
        valid4 = [jnp.concatenate([v] * 4, axis=0) for v in valid]
        dks = [jnp.zeros((BLK, KV_W), F32)] * 3 + [jnp.zeros((C, KV_W), F32)]
        dvs = list(dks)
        q2s = [q_ref[:, p * 128:(p + 1) * 128] for p in range(4)]
        do2s = [do_ref[:, p * 128:(p + 1) * 128].astype(BF16) for p in range(4)]
        lse2s = [l_ref[:, p * 128:(p + 1) * 128] for p in range(4)]
        dq_h, dd_h = [], []
        for hh in range(2):
            sel = lane_lo == (hh == 0)
            qm = jnp.concatenate([jnp.where(sel, q2, jnp.zeros_like(q2)) for q2 in q2s], axis=0)
            dom = jnp.concatenate([jnp.where(sel, d2, jnp.zeros_like(d2)) for d2 in do2s], axis=0)
            lse_h = jnp.concatenate([l2[:, hh * HEAD_DIM:hh * HEAD_DIM + 1] for l2 in lse2s], axis=0)
            ps, dps = [], []
            for j in range(4):
                s = _nt(qm, ks[j]) * ATT_SCALE
                if j < 3:
                    s = jnp.where(valid4[j], s, NEG_INF)
                ps.append(jnp.exp(s - lse_h))
                dps.append(_nt(dom, vs[j]))
            dd = (ps[0] * dps[0]).sum(-1, keepdims=True) + (ps[1] * dps[1]).sum(-1, keepdims=True)
            dd = dd + (ps[2] * dps[2]).sum(-1, keepdims=True) + (ps[3] * dps[3]).sum(-1, keepdims=True)
            dq = jnp.zeros((4 * BLK, 128), F32)
            for j in range(4):
                dsb = (ps[j] * (dps[j] - dd) * ATT_SCALE).astype(BF16)
                dq += _nn(dsb, ks[j])
                dks[j] = dks[j] + _tn(dsb, qm)
                dvs[j] = dvs[j] + _tn(ps[j].astype(BF16), dom)
            dq_h.append(dq)
            dd_h.append(dd)
        for p in range(4):
            sl = slice(p * 128, (p + 1) * 128)
            rows = slice(p * BLK, (p + 1) * BLK)
            dq_ref[:, sl] = jnp.where(lane_lo, dq_h[0][rows], dq_h[1][rows])
            dd2 = jnp.where(lane_lo, jnp.broadcast_to(dd_h[0][rows], (BLK, 128)), jnp.broadcast_to(dd_h[1][rows], (BLK, 128)))
            psink = jnp.exp(sk_ref[p:p + 1, :] - lse2s[p])
            ds_ref[0, p:p + 1, :] += -jnp.sum(psink * dd2, axis=0, keepdims=True)
        dk_ref[0:C, :] += dks[3]
        dv_ref[0:C, :] += dvs[3]
        for j in range(3):
            r0 = pl.multiple_of((cbk + jnp.clip(n - cbk + j - 1, 0, lbk - 1)) * BLK, BLK)
            dk_ref[pl.ds(r0, BLK), :] += dks[j]
            dv_ref[pl.ds(r0, BLK), :] += dvs[j]

    qspec = pl.BlockSpec((BLK, ATT_W), lambda s, n: (s * nbs + n, 0))
    kvout = pl.BlockSpec((PS, KV_W), lambda s, n: (s, 0))
    return _host_call(
        body, rider, name, (2, nbs),
        [qspec] + win + [ctx] + win + [ctx] + [pl.BlockSpec((8, 128), lambda s, n: (0, 0)), qspec, qspec],
        [qspec, kvout, kvout, pl.BlockSpec((1, 8, 128), lambda s, n: (s, 0, 0))],
        [_sds((lay.T, ATT_W), F32), _sds((lay.T, KV_W), F32), _sds((lay.T, KV_W), F32), _sds((2, 8, 128), F32)],
        (qr, kr, kr, kr, kr, vb, vb, vb, vb, sink_tab, lse, datt), ("parallel", "arbitrary"), 12, 4)


def _winsum(x, r):
    n = x.shape[0]
    t = lax.broadcasted_iota(jnp.int32, x.shape, 0)
    acc = x
    for o in range(1, r + 1):
        acc = acc + jnp.where(t >= o, pltpu.roll(x, o, 0), 0.0) + jnp.where(t < n - o, pltpu.roll(x, n - o, 0), 0.0)
    return acc


def _wincount(n, r):
    t = lax.broadcasted_iota(jnp.int32, (n, 128), 0)
    return (jnp.minimum(t + r, n - 1) - jnp.maximum(t - r, 0) + 1).astype(F32)


def pool_fwd(lay, u, w_pool, scale, name):
    segs = [(0, lay.C), (lay.C, lay.L)]

    def body(u_ref, w_ref, s_ref, o_ref):
        for r0, n in segs:
            for g, wd in enumerate(POOL_WINDOWS):
                sl = slice(g * 128, (g + 1) * 128)
                x = u_ref[r0:r0 + n, sl]
                d = _winsum(x, wd // 2) / _wincount(n, wd // 2) - x
                y = _nn(d.astype(BF16), w_ref[g].astype(BF16)) * s_ref[:, sl]
                o_ref[r0:r0 + n, sl] = y.astype(o_ref.dtype)

    spec = pl.BlockSpec((lay.PS, POOL_W), lambda s: (s, 0))
    return pl.pallas_call(
        body, name=name, out_shape=_sds((lay.T, POOL_W), BF16), grid=(2,),
        in_specs=[spec, pl.BlockSpec(w_pool.shape, lambda s: (0, 0, 0)), pl.BlockSpec((1, POOL_W), lambda s: (0, 0))],
        out_specs=spec, compiler_params=_cp(("parallel",), VMEM_BIG))(u, w_pool, scale)


def pool_bwd(lay, u, dcat, w_pool, scale, name):
    segs = [(0, lay.C), (lay.C, lay.L)]

    def body(u_ref, dp_ref, w_ref, s_ref, du_ref, dw_ref, dsc_ref):
        for g, wd in enumerate(POOL_WINDOWS):
            sl = slice(g * 128, (g + 1) * 128)
            wb = w_ref[g].astype(BF16)
            dw = jnp.zeros((128, 128), F32)
            dsc = jnp.zeros((1, 128), F32)
            for r0, n in segs:
                x = u_ref[r0:r0 + n, sl]
                cnt = _wincount(n, wd // 2)
                d = (_winsum(x, wd // 2) / cnt - x).astype(BF16)
                dp = dp_ref[r0:r0 + n, sl]
                dsc += jnp.sum(_nn(d, wb) * dp, axis=0, keepdims=True)
                dyp = (dp * s_ref[:, sl]).astype(BF16)
                dw += _tn(d, dyp)
                dd = _nt(dyp, wb)
                du_ref[r0:r0 + n, sl] = _winsum(dd / cnt, wd // 2) - dd
            dw_ref[0, g] = dw
            dsc_ref[0, :, sl] = dsc

    spec = pl.BlockSpec((lay.PS, POOL_W), lambda s: (s, 0))
    return pl.pallas_call(
        body, name=name,
        out_shape=[_sds((lay.T, POOL_W), F32), _sds((2, 4, 128, 128), F32), _sds((2, 1, POOL_W), F32)], grid=(2,),
        in_specs=[spec, pl.BlockSpec((lay.PS, POOL_W), lambda s: (s, 1)), pl.BlockSpec(w_pool.shape, lambda s: (0, 0, 0)),
                  pl.BlockSpec((1, POOL_W), lambda s: (0, 0))],
        out_specs=[spec, pl.BlockSpec((1, 4, 128, 128), lambda s: (s, 0, 0, 0)), pl.BlockSpec((1, 1, POOL_W), lambda s: (s, 0, 0))],
        compiler_params=_cp(("parallel",), VMEM_BIG))(u, dcat, w_pool, scale)


CONV_OFFS = (-1, 0, 1, 2)
CW = 256


def _shift_rows(x, o):
    if o == 0:
        return x
    n = x.shape[0]
    t = lax.broadcasted_iota(jnp.int32, x.shape, 0)
    if o < 0:
        return jnp.where(t >= -o, pltpu.roll(x, -o, 0), 0.0)
    return jnp.where(t < n - o, pltpu.roll(x, n - o, 0), 0.0)


def conv_fwd(lay, p, col0, w, b, name):
    segs = [(0, lay.C), (lay.C, lay.L)]
    cb0 = col0 // CW

    def body(x_ref, w_ref, b_ref, o_ref):
        for r0, n in segs:
            x = x_ref[r0:r0 + n, :]
            y = jnp.broadcast_to(b_ref[...], x.shape)
            for k, o in enumerate(CONV_OFFS):
                y = y + _shift_rows(x, o) * w_ref[k:k + 1, :]
            o_ref[r0:r0 + n, :] = y

    return pl.pallas_call(
        body, name=name, out_shape=_sds((lay.T, D), F32), grid=(2, D // CW),
        in_specs=[pl.BlockSpec((lay.PS, CW), lambda s, j: (s, cb0 + j)), pl.BlockSpec((4, CW), lambda s, j: (0, j)),
                  pl.BlockSpec((1, CW), lambda s, j: (0, j))],
        out_specs=pl.BlockSpec((lay.PS, CW), lambda s, j: (s, j)),
        compiler_params=_cp(("parallel", "parallel")))(p, w, b)


def conv_bwd(lay, p, col0, w, duc, name):
    segs = [(0, lay.C), (lay.C, lay.L)]
    cb0 = col0 // CW

    def body(x_ref, w_ref, g_ref, du_ref, dw_ref, db_ref):
        dws = [jnp.zeros((1, CW), F32)] * 4
        db = jnp.zeros((1, CW), F32)
        for r0, n in segs:
            x = x_ref[r0:r0 + n, :]
            g = g_ref[r0:r0 + n, :]
            du = jnp.zeros_like(g)
            for k, o in enumerate(CONV_OFFS):
                du = du + _shift_rows(g, -o) * w_ref[k:k + 1, :]
                dws[k] = dws[k] + jnp.sum(g * _shift_rows(x, o), axis=0, keepdims=True)
            db = db + jnp.sum(g, axis=0, keepdims=True)
            du_ref[r0:r0 + n, :] = du.astype(du_ref.dtype)
        dw_ref[0] = jnp.concatenate(dws, axis=0)
        db_ref[0] = db

    return pl.pallas_call(
        body, name=name, out_shape=[_sds((lay.T, D), BF16), _sds((2, 4, D), F32), _sds((2, 1, D), F32)], grid=(2, D // CW),
        in_specs=[pl.BlockSpec((lay.PS, CW), lambda s, j: (s, cb0 + j)), pl.BlockSpec((4, CW), lambda s, j: (0, j)),
                  pl.BlockSpec((lay.PS, CW), lambda s, j: (s, j))],
        out_specs=[pl.BlockSpec((lay.PS, CW), lambda s, j: (s, j)), pl.BlockSpec((1, 4, CW), lambda s, j: (s, 0, j)),
                   pl.BlockSpec((1, 1, CW), lambda s, j: (s, 0, j))],
        compiler_params=_cp(("parallel", "parallel")))(p, w, duc)


def _softplus_neg(lam):
    z = -lam
    w = jnp.exp(-jnp.abs(z))
    log1p = jnp.where(w < 1e-2, w * (1.0 - w * (0.5 - w / 3.0)), jnp.log(1.0 + w))
    return jnp.maximum(z, 0.0) + log1p, -_sigmoid(z)


def _neg_expm1(x):
    series = -x * (1.0 + x * (0.5 + x * (1.0 / 6.0 + x * (1.0 / 24.0 + x * (1.0 / 120.0)))))
    return jnp.where(x > -0.05, series, 1.0 - jnp.exp(x))


def _lru_gates(x, xb, wa, wx, ba, bx, lam):
    r = _sigmoid(_nn(xb, wa.astype(BF16)) + ba)
    gi = _sigmoid(_nn(xb, wx.astype(BF16)) + bx)
    sp, dsp = _softplus_neg(lam)
    la = -LRU_C * r * sp
    a = jnp.exp(la)
    sq = jnp.sqrt(_neg_expm1(2.0 * la))
    return r, gi, sp, dsp, a, sq


def lru_coeffs(lay, uc, wa, wx, vec, name):
    tr = lay.tc

    def body(x_ref, wa_ref, wx_ref, v_ref, a_ref, b_ref):
        for h in range(8):
            sl = slice(h * 128, (h + 1) * 128)
            x = x_ref[:, sl]
            xb = x.astype(BF16)
            for d in range(2):
                _, gi, _, _, a, sq = _lru_gates(x, xb, wa_ref[d, h], wx_ref[d, h], v_ref[d:d + 1, sl],
                                                v_ref[2 + d:3 + d, sl], v_ref[4 + d:5 + d, sl])
                a_ref[d, h] = a
                b_ref[d, h] = sq * (gi * x)

    wspec = pl.BlockSpec((2, 8, 128, 128), lambda i: (0, 0, 0, 0))
    ospec = pl.BlockSpec((2, 8, tr, 128), lambda i: (0, 0, i, 0))
    return pl.pallas_call(
        body, name=name, out_shape=[_sds((2, 8, lay.T, 128), F32)] * 2, grid=(lay.T // tr,),
        in_specs=[pl.BlockSpec((tr, D), lambda i: (i, 0)), wspec, wspec, pl.BlockSpec((6, D), lambda i: (0, 0))],
        out_specs=[ospec, ospec], compiler_params=_cp(("parallel",), VMEM_BIG))(uc, wa, wx, vec)


def lru_coeffs_bwd(lay, uc, wa, wx, vec, da, db, name, rider=None):
    tr = lay.tc

    def body(x_ref, wa_ref, wx_ref, v_ref, da_ref, db_ref, dx_ref, dwa_ref, dwx_ref, dv_ref):
        @pl.when(pl.program_id(0) == 0)
        def _():
            dwa_ref[...] = jnp.zeros_like(dwa_ref)
            dwx_ref[...] = jnp.zeros_like(dwx_ref)
            dv_ref[...] = jnp.zeros_like(dv_ref)

        for h in range(8):
            sl = slice(h * 128, (h + 1) * 128)
            x = x_ref[:, sl]
            xb = x.astype(BF16)
            dx = jnp.zeros_like(x)
            for d in range(2):
                wab, wxb = wa_ref[d, h].astype(BF16), wx_ref[d, h].astype(BF16)
                r, gi, sp, dsp, a, sq = _lru_gates(x, xb, wa_ref[d, h], wx_ref[d, h], v_ref[d:d + 1, sl],
                                                   v_ref[2 + d:3 + d, sl], v_ref[4 + d:5 + d, sl])
                dbv, dav = db_ref[d, h], da_ref[d, h]
                t1 = dbv * sq
                dgi = t1 * x
                dx = dx + t1 * gi
                dla = dav * a - (dbv * gi * x) * (a * a) / sq
                dr = dla * (-LRU_C * sp)
                dlam = jnp.sum(dla * (-LRU_C * r), axis=0, keepdims=True) * dsp
                dpa = dr * r * (1.0 - r)
                dpx = dgi * gi * (1.0 - gi)
                dpab, dpxb = dpa.astype(BF16), dpx.astype(BF16)
                dwa_ref[d, h] += _tn(xb, dpab)
                dwx_ref[d, h] += _tn(xb, dpxb)
                dx = dx + _nt(dpab, wab) + _nt(dpxb, wxb)
                dv_ref[d:d + 1, sl] += jnp.sum(dpa, axis=0, keepdims=True)
                dv_ref[2 + d:3 + d, sl] += jnp.sum(dpx, axis=0, keepdims=True)
                dv_ref[4 + d:5 + d, sl] += dlam
            dx_ref[:, sl] = dx

    wspec = pl.BlockSpec((2, 8, 128, 128), lambda i: (0, 0, 0, 0))
    gspec = pl.BlockSpec((2, 8, tr, 128), lambda i: (0, 0, i, 0))
    vspec = pl.BlockSpec((6, D), lambda i: (0, 0))
    xspec = pl.BlockSpec((tr, D), lambda i: (i, 0))
    return _host_call(
        body, rider, name, (lay.T // tr,), [xspec, wspec, wspec, vspec, gspec, gspec], [xspec, wspec, wspec, vspec],
        [_sds((lay.T, D), F32), _sds((2, 8, 128, 128), F32), _sds((2, 8, 128, 128), F32), _sds((6, D), F32)],
        (uc, wa, wx, vec, da, db), ("arbitrary",), 6, 4)


GB = 2
SCAN_UNROLL = 8


def _tile_scan(a, b, up):
    t = lax.broadcasted_iota(jnp.int32, a.shape, 0)
    for d in (1, 2, 4):
        sh = 8 - d if up else d
        m = (t < 8 - d) if up else (t >= d)
        a_prev, b_prev = pltpu.roll(a, sh, 0), pltpu.roll(b, sh, 0)
        b = jnp.where(m, a * b_prev + b, b)
        a = jnp.where(m, a * a_prev, a)
    return a, b


def lru_scan(lay, a, b, name):
    segs = [(0, lay.C), (lay.C, lay.L)]

    def body(a_ref, b_ref, s_ref):
        for d in range(2):
            rev = d == 1
            state = tuple(jnp.zeros((1, 128), F32) for _ in range(GB))
            for base, n in segs:
                nt = n // 8

                def step(j, c, base=base, nt=nt, rev=rev, d=d):
                    c = list(c)
                    for u in range(SCAN_UNROLL):
                        jj = j * SCAN_UNROLL + u
                        r0 = pl.multiple_of(base + 8 * ((nt - 1 - jj) if rev else jj), 8)
                        for g in range(GB):
                            at, bt = _tile_scan(a_ref[d, g, pl.ds(r0, 8), :], b_ref[d, g, pl.ds(r0, 8), :], rev)
                            h = at * c[g] + bt
                            s_ref[d, g, pl.ds(r0, 8), :] = h
                            c[g] = h[0:1] if rev else h[7:8]
                    return tuple(c)

                state = lax.fori_loop(0, nt // SCAN_UNROLL, step, state)

    spec = pl.BlockSpec((2, GB, lay.PS, 128), lambda s, hb: (0, hb, s, 0))
    return pl.pallas_call(
        body, name=name, out_shape=_sds((2, 8, lay.T, 128), F32), grid=(2, 8 // GB),
        in_specs=[spec, spec], out_specs=spec, compiler_params=_cp(("parallel", "parallel"), VMEM_BIG))(a, b)


def lru_scan_bwd(lay, a, s, dy, name):
    segs = [(0, lay.C), (lay.C, lay.L)]
    C, PS = lay.C, lay.PS

    def body(a_ref, s_ref, g_ref, da_ref, db_ref):
        t = lax.broadcasted_iota(jnp.int32, (8, 128), 0)
        for d in range(2):
            rev = d == 1
            carry = tuple(jnp.zeros((1, 128), F32) for _ in range(GB))
            for si in (1, 0):
                base, n = segs[si]
                nt = n // 8

                def step(j, c, base=base, nt=nt, rev=rev, d=d):
                    c = list(c)
                    for u in range(SCAN_UNROLL):
                        jj = j * SCAN_UNROLL + u
                        r0 = pl.multiple_of(base + 8 * (jj if rev else (nt - 1 - jj)), 8)
                        if rev:
                            rn = pl.multiple_of(jnp.where(r0 == PS - 8, 0, r0 + 8), 8)
                            nb_zero = r0 == C - 8
                        else:
                            rn = pl.multiple_of(jnp.maximum(r0 - 8, 0), 8)
                            nb_zero = r0 == 0
                        for g in range(GB):
                            av = a_ref[d, g, pl.ds(r0, 8), :]
                            gv = g_ref[g, pl.ds(r0, 8), :]
                            sv = s_ref[d, g, pl.ds(r0, 8), :]
                            nbt = s_ref[d, g, pl.ds(rn, 8), :]
                            at, bt = _tile_scan(av, av * gv, not rev)
                            m = at * c[g] + bt
                            if rev:
                                m_next = jnp.where(t >= 1, pltpu.roll(m, 1, 0), c[g])
                                nb = jnp.where(nb_zero, 0.0, nbt[0:1])
                                h_prev = jnp.where(t < 7, pltpu.roll(sv, 7, 0), nb)
                                c[g] = m[7:8]
                            else:
                                m_next = jnp.where(t < 7, pltpu.roll(m, 7, 0), c[g])
                                nb = jnp.where(nb_zero, 0.0, nbt[7:8])
                                h_prev = jnp.where(t >= 1, pltpu.roll(sv, 1, 0), nb)
                                c[g] = m[0:1]
                            lam = gv + m_next
                            db_ref[d, g, pl.ds(r0, 8), :] = lam
                            da_ref[d, g, pl.ds(r0, 8), :] = lam * h_prev
                    return tuple(c)

                carry = lax.fori_loop(0, nt // SCAN_UNROLL, step, carry)

    spec = pl.BlockSpec((2, GB, lay.PS, 128), lambda s, hb: (0, hb, s, 0))
    return pl.pallas_call(
        body, name=name, out_shape=[_sds((2, 8, lay.T, 128), F32)] * 2, grid=(2, 8 // GB),
        in_specs=[spec, spec, pl.BlockSpec((GB, lay.PS, 128), lambda s, hb: (hb, s, 0))],
        out_specs=[spec, spec], compiler_params=_cp(("parallel", "parallel"), VMEM_BIG))(a, s, dy)


def _gelu(x):
    k = math.sqrt(2.0 / math.pi)
    t = jnp.tanh(k * (x + 0.044715 * x * x * x))
    return 0.5 * x * (1.0 + t), 0.5 * (1.0 + t) + 0.5 * x * (1.0 - t * t) * k * (1.0 + 3 * 0.044715 * x * x)


def lru_gate(lay, p, s, name):
    tr = lay.tr

    def body(g_ref, s_ref, o_ref):
        for h in range(8):
            sl = slice(h * 128, (h + 1) * 128)
            o_ref[:, sl] = (_gelu(g_ref[:, sl])[0] * (s_ref[0, h] + s_ref[1, h])).astype(o_ref.dtype)

    return pl.pallas_call(
        body, name=name, out_shape=_sds((lay.T, D), BF16), grid=(lay.nblk,),
        in_specs=[pl.BlockSpec((tr, D), lambda i: (i, 0)), pl.BlockSpec((2, 8, tr, 128), lambda i: (0, 0, i, 0))],
        out_specs=pl.BlockSpec((tr, D), lambda i: (i, 0)), compiler_params=_cp(("parallel",)))(p, s)


def lru_gate_bwd(lay, p, s, do, name):
    tr = lay.tr

    def body(g_ref, s_ref, do_ref, dg_ref, dy_ref):
        for h in range(8):
            sl = slice(h * 128, (h + 1) * 128)
            ge, dge = _gelu(g_ref[:, sl])
            dov = do_ref[:, sl]
            dg_ref[:, sl] = (dov * (s_ref[0, h] + s_ref[1, h]) * dge).astype(dg_ref.dtype)
            dy_ref[h] = dov * ge

    xspec = pl.BlockSpec((tr, D), lambda i: (i, 0))
    return pl.pallas_call(
        body, name=name, out_shape=[_sds((lay.T, D), BF16), _sds((8, lay.T, 128), F32)], grid=(lay.nblk,),
        in_specs=[xspec, pl.BlockSpec((2, 8, tr, 128), lambda i: (0, 0, i, 0)), xspec],
        out_specs=[xspec, pl.BlockSpec((8, tr, 128), lambda i: (0, i, 0))],
        compiler_params=_cp(("parallel",)))(p, s, do)


def silu_rows(x, name):
    def body(x_ref, o_ref):
        v = x_ref[...]
        o_ref[...] = (v * _sigmoid(v)).astype(o_ref.dtype)
    return pl.pallas_call(body, name=name, out_shape=_sds(x.shape, BF16), in_specs=[VMEM_SPEC], out_specs=VMEM_SPEC)(x)


def mod_grad_rows(gath, name):
    w = gath.shape[-1]

    def body(g_ref, dm_ref, db_ref):
        dm_ref[...] = jnp.zeros_like(dm_ref)
        for l in range(2):
            ctx = g_ref[0, 3 * l + 2:3 * l + 3, :]
            tot = g_ref[0, 3 * l:3 * l + 1, :] + g_ref[0, 3 * l + 1:3 * l + 2, :]
            for k in range(8):
                dm_ref[l, 2 * k:2 * k + 2, :] = g_ref[k, 3 * l:3 * l + 2, :]
                if k:
                    ctx = ctx + g_ref[k, 3 * l + 2:3 * l + 3, :]
                    tot = tot + (g_ref[k, 3 * l:3 * l + 1, :] + g_ref[k, 3 * l + 1:3 * l + 2, :])
            dm_ref[l, 16:17, :] = ctx
            db_ref[l:l + 1, :] = tot + ctx

    return pl.pallas_call(body, name=name, out_shape=[_sds((2, 32, w), F32), _sds((2, w), F32)],
                          in_specs=[VMEM_SPEC], out_specs=[VMEM_SPEC, VMEM_SPEC])(gath)


def cctx_grad(p, c_ctx, name):
    def body(a_ref, c_ref, o_ref):
        cv = c_ref[...]
        sg = _sigmoid(cv)
        o_ref[...] = 0.5 * (a_ref[0, 0:1, :] + a_ref[1, 0:1, :]) * (sg * (1.0 + cv * (1.0 - sg)))
    return pl.pallas_call(body, name=name, out_shape=_sds((1, D), F32), in_specs=[VMEM_SPEC] * 2,
                          out_specs=VMEM_SPEC)(p, c_ctx)


def loss_and_grad(lay, h, tgt, name):
    def fn(hb, tb):
        lat = (pl.program_id(0) % lay.bps) >= lay.cb
        e = jnp.where(lat, hb - tb, 0.0)
        return e * (1.0 / D), jnp.sum(e * e, axis=0, keepdims=True) * (0.5 / D)
    return rowwise(lay, name, fn, [h, tgt], outs=[(D, F32)], sums=[(1, D)])


def adamw(w, g, m, v, name):
    shape = w.shape
    w2, g2, m2, v2 = (t.reshape(-1, shape[-1]) for t in (w, g, m, v))
    rows, width = w2.shape
    tr = 256 if rows % 256 == 0 else rows
    c1 = 1.0 - ADAM_B1 ** ADAM_STEP
    c2 = 1.0 - ADAM_B2 ** ADAM_STEP

    def body(w_ref, g_ref, m_ref, v_ref, d_ref, mo_ref, vo_ref):
        gv = g_ref[...]
        mn = ADAM_B1 * m_ref[...] + (1.0 - ADAM_B1) * gv
        vn = ADAM_B2 * v_ref[...] + (1.0 - ADAM_B2) * (gv * gv)
        d_ref[...] = -ADAM_LR * ((mn / c1) / (jnp.sqrt(vn / c2) + ADAM_EPS) + ADAM_WD * w_ref[...])
        mo_ref[...] = mn
        vo_ref[...] = vn

    spec = pl.BlockSpec((tr, width), lambda i: (i, 0))
    d, mn, vn = pl.pallas_call(body, name=name, out_shape=[_sds((rows, width), F32)] * 3, grid=(rows // tr,),
                               in_specs=[spec] * 4, out_specs=[spec] * 3, compiler_params=_cp(("parallel",)))(w2, g2, m2, v2)
    return d.reshape(shape), mn.reshape(shape), vn.reshape(shape)


def adamw_ffn(w, m, v, red, kind, ns, name):
    shape = w.shape
    w2, m2, v2 = (t.reshape(-1, shape[-1]) for t in (w, m, v))
    rows, width = w2.shape
    c1 = 1.0 - ADAM_B1 ** ADAM_STEP
    c2 = 1.0 - ADAM_B2 ** ADAM_STEP
    tr, nb = ns // 2, 2
    gspec = pl.BlockSpec((tr, D), lambda i: (((i // nb) * 3 + kind) * nb + i % nb, 0))

    def body(w_ref, g_ref, m_ref, v_ref, go_ref, d_ref, mo_ref, vo_ref):
        gv = g_ref[...]
        mn = ADAM_B1 * m_ref[...] + (1.0 - ADAM_B1) * gv
        vn = ADAM_B2 * v_ref[...] + (1.0 - ADAM_B2) * (gv * gv)
        go_ref[...] = gv
        d_ref[...] = -ADAM_LR * ((mn / c1) / (jnp.sqrt(vn / c2) + ADAM_EPS) + ADAM_WD * w_ref[...])
        mo_ref[...] = mn
        vo_ref[...] = vn

    spec = pl.BlockSpec((tr, width), lambda i: (i, 0))
    outs = pl.pallas_call(body, name=name, out_shape=[_sds((rows, width), F32)] * 4, grid=(rows // tr,),
                          in_specs=[spec, gspec, spec, spec], out_specs=[spec] * 4,
                          compiler_params=_cp(("parallel",)))(w2, red, m2, v2)
    return tuple(t.reshape(shape) for t in outs)


def mod_mm(sc, w_mod, bias, name):
    wm = w_mod.shape[-1]
    tn = _pick(wm, (768, 512, 384, 256, 128))

    def body(a_ref, b_ref, c_ref, o_ref):
        o_ref[...] = _nn(a_ref[...], b_ref[...].astype(BF16)) + c_ref[...]

    return pl.pallas_call(
        body, name=name, out_shape=_sds((DEPTH, 32, wm), F32), grid=(DEPTH, wm // tn),
        in_specs=[pl.BlockSpec((32, D), lambda l, j: (0, 0)), pl.BlockSpec((None, D, tn), lambda l, j: (l, 0, j)),
                  pl.BlockSpec((None, 1, tn), lambda l, j: (l, 0, j))],
        out_specs=pl.BlockSpec((None, 32, tn), lambda l, j: (l, 0, j)),
        compiler_params=_cp(("parallel", "parallel")))(sc, w_mod, bias)


def wmod_dw(sc, dcol, name):
    wm = dcol.shape[-1]
    tm = 256

    def body(a_ref, b_ref, o_ref):
        o_ref[...] = _tn(a_ref[...], b_ref[...].astype(BF16))

    return pl.pallas_call(
        body, name=name, out_shape=_sds((DEPTH, D, wm), F32), grid=(DEPTH, D // tm),
        in_specs=[pl.BlockSpec((32, tm), lambda l, i: (0, i)), pl.BlockSpec((None, 32, wm), lambda l, i: (l, 0, 0))],
        out_specs=pl.BlockSpec((None, tm, wm), lambda l, i: (l, i, 0)),
        compiler_params=_cp(("parallel", "parallel")))(sc, dcol)


def cctx_dx(drow, w_mod, name):
    wm = w_mod.shape[-1]

    def body(a_ref, b_ref, o_ref):
        o_ref[...] = _nt(a_ref[...].astype(BF16), b_ref[...].astype(BF16))

    return pl.pallas_call(
        body, name=name, out_shape=_sds((DEPTH, 16, D), F32), grid=(DEPTH,),
        in_specs=[pl.BlockSpec((None, 16, wm), lambda l: (l, 0, 0)), pl.BlockSpec((None, D, wm), lambda l: (l, 0, 0))],
        out_specs=pl.BlockSpec((None, 16, D), lambda l: (l, 0, 0)), compiler_params=_cp(("parallel",), VMEM_BIG))(drow, w_mod)


HEAD_PERM = (0, 4, 1, 5, 2, 6, 3, 7)


def _rot_rows(wt):
    return jnp.concatenate([-wt[32:64], wt[0:32]], axis=0)


def _unrot_rows(g):
    return jnp.concatenate([g[32:64], -g[0:32]], axis=0)


def _heads(a, n):
    return [a[64 * i:64 * (i + 1)] for i in range(n)]


def kernel(x, c, ctx, c_ctx, w_mod, b_mod, ln_g, ln_b, ffn_w_gate, ffn_w_up, ffn_w_down, mix_ab_w_in, attn_sink, pool_w, pool_scale, mix_ab_w_out, lru_w_in, lru_conv_w, lru_conv_b, lru_wa, lru_ba, lru_wx, lru_bx, lru_lambda, lru_w_out, loss_target, m_c_ctx, m_w_mod, m_b_mod, m_ln_g, m_ln_b, m_ffn_w_gate, m_ffn_w_up, m_ffn_w_down, m_mix_ab_w_in, m_attn_sink, m_pool_w, m_pool_scale, m_mix_ab_w_out, m_lru_w_in, m_lru_conv_w, m_lru_conv_b, m_lru_wa, m_lru_ba, m_lru_wx, m_lru_bx, m_lru_lambda, m_lru_w_out, v_c_ctx, v_w_mod, v_b_mod, v_ln_g, v_ln_b, v_ffn_w_gate, v_ffn_w_up, v_ffn_w_down, v_mix_ab_w_in, v_attn_sink, v_pool_w, v_pool_scale, v_mix_ab_w_out, v_lru_w_in, v_lru_conv_w, v_lru_conv_b, v_lru_wa, v_lru_ba, v_lru_wx, v_lru_bx, v_lru_lambda, v_lru_w_out):
    n_lat, n_ctx = x.shape[1], ctx.shape[1]
    lay = Layout(n_ctx, n_lat)
    T = lay.T
    ns = ffn_w_gate.shape[-1]
    n_li, n_ai = lru_w_in.shape[-1], mix_ab_w_in.shape[-1]
    n_ao, n_lo = mix_ab_w_out.shape[1], lru_w_out.shape[1]
    wm = w_mod.shape[-1]
    dsh = ln_g.shape[-1]
    mx, my, mc = lax.axis_index("x"), lax.axis_index("y"), lax.axis_index("c")
    chip = 2 * mx + my
    me = 2 * chip + mc

    c_all = all_gather8(c, "ag8_c").reshape(16, D)
    cc = jnp.concatenate([c_all, c_ctx[None, :], jnp.zeros((15, D), F32)], axis=0)
    sc = silu_rows(cc, "silu_c")
    bias = lax.dynamic_slice(b_mod, (0, chip * wm), (DEPTH, wm)).reshape(DEPTH, 1, wm)
    modg = all_gather_chips(mod_mm(sc, w_mod, bias, "mod_mm"), "ag_mod")
    modtab = []
    for l in range(DEPTH):
        full = jnp.transpose(modg[:, l], (1, 0, 2)).reshape(32, N_CHIP * wm)
        mine = lax.dynamic_slice(full, (2 * me, 0), (2, N_CHIP * wm))
        modtab.append(jnp.concatenate([mine, full[16:17]], axis=0).reshape(3, N_MOD, D))

    small = jnp.concatenate([ln_g.reshape(6, dsh), ln_b.reshape(6, dsh), lru_conv_w[0], lru_conv_b, lru_ba[0],
                             lru_bx[0], lru_lambda[0], jnp.zeros((9, dsh), F32)], axis=0)
    small = all_gather_chips(small.reshape(2, 16, dsh), "ag_small").reshape(N_CHIP, 32, dsh)
    small = jnp.transpose(small, (1, 0, 2)).reshape(32, D)
    ln_g_f, ln_b_f = small[0:6].reshape(2, 3, D), small[6:12].reshape(2, 3, D)
    conv_w_f, conv_b_f = small[12:16], small[16:17]
    lru_vec = small[17:23]

    hh = 3 * ns // 2
    gate_t, up_t = jnp.swapaxes(ffn_w_gate, -1, -2), jnp.swapaxes(ffn_w_up, -1, -2)
    extra = [0, n_ai + n_ao, n_li + n_lo, 0]
    placed = [ffn_place(gate_t, up_t, ffn_w_down, g // 2, g % 2, f"ag_ffn{g}_place", extra[g]) for g in range(4)]
    placed[1] = place_rows(placed[1], jnp.concatenate([mix_ab_w_in[0].T, mix_ab_w_out[0]], axis=0).astype(BF16), 3 * ns,
                           "ag_mixa_place")
    placed[2] = place_rows(placed[2], jnp.concatenate([lru_w_in[0].T, lru_w_out[0]], axis=0).astype(BF16), 3 * ns,
                           "ag_mixc_place")
    placed = [p.reshape(N_CHIP, 2, p.shape[1] // 2, D) for p in placed]
    wb = [gather_placed(placed[0], "ag_ffn0"), None, None, None]
    mixw = {}

    def mixa_w():
        if "a" not in mixw:
            full = wb[1].reshape(N_CHIP, -1, D)
            ab_in_t = full[:, 3 * ns:3 * ns + n_ai].reshape(N_CHIP * n_ai, D)
            ab_out = full[:, 3 * ns + n_ai:].reshape(N_CHIP * n_ao, D)
            qh, kh = _heads(ab_in_t[Q0:K0], N_HEADS), _heads(ab_in_t[K0:V0], N_KV)
            w_ext_t = jnp.concatenate([qh[h] for h in HEAD_PERM] + [ab_in_t[K0:QR0]]
                                      + [_rot_rows(qh[h]) for h in HEAD_PERM] + [_rot_rows(t) for t in kh], axis=0)
            oh = _heads(ab_out[0:ATT_W], N_HEADS)
            mixw["a"] = (w_ext_t, jnp.concatenate([oh[h] for h in HEAD_PERM] + [ab_out[ATT_W:]], axis=0))
        return mixw["a"]

    def mixc_w():
        if "c" not in mixw:
            full = wb[2].reshape(N_CHIP, -1, D)
            mixw["c"] = (full[:, 3 * ns:3 * ns + n_li].reshape(N_CHIP * n_li, D),
                         full[:, 3 * ns + n_li:].reshape(N_CHIP * n_lo, D))
        return mixw["c"]

    t = jnp.arange(n_lat)
    inv = ROPE_THETA ** (-jnp.arange(16, dtype=F32) / 16.0)
    ang = jnp.concatenate([(t // GRID_W).astype(F32)[:, None] * inv, (t % GRID_W).astype(F32)[:, None] * inv], axis=-1)
    cos1 = jnp.concatenate([jnp.ones((n_ctx, 32), F32), jnp.cos(ang)], axis=0)
    sin1 = jnp.concatenate([jnp.zeros((n_ctx, 32), F32), jnp.sin(ang)], axis=0)
    cos_t = jnp.tile(cos1, (2, 4))
    sin_t = jnp.tile(sin1, (2, 4))
    sk = attn_sink[0]
    sink_tab = jnp.concatenate([jnp.repeat(jnp.stack([sk[:4], sk[4:]], axis=1), HEAD_DIM, axis=1),
                                jnp.zeros((4, 128), F32)], axis=0)
    pscale = pool_scale.reshape(1, POOL_W)

    h0 = jnp.concatenate([ctx, x], axis=1).reshape(T, D)
    tgt = loss_target.reshape(2 * n_lat, D)

    def lnv(l, j):
        return jnp.stack([ln_g_f[l, j], ln_b_f[l, j]])

    subs = [(0, 0, 0.5, 0), (0, 3, 1.0, 1), (0, 6, 0.5, 2), (1, 0, 0.5, 0), (1, 3, 1.0, 1), (1, 6, 0.5, 2)]

    def ffn_core(hm, l, f):
        tag = f"l{l}f{f}"
        gi = 2 * l + f
        w = wb[gi].reshape(N_CHIP, -1, D)
        if gi == 3:
            up, sl, a = ffn_up(lay, hm, w, 0, 1, ns, f"ffn_up_{tag}")
            (y,) = slab_nn_acc(lay, [a], w, [2], ns, f"ffn_down_{tag}")
            return y, dict(up=up, sl=sl, a=a, nbuf=None)
        up, sl, a, nbuf = ffn_up(lay, hm, w, 0, 1, ns, f"ffn_up_{tag}", rider=rider_gather_xy(placed[gi + 1]))
        y, nbuf = slab_nn_acc(lay, [a], w, [2], ns, f"ffn_down_{tag}", rider=rider_gather_fwd(nbuf))
        return y, dict(up=up, sl=sl, a=a, nbuf=nbuf)

    def mixa_core(hm):
        p = mm_nt(hm, mixa_w()[0], "mixa_in")
        qr, kr, vb, u = rope_fwd(lay, p, cos_t, sin_t, "rope")
        att, lse = attn_fwd(lay, qr, kr, vb, sink_tab, "attn")
        pool = pool_fwd(lay, u, pool_w[0], pscale, "pool")
        cat = jnp.concatenate([att, pool], axis=1)
        return mm_nn(cat, mixa_w()[1], "mixa_out"), dict(qr=qr, kr=kr, vb=vb, u=u, lse=lse, cat=cat)

    def mixc_core(hm):
        p = mm_nt(hm, mixc_w()[0], "mixc_in")
        uc = conv_fwd(lay, p, D, conv_w_f, conv_b_f, "conv")
        a, b = lru_coeffs(lay, uc, lru_wa[0], lru_wx[0], lru_vec, "lru_coef")
        s = lru_scan(lay, a, b, "lru_scan")
        o = lru_gate(lay, p, s, "lru_gate")
        return mm_nn(o, mixc_w()[1], "mixc_out"), dict(p=p, uc=uc, a=a, s=s, o=o)

    recs = []
    h = h0
    hm = modulate(lay, h0, modtab[0], 0, 1, "mod_first")
    for k, (l, k0, coef, j) in enumerate(subs):
        if k0 == 3:
            y, core = mixa_core(hm) if l == 0 else mixc_core(hm)
        else:
            y, core = ffn_core(hm, l, k0 // 6)
        nxt = None if k == 5 else (modtab[subs[k + 1][0]], subs[k + 1][1], subs[k + 1][1] + 1)
        nbuf = core.pop("nbuf", None)
        res = resid_ln(lay, h, y, modtab[l], k0 + 2, coef, lnv(l, j), f"ln_s{k}", nxt=nxt,
                       rider=None if nbuf is None else rider_gather_d2d(nbuf))
        if nbuf is not None:
            wb[2 * l + k0 // 6 + 1] = res[-1]
        recs.append(dict(h=h, hm=hm, y=y, xhat=res[1], rstd=res[2], **core))
        h = res[0]
        hm = res[3] if nxt is not None else None

    dout, lparts = loss_and_grad(lay, h, tgt, "loss")
    loss = lax.psum(jnp.sum(lparts), ("x", "y", "c"))

    dln = {}
    dms = {}
    mixg = {}
    ffn_red = [lax.empty((4, 2, hh, D), F32)]
    mix_red = {}
    pending = []

    def rs_sib(p):
        return None if p is None else rider_reduce_sib(p["buf"])

    def rs_add2(p, recv):
        p["q"] = add_own_half(p["buf"], recv, BF16, f"rs_add2_{p['key']}")

    def rs_join(p, arr):
        if isinstance(p["key"], int):
            return rider_join(sum_slots(p["q"], arr, f"rs_add4_{p['key']}", dst=ffn_red[0], g=p["key"]), p["key"])
        return rider_join(sum_slots(p["q"], arr, f"rs_add4_{p['key']}"))

    def rs_done(p, joined):
        if isinstance(p["key"], int):
            ffn_red[0] = joined
        else:
            mix_red[p["key"]] = joined.reshape(-1, D)

    def ffn_core_bwd(dy, r, l, f):
        tag = f"l{l}f{f}"
        gi = 2 * l + f
        w = wb[gi].reshape(N_CHIP, -1, D)
        p = pending.pop() if pending else None
        gb = lax.empty((N_CHIP, 3 * ns, D), F32)
        if p is None:
            dg, du = ffn_bwd_da(lay, dy, w, 2, r["up"], r["sl"], ns, f"ffn_da_{tag}")
            (gb,) = slab_tn(lay, r["a"], dy, gb, 2, ns, f"ffn_dwd_{tag}")
            (gb,) = slab_tn(lay, dg, r["hm"], gb, 0, ns, f"ffn_dwg_{tag}")
            (gb,) = slab_tn(lay, du, r["hm"], gb, 1, ns, f"ffn_dwu_{tag}")
            (dhm,) = slab_nn_acc(lay, [dg, du], w, [0, 1], ns, f"ffn_dh_{tag}")
        elif gi == 0:
            dg, du, recv = ffn_bwd_da(lay, dy, w, 2, r["up"], r["sl"], ns, f"ffn_da_{tag}", rider=rs_sib(p))
            rs_add2(p, recv)
            gb, arr = slab_tn(lay, r["a"], dy, gb, 2, ns, f"ffn_dwd_{tag}", rider=rider_reduce_copies(p["q"]))
            gb, joined = slab_tn(lay, dg, r["hm"], gb, 0, ns, f"ffn_dwg_{tag}", rider=rs_join(p, arr))
            rs_done(p, joined)
            (gb,) = slab_tn(lay, du, r["hm"], gb, 1, ns, f"ffn_dwu_{tag}")
            own = gb.reshape(N_CHIP, 2, hh, D)
            dhm, recv = slab_nn_acc(lay, [dg, du], w, [0, 1], ns, f"ffn_dh_{tag}", rider=rider_reduce_sib(own))
            pending.append(dict(buf=own, key=gi, recv=recv))
            return dhm
        else:
            dg, du, recv = ffn_bwd_da(lay, dy, w, 2, r["up"], r["sl"], ns, f"ffn_da_{tag}", rider=rs_sib(p))
            rs_add2(p, recv)
            gb, arr = slab_tn(lay, r["a"], dy, gb, 2, ns, f"ffn_dwd_{tag}", rider=rider_reduce_copy(p["q"], 0))
            gb, arr = slab_tn(lay, dg, r["hm"], gb, 0, ns, f"ffn_dwg_{tag}", rider=rider_reduce_copy(p["q"], 1, arr))
            gb, arr = slab_tn(lay, du, r["hm"], gb, 1, ns, f"ffn_dwu_{tag}", rider=rider_reduce_copy(p["q"], 2, arr))
            dhm, joined = slab_nn_acc(lay, [dg, du], w, [0, 1], ns, f"ffn_dh_{tag}", rider=rs_join(p, arr))
            rs_done(p, joined)
        pending.append(dict(buf=gb.reshape(N_CHIP, 2, hh, D), key=gi))
        return dhm

    def mixc_core_bwd(dy, r):
        p = pending.pop() if pending else None
        w_in_t, w_out = mixc_w()
        if p is None:
            do_c = mm_nt(dy, w_out, "mixc_out_dx")
        else:
            do_c, recv = mm_nt(dy, w_out, "mixc_out_dx", rider=rs_sib(p))
            rs_add2(p, recv)
        g_out = mm_tn(r["o"], dy, "mixc_out_dw")
        dgate, dyg = lru_gate_bwd(lay, r["p"], r["s"], do_c, "lru_gate_b")
        da_c, db_c = lru_scan_bwd(lay, r["a"], r["s"], dyg, "lru_scan_b")
        res = lru_coeffs_bwd(lay, r["uc"], lru_wa[0], lru_wx[0], lru_vec, da_c, db_c, "lru_coef_b",
                             rider=None if p is None else rider_reduce_copies(p["q"]))
        duc, mixg["wa"], mixg["wx"], mixg["vec"] = res[:4]
        du_c, mixg["cw"], mixg["cb"] = conv_bwd(lay, r["p"], D, conv_w_f, duc, "conv_b")
        dp_c = jnp.concatenate([dgate, du_c], axis=1)
        if p is None:
            g_in_t = mm_tn(dp_c, r["hm"], "mixc_in_dw")
        else:
            g_in_t, joined = mm_tn(dp_c, r["hm"], "mixc_in_dw", rider=rs_join(p, res[4]))
            rs_done(p, joined)
        buf = jnp.concatenate([g_in_t.reshape(N_CHIP, n_li, D), g_out.reshape(N_CHIP, n_lo, D)], axis=1)
        pending.append(dict(buf=buf.reshape(N_CHIP, 2, (n_li + n_lo) // 2, D), key="c"))
        return mm_nn(dp_c, w_in_t, "mixc_in_dx")

    def mixa_core_bwd(dy, r):
        p = pending.pop() if pending else None
        w_ext_t, w_out_ext = mixa_w()
        if p is None:
            dcat = mm_nt(dy, w_out_ext, "mixa_out_dx")
        else:
            dcat, recv = mm_nt(dy, w_out_ext, "mixa_out_dx", rider=rs_sib(p))
            rs_add2(p, recv)
        g_out_ext = mm_tn(r["cat"], dy, "mixa_out_dw")
        res = attn_bwd(lay, r["qr"], r["kr"], r["vb"], sink_tab, r["lse"], dcat, "attn_b",
                       rider=None if p is None else rider_reduce_copies(p["q"]))
        dqr, dkr, dv, mixg["sink"] = res[:4]
        du_a, mixg["pw"], mixg["ps"] = pool_bwd(lay, r["u"], dcat, pool_w[0], pscale, "pool_b")
        dp_a = rope_bwd(lay, dqr, dkr, dv, du_a, cos_t, sin_t, "rope_b")
        if p is None:
            g_ext_t = mm_tn(dp_a, r["hm"], "mixa_in_dw")
        else:
            g_ext_t, joined = mm_tn(dp_a, r["hm"], "mixa_in_dw", rider=rs_join(p, res[4]))
            rs_done(p, joined)
        gq, gqr = _heads(g_ext_t[Q0:K0], N_HEADS), _heads(g_ext_t[QR0:KR0], N_HEADS)
        g_q = [None] * N_HEADS
        for i, h in enumerate(HEAD_PERM):
            g_q[h] = gq[i] + _unrot_rows(gqr[i])
        gk = [a + _unrot_rows(b) for a, b in zip(_heads(g_ext_t[K0:V0], N_KV), _heads(g_ext_t[KR0:PEXT], N_KV))]
        g_ab_in_t = jnp.concatenate(g_q + gk + [g_ext_t[V0:QR0]], axis=0)
        go = _heads(g_out_ext[0:ATT_W], N_HEADS)
        g_o = [None] * N_HEADS
        for i, h in enumerate(HEAD_PERM):
            g_o[h] = go[i]
        g_ab_out = jnp.concatenate(g_o + [g_out_ext[ATT_W:]], axis=0)
        buf = jnp.concatenate([g_ab_in_t.reshape(N_CHIP, n_ai, D), g_ab_out.reshape(N_CHIP, n_ao, D)], axis=1)
        pending.append(dict(buf=buf.reshape(N_CHIP, 2, (n_ai + n_ao) // 2, D), key="a"))
        return mm_nn(dp_a, w_ext_t, "mixa_in_dx")

    l, k0, coef, j = subs[5]
    dy, dres, s1 = ln_bwd(lay, dout, recs[5]["xhat"], recs[5]["rstd"], recs[5]["y"], modtab[l], k0 + 2, coef, lnv(l, j),
                          "lnb_s5")
    for k in range(5, -1, -1):
        l, k0, coef, j = subs[k]
        r = recs[k]
        if k0 == 3:
            dhm = mixa_core_bwd(dy, r) if l == 0 else mixc_core_bwd(dy, r)
        else:
            dhm = ffn_core_bwd(dy, r, l, k0 // 6)
        dln[(l, j)] = s1
        if k > 0:
            lp, k0p, coefp, jp = subs[k - 1]
            rp = recs[k - 1]
            dy, dres, s1, s2 = modb_lnb(lay, dres, dhm, r["h"], modtab[l], k0 + 1, rp["xhat"], rp["rstd"], rp["y"],
                                        modtab[lp], k0p + 2, coefp, lnv(lp, jp), f"modb_lnb_s{k}")
        else:
            gx, s2 = mod_bwd(lay, dres, dhm, r["h"], modtab[l], k0 + 1, "modb_s0")
        dms[(l, k0)] = s2
    sums = block_sums(lay, list(dln.values()) + list(dms.values()), "block_sums")
    dln, dms = dict(zip(dln, sums[:len(dln)])), dict(zip(dms, sums[len(dln):]))
    grad_x = gx.reshape(2, n_lat, D)
    g_wa, g_wx, g_vec, g_cw, g_cb, g_sink, g_pw, g_ps = (mixg[n] for n in ("wa", "wx", "vec", "cw", "cb", "sink", "pw", "ps"))

    rows = []
    for l in range(DEPTH):
        per_k = []
        for k0, j in ((0, 0), (3, 1), (6, 2)):
            per_k += [dms[(l, k0)][:3, 0], dms[(l, k0)][:3, 1], dln[(l, j)][:3, 2]]
        rows.append(jnp.stack(per_k, axis=1).reshape(3, N_MOD * D))
    dmod_loc = jnp.concatenate(rows + [jnp.zeros((2, N_MOD * D), F32)], axis=0)
    dmod_all, g_b_mod = mod_grad_rows(all_gather8(dmod_loc, "ag8_dmod"), "dmod_rows")
    dcol = lax.dynamic_slice(dmod_all, (0, 0, chip * wm), (DEPTH, 32, wm))
    g_w_mod = wmod_dw(sc, dcol, "wmod_dw")
    g_cctx = cctx_grad(cctx_dx(dcol[:, 16:32], w_mod, "cctx_dx"), c_ctx[None, :], "cctx_grad")

    g_ln_g =jnp.stack([jnp.stack([dln[(l, j)][3, 1] for j in range(3)]) for l in range(DEPTH)])
    g_ln_b = jnp.stack([jnp.stack([dln[(l, j)][3, 0] for j in range(3)]) for l in range(DEPTH)])
    sink_row = jnp.sum(g_sink, axis=0)[:4]
    g_sink8 = jnp.concatenate([sink_row[:, 0], sink_row[:, HEAD_DIM]])
    misc = jnp.concatenate([g_sink8, jnp.sum(g_ps, axis=0).reshape(POOL_W), jnp.zeros((D - 8 - POOL_W,), F32)])
    small_g = jnp.concatenate([
        g_ln_g.reshape(6, D), g_ln_b.reshape(6, D), jnp.sum(g_cw, axis=0), jnp.sum(g_cb, axis=0), g_vec,
        misc[None, :], jnp.sum(g_pw, axis=0).reshape(64, D), g_wa.reshape(256, D), g_wx.reshape(256, D), g_cctx,
        jnp.zeros((39, D), F32)], axis=0)
    n_small = small_g.shape[0] // N_CHIP
    last = pending.pop()
    ffn_red = reduce_scatter_chips(last["buf"], f"ffn{last['key']}", wire=BF16, dst=ffn_red[0], g=last["key"],
                                   recv=last.get("recv")).reshape(12 * ns, D)
    small_red = reduce_scatter_chips(small_g.reshape(N_CHIP, 2, n_small // 2, D), "small")
    small_red = all_gather_chips(small_red, "ag_smallg").reshape(N_CHIP * n_small, D)

    ffn_kind = dict(ffn_w_gate=0, ffn_w_up=1, ffn_w_down=2)

    def cols(a):
        return lax.dynamic_slice_in_dim(a, chip * dsh, dsh, axis=a.ndim - 1)

    sr = small_red
    grads = dict(
        c_ctx=sr[600], w_mod=g_w_mod, b_mod=g_b_mod,
        ln_g=cols(sr[0:6]).reshape(2, 3, dsh), ln_b=cols(sr[6:12]).reshape(2, 3, dsh),
        mix_ab_w_in=mix_red["a"][0:n_ai][None], attn_sink=sr[23, 0:8][None], pool_w=sr[24:88].reshape(1, 4, 128, 128),
        pool_scale=sr[23, 8:8 + POOL_W][None], mix_ab_w_out=mix_red["a"][n_ai:][None],
        lru_w_in=mix_red["c"][0:n_li].T[None],
        lru_conv_w=cols(sr[12:16])[None], lru_conv_b=cols(sr[16:17]), lru_wa=sr[88:344].reshape(1, 2, 8, 128, 128),
        lru_ba=cols(sr[17:19])[None], lru_wx=sr[344:600].reshape(1, 2, 8, 128, 128), lru_bx=cols(sr[19:21])[None],
        lru_lambda=cols(sr[21:23])[None], lru_w_out=mix_red["c"][n_li:][None])
    params = dict(c_ctx=(c_ctx, m_c_ctx, v_c_ctx), w_mod=(w_mod, m_w_mod, v_w_mod), b_mod=(b_mod, m_b_mod, v_b_mod),
                  ln_g=(ln_g, m_ln_g, v_ln_g), ln_b=(ln_b, m_ln_b, v_ln_b),
                  ffn_w_gate=(ffn_w_gate, m_ffn_w_gate, v_ffn_w_gate), ffn_w_up=(ffn_w_up, m_ffn_w_up, v_ffn_w_up),
                  ffn_w_down=(ffn_w_down, m_ffn_w_down, v_ffn_w_down),
                  mix_ab_w_in=(mix_ab_w_in, m_mix_ab_w_in, v_mix_ab_w_in), attn_sink=(attn_sink, m_attn_sink, v_attn_sink),
                  pool_w=(pool_w, m_pool_w, v_pool_w), pool_scale=(pool_scale, m_pool_scale, v_pool_scale),
                  mix_ab_w_out=(mix_ab_w_out, m_mix_ab_w_out, v_mix_ab_w_out), lru_w_in=(lru_w_in, m_lru_w_in, v_lru_w_in),
                  lru_conv_w=(lru_conv_w, m_lru_conv_w, v_lru_conv_w), lru_conv_b=(lru_conv_b, m_lru_conv_b, v_lru_conv_b),
                  lru_wa=(lru_wa, m_lru_wa, v_lru_wa), lru_ba=(lru_ba, m_lru_ba, v_lru_ba), lru_wx=(lru_wx, m_lru_wx, v_lru_wx),
                  lru_bx=(lru_bx, m_lru_bx, v_lru_bx), lru_lambda=(lru_lambda, m_lru_lambda, v_lru_lambda),
                  lru_w_out=(lru_w_out, m_lru_w_out, v_lru_w_out))
    gl, dl, ml, vl = [], [], [], []
    transposed = ("ffn_w_gate", "ffn_w_up", "mix_ab_w_in")
    for name, (w, m, v) in params.items():
        if name in transposed:
            w, m, v = (jnp.swapaxes(t, -1, -2) for t in (w, m, v))
        if name in ffn_kind:
            g, d, mn, vn = adamw_ffn(w, m, v, ffn_red, ffn_kind[name], ns, f"adamw_{name}")
        else:
            g = grads[name].reshape(w.shape)
            d, mn, vn = adamw(w, g, m, v, f"adamw_{name}")
        if name in transposed:
            g, d, mn, vn = (jnp.swapaxes(t, -1, -2) for t in (g, d, mn, vn))
        gl.append(g)
        dl.append(d)
        ml.append(mn)
        vl.append(vn)
    return (loss, grad_x, *gl, *dl, *ml, *vl)
```

```python
import functools
import math

import jax
import jax.numpy as jnp
from jax import lax
from jax.experimental import pallas as pl
from jax.experimental.pallas import tpu as pltpu

F32, BF16 = jnp.float32, jnp.bfloat16
MESH = pl.DeviceIdType.MESH
ANY = pl.BlockSpec(memory_space=pl.ANY)
VMEM_SPEC = pl.BlockSpec(memory_space=pltpu.VMEM)

D = 1024
N_CHIP = 4
HEAD_DIM, N_HEADS, N_KV = 64, 8, 2
ATT_W, KV_W, POOL_W = 512, 128, 512
POOL_WINDOWS = (2, 4, 8, 16)
BLK = 128
ATT_SCALE = HEAD_DIM ** -0.5
ROPE_THETA = 10000.0
GRID_W = 64
LRU_C = 8.0
LN_EPS = 1e-5
NEG_INF = -1e30
DEPTH = 2
ALPHA = (2 * DEPTH) ** 0.25
N_MOD = 9
ADAM_LR, ADAM_B1, ADAM_B2, ADAM_EPS, ADAM_WD, ADAM_STEP = 0.001, 0.9, 0.999, 1e-08, 0.01, 10
VMEM_BIG = 48 * 1024 * 1024


def _cp(sem=None, vmem=None):
    kw = {}
    if sem is not None:
        kw["dimension_semantics"] = sem
    if vmem is not None:
        kw["vmem_limit_bytes"] = vmem
    return pltpu.CompilerParams(**kw)


def _sds(shape, dtype):
    return jax.ShapeDtypeStruct(tuple(shape), dtype)


def _pick(n, cands):
    for c in cands:
        if n % c == 0:
            return c
    return n


def _dot(a, b, dims):
    return lax.dot_general(a, b, (dims, ((), ())), preferred_element_type=F32)


def _nn(a, b):
    return _dot(a, b, ((1,), (0,)))


def _nt(a, b):
    return _dot(a, b, ((1,), (1,)))


def _tn(a, b):
    return _dot(a, b, ((0,), (0,)))


def _sigmoid(x):
    return 0.5 * jnp.tanh(0.5 * x) + 0.5


def _me():
    return lax.axis_index("x"), lax.axis_index("y"), lax.axis_index("c")


def _rcopy(src, dst, ssem, rsem, dev):
    return pltpu.make_async_remote_copy(src_ref=src, dst_ref=dst, send_sem=ssem, recv_sem=rsem,
                                        device_id=dev, device_id_type=MESH)


def all_gather8(x, name):
    def body(x_ref, o_ref, ssem, rsem, lsem):
        mx, my, mc = _me()
        me = 4 * mx + 2 * my + mc
        loc = pltpu.make_async_copy(x_ref, o_ref.at[me], lsem)
        loc.start()
        peers = []
        for m in range(1, 8):
            px = 1 - mx if (m >> 2) & 1 else mx
            py = 1 - my if (m >> 1) & 1 else my
            pc = 1 - mc if m & 1 else mc
            peers.append((px, py, pc))
        sends = [_rcopy(x_ref, o_ref.at[me], ssem.at[k], rsem.at[k], p) for k, p in enumerate(peers)]
        for cp in sends:
            cp.start()
        for k, (px, py, pc) in enumerate(peers):
            _rcopy(x_ref, o_ref.at[4 * px + 2 * py + pc], ssem.at[k], rsem.at[k], (px, py, pc)).wait_recv()
        for cp in sends:
            cp.wait_send()
        loc.wait()

    return pl.pallas_call(
        body, name=name, out_shape=_sds((8,) + x.shape, x.dtype),
        in_specs=[VMEM_SPEC], out_specs=VMEM_SPEC,
        scratch_shapes=[pltpu.SemaphoreType.DMA((7,)), pltpu.SemaphoreType.DMA((7,)), pltpu.SemaphoreType.DMA],
    )(x)


_ROW_BLOCKS = (512, 384, 352, 256, 224, 128)


def _idx(v):
    return jnp.reshape(v, (1,)).astype(jnp.int32)


def place_slab(shard, name):
    _, h, w = shard.shape
    th = _pick(h, _ROW_BLOCKS)

    def body(s_ref, x_ref, o_ref):
        del s_ref
        o_ref[...] = x_ref[...]

    return pl.pallas_call(
        body, name=name, out_shape=_sds((N_CHIP,) + shard.shape, shard.dtype),
        grid_spec=pltpu.PrefetchScalarGridSpec(
            num_scalar_prefetch=1, grid=(2, h // th),
            in_specs=[pl.BlockSpec((None, th, w), lambda k, r, s: (k, r, 0))],
            out_specs=pl.BlockSpec((None, None, th, w), lambda k, r, s: (s[0], k, r, 0))),
    )(_idx(2 * lax.axis_index("x") + lax.axis_index("y")), shard)


def place_rows(buf, rows, r0, name):
    e, w = rows.shape
    tb = 64

    def body(s_ref, x_ref, b_ref, o_ref):
        del s_ref, b_ref
        o_ref[...] = x_ref[...]

    return pl.pallas_call(
        body, name=name, out_shape=_sds(buf.shape, buf.dtype),
        grid_spec=pltpu.PrefetchScalarGridSpec(
            num_scalar_prefetch=1, grid=(e // tb,),
            in_specs=[pl.BlockSpec((tb, w), lambda j, s: (j, 0)), ANY],
            out_specs=pl.BlockSpec((None, tb, w), lambda j, s: (s[0], r0 // tb + j, 0))),
        input_output_aliases={2: 0},
    )(_idx(2 * lax.axis_index("x") + lax.axis_index("y")), rows, buf)


def ffn_place(w_gate_t, w_up_t, w_down, l, f, name, extra=0):
    ns = w_down.shape[-2]
    tr, nb = ns // 2, 2

    def body(s_ref, g_ref, u_ref, d_ref, o_ref):
        del s_ref
        k = pl.program_id(0)

        @pl.when(k == 0)
        def _():
            o_ref[...] = g_ref[...].astype(BF16)

        @pl.when(k == 1)
        def _():
            o_ref[...] = u_ref[...].astype(BF16)

        @pl.when(k == 2)
        def _():
            o_ref[...] = d_ref[...].astype(BF16)

    def spec(q):
        return pl.BlockSpec((None, None, tr, D), lambda k, j, s: (l, f, jnp.where(k == q, j, 0), 0))

    return pl.pallas_call(
        body, name=name, out_shape=_sds((N_CHIP, 3 * ns + extra, D), BF16),
        grid_spec=pltpu.PrefetchScalarGridSpec(
            num_scalar_prefetch=1, grid=(3, nb), in_specs=[spec(0), spec(1), spec(2)],
            out_specs=pl.BlockSpec((None, tr, D), lambda k, j, s: (s[0], k * nb + j, 0))),
    )(_idx(2 * lax.axis_index("x") + lax.axis_index("y")), w_gate_t, w_up_t, w_down)


def all_gather_chips(shard, name):
    return gather_placed(place_slab(shard, name + "_place"), name)


def gather_placed(full, name):
    h = full.shape[2]
    lo, hi = pl.ds(0, h // 2), pl.ds(h // 2, h - h // 2)

    def body(x_ref, o_ref, ssem, rsem):
        del x_ref
        mx, my, mc = _me()
        s, xs, ys, ds = 2 * mx + my, 2 * (1 - mx) + my, 2 * mx + (1 - my), 2 * (1 - mx) + (1 - my)
        xn, yn, sib = (1 - mx, my, mc), (mx, 1 - my, mc), (mx, my, 1 - mc)

        def cp(k, src, dst, dev):
            return _rcopy(src, dst, ssem.at[k], rsem.at[k], dev)

        own = o_ref.at[s, mc]
        sent = [cp(0, own, own, xn), cp(1, own, own, yn)]
        for c in sent:
            c.start()
        cp(0, own, o_ref.at[xs, mc], xn).wait_recv()
        sent += [cp(2, o_ref.at[xs, mc, lo], o_ref.at[xs, mc, lo], yn), cp(4, o_ref.at[xs, mc], o_ref.at[xs, mc], sib)]
        sent[-2].start()
        sent[-1].start()
        cp(1, own, o_ref.at[ys, mc], yn).wait_recv()
        sent += [cp(3, o_ref.at[ys, mc, hi], o_ref.at[ys, mc, hi], xn), cp(5, o_ref.at[ys, mc], o_ref.at[ys, mc], sib)]
        sent[-2].start()
        sent[-1].start()
        cp(2, own, o_ref.at[ds, mc, lo], yn).wait_recv()
        cp(3, own, o_ref.at[ds, mc, hi], xn).wait_recv()
        sent.append(cp(6, o_ref.at[ds, mc], o_ref.at[ds, mc], sib))
        sent[-1].start()
        for k, slot in ((4, xs), (5, ys), (6, ds)):
            cp(k, own, o_ref.at[slot, 1 - mc], sib).wait_recv()
        for c in sent:
            c.wait_send()

    return pl.pallas_call(
        body, name=name, out_shape=_sds(full.shape, full.dtype), in_specs=[ANY], out_specs=ANY,
        input_output_aliases={0: 0},
        scratch_shapes=[pltpu.SemaphoreType.DMA((7,)), pltpu.SemaphoreType.DMA((7,))],
    )(full)


def sibling_send_other_half(buf, name):
    def body(x_ref, o_ref, ssem, rsem):
        mx, my, mc = _me()
        sib = (mx, my, 1 - mc)
        cps = [_rcopy(x_ref.at[k, 1 - mc], o_ref.at[k], ssem.at[k], rsem.at[k], sib) for k in range(N_CHIP)]
        for cp in cps:
            cp.start()
        for cp in cps:
            cp.wait_recv()
        for cp in cps:
            cp.wait_send()

    n, _, h, w = buf.shape
    return pl.pallas_call(
        body, name=name, out_shape=_sds((n, h, w), buf.dtype), in_specs=[ANY], out_specs=ANY,
        scratch_shapes=[pltpu.SemaphoreType.DMA((N_CHIP,)), pltpu.SemaphoreType.DMA((N_CHIP,))],
    )(buf)


def chips_all_to_all(q, name):
    def body(x_ref, o_ref, ssem, rsem):
        mx, my, mc = _me()
        s = 2 * mx + my
        chips = [(1 - mx, my), (mx, 1 - my), (1 - mx, 1 - my)]
        cps = [_rcopy(x_ref.at[2 * px + py], o_ref.at[s], ssem.at[j], rsem.at[j], (px, py, mc))
               for j, (px, py) in enumerate(chips)]
        for cp in cps:
            cp.start()
        for j, (px, py) in enumerate(chips):
            ps = 2 * px + py
            _rcopy(x_ref.at[ps], o_ref.at[ps], ssem.at[j], rsem.at[j], (px, py, mc)).wait_recv()
        for cp in cps:
            cp.wait_send()

    return pl.pallas_call(
        body, name=name, out_shape=_sds(q.shape, q.dtype), in_specs=[ANY], out_specs=ANY,
        scratch_shapes=[pltpu.SemaphoreType.DMA((3,)), pltpu.SemaphoreType.DMA((3,))],
    )(q)


def sibling_join_halves(both, name, g=None):
    def body(x_ref, o_ref, ssem, rsem):
        del x_ref
        mx, my, mc = _me()
        sib = (mx, my, 1 - mc)
        o = o_ref if g is None else o_ref.at[g]
        cp = _rcopy(o.at[mc], o.at[mc], ssem, rsem, sib)
        cp.start()
        _rcopy(o.at[1 - mc], o.at[1 - mc], ssem, rsem, sib).wait_recv()
        cp.wait_send()

    return pl.pallas_call(
        body, name=name, out_shape=_sds(both.shape, both.dtype), in_specs=[ANY], out_specs=ANY,
        input_output_aliases={0: 0}, scratch_shapes=[pltpu.SemaphoreType.DMA, pltpu.SemaphoreType.DMA],
    )(both)


def add_own_half(buf, recv, wire, name):
    n, _, h, w = buf.shape
    th = _pick(h, _ROW_BLOCKS)

    def body(c_ref, a_ref, b_ref, o_ref):
        del c_ref
        o_ref[...] = (a_ref[...] + b_ref[...]).astype(o_ref.dtype)

    return pl.pallas_call(
        body, name=name, out_shape=_sds((n, h, w), wire),
        grid_spec=pltpu.PrefetchScalarGridSpec(
            num_scalar_prefetch=1, grid=(n, h // th),
            in_specs=[pl.BlockSpec((None, None, th, w), lambda k, r, c: (k, c[0], r, 0)),
                      pl.BlockSpec((None, th, w), lambda k, r, c: (k, r, 0))],
            out_specs=pl.BlockSpec((None, th, w), lambda k, r, c: (k, r, 0))),
    )(_idx(lax.axis_index("c")), buf, recv)


def sum_slots(q, r, name, dst=None, g=None):
    n, h, w = r.shape
    th = _pick(h, _ROW_BLOCKS)

    def body(i_ref, q_ref, r1, r2, r3, *rest):
        del i_ref
        rest[-1][...] = ((q_ref[...].astype(F32) + r1[...].astype(F32)) + r2[...].astype(F32)) + r3[...].astype(F32)

    def slot(d):
        return lambda i, ix: ((ix[0] + d) % N_CHIP, i, 0)

    idx = jnp.stack([2 * lax.axis_index("x") + lax.axis_index("y"), lax.axis_index("c")]).astype(jnp.int32)
    in_specs = [pl.BlockSpec((None, th, w), slot(d)) for d in (0, 1, 2, 3)]
    if dst is None:
        return pl.pallas_call(
            body, name=name, out_shape=_sds((2, h, w), F32),
            grid_spec=pltpu.PrefetchScalarGridSpec(
                num_scalar_prefetch=1, grid=(h // th,), in_specs=in_specs,
                out_specs=pl.BlockSpec((None, th, w), lambda i, ix: (ix[1], i, 0))),
        )(idx, q, r, r, r)
    return pl.pallas_call(
        body, name=name, out_shape=_sds(dst.shape, F32),
        grid_spec=pltpu.PrefetchScalarGridSpec(
            num_scalar_prefetch=1, grid=(h // th,), in_specs=in_specs + [ANY],
            out_specs=pl.BlockSpec((None, None, th, w), lambda i, ix: (g, ix[1], i, 0))),
        input_output_aliases={5: 0},
    )(idx, q, r, r, r, dst)


def reduce_scatter_chips(buf, tag, wire=F32, dst=None, g=None, recv=None):
    if recv is None:
        recv = sibling_send_other_half(buf, f"rs_sib_{tag}")
    q = add_own_half(buf, recv, wire, f"rs_add2_{tag}")
    r = chips_all_to_all(q, f"rs_a2a_{tag}")
    red = sum_slots(q, r, f"rs_add4_{tag}", dst=dst, g=g)
    return sibling_join_halves(red, f"rs_join_{tag}", g=g)


class Layout:
    def __init__(self, n_ctx, n_lat):
        self.C, self.L = n_ctx, n_lat
        self.PS = n_ctx + n_lat
        self.T = 2 * self.PS
        self.tr = _pick(math.gcd(n_ctx, n_lat), (256, 128))
        self.bps = self.PS // self.tr
        self.cb = n_ctx // self.tr
        self.nblk = self.T // self.tr
        self.tm = _pick(self.T, (1152, 768, 512, 256, 128))
        self.tm2 = _pick(self.T, (2304, 1152, 768, 512, 256, 128))
        self.tc = _pick(self.T, (512, 256, 128))

    def seg(self, i):
        return jnp.where(i % self.bps < self.cb, 2, i // self.bps)


def rowwise(lay, name, fn, rows, segs=(), vecs=(), outs=(), sums=(), rider=None):
    tr, nblk = lay.tr, lay.nblk
    n_r, n_s, n_v, n_o = len(rows), len(segs), len(vecs), len(outs)
    lat_only = any(o[2:] for o in outs) or any(a.shape[0] != lay.T for a in rows)
    nsub = 1 if lat_only else (3 if nblk % 3 == 0 else 2 if nblk % 2 == 0 else 1)
    tb = tr * nsub

    def body(*refs):
        ins = refs[:n_r + n_s + n_v]
        ors = refs[n_r + n_s + n_v:]
        for sub in range(nsub):
            rs = slice(sub * tr, (sub + 1) * tr)
            seg = lay.seg(pl.program_id(0) * nsub + sub)
            vals = [r[rs, :] for r in ins[:n_r]] + [r[seg] for r in ins[n_r:n_r + n_s]] + [r[...] for r in ins[n_r + n_s:]]
            res = fn(*vals)
            for k in range(n_o):
                ors[k][rs, :] = res[k].astype(ors[k].dtype)
            for k in range(len(sums)):
                ors[n_o + k][sub] = res[n_o + k]

    def all_rows(i):
        return (i, 0)

    def lat_rows(i):
        return ((i // lay.bps) * (lay.bps - lay.cb) + jnp.maximum(i % lay.bps - lay.cb, 0), 0)

    in_specs = [pl.BlockSpec((tb, a.shape[1]), all_rows if a.shape[0] == lay.T else lat_rows) for a in rows]
    in_specs += [pl.BlockSpec(a.shape, lambda i: (0, 0, 0)) for a in segs]
    in_specs += [pl.BlockSpec(a.shape, lambda i: (0, 0)) for a in vecs]
    out_shape = [_sds((2 * lay.L if o[2:] else lay.T, o[0]), o[1]) for o in outs]
    out_shape += [_sds((nblk, r, w), F32) for r, w in sums]
    out_specs = [pl.BlockSpec((tb, o[0]), lat_rows if o[2:] else all_rows) for o in outs]
    out_specs += [pl.BlockSpec((nsub, r, w), lambda i: (i, 0, 0)) for r, w in sums]
    sem = "arbitrary" if any(o[2:] for o in outs) else "parallel"
    if rider is None:
        return pl.pallas_call(body, name=name, out_shape=out_shape, grid=(nblk // nsub,), in_specs=in_specs,
                              out_specs=out_specs, compiler_params=_cp((sem,), VMEM_BIG))(*rows, *segs, *vecs)
    return _host_call(body, rider, name, (nblk // nsub,), in_specs, out_specs, out_shape, (*rows, *segs, *vecs), (sem,),
                      n_r + n_s + n_v, n_o + len(sums))


def modulate(lay, h, mod, k_shift, k_scale, name):
    def fn(hb, m):
        return (hb * (1.0 + m[k_scale:k_scale + 1]) + m[k_shift:k_shift + 1],)
    return rowwise(lay, name, fn, [h], segs=[mod], outs=[(D, BF16)])[0]


def resid_ln(lay, h, y, mod, k_gate, coef, lnv, name, nxt=None, rider=None):
    def fn(hb, yb, m, *rest):
        ln = rest[-1]
        z = ALPHA * hb + (coef * m[k_gate:k_gate + 1]) * yb
        mu = jnp.mean(z, axis=-1, keepdims=True)
        zc = z - mu
        var = jnp.mean(zc * zc, axis=-1, keepdims=True)
        rstd = lax.rsqrt(var + LN_EPS)
        xhat = zc * rstd
        out = xhat * ln[0:1] + ln[1:2]
        if nxt is None:
            return out, xhat, rstd
        mn = rest[0]
        return out, xhat, rstd, out * (1.0 + mn[nxt[2]:nxt[2] + 1]) + mn[nxt[1]:nxt[1] + 1]
    segs = [mod] if nxt is None else [mod, nxt[0]]
    outs = [(D, F32), (D, F32), (1, F32)] + ([] if nxt is None else [(D, BF16)])
    return rowwise(lay, name, fn, [h, y], segs=segs, vecs=[lnv], outs=outs, rider=rider)


def _ln_bwd_math(do, xh, rs, yb, gate, coef, ln):
    dxh = do * ln[0:1]
    m1 = jnp.mean(dxh, axis=-1, keepdims=True)
    m2 = jnp.mean(dxh * xh, axis=-1, keepdims=True)
    dz = rs * (dxh - m1 - xh * m2)
    s = jnp.concatenate([jnp.sum(do, axis=0, keepdims=True), jnp.sum(do * xh, axis=0, keepdims=True),
                         jnp.sum(coef * dz * yb, axis=0, keepdims=True)], axis=0)
    return (coef * gate) * dz, ALPHA * dz, s


def _mod_bwd_math(dr, dm, hb, scale):
    s = jnp.concatenate([jnp.sum(dm, axis=0, keepdims=True), jnp.sum(dm * hb, axis=0, keepdims=True)], axis=0)
    return dr + dm * (1.0 + scale), s


def ln_bwd(lay, dout, xhat, rstd, y, mod, k_gate, coef, lnv, name):
    def fn(do, xh, rs, yb, m, ln):
        return _ln_bwd_math(do, xh, rs, yb, m[k_gate:k_gate + 1], coef, ln)
    return rowwise(lay, name, fn, [dout, xhat, rstd, y], segs=[mod], vecs=[lnv],
                   outs=[(D, BF16), (D, F32)], sums=[(3, D)])


def mod_bwd(lay, dres, dhm, h, mod, k_scale, name, rider=None):
    def fn(dr, dm, hb, m):
        return _mod_bwd_math(dr, dm, hb, m[k_scale:k_scale + 1])
    return rowwise(lay, name, fn, [dres, dhm, h], segs=[mod], outs=[(D, F32, "lat")], sums=[(2, D)], rider=rider)


def modb_lnb(lay, dres, dhm, mod, k_scale, xhat, rstd, y, mod_p, k_gate, coef, lnv, name, rider=None):
    def fn(dr, dm, xh, rs, yb, m, mp, ln):
        dh, s2 = _mod_bwd_math(dr, dm, xh * ln[0:1] + ln[1:2], m[k_scale:k_scale + 1])
        dy, dres_p, s1 = _ln_bwd_math(dh, xh, rs, yb, mp[k_gate:k_gate + 1], coef, ln)
        return dy, dres_p, s1, s2
    return rowwise(lay, name, fn, [dres, dhm, xhat, rstd, y], segs=[mod, mod_p], vecs=[lnv],
                   outs=[(D, BF16), (D, F32)], sums=[(3, D), (2, D)], rider=rider)


def block_sums(lay, parts_list, name):
    n = len(parts_list)

    def body(*refs):
        for p_ref, o_ref in zip(refs[:n], refs[n:]):
            acc = [None, None, None]
            for i in range(lay.nblk):
                sg = 2 if i % lay.bps < lay.cb else i // lay.bps
                acc[sg] = p_ref[i] if acc[sg] is None else acc[sg] + p_ref[i]
            for k in range(3):
                o_ref[k] = acc[k]
            o_ref[3] = (acc[0] + acc[1]) + acc[2]

    return pl.pallas_call(body, name=name, out_shape=[_sds((4,) + p.shape[1:], F32) for p in parts_list],
                          in_specs=[VMEM_SPEC] * n, out_specs=[VMEM_SPEC] * n)(*parts_list)


def mm_nn(a, b, name, out_dtype=F32):
    m, k = a.shape
    n = b.shape[1]
    tm = _pick(m, (1152, 768, 512, 256, 128, 64, 32, 16, 8))
    tn = _pick(n, (1024, 768, 640, 512, 384, 256, 128))

    def body(a_ref, b_ref, o_ref):
        o_ref[...] = _nn(a_ref[...].astype(BF16), b_ref[...].astype(BF16)).astype(o_ref.dtype)

    return pl.pallas_call(body, name=name, out_shape=_sds((m, n), out_dtype), grid=(m // tm, n // tn),
                          in_specs=[pl.BlockSpec((tm, k), lambda i, j: (i, 0)), pl.BlockSpec((k, tn), lambda i, j: (0, j))],
                          out_specs=pl.BlockSpec((tm, tn), lambda i, j: (i, j)),
                          compiler_params=_cp(("parallel", "parallel"), VMEM_BIG))(a, b)


def mm_nt(a, b, name, out_dtype=F32, rider=None):
    m, k = a.shape
    n = b.shape[0]
    tm = _pick(m, (1152, 768, 512, 256, 128, 64, 32, 16, 8))
    tn = _pick(n, (1024, 768, 640, 512, 384, 256, 128))

    def body(a_ref, b_ref, o_ref):
        o_ref[...] = _nt(a_ref[...].astype(BF16), b_ref[...].astype(BF16)).astype(o_ref.dtype)

    res = _host_call(body, rider, name, (m // tm, n // tn),
                     [pl.BlockSpec((tm, k), lambda i, j: (i, 0)), pl.BlockSpec((tn, k), lambda i, j: (j, 0))],
                     [pl.BlockSpec((tm, tn), lambda i, j: (i, j))], [_sds((m, n), out_dtype)], (a, b),
                     ("parallel", "parallel"), 2, 1)
    return res[0] if rider is None else res


def mm_tn(a, b, name, rider=None):
    t, m = a.shape
    n = b.shape[1]
    tk = _pick(t, (1152, 768, 512, 256, 128, 64, 32, 16))
    tm = _pick(m, (512, 384, 256, 128))

    def body(a_ref, b_ref, o_ref):
        @pl.when(pl.program_id(1) == 0)
        def _():
            o_ref[...] = jnp.zeros_like(o_ref)
        o_ref[...] += _tn(a_ref[...].astype(BF16), b_ref[...].astype(BF16))

    res = _host_call(body, rider, name, (m // tm, t // tk),
                     [pl.BlockSpec((tk, tm), lambda i, k: (k, i)), pl.BlockSpec((tk, n), lambda i, k: (k, 0))],
                     [pl.BlockSpec((tm, n), lambda i, k: (i, 0))], [_sds((m, n), F32)], (a, b),
                     ("parallel", "arbitrary"), 2, 1)
    return res[0] if rider is None else res


class Rider:
    def __init__(self, ins, outs, aliases, nsem, start, wait):
        self.ins, self.outs, self.aliases, self.nsem, self.start, self.wait = ins, outs, aliases, nsem, start, wait


def _chips_of(mx, my):
    return [(1 - mx, my), (mx, 1 - my), (1 - mx, 1 - my)]


def rider_gather_d2d(buf):
    def start(ins, outs, ssem, rsem):
        o = outs[0]
        mx, my, mc = _me()
        for j, (px, py) in enumerate(_chips_of(mx, my)):
            ps = 2 * px + py
            _rcopy(o.at[ps, mc], o.at[ps, mc], ssem.at[j], rsem.at[j], (mx, my, 1 - mc)).start()

    def wait(ins, outs, ssem, rsem):
        o = outs[0]
        mx, my, mc = _me()
        sib = (mx, my, 1 - mc)
        for j, (px, py) in enumerate(_chips_of(mx, my)):
            ps = 2 * px + py
            _rcopy(o.at[ps, 1 - mc], o.at[ps, 1 - mc], ssem.at[j], rsem.at[j], sib).wait_recv()
        for j, (px, py) in enumerate(_chips_of(mx, my)):
            ps = 2 * px + py
            _rcopy(o.at[ps, mc], o.at[ps, mc], ssem.at[j], rsem.at[j], sib).wait_send()

    return Rider([buf], [_sds(buf.shape, buf.dtype)], {0: 0}, 3, start, wait)


def rider_reduce_sib(buf):
    n, _, h, w = buf.shape

    def start(ins, outs, ssem, rsem):
        mx, my, mc = _me()
        for k in range(N_CHIP):
            _rcopy(ins[0].at[k, 1 - mc], outs[0].at[k], ssem.at[k], rsem.at[k], (mx, my, 1 - mc)).start()

    def wait(ins, outs, ssem, rsem):
        mx, my, mc = _me()
        for k in range(N_CHIP):
            _rcopy(ins[0].at[k, 1 - mc], outs[0].at[k], ssem.at[k], rsem.at[k], (mx, my, 1 - mc)).wait_recv()
        for k in range(N_CHIP):
            _rcopy(ins[0].at[k, 1 - mc], outs[0].at[k], ssem.at[k], rsem.at[k], (mx, my, 1 - mc)).wait_send()

    return Rider([buf], [_sds((n, h, w), buf.dtype)], {}, N_CHIP, start, wait)


def rider_gather_xy(buf):
    def peers():
        mx, my, mc = _me()
        return 2 * mx + my, mc, [(1 - mx, my), (mx, 1 - my)]

    def start(ins, outs, ssem, rsem):
        o = outs[0]
        s, mc, nb = peers()
        for j, (px, py) in enumerate(nb):
            _rcopy(o.at[s, mc], o.at[s, mc], ssem.at[j], rsem.at[j], (px, py, mc)).start()

    def wait(ins, outs, ssem, rsem):
        o = outs[0]
        s, mc, nb = peers()
        for j, (px, py) in enumerate(nb):
            _rcopy(o.at[2 * px + py, mc], o.at[2 * px + py, mc], ssem.at[j], rsem.at[j], (px, py, mc)).wait_recv()
        for j, (px, py) in enumerate(nb):
            _rcopy(o.at[s, mc], o.at[s, mc], ssem.at[j], rsem.at[j], (px, py, mc)).wait_send()

    return Rider([buf], [_sds(buf.shape, buf.dtype)], {0: 0}, 2, start, wait)


def rider_gather_fwd(buf):
    h2 = buf.shape[2] // 2
    lo, hi = pl.ds(0, h2), pl.ds(h2, buf.shape[2] - h2)

    def start(ins, outs, ssem, rsem):
        o = outs[0]
        mx, my, mc = _me()
        xs, ys = 2 * (1 - mx) + my, 2 * mx + (1 - my)
        _rcopy(o.at[xs, mc, lo], o.at[xs, mc, lo], ssem.at[0], rsem.at[0], (mx, 1 - my, mc)).start()
        _rcopy(o.at[ys, mc, hi], o.at[ys, mc, hi], ssem.at[1], rsem.at[1], (1 - mx, my, mc)).start()

    def wait(ins, outs, ssem, rsem):
        o = outs[0]
        mx, my, mc = _me()
        xs, ys, ds = 2 * (1 - mx) + my, 2 * mx + (1 - my), 2 * (1 - mx) + (1 - my)
        _rcopy(o.at[ds, mc, lo], o.at[ds, mc, lo], ssem.at[0], rsem.at[0], (mx, 1 - my, mc)).wait_recv()
        _rcopy(o.at[ds, mc, hi], o.at[ds, mc, hi], ssem.at[1], rsem.at[1], (1 - mx, my, mc)).wait_recv()
        _rcopy(o.at[xs, mc, lo], o.at[xs, mc, lo], ssem.at[0], rsem.at[0], (mx, 1 - my, mc)).wait_send()
        _rcopy(o.at[ys, mc, hi], o.at[ys, mc, hi], ssem.at[1], rsem.at[1], (1 - mx, my, mc)).wait_send()

    return Rider([buf], [_sds(buf.shape, buf.dtype)], {0: 0}, 2, start, wait)


def rider_reduce_copy(q, j, r=None):
    def peer():
        mx, my, mc = _me()
        px, py = _chips_of(mx, my)[j]
        return 2 * mx + my, 2 * px + py, (px, py, mc)

    def start(ins, outs, ssem, rsem):
        s, ps, dev = peer()
        _rcopy(ins[0].at[ps], outs[0].at[s], ssem.at[0], rsem.at[0], dev).start()

    def wait(ins, outs, ssem, rsem):
        s, ps, dev = peer()
        _rcopy(ins[0].at[ps], outs[0].at[ps], ssem.at[0], rsem.at[0], dev).wait_recv()
        _rcopy(ins[0].at[ps], outs[0].at[s], ssem.at[0], rsem.at[0], dev).wait_send()

    if r is None:
        return Rider([q], [_sds(q.shape, q.dtype)], {}, 1, start, wait)
    return Rider([q, r], [_sds(q.shape, q.dtype)], {1: 0}, 1, start, wait)


def rider_reduce_copies(q):
    def start(ins, outs, ssem, rsem):
        mx, my, mc = _me()
        s = 2 * mx + my
        for j, (px, py) in enumerate(_chips_of(mx, my)):
            _rcopy(ins[0].at[2 * px + py], outs[0].at[s], ssem.at[j], rsem.at[j], (px, py, mc)).start()

    def wait(ins, outs, ssem, rsem):
        mx, my, mc = _me()
        s = 2 * mx + my
        for j, (px, py) in enumerate(_chips_of(mx, my)):
            ps = 2 * px + py
            _rcopy(ins[0].at[ps], outs[0].at[ps], ssem.at[j], rsem.at[j], (px, py, mc)).wait_recv()
        for j, (px, py) in enumerate(_chips_of(mx, my)):
            _rcopy(ins[0].at[2 * px + py], outs[0].at[s], ssem.at[j], rsem.at[j], (px, py, mc)).wait_send()

    return Rider([q], [_sds(q.shape, q.dtype)], {}, 3, start, wait)


def rider_join(buf, g=None):
    def start(ins, outs, ssem, rsem):
        o = outs[0] if g is None else outs[0].at[g]
        mx, my, mc = _me()
        _rcopy(o.at[mc], o.at[mc], ssem.at[0], rsem.at[0], (mx, my, 1 - mc)).start()

    def wait(ins, outs, ssem, rsem):
        o = outs[0] if g is None else outs[0].at[g]
        mx, my, mc = _me()
        _rcopy(o.at[1 - mc], o.at[1 - mc], ssem.at[0], rsem.at[0], (mx, my, 1 - mc)).wait_recv()
        _rcopy(o.at[mc], o.at[mc], ssem.at[0], rsem.at[0], (mx, my, 1 - mc)).wait_send()

    return Rider([buf], [_sds(buf.shape, buf.dtype)], {0: 0}, 1, start, wait)


def _host_call(body, rider, name, grid, in_specs, out_specs, out_shape, operands, sem, n_in, n_out, aliases=None):
    aliases = dict(aliases or {})
    if rider is None:
        return pl.pallas_call(body, name=name, out_shape=out_shape, grid=grid, in_specs=in_specs, out_specs=out_specs,
                              input_output_aliases=aliases, compiler_params=_cp(sem, VMEM_BIG))(*operands)
    n_ri, n_ro = len(rider.ins), len(rider.outs)
    aliases.update({n_in + a: n_out + b for a, b in rider.aliases.items()})

    def hosted(*refs):
        ins, r_in = refs[:n_in], refs[n_in:n_in + n_ri]
        outs, r_out = refs[n_in + n_ri:n_in + n_ri + n_out], refs[n_in + n_ri + n_out:n_in + n_ri + n_out + n_ro]
        ssem, rsem = refs[-2], refs[-1]
        first = functools.reduce(lambda a, b: a & b, [pl.program_id(k) == 0 for k in range(len(grid))])
        last = functools.reduce(lambda a, b: a & b, [pl.program_id(k) == grid[k] - 1 for k in range(len(grid))])

        @pl.when(first)
        def _():
            rider.start(r_in, r_out, ssem, rsem)
        body(*ins, *outs)

        @pl.when(last)
        def _():
            rider.wait(r_in, r_out, ssem, rsem)

    return pl.pallas_call(
        hosted, name=name, out_shape=list(out_shape) + list(rider.outs), grid=grid,
        in_specs=list(in_specs) + [ANY] * n_ri, out_specs=list(out_specs) + [ANY] * n_ro,
        input_output_aliases=aliases,
        scratch_shapes=[pltpu.SemaphoreType.DMA((rider.nsem,)), pltpu.SemaphoreType.DMA((rider.nsem,))],
        compiler_params=_cp(("arbitrary",) * len(grid), VMEM_BIG))(*operands, *rider.ins)


def ffn_up(lay, hm, wbuf, ig, iu, ns, name, rider=None):
    tm = lay.tm

    def body(h_ref, wg_ref, wu_ref, up_ref, sl_ref, a_ref):
        hb = h_ref[...]
        g = _nt(hb, wg_ref[0])
        u = _nt(hb, wu_ref[0])
        sg = _sigmoid(g)
        sl = g * sg
        up_ref[0] = (u * (sg + sl * (1.0 - sg))).astype(BF16)
        sl_ref[0] = sl.astype(BF16)
        a_ref[0] = (sl * u).astype(BF16)

    spec_o = pl.BlockSpec((1, tm, ns), lambda s, i: (s, i, 0))
    return _host_call(
        body, rider, name, (N_CHIP, lay.T // tm),
        [pl.BlockSpec((tm, D), lambda s, i: (i, 0)), pl.BlockSpec((1, ns, D), lambda s, i: (s, ig, 0)),
         pl.BlockSpec((1, ns, D), lambda s, i: (s, iu, 0))],
        [spec_o] * 3, [_sds((N_CHIP, lay.T, ns), BF16)] * 3, (hm, wbuf, wbuf), ("parallel", "parallel"), 3, 3)


def slab_nn_acc(lay, zs, wbuf, idxs, ns, name, rider=None):
    tm = lay.tm2
    npair = len(zs)

    def body(*refs):
        o_ref = refs[-1]

        @pl.when(pl.program_id(1) == 0)
        def _():
            o_ref[...] = jnp.zeros_like(o_ref)
        acc = _nn(refs[0][0], refs[npair][0])
        for p in range(1, npair):
            acc += _nn(refs[p][0], refs[npair + p][0])
        o_ref[...] += acc

    in_specs = [pl.BlockSpec((1, tm, ns), lambda i, s: (s, i, 0)) for _ in zs]
    in_specs += [pl.BlockSpec((1, ns, D), functools.partial(lambda i, s, q: (s, q, 0), q=q)) for q in idxs]
    return _host_call(body, rider, name, (lay.T // tm, N_CHIP), in_specs, [pl.BlockSpec((tm, D), lambda i, s: (i, 0))],
                      [_sds((lay.T, D), F32)], (*zs, *([wbuf] * npair)), ("parallel", "arbitrary"), 2 * npair, 1)


def ffn_bwd_da(lay, dy, wbuf, idn, up, sl, ns, name, rider=None):
    tm = lay.tm

    def body(dy_ref, wd_ref, up_ref, sl_ref, dg_ref, du_ref):
        da = _nt(dy_ref[...], wd_ref[0])
        dg_ref[0] = (da * up_ref[0].astype(F32)).astype(BF16)
        du_ref[0] = (da * sl_ref[0].astype(F32)).astype(BF16)

    spec_z = pl.BlockSpec((1, tm, ns), lambda s, i: (s, i, 0))
    return _host_call(
        body, rider, name, (N_CHIP, lay.T // tm),
        [pl.BlockSpec((tm, D), lambda s, i: (i, 0)), pl.BlockSpec((1, ns, D), lambda s, i: (s, idn, 0)), spec_z, spec_z],
        [spec_z] * 2, [_sds((N_CHIP, lay.T, ns), BF16)] * 2, (dy, wbuf, up, sl), ("parallel", "parallel"), 4, 2)


def slab_tn(lay, z, x, gbuf, idx, ns, name, rider=None):
    tk = _pick(lay.T, (768, 512, 256, 128))

    def body(z_ref, x_ref, g_in, o_ref):
        del g_in

        @pl.when(pl.program_id(0) == 0)
        def _():
            o_ref[...] = jnp.zeros_like(o_ref)
        xv = x_ref[...]
        for s in range(N_CHIP):
            o_ref[s] += _tn(z_ref[s], xv)

    return _host_call(
        body, rider, name, (lay.T // tk,),
        [pl.BlockSpec((N_CHIP, tk, ns), lambda k: (0, k, 0)), pl.BlockSpec((tk, D), lambda k: (k, 0)), ANY],
        [pl.BlockSpec((N_CHIP, ns, D), lambda k: (0, idx, 0))], [_sds(gbuf.shape, F32)], (z, x, gbuf),
        ("arbitrary",), 3, 1, aliases={2: 0})


Q0, K0, V0, U0, QR0, KR0, PEXT = 0, 512, 640, 768, 1280, 1792, 1920


def rope_fwd(lay, p, cos, sin, name):
    def fn(pb, cs, sn):
        cs4 = jnp.concatenate([cs] * 4, axis=1)
        sn4 = jnp.concatenate([sn] * 4, axis=1)
        qr = pb[:, Q0:K0] * cs4 + pb[:, QR0:KR0] * sn4
        kr = pb[:, K0:V0] * cs + pb[:, KR0:PEXT] * sn
        return qr, kr, pb[:, V0:U0], pb[:, U0:QR0]
    return rowwise(lay, name, fn, [p, cos, sin], outs=[(ATT_W, BF16), (KV_W, BF16), (KV_W, BF16), (POOL_W, F32)])


def rope_bwd(lay, dqr, dkr, dv, du, cos, sin, name):
    def fn(dq, dk, dvb, dub, cs, sn):
        cs4 = jnp.concatenate([cs] * 4, axis=1)
        sn4 = jnp.concatenate([sn] * 4, axis=1)
        return (jnp.concatenate([dq * cs4, dk * cs, dvb, dub, dq * sn4, dk * sn], axis=1),)
    return rowwise(lay, name, fn, [dqr, dkr, dv, du, cos, sin], outs=[(PEXT, BF16)])[0]


def _attn_specs(lay):
    nbs, cbk, lbk = lay.PS // BLK, lay.C // BLK, lay.L // BLK

    def kv_map(j):
        return lambda s, n: (s * nbs + cbk + jnp.clip(n - cbk + j - 1, 0, lbk - 1), 0)

    win = [pl.BlockSpec((BLK, KV_W), kv_map(j)) for j in range(3)]
    ctx = pl.BlockSpec((lay.C, KV_W), lambda s, n: (s * (lay.PS // lay.C), 0))
    return nbs, cbk, lbk, win, ctx


def _attn_masks(n, cbk, lbk):
    row = lax.broadcasted_iota(jnp.int32, (BLK, BLK), 0)
    col = lax.broadcasted_iota(jnp.int32, (BLK, BLK), 1)
    m = n - cbk
    lat = n >= cbk
    valid = [lat & (m >= 1) & (col >= row), lat & (col >= 0), lat & (m <= lbk - 2) & (col <= row)]
    lane_lo = lax.broadcasted_iota(jnp.int32, (BLK, 2 * HEAD_DIM), 1) < HEAD_DIM
    return valid, lane_lo


def attn_fwd(lay, qr, kr, vb, sink_tab, name):
    nbs, cbk, lbk, win, ctx = _attn_specs(lay)

    def body(q_ref, k0, k1, k2, kc_ref, v0, v1, v2, vc_ref, sk_ref, o_ref, l_ref):
        n = pl.program_id(1)
        valid, lane_lo = _attn_masks(n, cbk, lbk)
        valid4 = [jnp.concatenate([v] * 4, axis=0) for v in valid]
        ks = [k0[...], k1[...], k2[...]]
        vs = [v0[...], v1[...], v2[...]]
        kc, vc = kc_ref[...], vc_ref[...]
        q2s = [q_ref[:, p * 128:(p + 1) * 128] for p in range(4)]
        outs, lses = [], []
        for hh in range(2):
            sel = lane_lo == (hh == 0)
            qm = jnp.concatenate([jnp.where(sel, q2, jnp.zeros_like(q2)) for q2 in q2s], axis=0)
            sk = jnp.concatenate([jnp.broadcast_to(sk_ref[p:p + 1, hh * HEAD_DIM:hh * HEAD_DIM + 1], (BLK, 1))
                                  for p in range(4)], axis=0)
            sw = [jnp.where(valid4[j], _nt(qm, ks[j]) * ATT_SCALE, NEG_INF) for j in range(3)]
            sc = _nt(qm, kc) * ATT_SCALE
            mx = jnp.maximum(jnp.maximum(jnp.maximum(sw[0].max(-1, keepdims=True), sw[1].max(-1, keepdims=True)),
                                         jnp.maximum(sw[2].max(-1, keepdims=True), sc.max(-1, keepdims=True))), sk)
            ew = [jnp.exp(s - mx) for s in sw]
            ec = jnp.exp(sc - mx)
            den = ew[0].sum(-1, keepdims=True) + ew[1].sum(-1, keepdims=True) + ew[2].sum(-1, keepdims=True)
            den = den + ec.sum(-1, keepdims=True) + jnp.exp(sk - mx)
            o = _nn((ec / den).astype(BF16), vc)
            for j in range(3):
                o += _nn((ew[j] / den).astype(BF16), vs[j])
            outs.append(o)
            lses.append(mx + jnp.log(den))
        for p in range(4):
            rows = slice(p * BLK, (p + 1) * BLK)
            o_ref[:, p * 128:(p + 1) * 128] = jnp.where(lane_lo, outs[0][rows], outs[1][rows]).astype(o_ref.dtype)
            l_ref[:, p * 128:(p + 1) * 128] = jnp.where(lane_lo, jnp.broadcast_to(lses[0][rows], (BLK, 128)),
                                                        jnp.broadcast_to(lses[1][rows], (BLK, 128)))

    qspec = pl.BlockSpec((BLK, ATT_W), lambda s, n: (s * nbs + n, 0))
    return pl.pallas_call(
        body, name=name, out_shape=[_sds((lay.T, ATT_W), BF16), _sds((lay.T, ATT_W), F32)], grid=(2, nbs),
        in_specs=[qspec] + win + [ctx] + win + [ctx] + [pl.BlockSpec((8, 128), lambda s, n: (0, 0))],
        out_specs=[qspec, qspec], compiler_params=_cp(("parallel", "parallel")))(qr, kr, kr, kr, kr, vb, vb, vb, vb, sink_tab)


def attn_bwd(lay, qr, kr, vb, sink_tab, lse, datt, name, rider=None):
    nbs, cbk, lbk, win, ctx = _attn_specs(lay)
    C, PS = lay.C, lay.PS

    def body(q_ref, k0, k1, k2, kc_ref, v0, v1, v2, vc_ref, sk_ref, l_ref, do_ref, dq_ref, dk_ref, dv_ref, ds_ref):
        n = pl.program_id(1)
        valid, lane_lo = _attn_masks(n, cbk, lbk)

        @pl.when(n == 0)
        def _():
            dk_ref[...] = jnp.zeros_like(dk_ref)
            dv_ref[...] = jnp.zeros_like(dv_ref)
            ds_ref[...] = jnp.zeros_like(ds_ref)

        ks = [k0[...], k1[...], k2[...], kc_ref[...]]
        vs = [v0[...], v1[...], v2[...], vc_ref[...]]
        valid4 = [jnp.concatenate([v] * 4, axis=0) for v in valid]
        dks = [jnp.zeros((BLK, KV_W), F32)] * 3 + [jnp.zeros((C, KV_W), F32)]
        dvs = list(dks)
        q2s = [q_ref[:, p * 128:(p + 1) * 128] for p in range(4)]
        do2s = [do_ref[:, p * 128:(p + 1) * 128].astype(BF16) for p in range(4)]
        lse2s = [l_ref[:, p * 128:(p + 1) * 128] for p in range(4)]
        dq_h, dd_h = [], []
        for hh in range(2):
            sel = lane_lo == (hh == 0)
            qm = jnp.concatenate([jnp.where(sel, q2, jnp.zeros_like(q2)) for q2 in q2s], axis=0)
            dom = jnp.concatenate([jnp.where(sel, d2, jnp.zeros_like(d2)) for d2 in do2s], axis=0)
            lse_h = jnp.concatenate([l2[:, hh * HEAD_DIM:hh * HEAD_DIM + 1] for l2 in lse2s], axis=0)
            ps, dps = [], []
            for j in range(4):
                s = _nt(qm, ks[j]) * ATT_SCALE
                if j < 3:
                    s = jnp.where(valid4[j], s, NEG_INF)
                ps.append(jnp.exp(s - lse_h))
                dps.append(_nt(dom, vs[j]))
            dd = (ps[0] * dps[0]).sum(-1, keepdims=True) + (ps[1] * dps[1]).sum(-1, keepdims=True)
            dd = dd + (ps[2] * dps[2]).sum(-1, keepdims=True) + (ps[3] * dps[3]).sum(-1, keepdims=True)
            dq = jnp.zeros((4 * BLK, 128), F32)
            for j in range(4):
                dsb = (ps[j] * (dps[j] - dd) * ATT_SCALE).astype(BF16)
                dq += _nn(dsb, ks[j])
                dks[j] = dks[j] + _tn(dsb, qm)
                dvs[j] = dvs[j] + _tn(ps[j].astype(BF16), dom)
            dq_h.append(dq)
            dd_h.append(dd)
        for p in range(4):
            sl = slice(p * 128, (p + 1) * 128)
            rows = slice(p * BLK, (p + 1) * BLK)
            dq_ref[:, sl] = jnp.where(lane_lo, dq_h[0][rows], dq_h[1][rows])
            dd2 = jnp.where(lane_lo, jnp.broadcast_to(dd_h[0][rows], (BLK, 128)), jnp.broadcast_to(dd_h[1][rows], (BLK, 128)))
            psink = jnp.exp(sk_ref[p:p + 1, :] - lse2s[p])
            ds_ref[0, p:p + 1, :] += -jnp.sum(psink * dd2, axis=0, keepdims=True)
        dk_ref[0:C, :] += dks[3]
        dv_ref[0:C, :] += dvs[3]
        for j in range(3):
            r0 = pl.multiple_of((cbk + jnp.clip(n - cbk + j - 1, 0, lbk - 1)) * BLK, BLK)
            dk_ref[pl.ds(r0, BLK), :] += dks[j]
            dv_ref[pl.ds(r0, BLK), :] += dvs[j]

    qspec = pl.BlockSpec((BLK, ATT_W), lambda s, n: (s * nbs + n, 0))
    kvout = pl.BlockSpec((PS, KV_W), lambda s, n: (s, 0))
    return _host_call(
        body, rider, name, (2, nbs),
        [qspec] + win + [ctx] + win + [ctx] + [pl.BlockSpec((8, 128), lambda s, n: (0, 0)), qspec, qspec],
        [qspec, kvout, kvout, pl.BlockSpec((1, 8, 128), lambda s, n: (s, 0, 0))],
        [_sds((lay.T, ATT_W), F32), _sds((lay.T, KV_W), F32), _sds((lay.T, KV_W), F32), _sds((2, 8, 128), F32)],
        (qr, kr, kr, kr, kr, vb, vb, vb, vb, sink_tab, lse, datt), ("parallel", "arbitrary"), 12, 4)


def _winsum(x, r):
    n = x.shape[0]
    t = lax.broadcasted_iota(jnp.int32, x.shape, 0)
    acc = x
    for o in range(1, r + 1):
        acc = acc + jnp.where(t >= o, pltpu.roll(x, o, 0), 0.0) + jnp.where(t < n - o, pltpu.roll(x, n - o, 0), 0.0)
    return acc


def _wincount(n, r):
    t = lax.broadcasted_iota(jnp.int32, (n, 128), 0)
    return (jnp.minimum(t + r, n - 1) - jnp.maximum(t - r, 0) + 1).astype(F32)


def pool_fwd(lay, u, w_pool, scale, name):
    segs = [(0, lay.C), (lay.C, lay.L)]

    def body(u_ref, w_ref, s_ref, o_ref):
        for r0, n in segs:
            for g, wd in enumerate(POOL_WINDOWS):
                sl = slice(g * 128, (g + 1) * 128)
                x = u_ref[r0:r0 + n, sl]
                d = _winsum(x, wd // 2) / _wincount(n, wd // 2) - x
                y = _nn(d.astype(BF16), w_ref[g].astype(BF16)) * s_ref[:, sl]
                o_ref[r0:r0 + n, sl] = y.astype(o_ref.dtype)

    spec = pl.BlockSpec((lay.PS, POOL_W), lambda s: (s, 0))
    return pl.pallas_call(
        body, name=name, out_shape=_sds((lay.T, POOL_W), BF16), grid=(2,),
        in_specs=[spec, pl.BlockSpec(w_pool.shape, lambda s: (0, 0, 0)), pl.BlockSpec((1, POOL_W), lambda s: (0, 0))],
        out_specs=spec, compiler_params=_cp(("parallel",), VMEM_BIG))(u, w_pool, scale)


def pool_bwd(lay, u, dcat, w_pool, scale, name):
    segs = [(0, lay.C), (lay.C, lay.L)]

    def body(u_ref, dp_ref, w_ref, s_ref, du_ref, dw_ref, dsc_ref):
        for g, wd in enumerate(POOL_WINDOWS):
            sl = slice(g * 128, (g + 1) * 128)
            wb = w_ref[g].astype(BF16)
            dw = jnp.zeros((128, 128), F32)
            dsc = jnp.zeros((1, 128), F32)
            for r0, n in segs:
                x = u_ref[r0:r0 + n, sl]
                cnt = _wincount(n, wd // 2)
                d = (_winsum(x, wd // 2) / cnt - x).astype(BF16)
                dp = dp_ref[r0:r0 + n, sl]
                dsc += jnp.sum(_nn(d, wb) * dp, axis=0, keepdims=True)
                dyp = (dp * s_ref[:, sl]).astype(BF16)
                dw += _tn(d, dyp)
                dd = _nt(dyp, wb)
                du_ref[r0:r0 + n, sl] = _winsum(dd / cnt, wd // 2) - dd
            dw_ref[0, g] = dw
            dsc_ref[0, :, sl] = dsc

    spec = pl.BlockSpec((lay.PS, POOL_W), lambda s: (s, 0))
    return pl.pallas_call(
        body, name=name,
        out_shape=[_sds((lay.T, POOL_W), F32), _sds((2, 4, 128, 128), F32), _sds((2, 1, POOL_W), F32)], grid=(2,),
        in_specs=[spec, pl.BlockSpec((lay.PS, POOL_W), lambda s: (s, 1)), pl.BlockSpec(w_pool.shape, lambda s: (0, 0, 0)),
                  pl.BlockSpec((1, POOL_W), lambda s: (0, 0))],
        out_specs=[spec, pl.BlockSpec((1, 4, 128, 128), lambda s: (s, 0, 0, 0)), pl.BlockSpec((1, 1, POOL_W), lambda s: (s, 0, 0))],
        compiler_params=_cp(("parallel",), VMEM_BIG))(u, dcat, w_pool, scale)


CONV_OFFS = (-1, 0, 1, 2)
CW = 256


def _shift_rows(x, o):
    if o == 0:
        return x
    n = x.shape[0]
    t = lax.broadcasted_iota(jnp.int32, x.shape, 0)
    if o < 0:
        return jnp.where(t >= -o, pltpu.roll(x, -o, 0), 0.0)
    return jnp.where(t < n - o, pltpu.roll(x, n - o, 0), 0.0)


def conv_fwd(lay, p, col0, w, b, name):
    segs = [(0, lay.C), (lay.C, lay.L)]
    cb0 = col0 // CW

    def body(x_ref, w_ref, b_ref, o_ref):
        for r0, n in segs:
            x = x_ref[r0:r0 + n, :]
            y = jnp.broadcast_to(b_ref[...], x.shape)
            for k, o in enumerate(CONV_OFFS):
                y = y + _shift_rows(x, o) * w_ref[k:k + 1, :]
            o_ref[r0:r0 + n, :] = y

    return pl.pallas_call(
        body, name=name, out_shape=_sds((lay.T, D), F32), grid=(2, D // CW),
        in_specs=[pl.BlockSpec((lay.PS, CW), lambda s, j: (s, cb0 + j)), pl.BlockSpec((4, CW), lambda s, j: (0, j)),
                  pl.BlockSpec((1, CW), lambda s, j: (0, j))],
        out_specs=pl.BlockSpec((lay.PS, CW), lambda s, j: (s, j)),
        compiler_params=_cp(("parallel", "parallel")))(p, w, b)


def conv_bwd(lay, p, col0, w, duc, name):
    segs = [(0, lay.C), (lay.C, lay.L)]
    cb0 = col0 // CW

    def body(x_ref, w_ref, g_ref, du_ref, dw_ref, db_ref):
        dws = [jnp.zeros((1, CW), F32)] * 4
        db = jnp.zeros((1, CW), F32)
        for r0, n in segs:
            x = x_ref[r0:r0 + n, :]
            g = g_ref[r0:r0 + n, :]
            du = jnp.zeros_like(g)
            for k, o in enumerate(CONV_OFFS):
                du = du + _shift_rows(g, -o) * w_ref[k:k + 1, :]
                dws[k] = dws[k] + jnp.sum(g * _shift_rows(x, o), axis=0, keepdims=True)
            db = db + jnp.sum(g, axis=0, keepdims=True)
            du_ref[r0:r0 + n, :] = du.astype(du_ref.dtype)
        dw_ref[0] = jnp.concatenate(dws, axis=0)
        db_ref[0] = db

    return pl.pallas_call(
        body, name=name, out_shape=[_sds((lay.T, D), BF16), _sds((2, 4, D), F32), _sds((2, 1, D), F32)], grid=(2, D // CW),
        in_specs=[pl.BlockSpec((lay.PS, CW), lambda s, j: (s, cb0 + j)), pl.BlockSpec((4, CW), lambda s, j: (0, j)),
                  pl.BlockSpec((lay.PS, CW), lambda s, j: (s, j))],
        out_specs=[pl.BlockSpec((lay.PS, CW), lambda s, j: (s, j)), pl.BlockSpec((1, 4, CW), lambda s, j: (s, 0, j)),
                   pl.BlockSpec((1, 1, CW), lambda s, j: (s, 0, j))],
        compiler_params=_cp(("parallel", "parallel")))(p, w, duc)


def _softplus_neg(lam):
    z = -lam
    w = jnp.exp(-jnp.abs(z))
    log1p = jnp.where(w < 1e-2, w * (1.0 - w * (0.5 - w / 3.0)), jnp.log(1.0 + w))
    return jnp.maximum(z, 0.0) + log1p, -_sigmoid(z)


def _neg_expm1(x):
    series = -x * (1.0 + x * (0.5 + x * (1.0 / 6.0 + x * (1.0 / 24.0 + x * (1.0 / 120.0)))))
    return jnp.where(x > -0.05, series, 1.0 - jnp.exp(x))


def _lru_gates(x, xb, wa, wx, ba, bx, lam):
    r = _sigmoid(_nn(xb, wa.astype(BF16)) + ba)
    gi = _sigmoid(_nn(xb, wx.astype(BF16)) + bx)
    sp, dsp = _softplus_neg(lam)
    la = -LRU_C * r * sp
    a = jnp.exp(la)
    sq = jnp.sqrt(_neg_expm1(2.0 * la))
    return r, gi, sp, dsp, a, sq


def lru_coeffs(lay, uc, wa, wx, vec, name):
    tr = lay.tc

    def body(x_ref, wa_ref, wx_ref, v_ref, a_ref, b_ref):
        for h in range(8):
            sl = slice(h * 128, (h + 1) * 128)
            x = x_ref[:, sl]
            xb = x.astype(BF16)
            for d in range(2):
                _, gi, _, _, a, sq = _lru_gates(x, xb, wa_ref[d, h], wx_ref[d, h], v_ref[d:d + 1, sl],
                                                v_ref[2 + d:3 + d, sl], v_ref[4 + d:5 + d, sl])
                a_ref[d, h] = a
                b_ref[d, h] = sq * (gi * x)

    wspec = pl.BlockSpec((2, 8, 128, 128), lambda i: (0, 0, 0, 0))
    ospec = pl.BlockSpec((2, 8, tr, 128), lambda i: (0, 0, i, 0))
    return pl.pallas_call(
        body, name=name, out_shape=[_sds((2, 8, lay.T, 128), F32)] * 2, grid=(lay.T // tr,),
        in_specs=[pl.BlockSpec((tr, D), lambda i: (i, 0)), wspec, wspec, pl.BlockSpec((6, D), lambda i: (0, 0))],
        out_specs=[ospec, ospec], compiler_params=_cp(("parallel",), VMEM_BIG))(uc, wa, wx, vec)


def lru_coeffs_bwd(lay, uc, wa, wx, vec, da, db, name, rider=None):
    tr = lay.tc

    def body(x_ref, wa_ref, wx_ref, v_ref, da_ref, db_ref, dx_ref, dwa_ref, dwx_ref, dv_ref):
        @pl.when(pl.program_id(0) == 0)
        def _():
            dwa_ref[...] = jnp.zeros_like(dwa_ref)
            dwx_ref[...] = jnp.zeros_like(dwx_ref)
            dv_ref[...] = jnp.zeros_like(dv_ref)

        for h in range(8):
            sl = slice(h * 128, (h + 1) * 128)
            x = x_ref[:, sl]
            xb = x.astype(BF16)
            dx = jnp.zeros_like(x)
            for d in range(2):
                wab, wxb = wa_ref[d, h].astype(BF16), wx_ref[d, h].astype(BF16)
                r, gi, sp, dsp, a, sq = _lru_gates(x, xb, wa_ref[d, h], wx_ref[d, h], v_ref[d:d + 1, sl],
                                                   v_ref[2 + d:3 + d, sl], v_ref[4 + d:5 + d, sl])
                dbv, dav = db_ref[d, h], da_ref[d, h]
                t1 = dbv * sq
                dgi = t1 * x
                dx = dx + t1 * gi
                dla = dav * a - (dbv * gi * x) * (a * a) / sq
                dr = dla * (-LRU_C * sp)
                dlam = jnp.sum(dla * (-LRU_C * r), axis=0, keepdims=True) * dsp
                dpa = dr * r * (1.0 - r)
                dpx = dgi * gi * (1.0 - gi)
                dpab, dpxb = dpa.astype(BF16), dpx.astype(BF16)
                dwa_ref[d, h] += _tn(xb, dpab)
                dwx_ref[d, h] += _tn(xb, dpxb)
                dx = dx + _nt(dpab, wab) + _nt(dpxb, wxb)
                dv_ref[d:d + 1, sl] += jnp.sum(dpa, axis=0, keepdims=True)
                dv_ref[2 + d:3 + d, sl] += jnp.sum(dpx, axis=0, keepdims=True)
                dv_ref[4 + d:5 + d, sl] += dlam
            dx_ref[:, sl] = dx

    wspec = pl.BlockSpec((2, 8, 128, 128), lambda i: (0, 0, 0, 0))
    gspec = pl.BlockSpec((2, 8, tr, 128), lambda i: (0, 0, i, 0))
    vspec = pl.BlockSpec((6, D), lambda i: (0, 0))
    xspec = pl.BlockSpec((tr, D), lambda i: (i, 0))
    return _host_call(
        body, rider, name, (lay.T // tr,), [xspec, wspec, wspec, vspec, gspec, gspec], [xspec, wspec, wspec, vspec],
        [_sds((lay.T, D), F32), _sds((2, 8, 128, 128), F32), _sds((2, 8, 128, 128), F32), _sds((6, D), F32)],
        (uc, wa, wx, vec, da, db), ("arbitrary",), 6, 4)


GB = 2
SCAN_UNROLL = 8


def _tile_scan(a, b, up):
    t = lax.broadcasted_iota(jnp.int32, a.shape, 0)
    for d in (1, 2, 4):
        sh = 8 - d if up else d
        m = (t < 8 - d) if up else (t >= d)
        a_prev, b_prev = pltpu.roll(a, sh, 0), pltpu.roll(b, sh, 0)
        b = jnp.where(m, a * b_prev + b, b)
        a = jnp.where(m, a * a_prev, a)
    return a, b


def lru_scan(lay, a, b, name):
    segs = [(0, lay.C), (lay.C, lay.L)]

    def body(a_ref, b_ref, s_ref):
        for d in range(2):
            rev = d == 1
            state = tuple(jnp.zeros((1, 128), F32) for _ in range(GB))
            for base, n in segs:
                nt = n // 8

                def step(j, c, base=base, nt=nt, rev=rev, d=d):
                    c = list(c)
                    for u in range(SCAN_UNROLL):
                        jj = j * SCAN_UNROLL + u
                        r0 = pl.multiple_of(base + 8 * ((nt - 1 - jj) if rev else jj), 8)
                        for g in range(GB):
                            at, bt = _tile_scan(a_ref[d, g, pl.ds(r0, 8), :], b_ref[d, g, pl.ds(r0, 8), :], rev)
                            h = at * c[g] + bt
                            s_ref[d, g, pl.ds(r0, 8), :] = h
                            c[g] = h[0:1] if rev else h[7:8]
                    return tuple(c)

                state = lax.fori_loop(0, nt // SCAN_UNROLL, step, state)

    spec = pl.BlockSpec((2, GB, lay.PS, 128), lambda s, hb: (0, hb, s, 0))
    return pl.pallas_call(
        body, name=name, out_shape=_sds((2, 8, lay.T, 128), F32), grid=(2, 8 // GB),
        in_specs=[spec, spec], out_specs=spec, compiler_params=_cp(("parallel", "parallel"), VMEM_BIG))(a, b)


def lru_scan_bwd(lay, a, s, dy, name):
    segs = [(0, lay.C), (lay.C, lay.L)]
    C, PS = lay.C, lay.PS

    def body(a_ref, s_ref, g_ref, da_ref, db_ref):
        t = lax.broadcasted_iota(jnp.int32, (8, 128), 0)
        for d in range(2):
            rev = d == 1
            carry = tuple(jnp.zeros((1, 128), F32) for _ in range(GB))
            for si in (1, 0):
                base, n = segs[si]
                nt = n // 8

                def step(j, c, base=base, nt=nt, rev=rev, d=d):
                    c = list(c)
                    for u in range(SCAN_UNROLL):
                        jj = j * SCAN_UNROLL + u
                        r0 = pl.multiple_of(base + 8 * (jj if rev else (nt - 1 - jj)), 8)
                        if rev:
                            rn = pl.multiple_of(jnp.where(r0 == PS - 8, 0, r0 + 8), 8)
                            nb_zero = r0 == C - 8
                        else:
                            rn = pl.multiple_of(jnp.maximum(r0 - 8, 0), 8)
                            nb_zero = r0 == 0
                        for g in range(GB):
                            av = a_ref[d, g, pl.ds(r0, 8), :]
                            gv = g_ref[g, pl.ds(r0, 8), :]
                            sv = s_ref[d, g, pl.ds(r0, 8), :]
                            nbt = s_ref[d, g, pl.ds(rn, 8), :]
                            at, bt = _tile_scan(av, av * gv, not rev)
                            m = at * c[g] + bt
                            if rev:
                                m_next = jnp.where(t >= 1, pltpu.roll(m, 1, 0), c[g])
                                nb = jnp.where(nb_zero, 0.0, nbt[0:1])
                                h_prev = jnp.where(t < 7, pltpu.roll(sv, 7, 0), nb)
                                c[g] = m[7:8]
                            else:
                                m_next = jnp.where(t < 7, pltpu.roll(m, 7, 0), c[g])
                                nb = jnp.where(nb_zero, 0.0, nbt[7:8])
                                h_prev = jnp.where(t >= 1, pltpu.roll(sv, 1, 0), nb)
                                c[g] = m[0:1]
                            lam = gv + m_next
                            db_ref[d, g, pl.ds(r0, 8), :] = lam
                            da_ref[d, g, pl.ds(r0, 8), :] = lam * h_prev
                    return tuple(c)

                carry = lax.fori_loop(0, nt // SCAN_UNROLL, step, carry)

    spec = pl.BlockSpec((2, GB, lay.PS, 128), lambda s, hb: (0, hb, s, 0))
    return pl.pallas_call(
        body, name=name, out_shape=[_sds((2, 8, lay.T, 128), F32)] * 2, grid=(2, 8 // GB),
        in_specs=[spec, spec, pl.BlockSpec((GB, lay.PS, 128), lambda s, hb: (hb, s, 0))],
        out_specs=[spec, spec], compiler_params=_cp(("parallel", "parallel"), VMEM_BIG))(a, s, dy)


def _gelu(x):
    k = math.sqrt(2.0 / math.pi)
    t = jnp.tanh(k * (x + 0.044715 * x * x * x))
    return 0.5 * x * (1.0 + t), 0.5 * (1.0 + t) + 0.5 * x * (1.0 - t * t) * k * (1.0 + 3 * 0.044715 * x * x)


def lru_gate(lay, p, s, name):
    tr = lay.tr

    def body(g_ref, s_ref, o_ref):
        for h in range(8):
            sl = slice(h * 128, (h + 1) * 128)
            o_ref[:, sl] = (_gelu(g_ref[:, sl])[0] * (s_ref[0, h] + s_ref[1, h])).astype(o_ref.dtype)

    return pl.pallas_call(
        body, name=name, out_shape=_sds((lay.T, D), BF16), grid=(lay.nblk,),
        in_specs=[pl.BlockSpec((tr, D), lambda i: (i, 0)), pl.BlockSpec((2, 8, tr, 128), lambda i: (0, 0, i, 0))],
        out_specs=pl.BlockSpec((tr, D), lambda i: (i, 0)), compiler_params=_cp(("parallel",)))(p, s)


def lru_gate_bwd(lay, p, s, do, name):
    tr = lay.tr

    def body(g_ref, s_ref, do_ref, dg_ref, dy_ref):
        for h in range(8):
            sl = slice(h * 128, (h + 1) * 128)
            ge, dge = _gelu(g_ref[:, sl])
            dov = do_ref[:, sl]
            dg_ref[:, sl] = (dov * (s_ref[0, h] + s_ref[1, h]) * dge).astype(dg_ref.dtype)
            dy_ref[h] = dov * ge

    xspec = pl.BlockSpec((tr, D), lambda i: (i, 0))
    return pl.pallas_call(
        body, name=name, out_shape=[_sds((lay.T, D), BF16), _sds((8, lay.T, 128), F32)], grid=(lay.nblk,),
        in_specs=[xspec, pl.BlockSpec((2, 8, tr, 128), lambda i: (0, 0, i, 0)), xspec],
        out_specs=[xspec, pl.BlockSpec((8, tr, 128), lambda i: (0, i, 0))],
        compiler_params=_cp(("parallel",)))(p, s, do)


def silu_rows(x, name):
    def body(x_ref, o_ref):
        v = x_ref[...]
        o_ref[...] = (v * _sigmoid(v)).astype(o_ref.dtype)
    return pl.pallas_call(body, name=name, out_shape=_sds(x.shape, BF16), in_specs=[VMEM_SPEC], out_specs=VMEM_SPEC)(x)


def mod_grad_rows(gath, name):
    w = gath.shape[-1]

    def body(g_ref, dm_ref, db_ref):
        dm_ref[...] = jnp.zeros_like(dm_ref)
        for l in range(2):
            ctx = g_ref[0, 3 * l + 2:3 * l + 3, :]
            tot = g_ref[0, 3 * l:3 * l + 1, :] + g_ref[0, 3 * l + 1:3 * l + 2, :]
            for k in range(8):
                dm_ref[l, 2 * k:2 * k + 2, :] = g_ref[k, 3 * l:3 * l + 2, :]
                if k:
                    ctx = ctx + g_ref[k, 3 * l + 2:3 * l + 3, :]
                    tot = tot + (g_ref[k, 3 * l:3 * l + 1, :] + g_ref[k, 3 * l + 1:3 * l + 2, :])
            dm_ref[l, 16:17, :] = ctx
            db_ref[l:l + 1, :] = tot + ctx

    return pl.pallas_call(body, name=name, out_shape=[_sds((2, 32, w), F32), _sds((2, w), F32)],
                          in_specs=[VMEM_SPEC], out_specs=[VMEM_SPEC, VMEM_SPEC])(gath)


def cctx_grad(p, c_ctx, name):
    def body(a_ref, c_ref, o_ref):
        cv = c_ref[...]
        sg = _sigmoid(cv)
        o_ref[...] = 0.5 * (a_ref[0, 0:1, :] + a_ref[1, 0:1, :]) * (sg * (1.0 + cv * (1.0 - sg)))
    return pl.pallas_call(body, name=name, out_shape=_sds((1, D), F32), in_specs=[VMEM_SPEC] * 2,
                          out_specs=VMEM_SPEC)(p, c_ctx)


def loss_and_grad(lay, h, tgt, name):
    def fn(hb, tb):
        lat = (pl.program_id(0) % lay.bps) >= lay.cb
        e = jnp.where(lat, hb - tb, 0.0)
        return e * (1.0 / D), jnp.sum(e * e, axis=0, keepdims=True) * (0.5 / D)
    return rowwise(lay, name, fn, [h, tgt], outs=[(D, F32)], sums=[(1, D)])


def adamw(w, g, m, v, name):
    shape = w.shape
    w2, g2, m2, v2 = (t.reshape(-1, shape[-1]) for t in (w, g, m, v))
    rows, width = w2.shape
    tr = 256 if rows % 256 == 0 else rows
    c1 = 1.0 - ADAM_B1 ** ADAM_STEP
    c2 = 1.0 - ADAM_B2 ** ADAM_STEP

    def body(w_ref, g_ref, m_ref, v_ref, d_ref, mo_ref, vo_ref):
        gv = g_ref[...]
        mn = ADAM_B1 * m_ref[...] + (1.0 - ADAM_B1) * gv
        vn = ADAM_B2 * v_ref[...] + (1.0 - ADAM_B2) * (gv * gv)
        d_ref[...] = -ADAM_LR * ((mn / c1) / (jnp.sqrt(vn / c2) + ADAM_EPS) + ADAM_WD * w_ref[...])
        mo_ref[...] = mn
        vo_ref[...] = vn

    spec = pl.BlockSpec((tr, width), lambda i: (i, 0))
    d, mn, vn = pl.pallas_call(body, name=name, out_shape=[_sds((rows, width), F32)] * 3, grid=(rows // tr,),
                               in_specs=[spec] * 4, out_specs=[spec] * 3, compiler_params=_cp(("parallel",)))(w2, g2, m2, v2)
    return d.reshape(shape), mn.reshape(shape), vn.reshape(shape)


def adamw_ffn(w, m, v, red, kind, ns, name):
    shape = w.shape
    w2, m2, v2 = (t.reshape(-1, shape[-1]) for t in (w, m, v))
    rows, width = w2.shape
    c1 = 1.0 - ADAM_B1 ** ADAM_STEP
    c2 = 1.0 - ADAM_B2 ** ADAM_STEP
    tr, nb = ns // 2, 2
    gspec = pl.BlockSpec((tr, D), lambda i: (((i // nb) * 3 + kind) * nb + i % nb, 0))

    def body(w_ref, g_ref, m_ref, v_ref, go_ref, d_ref, mo_ref, vo_ref):
        gv = g_ref[...]
        mn = ADAM_B1 * m_ref[...] + (1.0 - ADAM_B1) * gv
        vn = ADAM_B2 * v_ref[...] + (1.0 - ADAM_B2) * (gv * gv)
        go_ref[...] = gv
        d_ref[...] = -ADAM_LR * ((mn / c1) / (jnp.sqrt(vn / c2) + ADAM_EPS) + ADAM_WD * w_ref[...])
        mo_ref[...] = mn
        vo_ref[...] = vn

    spec = pl.BlockSpec((tr, width), lambda i: (i, 0))
    outs = pl.pallas_call(body, name=name, out_shape=[_sds((rows, width), F32)] * 4, grid=(rows // tr,),
                          in_specs=[spec, gspec, spec, spec], out_specs=[spec] * 4,
                          compiler_params=_cp(("parallel",)))(w2, red, m2, v2)
    return tuple(t.reshape(shape) for t in outs)


def mod_mm(sc, w_mod, bias, name):
    wm = w_mod.shape[-1]
    tn = _pick(wm, (768, 512, 384, 256, 128))

    def body(a_ref, b_ref, c_ref, o_ref):
        o_ref[...] = _nn(a_ref[...], b_ref[...].astype(BF16)) + c_ref[...]

    return pl.pallas_call(
        body, name=name, out_shape=_sds((DEPTH, 32, wm), F32), grid=(DEPTH, wm // tn),
        in_specs=[pl.BlockSpec((32, D), lambda l, j: (0, 0)), pl.BlockSpec((None, D, tn), lambda l, j: (l, 0, j)),
                  pl.BlockSpec((None, 1, tn), lambda l, j: (l, 0, j))],
        out_specs=pl.BlockSpec((None, 32, tn), lambda l, j: (l, 0, j)),
        compiler_params=_cp(("parallel", "parallel")))(sc, w_mod, bias)


def wmod_dw(sc, dcol, name):
    wm = dcol.shape[-1]
    tm = 256

    def body(a_ref, b_ref, o_ref):
        o_ref[...] = _tn(a_ref[...], b_ref[...].astype(BF16))

    return pl.pallas_call(
        body, name=name, out_shape=_sds((DEPTH, D, wm), F32), grid=(DEPTH, D // tm),
        in_specs=[pl.BlockSpec((32, tm), lambda l, i: (0, i)), pl.BlockSpec((None, 32, wm), lambda l, i: (l, 0, 0))],
        out_specs=pl.BlockSpec((None, tm, wm), lambda l, i: (l, i, 0)),
        compiler_params=_cp(("parallel", "parallel")))(sc, dcol)


def cctx_dx(drow, w_mod, name):
    wm = w_mod.shape[-1]

    def body(a_ref, b_ref, o_ref):
        o_ref[...] = _nt(a_ref[...].astype(BF16), b_ref[...].astype(BF16))

    return pl.pallas_call(
        body, name=name, out_shape=_sds((DEPTH, 16, D), F32), grid=(DEPTH,),
        in_specs=[pl.BlockSpec((None, 16, wm), lambda l: (l, 0, 0)), pl.BlockSpec((None, D, wm), lambda l: (l, 0, 0))],
        out_specs=pl.BlockSpec((None, 16, D), lambda l: (l, 0, 0)), compiler_params=_cp(("parallel",), VMEM_BIG))(drow, w_mod)


HEAD_PERM = (0, 4, 1, 5, 2, 6, 3, 7)


def _rot_rows(wt):
    return jnp.concatenate([-wt[32:64], wt[0:32]], axis=0)


def _unrot_rows(g):
    return jnp.concatenate([g[32:64], -g[0:32]], axis=0)


def _heads(a, n):
    return [a[64 * i:64 * (i + 1)] for i in range(n)]


def kernel(x, c, ctx, c_ctx, w_mod, b_mod, ln_g, ln_b, ffn_w_gate, ffn_w_up, ffn_w_down, mix_ab_w_in, attn_sink, pool_w, pool_scale, mix_ab_w_out, lru_w_in, lru_conv_w, lru_conv_b, lru_wa, lru_ba, lru_wx, lru_bx, lru_lambda, lru_w_out, loss_target, m_c_ctx, m_w_mod, m_b_mod, m_ln_g, m_ln_b, m_ffn_w_gate, m_ffn_w_up, m_ffn_w_down, m_mix_ab_w_in, m_attn_sink, m_pool_w, m_pool_scale, m_mix_ab_w_out, m_lru_w_in, m_lru_conv_w, m_lru_conv_b, m_lru_wa, m_lru_ba, m_lru_wx, m_lru_bx, m_lru_lambda, m_lru_w_out, v_c_ctx, v_w_mod, v_b_mod, v_ln_g, v_ln_b, v_ffn_w_gate, v_ffn_w_up, v_ffn_w_down, v_mix_ab_w_in, v_attn_sink, v_pool_w, v_pool_scale, v_mix_ab_w_out, v_lru_w_in, v_lru_conv_w, v_lru_conv_b, v_lru_wa, v_lru_ba, v_lru_wx, v_lru_bx, v_lru_lambda, v_lru_w_out):
    n_lat, n_ctx = x.shape[1], ctx.shape[1]
    lay = Layout(n_ctx, n_lat)
    T = lay.T
    ns = ffn_w_gate.shape[-1]
    n_li, n_ai = lru_w_in.shape[-1], mix_ab_w_in.shape[-1]
    n_ao, n_lo = mix_ab_w_out.shape[1], lru_w_out.shape[1]
    wm = w_mod.shape[-1]
    dsh = ln_g.shape[-1]
    mx, my, mc = lax.axis_index("x"), lax.axis_index("y"), lax.axis_index("c")
    chip = 2 * mx + my
    me = 2 * chip + mc

    c_all = all_gather8(c, "ag8_c").reshape(16, D)
    cc = jnp.concatenate([c_all, c_ctx[None, :], jnp.zeros((15, D), F32)], axis=0)
    sc = silu_rows(cc, "silu_c")
    bias = lax.dynamic_slice(b_mod, (0, chip * wm), (DEPTH, wm)).reshape(DEPTH, 1, wm)
    modg = all_gather_chips(mod_mm(sc, w_mod, bias, "mod_mm"), "ag_mod")
    modtab = []
    for l in range(DEPTH):
        full = jnp.transpose(modg[:, l], (1, 0, 2)).reshape(32, N_CHIP * wm)
        mine = lax.dynamic_slice(full, (2 * me, 0), (2, N_CHIP * wm))
        modtab.append(jnp.concatenate([mine, full[16:17]], axis=0).reshape(3, N_MOD, D))

    small = jnp.concatenate([ln_g.reshape(6, dsh), ln_b.reshape(6, dsh), lru_conv_w[0], lru_conv_b, lru_ba[0],
                             lru_bx[0], lru_lambda[0], jnp.zeros((9, dsh), F32)], axis=0)
    small = all_gather_chips(small.reshape(2, 16, dsh), "ag_small").reshape(N_CHIP, 32, dsh)
    small = jnp.transpose(small, (1, 0, 2)).reshape(32, D)
    ln_g_f, ln_b_f = small[0:6].reshape(2, 3, D), small[6:12].reshape(2, 3, D)
    conv_w_f, conv_b_f = small[12:16], small[16:17]
    lru_vec = small[17:23]

    hh = 3 * ns // 2
    gate_t, up_t = jnp.swapaxes(ffn_w_gate, -1, -2), jnp.swapaxes(ffn_w_up, -1, -2)
    extra = [0, n_ai + n_ao, n_li + n_lo, 0]
    placed = [ffn_place(gate_t, up_t, ffn_w_down, g // 2, g % 2, f"ag_ffn{g}_place", extra[g]) for g in range(4)]
    placed[1] = place_rows(placed[1], jnp.concatenate([mix_ab_w_in[0].T, mix_ab_w_out[0]], axis=0).astype(BF16), 3 * ns,
                           "ag_mixa_place")
    placed[2] = place_rows(placed[2], jnp.concatenate([lru_w_in[0].T, lru_w_out[0]], axis=0).astype(BF16), 3 * ns,
                           "ag_mixc_place")
    placed = [p.reshape(N_CHIP, 2, p.shape[1] // 2, D) for p in placed]
    wb = [gather_placed(placed[0], "ag_ffn0"), None, None, None]
    mixw = {}

    def mixa_w():
        if "a" not in mixw:
            full = wb[1].reshape(N_CHIP, -1, D)
            ab_in_t = full[:, 3 * ns:3 * ns + n_ai].reshape(N_CHIP * n_ai, D)
            ab_out = full[:, 3 * ns + n_ai:].reshape(N_CHIP * n_ao, D)
            qh, kh = _heads(ab_in_t[Q0:K0], N_HEADS), _heads(ab_in_t[K0:V0], N_KV)
            w_ext_t = jnp.concatenate([qh[h] for h in HEAD_PERM] + [ab_in_t[K0:QR0]]
                                      + [_rot_rows(qh[h]) for h in HEAD_PERM] + [_rot_rows(t) for t in kh], axis=0)
            oh = _heads(ab_out[0:ATT_W], N_HEADS)
            mixw["a"] = (w_ext_t, jnp.concatenate([oh[h] for h in HEAD_PERM] + [ab_out[ATT_W:]], axis=0))
        return mixw["a"]

    def mixc_w():
        if "c" not in mixw:
            full = wb[2].reshape(N_CHIP, -1, D)
            mixw["c"] = (full[:, 3 * ns:3 * ns + n_li].reshape(N_CHIP * n_li, D),
                         full[:, 3 * ns + n_li:].reshape(N_CHIP * n_lo, D))
        return mixw["c"]

    t = jnp.arange(n_lat)
    inv = ROPE_THETA ** (-jnp.arange(16, dtype=F32) / 16.0)
    ang = jnp.concatenate([(t // GRID_W).astype(F32)[:, None] * inv, (t % GRID_W).astype(F32)[:, None] * inv], axis=-1)
    cos1 = jnp.concatenate([jnp.ones((n_ctx, 32), F32), jnp.cos(ang)], axis=0)
    sin1 = jnp.concatenate([jnp.zeros((n_ctx, 32), F32), jnp.sin(ang)], axis=0)
    cos_t = jnp.tile(cos1, (2, 4))
    sin_t = jnp.tile(sin1, (2, 4))
    sk = attn_sink[0]
    sink_tab = jnp.concatenate([jnp.repeat(jnp.stack([sk[:4], sk[4:]], axis=1), HEAD_DIM, axis=1),
                                jnp.zeros((4, 128), F32)], axis=0)
    pscale = pool_scale.reshape(1, POOL_W)

    h0 = jnp.concatenate([ctx, x], axis=1).reshape(T, D)
    tgt = loss_target.reshape(2 * n_lat, D)

    def lnv(l, j):
        return jnp.stack([ln_g_f[l, j], ln_b_f[l, j]])

    subs = [(0, 0, 0.5, 0), (0, 3, 1.0, 1), (0, 6, 0.5, 2), (1, 0, 0.5, 0), (1, 3, 1.0, 1), (1, 6, 0.5, 2)]

    def ffn_core(hm, l, f):
        tag = f"l{l}f{f}"
        gi = 2 * l + f
        w = wb[gi].reshape(N_CHIP, -1, D)
        if gi == 3:
            up, sl, a = ffn_up(lay, hm, w, 0, 1, ns, f"ffn_up_{tag}")
            (y,) = slab_nn_acc(lay, [a], w, [2], ns, f"ffn_down_{tag}")
            return y, dict(up=up, sl=sl, a=a, nbuf=None)
        up, sl, a, nbuf = ffn_up(lay, hm, w, 0, 1, ns, f"ffn_up_{tag}", rider=rider_gather_xy(placed[gi + 1]))
        y, nbuf = slab_nn_acc(lay, [a], w, [2], ns, f"ffn_down_{tag}", rider=rider_gather_fwd(nbuf))
        return y, dict(up=up, sl=sl, a=a, nbuf=nbuf)

    def mixa_core(hm):
        p = mm_nt(hm, mixa_w()[0], "mixa_in")
        qr, kr, vb, u = rope_fwd(lay, p, cos_t, sin_t, "rope")
        att, lse = attn_fwd(lay, qr, kr, vb, sink_tab, "attn")
        pool = pool_fwd(lay, u, pool_w[0], pscale, "pool")
        cat = jnp.concatenate([att, pool], axis=1)
        return mm_nn(cat, mixa_w()[1], "mixa_out"), dict(qr=qr, kr=kr, vb=vb, u=u, lse=lse, cat=cat)

    def mixc_core(hm):
        p = mm_nt(hm, mixc_w()[0], "mixc_in")
        uc = conv_fwd(lay, p, D, conv_w_f, conv_b_f, "conv")
        a, b = lru_coeffs(lay, uc, lru_wa[0], lru_wx[0], lru_vec, "lru_coef")
        s = lru_scan(lay, a, b, "lru_scan")
        o = lru_gate(lay, p, s, "lru_gate")
        return mm_nn(o, mixc_w()[1], "mixc_out"), dict(p=p, uc=uc, a=a, s=s, o=o)

    recs = []
    h = h0
    hm = modulate(lay, h0, modtab[0], 0, 1, "mod_first")
    for k, (l, k0, coef, j) in enumerate(subs):
        if k0 == 3:
            y, core = mixa_core(hm) if l == 0 else mixc_core(hm)
        else:
            y, core = ffn_core(hm, l, k0 // 6)
        nxt = None if k == 5 else (modtab[subs[k + 1][0]], subs[k + 1][1], subs[k + 1][1] + 1)
        nbuf = core.pop("nbuf", None)
        res = resid_ln(lay, h, y, modtab[l], k0 + 2, coef, lnv(l, j), f"ln_s{k}", nxt=nxt,
                       rider=None if nbuf is None else rider_gather_d2d(nbuf))
        if nbuf is not None:
            wb[2 * l + k0 // 6 + 1] = res[-1]
        recs.append(dict(h=h, hm=hm, y=y, xhat=res[1], rstd=res[2], **core))
        h = res[0]
        hm = res[3] if nxt is not None else None

    dout, lparts = loss_and_grad(lay, h, tgt, "loss")
    loss = lax.psum(jnp.sum(lparts), ("x", "y", "c"))

    dln = {}
    dms = {}
    mixg = {}
    ffn_red = [lax.empty((4, 2, hh, D), F32)]
    mix_red = {}
    pending = []

    def rs_sib(p):
        return None if p is None else rider_reduce_sib(p["buf"])

    def rs_add2(p, recv):
        p["q"] = add_own_half(p["buf"], recv, BF16, f"rs_add2_{p['key']}")

    def rs_join(p, arr):
        if isinstance(p["key"], int):
            return rider_join(sum_slots(p["q"], arr, f"rs_add4_{p['key']}", dst=ffn_red[0], g=p["key"]), p["key"])
        return rider_join(sum_slots(p["q"], arr, f"rs_add4_{p['key']}"))

    def rs_done(p, joined):
        if isinstance(p["key"], int):
            ffn_red[0] = joined
        else:
            mix_red[p["key"]] = joined.reshape(-1, D)

    def ffn_core_bwd(dy, r, l, f):
        tag = f"l{l}f{f}"
        gi = 2 * l + f
        w = wb[gi].reshape(N_CHIP, -1, D)
        p = pending.pop() if pending else None
        gb = lax.empty((N_CHIP, 3 * ns, D), F32)
        if p is None:
            dg, du = ffn_bwd_da(lay, dy, w, 2, r["up"], r["sl"], ns, f"ffn_da_{tag}")
            (gb,) = slab_tn(lay, r["a"], dy, gb, 2, ns, f"ffn_dwd_{tag}")
            (gb,) = slab_tn(lay, dg, r["hm"], gb, 0, ns, f"ffn_dwg_{tag}")
            (gb,) = slab_tn(lay, du, r["hm"], gb, 1, ns, f"ffn_dwu_{tag}")
            (dhm,) = slab_nn_acc(lay, [dg, du], w, [0, 1], ns, f"ffn_dh_{tag}")
        elif gi == 0:
            dg, du, recv = ffn_bwd_da(lay, dy, w, 2, r["up"], r["sl"], ns, f"ffn_da_{tag}", rider=rs_sib(p))
            rs_add2(p, recv)
            gb, arr = slab_tn(lay, r["a"], dy, gb, 2, ns, f"ffn_dwd_{tag}", rider=rider_reduce_copies(p["q"]))
            gb, joined = slab_tn(lay, dg, r["hm"], gb, 0, ns, f"ffn_dwg_{tag}", rider=rs_join(p, arr))
            rs_done(p, joined)
            (gb,) = slab_tn(lay, du, r["hm"], gb, 1, ns, f"ffn_dwu_{tag}")
            own = gb.reshape(N_CHIP, 2, hh, D)
            dhm, recv = slab_nn_acc(lay, [dg, du], w, [0, 1], ns, f"ffn_dh_{tag}", rider=rider_reduce_sib(own))
            pending.append(dict(buf=own, key=gi, recv=recv))
            return dhm
        else:
            dg, du, recv = ffn_bwd_da(lay, dy, w, 2, r["up"], r["sl"], ns, f"ffn_da_{tag}", rider=rs_sib(p))
            rs_add2(p, recv)
            gb, arr = slab_tn(lay, r["a"], dy, gb, 2, ns, f"ffn_dwd_{tag}", rider=rider_reduce_copy(p["q"], 0))
            gb, arr = slab_tn(lay, dg, r["hm"], gb, 0, ns, f"ffn_dwg_{tag}", rider=rider_reduce_copy(p["q"], 1, arr))
            gb, arr = slab_tn(lay, du, r["hm"], gb, 1, ns, f"ffn_dwu_{tag}", rider=rider_reduce_copy(p["q"], 2, arr))
            dhm, joined = slab_nn_acc(lay, [dg, du], w, [0, 1], ns, f"ffn_dh_{tag}", rider=rs_join(p, arr))
            rs_done(p, joined)
        pending.append(dict(buf=gb.reshape(N_CHIP, 2, hh, D), key=gi))
        return dhm

    def mixc_core_bwd(dy, r):
        p = pending.pop() if pending else None
        w_in_t, w_out = mixc_w()
        if p is None:
            do_c = mm_nt(dy, w_out, "mixc_out_dx")
        else:
            do_c, recv = mm_nt(dy, w_out, "mixc_out_dx", rider=rs_sib(p))
            rs_add2(p, recv)
        g_out = mm_tn(r["o"], dy, "mixc_out_dw")
        dgate, dyg = lru_gate_bwd(lay, r["p"], r["s"], do_c, "lru_gate_b")
        da_c, db_c = lru_scan_bwd(lay, r["a"], r["s"], dyg, "lru_scan_b")
        res = lru_coeffs_bwd(lay, r["uc"], lru_wa[0], lru_wx[0], lru_vec, da_c, db_c, "lru_coef_b",
                             rider=None if p is None else rider_reduce_copies(p["q"]))
        duc, mixg["wa"], mixg["wx"], mixg["vec"] = res[:4]
        du_c, mixg["cw"], mixg["cb"] = conv_bwd(lay, r["p"], D, conv_w_f, duc, "conv_b")
        dp_c = jnp.concatenate([dgate, du_c], axis=1)
        if p is None:
            g_in_t = mm_tn(dp_c, r["hm"], "mixc_in_dw")
        else:
            g_in_t, joined = mm_tn(dp_c, r["hm"], "mixc_in_dw", rider=rs_join(p, res[4]))
            rs_done(p, joined)
        buf = jnp.concatenate([g_in_t.reshape(N_CHIP, n_li, D), g_out.reshape(N_CHIP, n_lo, D)], axis=1)
        pending.append(dict(buf=buf.reshape(N_CHIP, 2, (n_li + n_lo) // 2, D), key="c"))
        return mm_nn(dp_c, w_in_t, "mixc_in_dx")

    def mixa_core_bwd(dy, r):
        p = pending.pop() if pending else None
        w_ext_t, w_out_ext = mixa_w()
        if p is None:
            dcat = mm_nt(dy, w_out_ext, "mixa_out_dx")
        else:
            dcat, recv = mm_nt(dy, w_out_ext, "mixa_out_dx", rider=rs_sib(p))
            rs_add2(p, recv)
        g_out_ext = mm_tn(r["cat"], dy, "mixa_out_dw")
        res = attn_bwd(lay, r["qr"], r["kr"], r["vb"], sink_tab, r["lse"], dcat, "attn_b",
                       rider=None if p is None else rider_reduce_copies(p["q"]))
        dqr, dkr, dv, mixg["sink"] = res[:4]
        du_a, mixg["pw"], mixg["ps"] = pool_bwd(lay, r["u"], dcat, pool_w[0], pscale, "pool_b")
        dp_a = rope_bwd(lay, dqr, dkr, dv, du_a, cos_t, sin_t, "rope_b")
        if p is None:
            g_ext_t = mm_tn(dp_a, r["hm"], "mixa_in_dw")
        else:
            g_ext_t, joined = mm_tn(dp_a, r["hm"], "mixa_in_dw", rider=rs_join(p, res[4]))
            rs_done(p, joined)
        gq, gqr = _heads(g_ext_t[Q0:K0], N_HEADS), _heads(g_ext_t[QR0:KR0], N_HEADS)
        g_q = [None] * N_HEADS
        for i, h in enumerate(HEAD_PERM):
            g_q[h] = gq[i] + _unrot_rows(gqr[i])
        gk = [a + _unrot_rows(b) for a, b in zip(_heads(g_ext_t[K0:V0], N_KV), _heads(g_ext_t[KR0:PEXT], N_KV))]
        g_ab_in_t = jnp.concatenate(g_q + gk + [g_ext_t[V0:QR0]], axis=0)
        go = _heads(g_out_ext[0:ATT_W], N_HEADS)
        g_o = [None] * N_HEADS
        for i, h in enumerate(HEAD_PERM):
            g_o[h] = go[i]
        g_ab_out = jnp.concatenate(g_o + [g_out_ext[ATT_W:]], axis=0)
        buf = jnp.concatenate([g_ab_in_t.reshape(N_CHIP, n_ai, D), g_ab_out.reshape(N_CHIP, n_ao, D)], axis=1)
        pending.append(dict(buf=buf.reshape(N_CHIP, 2, (n_ai + n_ao) // 2, D), key="a"))
        return mm_nn(dp_a, w_ext_t, "mixa_in_dx")

    l, k0, coef, j = subs[5]
    dy, dres, s1 = ln_bwd(lay, dout, recs[5]["xhat"], recs[5]["rstd"], recs[5]["y"], modtab[l], k0 + 2, coef, lnv(l, j),
                          "lnb_s5")
    for k in range(5, -1, -1):
        l, k0, coef, j = subs[k]
        r = recs[k]
        if k0 == 3:
            dhm = mixa_core_bwd(dy, r) if l == 0 else mixc_core_bwd(dy, r)
        else:
            dhm = ffn_core_bwd(dy, r, l, k0 // 6)
        dln[(l, j)] = s1
        if k > 0:
            lp, k0p, coefp, jp = subs[k - 1]
            rp = recs[k - 1]
            dy, dres, s1, s2 = modb_lnb(lay, dres, dhm, modtab[l], k0 + 1, rp["xhat"], rp["rstd"], rp["y"],
                                        modtab[lp], k0p + 2, coefp, lnv(lp, jp), f"modb_lnb_s{k}")
        else:
            gx, s2 = mod_bwd(lay, dres, dhm, r["h"], modtab[l], k0 + 1, "modb_s0")
        dms[(l, k0)] = s2
    sums = block_sums(lay, list(dln.values()) + list(dms.values()), "block_sums")
    dln, dms = dict(zip(dln, sums[:len(dln)])), dict(zip(dms, sums[len(dln):]))
    grad_x = gx.reshape(2, n_lat, D)
    g_wa, g_wx, g_vec, g_cw, g_cb, g_sink, g_pw, g_ps = (mixg[n] for n in ("wa", "wx", "vec", "cw", "cb", "sink", "pw", "ps"))

    rows = []
    for l in range(DEPTH):
        per_k = []
        for k0, j in ((0, 0), (3, 1), (6, 2)):
            per_k += [dms[(l, k0)][:3, 0], dms[(l, k0)][:3, 1], dln[(l, j)][:3, 2]]
        rows.append(jnp.stack(per_k, axis=1).reshape(3, N_MOD * D))
    dmod_loc = jnp.concatenate(rows + [jnp.zeros((2, N_MOD * D), F32)], axis=0)
    dmod_all, g_b_mod = mod_grad_rows(all_gather8(dmod_loc, "ag8_dmod"), "dmod_rows")
    dcol = lax.dynamic_slice(dmod_all, (0, 0, chip * wm), (DEPTH, 32, wm))
    g_w_mod = wmod_dw(sc, dcol, "wmod_dw")
    g_cctx = cctx_grad(cctx_dx(dcol[:, 16:32], w_mod, "cctx_dx"), c_ctx[None, :], "cctx_grad")

    g_ln_g =jnp.stack([jnp.stack([dln[(l, j)][3, 1] for j in range(3)]) for l in range(DEPTH)])
    g_ln_b = jnp.stack([jnp.stack([dln[(l, j)][3, 0] for j in range(3)]) for l in range(DEPTH)])
    sink_row = jnp.sum(g_sink, axis=0)[:4]
    g_sink8 = jnp.concatenate([sink_row[:, 0], sink_row[:, HEAD_DIM]])
    misc = jnp.concatenate([g_sink8, jnp.sum(g_ps, axis=0).reshape(POOL_W), jnp.zeros((D - 8 - POOL_W,), F32)])
    small_g = jnp.concatenate([
        g_ln_g.reshape(6, D), g_ln_b.reshape(6, D), jnp.sum(g_cw, axis=0), jnp.sum(g_cb, axis=0), g_vec,
        misc[None, :], jnp.sum(g_pw, axis=0).reshape(64, D), g_wa.reshape(256, D), g_wx.reshape(256, D), g_cctx,
        jnp.zeros((39, D), F32)], axis=0)
    n_small = small_g.shape[0] // N_CHIP
    last = pending.pop()
    ffn_red = reduce_scatter_chips(last["buf"], f"ffn{last['key']}", wire=BF16, dst=ffn_red[0], g=last["key"],
                                   recv=last.get("recv")).reshape(12 * ns, D)
    small_red = reduce_scatter_chips(small_g.reshape(N_CHIP, 2, n_small // 2, D), "small")
    small_red = all_gather_chips(small_red, "ag_smallg").reshape(N_CHIP * n_small, D)

    ffn_kind = dict(ffn_w_gate=0, ffn_w_up=1, ffn_w_down=2)

    def cols(a):
        return lax.dynamic_slice_in_dim(a, chip * dsh, dsh, axis=a.ndim - 1)

    sr = small_red
    grads = dict(
        c_ctx=sr[600], w_mod=g_w_mod, b_mod=g_b_mod,
        ln_g=cols(sr[0:6]).reshape(2, 3, dsh), ln_b=cols(sr[6:12]).reshape(2, 3, dsh),
        mix_ab_w_in=mix_red["a"][0:n_ai][None], attn_sink=sr[23, 0:8][None], pool_w=sr[24:88].reshape(1, 4, 128, 128),
        pool_scale=sr[23, 8:8 + POOL_W][None], mix_ab_w_out=mix_red["a"][n_ai:][None],
        lru_w_in=mix_red["c"][0:n_li].T[None],
        lru_conv_w=cols(sr[12:16])[None], lru_conv_b=cols(sr[16:17]), lru_wa=sr[88:344].reshape(1, 2, 8, 128, 128),
        lru_ba=cols(sr[17:19])[None], lru_wx=sr[344:600].reshape(1, 2, 8, 128, 128), lru_bx=cols(sr[19:21])[None],
        lru_lambda=cols(sr[21:23])[None], lru_w_out=mix_red["c"][n_li:][None])
    params = dict(c_ctx=(c_ctx, m_c_ctx, v_c_ctx), w_mod=(w_mod, m_w_mod, v_w_mod), b_mod=(b_mod, m_b_mod, v_b_mod),
                  ln_g=(ln_g, m_ln_g, v_ln_g), ln_b=(ln_b, m_ln_b, v_ln_b),
                  ffn_w_gate=(ffn_w_gate, m_ffn_w_gate, v_ffn_w_gate), ffn_w_up=(ffn_w_up, m_ffn_w_up, v_ffn_w_up),
                  ffn_w_down=(ffn_w_down, m_ffn_w_down, v_ffn_w_down),
                  mix_ab_w_in=(mix_ab_w_in, m_mix_ab_w_in, v_mix_ab_w_in), attn_sink=(attn_sink, m_attn_sink, v_attn_sink),
                  pool_w=(pool_w, m_pool_w, v_pool_w), pool_scale=(pool_scale, m_pool_scale, v_pool_scale),
                  mix_ab_w_out=(mix_ab_w_out, m_mix_ab_w_out, v_mix_ab_w_out), lru_w_in=(lru_w_in, m_lru_w_in, v_lru_w_in),
                  lru_conv_w=(lru_conv_w, m_lru_conv_w, v_lru_conv_w), lru_conv_b=(lru_conv_b, m_lru_conv_b, v_lru_conv_b),
                  lru_wa=(lru_wa, m_lru_wa, v_lru_wa), lru_ba=(lru_ba, m_lru_ba, v_lru_ba), lru_wx=(lru_wx, m_lru_wx, v_lru_wx),
                  lru_bx=(lru_bx, m_lru_bx, v_lru_bx), lru_lambda=(lru_lambda, m_lru_lambda, v_lru_lambda),
                  lru_w_out=(lru_w_out, m_lru_w_out, v_lru_w_out))
    gl, dl, ml, vl = [], [], [], []
    transposed = ("ffn_w_gate", "ffn_w_up", "mix_ab_w_in")
    for name, (w, m, v) in params.items():
        if name in transposed:
            w, m, v = (jnp.swapaxes(t, -1, -2) for t in (w, m, v))
        if name in ffn_kind:
            g, d, mn, vn = adamw_ffn(w, m, v, ffn_red, ffn_kind[name], ns, f"adamw_{name}")
        else:
            g = grads[name].reshape(w.shape)
            d, mn, vn = adamw(w, g, m, v, f"adamw_{name}")
        if name in transposed:
            g, d, mn, vn = (jnp.swapaxes(t, -1, -2) for t in (g, d, mn, vn))
        gl.append(g)
        dl.append(d)
        ml.append(mn)
        vl.append(vn)
    return (loss, grad_x, *gl, *dl, *ml, *vl)
```

```python
import functools
import math

import jax
import jax.numpy as jnp
from jax import lax
from jax.experimental import pallas as pl
from jax.experimental.pallas import tpu as pltpu

F32, BF16 = jnp.float32, jnp.bfloat16
MESH = pl.DeviceIdType.MESH
ANY = pl.BlockSpec(memory_space=pl.ANY)
VMEM_SPEC = pl.BlockSpec(memory_space=pltpu.VMEM)

D = 1024
N_CHIP = 4
HEAD_DIM, N_HEADS, N_KV = 64, 8, 2
ATT_W, KV_W, POOL_W = 512, 128, 512
POOL_WINDOWS = (2, 4, 8, 16)
BLK = 128
ATT_SCALE = HEAD_DIM ** -0.5
ROPE_THETA = 10000.0
GRID_W = 64
LRU_C = 8.0
LN_EPS = 1e-5
NEG_INF = -1e30
DEPTH = 2
ALPHA = (2 * DEPTH) ** 0.25
N_MOD = 9
ADAM_LR, ADAM_B1, ADAM_B2, ADAM_EPS, ADAM_WD, ADAM_STEP = 0.001, 0.9, 0.999, 1e-08, 0.01, 10
VMEM_BIG = 48 * 1024 * 1024


def _cp(sem=None, vmem=None):
    kw = {}
    if sem is not None:
        kw["dimension_semantics"] = sem
    if vmem is not None:
        kw["vmem_limit_bytes"] = vmem
    return pltpu.CompilerParams(**kw)


def _sds(shape, dtype):
    return jax.ShapeDtypeStruct(tuple(shape), dtype)


def _pick(n, cands):
    for c in cands:
        if n % c == 0:
            return c
    return n


def _dot(a, b, dims):
    return lax.dot_general(a, b, (dims, ((), ())), preferred_element_type=F32)


def _nn(a, b):
    return _dot(a, b, ((1,), (0,)))


def _nt(a, b):
    return _dot(a, b, ((1,), (1,)))


def _tn(a, b):
    return _dot(a, b, ((0,), (0,)))


def _sigmoid(x):
    return 0.5 * jnp.tanh(0.5 * x) + 0.5


def _me():
    return lax.axis_index("x"), lax.axis_index("y"), lax.axis_index("c")


def _rcopy(src, dst, ssem, rsem, dev):
    return pltpu.make_async_remote_copy(src_ref=src, dst_ref=dst, send_sem=ssem, recv_sem=rsem,
                                        device_id=dev, device_id_type=MESH)


def all_gather8(x, name):
    def body(x_ref, o_ref, ssem, rsem, lsem):
        mx, my, mc = _me()
        me = 4 * mx + 2 * my + mc
        loc = pltpu.make_async_copy(x_ref, o_ref.at[me], lsem)
        loc.start()
        peers = []
        for m in range(1, 8):
            px = 1 - mx if (m >> 2) & 1 else mx
            py = 1 - my if (m >> 1) & 1 else my
            pc = 1 - mc if m & 1 else mc
            peers.append((px, py, pc))
        sends = [_rcopy(x_ref, o_ref.at[me], ssem.at[k], rsem.at[k], p) for k, p in enumerate(peers)]
        for cp in sends:
            cp.start()
        for k, (px, py, pc) in enumerate(peers):
            _rcopy(x_ref, o_ref.at[4 * px + 2 * py + pc], ssem.at[k], rsem.at[k], (px, py, pc)).wait_recv()
        for cp in sends:
            cp.wait_send()
        loc.wait()

    return pl.pallas_call(
        body, name=name, out_shape=_sds((8,) + x.shape, x.dtype),
        in_specs=[VMEM_SPEC], out_specs=VMEM_SPEC,
        scratch_shapes=[pltpu.SemaphoreType.DMA((7,)), pltpu.SemaphoreType.DMA((7,)), pltpu.SemaphoreType.DMA],
    )(x)


_ROW_BLOCKS = (512, 384, 352, 256, 224, 128)


def _idx(v):
    return jnp.reshape(v, (1,)).astype(jnp.int32)


def place_slab(shard, name):
    _, h, w = shard.shape
    th = _pick(h, _ROW_BLOCKS)

    def body(s_ref, x_ref, o_ref):
        del s_ref
        o_ref[...] = x_ref[...]

    return pl.pallas_call(
        body, name=name, out_shape=_sds((N_CHIP,) + shard.shape, shard.dtype),
        grid_spec=pltpu.PrefetchScalarGridSpec(
            num_scalar_prefetch=1, grid=(2, h // th),
            in_specs=[pl.BlockSpec((None, th, w), lambda k, r, s: (k, r, 0))],
            out_specs=pl.BlockSpec((None, None, th, w), lambda k, r, s: (s[0], k, r, 0))),
    )(_idx(2 * lax.axis_index("x") + lax.axis_index("y")), shard)


def place_rows(buf, rows, r0, name):
    e, w = rows.shape
    tb = 64

    def body(s_ref, x_ref, b_ref, o_ref):
        del s_ref, b_ref
        o_ref[...] = x_ref[...]

    return pl.pallas_call(
        body, name=name, out_shape=_sds(buf.shape, buf.dtype),
        grid_spec=pltpu.PrefetchScalarGridSpec(
            num_scalar_prefetch=1, grid=(e // tb,),
            in_specs=[pl.BlockSpec((tb, w), lambda j, s: (j, 0)), ANY],
            out_specs=pl.BlockSpec((None, tb, w), lambda j, s: (s[0], r0 // tb + j, 0))),
        input_output_aliases={2: 0},
    )(_idx(2 * lax.axis_index("x") + lax.axis_index("y")), rows, buf)


def ffn_place(w_gate_t, w_up_t, w_down, l, f, name, extra=0):
    ns = w_down.shape[-2]
    tr, nb = ns // 2, 2

    def body(s_ref, g_ref, u_ref, d_ref, o_ref):
        del s_ref
        k = pl.program_id(0)

        @pl.when(k == 0)
        def _():
            o_ref[...] = g_ref[...].astype(BF16)

        @pl.when(k == 1)
        def _():
            o_ref[...] = u_ref[...].astype(BF16)

        @pl.when(k == 2)
        def _():
            o_ref[...] = d_ref[...].astype(BF16)

    def spec(q):
        return pl.BlockSpec((None, None, tr, D), lambda k, j, s: (l, f, jnp.where(k == q, j, 0), 0))

    return pl.pallas_call(
        body, name=name, out_shape=_sds((N_CHIP, 3 * ns + extra, D), BF16),
        grid_spec=pltpu.PrefetchScalarGridSpec(
            num_scalar_prefetch=1, grid=(3, nb), in_specs=[spec(0), spec(1), spec(2)],
            out_specs=pl.BlockSpec((None, tr, D), lambda k, j, s: (s[0], k * nb + j, 0))),
    )(_idx(2 * lax.axis_index("x") + lax.axis_index("y")), w_gate_t, w_up_t, w_down)


def all_gather_chips(shard, name):
    return gather_placed(place_slab(shard, name + "_place"), name)


def gather_placed(full, name):
    h = full.shape[2]
    lo, hi = pl.ds(0, h // 2), pl.ds(h // 2, h - h // 2)

    def body(x_ref, o_ref, ssem, rsem):
        del x_ref
        mx, my, mc = _me()
        s, xs, ys, ds = 2 * mx + my, 2 * (1 - mx) + my, 2 * mx + (1 - my), 2 * (1 - mx) + (1 - my)
        xn, yn, sib = (1 - mx, my, mc), (mx, 1 - my, mc), (mx, my, 1 - mc)

        def cp(k, src, dst, dev):
            return _rcopy(src, dst, ssem.at[k], rsem.at[k], dev)

        own = o_ref.at[s, mc]
        sent = [cp(0, own, own, xn), cp(1, own, own, yn)]
        for c in sent:
            c.start()
        cp(0, own, o_ref.at[xs, mc], xn).wait_recv()
        sent += [cp(2, o_ref.at[xs, mc, lo], o_ref.at[xs, mc, lo], yn), cp(4, o_ref.at[xs, mc], o_ref.at[xs, mc], sib)]
        sent[-2].start()
        sent[-1].start()
        cp(1, own, o_ref.at[ys, mc], yn).wait_recv()
        sent += [cp(3, o_ref.at[ys, mc, hi], o_ref.at[ys, mc, hi], xn), cp(5, o_ref.at[ys, mc], o_ref.at[ys, mc], sib)]
        sent[-2].start()
        sent[-1].start()
        cp(2, own, o_ref.at[ds, mc, lo], yn).wait_recv()
        cp(3, own, o_ref.at[ds, mc, hi], xn).wait_recv()
        sent.append(cp(6, o_ref.at[ds, mc], o_ref.at[ds, mc], sib))
        sent[-1].start()
        for k, slot in ((4, xs), (5, ys), (6, ds)):
            cp(k, own, o_ref.at[slot, 1 - mc], sib).wait_recv()
        for c in sent:
            c.wait_send()

    return pl.pallas_call(
        body, name=name, out_shape=_sds(full.shape, full.dtype), in_specs=[ANY], out_specs=ANY,
        input_output_aliases={0: 0},
        scratch_shapes=[pltpu.SemaphoreType.DMA((7,)), pltpu.SemaphoreType.DMA((7,))],
    )(full)


def sibling_send_other_half(buf, name):
    def body(x_ref, o_ref, ssem, rsem):
        mx, my, mc = _me()
        sib = (mx, my, 1 - mc)
        cps = [_rcopy(x_ref.at[k, 1 - mc], o_ref.at[k], ssem.at[k], rsem.at[k], sib) for k in range(N_CHIP)]
        for cp in cps:
            cp.start()
        for cp in cps:
            cp.wait_recv()
        for cp in cps:
            cp.wait_send()

    n, _, h, w = buf.shape
    return pl.pallas_call(
        body, name=name, out_shape=_sds((n, h, w), buf.dtype), in_specs=[ANY], out_specs=ANY,
        scratch_shapes=[pltpu.SemaphoreType.DMA((N_CHIP,)), pltpu.SemaphoreType.DMA((N_CHIP,))],
    )(buf)


def chips_all_to_all(q, name):
    def body(x_ref, o_ref, ssem, rsem):
        mx, my, mc = _me()
        s = 2 * mx + my
        chips = [(1 - mx, my), (mx, 1 - my), (1 - mx, 1 - my)]
        cps = [_rcopy(x_ref.at[2 * px + py], o_ref.at[s], ssem.at[j], rsem.at[j], (px, py, mc))
               for j, (px, py) in enumerate(chips)]
        for cp in cps:
            cp.start()
        for j, (px, py) in enumerate(chips):
            ps = 2 * px + py
            _rcopy(x_ref.at[ps], o_ref.at[ps], ssem.at[j], rsem.at[j], (px, py, mc)).wait_recv()
        for cp in cps:
            cp.wait_send()

    return pl.pallas_call(
        body, name=name, out_shape=_sds(q.shape, q.dtype), in_specs=[ANY], out_specs=ANY,
        scratch_shapes=[pltpu.SemaphoreType.DMA((3,)), pltpu.SemaphoreType.DMA((3,))],
    )(q)


def sibling_join_halves(both, name, g=None):
    def body(x_ref, o_ref, ssem, rsem):
        del x_ref
        mx, my, mc = _me()
        sib = (mx, my, 1 - mc)
        o = o_ref if g is None else o_ref.at[g]
        cp = _rcopy(o.at[mc], o.at[mc], ssem, rsem, sib)
        cp.start()
        _rcopy(o.at[1 - mc], o.at[1 - mc], ssem, rsem, sib).wait_recv()
        cp.wait_send()

    return pl.pallas_call(
        body, name=name, out_shape=_sds(both.shape, both.dtype), in_specs=[ANY], out_specs=ANY,
        input_output_aliases={0: 0}, scratch_shapes=[pltpu.SemaphoreType.DMA, pltpu.SemaphoreType.DMA],
    )(both)


def add_own_half(buf, recv, wire, name):
    n, _, h, w = buf.shape
    th = _pick(h, _ROW_BLOCKS)

    def body(c_ref, a_ref, b_ref, o_ref):
        del c_ref
        o_ref[...] = (a_ref[...] + b_ref[...]).astype(o_ref.dtype)

    return pl.pallas_call(
        body, name=name, out_shape=_sds((n, h, w), wire),
        grid_spec=pltpu.PrefetchScalarGridSpec(
            num_scalar_prefetch=1, grid=(n, h // th),
            in_specs=[pl.BlockSpec((None, None, th, w), lambda k, r, c: (k, c[0], r, 0)),
                      pl.BlockSpec((None, th, w), lambda k, r, c: (k, r, 0))],
            out_specs=pl.BlockSpec((None, th, w), lambda k, r, c: (k, r, 0))),
    )(_idx(lax.axis_index("c")), buf, recv)


def sum_slots(q, r, name, dst=None, g=None):
    n, h, w = r.shape
    th = _pick(h, _ROW_BLOCKS)

    def body(i_ref, q_ref, r1, r2, r3, *rest):
        del i_ref
        rest[-1][...] = ((q_ref[...].astype(F32) + r1[...].astype(F32)) + r2[...].astype(F32)) + r3[...].astype(F32)

    def slot(d):
        return lambda i, ix: ((ix[0] + d) % N_CHIP, i, 0)

    idx = jnp.stack([2 * lax.axis_index("x") + lax.axis_index("y"), lax.axis_index("c")]).astype(jnp.int32)
    in_specs = [pl.BlockSpec((None, th, w), slot(d)) for d in (0, 1, 2, 3)]
    if dst is None:
        return pl.pallas_call(
            body, name=name, out_shape=_sds((2, h, w), F32),
            grid_spec=pltpu.PrefetchScalarGridSpec(
                num_scalar_prefetch=1, grid=(h // th,), in_specs=in_specs,
                out_specs=pl.BlockSpec((None, th, w), lambda i, ix: (ix[1], i, 0))),
        )(idx, q, r, r, r)
    return pl.pallas_call(
        body, name=name, out_shape=_sds(dst.shape, F32),
        grid_spec=pltpu.PrefetchScalarGridSpec(
            num_scalar_prefetch=1, grid=(h // th,), in_specs=in_specs + [ANY],
            out_specs=pl.BlockSpec((None, None, th, w), lambda i, ix: (g, ix[1], i, 0))),
        input_output_aliases={5: 0},
    )(idx, q, r, r, r, dst)


def reduce_scatter_chips(buf, tag, wire=F32, dst=None, g=None, recv=None):
    if recv is None:
        recv = sibling_send_other_half(buf, f"rs_sib_{tag}")
    q = add_own_half(buf, recv, wire, f"rs_add2_{tag}")
    r = chips_all_to_all(q, f"rs_a2a_{tag}")
    red = sum_slots(q, r, f"rs_add4_{tag}", dst=dst, g=g)
    return sibling_join_halves(red, f"rs_join_{tag}", g=g)


class Layout:
    def __init__(self, n_ctx, n_lat):
        self.C, self.L = n_ctx, n_lat
        self.PS = n_ctx + n_lat
        self.T = 2 * self.PS
        self.tr = _pick(math.gcd(n_ctx, n_lat), (256, 128))
        self.bps = self.PS // self.tr
        self.cb = n_ctx // self.tr
        self.nblk = self.T // self.tr
        self.tm = _pick(self.T, (1152, 768, 512, 256, 128))
        self.tm2 = _pick(self.T, (2304, 1152, 768, 512, 256, 128))
        self.tc = _pick(self.T, (512, 256, 128))

    def seg(self, i):
        return jnp.where(i % self.bps < self.cb, 2, i // self.bps)


def rowwise(lay, name, fn, rows, segs=(), vecs=(), outs=(), sums=(), rider=None):
    tr, nblk = lay.tr, lay.nblk
    n_r, n_s, n_v, n_o = len(rows), len(segs), len(vecs), len(outs)
    lat_only = any(o[2:] for o in outs) or any(a.shape[0] != lay.T for a in rows)
    nsub = 1 if lat_only else (3 if nblk % 3 == 0 else 2 if nblk % 2 == 0 else 1)
    tb = tr * nsub

    def body(*refs):
        ins = refs[:n_r + n_s + n_v]
        ors = refs[n_r + n_s + n_v:]
        for sub in range(nsub):
            rs = slice(sub * tr, (sub + 1) * tr)
            seg = lay.seg(pl.program_id(0) * nsub + sub)
            vals = [r[rs, :] for r in ins[:n_r]] + [r[seg] for r in ins[n_r:n_r + n_s]] + [r[...] for r in ins[n_r + n_s:]]
            res = fn(*vals)
            for k in range(n_o):
                ors[k][rs, :] = res[k].astype(ors[k].dtype)
            for k in range(len(sums)):
                ors[n_o + k][sub] = res[n_o + k]

    def all_rows(i):
        return (i, 0)

    def lat_rows(i):
        return ((i // lay.bps) * (lay.bps - lay.cb) + jnp.maximum(i % lay.bps - lay.cb, 0), 0)

    in_specs = [pl.BlockSpec((tb, a.shape[1]), all_rows if a.shape[0] == lay.T else lat_rows) for a in rows]
    in_specs += [pl.BlockSpec(a.shape, lambda i: (0, 0, 0)) for a in segs]
    in_specs += [pl.BlockSpec(a.shape, lambda i: (0, 0)) for a in vecs]
    out_shape = [_sds((2 * lay.L if o[2:] else lay.T, o[0]), o[1]) for o in outs]
    out_shape += [_sds((nblk, r, w), F32) for r, w in sums]
    out_specs = [pl.BlockSpec((tb, o[0]), lat_rows if o[2:] else all_rows) for o in outs]
    out_specs += [pl.BlockSpec((nsub, r, w), lambda i: (i, 0, 0)) for r, w in sums]
    sem = "arbitrary" if any(o[2:] for o in outs) else "parallel"
    if rider is None:
        return pl.pallas_call(body, name=name, out_shape=out_shape, grid=(nblk // nsub,), in_specs=in_specs,
                              out_specs=out_specs, compiler_params=_cp((sem,), VMEM_BIG))(*rows, *segs, *vecs)
    return _host_call(body, rider, name, (nblk // nsub,), in_specs, out_specs, out_shape, (*rows, *segs, *vecs), (sem,),
                      n_r + n_s + n_v, n_o + len(sums))


def modulate(lay, h, mod, k_shift, k_scale, name):
    def fn(hb, m):
        return (hb * (1.0 + m[k_scale:k_scale + 1]) + m[k_shift:k_shift + 1],)
    return rowwise(lay, name, fn, [h], segs=[mod], outs=[(D, BF16)])[0]


def resid_ln(lay, h, y, mod, k_gate, coef, lnv, name, nxt=None, prev_ln=None, rider=None):
    def fn(hb, yb, m, *rest):
        ln = rest[-1]
        if prev_ln is not None:
            hb = hb * rest[-2][0:1] + rest[-2][1:2]
        z = ALPHA * hb + (coef * m[k_gate:k_gate + 1]) * yb
        mu = jnp.mean(z, axis=-1, keepdims=True)
        zc = z - mu
        var = jnp.mean(zc * zc, axis=-1, keepdims=True)
        rstd = lax.rsqrt(var + LN_EPS)
        xhat = zc * rstd
        out = xhat * ln[0:1] + ln[1:2]
        if nxt is None:
            return xhat, rstd, out
        mn = rest[0]
        return xhat, rstd, out * (1.0 + mn[nxt[2]:nxt[2] + 1]) + mn[nxt[1]:nxt[1] + 1]
    segs = [mod] if nxt is None else [mod, nxt[0]]
    vecs = [lnv] if prev_ln is None else [prev_ln, lnv]
    outs = [(D, F32), (1, F32), (D, F32) if nxt is None else (D, BF16)]
    return rowwise(lay, name, fn, [h, y], segs=segs, vecs=vecs, outs=outs, rider=rider)


def _ln_bwd_math(do, xh, rs, yb, gate, coef, ln):
    dxh = do * ln[0:1]
    m1 = jnp.mean(dxh, axis=-1, keepdims=True)
    m2 = jnp.mean(dxh * xh, axis=-1, keepdims=True)
    dz = rs * (dxh - m1 - xh * m2)
    s = jnp.concatenate([jnp.sum(do, axis=0, keepdims=True), jnp.sum(do * xh, axis=0, keepdims=True),
                         jnp.sum(coef * dz * yb, axis=0, keepdims=True)], axis=0)
    return (coef * gate) * dz, ALPHA * dz, s


def _mod_bwd_math(dr, dm, hb, scale):
    s = jnp.concatenate([jnp.sum(dm, axis=0, keepdims=True), jnp.sum(dm * hb, axis=0, keepdims=True)], axis=0)
    return dr + dm * (1.0 + scale), s


def ln_bwd(lay, dout, xhat, rstd, y, mod, k_gate, coef, lnv, name):
    def fn(do, xh, rs, yb, m, ln):
        return _ln_bwd_math(do, xh, rs, yb, m[k_gate:k_gate + 1], coef, ln)
    return rowwise(lay, name, fn, [dout, xhat, rstd, y], segs=[mod], vecs=[lnv],
                   outs=[(D, BF16), (D, F32)], sums=[(3, D)])


def mod_bwd(lay, dres, dhm, h, mod, k_scale, name, rider=None):
    def fn(dr, dm, hb, m):
        return _mod_bwd_math(dr, dm, hb, m[k_scale:k_scale + 1])
    return rowwise(lay, name, fn, [dres, dhm, h], segs=[mod], outs=[(D, F32, "lat")], sums=[(2, D)], rider=rider)


def modb_lnb(lay, dres, dhm, mod, k_scale, xhat, rstd, y, mod_p, k_gate, coef, lnv, name, rider=None):
    def fn(dr, dm, xh, rs, yb, m, mp, ln):
        dh, s2 = _mod_bwd_math(dr, dm, xh * ln[0:1] + ln[1:2], m[k_scale:k_scale + 1])
        dy, dres_p, s1 = _ln_bwd_math(dh, xh, rs, yb, mp[k_gate:k_gate + 1], coef, ln)
        return dy, dres_p, s1, s2
    return rowwise(lay, name, fn, [dres, dhm, xhat, rstd, y], segs=[mod, mod_p], vecs=[lnv],
                   outs=[(D, BF16), (D, F32)], sums=[(3, D), (2, D)], rider=rider)


def block_sums(lay, parts_list, name):
    n = len(parts_list)

    def body(*refs):
        for p_ref, o_ref in zip(refs[:n], refs[n:]):
            acc = [None, None, None]
            for i in range(lay.nblk):
                sg = 2 if i % lay.bps < lay.cb else i // lay.bps
                acc[sg] = p_ref[i] if acc[sg] is None else acc[sg] + p_ref[i]
            for k in range(3):
                o_ref[k] = acc[k]
            o_ref[3] = (acc[0] + acc[1]) + acc[2]

    return pl.pallas_call(body, name=name, out_shape=[_sds((4,) + p.shape[1:], F32) for p in parts_list],
                          in_specs=[VMEM_SPEC] * n, out_specs=[VMEM_SPEC] * n)(*parts_list)


def mm_nn(a, b, name, out_dtype=F32):
    m, k = a.shape
    n = b.shape[1]
    tm = _pick(m, (1152, 768, 512, 256, 128, 64, 32, 16, 8))
    tn = _pick(n, (1024, 768, 640, 512, 384, 256, 128))

    def body(a_ref, b_ref, o_ref):
        o_ref[...] = _nn(a_ref[...].astype(BF16), b_ref[...].astype(BF16)).astype(o_ref.dtype)

    return pl.pallas_call(body, name=name, out_shape=_sds((m, n), out_dtype), grid=(m // tm, n // tn),
                          in_specs=[pl.BlockSpec((tm, k), lambda i, j: (i, 0)), pl.BlockSpec((k, tn), lambda i, j: (0, j))],
                          out_specs=pl.BlockSpec((tm, tn), lambda i, j: (i, j)),
                          compiler_params=_cp(("parallel", "parallel"), VMEM_BIG))(a, b)


def mm_nt(a, b, name, out_dtype=F32, rider=None):
    m, k = a.shape
    n = b.shape[0]
    tm = _pick(m, (1152, 768, 512, 256, 128, 64, 32, 16, 8))
    tn = _pick(n, (1024, 768, 640, 512, 384, 256, 128))

    def body(a_ref, b_ref, o_ref):
        o_ref[...] = _nt(a_ref[...].astype(BF16), b_ref[...].astype(BF16)).astype(o_ref.dtype)

    res = _host_call(body, rider, name, (m // tm, n // tn),
                     [pl.BlockSpec((tm, k), lambda i, j: (i, 0)), pl.BlockSpec((tn, k), lambda i, j: (j, 0))],
                     [pl.BlockSpec((tm, tn), lambda i, j: (i, j))], [_sds((m, n), out_dtype)], (a, b),
                     ("parallel", "parallel"), 2, 1)
    return res[0] if rider is None else res


def mm_tn(a, b, name, rider=None):
    t, m = a.shape
    n = b.shape[1]
    tk = _pick(t, (1152, 768, 512, 256, 128, 64, 32, 16))
    tm = _pick(m, (512, 384, 256, 128))

    def body(a_ref, b_ref, o_ref):
        @pl.when(pl.program_id(1) == 0)
        def _():
            o_ref[...] = jnp.zeros_like(o_ref)
        o_ref[...] += _tn(a_ref[...].astype(BF16), b_ref[...].astype(BF16))

    res = _host_call(body, rider, name, (m // tm, t // tk),
                     [pl.BlockSpec((tk, tm), lambda i, k: (k, i)), pl.BlockSpec((tk, n), lambda i, k: (k, 0))],
                     [pl.BlockSpec((tm, n), lambda i, k: (i, 0))], [_sds((m, n), F32)], (a, b),
                     ("parallel", "arbitrary"), 2, 1)
    return res[0] if rider is None else res


class Rider:
    def __init__(self, ins, outs, aliases, nsem, start, wait):
        self.ins, self.outs, self.aliases, self.nsem, self.start, self.wait = ins, outs, aliases, nsem, start, wait


def _chips_of(mx, my):
    return [(1 - mx, my), (mx, 1 - my), (1 - mx, 1 - my)]


def rider_gather_d2d(buf):
    def start(ins, outs, ssem, rsem):
        o = outs[0]
        mx, my, mc = _me()
        for j, (px, py) in enumerate(_chips_of(mx, my)):
            ps = 2 * px + py
            _rcopy(o.at[ps, mc], o.at[ps, mc], ssem.at[j], rsem.at[j], (mx, my, 1 - mc)).start()

    def wait(ins, outs, ssem, rsem):
        o = outs[0]
        mx, my, mc = _me()
        sib = (mx, my, 1 - mc)
        for j, (px, py) in enumerate(_chips_of(mx, my)):
            ps = 2 * px + py
            _rcopy(o.at[ps, 1 - mc], o.at[ps, 1 - mc], ssem.at[j], rsem.at[j], sib).wait_recv()
        for j, (px, py) in enumerate(_chips_of(mx, my)):
            ps = 2 * px + py
            _rcopy(o.at[ps, mc], o.at[ps, mc], ssem.at[j], rsem.at[j], sib).wait_send()

    return Rider([buf], [_sds(buf.shape, buf.dtype)], {0: 0}, 3, start, wait)


def rider_reduce_sib(buf):
    n, _, h, w = buf.shape

    def start(ins, outs, ssem, rsem):
        mx, my, mc = _me()
        for k in range(N_CHIP):
            _rcopy(ins[0].at[k, 1 - mc], outs[0].at[k], ssem.at[k], rsem.at[k], (mx, my, 1 - mc)).start()

    def wait(ins, outs, ssem, rsem):
        mx, my, mc = _me()
        for k in range(N_CHIP):
            _rcopy(ins[0].at[k, 1 - mc], outs[0].at[k], ssem.at[k], rsem.at[k], (mx, my, 1 - mc)).wait_recv()
        for k in range(N_CHIP):
            _rcopy(ins[0].at[k, 1 - mc], outs[0].at[k], ssem.at[k], rsem.at[k], (mx, my, 1 - mc)).wait_send()

    return Rider([buf], [_sds((n, h, w), buf.dtype)], {}, N_CHIP, start, wait)


def rider_gather_xy(buf):
    def peers():
        mx, my, mc = _me()
        return 2 * mx + my, mc, [(1 - mx, my), (mx, 1 - my)]

    def start(ins, outs, ssem, rsem):
        o = outs[0]
        s, mc, nb = peers()
        for j, (px, py) in enumerate(nb):
            _rcopy(o.at[s, mc], o.at[s, mc], ssem.at[j], rsem.at[j], (px, py, mc)).start()

    def wait(ins, outs, ssem, rsem):
        o = outs[0]
        s, mc, nb = peers()
        for j, (px, py) in enumerate(nb):
            _rcopy(o.at[2 * px + py, mc], o.at[2 * px + py, mc], ssem.at[j], rsem.at[j], (px, py, mc)).wait_recv()
        for j, (px, py) in enumerate(nb):
            _rcopy(o.at[s, mc], o.at[s, mc], ssem.at[j], rsem.at[j], (px, py, mc)).wait_send()

    return Rider([buf], [_sds(buf.shape, buf.dtype)], {0: 0}, 2, start, wait)


def rider_gather_fwd(buf):
    h2 = buf.shape[2] // 2
    lo, hi = pl.ds(0, h2), pl.ds(h2, buf.shape[2] - h2)

    def start(ins, outs, ssem, rsem):
        o = outs[0]
        mx, my, mc = _me()
        xs, ys = 2 * (1 - mx) + my, 2 * mx + (1 - my)
        _rcopy(o.at[xs, mc, lo], o.at[xs, mc, lo], ssem.at[0], rsem.at[0], (mx, 1 - my, mc)).start()
        _rcopy(o.at[ys, mc, hi], o.at[ys, mc, hi], ssem.at[1], rsem.at[1], (1 - mx, my, mc)).start()

    def wait(ins, outs, ssem, rsem):
        o = outs[0]
        mx, my, mc = _me()
        xs, ys, ds = 2 * (1 - mx) + my, 2 * mx + (1 - my), 2 * (1 - mx) + (1 - my)
        _rcopy(o.at[ds, mc, lo], o.at[ds, mc, lo], ssem.at[0], rsem.at[0], (mx, 1 - my, mc)).wait_recv()
        _rcopy(o.at[ds, mc, hi], o.at[ds, mc, hi], ssem.at[1], rsem.at[1], (1 - mx, my, mc)).wait_recv()
        _rcopy(o.at[xs, mc, lo], o.at[xs, mc, lo], ssem.at[0], rsem.at[0], (mx, 1 - my, mc)).wait_send()
        _rcopy(o.at[ys, mc, hi], o.at[ys, mc, hi], ssem.at[1], rsem.at[1], (1 - mx, my, mc)).wait_send()

    return Rider([buf], [_sds(buf.shape, buf.dtype)], {0: 0}, 2, start, wait)


def rider_reduce_copy(q, j, r=None):
    def peer():
        mx, my, mc = _me()
        px, py = _chips_of(mx, my)[j]
        return 2 * mx + my, 2 * px + py, (px, py, mc)

    def start(ins, outs, ssem, rsem):
        s, ps, dev = peer()
        _rcopy(ins[0].at[ps], outs[0].at[s], ssem.at[0], rsem.at[0], dev).start()

    def wait(ins, outs, ssem, rsem):
        s, ps, dev = peer()
        _rcopy(ins[0].at[ps], outs[0].at[ps], ssem.at[0], rsem.at[0], dev).wait_recv()
        _rcopy(ins[0].at[ps], outs[0].at[s], ssem.at[0], rsem.at[0], dev).wait_send()

    if r is None:
        return Rider([q], [_sds(q.shape, q.dtype)], {}, 1, start, wait)
    return Rider([q, r], [_sds(q.shape, q.dtype)], {1: 0}, 1, start, wait)


def rider_reduce_copies(q):
    def start(ins, outs, ssem, rsem):
        mx, my, mc = _me()
        s = 2 * mx + my
        for j, (px, py) in enumerate(_chips_of(mx, my)):
            _rcopy(ins[0].at[2 * px + py], outs[0].at[s], ssem.at[j], rsem.at[j], (px, py, mc)).start()

    def wait(ins, outs, ssem, rsem):
        mx, my, mc = _me()
        s = 2 * mx + my
        for j, (px, py) in enumerate(_chips_of(mx, my)):
            ps = 2 * px + py
            _rcopy(ins[0].at[ps], outs[0].at[ps], ssem.at[j], rsem.at[j], (px, py, mc)).wait_recv()
        for j, (px, py) in enumerate(_chips_of(mx, my)):
            _rcopy(ins[0].at[2 * px + py], outs[0].at[s], ssem.at[j], rsem.at[j], (px, py, mc)).wait_send()

    return Rider([q], [_sds(q.shape, q.dtype)], {}, 3, start, wait)


def rider_join(buf, g=None):
    def start(ins, outs, ssem, rsem):
        o = outs[0] if g is None else outs[0].at[g]
        mx, my, mc = _me()
        _rcopy(o.at[mc], o.at[mc], ssem.at[0], rsem.at[0], (mx, my, 1 - mc)).start()

    def wait(ins, outs, ssem, rsem):
        o = outs[0] if g is None else outs[0].at[g]
        mx, my, mc = _me()
        _rcopy(o.at[1 - mc], o.at[1 - mc], ssem.at[0], rsem.at[0], (mx, my, 1 - mc)).wait_recv()
        _rcopy(o.at[mc], o.at[mc], ssem.at[0], rsem.at[0], (mx, my, 1 - mc)).wait_send()

    return Rider([buf], [_sds(buf.shape, buf.dtype)], {0: 0}, 1, start, wait)


def _host_call(body, rider, name, grid, in_specs, out_specs, out_shape, operands, sem, n_in, n_out, aliases=None):
    aliases = dict(aliases or {})
    if rider is None:
        return pl.pallas_call(body, name=name, out_shape=out_shape, grid=grid, in_specs=in_specs, out_specs=out_specs,
                              input_output_aliases=aliases, compiler_params=_cp(sem, VMEM_BIG))(*operands)
    n_ri, n_ro = len(rider.ins), len(rider.outs)
    aliases.update({n_in + a: n_out + b for a, b in rider.aliases.items()})

    def hosted(*refs):
        ins, r_in = refs[:n_in], refs[n_in:n_in + n_ri]
        outs, r_out = refs[n_in + n_ri:n_in + n_ri + n_out], refs[n_in + n_ri + n_out:n_in + n_ri + n_out + n_ro]
        ssem, rsem = refs[-2], refs[-1]
        first = functools.reduce(lambda a, b: a & b, [pl.program_id(k) == 0 for k in range(len(grid))])
        last = functools.reduce(lambda a, b: a & b, [pl.program_id(k) == grid[k] - 1 for k in range(len(grid))])

        @pl.when(first)
        def _():
            rider.start(r_in, r_out, ssem, rsem)
        body(*ins, *outs)

        @pl.when(last)
        def _():
            rider.wait(r_in, r_out, ssem, rsem)

    return pl.pallas_call(
        hosted, name=name, out_shape=list(out_shape) + list(rider.outs), grid=grid,
        in_specs=list(in_specs) + [ANY] * n_ri, out_specs=list(out_specs) + [ANY] * n_ro,
        input_output_aliases=aliases,
        scratch_shapes=[pltpu.SemaphoreType.DMA((rider.nsem,)), pltpu.SemaphoreType.DMA((rider.nsem,))],
        compiler_params=_cp(("arbitrary",) * len(grid), VMEM_BIG))(*operands, *rider.ins)


def ffn_up(lay, hm, wbuf, ig, iu, ns, name, rider=None):
    tm = lay.tm

    def body(h_ref, wg_ref, wu_ref, up_ref, sl_ref, a_ref):
        hb = h_ref[...]
        g = _nt(hb, wg_ref[0])
        u = _nt(hb, wu_ref[0])
        sg = _sigmoid(g)
        sl = g * sg
        up_ref[0] = (u * (sg + sl * (1.0 - sg))).astype(BF16)
        sl_ref[0] = sl.astype(BF16)
        a_ref[0] = (sl * u).astype(BF16)

    spec_o = pl.BlockSpec((1, tm, ns), lambda s, i: (s, i, 0))
    return _host_call(
        body, rider, name, (N_CHIP, lay.T // tm),
        [pl.BlockSpec((tm, D), lambda s, i: (i, 0)), pl.BlockSpec((1, ns, D), lambda s, i: (s, ig, 0)),
         pl.BlockSpec((1, ns, D), lambda s, i: (s, iu, 0))],
        [spec_o] * 3, [_sds((N_CHIP, lay.T, ns), BF16)] * 3, (hm, wbuf, wbuf), ("parallel", "parallel"), 3, 3)


def slab_nn_acc(lay, zs, wbuf, idxs, ns, name, rider=None):
    tm = lay.tm2
    npair = len(zs)

    def body(*refs):
        o_ref = refs[-1]

        @pl.when(pl.program_id(1) == 0)
        def _():
            o_ref[...] = jnp.zeros_like(o_ref)
        acc = _nn(refs[0][0], refs[npair][0])
        for p in range(1, npair):
            acc += _nn(refs[p][0], refs[npair + p][0])
        o_ref[...] += acc

    in_specs = [pl.BlockSpec((1, tm, ns), lambda i, s: (s, i, 0)) for _ in zs]
    in_specs += [pl.BlockSpec((1, ns, D), functools.partial(lambda i, s, q: (s, q, 0), q=q)) for q in idxs]
    return _host_call(body, rider, name, (lay.T // tm, N_CHIP), in_specs, [pl.BlockSpec((tm, D), lambda i, s: (i, 0))],
                      [_sds((lay.T, D), F32)], (*zs, *([wbuf] * npair)), ("parallel", "arbitrary"), 2 * npair, 1)


def ffn_bwd_da(lay, dy, wbuf, idn, up, sl, ns, name, rider=None):
    tm = lay.tm

    def body(dy_ref, wd_ref, up_ref, sl_ref, dg_ref, du_ref):
        da = _nt(dy_ref[...], wd_ref[0])
        dg_ref[0] = (da * up_ref[0].astype(F32)).astype(BF16)
        du_ref[0] = (da * sl_ref[0].astype(F32)).astype(BF16)

    spec_z = pl.BlockSpec((1, tm, ns), lambda s, i: (s, i, 0))
    return _host_call(
        body, rider, name, (N_CHIP, lay.T // tm),
        [pl.BlockSpec((tm, D), lambda s, i: (i, 0)), pl.BlockSpec((1, ns, D), lambda s, i: (s, idn, 0)), spec_z, spec_z],
        [spec_z] * 2, [_sds((N_CHIP, lay.T, ns), BF16)] * 2, (dy, wbuf, up, sl), ("parallel", "parallel"), 4, 2)


def slab_tn(lay, z, x, gbuf, idx, ns, name, rider=None):
    tk = _pick(lay.T, (768, 512, 256, 128))

    def body(z_ref, x_ref, g_in, o_ref):
        del g_in

        @pl.when(pl.program_id(0) == 0)
        def _():
            o_ref[...] = jnp.zeros_like(o_ref)
        xv = x_ref[...]
        for s in range(N_CHIP):
            o_ref[s] += _tn(z_ref[s], xv)

    return _host_call(
        body, rider, name, (lay.T // tk,),
        [pl.BlockSpec((N_CHIP, tk, ns), lambda k: (0, k, 0)), pl.BlockSpec((tk, D), lambda k: (k, 0)), ANY],
        [pl.BlockSpec((N_CHIP, ns, D), lambda k: (0, idx, 0))], [_sds(gbuf.shape, F32)], (z, x, gbuf),
        ("arbitrary",), 3, 1, aliases={2: 0})


Q0, K0, V0, U0, QR0, KR0, PEXT = 0, 512, 640, 768, 1280, 1792, 1920


def rope_fwd(lay, p, cos, sin, name):
    def fn(pb, cs, sn):
        cs4 = jnp.concatenate([cs] * 4, axis=1)
        sn4 = jnp.concatenate([sn] * 4, axis=1)
        qr = pb[:, Q0:K0] * cs4 + pb[:, QR0:KR0] * sn4
        kr = pb[:, K0:V0] * cs + pb[:, KR0:PEXT] * sn
        return qr, kr, pb[:, V0:U0], pb[:, U0:QR0]
    return rowwise(lay, name, fn, [p, cos, sin], outs=[(ATT_W, BF16), (KV_W, BF16), (KV_W, BF16), (POOL_W, F32)])


def rope_bwd(lay, dqr, dkr, dv, du, cos, sin, name):
    def fn(dq, dk, dvb, dub, cs, sn):
        cs4 = jnp.concatenate([cs] * 4, axis=1)
        sn4 = jnp.concatenate([sn] * 4, axis=1)
        return (jnp.concatenate([dq * cs4, dk * cs, dvb, dub, dq * sn4, dk * sn], axis=1),)
    return rowwise(lay, name, fn, [dqr, dkr, dv, du, cos, sin], outs=[(PEXT, BF16)])[0]


def _attn_specs(lay):
    nbs, cbk, lbk = lay.PS // BLK, lay.C // BLK, lay.L // BLK

    def kv_map(j):
        return lambda s, n: (s * nbs + cbk + jnp.clip(n - cbk + j - 1, 0, lbk - 1), 0)

    win = [pl.BlockSpec((BLK, KV_W), kv_map(j)) for j in range(3)]
    ctx = pl.BlockSpec((lay.C, KV_W), lambda s, n: (s * (lay.PS // lay.C), 0))
    return nbs, cbk, lbk, win, ctx


def _attn_masks(n, cbk, lbk):
    row = lax.broadcasted_iota(jnp.int32, (BLK, BLK), 0)
    col = lax.broadcasted_iota(jnp.int32, (BLK, BLK), 1)
    m = n - cbk
    lat = n >= cbk
    valid = [lat & (m >= 1) & (col >= row), lat & (col >= 0), lat & (m <= lbk - 2) & (col <= row)]
    lane_lo = lax.broadcasted_iota(jnp.int32, (BLK, 2 * HEAD_DIM), 1) < HEAD_DIM
    return valid, lane_lo


def attn_fwd(lay, qr, kr, vb, sink_tab, name):
    nbs, cbk, lbk, win, ctx = _attn_specs(lay)

    def body(q_ref, k0, k1, k2, kc_ref, v0, v1, v2, vc_ref, sk_ref, o_ref, l_ref):
        n = pl.program_id(1)
        valid, lane_lo = _attn_masks(n, cbk, lbk)
        valid4 = [jnp.concatenate([v] * 4, axis=0) for v in valid]
        ks = [k0[...], k1[...], k2[...]]
        vs = [v0[...], v1[...], v2[...]]
        kc, vc = kc_ref[...], vc_ref[...]
        q2s = [q_ref[:, p * 128:(p + 1) * 128] for p in range(4)]
        outs, lses = [], []
        for hh in range(2):
            sel = lane_lo == (hh == 0)
            qm = jnp.concatenate([jnp.where(sel, q2, jnp.zeros_like(q2)) for q2 in q2s], axis=0)
            sk = jnp.concatenate([jnp.broadcast_to(sk_ref[p:p + 1, hh * HEAD_DIM:hh * HEAD_DIM + 1], (BLK, 1))
                                  for p in range(4)], axis=0)
            sw = [jnp.where(valid4[j], _nt(qm, ks[j]) * ATT_SCALE, NEG_INF) for j in range(3)]
            sc = _nt(qm, kc) * ATT_SCALE
            mx = jnp.maximum(jnp.maximum(jnp.maximum(sw[0].max(-1, keepdims=True), sw[1].max(-1, keepdims=True)),
                                         jnp.maximum(sw[2].max(-1, keepdims=True), sc.max(-1, keepdims=True))), sk)
            ew = [jnp.exp(s - mx) for s in sw]
            ec = jnp.exp(sc - mx)
            den = ew[0].sum(-1, keepdims=True) + ew[1].sum(-1, keepdims=True) + ew[2].sum(-1, keepdims=True)
            den = den + ec.sum(-1, keepdims=True) + jnp.exp(sk - mx)
            o = _nn((ec / den).astype(BF16), vc)
            for j in range(3):
                o += _nn((ew[j] / den).astype(BF16), vs[j])
            outs.append(o)
            lses.append(mx + jnp.log(den))
        for p in range(4):
            rows = slice(p * BLK, (p + 1) * BLK)
            o_ref[:, p * 128:(p + 1) * 128] = jnp.where(lane_lo, outs[0][rows], outs[1][rows]).astype(o_ref.dtype)
            l_ref[:, p * 128:(p + 1) * 128] = jnp.where(lane_lo, jnp.broadcast_to(lses[0][rows], (BLK, 128)),
                                                        jnp.broadcast_to(lses[1][rows], (BLK, 128)))

    qspec = pl.BlockSpec((BLK, ATT_W), lambda s, n: (s * nbs + n, 0))
    return pl.pallas_call(
        body, name=name, out_shape=[_sds((lay.T, ATT_W), BF16), _sds((lay.T, ATT_W), F32)], grid=(2, nbs),
        in_specs=[qspec] + win + [ctx] + win + [ctx] + [pl.BlockSpec((8, 128), lambda s, n: (0, 0))],
        out_specs=[qspec, qspec], compiler_params=_cp(("parallel", "parallel")))(qr, kr, kr, kr, kr, vb, vb, vb, vb, sink_tab)


def attn_bwd(lay, qr, kr, vb, sink_tab, lse, datt, name, rider=None):
    nbs, cbk, lbk, win, ctx = _attn_specs(lay)
    C, PS = lay.C, lay.PS

    def body(q_ref, k0, k1, k2, kc_ref, v0, v1, v2, vc_ref, sk_ref, l_ref, do_ref, dq_ref, dk_ref, dv_ref, ds_ref):
        n = pl.program_id(1)
        valid, lane_lo = _attn_masks(n, cbk, lbk)

        @pl.when(n == 0)
        def _():
            dk_ref[...] = jnp.zeros_like(dk_ref)
            dv_ref[...] = jnp.zeros_like(dv_ref)
            ds_ref[...] = jnp.zeros_like(ds_ref)

        ks = [k0[...], k1[...], k2[...], kc_ref[...]]
        vs = [v0[...], v1[...], v2[...], vc_ref[...]]
        valid4 = [jnp.concatenate([v] * 4, axis=0) for v in valid]
        dks = [jnp.zeros((BLK, KV_W), F32)] * 3 + [jnp.zeros((C, KV_W), F32)]
        dvs = list(dks)
        q2s = [q_ref[:, p * 128:(p + 1) * 128] for p in range(4)]
        do2s = [do_ref[:, p * 128:(p + 1) * 128].astype(BF16) for p in range(4)]
        lse2s = [l_ref[:, p * 128:(p + 1) * 128] for p in range(4)]
        dq_h, dd_h = [], []
        for hh in range(2):
            sel = lane_lo == (hh == 0)
            qm = jnp.concatenate([jnp.where(sel, q2, jnp.zeros_like(q2)) for q2 in q2s], axis=0)
            dom = jnp.concatenate([jnp.where(sel, d2, jnp.zeros_like(d2)) for d2 in do2s], axis=0)
            lse_h = jnp.concatenate([l2[:, hh * HEAD_DIM:hh * HEAD_DIM + 1] for l2 in lse2s], axis=0)
            ps, dps = [], []
            for j in range(4):
                s = _nt(qm, ks[j]) * ATT_SCALE
                if j < 3:
                    s = jnp.where(valid4[j], s, NEG_INF)
                ps.append(jnp.exp(s - lse_h))
                dps.append(_nt(dom, vs[j]))
            dd = (ps[0] * dps[0]).sum(-1, keepdims=True) + (ps[1] * dps[1]).sum(-1, keepdims=True)
            dd = dd + (ps[2] * dps[2]).sum(-1, keepdims=True) + (ps[3] * dps[3]).sum(-1, keepdims=True)
            dq = jnp.zeros((4 * BLK, 128), F32)
            for j in range(4):
                dsb = (ps[j] * (dps[j] - dd) * ATT_SCALE).astype(BF16)
                dq += _nn(dsb, ks[j])
                dks[j] = dks[j] + _tn(dsb, qm)
                dvs[j] = dvs[j] + _tn(ps[j].astype(BF16), dom)
            dq_h.append(dq)
            dd_h.append(dd)
        for p in range(4):
            sl = slice(p * 128, (p + 1) * 128)
            rows = slice(p * BLK, (p + 1) * BLK)
            dq_ref[:, sl] = jnp.where(lane_lo, dq_h[0][rows], dq_h[1][rows])
            dd2 = jnp.where(lane_lo, jnp.broadcast_to(dd_h[0][rows], (BLK, 128)), jnp.broadcast_to(dd_h[1][rows], (BLK, 128)))
            psink = jnp.exp(sk_ref[p:p + 1, :] - lse2s[p])
            ds_ref[0, p:p + 1, :] += -jnp.sum(psink * dd2, axis=0, keepdims=True)
        dk_ref[0:C, :] += dks[3]
        dv_ref[0:C, :] += dvs[3]
        for j in range(3):
            r0 = pl.multiple_of((cbk + jnp.clip(n - cbk + j - 1, 0, lbk - 1)) * BLK, BLK)
            dk_ref[pl.ds(r0, BLK), :] += dks[j]
            dv_ref[pl.ds(r0, BLK), :] += dvs[j]

    qspec = pl.BlockSpec((BLK, ATT_W), lambda s, n: (s * nbs + n, 0))
    kvout = pl.BlockSpec((PS, KV_W), lambda s, n: (s, 0))
    return _host_call(
        body, rider, name, (2, nbs),
        [qspec] + win + [ctx] + win + [ctx] + [pl.BlockSpec((8, 128), lambda s, n: (0, 0)), qspec, qspec],
        [qspec, kvout, kvout, pl.BlockSpec((1, 8, 128), lambda s, n: (s, 0, 0))],
        [_sds((lay.T, ATT_W), F32), _sds((lay.T, KV_W), F32), _sds((lay.T, KV_W), F32), _sds((2, 8, 128), F32)],
        (qr, kr, kr, kr, kr, vb, vb, vb, vb, sink_tab, lse, datt), ("parallel", "arbitrary"), 12, 4)


def _winsum(x, r):
    n = x.shape[0]
    t = lax.broadcasted_iota(jnp.int32, x.shape, 0)
    acc = x
    for o in range(1, r + 1):
        acc = acc + jnp.where(t >= o, pltpu.roll(x, o, 0), 0.0) + jnp.where(t < n - o, pltpu.roll(x, n - o, 0), 0.0)
    return acc


def _wincount(n, r):
    t = lax.broadcasted_iota(jnp.int32, (n, 128), 0)
    return (jnp.minimum(t + r, n - 1) - jnp.maximum(t - r, 0) + 1).astype(F32)


def pool_fwd(lay, u, w_pool, scale, name):
    segs = [(0, lay.C), (lay.C, lay.L)]

    def body(u_ref, w_ref, s_ref, o_ref):
        for r0, n in segs:
            for g, wd in enumerate(POOL_WINDOWS):
                sl = slice(g * 128, (g + 1) * 128)
                x = u_ref[r0:r0 + n, sl]
                d = _winsum(x, wd // 2) / _wincount(n, wd // 2) - x
                y = _nn(d.astype(BF16), w_ref[g].astype(BF16)) * s_ref[:, sl]
                o_ref[r0:r0 + n, sl] = y.astype(o_ref.dtype)

    spec = pl.BlockSpec((lay.PS, POOL_W), lambda s: (s, 0))
    return pl.pallas_call(
        body, name=name, out_shape=_sds((lay.T, POOL_W), BF16), grid=(2,),
        in_specs=[spec, pl.BlockSpec(w_pool.shape, lambda s: (0, 0, 0)), pl.BlockSpec((1, POOL_W), lambda s: (0, 0))],
        out_specs=spec, compiler_params=_cp(("parallel",), VMEM_BIG))(u, w_pool, scale)


def pool_bwd(lay, u, dcat, w_pool, scale, name):
    segs = [(0, lay.C), (lay.C, lay.L)]

    def body(u_ref, dp_ref, w_ref, s_ref, du_ref, dw_ref, dsc_ref):
        for g, wd in enumerate(POOL_WINDOWS):
            sl = slice(g * 128, (g + 1) * 128)
            wb = w_ref[g].astype(BF16)
            dw = jnp.zeros((128, 128), F32)
            dsc = jnp.zeros((1, 128), F32)
            for r0, n in segs:
                x = u_ref[r0:r0 + n, sl]
                cnt = _wincount(n, wd // 2)
                d = (_winsum(x, wd // 2) / cnt - x).astype(BF16)
                dp = dp_ref[r0:r0 + n, sl]
                dsc += jnp.sum(_nn(d, wb) * dp, axis=0, keepdims=True)
                dyp = (dp * s_ref[:, sl]).astype(BF16)
                dw += _tn(d, dyp)
                dd = _nt(dyp, wb)
                du_ref[r0:r0 + n, sl] = _winsum(dd / cnt, wd // 2) - dd
            dw_ref[0, g] = dw
            dsc_ref[0, :, sl] = dsc

    spec = pl.BlockSpec((lay.PS, POOL_W), lambda s: (s, 0))
    return pl.pallas_call(
        body, name=name,
        out_shape=[_sds((lay.T, POOL_W), F32), _sds((2, 4, 128, 128), F32), _sds((2, 1, POOL_W), F32)], grid=(2,),
        in_specs=[spec, pl.BlockSpec((lay.PS, POOL_W), lambda s: (s, 1)), pl.BlockSpec(w_pool.shape, lambda s: (0, 0, 0)),
                  pl.BlockSpec((1, POOL_W), lambda s: (0, 0))],
        out_specs=[spec, pl.BlockSpec((1, 4, 128, 128), lambda s: (s, 0, 0, 0)), pl.BlockSpec((1, 1, POOL_W), lambda s: (s, 0, 0))],
        compiler_params=_cp(("parallel",), VMEM_BIG))(u, dcat, w_pool, scale)


CONV_OFFS = (-1, 0, 1, 2)
CW = 256


def _shift_rows(x, o):
    if o == 0:
        return x
    n = x.shape[0]
    t = lax.broadcasted_iota(jnp.int32, x.shape, 0)
    if o < 0:
        return jnp.where(t >= -o, pltpu.roll(x, -o, 0), 0.0)
    return jnp.where(t < n - o, pltpu.roll(x, n - o, 0), 0.0)


def conv_fwd(lay, p, col0, w, b, name):
    segs = [(0, lay.C), (lay.C, lay.L)]
    cb0 = col0 // CW

    def body(x_ref, w_ref, b_ref, o_ref):
        for r0, n in segs:
            x = x_ref[r0:r0 + n, :]
            y = jnp.broadcast_to(b_ref[...], x.shape)
            for k, o in enumerate(CONV_OFFS):
                y = y + _shift_rows(x, o) * w_ref[k:k + 1, :]
            o_ref[r0:r0 + n, :] = y

    return pl.pallas_call(
        body, name=name, out_shape=_sds((lay.T, D), F32), grid=(2, D // CW),
        in_specs=[pl.BlockSpec((lay.PS, CW), lambda s, j: (s, cb0 + j)), pl.BlockSpec((4, CW), lambda s, j: (0, j)),
                  pl.BlockSpec((1, CW), lambda s, j: (0, j))],
        out_specs=pl.BlockSpec((lay.PS, CW), lambda s, j: (s, j)),
        compiler_params=_cp(("parallel", "parallel")))(p, w, b)


def conv_bwd(lay, p, col0, w, duc, name):
    segs = [(0, lay.C), (lay.C, lay.L)]
    cb0 = col0 // CW

    def body(x_ref, w_ref, g_ref, du_ref, dw_ref, db_ref):
        dws = [jnp.zeros((1, CW), F32)] * 4
        db = jnp.zeros((1, CW), F32)
        for r0, n in segs:
            x = x_ref[r0:r0 + n, :]
            g = g_ref[r0:r0 + n, :]
            du = jnp.zeros_like(g)
            for k, o in enumerate(CONV_OFFS):
                du = du + _shift_rows(g, -o) * w_ref[k:k + 1, :]
                dws[k] = dws[k] + jnp.sum(g * _shift_rows(x, o), axis=0, keepdims=True)
            db = db + jnp.sum(g, axis=0, keepdims=True)
            du_ref[r0:r0 + n, :] = du.astype(du_ref.dtype)
        dw_ref[0] = jnp.concatenate(dws, axis=0)
        db_ref[0] = db

    return pl.pallas_call(
        body, name=name, out_shape=[_sds((lay.T, D), BF16), _sds((2, 4, D), F32), _sds((2, 1, D), F32)], grid=(2, D // CW),
        in_specs=[pl.BlockSpec((lay.PS, CW), lambda s, j: (s, cb0 + j)), pl.BlockSpec((4, CW), lambda s, j: (0, j)),
                  pl.BlockSpec((lay.PS, CW), lambda s, j: (s, j))],
        out_specs=[pl.BlockSpec((lay.PS, CW), lambda s, j: (s, j)), pl.BlockSpec((1, 4, CW), lambda s, j: (s, 0, j)),
                   pl.BlockSpec((1, 1, CW), lambda s, j: (s, 0, j))],
        compiler_params=_cp(("parallel", "parallel")))(p, w, duc)


def _softplus_neg(lam):
    z = -lam
    w = jnp.exp(-jnp.abs(z))
    log1p = jnp.where(w < 1e-2, w * (1.0 - w * (0.5 - w / 3.0)), jnp.log(1.0 + w))
    return jnp.maximum(z, 0.0) + log1p, -_sigmoid(z)


def _neg_expm1(x):
    series = -x * (1.0 + x * (0.5 + x * (1.0 / 6.0 + x * (1.0 / 24.0 + x * (1.0 / 120.0)))))
    return jnp.where(x > -0.05, series, 1.0 - jnp.exp(x))


def _lru_gates(x, xb, wa, wx, ba, bx, lam):
    r = _sigmoid(_nn(xb, wa.astype(BF16)) + ba)
    gi = _sigmoid(_nn(xb, wx.astype(BF16)) + bx)
    sp, dsp = _softplus_neg(lam)
    la = -LRU_C * r * sp
    a = jnp.exp(la)
    sq = jnp.sqrt(_neg_expm1(2.0 * la))
    return r, gi, sp, dsp, a, sq


def lru_coeffs(lay, uc, wa, wx, vec, name):
    tr = lay.tc

    def body(x_ref, wa_ref, wx_ref, v_ref, a_ref, b_ref):
        for h in range(8):
            sl = slice(h * 128, (h + 1) * 128)
            x = x_ref[:, sl]
            xb = x.astype(BF16)
            for d in range(2):
                _, gi, _, _, a, sq = _lru_gates(x, xb, wa_ref[d, h], wx_ref[d, h], v_ref[d:d + 1, sl],
                                                v_ref[2 + d:3 + d, sl], v_ref[4 + d:5 + d, sl])
                a_ref[d, h] = a
                b_ref[d, h] = sq * (gi * x)

    wspec = pl.BlockSpec((2, 8, 128, 128), lambda i: (0, 0, 0, 0))
    ospec = pl.BlockSpec((2, 8, tr, 128), lambda i: (0, 0, i, 0))
    return pl.pallas_call(
        body, name=name, out_shape=[_sds((2, 8, lay.T, 128), F32)] * 2, grid=(lay.T // tr,),
        in_specs=[pl.BlockSpec((tr, D), lambda i: (i, 0)), wspec, wspec, pl.BlockSpec((6, D), lambda i: (0, 0))],
        out_specs=[ospec, ospec], compiler_params=_cp(("parallel",), VMEM_BIG))(uc, wa, wx, vec)


def lru_coeffs_bwd(lay, uc, wa, wx, vec, da, db, name, rider=None):
    tr = lay.tc

    def body(x_ref, wa_ref, wx_ref, v_ref, da_ref, db_ref, dx_ref, dwa_ref, dwx_ref, dv_ref):
        @pl.when(pl.program_id(0) == 0)
        def _():
            dwa_ref[...] = jnp.zeros_like(dwa_ref)
            dwx_ref[...] = jnp.zeros_like(dwx_ref)
            dv_ref[...] = jnp.zeros_like(dv_ref)

        for h in range(8):
            sl = slice(h * 128, (h + 1) * 128)
            x = x_ref[:, sl]
            xb = x.astype(BF16)
            dx = jnp.zeros_like(x)
            for d in range(2):
                wab, wxb = wa_ref[d, h].astype(BF16), wx_ref[d, h].astype(BF16)
                r, gi, sp, dsp, a, sq = _lru_gates(x, xb, wa_ref[d, h], wx_ref[d, h], v_ref[d:d + 1, sl],
                                                   v_ref[2 + d:3 + d, sl], v_ref[4 + d:5 + d, sl])
                dbv, dav = db_ref[d, h], da_ref[d, h]
                t1 = dbv * sq
                dgi = t1 * x
                dx = dx + t1 * gi
                dla = dav * a - (dbv * gi * x) * (a * a) / sq
                dr = dla * (-LRU_C * sp)
                dlam = jnp.sum(dla * (-LRU_C * r), axis=0, keepdims=True) * dsp
                dpa = dr * r * (1.0 - r)
                dpx = dgi * gi * (1.0 - gi)
                dpab, dpxb = dpa.astype(BF16), dpx.astype(BF16)
                dwa_ref[d, h] += _tn(xb, dpab)
                dwx_ref[d, h] += _tn(xb, dpxb)
                dx = dx + _nt(dpab, wab) + _nt(dpxb, wxb)
                dv_ref[d:d + 1, sl] += jnp.sum(dpa, axis=0, keepdims=True)
                dv_ref[2 + d:3 + d, sl] += jnp.sum(dpx, axis=0, keepdims=True)
                dv_ref[4 + d:5 + d, sl] += dlam
            dx_ref[:, sl] = dx

    wspec = pl.BlockSpec((2, 8, 128, 128), lambda i: (0, 0, 0, 0))
    gspec = pl.BlockSpec((2, 8, tr, 128), lambda i: (0, 0, i, 0))
    vspec = pl.BlockSpec((6, D), lambda i: (0, 0))
    xspec = pl.BlockSpec((tr, D), lambda i: (i, 0))
    return _host_call(
        body, rider, name, (lay.T // tr,), [xspec, wspec, wspec, vspec, gspec, gspec], [xspec, wspec, wspec, vspec],
        [_sds((lay.T, D), F32), _sds((2, 8, 128, 128), F32), _sds((2, 8, 128, 128), F32), _sds((6, D), F32)],
        (uc, wa, wx, vec, da, db), ("arbitrary",), 6, 4)


GB = 2
SCAN_UNROLL = 8


def _tile_scan(a, b, up):
    t = lax.broadcasted_iota(jnp.int32, a.shape, 0)
    for d in (1, 2, 4):
        sh = 8 - d if up else d
        m = (t < 8 - d) if up else (t >= d)
        a_prev, b_prev = pltpu.roll(a, sh, 0), pltpu.roll(b, sh, 0)
        b = jnp.where(m, a * b_prev + b, b)
        a = jnp.where(m, a * a_prev, a)
    return a, b


def lru_scan(lay, a, b, name):
    segs = [(0, lay.C), (lay.C, lay.L)]

    def body(a_ref, b_ref, s_ref):
        for d in range(2):
            rev = d == 1
            state = tuple(jnp.zeros((1, 128), F32) for _ in range(GB))
            for base, n in segs:
                nt = n // 8

                def step(j, c, base=base, nt=nt, rev=rev, d=d):
                    c = list(c)
                    for u in range(SCAN_UNROLL):
                        jj = j * SCAN_UNROLL + u
                        r0 = pl.multiple_of(base + 8 * ((nt - 1 - jj) if rev else jj), 8)
                        for g in range(GB):
                            at, bt = _tile_scan(a_ref[d, g, pl.ds(r0, 8), :], b_ref[d, g, pl.ds(r0, 8), :], rev)
                            h = at * c[g] + bt
                            s_ref[d, g, pl.ds(r0, 8), :] = h
                            c[g] = h[0:1] if rev else h[7:8]
                    return tuple(c)

                state = lax.fori_loop(0, nt // SCAN_UNROLL, step, state)

    spec = pl.BlockSpec((2, GB, lay.PS, 128), lambda s, hb: (0, hb, s, 0))
    return pl.pallas_call(
        body, name=name, out_shape=_sds((2, 8, lay.T, 128), F32), grid=(2, 8 // GB),
        in_specs=[spec, spec], out_specs=spec, compiler_params=_cp(("parallel", "parallel"), VMEM_BIG))(a, b)


def lru_scan_bwd(lay, a, s, dy, name):
    segs = [(0, lay.C), (lay.C, lay.L)]
    C, PS = lay.C, lay.PS

    def body(a_ref, s_ref, g_ref, da_ref, db_ref):
        t = lax.broadcasted_iota(jnp.int32, (8, 128), 0)
        for d in range(2):
            rev = d == 1
            carry = tuple(jnp.zeros((1, 128), F32) for _ in range(GB))
            for si in (1, 0):
                base, n = segs[si]
                nt = n // 8

                def step(j, c, base=base, nt=nt, rev=rev, d=d):
                    c = list(c)
                    for u in range(SCAN_UNROLL):
                        jj = j * SCAN_UNROLL + u
                        r0 = pl.multiple_of(base + 8 * (jj if rev else (nt - 1 - jj)), 8)
                        if rev:
                            rn = pl.multiple_of(jnp.where(r0 == PS - 8, 0, r0 + 8), 8)
                            nb_zero = r0 == C - 8
                        else:
                            rn = pl.multiple_of(jnp.maximum(r0 - 8, 0), 8)
                            nb_zero = r0 == 0
                        for g in range(GB):
                            av = a_ref[d, g, pl.ds(r0, 8), :]
                            gv = g_ref[g, pl.ds(r0, 8), :]
                            sv = s_ref[d, g, pl.ds(r0, 8), :]
                            nbt = s_ref[d, g, pl.ds(rn, 8), :]
                            at, bt = _tile_scan(av, av * gv, not rev)
                            m = at * c[g] + bt
                            if rev:
                                m_next = jnp.where(t >= 1, pltpu.roll(m, 1, 0), c[g])
                                nb = jnp.where(nb_zero, 0.0, nbt[0:1])
                                h_prev = jnp.where(t < 7, pltpu.roll(sv, 7, 0), nb)
                                c[g] = m[7:8]
                            else:
                                m_next = jnp.where(t < 7, pltpu.roll(m, 7, 0), c[g])
                                nb = jnp.where(nb_zero, 0.0, nbt[7:8])
                                h_prev = jnp.where(t >= 1, pltpu.roll(sv, 1, 0), nb)
                                c[g] = m[0:1]
                            lam = gv + m_next
                            db_ref[d, g, pl.ds(r0, 8), :] = lam
                            da_ref[d, g, pl.ds(r0, 8), :] = lam * h_prev
                    return tuple(c)

                carry = lax.fori_loop(0, nt // SCAN_UNROLL, step, carry)

    spec = pl.BlockSpec((2, GB, lay.PS, 128), lambda s, hb: (0, hb, s, 0))
    return pl.pallas_call(
        body, name=name, out_shape=[_sds((2, 8, lay.T, 128), F32)] * 2, grid=(2, 8 // GB),
        in_specs=[spec, spec, pl.BlockSpec((GB, lay.PS, 128), lambda s, hb: (hb, s, 0))],
        out_specs=[spec, spec], compiler_params=_cp(("parallel", "parallel"), VMEM_BIG))(a, s, dy)


def _gelu(x):
    k = math.sqrt(2.0 / math.pi)
    t = jnp.tanh(k * (x + 0.044715 * x * x * x))
    return 0.5 * x * (1.0 + t), 0.5 * (1.0 + t) + 0.5 * x * (1.0 - t * t) * k * (1.0 + 3 * 0.044715 * x * x)


def lru_gate(lay, p, s, name):
    tr = lay.tr

    def body(g_ref, s_ref, o_ref):
        for h in range(8):
            sl = slice(h * 128, (h + 1) * 128)
            o_ref[:, sl] = (_gelu(g_ref[:, sl])[0] * (s_ref[0, h] + s_ref[1, h])).astype(o_ref.dtype)

    return pl.pallas_call(
        body, name=name, out_shape=_sds((lay.T, D), BF16), grid=(lay.nblk,),
        in_specs=[pl.BlockSpec((tr, D), lambda i: (i, 0)), pl.BlockSpec((2, 8, tr, 128), lambda i: (0, 0, i, 0))],
        out_specs=pl.BlockSpec((tr, D), lambda i: (i, 0)), compiler_params=_cp(("parallel",)))(p, s)


def lru_gate_bwd(lay, p, s, do, name):
    tr = lay.tr

    def body(g_ref, s_ref, do_ref, dg_ref, dy_ref):
        for h in range(8):
            sl = slice(h * 128, (h + 1) * 128)
            ge, dge = _gelu(g_ref[:, sl])
            dov = do_ref[:, sl]
            dg_ref[:, sl] = (dov * (s_ref[0, h] + s_ref[1, h]) * dge).astype(dg_ref.dtype)
            dy_ref[h] = dov * ge

    xspec = pl.BlockSpec((tr, D), lambda i: (i, 0))
    return pl.pallas_call(
        body, name=name, out_shape=[_sds((lay.T, D), BF16), _sds((8, lay.T, 128), F32)], grid=(lay.nblk,),
        in_specs=[xspec, pl.BlockSpec((2, 8, tr, 128), lambda i: (0, 0, i, 0)), xspec],
        out_specs=[xspec, pl.BlockSpec((8, tr, 128), lambda i: (0, i, 0))],
        compiler_params=_cp(("parallel",)))(p, s, do)


def silu_rows(x, name):
    def body(x_ref, o_ref):
        v = x_ref[...]
        o_ref[...] = (v * _sigmoid(v)).astype(o_ref.dtype)
    return pl.pallas_call(body, name=name, out_shape=_sds(x.shape, BF16), in_specs=[VMEM_SPEC], out_specs=VMEM_SPEC)(x)


def mod_grad_rows(gath, name):
    w = gath.shape[-1]

    def body(g_ref, dm_ref, db_ref):
        dm_ref[...] = jnp.zeros_like(dm_ref)
        for l in range(2):
            ctx = g_ref[0, 3 * l + 2:3 * l + 3, :]
            tot = g_ref[0, 3 * l:3 * l + 1, :] + g_ref[0, 3 * l + 1:3 * l + 2, :]
            for k in range(8):
                dm_ref[l, 2 * k:2 * k + 2, :] = g_ref[k, 3 * l:3 * l + 2, :]
                if k:
                    ctx = ctx + g_ref[k, 3 * l + 2:3 * l + 3, :]
                    tot = tot + (g_ref[k, 3 * l:3 * l + 1, :] + g_ref[k, 3 * l + 1:3 * l + 2, :])
            dm_ref[l, 16:17, :] = ctx
            db_ref[l:l + 1, :] = tot + ctx

    return pl.pallas_call(body, name=name, out_shape=[_sds((2, 32, w), F32), _sds((2, w), F32)],
                          in_specs=[VMEM_SPEC], out_specs=[VMEM_SPEC, VMEM_SPEC])(gath)


def cctx_grad(p, c_ctx, name):
    def body(a_ref, c_ref, o_ref):
        cv = c_ref[...]
        sg = _sigmoid(cv)
        o_ref[...] = 0.5 * (a_ref[0, 0:1, :] + a_ref[1, 0:1, :]) * (sg * (1.0 + cv * (1.0 - sg)))
    return pl.pallas_call(body, name=name, out_shape=_sds((1, D), F32), in_specs=[VMEM_SPEC] * 2,
                          out_specs=VMEM_SPEC)(p, c_ctx)


def loss_and_grad(lay, h, tgt, name):
    def fn(hb, tb):
        lat = (pl.program_id(0) % lay.bps) >= lay.cb
        e = jnp.where(lat, hb - tb, 0.0)
        return e * (1.0 / D), jnp.sum(e * e, axis=0, keepdims=True) * (0.5 / D)
    return rowwise(lay, name, fn, [h, tgt], outs=[(D, F32)], sums=[(1, D)])


def adamw(w, g, m, v, name):
    shape = w.shape
    w2, g2, m2, v2 = (t.reshape(-1, shape[-1]) for t in (w, g, m, v))
    rows, width = w2.shape
    tr = 256 if rows % 256 == 0 else rows
    c1 = 1.0 - ADAM_B1 ** ADAM_STEP
    c2 = 1.0 - ADAM_B2 ** ADAM_STEP

    def body(w_ref, g_ref, m_ref, v_ref, d_ref, mo_ref, vo_ref):
        gv = g_ref[...]
        mn = ADAM_B1 * m_ref[...] + (1.0 - ADAM_B1) * gv
        vn = ADAM_B2 * v_ref[...] + (1.0 - ADAM_B2) * (gv * gv)
        d_ref[...] = -ADAM_LR * ((mn / c1) / (jnp.sqrt(vn / c2) + ADAM_EPS) + ADAM_WD * w_ref[...])
        mo_ref[...] = mn
        vo_ref[...] = vn

    spec = pl.BlockSpec((tr, width), lambda i: (i, 0))
    d, mn, vn = pl.pallas_call(body, name=name, out_shape=[_sds((rows, width), F32)] * 3, grid=(rows // tr,),
                               in_specs=[spec] * 4, out_specs=[spec] * 3, compiler_params=_cp(("parallel",)))(w2, g2, m2, v2)
    return d.reshape(shape), mn.reshape(shape), vn.reshape(shape)


def adamw_ffn(w, m, v, red, kind, ns, name):
    shape = w.shape
    w2, m2, v2 = (t.reshape(-1, shape[-1]) for t in (w, m, v))
    rows, width = w2.shape
    c1 = 1.0 - ADAM_B1 ** ADAM_STEP
    c2 = 1.0 - ADAM_B2 ** ADAM_STEP
    tr, nb = ns // 2, 2
    gspec = pl.BlockSpec((tr, D), lambda i: (((i // nb) * 3 + kind) * nb + i % nb, 0))

    def body(w_ref, g_ref, m_ref, v_ref, go_ref, d_ref, mo_ref, vo_ref):
        gv = g_ref[...]
        mn = ADAM_B1 * m_ref[...] + (1.0 - ADAM_B1) * gv
        vn = ADAM_B2 * v_ref[...] + (1.0 - ADAM_B2) * (gv * gv)
        go_ref[...] = gv
        d_ref[...] = -ADAM_LR * ((mn / c1) / (jnp.sqrt(vn / c2) + ADAM_EPS) + ADAM_WD * w_ref[...])
        mo_ref[...] = mn
        vo_ref[...] = vn

    spec = pl.BlockSpec((tr, width), lambda i: (i, 0))
    outs = pl.pallas_call(body, name=name, out_shape=[_sds((rows, width), F32)] * 4, grid=(rows // tr,),
                          in_specs=[spec, gspec, spec, spec], out_specs=[spec] * 4,
                          compiler_params=_cp(("parallel",)))(w2, red, m2, v2)
    return tuple(t.reshape(shape) for t in outs)


def mod_mm(sc, w_mod, bias, name):
    wm = w_mod.shape[-1]
    tn = _pick(wm, (768, 512, 384, 256, 128))

    def body(a_ref, b_ref, c_ref, o_ref):
        o_ref[...] = _nn(a_ref[...], b_ref[...].astype(BF16)) + c_ref[...]

    return pl.pallas_call(
        body, name=name, out_shape=_sds((DEPTH, 32, wm), F32), grid=(DEPTH, wm // tn),
        in_specs=[pl.BlockSpec((32, D), lambda l, j: (0, 0)), pl.BlockSpec((None, D, tn), lambda l, j: (l, 0, j)),
                  pl.BlockSpec((None, 1, tn), lambda l, j: (l, 0, j))],
        out_specs=pl.BlockSpec((None, 32, tn), lambda l, j: (l, 0, j)),
        compiler_params=_cp(("parallel", "parallel")))(sc, w_mod, bias)


def wmod_dw(sc, dcol, name):
    wm = dcol.shape[-1]
    tm = 256

    def body(a_ref, b_ref, o_ref):
        o_ref[...] = _tn(a_ref[...], b_ref[...].astype(BF16))

    return pl.pallas_call(
        body, name=name, out_shape=_sds((DEPTH, D, wm), F32), grid=(DEPTH, D // tm),
        in_specs=[pl.BlockSpec((32, tm), lambda l, i: (0, i)), pl.BlockSpec((None, 32, wm), lambda l, i: (l, 0, 0))],
        out_specs=pl.BlockSpec((None, tm, wm), lambda l, i: (l, i, 0)),
        compiler_params=_cp(("parallel", "parallel")))(sc, dcol)


def cctx_dx(drow, w_mod, name):
    wm = w_mod.shape[-1]

    def body(a_ref, b_ref, o_ref):
        o_ref[...] = _nt(a_ref[...].astype(BF16), b_ref[...].astype(BF16))

    return pl.pallas_call(
        body, name=name, out_shape=_sds((DEPTH, 16, D), F32), grid=(DEPTH,),
        in_specs=[pl.BlockSpec((None, 16, wm), lambda l: (l, 0, 0)), pl.BlockSpec((None, D, wm), lambda l: (l, 0, 0))],
        out_specs=pl.BlockSpec((None, 16, D), lambda l: (l, 0, 0)), compiler_params=_cp(("parallel",), VMEM_BIG))(drow, w_mod)


HEAD_PERM = (0, 4, 1, 5, 2, 6, 3, 7)


def _rot_rows(wt):
    return jnp.concatenate([-wt[32:64], wt[0:32]], axis=0)


def _unrot_rows(g):
    return jnp.concatenate([g[32:64], -g[0:32]], axis=0)


def _heads(a, n):
    return [a[64 * i:64 * (i + 1)] for i in range(n)]


def kernel(x, c, ctx, c_ctx, w_mod, b_mod, ln_g, ln_b, ffn_w_gate, ffn_w_up, ffn_w_down, mix_ab_w_in, attn_sink, pool_w, pool_scale, mix_ab_w_out, lru_w_in, lru_conv_w, lru_conv_b, lru_wa, lru_ba, lru_wx, lru_bx, lru_lambda, lru_w_out, loss_target, m_c_ctx, m_w_mod, m_b_mod, m_ln_g, m_ln_b, m_ffn_w_gate, m_ffn_w_up, m_ffn_w_down, m_mix_ab_w_in, m_attn_sink, m_pool_w, m_pool_scale, m_mix_ab_w_out, m_lru_w_in, m_lru_conv_w, m_lru_conv_b, m_lru_wa, m_lru_ba, m_lru_wx, m_lru_bx, m_lru_lambda, m_lru_w_out, v_c_ctx, v_w_mod, v_b_mod, v_ln_g, v_ln_b, v_ffn_w_gate, v_ffn_w_up, v_ffn_w_down, v_mix_ab_w_in, v_attn_sink, v_pool_w, v_pool_scale, v_mix_ab_w_out, v_lru_w_in, v_lru_conv_w, v_lru_conv_b, v_lru_wa, v_lru_ba, v_lru_wx, v_lru_bx, v_lru_lambda, v_lru_w_out):
    n_lat, n_ctx = x.shape[1], ctx.shape[1]
    lay = Layout(n_ctx, n_lat)
    T = lay.T
    ns = ffn_w_gate.shape[-1]
    n_li, n_ai = lru_w_in.shape[-1], mix_ab_w_in.shape[-1]
    n_ao, n_lo = mix_ab_w_out.shape[1], lru_w_out.shape[1]
    wm = w_mod.shape[-1]
    dsh = ln_g.shape[-1]
    mx, my, mc = lax.axis_index("x"), lax.axis_index("y"), lax.axis_index("c")
    chip = 2 * mx + my
    me = 2 * chip + mc

    c_all = all_gather8(c, "ag8_c").reshape(16, D)
    cc = jnp.concatenate([c_all, c_ctx[None, :], jnp.zeros((15, D), F32)], axis=0)
    sc = silu_rows(cc, "silu_c")
    bias = lax.dynamic_slice(b_mod, (0, chip * wm), (DEPTH, wm)).reshape(DEPTH, 1, wm)
    modg = all_gather_chips(mod_mm(sc, w_mod, bias, "mod_mm"), "ag_mod")
    modtab = []
    for l in range(DEPTH):
        full = jnp.transpose(modg[:, l], (1, 0, 2)).reshape(32, N_CHIP * wm)
        mine = lax.dynamic_slice(full, (2 * me, 0), (2, N_CHIP * wm))
        modtab.append(jnp.concatenate([mine, full[16:17]], axis=0).reshape(3, N_MOD, D))

    small = jnp.concatenate([ln_g.reshape(6, dsh), ln_b.reshape(6, dsh), lru_conv_w[0], lru_conv_b, lru_ba[0],
                             lru_bx[0], lru_lambda[0], jnp.zeros((9, dsh), F32)], axis=0)
    small = all_gather_chips(small.reshape(2, 16, dsh), "ag_small").reshape(N_CHIP, 32, dsh)
    small = jnp.transpose(small, (1, 0, 2)).reshape(32, D)
    ln_g_f, ln_b_f = small[0:6].reshape(2, 3, D), small[6:12].reshape(2, 3, D)
    conv_w_f, conv_b_f = small[12:16], small[16:17]
    lru_vec = small[17:23]

    hh = 3 * ns // 2
    gate_t, up_t = jnp.swapaxes(ffn_w_gate, -1, -2), jnp.swapaxes(ffn_w_up, -1, -2)
    extra = [0, n_ai + n_ao, n_li + n_lo, 0]
    placed = [ffn_place(gate_t, up_t, ffn_w_down, g // 2, g % 2, f"ag_ffn{g}_place", extra[g]) for g in range(4)]
    placed[1] = place_rows(placed[1], jnp.concatenate([mix_ab_w_in[0].T, mix_ab_w_out[0]], axis=0).astype(BF16), 3 * ns,
                           "ag_mixa_place")
    placed[2] = place_rows(placed[2], jnp.concatenate([lru_w_in[0].T, lru_w_out[0]], axis=0).astype(BF16), 3 * ns,
                           "ag_mixc_place")
    placed = [p.reshape(N_CHIP, 2, p.shape[1] // 2, D) for p in placed]
    wb = [gather_placed(placed[0], "ag_ffn0"), None, None, None]
    mixw = {}

    def mixa_w():
        if "a" not in mixw:
            full = wb[1].reshape(N_CHIP, -1, D)
            ab_in_t = full[:, 3 * ns:3 * ns + n_ai].reshape(N_CHIP * n_ai, D)
            ab_out = full[:, 3 * ns + n_ai:].reshape(N_CHIP * n_ao, D)
            qh, kh = _heads(ab_in_t[Q0:K0], N_HEADS), _heads(ab_in_t[K0:V0], N_KV)
            w_ext_t = jnp.concatenate([qh[h] for h in HEAD_PERM] + [ab_in_t[K0:QR0]]
                                      + [_rot_rows(qh[h]) for h in HEAD_PERM] + [_rot_rows(t) for t in kh], axis=0)
            oh = _heads(ab_out[0:ATT_W], N_HEADS)
            mixw["a"] = (w_ext_t, jnp.concatenate([oh[h] for h in HEAD_PERM] + [ab_out[ATT_W:]], axis=0))
        return mixw["a"]

    def mixc_w():
        if "c" not in mixw:
            full = wb[2].reshape(N_CHIP, -1, D)
            mixw["c"] = (full[:, 3 * ns:3 * ns + n_li].reshape(N_CHIP * n_li, D),
                         full[:, 3 * ns + n_li:].reshape(N_CHIP * n_lo, D))
        return mixw["c"]

    t = jnp.arange(n_lat)
    inv = ROPE_THETA ** (-jnp.arange(16, dtype=F32) / 16.0)
    ang = jnp.concatenate([(t // GRID_W).astype(F32)[:, None] * inv, (t % GRID_W).astype(F32)[:, None] * inv], axis=-1)
    cos1 = jnp.concatenate([jnp.ones((n_ctx, 32), F32), jnp.cos(ang)], axis=0)
    sin1 = jnp.concatenate([jnp.zeros((n_ctx, 32), F32), jnp.sin(ang)], axis=0)
    cos_t = jnp.tile(cos1, (2, 4))
    sin_t = jnp.tile(sin1, (2, 4))
    sk = attn_sink[0]
    sink_tab = jnp.concatenate([jnp.repeat(jnp.stack([sk[:4], sk[4:]], axis=1), HEAD_DIM, axis=1),
                                jnp.zeros((4, 128), F32)], axis=0)
    pscale = pool_scale.reshape(1, POOL_W)

    h0 = jnp.concatenate([ctx, x], axis=1).reshape(T, D)
    tgt = loss_target.reshape(2 * n_lat, D)

    def lnv(l, j):
        return jnp.stack([ln_g_f[l, j], ln_b_f[l, j]])

    subs = [(0, 0, 0.5, 0), (0, 3, 1.0, 1), (0, 6, 0.5, 2), (1, 0, 0.5, 0), (1, 3, 1.0, 1), (1, 6, 0.5, 2)]

    def ffn_core(hm, l, f):
        tag = f"l{l}f{f}"
        gi = 2 * l + f
        w = wb[gi].reshape(N_CHIP, -1, D)
        if gi == 3:
            up, sl, a = ffn_up(lay, hm, w, 0, 1, ns, f"ffn_up_{tag}")
            (y,) = slab_nn_acc(lay, [a], w, [2], ns, f"ffn_down_{tag}")
            return y, dict(up=up, sl=sl, a=a, nbuf=None)
        up, sl, a, nbuf = ffn_up(lay, hm, w, 0, 1, ns, f"ffn_up_{tag}", rider=rider_gather_xy(placed[gi + 1]))
        y, nbuf = slab_nn_acc(lay, [a], w, [2], ns, f"ffn_down_{tag}", rider=rider_gather_fwd(nbuf))
        return y, dict(up=up, sl=sl, a=a, nbuf=nbuf)

    def mixa_core(hm):
        p = mm_nt(hm, mixa_w()[0], "mixa_in")
        qr, kr, vb, u = rope_fwd(lay, p, cos_t, sin_t, "rope")
        att, lse = attn_fwd(lay, qr, kr, vb, sink_tab, "attn")
        pool = pool_fwd(lay, u, pool_w[0], pscale, "pool")
        cat = jnp.concatenate([att, pool], axis=1)
        return mm_nn(cat, mixa_w()[1], "mixa_out"), dict(qr=qr, kr=kr, vb=vb, u=u, lse=lse, cat=cat)

    def mixc_core(hm):
        p = mm_nt(hm, mixc_w()[0], "mixc_in")
        uc = conv_fwd(lay, p, D, conv_w_f, conv_b_f, "conv")
        a, b = lru_coeffs(lay, uc, lru_wa[0], lru_wx[0], lru_vec, "lru_coef")
        s = lru_scan(lay, a, b, "lru_scan")
        o = lru_gate(lay, p, s, "lru_gate")
        return mm_nn(o, mixc_w()[1], "mixc_out"), dict(p=p, uc=uc, a=a, s=s, o=o)

    recs = []
    h = h0
    hm = modulate(lay, h0, modtab[0], 0, 1, "mod_first")
    for k, (l, k0, coef, j) in enumerate(subs):
        if k0 == 3:
            y, core = mixa_core(hm) if l == 0 else mixc_core(hm)
        else:
            y, core = ffn_core(hm, l, k0 // 6)
        nxt = None if k == 5 else (modtab[subs[k + 1][0]], subs[k + 1][1], subs[k + 1][1] + 1)
        nbuf = core.pop("nbuf", None)
        res = resid_ln(lay, h, y, modtab[l], k0 + 2, coef, lnv(l, j), f"ln_s{k}", nxt=nxt,
                       prev_ln=None if k == 0 else lnv(*subs[k - 1][::3]),
                       rider=None if nbuf is None else rider_gather_d2d(nbuf))
        if nbuf is not None:
            wb[2 * l + k0 // 6 + 1] = res[-1]
        recs.append(dict(h=h, hm=hm, y=y, xhat=res[0], rstd=res[1], **core))
        h, hm = res[0], res[2]

    dout, lparts = loss_and_grad(lay, hm, tgt, "loss")
    loss = lax.psum(jnp.sum(lparts), ("x", "y", "c"))

    dln = {}
    dms = {}
    mixg = {}
    ffn_red = [lax.empty((4, 2, hh, D), F32)]
    mix_red = {}
    pending = []

    def rs_sib(p):
        return None if p is None else rider_reduce_sib(p["buf"])

    def rs_add2(p, recv):
        p["q"] = add_own_half(p["buf"], recv, BF16, f"rs_add2_{p['key']}")

    def rs_join(p, arr):
        if isinstance(p["key"], int):
            return rider_join(sum_slots(p["q"], arr, f"rs_add4_{p['key']}", dst=ffn_red[0], g=p["key"]), p["key"])
        return rider_join(sum_slots(p["q"], arr, f"rs_add4_{p['key']}"))

    def rs_done(p, joined):
        if isinstance(p["key"], int):
            ffn_red[0] = joined
        else:
            mix_red[p["key"]] = joined.reshape(-1, D)

    def ffn_core_bwd(dy, r, l, f):
        tag = f"l{l}f{f}"
        gi = 2 * l + f
        w = wb[gi].reshape(N_CHIP, -1, D)
        p = pending.pop() if pending else None
        gb = lax.empty((N_CHIP, 3 * ns, D), F32)
        if p is None:
            dg, du = ffn_bwd_da(lay, dy, w, 2, r["up"], r["sl"], ns, f"ffn_da_{tag}")
            (gb,) = slab_tn(lay, r["a"], dy, gb, 2, ns, f"ffn_dwd_{tag}")
            (gb,) = slab_tn(lay, dg, r["hm"], gb, 0, ns, f"ffn_dwg_{tag}")
            (gb,) = slab_tn(lay, du, r["hm"], gb, 1, ns, f"ffn_dwu_{tag}")
            (dhm,) = slab_nn_acc(lay, [dg, du], w, [0, 1], ns, f"ffn_dh_{tag}")
        elif gi == 0:
            dg, du, recv = ffn_bwd_da(lay, dy, w, 2, r["up"], r["sl"], ns, f"ffn_da_{tag}", rider=rs_sib(p))
            rs_add2(p, recv)
            gb, arr = slab_tn(lay, r["a"], dy, gb, 2, ns, f"ffn_dwd_{tag}", rider=rider_reduce_copies(p["q"]))
            gb, joined = slab_tn(lay, dg, r["hm"], gb, 0, ns, f"ffn_dwg_{tag}", rider=rs_join(p, arr))
            rs_done(p, joined)
            (gb,) = slab_tn(lay, du, r["hm"], gb, 1, ns, f"ffn_dwu_{tag}")
            own = gb.reshape(N_CHIP, 2, hh, D)
            dhm, recv = slab_nn_acc(lay, [dg, du], w, [0, 1], ns, f"ffn_dh_{tag}", rider=rider_reduce_sib(own))
            pending.append(dict(buf=own, key=gi, recv=recv))
            return dhm
        else:
            dg, du, recv = ffn_bwd_da(lay, dy, w, 2, r["up"], r["sl"], ns, f"ffn_da_{tag}", rider=rs_sib(p))
            rs_add2(p, recv)
            gb, arr = slab_tn(lay, r["a"], dy, gb, 2, ns, f"ffn_dwd_{tag}", rider=rider_reduce_copy(p["q"], 0))
            gb, arr = slab_tn(lay, dg, r["hm"], gb, 0, ns, f"ffn_dwg_{tag}", rider=rider_reduce_copy(p["q"], 1, arr))
            gb, arr = slab_tn(lay, du, r["hm"], gb, 1, ns, f"ffn_dwu_{tag}", rider=rider_reduce_copy(p["q"], 2, arr))
            dhm, joined = slab_nn_acc(lay, [dg, du], w, [0, 1], ns, f"ffn_dh_{tag}", rider=rs_join(p, arr))
            rs_done(p, joined)
        pending.append(dict(buf=gb.reshape(N_CHIP, 2, hh, D), key=gi))
        return dhm

    def mixc_core_bwd(dy, r):
        p = pending.pop() if pending else None
        w_in_t, w_out = mixc_w()
        if p is None:
            do_c = mm_nt(dy, w_out, "mixc_out_dx")
        else:
            do_c, recv = mm_nt(dy, w_out, "mixc_out_dx", rider=rs_sib(p))
            rs_add2(p, recv)
        g_out = mm_tn(r["o"], dy, "mixc_out_dw")
        dgate, dyg = lru_gate_bwd(lay, r["p"], r["s"], do_c, "lru_gate_b")
        da_c, db_c = lru_scan_bwd(lay, r["a"], r["s"], dyg, "lru_scan_b")
        res = lru_coeffs_bwd(lay, r["uc"], lru_wa[0], lru_wx[0], lru_vec, da_c, db_c, "lru_coef_b",
                             rider=None if p is None else rider_reduce_copies(p["q"]))
        duc, mixg["wa"], mixg["wx"], mixg["vec"] = res[:4]
        du_c, mixg["cw"], mixg["cb"] = conv_bwd(lay, r["p"], D, conv_w_f, duc, "conv_b")
        dp_c = jnp.concatenate([dgate, du_c], axis=1)
        if p is None:
            g_in_t = mm_tn(dp_c, r["hm"], "mixc_in_dw")
        else:
            g_in_t, joined = mm_tn(dp_c, r["hm"], "mixc_in_dw", rider=rs_join(p, res[4]))
            rs_done(p, joined)
        buf = jnp.concatenate([g_in_t.reshape(N_CHIP, n_li, D), g_out.reshape(N_CHIP, n_lo, D)], axis=1)
        pending.append(dict(buf=buf.reshape(N_CHIP, 2, (n_li + n_lo) // 2, D), key="c"))
        return mm_nn(dp_c, w_in_t, "mixc_in_dx")

    def mixa_core_bwd(dy, r):
        p = pending.pop() if pending else None
        w_ext_t, w_out_ext = mixa_w()
        if p is None:
            dcat = mm_nt(dy, w_out_ext, "mixa_out_dx")
        else:
            dcat, recv = mm_nt(dy, w_out_ext, "mixa_out_dx", rider=rs_sib(p))
            rs_add2(p, recv)
        g_out_ext = mm_tn(r["cat"], dy, "mixa_out_dw")
        res = attn_bwd(lay, r["qr"], r["kr"], r["vb"], sink_tab, r["lse"], dcat, "attn_b",
                       rider=None if p is None else rider_reduce_copies(p["q"]))
        dqr, dkr, dv, mixg["sink"] = res[:4]
        du_a, mixg["pw"], mixg["ps"] = pool_bwd(lay, r["u"], dcat, pool_w[0], pscale, "pool_b")
        dp_a = rope_bwd(lay, dqr, dkr, dv, du_a, cos_t, sin_t, "rope_b")
        if p is None:
            g_ext_t = mm_tn(dp_a, r["hm"], "mixa_in_dw")
        else:
            g_ext_t, joined = mm_tn(dp_a, r["hm"], "mixa_in_dw", rider=rs_join(p, res[4]))
            rs_done(p, joined)
        gq, gqr = _heads(g_ext_t[Q0:K0], N_HEADS), _heads(g_ext_t[QR0:KR0], N_HEADS)
        g_q = [None] * N_HEADS
        for i, h in enumerate(HEAD_PERM):
            g_q[h] = gq[i] + _unrot_rows(gqr[i])
        gk = [a + _unrot_rows(b) for a, b in zip(_heads(g_ext_t[K0:V0], N_KV), _heads(g_ext_t[KR0:PEXT], N_KV))]
        g_ab_in_t = jnp.concatenate(g_q + gk + [g_ext_t[V0:QR0]], axis=0)
        go = _heads(g_out_ext[0:ATT_W], N_HEADS)
        g_o = [None] * N_HEADS
        for i, h in enumerate(HEAD_PERM):
            g_o[h] = go[i]
        g_ab_out = jnp.concatenate(g_o + [g_out_ext[ATT_W:]], axis=0)
        buf = jnp.concatenate([g_ab_in_t.reshape(N_CHIP, n_ai, D), g_ab_out.reshape(N_CHIP, n_ao, D)], axis=1)
        pending.append(dict(buf=buf.reshape(N_CHIP, 2, (n_ai + n_ao) // 2, D), key="a"))
        return mm_nn(dp_a, w_ext_t, "mixa_in_dx")

    l, k0, coef, j = subs[5]
    dy, dres, s1 = ln_bwd(lay, dout, recs[5]["xhat"], recs[5]["rstd"], recs[5]["y"], modtab[l], k0 + 2, coef, lnv(l, j),
                          "lnb_s5")
    for k in range(5, -1, -1):
        l, k0, coef, j = subs[k]
        r = recs[k]
        if k0 == 3:
            dhm = mixa_core_bwd(dy, r) if l == 0 else mixc_core_bwd(dy, r)
        else:
            dhm = ffn_core_bwd(dy, r, l, k0 // 6)
        dln[(l, j)] = s1
        if k > 0:
            lp, k0p, coefp, jp = subs[k - 1]
            rp = recs[k - 1]
            dy, dres, s1, s2 = modb_lnb(lay, dres, dhm, modtab[l], k0 + 1, rp["xhat"], rp["rstd"], rp["y"],
                                        modtab[lp], k0p + 2, coefp, lnv(lp, jp), f"modb_lnb_s{k}")
        else:
            gx, s2 = mod_bwd(lay, dres, dhm, r["h"], modtab[l], k0 + 1, "modb_s0")
        dms[(l, k0)] = s2
    sums = block_sums(lay, list(dln.values()) + list(dms.values()), "block_sums")
    dln, dms = dict(zip(dln, sums[:len(dln)])), dict(zip(dms, sums[len(dln):]))
    grad_x = gx.reshape(2, n_lat, D)
    g_wa, g_wx, g_vec, g_cw, g_cb, g_sink, g_pw, g_ps = (mixg[n] for n in ("wa", "wx", "vec", "cw", "cb", "sink", "pw", "ps"))

    rows = []
    for l in range(DEPTH):
        per_k = []
        for k0, j in ((0, 0), (3, 1), (6, 2)):
            per_k += [dms[(l, k0)][:3, 0], dms[(l, k0)][:3, 1], dln[(l, j)][:3, 2]]
        rows.append(jnp.stack(per_k, axis=1).reshape(3, N_MOD * D))
    dmod_loc = jnp.concatenate(rows + [jnp.zeros((2, N_MOD * D), F32)], axis=0)
    dmod_all, g_b_mod = mod_grad_rows(all_gather8(dmod_loc, "ag8_dmod"), "dmod_rows")
    dcol = lax.dynamic_slice(dmod_all, (0, 0, chip * wm), (DEPTH, 32, wm))
    g_w_mod = wmod_dw(sc, dcol, "wmod_dw")
    g_cctx = cctx_grad(cctx_dx(dcol[:, 16:32], w_mod, "cctx_dx"), c_ctx[None, :], "cctx_grad")

    g_ln_g =jnp.stack([jnp.stack([dln[(l, j)][3, 1] for j in range(3)]) for l in range(DEPTH)])
    g_ln_b = jnp.stack([jnp.stack([dln[(l, j)][3, 0] for j in range(3)]) for l in range(DEPTH)])
    sink_row = jnp.sum(g_sink, axis=0)[:4]
    g_sink8 = jnp.concatenate([sink_row[:, 0], sink_row[:, HEAD_DIM]])
    misc = jnp.concatenate([g_sink8, jnp.sum(g_ps, axis=0).reshape(POOL_W), jnp.zeros((D - 8 - POOL_W,), F32)])
    small_g = jnp.concatenate([
        g_ln_g.reshape(6, D), g_ln_b.reshape(6, D), jnp.sum(g_cw, axis=0), jnp.sum(g_cb, axis=0), g_vec,
        misc[None, :], jnp.sum(g_pw, axis=0).reshape(64, D), g_wa.reshape(256, D), g_wx.reshape(256, D), g_cctx,
        jnp.zeros((39, D), F32)], axis=0)
    n_small = small_g.shape[0] // N_CHIP
    last = pending.pop()
    ffn_red = reduce_scatter_chips(last["buf"], f"ffn{last['key']}", wire=BF16, dst=ffn_red[0], g=last["key"],
                                   recv=last.get("recv")).reshape(12 * ns, D)
    small_red = reduce_scatter_chips(small_g.reshape(N_CHIP, 2, n_small // 2, D), "small")
    small_red = all_gather_chips(small_red, "ag_smallg").reshape(N_CHIP * n_small, D)

    ffn_kind = dict(ffn_w_gate=0, ffn_w_up=1, ffn_w_down=2)

    def cols(a):
        return lax.dynamic_slice_in_dim(a, chip * dsh, dsh, axis=a.ndim - 1)

    sr = small_red
    grads = dict(
        c_ctx=sr[600], w_mod=g_w_mod, b_mod=g_b_mod,
        ln_g=cols(sr[0:6]).reshape(2, 3, dsh), ln_b=cols(sr[6:12]).reshape(2, 3, dsh),
        mix_ab_w_in=mix_red["a"][0:n_ai][None], attn_sink=sr[23, 0:8][None], pool_w=sr[24:88].reshape(1, 4, 128, 128),
        pool_scale=sr[23, 8:8 + POOL_W][None], mix_ab_w_out=mix_red["a"][n_ai:][None],
        lru_w_in=mix_red["c"][0:n_li].T[None],
        lru_conv_w=cols(sr[12:16])[None], lru_conv_b=cols(sr[16:17]), lru_wa=sr[88:344].reshape(1, 2, 8, 128, 128),
        lru_ba=cols(sr[17:19])[None], lru_wx=sr[344:600].reshape(1, 2, 8, 128, 128), lru_bx=cols(sr[19:21])[None],
        lru_lambda=cols(sr[21:23])[None], lru_w_out=mix_red["c"][n_li:][None])
    params = dict(c_ctx=(c_ctx, m_c_ctx, v_c_ctx), w_mod=(w_mod, m_w_mod, v_w_mod), b_mod=(b_mod, m_b_mod, v_b_mod),
                  ln_g=(ln_g, m_ln_g, v_ln_g), ln_b=(ln_b, m_ln_b, v_ln_b),
                  ffn_w_gate=(ffn_w_gate, m_ffn_w_gate, v_ffn_w_gate), ffn_w_up=(ffn_w_up, m_ffn_w_up, v_ffn_w_up),
                  ffn_w_down=(ffn_w_down, m_ffn_w_down, v_ffn_w_down),
                  mix_ab_w_in=(mix_ab_w_in, m_mix_ab_w_in, v_mix_ab_w_in), attn_sink=(attn_sink, m_attn_sink, v_attn_sink),
                  pool_w=(pool_w, m_pool_w, v_pool_w), pool_scale=(pool_scale, m_pool_scale, v_pool_scale),
                  mix_ab_w_out=(mix_ab_w_out, m_mix_ab_w_out, v_mix_ab_w_out), lru_w_in=(lru_w_in, m_lru_w_in, v_lru_w_in),
                  lru_conv_w=(lru_conv_w, m_lru_conv_w, v_lru_conv_w), lru_conv_b=(lru_conv_b, m_lru_conv_b, v_lru_conv_b),
                  lru_wa=(lru_wa, m_lru_wa, v_lru_wa), lru_ba=(lru_ba, m_lru_ba, v_lru_ba), lru_wx=(lru_wx, m_lru_wx, v_lru_wx),
                  lru_bx=(lru_bx, m_lru_bx, v_lru_bx), lru_lambda=(lru_lambda, m_lru_lambda, v_lru_lambda),
                  lru_w_out=(lru_w_out, m_lru_w_out, v_lru_w_out))
    gl, dl, ml, vl = [], [], [], []
    transposed = ("ffn_w_gate", "ffn_w_up", "mix_ab_w_in")
    for name, (w, m, v) in params.items():
        if name in transposed:
            w, m, v = (jnp.swapaxes(t, -1, -2) for t in (w, m, v))
        if name in ffn_kind:
            g, d, mn, vn = adamw_ffn(w, m, v, ffn_red, ffn_kind[name], ns, f"adamw_{name}")
        else:
            g = grads[name].reshape(w.shape)
            d, mn, vn = adamw(w, g, m, v, f"adamw_{name}")
        if name in transposed:
            g, d, mn, vn = (jnp.swapaxes(t, -1, -2) for t in (g, d, mn, vn))
        gl.append(g)
        dl.append(d)
        ml.append(mn)
        vl.append(vn)
    return (loss, grad_x, *gl, *dl, *ml, *vl)
```

```python
import functools
import math

import jax
import jax.numpy as jnp
from jax import lax
from jax.experimental import pallas as pl
from jax.experimental.pallas import tpu as pltpu

F32, BF16 = jnp.float32, jnp.bfloat16
MESH = pl.DeviceIdType.MESH
ANY = pl.BlockSpec(memory_space=pl.ANY)
VMEM_SPEC = pl.BlockSpec(memory_space=pltpu.VMEM)

D = 1024
N_CHIP = 4
HEAD_DIM, N_HEADS, N_KV = 64, 8, 2
ATT_W, KV_W, POOL_W = 512, 128, 512
POOL_WINDOWS = (2, 4, 8, 16)
BLK = 128
ATT_SCALE = HEAD_DIM ** -0.5
ROPE_THETA = 10000.0
GRID_W = 64
LRU_C = 8.0
LN_EPS = 1e-5
NEG_INF = -1e30
DEPTH = 2
ALPHA = (2 * DEPTH) ** 0.25
N_MOD = 9
ADAM_LR, ADAM_B1, ADAM_B2, ADAM_EPS, ADAM_WD, ADAM_STEP = 0.001, 0.9, 0.999, 1e-08, 0.01, 10
VMEM_BIG = 48 * 1024 * 1024


def _cp(sem=None, vmem=None):
    kw = {}
    if sem is not None:
        kw["dimension_semantics"] = sem
    if vmem is not None:
        kw["vmem_limit_bytes"] = vmem
    return pltpu.CompilerParams(**kw)


def _sds(shape, dtype):
    return jax.ShapeDtypeStruct(tuple(shape), dtype)


def _pick(n, cands):
    for c in cands:
        if n % c == 0:
            return c
    return n


def _dot(a, b, dims):
    return lax.dot_general(a, b, (dims, ((), ())), preferred_element_type=F32)


def _nn(a, b):
    return _dot(a, b, ((1,), (0,)))


def _nt(a, b):
    return _dot(a, b, ((1,), (1,)))


def _tn(a, b):
    return _dot(a, b, ((0,), (0,)))


def _sigmoid(x):
    return 0.5 * jnp.tanh(0.5 * x) + 0.5


def _me():
    return lax.axis_index("x"), lax.axis_index("y"), lax.axis_index("c")


def _rcopy(src, dst, ssem, rsem, dev):
    return pltpu.make_async_remote_copy(src_ref=src, dst_ref=dst, send_sem=ssem, recv_sem=rsem,
                                        device_id=dev, device_id_type=MESH)


def all_gather8(x, name):
    def body(x_ref, o_ref, ssem, rsem, lsem):
        mx, my, mc = _me()
        me = 4 * mx + 2 * my + mc
        loc = pltpu.make_async_copy(x_ref, o_ref.at[me], lsem)
        loc.start()
        peers = []
        for m in range(1, 8):
            px = 1 - mx if (m >> 2) & 1 else mx
            py = 1 - my if (m >> 1) & 1 else my
            pc = 1 - mc if m & 1 else mc
            peers.append((px, py, pc))
        sends = [_rcopy(x_ref, o_ref.at[me], ssem.at[k], rsem.at[k], p) for k, p in enumerate(peers)]
        for cp in sends:
            cp.start()
        for k, (px, py, pc) in enumerate(peers):
            _rcopy(x_ref, o_ref.at[4 * px + 2 * py + pc], ssem.at[k], rsem.at[k], (px, py, pc)).wait_recv()
        for cp in sends:
            cp.wait_send()
        loc.wait()

    return pl.pallas_call(
        body, name=name, out_shape=_sds((8,) + x.shape, x.dtype),
        in_specs=[VMEM_SPEC], out_specs=VMEM_SPEC,
        scratch_shapes=[pltpu.SemaphoreType.DMA((7,)), pltpu.SemaphoreType.DMA((7,)), pltpu.SemaphoreType.DMA],
    )(x)


_ROW_BLOCKS = (512, 384, 352, 256, 224, 128)


def _idx(v):
    return jnp.reshape(v, (1,)).astype(jnp.int32)


def place_slab(shard, name):
    _, h, w = shard.shape
    th = _pick(h, _ROW_BLOCKS)

    def body(s_ref, x_ref, o_ref):
        del s_ref
        o_ref[...] = x_ref[...]

    return pl.pallas_call(
        body, name=name, out_shape=_sds((N_CHIP,) + shard.shape, shard.dtype),
        grid_spec=pltpu.PrefetchScalarGridSpec(
            num_scalar_prefetch=1, grid=(2, h // th),
            in_specs=[pl.BlockSpec((None, th, w), lambda k, r, s: (k, r, 0))],
            out_specs=pl.BlockSpec((None, None, th, w), lambda k, r, s: (s[0], k, r, 0))),
    )(_idx(2 * lax.axis_index("x") + lax.axis_index("y")), shard)


def place_rows(buf, rows, r0, name):
    e, w = rows.shape
    tb = 64

    def body(s_ref, x_ref, b_ref, o_ref):
        del s_ref, b_ref
        o_ref[...] = x_ref[...]

    return pl.pallas_call(
        body, name=name, out_shape=_sds(buf.shape, buf.dtype),
        grid_spec=pltpu.PrefetchScalarGridSpec(
            num_scalar_prefetch=1, grid=(e // tb,),
            in_specs=[pl.BlockSpec((tb, w), lambda j, s: (j, 0)), ANY],
            out_specs=pl.BlockSpec((None, tb, w), lambda j, s: (s[0], r0 // tb + j, 0))),
        input_output_aliases={2: 0},
    )(_idx(2 * lax.axis_index("x") + lax.axis_index("y")), rows, buf)


def ffn_place(w_gate_t, w_up_t, w_down, l, f, name, extra=0):
    ns = w_down.shape[-2]
    tr, nb = ns // 2, 2

    def body(s_ref, g_ref, u_ref, d_ref, o_ref):
        del s_ref
        k = pl.program_id(0)

        @pl.when(k == 0)
        def _():
            o_ref[...] = g_ref[...].astype(BF16)

        @pl.when(k == 1)
        def _():
            o_ref[...] = u_ref[...].astype(BF16)

        @pl.when(k == 2)
        def _():
            o_ref[...] = d_ref[...].astype(BF16)

    def spec(q):
        return pl.BlockSpec((None, None, tr, D), lambda k, j, s: (l, f, jnp.where(k == q, j, 0), 0))

    return pl.pallas_call(
        body, name=name, out_shape=_sds((N_CHIP, 3 * ns + extra, D), BF16),
        grid_spec=pltpu.PrefetchScalarGridSpec(
            num_scalar_prefetch=1, grid=(3, nb), in_specs=[spec(0), spec(1), spec(2)],
            out_specs=pl.BlockSpec((None, tr, D), lambda k, j, s: (s[0], k * nb + j, 0))),
    )(_idx(2 * lax.axis_index("x") + lax.axis_index("y")), w_gate_t, w_up_t, w_down)


def all_gather_chips(shard, name):
    return gather_placed(place_slab(shard, name + "_place"), name)


def gather_placed(full, name):
    h = full.shape[2]
    lo, hi = pl.ds(0, h // 2), pl.ds(h // 2, h - h // 2)

    def body(x_ref, o_ref, ssem, rsem):
        del x_ref
        mx, my, mc = _me()
        s, xs, ys, ds = 2 * mx + my, 2 * (1 - mx) + my, 2 * mx + (1 - my), 2 * (1 - mx) + (1 - my)
        xn, yn, sib = (1 - mx, my, mc), (mx, 1 - my, mc), (mx, my, 1 - mc)

        def cp(k, src, dst, dev):
            return _rcopy(src, dst, ssem.at[k], rsem.at[k], dev)

        own = o_ref.at[s, mc]
        sent = [cp(0, own, own, xn), cp(1, own, own, yn)]
        for c in sent:
            c.start()
        cp(0, own, o_ref.at[xs, mc], xn).wait_recv()
        sent += [cp(2, o_ref.at[xs, mc, lo], o_ref.at[xs, mc, lo], yn), cp(4, o_ref.at[xs, mc], o_ref.at[xs, mc], sib)]
        sent[-2].start()
        sent[-1].start()
        cp(1, own, o_ref.at[ys, mc], yn).wait_recv()
        sent += [cp(3, o_ref.at[ys, mc, hi], o_ref.at[ys, mc, hi], xn), cp(5, o_ref.at[ys, mc], o_ref.at[ys, mc], sib)]
        sent[-2].start()
        sent[-1].start()
        cp(2, own, o_ref.at[ds, mc, lo], yn).wait_recv()
        cp(3, own, o_ref.at[ds, mc, hi], xn).wait_recv()
        sent.append(cp(6, o_ref.at[ds, mc], o_ref.at[ds, mc], sib))
        sent[-1].start()
        for k, slot in ((4, xs), (5, ys), (6, ds)):
            cp(k, own, o_ref.at[slot, 1 - mc], sib).wait_recv()
        for c in sent:
            c.wait_send()

    return pl.pallas_call(
        body, name=name, out_shape=_sds(full.shape, full.dtype), in_specs=[ANY], out_specs=ANY,
        input_output_aliases={0: 0},
        scratch_shapes=[pltpu.SemaphoreType.DMA((7,)), pltpu.SemaphoreType.DMA((7,))],
    )(full)


def sibling_send_other_half(buf, name):
    def body(x_ref, o_ref, ssem, rsem):
        mx, my, mc = _me()
        sib = (mx, my, 1 - mc)
        cps = [_rcopy(x_ref.at[k, 1 - mc], o_ref.at[k], ssem.at[k], rsem.at[k], sib) for k in range(N_CHIP)]
        for cp in cps:
            cp.start()
        for cp in cps:
            cp.wait_recv()
        for cp in cps:
            cp.wait_send()

    n, _, h, w = buf.shape
    return pl.pallas_call(
        body, name=name, out_shape=_sds((n, h, w), buf.dtype), in_specs=[ANY], out_specs=ANY,
        scratch_shapes=[pltpu.SemaphoreType.DMA((N_CHIP,)), pltpu.SemaphoreType.DMA((N_CHIP,))],
    )(buf)


def chips_all_to_all(q, name):
    def body(x_ref, o_ref, ssem, rsem):
        mx, my, mc = _me()
        s = 2 * mx + my
        chips = [(1 - mx, my), (mx, 1 - my), (1 - mx, 1 - my)]
        cps = [_rcopy(x_ref.at[2 * px + py], o_ref.at[s], ssem.at[j], rsem.at[j], (px, py, mc))
               for j, (px, py) in enumerate(chips)]
        for cp in cps:
            cp.start()
        for j, (px, py) in enumerate(chips):
            ps = 2 * px + py
            _rcopy(x_ref.at[ps], o_ref.at[ps], ssem.at[j], rsem.at[j], (px, py, mc)).wait_recv()
        for cp in cps:
            cp.wait_send()

    return pl.pallas_call(
        body, name=name, out_shape=_sds(q.shape, q.dtype), in_specs=[ANY], out_specs=ANY,
        scratch_shapes=[pltpu.SemaphoreType.DMA((3,)), pltpu.SemaphoreType.DMA((3,))],
    )(q)


def sibling_join_halves(both, name, g=None):
    def body(x_ref, o_ref, ssem, rsem):
        del x_ref
        mx, my, mc = _me()
        sib = (mx, my, 1 - mc)
        o = o_ref if g is None else o_ref.at[g]
        cp = _rcopy(o.at[mc], o.at[mc], ssem, rsem, sib)
        cp.start()
        _rcopy(o.at[1 - mc], o.at[1 - mc], ssem, rsem, sib).wait_recv()
        cp.wait_send()

    return pl.pallas_call(
        body, name=name, out_shape=_sds(both.shape, both.dtype), in_specs=[ANY], out_specs=ANY,
        input_output_aliases={0: 0}, scratch_shapes=[pltpu.SemaphoreType.DMA, pltpu.SemaphoreType.DMA],
    )(both)


def add_own_half(buf, recv, wire, name):
    n, _, h, w = buf.shape
    th = _pick(h, _ROW_BLOCKS)

    def body(c_ref, a_ref, b_ref, o_ref):
        del c_ref
        o_ref[...] = (a_ref[...] + b_ref[...]).astype(o_ref.dtype)

    return pl.pallas_call(
        body, name=name, out_shape=_sds((n, h, w), wire),
        grid_spec=pltpu.PrefetchScalarGridSpec(
            num_scalar_prefetch=1, grid=(n, h // th),
            in_specs=[pl.BlockSpec((None, None, th, w), lambda k, r, c: (k, c[0], r, 0)),
                      pl.BlockSpec((None, th, w), lambda k, r, c: (k, r, 0))],
            out_specs=pl.BlockSpec((None, th, w), lambda k, r, c: (k, r, 0))),
    )(_idx(lax.axis_index("c")), buf, recv)


def sum_slots(q, r, name, dst=None, g=None):
    n, h, w = r.shape
    th = _pick(h, _ROW_BLOCKS)

    def body(i_ref, q_ref, r1, r2, r3, *rest):
        del i_ref
        rest[-1][...] = ((q_ref[...].astype(F32) + r1[...].astype(F32)) + r2[...].astype(F32)) + r3[...].astype(F32)

    def slot(d):
        return lambda i, ix: ((ix[0] + d) % N_CHIP, i, 0)

    idx = jnp.stack([2 * lax.axis_index("x") + lax.axis_index("y"), lax.axis_index("c")]).astype(jnp.int32)
    in_specs = [pl.BlockSpec((None, th, w), slot(d)) for d in (0, 1, 2, 3)]
    if dst is None:
        return pl.pallas_call(
            body, name=name, out_shape=_sds((2, h, w), F32),
            grid_spec=pltpu.PrefetchScalarGridSpec(
                num_scalar_prefetch=1, grid=(h // th,), in_specs=in_specs,
                out_specs=pl.BlockSpec((None, th, w), lambda i, ix: (ix[1], i, 0))),
        )(idx, q, r, r, r)
    return pl.pallas_call(
        body, name=name, out_shape=_sds(dst.shape, F32),
        grid_spec=pltpu.PrefetchScalarGridSpec(
            num_scalar_prefetch=1, grid=(h // th,), in_specs=in_specs + [ANY],
            out_specs=pl.BlockSpec((None, None, th, w), lambda i, ix: (g, ix[1], i, 0))),
        input_output_aliases={5: 0},
    )(idx, q, r, r, r, dst)


def reduce_scatter_chips(buf, tag, wire=F32, dst=None, g=None, recv=None):
    if recv is None:
        recv = sibling_send_other_half(buf, f"rs_sib_{tag}")
    q = add_own_half(buf, recv, wire, f"rs_add2_{tag}")
    r = chips_all_to_all(q, f"rs_a2a_{tag}")
    red = sum_slots(q, r, f"rs_add4_{tag}", dst=dst, g=g)
    return sibling_join_halves(red, f"rs_join_{tag}", g=g)


class Layout:
    def __init__(self, n_ctx, n_lat):
        self.C, self.L = n_ctx, n_lat
        self.PS = n_ctx + n_lat
        self.T = 2 * self.PS
        self.tr = _pick(math.gcd(n_ctx, n_lat), (256, 128))
        self.bps = self.PS // self.tr
        self.cb = n_ctx // self.tr
        self.nblk = self.T // self.tr
        self.tm = _pick(self.T, (1152, 768, 512, 256, 128))
        self.tm2 = _pick(self.T, (2304, 1152, 768, 512, 256, 128))
        self.tc = _pick(self.T, (512, 256, 128))

    def seg(self, i):
        return jnp.where(i % self.bps < self.cb, 2, i // self.bps)


def rowwise(lay, name, fn, rows, segs=(), vecs=(), outs=(), sums=(), rider=None):
    tr, nblk = lay.tr, lay.nblk
    n_r, n_s, n_v, n_o = len(rows), len(segs), len(vecs), len(outs)
    lat_only = any(o[2:] for o in outs) or any(a.shape[0] != lay.T for a in rows)
    nsub = 1 if lat_only else (3 if nblk % 3 == 0 else 2 if nblk % 2 == 0 else 1)
    tb = tr * nsub

    def body(*refs):
        ins = refs[:n_r + n_s + n_v]
        ors = refs[n_r + n_s + n_v:]
        for sub in range(nsub):
            rs = slice(sub * tr, (sub + 1) * tr)
            seg = lay.seg(pl.program_id(0) * nsub + sub)
            vals = [r[rs, :] for r in ins[:n_r]] + [r[seg] for r in ins[n_r:n_r + n_s]] + [r[...] for r in ins[n_r + n_s:]]
            res = fn(*vals)
            for k in range(n_o):
                ors[k][rs, :] = res[k].astype(ors[k].dtype)
            for k in range(len(sums)):
                ors[n_o + k][sub] = res[n_o + k]

    def all_rows(i):
        return (i, 0)

    def lat_rows(i):
        return ((i // lay.bps) * (lay.bps - lay.cb) + jnp.maximum(i % lay.bps - lay.cb, 0), 0)

    in_specs = [pl.BlockSpec((tb, a.shape[1]), all_rows if a.shape[0] == lay.T else lat_rows) for a in rows]
    in_specs += [pl.BlockSpec(a.shape, lambda i: (0, 0, 0)) for a in segs]
    in_specs += [pl.BlockSpec(a.shape, lambda i: (0, 0)) for a in vecs]
    out_shape = [_sds((2 * lay.L if o[2:] else lay.T, o[0]), o[1]) for o in outs]
    out_shape += [_sds((nblk, r, w), F32) for r, w in sums]
    out_specs = [pl.BlockSpec((tb, o[0]), lat_rows if o[2:] else all_rows) for o in outs]
    out_specs += [pl.BlockSpec((nsub, r, w), lambda i: (i, 0, 0)) for r, w in sums]
    sem = "arbitrary" if any(o[2:] for o in outs) else "parallel"
    if rider is None:
        return pl.pallas_call(body, name=name, out_shape=out_shape, grid=(nblk // nsub,), in_specs=in_specs,
                              out_specs=out_specs, compiler_params=_cp((sem,), VMEM_BIG))(*rows, *segs, *vecs)
    return _host_call(body, rider, name, (nblk // nsub,), in_specs, out_specs, out_shape, (*rows, *segs, *vecs), (sem,),
                      n_r + n_s + n_v, n_o + len(sums))


def modulate(lay, h, mod, k_shift, k_scale, name):
    def fn(hb, m):
        return (hb * (1.0 + m[k_scale:k_scale + 1]) + m[k_shift:k_shift + 1],)
    return rowwise(lay, name, fn, [h], segs=[mod], outs=[(D, BF16)])[0]


def resid_ln(lay, h, y, mod, k_gate, coef, lnv, name, nxt=None, prev_ln=None, rider=None):
    def fn(hb, yb, m, *rest):
        ln = rest[-1]
        if prev_ln is not None:
            hb = hb * rest[-2][0:1] + rest[-2][1:2]
        z = ALPHA * hb + (coef * m[k_gate:k_gate + 1]) * yb
        mu = jnp.mean(z, axis=-1, keepdims=True)
        zc = z - mu
        var = jnp.mean(zc * zc, axis=-1, keepdims=True)
        rstd = lax.rsqrt(var + LN_EPS)
        xhat = zc * rstd
        out = xhat * ln[0:1] + ln[1:2]
        if nxt is None:
            return xhat, rstd
        mn = rest[0]
        return xhat, rstd, out * (1.0 + mn[nxt[2]:nxt[2] + 1]) + mn[nxt[1]:nxt[1] + 1]
    segs = [mod] if nxt is None else [mod, nxt[0]]
    vecs = [lnv] if prev_ln is None else [prev_ln, lnv]
    outs = [(D, F32), (1, F32)] + ([] if nxt is None else [(D, BF16)])
    return rowwise(lay, name, fn, [h, y], segs=segs, vecs=vecs, outs=outs, rider=rider)


def _ln_bwd_math(do, xh, rs, yb, gate, coef, ln):
    dxh = do * ln[0:1]
    m1 = jnp.mean(dxh, axis=-1, keepdims=True)
    m2 = jnp.mean(dxh * xh, axis=-1, keepdims=True)
    dz = rs * (dxh - m1 - xh * m2)
    s = jnp.concatenate([jnp.sum(do, axis=0, keepdims=True), jnp.sum(do * xh, axis=0, keepdims=True),
                         jnp.sum(coef * dz * yb, axis=0, keepdims=True)], axis=0)
    return (coef * gate) * dz, ALPHA * dz, s


def _mod_bwd_math(dr, dm, hb, scale):
    s = jnp.concatenate([jnp.sum(dm, axis=0, keepdims=True), jnp.sum(dm * hb, axis=0, keepdims=True)], axis=0)
    return dr + dm * (1.0 + scale), s


def mod_bwd(lay, dres, dhm, h, mod, k_scale, name, rider=None):
    def fn(dr, dm, hb, m):
        return _mod_bwd_math(dr, dm, hb, m[k_scale:k_scale + 1])
    return rowwise(lay, name, fn, [dres, dhm, h], segs=[mod], outs=[(D, F32, "lat")], sums=[(2, D)], rider=rider)


def modb_lnb(lay, dres, dhm, mod, k_scale, xhat, rstd, y, mod_p, k_gate, coef, lnv, name, rider=None):
    def fn(dr, dm, xh, rs, yb, m, mp, ln):
        dh, s2 = _mod_bwd_math(dr, dm, xh * ln[0:1] + ln[1:2], m[k_scale:k_scale + 1])
        dy, dres_p, s1 = _ln_bwd_math(dh, xh, rs, yb, mp[k_gate:k_gate + 1], coef, ln)
        return dy, dres_p, s1, s2
    return rowwise(lay, name, fn, [dres, dhm, xhat, rstd, y], segs=[mod, mod_p], vecs=[lnv],
                   outs=[(D, BF16), (D, F32)], sums=[(3, D), (2, D)], rider=rider)


def block_sums(lay, parts_list, name):
    n = len(parts_list)

    def body(*refs):
        for p_ref, o_ref in zip(refs[:n], refs[n:]):
            acc = [None, None, None]
            for i in range(lay.nblk):
                sg = 2 if i % lay.bps < lay.cb else i // lay.bps
                acc[sg] = p_ref[i] if acc[sg] is None else acc[sg] + p_ref[i]
            for k in range(3):
                o_ref[k] = acc[k]
            o_ref[3] = (acc[0] + acc[1]) + acc[2]

    return pl.pallas_call(body, name=name, out_shape=[_sds((4,) + p.shape[1:], F32) for p in parts_list],
                          in_specs=[VMEM_SPEC] * n, out_specs=[VMEM_SPEC] * n)(*parts_list)


def mm_nn(a, b, name, out_dtype=F32):
    m, k = a.shape
    n = b.shape[1]
    tm = _pick(m, (1152, 768, 512, 256, 128, 64, 32, 16, 8))
    tn = _pick(n, (1024, 768, 640, 512, 384, 256, 128))

    def body(a_ref, b_ref, o_ref):
        o_ref[...] = _nn(a_ref[...].astype(BF16), b_ref[...].astype(BF16)).astype(o_ref.dtype)

    return pl.pallas_call(body, name=name, out_shape=_sds((m, n), out_dtype), grid=(m // tm, n // tn),
                          in_specs=[pl.BlockSpec((tm, k), lambda i, j: (i, 0)), pl.BlockSpec((k, tn), lambda i, j: (0, j))],
                          out_specs=pl.BlockSpec((tm, tn), lambda i, j: (i, j)),
                          compiler_params=_cp(("parallel", "parallel"), VMEM_BIG))(a, b)


def mm_nt(a, b, name, out_dtype=F32, rider=None):
    m, k = a.shape
    n = b.shape[0]
    tm = _pick(m, (1152, 768, 512, 256, 128, 64, 32, 16, 8))
    tn = _pick(n, (1024, 768, 640, 512, 384, 256, 128))

    def body(a_ref, b_ref, o_ref):
        o_ref[...] = _nt(a_ref[...].astype(BF16), b_ref[...].astype(BF16)).astype(o_ref.dtype)

    res = _host_call(body, rider, name, (m // tm, n // tn),
                     [pl.BlockSpec((tm, k), lambda i, j: (i, 0)), pl.BlockSpec((tn, k), lambda i, j: (j, 0))],
                     [pl.BlockSpec((tm, tn), lambda i, j: (i, j))], [_sds((m, n), out_dtype)], (a, b),
                     ("parallel", "parallel"), 2, 1)
    return res[0] if rider is None else res


def mm_tn(a, b, name, rider=None):
    t, m = a.shape
    n = b.shape[1]
    tk = _pick(t, (1152, 768, 512, 256, 128, 64, 32, 16))
    tm = _pick(m, (512, 384, 256, 128))

    def body(a_ref, b_ref, o_ref):
        @pl.when(pl.program_id(1) == 0)
        def _():
            o_ref[...] = jnp.zeros_like(o_ref)
        o_ref[...] += _tn(a_ref[...].astype(BF16), b_ref[...].astype(BF16))

    res = _host_call(body, rider, name, (m // tm, t // tk),
                     [pl.BlockSpec((tk, tm), lambda i, k: (k, i)), pl.BlockSpec((tk, n), lambda i, k: (k, 0))],
                     [pl.BlockSpec((tm, n), lambda i, k: (i, 0))], [_sds((m, n), F32)], (a, b),
                     ("parallel", "arbitrary"), 2, 1)
    return res[0] if rider is None else res


class Rider:
    def __init__(self, ins, outs, aliases, nsem, start, wait):
        self.ins, self.outs, self.aliases, self.nsem, self.start, self.wait = ins, outs, aliases, nsem, start, wait


def _chips_of(mx, my):
    return [(1 - mx, my), (mx, 1 - my), (1 - mx, 1 - my)]


def rider_gather_d2d(buf):
    def start(ins, outs, ssem, rsem):
        o = outs[0]
        mx, my, mc = _me()
        for j, (px, py) in enumerate(_chips_of(mx, my)):
            ps = 2 * px + py
            _rcopy(o.at[ps, mc], o.at[ps, mc], ssem.at[j], rsem.at[j], (mx, my, 1 - mc)).start()

    def wait(ins, outs, ssem, rsem):
        o = outs[0]
        mx, my, mc = _me()
        sib = (mx, my, 1 - mc)
        for j, (px, py) in enumerate(_chips_of(mx, my)):
            ps = 2 * px + py
            _rcopy(o.at[ps, 1 - mc], o.at[ps, 1 - mc], ssem.at[j], rsem.at[j], sib).wait_recv()
        for j, (px, py) in enumerate(_chips_of(mx, my)):
            ps = 2 * px + py
            _rcopy(o.at[ps, mc], o.at[ps, mc], ssem.at[j], rsem.at[j], sib).wait_send()

    return Rider([buf], [_sds(buf.shape, buf.dtype)], {0: 0}, 3, start, wait)


def rider_reduce_sib(buf):
    n, _, h, w = buf.shape

    def start(ins, outs, ssem, rsem):
        mx, my, mc = _me()
        for k in range(N_CHIP):
            _rcopy(ins[0].at[k, 1 - mc], outs[0].at[k], ssem.at[k], rsem.at[k], (mx, my, 1 - mc)).start()

    def wait(ins, outs, ssem, rsem):
        mx, my, mc = _me()
        for k in range(N_CHIP):
            _rcopy(ins[0].at[k, 1 - mc], outs[0].at[k], ssem.at[k], rsem.at[k], (mx, my, 1 - mc)).wait_recv()
        for k in range(N_CHIP):
            _rcopy(ins[0].at[k, 1 - mc], outs[0].at[k], ssem.at[k], rsem.at[k], (mx, my, 1 - mc)).wait_send()

    return Rider([buf], [_sds((n, h, w), buf.dtype)], {}, N_CHIP, start, wait)


def rider_gather_xy(buf):
    def peers():
        mx, my, mc = _me()
        return 2 * mx + my, mc, [(1 - mx, my), (mx, 1 - my)]

    def start(ins, outs, ssem, rsem):
        o = outs[0]
        s, mc, nb = peers()
        for j, (px, py) in enumerate(nb):
            _rcopy(o.at[s, mc], o.at[s, mc], ssem.at[j], rsem.at[j], (px, py, mc)).start()

    def wait(ins, outs, ssem, rsem):
        o = outs[0]
        s, mc, nb = peers()
        for j, (px, py) in enumerate(nb):
            _rcopy(o.at[2 * px + py, mc], o.at[2 * px + py, mc], ssem.at[j], rsem.at[j], (px, py, mc)).wait_recv()
        for j, (px, py) in enumerate(nb):
            _rcopy(o.at[s, mc], o.at[s, mc], ssem.at[j], rsem.at[j], (px, py, mc)).wait_send()

    return Rider([buf], [_sds(buf.shape, buf.dtype)], {0: 0}, 2, start, wait)


def rider_gather_fwd(buf):
    h2 = buf.shape[2] // 2
    lo, hi = pl.ds(0, h2), pl.ds(h2, buf.shape[2] - h2)

    def start(ins, outs, ssem, rsem):
        o = outs[0]
        mx, my, mc = _me()
        xs, ys = 2 * (1 - mx) + my, 2 * mx + (1 - my)
        _rcopy(o.at[xs, mc, lo], o.at[xs, mc, lo], ssem.at[0], rsem.at[0], (mx, 1 - my, mc)).start()
        _rcopy(o.at[ys, mc, hi], o.at[ys, mc, hi], ssem.at[1], rsem.at[1], (1 - mx, my, mc)).start()

    def wait(ins, outs, ssem, rsem):
        o = outs[0]
        mx, my, mc = _me()
        xs, ys, ds = 2 * (1 - mx) + my, 2 * mx + (1 - my), 2 * (1 - mx) + (1 - my)
        _rcopy(o.at[ds, mc, lo], o.at[ds, mc, lo], ssem.at[0], rsem.at[0], (mx, 1 - my, mc)).wait_recv()
        _rcopy(o.at[ds, mc, hi], o.at[ds, mc, hi], ssem.at[1], rsem.at[1], (1 - mx, my, mc)).wait_recv()
        _rcopy(o.at[xs, mc, lo], o.at[xs, mc, lo], ssem.at[0], rsem.at[0], (mx, 1 - my, mc)).wait_send()
        _rcopy(o.at[ys, mc, hi], o.at[ys, mc, hi], ssem.at[1], rsem.at[1], (1 - mx, my, mc)).wait_send()

    return Rider([buf], [_sds(buf.shape, buf.dtype)], {0: 0}, 2, start, wait)


def rider_reduce_copy(q, j, r=None):
    def peer():
        mx, my, mc = _me()
        px, py = _chips_of(mx, my)[j]
        return 2 * mx + my, 2 * px + py, (px, py, mc)

    def start(ins, outs, ssem, rsem):
        s, ps, dev = peer()
        _rcopy(ins[0].at[ps], outs[0].at[s], ssem.at[0], rsem.at[0], dev).start()

    def wait(ins, outs, ssem, rsem):
        s, ps, dev = peer()
        _rcopy(ins[0].at[ps], outs[0].at[ps], ssem.at[0], rsem.at[0], dev).wait_recv()
        _rcopy(ins[0].at[ps], outs[0].at[s], ssem.at[0], rsem.at[0], dev).wait_send()

    if r is None:
        return Rider([q], [_sds(q.shape, q.dtype)], {}, 1, start, wait)
    return Rider([q, r], [_sds(q.shape, q.dtype)], {1: 0}, 1, start, wait)


def rider_reduce_copies(q):
    def start(ins, outs, ssem, rsem):
        mx, my, mc = _me()
        s = 2 * mx + my
        for j, (px, py) in enumerate(_chips_of(mx, my)):
            _rcopy(ins[0].at[2 * px + py], outs[0].at[s], ssem.at[j], rsem.at[j], (px, py, mc)).start()

    def wait(ins, outs, ssem, rsem):
        mx, my, mc = _me()
        s = 2 * mx + my
        for j, (px, py) in enumerate(_chips_of(mx, my)):
            ps = 2 * px + py
            _rcopy(ins[0].at[ps], outs[0].at[ps], ssem.at[j], rsem.at[j], (px, py, mc)).wait_recv()
        for j, (px, py) in enumerate(_chips_of(mx, my)):
            _rcopy(ins[0].at[2 * px + py], outs[0].at[s], ssem.at[j], rsem.at[j], (px, py, mc)).wait_send()

    return Rider([q], [_sds(q.shape, q.dtype)], {}, 3, start, wait)


def rider_join(buf, g=None):
    def start(ins, outs, ssem, rsem):
        o = outs[0] if g is None else outs[0].at[g]
        mx, my, mc = _me()
        _rcopy(o.at[mc], o.at[mc], ssem.at[0], rsem.at[0], (mx, my, 1 - mc)).start()

    def wait(ins, outs, ssem, rsem):
        o = outs[0] if g is None else outs[0].at[g]
        mx, my, mc = _me()
        _rcopy(o.at[1 - mc], o.at[1 - mc], ssem.at[0], rsem.at[0], (mx, my, 1 - mc)).wait_recv()
        _rcopy(o.at[mc], o.at[mc], ssem.at[0], rsem.at[0], (mx, my, 1 - mc)).wait_send()

    return Rider([buf], [_sds(buf.shape, buf.dtype)], {0: 0}, 1, start, wait)


def _host_call(body, rider, name, grid, in_specs, out_specs, out_shape, operands, sem, n_in, n_out, aliases=None):
    aliases = dict(aliases or {})
    if rider is None:
        return pl.pallas_call(body, name=name, out_shape=out_shape, grid=grid, in_specs=in_specs, out_specs=out_specs,
                              input_output_aliases=aliases, compiler_params=_cp(sem, VMEM_BIG))(*operands)
    n_ri, n_ro = len(rider.ins), len(rider.outs)
    aliases.update({n_in + a: n_out + b for a, b in rider.aliases.items()})

    def hosted(*refs):
        ins, r_in = refs[:n_in], refs[n_in:n_in + n_ri]
        outs, r_out = refs[n_in + n_ri:n_in + n_ri + n_out], refs[n_in + n_ri + n_out:n_in + n_ri + n_out + n_ro]
        ssem, rsem = refs[-2], refs[-1]
        first = functools.reduce(lambda a, b: a & b, [pl.program_id(k) == 0 for k in range(len(grid))])
        last = functools.reduce(lambda a, b: a & b, [pl.program_id(k) == grid[k] - 1 for k in range(len(grid))])

        @pl.when(first)
        def _():
            rider.start(r_in, r_out, ssem, rsem)
        body(*ins, *outs)

        @pl.when(last)
        def _():
            rider.wait(r_in, r_out, ssem, rsem)

    return pl.pallas_call(
        hosted, name=name, out_shape=list(out_shape) + list(rider.outs), grid=grid,
        in_specs=list(in_specs) + [ANY] * n_ri, out_specs=list(out_specs) + [ANY] * n_ro,
        input_output_aliases=aliases,
        scratch_shapes=[pltpu.SemaphoreType.DMA((rider.nsem,)), pltpu.SemaphoreType.DMA((rider.nsem,))],
        compiler_params=_cp(("arbitrary",) * len(grid), VMEM_BIG))(*operands, *rider.ins)


def ffn_up(lay, hm, wbuf, ig, iu, ns, name, rider=None):
    tm = lay.tm

    def body(h_ref, wg_ref, wu_ref, up_ref, sl_ref, a_ref):
        hb = h_ref[...]
        g = _nt(hb, wg_ref[0])
        u = _nt(hb, wu_ref[0])
        sg = _sigmoid(g)
        sl = g * sg
        up_ref[0] = (u * (sg + sl * (1.0 - sg))).astype(BF16)
        sl_ref[0] = sl.astype(BF16)
        a_ref[0] = (sl * u).astype(BF16)

    spec_o = pl.BlockSpec((1, tm, ns), lambda s, i: (s, i, 0))
    return _host_call(
        body, rider, name, (N_CHIP, lay.T // tm),
        [pl.BlockSpec((tm, D), lambda s, i: (i, 0)), pl.BlockSpec((1, ns, D), lambda s, i: (s, ig, 0)),
         pl.BlockSpec((1, ns, D), lambda s, i: (s, iu, 0))],
        [spec_o] * 3, [_sds((N_CHIP, lay.T, ns), BF16)] * 3, (hm, wbuf, wbuf), ("parallel", "parallel"), 3, 3)


def slab_nn_acc(lay, zs, wbuf, idxs, ns, name, rider=None):
    tm = lay.tm2
    npair = len(zs)

    def body(*refs):
        o_ref = refs[-1]

        @pl.when(pl.program_id(1) == 0)
        def _():
            o_ref[...] = jnp.zeros_like(o_ref)
        acc = _nn(refs[0][0], refs[npair][0])
        for p in range(1, npair):
            acc += _nn(refs[p][0], refs[npair + p][0])
        o_ref[...] += acc

    in_specs = [pl.BlockSpec((1, tm, ns), lambda i, s: (s, i, 0)) for _ in zs]
    in_specs += [pl.BlockSpec((1, ns, D), functools.partial(lambda i, s, q: (s, q, 0), q=q)) for q in idxs]
    return _host_call(body, rider, name, (lay.T // tm, N_CHIP), in_specs, [pl.BlockSpec((tm, D), lambda i, s: (i, 0))],
                      [_sds((lay.T, D), F32)], (*zs, *([wbuf] * npair)), ("parallel", "arbitrary"), 2 * npair, 1)


def ffn_bwd_da(lay, dy, wbuf, idn, up, sl, ns, name, rider=None):
    tm = lay.tm

    def body(dy_ref, wd_ref, up_ref, sl_ref, dg_ref, du_ref):
        da = _nt(dy_ref[...], wd_ref[0])
        dg_ref[0] = (da * up_ref[0].astype(F32)).astype(BF16)
        du_ref[0] = (da * sl_ref[0].astype(F32)).astype(BF16)

    spec_z = pl.BlockSpec((1, tm, ns), lambda s, i: (s, i, 0))
    return _host_call(
        body, rider, name, (N_CHIP, lay.T // tm),
        [pl.BlockSpec((tm, D), lambda s, i: (i, 0)), pl.BlockSpec((1, ns, D), lambda s, i: (s, idn, 0)), spec_z, spec_z],
        [spec_z] * 2, [_sds((N_CHIP, lay.T, ns), BF16)] * 2, (dy, wbuf, up, sl), ("parallel", "parallel"), 4, 2)


def slab_tn(lay, z, x, gbuf, idx, ns, name, rider=None):
    tk = _pick(lay.T, (768, 512, 256, 128))

    def body(z_ref, x_ref, g_in, o_ref):
        del g_in

        @pl.when(pl.program_id(0) == 0)
        def _():
            o_ref[...] = jnp.zeros_like(o_ref)
        xv = x_ref[...]
        for s in range(N_CHIP):
            o_ref[s] += _tn(z_ref[s], xv)

    return _host_call(
        body, rider, name, (lay.T // tk,),
        [pl.BlockSpec((N_CHIP, tk, ns), lambda k: (0, k, 0)), pl.BlockSpec((tk, D), lambda k: (k, 0)), ANY],
        [pl.BlockSpec((N_CHIP, ns, D), lambda k: (0, idx, 0))], [_sds(gbuf.shape, F32)], (z, x, gbuf),
        ("arbitrary",), 3, 1, aliases={2: 0})


Q0, K0, V0, U0, QR0, KR0, PEXT = 0, 512, 640, 768, 1280, 1792, 1920


def rope_fwd(lay, p, cos, sin, name):
    def fn(pb, cs, sn):
        cs4 = jnp.concatenate([cs] * 4, axis=1)
        sn4 = jnp.concatenate([sn] * 4, axis=1)
        qr = pb[:, Q0:K0] * cs4 + pb[:, QR0:KR0] * sn4
        kr = pb[:, K0:V0] * cs + pb[:, KR0:PEXT] * sn
        return qr, kr, pb[:, V0:U0], pb[:, U0:QR0]
    return rowwise(lay, name, fn, [p, cos, sin], outs=[(ATT_W, BF16), (KV_W, BF16), (KV_W, BF16), (POOL_W, F32)])


def rope_bwd(lay, dqr, dkr, dv, du, cos, sin, name):
    def fn(dq, dk, dvb, dub, cs, sn):
        cs4 = jnp.concatenate([cs] * 4, axis=1)
        sn4 = jnp.concatenate([sn] * 4, axis=1)
        return (jnp.concatenate([dq * cs4, dk * cs, dvb, dub, dq * sn4, dk * sn], axis=1),)
    return rowwise(lay, name, fn, [dqr, dkr, dv, du, cos, sin], outs=[(PEXT, BF16)])[0]


def _attn_specs(lay):
    nbs, cbk, lbk = lay.PS // BLK, lay.C // BLK, lay.L // BLK

    def kv_map(j):
        return lambda s, n: (s * nbs + cbk + jnp.clip(n - cbk + j - 1, 0, lbk - 1), 0)

    win = [pl.BlockSpec((BLK, KV_W), kv_map(j)) for j in range(3)]
    ctx = pl.BlockSpec((lay.C, KV_W), lambda s, n: (s * (lay.PS // lay.C), 0))
    return nbs, cbk, lbk, win, ctx


def _attn_masks(n, cbk, lbk):
    row = lax.broadcasted_iota(jnp.int32, (BLK, BLK), 0)
    col = lax.broadcasted_iota(jnp.int32, (BLK, BLK), 1)
    m = n - cbk
    lat = n >= cbk
    valid = [lat & (m >= 1) & (col >= row), lat & (col >= 0), lat & (m <= lbk - 2) & (col <= row)]
    lane_lo = lax.broadcasted_iota(jnp.int32, (BLK, 2 * HEAD_DIM), 1) < HEAD_DIM
    return valid, lane_lo


def attn_fwd(lay, qr, kr, vb, sink_tab, name):
    nbs, cbk, lbk, win, ctx = _attn_specs(lay)

    def body(q_ref, k0, k1, k2, kc_ref, v0, v1, v2, vc_ref, sk_ref, o_ref, l_ref):
        n = pl.program_id(1)
        valid, lane_lo = _attn_masks(n, cbk, lbk)
        valid4 = [jnp.concatenate([v] * 4, axis=0) for v in valid]
        ks = [k0[...], k1[...], k2[...]]
        vs = [v0[...], v1[...], v2[...]]
        kc, vc = kc_ref[...], vc_ref[...]
        q2s = [q_ref[:, p * 128:(p + 1) * 128] for p in range(4)]
        outs, lses = [], []
        for hh in range(2):
            sel = lane_lo == (hh == 0)
            qm = jnp.concatenate([jnp.where(sel, q2, jnp.zeros_like(q2)) for q2 in q2s], axis=0)
            sk = jnp.concatenate([jnp.broadcast_to(sk_ref[p:p + 1, hh * HEAD_DIM:hh * HEAD_DIM + 1], (BLK, 1))
                                  for p in range(4)], axis=0)
            sw = [jnp.where(valid4[j], _nt(qm, ks[j]) * ATT_SCALE, NEG_INF) for j in range(3)]
            sc = _nt(qm, kc) * ATT_SCALE
            mx = jnp.maximum(jnp.maximum(jnp.maximum(sw[0].max(-1, keepdims=True), sw[1].max(-1, keepdims=True)),
                                         jnp.maximum(sw[2].max(-1, keepdims=True), sc.max(-1, keepdims=True))), sk)
            ew = [jnp.exp(s - mx) for s in sw]
            ec = jnp.exp(sc - mx)
            den = ew[0].sum(-1, keepdims=True) + ew[1].sum(-1, keepdims=True) + ew[2].sum(-1, keepdims=True)
            den = den + ec.sum(-1, keepdims=True) + jnp.exp(sk - mx)
            o = _nn((ec / den).astype(BF16), vc)
            for j in range(3):
                o += _nn((ew[j] / den).astype(BF16), vs[j])
            outs.append(o)
            lses.append(mx + jnp.log(den))
        for p in range(4):
            rows = slice(p * BLK, (p + 1) * BLK)
            o_ref[:, p * 128:(p + 1) * 128] = jnp.where(lane_lo, outs[0][rows], outs[1][rows]).astype(o_ref.dtype)
            l_ref[:, p * 128:(p + 1) * 128] = jnp.where(lane_lo, jnp.broadcast_to(lses[0][rows], (BLK, 128)),
                                                        jnp.broadcast_to(lses[1][rows], (BLK, 128)))

    qspec = pl.BlockSpec((BLK, ATT_W), lambda s, n: (s * nbs + n, 0))
    return pl.pallas_call(
        body, name=name, out_shape=[_sds((lay.T, ATT_W), BF16), _sds((lay.T, ATT_W), F32)], grid=(2, nbs),
        in_specs=[qspec] + win + [ctx] + win + [ctx] + [pl.BlockSpec((8, 128), lambda s, n: (0, 0))],
        out_specs=[qspec, qspec], compiler_params=_cp(("parallel", "parallel")))(qr, kr, kr, kr, kr, vb, vb, vb, vb, sink_tab)


def attn_bwd(lay, qr, kr, vb, sink_tab, lse, datt, name, rider=None):
    nbs, cbk, lbk, win, ctx = _attn_specs(lay)
    C, PS = lay.C, lay.PS

    def body(q_ref, k0, k1, k2, kc_ref, v0, v1, v2, vc_ref, sk_ref, l_ref, do_ref, dq_ref, dk_ref, dv_ref, ds_ref):
        n = pl.program_id(1)
        valid, lane_lo = _attn_masks(n, cbk, lbk)

        @pl.when(n == 0)
        def _():
            dk_ref[...] = jnp.zeros_like(dk_ref)
            dv_ref[...] = jnp.zeros_like(dv_ref)
            ds_ref[...] = jnp.zeros_like(ds_ref)

        ks = [k0[...], k1[...], k2[...], kc_ref[...]]
        vs = [v0[...], v1[...], v2[...], vc_ref[...]]
        valid4 = [jnp.concatenate([v] * 4, axis=0) for v in valid]
        dks = [jnp.zeros((BLK, KV_W), F32)] * 3 + [jnp.zeros((C, KV_W), F32)]
        dvs = list(dks)
        q2s = [q_ref[:, p * 128:(p + 1) * 128] for p in range(4)]
        do2s = [do_ref[:, p * 128:(p + 1) * 128].astype(BF16) for p in range(4)]
        lse2s = [l_ref[:, p * 128:(p + 1) * 128] for p in range(4)]
        dq_h, dd_h = [], []
        for hh in range(2):
            sel = lane_lo == (hh == 0)
            qm = jnp.concatenate([jnp.where(sel, q2, jnp.zeros_like(q2)) for q2 in q2s], axis=0)
            dom = jnp.concatenate([jnp.where(sel, d2, jnp.zeros_like(d2)) for d2 in do2s], axis=0)
            lse_h = jnp.concatenate([l2[:, hh * HEAD_DIM:hh * HEAD_DIM + 1] for l2 in lse2s], axis=0)
            ps, dps = [], []
            for j in range(4):
                s = _nt(qm, ks[j]) * ATT_SCALE
                if j < 3:
                    s = jnp.where(valid4[j], s, NEG_INF)
                ps.append(jnp.exp(s - lse_h))
                dps.append(_nt(dom, vs[j]))
            dd = (ps[0] * dps[0]).sum(-1, keepdims=True) + (ps[1] * dps[1]).sum(-1, keepdims=True)
            dd = dd + (ps[2] * dps[2]).sum(-1, keepdims=True) + (ps[3] * dps[3]).sum(-1, keepdims=True)
            dq = jnp.zeros((4 * BLK, 128), F32)
            for j in range(4):
                dsb = (ps[j] * (dps[j] - dd) * ATT_SCALE).astype(BF16)
                dq += _nn(dsb, ks[j])
                dks[j] = dks[j] + _tn(dsb, qm)
                dvs[j] = dvs[j] + _tn(ps[j].astype(BF16), dom)
            dq_h.append(dq)
            dd_h.append(dd)
        for p in range(4):
            sl = slice(p * 128, (p + 1) * 128)
            rows = slice(p * BLK, (p + 1) * BLK)
            dq_ref[:, sl] = jnp.where(lane_lo, dq_h[0][rows], dq_h[1][rows])
            dd2 = jnp.where(lane_lo, jnp.broadcast_to(dd_h[0][rows], (BLK, 128)), jnp.broadcast_to(dd_h[1][rows], (BLK, 128)))
            psink = jnp.exp(sk_ref[p:p + 1, :] - lse2s[p])
            ds_ref[0, p:p + 1, :] += -jnp.sum(psink * dd2, axis=0, keepdims=True)
        dk_ref[0:C, :] += dks[3]
        dv_ref[0:C, :] += dvs[3]
        for j in range(3):
            r0 = pl.multiple_of((cbk + jnp.clip(n - cbk + j - 1, 0, lbk - 1)) * BLK, BLK)
            dk_ref[pl.ds(r0, BLK), :] += dks[j]
            dv_ref[pl.ds(r0, BLK), :] += dvs[j]

    qspec = pl.BlockSpec((BLK, ATT_W), lambda s, n: (s * nbs + n, 0))
    kvout = pl.BlockSpec((PS, KV_W), lambda s, n: (s, 0))
    return _host_call(
        body, rider, name, (2, nbs),
        [qspec] + win + [ctx] + win + [ctx] + [pl.BlockSpec((8, 128), lambda s, n: (0, 0)), qspec, qspec],
        [qspec, kvout, kvout, pl.BlockSpec((1, 8, 128), lambda s, n: (s, 0, 0))],
        [_sds((lay.T, ATT_W), F32), _sds((lay.T, KV_W), F32), _sds((lay.T, KV_W), F32), _sds((2, 8, 128), F32)],
        (qr, kr, kr, kr, kr, vb, vb, vb, vb, sink_tab, lse, datt), ("parallel", "arbitrary"), 12, 4)


def _winsum(x, r):
    n = x.shape[0]
    t = lax.broadcasted_iota(jnp.int32, x.shape, 0)
    acc = x
    for o in range(1, r + 1):
        acc = acc + jnp.where(t >= o, pltpu.roll(x, o, 0), 0.0) + jnp.where(t < n - o, pltpu.roll(x, n - o, 0), 0.0)
    return acc


def _wincount(n, r):
    t = lax.broadcasted_iota(jnp.int32, (n, 128), 0)
    return (jnp.minimum(t + r, n - 1) - jnp.maximum(t - r, 0) + 1).astype(F32)


def pool_fwd(lay, u, w_pool, scale, name):
    segs = [(0, lay.C), (lay.C, lay.L)]

    def body(u_ref, w_ref, s_ref, o_ref):
        for r0, n in segs:
            for g, wd in enumerate(POOL_WINDOWS):
                sl = slice(g * 128, (g + 1) * 128)
                x = u_ref[r0:r0 + n, sl]
                d = _winsum(x, wd // 2) / _wincount(n, wd // 2) - x
                y = _nn(d.astype(BF16), w_ref[g].astype(BF16)) * s_ref[:, sl]
                o_ref[r0:r0 + n, sl] = y.astype(o_ref.dtype)

    spec = pl.BlockSpec((lay.PS, POOL_W), lambda s: (s, 0))
    return pl.pallas_call(
        body, name=name, out_shape=_sds((lay.T, POOL_W), BF16), grid=(2,),
        in_specs=[spec, pl.BlockSpec(w_pool.shape, lambda s: (0, 0, 0)), pl.BlockSpec((1, POOL_W), lambda s: (0, 0))],
        out_specs=spec, compiler_params=_cp(("parallel",), VMEM_BIG))(u, w_pool, scale)


def pool_bwd(lay, u, dcat, w_pool, scale, name):
    segs = [(0, lay.C), (lay.C, lay.L)]

    def body(u_ref, dp_ref, w_ref, s_ref, du_ref, dw_ref, dsc_ref):
        for g, wd in enumerate(POOL_WINDOWS):
            sl = slice(g * 128, (g + 1) * 128)
            wb = w_ref[g].astype(BF16)
            dw = jnp.zeros((128, 128), F32)
            dsc = jnp.zeros((1, 128), F32)
            for r0, n in segs:
                x = u_ref[r0:r0 + n, sl]
                cnt = _wincount(n, wd // 2)
                d = (_winsum(x, wd // 2) / cnt - x).astype(BF16)
                dp = dp_ref[r0:r0 + n, sl]
                dsc += jnp.sum(_nn(d, wb) * dp, axis=0, keepdims=True)
                dyp = (dp * s_ref[:, sl]).astype(BF16)
                dw += _tn(d, dyp)
                dd = _nt(dyp, wb)
                du_ref[r0:r0 + n, sl] = _winsum(dd / cnt, wd // 2) - dd
            dw_ref[0, g] = dw
            dsc_ref[0, :, sl] = dsc

    spec = pl.BlockSpec((lay.PS, POOL_W), lambda s: (s, 0))
    return pl.pallas_call(
        body, name=name,
        out_shape=[_sds((lay.T, POOL_W), F32), _sds((2, 4, 128, 128), F32), _sds((2, 1, POOL_W), F32)], grid=(2,),
        in_specs=[spec, pl.BlockSpec((lay.PS, POOL_W), lambda s: (s, 1)), pl.BlockSpec(w_pool.shape, lambda s: (0, 0, 0)),
                  pl.BlockSpec((1, POOL_W), lambda s: (0, 0))],
        out_specs=[spec, pl.BlockSpec((1, 4, 128, 128), lambda s: (s, 0, 0, 0)), pl.BlockSpec((1, 1, POOL_W), lambda s: (s, 0, 0))],
        compiler_params=_cp(("parallel",), VMEM_BIG))(u, dcat, w_pool, scale)


CONV_OFFS = (-1, 0, 1, 2)
CW = 256


def _shift_rows(x, o):
    if o == 0:
        return x
    n = x.shape[0]
    t = lax.broadcasted_iota(jnp.int32, x.shape, 0)
    if o < 0:
        return jnp.where(t >= -o, pltpu.roll(x, -o, 0), 0.0)
    return jnp.where(t < n - o, pltpu.roll(x, n - o, 0), 0.0)


def conv_fwd(lay, p, col0, w, b, name):
    segs = [(0, lay.C), (lay.C, lay.L)]
    cb0 = col0 // CW

    def body(x_ref, w_ref, b_ref, o_ref):
        for r0, n in segs:
            x = x_ref[r0:r0 + n, :]
            y = jnp.broadcast_to(b_ref[...], x.shape)
            for k, o in enumerate(CONV_OFFS):
                y = y + _shift_rows(x, o) * w_ref[k:k + 1, :]
            o_ref[r0:r0 + n, :] = y

    return pl.pallas_call(
        body, name=name, out_shape=_sds((lay.T, D), F32), grid=(2, D // CW),
        in_specs=[pl.BlockSpec((lay.PS, CW), lambda s, j: (s, cb0 + j)), pl.BlockSpec((4, CW), lambda s, j: (0, j)),
                  pl.BlockSpec((1, CW), lambda s, j: (0, j))],
        out_specs=pl.BlockSpec((lay.PS, CW), lambda s, j: (s, j)),
        compiler_params=_cp(("parallel", "parallel")))(p, w, b)


def conv_bwd(lay, p, col0, w, duc, name):
    segs = [(0, lay.C), (lay.C, lay.L)]
    cb0 = col0 // CW

    def body(x_ref, w_ref, g_ref, du_ref, dw_ref, db_ref):
        dws = [jnp.zeros((1, CW), F32)] * 4
        db = jnp.zeros((1, CW), F32)
        for r0, n in segs:
            x = x_ref[r0:r0 + n, :]
            g = g_ref[r0:r0 + n, :]
            du = jnp.zeros_like(g)
            for k, o in enumerate(CONV_OFFS):
                du = du + _shift_rows(g, -o) * w_ref[k:k + 1, :]
                dws[k] = dws[k] + jnp.sum(g * _shift_rows(x, o), axis=0, keepdims=True)
            db = db + jnp.sum(g, axis=0, keepdims=True)
            du_ref[r0:r0 + n, :] = du.astype(du_ref.dtype)
        dw_ref[0] = jnp.concatenate(dws, axis=0)
        db_ref[0] = db

    return pl.pallas_call(
        body, name=name, out_shape=[_sds((lay.T, D), BF16), _sds((2, 4, D), F32), _sds((2, 1, D), F32)], grid=(2, D // CW),
        in_specs=[pl.BlockSpec((lay.PS, CW), lambda s, j: (s, cb0 + j)), pl.BlockSpec((4, CW), lambda s, j: (0, j)),
                  pl.BlockSpec((lay.PS, CW), lambda s, j: (s, j))],
        out_specs=[pl.BlockSpec((lay.PS, CW), lambda s, j: (s, j)), pl.BlockSpec((1, 4, CW), lambda s, j: (s, 0, j)),
                   pl.BlockSpec((1, 1, CW), lambda s, j: (s, 0, j))],
        compiler_params=_cp(("parallel", "parallel")))(p, w, duc)


def _softplus_neg(lam):
    z = -lam
    w = jnp.exp(-jnp.abs(z))
    log1p = jnp.where(w < 1e-2, w * (1.0 - w * (0.5 - w / 3.0)), jnp.log(1.0 + w))
    return jnp.maximum(z, 0.0) + log1p, -_sigmoid(z)


def _neg_expm1(x):
    series = -x * (1.0 + x * (0.5 + x * (1.0 / 6.0 + x * (1.0 / 24.0 + x * (1.0 / 120.0)))))
    return jnp.where(x > -0.05, series, 1.0 - jnp.exp(x))


def _lru_gates(x, xb, wa, wx, ba, bx, lam):
    r = _sigmoid(_nn(xb, wa.astype(BF16)) + ba)
    gi = _sigmoid(_nn(xb, wx.astype(BF16)) + bx)
    sp, dsp = _softplus_neg(lam)
    la = -LRU_C * r * sp
    a = jnp.exp(la)
    sq = jnp.sqrt(_neg_expm1(2.0 * la))
    return r, gi, sp, dsp, a, sq


def lru_coeffs(lay, uc, wa, wx, vec, name):
    tr = lay.tc

    def body(x_ref, wa_ref, wx_ref, v_ref, a_ref, b_ref):
        for h in range(8):
            sl = slice(h * 128, (h + 1) * 128)
            x = x_ref[:, sl]
            xb = x.astype(BF16)
            for d in range(2):
                _, gi, _, _, a, sq = _lru_gates(x, xb, wa_ref[d, h], wx_ref[d, h], v_ref[d:d + 1, sl],
                                                v_ref[2 + d:3 + d, sl], v_ref[4 + d:5 + d, sl])
                a_ref[d, h] = a
                b_ref[d, h] = sq * (gi * x)

    wspec = pl.BlockSpec((2, 8, 128, 128), lambda i: (0, 0, 0, 0))
    ospec = pl.BlockSpec((2, 8, tr, 128), lambda i: (0, 0, i, 0))
    return pl.pallas_call(
        body, name=name, out_shape=[_sds((2, 8, lay.T, 128), F32)] * 2, grid=(lay.T // tr,),
        in_specs=[pl.BlockSpec((tr, D), lambda i: (i, 0)), wspec, wspec, pl.BlockSpec((6, D), lambda i: (0, 0))],
        out_specs=[ospec, ospec], compiler_params=_cp(("parallel",), VMEM_BIG))(uc, wa, wx, vec)


def lru_coeffs_bwd(lay, uc, wa, wx, vec, da, db, name, rider=None):
    tr = lay.tc

    def body(x_ref, wa_ref, wx_ref, v_ref, da_ref, db_ref, dx_ref, dwa_ref, dwx_ref, dv_ref):
        @pl.when(pl.program_id(0) == 0)
        def _():
            dwa_ref[...] = jnp.zeros_like(dwa_ref)
            dwx_ref[...] = jnp.zeros_like(dwx_ref)
            dv_ref[...] = jnp.zeros_like(dv_ref)

        for h in range(8):
            sl = slice(h * 128, (h + 1) * 128)
            x = x_ref[:, sl]
            xb = x.astype(BF16)
            dx = jnp.zeros_like(x)
            for d in range(2):
                wab, wxb = wa_ref[d, h].astype(BF16), wx_ref[d, h].astype(BF16)
                r, gi, sp, dsp, a, sq = _lru_gates(x, xb, wa_ref[d, h], wx_ref[d, h], v_ref[d:d + 1, sl],
                                                   v_ref[2 + d:3 + d, sl], v_ref[4 + d:5 + d, sl])
                dbv, dav = db_ref[d, h], da_ref[d, h]
                t1 = dbv * sq
                dgi = t1 * x
                dx = dx + t1 * gi
                dla = dav * a - (dbv * gi * x) * (a * a) / sq
                dr = dla * (-LRU_C * sp)
                dlam = jnp.sum(dla * (-LRU_C * r), axis=0, keepdims=True) * dsp
                dpa = dr * r * (1.0 - r)
                dpx = dgi * gi * (1.0 - gi)
                dpab, dpxb = dpa.astype(BF16), dpx.astype(BF16)
                dwa_ref[d, h] += _tn(xb, dpab)
                dwx_ref[d, h] += _tn(xb, dpxb)
                dx = dx + _nt(dpab, wab) + _nt(dpxb, wxb)
                dv_ref[d:d + 1, sl] += jnp.sum(dpa, axis=0, keepdims=True)
                dv_ref[2 + d:3 + d, sl] += jnp.sum(dpx, axis=0, keepdims=True)
                dv_ref[4 + d:5 + d, sl] += dlam
            dx_ref[:, sl] = dx

    wspec = pl.BlockSpec((2, 8, 128, 128), lambda i: (0, 0, 0, 0))
    gspec = pl.BlockSpec((2, 8, tr, 128), lambda i: (0, 0, i, 0))
    vspec = pl.BlockSpec((6, D), lambda i: (0, 0))
    xspec = pl.BlockSpec((tr, D), lambda i: (i, 0))
    return _host_call(
        body, rider, name, (lay.T // tr,), [xspec, wspec, wspec, vspec, gspec, gspec], [xspec, wspec, wspec, vspec],
        [_sds((lay.T, D), F32), _sds((2, 8, 128, 128), F32), _sds((2, 8, 128, 128), F32), _sds((6, D), F32)],
        (uc, wa, wx, vec, da, db), ("arbitrary",), 6, 4)


GB = 2
SCAN_UNROLL = 8


def _tile_scan(a, b, up):
    t = lax.broadcasted_iota(jnp.int32, a.shape, 0)
    for d in (1, 2, 4):
        sh = 8 - d if up else d
        m = (t < 8 - d) if up else (t >= d)
        a_prev, b_prev = pltpu.roll(a, sh, 0), pltpu.roll(b, sh, 0)
        b = jnp.where(m, a * b_prev + b, b)
        a = jnp.where(m, a * a_prev, a)
    return a, b


def lru_scan(lay, a, b, name):
    segs = [(0, lay.C), (lay.C, lay.L)]

    def body(a_ref, b_ref, s_ref):
        for d in range(2):
            rev = d == 1
            state = tuple(jnp.zeros((1, 128), F32) for _ in range(GB))
            for base, n in segs:
                nt = n // 8

                def step(j, c, base=base, nt=nt, rev=rev, d=d):
                    c = list(c)
                    for u in range(SCAN_UNROLL):
                        jj = j * SCAN_UNROLL + u
                        r0 = pl.multiple_of(base + 8 * ((nt - 1 - jj) if rev else jj), 8)
                        for g in range(GB):
                            at, bt = _tile_scan(a_ref[d, g, pl.ds(r0, 8), :], b_ref[d, g, pl.ds(r0, 8), :], rev)
                            h = at * c[g] + bt
                            s_ref[d, g, pl.ds(r0, 8), :] = h
                            c[g] = h[0:1] if rev else h[7:8]
                    return tuple(c)

                state = lax.fori_loop(0, nt // SCAN_UNROLL, step, state)

    spec = pl.BlockSpec((2, GB, lay.PS, 128), lambda s, hb: (0, hb, s, 0))
    return pl.pallas_call(
        body, name=name, out_shape=_sds((2, 8, lay.T, 128), F32), grid=(2, 8 // GB),
        in_specs=[spec, spec], out_specs=spec, compiler_params=_cp(("parallel", "parallel"), VMEM_BIG))(a, b)


def lru_scan_bwd(lay, a, s, dy, name):
    segs = [(0, lay.C), (lay.C, lay.L)]
    C, PS = lay.C, lay.PS

    def body(a_ref, s_ref, g_ref, da_ref, db_ref):
        t = lax.broadcasted_iota(jnp.int32, (8, 128), 0)
        for d in range(2):
            rev = d == 1
            carry = tuple(jnp.zeros((1, 128), F32) for _ in range(GB))
            for si in (1, 0):
                base, n = segs[si]
                nt = n // 8

                def step(j, c, base=base, nt=nt, rev=rev, d=d):
                    c = list(c)
                    for u in range(SCAN_UNROLL):
                        jj = j * SCAN_UNROLL + u
                        r0 = pl.multiple_of(base + 8 * (jj if rev else (nt - 1 - jj)), 8)
                        if rev:
                            rn = pl.multiple_of(jnp.where(r0 == PS - 8, 0, r0 + 8), 8)
                            nb_zero = r0 == C - 8
                        else:
                            rn = pl.multiple_of(jnp.maximum(r0 - 8, 0), 8)
                            nb_zero = r0 == 0
                        for g in range(GB):
                            av = a_ref[d, g, pl.ds(r0, 8), :]
                            gv = g_ref[g, pl.ds(r0, 8), :]
                            sv = s_ref[d, g, pl.ds(r0, 8), :]
                            nbt = s_ref[d, g, pl.ds(rn, 8), :]
                            at, bt = _tile_scan(av, av * gv, not rev)
                            m = at * c[g] + bt
                            if rev:
                                m_next = jnp.where(t >= 1, pltpu.roll(m, 1, 0), c[g])
                                nb = jnp.where(nb_zero, 0.0, nbt[0:1])
                                h_prev = jnp.where(t < 7, pltpu.roll(sv, 7, 0), nb)
                                c[g] = m[7:8]
                            else:
                                m_next = jnp.where(t < 7, pltpu.roll(m, 7, 0), c[g])
                                nb = jnp.where(nb_zero, 0.0, nbt[7:8])
                                h_prev = jnp.where(t >= 1, pltpu.roll(sv, 1, 0), nb)
                                c[g] = m[0:1]
                            lam = gv + m_next
                            db_ref[d, g, pl.ds(r0, 8), :] = lam
                            da_ref[d, g, pl.ds(r0, 8), :] = lam * h_prev
                    return tuple(c)

                carry = lax.fori_loop(0, nt // SCAN_UNROLL, step, carry)

    spec = pl.BlockSpec((2, GB, lay.PS, 128), lambda s, hb: (0, hb, s, 0))
    return pl.pallas_call(
        body, name=name, out_shape=[_sds((2, 8, lay.T, 128), F32)] * 2, grid=(2, 8 // GB),
        in_specs=[spec, spec, pl.BlockSpec((GB, lay.PS, 128), lambda s, hb: (hb, s, 0))],
        out_specs=[spec, spec], compiler_params=_cp(("parallel", "parallel"), VMEM_BIG))(a, s, dy)


def _gelu(x):
    k = math.sqrt(2.0 / math.pi)
    t = jnp.tanh(k * (x + 0.044715 * x * x * x))
    return 0.5 * x * (1.0 + t), 0.5 * (1.0 + t) + 0.5 * x * (1.0 - t * t) * k * (1.0 + 3 * 0.044715 * x * x)


def lru_gate(lay, p, s, name):
    tr = lay.tr

    def body(g_ref, s_ref, o_ref):
        for h in range(8):
            sl = slice(h * 128, (h + 1) * 128)
            o_ref[:, sl] = (_gelu(g_ref[:, sl])[0] * (s_ref[0, h] + s_ref[1, h])).astype(o_ref.dtype)

    return pl.pallas_call(
        body, name=name, out_shape=_sds((lay.T, D), BF16), grid=(lay.nblk,),
        in_specs=[pl.BlockSpec((tr, D), lambda i: (i, 0)), pl.BlockSpec((2, 8, tr, 128), lambda i: (0, 0, i, 0))],
        out_specs=pl.BlockSpec((tr, D), lambda i: (i, 0)), compiler_params=_cp(("parallel",)))(p, s)


def lru_gate_bwd(lay, p, s, do, name):
    tr = lay.tr

    def body(g_ref, s_ref, do_ref, dg_ref, dy_ref):
        for h in range(8):
            sl = slice(h * 128, (h + 1) * 128)
            ge, dge = _gelu(g_ref[:, sl])
            dov = do_ref[:, sl]
            dg_ref[:, sl] = (dov * (s_ref[0, h] + s_ref[1, h]) * dge).astype(dg_ref.dtype)
            dy_ref[h] = dov * ge

    xspec = pl.BlockSpec((tr, D), lambda i: (i, 0))
    return pl.pallas_call(
        body, name=name, out_shape=[_sds((lay.T, D), BF16), _sds((8, lay.T, 128), F32)], grid=(lay.nblk,),
        in_specs=[xspec, pl.BlockSpec((2, 8, tr, 128), lambda i: (0, 0, i, 0)), xspec],
        out_specs=[xspec, pl.BlockSpec((8, tr, 128), lambda i: (0, i, 0))],
        compiler_params=_cp(("parallel",)))(p, s, do)


def silu_rows(x, name):
    def body(x_ref, o_ref):
        v = x_ref[...]
        o_ref[...] = (v * _sigmoid(v)).astype(o_ref.dtype)
    return pl.pallas_call(body, name=name, out_shape=_sds(x.shape, BF16), in_specs=[VMEM_SPEC], out_specs=VMEM_SPEC)(x)


def mod_grad_rows(gath, name):
    w = gath.shape[-1]

    def body(g_ref, dm_ref, db_ref):
        dm_ref[...] = jnp.zeros_like(dm_ref)
        for l in range(2):
            ctx = g_ref[0, 3 * l + 2:3 * l + 3, :]
            tot = g_ref[0, 3 * l:3 * l + 1, :] + g_ref[0, 3 * l + 1:3 * l + 2, :]
            for k in range(8):
                dm_ref[l, 2 * k:2 * k + 2, :] = g_ref[k, 3 * l:3 * l + 2, :]
                if k:
                    ctx = ctx + g_ref[k, 3 * l + 2:3 * l + 3, :]
                    tot = tot + (g_ref[k, 3 * l:3 * l + 1, :] + g_ref[k, 3 * l + 1:3 * l + 2, :])
            dm_ref[l, 16:17, :] = ctx
            db_ref[l:l + 1, :] = tot + ctx

    return pl.pallas_call(body, name=name, out_shape=[_sds((2, 32, w), F32), _sds((2, w), F32)],
                          in_specs=[VMEM_SPEC], out_specs=[VMEM_SPEC, VMEM_SPEC])(gath)


def cctx_grad(p, c_ctx, name):
    def body(a_ref, c_ref, o_ref):
        cv = c_ref[...]
        sg = _sigmoid(cv)
        o_ref[...] = 0.5 * (a_ref[0, 0:1, :] + a_ref[1, 0:1, :]) * (sg * (1.0 + cv * (1.0 - sg)))
    return pl.pallas_call(body, name=name, out_shape=_sds((1, D), F32), in_specs=[VMEM_SPEC] * 2,
                          out_specs=VMEM_SPEC)(p, c_ctx)


def loss_lnb(lay, xhat, rstd, y, tgt, mod, k_gate, coef, lnv, name):
    def fn(xh, rs, yb, tb, m, ln):
        lat = (pl.program_id(0) % lay.bps) >= lay.cb
        e = jnp.where(lat, xh * ln[0:1] + ln[1:2] - tb, 0.0)
        dy, dres, s1 = _ln_bwd_math(e * (1.0 / D), xh, rs, yb, m[k_gate:k_gate + 1], coef, ln)
        return dy, dres, s1, jnp.sum(e * e, axis=0, keepdims=True) * (0.5 / D)
    return rowwise(lay, name, fn, [xhat, rstd, y, tgt], segs=[mod], vecs=[lnv],
                   outs=[(D, BF16), (D, F32)], sums=[(3, D), (1, D)])


def adamw(w, g, m, v, name):
    shape = w.shape
    w2, g2, m2, v2 = (t.reshape(-1, shape[-1]) for t in (w, g, m, v))
    rows, width = w2.shape
    tr = 256 if rows % 256 == 0 else rows
    c1 = 1.0 - ADAM_B1 ** ADAM_STEP
    c2 = 1.0 - ADAM_B2 ** ADAM_STEP

    def body(w_ref, g_ref, m_ref, v_ref, d_ref, mo_ref, vo_ref):
        gv = g_ref[...]
        mn = ADAM_B1 * m_ref[...] + (1.0 - ADAM_B1) * gv
        vn = ADAM_B2 * v_ref[...] + (1.0 - ADAM_B2) * (gv * gv)
        d_ref[...] = -ADAM_LR * ((mn / c1) / (jnp.sqrt(vn / c2) + ADAM_EPS) + ADAM_WD * w_ref[...])
        mo_ref[...] = mn
        vo_ref[...] = vn

    spec = pl.BlockSpec((tr, width), lambda i: (i, 0))
    d, mn, vn = pl.pallas_call(body, name=name, out_shape=[_sds((rows, width), F32)] * 3, grid=(rows // tr,),
                               in_specs=[spec] * 4, out_specs=[spec] * 3, compiler_params=_cp(("parallel",)))(w2, g2, m2, v2)
    return d.reshape(shape), mn.reshape(shape), vn.reshape(shape)


def adamw_ffn(w, m, v, red, kind, ns, name):
    shape = w.shape
    w2, m2, v2 = (t.reshape(-1, shape[-1]) for t in (w, m, v))
    rows, width = w2.shape
    c1 = 1.0 - ADAM_B1 ** ADAM_STEP
    c2 = 1.0 - ADAM_B2 ** ADAM_STEP
    tr, nb = ns // 2, 2
    gspec = pl.BlockSpec((tr, D), lambda i: (((i // nb) * 3 + kind) * nb + i % nb, 0))

    def body(w_ref, g_ref, m_ref, v_ref, go_ref, d_ref, mo_ref, vo_ref):
        gv = g_ref[...]
        mn = ADAM_B1 * m_ref[...] + (1.0 - ADAM_B1) * gv
        vn = ADAM_B2 * v_ref[...] + (1.0 - ADAM_B2) * (gv * gv)
        go_ref[...] = gv
        d_ref[...] = -ADAM_LR * ((mn / c1) / (jnp.sqrt(vn / c2) + ADAM_EPS) + ADAM_WD * w_ref[...])
        mo_ref[...] = mn
        vo_ref[...] = vn

    spec = pl.BlockSpec((tr, width), lambda i: (i, 0))
    outs = pl.pallas_call(body, name=name, out_shape=[_sds((rows, width), F32)] * 4, grid=(rows // tr,),
                          in_specs=[spec, gspec, spec, spec], out_specs=[spec] * 4,
                          compiler_params=_cp(("parallel",)))(w2, red, m2, v2)
    return tuple(t.reshape(shape) for t in outs)


def mod_mm(sc, w_mod, bias, name):
    wm = w_mod.shape[-1]
    tn = _pick(wm, (768, 512, 384, 256, 128))

    def body(a_ref, b_ref, c_ref, o_ref):
        o_ref[...] = _nn(a_ref[...], b_ref[...].astype(BF16)) + c_ref[...]

    return pl.pallas_call(
        body, name=name, out_shape=_sds((DEPTH, 32, wm), F32), grid=(DEPTH, wm // tn),
        in_specs=[pl.BlockSpec((32, D), lambda l, j: (0, 0)), pl.BlockSpec((None, D, tn), lambda l, j: (l, 0, j)),
                  pl.BlockSpec((None, 1, tn), lambda l, j: (l, 0, j))],
        out_specs=pl.BlockSpec((None, 32, tn), lambda l, j: (l, 0, j)),
        compiler_params=_cp(("parallel", "parallel")))(sc, w_mod, bias)


def wmod_dw(sc, dcol, name):
    wm = dcol.shape[-1]
    tm = 256

    def body(a_ref, b_ref, o_ref):
        o_ref[...] = _tn(a_ref[...], b_ref[...].astype(BF16))

    return pl.pallas_call(
        body, name=name, out_shape=_sds((DEPTH, D, wm), F32), grid=(DEPTH, D // tm),
        in_specs=[pl.BlockSpec((32, tm), lambda l, i: (0, i)), pl.BlockSpec((None, 32, wm), lambda l, i: (l, 0, 0))],
        out_specs=pl.BlockSpec((None, tm, wm), lambda l, i: (l, i, 0)),
        compiler_params=_cp(("parallel", "parallel")))(sc, dcol)


def cctx_dx(drow, w_mod, name):
    wm = w_mod.shape[-1]

    def body(a_ref, b_ref, o_ref):
        o_ref[...] = _nt(a_ref[...].astype(BF16), b_ref[...].astype(BF16))

    return pl.pallas_call(
        body, name=name, out_shape=_sds((DEPTH, 16, D), F32), grid=(DEPTH,),
        in_specs=[pl.BlockSpec((None, 16, wm), lambda l: (l, 0, 0)), pl.BlockSpec((None, D, wm), lambda l: (l, 0, 0))],
        out_specs=pl.BlockSpec((None, 16, D), lambda l: (l, 0, 0)), compiler_params=_cp(("parallel",), VMEM_BIG))(drow, w_mod)


HEAD_PERM = (0, 4, 1, 5, 2, 6, 3, 7)


def _rot_rows(wt):
    return jnp.concatenate([-wt[32:64], wt[0:32]], axis=0)


def _unrot_rows(g):
    return jnp.concatenate([g[32:64], -g[0:32]], axis=0)


def _heads(a, n):
    return [a[64 * i:64 * (i + 1)] for i in range(n)]


def kernel(x, c, ctx, c_ctx, w_mod, b_mod, ln_g, ln_b, ffn_w_gate, ffn_w_up, ffn_w_down, mix_ab_w_in, attn_sink, pool_w, pool_scale, mix_ab_w_out, lru_w_in, lru_conv_w, lru_conv_b, lru_wa, lru_ba, lru_wx, lru_bx, lru_lambda, lru_w_out, loss_target, m_c_ctx, m_w_mod, m_b_mod, m_ln_g, m_ln_b, m_ffn_w_gate, m_ffn_w_up, m_ffn_w_down, m_mix_ab_w_in, m_attn_sink, m_pool_w, m_pool_scale, m_mix_ab_w_out, m_lru_w_in, m_lru_conv_w, m_lru_conv_b, m_lru_wa, m_lru_ba, m_lru_wx, m_lru_bx, m_lru_lambda, m_lru_w_out, v_c_ctx, v_w_mod, v_b_mod, v_ln_g, v_ln_b, v_ffn_w_gate, v_ffn_w_up, v_ffn_w_down, v_mix_ab_w_in, v_attn_sink, v_pool_w, v_pool_scale, v_mix_ab_w_out, v_lru_w_in, v_lru_conv_w, v_lru_conv_b, v_lru_wa, v_lru_ba, v_lru_wx, v_lru_bx, v_lru_lambda, v_lru_w_out):
    n_lat, n_ctx = x.shape[1], ctx.shape[1]
    lay = Layout(n_ctx, n_lat)
    T = lay.T
    ns = ffn_w_gate.shape[-1]
    n_li, n_ai = lru_w_in.shape[-1], mix_ab_w_in.shape[-1]
    n_ao, n_lo = mix_ab_w_out.shape[1], lru_w_out.shape[1]
    wm = w_mod.shape[-1]
    dsh = ln_g.shape[-1]
    mx, my, mc = lax.axis_index("x"), lax.axis_index("y"), lax.axis_index("c")
    chip = 2 * mx + my
    me = 2 * chip + mc

    c_all = all_gather8(c, "ag8_c").reshape(16, D)
    cc = jnp.concatenate([c_all, c_ctx[None, :], jnp.zeros((15, D), F32)], axis=0)
    sc = silu_rows(cc, "silu_c")
    bias = lax.dynamic_slice(b_mod, (0, chip * wm), (DEPTH, wm)).reshape(DEPTH, 1, wm)
    modg = all_gather_chips(mod_mm(sc, w_mod, bias, "mod_mm"), "ag_mod")
    modtab = []
    for l in range(DEPTH):
        full = jnp.transpose(modg[:, l], (1, 0, 2)).reshape(32, N_CHIP * wm)
        mine = lax.dynamic_slice(full, (2 * me, 0), (2, N_CHIP * wm))
        modtab.append(jnp.concatenate([mine, full[16:17]], axis=0).reshape(3, N_MOD, D))

    small = jnp.concatenate([ln_g.reshape(6, dsh), ln_b.reshape(6, dsh), lru_conv_w[0], lru_conv_b, lru_ba[0],
                             lru_bx[0], lru_lambda[0], jnp.zeros((9, dsh), F32)], axis=0)
    small = all_gather_chips(small.reshape(2, 16, dsh), "ag_small").reshape(N_CHIP, 32, dsh)
    small = jnp.transpose(small, (1, 0, 2)).reshape(32, D)
    ln_g_f, ln_b_f = small[0:6].reshape(2, 3, D), small[6:12].reshape(2, 3, D)
    conv_w_f, conv_b_f = small[12:16], small[16:17]
    lru_vec = small[17:23]

    hh = 3 * ns // 2
    gate_t, up_t = jnp.swapaxes(ffn_w_gate, -1, -2), jnp.swapaxes(ffn_w_up, -1, -2)
    extra = [0, n_ai + n_ao, n_li + n_lo, 0]
    placed = [ffn_place(gate_t, up_t, ffn_w_down, g // 2, g % 2, f"ag_ffn{g}_place", extra[g]) for g in range(4)]
    placed[1] = place_rows(placed[1], jnp.concatenate([mix_ab_w_in[0].T, mix_ab_w_out[0]], axis=0).astype(BF16), 3 * ns,
                           "ag_mixa_place")
    placed[2] = place_rows(placed[2], jnp.concatenate([lru_w_in[0].T, lru_w_out[0]], axis=0).astype(BF16), 3 * ns,
                           "ag_mixc_place")
    placed = [p.reshape(N_CHIP, 2, p.shape[1] // 2, D) for p in placed]
    wb = [gather_placed(placed[0], "ag_ffn0"), None, None, None]
    mixw = {}

    def mixa_w():
        if "a" not in mixw:
            full = wb[1].reshape(N_CHIP, -1, D)
            ab_in_t = full[:, 3 * ns:3 * ns + n_ai].reshape(N_CHIP * n_ai, D)
            ab_out = full[:, 3 * ns + n_ai:].reshape(N_CHIP * n_ao, D)
            qh, kh = _heads(ab_in_t[Q0:K0], N_HEADS), _heads(ab_in_t[K0:V0], N_KV)
            w_ext_t = jnp.concatenate([qh[h] for h in HEAD_PERM] + [ab_in_t[K0:QR0]]
                                      + [_rot_rows(qh[h]) for h in HEAD_PERM] + [_rot_rows(t) for t in kh], axis=0)
            oh = _heads(ab_out[0:ATT_W], N_HEADS)
            mixw["a"] = (w_ext_t, jnp.concatenate([oh[h] for h in HEAD_PERM] + [ab_out[ATT_W:]], axis=0))
        return mixw["a"]

    def mixc_w():
        if "c" not in mixw:
            full = wb[2].reshape(N_CHIP, -1, D)
            mixw["c"] = (full[:, 3 * ns:3 * ns + n_li].reshape(N_CHIP * n_li, D),
                         full[:, 3 * ns + n_li:].reshape(N_CHIP * n_lo, D))
        return mixw["c"]

    t = jnp.arange(n_lat)
    inv = ROPE_THETA ** (-jnp.arange(16, dtype=F32) / 16.0)
    ang = jnp.concatenate([(t // GRID_W).astype(F32)[:, None] * inv, (t % GRID_W).astype(F32)[:, None] * inv], axis=-1)
    cos1 = jnp.concatenate([jnp.ones((n_ctx, 32), F32), jnp.cos(ang)], axis=0)
    sin1 = jnp.concatenate([jnp.zeros((n_ctx, 32), F32), jnp.sin(ang)], axis=0)
    cos_t = jnp.tile(cos1, (2, 4))
    sin_t = jnp.tile(sin1, (2, 4))
    sk = attn_sink[0]
    sink_tab = jnp.concatenate([jnp.repeat(jnp.stack([sk[:4], sk[4:]], axis=1), HEAD_DIM, axis=1),
                                jnp.zeros((4, 128), F32)], axis=0)
    pscale = pool_scale.reshape(1, POOL_W)

    h0 = jnp.concatenate([ctx, x], axis=1).reshape(T, D)
    tgt = loss_target.reshape(2 * n_lat, D)

    def lnv(l, j):
        return jnp.stack([ln_g_f[l, j], ln_b_f[l, j]])

    subs = [(0, 0, 0.5, 0), (0, 3, 1.0, 1), (0, 6, 0.5, 2), (1, 0, 0.5, 0), (1, 3, 1.0, 1), (1, 6, 0.5, 2)]

    def ffn_core(hm, l, f):
        tag = f"l{l}f{f}"
        gi = 2 * l + f
        w = wb[gi].reshape(N_CHIP, -1, D)
        if gi == 3:
            up, sl, a = ffn_up(lay, hm, w, 0, 1, ns, f"ffn_up_{tag}")
            (y,) = slab_nn_acc(lay, [a], w, [2], ns, f"ffn_down_{tag}")
            return y, dict(up=up, sl=sl, a=a, nbuf=None)
        up, sl, a, nbuf = ffn_up(lay, hm, w, 0, 1, ns, f"ffn_up_{tag}", rider=rider_gather_xy(placed[gi + 1]))
        y, nbuf = slab_nn_acc(lay, [a], w, [2], ns, f"ffn_down_{tag}", rider=rider_gather_fwd(nbuf))
        return y, dict(up=up, sl=sl, a=a, nbuf=nbuf)

    def mixa_core(hm):
        p = mm_nt(hm, mixa_w()[0], "mixa_in")
        qr, kr, vb, u = rope_fwd(lay, p, cos_t, sin_t, "rope")
        att, lse = attn_fwd(lay, qr, kr, vb, sink_tab, "attn")
        pool = pool_fwd(lay, u, pool_w[0], pscale, "pool")
        cat = jnp.concatenate([att, pool], axis=1)
        return mm_nn(cat, mixa_w()[1], "mixa_out"), dict(qr=qr, kr=kr, vb=vb, u=u, lse=lse, cat=cat)

    def mixc_core(hm):
        p = mm_nt(hm, mixc_w()[0], "mixc_in")
        uc = conv_fwd(lay, p, D, conv_w_f, conv_b_f, "conv")
        a, b = lru_coeffs(lay, uc, lru_wa[0], lru_wx[0], lru_vec, "lru_coef")
        s = lru_scan(lay, a, b, "lru_scan")
        o = lru_gate(lay, p, s, "lru_gate")
        return mm_nn(o, mixc_w()[1], "mixc_out"), dict(p=p, uc=uc, a=a, s=s, o=o)

    recs = []
    h = h0
    hm = modulate(lay, h0, modtab[0], 0, 1, "mod_first")
    for k, (l, k0, coef, j) in enumerate(subs):
        if k0 == 3:
            y, core = mixa_core(hm) if l == 0 else mixc_core(hm)
        else:
            y, core = ffn_core(hm, l, k0 // 6)
        nxt = None if k == 5 else (modtab[subs[k + 1][0]], subs[k + 1][1], subs[k + 1][1] + 1)
        nbuf = core.pop("nbuf", None)
        res = resid_ln(lay, h, y, modtab[l], k0 + 2, coef, lnv(l, j), f"ln_s{k}", nxt=nxt,
                       prev_ln=None if k == 0 else lnv(*subs[k - 1][::3]),
                       rider=None if nbuf is None else rider_gather_d2d(nbuf))
        if nbuf is not None:
            wb[2 * l + k0 // 6 + 1] = res[-1]
        recs.append(dict(h=h, hm=hm, y=y, xhat=res[0], rstd=res[1], **core))
        h, hm = res[0], (None if nxt is None else res[2])


    dln = {}
    dms = {}
    mixg = {}
    ffn_red = [lax.empty((4, 2, hh, D), F32)]
    mix_red = {}
    pending = []

    def rs_sib(p):
        return None if p is None else rider_reduce_sib(p["buf"])

    def rs_add2(p, recv):
        p["q"] = add_own_half(p["buf"], recv, BF16, f"rs_add2_{p['key']}")

    def rs_join(p, arr):
        if isinstance(p["key"], int):
            return rider_join(sum_slots(p["q"], arr, f"rs_add4_{p['key']}", dst=ffn_red[0], g=p["key"]), p["key"])
        return rider_join(sum_slots(p["q"], arr, f"rs_add4_{p['key']}"))

    def rs_done(p, joined):
        if isinstance(p["key"], int):
            ffn_red[0] = joined
        else:
            mix_red[p["key"]] = joined.reshape(-1, D)

    def ffn_core_bwd(dy, r, l, f):
        tag = f"l{l}f{f}"
        gi = 2 * l + f
        w = wb[gi].reshape(N_CHIP, -1, D)
        p = pending.pop() if pending else None
        gb = lax.empty((N_CHIP, 3 * ns, D), F32)
        if p is None:
            dg, du = ffn_bwd_da(lay, dy, w, 2, r["up"], r["sl"], ns, f"ffn_da_{tag}")
            (gb,) = slab_tn(lay, r["a"], dy, gb, 2, ns, f"ffn_dwd_{tag}")
            (gb,) = slab_tn(lay, dg, r["hm"], gb, 0, ns, f"ffn_dwg_{tag}")
            (gb,) = slab_tn(lay, du, r["hm"], gb, 1, ns, f"ffn_dwu_{tag}")
            (dhm,) = slab_nn_acc(lay, [dg, du], w, [0, 1], ns, f"ffn_dh_{tag}")
        elif gi == 0:
            dg, du, recv = ffn_bwd_da(lay, dy, w, 2, r["up"], r["sl"], ns, f"ffn_da_{tag}", rider=rs_sib(p))
            rs_add2(p, recv)
            gb, arr = slab_tn(lay, r["a"], dy, gb, 2, ns, f"ffn_dwd_{tag}", rider=rider_reduce_copies(p["q"]))
            gb, joined = slab_tn(lay, dg, r["hm"], gb, 0, ns, f"ffn_dwg_{tag}", rider=rs_join(p, arr))
            rs_done(p, joined)
            (gb,) = slab_tn(lay, du, r["hm"], gb, 1, ns, f"ffn_dwu_{tag}")
            own = gb.reshape(N_CHIP, 2, hh, D)
            dhm, recv = slab_nn_acc(lay, [dg, du], w, [0, 1], ns, f"ffn_dh_{tag}", rider=rider_reduce_sib(own))
            pending.append(dict(buf=own, key=gi, recv=recv))
            return dhm
        else:
            dg, du, recv = ffn_bwd_da(lay, dy, w, 2, r["up"], r["sl"], ns, f"ffn_da_{tag}", rider=rs_sib(p))
            rs_add2(p, recv)
            gb, arr = slab_tn(lay, r["a"], dy, gb, 2, ns, f"ffn_dwd_{tag}", rider=rider_reduce_copy(p["q"], 0))
            gb, arr = slab_tn(lay, dg, r["hm"], gb, 0, ns, f"ffn_dwg_{tag}", rider=rider_reduce_copy(p["q"], 1, arr))
            gb, arr = slab_tn(lay, du, r["hm"], gb, 1, ns, f"ffn_dwu_{tag}", rider=rider_reduce_copy(p["q"], 2, arr))
            dhm, joined = slab_nn_acc(lay, [dg, du], w, [0, 1], ns, f"ffn_dh_{tag}", rider=rs_join(p, arr))
            rs_done(p, joined)
        pending.append(dict(buf=gb.reshape(N_CHIP, 2, hh, D), key=gi))
        return dhm

    def mixc_core_bwd(dy, r):
        p = pending.pop() if pending else None
        w_in_t, w_out = mixc_w()
        if p is None:
            do_c = mm_nt(dy, w_out, "mixc_out_dx")
        else:
            do_c, recv = mm_nt(dy, w_out, "mixc_out_dx", rider=rs_sib(p))
            rs_add2(p, recv)
        g_out = mm_tn(r["o"], dy, "mixc_out_dw")
        dgate, dyg = lru_gate_bwd(lay, r["p"], r["s"], do_c, "lru_gate_b")
        da_c, db_c = lru_scan_bwd(lay, r["a"], r["s"], dyg, "lru_scan_b")
        res = lru_coeffs_bwd(lay, r["uc"], lru_wa[0], lru_wx[0], lru_vec, da_c, db_c, "lru_coef_b",
                             rider=None if p is None else rider_reduce_copies(p["q"]))
        duc, mixg["wa"], mixg["wx"], mixg["vec"] = res[:4]
        du_c, mixg["cw"], mixg["cb"] = conv_bwd(lay, r["p"], D, conv_w_f, duc, "conv_b")
        dp_c = jnp.concatenate([dgate, du_c], axis=1)
        if p is None:
            g_in_t = mm_tn(dp_c, r["hm"], "mixc_in_dw")
        else:
            g_in_t, joined = mm_tn(dp_c, r["hm"], "mixc_in_dw", rider=rs_join(p, res[4]))
            rs_done(p, joined)
        buf = jnp.concatenate([g_in_t.reshape(N_CHIP, n_li, D), g_out.reshape(N_CHIP, n_lo, D)], axis=1)
        pending.append(dict(buf=buf.reshape(N_CHIP, 2, (n_li + n_lo) // 2, D), key="c"))
        return mm_nn(dp_c, w_in_t, "mixc_in_dx")

    def mixa_core_bwd(dy, r):
        p = pending.pop() if pending else None
        w_ext_t, w_out_ext = mixa_w()
        if p is None:
            dcat = mm_nt(dy, w_out_ext, "mixa_out_dx")
        else:
            dcat, recv = mm_nt(dy, w_out_ext, "mixa_out_dx", rider=rs_sib(p))
            rs_add2(p, recv)
        g_out_ext = mm_tn(r["cat"], dy, "mixa_out_dw")
        res = attn_bwd(lay, r["qr"], r["kr"], r["vb"], sink_tab, r["lse"], dcat, "attn_b",
                       rider=None if p is None else rider_reduce_copies(p["q"]))
        dqr, dkr, dv, mixg["sink"] = res[:4]
        du_a, mixg["pw"], mixg["ps"] = pool_bwd(lay, r["u"], dcat, pool_w[0], pscale, "pool_b")
        dp_a = rope_bwd(lay, dqr, dkr, dv, du_a, cos_t, sin_t, "rope_b")
        if p is None:
            g_ext_t = mm_tn(dp_a, r["hm"], "mixa_in_dw")
        else:
            g_ext_t, joined = mm_tn(dp_a, r["hm"], "mixa_in_dw", rider=rs_join(p, res[4]))
            rs_done(p, joined)
        gq, gqr = _heads(g_ext_t[Q0:K0], N_HEADS), _heads(g_ext_t[QR0:KR0], N_HEADS)
        g_q = [None] * N_HEADS
        for i, h in enumerate(HEAD_PERM):
            g_q[h] = gq[i] + _unrot_rows(gqr[i])
        gk = [a + _unrot_rows(b) for a, b in zip(_heads(g_ext_t[K0:V0], N_KV), _heads(g_ext_t[KR0:PEXT], N_KV))]
        g_ab_in_t = jnp.concatenate(g_q + gk + [g_ext_t[V0:QR0]], axis=0)
        go = _heads(g_out_ext[0:ATT_W], N_HEADS)
        g_o = [None] * N_HEADS
        for i, h in enumerate(HEAD_PERM):
            g_o[h] = go[i]
        g_ab_out = jnp.concatenate(g_o + [g_out_ext[ATT_W:]], axis=0)
        buf = jnp.concatenate([g_ab_in_t.reshape(N_CHIP, n_ai, D), g_ab_out.reshape(N_CHIP, n_ao, D)], axis=1)
        pending.append(dict(buf=buf.reshape(N_CHIP, 2, (n_ai + n_ao) // 2, D), key="a"))
        return mm_nn(dp_a, w_ext_t, "mixa_in_dx")

    l, k0, coef, j = subs[5]
    dy, dres, s1, lparts = loss_lnb(lay, recs[5]["xhat"], recs[5]["rstd"], recs[5]["y"], tgt, modtab[l], k0 + 2, coef,
                                    lnv(l, j), "loss_lnb")
    loss = lax.psum(jnp.sum(lparts), ("x", "y", "c"))
    for k in range(5, -1, -1):
        l, k0, coef, j = subs[k]
        r = recs[k]
        if k0 == 3:
            dhm = mixa_core_bwd(dy, r) if l == 0 else mixc_core_bwd(dy, r)
        else:
            dhm = ffn_core_bwd(dy, r, l, k0 // 6)
        dln[(l, j)] = s1
        if k > 0:
            lp, k0p, coefp, jp = subs[k - 1]
            rp = recs[k - 1]
            dy, dres, s1, s2 = modb_lnb(lay, dres, dhm, modtab[l], k0 + 1, rp["xhat"], rp["rstd"], rp["y"],
                                        modtab[lp], k0p + 2, coefp, lnv(lp, jp), f"modb_lnb_s{k}")
        else:
            gx, s2 = mod_bwd(lay, dres, dhm, r["h"], modtab[l], k0 + 1, "modb_s0")
        dms[(l, k0)] = s2
    sums = block_sums(lay, list(dln.values()) + list(dms.values()), "block_sums")
    dln, dms = dict(zip(dln, sums[:len(dln)])), dict(zip(dms, sums[len(dln):]))
    grad_x = gx.reshape(2, n_lat, D)
    g_wa, g_wx, g_vec, g_cw, g_cb, g_sink, g_pw, g_ps = (mixg[n] for n in ("wa", "wx", "vec", "cw", "cb", "sink", "pw", "ps"))

    rows = []
    for l in range(DEPTH):
        per_k = []
        for k0, j in ((0, 0), (3, 1), (6, 2)):
            per_k += [dms[(l, k0)][:3, 0], dms[(l, k0)][:3, 1], dln[(l, j)][:3, 2]]
        rows.append(jnp.stack(per_k, axis=1).reshape(3, N_MOD * D))
    dmod_loc = jnp.concatenate(rows + [jnp.zeros((2, N_MOD * D), F32)], axis=0)
    dmod_all, g_b_mod = mod_grad_rows(all_gather8(dmod_loc, "ag8_dmod"), "dmod_rows")
    dcol = lax.dynamic_slice(dmod_all, (0, 0, chip * wm), (DEPTH, 32, wm))
    g_w_mod = wmod_dw(sc, dcol, "wmod_dw")
    g_cctx = cctx_grad(cctx_dx(dcol[:, 16:32], w_mod, "cctx_dx"), c_ctx[None, :], "cctx_grad")

    g_ln_g =jnp.stack([jnp.stack([dln[(l, j)][3, 1] for j in range(3)]) for l in range(DEPTH)])
    g_ln_b = jnp.stack([jnp.stack([dln[(l, j)][3, 0] for j in range(3)]) for l in range(DEPTH)])
    sink_row = jnp.sum(g_sink, axis=0)[:4]
    g_sink8 = jnp.concatenate([sink_row[:, 0], sink_row[:, HEAD_DIM]])
    misc = jnp.concatenate([g_sink8, jnp.sum(g_ps, axis=0).reshape(POOL_W), jnp.zeros((D - 8 - POOL_W,), F32)])
    small_g = jnp.concatenate([
        g_ln_g.reshape(6, D), g_ln_b.reshape(6, D), jnp.sum(g_cw, axis=0), jnp.sum(g_cb, axis=0), g_vec,
        misc[None, :], jnp.sum(g_pw, axis=0).reshape(64, D), g_wa.reshape(256, D), g_wx.reshape(256, D), g_cctx,
        jnp.zeros((39, D), F32)], axis=0)
    n_small = small_g.shape[0] // N_CHIP
    last = pending.pop()
    ffn_red = reduce_scatter_chips(last["buf"], f"ffn{last['key']}", wire=BF16, dst=ffn_red[0], g=last["key"],
                                   recv=last.get("recv")).reshape(12 * ns, D)
    small_red = reduce_scatter_chips(small_g.reshape(N_CHIP, 2, n_small // 2, D), "small")
    small_red = all_gather_chips(small_red, "ag_smallg").reshape(N_CHIP * n_small, D)

    ffn_kind = dict(ffn_w_gate=0, ffn_w_up=1, ffn_w_down=2)

    def cols(a):
        return lax.dynamic_slice_in_dim(a, chip * dsh, dsh, axis=a.ndim - 1)

    sr = small_red
    grads = dict(
        c_ctx=sr[600], w_mod=g_w_mod, b_mod=g_b_mod,
        ln_g=cols(sr[0:6]).reshape(2, 3, dsh), ln_b=cols(sr[6:12]).reshape(2, 3, dsh),
        mix_ab_w_in=mix_red["a"][0:n_ai][None], attn_sink=sr[23, 0:8][None], pool_w=sr[24:88].reshape(1, 4, 128, 128),
        pool_scale=sr[23, 8:8 + POOL_W][None], mix_ab_w_out=mix_red["a"][n_ai:][None],
        lru_w_in=mix_red["c"][0:n_li].T[None],
        lru_conv_w=cols(sr[12:16])[None], lru_conv_b=cols(sr[16:17]), lru_wa=sr[88:344].reshape(1, 2, 8, 128, 128),
        lru_ba=cols(sr[17:19])[None], lru_wx=sr[344:600].reshape(1, 2, 8, 128, 128), lru_bx=cols(sr[19:21])[None],
        lru_lambda=cols(sr[21:23])[None], lru_w_out=mix_red["c"][n_li:][None])
    params = dict(c_ctx=(c_ctx, m_c_ctx, v_c_ctx), w_mod=(w_mod, m_w_mod, v_w_mod), b_mod=(b_mod, m_b_mod, v_b_mod),
                  ln_g=(ln_g, m_ln_g, v_ln_g), ln_b=(ln_b, m_ln_b, v_ln_b),
                  ffn_w_gate=(ffn_w_gate, m_ffn_w_gate, v_ffn_w_gate), ffn_w_up=(ffn_w_up, m_ffn_w_up, v_ffn_w_up),
                  ffn_w_down=(ffn_w_down, m_ffn_w_down, v_ffn_w_down),
                  mix_ab_w_in=(mix_ab_w_in, m_mix_ab_w_in, v_mix_ab_w_in), attn_sink=(attn_sink, m_attn_sink, v_attn_sink),
                  pool_w=(pool_w, m_pool_w, v_pool_w), pool_scale=(pool_scale, m_pool_scale, v_pool_scale),
                  mix_ab_w_out=(mix_ab_w_out, m_mix_ab_w_out, v_mix_ab_w_out), lru_w_in=(lru_w_in, m_lru_w_in, v_lru_w_in),
                  lru_conv_w=(lru_conv_w, m_lru_conv_w, v_lru_conv_w), lru_conv_b=(lru_conv_b, m_lru_conv_b, v_lru_conv_b),
                  lru_wa=(lru_wa, m_lru_wa, v_lru_wa), lru_ba=(lru_ba, m_lru_ba, v_lru_ba), lru_wx=(lru_wx, m_lru_wx, v_lru_wx),
                  lru_bx=(lru_bx, m_lru_bx, v_lru_bx), lru_lambda=(lru_lambda, m_lru_lambda, v_lru_lambda),
                  lru_w_out=(lru_w_out, m_lru_w_out, v_lru_w_out))
    gl, dl, ml, vl = [], [], [], []
    transposed = ("ffn_w_gate", "ffn_w_up", "mix_ab_w_in")
    for name, (w, m, v) in params.items():
        if name in transposed:
            w, m, v = (jnp.swapaxes(t, -1, -2) for t in (w, m, v))
        if name in ffn_kind:
            g, d, mn, vn = adamw_ffn(w, m, v, ffn_red, ffn_kind[name], ns, f"adamw_{name}")
        else:
            g = grads[name].reshape(w.shape)
            d, mn, vn = adamw(w, g, m, v, f"adamw_{name}")
        if name in transposed:
            g, d, mn, vn = (jnp.swapaxes(t, -1, -2) for t in (g, d, mn, vn))
        gl.append(g)
        dl.append(d)
        ml.append(mn)
        vl.append(vn)
    return (loss, grad_x, *gl, *dl, *ml, *vl)
```

```python
import functools
import math

import jax
import jax.numpy as jnp
from jax import lax
from jax.experimental import pallas as pl
from jax.experimental.pallas import tpu as pltpu

F32, BF16 = jnp.float32, jnp.bfloat16
MESH = pl.DeviceIdType.MESH
ANY = pl.BlockSpec(memory_space=pl.ANY)
VMEM_SPEC = pl.BlockSpec(memory_space=pltpu.VMEM)

D = 1024
N_CHIP = 4
HEAD_DIM, N_HEADS, N_KV = 64, 8, 2
ATT_W, KV_W, POOL_W = 512, 128, 512
POOL_WINDOWS = (2, 4, 8, 16)
BLK = 128
ATT_SCALE = HEAD_DIM ** -0.5
ROPE_THETA = 10000.0
GRID_W = 64
LRU_C = 8.0
LN_EPS = 1e-5
NEG_INF = -1e30
DEPTH = 2
ALPHA = (2 * DEPTH) ** 0.25
N_MOD = 9
ADAM_LR, ADAM_B1, ADAM_B2, ADAM_EPS, ADAM_WD, ADAM_STEP = 0.001, 0.9, 0.999, 1e-08, 0.01, 10
VMEM_BIG = 48 * 1024 * 1024


def _cp(sem=None, vmem=None):
    kw = {}
    if sem is not None:
        kw["dimension_semantics"] = sem
    if vmem is not None:
        kw["vmem_limit_bytes"] = vmem
    return pltpu.CompilerParams(**kw)


def _sds(shape, dtype):
    return jax.ShapeDtypeStruct(tuple(shape), dtype)


def _pick(n, cands):
    for c in cands:
        if n % c == 0:
            return c
    return n


def _dot(a, b, dims):
    return lax.dot_general(a, b, (dims, ((), ())), preferred_element_type=F32)


def _nn(a, b):
    return _dot(a, b, ((1,), (0,)))


def _nt(a, b):
    return _dot(a, b, ((1,), (1,)))


def _tn(a, b):
    return _dot(a, b, ((0,), (0,)))


def _sigmoid(x):
    return 0.5 * jnp.tanh(0.5 * x) + 0.5


def _me():
    return lax.axis_index("x"), lax.axis_index("y"), lax.axis_index("c")


def _rcopy(src, dst, ssem, rsem, dev):
    return pltpu.make_async_remote_copy(src_ref=src, dst_ref=dst, send_sem=ssem, recv_sem=rsem,
                                        device_id=dev, device_id_type=MESH)


def all_gather8(x, name):
    def body(x_ref, o_ref, ssem, rsem, lsem):
        mx, my, mc = _me()
        me = 4 * mx + 2 * my + mc
        loc = pltpu.make_async_copy(x_ref, o_ref.at[me], lsem)
        loc.start()
        peers = []
        for m in range(1, 8):
            px = 1 - mx if (m >> 2) & 1 else mx
            py = 1 - my if (m >> 1) & 1 else my
            pc = 1 - mc if m & 1 else mc
            peers.append((px, py, pc))
        sends = [_rcopy(x_ref, o_ref.at[me], ssem.at[k], rsem.at[k], p) for k, p in enumerate(peers)]
        for cp in sends:
            cp.start()
        for k, (px, py, pc) in enumerate(peers):
            _rcopy(x_ref, o_ref.at[4 * px + 2 * py + pc], ssem.at[k], rsem.at[k], (px, py, pc)).wait_recv()
        for cp in sends:
            cp.wait_send()
        loc.wait()

    return pl.pallas_call(
        body, name=name, out_shape=_sds((8,) + x.shape, x.dtype),
        in_specs=[VMEM_SPEC], out_specs=VMEM_SPEC,
        scratch_shapes=[pltpu.SemaphoreType.DMA((7,)), pltpu.SemaphoreType.DMA((7,)), pltpu.SemaphoreType.DMA],
    )(x)


_ROW_BLOCKS = (512, 384, 352, 256, 224, 128)


def _idx(v):
    return jnp.reshape(v, (1,)).astype(jnp.int32)


def place_slab(shard, name):
    _, h, w = shard.shape
    th = _pick(h, _ROW_BLOCKS)

    def body(s_ref, x_ref, o_ref):
        del s_ref
        o_ref[...] = x_ref[...]

    return pl.pallas_call(
        body, name=name, out_shape=_sds((N_CHIP,) + shard.shape, shard.dtype),
        grid_spec=pltpu.PrefetchScalarGridSpec(
            num_scalar_prefetch=1, grid=(2, h // th),
            in_specs=[pl.BlockSpec((None, th, w), lambda k, r, s: (k, r, 0))],
            out_specs=pl.BlockSpec((None, None, th, w), lambda k, r, s: (s[0], k, r, 0))),
    )(_idx(2 * lax.axis_index("x") + lax.axis_index("y")), shard)


def place_rows(buf, rows, r0, name):
    e, w = rows.shape
    tb = 64

    def body(s_ref, x_ref, b_ref, o_ref):
        del s_ref, b_ref
        o_ref[...] = x_ref[...]

    return pl.pallas_call(
        body, name=name, out_shape=_sds(buf.shape, buf.dtype),
        grid_spec=pltpu.PrefetchScalarGridSpec(
            num_scalar_prefetch=1, grid=(e // tb,),
            in_specs=[pl.BlockSpec((tb, w), lambda j, s: (j, 0)), ANY],
            out_specs=pl.BlockSpec((None, tb, w), lambda j, s: (s[0], r0 // tb + j, 0))),
        input_output_aliases={2: 0},
    )(_idx(2 * lax.axis_index("x") + lax.axis_index("y")), rows, buf)


def ffn_place(w_gate_t, w_up_t, w_down, l, f, name, extra=0):
    ns = w_down.shape[-2]
    tr, nb = ns // 2, 2

    def body(s_ref, g_ref, u_ref, d_ref, o_ref):
        del s_ref
        k = pl.program_id(0)

        @pl.when(k == 0)
        def _():
            o_ref[...] = g_ref[...].astype(BF16)

        @pl.when(k == 1)
        def _():
            o_ref[...] = u_ref[...].astype(BF16)

        @pl.when(k == 2)
        def _():
            o_ref[...] = d_ref[...].astype(BF16)

    def spec(q):
        return pl.BlockSpec((None, None, tr, D), lambda k, j, s: (l, f, jnp.where(k == q, j, 0), 0))

    return pl.pallas_call(
        body, name=name, out_shape=_sds((N_CHIP, 3 * ns + extra, D), BF16),
        grid_spec=pltpu.PrefetchScalarGridSpec(
            num_scalar_prefetch=1, grid=(3, nb), in_specs=[spec(0), spec(1), spec(2)],
            out_specs=pl.BlockSpec((None, tr, D), lambda k, j, s: (s[0], k * nb + j, 0))),
    )(_idx(2 * lax.axis_index("x") + lax.axis_index("y")), w_gate_t, w_up_t, w_down)


def all_gather_chips(shard, name):
    return gather_placed(place_slab(shard, name + "_place"), name)


def gather_placed(full, name):
    h = full.shape[2]
    lo, hi = pl.ds(0, h // 2), pl.ds(h // 2, h - h // 2)

    def body(x_ref, o_ref, ssem, rsem):
        del x_ref
        mx, my, mc = _me()
        s, xs, ys, ds = 2 * mx + my, 2 * (1 - mx) + my, 2 * mx + (1 - my), 2 * (1 - mx) + (1 - my)
        xn, yn, sib = (1 - mx, my, mc), (mx, 1 - my, mc), (mx, my, 1 - mc)

        def cp(k, src, dst, dev):
            return _rcopy(src, dst, ssem.at[k], rsem.at[k], dev)

        own = o_ref.at[s, mc]
        sent = [cp(0, own, own, xn), cp(1, own, own, yn)]
        for c in sent:
            c.start()
        cp(0, own, o_ref.at[xs, mc], xn).wait_recv()
        sent += [cp(2, o_ref.at[xs, mc, lo], o_ref.at[xs, mc, lo], yn), cp(4, o_ref.at[xs, mc], o_ref.at[xs, mc], sib)]
        sent[-2].start()
        sent[-1].start()
        cp(1, own, o_ref.at[ys, mc], yn).wait_recv()
        sent += [cp(3, o_ref.at[ys, mc, hi], o_ref.at[ys, mc, hi], xn), cp(5, o_ref.at[ys, mc], o_ref.at[ys, mc], sib)]
        sent[-2].start()
        sent[-1].start()
        cp(2, own, o_ref.at[ds, mc, lo], yn).wait_recv()
        cp(3, own, o_ref.at[ds, mc, hi], xn).wait_recv()
        sent.append(cp(6, o_ref.at[ds, mc], o_ref.at[ds, mc], sib))
        sent[-1].start()
        for k, slot in ((4, xs), (5, ys), (6, ds)):
            cp(k, own, o_ref.at[slot, 1 - mc], sib).wait_recv()
        for c in sent:
            c.wait_send()

    return pl.pallas_call(
        body, name=name, out_shape=_sds(full.shape, full.dtype), in_specs=[ANY], out_specs=ANY,
        input_output_aliases={0: 0},
        scratch_shapes=[pltpu.SemaphoreType.DMA((7,)), pltpu.SemaphoreType.DMA((7,))],
    )(full)


def sibling_send_other_half(buf, name):
    def body(x_ref, o_ref, ssem, rsem):
        mx, my, mc = _me()
        sib = (mx, my, 1 - mc)
        cps = [_rcopy(x_ref.at[k, 1 - mc], o_ref.at[k], ssem.at[k], rsem.at[k], sib) for k in range(N_CHIP)]
        for cp in cps:
            cp.start()
        for cp in cps:
            cp.wait_recv()
        for cp in cps:
            cp.wait_send()

    n, _, h, w = buf.shape
    return pl.pallas_call(
        body, name=name, out_shape=_sds((n, h, w), buf.dtype), in_specs=[ANY], out_specs=ANY,
        scratch_shapes=[pltpu.SemaphoreType.DMA((N_CHIP,)), pltpu.SemaphoreType.DMA((N_CHIP,))],
    )(buf)


def chips_all_to_all(q, name):
    def body(x_ref, o_ref, ssem, rsem):
        mx, my, mc = _me()
        s = 2 * mx + my
        chips = [(1 - mx, my), (mx, 1 - my), (1 - mx, 1 - my)]
        cps = [_rcopy(x_ref.at[2 * px + py], o_ref.at[s], ssem.at[j], rsem.at[j], (px, py, mc))
               for j, (px, py) in enumerate(chips)]
        for cp in cps:
            cp.start()
        for j, (px, py) in enumerate(chips):
            ps = 2 * px + py
            _rcopy(x_ref.at[ps], o_ref.at[ps], ssem.at[j], rsem.at[j], (px, py, mc)).wait_recv()
        for cp in cps:
            cp.wait_send()

    return pl.pallas_call(
        body, name=name, out_shape=_sds(q.shape, q.dtype), in_specs=[ANY], out_specs=ANY,
        scratch_shapes=[pltpu.SemaphoreType.DMA((3,)), pltpu.SemaphoreType.DMA((3,))],
    )(q)


def sibling_join_halves(both, name, g=None):
    def body(x_ref, o_ref, ssem, rsem):
        del x_ref
        mx, my, mc = _me()
        sib = (mx, my, 1 - mc)
        o = o_ref if g is None else o_ref.at[g]
        cp = _rcopy(o.at[mc], o.at[mc], ssem, rsem, sib)
        cp.start()
        _rcopy(o.at[1 - mc], o.at[1 - mc], ssem, rsem, sib).wait_recv()
        cp.wait_send()

    return pl.pallas_call(
        body, name=name, out_shape=_sds(both.shape, both.dtype), in_specs=[ANY], out_specs=ANY,
        input_output_aliases={0: 0}, scratch_shapes=[pltpu.SemaphoreType.DMA, pltpu.SemaphoreType.DMA],
    )(both)


def add_own_half(buf, recv, wire, name):
    n, _, h, w = buf.shape
    th = _pick(h, _ROW_BLOCKS)

    def body(c_ref, a_ref, b_ref, o_ref):
        del c_ref
        o_ref[...] = (a_ref[...] + b_ref[...]).astype(o_ref.dtype)

    return pl.pallas_call(
        body, name=name, out_shape=_sds((n, h, w), wire),
        grid_spec=pltpu.PrefetchScalarGridSpec(
            num_scalar_prefetch=1, grid=(n, h // th),
            in_specs=[pl.BlockSpec((None, None, th, w), lambda k, r, c: (k, c[0], r, 0)),
                      pl.BlockSpec((None, th, w), lambda k, r, c: (k, r, 0))],
            out_specs=pl.BlockSpec((None, th, w), lambda k, r, c: (k, r, 0))),
    )(_idx(lax.axis_index("c")), buf, recv)


def sum_slots(q, r, name, dst=None, g=None):
    n, h, w = r.shape
    th = _pick(h, _ROW_BLOCKS)

    def body(i_ref, q_ref, r1, r2, r3, *rest):
        del i_ref
        rest[-1][...] = ((q_ref[...].astype(F32) + r1[...].astype(F32)) + r2[...].astype(F32)) + r3[...].astype(F32)

    def slot(d):
        return lambda i, ix: ((ix[0] + d) % N_CHIP, i, 0)

    idx = jnp.stack([2 * lax.axis_index("x") + lax.axis_index("y"), lax.axis_index("c")]).astype(jnp.int32)
    in_specs = [pl.BlockSpec((None, th, w), slot(d)) for d in (0, 1, 2, 3)]
    if dst is None:
        return pl.pallas_call(
            body, name=name, out_shape=_sds((2, h, w), F32),
            grid_spec=pltpu.PrefetchScalarGridSpec(
                num_scalar_prefetch=1, grid=(h // th,), in_specs=in_specs,
                out_specs=pl.BlockSpec((None, th, w), lambda i, ix: (ix[1], i, 0))),
        )(idx, q, r, r, r)
    return pl.pallas_call(
        body, name=name, out_shape=_sds(dst.shape, F32),
        grid_spec=pltpu.PrefetchScalarGridSpec(
            num_scalar_prefetch=1, grid=(h // th,), in_specs=in_specs + [ANY],
            out_specs=pl.BlockSpec((None, None, th, w), lambda i, ix: (g, ix[1], i, 0))),
        input_output_aliases={5: 0},
    )(idx, q, r, r, r, dst)


def reduce_scatter_chips(buf, tag, wire=F32, dst=None, g=None, recv=None):
    if recv is None:
        recv = sibling_send_other_half(buf, f"rs_sib_{tag}")
    q = add_own_half(buf, recv, wire, f"rs_add2_{tag}")
    r = chips_all_to_all(q, f"rs_a2a_{tag}")
    red = sum_slots(q, r, f"rs_add4_{tag}", dst=dst, g=g)
    return sibling_join_halves(red, f"rs_join_{tag}", g=g)


class Layout:
    def __init__(self, n_ctx, n_lat):
        self.C, self.L = n_ctx, n_lat
        self.PS = n_ctx + n_lat
        self.T = 2 * self.PS
        self.tr = _pick(math.gcd(n_ctx, n_lat), (256, 128))
        self.bps = self.PS // self.tr
        self.cb = n_ctx // self.tr
        self.nblk = self.T // self.tr
        self.tm = _pick(self.T, (1152, 768, 512, 256, 128))
        self.tm2 = _pick(self.T, (2304, 1152, 768, 512, 256, 128))
        self.tc = _pick(self.T, (512, 256, 128))

    def seg(self, i):
        return jnp.where(i % self.bps < self.cb, 2, i // self.bps)


def rowwise(lay, name, fn, rows, segs=(), vecs=(), outs=(), sums=(), rider=None):
    tr, nblk = lay.tr, lay.nblk
    n_r, n_s, n_v, n_o = len(rows), len(segs), len(vecs), len(outs)
    lat_only = any(o[2:] for o in outs) or any(a.shape[0] != lay.T for a in rows)
    nsub = 1 if lat_only else (3 if nblk % 3 == 0 else 2 if nblk % 2 == 0 else 1)
    tb = tr * nsub

    def body(*refs):
        ins = refs[:n_r + n_s + n_v]
        ors = refs[n_r + n_s + n_v:]
        for sub in range(nsub):
            rs = slice(sub * tr, (sub + 1) * tr)
            seg = lay.seg(pl.program_id(0) * nsub + sub)
            vals = [r[rs, :] for r in ins[:n_r]] + [r[seg] for r in ins[n_r:n_r + n_s]] + [r[...] for r in ins[n_r + n_s:]]
            res = fn(*vals)
            for k in range(n_o):
                ors[k][rs, :] = res[k].astype(ors[k].dtype)
            for k in range(len(sums)):
                ors[n_o + k][sub] = res[n_o + k]

    def all_rows(i):
        return (i, 0)

    def lat_rows(i):
        return ((i // lay.bps) * (lay.bps - lay.cb) + jnp.maximum(i % lay.bps - lay.cb, 0), 0)

    in_specs = [pl.BlockSpec((tb, a.shape[1]), all_rows if a.shape[0] == lay.T else lat_rows) for a in rows]
    in_specs += [pl.BlockSpec(a.shape, lambda i: (0, 0, 0)) for a in segs]
    in_specs += [pl.BlockSpec(a.shape, lambda i: (0, 0)) for a in vecs]
    out_shape = [_sds((2 * lay.L if o[2:] else lay.T, o[0]), o[1]) for o in outs]
    out_shape += [_sds((nblk, r, w), F32) for r, w in sums]
    out_specs = [pl.BlockSpec((tb, o[0]), lat_rows if o[2:] else all_rows) for o in outs]
    out_specs += [pl.BlockSpec((nsub, r, w), lambda i: (i, 0, 0)) for r, w in sums]
    sem = "arbitrary" if any(o[2:] for o in outs) else "parallel"
    if rider is None:
        return pl.pallas_call(body, name=name, out_shape=out_shape, grid=(nblk // nsub,), in_specs=in_specs,
                              out_specs=out_specs, compiler_params=_cp((sem,), VMEM_BIG))(*rows, *segs, *vecs)
    return _host_call(body, rider, name, (nblk // nsub,), in_specs, out_specs, out_shape, (*rows, *segs, *vecs), (sem,),
                      n_r + n_s + n_v, n_o + len(sums))


def modulate(lay, h, mod, k_shift, k_scale, name):
    def fn(hb, m):
        return (hb * (1.0 + m[k_scale:k_scale + 1]) + m[k_shift:k_shift + 1],)
    return rowwise(lay, name, fn, [h], segs=[mod], outs=[(D, BF16)])[0]


def resid_ln(lay, h, y, mod, k_gate, coef, lnv, name, nxt=None, prev_ln=None, rider=None):
    def fn(hb, yb, m, *rest):
        ln = rest[-1]
        if prev_ln is not None:
            hb = hb * rest[-2][0:1] + rest[-2][1:2]
        z = ALPHA * hb + (coef * m[k_gate:k_gate + 1]) * yb
        mu = jnp.mean(z, axis=-1, keepdims=True)
        zc = z - mu
        var = jnp.mean(zc * zc, axis=-1, keepdims=True)
        rstd = lax.rsqrt(var + LN_EPS)
        xhat = zc * rstd
        out = xhat * ln[0:1] + ln[1:2]
        if nxt is None:
            return xhat, rstd
        mn = rest[0]
        return xhat, rstd, out * (1.0 + mn[nxt[2]:nxt[2] + 1]) + mn[nxt[1]:nxt[1] + 1]
    segs = [mod] if nxt is None else [mod, nxt[0]]
    vecs = [lnv] if prev_ln is None else [prev_ln, lnv]
    outs = [(D, F32), (1, F32)] + ([] if nxt is None else [(D, BF16)])
    return rowwise(lay, name, fn, [h, y], segs=segs, vecs=vecs, outs=outs, rider=rider)


def _ln_bwd_math(do, xh, rs, yb, gate, coef, ln):
    dxh = do * ln[0:1]
    m1 = jnp.mean(dxh, axis=-1, keepdims=True)
    m2 = jnp.mean(dxh * xh, axis=-1, keepdims=True)
    dz = rs * (dxh - m1 - xh * m2)
    s = jnp.concatenate([jnp.sum(do, axis=0, keepdims=True), jnp.sum(do * xh, axis=0, keepdims=True),
                         jnp.sum(coef * dz * yb, axis=0, keepdims=True)], axis=0)
    return (coef * gate) * dz, ALPHA * dz, s


def _mod_bwd_math(dr, dm, hb, scale):
    s = jnp.concatenate([jnp.sum(dm, axis=0, keepdims=True), jnp.sum(dm * hb, axis=0, keepdims=True)], axis=0)
    return dr + dm * (1.0 + scale), s


def mod_bwd(lay, dres, dhm, h, mod, k_scale, name, rider=None):
    def fn(dr, dm, hb, m):
        return _mod_bwd_math(dr, dm, hb, m[k_scale:k_scale + 1])
    return rowwise(lay, name, fn, [dres, dhm, h], segs=[mod], outs=[(D, F32, "lat")], sums=[(2, D)], rider=rider)


def modb_lnb(lay, dres, dhm, mod, k_scale, xhat, rstd, y, mod_p, k_gate, coef, lnv, name, rider=None):
    def fn(dr, dm, xh, rs, yb, m, mp, ln):
        dh, s2 = _mod_bwd_math(dr, dm, xh * ln[0:1] + ln[1:2], m[k_scale:k_scale + 1])
        dy, dres_p, s1 = _ln_bwd_math(dh, xh, rs, yb, mp[k_gate:k_gate + 1], coef, ln)
        return dy, dres_p, s1, s2
    return rowwise(lay, name, fn, [dres, dhm, xhat, rstd, y], segs=[mod, mod_p], vecs=[lnv],
                   outs=[(D, BF16), (D, F32)], sums=[(3, D), (2, D)], rider=rider)


def block_sums(lay, parts_list, name):
    n = len(parts_list)

    def body(*refs):
        for p_ref, o_ref in zip(refs[:n], refs[n:]):
            acc = [None, None, None]
            for i in range(lay.nblk):
                sg = 2 if i % lay.bps < lay.cb else i // lay.bps
                acc[sg] = p_ref[i] if acc[sg] is None else acc[sg] + p_ref[i]
            for k in range(3):
                o_ref[k] = acc[k]
            o_ref[3] = (acc[0] + acc[1]) + acc[2]

    return pl.pallas_call(body, name=name, out_shape=[_sds((4,) + p.shape[1:], F32) for p in parts_list],
                          in_specs=[VMEM_SPEC] * n, out_specs=[VMEM_SPEC] * n)(*parts_list)


def mm_nn(a, b, name, out_dtype=F32):
    m, k = a.shape
    n = b.shape[1]
    tm = _pick(m, (1152, 768, 512, 256, 128, 64, 32, 16, 8))
    tn = _pick(n, (1024, 768, 640, 512, 384, 256, 128))

    def body(a_ref, b_ref, o_ref):
        o_ref[...] = _nn(a_ref[...].astype(BF16), b_ref[...].astype(BF16)).astype(o_ref.dtype)

    return pl.pallas_call(body, name=name, out_shape=_sds((m, n), out_dtype), grid=(m // tm, n // tn),
                          in_specs=[pl.BlockSpec((tm, k), lambda i, j: (i, 0)), pl.BlockSpec((k, tn), lambda i, j: (0, j))],
                          out_specs=pl.BlockSpec((tm, tn), lambda i, j: (i, j)),
                          compiler_params=_cp(("parallel", "parallel"), VMEM_BIG))(a, b)


def mm_nt(a, b, name, out_dtype=F32, rider=None):
    m, k = a.shape
    n = b.shape[0]
    tm = _pick(m, (1152, 768, 512, 256, 128, 64, 32, 16, 8))
    tn = _pick(n, (1024, 768, 640, 512, 384, 256, 128))

    def body(a_ref, b_ref, o_ref):
        o_ref[...] = _nt(a_ref[...].astype(BF16), b_ref[...].astype(BF16)).astype(o_ref.dtype)

    res = _host_call(body, rider, name, (m // tm, n // tn),
                     [pl.BlockSpec((tm, k), lambda i, j: (i, 0)), pl.BlockSpec((tn, k), lambda i, j: (j, 0))],
                     [pl.BlockSpec((tm, tn), lambda i, j: (i, j))], [_sds((m, n), out_dtype)], (a, b),
                     ("parallel", "parallel"), 2, 1)
    return res[0] if rider is None else res


def mm_tn(a, b, name, rider=None):
    t, m = a.shape
    n = b.shape[1]
    tk = _pick(t, (1152, 768, 512, 256, 128, 64, 32, 16))
    tm = _pick(m, (512, 384, 256, 128))

    def body(a_ref, b_ref, o_ref):
        @pl.when(pl.program_id(1) == 0)
        def _():
            o_ref[...] = jnp.zeros_like(o_ref)
        o_ref[...] += _tn(a_ref[...].astype(BF16), b_ref[...].astype(BF16))

    res = _host_call(body, rider, name, (m // tm, t // tk),
                     [pl.BlockSpec((tk, tm), lambda i, k: (k, i)), pl.BlockSpec((tk, n), lambda i, k: (k, 0))],
                     [pl.BlockSpec((tm, n), lambda i, k: (i, 0))], [_sds((m, n), F32)], (a, b),
                     ("parallel", "arbitrary"), 2, 1)
    return res[0] if rider is None else res


class Rider:
    def __init__(self, ins, outs, aliases, nsem, start, wait):
        self.ins, self.outs, self.aliases, self.nsem, self.start, self.wait = ins, outs, aliases, nsem, start, wait


def _chips_of(mx, my):
    return [(1 - mx, my), (mx, 1 - my), (1 - mx, 1 - my)]


def rider_gather_d2d(buf):
    def start(ins, outs, ssem, rsem):
        o = outs[0]
        mx, my, mc = _me()
        for j, (px, py) in enumerate(_chips_of(mx, my)):
            ps = 2 * px + py
            _rcopy(o.at[ps, mc], o.at[ps, mc], ssem.at[j], rsem.at[j], (mx, my, 1 - mc)).start()

    def wait(ins, outs, ssem, rsem):
        o = outs[0]
        mx, my, mc = _me()
        sib = (mx, my, 1 - mc)
        for j, (px, py) in enumerate(_chips_of(mx, my)):
            ps = 2 * px + py
            _rcopy(o.at[ps, 1 - mc], o.at[ps, 1 - mc], ssem.at[j], rsem.at[j], sib).wait_recv()
        for j, (px, py) in enumerate(_chips_of(mx, my)):
            ps = 2 * px + py
            _rcopy(o.at[ps, mc], o.at[ps, mc], ssem.at[j], rsem.at[j], sib).wait_send()

    return Rider([buf], [_sds(buf.shape, buf.dtype)], {0: 0}, 3, start, wait)


def rider_reduce_sib(buf):
    n, _, h, w = buf.shape

    def start(ins, outs, ssem, rsem):
        mx, my, mc = _me()
        for k in range(N_CHIP):
            _rcopy(ins[0].at[k, 1 - mc], outs[0].at[k], ssem.at[k], rsem.at[k], (mx, my, 1 - mc)).start()

    def wait(ins, outs, ssem, rsem):
        mx, my, mc = _me()
        for k in range(N_CHIP):
            _rcopy(ins[0].at[k, 1 - mc], outs[0].at[k], ssem.at[k], rsem.at[k], (mx, my, 1 - mc)).wait_recv()
        for k in range(N_CHIP):
            _rcopy(ins[0].at[k, 1 - mc], outs[0].at[k], ssem.at[k], rsem.at[k], (mx, my, 1 - mc)).wait_send()

    return Rider([buf], [_sds((n, h, w), buf.dtype)], {}, N_CHIP, start, wait)


def rider_gather_xy(buf):
    def peers():
        mx, my, mc = _me()
        return 2 * mx + my, mc, [(1 - mx, my), (mx, 1 - my)]

    def start(ins, outs, ssem, rsem):
        o = outs[0]
        s, mc, nb = peers()
        for j, (px, py) in enumerate(nb):
            _rcopy(o.at[s, mc], o.at[s, mc], ssem.at[j], rsem.at[j], (px, py, mc)).start()

    def wait(ins, outs, ssem, rsem):
        o = outs[0]
        s, mc, nb = peers()
        for j, (px, py) in enumerate(nb):
            _rcopy(o.at[2 * px + py, mc], o.at[2 * px + py, mc], ssem.at[j], rsem.at[j], (px, py, mc)).wait_recv()
        for j, (px, py) in enumerate(nb):
            _rcopy(o.at[s, mc], o.at[s, mc], ssem.at[j], rsem.at[j], (px, py, mc)).wait_send()

    return Rider([buf], [_sds(buf.shape, buf.dtype)], {0: 0}, 2, start, wait)


def rider_gather_fwd(buf):
    h2 = buf.shape[2] // 2
    lo, hi = pl.ds(0, h2), pl.ds(h2, buf.shape[2] - h2)

    def start(ins, outs, ssem, rsem):
        o = outs[0]
        mx, my, mc = _me()
        xs, ys = 2 * (1 - mx) + my, 2 * mx + (1 - my)
        _rcopy(o.at[xs, mc, lo], o.at[xs, mc, lo], ssem.at[0], rsem.at[0], (mx, 1 - my, mc)).start()
        _rcopy(o.at[ys, mc, hi], o.at[ys, mc, hi], ssem.at[1], rsem.at[1], (1 - mx, my, mc)).start()

    def wait(ins, outs, ssem, rsem):
        o = outs[0]
        mx, my, mc = _me()
        xs, ys, ds = 2 * (1 - mx) + my, 2 * mx + (1 - my), 2 * (1 - mx) + (1 - my)
        _rcopy(o.at[ds, mc, lo], o.at[ds, mc, lo], ssem.at[0], rsem.at[0], (mx, 1 - my, mc)).wait_recv()
        _rcopy(o.at[ds, mc, hi], o.at[ds, mc, hi], ssem.at[1], rsem.at[1], (1 - mx, my, mc)).wait_recv()
        _rcopy(o.at[xs, mc, lo], o.at[xs, mc, lo], ssem.at[0], rsem.at[0], (mx, 1 - my, mc)).wait_send()
        _rcopy(o.at[ys, mc, hi], o.at[ys, mc, hi], ssem.at[1], rsem.at[1], (1 - mx, my, mc)).wait_send()

    return Rider([buf], [_sds(buf.shape, buf.dtype)], {0: 0}, 2, start, wait)


def rider_reduce_copy(q, j, r=None):
    def peer():
        mx, my, mc = _me()
        px, py = _chips_of(mx, my)[j]
        return 2 * mx + my, 2 * px + py, (px, py, mc)

    def start(ins, outs, ssem, rsem):
        s, ps, dev = peer()
        _rcopy(ins[0].at[ps], outs[0].at[s], ssem.at[0], rsem.at[0], dev).start()

    def wait(ins, outs, ssem, rsem):
        s, ps, dev = peer()
        _rcopy(ins[0].at[ps], outs[0].at[ps], ssem.at[0], rsem.at[0], dev).wait_recv()
        _rcopy(ins[0].at[ps], outs[0].at[s], ssem.at[0], rsem.at[0], dev).wait_send()

    if r is None:
        return Rider([q], [_sds(q.shape, q.dtype)], {}, 1, start, wait)
    return Rider([q, r], [_sds(q.shape, q.dtype)], {1: 0}, 1, start, wait)


def rider_reduce_copies(q):
    def start(ins, outs, ssem, rsem):
        mx, my, mc = _me()
        s = 2 * mx + my
        for j, (px, py) in enumerate(_chips_of(mx, my)):
            _rcopy(ins[0].at[2 * px + py], outs[0].at[s], ssem.at[j], rsem.at[j], (px, py, mc)).start()

    def wait(ins, outs, ssem, rsem):
        mx, my, mc = _me()
        s = 2 * mx + my
        for j, (px, py) in enumerate(_chips_of(mx, my)):
            ps = 2 * px + py
            _rcopy(ins[0].at[ps], outs[0].at[ps], ssem.at[j], rsem.at[j], (px, py, mc)).wait_recv()
        for j, (px, py) in enumerate(_chips_of(mx, my)):
            _rcopy(ins[0].at[2 * px + py], outs[0].at[s], ssem.at[j], rsem.at[j], (px, py, mc)).wait_send()

    return Rider([q], [_sds(q.shape, q.dtype)], {}, 3, start, wait)


def rider_join(buf, g=None):
    def start(ins, outs, ssem, rsem):
        o = outs[0] if g is None else outs[0].at[g]
        mx, my, mc = _me()
        _rcopy(o.at[mc], o.at[mc], ssem.at[0], rsem.at[0], (mx, my, 1 - mc)).start()

    def wait(ins, outs, ssem, rsem):
        o = outs[0] if g is None else outs[0].at[g]
        mx, my, mc = _me()
        _rcopy(o.at[1 - mc], o.at[1 - mc], ssem.at[0], rsem.at[0], (mx, my, 1 - mc)).wait_recv()
        _rcopy(o.at[mc], o.at[mc], ssem.at[0], rsem.at[0], (mx, my, 1 - mc)).wait_send()

    return Rider([buf], [_sds(buf.shape, buf.dtype)], {0: 0}, 1, start, wait)


def _host_call(body, rider, name, grid, in_specs, out_specs, out_shape, operands, sem, n_in, n_out, aliases=None):
    aliases = dict(aliases or {})
    if rider is None:
        return pl.pallas_call(body, name=name, out_shape=out_shape, grid=grid, in_specs=in_specs, out_specs=out_specs,
                              input_output_aliases=aliases, compiler_params=_cp(sem, VMEM_BIG))(*operands)
    n_ri, n_ro = len(rider.ins), len(rider.outs)
    aliases.update({n_in + a: n_out + b for a, b in rider.aliases.items()})

    def hosted(*refs):
        ins, r_in = refs[:n_in], refs[n_in:n_in + n_ri]
        outs, r_out = refs[n_in + n_ri:n_in + n_ri + n_out], refs[n_in + n_ri + n_out:n_in + n_ri + n_out + n_ro]
        ssem, rsem = refs[-2], refs[-1]
        first = functools.reduce(lambda a, b: a & b, [pl.program_id(k) == 0 for k in range(len(grid))])
        last = functools.reduce(lambda a, b: a & b, [pl.program_id(k) == grid[k] - 1 for k in range(len(grid))])

        @pl.when(first)
        def _():
            rider.start(r_in, r_out, ssem, rsem)
        body(*ins, *outs)

        @pl.when(last)
        def _():
            rider.wait(r_in, r_out, ssem, rsem)

    return pl.pallas_call(
        hosted, name=name, out_shape=list(out_shape) + list(rider.outs), grid=grid,
        in_specs=list(in_specs) + [ANY] * n_ri, out_specs=list(out_specs) + [ANY] * n_ro,
        input_output_aliases=aliases,
        scratch_shapes=[pltpu.SemaphoreType.DMA((rider.nsem,)), pltpu.SemaphoreType.DMA((rider.nsem,))],
        compiler_params=_cp(("arbitrary",) * len(grid), VMEM_BIG))(*operands, *rider.ins)


def ffn_up(lay, hm, wbuf, ig, iu, ns, name, rider=None):
    tm = lay.tm

    def body(h_ref, wg_ref, wu_ref, up_ref, sl_ref, a_ref):
        hb = h_ref[...]
        g = _nt(hb, wg_ref[0])
        u = _nt(hb, wu_ref[0])
        sg = _sigmoid(g)
        sl = g * sg
        up_ref[0] = (u * (sg + sl * (1.0 - sg))).astype(BF16)
        sl_ref[0] = sl.astype(BF16)
        a_ref[0] = (sl * u).astype(BF16)

    spec_o = pl.BlockSpec((1, tm, ns), lambda s, i: (s, i, 0))
    return _host_call(
        body, rider, name, (N_CHIP, lay.T // tm),
        [pl.BlockSpec((tm, D), lambda s, i: (i, 0)), pl.BlockSpec((1, ns, D), lambda s, i: (s, ig, 0)),
         pl.BlockSpec((1, ns, D), lambda s, i: (s, iu, 0))],
        [spec_o] * 3, [_sds((N_CHIP, lay.T, ns), BF16)] * 3, (hm, wbuf, wbuf), ("parallel", "parallel"), 3, 3)


def slab_nn_acc(lay, zs, wbuf, idxs, ns, name, rider=None):
    tm = lay.tm2
    npair = len(zs)

    def body(*refs):
        o_ref = refs[-1]

        @pl.when(pl.program_id(1) == 0)
        def _():
            o_ref[...] = jnp.zeros_like(o_ref)
        acc = _nn(refs[0][0], refs[npair][0])
        for p in range(1, npair):
            acc += _nn(refs[p][0], refs[npair + p][0])
        o_ref[...] += acc

    in_specs = [pl.BlockSpec((1, tm, ns), lambda i, s: (s, i, 0)) for _ in zs]
    in_specs += [pl.BlockSpec((1, ns, D), functools.partial(lambda i, s, q: (s, q, 0), q=q)) for q in idxs]
    return _host_call(body, rider, name, (lay.T // tm, N_CHIP), in_specs, [pl.BlockSpec((tm, D), lambda i, s: (i, 0))],
                      [_sds((lay.T, D), F32)], (*zs, *([wbuf] * npair)), ("parallel", "arbitrary"), 2 * npair, 1)


def ffn_bwd_da(lay, dy, wbuf, idn, up, sl, ns, name, rider=None):
    tm = lay.tm

    def body(dy_ref, wd_ref, up_ref, sl_ref, dg_ref, du_ref):
        da = _nt(dy_ref[...], wd_ref[0])
        dg_ref[0] = (da * up_ref[0].astype(F32)).astype(BF16)
        du_ref[0] = (da * sl_ref[0].astype(F32)).astype(BF16)

    spec_z = pl.BlockSpec((1, tm, ns), lambda s, i: (s, i, 0))
    return _host_call(
        body, rider, name, (N_CHIP, lay.T // tm),
        [pl.BlockSpec((tm, D), lambda s, i: (i, 0)), pl.BlockSpec((1, ns, D), lambda s, i: (s, idn, 0)), spec_z, spec_z],
        [spec_z] * 2, [_sds((N_CHIP, lay.T, ns), BF16)] * 2, (dy, wbuf, up, sl), ("parallel", "parallel"), 4, 2)


def slab_tn(lay, z, x, gbuf, idx, ns, name, rider=None):
    tk = _pick(lay.T, (768, 512, 256, 128))

    def body(z_ref, x_ref, g_in, o_ref):
        del g_in

        @pl.when(pl.program_id(0) == 0)
        def _():
            o_ref[...] = jnp.zeros_like(o_ref)
        xv = x_ref[...]
        for s in range(N_CHIP):
            o_ref[s] += _tn(z_ref[s], xv)

    return _host_call(
        body, rider, name, (lay.T // tk,),
        [pl.BlockSpec((N_CHIP, tk, ns), lambda k: (0, k, 0)), pl.BlockSpec((tk, D), lambda k: (k, 0)), ANY],
        [pl.BlockSpec((N_CHIP, ns, D), lambda k: (0, idx, 0))], [_sds(gbuf.shape, F32)], (z, x, gbuf),
        ("arbitrary",), 3, 1, aliases={2: 0})


Q0, K0, V0, U0, QR0, KR0, PEXT = 0, 512, 640, 768, 1280, 1792, 1920


def rope_fwd(lay, p, cos, sin, name):
    def fn(pb, cs, sn):
        cs4 = jnp.concatenate([cs] * 4, axis=1)
        sn4 = jnp.concatenate([sn] * 4, axis=1)
        qr = pb[:, Q0:K0] * cs4 + pb[:, QR0:KR0] * sn4
        kr = pb[:, K0:V0] * cs + pb[:, KR0:PEXT] * sn
        return qr, kr, pb[:, V0:U0], pb[:, U0:QR0]
    return rowwise(lay, name, fn, [p, cos, sin], outs=[(ATT_W, BF16), (KV_W, BF16), (KV_W, BF16), (POOL_W, F32)])


def rope_bwd(lay, dqr, dkr, dv, du, cos, sin, name):
    def fn(dq, dk, dvb, dub, cs, sn):
        cs4 = jnp.concatenate([cs] * 4, axis=1)
        sn4 = jnp.concatenate([sn] * 4, axis=1)
        return (jnp.concatenate([dq * cs4, dk * cs, dvb, dub, dq * sn4, dk * sn], axis=1),)
    return rowwise(lay, name, fn, [dqr, dkr, dv, du, cos, sin], outs=[(PEXT, BF16)])[0]


def _attn_specs(lay):
    nbs, cbk, lbk = lay.PS // BLK, lay.C // BLK, lay.L // BLK

    def kv_map(j):
        return lambda s, n: (s * nbs + cbk + jnp.clip(n - cbk + j - 1, 0, lbk - 1), 0)

    win = [pl.BlockSpec((BLK, KV_W), kv_map(j)) for j in range(3)]
    ctx = pl.BlockSpec((lay.C, KV_W), lambda s, n: (s * (lay.PS // lay.C), 0))
    return nbs, cbk, lbk, win, ctx


def _attn_masks(n, cbk, lbk):
    row = lax.broadcasted_iota(jnp.int32, (BLK, BLK), 0)
    col = lax.broadcasted_iota(jnp.int32, (BLK, BLK), 1)
    m = n - cbk
    lat = n >= cbk
    valid = [lat & (m >= 1) & (col >= row), lat & (col >= 0), lat & (m <= lbk - 2) & (col <= row)]
    lane_lo = lax.broadcasted_iota(jnp.int32, (BLK, 2 * HEAD_DIM), 1) < HEAD_DIM
    return valid, lane_lo


def attn_fwd(lay, qr, kr, vb, sink_tab, name):
    nbs, cbk, lbk, win, ctx = _attn_specs(lay)

    def body(q_ref, k0, k1, k2, kc_ref, v0, v1, v2, vc_ref, sk_ref, o_ref, l_ref):
        n = pl.program_id(1)
        valid, lane_lo = _attn_masks(n, cbk, lbk)
        valid4 = [jnp.concatenate([v] * 4, axis=0) for v in valid]
        ks = [k0[...], k1[...], k2[...]]
        vs = [v0[...], v1[...], v2[...]]
        kc, vc = kc_ref[...], vc_ref[...]
        q2s = [q_ref[:, p * 128:(p + 1) * 128] for p in range(4)]
        outs, lses = [], []
        for hh in range(2):
            sel = lane_lo == (hh == 0)
            qm = jnp.concatenate([jnp.where(sel, q2, jnp.zeros_like(q2)) for q2 in q2s], axis=0)
            sk = jnp.concatenate([jnp.broadcast_to(sk_ref[p:p + 1, hh * HEAD_DIM:hh * HEAD_DIM + 1], (BLK, 1))
                                  for p in range(4)], axis=0)
            sw = [jnp.where(valid4[j], _nt(qm, ks[j]) * ATT_SCALE, NEG_INF) for j in range(3)]
            sc = _nt(qm, kc) * ATT_SCALE
            mx = jnp.maximum(jnp.maximum(jnp.maximum(sw[0].max(-1, keepdims=True), sw[1].max(-1, keepdims=True)),
                                         jnp.maximum(sw[2].max(-1, keepdims=True), sc.max(-1, keepdims=True))), sk)
            ew = [jnp.exp(s - mx) for s in sw]
            ec = jnp.exp(sc - mx)
            den = ew[0].sum(-1, keepdims=True) + ew[1].sum(-1, keepdims=True) + ew[2].sum(-1, keepdims=True)
            den = den + ec.sum(-1, keepdims=True) + jnp.exp(sk - mx)
            o = _nn((ec / den).astype(BF16), vc)
            for j in range(3):
                o += _nn((ew[j] / den).astype(BF16), vs[j])
            outs.append(o)
            lses.append(mx + jnp.log(den))
        for p in range(4):
            rows = slice(p * BLK, (p + 1) * BLK)
            o_ref[:, p * 128:(p + 1) * 128] = jnp.where(lane_lo, outs[0][rows], outs[1][rows]).astype(o_ref.dtype)
            l_ref[:, p * 128:(p + 1) * 128] = jnp.where(lane_lo, jnp.broadcast_to(lses[0][rows], (BLK, 128)),
                                                        jnp.broadcast_to(lses[1][rows], (BLK, 128)))

    qspec = pl.BlockSpec((BLK, ATT_W), lambda s, n: (s * nbs + n, 0))
    return pl.pallas_call(
        body, name=name, out_shape=[_sds((lay.T, ATT_W + POOL_W), BF16), _sds((lay.T, ATT_W), F32)], grid=(2, nbs),
        in_specs=[qspec] + win + [ctx] + win + [ctx] + [pl.BlockSpec((8, 128), lambda s, n: (0, 0))],
        out_specs=[qspec, qspec], compiler_params=_cp(("parallel", "parallel")))(qr, kr, kr, kr, kr, vb, vb, vb, vb, sink_tab)


def attn_bwd(lay, qr, kr, vb, sink_tab, lse, datt, name, rider=None):
    nbs, cbk, lbk, win, ctx = _attn_specs(lay)
    C, PS = lay.C, lay.PS

    def body(q_ref, k0, k1, k2, kc_ref, v0, v1, v2, vc_ref, sk_ref, l_ref, do_ref, dq_ref, dk_ref, dv_ref, ds_ref):
        n = pl.program_id(1)
        valid, lane_lo = _attn_masks(n, cbk, lbk)

        @pl.when(n == 0)
        def _():
            dk_ref[...] = jnp.zeros_like(dk_ref)
            dv_ref[...] = jnp.zeros_like(dv_ref)
            ds_ref[...] = jnp.zeros_like(ds_ref)

        ks = [k0[...], k1[...], k2[...], kc_ref[...]]
        vs = [v0[...], v1[...], v2[...], vc_ref[...]]
        valid4 = [jnp.concatenate([v] * 4, axis=0) for v in valid]
        dks = [jnp.zeros((BLK, KV_W), F32)] * 3 + [jnp.zeros((C, KV_W), F32)]
        dvs = list(dks)
        q2s = [q_ref[:, p * 128:(p + 1) * 128] for p in range(4)]
        do2s = [do_ref[:, p * 128:(p + 1) * 128].astype(BF16) for p in range(4)]
        lse2s = [l_ref[:, p * 128:(p + 1) * 128] for p in range(4)]
        dq_h, dd_h = [], []
        for hh in range(2):
            sel = lane_lo == (hh == 0)
            qm = jnp.concatenate([jnp.where(sel, q2, jnp.zeros_like(q2)) for q2 in q2s], axis=0)
            dom = jnp.concatenate([jnp.where(sel, d2, jnp.zeros_like(d2)) for d2 in do2s], axis=0)
            lse_h = jnp.concatenate([l2[:, hh * HEAD_DIM:hh * HEAD_DIM + 1] for l2 in lse2s], axis=0)
            ps, dps = [], []
            for j in range(4):
                s = _nt(qm, ks[j]) * ATT_SCALE
                if j < 3:
                    s = jnp.where(valid4[j], s, NEG_INF)
                ps.append(jnp.exp(s - lse_h))
                dps.append(_nt(dom, vs[j]))
            dd = (ps[0] * dps[0]).sum(-1, keepdims=True) + (ps[1] * dps[1]).sum(-1, keepdims=True)
            dd = dd + (ps[2] * dps[2]).sum(-1, keepdims=True) + (ps[3] * dps[3]).sum(-1, keepdims=True)
            dq = jnp.zeros((4 * BLK, 128), F32)
            for j in range(4):
                dsb = (ps[j] * (dps[j] - dd) * ATT_SCALE).astype(BF16)
                dq += _nn(dsb, ks[j])
                dks[j] = dks[j] + _tn(dsb, qm)
                dvs[j] = dvs[j] + _tn(ps[j].astype(BF16), dom)
            dq_h.append(dq)
            dd_h.append(dd)
        for p in range(4):
            sl = slice(p * 128, (p + 1) * 128)
            rows = slice(p * BLK, (p + 1) * BLK)
            dq_ref[:, sl] = jnp.where(lane_lo, dq_h[0][rows], dq_h[1][rows])
            dd2 = jnp.where(lane_lo, jnp.broadcast_to(dd_h[0][rows], (BLK, 128)), jnp.broadcast_to(dd_h[1][rows], (BLK, 128)))
            psink = jnp.exp(sk_ref[p:p + 1, :] - lse2s[p])
            ds_ref[0, p:p + 1, :] += -jnp.sum(psink * dd2, axis=0, keepdims=True)
        dk_ref[0:C, :] += dks[3]
        dv_ref[0:C, :] += dvs[3]
        for j in range(3):
            r0 = pl.multiple_of((cbk + jnp.clip(n - cbk + j - 1, 0, lbk - 1)) * BLK, BLK)
            dk_ref[pl.ds(r0, BLK), :] += dks[j]
            dv_ref[pl.ds(r0, BLK), :] += dvs[j]

    qspec = pl.BlockSpec((BLK, ATT_W), lambda s, n: (s * nbs + n, 0))
    kvout = pl.BlockSpec((PS, KV_W), lambda s, n: (s, 0))
    return _host_call(
        body, rider, name, (2, nbs),
        [qspec] + win + [ctx] + win + [ctx] + [pl.BlockSpec((8, 128), lambda s, n: (0, 0)), qspec, qspec],
        [qspec, kvout, kvout, pl.BlockSpec((1, 8, 128), lambda s, n: (s, 0, 0))],
        [_sds((lay.T, ATT_W), F32), _sds((lay.T, KV_W), F32), _sds((lay.T, KV_W), F32), _sds((2, 8, 128), F32)],
        (qr, kr, kr, kr, kr, vb, vb, vb, vb, sink_tab, lse, datt), ("parallel", "arbitrary"), 12, 4)


def _winsum(x, r):
    n = x.shape[0]
    t = lax.broadcasted_iota(jnp.int32, x.shape, 0)
    acc = x
    for o in range(1, r + 1):
        acc = acc + jnp.where(t >= o, pltpu.roll(x, o, 0), 0.0) + jnp.where(t < n - o, pltpu.roll(x, n - o, 0), 0.0)
    return acc


def _wincount(n, r):
    t = lax.broadcasted_iota(jnp.int32, (n, 128), 0)
    return (jnp.minimum(t + r, n - 1) - jnp.maximum(t - r, 0) + 1).astype(F32)


def pool_fwd(lay, u, w_pool, scale, cat, name):
    segs = [(0, lay.C), (lay.C, lay.L)]

    def body(u_ref, w_ref, s_ref, c_in, o_ref):
        del c_in
        for r0, n in segs:
            for g, wd in enumerate(POOL_WINDOWS):
                sl = slice(g * 128, (g + 1) * 128)
                x = u_ref[r0:r0 + n, sl]
                d = _winsum(x, wd // 2) / _wincount(n, wd // 2) - x
                y = _nn(d.astype(BF16), w_ref[g].astype(BF16)) * s_ref[:, sl]
                o_ref[r0:r0 + n, sl] = y.astype(o_ref.dtype)

    spec = pl.BlockSpec((lay.PS, POOL_W), lambda s: (s, 0))
    return pl.pallas_call(
        body, name=name, out_shape=_sds(cat.shape, BF16), grid=(2,),
        in_specs=[spec, pl.BlockSpec(w_pool.shape, lambda s: (0, 0, 0)), pl.BlockSpec((1, POOL_W), lambda s: (0, 0)), ANY],
        out_specs=pl.BlockSpec((lay.PS, POOL_W), lambda s: (s, 1)), input_output_aliases={3: 0},
        compiler_params=_cp(("parallel",), VMEM_BIG))(u, w_pool, scale, cat)


def pool_bwd(lay, u, dcat, w_pool, scale, name):
    segs = [(0, lay.C), (lay.C, lay.L)]

    def body(u_ref, dp_ref, w_ref, s_ref, du_ref, dw_ref, dsc_ref):
        for g, wd in enumerate(POOL_WINDOWS):
            sl = slice(g * 128, (g + 1) * 128)
            wb = w_ref[g].astype(BF16)
            dw = jnp.zeros((128, 128), F32)
            dsc = jnp.zeros((1, 128), F32)
            for r0, n in segs:
                x = u_ref[r0:r0 + n, sl]
                cnt = _wincount(n, wd // 2)
                d = (_winsum(x, wd // 2) / cnt - x).astype(BF16)
                dp = dp_ref[r0:r0 + n, sl]
                dsc += jnp.sum(_nn(d, wb) * dp, axis=0, keepdims=True)
                dyp = (dp * s_ref[:, sl]).astype(BF16)
                dw += _tn(d, dyp)
                dd = _nt(dyp, wb)
                du_ref[r0:r0 + n, sl] = _winsum(dd / cnt, wd // 2) - dd
            dw_ref[0, g] = dw
            dsc_ref[0, :, sl] = dsc

    spec = pl.BlockSpec((lay.PS, POOL_W), lambda s: (s, 0))
    return pl.pallas_call(
        body, name=name,
        out_shape=[_sds((lay.T, POOL_W), F32), _sds((2, 4, 128, 128), F32), _sds((2, 1, POOL_W), F32)], grid=(2,),
        in_specs=[spec, pl.BlockSpec((lay.PS, POOL_W), lambda s: (s, 1)), pl.BlockSpec(w_pool.shape, lambda s: (0, 0, 0)),
                  pl.BlockSpec((1, POOL_W), lambda s: (0, 0))],
        out_specs=[spec, pl.BlockSpec((1, 4, 128, 128), lambda s: (s, 0, 0, 0)), pl.BlockSpec((1, 1, POOL_W), lambda s: (s, 0, 0))],
        compiler_params=_cp(("parallel",), VMEM_BIG))(u, dcat, w_pool, scale)


CONV_OFFS = (-1, 0, 1, 2)
CW = 256


def _shift_rows(x, o):
    if o == 0:
        return x
    n = x.shape[0]
    t = lax.broadcasted_iota(jnp.int32, x.shape, 0)
    if o < 0:
        return jnp.where(t >= -o, pltpu.roll(x, -o, 0), 0.0)
    return jnp.where(t < n - o, pltpu.roll(x, n - o, 0), 0.0)


def conv_fwd(lay, p, col0, w, b, name):
    segs = [(0, lay.C), (lay.C, lay.L)]
    cb0 = col0 // CW

    def body(x_ref, w_ref, b_ref, o_ref):
        for r0, n in segs:
            x = x_ref[r0:r0 + n, :]
            y = jnp.broadcast_to(b_ref[...], x.shape)
            for k, o in enumerate(CONV_OFFS):
                y = y + _shift_rows(x, o) * w_ref[k:k + 1, :]
            o_ref[r0:r0 + n, :] = y

    return pl.pallas_call(
        body, name=name, out_shape=_sds((lay.T, D), F32), grid=(2, D // CW),
        in_specs=[pl.BlockSpec((lay.PS, CW), lambda s, j: (s, cb0 + j)), pl.BlockSpec((4, CW), lambda s, j: (0, j)),
                  pl.BlockSpec((1, CW), lambda s, j: (0, j))],
        out_specs=pl.BlockSpec((lay.PS, CW), lambda s, j: (s, j)),
        compiler_params=_cp(("parallel", "parallel")))(p, w, b)


def conv_bwd(lay, p, col0, w, duc, dp, name):
    segs = [(0, lay.C), (lay.C, lay.L)]
    cb0 = col0 // CW

    def body(x_ref, w_ref, g_ref, dp_in, du_ref, dw_ref, db_ref):
        del dp_in
        dws =[jnp.zeros((1, CW), F32)] * 4
        db = jnp.zeros((1, CW), F32)
        for r0, n in segs:
            x = x_ref[r0:r0 + n, :]
            g = g_ref[r0:r0 + n, :]
            du = jnp.zeros_like(g)
            for k, o in enumerate(CONV_OFFS):
                du = du + _shift_rows(g, -o) * w_ref[k:k + 1, :]
                dws[k] = dws[k] + jnp.sum(g * _shift_rows(x, o), axis=0, keepdims=True)
            db = db + jnp.sum(g, axis=0, keepdims=True)
            du_ref[r0:r0 + n, :] = du.astype(du_ref.dtype)
        dw_ref[0] = jnp.concatenate(dws, axis=0)
        db_ref[0] = db

    return pl.pallas_call(
        body, name=name, out_shape=[_sds(dp.shape, BF16), _sds((2, 4, D), F32), _sds((2, 1, D), F32)], grid=(2, D // CW),
        in_specs=[pl.BlockSpec((lay.PS, CW), lambda s, j: (s, cb0 + j)), pl.BlockSpec((4, CW), lambda s, j: (0, j)),
                  pl.BlockSpec((lay.PS, CW), lambda s, j: (s, j)), ANY],
        out_specs=[pl.BlockSpec((lay.PS, CW), lambda s, j: (s, D // CW + j)), pl.BlockSpec((1, 4, CW), lambda s, j: (s, 0, j)),
                   pl.BlockSpec((1, 1, CW), lambda s, j: (s, 0, j))],
        input_output_aliases={3: 0}, compiler_params=_cp(("parallel", "parallel")))(p, w, duc, dp)


def _softplus_neg(lam):
    z = -lam
    w = jnp.exp(-jnp.abs(z))
    log1p = jnp.where(w < 1e-2, w * (1.0 - w * (0.5 - w / 3.0)), jnp.log(1.0 + w))
    return jnp.maximum(z, 0.0) + log1p, -_sigmoid(z)


def _neg_expm1(x):
    series = -x * (1.0 + x * (0.5 + x * (1.0 / 6.0 + x * (1.0 / 24.0 + x * (1.0 / 120.0)))))
    return jnp.where(x > -0.05, series, 1.0 - jnp.exp(x))


def _lru_gates(x, xb, wa, wx, ba, bx, lam):
    r = _sigmoid(_nn(xb, wa.astype(BF16)) + ba)
    gi = _sigmoid(_nn(xb, wx.astype(BF16)) + bx)
    sp, dsp = _softplus_neg(lam)
    la = -LRU_C * r * sp
    a = jnp.exp(la)
    sq = jnp.sqrt(_neg_expm1(2.0 * la))
    return r, gi, sp, dsp, a, sq


def lru_coeffs(lay, uc, wa, wx, vec, name):
    tr = lay.tc

    def body(x_ref, wa_ref, wx_ref, v_ref, a_ref, b_ref):
        for h in range(8):
            sl = slice(h * 128, (h + 1) * 128)
            x = x_ref[:, sl]
            xb = x.astype(BF16)
            for d in range(2):
                _, gi, _, _, a, sq = _lru_gates(x, xb, wa_ref[d, h], wx_ref[d, h], v_ref[d:d + 1, sl],
                                                v_ref[2 + d:3 + d, sl], v_ref[4 + d:5 + d, sl])
                a_ref[d, h] = a
                b_ref[d, h] = sq * (gi * x)

    wspec = pl.BlockSpec((2, 8, 128, 128), lambda i: (0, 0, 0, 0))
    ospec = pl.BlockSpec((2, 8, tr, 128), lambda i: (0, 0, i, 0))
    return pl.pallas_call(
        body, name=name, out_shape=[_sds((2, 8, lay.T, 128), F32)] * 2, grid=(lay.T // tr,),
        in_specs=[pl.BlockSpec((tr, D), lambda i: (i, 0)), wspec, wspec, pl.BlockSpec((6, D), lambda i: (0, 0))],
        out_specs=[ospec, ospec], compiler_params=_cp(("parallel",), VMEM_BIG))(uc, wa, wx, vec)


def lru_coeffs_bwd(lay, uc, wa, wx, vec, da, db, name, rider=None):
    tr = lay.tc

    def body(x_ref, wa_ref, wx_ref, v_ref, da_ref, db_ref, dx_ref, dwa_ref, dwx_ref, dv_ref):
        @pl.when(pl.program_id(0) == 0)
        def _():
            dwa_ref[...] = jnp.zeros_like(dwa_ref)
            dwx_ref[...] = jnp.zeros_like(dwx_ref)
            dv_ref[...] = jnp.zeros_like(dv_ref)

        for h in range(8):
            sl = slice(h * 128, (h + 1) * 128)
            x = x_ref[:, sl]
            xb = x.astype(BF16)
            dx = jnp.zeros_like(x)
            for d in range(2):
                wab, wxb = wa_ref[d, h].astype(BF16), wx_ref[d, h].astype(BF16)
                r, gi, sp, dsp, a, sq = _lru_gates(x, xb, wa_ref[d, h], wx_ref[d, h], v_ref[d:d + 1, sl],
                                                   v_ref[2 + d:3 + d, sl], v_ref[4 + d:5 + d, sl])
                dbv, dav = db_ref[d, h], da_ref[d, h]
                t1 = dbv * sq
                dgi = t1 * x
                dx = dx + t1 * gi
                dla = dav * a - (dbv * gi * x) * (a * a) / sq
                dr = dla * (-LRU_C * sp)
                dlam = jnp.sum(dla * (-LRU_C * r), axis=0, keepdims=True) * dsp
                dpa = dr * r * (1.0 - r)
                dpx = dgi * gi * (1.0 - gi)
                dpab, dpxb = dpa.astype(BF16), dpx.astype(BF16)
                dwa_ref[d, h] += _tn(xb, dpab)
                dwx_ref[d, h] += _tn(xb, dpxb)
                dx = dx + _nt(dpab, wab) + _nt(dpxb, wxb)
                dv_ref[d:d + 1, sl] += jnp.sum(dpa, axis=0, keepdims=True)
                dv_ref[2 + d:3 + d, sl] += jnp.sum(dpx, axis=0, keepdims=True)
                dv_ref[4 + d:5 + d, sl] += dlam
            dx_ref[:, sl] = dx

    wspec = pl.BlockSpec((2, 8, 128, 128), lambda i: (0, 0, 0, 0))
    gspec = pl.BlockSpec((2, 8, tr, 128), lambda i: (0, 0, i, 0))
    vspec = pl.BlockSpec((6, D), lambda i: (0, 0))
    xspec = pl.BlockSpec((tr, D), lambda i: (i, 0))
    return _host_call(
        body, rider, name, (lay.T // tr,), [xspec, wspec, wspec, vspec, gspec, gspec], [xspec, wspec, wspec, vspec],
        [_sds((lay.T, D), F32), _sds((2, 8, 128, 128), F32), _sds((2, 8, 128, 128), F32), _sds((6, D), F32)],
        (uc, wa, wx, vec, da, db), ("arbitrary",), 6, 4)


GB = 2
SCAN_UNROLL = 8


def _tile_scan(a, b, up):
    t = lax.broadcasted_iota(jnp.int32, a.shape, 0)
    for d in (1, 2, 4):
        sh = 8 - d if up else d
        m = (t < 8 - d) if up else (t >= d)
        a_prev, b_prev = pltpu.roll(a, sh, 0), pltpu.roll(b, sh, 0)
        b = jnp.where(m, a * b_prev + b, b)
        a = jnp.where(m, a * a_prev, a)
    return a, b


def lru_scan(lay, a, b, name):
    segs = [(0, lay.C), (lay.C, lay.L)]

    def body(a_ref, b_ref, s_ref):
        for d in range(2):
            rev = d == 1
            state = tuple(jnp.zeros((1, 128), F32) for _ in range(GB))
            for base, n in segs:
                nt = n // 8

                def step(j, c, base=base, nt=nt, rev=rev, d=d):
                    c = list(c)
                    for u in range(SCAN_UNROLL):
                        jj = j * SCAN_UNROLL + u
                        r0 = pl.multiple_of(base + 8 * ((nt - 1 - jj) if rev else jj), 8)
                        for g in range(GB):
                            at, bt = _tile_scan(a_ref[d, g, pl.ds(r0, 8), :], b_ref[d, g, pl.ds(r0, 8), :], rev)
                            h = at * c[g] + bt
                            s_ref[d, g, pl.ds(r0, 8), :] = h
                            c[g] = h[0:1] if rev else h[7:8]
                    return tuple(c)

                state = lax.fori_loop(0, nt // SCAN_UNROLL, step, state)

    spec = pl.BlockSpec((2, GB, lay.PS, 128), lambda s, hb: (0, hb, s, 0))
    return pl.pallas_call(
        body, name=name, out_shape=_sds((2, 8, lay.T, 128), F32), grid=(2, 8 // GB),
        in_specs=[spec, spec], out_specs=spec, compiler_params=_cp(("parallel", "parallel"), VMEM_BIG))(a, b)


def lru_scan_bwd(lay, a, s, dy, name):
    segs = [(0, lay.C), (lay.C, lay.L)]
    C, PS = lay.C, lay.PS

    def body(a_ref, s_ref, g_ref, da_ref, db_ref):
        t = lax.broadcasted_iota(jnp.int32, (8, 128), 0)
        for d in range(2):
            rev = d == 1
            carry = tuple(jnp.zeros((1, 128), F32) for _ in range(GB))
            for si in (1, 0):
                base, n = segs[si]
                nt = n // 8

                def step(j, c, base=base, nt=nt, rev=rev, d=d):
                    c = list(c)
                    for u in range(SCAN_UNROLL):
                        jj = j * SCAN_UNROLL + u
                        r0 = pl.multiple_of(base + 8 * (jj if rev else (nt - 1 - jj)), 8)
                        if rev:
                            rn = pl.multiple_of(jnp.where(r0 == PS - 8, 0, r0 + 8), 8)
                            nb_zero = r0 == C - 8
                        else:
                            rn = pl.multiple_of(jnp.maximum(r0 - 8, 0), 8)
                            nb_zero = r0 == 0
                        for g in range(GB):
                            av = a_ref[d, g, pl.ds(r0, 8), :]
                            gv = g_ref[g, pl.ds(r0, 8), :]
                            sv = s_ref[d, g, pl.ds(r0, 8), :]
                            nbt = s_ref[d, g, pl.ds(rn, 8), :]
                            at, bt = _tile_scan(av, av * gv, not rev)
                            m = at * c[g] + bt
                            if rev:
                                m_next = jnp.where(t >= 1, pltpu.roll(m, 1, 0), c[g])
                                nb = jnp.where(nb_zero, 0.0, nbt[0:1])
                                h_prev = jnp.where(t < 7, pltpu.roll(sv, 7, 0), nb)
                                c[g] = m[7:8]
                            else:
                                m_next = jnp.where(t < 7, pltpu.roll(m, 7, 0), c[g])
                                nb = jnp.where(nb_zero, 0.0, nbt[7:8])
                                h_prev = jnp.where(t >= 1, pltpu.roll(sv, 1, 0), nb)
                                c[g] = m[0:1]
                            lam = gv + m_next
                            db_ref[d, g, pl.ds(r0, 8), :] = lam
                            da_ref[d, g, pl.ds(r0, 8), :] = lam * h_prev
                    return tuple(c)

                carry = lax.fori_loop(0, nt // SCAN_UNROLL, step, carry)

    spec = pl.BlockSpec((2, GB, lay.PS, 128), lambda s, hb: (0, hb, s, 0))
    return pl.pallas_call(
        body, name=name, out_shape=[_sds((2, 8, lay.T, 128), F32)] * 2, grid=(2, 8 // GB),
        in_specs=[spec, spec, pl.BlockSpec((GB, lay.PS, 128), lambda s, hb: (hb, s, 0))],
        out_specs=[spec, spec], compiler_params=_cp(("parallel", "parallel"), VMEM_BIG))(a, s, dy)


def _gelu(x):
    k = math.sqrt(2.0 / math.pi)
    t = jnp.tanh(k * (x + 0.044715 * x * x * x))
    return 0.5 * x * (1.0 + t), 0.5 * (1.0 + t) + 0.5 * x * (1.0 - t * t) * k * (1.0 + 3 * 0.044715 * x * x)


def lru_gate(lay, p, s, name):
    tr = lay.tr

    def body(g_ref, s_ref, o_ref):
        for h in range(8):
            sl = slice(h * 128, (h + 1) * 128)
            o_ref[:, sl] = (_gelu(g_ref[:, sl])[0] * (s_ref[0, h] + s_ref[1, h])).astype(o_ref.dtype)

    return pl.pallas_call(
        body, name=name, out_shape=_sds((lay.T, D), BF16), grid=(lay.nblk,),
        in_specs=[pl.BlockSpec((tr, D), lambda i: (i, 0)), pl.BlockSpec((2, 8, tr, 128), lambda i: (0, 0, i, 0))],
        out_specs=pl.BlockSpec((tr, D), lambda i: (i, 0)), compiler_params=_cp(("parallel",)))(p, s)


def lru_gate_bwd(lay, p, s, do, name):
    tr = lay.tr

    def body(g_ref, s_ref, do_ref, dg_ref, dy_ref):
        for h in range(8):
            sl = slice(h * 128, (h + 1) * 128)
            ge, dge = _gelu(g_ref[:, sl])
            dov = do_ref[:, sl]
            dg_ref[:, sl] = (dov * (s_ref[0, h] + s_ref[1, h]) * dge).astype(dg_ref.dtype)
            dy_ref[h] = dov * ge

    xspec = pl.BlockSpec((tr, D), lambda i: (i, 0))
    return pl.pallas_call(
        body, name=name, out_shape=[_sds((lay.T, 2 * D), BF16), _sds((8, lay.T, 128), F32)], grid=(lay.nblk,),
        in_specs=[xspec, pl.BlockSpec((2, 8, tr, 128), lambda i: (0, 0, i, 0)), xspec],
        out_specs=[xspec, pl.BlockSpec((8, tr, 128), lambda i: (0, i, 0))],
        compiler_params=_cp(("parallel",)))(p, s, do)


def silu_rows(x, name):
    def body(x_ref, o_ref):
        v = x_ref[...]
        o_ref[...] = (v * _sigmoid(v)).astype(o_ref.dtype)
    return pl.pallas_call(body, name=name, out_shape=_sds(x.shape, BF16), in_specs=[VMEM_SPEC], out_specs=VMEM_SPEC)(x)


def mod_grad_rows(gath, name):
    w = gath.shape[-1]

    def body(g_ref, dm_ref, db_ref):
        dm_ref[...] = jnp.zeros_like(dm_ref)
        for l in range(2):
            ctx = g_ref[0, 3 * l + 2:3 * l + 3, :]
            tot = g_ref[0, 3 * l:3 * l + 1, :] + g_ref[0, 3 * l + 1:3 * l + 2, :]
            for k in range(8):
                dm_ref[l, 2 * k:2 * k + 2, :] = g_ref[k, 3 * l:3 * l + 2, :]
                if k:
                    ctx = ctx + g_ref[k, 3 * l + 2:3 * l + 3, :]
                    tot = tot + (g_ref[k, 3 * l:3 * l + 1, :] + g_ref[k, 3 * l + 1:3 * l + 2, :])
            dm_ref[l, 16:17, :] = ctx
            db_ref[l:l + 1, :] = tot + ctx

    return pl.pallas_call(body, name=name, out_shape=[_sds((2, 32, w), F32), _sds((2, w), F32)],
                          in_specs=[VMEM_SPEC], out_specs=[VMEM_SPEC, VMEM_SPEC])(gath)


def cctx_grad(p, c_ctx, name):
    def body(a_ref, c_ref, o_ref):
        cv = c_ref[...]
        sg = _sigmoid(cv)
        o_ref[...] = 0.5 * (a_ref[0, 0:1, :] + a_ref[1, 0:1, :]) * (sg * (1.0 + cv * (1.0 - sg)))
    return pl.pallas_call(body, name=name, out_shape=_sds((1, D), F32), in_specs=[VMEM_SPEC] * 2,
                          out_specs=VMEM_SPEC)(p, c_ctx)


def loss_lnb(lay, xhat, rstd, y, tgt, mod, k_gate, coef, lnv, name):
    def fn(xh, rs, yb, tb, m, ln):
        lat = (pl.program_id(0) % lay.bps) >= lay.cb
        e = jnp.where(lat, xh * ln[0:1] + ln[1:2] - tb, 0.0)
        dy, dres, s1 = _ln_bwd_math(e * (1.0 / D), xh, rs, yb, m[k_gate:k_gate + 1], coef, ln)
        return dy, dres, s1, jnp.sum(e * e, axis=0, keepdims=True) * (0.5 / D)
    return rowwise(lay, name, fn, [xhat, rstd, y, tgt], segs=[mod], vecs=[lnv],
                   outs=[(D, BF16), (D, F32)], sums=[(3, D), (1, D)])


def adamw(w, g, m, v, name):
    shape = w.shape
    w2, g2, m2, v2 = (t.reshape(-1, shape[-1]) for t in (w, g, m, v))
    rows, width = w2.shape
    tr = 256 if rows % 256 == 0 else rows
    c1 = 1.0 - ADAM_B1 ** ADAM_STEP
    c2 = 1.0 - ADAM_B2 ** ADAM_STEP

    def body(w_ref, g_ref, m_ref, v_ref, d_ref, mo_ref, vo_ref):
        gv = g_ref[...]
        mn = ADAM_B1 * m_ref[...] + (1.0 - ADAM_B1) * gv
        vn = ADAM_B2 * v_ref[...] + (1.0 - ADAM_B2) * (gv * gv)
        d_ref[...] = -ADAM_LR * ((mn / c1) / (jnp.sqrt(vn / c2) + ADAM_EPS) + ADAM_WD * w_ref[...])
        mo_ref[...] = mn
        vo_ref[...] = vn

    spec = pl.BlockSpec((tr, width), lambda i: (i, 0))
    d, mn, vn = pl.pallas_call(body, name=name, out_shape=[_sds((rows, width), F32)] * 3, grid=(rows // tr,),
                               in_specs=[spec] * 4, out_specs=[spec] * 3, compiler_params=_cp(("parallel",)))(w2, g2, m2, v2)
    return d.reshape(shape), mn.reshape(shape), vn.reshape(shape)


def adamw_ffn(w, m, v, red, kind, ns, name):
    shape = w.shape
    w2, m2, v2 = (t.reshape(-1, shape[-1]) for t in (w, m, v))
    rows, width = w2.shape
    c1 = 1.0 - ADAM_B1 ** ADAM_STEP
    c2 = 1.0 - ADAM_B2 ** ADAM_STEP
    tr, nb = ns // 2, 2
    gspec = pl.BlockSpec((tr, D), lambda i: (((i // nb) * 3 + kind) * nb + i % nb, 0))

    def body(w_ref, g_ref, m_ref, v_ref, go_ref, d_ref, mo_ref, vo_ref):
        gv = g_ref[...]
        mn = ADAM_B1 * m_ref[...] + (1.0 - ADAM_B1) * gv
        vn = ADAM_B2 * v_ref[...] + (1.0 - ADAM_B2) * (gv * gv)
        go_ref[...] = gv
        d_ref[...] = -ADAM_LR * ((mn / c1) / (jnp.sqrt(vn / c2) + ADAM_EPS) + ADAM_WD * w_ref[...])
        mo_ref[...] = mn
        vo_ref[...] = vn

    spec = pl.BlockSpec((tr, width), lambda i: (i, 0))
    outs = pl.pallas_call(body, name=name, out_shape=[_sds((rows, width), F32)] * 4, grid=(rows // tr,),
                          in_specs=[spec, gspec, spec, spec], out_specs=[spec] * 4,
                          compiler_params=_cp(("parallel",)))(w2, red, m2, v2)
    return tuple(t.reshape(shape) for t in outs)


def mod_mm(sc, w_mod, bias, name):
    wm = w_mod.shape[-1]
    tn = _pick(wm, (768, 512, 384, 256, 128))

    def body(a_ref, b_ref, c_ref, o_ref):
        o_ref[...] = _nn(a_ref[...], b_ref[...].astype(BF16)) + c_ref[...]

    return pl.pallas_call(
        body, name=name, out_shape=_sds((DEPTH, 32, wm), F32), grid=(DEPTH, wm // tn),
        in_specs=[pl.BlockSpec((32, D), lambda l, j: (0, 0)), pl.BlockSpec((None, D, tn), lambda l, j: (l, 0, j)),
                  pl.BlockSpec((None, 1, tn), lambda l, j: (l, 0, j))],
        out_specs=pl.BlockSpec((None, 32, tn), lambda l, j: (l, 0, j)),
        compiler_params=_cp(("parallel", "parallel")))(sc, w_mod, bias)


def wmod_dw(sc, dcol, name):
    wm = dcol.shape[-1]
    tm = 256

    def body(a_ref, b_ref, o_ref):
        o_ref[...] = _tn(a_ref[...], b_ref[...].astype(BF16))

    return pl.pallas_call(
        body, name=name, out_shape=_sds((DEPTH, D, wm), F32), grid=(DEPTH, D // tm),
        in_specs=[pl.BlockSpec((32, tm), lambda l, i: (0, i)), pl.BlockSpec((None, 32, wm), lambda l, i: (l, 0, 0))],
        out_specs=pl.BlockSpec((None, tm, wm), lambda l, i: (l, i, 0)),
        compiler_params=_cp(("parallel", "parallel")))(sc, dcol)


def cctx_dx(drow, w_mod, name):
    wm = w_mod.shape[-1]

    def body(a_ref, b_ref, o_ref):
        o_ref[...] = _nt(a_ref[...].astype(BF16), b_ref[...].astype(BF16))

    return pl.pallas_call(
        body, name=name, out_shape=_sds((DEPTH, 16, D), F32), grid=(DEPTH,),
        in_specs=[pl.BlockSpec((None, 16, wm), lambda l: (l, 0, 0)), pl.BlockSpec((None, D, wm), lambda l: (l, 0, 0))],
        out_specs=pl.BlockSpec((None, 16, D), lambda l: (l, 0, 0)), compiler_params=_cp(("parallel",), VMEM_BIG))(drow, w_mod)


HEAD_PERM = (0, 4, 1, 5, 2, 6, 3, 7)


def _rot_rows(wt):
    return jnp.concatenate([-wt[32:64], wt[0:32]], axis=0)


def _unrot_rows(g):
    return jnp.concatenate([g[32:64], -g[0:32]], axis=0)


def _heads(a, n):
    return [a[64 * i:64 * (i + 1)] for i in range(n)]


def kernel(x, c, ctx, c_ctx, w_mod, b_mod, ln_g, ln_b, ffn_w_gate, ffn_w_up, ffn_w_down, mix_ab_w_in, attn_sink, pool_w, pool_scale, mix_ab_w_out, lru_w_in, lru_conv_w, lru_conv_b, lru_wa, lru_ba, lru_wx, lru_bx, lru_lambda, lru_w_out, loss_target, m_c_ctx, m_w_mod, m_b_mod, m_ln_g, m_ln_b, m_ffn_w_gate, m_ffn_w_up, m_ffn_w_down, m_mix_ab_w_in, m_attn_sink, m_pool_w, m_pool_scale, m_mix_ab_w_out, m_lru_w_in, m_lru_conv_w, m_lru_conv_b, m_lru_wa, m_lru_ba, m_lru_wx, m_lru_bx, m_lru_lambda, m_lru_w_out, v_c_ctx, v_w_mod, v_b_mod, v_ln_g, v_ln_b, v_ffn_w_gate, v_ffn_w_up, v_ffn_w_down, v_mix_ab_w_in, v_attn_sink, v_pool_w, v_pool_scale, v_mix_ab_w_out, v_lru_w_in, v_lru_conv_w, v_lru_conv_b, v_lru_wa, v_lru_ba, v_lru_wx, v_lru_bx, v_lru_lambda, v_lru_w_out):
    n_lat, n_ctx = x.shape[1], ctx.shape[1]
    lay = Layout(n_ctx, n_lat)
    T = lay.T
    ns = ffn_w_gate.shape[-1]
    n_li, n_ai = lru_w_in.shape[-1], mix_ab_w_in.shape[-1]
    n_ao, n_lo = mix_ab_w_out.shape[1], lru_w_out.shape[1]
    wm = w_mod.shape[-1]
    dsh = ln_g.shape[-1]
    mx, my, mc = lax.axis_index("x"), lax.axis_index("y"), lax.axis_index("c")
    chip = 2 * mx + my
    me = 2 * chip + mc

    c_all = all_gather8(c, "ag8_c").reshape(16, D)
    cc = jnp.concatenate([c_all, c_ctx[None, :], jnp.zeros((15, D), F32)], axis=0)
    sc = silu_rows(cc, "silu_c")
    bias = lax.dynamic_slice(b_mod, (0, chip * wm), (DEPTH, wm)).reshape(DEPTH, 1, wm)
    modg = all_gather_chips(mod_mm(sc, w_mod, bias, "mod_mm"), "ag_mod")
    modtab = []
    for l in range(DEPTH):
        full = jnp.transpose(modg[:, l], (1, 0, 2)).reshape(32, N_CHIP * wm)
        mine = lax.dynamic_slice(full, (2 * me, 0), (2, N_CHIP * wm))
        modtab.append(jnp.concatenate([mine, full[16:17]], axis=0).reshape(3, N_MOD, D))

    small = jnp.concatenate([ln_g.reshape(6, dsh), ln_b.reshape(6, dsh), lru_conv_w[0], lru_conv_b, lru_ba[0],
                             lru_bx[0], lru_lambda[0], jnp.zeros((9, dsh), F32)], axis=0)
    small = all_gather_chips(small.reshape(2, 16, dsh), "ag_small").reshape(N_CHIP, 32, dsh)
    small = jnp.transpose(small, (1, 0, 2)).reshape(32, D)
    ln_g_f, ln_b_f = small[0:6].reshape(2, 3, D), small[6:12].reshape(2, 3, D)
    conv_w_f, conv_b_f = small[12:16], small[16:17]
    lru_vec = small[17:23]

    hh = 3 * ns // 2
    gate_t, up_t = jnp.swapaxes(ffn_w_gate, -1, -2), jnp.swapaxes(ffn_w_up, -1, -2)
    extra = [0, n_ai + n_ao, n_li + n_lo, 0]
    placed = [ffn_place(gate_t, up_t, ffn_w_down, g // 2, g % 2, f"ag_ffn{g}_place", extra[g]) for g in range(4)]
    placed[1] = place_rows(placed[1], jnp.concatenate([mix_ab_w_in[0].T, mix_ab_w_out[0]], axis=0).astype(BF16), 3 * ns,
                           "ag_mixa_place")
    placed[2] = place_rows(placed[2], jnp.concatenate([lru_w_in[0].T, lru_w_out[0]], axis=0).astype(BF16), 3 * ns,
                           "ag_mixc_place")
    placed = [p.reshape(N_CHIP, 2, p.shape[1] // 2, D) for p in placed]
    wb = [gather_placed(placed[0], "ag_ffn0"), None, None, None]
    mixw = {}

    def mixa_w():
        if "a" not in mixw:
            full = wb[1].reshape(N_CHIP, -1, D)
            ab_in_t = full[:, 3 * ns:3 * ns + n_ai].reshape(N_CHIP * n_ai, D)
            ab_out = full[:, 3 * ns + n_ai:].reshape(N_CHIP * n_ao, D)
            qh, kh = _heads(ab_in_t[Q0:K0], N_HEADS), _heads(ab_in_t[K0:V0], N_KV)
            w_ext_t = jnp.concatenate([qh[h] for h in HEAD_PERM] + [ab_in_t[K0:QR0]]
                                      + [_rot_rows(qh[h]) for h in HEAD_PERM] + [_rot_rows(t) for t in kh], axis=0)
            oh = _heads(ab_out[0:ATT_W], N_HEADS)
            mixw["a"] = (w_ext_t, jnp.concatenate([oh[h] for h in HEAD_PERM] + [ab_out[ATT_W:]], axis=0))
        return mixw["a"]

    def mixc_w():
        if "c" not in mixw:
            full = wb[2].reshape(N_CHIP, -1, D)
            mixw["c"] = (full[:, 3 * ns:3 * ns + n_li].reshape(N_CHIP * n_li, D),
                         full[:, 3 * ns + n_li:].reshape(N_CHIP * n_lo, D))
        return mixw["c"]

    t = jnp.arange(n_lat)
    inv = ROPE_THETA ** (-jnp.arange(16, dtype=F32) / 16.0)
    ang = jnp.concatenate([(t // GRID_W).astype(F32)[:, None] * inv, (t % GRID_W).astype(F32)[:, None] * inv], axis=-1)
    cos1 = jnp.concatenate([jnp.ones((n_ctx, 32), F32), jnp.cos(ang)], axis=0)
    sin1 = jnp.concatenate([jnp.zeros((n_ctx, 32), F32), jnp.sin(ang)], axis=0)
    cos_t = jnp.tile(cos1, (2, 4))
    sin_t = jnp.tile(sin1, (2, 4))
    sk = attn_sink[0]
    sink_tab = jnp.concatenate([jnp.repeat(jnp.stack([sk[:4], sk[4:]], axis=1), HEAD_DIM, axis=1),
                                jnp.zeros((4, 128), F32)], axis=0)
    pscale = pool_scale.reshape(1, POOL_W)

    h0 = jnp.concatenate([ctx, x], axis=1).reshape(T, D)
    tgt = loss_target.reshape(2 * n_lat, D)

    def lnv(l, j):
        return jnp.stack([ln_g_f[l, j], ln_b_f[l, j]])

    subs = [(0, 0, 0.5, 0), (0, 3, 1.0, 1), (0, 6, 0.5, 2), (1, 0, 0.5, 0), (1, 3, 1.0, 1), (1, 6, 0.5, 2)]

    def ffn_core(hm, l, f):
        tag = f"l{l}f{f}"
        gi = 2 * l + f
        w = wb[gi].reshape(N_CHIP, -1, D)
        if gi == 3:
            up, sl, a = ffn_up(lay, hm, w, 0, 1, ns, f"ffn_up_{tag}")
            (y,) = slab_nn_acc(lay, [a], w, [2], ns, f"ffn_down_{tag}")
            return y, dict(up=up, sl=sl, a=a, nbuf=None)
        up, sl, a, nbuf = ffn_up(lay, hm, w, 0, 1, ns, f"ffn_up_{tag}", rider=rider_gather_xy(placed[gi + 1]))
        y, nbuf = slab_nn_acc(lay, [a], w, [2], ns, f"ffn_down_{tag}", rider=rider_gather_fwd(nbuf))
        return y, dict(up=up, sl=sl, a=a, nbuf=nbuf)

    def mixa_core(hm):
        p = mm_nt(hm, mixa_w()[0], "mixa_in")
        qr, kr, vb, u = rope_fwd(lay, p, cos_t, sin_t, "rope")
        cat, lse = attn_fwd(lay, qr, kr, vb, sink_tab, "attn")
        cat = pool_fwd(lay, u, pool_w[0], pscale, cat, "pool")
        return mm_nn(cat, mixa_w()[1], "mixa_out"), dict(qr=qr, kr=kr, vb=vb, u=u, lse=lse, cat=cat)

    def mixc_core(hm):
        p = mm_nt(hm, mixc_w()[0], "mixc_in")
        uc = conv_fwd(lay, p, D, conv_w_f, conv_b_f, "conv")
        a, b = lru_coeffs(lay, uc, lru_wa[0], lru_wx[0], lru_vec, "lru_coef")
        s = lru_scan(lay, a, b, "lru_scan")
        o = lru_gate(lay, p, s, "lru_gate")
        return mm_nn(o, mixc_w()[1], "mixc_out"), dict(p=p, uc=uc, a=a, s=s, o=o)

    recs = []
    h = h0
    hm = modulate(lay, h0, modtab[0], 0, 1, "mod_first")
    for k, (l, k0, coef, j) in enumerate(subs):
        if k0 == 3:
            y, core = mixa_core(hm) if l == 0 else mixc_core(hm)
        else:
            y, core = ffn_core(hm, l, k0 // 6)
        nxt = None if k == 5 else (modtab[subs[k + 1][0]], subs[k + 1][1], subs[k + 1][1] + 1)
        nbuf = core.pop("nbuf", None)
        res = resid_ln(lay, h, y, modtab[l], k0 + 2, coef, lnv(l, j), f"ln_s{k}", nxt=nxt,
                       prev_ln=None if k == 0 else lnv(*subs[k - 1][::3]),
                       rider=None if nbuf is None else rider_gather_d2d(nbuf))
        if nbuf is not None:
            wb[2 * l + k0 // 6 + 1] = res[-1]
        recs.append(dict(h=h, hm=hm, y=y, xhat=res[0], rstd=res[1], **core))
        h, hm = res[0], (None if nxt is None else res[2])


    dln = {}
    dms = {}
    mixg = {}
    ffn_red = [lax.empty((4, 2, hh, D), F32)]
    mix_red = {}
    pending = []

    def rs_sib(p):
        return None if p is None else rider_reduce_sib(p["buf"])

    def rs_add2(p, recv):
        p["q"] = add_own_half(p["buf"], recv, BF16, f"rs_add2_{p['key']}")

    def rs_join(p, arr):
        if isinstance(p["key"], int):
            return rider_join(sum_slots(p["q"], arr, f"rs_add4_{p['key']}", dst=ffn_red[0], g=p["key"]), p["key"])
        return rider_join(sum_slots(p["q"], arr, f"rs_add4_{p['key']}"))

    def rs_done(p, joined):
        if isinstance(p["key"], int):
            ffn_red[0] = joined
        else:
            mix_red[p["key"]] = joined.reshape(-1, D)

    def ffn_core_bwd(dy, r, l, f):
        tag = f"l{l}f{f}"
        gi = 2 * l + f
        w = wb[gi].reshape(N_CHIP, -1, D)
        p = pending.pop() if pending else None
        gb = lax.empty((N_CHIP, 3 * ns, D), F32)
        if p is None:
            dg, du = ffn_bwd_da(lay, dy, w, 2, r["up"], r["sl"], ns, f"ffn_da_{tag}")
            (gb,) = slab_tn(lay, r["a"], dy, gb, 2, ns, f"ffn_dwd_{tag}")
            (gb,) = slab_tn(lay, dg, r["hm"], gb, 0, ns, f"ffn_dwg_{tag}")
            (gb,) = slab_tn(lay, du, r["hm"], gb, 1, ns, f"ffn_dwu_{tag}")
            (dhm,) = slab_nn_acc(lay, [dg, du], w, [0, 1], ns, f"ffn_dh_{tag}")
        elif gi == 0:
            dg, du, recv = ffn_bwd_da(lay, dy, w, 2, r["up"], r["sl"], ns, f"ffn_da_{tag}", rider=rs_sib(p))
            rs_add2(p, recv)
            gb, arr = slab_tn(lay, r["a"], dy, gb, 2, ns, f"ffn_dwd_{tag}", rider=rider_reduce_copies(p["q"]))
            gb, joined = slab_tn(lay, dg, r["hm"], gb, 0, ns, f"ffn_dwg_{tag}", rider=rs_join(p, arr))
            rs_done(p, joined)
            (gb,) = slab_tn(lay, du, r["hm"], gb, 1, ns, f"ffn_dwu_{tag}")
            own = gb.reshape(N_CHIP, 2, hh, D)
            dhm, recv = slab_nn_acc(lay, [dg, du], w, [0, 1], ns, f"ffn_dh_{tag}", rider=rider_reduce_sib(own))
            pending.append(dict(buf=own, key=gi, recv=recv))
            return dhm
        else:
            dg, du, recv = ffn_bwd_da(lay, dy, w, 2, r["up"], r["sl"], ns, f"ffn_da_{tag}", rider=rs_sib(p))
            rs_add2(p, recv)
            gb, arr = slab_tn(lay, r["a"], dy, gb, 2, ns, f"ffn_dwd_{tag}", rider=rider_reduce_copy(p["q"], 0))
            gb, arr = slab_tn(lay, dg, r["hm"], gb, 0, ns, f"ffn_dwg_{tag}", rider=rider_reduce_copy(p["q"], 1, arr))
            gb, arr = slab_tn(lay, du, r["hm"], gb, 1, ns, f"ffn_dwu_{tag}", rider=rider_reduce_copy(p["q"], 2, arr))
            dhm, joined = slab_nn_acc(lay, [dg, du], w, [0, 1], ns, f"ffn_dh_{tag}", rider=rs_join(p, arr))
            rs_done(p, joined)
        pending.append(dict(buf=gb.reshape(N_CHIP, 2, hh, D), key=gi))
        return dhm

    def mixc_core_bwd(dy, r):
        p = pending.pop() if pending else None
        w_in_t, w_out = mixc_w()
        if p is None:
            do_c = mm_nt(dy, w_out, "mixc_out_dx")
        else:
            do_c, recv = mm_nt(dy, w_out, "mixc_out_dx", rider=rs_sib(p))
            rs_add2(p, recv)
        g_out = mm_tn(r["o"], dy, "mixc_out_dw")
        dgate, dyg = lru_gate_bwd(lay, r["p"], r["s"], do_c, "lru_gate_b")
        da_c, db_c = lru_scan_bwd(lay, r["a"], r["s"], dyg, "lru_scan_b")
        res = lru_coeffs_bwd(lay, r["uc"], lru_wa[0], lru_wx[0], lru_vec, da_c, db_c, "lru_coef_b",
                             rider=None if p is None else rider_reduce_copies(p["q"]))
        duc, mixg["wa"], mixg["wx"], mixg["vec"] = res[:4]
        dp_c, mixg["cw"], mixg["cb"] = conv_bwd(lay, r["p"], D, conv_w_f, duc, dgate, "conv_b")
        if p is None:
            g_in_t = mm_tn(dp_c, r["hm"], "mixc_in_dw")
        else:
            g_in_t, joined = mm_tn(dp_c, r["hm"], "mixc_in_dw", rider=rs_join(p, res[4]))
            rs_done(p, joined)
        buf = jnp.concatenate([g_in_t.reshape(N_CHIP, n_li, D), g_out.reshape(N_CHIP, n_lo, D)], axis=1)
        pending.append(dict(buf=buf.reshape(N_CHIP, 2, (n_li + n_lo) // 2, D), key="c"))
        return mm_nn(dp_c, w_in_t, "mixc_in_dx")

    def mixa_core_bwd(dy, r):
        p = pending.pop() if pending else None
        w_ext_t, w_out_ext = mixa_w()
        if p is None:
            dcat = mm_nt(dy, w_out_ext, "mixa_out_dx")
        else:
            dcat, recv = mm_nt(dy, w_out_ext, "mixa_out_dx", rider=rs_sib(p))
            rs_add2(p, recv)
        g_out_ext = mm_tn(r["cat"], dy, "mixa_out_dw")
        res = attn_bwd(lay, r["qr"], r["kr"], r["vb"], sink_tab, r["lse"], dcat, "attn_b",
                       rider=None if p is None else rider_reduce_copies(p["q"]))
        dqr, dkr, dv, mixg["sink"] = res[:4]
        du_a, mixg["pw"], mixg["ps"] = pool_bwd(lay, r["u"], dcat, pool_w[0], pscale, "pool_b")
        dp_a = rope_bwd(lay, dqr, dkr, dv, du_a, cos_t, sin_t, "rope_b")
        if p is None:
            g_ext_t = mm_tn(dp_a, r["hm"], "mixa_in_dw")
        else:
            g_ext_t, joined = mm_tn(dp_a, r["hm"], "mixa_in_dw", rider=rs_join(p, res[4]))
            rs_done(p, joined)
        gq, gqr = _heads(g_ext_t[Q0:K0], N_HEADS), _heads(g_ext_t[QR0:KR0], N_HEADS)
        g_q = [None] * N_HEADS
        for i, h in enumerate(HEAD_PERM):
            g_q[h] = gq[i] + _unrot_rows(gqr[i])
        gk = [a + _unrot_rows(b) for a, b in zip(_heads(g_ext_t[K0:V0], N_KV), _heads(g_ext_t[KR0:PEXT], N_KV))]
        g_ab_in_t = jnp.concatenate(g_q + gk + [g_ext_t[V0:QR0]], axis=0)
        go = _heads(g_out_ext[0:ATT_W], N_HEADS)
        g_o = [None] * N_HEADS
        for i, h in enumerate(HEAD_PERM):
            g_o[h] = go[i]
        g_ab_out = jnp.concatenate(g_o + [g_out_ext[ATT_W:]], axis=0)
        buf = jnp.concatenate([g_ab_in_t.reshape(N_CHIP, n_ai, D), g_ab_out.reshape(N_CHIP, n_ao, D)], axis=1)
        pending.append(dict(buf=buf.reshape(N_CHIP, 2, (n_ai + n_ao) // 2, D), key="a"))
        return mm_nn(dp_a, w_ext_t, "mixa_in_dx")

    l, k0, coef, j = subs[5]
    dy, dres, s1, lparts = loss_lnb(lay, recs[5]["xhat"], recs[5]["rstd"], recs[5]["y"], tgt, modtab[l], k0 + 2, coef,
                                    lnv(l, j), "loss_lnb")
    loss = lax.psum(jnp.sum(lparts), ("x", "y", "c"))
    for k in range(5, -1, -1):
        l, k0, coef, j = subs[k]
        r = recs[k]
        if k0 == 3:
            dhm = mixa_core_bwd(dy, r) if l == 0 else mixc_core_bwd(dy, r)
        else:
            dhm = ffn_core_bwd(dy, r, l, k0 // 6)
        dln[(l, j)] = s1
        if k > 0:
            lp, k0p, coefp, jp = subs[k - 1]
            rp = recs[k - 1]
            dy, dres, s1, s2 = modb_lnb(lay, dres, dhm, modtab[l], k0 + 1, rp["xhat"], rp["rstd"], rp["y"],
                                        modtab[lp], k0p + 2, coefp, lnv(lp, jp), f"modb_lnb_s{k}")
        else:
            gx, s2 = mod_bwd(lay, dres, dhm, r["h"], modtab[l], k0 + 1, "modb_s0")
        dms[(l, k0)] = s2
    sums = block_sums(lay, list(dln.values()) + list(dms.values()), "block_sums")
    dln, dms = dict(zip(dln, sums[:len(dln)])), dict(zip(dms, sums[len(dln):]))
    grad_x = gx.reshape(2, n_lat, D)
    g_wa, g_wx, g_vec, g_cw, g_cb, g_sink, g_pw, g_ps = (mixg[n] for n in ("wa", "wx", "vec", "cw", "cb", "sink", "pw", "ps"))

    rows = []
    for l in range(DEPTH):
        per_k = []
        for k0, j in ((0, 0), (3, 1), (6, 2)):
            per_k += [dms[(l, k0)][:3, 0], dms[(l, k0)][:3, 1], dln[(l, j)][:3, 2]]
        rows.append(jnp.stack(per_k, axis=1).reshape(3, N_MOD * D))
    dmod_loc = jnp.concatenate(rows + [jnp.zeros((2, N_MOD * D), F32)], axis=0)
    dmod_all, g_b_mod = mod_grad_rows(all_gather8(dmod_loc, "ag8_dmod"), "dmod_rows")
    dcol = lax.dynamic_slice(dmod_all, (0, 0, chip * wm), (DEPTH, 32, wm))
    g_w_mod = wmod_dw(sc, dcol, "wmod_dw")
    g_cctx = cctx_grad(cctx_dx(dcol[:, 16:32], w_mod, "cctx_dx"), c_ctx[None, :], "cctx_grad")

    g_ln_g =jnp.stack([jnp.stack([dln[(l, j)][3, 1] for j in range(3)]) for l in range(DEPTH)])
    g_ln_b = jnp.stack([jnp.stack([dln[(l, j)][3, 0] for j in range(3)]) for l in range(DEPTH)])
    sink_row = jnp.sum(g_sink, axis=0)[:4]
    g_sink8 = jnp.concatenate([sink_row[:, 0], sink_row[:, HEAD_DIM]])
    misc = jnp.concatenate([g_sink8, jnp.sum(g_ps, axis=0).reshape(POOL_W), jnp.zeros((D - 8 - POOL_W,), F32)])
    small_g = jnp.concatenate([
        g_ln_g.reshape(6, D), g_ln_b.reshape(6, D), jnp.sum(g_cw, axis=0), jnp.sum(g_cb, axis=0), g_vec,
        misc[None, :], jnp.sum(g_pw, axis=0).reshape(64, D), g_wa.reshape(256, D), g_wx.reshape(256, D), g_cctx,
        jnp.zeros((39, D), F32)], axis=0)
    n_small = small_g.shape[0] // N_CHIP
    last = pending.pop()
    ffn_red = reduce_scatter_chips(last["buf"], f"ffn{last['key']}", wire=BF16, dst=ffn_red[0], g=last["key"],
                                   recv=last.get("recv")).reshape(12 * ns, D)
    small_red = reduce_scatter_chips(small_g.reshape(N_CHIP, 2, n_small // 2, D), "small")
    small_red = all_gather_chips(small_red, "ag_smallg").reshape(N_CHIP * n_small, D)

    ffn_kind = dict(ffn_w_gate=0, ffn_w_up=1, ffn_w_down=2)

    def cols(a):
        return lax.dynamic_slice_in_dim(a, chip * dsh, dsh, axis=a.ndim - 1)

    sr = small_red
    grads = dict(
        c_ctx=sr[600], w_mod=g_w_mod, b_mod=g_b_mod,
        ln_g=cols(sr[0:6]).reshape(2, 3, dsh), ln_b=cols(sr[6:12]).reshape(2, 3, dsh),
        mix_ab_w_in=mix_red["a"][0:n_ai][None], attn_sink=sr[23, 0:8][None], pool_w=sr[24:88].reshape(1, 4, 128, 128),
        pool_scale=sr[23, 8:8 + POOL_W][None], mix_ab_w_out=mix_red["a"][n_ai:][None],
        lru_w_in=mix_red["c"][0:n_li].T[None],
        lru_conv_w=cols(sr[12:16])[None], lru_conv_b=cols(sr[16:17]), lru_wa=sr[88:344].reshape(1, 2, 8, 128, 128),
        lru_ba=cols(sr[17:19])[None], lru_wx=sr[344:600].reshape(1, 2, 8, 128, 128), lru_bx=cols(sr[19:21])[None],
        lru_lambda=cols(sr[21:23])[None], lru_w_out=mix_red["c"][n_li:][None])
    params = dict(c_ctx=(c_ctx, m_c_ctx, v_c_ctx), w_mod=(w_mod, m_w_mod, v_w_mod), b_mod=(b_mod, m_b_mod, v_b_mod),
                  ln_g=(ln_g, m_ln_g, v_ln_g), ln_b=(ln_b, m_ln_b, v_ln_b),
                  ffn_w_gate=(ffn_w_gate, m_ffn_w_gate, v_ffn_w_gate), ffn_w_up=(ffn_w_up, m_ffn_w_up, v_ffn_w_up),
                  ffn_w_down=(ffn_w_down, m_ffn_w_down, v_ffn_w_down),
                  mix_ab_w_in=(mix_ab_w_in, m_mix_ab_w_in, v_mix_ab_w_in), attn_sink=(attn_sink, m_attn_sink, v_attn_sink),
                  pool_w=(pool_w, m_pool_w, v_pool_w), pool_scale=(pool_scale, m_pool_scale, v_pool_scale),
                  mix_ab_w_out=(mix_ab_w_out, m_mix_ab_w_out, v_mix_ab_w_out), lru_w_in=(lru_w_in, m_lru_w_in, v_lru_w_in),
                  lru_conv_w=(lru_conv_w, m_lru_conv_w, v_lru_conv_w), lru_conv_b=(lru_conv_b, m_lru_conv_b, v_lru_conv_b),
                  lru_wa=(lru_wa, m_lru_wa, v_lru_wa), lru_ba=(lru_ba, m_lru_ba, v_lru_ba), lru_wx=(lru_wx, m_lru_wx, v_lru_wx),
                  lru_bx=(lru_bx, m_lru_bx, v_lru_bx), lru_lambda=(lru_lambda, m_lru_lambda, v_lru_lambda),
                  lru_w_out=(lru_w_out, m_lru_w_out, v_lru_w_out))
    gl, dl, ml, vl = [], [], [], []
    transposed = ("ffn_w_gate", "ffn_w_up", "mix_ab_w_in")
    for name, (w, m, v) in params.items():
        if name in transposed:
            w, m, v = (jnp.swapaxes(t, -1, -2) for t in (w, m, v))
        if name in ffn_kind:
            g, d, mn, vn = adamw_ffn(w, m, v, ffn_red, ffn_kind[name], ns, f"adamw_{name}")
        else:
            g = grads[name].reshape(w.shape)
            d, mn, vn = adamw(w, g, m, v, f"adamw_{name}")
        if name in transposed:
            g, d, mn, vn = (jnp.swapaxes(t, -1, -2) for t in (g, d, mn, vn))
        gl.append(g)
        dl.append(d)
        ml.append(mn)
        vl.append(vn)
    return (loss, grad_x, *gl, *dl, *ml, *vl)
```

```python
import functools
import math

import jax
import jax.numpy as jnp
from jax import lax
from jax.experimental import pallas as pl
from jax.experimental.pallas import tpu as pltpu

F32, BF16 = jnp.float32, jnp.bfloat16
MESH = pl.DeviceIdType.MESH
ANY = pl.BlockSpec(memory_space=pl.ANY)
VMEM_SPEC = pl.BlockSpec(memory_space=pltpu.VMEM)

D = 1024
N_CHIP = 4
HEAD_DIM, N_HEADS, N_KV = 64, 8, 2
ATT_W, KV_W, POOL_W = 512, 128, 512
POOL_WINDOWS = (2, 4, 8, 16)
BLK = 128
ATT_SCALE = HEAD_DIM ** -0.5
ROPE_THETA = 10000.0
GRID_W = 64
LRU_C = 8.0
LN_EPS = 1e-5
NEG_INF = -1e30
DEPTH = 2
ALPHA = (2 * DEPTH) ** 0.25
N_MOD = 9
ADAM_LR, ADAM_B1, ADAM_B2, ADAM_EPS, ADAM_WD, ADAM_STEP = 0.001, 0.9, 0.999, 1e-08, 0.01, 10
VMEM_BIG = 48 * 1024 * 1024


def _cp(sem=None, vmem=None):
    kw = {}
    if sem is not None:
        kw["dimension_semantics"] = sem
    if vmem is not None:
        kw["vmem_limit_bytes"] = vmem
    return pltpu.CompilerParams(**kw)


def _sds(shape, dtype):
    return jax.ShapeDtypeStruct(tuple(shape), dtype)


def _pick(n, cands):
    for c in cands:
        if n % c == 0:
            return c
    return n


def _dot(a, b, dims):
    return lax.dot_general(a, b, (dims, ((), ())), preferred_element_type=F32)


def _nn(a, b):
    return _dot(a, b, ((1,), (0,)))


def _nt(a, b):
    return _dot(a, b, ((1,), (1,)))


def _tn(a, b):
    return _dot(a, b, ((0,), (0,)))


def _sigmoid(x):
    return 0.5 * jnp.tanh(0.5 * x) + 0.5


def _me():
    return lax.axis_index("x"), lax.axis_index("y"), lax.axis_index("c")


def _rcopy(src, dst, ssem, rsem, dev):
    return pltpu.make_async_remote_copy(src_ref=src, dst_ref=dst, send_sem=ssem, recv_sem=rsem,
                                        device_id=dev, device_id_type=MESH)


def all_gather8(x, name):
    def body(x_ref, o_ref, ssem, rsem, lsem):
        mx, my, mc = _me()
        me = 4 * mx + 2 * my + mc
        loc = pltpu.make_async_copy(x_ref, o_ref.at[me], lsem)
        loc.start()
        peers = []
        for m in range(1, 8):
            px = 1 - mx if (m >> 2) & 1 else mx
            py = 1 - my if (m >> 1) & 1 else my
            pc = 1 - mc if m & 1 else mc
            peers.append((px, py, pc))
        sends = [_rcopy(x_ref, o_ref.at[me], ssem.at[k], rsem.at[k], p) for k, p in enumerate(peers)]
        for cp in sends:
            cp.start()
        for k, (px, py, pc) in enumerate(peers):
            _rcopy(x_ref, o_ref.at[4 * px + 2 * py + pc], ssem.at[k], rsem.at[k], (px, py, pc)).wait_recv()
        for cp in sends:
            cp.wait_send()
        loc.wait()

    return pl.pallas_call(
        body, name=name, out_shape=_sds((8,) + x.shape, x.dtype),
        in_specs=[VMEM_SPEC], out_specs=VMEM_SPEC,
        scratch_shapes=[pltpu.SemaphoreType.DMA((7,)), pltpu.SemaphoreType.DMA((7,)), pltpu.SemaphoreType.DMA],
    )(x)


_ROW_BLOCKS = (512, 384, 352, 256, 224, 128)


def _idx(v):
    return jnp.reshape(v, (1,)).astype(jnp.int32)


def place_slab(shard, name):
    _, h, w = shard.shape
    th = _pick(h, _ROW_BLOCKS)

    def body(s_ref, x_ref, o_ref):
        del s_ref
        o_ref[...] = x_ref[...]

    return pl.pallas_call(
        body, name=name, out_shape=_sds((N_CHIP,) + shard.shape, shard.dtype),
        grid_spec=pltpu.PrefetchScalarGridSpec(
            num_scalar_prefetch=1, grid=(2, h // th),
            in_specs=[pl.BlockSpec((None, th, w), lambda k, r, s: (k, r, 0))],
            out_specs=pl.BlockSpec((None, None, th, w), lambda k, r, s: (s[0], k, r, 0))),
    )(_idx(2 * lax.axis_index("x") + lax.axis_index("y")), shard)


def place_rows(buf, rows, r0, name):
    e, w = rows.shape
    tb = 64

    def body(s_ref, x_ref, b_ref, o_ref):
        del s_ref, b_ref
        o_ref[...] = x_ref[...]

    return pl.pallas_call(
        body, name=name, out_shape=_sds(buf.shape, buf.dtype),
        grid_spec=pltpu.PrefetchScalarGridSpec(
            num_scalar_prefetch=1, grid=(e // tb,),
            in_specs=[pl.BlockSpec((tb, w), lambda j, s: (j, 0)), ANY],
            out_specs=pl.BlockSpec((None, tb, w), lambda j, s: (s[0], r0 // tb + j, 0))),
        input_output_aliases={2: 0},
    )(_idx(2 * lax.axis_index("x") + lax.axis_index("y")), rows, buf)


def ffn_place(w_gate_t, w_up_t, w_down, l, f, name, extra=0):
    ns = w_down.shape[-2]
    tr, nb = ns // 2, 2

    def body(s_ref, g_ref, u_ref, d_ref, o_ref):
        del s_ref
        k = pl.program_id(0)

        @pl.when(k == 0)
        def _():
            o_ref[...] = g_ref[...].astype(BF16)

        @pl.when(k == 1)
        def _():
            o_ref[...] = u_ref[...].astype(BF16)

        @pl.when(k == 2)
        def _():
            o_ref[...] = d_ref[...].astype(BF16)

    def spec(q):
        return pl.BlockSpec((None, None, tr, D), lambda k, j, s: (l, f, jnp.where(k == q, j, 0), 0))

    return pl.pallas_call(
        body, name=name, out_shape=_sds((N_CHIP, 3 * ns + extra, D), BF16),
        grid_spec=pltpu.PrefetchScalarGridSpec(
            num_scalar_prefetch=1, grid=(3, nb), in_specs=[spec(0), spec(1), spec(2)],
            out_specs=pl.BlockSpec((None, tr, D), lambda k, j, s: (s[0], k * nb + j, 0))),
    )(_idx(2 * lax.axis_index("x") + lax.axis_index("y")), w_gate_t, w_up_t, w_down)


def all_gather_chips(shard, name):
    return gather_placed(place_slab(shard, name + "_place"), name)


def gather_placed(full, name):
    h = full.shape[2]
    lo, hi = pl.ds(0, h // 2), pl.ds(h // 2, h - h // 2)

    def body(x_ref, o_ref, ssem, rsem):
        del x_ref
        mx, my, mc = _me()
        s, xs, ys, ds = 2 * mx + my, 2 * (1 - mx) + my, 2 * mx + (1 - my), 2 * (1 - mx) + (1 - my)
        xn, yn, sib = (1 - mx, my, mc), (mx, 1 - my, mc), (mx, my, 1 - mc)

        def cp(k, src, dst, dev):
            return _rcopy(src, dst, ssem.at[k], rsem.at[k], dev)

        own = o_ref.at[s, mc]
        sent = [cp(0, own, own, xn), cp(1, own, own, yn)]
        for c in sent:
            c.start()
        cp(0, own, o_ref.at[xs, mc], xn).wait_recv()
        sent += [cp(2, o_ref.at[xs, mc, lo], o_ref.at[xs, mc, lo], yn), cp(4, o_ref.at[xs, mc], o_ref.at[xs, mc], sib)]
        sent[-2].start()
        sent[-1].start()
        cp(1, own, o_ref.at[ys, mc], yn).wait_recv()
        sent += [cp(3, o_ref.at[ys, mc, hi], o_ref.at[ys, mc, hi], xn), cp(5, o_ref.at[ys, mc], o_ref.at[ys, mc], sib)]
        sent[-2].start()
        sent[-1].start()
        cp(2, own, o_ref.at[ds, mc, lo], yn).wait_recv()
        cp(3, own, o_ref.at[ds, mc, hi], xn).wait_recv()
        sent.append(cp(6, o_ref.at[ds, mc], o_ref.at[ds, mc], sib))
        sent[-1].start()
        for k, slot in ((4, xs), (5, ys), (6, ds)):
            cp(k, own, o_ref.at[slot, 1 - mc], sib).wait_recv()
        for c in sent:
            c.wait_send()

    return pl.pallas_call(
        body, name=name, out_shape=_sds(full.shape, full.dtype), in_specs=[ANY], out_specs=ANY,
        input_output_aliases={0: 0},
        scratch_shapes=[pltpu.SemaphoreType.DMA((7,)), pltpu.SemaphoreType.DMA((7,))],
    )(full)


def sibling_send_other_half(buf, name):
    def body(x_ref, o_ref, ssem, rsem):
        mx, my, mc = _me()
        sib = (mx, my, 1 - mc)
        cps = [_rcopy(x_ref.at[k, 1 - mc], o_ref.at[k], ssem.at[k], rsem.at[k], sib) for k in range(N_CHIP)]
        for cp in cps:
            cp.start()
        for cp in cps:
            cp.wait_recv()
        for cp in cps:
            cp.wait_send()

    n, _, h, w = buf.shape
    return pl.pallas_call(
        body, name=name, out_shape=_sds((n, h, w), buf.dtype), in_specs=[ANY], out_specs=ANY,
        scratch_shapes=[pltpu.SemaphoreType.DMA((N_CHIP,)), pltpu.SemaphoreType.DMA((N_CHIP,))],
    )(buf)


def chips_all_to_all(q, name):
    def body(x_ref, o_ref, ssem, rsem):
        mx, my, mc = _me()
        s = 2 * mx + my
        chips = [(1 - mx, my), (mx, 1 - my), (1 - mx, 1 - my)]
        cps = [_rcopy(x_ref.at[2 * px + py], o_ref.at[s], ssem.at[j], rsem.at[j], (px, py, mc))
               for j, (px, py) in enumerate(chips)]
        for cp in cps:
            cp.start()
        for j, (px, py) in enumerate(chips):
            ps = 2 * px + py
            _rcopy(x_ref.at[ps], o_ref.at[ps], ssem.at[j], rsem.at[j], (px, py, mc)).wait_recv()
        for cp in cps:
            cp.wait_send()

    return pl.pallas_call(
        body, name=name, out_shape=_sds(q.shape, q.dtype), in_specs=[ANY], out_specs=ANY,
        scratch_shapes=[pltpu.SemaphoreType.DMA((3,)), pltpu.SemaphoreType.DMA((3,))],
    )(q)


def sibling_join_halves(both, name, g=None):
    def body(x_ref, o_ref, ssem, rsem):
        del x_ref
        mx, my, mc = _me()
        sib = (mx, my, 1 - mc)
        o = o_ref if g is None else o_ref.at[g]
        cp = _rcopy(o.at[mc], o.at[mc], ssem, rsem, sib)
        cp.start()
        _rcopy(o.at[1 - mc], o.at[1 - mc], ssem, rsem, sib).wait_recv()
        cp.wait_send()

    return pl.pallas_call(
        body, name=name, out_shape=_sds(both.shape, both.dtype), in_specs=[ANY], out_specs=ANY,
        input_output_aliases={0: 0}, scratch_shapes=[pltpu.SemaphoreType.DMA, pltpu.SemaphoreType.DMA],
    )(both)


def add_own_half(buf, recv, wire, name):
    n, _, h, w = buf.shape
    th = _pick(h, _ROW_BLOCKS)

    def body(c_ref, a_ref, b_ref, o_ref):
        del c_ref
        o_ref[...] = (a_ref[...] + b_ref[...]).astype(o_ref.dtype)

    return pl.pallas_call(
        body, name=name, out_shape=_sds((n, h, w), wire),
        grid_spec=pltpu.PrefetchScalarGridSpec(
            num_scalar_prefetch=1, grid=(n, h // th),
            in_specs=[pl.BlockSpec((None, None, th, w), lambda k, r, c: (k, c[0], r, 0)),
                      pl.BlockSpec((None, th, w), lambda k, r, c: (k, r, 0))],
            out_specs=pl.BlockSpec((None, th, w), lambda k, r, c: (k, r, 0))),
    )(_idx(lax.axis_index("c")), buf, recv)


def sum_slots(q, r, name, dst=None, g=None):
    n, h, w = r.shape
    th = _pick(h, _ROW_BLOCKS)

    def body(i_ref, q_ref, r1, r2, r3, *rest):
        del i_ref
        rest[-1][...] = ((q_ref[...].astype(F32) + r1[...].astype(F32)) + r2[...].astype(F32)) + r3[...].astype(F32)

    def slot(d):
        return lambda i, ix: ((ix[0] + d) % N_CHIP, i, 0)

    idx = jnp.stack([2 * lax.axis_index("x") + lax.axis_index("y"), lax.axis_index("c")]).astype(jnp.int32)
    in_specs = [pl.BlockSpec((None, th, w), slot(d)) for d in (0, 1, 2, 3)]
    if dst is None:
        return pl.pallas_call(
            body, name=name, out_shape=_sds((2, h, w), F32),
            grid_spec=pltpu.PrefetchScalarGridSpec(
                num_scalar_prefetch=1, grid=(h // th,), in_specs=in_specs,
                out_specs=pl.BlockSpec((None, th, w), lambda i, ix: (ix[1], i, 0))),
        )(idx, q, r, r, r)
    return pl.pallas_call(
        body, name=name, out_shape=_sds(dst.shape, F32),
        grid_spec=pltpu.PrefetchScalarGridSpec(
            num_scalar_prefetch=1, grid=(h // th,), in_specs=in_specs + [ANY],
            out_specs=pl.BlockSpec((None, None, th, w), lambda i, ix: (g, ix[1], i, 0))),
        input_output_aliases={5: 0},
    )(idx, q, r, r, r, dst)


def reduce_scatter_chips(buf, tag, wire=F32, dst=None, g=None, recv=None):
    if recv is None:
        recv = sibling_send_other_half(buf, f"rs_sib_{tag}")
    q = add_own_half(buf, recv, wire, f"rs_add2_{tag}")
    r = chips_all_to_all(q, f"rs_a2a_{tag}")
    red = sum_slots(q, r, f"rs_add4_{tag}", dst=dst, g=g)
    return sibling_join_halves(red, f"rs_join_{tag}", g=g)


class Layout:
    def __init__(self, n_ctx, n_lat):
        self.C, self.L = n_ctx, n_lat
        self.PS = n_ctx + n_lat
        self.T = 2 * self.PS
        self.tr = _pick(math.gcd(n_ctx, n_lat), (256, 128))
        self.bps = self.PS // self.tr
        self.cb = n_ctx // self.tr
        self.nblk = self.T // self.tr
        self.tm = _pick(self.T, (1152, 768, 512, 256, 128))
        self.tm2 = _pick(self.T, (2304, 1152, 768, 512, 256, 128))
        self.tc = _pick(self.T, (768, 512, 256, 128))

    def seg(self, i):
        return jnp.where(i % self.bps < self.cb, 2, i // self.bps)


def rowwise(lay, name, fn, rows, segs=(), vecs=(), outs=(), sums=(), rider=None):
    tr, nblk = lay.tr, lay.nblk
    n_r, n_s, n_v, n_o = len(rows), len(segs), len(vecs), len(outs)
    lat_only = any(o[2:] for o in outs) or any(a.shape[0] != lay.T for a in rows)
    nsub = 1 if lat_only else (3 if nblk % 3 == 0 else 2 if nblk % 2 == 0 else 1)
    tb = tr * nsub

    def body(*refs):
        ins = refs[:n_r + n_s + n_v]
        ors = refs[n_r + n_s + n_v:]
        for sub in range(nsub):
            rs = slice(sub * tr, (sub + 1) * tr)
            seg = lay.seg(pl.program_id(0) * nsub + sub)
            vals = [r[rs, :] for r in ins[:n_r]] + [r[seg] for r in ins[n_r:n_r + n_s]] + [r[...] for r in ins[n_r + n_s:]]
            res = fn(*vals)
            for k in range(n_o):
                ors[k][rs, :] = res[k].astype(ors[k].dtype)
            for k in range(len(sums)):
                ors[n_o + k][sub] = res[n_o + k]

    def all_rows(i):
        return (i, 0)

    def lat_rows(i):
        return ((i // lay.bps) * (lay.bps - lay.cb) + jnp.maximum(i % lay.bps - lay.cb, 0), 0)

    in_specs = [pl.BlockSpec((tb, a.shape[1]), all_rows if a.shape[0] == lay.T else lat_rows) for a in rows]
    in_specs += [pl.BlockSpec(a.shape, lambda i: (0, 0, 0)) for a in segs]
    in_specs += [pl.BlockSpec(a.shape, lambda i: (0, 0)) for a in vecs]
    out_shape = [_sds((2 * lay.L if o[2:] else lay.T, o[0]), o[1]) for o in outs]
    out_shape += [_sds((nblk, r, w), F32) for r, w in sums]
    out_specs = [pl.BlockSpec((tb, o[0]), lat_rows if o[2:] else all_rows) for o in outs]
    out_specs += [pl.BlockSpec((nsub, r, w), lambda i: (i, 0, 0)) for r, w in sums]
    sem = "arbitrary" if any(o[2:] for o in outs) else "parallel"
    if rider is None:
        return pl.pallas_call(body, name=name, out_shape=out_shape, grid=(nblk // nsub,), in_specs=in_specs,
                              out_specs=out_specs, compiler_params=_cp((sem,), VMEM_BIG))(*rows, *segs, *vecs)
    return _host_call(body, rider, name, (nblk // nsub,), in_specs, out_specs, out_shape, (*rows, *segs, *vecs), (sem,),
                      n_r + n_s + n_v, n_o + len(sums))


def modulate(lay, h, mod, k_shift, k_scale, name):
    def fn(hb, m):
        return (hb * (1.0 + m[k_scale:k_scale + 1]) + m[k_shift:k_shift + 1],)
    return rowwise(lay, name, fn, [h], segs=[mod], outs=[(D, BF16)])[0]


def resid_ln(lay, h, y, mod, k_gate, coef, lnv, name, nxt=None, prev_ln=None, rider=None):
    def fn(hb, yb, m, *rest):
        ln = rest[-1]
        if prev_ln is not None:
            hb = hb * rest[-2][0:1] + rest[-2][1:2]
        z = ALPHA * hb + (coef * m[k_gate:k_gate + 1]) * yb
        mu = jnp.mean(z, axis=-1, keepdims=True)
        zc = z - mu
        var = jnp.mean(zc * zc, axis=-1, keepdims=True)
        rstd = lax.rsqrt(var + LN_EPS)
        xhat = zc * rstd
        out = xhat * ln[0:1] + ln[1:2]
        if nxt is None:
            return xhat, rstd
        mn = rest[0]
        return xhat, rstd, out * (1.0 + mn[nxt[2]:nxt[2] + 1]) + mn[nxt[1]:nxt[1] + 1]
    segs = [mod] if nxt is None else [mod, nxt[0]]
    vecs = [lnv] if prev_ln is None else [prev_ln, lnv]
    outs = [(D, F32), (1, F32)] + ([] if nxt is None else [(D, BF16)])
    return rowwise(lay, name, fn, [h, y], segs=segs, vecs=vecs, outs=outs, rider=rider)


def _ln_bwd_math(do, xh, rs, yb, gate, coef, ln):
    dxh = do * ln[0:1]
    m1 = jnp.mean(dxh, axis=-1, keepdims=True)
    m2 = jnp.mean(dxh * xh, axis=-1, keepdims=True)
    dz = rs * (dxh - m1 - xh * m2)
    s = jnp.concatenate([jnp.sum(do, axis=0, keepdims=True), jnp.sum(do * xh, axis=0, keepdims=True),
                         jnp.sum(coef * dz * yb, axis=0, keepdims=True)], axis=0)
    return (coef * gate) * dz, ALPHA * dz, s


def _mod_bwd_math(dr, dm, hb, scale):
    s = jnp.concatenate([jnp.sum(dm, axis=0, keepdims=True), jnp.sum(dm * hb, axis=0, keepdims=True)], axis=0)
    return dr + dm * (1.0 + scale), s


def mod_bwd(lay, dres, dhm, h, mod, k_scale, name, rider=None):
    def fn(dr, dm, hb, m):
        return _mod_bwd_math(dr, dm, hb, m[k_scale:k_scale + 1])
    return rowwise(lay, name, fn, [dres, dhm, h], segs=[mod], outs=[(D, F32, "lat")], sums=[(2, D)], rider=rider)


def modb_lnb(lay, dres, dhm, mod, k_scale, xhat, rstd, y, mod_p, k_gate, coef, lnv, name, rider=None):
    def fn(dr, dm, xh, rs, yb, m, mp, ln):
        dh, s2 = _mod_bwd_math(dr, dm, xh * ln[0:1] + ln[1:2], m[k_scale:k_scale + 1])
        dy, dres_p, s1 = _ln_bwd_math(dh, xh, rs, yb, mp[k_gate:k_gate + 1], coef, ln)
        return dy, dres_p, s1, s2
    return rowwise(lay, name, fn, [dres, dhm, xhat, rstd, y], segs=[mod, mod_p], vecs=[lnv],
                   outs=[(D, BF16), (D, F32)], sums=[(3, D), (2, D)], rider=rider)


def block_sums(lay, parts_list, name):
    n = len(parts_list)

    def body(*refs):
        for p_ref, o_ref in zip(refs[:n], refs[n:]):
            acc = [None, None, None]
            for i in range(lay.nblk):
                sg = 2 if i % lay.bps < lay.cb else i // lay.bps
                acc[sg] = p_ref[i] if acc[sg] is None else acc[sg] + p_ref[i]
            for k in range(3):
                o_ref[k] = acc[k]
            o_ref[3] = (acc[0] + acc[1]) + acc[2]

    return pl.pallas_call(body, name=name, out_shape=[_sds((4,) + p.shape[1:], F32) for p in parts_list],
                          in_specs=[VMEM_SPEC] * n, out_specs=[VMEM_SPEC] * n)(*parts_list)


def mm_nn(a, b, name, out_dtype=F32):
    m, k = a.shape
    n = b.shape[1]
    tm = _pick(m, (1152, 768, 512, 256, 128, 64, 32, 16, 8))
    tn = _pick(n, (1024, 768, 640, 512, 384, 256, 128))

    def body(a_ref, b_ref, o_ref):
        o_ref[...] = _nn(a_ref[...].astype(BF16), b_ref[...].astype(BF16)).astype(o_ref.dtype)

    return pl.pallas_call(body, name=name, out_shape=_sds((m, n), out_dtype), grid=(m // tm, n // tn),
                          in_specs=[pl.BlockSpec((tm, k), lambda i, j: (i, 0)), pl.BlockSpec((k, tn), lambda i, j: (0, j))],
                          out_specs=pl.BlockSpec((tm, tn), lambda i, j: (i, j)),
                          compiler_params=_cp(("parallel", "parallel"), VMEM_BIG))(a, b)


def mm_nt(a, b, name, out_dtype=F32, rider=None):
    m, k = a.shape
    n = b.shape[0]
    tm = _pick(m, (1152, 768, 512, 256, 128, 64, 32, 16, 8))
    tn = _pick(n, (1024, 768, 640, 512, 384, 256, 128))

    def body(a_ref, b_ref, o_ref):
        o_ref[...] = _nt(a_ref[...].astype(BF16), b_ref[...].astype(BF16)).astype(o_ref.dtype)

    res = _host_call(body, rider, name, (m // tm, n // tn),
                     [pl.BlockSpec((tm, k), lambda i, j: (i, 0)), pl.BlockSpec((tn, k), lambda i, j: (j, 0))],
                     [pl.BlockSpec((tm, tn), lambda i, j: (i, j))], [_sds((m, n), out_dtype)], (a, b),
                     ("parallel", "parallel"), 2, 1)
    return res[0] if rider is None else res


def mm_tn(a, b, name, rider=None):
    t, m = a.shape
    n = b.shape[1]
    tk = _pick(t, (1152, 768, 512, 256, 128, 64, 32, 16))
    tm = _pick(m, (512, 384, 256, 128))

    def body(a_ref, b_ref, o_ref):
        @pl.when(pl.program_id(1) == 0)
        def _():
            o_ref[...] = jnp.zeros_like(o_ref)
        o_ref[...] += _tn(a_ref[...].astype(BF16), b_ref[...].astype(BF16))

    res = _host_call(body, rider, name, (m // tm, t // tk),
                     [pl.BlockSpec((tk, tm), lambda i, k: (k, i)), pl.BlockSpec((tk, n), lambda i, k: (k, 0))],
                     [pl.BlockSpec((tm, n), lambda i, k: (i, 0))], [_sds((m, n), F32)], (a, b),
                     ("parallel", "arbitrary"), 2, 1)
    return res[0] if rider is None else res


class Rider:
    def __init__(self, ins, outs, aliases, nsem, start, wait):
        self.ins, self.outs, self.aliases, self.nsem, self.start, self.wait = ins, outs, aliases, nsem, start, wait


def _chips_of(mx, my):
    return [(1 - mx, my), (mx, 1 - my), (1 - mx, 1 - my)]


def rider_gather_d2d(buf):
    def start(ins, outs, ssem, rsem):
        o = outs[0]
        mx, my, mc = _me()
        for j, (px, py) in enumerate(_chips_of(mx, my)):
            ps = 2 * px + py
            _rcopy(o.at[ps, mc], o.at[ps, mc], ssem.at[j], rsem.at[j], (mx, my, 1 - mc)).start()

    def wait(ins, outs, ssem, rsem):
        o = outs[0]
        mx, my, mc = _me()
        sib = (mx, my, 1 - mc)
        for j, (px, py) in enumerate(_chips_of(mx, my)):
            ps = 2 * px + py
            _rcopy(o.at[ps, 1 - mc], o.at[ps, 1 - mc], ssem.at[j], rsem.at[j], sib).wait_recv()
        for j, (px, py) in enumerate(_chips_of(mx, my)):
            ps = 2 * px + py
            _rcopy(o.at[ps, mc], o.at[ps, mc], ssem.at[j], rsem.at[j], sib).wait_send()

    return Rider([buf], [_sds(buf.shape, buf.dtype)], {0: 0}, 3, start, wait)


def rider_reduce_sib(buf):
    n, _, h, w = buf.shape

    def start(ins, outs, ssem, rsem):
        mx, my, mc = _me()
        for k in range(N_CHIP):
            _rcopy(ins[0].at[k, 1 - mc], outs[0].at[k], ssem.at[k], rsem.at[k], (mx, my, 1 - mc)).start()

    def wait(ins, outs, ssem, rsem):
        mx, my, mc = _me()
        for k in range(N_CHIP):
            _rcopy(ins[0].at[k, 1 - mc], outs[0].at[k], ssem.at[k], rsem.at[k], (mx, my, 1 - mc)).wait_recv()
        for k in range(N_CHIP):
            _rcopy(ins[0].at[k, 1 - mc], outs[0].at[k], ssem.at[k], rsem.at[k], (mx, my, 1 - mc)).wait_send()

    return Rider([buf], [_sds((n, h, w), buf.dtype)], {}, N_CHIP, start, wait)


def rider_gather_xy(buf):
    def peers():
        mx, my, mc = _me()
        return 2 * mx + my, mc, [(1 - mx, my), (mx, 1 - my)]

    def start(ins, outs, ssem, rsem):
        o = outs[0]
        s, mc, nb = peers()
        for j, (px, py) in enumerate(nb):
            _rcopy(o.at[s, mc], o.at[s, mc], ssem.at[j], rsem.at[j], (px, py, mc)).start()

    def wait(ins, outs, ssem, rsem):
        o = outs[0]
        s, mc, nb = peers()
        for j, (px, py) in enumerate(nb):
            _rcopy(o.at[2 * px + py, mc], o.at[2 * px + py, mc], ssem.at[j], rsem.at[j], (px, py, mc)).wait_recv()
        for j, (px, py) in enumerate(nb):
            _rcopy(o.at[s, mc], o.at[s, mc], ssem.at[j], rsem.at[j], (px, py, mc)).wait_send()

    return Rider([buf], [_sds(buf.shape, buf.dtype)], {0: 0}, 2, start, wait)


def rider_gather_fwd(buf):
    h2 = buf.shape[2] // 2
    lo, hi = pl.ds(0, h2), pl.ds(h2, buf.shape[2] - h2)

    def start(ins, outs, ssem, rsem):
        o = outs[0]
        mx, my, mc = _me()
        xs, ys = 2 * (1 - mx) + my, 2 * mx + (1 - my)
        _rcopy(o.at[xs, mc, lo], o.at[xs, mc, lo], ssem.at[0], rsem.at[0], (mx, 1 - my, mc)).start()
        _rcopy(o.at[ys, mc, hi], o.at[ys, mc, hi], ssem.at[1], rsem.at[1], (1 - mx, my, mc)).start()

    def wait(ins, outs, ssem, rsem):
        o = outs[0]
        mx, my, mc = _me()
        xs, ys, ds = 2 * (1 - mx) + my, 2 * mx + (1 - my), 2 * (1 - mx) + (1 - my)
        _rcopy(o.at[ds, mc, lo], o.at[ds, mc, lo], ssem.at[0], rsem.at[0], (mx, 1 - my, mc)).wait_recv()
        _rcopy(o.at[ds, mc, hi], o.at[ds, mc, hi], ssem.at[1], rsem.at[1], (1 - mx, my, mc)).wait_recv()
        _rcopy(o.at[xs, mc, lo], o.at[xs, mc, lo], ssem.at[0], rsem.at[0], (mx, 1 - my, mc)).wait_send()
        _rcopy(o.at[ys, mc, hi], o.at[ys, mc, hi], ssem.at[1], rsem.at[1], (1 - mx, my, mc)).wait_send()

    return Rider([buf], [_sds(buf.shape, buf.dtype)], {0: 0}, 2, start, wait)


def rider_reduce_copy(q, j, r=None):
    def peer():
        mx, my, mc = _me()
        px, py = _chips_of(mx, my)[j]
        return 2 * mx + my, 2 * px + py, (px, py, mc)

    def start(ins, outs, ssem, rsem):
        s, ps, dev = peer()
        _rcopy(ins[0].at[ps], outs[0].at[s], ssem.at[0], rsem.at[0], dev).start()

    def wait(ins, outs, ssem, rsem):
        s, ps, dev = peer()
        _rcopy(ins[0].at[ps], outs[0].at[ps], ssem.at[0], rsem.at[0], dev).wait_recv()
        _rcopy(ins[0].at[ps], outs[0].at[s], ssem.at[0], rsem.at[0], dev).wait_send()

    if r is None:
        return Rider([q], [_sds(q.shape, q.dtype)], {}, 1, start, wait)
    return Rider([q, r], [_sds(q.shape, q.dtype)], {1: 0}, 1, start, wait)


def rider_reduce_copies(q):
    def start(ins, outs, ssem, rsem):
        mx, my, mc = _me()
        s = 2 * mx + my
        for j, (px, py) in enumerate(_chips_of(mx, my)):
            _rcopy(ins[0].at[2 * px + py], outs[0].at[s], ssem.at[j], rsem.at[j], (px, py, mc)).start()

    def wait(ins, outs, ssem, rsem):
        mx, my, mc = _me()
        s = 2 * mx + my
        for j, (px, py) in enumerate(_chips_of(mx, my)):
            ps = 2 * px + py
            _rcopy(ins[0].at[ps], outs[0].at[ps], ssem.at[j], rsem.at[j], (px, py, mc)).wait_recv()
        for j, (px, py) in enumerate(_chips_of(mx, my)):
            _rcopy(ins[0].at[2 * px + py], outs[0].at[s], ssem.at[j], rsem.at[j], (px, py, mc)).wait_send()

    return Rider([q], [_sds(q.shape, q.dtype)], {}, 3, start, wait)


def rider_join(buf, g=None):
    def start(ins, outs, ssem, rsem):
        o = outs[0] if g is None else outs[0].at[g]
        mx, my, mc = _me()
        _rcopy(o.at[mc], o.at[mc], ssem.at[0], rsem.at[0], (mx, my, 1 - mc)).start()

    def wait(ins, outs, ssem, rsem):
        o = outs[0] if g is None else outs[0].at[g]
        mx, my, mc = _me()
        _rcopy(o.at[1 - mc], o.at[1 - mc], ssem.at[0], rsem.at[0], (mx, my, 1 - mc)).wait_recv()
        _rcopy(o.at[mc], o.at[mc], ssem.at[0], rsem.at[0], (mx, my, 1 - mc)).wait_send()

    return Rider([buf], [_sds(buf.shape, buf.dtype)], {0: 0}, 1, start, wait)


def _host_call(body, rider, name, grid, in_specs, out_specs, out_shape, operands, sem, n_in, n_out, aliases=None):
    aliases = dict(aliases or {})
    if rider is None:
        return pl.pallas_call(body, name=name, out_shape=out_shape, grid=grid, in_specs=in_specs, out_specs=out_specs,
                              input_output_aliases=aliases, compiler_params=_cp(sem, VMEM_BIG))(*operands)
    n_ri, n_ro = len(rider.ins), len(rider.outs)
    aliases.update({n_in + a: n_out + b for a, b in rider.aliases.items()})

    def hosted(*refs):
        ins, r_in = refs[:n_in], refs[n_in:n_in + n_ri]
        outs, r_out = refs[n_in + n_ri:n_in + n_ri + n_out], refs[n_in + n_ri + n_out:n_in + n_ri + n_out + n_ro]
        ssem, rsem = refs[-2], refs[-1]
        first = functools.reduce(lambda a, b: a & b, [pl.program_id(k) == 0 for k in range(len(grid))])
        last = functools.reduce(lambda a, b: a & b, [pl.program_id(k) == grid[k] - 1 for k in range(len(grid))])

        @pl.when(first)
        def _():
            rider.start(r_in, r_out, ssem, rsem)
        body(*ins, *outs)

        @pl.when(last)
        def _():
            rider.wait(r_in, r_out, ssem, rsem)

    return pl.pallas_call(
        hosted, name=name, out_shape=list(out_shape) + list(rider.outs), grid=grid,
        in_specs=list(in_specs) + [ANY] * n_ri, out_specs=list(out_specs) + [ANY] * n_ro,
        input_output_aliases=aliases,
        scratch_shapes=[pltpu.SemaphoreType.DMA((rider.nsem,)), pltpu.SemaphoreType.DMA((rider.nsem,))],
        compiler_params=_cp(("arbitrary",) * len(grid), VMEM_BIG))(*operands, *rider.ins)


def ffn_up(lay, hm, wbuf, ig, iu, ns, name, rider=None):
    tm = lay.tm

    def body(h_ref, wg_ref, wu_ref, up_ref, sl_ref, a_ref):
        hb = h_ref[...]
        g = _nt(hb, wg_ref[0])
        u = _nt(hb, wu_ref[0])
        sg = _sigmoid(g)
        sl = g * sg
        up_ref[0] = (u * (sg + sl * (1.0 - sg))).astype(BF16)
        sl_ref[0] = sl.astype(BF16)
        a_ref[0] = (sl * u).astype(BF16)

    spec_o = pl.BlockSpec((1, tm, ns), lambda s, i: (s, i, 0))
    return _host_call(
        body, rider, name, (N_CHIP, lay.T // tm),
        [pl.BlockSpec((tm, D), lambda s, i: (i, 0)), pl.BlockSpec((1, ns, D), lambda s, i: (s, ig, 0)),
         pl.BlockSpec((1, ns, D), lambda s, i: (s, iu, 0))],
        [spec_o] * 3, [_sds((N_CHIP, lay.T, ns), BF16)] * 3, (hm, wbuf, wbuf), ("parallel", "parallel"), 3, 3)


def slab_nn_acc(lay, zs, wbuf, idxs, ns, name, rider=None):
    tm = lay.tm2
    npair = len(zs)

    def body(*refs):
        o_ref = refs[-1]

        @pl.when(pl.program_id(1) == 0)
        def _():
            o_ref[...] = jnp.zeros_like(o_ref)
        acc = _nn(refs[0][0], refs[npair][0])
        for p in range(1, npair):
            acc += _nn(refs[p][0], refs[npair + p][0])
        o_ref[...] += acc

    in_specs = [pl.BlockSpec((1, tm, ns), lambda i, s: (s, i, 0)) for _ in zs]
    in_specs += [pl.BlockSpec((1, ns, D), functools.partial(lambda i, s, q: (s, q, 0), q=q)) for q in idxs]
    return _host_call(body, rider, name, (lay.T // tm, N_CHIP), in_specs, [pl.BlockSpec((tm, D), lambda i, s: (i, 0))],
                      [_sds((lay.T, D), F32)], (*zs, *([wbuf] * npair)), ("parallel", "arbitrary"), 2 * npair, 1)


def ffn_bwd_da(lay, dy, wbuf, idn, up, sl, ns, name, rider=None):
    tm = lay.tm

    def body(dy_ref, wd_ref, up_ref, sl_ref, dg_ref, du_ref):
        da = _nt(dy_ref[...], wd_ref[0])
        dg_ref[0] = (da * up_ref[0].astype(F32)).astype(BF16)
        du_ref[0] = (da * sl_ref[0].astype(F32)).astype(BF16)

    spec_z = pl.BlockSpec((1, tm, ns), lambda s, i: (s, i, 0))
    return _host_call(
        body, rider, name, (N_CHIP, lay.T // tm),
        [pl.BlockSpec((tm, D), lambda s, i: (i, 0)), pl.BlockSpec((1, ns, D), lambda s, i: (s, idn, 0)), spec_z, spec_z],
        [spec_z] * 2, [_sds((N_CHIP, lay.T, ns), BF16)] * 2, (dy, wbuf, up, sl), ("parallel", "parallel"), 4, 2)


def slab_tn(lay, z, x, gbuf, idx, ns, name, rider=None):
    tk = _pick(lay.T, (768, 512, 256, 128))

    def body(z_ref, x_ref, g_in, o_ref):
        del g_in

        @pl.when(pl.program_id(0) == 0)
        def _():
            o_ref[...] = jnp.zeros_like(o_ref)
        xv = x_ref[...]
        for s in range(N_CHIP):
            o_ref[s] += _tn(z_ref[s], xv)

    return _host_call(
        body, rider, name, (lay.T // tk,),
        [pl.BlockSpec((N_CHIP, tk, ns), lambda k: (0, k, 0)), pl.BlockSpec((tk, D), lambda k: (k, 0)), ANY],
        [pl.BlockSpec((N_CHIP, ns, D), lambda k: (0, idx, 0))], [_sds(gbuf.shape, F32)], (z, x, gbuf),
        ("arbitrary",), 3, 1, aliases={2: 0})


Q0, K0, V0, U0, QR0, KR0, PEXT = 0, 512, 640, 768, 1280, 1792, 1920


def rope_fwd(lay, p, cos, sin, name):
    def fn(pb, cs, sn):
        cs4 = jnp.concatenate([cs] * 4, axis=1)
        sn4 = jnp.concatenate([sn] * 4, axis=1)
        qr = pb[:, Q0:K0] * cs4 + pb[:, QR0:KR0] * sn4
        kr = pb[:, K0:V0] * cs + pb[:, KR0:PEXT] * sn
        return qr, kr, pb[:, V0:U0], pb[:, U0:QR0]
    return rowwise(lay, name, fn, [p, cos, sin], outs=[(ATT_W, BF16), (KV_W, BF16), (KV_W, BF16), (POOL_W, F32)])


def rope_bwd(lay, dqr, dkr, dv, du, cos, sin, name):
    def fn(dq, dk, dvb, dub, cs, sn):
        cs4 = jnp.concatenate([cs] * 4, axis=1)
        sn4 = jnp.concatenate([sn] * 4, axis=1)
        return (jnp.concatenate([dq * cs4, dk * cs, dvb, dub, dq * sn4, dk * sn], axis=1),)
    return rowwise(lay, name, fn, [dqr, dkr, dv, du, cos, sin], outs=[(PEXT, BF16)])[0]


def _attn_specs(lay):
    nbs, cbk, lbk = lay.PS // BLK, lay.C // BLK, lay.L // BLK

    def kv_map(j):
        return lambda s, n: (s * nbs + cbk + jnp.clip(n - cbk + j - 1, 0, lbk - 1), 0)

    win = [pl.BlockSpec((BLK, KV_W), kv_map(j)) for j in range(3)]
    ctx = pl.BlockSpec((lay.C, KV_W), lambda s, n: (s * (lay.PS // lay.C), 0))
    return nbs, cbk, lbk, win, ctx


def _attn_masks(n, cbk, lbk):
    row = lax.broadcasted_iota(jnp.int32, (BLK, BLK), 0)
    col = lax.broadcasted_iota(jnp.int32, (BLK, BLK), 1)
    m = n - cbk
    lat = n >= cbk
    valid = [lat & (m >= 1) & (col >= row), lat & (col >= 0), lat & (m <= lbk - 2) & (col <= row)]
    lane_lo = lax.broadcasted_iota(jnp.int32, (BLK, 2 * HEAD_DIM), 1) < HEAD_DIM
    return valid, lane_lo


def attn_fwd(lay, qr, kr, vb, sink_tab, name):
    nbs, cbk, lbk, win, ctx = _attn_specs(lay)

    def body(q_ref, k0, k1, k2, kc_ref, v0, v1, v2, vc_ref, sk_ref, o_ref, l_ref):
        n = pl.program_id(1)
        valid, lane_lo = _attn_masks(n, cbk, lbk)
        valid4 = [jnp.concatenate([v] * 4, axis=0) for v in valid]
        ks = [k0[...], k1[...], k2[...]]
        vs = [v0[...], v1[...], v2[...]]
        kc, vc = kc_ref[...], vc_ref[...]
        q2s = [q_ref[:, p * 128:(p + 1) * 128] for p in range(4)]
        outs, lses = [], []
        for hh in range(2):
            sel = lane_lo == (hh == 0)
            qm = jnp.concatenate([jnp.where(sel, q2, jnp.zeros_like(q2)) for q2 in q2s], axis=0)
            sk = jnp.concatenate([jnp.broadcast_to(sk_ref[p:p + 1, hh * HEAD_DIM:hh * HEAD_DIM + 1], (BLK, 1))
                                  for p in range(4)], axis=0)
            sw = [jnp.where(valid4[j], _nt(qm, ks[j]) * ATT_SCALE, NEG_INF) for j in range(3)]
            sc = _nt(qm, kc) * ATT_SCALE
            mx = jnp.maximum(jnp.maximum(jnp.maximum(sw[0].max(-1, keepdims=True), sw[1].max(-1, keepdims=True)),
                                         jnp.maximum(sw[2].max(-1, keepdims=True), sc.max(-1, keepdims=True))), sk)
            ew = [jnp.exp(s - mx) for s in sw]
            ec = jnp.exp(sc - mx)
            den = ew[0].sum(-1, keepdims=True) + ew[1].sum(-1, keepdims=True) + ew[2].sum(-1, keepdims=True)
            den = den + ec.sum(-1, keepdims=True) + jnp.exp(sk - mx)
            o = _nn((ec / den).astype(BF16), vc)
            for j in range(3):
                o += _nn((ew[j] / den).astype(BF16), vs[j])
            outs.append(o)
            lses.append(mx + jnp.log(den))
        for p in range(4):
            rows = slice(p * BLK, (p + 1) * BLK)
            o_ref[:, p * 128:(p + 1) * 128] = jnp.where(lane_lo, outs[0][rows], outs[1][rows]).astype(o_ref.dtype)
            l_ref[:, p * 128:(p + 1) * 128] = jnp.where(lane_lo, jnp.broadcast_to(lses[0][rows], (BLK, 128)),
                                                        jnp.broadcast_to(lses[1][rows], (BLK, 128)))

    qspec = pl.BlockSpec((BLK, ATT_W), lambda s, n: (s * nbs + n, 0))
    return pl.pallas_call(
        body, name=name, out_shape=[_sds((lay.T, ATT_W + POOL_W), BF16), _sds((lay.T, ATT_W), F32)], grid=(2, nbs),
        in_specs=[qspec] + win + [ctx] + win + [ctx] + [pl.BlockSpec((8, 128), lambda s, n: (0, 0))],
        out_specs=[qspec, qspec], compiler_params=_cp(("parallel", "parallel")))(qr, kr, kr, kr, kr, vb, vb, vb, vb, sink_tab)


def attn_bwd(lay, qr, kr, vb, sink_tab, lse, datt, name, rider=None):
    nbs, cbk, lbk, win, ctx = _attn_specs(lay)
    C, PS = lay.C, lay.PS

    def body(q_ref, k0, k1, k2, kc_ref, v0, v1, v2, vc_ref, sk_ref, l_ref, do_ref, dq_ref, dk_ref, dv_ref, ds_ref):
        n = pl.program_id(1)
        valid, lane_lo = _attn_masks(n, cbk, lbk)

        @pl.when(n == 0)
        def _():
            dk_ref[...] = jnp.zeros_like(dk_ref)
            dv_ref[...] = jnp.zeros_like(dv_ref)
            ds_ref[...] = jnp.zeros_like(ds_ref)

        ks = [k0[...], k1[...], k2[...], kc_ref[...]]
        vs = [v0[...], v1[...], v2[...], vc_ref[...]]
        valid4 = [jnp.concatenate([v] * 4, axis=0) for v in valid]
        dks = [jnp.zeros((BLK, KV_W), F32)] * 3 + [jnp.zeros((C, KV_W), F32)]
        dvs = list(dks)
        q2s = [q_ref[:, p * 128:(p + 1) * 128] for p in range(4)]
        do2s = [do_ref[:, p * 128:(p + 1) * 128].astype(BF16) for p in range(4)]
        lse2s = [l_ref[:, p * 128:(p + 1) * 128] for p in range(4)]
        dq_h, dd_h = [], []
        for hh in range(2):
            sel = lane_lo == (hh == 0)
            qm = jnp.concatenate([jnp.where(sel, q2, jnp.zeros_like(q2)) for q2 in q2s], axis=0)
            dom = jnp.concatenate([jnp.where(sel, d2, jnp.zeros_like(d2)) for d2 in do2s], axis=0)
            lse_h = jnp.concatenate([l2[:, hh * HEAD_DIM:hh * HEAD_DIM + 1] for l2 in lse2s], axis=0)
            ps, dps = [], []
            for j in range(4):
                s = _nt(qm, ks[j]) * ATT_SCALE
                if j < 3:
                    s = jnp.where(valid4[j], s, NEG_INF)
                ps.append(jnp.exp(s - lse_h))
                dps.append(_nt(dom, vs[j]))
            dd = (ps[0] * dps[0]).sum(-1, keepdims=True) + (ps[1] * dps[1]).sum(-1, keepdims=True)
            dd = dd + (ps[2] * dps[2]).sum(-1, keepdims=True) + (ps[3] * dps[3]).sum(-1, keepdims=True)
            dq = jnp.zeros((4 * BLK, 128), F32)
            for j in range(4):
                dsb = (ps[j] * (dps[j] - dd) * ATT_SCALE).astype(BF16)
                dq += _nn(dsb, ks[j])
                dks[j] = dks[j] + _tn(dsb, qm)
                dvs[j] = dvs[j] + _tn(ps[j].astype(BF16), dom)
            dq_h.append(dq)
            dd_h.append(dd)
        for p in range(4):
            sl = slice(p * 128, (p + 1) * 128)
            rows = slice(p * BLK, (p + 1) * BLK)
            dq_ref[:, sl] = jnp.where(lane_lo, dq_h[0][rows], dq_h[1][rows])
            dd2 = jnp.where(lane_lo, jnp.broadcast_to(dd_h[0][rows], (BLK, 128)), jnp.broadcast_to(dd_h[1][rows], (BLK, 128)))
            psink = jnp.exp(sk_ref[p:p + 1, :] - lse2s[p])
            ds_ref[0, p:p + 1, :] += -jnp.sum(psink * dd2, axis=0, keepdims=True)
        dk_ref[0:C, :] += dks[3]
        dv_ref[0:C, :] += dvs[3]
        for j in range(3):
            r0 = pl.multiple_of((cbk + jnp.clip(n - cbk + j - 1, 0, lbk - 1)) * BLK, BLK)
            dk_ref[pl.ds(r0, BLK), :] += dks[j]
            dv_ref[pl.ds(r0, BLK), :] += dvs[j]

    qspec = pl.BlockSpec((BLK, ATT_W), lambda s, n: (s * nbs + n, 0))
    kvout = pl.BlockSpec((PS, KV_W), lambda s, n: (s, 0))
    return _host_call(
        body, rider, name, (2, nbs),
        [qspec] + win + [ctx] + win + [ctx] + [pl.BlockSpec((8, 128), lambda s, n: (0, 0)), qspec, qspec],
        [qspec, kvout, kvout, pl.BlockSpec((1, 8, 128), lambda s, n: (s, 0, 0))],
        [_sds((lay.T, ATT_W), F32), _sds((lay.T, KV_W), F32), _sds((lay.T, KV_W), F32), _sds((2, 8, 128), F32)],
        (qr, kr, kr, kr, kr, vb, vb, vb, vb, sink_tab, lse, datt), ("parallel", "arbitrary"), 12, 4)


def _winsum(x, r):
    n = x.shape[0]
    t = lax.broadcasted_iota(jnp.int32, x.shape, 0)
    acc = x
    for o in range(1, r + 1):
        acc = acc + jnp.where(t >= o, pltpu.roll(x, o, 0), 0.0) + jnp.where(t < n - o, pltpu.roll(x, n - o, 0), 0.0)
    return acc


def _wincount(n, r):
    t = lax.broadcasted_iota(jnp.int32, (n, 128), 0)
    return (jnp.minimum(t + r, n - 1) - jnp.maximum(t - r, 0) + 1).astype(F32)


def pool_fwd(lay, u, w_pool, scale, cat, name):
    segs = [(0, lay.C), (lay.C, lay.L)]

    def body(u_ref, w_ref, s_ref, c_in, o_ref):
        del c_in
        for r0, n in segs:
            for g, wd in enumerate(POOL_WINDOWS):
                sl = slice(g * 128, (g + 1) * 128)
                x = u_ref[r0:r0 + n, sl]
                d = _winsum(x, wd // 2) / _wincount(n, wd // 2) - x
                y = _nn(d.astype(BF16), w_ref[g].astype(BF16)) * s_ref[:, sl]
                o_ref[r0:r0 + n, sl] = y.astype(o_ref.dtype)

    spec = pl.BlockSpec((lay.PS, POOL_W), lambda s: (s, 0))
    return pl.pallas_call(
        body, name=name, out_shape=_sds(cat.shape, BF16), grid=(2,),
        in_specs=[spec, pl.BlockSpec(w_pool.shape, lambda s: (0, 0, 0)), pl.BlockSpec((1, POOL_W), lambda s: (0, 0)), ANY],
        out_specs=pl.BlockSpec((lay.PS, POOL_W), lambda s: (s, 1)), input_output_aliases={3: 0},
        compiler_params=_cp(("parallel",), VMEM_BIG))(u, w_pool, scale, cat)


def pool_bwd(lay, u, dcat, w_pool, scale, name):
    segs = [(0, lay.C), (lay.C, lay.L)]

    def body(u_ref, dp_ref, w_ref, s_ref, du_ref, dw_ref, dsc_ref):
        for g, wd in enumerate(POOL_WINDOWS):
            sl = slice(g * 128, (g + 1) * 128)
            wb = w_ref[g].astype(BF16)
            dw = jnp.zeros((128, 128), F32)
            dsc = jnp.zeros((1, 128), F32)
            for r0, n in segs:
                x = u_ref[r0:r0 + n, sl]
                cnt = _wincount(n, wd // 2)
                d = (_winsum(x, wd // 2) / cnt - x).astype(BF16)
                dp = dp_ref[r0:r0 + n, sl]
                dsc += jnp.sum(_nn(d, wb) * dp, axis=0, keepdims=True)
                dyp = (dp * s_ref[:, sl]).astype(BF16)
                dw += _tn(d, dyp)
                dd = _nt(dyp, wb)
                du_ref[r0:r0 + n, sl] = _winsum(dd / cnt, wd // 2) - dd
            dw_ref[0, g] = dw
            dsc_ref[0, :, sl] = dsc

    spec = pl.BlockSpec((lay.PS, POOL_W), lambda s: (s, 0))
    return pl.pallas_call(
        body, name=name,
        out_shape=[_sds((lay.T, POOL_W), F32), _sds((2, 4, 128, 128), F32), _sds((2, 1, POOL_W), F32)], grid=(2,),
        in_specs=[spec, pl.BlockSpec((lay.PS, POOL_W), lambda s: (s, 1)), pl.BlockSpec(w_pool.shape, lambda s: (0, 0, 0)),
                  pl.BlockSpec((1, POOL_W), lambda s: (0, 0))],
        out_specs=[spec, pl.BlockSpec((1, 4, 128, 128), lambda s: (s, 0, 0, 0)), pl.BlockSpec((1, 1, POOL_W), lambda s: (s, 0, 0))],
        compiler_params=_cp(("parallel",), VMEM_BIG))(u, dcat, w_pool, scale)


CONV_OFFS = (-1, 0, 1, 2)
CW = 256


def _shift_rows(x, o):
    if o == 0:
        return x
    n = x.shape[0]
    t = lax.broadcasted_iota(jnp.int32, x.shape, 0)
    if o < 0:
        return jnp.where(t >= -o, pltpu.roll(x, -o, 0), 0.0)
    return jnp.where(t < n - o, pltpu.roll(x, n - o, 0), 0.0)


def conv_fwd(lay, p, col0, w, b, name):
    segs = [(0, lay.C), (lay.C, lay.L)]
    cb0 = col0 // CW

    def body(x_ref, w_ref, b_ref, o_ref):
        for r0, n in segs:
            x = x_ref[r0:r0 + n, :]
            y = jnp.broadcast_to(b_ref[...], x.shape)
            for k, o in enumerate(CONV_OFFS):
                y = y + _shift_rows(x, o) * w_ref[k:k + 1, :]
            o_ref[r0:r0 + n, :] = y

    return pl.pallas_call(
        body, name=name, out_shape=_sds((lay.T, D), F32), grid=(2, D // CW),
        in_specs=[pl.BlockSpec((lay.PS, CW), lambda s, j: (s, cb0 + j)), pl.BlockSpec((4, CW), lambda s, j: (0, j)),
                  pl.BlockSpec((1, CW), lambda s, j: (0, j))],
        out_specs=pl.BlockSpec((lay.PS, CW), lambda s, j: (s, j)),
        compiler_params=_cp(("parallel", "parallel")))(p, w, b)


def conv_bwd(lay, p, col0, w, duc, dp, name):
    segs = [(0, lay.C), (lay.C, lay.L)]
    cb0 = col0 // CW

    def body(x_ref, w_ref, g_ref, dp_in, du_ref, dw_ref, db_ref):
        del dp_in
        dws =[jnp.zeros((1, CW), F32)] * 4
        db = jnp.zeros((1, CW), F32)
        for r0, n in segs:
            x = x_ref[r0:r0 + n, :]
            g = g_ref[r0:r0 + n, :]
            du = jnp.zeros_like(g)
            for k, o in enumerate(CONV_OFFS):
                du = du + _shift_rows(g, -o) * w_ref[k:k + 1, :]
                dws[k] = dws[k] + jnp.sum(g * _shift_rows(x, o), axis=0, keepdims=True)
            db = db + jnp.sum(g, axis=0, keepdims=True)
            du_ref[r0:r0 + n, :] = du.astype(du_ref.dtype)
        dw_ref[0] = jnp.concatenate(dws, axis=0)
        db_ref[0] = db

    return pl.pallas_call(
        body, name=name, out_shape=[_sds(dp.shape, BF16), _sds((2, 4, D), F32), _sds((2, 1, D), F32)], grid=(2, D // CW),
        in_specs=[pl.BlockSpec((lay.PS, CW), lambda s, j: (s, cb0 + j)), pl.BlockSpec((4, CW), lambda s, j: (0, j)),
                  pl.BlockSpec((lay.PS, CW), lambda s, j: (s, j)), ANY],
        out_specs=[pl.BlockSpec((lay.PS, CW), lambda s, j: (s, D // CW + j)), pl.BlockSpec((1, 4, CW), lambda s, j: (s, 0, j)),
                   pl.BlockSpec((1, 1, CW), lambda s, j: (s, 0, j))],
        input_output_aliases={3: 0}, compiler_params=_cp(("parallel", "parallel")))(p, w, duc, dp)


def _softplus_neg(lam):
    z = -lam
    w = jnp.exp(-jnp.abs(z))
    log1p = jnp.where(w < 1e-2, w * (1.0 - w * (0.5 - w / 3.0)), jnp.log(1.0 + w))
    return jnp.maximum(z, 0.0) + log1p, -_sigmoid(z)


def _neg_expm1(x):
    series = -x * (1.0 + x * (0.5 + x * (1.0 / 6.0 + x * (1.0 / 24.0 + x * (1.0 / 120.0)))))
    return jnp.where(x > -0.05, series, 1.0 - jnp.exp(x))


def _lru_gates(x, xb, wa, wx, ba, bx, lam):
    r = _sigmoid(_nn(xb, wa.astype(BF16)) + ba)
    gi = _sigmoid(_nn(xb, wx.astype(BF16)) + bx)
    sp, dsp = _softplus_neg(lam)
    la = -LRU_C * r * sp
    a = jnp.exp(la)
    sq = jnp.sqrt(_neg_expm1(2.0 * la))
    return r, gi, sp, dsp, a, sq


def lru_coeffs(lay, uc, wa, wx, vec, name):
    tr = lay.tc

    def body(x_ref, wa_ref, wx_ref, v_ref, a_ref, b_ref):
        for h in range(8):
            sl = slice(h * 128, (h + 1) * 128)
            x = x_ref[:, sl]
            xb = x.astype(BF16)
            for d in range(2):
                _, gi, _, _, a, sq = _lru_gates(x, xb, wa_ref[d, h], wx_ref[d, h], v_ref[d:d + 1, sl],
                                                v_ref[2 + d:3 + d, sl], v_ref[4 + d:5 + d, sl])
                a_ref[d, h] = a
                b_ref[d, h] = sq * (gi * x)

    wspec = pl.BlockSpec((2, 8, 128, 128), lambda i: (0, 0, 0, 0))
    ospec = pl.BlockSpec((2, 8, tr, 128), lambda i: (0, 0, i, 0))
    return pl.pallas_call(
        body, name=name, out_shape=[_sds((2, 8, lay.T, 128), F32)] * 2, grid=(lay.T // tr,),
        in_specs=[pl.BlockSpec((tr, D), lambda i: (i, 0)), wspec, wspec, pl.BlockSpec((6, D), lambda i: (0, 0))],
        out_specs=[ospec, ospec], compiler_params=_cp(("parallel",), VMEM_BIG))(uc, wa, wx, vec)


def lru_coeffs_bwd(lay, uc, wa, wx, vec, da, db, name, rider=None):
    tr = lay.tc

    def body(x_ref, wa_ref, wx_ref, v_ref, da_ref, db_ref, dx_ref, dwa_ref, dwx_ref, dv_ref):
        @pl.when(pl.program_id(0) == 0)
        def _():
            dwa_ref[...] = jnp.zeros_like(dwa_ref)
            dwx_ref[...] = jnp.zeros_like(dwx_ref)
            dv_ref[...] = jnp.zeros_like(dv_ref)

        for h in range(8):
            sl = slice(h * 128, (h + 1) * 128)
            x = x_ref[:, sl]
            xb = x.astype(BF16)
            dx = jnp.zeros_like(x)
            for d in range(2):
                wab, wxb = wa_ref[d, h].astype(BF16), wx_ref[d, h].astype(BF16)
                r, gi, sp, dsp, a, sq = _lru_gates(x, xb, wa_ref[d, h], wx_ref[d, h], v_ref[d:d + 1, sl],
                                                   v_ref[2 + d:3 + d, sl], v_ref[4 + d:5 + d, sl])
                dbv, dav = db_ref[d, h], da_ref[d, h]
                t1 = dbv * sq
                dgi = t1 * x
                dx = dx + t1 * gi
                dla = dav * a - (dbv * gi * x) * (a * a) / sq
                dr = dla * (-LRU_C * sp)
                dlam = jnp.sum(dla * (-LRU_C * r), axis=0, keepdims=True) * dsp
                dpa = dr * r * (1.0 - r)
                dpx = dgi * gi * (1.0 - gi)
                dpab, dpxb = dpa.astype(BF16), dpx.astype(BF16)
                dwa_ref[d, h] += _tn(xb, dpab)
                dwx_ref[d, h] += _tn(xb, dpxb)
                dx = dx + _nt(dpab, wab) + _nt(dpxb, wxb)
                dv_ref[d:d + 1, sl] += jnp.sum(dpa, axis=0, keepdims=True)
                dv_ref[2 + d:3 + d, sl] += jnp.sum(dpx, axis=0, keepdims=True)
                dv_ref[4 + d:5 + d, sl] += dlam
            dx_ref[:, sl] = dx

    wspec = pl.BlockSpec((2, 8, 128, 128), lambda i: (0, 0, 0, 0))
    gspec = pl.BlockSpec((2, 8, tr, 128), lambda i: (0, 0, i, 0))
    vspec = pl.BlockSpec((6, D), lambda i: (0, 0))
    xspec = pl.BlockSpec((tr, D), lambda i: (i, 0))
    return _host_call(
        body, rider, name, (lay.T // tr,), [xspec, wspec, wspec, vspec, gspec, gspec], [xspec, wspec, wspec, vspec],
        [_sds((lay.T, D), F32), _sds((2, 8, 128, 128), F32), _sds((2, 8, 128, 128), F32), _sds((6, D), F32)],
        (uc, wa, wx, vec, da, db), ("arbitrary",), 6, 4)


GB = 2
SCAN_UNROLL = 8


def _tile_scan(a, b, up):
    t = lax.broadcasted_iota(jnp.int32, a.shape, 0)
    for d in (1, 2, 4):
        sh = 8 - d if up else d
        m = (t < 8 - d) if up else (t >= d)
        a_prev, b_prev = pltpu.roll(a, sh, 0), pltpu.roll(b, sh, 0)
        b = jnp.where(m, a * b_prev + b, b)
        a = jnp.where(m, a * a_prev, a)
    return a, b


def lru_scan(lay, a, b, name):
    segs = [(0, lay.C), (lay.C, lay.L)]

    def body(a_ref, b_ref, s_ref):
        for d in range(2):
            rev = d == 1
            state = tuple(jnp.zeros((1, 128), F32) for _ in range(GB))
            for base, n in segs:
                nt = n // 8

                def step(j, c, base=base, nt=nt, rev=rev, d=d):
                    c = list(c)
                    for u in range(SCAN_UNROLL):
                        jj = j * SCAN_UNROLL + u
                        r0 = pl.multiple_of(base + 8 * ((nt - 1 - jj) if rev else jj), 8)
                        for g in range(GB):
                            at, bt = _tile_scan(a_ref[d, g, pl.ds(r0, 8), :], b_ref[d, g, pl.ds(r0, 8), :], rev)
                            h = at * c[g] + bt
                            s_ref[d, g, pl.ds(r0, 8), :] = h
                            c[g] = h[0:1] if rev else h[7:8]
                    return tuple(c)

                state = lax.fori_loop(0, nt // SCAN_UNROLL, step, state)

    spec = pl.BlockSpec((2, GB, lay.PS, 128), lambda s, hb: (0, hb, s, 0))
    return pl.pallas_call(
        body, name=name, out_shape=_sds((2, 8, lay.T, 128), F32), grid=(2, 8 // GB),
        in_specs=[spec, spec], out_specs=spec, compiler_params=_cp(("parallel", "parallel"), VMEM_BIG))(a, b)


def lru_scan_bwd(lay, a, s, dy, name):
    segs = [(0, lay.C), (lay.C, lay.L)]
    C, PS = lay.C, lay.PS

    def body(a_ref, s_ref, g_ref, da_ref, db_ref):
        t = lax.broadcasted_iota(jnp.int32, (8, 128), 0)
        for d in range(2):
            rev = d == 1
            carry = tuple(jnp.zeros((1, 128), F32) for _ in range(GB))
            for si in (1, 0):
                base, n = segs[si]
                nt = n // 8

                def step(j, c, base=base, nt=nt, rev=rev, d=d):
                    c = list(c)
                    for u in range(SCAN_UNROLL):
                        jj = j * SCAN_UNROLL + u
                        r0 = pl.multiple_of(base + 8 * (jj if rev else (nt - 1 - jj)), 8)
                        if rev:
                            rn = pl.multiple_of(jnp.where(r0 == PS - 8, 0, r0 + 8), 8)
                            nb_zero = r0 == C - 8
                        else:
                            rn = pl.multiple_of(jnp.maximum(r0 - 8, 0), 8)
                            nb_zero = r0 == 0
                        for g in range(GB):
                            av = a_ref[d, g, pl.ds(r0, 8), :]
                            gv = g_ref[g, pl.ds(r0, 8), :]
                            sv = s_ref[d, g, pl.ds(r0, 8), :]
                            nbt = s_ref[d, g, pl.ds(rn, 8), :]
                            at, bt = _tile_scan(av, av * gv, not rev)
                            m = at * c[g] + bt
                            if rev:
                                m_next = jnp.where(t >= 1, pltpu.roll(m, 1, 0), c[g])
                                nb = jnp.where(nb_zero, 0.0, nbt[0:1])
                                h_prev = jnp.where(t < 7, pltpu.roll(sv, 7, 0), nb)
                                c[g] = m[7:8]
                            else:
                                m_next = jnp.where(t < 7, pltpu.roll(m, 7, 0), c[g])
                                nb = jnp.where(nb_zero, 0.0, nbt[7:8])
                                h_prev = jnp.where(t >= 1, pltpu.roll(sv, 1, 0), nb)
                                c[g] = m[0:1]
                            lam = gv + m_next
                            db_ref[d, g, pl.ds(r0, 8), :] = lam
                            da_ref[d, g, pl.ds(r0, 8), :] = lam * h_prev
                    return tuple(c)

                carry = lax.fori_loop(0, nt // SCAN_UNROLL, step, carry)

    spec = pl.BlockSpec((2, GB, lay.PS, 128), lambda s, hb: (0, hb, s, 0))
    return pl.pallas_call(
        body, name=name, out_shape=[_sds((2, 8, lay.T, 128), F32)] * 2, grid=(2, 8 // GB),
        in_specs=[spec, spec, pl.BlockSpec((GB, lay.PS, 128), lambda s, hb: (hb, s, 0))],
        out_specs=[spec, spec], compiler_params=_cp(("parallel", "parallel"), VMEM_BIG))(a, s, dy)


def _gelu(x):
    k = math.sqrt(2.0 / math.pi)
    t = jnp.tanh(k * (x + 0.044715 * x * x * x))
    return 0.5 * x * (1.0 + t), 0.5 * (1.0 + t) + 0.5 * x * (1.0 - t * t) * k * (1.0 + 3 * 0.044715 * x * x)


def lru_gate(lay, p, s, name):
    tr = lay.tr

    def body(g_ref, s_ref, o_ref):
        for h in range(8):
            sl = slice(h * 128, (h + 1) * 128)
            o_ref[:, sl] = (_gelu(g_ref[:, sl])[0] * (s_ref[0, h] + s_ref[1, h])).astype(o_ref.dtype)

    return pl.pallas_call(
        body, name=name, out_shape=_sds((lay.T, D), BF16), grid=(lay.nblk,),
        in_specs=[pl.BlockSpec((tr, D), lambda i: (i, 0)), pl.BlockSpec((2, 8, tr, 128), lambda i: (0, 0, i, 0))],
        out_specs=pl.BlockSpec((tr, D), lambda i: (i, 0)), compiler_params=_cp(("parallel",)))(p, s)


def lru_gate_bwd(lay, p, s, do, name):
    tr = lay.tr

    def body(g_ref, s_ref, do_ref, dg_ref, dy_ref):
        for h in range(8):
            sl = slice(h * 128, (h + 1) * 128)
            ge, dge = _gelu(g_ref[:, sl])
            dov = do_ref[:, sl]
            dg_ref[:, sl] = (dov * (s_ref[0, h] + s_ref[1, h]) * dge).astype(dg_ref.dtype)
            dy_ref[h] = dov * ge

    xspec = pl.BlockSpec((tr, D), lambda i: (i, 0))
    return pl.pallas_call(
        body, name=name, out_shape=[_sds((lay.T, 2 * D), BF16), _sds((8, lay.T, 128), F32)], grid=(lay.nblk,),
        in_specs=[xspec, pl.BlockSpec((2, 8, tr, 128), lambda i: (0, 0, i, 0)), xspec],
        out_specs=[xspec, pl.BlockSpec((8, tr, 128), lambda i: (0, i, 0))],
        compiler_params=_cp(("parallel",)))(p, s, do)


def silu_rows(x, name):
    def body(x_ref, o_ref):
        v = x_ref[...]
        o_ref[...] = (v * _sigmoid(v)).astype(o_ref.dtype)
    return pl.pallas_call(body, name=name, out_shape=_sds(x.shape, BF16), in_specs=[VMEM_SPEC], out_specs=VMEM_SPEC)(x)


def mod_grad_rows(gath, name):
    w = gath.shape[-1]

    def body(g_ref, dm_ref, db_ref):
        dm_ref[...] = jnp.zeros_like(dm_ref)
        for l in range(2):
            ctx = g_ref[0, 3 * l + 2:3 * l + 3, :]
            tot = g_ref[0, 3 * l:3 * l + 1, :] + g_ref[0, 3 * l + 1:3 * l + 2, :]
            for k in range(8):
                dm_ref[l, 2 * k:2 * k + 2, :] = g_ref[k, 3 * l:3 * l + 2, :]
                if k:
                    ctx = ctx + g_ref[k, 3 * l + 2:3 * l + 3, :]
                    tot = tot + (g_ref[k, 3 * l:3 * l + 1, :] + g_ref[k, 3 * l + 1:3 * l + 2, :])
            dm_ref[l, 16:17, :] = ctx
            db_ref[l:l + 1, :] = tot + ctx

    return pl.pallas_call(body, name=name, out_shape=[_sds((2, 32, w), F32), _sds((2, w), F32)],
                          in_specs=[VMEM_SPEC], out_specs=[VMEM_SPEC, VMEM_SPEC])(gath)


def cctx_grad(p, c_ctx, name):
    def body(a_ref, c_ref, o_ref):
        cv = c_ref[...]
        sg = _sigmoid(cv)
        o_ref[...] = 0.5 * (a_ref[0, 0:1, :] + a_ref[1, 0:1, :]) * (sg * (1.0 + cv * (1.0 - sg)))
    return pl.pallas_call(body, name=name, out_shape=_sds((1, D), F32), in_specs=[VMEM_SPEC] * 2,
                          out_specs=VMEM_SPEC)(p, c_ctx)


def loss_lnb(lay, xhat, rstd, y, tgt, mod, k_gate, coef, lnv, name):
    def fn(xh, rs, yb, tb, m, ln):
        lat = (pl.program_id(0) % lay.bps) >= lay.cb
        e = jnp.where(lat, xh * ln[0:1] + ln[1:2] - tb, 0.0)
        dy, dres, s1 = _ln_bwd_math(e * (1.0 / D), xh, rs, yb, m[k_gate:k_gate + 1], coef, ln)
        return dy, dres, s1, jnp.sum(e * e, axis=0, keepdims=True) * (0.5 / D)
    return rowwise(lay, name, fn, [xhat, rstd, y, tgt], segs=[mod], vecs=[lnv],
                   outs=[(D, BF16), (D, F32)], sums=[(3, D), (1, D)])


def adamw(w, g, m, v, name):
    shape = w.shape
    w2, g2, m2, v2 = (t.reshape(-1, shape[-1]) for t in (w, g, m, v))
    rows, width = w2.shape
    tr = 256 if rows % 256 == 0 else rows
    c1 = 1.0 - ADAM_B1 ** ADAM_STEP
    c2 = 1.0 - ADAM_B2 ** ADAM_STEP

    def body(w_ref, g_ref, m_ref, v_ref, d_ref, mo_ref, vo_ref):
        gv = g_ref[...]
        mn = ADAM_B1 * m_ref[...] + (1.0 - ADAM_B1) * gv
        vn = ADAM_B2 * v_ref[...] + (1.0 - ADAM_B2) * (gv * gv)
        d_ref[...] = -ADAM_LR * ((mn / c1) / (jnp.sqrt(vn / c2) + ADAM_EPS) + ADAM_WD * w_ref[...])
        mo_ref[...] = mn
        vo_ref[...] = vn

    spec = pl.BlockSpec((tr, width), lambda i: (i, 0))
    d, mn, vn = pl.pallas_call(body, name=name, out_shape=[_sds((rows, width), F32)] * 3, grid=(rows // tr,),
                               in_specs=[spec] * 4, out_specs=[spec] * 3, compiler_params=_cp(("parallel",)))(w2, g2, m2, v2)
    return d.reshape(shape), mn.reshape(shape), vn.reshape(shape)


def adamw_ffn(w, m, v, red, kind, ns, name):
    shape = w.shape
    w2, m2, v2 = (t.reshape(-1, shape[-1]) for t in (w, m, v))
    rows, width = w2.shape
    c1 = 1.0 - ADAM_B1 ** ADAM_STEP
    c2 = 1.0 - ADAM_B2 ** ADAM_STEP
    tr, nb = ns // 2, 2
    gspec = pl.BlockSpec((tr, D), lambda i: (((i // nb) * 3 + kind) * nb + i % nb, 0))

    def body(w_ref, g_ref, m_ref, v_ref, go_ref, d_ref, mo_ref, vo_ref):
        gv = g_ref[...]
        mn = ADAM_B1 * m_ref[...] + (1.0 - ADAM_B1) * gv
        vn = ADAM_B2 * v_ref[...] + (1.0 - ADAM_B2) * (gv * gv)
        go_ref[...] = gv
        d_ref[...] = -ADAM_LR * ((mn / c1) / (jnp.sqrt(vn / c2) + ADAM_EPS) + ADAM_WD * w_ref[...])
        mo_ref[...] = mn
        vo_ref[...] = vn

    spec = pl.BlockSpec((tr, width), lambda i: (i, 0))
    outs = pl.pallas_call(body, name=name, out_shape=[_sds((rows, width), F32)] * 4, grid=(rows // tr,),
                          in_specs=[spec, gspec, spec, spec], out_specs=[spec] * 4,
                          compiler_params=_cp(("parallel",)))(w2, red, m2, v2)
    return tuple(t.reshape(shape) for t in outs)


def mod_mm(sc, w_mod, bias, name):
    wm = w_mod.shape[-1]
    tn = _pick(wm, (768, 512, 384, 256, 128))

    def body(a_ref, b_ref, c_ref, o_ref):
        o_ref[...] = _nn(a_ref[...], b_ref[...].astype(BF16)) + c_ref[...]

    return pl.pallas_call(
        body, name=name, out_shape=_sds((DEPTH, 32, wm), F32), grid=(DEPTH, wm // tn),
        in_specs=[pl.BlockSpec((32, D), lambda l, j: (0, 0)), pl.BlockSpec((None, D, tn), lambda l, j: (l, 0, j)),
                  pl.BlockSpec((None, 1, tn), lambda l, j: (l, 0, j))],
        out_specs=pl.BlockSpec((None, 32, tn), lambda l, j: (l, 0, j)),
        compiler_params=_cp(("parallel", "parallel")))(sc, w_mod, bias)


def wmod_dw(sc, dcol, name):
    wm = dcol.shape[-1]
    tm = 256

    def body(a_ref, b_ref, o_ref):
        o_ref[...] = _tn(a_ref[...], b_ref[...].astype(BF16))

    return pl.pallas_call(
        body, name=name, out_shape=_sds((DEPTH, D, wm), F32), grid=(DEPTH, D // tm),
        in_specs=[pl.BlockSpec((32, tm), lambda l, i: (0, i)), pl.BlockSpec((None, 32, wm), lambda l, i: (l, 0, 0))],
        out_specs=pl.BlockSpec((None, tm, wm), lambda l, i: (l, i, 0)),
        compiler_params=_cp(("parallel", "parallel")))(sc, dcol)


def cctx_dx(drow, w_mod, name):
    wm = w_mod.shape[-1]

    def body(a_ref, b_ref, o_ref):
        o_ref[...] = _nt(a_ref[...].astype(BF16), b_ref[...].astype(BF16))

    return pl.pallas_call(
        body, name=name, out_shape=_sds((DEPTH, 16, D), F32), grid=(DEPTH,),
        in_specs=[pl.BlockSpec((None, 16, wm), lambda l: (l, 0, 0)), pl.BlockSpec((None, D, wm), lambda l: (l, 0, 0))],
        out_specs=pl.BlockSpec((None, 16, D), lambda l: (l, 0, 0)), compiler_params=_cp(("parallel",), VMEM_BIG))(drow, w_mod)


HEAD_PERM = (0, 4, 1, 5, 2, 6, 3, 7)


def _rot_rows(wt):
    return jnp.concatenate([-wt[32:64], wt[0:32]], axis=0)


def _unrot_rows(g):
    return jnp.concatenate([g[32:64], -g[0:32]], axis=0)


def _heads(a, n):
    return [a[64 * i:64 * (i + 1)] for i in range(n)]


def kernel(x, c, ctx, c_ctx, w_mod, b_mod, ln_g, ln_b, ffn_w_gate, ffn_w_up, ffn_w_down, mix_ab_w_in, attn_sink, pool_w, pool_scale, mix_ab_w_out, lru_w_in, lru_conv_w, lru_conv_b, lru_wa, lru_ba, lru_wx, lru_bx, lru_lambda, lru_w_out, loss_target, m_c_ctx, m_w_mod, m_b_mod, m_ln_g, m_ln_b, m_ffn_w_gate, m_ffn_w_up, m_ffn_w_down, m_mix_ab_w_in, m_attn_sink, m_pool_w, m_pool_scale, m_mix_ab_w_out, m_lru_w_in, m_lru_conv_w, m_lru_conv_b, m_lru_wa, m_lru_ba, m_lru_wx, m_lru_bx, m_lru_lambda, m_lru_w_out, v_c_ctx, v_w_mod, v_b_mod, v_ln_g, v_ln_b, v_ffn_w_gate, v_ffn_w_up, v_ffn_w_down, v_mix_ab_w_in, v_attn_sink, v_pool_w, v_pool_scale, v_mix_ab_w_out, v_lru_w_in, v_lru_conv_w, v_lru_conv_b, v_lru_wa, v_lru_ba, v_lru_wx, v_lru_bx, v_lru_lambda, v_lru_w_out):
    n_lat, n_ctx = x.shape[1], ctx.shape[1]
    lay = Layout(n_ctx, n_lat)
    T = lay.T
    ns = ffn_w_gate.shape[-1]
    n_li, n_ai = lru_w_in.shape[-1], mix_ab_w_in.shape[-1]
    n_ao, n_lo = mix_ab_w_out.shape[1], lru_w_out.shape[1]
    wm = w_mod.shape[-1]
    dsh = ln_g.shape[-1]
    mx, my, mc = lax.axis_index("x"), lax.axis_index("y"), lax.axis_index("c")
    chip = 2 * mx + my
    me = 2 * chip + mc

    c_all = all_gather8(c, "ag8_c").reshape(16, D)
    cc = jnp.concatenate([c_all, c_ctx[None, :], jnp.zeros((15, D), F32)], axis=0)
    sc = silu_rows(cc, "silu_c")
    bias = lax.dynamic_slice(b_mod, (0, chip * wm), (DEPTH, wm)).reshape(DEPTH, 1, wm)
    modg = all_gather_chips(mod_mm(sc, w_mod, bias, "mod_mm"), "ag_mod")
    modtab = []
    for l in range(DEPTH):
        full = jnp.transpose(modg[:, l], (1, 0, 2)).reshape(32, N_CHIP * wm)
        mine = lax.dynamic_slice(full, (2 * me, 0), (2, N_CHIP * wm))
        modtab.append(jnp.concatenate([mine, full[16:17]], axis=0).reshape(3, N_MOD, D))

    small = jnp.concatenate([ln_g.reshape(6, dsh), ln_b.reshape(6, dsh), lru_conv_w[0], lru_conv_b, lru_ba[0],
                             lru_bx[0], lru_lambda[0], jnp.zeros((9, dsh), F32)], axis=0)
    small = all_gather_chips(small.reshape(2, 16, dsh), "ag_small").reshape(N_CHIP, 32, dsh)
    small = jnp.transpose(small, (1, 0, 2)).reshape(32, D)
    ln_g_f, ln_b_f = small[0:6].reshape(2, 3, D), small[6:12].reshape(2, 3, D)
    conv_w_f, conv_b_f = small[12:16], small[16:17]
    lru_vec = small[17:23]

    hh = 3 * ns // 2
    gate_t, up_t = jnp.swapaxes(ffn_w_gate, -1, -2), jnp.swapaxes(ffn_w_up, -1, -2)
    extra = [0, n_ai + n_ao, n_li + n_lo, 0]
    placed = [ffn_place(gate_t, up_t, ffn_w_down, g // 2, g % 2, f"ag_ffn{g}_place", extra[g]) for g in range(4)]
    placed[1] = place_rows(placed[1], jnp.concatenate([mix_ab_w_in[0].T, mix_ab_w_out[0]], axis=0).astype(BF16), 3 * ns,
                           "ag_mixa_place")
    placed[2] = place_rows(placed[2], jnp.concatenate([lru_w_in[0].T, lru_w_out[0]], axis=0).astype(BF16), 3 * ns,
                           "ag_mixc_place")
    placed = [p.reshape(N_CHIP, 2, p.shape[1] // 2, D) for p in placed]
    wb = [gather_placed(placed[0], "ag_ffn0"), None, None, None]
    mixw = {}

    def mixa_w():
        if "a" not in mixw:
            full = wb[1].reshape(N_CHIP, -1, D)
            ab_in_t = full[:, 3 * ns:3 * ns + n_ai].reshape(N_CHIP * n_ai, D)
            ab_out = full[:, 3 * ns + n_ai:].reshape(N_CHIP * n_ao, D)
            qh, kh = _heads(ab_in_t[Q0:K0], N_HEADS), _heads(ab_in_t[K0:V0], N_KV)
            w_ext_t = jnp.concatenate([qh[h] for h in HEAD_PERM] + [ab_in_t[K0:QR0]]
                                      + [_rot_rows(qh[h]) for h in HEAD_PERM] + [_rot_rows(t) for t in kh], axis=0)
            oh = _heads(ab_out[0:ATT_W], N_HEADS)
            mixw["a"] = (w_ext_t, jnp.concatenate([oh[h] for h in HEAD_PERM] + [ab_out[ATT_W:]], axis=0))
        return mixw["a"]

    def mixc_w():
        if "c" not in mixw:
            full = wb[2].reshape(N_CHIP, -1, D)
            mixw["c"] = (full[:, 3 * ns:3 * ns + n_li].reshape(N_CHIP * n_li, D),
                         full[:, 3 * ns + n_li:].reshape(N_CHIP * n_lo, D))
        return mixw["c"]

    t = jnp.arange(n_lat)
    inv = ROPE_THETA ** (-jnp.arange(16, dtype=F32) / 16.0)
    ang = jnp.concatenate([(t // GRID_W).astype(F32)[:, None] * inv, (t % GRID_W).astype(F32)[:, None] * inv], axis=-1)
    cos1 = jnp.concatenate([jnp.ones((n_ctx, 32), F32), jnp.cos(ang)], axis=0)
    sin1 = jnp.concatenate([jnp.zeros((n_ctx, 32), F32), jnp.sin(ang)], axis=0)
    cos_t = jnp.tile(cos1, (2, 4))
    sin_t = jnp.tile(sin1, (2, 4))
    sk = attn_sink[0]
    sink_tab = jnp.concatenate([jnp.repeat(jnp.stack([sk[:4], sk[4:]], axis=1), HEAD_DIM, axis=1),
                                jnp.zeros((4, 128), F32)], axis=0)
    pscale = pool_scale.reshape(1, POOL_W)

    h0 = jnp.concatenate([ctx, x], axis=1).reshape(T, D)
    tgt = loss_target.reshape(2 * n_lat, D)

    def lnv(l, j):
        return jnp.stack([ln_g_f[l, j], ln_b_f[l, j]])

    subs = [(0, 0, 0.5, 0), (0, 3, 1.0, 1), (0, 6, 0.5, 2), (1, 0, 0.5, 0), (1, 3, 1.0, 1), (1, 6, 0.5, 2)]

    def ffn_core(hm, l, f):
        tag = f"l{l}f{f}"
        gi = 2 * l + f
        w = wb[gi].reshape(N_CHIP, -1, D)
        if gi == 3:
            up, sl, a = ffn_up(lay, hm, w, 0, 1, ns, f"ffn_up_{tag}")
            (y,) = slab_nn_acc(lay, [a], w, [2], ns, f"ffn_down_{tag}")
            return y, dict(up=up, sl=sl, a=a, nbuf=None)
        up, sl, a, nbuf = ffn_up(lay, hm, w, 0, 1, ns, f"ffn_up_{tag}", rider=rider_gather_xy(placed[gi + 1]))
        y, nbuf = slab_nn_acc(lay, [a], w, [2], ns, f"ffn_down_{tag}", rider=rider_gather_fwd(nbuf))
        return y, dict(up=up, sl=sl, a=a, nbuf=nbuf)

    def mixa_core(hm):
        p = mm_nt(hm, mixa_w()[0], "mixa_in")
        qr, kr, vb, u = rope_fwd(lay, p, cos_t, sin_t, "rope")
        cat, lse = attn_fwd(lay, qr, kr, vb, sink_tab, "attn")
        cat = pool_fwd(lay, u, pool_w[0], pscale, cat, "pool")
        return mm_nn(cat, mixa_w()[1], "mixa_out"), dict(qr=qr, kr=kr, vb=vb, u=u, lse=lse, cat=cat)

    def mixc_core(hm):
        p = mm_nt(hm, mixc_w()[0], "mixc_in")
        uc = conv_fwd(lay, p, D, conv_w_f, conv_b_f, "conv")
        a, b = lru_coeffs(lay, uc, lru_wa[0], lru_wx[0], lru_vec, "lru_coef")
        s = lru_scan(lay, a, b, "lru_scan")
        o = lru_gate(lay, p, s, "lru_gate")
        return mm_nn(o, mixc_w()[1], "mixc_out"), dict(p=p, uc=uc, a=a, s=s, o=o)

    recs = []
    h = h0
    hm = modulate(lay, h0, modtab[0], 0, 1, "mod_first")
    for k, (l, k0, coef, j) in enumerate(subs):
        if k0 == 3:
            y, core = mixa_core(hm) if l == 0 else mixc_core(hm)
        else:
            y, core = ffn_core(hm, l, k0 // 6)
        nxt = None if k == 5 else (modtab[subs[k + 1][0]], subs[k + 1][1], subs[k + 1][1] + 1)
        nbuf = core.pop("nbuf", None)
        res = resid_ln(lay, h, y, modtab[l], k0 + 2, coef, lnv(l, j), f"ln_s{k}", nxt=nxt,
                       prev_ln=None if k == 0 else lnv(*subs[k - 1][::3]),
                       rider=None if nbuf is None else rider_gather_d2d(nbuf))
        if nbuf is not None:
            wb[2 * l + k0 // 6 + 1] = res[-1]
        recs.append(dict(h=h, hm=hm, y=y, xhat=res[0], rstd=res[1], **core))
        h, hm = res[0], (None if nxt is None else res[2])


    dln = {}
    dms = {}
    mixg = {}
    ffn_red = [lax.empty((4, 2, hh, D), F32)]
    mix_red = {}
    pending = []

    def rs_sib(p):
        return None if p is None else rider_reduce_sib(p["buf"])

    def rs_add2(p, recv):
        p["q"] = add_own_half(p["buf"], recv, BF16, f"rs_add2_{p['key']}")

    def rs_join(p, arr):
        if isinstance(p["key"], int):
            return rider_join(sum_slots(p["q"], arr, f"rs_add4_{p['key']}", dst=ffn_red[0], g=p["key"]), p["key"])
        return rider_join(sum_slots(p["q"], arr, f"rs_add4_{p['key']}"))

    def rs_done(p, joined):
        if isinstance(p["key"], int):
            ffn_red[0] = joined
        else:
            mix_red[p["key"]] = joined.reshape(-1, D)

    def ffn_core_bwd(dy, r, l, f):
        tag = f"l{l}f{f}"
        gi = 2 * l + f
        w = wb[gi].reshape(N_CHIP, -1, D)
        p = pending.pop() if pending else None
        gb = lax.empty((N_CHIP, 3 * ns, D), F32)
        if p is None:
            dg, du = ffn_bwd_da(lay, dy, w, 2, r["up"], r["sl"], ns, f"ffn_da_{tag}")
            (gb,) = slab_tn(lay, r["a"], dy, gb, 2, ns, f"ffn_dwd_{tag}")
            (gb,) = slab_tn(lay, dg, r["hm"], gb, 0, ns, f"ffn_dwg_{tag}")
            (gb,) = slab_tn(lay, du, r["hm"], gb, 1, ns, f"ffn_dwu_{tag}")
            (dhm,) = slab_nn_acc(lay, [dg, du], w, [0, 1], ns, f"ffn_dh_{tag}")
        elif gi == 0:
            dg, du, recv = ffn_bwd_da(lay, dy, w, 2, r["up"], r["sl"], ns, f"ffn_da_{tag}", rider=rs_sib(p))
            rs_add2(p, recv)
            gb, arr = slab_tn(lay, r["a"], dy, gb, 2, ns, f"ffn_dwd_{tag}", rider=rider_reduce_copies(p["q"]))
            gb, joined = slab_tn(lay, dg, r["hm"], gb, 0, ns, f"ffn_dwg_{tag}", rider=rs_join(p, arr))
            rs_done(p, joined)
            (gb,) = slab_tn(lay, du, r["hm"], gb, 1, ns, f"ffn_dwu_{tag}")
            own = gb.reshape(N_CHIP, 2, hh, D)
            dhm, recv = slab_nn_acc(lay, [dg, du], w, [0, 1], ns, f"ffn_dh_{tag}", rider=rider_reduce_sib(own))
            pending.append(dict(buf=own, key=gi, recv=recv))
            return dhm
        else:
            dg, du, recv = ffn_bwd_da(lay, dy, w, 2, r["up"], r["sl"], ns, f"ffn_da_{tag}", rider=rs_sib(p))
            rs_add2(p, recv)
            gb, arr = slab_tn(lay, r["a"], dy, gb, 2, ns, f"ffn_dwd_{tag}", rider=rider_reduce_copy(p["q"], 0))
            gb, arr = slab_tn(lay, dg, r["hm"], gb, 0, ns, f"ffn_dwg_{tag}", rider=rider_reduce_copy(p["q"], 1, arr))
            gb, arr = slab_tn(lay, du, r["hm"], gb, 1, ns, f"ffn_dwu_{tag}", rider=rider_reduce_copy(p["q"], 2, arr))
            dhm, joined = slab_nn_acc(lay, [dg, du], w, [0, 1], ns, f"ffn_dh_{tag}", rider=rs_join(p, arr))
            rs_done(p, joined)
        pending.append(dict(buf=gb.reshape(N_CHIP, 2, hh, D), key=gi))
        return dhm

    def mixc_core_bwd(dy, r):
        p = pending.pop() if pending else None
        w_in_t, w_out = mixc_w()
        if p is None:
            do_c = mm_nt(dy, w_out, "mixc_out_dx")
        else:
            do_c, recv = mm_nt(dy, w_out, "mixc_out_dx", rider=rs_sib(p))
            rs_add2(p, recv)
        g_out = mm_tn(r["o"], dy, "mixc_out_dw")
        dgate, dyg = lru_gate_bwd(lay, r["p"], r["s"], do_c, "lru_gate_b")
        da_c, db_c = lru_scan_bwd(lay, r["a"], r["s"], dyg, "lru_scan_b")
        res = lru_coeffs_bwd(lay, r["uc"], lru_wa[0], lru_wx[0], lru_vec, da_c, db_c, "lru_coef_b",
                             rider=None if p is None else rider_reduce_copies(p["q"]))
        duc, mixg["wa"], mixg["wx"], mixg["vec"] = res[:4]
        dp_c, mixg["cw"], mixg["cb"] = conv_bwd(lay, r["p"], D, conv_w_f, duc, dgate, "conv_b")
        if p is None:
            g_in_t = mm_tn(dp_c, r["hm"], "mixc_in_dw")
        else:
            g_in_t, joined = mm_tn(dp_c, r["hm"], "mixc_in_dw", rider=rs_join(p, res[4]))
            rs_done(p, joined)
        buf = jnp.concatenate([g_in_t.reshape(N_CHIP, n_li, D), g_out.reshape(N_CHIP, n_lo, D)], axis=1)
        pending.append(dict(buf=buf.reshape(N_CHIP, 2, (n_li + n_lo) // 2, D), key="c"))
        return mm_nn(dp_c, w_in_t, "mixc_in_dx")

    def mixa_core_bwd(dy, r):
        p = pending.pop() if pending else None
        w_ext_t, w_out_ext = mixa_w()
        if p is None:
            dcat = mm_nt(dy, w_out_ext, "mixa_out_dx")
        else:
            dcat, recv = mm_nt(dy, w_out_ext, "mixa_out_dx", rider=rs_sib(p))
            rs_add2(p, recv)
        g_out_ext = mm_tn(r["cat"], dy, "mixa_out_dw")
        res = attn_bwd(lay, r["qr"], r["kr"], r["vb"], sink_tab, r["lse"], dcat, "attn_b",
                       rider=None if p is None else rider_reduce_copies(p["q"]))
        dqr, dkr, dv, mixg["sink"] = res[:4]
        du_a, mixg["pw"], mixg["ps"] = pool_bwd(lay, r["u"], dcat, pool_w[0], pscale, "pool_b")
        dp_a = rope_bwd(lay, dqr, dkr, dv, du_a, cos_t, sin_t, "rope_b")
        if p is None:
            g_ext_t = mm_tn(dp_a, r["hm"], "mixa_in_dw")
        else:
            g_ext_t, joined = mm_tn(dp_a, r["hm"], "mixa_in_dw", rider=rs_join(p, res[4]))
            rs_done(p, joined)
        gq, gqr = _heads(g_ext_t[Q0:K0], N_HEADS), _heads(g_ext_t[QR0:KR0], N_HEADS)
        g_q = [None] * N_HEADS
        for i, h in enumerate(HEAD_PERM):
            g_q[h] = gq[i] + _unrot_rows(gqr[i])
        gk = [a + _unrot_rows(b) for a, b in zip(_heads(g_ext_t[K0:V0], N_KV), _heads(g_ext_t[KR0:PEXT], N_KV))]
        g_ab_in_t = jnp.concatenate(g_q + gk + [g_ext_t[V0:QR0]], axis=0)
        go = _heads(g_out_ext[0:ATT_W], N_HEADS)
        g_o = [None] * N_HEADS
        for i, h in enumerate(HEAD_PERM):
            g_o[h] = go[i]
        g_ab_out = jnp.concatenate(g_o + [g_out_ext[ATT_W:]], axis=0)
        buf = jnp.concatenate([g_ab_in_t.reshape(N_CHIP, n_ai, D), g_ab_out.reshape(N_CHIP, n_ao, D)], axis=1)
        pending.append(dict(buf=buf.reshape(N_CHIP, 2, (n_ai + n_ao) // 2, D), key="a"))
        return mm_nn(dp_a, w_ext_t, "mixa_in_dx")

    l, k0, coef, j = subs[5]
    dy, dres, s1, lparts = loss_lnb(lay, recs[5]["xhat"], recs[5]["rstd"], recs[5]["y"], tgt, modtab[l], k0 + 2, coef,
                                    lnv(l, j), "loss_lnb")
    loss = lax.psum(jnp.sum(lparts), ("x", "y", "c"))
    for k in range(5, -1, -1):
        l, k0, coef, j = subs[k]
        r = recs[k]
        if k0 == 3:
            dhm = mixa_core_bwd(dy, r) if l == 0 else mixc_core_bwd(dy, r)
        else:
            dhm = ffn_core_bwd(dy, r, l, k0 // 6)
        dln[(l, j)] = s1
        if k > 0:
            lp, k0p, coefp, jp = subs[k - 1]
            rp = recs[k - 1]
            dy, dres, s1, s2 = modb_lnb(lay, dres, dhm, modtab[l], k0 + 1, rp["xhat"], rp["rstd"], rp["y"],
                                        modtab[lp], k0p + 2, coefp, lnv(lp, jp), f"modb_lnb_s{k}")
        else:
            gx, s2 = mod_bwd(lay, dres, dhm, r["h"], modtab[l], k0 + 1, "modb_s0")
        dms[(l, k0)] = s2
    sums = block_sums(lay, list(dln.values()) + list(dms.values()), "block_sums")
    dln, dms = dict(zip(dln, sums[:len(dln)])), dict(zip(dms, sums[len(dln):]))
    grad_x = gx.reshape(2, n_lat, D)
    g_wa, g_wx, g_vec, g_cw, g_cb, g_sink, g_pw, g_ps = (mixg[n] for n in ("wa", "wx", "vec", "cw", "cb", "sink", "pw", "ps"))

    rows = []
    for l in range(DEPTH):
        per_k = []
        for k0, j in ((0, 0), (3, 1), (6, 2)):
            per_k += [dms[(l, k0)][:3, 0], dms[(l, k0)][:3, 1], dln[(l, j)][:3, 2]]
        rows.append(jnp.stack(per_k, axis=1).reshape(3, N_MOD * D))
    dmod_loc = jnp.concatenate(rows + [jnp.zeros((2, N_MOD * D), F32)], axis=0)
    dmod_all, g_b_mod = mod_grad_rows(all_gather8(dmod_loc, "ag8_dmod"), "dmod_rows")
    dcol = lax.dynamic_slice(dmod_all, (0, 0, chip * wm), (DEPTH, 32, wm))
    g_w_mod = wmod_dw(sc, dcol, "wmod_dw")
    g_cctx = cctx_grad(cctx_dx(dcol[:, 16:32], w_mod, "cctx_dx"), c_ctx[None, :], "cctx_grad")

    g_ln_g =jnp.stack([jnp.stack([dln[(l, j)][3, 1] for j in range(3)]) for l in range(DEPTH)])
    g_ln_b = jnp.stack([jnp.stack([dln[(l, j)][3, 0] for j in range(3)]) for l in range(DEPTH)])
    sink_row = jnp.sum(g_sink, axis=0)[:4]
    g_sink8 = jnp.concatenate([sink_row[:, 0], sink_row[:, HEAD_DIM]])
    misc = jnp.concatenate([g_sink8, jnp.sum(g_ps, axis=0).reshape(POOL_W), jnp.zeros((D - 8 - POOL_W,), F32)])
    small_g = jnp.concatenate([
        g_ln_g.reshape(6, D), g_ln_b.reshape(6, D), jnp.sum(g_cw, axis=0), jnp.sum(g_cb, axis=0), g_vec,
        misc[None, :], jnp.sum(g_pw, axis=0).reshape(64, D), g_wa.reshape(256, D), g_wx.reshape(256, D), g_cctx,
        jnp.zeros((39, D), F32)], axis=0)
    n_small = small_g.shape[0] // N_CHIP
    last = pending.pop()
    ffn_red = reduce_scatter_chips(last["buf"], f"ffn{last['key']}", wire=BF16, dst=ffn_red[0], g=last["key"],
                                   recv=last.get("recv")).reshape(12 * ns, D)
    small_red = reduce_scatter_chips(small_g.reshape(N_CHIP, 2, n_small // 2, D), "small")
    small_red = all_gather_chips(small_red, "ag_smallg").reshape(N_CHIP * n_small, D)

    ffn_kind = dict(ffn_w_gate=0, ffn_w_up=1, ffn_w_down=2)

    def cols(a):
        return lax.dynamic_slice_in_dim(a, chip * dsh, dsh, axis=a.ndim - 1)

    sr = small_red
    grads = dict(
        c_ctx=sr[600], w_mod=g_w_mod, b_mod=g_b_mod,
        ln_g=cols(sr[0:6]).reshape(2, 3, dsh), ln_b=cols(sr[6:12]).reshape(2, 3, dsh),
        mix_ab_w_in=mix_red["a"][0:n_ai][None], attn_sink=sr[23, 0:8][None], pool_w=sr[24:88].reshape(1, 4, 128, 128),
        pool_scale=sr[23, 8:8 + POOL_W][None], mix_ab_w_out=mix_red["a"][n_ai:][None],
        lru_w_in=mix_red["c"][0:n_li].T[None],
        lru_conv_w=cols(sr[12:16])[None], lru_conv_b=cols(sr[16:17]), lru_wa=sr[88:344].reshape(1, 2, 8, 128, 128),
        lru_ba=cols(sr[17:19])[None], lru_wx=sr[344:600].reshape(1, 2, 8, 128, 128), lru_bx=cols(sr[19:21])[None],
        lru_lambda=cols(sr[21:23])[None], lru_w_out=mix_red["c"][n_li:][None])
    params = dict(c_ctx=(c_ctx, m_c_ctx, v_c_ctx), w_mod=(w_mod, m_w_mod, v_w_mod), b_mod=(b_mod, m_b_mod, v_b_mod),
                  ln_g=(ln_g, m_ln_g, v_ln_g), ln_b=(ln_b, m_ln_b, v_ln_b),
                  ffn_w_gate=(ffn_w_gate, m_ffn_w_gate, v_ffn_w_gate), ffn_w_up=(ffn_w_up, m_ffn_w_up, v_ffn_w_up),
                  ffn_w_down=(ffn_w_down, m_ffn_w_down, v_ffn_w_down),
                  mix_ab_w_in=(mix_ab_w_in, m_mix_ab_w_in, v_mix_ab_w_in), attn_sink=(attn_sink, m_attn_sink, v_attn_sink),
                  pool_w=(pool_w, m_pool_w, v_pool_w), pool_scale=(pool_scale, m_pool_scale, v_pool_scale),
                  mix_ab_w_out=(mix_ab_w_out, m_mix_ab_w_out, v_mix_ab_w_out), lru_w_in=(lru_w_in, m_lru_w_in, v_lru_w_in),
                  lru_conv_w=(lru_conv_w, m_lru_conv_w, v_lru_conv_w), lru_conv_b=(lru_conv_b, m_lru_conv_b, v_lru_conv_b),
                  lru_wa=(lru_wa, m_lru_wa, v_lru_wa), lru_ba=(lru_ba, m_lru_ba, v_lru_ba), lru_wx=(lru_wx, m_lru_wx, v_lru_wx),
                  lru_bx=(lru_bx, m_lru_bx, v_lru_bx), lru_lambda=(lru_lambda, m_lru_lambda, v_lru_lambda),
                  lru_w_out=(lru_w_out, m_lru_w_out, v_lru_w_out))
    gl, dl, ml, vl = [], [], [], []
    transposed = ("ffn_w_gate", "ffn_w_up", "mix_ab_w_in")
    for name, (w, m, v) in params.items():
        if name in transposed:
            w, m, v = (jnp.swapaxes(t, -1, -2) for t in (w, m, v))
        if name in ffn_kind:
            g, d, mn, vn = adamw_ffn(w, m, v, ffn_red, ffn_kind[name], ns, f"adamw_{name}")
        else:
            g = grads[name].reshape(w.shape)
            d, mn, vn = adamw(w, g, m, v, f"adamw_{name}")
        if name in transposed:
            g, d, mn, vn = (jnp.swapaxes(t, -1, -2) for t in (g, d, mn, vn))
        gl.append(g)
        dl.append(d)
        ml.append(mn)
        vl.append(vn)
    return (loss, grad_x, *gl, *dl, *ml, *vl)
```

```python
import functools
import math

import jax
import jax.numpy as jnp
from jax import lax
from jax.experimental import pallas as pl
from jax.experimental.pallas import tpu as pltpu

F32, BF16 = jnp.float32, jnp.bfloat16
MESH = pl.DeviceIdType.MESH
ANY = pl.BlockSpec(memory_space=pl.ANY)
VMEM_SPEC = pl.BlockSpec(memory_space=pltpu.VMEM)

D = 1024
N_CHIP = 4
HEAD_DIM, N_HEADS, N_KV = 64, 8, 2
ATT_W, KV_W, POOL_W = 512, 128, 512
POOL_WINDOWS = (2, 4, 8, 16)
BLK = 128
ATT_SCALE = HEAD_DIM ** -0.5
ROPE_THETA = 10000.0
GRID_W = 64
LRU_C = 8.0
LN_EPS = 1e-5
NEG_INF = -1e30
DEPTH = 2
ALPHA = (2 * DEPTH) ** 0.25
N_MOD = 9
ADAM_LR, ADAM_B1, ADAM_B2, ADAM_EPS, ADAM_WD, ADAM_STEP = 0.001, 0.9, 0.999, 1e-08, 0.01, 10
VMEM_BIG = 48 * 1024 * 1024


def _cp(sem=None, vmem=None):
    kw = {}
    if sem is not None:
        kw["dimension_semantics"] = sem
    if vmem is not None:
        kw["vmem_limit_bytes"] = vmem
    return pltpu.CompilerParams(**kw)


def _sds(shape, dtype):
    return jax.ShapeDtypeStruct(tuple(shape), dtype)


def _pick(n, cands):
    for c in cands:
        if n % c == 0:
            return c
    return n


def _dot(a, b, dims):
    return lax.dot_general(a, b, (dims, ((), ())), preferred_element_type=F32)


def _nn(a, b):
    return _dot(a, b, ((1,), (0,)))


def _nt(a, b):
    return _dot(a, b, ((1,), (1,)))


def _tn(a, b):
    return _dot(a, b, ((0,), (0,)))


def _sigmoid(x):
    return 0.5 * jnp.tanh(0.5 * x) + 0.5


def _me():
    return lax.axis_index("x"), lax.axis_index("y"), lax.axis_index("c")


def _rcopy(src, dst, ssem, rsem, dev):
    return pltpu.make_async_remote_copy(src_ref=src, dst_ref=dst, send_sem=ssem, recv_sem=rsem,
                                        device_id=dev, device_id_type=MESH)


def all_gather8(x, name):
    def body(x_ref, o_ref, ssem, rsem, lsem):
        mx, my, mc = _me()
        me = 4 * mx + 2 * my + mc
        loc = pltpu.make_async_copy(x_ref, o_ref.at[me], lsem)
        loc.start()
        peers = []
        for m in range(1, 8):
            px = 1 - mx if (m >> 2) & 1 else mx
            py = 1 - my if (m >> 1) & 1 else my
            pc = 1 - mc if m & 1 else mc
            peers.append((px, py, pc))
        sends = [_rcopy(x_ref, o_ref.at[me], ssem.at[k], rsem.at[k], p) for k, p in enumerate(peers)]
        for cp in sends:
            cp.start()
        for k, (px, py, pc) in enumerate(peers):
            _rcopy(x_ref, o_ref.at[4 * px + 2 * py + pc], ssem.at[k], rsem.at[k], (px, py, pc)).wait_recv()
        for cp in sends:
            cp.wait_send()
        loc.wait()

    return pl.pallas_call(
        body, name=name, out_shape=_sds((8,) + x.shape, x.dtype),
        in_specs=[VMEM_SPEC], out_specs=VMEM_SPEC,
        scratch_shapes=[pltpu.SemaphoreType.DMA((7,)), pltpu.SemaphoreType.DMA((7,)), pltpu.SemaphoreType.DMA],
    )(x)


_ROW_BLOCKS = (512, 384, 352, 256, 224, 128)


def _idx(v):
    return jnp.reshape(v, (1,)).astype(jnp.int32)


def place_slab(shard, name):
    _, h, w = shard.shape
    th = _pick(h, _ROW_BLOCKS)

    def body(s_ref, x_ref, o_ref):
        del s_ref
        o_ref[...] = x_ref[...]

    return pl.pallas_call(
        body, name=name, out_shape=_sds((N_CHIP,) + shard.shape, shard.dtype),
        grid_spec=pltpu.PrefetchScalarGridSpec(
            num_scalar_prefetch=1, grid=(2, h // th),
            in_specs=[pl.BlockSpec((None, th, w), lambda k, r, s: (k, r, 0))],
            out_specs=pl.BlockSpec((None, None, th, w), lambda k, r, s: (s[0], k, r, 0))),
    )(_idx(2 * lax.axis_index("x") + lax.axis_index("y")), shard)


def place_rows(buf, rows, r0, name):
    e, w = rows.shape
    tb = 64

    def body(s_ref, x_ref, b_ref, o_ref):
        del s_ref, b_ref
        o_ref[...] = x_ref[...]

    return pl.pallas_call(
        body, name=name, out_shape=_sds(buf.shape, buf.dtype),
        grid_spec=pltpu.PrefetchScalarGridSpec(
            num_scalar_prefetch=1, grid=(e // tb,),
            in_specs=[pl.BlockSpec((tb, w), lambda j, s: (j, 0)), ANY],
            out_specs=pl.BlockSpec((None, tb, w), lambda j, s: (s[0], r0 // tb + j, 0))),
        input_output_aliases={2: 0},
    )(_idx(2 * lax.axis_index("x") + lax.axis_index("y")), rows, buf)


def ffn_place(w_gate_t, w_up_t, w_down, l, f, name, extra=0):
    ns = w_down.shape[-2]
    tr, nb = ns // 2, 2

    def body(s_ref, g_ref, u_ref, d_ref, o_ref):
        del s_ref
        k = pl.program_id(0)

        @pl.when(k == 0)
        def _():
            o_ref[...] = g_ref[...].astype(BF16)

        @pl.when(k == 1)
        def _():
            o_ref[...] = u_ref[...].astype(BF16)

        @pl.when(k == 2)
        def _():
            o_ref[...] = d_ref[...].astype(BF16)

    def spec(q):
        return pl.BlockSpec((None, None, tr, D), lambda k, j, s: (l, f, jnp.where(k == q, j, 0), 0))

    return pl.pallas_call(
        body, name=name, out_shape=_sds((N_CHIP, 3 * ns + extra, D), BF16),
        grid_spec=pltpu.PrefetchScalarGridSpec(
            num_scalar_prefetch=1, grid=(3, nb), in_specs=[spec(0), spec(1), spec(2)],
            out_specs=pl.BlockSpec((None, tr, D), lambda k, j, s: (s[0], k * nb + j, 0))),
    )(_idx(2 * lax.axis_index("x") + lax.axis_index("y")), w_gate_t, w_up_t, w_down)


def all_gather_chips(shard, name):
    return gather_placed(place_slab(shard, name + "_place"), name)


def gather_placed(full, name):
    h = full.shape[2]
    lo, hi = pl.ds(0, h // 2), pl.ds(h // 2, h - h // 2)

    def body(x_ref, o_ref, ssem, rsem):
        del x_ref
        mx, my, mc = _me()
        s, xs, ys, ds = 2 * mx + my, 2 * (1 - mx) + my, 2 * mx + (1 - my), 2 * (1 - mx) + (1 - my)
        xn, yn, sib = (1 - mx, my, mc), (mx, 1 - my, mc), (mx, my, 1 - mc)

        def cp(k, src, dst, dev):
            return _rcopy(src, dst, ssem.at[k], rsem.at[k], dev)

        own = o_ref.at[s, mc]
        sent = [cp(0, own, own, xn), cp(1, own, own, yn)]
        for c in sent:
            c.start()
        cp(0, own, o_ref.at[xs, mc], xn).wait_recv()
        sent += [cp(2, o_ref.at[xs, mc, lo], o_ref.at[xs, mc, lo], yn), cp(4, o_ref.at[xs, mc], o_ref.at[xs, mc], sib)]
        sent[-2].start()
        sent[-1].start()
        cp(1, own, o_ref.at[ys, mc], yn).wait_recv()
        sent += [cp(3, o_ref.at[ys, mc, hi], o_ref.at[ys, mc, hi], xn), cp(5, o_ref.at[ys, mc], o_ref.at[ys, mc], sib)]
        sent[-2].start()
        sent[-1].start()
        cp(2, own, o_ref.at[ds, mc, lo], yn).wait_recv()
        cp(3, own, o_ref.at[ds, mc, hi], xn).wait_recv()
        sent.append(cp(6, o_ref.at[ds, mc], o_ref.at[ds, mc], sib))
        sent[-1].start()
        for k, slot in ((4, xs), (5, ys), (6, ds)):
            cp(k, own, o_ref.at[slot, 1 - mc], sib).wait_recv()
        for c in sent:
            c.wait_send()

    return pl.pallas_call(
        body, name=name, out_shape=_sds(full.shape, full.dtype), in_specs=[ANY], out_specs=ANY,
        input_output_aliases={0: 0},
        scratch_shapes=[pltpu.SemaphoreType.DMA((7,)), pltpu.SemaphoreType.DMA((7,))],
    )(full)


def sibling_send_other_half(buf, name):
    def body(x_ref, o_ref, ssem, rsem):
        mx, my, mc = _me()
        sib = (mx, my, 1 - mc)
        cps = [_rcopy(x_ref.at[k, 1 - mc], o_ref.at[k], ssem.at[k], rsem.at[k], sib) for k in range(N_CHIP)]
        for cp in cps:
            cp.start()
        for cp in cps:
            cp.wait_recv()
        for cp in cps:
            cp.wait_send()

    n, _, h, w = buf.shape
    return pl.pallas_call(
        body, name=name, out_shape=_sds((n, h, w), buf.dtype), in_specs=[ANY], out_specs=ANY,
        scratch_shapes=[pltpu.SemaphoreType.DMA((N_CHIP,)), pltpu.SemaphoreType.DMA((N_CHIP,))],
    )(buf)


def chips_all_to_all(q, name):
    def body(x_ref, o_ref, ssem, rsem):
        mx, my, mc = _me()
        s = 2 * mx + my
        chips = [(1 - mx, my), (mx, 1 - my), (1 - mx, 1 - my)]
        cps = [_rcopy(x_ref.at[2 * px + py], o_ref.at[s], ssem.at[j], rsem.at[j], (px, py, mc))
               for j, (px, py) in enumerate(chips)]
        for cp in cps:
            cp.start()
        for j, (px, py) in enumerate(chips):
            ps = 2 * px + py
            _rcopy(x_ref.at[ps], o_ref.at[ps], ssem.at[j], rsem.at[j], (px, py, mc)).wait_recv()
        for cp in cps:
            cp.wait_send()

    return pl.pallas_call(
        body, name=name, out_shape=_sds(q.shape, q.dtype), in_specs=[ANY], out_specs=ANY,
        scratch_shapes=[pltpu.SemaphoreType.DMA((3,)), pltpu.SemaphoreType.DMA((3,))],
    )(q)


def sibling_join_halves(both, name, g=None):
    def body(x_ref, o_ref, ssem, rsem):
        del x_ref
        mx, my, mc = _me()
        sib = (mx, my, 1 - mc)
        o = o_ref if g is None else o_ref.at[g]
        cp = _rcopy(o.at[mc], o.at[mc], ssem, rsem, sib)
        cp.start()
        _rcopy(o.at[1 - mc], o.at[1 - mc], ssem, rsem, sib).wait_recv()
        cp.wait_send()

    return pl.pallas_call(
        body, name=name, out_shape=_sds(both.shape, both.dtype), in_specs=[ANY], out_specs=ANY,
        input_output_aliases={0: 0}, scratch_shapes=[pltpu.SemaphoreType.DMA, pltpu.SemaphoreType.DMA],
    )(both)


def add_own_half(buf, recv, wire, name):
    n, _, h, w = buf.shape
    th = _pick(h, _ROW_BLOCKS)

    def body(c_ref, a_ref, b_ref, o_ref):
        del c_ref
        o_ref[...] = (a_ref[...] + b_ref[...]).astype(o_ref.dtype)

    return pl.pallas_call(
        body, name=name, out_shape=_sds((n, h, w), wire),
        grid_spec=pltpu.PrefetchScalarGridSpec(
            num_scalar_prefetch=1, grid=(n, h // th),
            in_specs=[pl.BlockSpec((None, None, th, w), lambda k, r, c: (k, c[0], r, 0)),
                      pl.BlockSpec((None, th, w), lambda k, r, c: (k, r, 0))],
            out_specs=pl.BlockSpec((None, th, w), lambda k, r, c: (k, r, 0))),
    )(_idx(lax.axis_index("c")), buf, recv)


def sum_slots(q, r, name, dst=None, g=None):
    n, h, w = r.shape
    th = _pick(h, _ROW_BLOCKS)

    def body(i_ref, q_ref, r1, r2, r3, *rest):
        del i_ref
        rest[-1][...] = ((q_ref[...].astype(F32) + r1[...].astype(F32)) + r2[...].astype(F32)) + r3[...].astype(F32)

    def slot(d):
        return lambda i, ix: ((ix[0] + d) % N_CHIP, i, 0)

    idx = jnp.stack([2 * lax.axis_index("x") + lax.axis_index("y"), lax.axis_index("c")]).astype(jnp.int32)
    in_specs = [pl.BlockSpec((None, th, w), slot(d)) for d in (0, 1, 2, 3)]
    if dst is None:
        return pl.pallas_call(
            body, name=name, out_shape=_sds((2, h, w), F32),
            grid_spec=pltpu.PrefetchScalarGridSpec(
                num_scalar_prefetch=1, grid=(h // th,), in_specs=in_specs,
                out_specs=pl.BlockSpec((None, th, w), lambda i, ix: (ix[1], i, 0))),
        )(idx, q, r, r, r)
    return pl.pallas_call(
        body, name=name, out_shape=_sds(dst.shape, F32),
        grid_spec=pltpu.PrefetchScalarGridSpec(
            num_scalar_prefetch=1, grid=(h // th,), in_specs=in_specs + [ANY],
            out_specs=pl.BlockSpec((None, None, th, w), lambda i, ix: (g, ix[1], i, 0))),
        input_output_aliases={5: 0},
    )(idx, q, r, r, r, dst)


def reduce_scatter_chips(buf, tag, wire=F32, dst=None, g=None, recv=None):
    if recv is None:
        recv = sibling_send_other_half(buf, f"rs_sib_{tag}")
    q = add_own_half(buf, recv, wire, f"rs_add2_{tag}")
    r = chips_all_to_all(q, f"rs_a2a_{tag}")
    red = sum_slots(q, r, f"rs_add4_{tag}", dst=dst, g=g)
    return sibling_join_halves(red, f"rs_join_{tag}", g=g)


class Layout:
    def __init__(self, n_ctx, n_lat):
        self.C, self.L = n_ctx, n_lat
        self.PS = n_ctx + n_lat
        self.T = 2 * self.PS
        self.tr = _pick(math.gcd(n_ctx, n_lat), (256, 128))
        self.bps = self.PS // self.tr
        self.cb = n_ctx // self.tr
        self.nblk = self.T // self.tr
        self.tm = _pick(self.T, (1152, 768, 512, 256, 128))
        self.tm2 = _pick(self.T, (2304, 1152, 768, 512, 256, 128))
        self.tc = _pick(self.T, (768, 512, 256, 128))

    def seg(self, i):
        return jnp.where(i % self.bps < self.cb, 2, i // self.bps)


def rowwise(lay, name, fn, rows, segs=(), vecs=(), outs=(), sums=(), rider=None):
    tr, nblk = lay.tr, lay.nblk
    n_r, n_s, n_v, n_o = len(rows), len(segs), len(vecs), len(outs)
    lat_only = any(o[2:] for o in outs) or any(a.shape[0] != lay.T for a in rows)
    nsub = 1 if lat_only else (3 if nblk % 3 == 0 else 2 if nblk % 2 == 0 else 1)
    tb = tr * nsub

    def body(*refs):
        ins = refs[:n_r + n_s + n_v]
        ors = refs[n_r + n_s + n_v:]
        for sub in range(nsub):
            rs = slice(sub * tr, (sub + 1) * tr)
            seg = lay.seg(pl.program_id(0) * nsub + sub)
            vals = [r[rs, :] for r in ins[:n_r]] + [r[seg] for r in ins[n_r:n_r + n_s]] + [r[...] for r in ins[n_r + n_s:]]
            res = fn(*vals)
            for k in range(n_o):
                ors[k][rs, :] = res[k].astype(ors[k].dtype)
            for k in range(len(sums)):
                ors[n_o + k][sub] = res[n_o + k]

    def all_rows(i):
        return (i, 0)

    def lat_rows(i):
        return ((i // lay.bps) * (lay.bps - lay.cb) + jnp.maximum(i % lay.bps - lay.cb, 0), 0)

    in_specs = [pl.BlockSpec((tb, a.shape[1]), all_rows if a.shape[0] == lay.T else lat_rows) for a in rows]
    in_specs += [pl.BlockSpec(a.shape, lambda i: (0, 0, 0)) for a in segs]
    in_specs += [pl.BlockSpec(a.shape, lambda i: (0, 0)) for a in vecs]
    out_shape = [_sds((2 * lay.L if o[2:] else lay.T, o[0]), o[1]) for o in outs]
    out_shape += [_sds((nblk, r, w), F32) for r, w in sums]
    out_specs = [pl.BlockSpec((tb, o[0]), lat_rows if o[2:] else all_rows) for o in outs]
    out_specs += [pl.BlockSpec((nsub, r, w), lambda i: (i, 0, 0)) for r, w in sums]
    sem = "arbitrary" if any(o[2:] for o in outs) else "parallel"
    if rider is None:
        return pl.pallas_call(body, name=name, out_shape=out_shape, grid=(nblk // nsub,), in_specs=in_specs,
                              out_specs=out_specs, compiler_params=_cp((sem,), VMEM_BIG))(*rows, *segs, *vecs)
    return _host_call(body, rider, name, (nblk // nsub,), in_specs, out_specs, out_shape, (*rows, *segs, *vecs), (sem,),
                      n_r + n_s + n_v, n_o + len(sums))


def modulate(lay, h, mod, k_shift, k_scale, name):
    def fn(hb, m):
        return (hb * (1.0 + m[k_scale:k_scale + 1]) + m[k_shift:k_shift + 1],)
    return rowwise(lay, name, fn, [h], segs=[mod], outs=[(D, BF16)])[0]


def resid_ln(lay, h, y, mod, k_gate, coef, lnv, name, nxt=None, prev_ln=None, rider=None):
    def fn(hb, yb, m, *rest):
        ln = rest[-1]
        if prev_ln is not None:
            hb = hb * rest[-2][0:1] + rest[-2][1:2]
        z = ALPHA * hb + (coef * m[k_gate:k_gate + 1]) * yb
        mu = jnp.mean(z, axis=-1, keepdims=True)
        zc = z - mu
        var = jnp.mean(zc * zc, axis=-1, keepdims=True)
        rstd = lax.rsqrt(var + LN_EPS)
        xhat = zc * rstd
        out = xhat * ln[0:1] + ln[1:2]
        if nxt is None:
            return xhat, rstd
        mn = rest[0]
        return xhat, rstd, out * (1.0 + mn[nxt[2]:nxt[2] + 1]) + mn[nxt[1]:nxt[1] + 1]
    segs = [mod] if nxt is None else [mod, nxt[0]]
    vecs = [lnv] if prev_ln is None else [prev_ln, lnv]
    outs = [(D, F32), (1, F32)] + ([] if nxt is None else [(D, BF16)])
    return rowwise(lay, name, fn, [h, y], segs=segs, vecs=vecs, outs=outs, rider=rider)


def _ln_bwd_math(do, xh, rs, yb, gate, coef, ln):
    dxh = do * ln[0:1]
    m1 = jnp.mean(dxh, axis=-1, keepdims=True)
    m2 = jnp.mean(dxh * xh, axis=-1, keepdims=True)
    dz = rs * (dxh - m1 - xh * m2)
    s = jnp.concatenate([jnp.sum(do, axis=0, keepdims=True), jnp.sum(do * xh, axis=0, keepdims=True),
                         jnp.sum(coef * dz * yb, axis=0, keepdims=True)], axis=0)
    return (coef * gate) * dz, ALPHA * dz, s


def _mod_bwd_math(dr, dm, hb, scale):
    s = jnp.concatenate([jnp.sum(dm, axis=0, keepdims=True), jnp.sum(dm * hb, axis=0, keepdims=True)], axis=0)
    return dr + dm * (1.0 + scale), s


def mod_bwd(lay, dres, dhm, h, mod, k_scale, name, rider=None):
    def fn(dr, dm, hb, m):
        return _mod_bwd_math(dr, dm, hb, m[k_scale:k_scale + 1])
    return rowwise(lay, name, fn, [dres, dhm, h], segs=[mod], outs=[(D, F32, "lat")], sums=[(2, D)], rider=rider)


def modb_lnb(lay, dres, dhm, mod, k_scale, xhat, rstd, y, mod_p, k_gate, coef, lnv, name, rider=None):
    def fn(dr, dm, xh, rs, yb, m, mp, ln):
        dh, s2 = _mod_bwd_math(dr, dm, xh * ln[0:1] + ln[1:2], m[k_scale:k_scale + 1])
        dy, dres_p, s1 = _ln_bwd_math(dh, xh, rs, yb, mp[k_gate:k_gate + 1], coef, ln)
        return dy, dres_p, s1, s2
    return rowwise(lay, name, fn, [dres, dhm, xhat, rstd, y], segs=[mod, mod_p], vecs=[lnv],
                   outs=[(D, BF16), (D, F32)], sums=[(3, D), (2, D)], rider=rider)


def block_sums(lay, parts_list, name):
    n = len(parts_list)

    def body(*refs):
        for p_ref, o_ref in zip(refs[:n], refs[n:]):
            acc = [None, None, None]
            for i in range(lay.nblk):
                sg = 2 if i % lay.bps < lay.cb else i // lay.bps
                acc[sg] = p_ref[i] if acc[sg] is None else acc[sg] + p_ref[i]
            for k in range(3):
                o_ref[k] = acc[k]
            o_ref[3] = (acc[0] + acc[1]) + acc[2]

    return pl.pallas_call(body, name=name, out_shape=[_sds((4,) + p.shape[1:], F32) for p in parts_list],
                          in_specs=[VMEM_SPEC] * n, out_specs=[VMEM_SPEC] * n)(*parts_list)


def mm_nn(a, b, name, out_dtype=F32):
    m, k = a.shape
    n = b.shape[1]
    tm = _pick(m, (1152, 768, 512, 256, 128, 64, 32, 16, 8))
    tn = _pick(n, (1024, 768, 640, 512, 384, 256, 128))

    def body(a_ref, b_ref, o_ref):
        o_ref[...] = _nn(a_ref[...].astype(BF16), b_ref[...].astype(BF16)).astype(o_ref.dtype)

    return pl.pallas_call(body, name=name, out_shape=_sds((m, n), out_dtype), grid=(m // tm, n // tn),
                          in_specs=[pl.BlockSpec((tm, k), lambda i, j: (i, 0)), pl.BlockSpec((k, tn), lambda i, j: (0, j))],
                          out_specs=pl.BlockSpec((tm, tn), lambda i, j: (i, j)),
                          compiler_params=_cp(("parallel", "parallel"), VMEM_BIG))(a, b)


def mm_nt(a, b, name, out_dtype=F32, rider=None):
    m, k = a.shape
    n = b.shape[0]
    tm = _pick(m, (1152, 768, 512, 256, 128, 64, 32, 16, 8))
    tn = _pick(n, (1024, 768, 640, 512, 384, 256, 128))

    def body(a_ref, b_ref, o_ref):
        o_ref[...] = _nt(a_ref[...].astype(BF16), b_ref[...].astype(BF16)).astype(o_ref.dtype)

    res = _host_call(body, rider, name, (m // tm, n // tn),
                     [pl.BlockSpec((tm, k), lambda i, j: (i, 0)), pl.BlockSpec((tn, k), lambda i, j: (j, 0))],
                     [pl.BlockSpec((tm, tn), lambda i, j: (i, j))], [_sds((m, n), out_dtype)], (a, b),
                     ("parallel", "parallel"), 2, 1)
    return res[0] if rider is None else res


def mm_tn(a, b, name, rider=None):
    t, m = a.shape
    n = b.shape[1]
    tk = _pick(t, (1152, 768, 512, 256, 128, 64, 32, 16))
    tm = _pick(m, (1024, 640, 512, 384, 256, 128))

    def body(a_ref, b_ref, o_ref):
        @pl.when(pl.program_id(1) == 0)
        def _():
            o_ref[...] = jnp.zeros_like(o_ref)
        o_ref[...] += _tn(a_ref[...].astype(BF16), b_ref[...].astype(BF16))

    res = _host_call(body, rider, name, (m // tm, t // tk),
                     [pl.BlockSpec((tk, tm), lambda i, k: (k, i)), pl.BlockSpec((tk, n), lambda i, k: (k, 0))],
                     [pl.BlockSpec((tm, n), lambda i, k: (i, 0))], [_sds((m, n), F32)], (a, b),
                     ("parallel", "arbitrary"), 2, 1)
    return res[0] if rider is None else res


class Rider:
    def __init__(self, ins, outs, aliases, nsem, start, wait):
        self.ins, self.outs, self.aliases, self.nsem, self.start, self.wait = ins, outs, aliases, nsem, start, wait


def _chips_of(mx, my):
    return [(1 - mx, my), (mx, 1 - my), (1 - mx, 1 - my)]


def rider_gather_d2d(buf):
    def start(ins, outs, ssem, rsem):
        o = outs[0]
        mx, my, mc = _me()
        for j, (px, py) in enumerate(_chips_of(mx, my)):
            ps = 2 * px + py
            _rcopy(o.at[ps, mc], o.at[ps, mc], ssem.at[j], rsem.at[j], (mx, my, 1 - mc)).start()

    def wait(ins, outs, ssem, rsem):
        o = outs[0]
        mx, my, mc = _me()
        sib = (mx, my, 1 - mc)
        for j, (px, py) in enumerate(_chips_of(mx, my)):
            ps = 2 * px + py
            _rcopy(o.at[ps, 1 - mc], o.at[ps, 1 - mc], ssem.at[j], rsem.at[j], sib).wait_recv()
        for j, (px, py) in enumerate(_chips_of(mx, my)):
            ps = 2 * px + py
            _rcopy(o.at[ps, mc], o.at[ps, mc], ssem.at[j], rsem.at[j], sib).wait_send()

    return Rider([buf], [_sds(buf.shape, buf.dtype)], {0: 0}, 3, start, wait)


def rider_reduce_sib(buf):
    n, _, h, w = buf.shape

    def start(ins, outs, ssem, rsem):
        mx, my, mc = _me()
        for k in range(N_CHIP):
            _rcopy(ins[0].at[k, 1 - mc], outs[0].at[k], ssem.at[k], rsem.at[k], (mx, my, 1 - mc)).start()

    def wait(ins, outs, ssem, rsem):
        mx, my, mc = _me()
        for k in range(N_CHIP):
            _rcopy(ins[0].at[k, 1 - mc], outs[0].at[k], ssem.at[k], rsem.at[k], (mx, my, 1 - mc)).wait_recv()
        for k in range(N_CHIP):
            _rcopy(ins[0].at[k, 1 - mc], outs[0].at[k], ssem.at[k], rsem.at[k], (mx, my, 1 - mc)).wait_send()

    return Rider([buf], [_sds((n, h, w), buf.dtype)], {}, N_CHIP, start, wait)


def rider_gather_xy(buf):
    def peers():
        mx, my, mc = _me()
        return 2 * mx + my, mc, [(1 - mx, my), (mx, 1 - my)]

    def start(ins, outs, ssem, rsem):
        o = outs[0]
        s, mc, nb = peers()
        for j, (px, py) in enumerate(nb):
            _rcopy(o.at[s, mc], o.at[s, mc], ssem.at[j], rsem.at[j], (px, py, mc)).start()

    def wait(ins, outs, ssem, rsem):
        o = outs[0]
        s, mc, nb = peers()
        for j, (px, py) in enumerate(nb):
            _rcopy(o.at[2 * px + py, mc], o.at[2 * px + py, mc], ssem.at[j], rsem.at[j], (px, py, mc)).wait_recv()
        for j, (px, py) in enumerate(nb):
            _rcopy(o.at[s, mc], o.at[s, mc], ssem.at[j], rsem.at[j], (px, py, mc)).wait_send()

    return Rider([buf], [_sds(buf.shape, buf.dtype)], {0: 0}, 2, start, wait)


def rider_gather_fwd(buf):
    h2 = buf.shape[2] // 2
    lo, hi = pl.ds(0, h2), pl.ds(h2, buf.shape[2] - h2)

    def start(ins, outs, ssem, rsem):
        o = outs[0]
        mx, my, mc = _me()
        xs, ys = 2 * (1 - mx) + my, 2 * mx + (1 - my)
        _rcopy(o.at[xs, mc, lo], o.at[xs, mc, lo], ssem.at[0], rsem.at[0], (mx, 1 - my, mc)).start()
        _rcopy(o.at[ys, mc, hi], o.at[ys, mc, hi], ssem.at[1], rsem.at[1], (1 - mx, my, mc)).start()

    def wait(ins, outs, ssem, rsem):
        o = outs[0]
        mx, my, mc = _me()
        xs, ys, ds = 2 * (1 - mx) + my, 2 * mx + (1 - my), 2 * (1 - mx) + (1 - my)
        _rcopy(o.at[ds, mc, lo], o.at[ds, mc, lo], ssem.at[0], rsem.at[0], (mx, 1 - my, mc)).wait_recv()
        _rcopy(o.at[ds, mc, hi], o.at[ds, mc, hi], ssem.at[1], rsem.at[1], (1 - mx, my, mc)).wait_recv()
        _rcopy(o.at[xs, mc, lo], o.at[xs, mc, lo], ssem.at[0], rsem.at[0], (mx, 1 - my, mc)).wait_send()
        _rcopy(o.at[ys, mc, hi], o.at[ys, mc, hi], ssem.at[1], rsem.at[1], (1 - mx, my, mc)).wait_send()

    return Rider([buf], [_sds(buf.shape, buf.dtype)], {0: 0}, 2, start, wait)


def rider_reduce_copy(q, j, r=None):
    def peer():
        mx, my, mc = _me()
        px, py = _chips_of(mx, my)[j]
        return 2 * mx + my, 2 * px + py, (px, py, mc)

    def start(ins, outs, ssem, rsem):
        s, ps, dev = peer()
        _rcopy(ins[0].at[ps], outs[0].at[s], ssem.at[0], rsem.at[0], dev).start()

    def wait(ins, outs, ssem, rsem):
        s, ps, dev = peer()
        _rcopy(ins[0].at[ps], outs[0].at[ps], ssem.at[0], rsem.at[0], dev).wait_recv()
        _rcopy(ins[0].at[ps], outs[0].at[s], ssem.at[0], rsem.at[0], dev).wait_send()

    if r is None:
        return Rider([q], [_sds(q.shape, q.dtype)], {}, 1, start, wait)
    return Rider([q, r], [_sds(q.shape, q.dtype)], {1: 0}, 1, start, wait)


def rider_reduce_copies(q):
    def start(ins, outs, ssem, rsem):
        mx, my, mc = _me()
        s = 2 * mx + my
        for j, (px, py) in enumerate(_chips_of(mx, my)):
            _rcopy(ins[0].at[2 * px + py], outs[0].at[s], ssem.at[j], rsem.at[j], (px, py, mc)).start()

    def wait(ins, outs, ssem, rsem):
        mx, my, mc = _me()
        s = 2 * mx + my
        for j, (px, py) in enumerate(_chips_of(mx, my)):
            ps = 2 * px + py
            _rcopy(ins[0].at[ps], outs[0].at[ps], ssem.at[j], rsem.at[j], (px, py, mc)).wait_recv()
        for j, (px, py) in enumerate(_chips_of(mx, my)):
            _rcopy(ins[0].at[2 * px + py], outs[0].at[s], ssem.at[j], rsem.at[j], (px, py, mc)).wait_send()

    return Rider([q], [_sds(q.shape, q.dtype)], {}, 3, start, wait)


def rider_join(buf, g=None):
    def start(ins, outs, ssem, rsem):
        o = outs[0] if g is None else outs[0].at[g]
        mx, my, mc = _me()
        _rcopy(o.at[mc], o.at[mc], ssem.at[0], rsem.at[0], (mx, my, 1 - mc)).start()

    def wait(ins, outs, ssem, rsem):
        o = outs[0] if g is None else outs[0].at[g]
        mx, my, mc = _me()
        _rcopy(o.at[1 - mc], o.at[1 - mc], ssem.at[0], rsem.at[0], (mx, my, 1 - mc)).wait_recv()
        _rcopy(o.at[mc], o.at[mc], ssem.at[0], rsem.at[0], (mx, my, 1 - mc)).wait_send()

    return Rider([buf], [_sds(buf.shape, buf.dtype)], {0: 0}, 1, start, wait)


def _host_call(body, rider, name, grid, in_specs, out_specs, out_shape, operands, sem, n_in, n_out, aliases=None):
    aliases = dict(aliases or {})
    if rider is None:
        return pl.pallas_call(body, name=name, out_shape=out_shape, grid=grid, in_specs=in_specs, out_specs=out_specs,
                              input_output_aliases=aliases, compiler_params=_cp(sem, VMEM_BIG))(*operands)
    n_ri, n_ro = len(rider.ins), len(rider.outs)
    aliases.update({n_in + a: n_out + b for a, b in rider.aliases.items()})

    def hosted(*refs):
        ins, r_in = refs[:n_in], refs[n_in:n_in + n_ri]
        outs, r_out = refs[n_in + n_ri:n_in + n_ri + n_out], refs[n_in + n_ri + n_out:n_in + n_ri + n_out + n_ro]
        ssem, rsem = refs[-2], refs[-1]
        first = functools.reduce(lambda a, b: a & b, [pl.program_id(k) == 0 for k in range(len(grid))])
        last = functools.reduce(lambda a, b: a & b, [pl.program_id(k) == grid[k] - 1 for k in range(len(grid))])

        @pl.when(first)
        def _():
            rider.start(r_in, r_out, ssem, rsem)
        body(*ins, *outs)

        @pl.when(last)
        def _():
            rider.wait(r_in, r_out, ssem, rsem)

    return pl.pallas_call(
        hosted, name=name, out_shape=list(out_shape) + list(rider.outs), grid=grid,
        in_specs=list(in_specs) + [ANY] * n_ri, out_specs=list(out_specs) + [ANY] * n_ro,
        input_output_aliases=aliases,
        scratch_shapes=[pltpu.SemaphoreType.DMA((rider.nsem,)), pltpu.SemaphoreType.DMA((rider.nsem,))],
        compiler_params=_cp(("arbitrary",) * len(grid), VMEM_BIG))(*operands, *rider.ins)


def ffn_up(lay, hm, wbuf, ig, iu, ns, name, rider=None):
    tm = lay.tm

    def body(h_ref, wg_ref, wu_ref, up_ref, sl_ref, a_ref):
        hb = h_ref[...]
        g = _nt(hb, wg_ref[0])
        u = _nt(hb, wu_ref[0])
        sg = _sigmoid(g)
        sl = g * sg
        up_ref[0] = (u * (sg + sl * (1.0 - sg))).astype(BF16)
        sl_ref[0] = sl.astype(BF16)
        a_ref[0] = (sl * u).astype(BF16)

    spec_o = pl.BlockSpec((1, tm, ns), lambda s, i: (s, i, 0))
    return _host_call(
        body, rider, name, (N_CHIP, lay.T // tm),
        [pl.BlockSpec((tm, D), lambda s, i: (i, 0)), pl.BlockSpec((1, ns, D), lambda s, i: (s, ig, 0)),
         pl.BlockSpec((1, ns, D), lambda s, i: (s, iu, 0))],
        [spec_o] * 3, [_sds((N_CHIP, lay.T, ns), BF16)] * 3, (hm, wbuf, wbuf), ("parallel", "parallel"), 3, 3)


def slab_nn_acc(lay, zs, wbuf, idxs, ns, name, rider=None):
    tm = lay.tm2
    npair = len(zs)

    def body(*refs):
        o_ref = refs[-1]

        @pl.when(pl.program_id(1) == 0)
        def _():
            o_ref[...] = jnp.zeros_like(o_ref)
        acc = _nn(refs[0][0], refs[npair][0])
        for p in range(1, npair):
            acc += _nn(refs[p][0], refs[npair + p][0])
        o_ref[...] += acc

    in_specs = [pl.BlockSpec((1, tm, ns), lambda i, s: (s, i, 0)) for _ in zs]
    in_specs += [pl.BlockSpec((1, ns, D), functools.partial(lambda i, s, q: (s, q, 0), q=q)) for q in idxs]
    return _host_call(body, rider, name, (lay.T // tm, N_CHIP), in_specs, [pl.BlockSpec((tm, D), lambda i, s: (i, 0))],
                      [_sds((lay.T, D), F32)], (*zs, *([wbuf] * npair)), ("parallel", "arbitrary"), 2 * npair, 1)


def ffn_bwd_da(lay, dy, wbuf, idn, up, sl, ns, name, rider=None):
    tm = lay.tm

    def body(dy_ref, wd_ref, up_ref, sl_ref, dg_ref, du_ref):
        da = _nt(dy_ref[...], wd_ref[0])
        dg_ref[0] = (da * up_ref[0].astype(F32)).astype(BF16)
        du_ref[0] = (da * sl_ref[0].astype(F32)).astype(BF16)

    spec_z = pl.BlockSpec((1, tm, ns), lambda s, i: (s, i, 0))
    return _host_call(
        body, rider, name, (N_CHIP, lay.T // tm),
        [pl.BlockSpec((tm, D), lambda s, i: (i, 0)), pl.BlockSpec((1, ns, D), lambda s, i: (s, idn, 0)), spec_z, spec_z],
        [spec_z] * 2, [_sds((N_CHIP, lay.T, ns), BF16)] * 2, (dy, wbuf, up, sl), ("parallel", "parallel"), 4, 2)


def slab_tn(lay, z, x, gbuf, idx, ns, name, rider=None):
    tk = _pick(lay.T, (768, 512, 256, 128))

    def body(z_ref, x_ref, g_in, o_ref):
        del g_in

        @pl.when(pl.program_id(0) == 0)
        def _():
            o_ref[...] = jnp.zeros_like(o_ref)
        xv = x_ref[...]
        for s in range(N_CHIP):
            o_ref[s] += _tn(z_ref[s], xv)

    return _host_call(
        body, rider, name, (lay.T // tk,),
        [pl.BlockSpec((N_CHIP, tk, ns), lambda k: (0, k, 0)), pl.BlockSpec((tk, D), lambda k: (k, 0)), ANY],
        [pl.BlockSpec((N_CHIP, ns, D), lambda k: (0, idx, 0))], [_sds(gbuf.shape, F32)], (z, x, gbuf),
        ("arbitrary",), 3, 1, aliases={2: 0})


Q0, K0, V0, U0, QR0, KR0, PEXT = 0, 512, 640, 768, 1280, 1792, 1920


def rope_fwd(lay, p, cos, sin, name):
    def fn(pb, cs, sn):
        cs4 = jnp.concatenate([cs] * 4, axis=1)
        sn4 = jnp.concatenate([sn] * 4, axis=1)
        qr = pb[:, Q0:K0] * cs4 + pb[:, QR0:KR0] * sn4
        kr = pb[:, K0:V0] * cs + pb[:, KR0:PEXT] * sn
        return qr, kr, pb[:, V0:U0], pb[:, U0:QR0]
    return rowwise(lay, name, fn, [p, cos, sin], outs=[(ATT_W, BF16), (KV_W, BF16), (KV_W, BF16), (POOL_W, F32)])


def rope_bwd(lay, dqr, dkr, dv, du, cos, sin, name):
    def fn(dq, dk, dvb, dub, cs, sn):
        cs4 = jnp.concatenate([cs] * 4, axis=1)
        sn4 = jnp.concatenate([sn] * 4, axis=1)
        return (jnp.concatenate([dq * cs4, dk * cs, dvb, dub, dq * sn4, dk * sn], axis=1),)
    return rowwise(lay, name, fn, [dqr, dkr, dv, du, cos, sin], outs=[(PEXT, BF16)])[0]


def _attn_specs(lay):
    nbs, cbk, lbk = lay.PS // BLK, lay.C // BLK, lay.L // BLK

    def kv_map(j):
        return lambda s, n: (s * nbs + cbk + jnp.clip(n - cbk + j - 1, 0, lbk - 1), 0)

    win = [pl.BlockSpec((BLK, KV_W), kv_map(j)) for j in range(3)]
    ctx = pl.BlockSpec((lay.C, KV_W), lambda s, n: (s * (lay.PS // lay.C), 0))
    return nbs, cbk, lbk, win, ctx


def _attn_masks(n, cbk, lbk):
    row = lax.broadcasted_iota(jnp.int32, (BLK, BLK), 0)
    col = lax.broadcasted_iota(jnp.int32, (BLK, BLK), 1)
    m = n - cbk
    lat = n >= cbk
    valid = [lat & (m >= 1) & (col >= row), lat & (col >= 0), lat & (m <= lbk - 2) & (col <= row)]
    lane_lo = lax.broadcasted_iota(jnp.int32, (BLK, 2 * HEAD_DIM), 1) < HEAD_DIM
    return valid, lane_lo


def attn_fwd(lay, qr, kr, vb, sink_tab, name):
    nbs, cbk, lbk, win, ctx = _attn_specs(lay)

    def body(q_ref, k0, k1, k2, kc_ref, v0, v1, v2, vc_ref, sk_ref, o_ref, l_ref):
        n = pl.program_id(1)
        valid, lane_lo = _attn_masks(n, cbk, lbk)
        valid4 = [jnp.concatenate([v] * 4, axis=0) for v in valid]
        ks = [k0[...], k1[...], k2[...]]
        vs = [v0[...], v1[...], v2[...]]
        kc, vc = kc_ref[...], vc_ref[...]
        q2s = [q_ref[:, p * 128:(p + 1) * 128] for p in range(4)]
        outs, lses = [], []
        for hh in range(2):
            sel = lane_lo == (hh == 0)
            qm = jnp.concatenate([jnp.where(sel, q2, jnp.zeros_like(q2)) for q2 in q2s], axis=0)
            sk = jnp.concatenate([jnp.broadcast_to(sk_ref[p:p + 1, hh * HEAD_DIM:hh * HEAD_DIM + 1], (BLK, 1))
                                  for p in range(4)], axis=0)
            sw = [jnp.where(valid4[j], _nt(qm, ks[j]) * ATT_SCALE, NEG_INF) for j in range(3)]
            sc = _nt(qm, kc) * ATT_SCALE
            mx = jnp.maximum(jnp.maximum(jnp.maximum(sw[0].max(-1, keepdims=True), sw[1].max(-1, keepdims=True)),
                                         jnp.maximum(sw[2].max(-1, keepdims=True), sc.max(-1, keepdims=True))), sk)
            ew = [jnp.exp(s - mx) for s in sw]
            ec = jnp.exp(sc - mx)
            den = ew[0].sum(-1, keepdims=True) + ew[1].sum(-1, keepdims=True) + ew[2].sum(-1, keepdims=True)
            den = den + ec.sum(-1, keepdims=True) + jnp.exp(sk - mx)
            o = _nn((ec / den).astype(BF16), vc)
            for j in range(3):
                o += _nn((ew[j] / den).astype(BF16), vs[j])
            outs.append(o)
            lses.append(mx + jnp.log(den))
        for p in range(4):
            rows = slice(p * BLK, (p + 1) * BLK)
            o_ref[:, p * 128:(p + 1) * 128] = jnp.where(lane_lo, outs[0][rows], outs[1][rows]).astype(o_ref.dtype)
            l_ref[:, p * 128:(p + 1) * 128] = jnp.where(lane_lo, jnp.broadcast_to(lses[0][rows], (BLK, 128)),
                                                        jnp.broadcast_to(lses[1][rows], (BLK, 128)))

    qspec = pl.BlockSpec((BLK, ATT_W), lambda s, n: (s * nbs + n, 0))
    return pl.pallas_call(
        body, name=name, out_shape=[_sds((lay.T, ATT_W + POOL_W), BF16), _sds((lay.T, ATT_W), F32)], grid=(2, nbs),
        in_specs=[qspec] + win + [ctx] + win + [ctx] + [pl.BlockSpec((8, 128), lambda s, n: (0, 0))],
        out_specs=[qspec, qspec], compiler_params=_cp(("parallel", "parallel")))(qr, kr, kr, kr, kr, vb, vb, vb, vb, sink_tab)


def attn_bwd(lay, qr, kr, vb, sink_tab, lse, datt, name, rider=None):
    nbs, cbk, lbk, win, ctx = _attn_specs(lay)
    C, PS = lay.C, lay.PS

    def body(q_ref, k0, k1, k2, kc_ref, v0, v1, v2, vc_ref, sk_ref, l_ref, do_ref, dq_ref, dk_ref, dv_ref, ds_ref):
        n = pl.program_id(1)
        valid, lane_lo = _attn_masks(n, cbk, lbk)

        @pl.when(n == 0)
        def _():
            dk_ref[...] = jnp.zeros_like(dk_ref)
            dv_ref[...] = jnp.zeros_like(dv_ref)
            ds_ref[...] = jnp.zeros_like(ds_ref)

        ks = [k0[...], k1[...], k2[...], kc_ref[...]]
        vs = [v0[...], v1[...], v2[...], vc_ref[...]]
        valid4 = [jnp.concatenate([v] * 4, axis=0) for v in valid]
        dks = [jnp.zeros((BLK, KV_W), F32)] * 3 + [jnp.zeros((C, KV_W), F32)]
        dvs = list(dks)
        q2s = [q_ref[:, p * 128:(p + 1) * 128] for p in range(4)]
        do2s = [do_ref[:, p * 128:(p + 1) * 128].astype(BF16) for p in range(4)]
        lse2s = [l_ref[:, p * 128:(p + 1) * 128] for p in range(4)]
        dq_h, dd_h = [], []
        for hh in range(2):
            sel = lane_lo == (hh == 0)
            qm = jnp.concatenate([jnp.where(sel, q2, jnp.zeros_like(q2)) for q2 in q2s], axis=0)
            dom = jnp.concatenate([jnp.where(sel, d2, jnp.zeros_like(d2)) for d2 in do2s], axis=0)
            lse_h = jnp.concatenate([l2[:, hh * HEAD_DIM:hh * HEAD_DIM + 1] for l2 in lse2s], axis=0)
            ps, dps = [], []
            for j in range(4):
                s = _nt(qm, ks[j]) * ATT_SCALE
                if j < 3:
                    s = jnp.where(valid4[j], s, NEG_INF)
                ps.append(jnp.exp(s - lse_h))
                dps.append(_nt(dom, vs[j]))
            dd = (ps[0] * dps[0]).sum(-1, keepdims=True) + (ps[1] * dps[1]).sum(-1, keepdims=True)
            dd = dd + (ps[2] * dps[2]).sum(-1, keepdims=True) + (ps[3] * dps[3]).sum(-1, keepdims=True)
            dq = jnp.zeros((4 * BLK, 128), F32)
            for j in range(4):
                dsb = (ps[j] * (dps[j] - dd) * ATT_SCALE).astype(BF16)
                dq += _nn(dsb, ks[j])
                dks[j] = dks[j] + _tn(dsb, qm)
                dvs[j] = dvs[j] + _tn(ps[j].astype(BF16), dom)
            dq_h.append(dq)
            dd_h.append(dd)
        for p in range(4):
            sl = slice(p * 128, (p + 1) * 128)
            rows = slice(p * BLK, (p + 1) * BLK)
            dq_ref[:, sl] = jnp.where(lane_lo, dq_h[0][rows], dq_h[1][rows])
            dd2 = jnp.where(lane_lo, jnp.broadcast_to(dd_h[0][rows], (BLK, 128)), jnp.broadcast_to(dd_h[1][rows], (BLK, 128)))
            psink = jnp.exp(sk_ref[p:p + 1, :] - lse2s[p])
            ds_ref[0, p:p + 1, :] += -jnp.sum(psink * dd2, axis=0, keepdims=True)
        dk_ref[0:C, :] += dks[3]
        dv_ref[0:C, :] += dvs[3]
        for j in range(3):
            r0 = pl.multiple_of((cbk + jnp.clip(n - cbk + j - 1, 0, lbk - 1)) * BLK, BLK)
            dk_ref[pl.ds(r0, BLK), :] += dks[j]
            dv_ref[pl.ds(r0, BLK), :] += dvs[j]

    qspec = pl.BlockSpec((BLK, ATT_W), lambda s, n: (s * nbs + n, 0))
    kvout = pl.BlockSpec((PS, KV_W), lambda s, n: (s, 0))
    return _host_call(
        body, rider, name, (2, nbs),
        [qspec] + win + [ctx] + win + [ctx] + [pl.BlockSpec((8, 128), lambda s, n: (0, 0)), qspec, qspec],
        [qspec, kvout, kvout, pl.BlockSpec((1, 8, 128), lambda s, n: (s, 0, 0))],
        [_sds((lay.T, ATT_W), F32), _sds((lay.T, KV_W), F32), _sds((lay.T, KV_W), F32), _sds((2, 8, 128), F32)],
        (qr, kr, kr, kr, kr, vb, vb, vb, vb, sink_tab, lse, datt), ("parallel", "arbitrary"), 12, 4)


def _winsum(x, r):
    n = x.shape[0]
    t = lax.broadcasted_iota(jnp.int32, x.shape, 0)
    acc = x
    for o in range(1, r + 1):
        acc = acc + jnp.where(t >= o, pltpu.roll(x, o, 0), 0.0) + jnp.where(t < n - o, pltpu.roll(x, n - o, 0), 0.0)
    return acc


def _wincount(n, r):
    t = lax.broadcasted_iota(jnp.int32, (n, 128), 0)
    return (jnp.minimum(t + r, n - 1) - jnp.maximum(t - r, 0) + 1).astype(F32)


def pool_fwd(lay, u, w_pool, scale, cat, name):
    segs = [(0, lay.C), (lay.C, lay.L)]

    def body(u_ref, w_ref, s_ref, c_in, o_ref):
        del c_in
        for r0, n in segs:
            for g, wd in enumerate(POOL_WINDOWS):
                sl = slice(g * 128, (g + 1) * 128)
                x = u_ref[r0:r0 + n, sl]
                d = _winsum(x, wd // 2) / _wincount(n, wd // 2) - x
                y = _nn(d.astype(BF16), w_ref[g].astype(BF16)) * s_ref[:, sl]
                o_ref[r0:r0 + n, sl] = y.astype(o_ref.dtype)

    spec = pl.BlockSpec((lay.PS, POOL_W), lambda s: (s, 0))
    return pl.pallas_call(
        body, name=name, out_shape=_sds(cat.shape, BF16), grid=(2,),
        in_specs=[spec, pl.BlockSpec(w_pool.shape, lambda s: (0, 0, 0)), pl.BlockSpec((1, POOL_W), lambda s: (0, 0)), ANY],
        out_specs=pl.BlockSpec((lay.PS, POOL_W), lambda s: (s, 1)), input_output_aliases={3: 0},
        compiler_params=_cp(("parallel",), VMEM_BIG))(u, w_pool, scale, cat)


def pool_bwd(lay, u, dcat, w_pool, scale, name):
    segs = [(0, lay.C), (lay.C, lay.L)]

    def body(u_ref, dp_ref, w_ref, s_ref, du_ref, dw_ref, dsc_ref):
        for g, wd in enumerate(POOL_WINDOWS):
            sl = slice(g * 128, (g + 1) * 128)
            wb = w_ref[g].astype(BF16)
            dw = jnp.zeros((128, 128), F32)
            dsc = jnp.zeros((1, 128), F32)
            for r0, n in segs:
                x = u_ref[r0:r0 + n, sl]
                cnt = _wincount(n, wd // 2)
                d = (_winsum(x, wd // 2) / cnt - x).astype(BF16)
                dp = dp_ref[r0:r0 + n, sl]
                dsc += jnp.sum(_nn(d, wb) * dp, axis=0, keepdims=True)
                dyp = (dp * s_ref[:, sl]).astype(BF16)
                dw += _tn(d, dyp)
                dd = _nt(dyp, wb)
                du_ref[r0:r0 + n, sl] = _winsum(dd / cnt, wd // 2) - dd
            dw_ref[0, g] = dw
            dsc_ref[0, :, sl] = dsc

    spec = pl.BlockSpec((lay.PS, POOL_W), lambda s: (s, 0))
    return pl.pallas_call(
        body, name=name,
        out_shape=[_sds((lay.T, POOL_W), F32), _sds((2, 4, 128, 128), F32), _sds((2, 1, POOL_W), F32)], grid=(2,),
        in_specs=[spec, pl.BlockSpec((lay.PS, POOL_W), lambda s: (s, 1)), pl.BlockSpec(w_pool.shape, lambda s: (0, 0, 0)),
                  pl.BlockSpec((1, POOL_W), lambda s: (0, 0))],
        out_specs=[spec, pl.BlockSpec((1, 4, 128, 128), lambda s: (s, 0, 0, 0)), pl.BlockSpec((1, 1, POOL_W), lambda s: (s, 0, 0))],
        compiler_params=_cp(("parallel",), VMEM_BIG))(u, dcat, w_pool, scale)


CONV_OFFS = (-1, 0, 1, 2)
CW = 256


def _shift_rows(x, o):
    if o == 0:
        return x
    n = x.shape[0]
    t = lax.broadcasted_iota(jnp.int32, x.shape, 0)
    if o < 0:
        return jnp.where(t >= -o, pltpu.roll(x, -o, 0), 0.0)
    return jnp.where(t < n - o, pltpu.roll(x, n - o, 0), 0.0)


def conv_fwd(lay, p, col0, w, b, name):
    segs = [(0, lay.C), (lay.C, lay.L)]
    cb0 = col0 // CW

    def body(x_ref, w_ref, b_ref, o_ref):
        for r0, n in segs:
            x = x_ref[r0:r0 + n, :]
            y = jnp.broadcast_to(b_ref[...], x.shape)
            for k, o in enumerate(CONV_OFFS):
                y = y + _shift_rows(x, o) * w_ref[k:k + 1, :]
            o_ref[r0:r0 + n, :] = y

    return pl.pallas_call(
        body, name=name, out_shape=_sds((lay.T, D), F32), grid=(2, D // CW),
        in_specs=[pl.BlockSpec((lay.PS, CW), lambda s, j: (s, cb0 + j)), pl.BlockSpec((4, CW), lambda s, j: (0, j)),
                  pl.BlockSpec((1, CW), lambda s, j: (0, j))],
        out_specs=pl.BlockSpec((lay.PS, CW), lambda s, j: (s, j)),
        compiler_params=_cp(("parallel", "parallel")))(p, w, b)


def conv_bwd(lay, p, col0, w, duc, dp, name):
    segs = [(0, lay.C), (lay.C, lay.L)]
    cb0 = col0 // CW

    def body(x_ref, w_ref, g_ref, dp_in, du_ref, dw_ref, db_ref):
        del dp_in
        dws =[jnp.zeros((1, CW), F32)] * 4
        db = jnp.zeros((1, CW), F32)
        for r0, n in segs:
            x = x_ref[r0:r0 + n, :]
            g = g_ref[r0:r0 + n, :]
            du = jnp.zeros_like(g)
            for k, o in enumerate(CONV_OFFS):
                du = du + _shift_rows(g, -o) * w_ref[k:k + 1, :]
                dws[k] = dws[k] + jnp.sum(g * _shift_rows(x, o), axis=0, keepdims=True)
            db = db + jnp.sum(g, axis=0, keepdims=True)
            du_ref[r0:r0 + n, :] = du.astype(du_ref.dtype)
        dw_ref[0] = jnp.concatenate(dws, axis=0)
        db_ref[0] = db

    return pl.pallas_call(
        body, name=name, out_shape=[_sds(dp.shape, BF16), _sds((2, 4, D), F32), _sds((2, 1, D), F32)], grid=(2, D // CW),
        in_specs=[pl.BlockSpec((lay.PS, CW), lambda s, j: (s, cb0 + j)), pl.BlockSpec((4, CW), lambda s, j: (0, j)),
                  pl.BlockSpec((lay.PS, CW), lambda s, j: (s, j)), ANY],
        out_specs=[pl.BlockSpec((lay.PS, CW), lambda s, j: (s, D // CW + j)), pl.BlockSpec((1, 4, CW), lambda s, j: (s, 0, j)),
                   pl.BlockSpec((1, 1, CW), lambda s, j: (s, 0, j))],
        input_output_aliases={3: 0}, compiler_params=_cp(("parallel", "parallel")))(p, w, duc, dp)


def _softplus_neg(lam):
    z = -lam
    w = jnp.exp(-jnp.abs(z))
    log1p = jnp.where(w < 1e-2, w * (1.0 - w * (0.5 - w / 3.0)), jnp.log(1.0 + w))
    return jnp.maximum(z, 0.0) + log1p, -_sigmoid(z)


def _neg_expm1(x):
    series = -x * (1.0 + x * (0.5 + x * (1.0 / 6.0 + x * (1.0 / 24.0 + x * (1.0 / 120.0)))))
    return jnp.where(x > -0.05, series, 1.0 - jnp.exp(x))


def _lru_gates(x, xb, wa, wx, ba, bx, lam):
    r = _sigmoid(_nn(xb, wa.astype(BF16)) + ba)
    gi = _sigmoid(_nn(xb, wx.astype(BF16)) + bx)
    sp, dsp = _softplus_neg(lam)
    la = -LRU_C * r * sp
    a = jnp.exp(la)
    sq = jnp.sqrt(_neg_expm1(2.0 * la))
    return r, gi, sp, dsp, a, sq


def lru_coeffs(lay, uc, wa, wx, vec, name):
    tr = lay.tc

    def body(x_ref, wa_ref, wx_ref, v_ref, a_ref, b_ref):
        for h in range(8):
            sl = slice(h * 128, (h + 1) * 128)
            x = x_ref[:, sl]
            xb = x.astype(BF16)
            for d in range(2):
                _, gi, _, _, a, sq = _lru_gates(x, xb, wa_ref[d, h], wx_ref[d, h], v_ref[d:d + 1, sl],
                                                v_ref[2 + d:3 + d, sl], v_ref[4 + d:5 + d, sl])
                a_ref[d, h] = a
                b_ref[d, h] = sq * (gi * x)

    wspec = pl.BlockSpec((2, 8, 128, 128), lambda i: (0, 0, 0, 0))
    ospec = pl.BlockSpec((2, 8, tr, 128), lambda i: (0, 0, i, 0))
    return pl.pallas_call(
        body, name=name, out_shape=[_sds((2, 8, lay.T, 128), F32)] * 2, grid=(lay.T // tr,),
        in_specs=[pl.BlockSpec((tr, D), lambda i: (i, 0)), wspec, wspec, pl.BlockSpec((6, D), lambda i: (0, 0))],
        out_specs=[ospec, ospec], compiler_params=_cp(("parallel",), VMEM_BIG))(uc, wa, wx, vec)


def lru_coeffs_bwd(lay, uc, wa, wx, vec, da, db, name, rider=None):
    tr = lay.tc

    def body(x_ref, wa_ref, wx_ref, v_ref, da_ref, db_ref, dx_ref, dwa_ref, dwx_ref, dv_ref):
        @pl.when(pl.program_id(0) == 0)
        def _():
            dwa_ref[...] = jnp.zeros_like(dwa_ref)
            dwx_ref[...] = jnp.zeros_like(dwx_ref)
            dv_ref[...] = jnp.zeros_like(dv_ref)

        for h in range(8):
            sl = slice(h * 128, (h + 1) * 128)
            x = x_ref[:, sl]
            xb = x.astype(BF16)
            dx = jnp.zeros_like(x)
            for d in range(2):
                wab, wxb = wa_ref[d, h].astype(BF16), wx_ref[d, h].astype(BF16)
                r, gi, sp, dsp, a, sq = _lru_gates(x, xb, wa_ref[d, h], wx_ref[d, h], v_ref[d:d + 1, sl],
                                                   v_ref[2 + d:3 + d, sl], v_ref[4 + d:5 + d, sl])
                dbv, dav = db_ref[d, h], da_ref[d, h]
                t1 = dbv * sq
                dgi = t1 * x
                dx = dx + t1 * gi
                dla = dav * a - (dbv * gi * x) * (a * a) / sq
                dr = dla * (-LRU_C * sp)
                dlam = jnp.sum(dla * (-LRU_C * r), axis=0, keepdims=True) * dsp
                dpa = dr * r * (1.0 - r)
                dpx = dgi * gi * (1.0 - gi)
                dpab, dpxb = dpa.astype(BF16), dpx.astype(BF16)
                dwa_ref[d, h] += _tn(xb, dpab)
                dwx_ref[d, h] += _tn(xb, dpxb)
                dx = dx + _nt(dpab, wab) + _nt(dpxb, wxb)
                dv_ref[d:d + 1, sl] += jnp.sum(dpa, axis=0, keepdims=True)
                dv_ref[2 + d:3 + d, sl] += jnp.sum(dpx, axis=0, keepdims=True)
                dv_ref[4 + d:5 + d, sl] += dlam
            dx_ref[:, sl] = dx

    wspec = pl.BlockSpec((2, 8, 128, 128), lambda i: (0, 0, 0, 0))
    gspec = pl.BlockSpec((2, 8, tr, 128), lambda i: (0, 0, i, 0))
    vspec = pl.BlockSpec((6, D), lambda i: (0, 0))
    xspec = pl.BlockSpec((tr, D), lambda i: (i, 0))
    return _host_call(
        body, rider, name, (lay.T // tr,), [xspec, wspec, wspec, vspec, gspec, gspec], [xspec, wspec, wspec, vspec],
        [_sds((lay.T, D), F32), _sds((2, 8, 128, 128), F32), _sds((2, 8, 128, 128), F32), _sds((6, D), F32)],
        (uc, wa, wx, vec, da, db), ("arbitrary",), 6, 4)


GB = 2
SCAN_UNROLL = 8


def _tile_scan(a, b, up):
    t = lax.broadcasted_iota(jnp.int32, a.shape, 0)
    for d in (1, 2, 4):
        sh = 8 - d if up else d
        m = (t < 8 - d) if up else (t >= d)
        a_prev, b_prev = pltpu.roll(a, sh, 0), pltpu.roll(b, sh, 0)
        b = jnp.where(m, a * b_prev + b, b)
        a = jnp.where(m, a * a_prev, a)
    return a, b


def lru_scan(lay, a, b, name):
    segs = [(0, lay.C), (lay.C, lay.L)]

    def body(a_ref, b_ref, s_ref):
        for d in range(2):
            rev = d == 1
            state = tuple(jnp.zeros((1, 128), F32) for _ in range(GB))
            for base, n in segs:
                nt = n // 8

                def step(j, c, base=base, nt=nt, rev=rev, d=d):
                    c = list(c)
                    for u in range(SCAN_UNROLL):
                        jj = j * SCAN_UNROLL + u
                        r0 = pl.multiple_of(base + 8 * ((nt - 1 - jj) if rev else jj), 8)
                        for g in range(GB):
                            at, bt = _tile_scan(a_ref[d, g, pl.ds(r0, 8), :], b_ref[d, g, pl.ds(r0, 8), :], rev)
                            h = at * c[g] + bt
                            s_ref[d, g, pl.ds(r0, 8), :] = h
                            c[g] = h[0:1] if rev else h[7:8]
                    return tuple(c)

                state = lax.fori_loop(0, nt // SCAN_UNROLL, step, state)

    spec = pl.BlockSpec((2, GB, lay.PS, 128), lambda s, hb: (0, hb, s, 0))
    return pl.pallas_call(
        body, name=name, out_shape=_sds((2, 8, lay.T, 128), F32), grid=(2, 8 // GB),
        in_specs=[spec, spec], out_specs=spec, compiler_params=_cp(("parallel", "parallel"), VMEM_BIG))(a, b)


def lru_scan_bwd(lay, a, s, dy, name):
    segs = [(0, lay.C), (lay.C, lay.L)]
    C, PS = lay.C, lay.PS

    def body(a_ref, s_ref, g_ref, da_ref, db_ref):
        t = lax.broadcasted_iota(jnp.int32, (8, 128), 0)
        for d in range(2):
            rev = d == 1
            carry = tuple(jnp.zeros((1, 128), F32) for _ in range(GB))
            for si in (1, 0):
                base, n = segs[si]
                nt = n // 8

                def step(j, c, base=base, nt=nt, rev=rev, d=d):
                    c = list(c)
                    for u in range(SCAN_UNROLL):
                        jj = j * SCAN_UNROLL + u
                        r0 = pl.multiple_of(base + 8 * (jj if rev else (nt - 1 - jj)), 8)
                        if rev:
                            rn = pl.multiple_of(jnp.where(r0 == PS - 8, 0, r0 + 8), 8)
                            nb_zero = r0 == C - 8
                        else:
                            rn = pl.multiple_of(jnp.maximum(r0 - 8, 0), 8)
                            nb_zero = r0 == 0
                        for g in range(GB):
                            av = a_ref[d, g, pl.ds(r0, 8), :]
                            gv = g_ref[g, pl.ds(r0, 8), :]
                            sv = s_ref[d, g, pl.ds(r0, 8), :]
                            nbt = s_ref[d, g, pl.ds(rn, 8), :]
                            at, bt = _tile_scan(av, av * gv, not rev)
                            m = at * c[g] + bt
                            if rev:
                                m_next = jnp.where(t >= 1, pltpu.roll(m, 1, 0), c[g])
                                nb = jnp.where(nb_zero, 0.0, nbt[0:1])
                                h_prev = jnp.where(t < 7, pltpu.roll(sv, 7, 0), nb)
                                c[g] = m[7:8]
                            else:
                                m_next = jnp.where(t < 7, pltpu.roll(m, 7, 0), c[g])
                                nb = jnp.where(nb_zero, 0.0, nbt[7:8])
                                h_prev = jnp.where(t >= 1, pltpu.roll(sv, 1, 0), nb)
                                c[g] = m[0:1]
                            lam = gv + m_next
                            db_ref[d, g, pl.ds(r0, 8), :] = lam
                            da_ref[d, g, pl.ds(r0, 8), :] = lam * h_prev
                    return tuple(c)

                carry = lax.fori_loop(0, nt // SCAN_UNROLL, step, carry)

    spec = pl.BlockSpec((2, GB, lay.PS, 128), lambda s, hb: (0, hb, s, 0))
    return pl.pallas_call(
        body, name=name, out_shape=[_sds((2, 8, lay.T, 128), F32)] * 2, grid=(2, 8 // GB),
        in_specs=[spec, spec, pl.BlockSpec((GB, lay.PS, 128), lambda s, hb: (hb, s, 0))],
        out_specs=[spec, spec], compiler_params=_cp(("parallel", "parallel"), VMEM_BIG))(a, s, dy)


def _gelu(x):
    k = math.sqrt(2.0 / math.pi)
    t = jnp.tanh(k * (x + 0.044715 * x * x * x))
    return 0.5 * x * (1.0 + t), 0.5 * (1.0 + t) + 0.5 * x * (1.0 - t * t) * k * (1.0 + 3 * 0.044715 * x * x)


def lru_gate(lay, p, s, name):
    tr = lay.tr

    def body(g_ref, s_ref, o_ref):
        for h in range(8):
            sl = slice(h * 128, (h + 1) * 128)
            o_ref[:, sl] = (_gelu(g_ref[:, sl])[0] * (s_ref[0, h] + s_ref[1, h])).astype(o_ref.dtype)

    return pl.pallas_call(
        body, name=name, out_shape=_sds((lay.T, D), BF16), grid=(lay.nblk,),
        in_specs=[pl.BlockSpec((tr, D), lambda i: (i, 0)), pl.BlockSpec((2, 8, tr, 128), lambda i: (0, 0, i, 0))],
        out_specs=pl.BlockSpec((tr, D), lambda i: (i, 0)), compiler_params=_cp(("parallel",)))(p, s)


def lru_gate_bwd(lay, p, s, do, name):
    tr = lay.tr

    def body(g_ref, s_ref, do_ref, dg_ref, dy_ref):
        for h in range(8):
            sl = slice(h * 128, (h + 1) * 128)
            ge, dge = _gelu(g_ref[:, sl])
            dov = do_ref[:, sl]
            dg_ref[:, sl] = (dov * (s_ref[0, h] + s_ref[1, h]) * dge).astype(dg_ref.dtype)
            dy_ref[h] = dov * ge

    xspec = pl.BlockSpec((tr, D), lambda i: (i, 0))
    return pl.pallas_call(
        body, name=name, out_shape=[_sds((lay.T, 2 * D), BF16), _sds((8, lay.T, 128), F32)], grid=(lay.nblk,),
        in_specs=[xspec, pl.BlockSpec((2, 8, tr, 128), lambda i: (0, 0, i, 0)), xspec],
        out_specs=[xspec, pl.BlockSpec((8, tr, 128), lambda i: (0, i, 0))],
        compiler_params=_cp(("parallel",)))(p, s, do)


def silu_rows(x, name):
    def body(x_ref, o_ref):
        v = x_ref[...]
        o_ref[...] = (v * _sigmoid(v)).astype(o_ref.dtype)
    return pl.pallas_call(body, name=name, out_shape=_sds(x.shape, BF16), in_specs=[VMEM_SPEC], out_specs=VMEM_SPEC)(x)


def mod_grad_rows(gath, name):
    w = gath.shape[-1]

    def body(g_ref, dm_ref, db_ref):
        dm_ref[...] = jnp.zeros_like(dm_ref)
        for l in range(2):
            ctx = g_ref[0, 3 * l + 2:3 * l + 3, :]
            tot = g_ref[0, 3 * l:3 * l + 1, :] + g_ref[0, 3 * l + 1:3 * l + 2, :]
            for k in range(8):
                dm_ref[l, 2 * k:2 * k + 2, :] = g_ref[k, 3 * l:3 * l + 2, :]
                if k:
                    ctx = ctx + g_ref[k, 3 * l + 2:3 * l + 3, :]
                    tot = tot + (g_ref[k, 3 * l:3 * l + 1, :] + g_ref[k, 3 * l + 1:3 * l + 2, :])
            dm_ref[l, 16:17, :] = ctx
            db_ref[l:l + 1, :] = tot + ctx

    return pl.pallas_call(body, name=name, out_shape=[_sds((2, 32, w), F32), _sds((2, w), F32)],
                          in_specs=[VMEM_SPEC], out_specs=[VMEM_SPEC, VMEM_SPEC])(gath)


def cctx_grad(p, c_ctx, name):
    def body(a_ref, c_ref, o_ref):
        cv = c_ref[...]
        sg = _sigmoid(cv)
        o_ref[...] = 0.5 * (a_ref[0, 0:1, :] + a_ref[1, 0:1, :]) * (sg * (1.0 + cv * (1.0 - sg)))
    return pl.pallas_call(body, name=name, out_shape=_sds((1, D), F32), in_specs=[VMEM_SPEC] * 2,
                          out_specs=VMEM_SPEC)(p, c_ctx)


def loss_lnb(lay, xhat, rstd, y, tgt, mod, k_gate, coef, lnv, name):
    def fn(xh, rs, yb, tb, m, ln):
        lat = (pl.program_id(0) % lay.bps) >= lay.cb
        e = jnp.where(lat, xh * ln[0:1] + ln[1:2] - tb, 0.0)
        dy, dres, s1 = _ln_bwd_math(e * (1.0 / D), xh, rs, yb, m[k_gate:k_gate + 1], coef, ln)
        return dy, dres, s1, jnp.sum(e * e, axis=0, keepdims=True) * (0.5 / D)
    return rowwise(lay, name, fn, [xhat, rstd, y, tgt], segs=[mod], vecs=[lnv],
                   outs=[(D, BF16), (D, F32)], sums=[(3, D), (1, D)])


def adamw(w, g, m, v, name):
    shape = w.shape
    w2, g2, m2, v2 = (t.reshape(-1, shape[-1]) for t in (w, g, m, v))
    rows, width = w2.shape
    tr = 256 if rows % 256 == 0 else rows
    c1 = 1.0 - ADAM_B1 ** ADAM_STEP
    c2 = 1.0 - ADAM_B2 ** ADAM_STEP

    def body(w_ref, g_ref, m_ref, v_ref, d_ref, mo_ref, vo_ref):
        gv = g_ref[...]
        mn = ADAM_B1 * m_ref[...] + (1.0 - ADAM_B1) * gv
        vn = ADAM_B2 * v_ref[...] + (1.0 - ADAM_B2) * (gv * gv)
        d_ref[...] = -ADAM_LR * ((mn / c1) / (jnp.sqrt(vn / c2) + ADAM_EPS) + ADAM_WD * w_ref[...])
        mo_ref[...] = mn
        vo_ref[...] = vn

    spec = pl.BlockSpec((tr, width), lambda i: (i, 0))
    d, mn, vn = pl.pallas_call(body, name=name, out_shape=[_sds((rows, width), F32)] * 3, grid=(rows // tr,),
                               in_specs=[spec] * 4, out_specs=[spec] * 3, compiler_params=_cp(("parallel",)))(w2, g2, m2, v2)
    return d.reshape(shape), mn.reshape(shape), vn.reshape(shape)


def adamw_ffn(w, m, v, red, kind, ns, name):
    shape = w.shape
    w2, m2, v2 = (t.reshape(-1, shape[-1]) for t in (w, m, v))
    rows, width = w2.shape
    c1 = 1.0 - ADAM_B1 ** ADAM_STEP
    c2 = 1.0 - ADAM_B2 ** ADAM_STEP
    tr, nb = ns // 2, 2
    gspec = pl.BlockSpec((tr, D), lambda i: (((i // nb) * 3 + kind) * nb + i % nb, 0))

    def body(w_ref, g_ref, m_ref, v_ref, go_ref, d_ref, mo_ref, vo_ref):
        gv = g_ref[...]
        mn = ADAM_B1 * m_ref[...] + (1.0 - ADAM_B1) * gv
        vn = ADAM_B2 * v_ref[...] + (1.0 - ADAM_B2) * (gv * gv)
        go_ref[...] = gv
        d_ref[...] = -ADAM_LR * ((mn / c1) / (jnp.sqrt(vn / c2) + ADAM_EPS) + ADAM_WD * w_ref[...])
        mo_ref[...] = mn
        vo_ref[...] = vn

    spec = pl.BlockSpec((tr, width), lambda i: (i, 0))
    outs = pl.pallas_call(body, name=name, out_shape=[_sds((rows, width), F32)] * 4, grid=(rows // tr,),
                          in_specs=[spec, gspec, spec, spec], out_specs=[spec] * 4,
                          compiler_params=_cp(("parallel",)))(w2, red, m2, v2)
    return tuple(t.reshape(shape) for t in outs)


def mod_mm(sc, w_mod, bias, name):
    wm = w_mod.shape[-1]
    tn = _pick(wm, (768, 512, 384, 256, 128))

    def body(a_ref, b_ref, c_ref, o_ref):
        o_ref[...] = _nn(a_ref[...], b_ref[...].astype(BF16)) + c_ref[...]

    return pl.pallas_call(
        body, name=name, out_shape=_sds((DEPTH, 32, wm), F32), grid=(DEPTH, wm // tn),
        in_specs=[pl.BlockSpec((32, D), lambda l, j: (0, 0)), pl.BlockSpec((None, D, tn), lambda l, j: (l, 0, j)),
                  pl.BlockSpec((None, 1, tn), lambda l, j: (l, 0, j))],
        out_specs=pl.BlockSpec((None, 32, tn), lambda l, j: (l, 0, j)),
        compiler_params=_cp(("parallel", "parallel")))(sc, w_mod, bias)


def wmod_dw(sc, dcol, name):
    wm = dcol.shape[-1]
    tm = 256

    def body(a_ref, b_ref, o_ref):
        o_ref[...] = _tn(a_ref[...], b_ref[...].astype(BF16))

    return pl.pallas_call(
        body, name=name, out_shape=_sds((DEPTH, D, wm), F32), grid=(DEPTH, D // tm),
        in_specs=[pl.BlockSpec((32, tm), lambda l, i: (0, i)), pl.BlockSpec((None, 32, wm), lambda l, i: (l, 0, 0))],
        out_specs=pl.BlockSpec((None, tm, wm), lambda l, i: (l, i, 0)),
        compiler_params=_cp(("parallel", "parallel")))(sc, dcol)


def cctx_dx(drow, w_mod, name):
    wm = w_mod.shape[-1]

    def body(a_ref, b_ref, o_ref):
        o_ref[...] = _nt(a_ref[...].astype(BF16), b_ref[...].astype(BF16))

    return pl.pallas_call(
        body, name=name, out_shape=_sds((DEPTH, 16, D), F32), grid=(DEPTH,),
        in_specs=[pl.BlockSpec((None, 16, wm), lambda l: (l, 0, 0)), pl.BlockSpec((None, D, wm), lambda l: (l, 0, 0))],
        out_specs=pl.BlockSpec((None, 16, D), lambda l: (l, 0, 0)), compiler_params=_cp(("parallel",), VMEM_BIG))(drow, w_mod)


HEAD_PERM = (0, 4, 1, 5, 2, 6, 3, 7)


def _rot_rows(wt):
    return jnp.concatenate([-wt[32:64], wt[0:32]], axis=0)


def _unrot_rows(g):
    return jnp.concatenate([g[32:64], -g[0:32]], axis=0)


def _heads(a, n):
    return [a[64 * i:64 * (i + 1)] for i in range(n)]


def kernel(x, c, ctx, c_ctx, w_mod, b_mod, ln_g, ln_b, ffn_w_gate, ffn_w_up, ffn_w_down, mix_ab_w_in, attn_sink, pool_w, pool_scale, mix_ab_w_out, lru_w_in, lru_conv_w, lru_conv_b, lru_wa, lru_ba, lru_wx, lru_bx, lru_lambda, lru_w_out, loss_target, m_c_ctx, m_w_mod, m_b_mod, m_ln_g, m_ln_b, m_ffn_w_gate, m_ffn_w_up, m_ffn_w_down, m_mix_ab_w_in, m_attn_sink, m_pool_w, m_pool_scale, m_mix_ab_w_out, m_lru_w_in, m_lru_conv_w, m_lru_conv_b, m_lru_wa, m_lru_ba, m_lru_wx, m_lru_bx, m_lru_lambda, m_lru_w_out, v_c_ctx, v_w_mod, v_b_mod, v_ln_g, v_ln_b, v_ffn_w_gate, v_ffn_w_up, v_ffn_w_down, v_mix_ab_w_in, v_attn_sink, v_pool_w, v_pool_scale, v_mix_ab_w_out, v_lru_w_in, v_lru_conv_w, v_lru_conv_b, v_lru_wa, v_lru_ba, v_lru_wx, v_lru_bx, v_lru_lambda, v_lru_w_out):
    n_lat, n_ctx = x.shape[1], ctx.shape[1]
    lay = Layout(n_ctx, n_lat)
    T = lay.T
    ns = ffn_w_gate.shape[-1]
    n_li, n_ai = lru_w_in.shape[-1], mix_ab_w_in.shape[-1]
    n_ao, n_lo = mix_ab_w_out.shape[1], lru_w_out.shape[1]
    wm = w_mod.shape[-1]
    dsh = ln_g.shape[-1]
    mx, my, mc = lax.axis_index("x"), lax.axis_index("y"), lax.axis_index("c")
    chip = 2 * mx + my
    me = 2 * chip + mc

    c_all = all_gather8(c, "ag8_c").reshape(16, D)
    cc = jnp.concatenate([c_all, c_ctx[None, :], jnp.zeros((15, D), F32)], axis=0)
    sc = silu_rows(cc, "silu_c")
    bias = lax.dynamic_slice(b_mod, (0, chip * wm), (DEPTH, wm)).reshape(DEPTH, 1, wm)
    modg = all_gather_chips(mod_mm(sc, w_mod, bias, "mod_mm"), "ag_mod")
    modtab = []
    for l in range(DEPTH):
        full = jnp.transpose(modg[:, l], (1, 0, 2)).reshape(32, N_CHIP * wm)
        mine = lax.dynamic_slice(full, (2 * me, 0), (2, N_CHIP * wm))
        modtab.append(jnp.concatenate([mine, full[16:17]], axis=0).reshape(3, N_MOD, D))

    small = jnp.concatenate([ln_g.reshape(6, dsh), ln_b.reshape(6, dsh), lru_conv_w[0], lru_conv_b, lru_ba[0],
                             lru_bx[0], lru_lambda[0], jnp.zeros((9, dsh), F32)], axis=0)
    small = all_gather_chips(small.reshape(2, 16, dsh), "ag_small").reshape(N_CHIP, 32, dsh)
    small = jnp.transpose(small, (1, 0, 2)).reshape(32, D)
    ln_g_f, ln_b_f = small[0:6].reshape(2, 3, D), small[6:12].reshape(2, 3, D)
    conv_w_f, conv_b_f = small[12:16], small[16:17]
    lru_vec = small[17:23]

    hh = 3 * ns // 2
    gate_t, up_t = jnp.swapaxes(ffn_w_gate, -1, -2), jnp.swapaxes(ffn_w_up, -1, -2)
    extra = [0, n_ai + n_ao, n_li + n_lo, 0]
    placed = [ffn_place(gate_t, up_t, ffn_w_down, g // 2, g % 2, f"ag_ffn{g}_place", extra[g]) for g in range(4)]
    placed[1] = place_rows(placed[1], jnp.concatenate([mix_ab_w_in[0].T, mix_ab_w_out[0]], axis=0).astype(BF16), 3 * ns,
                           "ag_mixa_place")
    placed[2] = place_rows(placed[2], jnp.concatenate([lru_w_in[0].T, lru_w_out[0]], axis=0).astype(BF16), 3 * ns,
                           "ag_mixc_place")
    placed = [p.reshape(N_CHIP, 2, p.shape[1] // 2, D) for p in placed]
    wb = [gather_placed(placed[0], "ag_ffn0"), None, None, None]
    mixw = {}

    def mixa_w():
        if "a" not in mixw:
            full = wb[1].reshape(N_CHIP, -1, D)
            ab_in_t = full[:, 3 * ns:3 * ns + n_ai].reshape(N_CHIP * n_ai, D)
            ab_out = full[:, 3 * ns + n_ai:].reshape(N_CHIP * n_ao, D)
            qh, kh = _heads(ab_in_t[Q0:K0], N_HEADS), _heads(ab_in_t[K0:V0], N_KV)
            w_ext_t = jnp.concatenate([qh[h] for h in HEAD_PERM] + [ab_in_t[K0:QR0]]
                                      + [_rot_rows(qh[h]) for h in HEAD_PERM] + [_rot_rows(t) for t in kh], axis=0)
            oh = _heads(ab_out[0:ATT_W], N_HEADS)
            mixw["a"] = (w_ext_t, jnp.concatenate([oh[h] for h in HEAD_PERM] + [ab_out[ATT_W:]], axis=0))
        return mixw["a"]

    def mixc_w():
        if "c" not in mixw:
            full = wb[2].reshape(N_CHIP, -1, D)
            mixw["c"] = (full[:, 3 * ns:3 * ns + n_li].reshape(N_CHIP * n_li, D),
                         full[:, 3 * ns + n_li:].reshape(N_CHIP * n_lo, D))
        return mixw["c"]

    t = jnp.arange(n_lat)
    inv = ROPE_THETA ** (-jnp.arange(16, dtype=F32) / 16.0)
    ang = jnp.concatenate([(t // GRID_W).astype(F32)[:, None] * inv, (t % GRID_W).astype(F32)[:, None] * inv], axis=-1)
    cos1 = jnp.concatenate([jnp.ones((n_ctx, 32), F32), jnp.cos(ang)], axis=0)
    sin1 = jnp.concatenate([jnp.zeros((n_ctx, 32), F32), jnp.sin(ang)], axis=0)
    cos_t = jnp.tile(cos1, (2, 4))
    sin_t = jnp.tile(sin1, (2, 4))
    sk = attn_sink[0]
    sink_tab = jnp.concatenate([jnp.repeat(jnp.stack([sk[:4], sk[4:]], axis=1), HEAD_DIM, axis=1),
                                jnp.zeros((4, 128), F32)], axis=0)
    pscale = pool_scale.reshape(1, POOL_W)

    h0 = jnp.concatenate([ctx, x], axis=1).reshape(T, D)
    tgt = loss_target.reshape(2 * n_lat, D)

    def lnv(l, j):
        return jnp.stack([ln_g_f[l, j], ln_b_f[l, j]])

    subs = [(0, 0, 0.5, 0), (0, 3, 1.0, 1), (0, 6, 0.5, 2), (1, 0, 0.5, 0), (1, 3, 1.0, 1), (1, 6, 0.5, 2)]

    def ffn_core(hm, l, f):
        tag = f"l{l}f{f}"
        gi = 2 * l + f
        w = wb[gi].reshape(N_CHIP, -1, D)
        if gi == 3:
            up, sl, a = ffn_up(lay, hm, w, 0, 1, ns, f"ffn_up_{tag}")
            (y,) = slab_nn_acc(lay, [a], w, [2], ns, f"ffn_down_{tag}")
            return y, dict(up=up, sl=sl, a=a, nbuf=None)
        up, sl, a, nbuf = ffn_up(lay, hm, w, 0, 1, ns, f"ffn_up_{tag}", rider=rider_gather_xy(placed[gi + 1]))
        y, nbuf = slab_nn_acc(lay, [a], w, [2], ns, f"ffn_down_{tag}", rider=rider_gather_fwd(nbuf))
        return y, dict(up=up, sl=sl, a=a, nbuf=nbuf)

    def mixa_core(hm):
        p = mm_nt(hm, mixa_w()[0], "mixa_in")
        qr, kr, vb, u = rope_fwd(lay, p, cos_t, sin_t, "rope")
        cat, lse = attn_fwd(lay, qr, kr, vb, sink_tab, "attn")
        cat = pool_fwd(lay, u, pool_w[0], pscale, cat, "pool")
        return mm_nn(cat, mixa_w()[1], "mixa_out"), dict(qr=qr, kr=kr, vb=vb, u=u, lse=lse, cat=cat)

    def mixc_core(hm):
        p = mm_nt(hm, mixc_w()[0], "mixc_in")
        uc = conv_fwd(lay, p, D, conv_w_f, conv_b_f, "conv")
        a, b = lru_coeffs(lay, uc, lru_wa[0], lru_wx[0], lru_vec, "lru_coef")
        s = lru_scan(lay, a, b, "lru_scan")
        o = lru_gate(lay, p, s, "lru_gate")
        return mm_nn(o, mixc_w()[1], "mixc_out"), dict(p=p, uc=uc, a=a, s=s, o=o)

    recs = []
    h = h0
    hm = modulate(lay, h0, modtab[0], 0, 1, "mod_first")
    for k, (l, k0, coef, j) in enumerate(subs):
        if k0 == 3:
            y, core = mixa_core(hm) if l == 0 else mixc_core(hm)
        else:
            y, core = ffn_core(hm, l, k0 // 6)
        nxt = None if k == 5 else (modtab[subs[k + 1][0]], subs[k + 1][1], subs[k + 1][1] + 1)
        nbuf = core.pop("nbuf", None)
        res = resid_ln(lay, h, y, modtab[l], k0 + 2, coef, lnv(l, j), f"ln_s{k}", nxt=nxt,
                       prev_ln=None if k == 0 else lnv(*subs[k - 1][::3]),
                       rider=None if nbuf is None else rider_gather_d2d(nbuf))
        if nbuf is not None:
            wb[2 * l + k0 // 6 + 1] = res[-1]
        recs.append(dict(h=h, hm=hm, y=y, xhat=res[0], rstd=res[1], **core))
        h, hm = res[0], (None if nxt is None else res[2])


    dln = {}
    dms = {}
    mixg = {}
    ffn_red = [lax.empty((4, 2, hh, D), F32)]
    mix_red = {}
    pending = []

    def rs_sib(p):
        return None if p is None else rider_reduce_sib(p["buf"])

    def rs_add2(p, recv):
        p["q"] = add_own_half(p["buf"], recv, BF16, f"rs_add2_{p['key']}")

    def rs_join(p, arr):
        if isinstance(p["key"], int):
            return rider_join(sum_slots(p["q"], arr, f"rs_add4_{p['key']}", dst=ffn_red[0], g=p["key"]), p["key"])
        return rider_join(sum_slots(p["q"], arr, f"rs_add4_{p['key']}"))

    def rs_done(p, joined):
        if isinstance(p["key"], int):
            ffn_red[0] = joined
        else:
            mix_red[p["key"]] = joined.reshape(-1, D)

    def ffn_core_bwd(dy, r, l, f):
        tag = f"l{l}f{f}"
        gi = 2 * l + f
        w = wb[gi].reshape(N_CHIP, -1, D)
        p = pending.pop() if pending else None
        gb = lax.empty((N_CHIP, 3 * ns, D), F32)
        if p is None:
            dg, du = ffn_bwd_da(lay, dy, w, 2, r["up"], r["sl"], ns, f"ffn_da_{tag}")
            (gb,) = slab_tn(lay, r["a"], dy, gb, 2, ns, f"ffn_dwd_{tag}")
            (gb,) = slab_tn(lay, dg, r["hm"], gb, 0, ns, f"ffn_dwg_{tag}")
            (gb,) = slab_tn(lay, du, r["hm"], gb, 1, ns, f"ffn_dwu_{tag}")
            (dhm,) = slab_nn_acc(lay, [dg, du], w, [0, 1], ns, f"ffn_dh_{tag}")
        elif gi == 0:
            dg, du, recv = ffn_bwd_da(lay, dy, w, 2, r["up"], r["sl"], ns, f"ffn_da_{tag}", rider=rs_sib(p))
            rs_add2(p, recv)
            gb, arr = slab_tn(lay, r["a"], dy, gb, 2, ns, f"ffn_dwd_{tag}", rider=rider_reduce_copies(p["q"]))
            gb, joined = slab_tn(lay, dg, r["hm"], gb, 0, ns, f"ffn_dwg_{tag}", rider=rs_join(p, arr))
            rs_done(p, joined)
            (gb,) = slab_tn(lay, du, r["hm"], gb, 1, ns, f"ffn_dwu_{tag}")
            own = gb.reshape(N_CHIP, 2, hh, D)
            dhm, recv = slab_nn_acc(lay, [dg, du], w, [0, 1], ns, f"ffn_dh_{tag}", rider=rider_reduce_sib(own))
            pending.append(dict(buf=own, key=gi, recv=recv))
            return dhm
        else:
            dg, du, recv = ffn_bwd_da(lay, dy, w, 2, r["up"], r["sl"], ns, f"ffn_da_{tag}", rider=rs_sib(p))
            rs_add2(p, recv)
            gb, arr = slab_tn(lay, r["a"], dy, gb, 2, ns, f"ffn_dwd_{tag}", rider=rider_reduce_copy(p["q"], 0))
            gb, arr = slab_tn(lay, dg, r["hm"], gb, 0, ns, f"ffn_dwg_{tag}", rider=rider_reduce_copy(p["q"], 1, arr))
            gb, arr = slab_tn(lay, du, r["hm"], gb, 1, ns, f"ffn_dwu_{tag}", rider=rider_reduce_copy(p["q"], 2, arr))
            dhm, joined = slab_nn_acc(lay, [dg, du], w, [0, 1], ns, f"ffn_dh_{tag}", rider=rs_join(p, arr))
            rs_done(p, joined)
        pending.append(dict(buf=gb.reshape(N_CHIP, 2, hh, D), key=gi))
        return dhm

    def mixc_core_bwd(dy, r):
        p = pending.pop() if pending else None
        w_in_t, w_out = mixc_w()
        if p is None:
            do_c = mm_nt(dy, w_out, "mixc_out_dx")
        else:
            do_c, recv = mm_nt(dy, w_out, "mixc_out_dx", rider=rs_sib(p))
            rs_add2(p, recv)
        g_out = mm_tn(r["o"], dy, "mixc_out_dw")
        dgate, dyg = lru_gate_bwd(lay, r["p"], r["s"], do_c, "lru_gate_b")
        da_c, db_c = lru_scan_bwd(lay, r["a"], r["s"], dyg, "lru_scan_b")
        res = lru_coeffs_bwd(lay, r["uc"], lru_wa[0], lru_wx[0], lru_vec, da_c, db_c, "lru_coef_b",
                             rider=None if p is None else rider_reduce_copies(p["q"]))
        duc, mixg["wa"], mixg["wx"], mixg["vec"] = res[:4]
        dp_c, mixg["cw"], mixg["cb"] = conv_bwd(lay, r["p"], D, conv_w_f, duc, dgate, "conv_b")
        if p is None:
            g_in_t = mm_tn(dp_c, r["hm"], "mixc_in_dw")
        else:
            g_in_t, joined = mm_tn(dp_c, r["hm"], "mixc_in_dw", rider=rs_join(p, res[4]))
            rs_done(p, joined)
        buf = jnp.concatenate([g_in_t.reshape(N_CHIP, n_li, D), g_out.reshape(N_CHIP, n_lo, D)], axis=1)
        pending.append(dict(buf=buf.reshape(N_CHIP, 2, (n_li + n_lo) // 2, D), key="c"))
        return mm_nn(dp_c, w_in_t, "mixc_in_dx")

    def mixa_core_bwd(dy, r):
        p = pending.pop() if pending else None
        w_ext_t, w_out_ext = mixa_w()
        if p is None:
            dcat = mm_nt(dy, w_out_ext, "mixa_out_dx")
        else:
            dcat, recv = mm_nt(dy, w_out_ext, "mixa_out_dx", rider=rs_sib(p))
            rs_add2(p, recv)
        g_out_ext = mm_tn(r["cat"], dy, "mixa_out_dw")
        res = attn_bwd(lay, r["qr"], r["kr"], r["vb"], sink_tab, r["lse"], dcat, "attn_b",
                       rider=None if p is None else rider_reduce_copies(p["q"]))
        dqr, dkr, dv, mixg["sink"] = res[:4]
        du_a, mixg["pw"], mixg["ps"] = pool_bwd(lay, r["u"], dcat, pool_w[0], pscale, "pool_b")
        dp_a = rope_bwd(lay, dqr, dkr, dv, du_a, cos_t, sin_t, "rope_b")
        if p is None:
            g_ext_t = mm_tn(dp_a, r["hm"], "mixa_in_dw")
        else:
            g_ext_t, joined = mm_tn(dp_a, r["hm"], "mixa_in_dw", rider=rs_join(p, res[4]))
            rs_done(p, joined)
        gq, gqr = _heads(g_ext_t[Q0:K0], N_HEADS), _heads(g_ext_t[QR0:KR0], N_HEADS)
        g_q = [None] * N_HEADS
        for i, h in enumerate(HEAD_PERM):
            g_q[h] = gq[i] + _unrot_rows(gqr[i])
        gk = [a + _unrot_rows(b) for a, b in zip(_heads(g_ext_t[K0:V0], N_KV), _heads(g_ext_t[KR0:PEXT], N_KV))]
        g_ab_in_t = jnp.concatenate(g_q + gk + [g_ext_t[V0:QR0]], axis=0)
        go = _heads(g_out_ext[0:ATT_W], N_HEADS)
        g_o = [None] * N_HEADS
        for i, h in enumerate(HEAD_PERM):
            g_o[h] = go[i]
        g_ab_out = jnp.concatenate(g_o + [g_out_ext[ATT_W:]], axis=0)
        buf = jnp.concatenate([g_ab_in_t.reshape(N_CHIP, n_ai, D), g_ab_out.reshape(N_CHIP, n_ao, D)], axis=1)
        pending.append(dict(buf=buf.reshape(N_CHIP, 2, (n_ai + n_ao) // 2, D), key="a"))
        return mm_nn(dp_a, w_ext_t, "mixa_in_dx")

    l, k0, coef, j = subs[5]
    dy, dres, s1, lparts = loss_lnb(lay, recs[5]["xhat"], recs[5]["rstd"], recs[5]["y"], tgt, modtab[l], k0 + 2, coef,
                                    lnv(l, j), "loss_lnb")
    loss = lax.psum(jnp.sum(lparts), ("x", "y", "c"))
    for k in range(5, -1, -1):
        l, k0, coef, j = subs[k]
        r = recs[k]
        if k0 == 3:
            dhm = mixa_core_bwd(dy, r) if l == 0 else mixc_core_bwd(dy, r)
        else:
            dhm = ffn_core_bwd(dy, r, l, k0 // 6)
        dln[(l, j)] = s1
        if k > 0:
            lp, k0p, coefp, jp = subs[k - 1]
            rp = recs[k - 1]
            dy, dres, s1, s2 = modb_lnb(lay, dres, dhm, modtab[l], k0 + 1, rp["xhat"], rp["rstd"], rp["y"],
                                        modtab[lp], k0p + 2, coefp, lnv(lp, jp), f"modb_lnb_s{k}")
        else:
            gx, s2 = mod_bwd(lay, dres, dhm, r["h"], modtab[l], k0 + 1, "modb_s0")
        dms[(l, k0)] = s2
    sums = block_sums(lay, list(dln.values()) + list(dms.values()), "block_sums")
    dln, dms = dict(zip(dln, sums[:len(dln)])), dict(zip(dms, sums[len(dln):]))
    grad_x = gx.reshape(2, n_lat, D)
    g_wa, g_wx, g_vec, g_cw, g_cb, g_sink, g_pw, g_ps = (mixg[n] for n in ("wa", "wx", "vec", "cw", "cb", "sink", "pw", "ps"))

    rows = []
    for l in range(DEPTH):
        per_k = []
        for k0, j in ((0, 0), (3, 1), (6, 2)):
            per_k += [dms[(l, k0)][:3, 0], dms[(l, k0)][:3, 1], dln[(l, j)][:3, 2]]
        rows.append(jnp.stack(per_k, axis=1).reshape(3, N_MOD * D))
    dmod_loc = jnp.concatenate(rows + [jnp.zeros((2, N_MOD * D), F32)], axis=0)
    dmod_all, g_b_mod = mod_grad_rows(all_gather8(dmod_loc, "ag8_dmod"), "dmod_rows")
    dcol = lax.dynamic_slice(dmod_all, (0, 0, chip * wm), (DEPTH, 32, wm))
    g_w_mod = wmod_dw(sc, dcol, "wmod_dw")
    g_cctx = cctx_grad(cctx_dx(dcol[:, 16:32], w_mod, "cctx_dx"), c_ctx[None, :], "cctx_grad")

    g_ln_g =jnp.stack([jnp.stack([dln[(l, j)][3, 1] for j in range(3)]) for l in range(DEPTH)])
    g_ln_b = jnp.stack([jnp.stack([dln[(l, j)][3, 0] for j in range(3)]) for l in range(DEPTH)])
    sink_row = jnp.sum(g_sink, axis=0)[:4]
    g_sink8 = jnp.concatenate([sink_row[:, 0], sink_row[:, HEAD_DIM]])
    misc = jnp.concatenate([g_sink8, jnp.sum(g_ps, axis=0).reshape(POOL_W), jnp.zeros((D - 8 - POOL_W,), F32)])
    small_g = jnp.concatenate([
        g_ln_g.reshape(6, D), g_ln_b.reshape(6, D), jnp.sum(g_cw, axis=0), jnp.sum(g_cb, axis=0), g_vec,
        misc[None, :], jnp.sum(g_pw, axis=0).reshape(64, D), g_wa.reshape(256, D), g_wx.reshape(256, D), g_cctx,
        jnp.zeros((39, D), F32)], axis=0)
    n_small = small_g.shape[0] // N_CHIP
    last = pending.pop()
    ffn_red = reduce_scatter_chips(last["buf"], f"ffn{last['key']}", wire=BF16, dst=ffn_red[0], g=last["key"],
                                   recv=last.get("recv")).reshape(12 * ns, D)
    small_red = reduce_scatter_chips(small_g.reshape(N_CHIP, 2, n_small // 2, D), "small")
    small_red = all_gather_chips(small_red, "ag_smallg").reshape(N_CHIP * n_small, D)

    ffn_kind = dict(ffn_w_gate=0, ffn_w_up=1, ffn_w_down=2)

    def cols(a):
        return lax.dynamic_slice_in_dim(a, chip * dsh, dsh, axis=a.ndim - 1)

    sr = small_red
    grads = dict(
        c_ctx=sr[600], w_mod=g_w_mod, b_mod=g_b_mod,
        ln_g=cols(sr[0:6]).reshape(2, 3, dsh), ln_b=cols(sr[6:12]).reshape(2, 3, dsh),
        mix_ab_w_in=mix_red["a"][0:n_ai][None], attn_sink=sr[23, 0:8][None], pool_w=sr[24:88].reshape(1, 4, 128, 128),
        pool_scale=sr[23, 8:8 + POOL_W][None], mix_ab_w_out=mix_red["a"][n_ai:][None],
        lru_w_in=mix_red["c"][0:n_li].T[None],
        lru_conv_w=cols(sr[12:16])[None], lru_conv_b=cols(sr[16:17]), lru_wa=sr[88:344].reshape(1, 2, 8, 128, 128),
        lru_ba=cols(sr[17:19])[None], lru_wx=sr[344:600].reshape(1, 2, 8, 128, 128), lru_bx=cols(sr[19:21])[None],
        lru_lambda=cols(sr[21:23])[None], lru_w_out=mix_red["c"][n_li:][None])
    params = dict(c_ctx=(c_ctx, m_c_ctx, v_c_ctx), w_mod=(w_mod, m_w_mod, v_w_mod), b_mod=(b_mod, m_b_mod, v_b_mod),
                  ln_g=(ln_g, m_ln_g, v_ln_g), ln_b=(ln_b, m_ln_b, v_ln_b),
                  ffn_w_gate=(ffn_w_gate, m_ffn_w_gate, v_ffn_w_gate), ffn_w_up=(ffn_w_up, m_ffn_w_up, v_ffn_w_up),
                  ffn_w_down=(ffn_w_down, m_ffn_w_down, v_ffn_w_down),
                  mix_ab_w_in=(mix_ab_w_in, m_mix_ab_w_in, v_mix_ab_w_in), attn_sink=(attn_sink, m_attn_sink, v_attn_sink),
                  pool_w=(pool_w, m_pool_w, v_pool_w), pool_scale=(pool_scale, m_pool_scale, v_pool_scale),
                  mix_ab_w_out=(mix_ab_w_out, m_mix_ab_w_out, v_mix_ab_w_out), lru_w_in=(lru_w_in, m_lru_w_in, v_lru_w_in),
                  lru_conv_w=(lru_conv_w, m_lru_conv_w, v_lru_conv_w), lru_conv_b=(lru_conv_b, m_lru_conv_b, v_lru_conv_b),
                  lru_wa=(lru_wa, m_lru_wa, v_lru_wa), lru_ba=(lru_ba, m_lru_ba, v_lru_ba), lru_wx=(lru_wx, m_lru_wx, v_lru_wx),
                  lru_bx=(lru_bx, m_lru_bx, v_lru_bx), lru_lambda=(lru_lambda, m_lru_lambda, v_lru_lambda),
                  lru_w_out=(lru_w_out, m_lru_w_out, v_lru_w_out))
    gl, dl, ml, vl = [], [], [], []
    transposed = ("ffn_w_gate", "ffn_w_up", "mix_ab_w_in")
    for name, (w, m, v) in params.items():
        if name in transposed:
            w, m, v = (jnp.swapaxes(t, -1, -2) for t in (w, m, v))
        if name in ffn_kind:
            g, d, mn, vn = adamw_ffn(w, m, v, ffn_red, ffn_kind[name], ns, f"adamw_{name}")
        else:
            g = grads[name].reshape(w.shape)
            d, mn, vn = adamw(w, g, m, v, f"adamw_{name}")
        if name in transposed:
            g, d, mn, vn = (jnp.swapaxes(t, -1, -2) for t in (g, d, mn, vn))
        gl.append(g)
        dl.append(d)
        ml.append(mn)
        vl.append(vn)
    return (loss, grad_x, *gl, *dl, *ml, *vl)
```
